```python
import jax, jax.numpy as jnp
from jax import lax
import numpy as np

D_MODEL = 1024
BATCH = 8
SEQ = 2048
DEPTH = 1

N_META = 16
Q_BLOCK = 128
MLA_HEADS = 8
MLA_Q_RANK = 384
MLA_KV_RANK = 128
MLA_NOPE_DIM = 64
MLA_ROPE_DIM = 32
MLA_QK_DIM = MLA_NOPE_DIM + MLA_ROPE_DIM
MLA_V_DIM = 64
MLA_V_WIDTH = MLA_HEADS * MLA_V_DIM
ROPE_THETA = 10000.0
FOX_HEADS = 8
FOX_HEAD_DIM = 64
FOX_WIDTH = FOX_HEADS * FOX_HEAD_DIM
D_FF = 2816
CONV_WIDTH = 3
LN_EPS = 1e-5
RMS_EPS = 1e-6
DN_ALPHA = (2 * DEPTH) ** 0.25
DN_BETA = (8 * DEPTH) ** -0.25
NEG_INF = -1e30
IN_SPLITS = (MLA_Q_RANK, MLA_KV_RANK, MLA_ROPE_DIM, FOX_WIDTH, FOX_WIDTH, FOX_WIDTH, FOX_HEADS, 2 * D_MODEL)
IN_TOTAL = MLA_Q_RANK + MLA_KV_RANK + MLA_ROPE_DIM + 3 * FOX_WIDTH + FOX_HEADS + 2 * D_MODEL

kernel_name = 'hybrid_mla_fox_convglu_deepnorm_meta'


def layer_norm(x, g, b):
    xf = x.astype(jnp.float32)
    mu = jnp.mean(xf, axis=-1, keepdims=True)
    var = jnp.mean(jnp.square(xf - mu), axis=-1, keepdims=True)
    y = (xf - mu) * lax.rsqrt(var + LN_EPS)
    return (y * g.astype(jnp.float32) + b.astype(jnp.float32)).astype(x.dtype)


def rms_norm(x, g):
    xf = x.astype(jnp.float32)
    y = xf * lax.rsqrt(jnp.mean(jnp.square(xf), axis=-1, keepdims=True) + RMS_EPS)
    return (y * g.astype(jnp.float32)).astype(x.dtype)


def apply_rope(t, pos):
    half = t.shape[-1] // 2
    inv_freq = ROPE_THETA ** (-jnp.arange(half, dtype=jnp.float32) / half)
    ang = pos.astype(jnp.float32)[:, None] * inv_freq[None, :]
    cos = jnp.cos(ang).astype(t.dtype)
    sin = jnp.sin(ang).astype(t.dtype)
    t1, t2 = t[..., :half], t[..., half:]
    return jnp.concatenate([t1 * cos - t2 * sin, t2 * cos + t1 * sin], axis=-1)


def causal_block_attention(q, k, v, scale, cum_logf=None):
    B, H, L, dk = q.shape
    dv = v.shape[-1]
    pad = (-N_META) % Q_BLOCK
    Lp = L + pad
    nb = Lp // Q_BLOCK
    pad4 = lambda a: jnp.pad(a, ((0, 0), (0, 0), (pad, 0), (0, 0)))
    q, k, v = pad4(q), pad4(k), pad4(v)
    kpos = jnp.arange(Lp)
    key_valid = kpos >= pad
    qb = q.reshape(B, H, nb, Q_BLOCK, dk).transpose(2, 0, 1, 3, 4)
    idx = jnp.arange(nb)
    if cum_logf is None:
        c = None
        xs = (idx, qb)
    else:
        c = jnp.pad(cum_logf, ((0, 0), (0, 0), (pad, 0)))
        cb = c.reshape(B, H, nb, Q_BLOCK).transpose(2, 0, 1, 3)
        xs = (idx, qb, cb)

    def one_block(blk):
        i, q_i = blk[0], blk[1]
        s = jnp.einsum('bhqd,bhkd->bhqk', q_i, k).astype(jnp.float32) * scale
        if c is not None:
            s = s + (blk[2][..., :, None] - c[:, :, None, :])
        qpos = i * Q_BLOCK + jnp.arange(Q_BLOCK)
        mask = (kpos[None, :] <= qpos[:, None]) & key_valid[None, :]
        s = jnp.where(mask[None, None], s, NEG_INF)
        p = jax.nn.softmax(s, axis=-1)
        return jnp.einsum('bhqk,bhkd->bhqd', p.astype(v.dtype), v)

    ob = lax.map(one_block, xs)
    o = ob.transpose(1, 2, 0, 3, 4).reshape(B, H, Lp, dv)
    return o[:, :, pad:, :]


def hybrid_mixer(h, w_in, b_gate, b_forget, q_norm_g, w_q_up, kv_norm_g, w_kv_up,
                 w_branch_mla, w_branch_fox, w_out):
    B, L, _ = h.shape
    proj = h @ w_in
    offs = np.cumsum(IN_SPLITS)[:-1].tolist()
    q_lat, kv_lat, k_rope, fq, fk, fv, f_logit, gate_logit = jnp.split(proj, offs, axis=-1)
    pos = jnp.arange(L)

    q = (rms_norm(q_lat, q_norm_g) @ w_q_up).reshape(B, L, MLA_HEADS, MLA_QK_DIM).transpose(0, 2, 1, 3)
    q_nope, q_pe = q[..., :MLA_NOPE_DIM], q[..., MLA_NOPE_DIM:]
    kv = (rms_norm(kv_lat, kv_norm_g) @ w_kv_up).reshape(B, L, MLA_HEADS, MLA_NOPE_DIM + MLA_V_DIM).transpose(0, 2, 1, 3)
    k_nope, v_mla = kv[..., :MLA_NOPE_DIM], kv[..., MLA_NOPE_DIM:]
    q_pe = apply_rope(q_pe, pos)
    k_pe = apply_rope(k_rope[:, None], pos)
    q_mla = jnp.concatenate([q_nope, q_pe], axis=-1)
    k_mla = jnp.concatenate([k_nope, jnp.broadcast_to(k_pe, (B, MLA_HEADS, L, MLA_ROPE_DIM))], axis=-1)
    o_mla = causal_block_attention(q_mla, k_mla, v_mla, MLA_QK_DIM ** -0.5)
    o_mla = o_mla.transpose(0, 2, 1, 3).reshape(B, L, MLA_V_WIDTH)

    heads = lambda t: t.reshape(B, L, FOX_HEADS, FOX_HEAD_DIM).transpose(0, 2, 1, 3)
    log_f = jax.nn.log_sigmoid((f_logit + b_forget).astype(jnp.float32))
    cum = lax.cumsum(log_f, axis=1).transpose(0, 2, 1)
    o_fox = causal_block_attention(heads(fq), heads(fk), heads(fv), FOX_HEAD_DIM ** -0.5, cum)
    o_fox = o_fox.transpose(0, 2, 1, 3).reshape(B, L, FOX_WIDTH)

    gates = jax.nn.sigmoid(gate_logit + b_gate)
    g_mla, g_fox = gates[..., :D_MODEL], gates[..., D_MODEL:]
    merged = g_mla * (o_mla @ w_branch_mla) + g_fox * (o_fox @ w_branch_fox)
    return merged @ w_out


def causal_depthwise_conv(x, w, b):
    C = x.shape[-1]
    y = lax.conv_general_dilated(x, w[:, None, :].astype(x.dtype), window_strides=(1,),
                                 padding=[(CONV_WIDTH - 1, 0)],
                                 dimension_numbers=('NWC', 'WIO', 'NWC'),
                                 feature_group_count=C)
    return y + b


def conv_glu_ffn(h, w_up, conv_w, conv_b, w_down):
    up = h @ w_up
    gate, val = up[..., :D_FF], up[..., D_FF:]
    gate = causal_depthwise_conv(gate, conv_w, conv_b)
    return (jax.nn.silu(gate) * val) @ w_down


def _fwd_setup_inputs(seed: int = 0) -> dict:
    key = jax.random.key(seed)
    ks = jax.random.split(key, 24)
    f32 = jnp.float32
    nrm = lambda k, shape, s: jax.random.normal(k, shape, f32) * s
    gain = lambda k, shape: 1.0 + 0.05 * jax.random.normal(k, shape, f32)
    return {
        'x': nrm(ks[0], (BATCH, SEQ, D_MODEL), 1.0),
        'meta_tokens': nrm(ks[1], (N_META, D_MODEL), 1.0),
        'ln_emb_g': gain(ks[2], (D_MODEL,)),
        'ln_emb_b': nrm(ks[3], (D_MODEL,), 0.02),
        'w_in': nrm(ks[4], (DEPTH, D_MODEL, IN_TOTAL), D_MODEL ** -0.5),
        'b_gate': nrm(ks[5], (DEPTH, 2 * D_MODEL), 0.02),
        'b_forget': jax.random.uniform(ks[6], (DEPTH, FOX_HEADS), f32, 2.0, 6.0),
        'q_norm_g': gain(ks[7], (DEPTH, MLA_Q_RANK)),
        'w_q_up': nrm(ks[8], (DEPTH, MLA_Q_RANK, MLA_HEADS * MLA_QK_DIM), MLA_Q_RANK ** -0.5),
        'kv_norm_g': gain(ks[9], (DEPTH, MLA_KV_RANK)),
        'w_kv_up': nrm(ks[10], (DEPTH, MLA_KV_RANK, MLA_HEADS * (MLA_NOPE_DIM + MLA_V_DIM)), MLA_KV_RANK ** -0.5),
        'w_branch_mla': nrm(ks[11], (DEPTH, MLA_V_WIDTH, D_MODEL), MLA_V_WIDTH ** -0.5 * DN_BETA),
        'w_branch_fox': nrm(ks[12], (DEPTH, FOX_WIDTH, D_MODEL), FOX_WIDTH ** -0.5 * DN_BETA),
        'w_out': nrm(ks[13], (DEPTH, D_MODEL, D_MODEL), D_MODEL ** -0.5 * DN_BETA),
        'ln_mix_g': gain(ks[14], (DEPTH, D_MODEL)),
        'ln_mix_b': nrm(ks[15], (DEPTH, D_MODEL), 0.02),
        'w_ffn_up': nrm(ks[16], (DEPTH, D_MODEL, 2 * D_FF), D_MODEL ** -0.5 * DN_BETA),
        'conv_w': nrm(ks[17], (DEPTH, CONV_WIDTH, D_FF), CONV_WIDTH ** -0.5),
        'conv_b': nrm(ks[18], (DEPTH, D_FF), 0.02),
        'w_ffn_down': nrm(ks[19], (DEPTH, D_FF, D_MODEL), D_FF ** -0.5 * DN_BETA),
        'ln_ffn_g': gain(ks[20], (DEPTH, D_MODEL)),
        'ln_ffn_b': nrm(ks[21], (DEPTH, D_MODEL), 0.02),
    }


def _fwd_reference(x, meta_tokens, ln_emb_g, ln_emb_b, w_in, b_gate, b_forget, q_norm_g, w_q_up,
              kv_norm_g, w_kv_up, w_branch_mla, w_branch_fox, w_out, ln_mix_g, ln_mix_b,
              w_ffn_up, conv_w, conv_b, w_ffn_down, ln_ffn_g, ln_ffn_b):
    B = x.shape[0]
    meta = jnp.broadcast_to(meta_tokens[None].astype(x.dtype), (B, N_META, D_MODEL))
    h = layer_norm(jnp.concatenate([meta, x], axis=1), ln_emb_g, ln_emb_b)
    for l in range(DEPTH):
        m = hybrid_mixer(h, w_in[l], b_gate[l], b_forget[l], q_norm_g[l], w_q_up[l], kv_norm_g[l],
                         w_kv_up[l], w_branch_mla[l], w_branch_fox[l], w_out[l])
        h = layer_norm(DN_ALPHA * h + m, ln_mix_g[l], ln_mix_b[l])
        f = conv_glu_ffn(h, w_ffn_up[l], conv_w[l], conv_b[l], w_ffn_down[l])
        h = layer_norm(DN_ALPHA * h + f, ln_ffn_g[l], ln_ffn_b[l])
    return h[:, N_META:, :]


import jax as _jax
import jax.numpy as _jnp

TWIN_FORMAT = 'train_step'
FWD_PARAMS = ['x', 'meta_tokens', 'ln_emb_g', 'ln_emb_b', 'w_in', 'b_gate', 'b_forget', 'q_norm_g', 'w_q_up', 'kv_norm_g', 'w_kv_up', 'w_branch_mla', 'w_branch_fox', 'w_out', 'ln_mix_g', 'ln_mix_b', 'w_ffn_up', 'conv_w', 'conv_b', 'w_ffn_down', 'ln_ffn_g', 'ln_ffn_b']
TWIN_WEIGHTS = ['meta_tokens', 'ln_emb_g', 'ln_emb_b', 'w_in', 'b_gate', 'b_forget', 'q_norm_g', 'w_q_up', 'kv_norm_g', 'w_kv_up', 'w_branch_mla', 'w_branch_fox', 'w_out', 'ln_mix_g', 'ln_mix_b', 'w_ffn_up', 'conv_w', 'conv_b', 'w_ffn_down', 'ln_ffn_g', 'ln_ffn_b']
TWIN_DIFF_INPUT = 'x'
TWIN_INPUTS = ['x', 'meta_tokens', 'ln_emb_g', 'ln_emb_b', 'w_in', 'b_gate', 'b_forget', 'q_norm_g', 'w_q_up', 'kv_norm_g', 'w_kv_up', 'w_branch_mla', 'w_branch_fox', 'w_out', 'ln_mix_g', 'ln_mix_b', 'w_ffn_up', 'conv_w', 'conv_b', 'w_ffn_down', 'ln_ffn_g', 'ln_ffn_b', 'loss_target', 'm_meta_tokens', 'm_ln_emb_g', 'm_ln_emb_b', 'm_w_in', 'm_b_gate', 'm_b_forget', 'm_q_norm_g', 'm_w_q_up', 'm_kv_norm_g', 'm_w_kv_up', 'm_w_branch_mla', 'm_w_branch_fox', 'm_w_out', 'm_ln_mix_g', 'm_ln_mix_b', 'm_w_ffn_up', 'm_conv_w', 'm_conv_b', 'm_w_ffn_down', 'm_ln_ffn_g', 'm_ln_ffn_b', 'v_meta_tokens', 'v_ln_emb_g', 'v_ln_emb_b', 'v_w_in', 'v_b_gate', 'v_b_forget', 'v_q_norm_g', 'v_w_q_up', 'v_kv_norm_g', 'v_w_kv_up', 'v_w_branch_mla', 'v_w_branch_fox', 'v_w_out', 'v_ln_mix_g', 'v_ln_mix_b', 'v_w_ffn_up', 'v_conv_w', 'v_conv_b', 'v_w_ffn_down', 'v_ln_ffn_g', 'v_ln_ffn_b']
TWIN_OUTPUTS = ['loss', 'grad_x', 'grad_meta_tokens', 'grad_ln_emb_g', 'grad_ln_emb_b', 'grad_w_in', 'grad_b_gate', 'grad_b_forget', 'grad_q_norm_g', 'grad_w_q_up', 'grad_kv_norm_g', 'grad_w_kv_up', 'grad_w_branch_mla', 'grad_w_branch_fox', 'grad_w_out', 'grad_ln_mix_g', 'grad_ln_mix_b', 'grad_w_ffn_up', 'grad_conv_w', 'grad_conv_b', 'grad_w_ffn_down', 'grad_ln_ffn_g', 'grad_ln_ffn_b', 'delta_meta_tokens', 'delta_ln_emb_g', 'delta_ln_emb_b', 'delta_w_in', 'delta_b_gate', 'delta_b_forget', 'delta_q_norm_g', 'delta_w_q_up', 'delta_kv_norm_g', 'delta_w_kv_up', 'delta_w_branch_mla', 'delta_w_branch_fox', 'delta_w_out', 'delta_ln_mix_g', 'delta_ln_mix_b', 'delta_w_ffn_up', 'delta_conv_w', 'delta_conv_b', 'delta_w_ffn_down', 'delta_ln_ffn_g', 'delta_ln_ffn_b', 'new_m_meta_tokens', 'new_m_ln_emb_g', 'new_m_ln_emb_b', 'new_m_w_in', 'new_m_b_gate', 'new_m_b_forget', 'new_m_q_norm_g', 'new_m_w_q_up', 'new_m_kv_norm_g', 'new_m_w_kv_up', 'new_m_w_branch_mla', 'new_m_w_branch_fox', 'new_m_w_out', 'new_m_ln_mix_g', 'new_m_ln_mix_b', 'new_m_w_ffn_up', 'new_m_conv_w', 'new_m_conv_b', 'new_m_w_ffn_down', 'new_m_ln_ffn_g', 'new_m_ln_ffn_b', 'new_v_meta_tokens', 'new_v_ln_emb_g', 'new_v_ln_emb_b', 'new_v_w_in', 'new_v_b_gate', 'new_v_b_forget', 'new_v_q_norm_g', 'new_v_w_q_up', 'new_v_kv_norm_g', 'new_v_w_kv_up', 'new_v_w_branch_mla', 'new_v_w_branch_fox', 'new_v_w_out', 'new_v_ln_mix_g', 'new_v_ln_mix_b', 'new_v_w_ffn_up', 'new_v_conv_w', 'new_v_conv_b', 'new_v_w_ffn_down', 'new_v_ln_ffn_g', 'new_v_ln_ffn_b']
TWIN_LEAF_KINDS = {'loss': 'loss', 'grad_x': 'grad_x', 'grad_meta_tokens': 'grad_w', 'grad_ln_emb_g': 'grad_w', 'grad_ln_emb_b': 'grad_w', 'grad_w_in': 'grad_w', 'grad_b_gate': 'grad_w', 'grad_b_forget': 'grad_w', 'grad_q_norm_g': 'grad_w', 'grad_w_q_up': 'grad_w', 'grad_kv_norm_g': 'grad_w', 'grad_w_kv_up': 'grad_w', 'grad_w_branch_mla': 'grad_w', 'grad_w_branch_fox': 'grad_w', 'grad_w_out': 'grad_w', 'grad_ln_mix_g': 'grad_w', 'grad_ln_mix_b': 'grad_w', 'grad_w_ffn_up': 'grad_w', 'grad_conv_w': 'grad_w', 'grad_conv_b': 'grad_w', 'grad_w_ffn_down': 'grad_w', 'grad_ln_ffn_g': 'grad_w', 'grad_ln_ffn_b': 'grad_w', 'delta_meta_tokens': 'delta_w', 'delta_ln_emb_g': 'delta_w', 'delta_ln_emb_b': 'delta_w', 'delta_w_in': 'delta_w', 'delta_b_gate': 'delta_w', 'delta_b_forget': 'delta_w', 'delta_q_norm_g': 'delta_w', 'delta_w_q_up': 'delta_w', 'delta_kv_norm_g': 'delta_w', 'delta_w_kv_up': 'delta_w', 'delta_w_branch_mla': 'delta_w', 'delta_w_branch_fox': 'delta_w', 'delta_w_out': 'delta_w', 'delta_ln_mix_g': 'delta_w', 'delta_ln_mix_b': 'delta_w', 'delta_w_ffn_up': 'delta_w', 'delta_conv_w': 'delta_w', 'delta_conv_b': 'delta_w', 'delta_w_ffn_down': 'delta_w', 'delta_ln_ffn_g': 'delta_w', 'delta_ln_ffn_b': 'delta_w', 'new_m_meta_tokens': 'new_m', 'new_m_ln_emb_g': 'new_m', 'new_m_ln_emb_b': 'new_m', 'new_m_w_in': 'new_m', 'new_m_b_gate': 'new_m', 'new_m_b_forget': 'new_m', 'new_m_q_norm_g': 'new_m', 'new_m_w_q_up': 'new_m', 'new_m_kv_norm_g': 'new_m', 'new_m_w_kv_up': 'new_m', 'new_m_w_branch_mla': 'new_m', 'new_m_w_branch_fox': 'new_m', 'new_m_w_out': 'new_m', 'new_m_ln_mix_g': 'new_m', 'new_m_ln_mix_b': 'new_m', 'new_m_w_ffn_up': 'new_m', 'new_m_conv_w': 'new_m', 'new_m_conv_b': 'new_m', 'new_m_w_ffn_down': 'new_m', 'new_m_ln_ffn_g': 'new_m', 'new_m_ln_ffn_b': 'new_m', 'new_v_meta_tokens': 'new_v', 'new_v_ln_emb_g': 'new_v', 'new_v_ln_emb_b': 'new_v', 'new_v_w_in': 'new_v', 'new_v_b_gate': 'new_v', 'new_v_b_forget': 'new_v', 'new_v_q_norm_g': 'new_v', 'new_v_w_q_up': 'new_v', 'new_v_kv_norm_g': 'new_v', 'new_v_w_kv_up': 'new_v', 'new_v_w_branch_mla': 'new_v', 'new_v_w_branch_fox': 'new_v', 'new_v_w_out': 'new_v', 'new_v_ln_mix_g': 'new_v', 'new_v_ln_mix_b': 'new_v', 'new_v_w_ffn_up': 'new_v', 'new_v_conv_w': 'new_v', 'new_v_conv_b': 'new_v', 'new_v_w_ffn_down': 'new_v', 'new_v_ln_ffn_g': 'new_v', 'new_v_ln_ffn_b': 'new_v'}


def _forward(args):
    return _fwd_reference(*[args[k] for k in FWD_PARAMS])


def _output_shape():
    out = _jax.eval_shape(lambda: _forward(_fwd_setup_inputs(0)))
    return out.shape, out.dtype

N_MICROBATCH = 1
ADAM_LR = 0.001
ADAM_B1 = 0.9
ADAM_B2 = 0.999
ADAM_EPS = 1e-08
ADAM_WD = 0.01
ADAM_STEP = 10
PER_EXAMPLE_BATCH_AXIS = {'x': 0, 'loss_target': 0}
SHARED_INPUTS = []
_WEIGHT_DTYPES = {'meta_tokens': _jnp.float32, 'ln_emb_g': _jnp.float32, 'ln_emb_b': _jnp.float32, 'w_in': _jnp.float32, 'b_gate': _jnp.float32, 'b_forget': _jnp.float32, 'q_norm_g': _jnp.float32, 'w_q_up': _jnp.float32, 'kv_norm_g': _jnp.float32, 'w_kv_up': _jnp.float32, 'w_branch_mla': _jnp.float32, 'w_branch_fox': _jnp.float32, 'w_out': _jnp.float32, 'ln_mix_g': _jnp.float32, 'ln_mix_b': _jnp.float32, 'w_ffn_up': _jnp.float32, 'conv_w': _jnp.float32, 'conv_b': _jnp.float32, 'w_ffn_down': _jnp.float32, 'ln_ffn_g': _jnp.float32, 'ln_ffn_b': _jnp.float32}
MOMENT_SCALE = {'meta_tokens': 8.077264e-04, 'ln_emb_g': 1.466272e+00, 'ln_emb_b': 2.839906e-01, 'w_in': 4.853737e-03, 'b_gate': 1.660570e-03, 'b_forget': 3.194036e-02, 'q_norm_g': 5.632217e-03, 'w_q_up': 3.937499e-03, 'kv_norm_g': 1.480983e-02, 'w_kv_up': 4.734985e-03, 'w_branch_mla': 6.391577e-03, 'w_branch_fox': 7.972884e-03, 'w_out': 1.018751e-02, 'ln_mix_g': 1.474899e+00, 'ln_mix_b': 2.808234e-01, 'w_ffn_up': 1.282619e-02, 'conv_w': 7.801113e-03, 'conv_b': 1.268713e-02, 'w_ffn_down': 2.095104e-02, 'ln_ffn_g': 1.620891e+01, 'ln_ffn_b': 4.578047e-01}


def _to_microbatches(a, axis):
    t = _jnp.moveaxis(a, axis, 0)
    t = t.reshape((N_MICROBATCH, t.shape[0] // N_MICROBATCH) + t.shape[1:])
    return _jnp.moveaxis(t, 1, axis + 1)


def setup_inputs(seed: int = 0) -> dict:
    inp = _fwd_setup_inputs(seed)
    key = _jax.random.fold_in(_jax.random.key(seed), 7919)
    shape, _ = _output_shape()
    out = dict(inp)
    out["loss_target"] = _jax.random.normal(_jax.random.fold_in(key, 0), shape, _jnp.float32)
    for i, name in enumerate(TWIN_WEIGHTS):
        w = inp[name].astype(_jnp.float32)
        if MOMENT_SCALE is None:
            s = _jnp.sqrt(_jnp.mean(_jnp.square(w)) + 1e-30)
        else:
            s = MOMENT_SCALE[name]
        km, kv = _jax.random.split(_jax.random.fold_in(key, i + 1))
        out[name] = w
        out["m_" + name] = s * _jax.random.normal(km, w.shape, _jnp.float32)
        out["v_" + name] = (s * s) * _jax.random.uniform(kv, w.shape, _jnp.float32, 0.5, 1.5)
    if N_MICROBATCH > 1:
        for name, axis in PER_EXAMPLE_BATCH_AXIS.items():
            out[name] = _to_microbatches(out[name], axis)
    return {'x': out['x'], 'meta_tokens': out['meta_tokens'], 'ln_emb_g': out['ln_emb_g'], 'ln_emb_b': out['ln_emb_b'], 'w_in': out['w_in'], 'b_gate': out['b_gate'], 'b_forget': out['b_forget'], 'q_norm_g': out['q_norm_g'], 'w_q_up': out['w_q_up'], 'kv_norm_g': out['kv_norm_g'], 'w_kv_up': out['w_kv_up'], 'w_branch_mla': out['w_branch_mla'], 'w_branch_fox': out['w_branch_fox'], 'w_out': out['w_out'], 'ln_mix_g': out['ln_mix_g'], 'ln_mix_b': out['ln_mix_b'], 'w_ffn_up': out['w_ffn_up'], 'conv_w': out['conv_w'], 'conv_b': out['conv_b'], 'w_ffn_down': out['w_ffn_down'], 'ln_ffn_g': out['ln_ffn_g'], 'ln_ffn_b': out['ln_ffn_b'], 'loss_target': out['loss_target'], 'm_meta_tokens': out['m_meta_tokens'], 'm_ln_emb_g': out['m_ln_emb_g'], 'm_ln_emb_b': out['m_ln_emb_b'], 'm_w_in': out['m_w_in'], 'm_b_gate': out['m_b_gate'], 'm_b_forget': out['m_b_forget'], 'm_q_norm_g': out['m_q_norm_g'], 'm_w_q_up': out['m_w_q_up'], 'm_kv_norm_g': out['m_kv_norm_g'], 'm_w_kv_up': out['m_w_kv_up'], 'm_w_branch_mla': out['m_w_branch_mla'], 'm_w_branch_fox': out['m_w_branch_fox'], 'm_w_out': out['m_w_out'], 'm_ln_mix_g': out['m_ln_mix_g'], 'm_ln_mix_b': out['m_ln_mix_b'], 'm_w_ffn_up': out['m_w_ffn_up'], 'm_conv_w': out['m_conv_w'], 'm_conv_b': out['m_conv_b'], 'm_w_ffn_down': out['m_w_ffn_down'], 'm_ln_ffn_g': out['m_ln_ffn_g'], 'm_ln_ffn_b': out['m_ln_ffn_b'], 'v_meta_tokens': out['v_meta_tokens'], 'v_ln_emb_g': out['v_ln_emb_g'], 'v_ln_emb_b': out['v_ln_emb_b'], 'v_w_in': out['v_w_in'], 'v_b_gate': out['v_b_gate'], 'v_b_forget': out['v_b_forget'], 'v_q_norm_g': out['v_q_norm_g'], 'v_w_q_up': out['v_w_q_up'], 'v_kv_norm_g': out['v_kv_norm_g'], 'v_w_kv_up': out['v_w_kv_up'], 'v_w_branch_mla': out['v_w_branch_mla'], 'v_w_branch_fox': out['v_w_branch_fox'], 'v_w_out': out['v_w_out'], 'v_ln_mix_g': out['v_ln_mix_g'], 'v_ln_mix_b': out['v_ln_mix_b'], 'v_w_ffn_up': out['v_w_ffn_up'], 'v_conv_w': out['v_conv_w'], 'v_conv_b': out['v_conv_b'], 'v_w_ffn_down': out['v_w_ffn_down'], 'v_ln_ffn_g': out['v_ln_ffn_g'], 'v_ln_ffn_b': out['v_ln_ffn_b']}


def _loss(weights, diff, rest, loss_target):
    with _jax.named_scope("forward"):
        args = {**rest, TWIN_DIFF_INPUT: diff, **{k: w.astype(_WEIGHT_DTYPES[k]) for k, w in weights.items()}}
        y = _forward(args)
    with _jax.named_scope("loss_head"):
        err = _jnp.square(y.astype(_jnp.float32) - loss_target)
        return 0.5 * _jnp.sum(_jnp.mean(err, axis=-1)) if err.ndim else 0.5 * err


def _adamw(w, g, m, v):
    m = ADAM_B1 * m + (1.0 - ADAM_B1) * g
    v = ADAM_B2 * v + (1.0 - ADAM_B2) * _jnp.square(g)
    m_hat = m / (1.0 - ADAM_B1 ** ADAM_STEP)
    v_hat = v / (1.0 - ADAM_B2 ** ADAM_STEP)
    delta = -ADAM_LR * (m_hat / (_jnp.sqrt(v_hat) + ADAM_EPS) + ADAM_WD * w)
    return delta, m, v


def reference(x, meta_tokens, ln_emb_g, ln_emb_b, w_in, b_gate, b_forget, q_norm_g, w_q_up, kv_norm_g, w_kv_up, w_branch_mla, w_branch_fox, w_out, ln_mix_g, ln_mix_b, w_ffn_up, conv_w, conv_b, w_ffn_down, ln_ffn_g, ln_ffn_b, loss_target, m_meta_tokens, m_ln_emb_g, m_ln_emb_b, m_w_in, m_b_gate, m_b_forget, m_q_norm_g, m_w_q_up, m_kv_norm_g, m_w_kv_up, m_w_branch_mla, m_w_branch_fox, m_w_out, m_ln_mix_g, m_ln_mix_b, m_w_ffn_up, m_conv_w, m_conv_b, m_w_ffn_down, m_ln_ffn_g, m_ln_ffn_b, v_meta_tokens, v_ln_emb_g, v_ln_emb_b, v_w_in, v_b_gate, v_b_forget, v_q_norm_g, v_w_q_up, v_kv_norm_g, v_w_kv_up, v_w_branch_mla, v_w_branch_fox, v_w_out, v_ln_mix_g, v_ln_mix_b, v_w_ffn_up, v_conv_w, v_conv_b, v_w_ffn_down, v_ln_ffn_g, v_ln_ffn_b):
    given = dict(x=x, meta_tokens=meta_tokens, ln_emb_g=ln_emb_g, ln_emb_b=ln_emb_b, w_in=w_in, b_gate=b_gate, b_forget=b_forget, q_norm_g=q_norm_g, w_q_up=w_q_up, kv_norm_g=kv_norm_g, w_kv_up=w_kv_up, w_branch_mla=w_branch_mla, w_branch_fox=w_branch_fox, w_out=w_out, ln_mix_g=ln_mix_g, ln_mix_b=ln_mix_b, w_ffn_up=w_ffn_up, conv_w=conv_w, conv_b=conv_b, w_ffn_down=w_ffn_down, ln_ffn_g=ln_ffn_g, ln_ffn_b=ln_ffn_b, loss_target=loss_target, m_meta_tokens=m_meta_tokens, m_ln_emb_g=m_ln_emb_g, m_ln_emb_b=m_ln_emb_b, m_w_in=m_w_in, m_b_gate=m_b_gate, m_b_forget=m_b_forget, m_q_norm_g=m_q_norm_g, m_w_q_up=m_w_q_up, m_kv_norm_g=m_kv_norm_g, m_w_kv_up=m_w_kv_up, m_w_branch_mla=m_w_branch_mla, m_w_branch_fox=m_w_branch_fox, m_w_out=m_w_out, m_ln_mix_g=m_ln_mix_g, m_ln_mix_b=m_ln_mix_b, m_w_ffn_up=m_w_ffn_up, m_conv_w=m_conv_w, m_conv_b=m_conv_b, m_w_ffn_down=m_w_ffn_down, m_ln_ffn_g=m_ln_ffn_g, m_ln_ffn_b=m_ln_ffn_b, v_meta_tokens=v_meta_tokens, v_ln_emb_g=v_ln_emb_g, v_ln_emb_b=v_ln_emb_b, v_w_in=v_w_in, v_b_gate=v_b_gate, v_b_forget=v_b_forget, v_q_norm_g=v_q_norm_g, v_w_q_up=v_w_q_up, v_kv_norm_g=v_kv_norm_g, v_w_kv_up=v_w_kv_up, v_w_branch_mla=v_w_branch_mla, v_w_branch_fox=v_w_branch_fox, v_w_out=v_w_out, v_ln_mix_g=v_ln_mix_g, v_ln_mix_b=v_ln_mix_b, v_w_ffn_up=v_w_ffn_up, v_conv_w=v_conv_w, v_conv_b=v_conv_b, v_w_ffn_down=v_w_ffn_down, v_ln_ffn_g=v_ln_ffn_g, v_ln_ffn_b=v_ln_ffn_b)
    weights = {n: given[n] for n in TWIN_WEIGHTS}
    shared = {n: given[n] for n in SHARED_INPUTS}
    per_example = {n: given[n] for n in ['x']}
    grad_fn = _jax.value_and_grad(_loss, argnums=(0, 1))

    def one_microbatch(ex, loss_target):
        ex = dict(ex)
        diff = ex.pop(TWIN_DIFF_INPUT)
        return grad_fn(weights, diff, {**shared, **ex}, loss_target)

    if N_MICROBATCH == 1:
        loss, (grad_w, grad_x) = one_microbatch(per_example, given["loss_target"])
    else:
        def body(carry, xs):
            loss_sum, grad_sum = carry
            l_k, (gw_k, gx_k) = one_microbatch(xs[0], xs[1])
            with _jax.named_scope("update"):
                return (loss_sum + l_k, _jax.tree.map(_jnp.add, grad_sum, gw_k)), gx_k

        init = (_jnp.zeros((), _jnp.float32), _jax.tree.map(_jnp.zeros_like, weights))
        (loss, grad_w), grad_x = _jax.lax.scan(body, init, (per_example, given["loss_target"]))
    with _jax.named_scope("update"):
        delta_w, new_m, new_v = {}, {}, {}
        for n in TWIN_WEIGHTS:
            delta_w[n], new_m[n], new_v[n] = _adamw(weights[n], grad_w[n], given["m_" + n], given["v_" + n])
    return (loss, grad_x, *[grad_w[n] for n in TWIN_WEIGHTS], *[delta_w[n] for n in TWIN_WEIGHTS],
            *[new_m[n] for n in TWIN_WEIGHTS], *[new_v[n] for n in TWIN_WEIGHTS])
```

```python
import functools

import jax
import jax.numpy as jnp
from jax import lax
from jax.experimental import pallas as pl
from jax.experimental.pallas import tpu as pltpu

F32 = jnp.float32
BF16 = jnp.bfloat16
MXU_DTYPE = BF16

N_DEV = 8
N_META = 16
D_MODEL = 1024
HEADS = 8
Q_RANK = 384
KV_RANK = 128
NOPE = 64
ROPE = 32
HALF = ROPE // 2
QK_DIM = NOPE + ROPE
V_DIM = 64
FOX_DIM = 64
FOX_W = HEADS * FOX_DIM
D_FF = 2816
ROPE_THETA = 10000.0
LN_EPS = 1e-5
RMS_EPS = 1e-6
ALPHA = 2.0 ** 0.25
NEG_INF = -1e30
IN_TOTAL = Q_RANK + KV_RANK + ROPE + 3 * FOX_W + HEADS + 2 * D_MODEL
C_QLAT = 0
C_KVLAT = Q_RANK
C_FQ = Q_RANK + KV_RANK
C_FK = C_FQ + FOX_W
C_FV = C_FK + FOX_W
C_GATE = C_FV + FOX_W
C_LAST = C_GATE + 2 * D_MODEL
IN_PAD = C_LAST + 128
LANE_KR = 0
LANE_FL = ROPE

ADAM_LR = 0.001
ADAM_B1 = 0.9
ADAM_B2 = 0.999
ADAM_EPS = 1e-08
ADAM_WD = 0.01
ADAM_STEP = 10

ROW_BLOCK = 128
VMEM_LIMIT = 56 * 1024 * 1024
HIGHEST = lax.Precision.HIGHEST


def _params(sem=None):
    return pltpu.CompilerParams(dimension_semantics=sem, vmem_limit_bytes=VMEM_LIMIT)


def _call(name, body, grid, ins, outs, scratch=(), sem=None):
    res = pl.pallas_call(
        body, name=name, grid=grid,
        in_specs=[s for _, s in ins],
        out_specs=[s for _, s in outs],
        out_shape=[o for o, _ in outs],
        scratch_shapes=list(scratch),
        compiler_params=_params(sem),
    )(*[a for a, _ in ins])
    return res


def _sds(shape, dtype):
    return jax.ShapeDtypeStruct(shape, dtype)


def _rows(br, c, cb=0):
    return pl.BlockSpec((br, c), lambda i: (i, cb))


def _whole(shape):
    n = len(shape)
    return pl.BlockSpec(shape, lambda i: (0,) * n)


def _pick(dim, cap, mult):
    best = None
    d = mult
    while d <= min(dim, cap):
        if dim % d == 0:
            best = d
        d += mult
    return best if best is not None else dim


def _matmul(name, a, b, *, ta=False, tb=False, out_dtype=F32, addend=None, alpha=1.0):
    if ta:
        k, m = a.shape
    else:
        m, k = a.shape
    if tb:
        n, k2 = b.shape
    else:
        k2, n = b.shape
    assert k == k2, (name, a.shape, b.shape)
    bm = _pick(m, 1088, 128 if ta else 16)
    bn = _pick(n, 512, 128)
    bk = _pick(k, 1088, 128 if (not ta or tb) else 16)
    nk = k // bk
    dims = (((0 if ta else 1,), (1 if tb else 0,)), ((), ()))
    has_add = addend is not None

    def body(*refs):
        if has_add:
            a_ref, b_ref, add_ref, o_ref, acc_ref = refs
        else:
            a_ref, b_ref, o_ref, acc_ref = refs
        kk = pl.program_id(2)

        @pl.when(kk == 0)
        def _():
            acc_ref[...] = jnp.zeros_like(acc_ref)

        acc_ref[...] += lax.dot_general(a_ref[...], b_ref[...], dims, preferred_element_type=F32)

        @pl.when(kk == nk - 1)
        def _():
            r = acc_ref[...]
            if has_add:
                r = r + alpha * add_ref[...]
            o_ref[...] = r.astype(o_ref.dtype)

    a_spec = pl.BlockSpec((bk, bm), lambda i, j, l: (l, i)) if ta else pl.BlockSpec((bm, bk), lambda i, j, l: (i, l))
    b_spec = pl.BlockSpec((bn, bk), lambda i, j, l: (j, l)) if tb else pl.BlockSpec((bk, bn), lambda i, j, l: (l, j))
    o_spec = pl.BlockSpec((bm, bn), lambda i, j, l: (i, j))
    ins = [(a, a_spec), (b, b_spec)]
    if has_add:
        ins.append((addend, o_spec))
    return _call(name, body, (m // bm, n // bn, nk), ins, [(_sds((m, n), out_dtype), o_spec)],
                 scratch=[pltpu.VMEM((bm, bn), F32)], sem=("parallel", "parallel", "arbitrary"))[0]


def _ln_stats(z):
    mu = jnp.mean(z, axis=-1, keepdims=True)
    zc = z - mu
    var = jnp.mean(zc * zc, axis=-1, keepdims=True)
    rstd = lax.rsqrt(var + LN_EPS)
    return zc * rstd, rstd


def _ln_fwd(name, a, res, g, b):
    r, d = a.shape
    br = ROW_BLOCK
    has_res = res is not None

    def body(*refs):
        if has_res:
            a_ref, r_ref, g_ref, b_ref, y_ref, yb_ref = refs
            z = ALPHA * a_ref[...] + r_ref[...]
        else:
            a_ref, g_ref, b_ref, y_ref, yb_ref = refs
            z = a_ref[...]
        xhat, _ = _ln_stats(z)
        y = xhat * g_ref[...] + b_ref[...]
        y_ref[...] = y
        yb_ref[...] = y.astype(yb_ref.dtype)

    ins = [(a, _rows(br, d))]
    if has_res:
        ins.append((res, _rows(br, d)))
    ins += [(g.reshape(1, d), _whole((1, d))), (b.reshape(1, d), _whole((1, d)))]
    outs = [(_sds((r, d), F32), _rows(br, d)), (_sds((r, d), MXU_DTYPE), _rows(br, d))]
    return _call(name, body, (r // br,), ins, outs, sem=("parallel",))


def _ln_bwd(name, a, res, dy, g):
    r, d = a.shape
    br = ROW_BLOCK
    has_res = res is not None

    def body(*refs):
        if has_res:
            a_ref, r_ref, dy_ref, g_ref, dz_ref, dzb_ref, dg_ref, db_ref = refs
            z = ALPHA * a_ref[...] + r_ref[...]
        else:
            a_ref, dy_ref, g_ref, dz_ref, dzb_ref, dg_ref, db_ref = refs
            z = a_ref[...]
        xhat, rstd = _ln_stats(z)
        dyv = dy_ref[...]
        dyg = dyv * g_ref[...]
        m1 = jnp.mean(dyg, axis=-1, keepdims=True)
        m2 = jnp.mean(dyg * xhat, axis=-1, keepdims=True)
        dz = rstd * (dyg - m1 - xhat * m2)
        dz_ref[...] = dz
        dzb_ref[...] = dz.astype(dzb_ref.dtype)

        @pl.when(pl.program_id(0) == 0)
        def _():
            dg_ref[...] = jnp.zeros_like(dg_ref)
            db_ref[...] = jnp.zeros_like(db_ref)

        dg_ref[...] += jnp.sum(dyv * xhat, axis=0, keepdims=True)
        db_ref[...] += jnp.sum(dyv, axis=0, keepdims=True)

    ins = [(a, _rows(br, d))]
    if has_res:
        ins.append((res, _rows(br, d)))
    ins += [(dy, _rows(br, d)), (g.reshape(1, d), _whole((1, d)))]
    outs = [(_sds((r, d), F32), _rows(br, d)), (_sds((r, d), MXU_DTYPE), _rows(br, d)),
            (_sds((1, d), F32), _whole((1, d))), (_sds((1, d), F32), _whole((1, d)))]
    return _call(name, body, (r // br,), ins, outs, sem=("arbitrary",))


def _rms_fwd(name, proj, cb, width, g):
    r = proj.shape[0]
    br = ROW_BLOCK

    def body(x_ref, g_ref, y_ref):
        x = x_ref[...]
        rstd = lax.rsqrt(jnp.mean(x * x, axis=-1, keepdims=True) + RMS_EPS)
        y_ref[...] = (x * rstd * g_ref[...]).astype(y_ref.dtype)

    return _call(name, body, (r // br,), [(proj, _rows(br, width, cb)), (g.reshape(1, width), _whole((1, width)))],
                 [(_sds((r, width), MXU_DTYPE), _rows(br, width))], sem=("parallel",))[0]


def _rms_bwd(name, proj, cb, width, dy, g):
    r = proj.shape[0]
    br = ROW_BLOCK

    def body(x_ref, dy_ref, g_ref, dx_ref, dg_ref):
        x = x_ref[...]
        rstd = lax.rsqrt(jnp.mean(x * x, axis=-1, keepdims=True) + RMS_EPS)
        nrm = x * rstd
        dyv = dy_ref[...]
        dyg = dyv * g_ref[...]
        dx = rstd * (dyg - nrm * jnp.mean(dyg * nrm, axis=-1, keepdims=True))
        dx_ref[...] = dx.astype(dx_ref.dtype)

        @pl.when(pl.program_id(0) == 0)
        def _():
            dg_ref[...] = jnp.zeros_like(dg_ref)

        dg_ref[...] += jnp.sum(dyv * nrm, axis=0, keepdims=True)

    return _call(name, body, (r // br,),
                 [(proj, _rows(br, width, cb)), (dy, _rows(br, width)), (g.reshape(1, width), _whole((1, width)))],
                 [(_sds((r, width), MXU_DTYPE), _rows(br, width)), (_sds((1, width), F32), _whole((1, width)))],
                 sem=("arbitrary",))


def _lane_iota(shape):
    return lax.broadcasted_iota(jnp.int32, shape, 1)


def _rope_fwd(q, proj, cos_t, sin_t):
    r = q.shape[0]
    br = ROW_BLOCK
    last_cb = C_LAST // 128

    def body(q1_ref, q2_ref, t_ref, c_ref, s_ref, o1_ref, o2_ref, k_ref):
        c = c_ref[...]
        s = s_ref[...]
        q1 = q1_ref[...]
        q2 = q2_ref[...]
        o1_ref[...] = q1 * c - q2 * s
        o2_ref[...] = q2 * c + q1 * s
        t = t_ref[...]
        lane = _lane_iota(t.shape)
        second = pltpu.roll(t, 128 - HALF, axis=1)
        first = pltpu.roll(t, HALF, axis=1)
        rot = jnp.where(lane < HALF, t * c - second * s, t * c + first * s)
        k_ref[...] = jnp.where(lane < ROPE, rot, 0.0)

    blk = _rows(br, 128)
    return _call("rope_fwd", body, (r // br,),
                 [(q, _rows(br, 128, 4)), (q, _rows(br, 128, 5)), (proj, _rows(br, 128, last_cb)),
                  (cos_t, blk), (sin_t, blk)],
                 [(_sds((r, 128), F32), blk)] * 3, sem=("parallel",))


def _rope_bwd(dq_nope, g1, g2, dkpe, dfl, cos_t, sin_t):
    r = g1.shape[0]
    br = ROW_BLOCK

    def body(n_ref, g1_ref, g2_ref, kp_ref, fl_ref, c_ref, s_ref, dq_ref, dl_ref):
        c = c_ref[...]
        s = s_ref[...]
        a = g1_ref[...]
        b = g2_ref[...]
        dq_ref[:, 0:HEADS * NOPE] = n_ref[...].astype(dq_ref.dtype)
        dq_ref[:, HEADS * NOPE:HEADS * NOPE + 128] = (a * c + b * s).astype(dq_ref.dtype)
        dq_ref[:, HEADS * NOPE + 128:HEADS * NOPE + 256] = (b * c - a * s).astype(dq_ref.dtype)
        t = kp_ref[0]
        for h in range(1, HEADS):
            t = t + kp_ref[h]
        lane = _lane_iota(t.shape)
        second = pltpu.roll(t, 128 - HALF, axis=1)
        first = pltpu.roll(t, HALF, axis=1)
        rot = jnp.where(lane < HALF, t * c + second * s, t * c - first * s)
        dl_ref[...] = (jnp.where(lane < ROPE, rot, 0.0) + fl_ref[...]).astype(dl_ref.dtype)

    blk = _rows(br, 128)
    return _call("rope_bwd", body, (r // br,),
                 [(dq_nope, _rows(br, HEADS * NOPE)), (g1, blk), (g2, blk),
                  (dkpe, pl.BlockSpec((HEADS, br, 128), lambda i: (0, i, 0))), (dfl, blk), (cos_t, blk), (sin_t, blk)],
                 [(_sds((r, HEADS * QK_DIM), MXU_DTYPE), _rows(br, HEADS * QK_DIM)), (_sds((r, 128), MXU_DTYPE), blk)],
                 sem=("parallel",))


def _log_sigmoid(x):
    return jnp.minimum(x, 0.0) - jnp.log(1.0 + jnp.exp(-jnp.abs(x)))


def _forget_fwd(proj, bf_row):
    r = proj.shape[0]
    br = ROW_BLOCK
    last_cb = C_LAST // 128

    def body(t_ref, b_ref, o_ref, carry_ref):
        @pl.when(pl.program_id(0) == 0)
        def _():
            carry_ref[...] = jnp.zeros_like(carry_ref)

        x = t_ref[...] + b_ref[...]
        lane = _lane_iota(x.shape)
        lf = jnp.where((lane >= LANE_FL) & (lane < LANE_FL + HEADS), _log_sigmoid(x), 0.0)
        tri = (lax.broadcasted_iota(jnp.int32, (br, br), 0) >= lax.broadcasted_iota(jnp.int32, (br, br), 1)).astype(F32)
        cum = jnp.dot(tri, lf, precision=HIGHEST, preferred_element_type=F32) + carry_ref[0:1, :]
        o_ref[...] = cum
        carry_ref[...] = jnp.broadcast_to(cum[br - 1:br, :], carry_ref.shape)

    return _call("forget_fwd", body, (r // br,),
                 [(proj, _rows(br, 128, last_cb)), (bf_row, _whole((1, 128)))],
                 [(_sds((r, 128), F32), _rows(br, 128))], scratch=[pltpu.VMEM((8, 128), F32)], sem=("arbitrary",))[0]


def _forget_bwd(proj, bf_row, dcum):
    r = proj.shape[0]
    br = ROW_BLOCK
    nb = r // br
    last_cb = C_LAST // 128

    def body(t_ref, b_ref, dc_ref, o_ref, db_ref, carry_ref):
        @pl.when(pl.program_id(0) == 0)
        def _():
            carry_ref[...] = jnp.zeros_like(carry_ref)
            db_ref[...] = jnp.zeros_like(db_ref)

        triu = (lax.broadcasted_iota(jnp.int32, (br, br), 0) <= lax.broadcasted_iota(jnp.int32, (br, br), 1)).astype(F32)
        dlf = jnp.dot(triu, dc_ref[...], precision=HIGHEST, preferred_element_type=F32) + carry_ref[0:1, :]
        carry_ref[...] = jnp.broadcast_to(dlf[0:1, :], carry_ref.shape)
        x = t_ref[...] + b_ref[...]
        lane = _lane_iota(x.shape)
        dfl = jnp.where((lane >= LANE_FL) & (lane < LANE_FL + HEADS), dlf * jax.nn.sigmoid(-x), 0.0)
        o_ref[...] = dfl
        db_ref[...] += jnp.sum(dfl, axis=0, keepdims=True)

    rev = pl.BlockSpec((br, 128), lambda i: (nb - 1 - i, 0))
    return _call("forget_bwd", body, (nb,),
                 [(proj, pl.BlockSpec((br, 128), lambda i: (nb - 1 - i, last_cb))), (bf_row, _whole((1, 128))), (dcum, rev)],
                 [(_sds((r, 128), F32), rev), (_sds((1, 128), F32), _whole((1, 128)))],
                 scratch=[pltpu.VMEM((8, 128), F32)], sem=("arbitrary",))


def _gate_fwd(proj, b_gate, bm, bfx):
    r, d = bm.shape
    br = ROW_BLOCK
    cb = C_GATE // d

    def body(gm_ref, gf_ref, b1_ref, b2_ref, bm_ref, bf_ref, o_ref):
        g1 = jax.nn.sigmoid(gm_ref[...] + b1_ref[...])
        g2 = jax.nn.sigmoid(gf_ref[...] + b2_ref[...])
        o_ref[...] = (g1 * bm_ref[...] + g2 * bf_ref[...]).astype(o_ref.dtype)

    b1 = b_gate[:d].reshape(1, d)
    b2 = b_gate[d:].reshape(1, d)
    return _call("gate_fwd", body, (r // br,),
                 [(proj, _rows(br, d, cb)), (proj, _rows(br, d, cb + 1)), (b1, _whole((1, d))), (b2, _whole((1, d))),
                  (bm, _rows(br, d)), (bfx, _rows(br, d))],
                 [(_sds((r, d), MXU_DTYPE), _rows(br, d))], sem=("parallel",))[0]


def _gate_bwd(proj, b_gate, bm, bfx, dmerged):
    r, d = bm.shape
    br = ROW_BLOCK
    cb = C_GATE // d

    def body(gm_ref, gf_ref, b1_ref, b2_ref, bm_ref, bf_ref, dm_ref, dbm_ref, dbf_ref, dgl_ref, dbg_ref):
        g1 = jax.nn.sigmoid(gm_ref[...] + b1_ref[...])
        g2 = jax.nn.sigmoid(gf_ref[...] + b2_ref[...])
        dm = dm_ref[...]
        dbm_ref[...] = (dm * g1).astype(dbm_ref.dtype)
        dbf_ref[...] = (dm * g2).astype(dbf_ref.dtype)
        dl1 = dm * bm_ref[...] * (g1 * (1.0 - g1))
        dl2 = dm * bf_ref[...] * (g2 * (1.0 - g2))
        dgl_ref[:, 0:d] = dl1.astype(dgl_ref.dtype)
        dgl_ref[:, d:2 * d] = dl2.astype(dgl_ref.dtype)

        @pl.when(pl.program_id(0) == 0)
        def _():
            dbg_ref[...] = jnp.zeros_like(dbg_ref)

        dbg_ref[:, 0:d] += jnp.sum(dl1, axis=0, keepdims=True)
        dbg_ref[:, d:2 * d] += jnp.sum(dl2, axis=0, keepdims=True)

    b1 = b_gate[:d].reshape(1, d)
    b2 = b_gate[d:].reshape(1, d)
    return _call("gate_bwd", body, (r // br,),
                 [(proj, _rows(br, d, cb)), (proj, _rows(br, d, cb + 1)), (b1, _whole((1, d))), (b2, _whole((1, d))),
                  (bm, _rows(br, d)), (bfx, _rows(br, d)), (dmerged, _rows(br, d))],
                 [(_sds((r, d), MXU_DTYPE), _rows(br, d)), (_sds((r, d), MXU_DTYPE), _rows(br, d)),
                  (_sds((r, 2 * d), MXU_DTYPE), _rows(br, 2 * d)), (_sds((1, 2 * d), F32), _whole((1, 2 * d)))],
                 sem=("arbitrary",))


def _conv_taps(gp, halo, first_block):
    br = gp.shape[0]
    halo = jnp.where(first_block, 0.0, halo)
    rid = lax.broadcasted_iota(jnp.int32, gp.shape, 0)
    g1 = jnp.where(rid == 0, halo[7:8, :], pltpu.roll(gp, 1, axis=0))
    g2 = jnp.where(rid == 0, halo[6:7, :], jnp.where(rid == 1, halo[7:8, :], pltpu.roll(gp, 2, axis=0)))
    return g1, g2


def _prev_halo(br, c):
    return pl.BlockSpec((8, c), lambda i: (jnp.maximum(i * (br // 8) - 1, 0), 0))


def _glu_fwd(up, conv_w, conv_b):
    r = up.shape[0]
    c = D_FF
    br = ROW_BLOCK

    def body(gp_ref, halo_ref, val_ref, w_ref, b_ref, o_ref):
        gp = gp_ref[...]
        g1, g2 = _conv_taps(gp, halo_ref[...], pl.program_id(0) == 0)
        gate = w_ref[0:1, :] * g2 + w_ref[1:2, :] * g1 + w_ref[2:3, :] * gp + b_ref[...]
        o_ref[...] = (gate * jax.nn.sigmoid(gate) * val_ref[...]).astype(o_ref.dtype)

    return _call("glu_fwd", body, (r // br,),
                 [(up, _rows(br, c, 0)), (up, _prev_halo(br, c)), (up, _rows(br, c, 1)),
                  (conv_w, _whole((3, c))), (conv_b.reshape(1, c), _whole((1, c)))],
                 [(_sds((r, c), MXU_DTYPE), _rows(br, c))], sem=("parallel",))[0]


def _glu_bwd_gate(up, conv_w, conv_b, d_act):
    r = up.shape[0]
    c = D_FF
    br = ROW_BLOCK

    def body(gp_ref, halo_ref, val_ref, w_ref, b_ref, da_ref, dg_ref, dv_ref, dw_ref, db_ref):
        gp = gp_ref[...]
        g1, g2 = _conv_taps(gp, halo_ref[...], pl.program_id(0) == 0)
        gate = w_ref[0:1, :] * g2 + w_ref[1:2, :] * g1 + w_ref[2:3, :] * gp + b_ref[...]
        sg = jax.nn.sigmoid(gate)
        da = da_ref[...]
        dv_ref[...] = (da * (gate * sg)).astype(dv_ref.dtype)
        dg = da * val_ref[...] * (sg * (1.0 + gate * (1.0 - sg)))
        dg_ref[...] = dg

        @pl.when(pl.program_id(0) == 0)
        def _():
            dw_ref[...] = jnp.zeros_like(dw_ref)
            db_ref[...] = jnp.zeros_like(db_ref)

        dw_ref[0:1, :] += jnp.sum(dg * g2, axis=0, keepdims=True)
        dw_ref[1:2, :] += jnp.sum(dg * g1, axis=0, keepdims=True)
        dw_ref[2:3, :] += jnp.sum(dg * gp, axis=0, keepdims=True)
        db_ref[...] += jnp.sum(dg, axis=0, keepdims=True)

    return _call("glu_bwd_gate", body, (r // br,),
                 [(up, _rows(br, c, 0)), (up, _prev_halo(br, c)), (up, _rows(br, c, 1)),
                  (conv_w, _whole((3, c))), (conv_b.reshape(1, c), _whole((1, c))), (d_act, _rows(br, c))],
                 [(_sds((r, c), F32), _rows(br, c)), (_sds((r, c), MXU_DTYPE), _rows(br, c)),
                  (_sds((8, c), F32), _whole((8, c))), (_sds((1, c), F32), _whole((1, c)))],
                 sem=("arbitrary",))


def _glu_bwd_conv(dg, dval, conv_w):
    r, c = dg.shape
    br = ROW_BLOCK
    nb = r // br

    def body(dg_ref, nxt_ref, dv_ref, w_ref, o_ref):
        x = dg_ref[...]
        nxt = jnp.where(pl.program_id(0) == nb - 1, 0.0, nxt_ref[...])
        rid = lax.broadcasted_iota(jnp.int32, x.shape, 0)
        u1 = jnp.where(rid == br - 1, nxt[0:1, :], pltpu.roll(x, br - 1, axis=0))
        u2 = jnp.where(rid == br - 1, nxt[1:2, :], jnp.where(rid == br - 2, nxt[0:1, :], pltpu.roll(x, br - 2, axis=0)))
        dgp = w_ref[2:3, :] * x + w_ref[1:2, :] * u1 + w_ref[0:1, :] * u2
        o_ref[:, 0:c] = dgp.astype(o_ref.dtype)
        o_ref[:, c:2 * c] = dv_ref[...]

    nxt_spec = pl.BlockSpec((8, c), lambda i: (jnp.minimum((i + 1) * (br // 8), r // 8 - 1), 0))
    return _call("glu_bwd_conv", body, (nb,),
                 [(dg, _rows(br, c)), (dg, nxt_spec), (dval, _rows(br, c)), (conv_w, _whole((3, c)))],
                 [(_sds((r, 2 * c), MXU_DTYPE), _rows(br, 2 * c))], sem=("parallel",))[0]


def _loss(h2, tgt, seq):
    r, d = h2.shape
    br = ROW_BLOCK

    def body(h_ref, t_ref, l_ref, d_ref):
        rid = lax.broadcasted_iota(jnp.int32, (br, d), 0) + pl.program_id(0) * br
        valid = (rid >= N_META) & (rid < N_META + seq)
        err = jnp.where(valid, h_ref[...] - t_ref[...], 0.0)
        d_ref[...] = err * (1.0 / d)

        @pl.when(pl.program_id(0) == 0)
        def _():
            l_ref[...] = jnp.zeros_like(l_ref)

        l_ref[...] += jnp.sum(jnp.sum(err * err, axis=1, keepdims=True), axis=0, keepdims=True) * (0.5 / d)

    return _call("loss", body, (r // br,), [(h2, _rows(br, d)), (tgt, _rows(br, d))],
                 [(_sds((1, 1), F32), _whole((1, 1))), (_sds((r, d), F32), _rows(br, d))], sem=("arbitrary",))


def _scores(q, k, scale, cq, ck, qi, kj, t):
    s = lax.dot_general(q, k, (((1,), (1,)), ((), ())), preferred_element_type=F32) * scale
    if cq is not None:
        s = s + (cq - ck)
    qpos = qi * t + lax.broadcasted_iota(jnp.int32, (t, t), 0)
    kpos = kj * t + lax.broadcasted_iota(jnp.int32, (t, t), 1)
    return jnp.where(kpos <= qpos, s, NEG_INF)


def _attn_fwd(name, q, k, v, scale, cq=None, ck=None):
    h, r, dk = q.shape
    dv = v.shape[-1]
    t = ROW_BLOCK
    nb = r // t
    bias = cq is not None

    def body(*refs):
        if bias:
            q_ref, k_ref, v_ref, cq_ref, ck_ref, o_ref, lse_ref = refs
        else:
            q_ref, k_ref, v_ref, o_ref, lse_ref = refs
        i = pl.program_id(1)
        qv = q_ref[...]
        cqv = cq_ref[...] if bias else None

        def step(j, carry):
            m, l, acc = carry
            off = pl.multiple_of(j * t, t)
            ks = k_ref[pl.ds(off, t), :]
            vs = v_ref[pl.ds(off, t), :]
            ckv = ck_ref[pl.ds(j, 1), :] if bias else None
            s = _scores(qv, ks, scale, cqv, ckv, i, j, t)
            m_new = jnp.maximum(m, jnp.max(s, axis=1, keepdims=True))
            p = jnp.exp(s - m_new)
            a = jnp.exp(m - m_new)
            l = a * l + jnp.sum(p, axis=1, keepdims=True)
            acc = a * acc + jnp.dot(p.astype(vs.dtype), vs, preferred_element_type=F32)
            return m_new, l, acc

        init = (jnp.full((t, 1), NEG_INF, F32), jnp.zeros((t, 1), F32), jnp.zeros((t, dv), F32))
        m, l, acc = lax.fori_loop(0, i + 1, step, init)
        o_ref[...] = acc / l
        lse_ref[...] = m + jnp.log(l)

    blk = lambda d: pl.BlockSpec((None, t, d), lambda hh, i: (hh, i, 0))
    full = lambda n, d: pl.BlockSpec((None, n, d), lambda hh, i: (hh, 0, 0))
    ins = [(q, blk(dk)), (k, full(r, dk)), (v, full(r, dv))]
    if bias:
        ins += [(cq, blk(1)), (ck, full(nb, t))]
    outs = [(_sds((h, r, dv), F32), blk(dv)), (_sds((h, r, 1), F32), blk(1))]
    return _call(name, body, (h, nb), ins, outs, sem=("parallel", "parallel"))


def _attn_bwd(name, q, k, v, o, lse, do, scale, cq=None, ck=None):
    h, r, dk = q.shape
    dv = v.shape[-1]
    t = ROW_BLOCK
    nb = r // t
    bias = cq is not None

    def body(*refs):
        if bias:
            (q_ref, k_ref, v_ref, o_ref, lse_ref, do_ref, cq_ref, ck_ref,
             dq_ref, dk_ref, dv_ref, dcq_ref, dck_ref, delta_ref) = refs
        else:
            q_ref, k_ref, v_ref, o_ref, lse_ref, do_ref, dq_ref, dk_ref, dv_ref, delta_ref = refs
        j = pl.program_id(1)

        @pl.when(j == 0)
        def _():
            delta_ref[...] = jnp.sum(do_ref[...] * o_ref[...], axis=1, keepdims=True)
            dq_ref[...] = jnp.zeros_like(dq_ref)
            if bias:
                dcq_ref[...] = jnp.zeros_like(dcq_ref)

        kv = k_ref[...]
        vv = v_ref[...]
        ckv = ck_ref[pl.ds(j, 1), :] if bias else None

        def step(i, carry):
            dk_acc, dv_acc, dck_acc = carry
            off = pl.multiple_of(i * t, t)
            rows = pl.ds(off, t)
            qv = q_ref[rows, :]
            dov = do_ref[rows, :].astype(vv.dtype)
            cqv = cq_ref[rows, :] if bias else None
            s = _scores(qv, kv, scale, cqv, ckv, i, j, t)
            p = jnp.exp(s - lse_ref[rows, :])
            dp = lax.dot_general(dov, vv, (((1,), (1,)), ((), ())), preferred_element_type=F32)
            ds = p * (dp - delta_ref[rows, :])
            pb = p.astype(vv.dtype)
            dsb = ds.astype(kv.dtype)
            dv_acc = dv_acc + lax.dot_general(pb, dov, (((0,), (0,)), ((), ())), preferred_element_type=F32)
            dk_acc = dk_acc + lax.dot_general(dsb, qv, (((0,), (0,)), ((), ())), preferred_element_type=F32)
            dq_ref[rows, :] += jnp.dot(dsb, kv, preferred_element_type=F32) * scale
            if bias:
                dcq_ref[rows, :] += jnp.sum(ds, axis=1, keepdims=True)
                dck_acc = dck_acc - jnp.sum(ds, axis=0, keepdims=True)
            return dk_acc, dv_acc, dck_acc

        init = (jnp.zeros((t, dk), F32), jnp.zeros((t, dv), F32), jnp.zeros((1, t), F32))
        dk_acc, dv_acc, dck_acc = lax.fori_loop(j, nb, step, init)
        dk_ref[...] = dk_acc * scale
        dv_ref[...] = dv_acc
        if bias:
            dck_ref[pl.ds(j, 1), :] = dck_acc

    blk = lambda d: pl.BlockSpec((None, t, d), lambda hh, jj: (hh, jj, 0))
    full = lambda n, d: pl.BlockSpec((None, n, d), lambda hh, jj: (hh, 0, 0))
    ins = [(q, full(r, dk)), (k, blk(dk)), (v, blk(dv)), (o, full(r, dv)), (lse, full(r, 1)), (do, full(r, dv))]
    outs = [(_sds((h, r, dk), F32), full(r, dk)), (_sds((h, r, dk), F32), blk(dk)), (_sds((h, r, dv), F32), blk(dv))]
    if bias:
        ins += [(cq, full(r, 1)), (ck, full(nb, t))]
        outs += [(_sds((h, r, 1), F32), full(r, 1)), (_sds((h, nb, t), F32), full(nb, t))]
    return _call(name, body, (h, nb), ins, outs, scratch=[pltpu.VMEM((r, 1), F32)], sem=("parallel", "arbitrary"))


MESH_ID = pl.DeviceIdType.MESH
ANY = pl.BlockSpec(memory_space=pl.ANY)


def _allgather(name, shard):
    def body(x_ref, out_ref, send_sems, recv_sems, local_sem):
        x, y, c = lax.axis_index("x"), lax.axis_index("y"), lax.axis_index("c")
        me, sibling = (x, y, c), (x, y, 1 - c)
        chips = [(1 - x, y), (x, 1 - y), (1 - x, 1 - y)]

        def slot(px, py, pc):
            return out_ref.at[4 * px + 2 * py + pc]

        def copy(k, block, to, src=None):
            return pltpu.make_async_remote_copy(
                src_ref=slot(*block) if src is None else src, dst_ref=slot(*block),
                send_sem=send_sems.at[k], recv_sem=recv_sems.at[k], device_id=to, device_id_type=MESH_ID)

        mine = pltpu.make_async_copy(x_ref, slot(*me), local_sem)
        mine.start()
        first = [copy(0, me, sibling, src=x_ref)]
        first += [copy(1 + j, me, (*chip, c), src=x_ref) for j, chip in enumerate(chips)]
        for cp in first:
            cp.start()
        passed = [copy(4 + j, (*chip, c), sibling) for j, chip in enumerate(chips)]
        for j, chip in enumerate(chips):
            copy(1 + j, (*chip, c), me).wait_recv()
            passed[j].start()
        copy(0, sibling, me).wait_recv()
        for j, chip in enumerate(chips):
            copy(4 + j, (*chip, 1 - c), me).wait_recv()
        for cp in first + passed:
            cp.wait_send()
        mine.wait()

    return pl.pallas_call(
        body, name=name, out_shape=_sds((N_DEV,) + shard.shape, shard.dtype),
        in_specs=[ANY], out_specs=ANY,
        scratch_shapes=[pltpu.SemaphoreType.DMA((7,)), pltpu.SemaphoreType.DMA((7,)), pltpu.SemaphoreType.DMA(())],
    )(shard)


def _exchange(name, parts):
    def body(g_ref, out_ref, send_sems, recv_sems, local_sem):
        x, y, c = lax.axis_index("x"), lax.axis_index("y"), lax.axis_index("c")
        my_id = 4 * x + 2 * y + c

        def peer(k):
            return (1 - x if k & 4 else x, 1 - y if k & 2 else y, 1 - c if k & 1 else c)

        def copy(k):
            px, py, pc = peer(k)
            return pltpu.make_async_remote_copy(
                src_ref=g_ref.at[4 * px + 2 * py + pc], dst_ref=out_ref.at[my_id],
                send_sem=send_sems.at[k - 1], recv_sem=recv_sems.at[k - 1],
                device_id=(px, py, pc), device_id_type=MESH_ID)

        def landed(k):
            px, py, pc = peer(k)
            return pltpu.make_async_remote_copy(
                src_ref=g_ref.at[my_id], dst_ref=out_ref.at[4 * px + 2 * py + pc],
                send_sem=send_sems.at[k - 1], recv_sem=recv_sems.at[k - 1],
                device_id=(px, py, pc), device_id_type=MESH_ID)

        mine = pltpu.make_async_copy(g_ref.at[my_id], out_ref.at[my_id], local_sem)
        mine.start()
        sends = [copy(k) for k in range(1, N_DEV)]
        for cp in sends:
            cp.start()
        for k in range(1, N_DEV):
            landed(k).wait_recv()
        for cp in sends:
            cp.wait_send()
        mine.wait()

    return pl.pallas_call(
        body, name=name, out_shape=_sds(parts.shape, parts.dtype),
        in_specs=[ANY], out_specs=ANY,
        scratch_shapes=[pltpu.SemaphoreType.DMA((7,)), pltpu.SemaphoreType.DMA((7,)), pltpu.SemaphoreType.DMA(())],
    )(parts)


def _adamw(name, parts, w, m, v):
    r, c = w.shape
    br = _pick(r, 256, 16)

    def body(p_ref, w_ref, m_ref, v_ref, g_ref, d_ref, nm_ref, nv_ref):
        g = p_ref[0].astype(F32)
        for k in range(1, N_DEV):
            g = g + p_ref[k].astype(F32)
        mm = ADAM_B1 * m_ref[...] + (1.0 - ADAM_B1) * g
        vv = ADAM_B2 * v_ref[...] + (1.0 - ADAM_B2) * (g * g)
        m_hat = mm / (1.0 - ADAM_B1 ** ADAM_STEP)
        v_hat = vv / (1.0 - ADAM_B2 ** ADAM_STEP)
        g_ref[...] = g
        d_ref[...] = -ADAM_LR * (m_hat / (jnp.sqrt(v_hat) + ADAM_EPS) + ADAM_WD * w_ref[...])
        nm_ref[...] = mm
        nv_ref[...] = vv

    spec = _rows(br, c)
    out = (_sds((r, c), F32), spec)
    return _call(name, body, (r // br,),
                 [(parts, pl.BlockSpec((N_DEV, br, c), lambda i: (0, i, 0))), (w, spec), (m, spec), (v, spec)],
                 [out] * 4, sem=("parallel",))


def _heads(x2d, d):
    r = x2d.shape[0]
    return x2d.reshape(r, HEADS, d).transpose(1, 0, 2)


def _unheads(xh):
    h, r, d = xh.shape
    return xh.transpose(1, 0, 2).reshape(r, h * d)


def _local_step(xcat, tgt, seq, p):
    r = xcat.shape[0]
    nb = r // ROW_BLOCK
    cd = MXU_DTYPE

    pos = jnp.arange(r, dtype=F32)
    inv_freq = ROPE_THETA ** (-jnp.arange(HALF, dtype=F32) / HALF)
    ang = pos[:, None] * inv_freq[None, :]
    cos_t = jnp.tile(jnp.cos(ang), (1, 128 // HALF))
    sin_t = jnp.tile(jnp.sin(ang), (1, 128 // HALF))
    bf_row = jnp.zeros((1, 128), F32).at[0, LANE_FL:LANE_FL + HEADS].set(p["b_forget"])

    h0, h0b = _ln_fwd("ln_emb_fwd", xcat, None, p["ln_emb_g"], p["ln_emb_b"])
    proj = _matmul("in_proj", h0b, p["w_in"])
    ql = _rms_fwd("q_norm_fwd", proj, 0, Q_RANK, p["q_norm_g"])
    kvl = _rms_fwd("kv_norm_fwd", proj, C_KVLAT // KV_RANK, KV_RANK, p["kv_norm_g"])
    qfull = _matmul("q_up", ql, p["w_q_up"])
    kvfull = _matmul("kv_up", kvl, p["w_kv_up"])
    qr1, qr2, kpe = _rope_fwd(qfull, proj, cos_t, sin_t)
    q_mla = jnp.concatenate([qfull[:, :HEADS * NOPE].reshape(r, HEADS, NOPE), qr1.reshape(r, HEADS, HALF),
                             qr2.reshape(r, HEADS, HALF)], axis=-1).transpose(1, 0, 2).astype(cd)
    kv3 = kvfull.reshape(r, HEADS, NOPE + V_DIM)
    k_mla = jnp.concatenate([kv3[..., :NOPE], jnp.broadcast_to(kpe[:, None, :ROPE], (r, HEADS, ROPE))],
                            axis=-1).transpose(1, 0, 2).astype(cd)
    v_mla = kv3[..., NOPE:].transpose(1, 0, 2).astype(cd)
    o_mla, lse_mla = _attn_fwd("mla_fwd", q_mla, k_mla, v_mla, QK_DIM ** -0.5)

    cum = _forget_fwd(proj, bf_row)
    cum_t = cum[:, LANE_FL:LANE_FL + HEADS].T
    cq = cum_t[:, :, None]
    ck = cum_t.reshape(HEADS, nb, ROW_BLOCK)
    fq = _heads(proj[:, C_FQ:C_FQ + FOX_W], FOX_DIM).astype(cd)
    fk = _heads(proj[:, C_FK:C_FK + FOX_W], FOX_DIM).astype(cd)
    fv = _heads(proj[:, C_FV:C_FV + FOX_W], FOX_DIM).astype(cd)
    o_fox, lse_fox = _attn_fwd("fox_fwd", fq, fk, fv, FOX_DIM ** -0.5, cq, ck)

    o_mla_b = _unheads(o_mla).astype(cd)
    o_fox_b = _unheads(o_fox).astype(cd)
    bm = _matmul("branch_mla", o_mla_b, p["w_branch_mla"])
    bfx = _matmul("branch_fox", o_fox_b, p["w_branch_fox"])
    merged = _gate_fwd(proj, p["b_gate"], bm, bfx)
    mix = _matmul("out_proj", merged, p["w_out"])
    h1, h1b = _ln_fwd("ln_mix_fwd", h0, mix, p["ln_mix_g"], p["ln_mix_b"])
    up = _matmul("ffn_up", h1b, p["w_ffn_up"])
    act = _glu_fwd(up, p["conv_w"], p["conv_b"])
    f = _matmul("ffn_down", act, p["w_ffn_down"])
    h2, _ = _ln_fwd("ln_ffn_fwd", h1, f, p["ln_ffn_g"], p["ln_ffn_b"])
    loss, dh2 = _loss(h2, tgt, seq)

    g = {}
    dz2, dz2b, g["ln_ffn_g"], g["ln_ffn_b"] = _ln_bwd("ln_ffn_bwd", h1, f, dh2, p["ln_ffn_g"])
    d_act = _matmul("ffn_down_dx", dz2b, p["w_ffn_down"], tb=True)
    g["w_ffn_down"] = _matmul("ffn_down_dw", act, dz2b, ta=True)
    dgate, dval, dcw, g["conv_b"] = _glu_bwd_gate(up, p["conv_w"], p["conv_b"], d_act)
    g["conv_w"] = dcw[:3]
    d_up = _glu_bwd_conv(dgate, dval, p["conv_w"])
    dh1 = _matmul("ffn_up_dx", d_up, p["w_ffn_up"], tb=True, addend=dz2, alpha=ALPHA)
    g["w_ffn_up"] = _matmul("ffn_up_dw", h1b, d_up, ta=True)
    dz1, dz1b, g["ln_mix_g"], g["ln_mix_b"] = _ln_bwd("ln_mix_bwd", h0, mix, dh1, p["ln_mix_g"])
    dmerged = _matmul("out_proj_dx", dz1b, p["w_out"], tb=True)
    g["w_out"] = _matmul("out_proj_dw", merged, dz1b, ta=True)
    d_bm, d_bf, d_gl, dbg = _gate_bwd(proj, p["b_gate"], bm, bfx, dmerged)
    g["b_gate"] = dbg
    d_o_mla = _matmul("branch_mla_dx", d_bm, p["w_branch_mla"], tb=True)
    g["w_branch_mla"] = _matmul("branch_mla_dw", o_mla_b, d_bm, ta=True)
    d_o_fox = _matmul("branch_fox_dx", d_bf, p["w_branch_fox"], tb=True)
    g["w_branch_fox"] = _matmul("branch_fox_dw", o_fox_b, d_bf, ta=True)

    dq_m, dk_m, dv_m = _attn_bwd("mla_bwd", q_mla, k_mla, v_mla, o_mla, lse_mla, _heads(d_o_mla, V_DIM), QK_DIM ** -0.5)
    dfq, dfk, dfv, dcq, dck = _attn_bwd("fox_bwd", fq, fk, fv, o_fox, lse_fox, _heads(d_o_fox, FOX_DIM),
                                        FOX_DIM ** -0.5, cq, ck)

    dcum = (dcq[:, :, 0] + dck.reshape(HEADS, r)).T
    dcum_pad = jnp.pad(dcum, ((0, 0), (LANE_FL, 128 - LANE_FL - HEADS)))
    dfl, dbf = _forget_bwd(proj, bf_row, dcum_pad)
    g["b_forget"] = dbf[:, LANE_FL:LANE_FL + HEADS]

    dq3 = dq_m.transpose(1, 0, 2)
    dk3 = dk_m.transpose(1, 0, 2)
    dkpe = jnp.pad(dk_m[:, :, NOPE:], ((0, 0), (0, 0), (0, 128 - ROPE)))
    dq_b, dlast = _rope_bwd(dq3[..., :NOPE].reshape(r, HEADS * NOPE), dq3[..., NOPE:NOPE + HALF].reshape(r, 128),
                            dq3[..., NOPE + HALF:].reshape(r, 128), dkpe, dfl, cos_t, sin_t)
    dkv_b = jnp.concatenate([dk3[..., :NOPE], dv_m.transpose(1, 0, 2)], axis=-1).reshape(r, HEADS * (NOPE + V_DIM)).astype(cd)
    d_ql = _matmul("q_up_dx", dq_b, p["w_q_up"], tb=True)
    g["w_q_up"] = _matmul("q_up_dw", ql, dq_b, ta=True)
    d_kvl = _matmul("kv_up_dx", dkv_b, p["w_kv_up"], tb=True)
    g["w_kv_up"] = _matmul("kv_up_dw", kvl, dkv_b, ta=True)
    d_qlat, g["q_norm_g"] = _rms_bwd("q_norm_bwd", proj, 0, Q_RANK, d_ql, p["q_norm_g"])
    d_kvlat, g["kv_norm_g"] = _rms_bwd("kv_norm_bwd", proj, C_KVLAT // KV_RANK, KV_RANK, d_kvl, p["kv_norm_g"])
    dproj = jnp.concatenate([d_qlat, d_kvlat, _unheads(dfq).astype(cd), _unheads(dfk).astype(cd),
                             _unheads(dfv).astype(cd), d_gl, dlast], axis=1)
    dh0 = _matmul("in_proj_dx", dproj, p["w_in"], tb=True, addend=dz1, alpha=ALPHA)
    g["w_in"] = _matmul("in_proj_dw", h0b, dproj, ta=True)
    dxcat, _, g["ln_emb_g"], g["ln_emb_b"] = _ln_bwd("ln_emb_bwd", xcat, None, dh0, p["ln_emb_g"])
    return loss, dxcat, g


def _perm_w_in(w):
    o = 0
    seg = {}
    for nm, wd in (("q", Q_RANK), ("kv", KV_RANK), ("kr", ROPE), ("fq", FOX_W), ("fk", FOX_W), ("fv", FOX_W),
                   ("fl", HEADS), ("gate", 2 * D_MODEL)):
        seg[nm] = w[:, o:o + wd]
        o += wd
    pad = jnp.zeros((w.shape[0], IN_PAD - C_LAST - ROPE - HEADS), w.dtype)
    return jnp.concatenate([seg["q"], seg["kv"], seg["fq"], seg["fk"], seg["fv"], seg["gate"], seg["kr"], seg["fl"], pad], axis=1)


def _unperm_w_in(wp):
    last = wp[:, C_LAST:]
    return jnp.concatenate([wp[:, C_QLAT:C_FQ], last[:, :ROPE], wp[:, C_FQ:C_GATE], last[:, ROPE:ROPE + HEADS],
                            wp[:, C_GATE:C_LAST]], axis=1)


def _perm_w_q(w):
    w3 = w.reshape(w.shape[0], HEADS, QK_DIM)
    return jnp.concatenate([w3[..., :NOPE].reshape(-1, HEADS * NOPE), w3[..., NOPE:NOPE + HALF].reshape(-1, 128),
                            w3[..., NOPE + HALF:].reshape(-1, 128)], axis=1)


def _unperm_w_q(wp):
    k = wp.shape[0]
    return jnp.concatenate([wp[:, :HEADS * NOPE].reshape(k, HEADS, NOPE), wp[:, HEADS * NOPE:HEADS * NOPE + 128].reshape(k, HEADS, HALF),
                            wp[:, HEADS * NOPE + 128:].reshape(k, HEADS, HALF)], axis=-1).reshape(k, HEADS * QK_DIM)


BIG = (("w_in", 1), ("w_q_up", 1), ("w_kv_up", 1), ("w_branch_mla", 1), ("w_branch_fox", 1), ("w_out", 0),
       ("w_ffn_up", 1), ("w_ffn_down", 0))
SMALL_SHARDED = (("meta_tokens", 1), ("conv_w", 1))
REPLICATED = ("ln_emb_g", "ln_emb_b", "b_gate", "b_forget", "q_norm_g", "kv_norm_g", "ln_mix_g", "ln_mix_b",
              "conv_b", "ln_ffn_g", "ln_ffn_b")
PACK_COLS = 1024


def _pack_rows(n_elems, mult):
    rows = -(-n_elems // PACK_COLS)
    return -(-rows // mult) * mult


def _pack(flat_list, dtype, mult):
    cat = jnp.concatenate([a.astype(dtype) for a in flat_list], axis=-1)
    n = cat.shape[-1]
    rows = _pack_rows(n, mult)
    padw = [(0, 0)] * (cat.ndim - 1) + [(0, rows * PACK_COLS - n)]
    cat = jnp.pad(cat, padw)
    return cat.reshape(cat.shape[:-1] + (rows, PACK_COLS))


def _gathered_full(g3, axis):
    n, r, c = g3.shape
    if axis == 0:
        return g3.reshape(n * r, c)
    return g3.transpose(1, 0, 2).reshape(r, n * c)


def _shard_major(full, axis):
    r, c = full.shape
    if axis == 0:
        return full.reshape(N_DEV, r // N_DEV, c)
    return full.reshape(r, N_DEV, c // N_DEV).transpose(1, 0, 2)


def kernel(x, meta_tokens, ln_emb_g, ln_emb_b, w_in, b_gate, b_forget, q_norm_g, w_q_up, kv_norm_g, w_kv_up, w_branch_mla, w_branch_fox, w_out, ln_mix_g, ln_mix_b, w_ffn_up, conv_w, conv_b, w_ffn_down, ln_ffn_g, ln_ffn_b, loss_target, m_meta_tokens, m_ln_emb_g, m_ln_emb_b, m_w_in, m_b_gate, m_b_forget, m_q_norm_g, m_w_q_up, m_kv_norm_g, m_w_kv_up, m_w_branch_mla, m_w_branch_fox, m_w_out, m_ln_mix_g, m_ln_mix_b, m_w_ffn_up, m_conv_w, m_conv_b, m_w_ffn_down, m_ln_ffn_g, m_ln_ffn_b, v_meta_tokens, v_ln_emb_g, v_ln_emb_b, v_w_in, v_b_gate, v_b_forget, v_q_norm_g, v_w_q_up, v_kv_norm_g, v_w_kv_up, v_w_branch_mla, v_w_branch_fox, v_w_out, v_ln_mix_g, v_ln_mix_b, v_w_ffn_up, v_conv_w, v_conv_b, v_w_ffn_down, v_ln_ffn_g, v_ln_ffn_b):
    names = ("meta_tokens", "ln_emb_g", "ln_emb_b", "w_in", "b_gate", "b_forget", "q_norm_g", "w_q_up", "kv_norm_g",
             "w_kv_up", "w_branch_mla", "w_branch_fox", "w_out", "ln_mix_g", "ln_mix_b", "w_ffn_up", "conv_w", "conv_b",
             "w_ffn_down", "ln_ffn_g", "ln_ffn_b")
    w_args = (meta_tokens, ln_emb_g, ln_emb_b, w_in, b_gate, b_forget, q_norm_g, w_q_up, kv_norm_g, w_kv_up,
              w_branch_mla, w_branch_fox, w_out, ln_mix_g, ln_mix_b, w_ffn_up, conv_w, conv_b, w_ffn_down, ln_ffn_g, ln_ffn_b)
    m_args = (m_meta_tokens, m_ln_emb_g, m_ln_emb_b, m_w_in, m_b_gate, m_b_forget, m_q_norm_g, m_w_q_up, m_kv_norm_g,
              m_w_kv_up, m_w_branch_mla, m_w_branch_fox, m_w_out, m_ln_mix_g, m_ln_mix_b, m_w_ffn_up, m_conv_w, m_conv_b,
              m_w_ffn_down, m_ln_ffn_g, m_ln_ffn_b)
    v_args = (v_meta_tokens, v_ln_emb_g, v_ln_emb_b, v_w_in, v_b_gate, v_b_forget, v_q_norm_g, v_w_q_up, v_kv_norm_g,
              v_w_kv_up, v_w_branch_mla, v_w_branch_fox, v_w_out, v_ln_mix_g, v_ln_mix_b, v_w_ffn_up, v_conv_w, v_conv_b,
              v_w_ffn_down, v_ln_ffn_g, v_ln_ffn_b)
    as2d = lambda a: a.reshape((-1, a.shape[-1])) if a.ndim != 1 else a.reshape(1, -1)
    w = {n: as2d(a) for n, a in zip(names, w_args)}
    m = {n: as2d(a) for n, a in zip(names, m_args)}
    v = {n: as2d(a) for n, a in zip(names, v_args)}
    out_shape = {n: a.shape for n, a in zip(names, w_args)}

    seq = x.shape[1]
    rows = -(-(N_META + seq) // ROW_BLOCK) * ROW_BLOCK

    big_pack = _pack([w[n].reshape(-1) for n, _ in BIG], MXU_DTYPE, 16)
    small_pack = _pack([w[n].reshape(-1) for n, _ in SMALL_SHARDED], F32, 8)
    big_all = _allgather("gather_weights", big_pack).reshape(N_DEV, -1)
    small_all = _allgather("gather_small", small_pack).reshape(N_DEV, -1)
    p = {}
    off = 0
    for n, ax in BIG:
        r, c = w[n].shape
        p[n] = _gathered_full(big_all[:, off:off + r * c].reshape(N_DEV, r, c), ax)
        off += r * c
    off = 0
    for n, ax in SMALL_SHARDED:
        r, c = w[n].shape
        p[n] = _gathered_full(small_all[:, off:off + r * c].reshape(N_DEV, r, c), ax)
        off += r * c
    p["w_in"] = _perm_w_in(p["w_in"])
    p["w_q_up"] = _perm_w_q(p["w_q_up"])
    for n in REPLICATED:
        p[n] = w[n].reshape(-1)

    zpad = jnp.zeros((rows - N_META - seq, D_MODEL), F32)
    xcat = jnp.concatenate([p["meta_tokens"], x[0], zpad], axis=0)
    tgt = jnp.concatenate([jnp.zeros((N_META, D_MODEL), F32), loss_target[0], zpad], axis=0)
    loss_part, dxcat, g = _local_step(xcat, tgt, seq, p)
    grad_x = dxcat[N_META:N_META + seq][None]
    g["meta_tokens"] = dxcat[:N_META]
    g["w_in"] = _unperm_w_in(g["w_in"])
    g["w_q_up"] = _unperm_w_q(g["w_q_up"])
    loss = lax.psum(loss_part[0, 0], ("x", "y", "c"))

    shard_list = [_shard_major(g[n], ax).reshape(N_DEV, -1) for n, ax in BIG + SMALL_SHARDED]
    g_pack = _pack(shard_list, MXU_DTYPE, 16)
    g_recv = _exchange("exchange_grads", g_pack).reshape(N_DEV, -1)
    rep_pack = _pack([g[n].reshape(-1) for n in REPLICATED], F32, 8)
    rep_all = _allgather("gather_small_grads", rep_pack).reshape(N_DEV, -1)

    res = {}
    off = 0
    for n, _ in BIG + SMALL_SHARDED:
        r, c = w[n].shape
        parts = g_recv[:, off:off + r * c].reshape(N_DEV, r, c)
        off += r * c
        res[n] = _adamw("adamw_" + n, parts, w[n], m[n], v[n])
    rep_w = _pack([w[n].reshape(-1) for n in REPLICATED], F32, 8)
    rep_m = _pack([m[n].reshape(-1) for n in REPLICATED], F32, 8)
    rep_v = _pack([v[n].reshape(-1) for n in REPLICATED], F32, 8)
    rep_res = _adamw("adamw_replicated", rep_all.reshape((N_DEV,) + rep_w.shape), rep_w, rep_m, rep_v)
    off = 0
    for n in REPLICATED:
        sz = w[n].size
        res[n] = tuple(a.reshape(-1)[off:off + sz] for a in rep_res)
        off += sz

    outs = [loss, grad_x]
    for idx in range(4):
        outs += [res[n][idx].reshape(out_shape[n]) for n in names]
    return tuple(outs)
```

```python
import jax
import jax.numpy as jnp
from jax import lax
from jax.experimental import pallas as pl
from jax.experimental.pallas import tpu as pltpu

F32 = jnp.float32
BF16 = jnp.bfloat16
MXU_DTYPE = BF16

N_DEV = 8
N_META = 16
D_MODEL = 1024
HEADS = 8
Q_RANK = 384
KV_RANK = 128
NOPE = 64
ROPE = 32
HALF = ROPE // 2
QK_DIM = NOPE + ROPE
V_DIM = 64
FOX_DIM = 64
FOX_W = HEADS * FOX_DIM
D_FF = 2816
ROPE_THETA = 10000.0
LN_EPS = 1e-5
RMS_EPS = 1e-6
ALPHA = 2.0 ** 0.25
NEG_INF = -1e30

HP = 128
HW = HEADS * HP
F_W = 3 * HW
R_QLAT = 0
R_KVLAT = Q_RANK
R_LAST = R_KVLAT + KV_RANK
R_GATE = D_MODEL
R_W = R_GATE + 2 * D_MODEL
LANE_FL = 0
LANE_PE = NOPE

ADAM_LR = 0.001
ADAM_B1 = 0.9
ADAM_B2 = 0.999
ADAM_EPS = 1e-08
ADAM_WD = 0.01
ADAM_STEP = 10

ROW_BLOCK = 128
BWD_HEADS = 2
VMEM_LIMIT = 56 * 1024 * 1024
HIGHEST = lax.Precision.HIGHEST
NT = (((1,), (1,)), ((), ()))
TN = (((0,), (0,)), ((), ()))


def _params(sem=None):
    return pltpu.CompilerParams(dimension_semantics=sem, vmem_limit_bytes=VMEM_LIMIT)


def _call(name, body, grid, ins, outs, scratch=(), sem=None):
    return pl.pallas_call(
        body, name=name, grid=grid,
        in_specs=[s for _, s in ins],
        out_specs=[s for _, s in outs],
        out_shape=[o for o, _ in outs],
        scratch_shapes=list(scratch),
        compiler_params=_params(sem),
    )(*[a for a, _ in ins])


def _sds(shape, dtype):
    return jax.ShapeDtypeStruct(shape, dtype)


def _rows(br, c, cb=0):
    return pl.BlockSpec((br, c), lambda i: (i, cb))


def _whole(shape):
    n = len(shape)
    return pl.BlockSpec(shape, lambda i: (0,) * n)


def _pick(dim, cap, mult):
    best = None
    d = mult
    while d <= min(dim, cap):
        if dim % d == 0:
            best = d
        d += mult
    return best if best is not None else dim


def _hs(h):
    return slice(h * HP, (h + 1) * HP)


def _matmul(name, a, b, *, ta=False, tb=False, out_dtype=F32, addend=None, alpha=1.0):
    if ta:
        k, m = a.shape
    else:
        m, k = a.shape
    if tb:
        n, k2 = b.shape
    else:
        k2, n = b.shape
    assert k == k2, (name, a.shape, b.shape)
    bm = _pick(m, 1088, 128 if ta else 16)
    bn = _pick(n, 512, 128)
    bk = _pick(k, 1088, 128 if (not ta or tb) else 16)
    nk = k // bk
    dims = (((0 if ta else 1,), (1 if tb else 0,)), ((), ()))
    has_add = addend is not None

    def body(*refs):
        if has_add:
            a_ref, b_ref, add_ref, o_ref, acc_ref = refs
        else:
            a_ref, b_ref, o_ref, acc_ref = refs
        kk = pl.program_id(2)

        @pl.when(kk == 0)
        def _():
            acc_ref[...] = jnp.zeros_like(acc_ref)

        acc_ref[...] += lax.dot_general(a_ref[...], b_ref[...], dims, preferred_element_type=F32)

        @pl.when(kk == nk - 1)
        def _():
            r = acc_ref[...]
            if has_add:
                r = r + alpha * add_ref[...]
            o_ref[...] = r.astype(o_ref.dtype)

    a_spec = pl.BlockSpec((bk, bm), lambda i, j, l: (l, i)) if ta else pl.BlockSpec((bm, bk), lambda i, j, l: (i, l))
    b_spec = pl.BlockSpec((bn, bk), lambda i, j, l: (j, l)) if tb else pl.BlockSpec((bk, bn), lambda i, j, l: (l, j))
    o_spec = pl.BlockSpec((bm, bn), lambda i, j, l: (i, j))
    ins = [(a, a_spec), (b, b_spec)]
    if has_add:
        ins.append((addend, o_spec))
    return _call(name, body, (m // bm, n // bn, nk), ins, [(_sds((m, n), out_dtype), o_spec)],
                 scratch=[pltpu.VMEM((bm, bn), F32)], sem=("parallel", "parallel", "arbitrary"))[0]


def _ln_stats(z):
    mu = jnp.mean(z, axis=-1, keepdims=True)
    zc = z - mu
    var = jnp.mean(zc * zc, axis=-1, keepdims=True)
    rstd = lax.rsqrt(var + LN_EPS)
    return zc * rstd, rstd


def _ln_fwd(name, a, res, g, b):
    r, d = a.shape
    br = ROW_BLOCK
    has_res = res is not None

    def body(*refs):
        if has_res:
            a_ref, r_ref, g_ref, b_ref, y_ref, yb_ref = refs
            z = ALPHA * a_ref[...] + r_ref[...]
        else:
            a_ref, g_ref, b_ref, y_ref, yb_ref = refs
            z = a_ref[...]
        xhat, _ = _ln_stats(z)
        y = xhat * g_ref[...] + b_ref[...]
        y_ref[...] = y
        yb_ref[...] = y.astype(yb_ref.dtype)

    ins = [(a, _rows(br, d))]
    if has_res:
        ins.append((res, _rows(br, d)))
    ins += [(g.reshape(1, d), _whole((1, d))), (b.reshape(1, d), _whole((1, d)))]
    outs = [(_sds((r, d), F32), _rows(br, d)), (_sds((r, d), MXU_DTYPE), _rows(br, d))]
    return _call(name, body, (r // br,), ins, outs, sem=("parallel",))


def _ln_bwd(name, a, res, dy, g):
    r, d = a.shape
    br = ROW_BLOCK
    has_res = res is not None

    def body(*refs):
        if has_res:
            a_ref, r_ref, dy_ref, g_ref, dz_ref, dzb_ref, dg_ref, db_ref = refs
            z = ALPHA * a_ref[...] + r_ref[...]
        else:
            a_ref, dy_ref, g_ref, dz_ref, dzb_ref, dg_ref, db_ref = refs
            z = a_ref[...]
        xhat, rstd = _ln_stats(z)
        dyv = dy_ref[...]
        dyg = dyv * g_ref[...]
        m1 = jnp.mean(dyg, axis=-1, keepdims=True)
        m2 = jnp.mean(dyg * xhat, axis=-1, keepdims=True)
        dz = rstd * (dyg - m1 - xhat * m2)
        dz_ref[...] = dz
        dzb_ref[...] = dz.astype(dzb_ref.dtype)

        @pl.when(pl.program_id(0) == 0)
        def _():
            dg_ref[...] = jnp.zeros_like(dg_ref)
            db_ref[...] = jnp.zeros_like(db_ref)

        dg_ref[...] += jnp.sum(dyv * xhat, axis=0, keepdims=True)
        db_ref[...] += jnp.sum(dyv, axis=0, keepdims=True)

    ins = [(a, _rows(br, d))]
    if has_res:
        ins.append((res, _rows(br, d)))
    ins += [(dy, _rows(br, d)), (g.reshape(1, d), _whole((1, d)))]
    outs = [(_sds((r, d), F32), _rows(br, d)), (_sds((r, d), MXU_DTYPE), _rows(br, d)),
            (_sds((1, d), F32), _whole((1, d))), (_sds((1, d), F32), _whole((1, d)))]
    return _call(name, body, (r // br,), ins, outs, sem=("arbitrary",))


def _rms_fwd(name, proj, cb, width, g):
    r = proj.shape[0]
    br = ROW_BLOCK

    def body(x_ref, g_ref, y_ref):
        x = x_ref[...]
        rstd = lax.rsqrt(jnp.mean(x * x, axis=-1, keepdims=True) + RMS_EPS)
        y_ref[...] = (x * rstd * g_ref[...]).astype(y_ref.dtype)

    return _call(name, body, (r // br,), [(proj, _rows(br, width, cb)), (g.reshape(1, width), _whole((1, width)))],
                 [(_sds((r, width), MXU_DTYPE), _rows(br, width))], sem=("parallel",))[0]


def _rms_bwd(name, proj, cb, width, dy, g):
    r = proj.shape[0]
    br = ROW_BLOCK

    def body(x_ref, dy_ref, g_ref, dx_ref, dg_ref):
        x = x_ref[...]
        rstd = lax.rsqrt(jnp.mean(x * x, axis=-1, keepdims=True) + RMS_EPS)
        nrm = x * rstd
        dyv = dy_ref[...]
        dyg = dyv * g_ref[...]
        dx = rstd * (dyg - nrm * jnp.mean(dyg * nrm, axis=-1, keepdims=True))
        dx_ref[...] = dx.astype(dx_ref.dtype)

        @pl.when(pl.program_id(0) == 0)
        def _():
            dg_ref[...] = jnp.zeros_like(dg_ref)

        dg_ref[...] += jnp.sum(dyv * nrm, axis=0, keepdims=True)

    return _call(name, body, (r // br,),
                 [(proj, _rows(br, width, cb)), (dy, _rows(br, width)), (g.reshape(1, width), _whole((1, width)))],
                 [(_sds((r, width), MXU_DTYPE), _rows(br, width)), (_sds((1, width), F32), _whole((1, width)))],
                 sem=("arbitrary",))


def _lane_iota(shape):
    return lax.broadcasted_iota(jnp.int32, shape, 1)


def _rotary(t, c, s, lane, sign):
    second = pltpu.roll(t, HP - HALF, axis=1)
    first = pltpu.roll(t, HALF, axis=1)
    lo = (lane >= LANE_PE) & (lane < LANE_PE + HALF)
    hi = (lane >= LANE_PE + HALF) & (lane < LANE_PE + ROPE)
    return jnp.where(lo, t * c - sign * second * s, jnp.where(hi, t * c + sign * first * s, t))


def _rope_fwd(q_raw, k_part, proj_r, cos_t, sin_t):
    r = q_raw.shape[0]
    br = ROW_BLOCK

    def body(q_ref, k_ref, t_ref, c_ref, s_ref, qo_ref, ko_ref):
        c = c_ref[...]
        s = s_ref[...]
        lane = _lane_iota((br, HP))
        pe = (lane >= LANE_PE) & (lane < LANE_PE + ROPE)
        kp = jnp.where(pe, _rotary(t_ref[...], c, s, lane, 1.0), 0.0)
        for h in range(HEADS):
            qo_ref[:, _hs(h)] = _rotary(q_ref[:, _hs(h)], c, s, lane, 1.0).astype(qo_ref.dtype)
            ko_ref[:, _hs(h)] = (k_ref[:, _hs(h)] + kp).astype(ko_ref.dtype)

    blk = _rows(br, HP)
    wide = _rows(br, HW)
    return _call("rope_fwd", body, (r // br,),
                 [(q_raw, wide), (k_part, wide), (proj_r, _rows(br, HP, R_LAST // HP)), (cos_t, blk), (sin_t, blk)],
                 [(_sds((r, HW), MXU_DTYPE), wide)] * 2, sem=("parallel",))


def _rope_bwd(dq, dk, dfl, cos_t, sin_t):
    r = dq.shape[0]
    br = ROW_BLOCK

    def body(dq_ref, dk_ref, fl_ref, c_ref, s_ref, dqo_ref, dko_ref, dl_ref):
        c = c_ref[...]
        s = s_ref[...]
        lane = _lane_iota((br, HP))
        pe = (lane >= LANE_PE) & (lane < LANE_PE + ROPE)
        acc = jnp.zeros((br, HP), F32)
        for h in range(HEADS):
            dqo_ref[:, _hs(h)] = _rotary(dq_ref[:, _hs(h)], c, s, lane, -1.0).astype(dqo_ref.dtype)
            dkh = dk_ref[:, _hs(h)]
            acc = acc + dkh
            dko_ref[:, _hs(h)] = dkh.astype(dko_ref.dtype)
        dl_ref[...] = (jnp.where(pe, _rotary(acc, c, s, lane, -1.0), 0.0) + fl_ref[...]).astype(dl_ref.dtype)

    blk = _rows(br, HP)
    wide = _rows(br, HW)
    return _call("rope_bwd", body, (r // br,),
                 [(dq, wide), (dk, wide), (dfl, blk), (cos_t, blk), (sin_t, blk)],
                 [(_sds((r, HW), MXU_DTYPE), wide), (_sds((r, HW), MXU_DTYPE), wide), (_sds((r, HP), MXU_DTYPE), blk)],
                 sem=("parallel",))


def _log_sigmoid(x):
    return jnp.minimum(x, 0.0) - jnp.log(1.0 + jnp.exp(-jnp.abs(x)))


def _forget_fwd(proj_r, bf_row):
    r = proj_r.shape[0]
    br = ROW_BLOCK

    def body(t_ref, b_ref, o_ref, ot_ref, carry_ref):
        @pl.when(pl.program_id(0) == 0)
        def _():
            carry_ref[...] = jnp.zeros_like(carry_ref)

        x = t_ref[...] + b_ref[...]
        lane = _lane_iota(x.shape)
        lf = jnp.where((lane >= LANE_FL) & (lane < LANE_FL + HEADS), _log_sigmoid(x), 0.0)
        tri = (lax.broadcasted_iota(jnp.int32, (br, br), 0) >= lax.broadcasted_iota(jnp.int32, (br, br), 1)).astype(F32)
        cum = jnp.dot(tri, lf, precision=HIGHEST, preferred_element_type=F32) + carry_ref[0:1, :]
        o_ref[...] = cum
        ot_ref[...] = cum.T[0:HEADS, :]
        carry_ref[...] = jnp.broadcast_to(cum[br - 1:br, :], carry_ref.shape)

    return _call("forget_fwd", body, (r // br,),
                 [(proj_r, _rows(br, HP, R_LAST // HP)), (bf_row, _whole((1, HP)))],
                 [(_sds((r, HP), F32), _rows(br, HP)), (_sds((HEADS, r), F32), pl.BlockSpec((HEADS, br), lambda i: (0, i)))],
                 scratch=[pltpu.VMEM((8, HP), F32)], sem=("arbitrary",))


def _forget_bwd(proj_r, bf_row, dcq, dck):
    r = proj_r.shape[0]
    br = ROW_BLOCK
    nb = r // br

    def body(t_ref, b_ref, dcq_ref, dck_ref, o_ref, db_ref, carry_ref):
        @pl.when(pl.program_id(0) == 0)
        def _():
            carry_ref[...] = jnp.zeros_like(carry_ref)
            db_ref[...] = jnp.zeros_like(db_ref)

        dck_t = jnp.concatenate([dck_ref[...], jnp.zeros((br - HEADS, br), F32)], axis=0).T
        dc = dcq_ref[...] + dck_t
        triu = (lax.broadcasted_iota(jnp.int32, (br, br), 0) <= lax.broadcasted_iota(jnp.int32, (br, br), 1)).astype(F32)
        dlf = jnp.dot(triu, dc, precision=HIGHEST, preferred_element_type=F32) + carry_ref[0:1, :]
        carry_ref[...] = jnp.broadcast_to(dlf[0:1, :], carry_ref.shape)
        x = t_ref[...] + b_ref[...]
        lane = _lane_iota(x.shape)
        dfl = jnp.where((lane >= LANE_FL) & (lane < LANE_FL + HEADS), dlf * jax.nn.sigmoid(-x), 0.0)
        o_ref[...] = dfl
        db_ref[...] += jnp.sum(dfl, axis=0, keepdims=True)

    rev = pl.BlockSpec((br, HP), lambda i: (nb - 1 - i, 0))
    return _call("forget_bwd", body, (nb,),
                 [(proj_r, pl.BlockSpec((br, HP), lambda i: (nb - 1 - i, R_LAST // HP))), (bf_row, _whole((1, HP))),
                  (dcq, rev), (dck, pl.BlockSpec((HEADS, br), lambda i: (0, nb - 1 - i)))],
                 [(_sds((r, HP), F32), rev), (_sds((1, HP), F32), _whole((1, HP)))],
                 scratch=[pltpu.VMEM((8, HP), F32)], sem=("arbitrary",))


def _gate_fwd(proj_r, b_gate, bm, bfx):
    r, d = bm.shape
    br = ROW_BLOCK
    cb = R_GATE // d

    def body(gm_ref, gf_ref, b1_ref, b2_ref, bm_ref, bf_ref, o_ref):
        g1 = jax.nn.sigmoid(gm_ref[...] + b1_ref[...])
        g2 = jax.nn.sigmoid(gf_ref[...] + b2_ref[...])
        o_ref[...] = (g1 * bm_ref[...] + g2 * bf_ref[...]).astype(o_ref.dtype)

    b1 = b_gate[:d].reshape(1, d)
    b2 = b_gate[d:].reshape(1, d)
    return _call("gate_fwd", body, (r // br,),
                 [(proj_r, _rows(br, d, cb)), (proj_r, _rows(br, d, cb + 1)), (b1, _whole((1, d))), (b2, _whole((1, d))),
                  (bm, _rows(br, d)), (bfx, _rows(br, d))],
                 [(_sds((r, d), MXU_DTYPE), _rows(br, d))], sem=("parallel",))[0]


def _gate_bwd(proj_r, b_gate, bm, bfx, dmerged):
    r, d = bm.shape
    br = ROW_BLOCK
    cb = R_GATE // d

    def body(gm_ref, gf_ref, b1_ref, b2_ref, bm_ref, bf_ref, dm_ref, dbm_ref, dbf_ref, dgl_ref, dbg_ref):
        g1 = jax.nn.sigmoid(gm_ref[...] + b1_ref[...])
        g2 = jax.nn.sigmoid(gf_ref[...] + b2_ref[...])
        dm = dm_ref[...]
        dbm_ref[...] = (dm * g1).astype(dbm_ref.dtype)
        dbf_ref[...] = (dm * g2).astype(dbf_ref.dtype)
        dl1 = dm * bm_ref[...] * (g1 * (1.0 - g1))
        dl2 = dm * bf_ref[...] * (g2 * (1.0 - g2))
        dgl_ref[:, 0:d] = dl1.astype(dgl_ref.dtype)
        dgl_ref[:, d:2 * d] = dl2.astype(dgl_ref.dtype)

        @pl.when(pl.program_id(0) == 0)
        def _():
            dbg_ref[...] = jnp.zeros_like(dbg_ref)

        dbg_ref[:, 0:d] += jnp.sum(dl1, axis=0, keepdims=True)
        dbg_ref[:, d:2 * d] += jnp.sum(dl2, axis=0, keepdims=True)

    b1 = b_gate[:d].reshape(1, d)
    b2 = b_gate[d:].reshape(1, d)
    return _call("gate_bwd", body, (r // br,),
                 [(proj_r, _rows(br, d, cb)), (proj_r, _rows(br, d, cb + 1)), (b1, _whole((1, d))), (b2, _whole((1, d))),
                  (bm, _rows(br, d)), (bfx, _rows(br, d)), (dmerged, _rows(br, d))],
                 [(_sds((r, d), MXU_DTYPE), _rows(br, d)), (_sds((r, d), MXU_DTYPE), _rows(br, d)),
                  (_sds((r, 2 * d), MXU_DTYPE), _rows(br, 2 * d)), (_sds((1, 2 * d), F32), _whole((1, 2 * d)))],
                 sem=("arbitrary",))


def _conv_taps(gp, halo, first_block):
    halo = jnp.where(first_block, 0.0, halo)
    rid = lax.broadcasted_iota(jnp.int32, gp.shape, 0)
    g1 = jnp.where(rid == 0, halo[7:8, :], pltpu.roll(gp, 1, axis=0))
    g2 = jnp.where(rid == 0, halo[6:7, :], jnp.where(rid == 1, halo[7:8, :], pltpu.roll(gp, 2, axis=0)))
    return g1, g2


def _prev_halo(br, c):
    return pl.BlockSpec((8, c), lambda i: (jnp.maximum(i * (br // 8) - 1, 0), 0))


def _glu_fwd(up, conv_w, conv_b):
    r = up.shape[0]
    c = D_FF
    br = ROW_BLOCK

    def body(gp_ref, halo_ref, val_ref, w_ref, b_ref, o_ref):
        gp = gp_ref[...]
        g1, g2 = _conv_taps(gp, halo_ref[...], pl.program_id(0) == 0)
        gate = w_ref[0:1, :] * g2 + w_ref[1:2, :] * g1 + w_ref[2:3, :] * gp + b_ref[...]
        o_ref[...] = (gate * jax.nn.sigmoid(gate) * val_ref[...]).astype(o_ref.dtype)

    return _call("glu_fwd", body, (r // br,),
                 [(up, _rows(br, c, 0)), (up, _prev_halo(br, c)), (up, _rows(br, c, 1)),
                  (conv_w, _whole((3, c))), (conv_b.reshape(1, c), _whole((1, c)))],
                 [(_sds((r, c), MXU_DTYPE), _rows(br, c))], sem=("parallel",))[0]


def _glu_bwd_gate(up, conv_w, conv_b, d_act):
    r = up.shape[0]
    c = D_FF
    br = ROW_BLOCK

    def body(gp_ref, halo_ref, val_ref, w_ref, b_ref, da_ref, dg_ref, dv_ref, dw_ref, db_ref):
        gp = gp_ref[...]
        g1, g2 = _conv_taps(gp, halo_ref[...], pl.program_id(0) == 0)
        gate = w_ref[0:1, :] * g2 + w_ref[1:2, :] * g1 + w_ref[2:3, :] * gp + b_ref[...]
        sg = jax.nn.sigmoid(gate)
        da = da_ref[...]
        dv_ref[...] = (da * (gate * sg)).astype(dv_ref.dtype)
        dg = da * val_ref[...] * (sg * (1.0 + gate * (1.0 - sg)))
        dg_ref[...] = dg

        @pl.when(pl.program_id(0) == 0)
        def _():
            dw_ref[...] = jnp.zeros_like(dw_ref)
            db_ref[...] = jnp.zeros_like(db_ref)

        dw_ref[0:1, :] += jnp.sum(dg * g2, axis=0, keepdims=True)
        dw_ref[1:2, :] += jnp.sum(dg * g1, axis=0, keepdims=True)
        dw_ref[2:3, :] += jnp.sum(dg * gp, axis=0, keepdims=True)
        db_ref[...] += jnp.sum(dg, axis=0, keepdims=True)

    return _call("glu_bwd_gate", body, (r // br,),
                 [(up, _rows(br, c, 0)), (up, _prev_halo(br, c)), (up, _rows(br, c, 1)),
                  (conv_w, _whole((3, c))), (conv_b.reshape(1, c), _whole((1, c))), (d_act, _rows(br, c))],
                 [(_sds((r, c), F32), _rows(br, c)), (_sds((r, c), MXU_DTYPE), _rows(br, c)),
                  (_sds((8, c), F32), _whole((8, c))), (_sds((1, c), F32), _whole((1, c)))],
                 sem=("arbitrary",))


def _glu_bwd_conv(dg, dval, conv_w):
    r, c = dg.shape
    br = ROW_BLOCK
    nb = r // br

    def body(dg_ref, nxt_ref, dv_ref, w_ref, o_ref):
        x = dg_ref[...]
        nxt = jnp.where(pl.program_id(0) == nb - 1, 0.0, nxt_ref[...])
        rid = lax.broadcasted_iota(jnp.int32, x.shape, 0)
        u1 = jnp.where(rid == br - 1, nxt[0:1, :], pltpu.roll(x, br - 1, axis=0))
        u2 = jnp.where(rid == br - 1, nxt[1:2, :], jnp.where(rid == br - 2, nxt[0:1, :], pltpu.roll(x, br - 2, axis=0)))
        dgp = w_ref[2:3, :] * x + w_ref[1:2, :] * u1 + w_ref[0:1, :] * u2
        o_ref[:, 0:c] = dgp.astype(o_ref.dtype)
        o_ref[:, c:2 * c] = dv_ref[...]

    nxt_spec = pl.BlockSpec((8, c), lambda i: (jnp.minimum((i + 1) * (br // 8), r // 8 - 1), 0))
    return _call("glu_bwd_conv", body, (nb,),
                 [(dg, _rows(br, c)), (dg, nxt_spec), (dval, _rows(br, c)), (conv_w, _whole((3, c)))],
                 [(_sds((r, 2 * c), MXU_DTYPE), _rows(br, 2 * c))], sem=("parallel",))[0]


def _loss(h2, tgt, seq):
    r, d = h2.shape
    br = ROW_BLOCK

    def body(h_ref, t_ref, l_ref, d_ref):
        rid = lax.broadcasted_iota(jnp.int32, (br, d), 0) + pl.program_id(0) * br
        valid = (rid >= N_META) & (rid < N_META + seq)
        err = jnp.where(valid, h_ref[...] - t_ref[...], 0.0)
        d_ref[...] = err * (1.0 / d)

        @pl.when(pl.program_id(0) == 0)
        def _():
            l_ref[...] = jnp.zeros_like(l_ref)

        l_ref[...] += jnp.sum(jnp.sum(err * err, axis=1, keepdims=True), axis=0, keepdims=True) * (0.5 / d)

    return _call("loss", body, (r // br,), [(h2, _rows(br, d)), (tgt, _rows(br, d))],
                 [(_sds((1, 1), F32), _whole((1, 1))), (_sds((r, d), F32), _rows(br, d))], sem=("arbitrary",))


def _attn_fwd(name, q, k, v, scale, cum=None, cum_t=None):
    (qa, qg), (ka, kg), (va, vg) = q, k, v
    r = qa.shape[0]
    t = ROW_BLOCK
    nb = r // t
    bias = cum is not None

    def body(*refs):
        if bias:
            q_ref, k_ref, v_ref, cum_ref, ct_ref, o_ref, ob_ref, lse_ref = refs
        else:
            q_ref, k_ref, v_ref, o_ref, ob_ref, lse_ref = refs
        i = pl.program_id(0)
        qs = [q_ref[:, _hs(h)] for h in range(HEADS)]
        lane = _lane_iota((t, HP))
        if bias:
            cum_blk = cum_ref[...]
            cqs = [jnp.sum(jnp.where(lane == h, cum_blk, 0.0), axis=1, keepdims=True) for h in range(HEADS)]
        causal = lax.broadcasted_iota(jnp.int32, (t, t), 1) <= lax.broadcasted_iota(jnp.int32, (t, t), 0)

        def block(j, carry, diagonal):
            off = pl.multiple_of(j * t, t)
            out = []
            for h in range(HEADS):
                m, l, acc = carry[h]
                kh = k_ref[pl.ds(off, t), _hs(h)]
                vh = v_ref[pl.ds(off, t), _hs(h)]
                s = lax.dot_general(qs[h], kh, NT, preferred_element_type=F32) * scale
                if bias:
                    s = s + (cqs[h] - ct_ref[h, pl.ds(j, 1), :])
                if diagonal:
                    s = jnp.where(causal, s, NEG_INF)
                m_new = jnp.maximum(m, jnp.max(s, axis=1, keepdims=True))
                p = jnp.exp(s - m_new)
                a = jnp.exp(m - m_new)
                l = a * l + jnp.sum(p, axis=1, keepdims=True)
                acc = a * acc + jnp.dot(p.astype(vh.dtype), vh, preferred_element_type=F32)
                out.append((m_new, l, acc))
            return tuple(out)

        init = tuple((jnp.full((t, 1), NEG_INF, F32), jnp.zeros((t, 1), F32), jnp.zeros((t, HP), F32))
                     for _ in range(HEADS))
        carry = lax.fori_loop(0, i, lambda j, c: block(j, c, False), init)
        carry = block(i, carry, True)
        lse = jnp.zeros((t, HP), F32)
        for h in range(HEADS):
            m, l, acc = carry[h]
            oh = acc / l
            o_ref[:, _hs(h)] = oh
            ob_ref[:, _hs(h)] = oh.astype(ob_ref.dtype)
            lse = jnp.where(lane == h, m + jnp.log(l), lse)
        lse_ref[...] = lse

    blk = lambda g: pl.BlockSpec((t, HW), lambda i: (i, g))
    res = lambda g: pl.BlockSpec((r, HW), lambda i: (0, g))
    ins = [(qa, blk(qg)), (ka, res(kg)), (va, res(vg))]
    if bias:
        ins += [(cum, _rows(t, HP)), (cum_t, _whole((HEADS, nb, t)))]
    outs = [(_sds((r, HW), F32), blk(0)), (_sds((r, HW), MXU_DTYPE), blk(0)), (_sds((r, HP), F32), _rows(t, HP))]
    return _call(name, body, (nb,), ins, outs, sem=("parallel",))


def _attn_delta(name, do, o):
    r = do.shape[0]
    br = ROW_BLOCK

    def body(do_ref, o_ref, d_ref, dob_ref):
        lane = _lane_iota((br, HP))
        d = jnp.zeros((br, HP), F32)
        for h in range(HEADS):
            dh = do_ref[:, _hs(h)]
            d = jnp.where(lane == h, jnp.sum(dh * o_ref[:, _hs(h)], axis=1, keepdims=True), d)
            dob_ref[:, _hs(h)] = dh.astype(dob_ref.dtype)
        d_ref[...] = d

    wide = _rows(br, HW)
    return _call(name, body, (r // br,), [(do, wide), (o, wide)],
                 [(_sds((r, HP), F32), _rows(br, HP)), (_sds((r, HW), MXU_DTYPE), wide)], sem=("parallel",))


def _attn_bwd(name, q, k, v, do_b, lse, delta, scale, cum=None, cum_t=None):
    (qa, qg), (ka, kg), (va, vg) = q, k, v
    r = qa.shape[0]
    t = ROW_BLOCK
    nb = r // t
    hg = BWD_HEADS
    w = hg * HP
    gpw = HW // w
    bias = cum is not None

    def body(*refs):
        if bias:
            (q_ref, k_ref, v_ref, do_ref, lse_ref, dl_ref, cum_ref, ct_ref,
             dq_ref, dk_ref, dv_ref, dcq_ref, dck_ref) = refs
        else:
            q_ref, k_ref, v_ref, do_ref, lse_ref, dl_ref, dq_ref, dk_ref, dv_ref = refs
        g = pl.program_id(0)
        j = pl.program_id(1)

        @pl.when(j == 0)
        def _():
            dq_ref[...] = jnp.zeros_like(dq_ref)

        if bias:
            @pl.when((g == 0) & (j == 0))
            def _():
                dcq_ref[...] = jnp.zeros_like(dcq_ref)

        lane = _lane_iota((t, HP))
        causal = lax.broadcasted_iota(jnp.int32, (t, t), 1) <= lax.broadcasted_iota(jnp.int32, (t, t), 0)
        ks = [k_ref[:, _hs(hh)] for hh in range(hg)]
        vs = [v_ref[:, _hs(hh)] for hh in range(hg)]
        cks = [ct_ref[g * hg + hh, pl.ds(j, 1), :] for hh in range(hg)] if bias else None

        def pick(blk, sel):
            return jnp.sum(jnp.where(sel, blk, 0.0), axis=1, keepdims=True)

        def block(i, carry, diagonal):
            rows = pl.ds(pl.multiple_of(i * t, t), t)
            lse_blk = lse_ref[rows, :]
            dl_blk = dl_ref[rows, :]
            if bias:
                cum_blk = cum_ref[rows, :]
                dcq_new = dcq_ref[rows, :]
            out = []
            for hh in range(hg):
                dk_acc, dv_acc, dck_acc = carry[hh]
                sel = lane == g * hg + hh
                qh = q_ref[rows, _hs(hh)]
                doh = do_ref[rows, _hs(hh)]
                s = lax.dot_general(qh, ks[hh], NT, preferred_element_type=F32) * scale
                if bias:
                    s = s + (pick(cum_blk, sel) - cks[hh])
                if diagonal:
                    s = jnp.where(causal, s, NEG_INF)
                p = jnp.exp(s - pick(lse_blk, sel))
                dp = lax.dot_general(doh, vs[hh], NT, preferred_element_type=F32)
                ds = p * (dp - pick(dl_blk, sel))
                pb = p.astype(doh.dtype)
                dsb = ds.astype(qh.dtype)
                dv_acc = dv_acc + lax.dot_general(pb, doh, TN, preferred_element_type=F32)
                dk_acc = dk_acc + lax.dot_general(dsb, qh, TN, preferred_element_type=F32)
                dq_ref[rows, _hs(hh)] += jnp.dot(dsb, ks[hh], preferred_element_type=F32) * scale
                if bias:
                    dcq_new = jnp.where(sel, dcq_new + jnp.sum(ds, axis=1, keepdims=True), dcq_new)
                    dck_acc = dck_acc - jnp.sum(ds, axis=0, keepdims=True)
                out.append((dk_acc, dv_acc, dck_acc))
            if bias:
                dcq_ref[rows, :] = dcq_new
            return tuple(out)

        init = tuple((jnp.zeros((t, HP), F32), jnp.zeros((t, HP), F32), jnp.zeros((1, t), F32)) for _ in range(hg))
        carry = block(j, init, True)
        carry = lax.fori_loop(j + 1, nb, lambda i, c: block(i, c, False), carry)
        for hh in range(hg):
            dk_acc, dv_acc, dck_acc = carry[hh]
            dk_ref[:, _hs(hh)] = dk_acc * scale
            dv_ref[:, _hs(hh)] = dv_acc
            if bias:
                dck_ref[g * hg + hh, pl.ds(j, 1), :] = dck_acc

    res = lambda grp: pl.BlockSpec((r, w), lambda g, j: (0, grp * gpw + g))
    blk = lambda grp: pl.BlockSpec((t, w), lambda g, j: (j, grp * gpw + g))
    lanes = pl.BlockSpec((r, HP), lambda g, j: (0, 0))
    ins = [(qa, res(qg)), (ka, blk(kg)), (va, blk(vg)), (do_b, res(0)), (lse, lanes), (delta, lanes)]
    outs = [(_sds((r, HW), F32), res(0)), (_sds((r, HW), F32), blk(0)), (_sds((r, HW), F32), blk(0))]
    if bias:
        tspec = pl.BlockSpec((HEADS, nb, t), lambda g, j: (0, 0, 0))
        ins += [(cum, lanes), (cum_t, tspec)]
        outs += [(_sds((r, HP), F32), lanes), (_sds((HEADS, nb, t), F32), tspec)]
    return _call(name, body, (gpw, nb), ins, outs, sem=("arbitrary", "arbitrary"))


MESH_ID = pl.DeviceIdType.MESH
ANY = pl.BlockSpec(memory_space=pl.ANY)


def _allgather(name, shards):
    n = len(shards)

    def body(*refs):
        x_refs, out_refs = refs[:n], refs[n:2 * n]
        send_sems, recv_sems, local_sems = refs[2 * n:]
        x, y, c = lax.axis_index("x"), lax.axis_index("y"), lax.axis_index("c")
        me, sibling = (x, y, c), (x, y, 1 - c)
        chips = [(1 - x, y), (x, 1 - y), (1 - x, 1 - y)]

        def slot(ti, px, py, pc):
            return out_refs[ti].at[4 * px + 2 * py + pc]

        def copy(ti, k, block, to, src=None):
            return pltpu.make_async_remote_copy(
                src_ref=slot(ti, *block) if src is None else src, dst_ref=slot(ti, *block),
                send_sem=send_sems.at[ti, k], recv_sem=recv_sems.at[ti, k], device_id=to, device_id_type=MESH_ID)

        mine = [pltpu.make_async_copy(x_refs[ti], slot(ti, *me), local_sems.at[ti]) for ti in range(n)]
        for cp in mine:
            cp.start()
        started = []
        for ti in range(n):
            first = [copy(ti, 0, me, sibling, src=x_refs[ti])]
            first += [copy(ti, 1 + j, me, (*chip, c), src=x_refs[ti]) for j, chip in enumerate(chips)]
            for cp in first:
                cp.start()
            started += first
        for ti in range(n):
            for j, chip in enumerate(chips):
                copy(ti, 1 + j, (*chip, c), me).wait_recv()
                fwd = copy(ti, 4 + j, (*chip, c), sibling)
                fwd.start()
                started.append(fwd)
        for ti in range(n):
            copy(ti, 0, sibling, me).wait_recv()
            for j, chip in enumerate(chips):
                copy(ti, 4 + j, (*chip, 1 - c), me).wait_recv()
        for cp in started:
            cp.wait_send()
        for cp in mine:
            cp.wait()

    return pl.pallas_call(
        body, name=name, out_shape=[_sds((N_DEV,) + s.shape, s.dtype) for s in shards],
        in_specs=[ANY] * n, out_specs=[ANY] * n,
        scratch_shapes=[pltpu.SemaphoreType.DMA((n, 7)), pltpu.SemaphoreType.DMA((n, 7)), pltpu.SemaphoreType.DMA((n,))],
    )(*shards)


def _exchange(name, parts):
    n = len(parts)

    def body(*refs):
        g_refs, out_refs = refs[:n], refs[n:2 * n]
        send_sems, recv_sems, local_sems = refs[2 * n:]
        x, y, c = lax.axis_index("x"), lax.axis_index("y"), lax.axis_index("c")
        my_id = 4 * x + 2 * y + c

        def peer(k):
            return (1 - x if k & 4 else x, 1 - y if k & 2 else y, 1 - c if k & 1 else c)

        def copy(ti, k, landing):
            px, py, pc = peer(k)
            pid = 4 * px + 2 * py + pc
            return pltpu.make_async_remote_copy(
                src_ref=g_refs[ti].at[pid], dst_ref=out_refs[ti].at[pid if landing else my_id],
                send_sem=send_sems.at[ti, k - 1], recv_sem=recv_sems.at[ti, k - 1],
                device_id=(px, py, pc), device_id_type=MESH_ID)

        mine = [pltpu.make_async_copy(g_refs[ti].at[my_id], out_refs[ti].at[my_id], local_sems.at[ti]) for ti in range(n)]
        for cp in mine:
            cp.start()
        sends = [copy(ti, k, False) for ti in range(n) for k in range(1, N_DEV)]
        for cp in sends:
            cp.start()
        for ti in range(n):
            for k in range(1, N_DEV):
                copy(ti, k, True).wait_recv()
        for cp in sends:
            cp.wait_send()
        for cp in mine:
            cp.wait()

    return pl.pallas_call(
        body, name=name, out_shape=[_sds(p.shape, p.dtype) for p in parts],
        in_specs=[ANY] * n, out_specs=[ANY] * n,
        scratch_shapes=[pltpu.SemaphoreType.DMA((n, 7)), pltpu.SemaphoreType.DMA((n, 7)), pltpu.SemaphoreType.DMA((n,))],
    )(*parts)


def _adamw(name, parts, w, m, v):
    r, c = w.shape
    br = _pick(r, 256, 16)

    def body(p_ref, w_ref, m_ref, v_ref, g_ref, d_ref, nm_ref, nv_ref):
        g = p_ref[0].astype(F32)
        for k in range(1, N_DEV):
            g = g + p_ref[k].astype(F32)
        mm = ADAM_B1 * m_ref[...] + (1.0 - ADAM_B1) * g
        vv = ADAM_B2 * v_ref[...] + (1.0 - ADAM_B2) * (g * g)
        m_hat = mm / (1.0 - ADAM_B1 ** ADAM_STEP)
        v_hat = vv / (1.0 - ADAM_B2 ** ADAM_STEP)
        g_ref[...] = g
        d_ref[...] = -ADAM_LR * (m_hat / (jnp.sqrt(v_hat) + ADAM_EPS) + ADAM_WD * w_ref[...])
        nm_ref[...] = mm
        nv_ref[...] = vv

    spec = _rows(br, c)
    out = (_sds((r, c), F32), spec)
    return _call(name, body, (r // br,),
                 [(parts, pl.BlockSpec((N_DEV, br, c), lambda i: (0, i, 0))), (w, spec), (m, spec), (v, spec)],
                 [out] * 4, sem=("parallel",))


def _pad_head_cols(w, d):
    k = w.shape[0]
    return jnp.pad(w.reshape(k, HEADS, d), ((0, 0), (0, 0), (0, HP - d))).reshape(k, HW)


def _unpad_head_cols(wp, d):
    k = wp.shape[0]
    return wp.reshape(k, HEADS, HP)[:, :, :d].reshape(k, HEADS * d)


def _pad_head_rows(w, d):
    n = w.shape[1]
    return jnp.pad(w.reshape(HEADS, d, n), ((0, 0), (0, HP - d), (0, 0))).reshape(HW, n)


def _unpad_head_rows(wp, d):
    n = wp.shape[1]
    return wp.reshape(HEADS, HP, n)[:, :d, :].reshape(HEADS * d, n)


IN_SEGS = (("q", Q_RANK), ("kv", KV_RANK), ("kr", ROPE), ("fq", FOX_W), ("fk", FOX_W), ("fv", FOX_W),
           ("fl", HEADS), ("gate", 2 * D_MODEL))


def _split_w_in(w):
    seg = {}
    o = 0
    for nm, wd in IN_SEGS:
        seg[nm] = w[:, o:o + wd]
        o += wd
    d = w.shape[0]
    z = lambda n: jnp.zeros((d, n), w.dtype)
    fused = jnp.concatenate([_pad_head_cols(seg[nm], FOX_DIM) for nm in ("fq", "fk", "fv")], axis=1)
    last = jnp.concatenate([seg["fl"], z(LANE_PE - HEADS), seg["kr"], z(HP - LANE_PE - ROPE)], axis=1)
    rest = jnp.concatenate([seg["q"], seg["kv"], last, z(R_GATE - R_LAST - HP), seg["gate"]], axis=1)
    return fused, rest


def _merge_w_in(fused, rest):
    f = [_unpad_head_cols(fused[:, i * HW:(i + 1) * HW], FOX_DIM) for i in range(3)]
    last = rest[:, R_LAST:R_LAST + HP]
    return jnp.concatenate([rest[:, R_QLAT:R_LAST], last[:, LANE_PE:LANE_PE + ROPE], f[0], f[1], f[2],
                            last[:, LANE_FL:LANE_FL + HEADS], rest[:, R_GATE:]], axis=1)


def _split_w_kv(w):
    k = w.shape[0]
    w3 = w.reshape(k, HEADS, NOPE + V_DIM)
    padl = lambda a: jnp.pad(a, ((0, 0), (0, 0), (0, HP - a.shape[-1]))).reshape(k, HW)
    return padl(w3[..., :NOPE]), padl(w3[..., NOPE:])


def _merge_w_kv(wk, wv):
    k = wk.shape[0]
    return jnp.concatenate([wk.reshape(k, HEADS, HP)[..., :NOPE], wv.reshape(k, HEADS, HP)[..., :V_DIM]],
                           axis=-1).reshape(k, HEADS * (NOPE + V_DIM))


def _local_step(xcat, tgt, seq, p):
    r = xcat.shape[0]
    nb = r // ROW_BLOCK
    cd = MXU_DTYPE

    w_f, w_r = _split_w_in(p["w_in"])
    w_q = _pad_head_cols(p["w_q_up"], QK_DIM)
    w_k, w_v = _split_w_kv(p["w_kv_up"])
    w_bm = _pad_head_rows(p["w_branch_mla"], V_DIM)
    w_bf = _pad_head_rows(p["w_branch_fox"], FOX_DIM)

    pos = jnp.arange(r, dtype=F32)
    inv_freq = ROPE_THETA ** (-jnp.arange(HALF, dtype=F32) / HALF)
    ang = pos[:, None] * inv_freq[None, :]
    cos_t = jnp.tile(jnp.cos(ang), (1, HP // HALF))
    sin_t = jnp.tile(jnp.sin(ang), (1, HP // HALF))
    bf_row = jnp.zeros((1, HP), F32).at[0, LANE_FL:LANE_FL + HEADS].set(p["b_forget"])

    h0, h0b = _ln_fwd("ln_emb_fwd", xcat, None, p["ln_emb_g"], p["ln_emb_b"])
    proj_f = _matmul("in_proj_f", h0b, w_f, out_dtype=cd)
    proj_r = _matmul("in_proj_r", h0b, w_r)
    ql = _rms_fwd("q_norm_fwd", proj_r, R_QLAT // Q_RANK, Q_RANK, p["q_norm_g"])
    kvl = _rms_fwd("kv_norm_fwd", proj_r, R_KVLAT // KV_RANK, KV_RANK, p["kv_norm_g"])
    q_raw = _matmul("q_up", ql, w_q)
    k_part = _matmul("k_up", kvl, w_k)
    v_mla = _matmul("v_up", kvl, w_v, out_dtype=cd)
    q_mla, k_mla = _rope_fwd(q_raw, k_part, proj_r, cos_t, sin_t)
    o_mla, o_mla_b, lse_mla = _attn_fwd("mla_fwd", (q_mla, 0), (k_mla, 0), (v_mla, 0), QK_DIM ** -0.5)

    cum, cum_t = _forget_fwd(proj_r, bf_row)
    cum_t = cum_t.reshape(HEADS, nb, ROW_BLOCK)
    o_fox, o_fox_b, lse_fox = _attn_fwd("fox_fwd", (proj_f, 0), (proj_f, 1), (proj_f, 2), FOX_DIM ** -0.5, cum, cum_t)

    bm = _matmul("branch_mla", o_mla_b, w_bm)
    bfx = _matmul("branch_fox", o_fox_b, w_bf)
    merged = _gate_fwd(proj_r, p["b_gate"], bm, bfx)
    mix = _matmul("out_proj", merged, p["w_out"])
    h1, h1b = _ln_fwd("ln_mix_fwd", h0, mix, p["ln_mix_g"], p["ln_mix_b"])
    up = _matmul("ffn_up", h1b, p["w_ffn_up"])
    act = _glu_fwd(up, p["conv_w"], p["conv_b"])
    f = _matmul("ffn_down", act, p["w_ffn_down"])
    h2, _ = _ln_fwd("ln_ffn_fwd", h1, f, p["ln_ffn_g"], p["ln_ffn_b"])
    loss, dh2 = _loss(h2, tgt, seq)

    g = {}
    dz2, dz2b, g["ln_ffn_g"], g["ln_ffn_b"] = _ln_bwd("ln_ffn_bwd", h1, f, dh2, p["ln_ffn_g"])
    d_act = _matmul("ffn_down_dx", dz2b, p["w_ffn_down"], tb=True)
    g["w_ffn_down"] = _matmul("ffn_down_dw", act, dz2b, ta=True)
    dgate, dval, dcw, g["conv_b"] = _glu_bwd_gate(up, p["conv_w"], p["conv_b"], d_act)
    g["conv_w"] = dcw[:3]
    d_up = _glu_bwd_conv(dgate, dval, p["conv_w"])
    dh1 = _matmul("ffn_up_dx", d_up, p["w_ffn_up"], tb=True, addend=dz2, alpha=ALPHA)
    g["w_ffn_up"] = _matmul("ffn_up_dw", h1b, d_up, ta=True)
    dz1, dz1b, g["ln_mix_g"], g["ln_mix_b"] = _ln_bwd("ln_mix_bwd", h0, mix, dh1, p["ln_mix_g"])
    dmerged = _matmul("out_proj_dx", dz1b, p["w_out"], tb=True)
    g["w_out"] = _matmul("out_proj_dw", merged, dz1b, ta=True)
    d_bm, d_bf, d_gl, g["b_gate"] = _gate_bwd(proj_r, p["b_gate"], bm, bfx, dmerged)
    d_o_mla = _matmul("branch_mla_dx", d_bm, w_bm, tb=True)
    g["w_branch_mla"] = _unpad_head_rows(_matmul("branch_mla_dw", o_mla_b, d_bm, ta=True), V_DIM)
    d_o_fox = _matmul("branch_fox_dx", d_bf, w_bf, tb=True)
    g["w_branch_fox"] = _unpad_head_rows(_matmul("branch_fox_dw", o_fox_b, d_bf, ta=True), FOX_DIM)

    dl_mla, do_mla_b = _attn_delta("mla_delta", d_o_mla, o_mla)
    dq_m, dk_m, dv_m = _attn_bwd("mla_bwd", (q_mla, 0), (k_mla, 0), (v_mla, 0), do_mla_b, lse_mla, dl_mla, QK_DIM ** -0.5)
    dl_fox, do_fox_b = _attn_delta("fox_delta", d_o_fox, o_fox)
    dfq, dfk, dfv, dcq, dck = _attn_bwd("fox_bwd", (proj_f, 0), (proj_f, 1), (proj_f, 2), do_fox_b, lse_fox, dl_fox,
                                        FOX_DIM ** -0.5, cum, cum_t)
    dfl, dbf = _forget_bwd(proj_r, bf_row, dcq, dck.reshape(HEADS, r))
    g["b_forget"] = dbf[:, LANE_FL:LANE_FL + HEADS]

    dq_b, dk_b, dlast = _rope_bwd(dq_m, dk_m, dfl, cos_t, sin_t)
    dv_b = dv_m.astype(cd)
    d_ql = _matmul("q_up_dx", dq_b, w_q, tb=True)
    g["w_q_up"] = _unpad_head_cols(_matmul("q_up_dw", ql, dq_b, ta=True), QK_DIM)
    d_kvl = _matmul("k_up_dx", dk_b, w_k, tb=True)
    d_kvl = _matmul("v_up_dx", dv_b, w_v, tb=True, addend=d_kvl)
    g["w_kv_up"] = _merge_w_kv(_matmul("k_up_dw", kvl, dk_b, ta=True), _matmul("v_up_dw", kvl, dv_b, ta=True))
    d_qlat, g["q_norm_g"] = _rms_bwd("q_norm_bwd", proj_r, R_QLAT // Q_RANK, Q_RANK, d_ql, p["q_norm_g"])
    d_kvlat, g["kv_norm_g"] = _rms_bwd("kv_norm_bwd", proj_r, R_KVLAT // KV_RANK, KV_RANK, d_kvl, p["kv_norm_g"])
    dproj_f = jnp.concatenate([dfq.astype(cd), dfk.astype(cd), dfv.astype(cd)], axis=1)
    dproj_r = jnp.concatenate([d_qlat, d_kvlat, dlast, jnp.zeros((r, R_GATE - R_LAST - HP), cd), d_gl], axis=1)
    dh0 = _matmul("in_proj_f_dx", dproj_f, w_f, tb=True, addend=dz1, alpha=ALPHA)
    dh0 = _matmul("in_proj_r_dx", dproj_r, w_r, tb=True, addend=dh0)
    g["w_in"] = _merge_w_in(_matmul("in_proj_f_dw", h0b, dproj_f, ta=True), _matmul("in_proj_r_dw", h0b, dproj_r, ta=True))
    dxcat, _, g["ln_emb_g"], g["ln_emb_b"] = _ln_bwd("ln_emb_bwd", xcat, None, dh0, p["ln_emb_g"])
    return loss, dxcat, g


BIG = (("w_in", 1), ("w_q_up", 1), ("w_kv_up", 1), ("w_branch_mla", 1), ("w_branch_fox", 1), ("w_out", 0),
       ("w_ffn_up", 1), ("w_ffn_down", 0))
SMALL_SHARDED = (("meta_tokens", 1), ("conv_w", 1))
REPLICATED = ("ln_emb_g", "ln_emb_b", "b_gate", "b_forget", "q_norm_g", "kv_norm_g", "ln_mix_g", "ln_mix_b",
              "conv_b", "ln_ffn_g", "ln_ffn_b")
PACK_COLS = 1024


def _pack(flat_list):
    cat = jnp.concatenate(flat_list)
    n = cat.shape[0]
    rows = -(-n // (8 * PACK_COLS)) * 8
    return jnp.pad(cat, (0, rows * PACK_COLS - n)).reshape(rows, PACK_COLS)


def _gathered_full(g3, axis):
    n, r, c = g3.shape
    if axis == 0:
        return g3.reshape(n * r, c)
    return g3.transpose(1, 0, 2).reshape(r, n * c)


def _shard_major(full, axis):
    r, c = full.shape
    if axis == 0:
        return full.reshape(N_DEV, r // N_DEV, c)
    return full.reshape(r, N_DEV, c // N_DEV).transpose(1, 0, 2)


def kernel(x, meta_tokens, ln_emb_g, ln_emb_b, w_in, b_gate, b_forget, q_norm_g, w_q_up, kv_norm_g, w_kv_up, w_branch_mla, w_branch_fox, w_out, ln_mix_g, ln_mix_b, w_ffn_up, conv_w, conv_b, w_ffn_down, ln_ffn_g, ln_ffn_b, loss_target, m_meta_tokens, m_ln_emb_g, m_ln_emb_b, m_w_in, m_b_gate, m_b_forget, m_q_norm_g, m_w_q_up, m_kv_norm_g, m_w_kv_up, m_w_branch_mla, m_w_branch_fox, m_w_out, m_ln_mix_g, m_ln_mix_b, m_w_ffn_up, m_conv_w, m_conv_b, m_w_ffn_down, m_ln_ffn_g, m_ln_ffn_b, v_meta_tokens, v_ln_emb_g, v_ln_emb_b, v_w_in, v_b_gate, v_b_forget, v_q_norm_g, v_w_q_up, v_kv_norm_g, v_w_kv_up, v_w_branch_mla, v_w_branch_fox, v_w_out, v_ln_mix_g, v_ln_mix_b, v_w_ffn_up, v_conv_w, v_conv_b, v_w_ffn_down, v_ln_ffn_g, v_ln_ffn_b):
    names = ("meta_tokens", "ln_emb_g", "ln_emb_b", "w_in", "b_gate", "b_forget", "q_norm_g", "w_q_up", "kv_norm_g",
             "w_kv_up", "w_branch_mla", "w_branch_fox", "w_out", "ln_mix_g", "ln_mix_b", "w_ffn_up", "conv_w", "conv_b",
             "w_ffn_down", "ln_ffn_g", "ln_ffn_b")
    w_args = (meta_tokens, ln_emb_g, ln_emb_b, w_in, b_gate, b_forget, q_norm_g, w_q_up, kv_norm_g, w_kv_up,
              w_branch_mla, w_branch_fox, w_out, ln_mix_g, ln_mix_b, w_ffn_up, conv_w, conv_b, w_ffn_down, ln_ffn_g, ln_ffn_b)
    m_args = (m_meta_tokens, m_ln_emb_g, m_ln_emb_b, m_w_in, m_b_gate, m_b_forget, m_q_norm_g, m_w_q_up, m_kv_norm_g,
              m_w_kv_up, m_w_branch_mla, m_w_branch_fox, m_w_out, m_ln_mix_g, m_ln_mix_b, m_w_ffn_up, m_conv_w, m_conv_b,
              m_w_ffn_down, m_ln_ffn_g, m_ln_ffn_b)
    v_args = (v_meta_tokens, v_ln_emb_g, v_ln_emb_b, v_w_in, v_b_gate, v_b_forget, v_q_norm_g, v_w_q_up, v_kv_norm_g,
              v_w_kv_up, v_w_branch_mla, v_w_branch_fox, v_w_out, v_ln_mix_g, v_ln_mix_b, v_w_ffn_up, v_conv_w, v_conv_b,
              v_w_ffn_down, v_ln_ffn_g, v_ln_ffn_b)
    as2d = lambda a: a.reshape((-1, a.shape[-1])) if a.ndim != 1 else a.reshape(1, -1)
    w = {n: as2d(a) for n, a in zip(names, w_args)}
    m = {n: as2d(a) for n, a in zip(names, m_args)}
    v = {n: as2d(a) for n, a in zip(names, v_args)}
    out_shape = {n: a.shape for n, a in zip(names, w_args)}

    seq = x.shape[1]
    rows = -(-(N_META + seq) // ROW_BLOCK) * ROW_BLOCK
    sharded = BIG + SMALL_SHARDED

    shards = [w[n].astype(MXU_DTYPE) for n, _ in BIG] + [w[n] for n, _ in SMALL_SHARDED]
    gathered = _allgather("gather_weights", shards)
    p = {n: _gathered_full(g3, ax) for (n, ax), g3 in zip(sharded, gathered)}
    for n in REPLICATED:
        p[n] = w[n].reshape(-1)

    zpad = jnp.zeros((rows - N_META - seq, D_MODEL), F32)
    xcat = jnp.concatenate([p["meta_tokens"], x[0], zpad], axis=0)
    tgt = jnp.concatenate([jnp.zeros((N_META, D_MODEL), F32), loss_target[0], zpad], axis=0)
    loss_part, dxcat, g = _local_step(xcat, tgt, seq, p)
    grad_x = dxcat[N_META:N_META + seq][None]
    g["meta_tokens"] = dxcat[:N_META]
    loss = lax.psum(loss_part[0, 0], ("x", "y", "c"))

    g_recv = _exchange("exchange_grads", [_shard_major(g[n], ax).astype(MXU_DTYPE) for n, ax in sharded])
    rep_all = _allgather("gather_small_grads", [_pack([g[n].reshape(-1) for n in REPLICATED])])[0]

    res = {}
    for (n, _), parts in zip(sharded, g_recv):
        res[n] = _adamw("adamw_" + n, parts, w[n], m[n], v[n])
    rep_w = _pack([w[n].reshape(-1) for n in REPLICATED])
    rep_m = _pack([m[n].reshape(-1) for n in REPLICATED])
    rep_v = _pack([v[n].reshape(-1) for n in REPLICATED])
    rep_res = _adamw("adamw_replicated", rep_all, rep_w, rep_m, rep_v)
    off = 0
    for n in REPLICATED:
        sz = w[n].size
        res[n] = tuple(a.reshape(-1)[off:off + sz] for a in rep_res)
        off += sz

    outs = [loss, grad_x]
    for idx in range(4):
        outs += [res[n][idx].reshape(out_shape[n]) for n in names]
    return tuple(outs)
```

```python
import jax
import jax.numpy as jnp
from jax import lax
from jax.experimental import pallas as pl
from jax.experimental.pallas import tpu as pltpu

F32 = jnp.float32
BF16 = jnp.bfloat16
MXU_DTYPE = BF16

N_DEV = 8
N_META = 16
D_MODEL = 1024
HEADS = 8
Q_RANK = 384
KV_RANK = 128
NOPE = 64
ROPE = 32
HALF = ROPE // 2
QK_DIM = NOPE + ROPE
V_DIM = 64
FOX_DIM = 64
FOX_W = HEADS * FOX_DIM
D_FF = 2816
ROPE_THETA = 10000.0
LN_EPS = 1e-5
RMS_EPS = 1e-6
ALPHA = 2.0 ** 0.25
NEG_INF = -1e30

HP = 128
HW = HEADS * HP
F_W = 3 * HW
R_QLAT = 0
R_KVLAT = Q_RANK
R_LAST = R_KVLAT + KV_RANK
R_GATE = D_MODEL
R_W = R_GATE + 2 * D_MODEL
LANE_FL = 0
LANE_PE = NOPE

ADAM_LR = 0.001
ADAM_B1 = 0.9
ADAM_B2 = 0.999
ADAM_EPS = 1e-08
ADAM_WD = 0.01
ADAM_STEP = 10

ROW_BLOCK = 128
ATT_TQ = 256
ATT_TK = 256
ATT_HEADS = 2
ROW_ALIGN = 256
VMEM_LIMIT = 56 * 1024 * 1024
HIGHEST = lax.Precision.HIGHEST
NT = (((1,), (1,)), ((), ()))
TN = (((0,), (0,)), ((), ()))


def _params(sem=None):
    return pltpu.CompilerParams(dimension_semantics=sem, vmem_limit_bytes=VMEM_LIMIT)


def _call(name, body, grid, ins, outs, scratch=(), sem=None):
    return pl.pallas_call(
        body, name=name, grid=grid,
        in_specs=[s for _, s in ins],
        out_specs=[s for _, s in outs],
        out_shape=[o for o, _ in outs],
        scratch_shapes=list(scratch),
        compiler_params=_params(sem),
    )(*[a for a, _ in ins])


def _sds(shape, dtype):
    return jax.ShapeDtypeStruct(shape, dtype)


def _rows(br, c, cb=0):
    return pl.BlockSpec((br, c), lambda i: (i, cb))


def _whole(shape):
    n = len(shape)
    return pl.BlockSpec(shape, lambda i: (0,) * n)


def _pick(dim, cap, mult):
    best = None
    d = mult
    while d <= min(dim, cap):
        if dim % d == 0:
            best = d
        d += mult
    return best if best is not None else dim


def _hs(h):
    return slice(h * HP, (h + 1) * HP)


def _matmul(name, a, b, *, ta=False, tb=False, out_dtype=F32, addend=None, alpha=1.0):
    if ta:
        k, m = a.shape
    else:
        m, k = a.shape
    if tb:
        n, k2 = b.shape
    else:
        k2, n = b.shape
    assert k == k2, (name, a.shape, b.shape)
    bm = _pick(m, 1152, 128 if ta else 16)
    bn = _pick(n, 512, 128)
    bk = _pick(k, 1152, 128 if (not ta or tb) else 16)
    nk = k // bk
    dims = (((0 if ta else 1,), (1 if tb else 0,)), ((), ()))
    has_add = addend is not None

    def body(*refs):
        if has_add:
            a_ref, b_ref, add_ref, o_ref, acc_ref = refs
        else:
            a_ref, b_ref, o_ref, acc_ref = refs
        kk = pl.program_id(2)

        @pl.when(kk == 0)
        def _():
            acc_ref[...] = jnp.zeros_like(acc_ref)

        acc_ref[...] += lax.dot_general(a_ref[...], b_ref[...], dims, preferred_element_type=F32)

        @pl.when(kk == nk - 1)
        def _():
            r = acc_ref[...]
            if has_add:
                r = r + alpha * add_ref[...]
            o_ref[...] = r.astype(o_ref.dtype)

    a_spec = pl.BlockSpec((bk, bm), lambda i, j, l: (l, i)) if ta else pl.BlockSpec((bm, bk), lambda i, j, l: (i, l))
    b_spec = pl.BlockSpec((bn, bk), lambda i, j, l: (j, l)) if tb else pl.BlockSpec((bk, bn), lambda i, j, l: (l, j))
    o_spec = pl.BlockSpec((bm, bn), lambda i, j, l: (i, j))
    ins = [(a, a_spec), (b, b_spec)]
    if has_add:
        ins.append((addend, o_spec))
    return _call(name, body, (m // bm, n // bn, nk), ins, [(_sds((m, n), out_dtype), o_spec)],
                 scratch=[pltpu.VMEM((bm, bn), F32)], sem=("parallel", "parallel", "arbitrary"))[0]


def _ln_stats(z):
    mu = jnp.mean(z, axis=-1, keepdims=True)
    zc = z - mu
    var = jnp.mean(zc * zc, axis=-1, keepdims=True)
    rstd = lax.rsqrt(var + LN_EPS)
    return zc * rstd, rstd


def _ln_fwd(name, a, res, g, b):
    r, d = a.shape
    br = ROW_BLOCK
    has_res = res is not None

    def body(*refs):
        if has_res:
            a_ref, r_ref, g_ref, b_ref, y_ref, yb_ref = refs
            z = ALPHA * a_ref[...] + r_ref[...]
        else:
            a_ref, g_ref, b_ref, y_ref, yb_ref = refs
            z = a_ref[...]
        xhat, _ = _ln_stats(z)
        y = xhat * g_ref[...] + b_ref[...]
        y_ref[...] = y
        yb_ref[...] = y.astype(yb_ref.dtype)

    ins = [(a, _rows(br, d))]
    if has_res:
        ins.append((res, _rows(br, d)))
    ins += [(g.reshape(1, d), _whole((1, d))), (b.reshape(1, d), _whole((1, d)))]
    outs = [(_sds((r, d), F32), _rows(br, d)), (_sds((r, d), MXU_DTYPE), _rows(br, d))]
    return _call(name, body, (r // br,), ins, outs, sem=("parallel",))


def _ln_bwd(name, a, res, dy, g):
    r, d = a.shape
    br = ROW_BLOCK
    has_res = res is not None

    def body(*refs):
        if has_res:
            a_ref, r_ref, dy_ref, g_ref, dz_ref, dzb_ref, dg_ref, db_ref = refs
            z = ALPHA * a_ref[...] + r_ref[...]
        else:
            a_ref, dy_ref, g_ref, dz_ref, dzb_ref, dg_ref, db_ref = refs
            z = a_ref[...]
        xhat, rstd = _ln_stats(z)
        dyv = dy_ref[...]
        dyg = dyv * g_ref[...]
        m1 = jnp.mean(dyg, axis=-1, keepdims=True)
        m2 = jnp.mean(dyg * xhat, axis=-1, keepdims=True)
        dz = rstd * (dyg - m1 - xhat * m2)
        dz_ref[...] = dz
        dzb_ref[...] = dz.astype(dzb_ref.dtype)

        @pl.when(pl.program_id(0) == 0)
        def _():
            dg_ref[...] = jnp.zeros_like(dg_ref)
            db_ref[...] = jnp.zeros_like(db_ref)

        dg_ref[...] += jnp.sum(dyv * xhat, axis=0, keepdims=True)
        db_ref[...] += jnp.sum(dyv, axis=0, keepdims=True)

    ins = [(a, _rows(br, d))]
    if has_res:
        ins.append((res, _rows(br, d)))
    ins += [(dy, _rows(br, d)), (g.reshape(1, d), _whole((1, d)))]
    outs = [(_sds((r, d), F32), _rows(br, d)), (_sds((r, d), MXU_DTYPE), _rows(br, d)),
            (_sds((1, d), F32), _whole((1, d))), (_sds((1, d), F32), _whole((1, d)))]
    return _call(name, body, (r // br,), ins, outs, sem=("arbitrary",))


def _rms_fwd(name, proj, cb, width, g):
    r = proj.shape[0]
    br = ROW_BLOCK

    def body(x_ref, g_ref, y_ref):
        x = x_ref[...]
        rstd = lax.rsqrt(jnp.mean(x * x, axis=-1, keepdims=True) + RMS_EPS)
        y_ref[...] = (x * rstd * g_ref[...]).astype(y_ref.dtype)

    return _call(name, body, (r // br,), [(proj, _rows(br, width, cb)), (g.reshape(1, width), _whole((1, width)))],
                 [(_sds((r, width), MXU_DTYPE), _rows(br, width))], sem=("parallel",))[0]


def _rms_bwd(name, proj, cb, width, dy, g):
    r = proj.shape[0]
    br = ROW_BLOCK

    def body(x_ref, dy_ref, g_ref, dx_ref, dg_ref):
        x = x_ref[...]
        rstd = lax.rsqrt(jnp.mean(x * x, axis=-1, keepdims=True) + RMS_EPS)
        nrm = x * rstd
        dyv = dy_ref[...]
        dyg = dyv * g_ref[...]
        dx = rstd * (dyg - nrm * jnp.mean(dyg * nrm, axis=-1, keepdims=True))
        dx_ref[...] = dx.astype(dx_ref.dtype)

        @pl.when(pl.program_id(0) == 0)
        def _():
            dg_ref[...] = jnp.zeros_like(dg_ref)

        dg_ref[...] += jnp.sum(dyv * nrm, axis=0, keepdims=True)

    return _call(name, body, (r // br,),
                 [(proj, _rows(br, width, cb)), (dy, _rows(br, width)), (g.reshape(1, width), _whole((1, width)))],
                 [(_sds((r, width), MXU_DTYPE), _rows(br, width)), (_sds((1, width), F32), _whole((1, width)))],
                 sem=("arbitrary",))


def _lane_iota(shape):
    return lax.broadcasted_iota(jnp.int32, shape, 1)


def _rotary(t, c, s, lane, sign):
    second = pltpu.roll(t, HP - HALF, axis=1)
    first = pltpu.roll(t, HALF, axis=1)
    lo = (lane >= LANE_PE) & (lane < LANE_PE + HALF)
    hi = (lane >= LANE_PE + HALF) & (lane < LANE_PE + ROPE)
    return jnp.where(lo, t * c - sign * second * s, jnp.where(hi, t * c + sign * first * s, t))


def _rope_fwd(q_raw, k_part, proj_r, cos_t, sin_t):
    r = q_raw.shape[0]
    br = ROW_BLOCK

    def body(q_ref, k_ref, t_ref, c_ref, s_ref, qo_ref, ko_ref):
        c = c_ref[...]
        s = s_ref[...]
        lane = _lane_iota((br, HP))
        pe = (lane >= LANE_PE) & (lane < LANE_PE + ROPE)
        kp = jnp.where(pe, _rotary(t_ref[...], c, s, lane, 1.0), 0.0)
        for h in range(HEADS):
            qo_ref[:, _hs(h)] = _rotary(q_ref[:, _hs(h)], c, s, lane, 1.0).astype(qo_ref.dtype)
            ko_ref[:, _hs(h)] = (k_ref[:, _hs(h)] + kp).astype(ko_ref.dtype)

    blk = _rows(br, HP)
    wide = _rows(br, HW)
    return _call("rope_fwd", body, (r // br,),
                 [(q_raw, wide), (k_part, wide), (proj_r, _rows(br, HP, R_LAST // HP)), (cos_t, blk), (sin_t, blk)],
                 [(_sds((r, HW), MXU_DTYPE), wide)] * 2, sem=("parallel",))


def _rope_bwd(dq, dk, dfl, cos_t, sin_t):
    r = dq.shape[0]
    br = ROW_BLOCK

    def body(dq_ref, dk_ref, fl_ref, c_ref, s_ref, dqo_ref, dko_ref, dl_ref):
        c = c_ref[...]
        s = s_ref[...]
        lane = _lane_iota((br, HP))
        pe = (lane >= LANE_PE) & (lane < LANE_PE + ROPE)
        acc = jnp.zeros((br, HP), F32)
        for h in range(HEADS):
            dqo_ref[:, _hs(h)] = _rotary(dq_ref[:, _hs(h)], c, s, lane, -1.0).astype(dqo_ref.dtype)
            dkh = dk_ref[:, _hs(h)]
            acc = acc + dkh
            dko_ref[:, _hs(h)] = dkh.astype(dko_ref.dtype)
        dl_ref[...] = (jnp.where(pe, _rotary(acc, c, s, lane, -1.0), 0.0) + fl_ref[...]).astype(dl_ref.dtype)

    blk = _rows(br, HP)
    wide = _rows(br, HW)
    return _call("rope_bwd", body, (r // br,),
                 [(dq, wide), (dk, wide), (dfl, blk), (cos_t, blk), (sin_t, blk)],
                 [(_sds((r, HW), MXU_DTYPE), wide), (_sds((r, HW), MXU_DTYPE), wide), (_sds((r, HP), MXU_DTYPE), blk)],
                 sem=("parallel",))


def _log_sigmoid(x):
    return jnp.minimum(x, 0.0) - jnp.log(1.0 + jnp.exp(-jnp.abs(x)))


def _head_lane(x, h, lane):
    return jnp.sum(jnp.where(lane == h, x, 0.0), axis=1, keepdims=True)


def _forget_fwd(proj_r, bf_row):
    r = proj_r.shape[0]
    br = ROW_BLOCK

    def body(t_ref, b_ref, ob_ref, ot_ref, carry_ref):
        @pl.when(pl.program_id(0) == 0)
        def _():
            carry_ref[...] = jnp.zeros_like(carry_ref)

        x = t_ref[...] + b_ref[...]
        lane = _lane_iota(x.shape)
        lf = jnp.where((lane >= LANE_FL) & (lane < LANE_FL + HEADS), _log_sigmoid(x), 0.0)
        tri = (lax.broadcasted_iota(jnp.int32, (br, br), 0) >= lax.broadcasted_iota(jnp.int32, (br, br), 1)).astype(F32)
        cum = jnp.dot(tri, lf, precision=HIGHEST, preferred_element_type=F32) + carry_ref[0:1, :]
        for h in range(HEADS):
            ob_ref[:, _hs(h)] = jnp.broadcast_to(_head_lane(cum, LANE_FL + h, lane), (br, HP))
        ot_ref[...] = cum.T[LANE_FL:LANE_FL + HEADS, :]
        carry_ref[...] = jnp.broadcast_to(cum[br - 1:br, :], carry_ref.shape)

    return _call("forget_fwd", body, (r // br,),
                 [(proj_r, _rows(br, HP, R_LAST // HP)), (bf_row, _whole((1, HP)))],
                 [(_sds((r, HW), F32), _rows(br, HW)), (_sds((HEADS, r), F32), pl.BlockSpec((HEADS, br), lambda i: (0, i)))],
                 scratch=[pltpu.VMEM((8, HP), F32)], sem=("arbitrary",))


def _forget_bwd(proj_r, bf_row, dcq_t, dck_b):
    r = proj_r.shape[0]
    br = ROW_BLOCK
    nb = r // br

    def body(t_ref, b_ref, dcq_ref, dck_ref, o_ref, db_ref, carry_ref):
        @pl.when(pl.program_id(0) == 0)
        def _():
            carry_ref[...] = jnp.zeros_like(carry_ref)
            db_ref[...] = jnp.zeros_like(db_ref)

        lane = _lane_iota((br, HP))
        dc = jnp.concatenate([dcq_ref[...], jnp.zeros((br - HEADS, br), F32)], axis=0).T
        for h in range(HEADS):
            dc = dc + jnp.where(lane == LANE_FL + h, dck_ref[:, h * HP:h * HP + 1], 0.0)
        triu = (lax.broadcasted_iota(jnp.int32, (br, br), 0) <= lax.broadcasted_iota(jnp.int32, (br, br), 1)).astype(F32)
        dlf = jnp.dot(triu, dc, precision=HIGHEST, preferred_element_type=F32) + carry_ref[0:1, :]
        carry_ref[...] = jnp.broadcast_to(dlf[0:1, :], carry_ref.shape)
        x = t_ref[...] + b_ref[...]
        dfl = jnp.where((lane >= LANE_FL) & (lane < LANE_FL + HEADS), dlf * jax.nn.sigmoid(-x), 0.0)
        o_ref[...] = dfl
        db_ref[...] += jnp.sum(dfl, axis=0, keepdims=True)

    rev = pl.BlockSpec((br, HP), lambda i: (nb - 1 - i, 0))
    return _call("forget_bwd", body, (nb,),
                 [(proj_r, pl.BlockSpec((br, HP), lambda i: (nb - 1 - i, R_LAST // HP))), (bf_row, _whole((1, HP))),
                  (dcq_t, pl.BlockSpec((HEADS, br), lambda i: (0, nb - 1 - i))),
                  (dck_b, pl.BlockSpec((br, HW), lambda i: (nb - 1 - i, 0)))],
                 [(_sds((r, HP), F32), rev), (_sds((1, HP), F32), _whole((1, HP)))],
                 scratch=[pltpu.VMEM((8, HP), F32)], sem=("arbitrary",))


def _gate_fwd(proj_r, b_gate, bm, bfx):
    r, d = bm.shape
    br = ROW_BLOCK
    cb = R_GATE // d

    def body(gm_ref, gf_ref, b1_ref, b2_ref, bm_ref, bf_ref, o_ref):
        g1 = jax.nn.sigmoid(gm_ref[...] + b1_ref[...])
        g2 = jax.nn.sigmoid(gf_ref[...] + b2_ref[...])
        o_ref[...] = (g1 * bm_ref[...] + g2 * bf_ref[...]).astype(o_ref.dtype)

    b1 = b_gate[:d].reshape(1, d)
    b2 = b_gate[d:].reshape(1, d)
    return _call("gate_fwd", body, (r // br,),
                 [(proj_r, _rows(br, d, cb)), (proj_r, _rows(br, d, cb + 1)), (b1, _whole((1, d))), (b2, _whole((1, d))),
                  (bm, _rows(br, d)), (bfx, _rows(br, d))],
                 [(_sds((r, d), MXU_DTYPE), _rows(br, d))], sem=("parallel",))[0]


def _gate_bwd(proj_r, b_gate, bm, bfx, dmerged):
    r, d = bm.shape
    br = ROW_BLOCK
    cb = R_GATE // d

    def body(gm_ref, gf_ref, b1_ref, b2_ref, bm_ref, bf_ref, dm_ref, dbm_ref, dbf_ref, dgl_ref, dbg_ref):
        g1 = jax.nn.sigmoid(gm_ref[...] + b1_ref[...])
        g2 = jax.nn.sigmoid(gf_ref[...] + b2_ref[...])
        dm = dm_ref[...]
        dbm_ref[...] = (dm * g1).astype(dbm_ref.dtype)
        dbf_ref[...] = (dm * g2).astype(dbf_ref.dtype)
        dl1 = dm * bm_ref[...] * (g1 * (1.0 - g1))
        dl2 = dm * bf_ref[...] * (g2 * (1.0 - g2))
        dgl_ref[:, 0:d] = dl1.astype(dgl_ref.dtype)
        dgl_ref[:, d:2 * d] = dl2.astype(dgl_ref.dtype)

        @pl.when(pl.program_id(0) == 0)
        def _():
            dbg_ref[...] = jnp.zeros_like(dbg_ref)

        dbg_ref[:, 0:d] += jnp.sum(dl1, axis=0, keepdims=True)
        dbg_ref[:, d:2 * d] += jnp.sum(dl2, axis=0, keepdims=True)

    b1 = b_gate[:d].reshape(1, d)
    b2 = b_gate[d:].reshape(1, d)
    return _call("gate_bwd", body, (r // br,),
                 [(proj_r, _rows(br, d, cb)), (proj_r, _rows(br, d, cb + 1)), (b1, _whole((1, d))), (b2, _whole((1, d))),
                  (bm, _rows(br, d)), (bfx, _rows(br, d)), (dmerged, _rows(br, d))],
                 [(_sds((r, d), MXU_DTYPE), _rows(br, d)), (_sds((r, d), MXU_DTYPE), _rows(br, d)),
                  (_sds((r, 2 * d), MXU_DTYPE), _rows(br, 2 * d)), (_sds((1, 2 * d), F32), _whole((1, 2 * d)))],
                 sem=("arbitrary",))


def _conv_taps(gp, halo, first_block):
    halo = jnp.where(first_block, 0.0, halo)
    rid = lax.broadcasted_iota(jnp.int32, gp.shape, 0)
    g1 = jnp.where(rid == 0, halo[7:8, :], pltpu.roll(gp, 1, axis=0))
    g2 = jnp.where(rid == 0, halo[6:7, :], jnp.where(rid == 1, halo[7:8, :], pltpu.roll(gp, 2, axis=0)))
    return g1, g2


def _prev_halo(br, c):
    return pl.BlockSpec((8, c), lambda i: (jnp.maximum(i * (br // 8) - 1, 0), 0))


def _glu_fwd(up, conv_w, conv_b):
    r = up.shape[0]
    c = D_FF
    br = ROW_BLOCK

    def body(gp_ref, halo_ref, val_ref, w_ref, b_ref, o_ref):
        gp = gp_ref[...]
        g1, g2 = _conv_taps(gp, halo_ref[...], pl.program_id(0) == 0)
        gate = w_ref[0:1, :] * g2 + w_ref[1:2, :] * g1 + w_ref[2:3, :] * gp + b_ref[...]
        o_ref[...] = (gate * jax.nn.sigmoid(gate) * val_ref[...]).astype(o_ref.dtype)

    return _call("glu_fwd", body, (r // br,),
                 [(up, _rows(br, c, 0)), (up, _prev_halo(br, c)), (up, _rows(br, c, 1)),
                  (conv_w, _whole((3, c))), (conv_b.reshape(1, c), _whole((1, c)))],
                 [(_sds((r, c), MXU_DTYPE), _rows(br, c))], sem=("parallel",))[0]


def _glu_bwd_gate(up, conv_w, conv_b, d_act):
    r = up.shape[0]
    c = D_FF
    br = ROW_BLOCK

    def body(gp_ref, halo_ref, val_ref, w_ref, b_ref, da_ref, dg_ref, dv_ref, dw_ref, db_ref):
        gp = gp_ref[...]
        g1, g2 = _conv_taps(gp, halo_ref[...], pl.program_id(0) == 0)
        gate = w_ref[0:1, :] * g2 + w_ref[1:2, :] * g1 + w_ref[2:3, :] * gp + b_ref[...]
        sg = jax.nn.sigmoid(gate)
        da = da_ref[...]
        dv_ref[...] = (da * (gate * sg)).astype(dv_ref.dtype)
        dg = da * val_ref[...] * (sg * (1.0 + gate * (1.0 - sg)))
        dg_ref[...] = dg

        @pl.when(pl.program_id(0) == 0)
        def _():
            dw_ref[...] = jnp.zeros_like(dw_ref)
            db_ref[...] = jnp.zeros_like(db_ref)

        dw_ref[0:1, :] += jnp.sum(dg * g2, axis=0, keepdims=True)
        dw_ref[1:2, :] += jnp.sum(dg * g1, axis=0, keepdims=True)
        dw_ref[2:3, :] += jnp.sum(dg * gp, axis=0, keepdims=True)
        db_ref[...] += jnp.sum(dg, axis=0, keepdims=True)

    return _call("glu_bwd_gate", body, (r // br,),
                 [(up, _rows(br, c, 0)), (up, _prev_halo(br, c)), (up, _rows(br, c, 1)),
                  (conv_w, _whole((3, c))), (conv_b.reshape(1, c), _whole((1, c))), (d_act, _rows(br, c))],
                 [(_sds((r, c), F32), _rows(br, c)), (_sds((r, c), MXU_DTYPE), _rows(br, c)),
                  (_sds((8, c), F32), _whole((8, c))), (_sds((1, c), F32), _whole((1, c)))],
                 sem=("arbitrary",))


def _glu_bwd_conv(dg, dval, conv_w):
    r, c = dg.shape
    br = ROW_BLOCK
    nb = r // br

    def body(dg_ref, nxt_ref, dv_ref, w_ref, o_ref):
        x = dg_ref[...]
        nxt = jnp.where(pl.program_id(0) == nb - 1, 0.0, nxt_ref[...])
        rid = lax.broadcasted_iota(jnp.int32, x.shape, 0)
        u1 = jnp.where(rid == br - 1, nxt[0:1, :], pltpu.roll(x, br - 1, axis=0))
        u2 = jnp.where(rid == br - 1, nxt[1:2, :], jnp.where(rid == br - 2, nxt[0:1, :], pltpu.roll(x, br - 2, axis=0)))
        dgp = w_ref[2:3, :] * x + w_ref[1:2, :] * u1 + w_ref[0:1, :] * u2
        o_ref[:, 0:c] = dgp.astype(o_ref.dtype)
        o_ref[:, c:2 * c] = dv_ref[...]

    nxt_spec = pl.BlockSpec((8, c), lambda i: (jnp.minimum((i + 1) * (br // 8), r // 8 - 1), 0))
    return _call("glu_bwd_conv", body, (nb,),
                 [(dg, _rows(br, c)), (dg, nxt_spec), (dval, _rows(br, c)), (conv_w, _whole((3, c)))],
                 [(_sds((r, 2 * c), MXU_DTYPE), _rows(br, 2 * c))], sem=("parallel",))[0]


def _loss(h2, tgt, seq):
    r, d = h2.shape
    br = ROW_BLOCK

    def body(h_ref, t_ref, l_ref, d_ref):
        rid = lax.broadcasted_iota(jnp.int32, (br, d), 0) + pl.program_id(0) * br
        valid = (rid >= N_META) & (rid < N_META + seq)
        err = jnp.where(valid, h_ref[...] - t_ref[...], 0.0)
        d_ref[...] = err * (1.0 / d)

        @pl.when(pl.program_id(0) == 0)
        def _():
            l_ref[...] = jnp.zeros_like(l_ref)

        l_ref[...] += jnp.sum(jnp.sum(err * err, axis=1, keepdims=True), axis=0, keepdims=True) * (0.5 / d)

    return _call("loss", body, (r // br,), [(h2, _rows(br, d)), (tgt, _rows(br, d))],
                 [(_sds((1, 1), F32), _whole((1, 1))), (_sds((r, d), F32), _rows(br, d))], sem=("arbitrary",))


def _attn_fwd(name, q, k, v, scale, cum_b=None, cum_t=None):
    (qa, qg), (ka, kg), (va, vg) = q, k, v
    r = qa.shape[0]
    tq, tk = ATT_TQ, ATT_TK
    nq, nk = r // tq, r // tk
    bias = cum_b is not None

    def body(*refs):
        if bias:
            q_ref, k_ref, v_ref, cb_ref, ct_ref, o_ref, ob_ref, lse_ref = refs
        else:
            q_ref, k_ref, v_ref, o_ref, ob_ref, lse_ref = refs
        i = pl.program_id(1)
        qs = [q_ref[:, _hs(hh)] for hh in range(hg)]
        cqs = [cb_ref[:, hh * HP:hh * HP + 1] for hh in range(hg)] if bias else None
        diff = lax.broadcasted_iota(jnp.int32, (tq, tk), 1) - lax.broadcasted_iota(jnp.int32, (tq, tk), 0)

        def step(j, carry, masked):
            off = pl.multiple_of(j * tk, tk)
            out = []
            for hh in range(hg):
                m, l, acc = carry[hh]
                kt = k_ref[pl.ds(off, tk), _hs(hh)]
                vt = v_ref[pl.ds(off, tk), _hs(hh)]
                s = lax.dot_general(qs[hh], kt, NT, preferred_element_type=F32) * scale
                if bias:
                    s = s + (cqs[hh] - ct_ref[hh, j])
                if masked:
                    s = jnp.where(diff <= i * tq - j * tk, s, NEG_INF)
                m_new = jnp.maximum(m, jnp.max(s, axis=1, keepdims=True))
                p = jnp.exp(s - m_new)
                a = jnp.exp(m - m_new)
                l = a * l + jnp.sum(p, axis=1, keepdims=True)
                acc = a * acc + jnp.dot(p.astype(vt.dtype), vt, preferred_element_type=F32)
                out.append((m_new, l, acc))
            return tuple(out)

        n_clear = (i * tq + 1) // tk
        n_all = ((i + 1) * tq - 1) // tk + 1
        carry = tuple((jnp.full((tq, 1), NEG_INF, F32), jnp.zeros((tq, 1), F32), jnp.zeros((tq, HP), F32))
                      for _ in range(hg))
        carry = lax.fori_loop(0, n_clear, lambda j, c: step(j, c, False), carry)
        carry = lax.fori_loop(n_clear, n_all, lambda j, c: step(j, c, True), carry)
        for hh in range(hg):
            m, l, acc = carry[hh]
            o = acc / l
            o_ref[:, _hs(hh)] = o
            ob_ref[:, _hs(hh)] = o.astype(ob_ref.dtype)
            lse_ref[hh] = jnp.broadcast_to(m + jnp.log(l), (tq, HP)).T[0:1, :]

    hg = ATT_HEADS
    w = hg * HP
    gpw = HW // w
    tile = lambda g: pl.BlockSpec((tq, w), lambda h, i: (i, g * gpw + h))
    res = lambda g: pl.BlockSpec((r, w), lambda h, i: (0, g * gpw + h))
    ins = [(qa, tile(qg)), (ka, res(kg)), (va, res(vg))]
    if bias:
        ins += [(cum_b, tile(0)),
                (cum_t.reshape(HEADS, nk, 1, tk), pl.BlockSpec((hg, nk, 1, tk), lambda h, i: (h, 0, 0, 0)))]
    outs = [(_sds((r, HW), F32), tile(0)), (_sds((r, HW), MXU_DTYPE), tile(0)),
            (_sds((HEADS, nq, 1, tq), F32), pl.BlockSpec((hg, None, 1, tq), lambda h, i: (h, i, 0, 0)))]
    o, ob, lse = _call(name, body, (gpw, nq), ins, outs, sem=("parallel", "parallel"))
    return o, ob, lse.reshape(HEADS, r)


def _attn_delta(name, do, o):
    r = do.shape[0]
    br = ROW_BLOCK

    def body(do_ref, o_ref, d_ref, dob_ref):
        lane = _lane_iota((br, HP))
        d = jnp.zeros((br, HP), F32)
        for h in range(HEADS):
            dh = do_ref[:, _hs(h)]
            d = jnp.where(lane == h, jnp.sum(dh * o_ref[:, _hs(h)], axis=1, keepdims=True), d)
            dob_ref[:, _hs(h)] = dh.astype(dob_ref.dtype)
        d_ref[...] = d.T[0:HEADS, :]

    wide = _rows(br, HW)
    return _call(name, body, (r // br,), [(do, wide), (o, wide)],
                 [(_sds((HEADS, r), F32), pl.BlockSpec((HEADS, br), lambda i: (0, i))), (_sds((r, HW), MXU_DTYPE), wide)],
                 sem=("parallel",))


def _attn_bwd(name, q, k, v, do_b, lse_t, delta_t, scale, cum_b=None, cum_t=None):
    (qa, qg), (ka, kg), (va, vg) = q, k, v
    r = qa.shape[0]
    tq, tk = ATT_TQ, ATT_TK
    nq, nk = r // tq, r // tk
    bias = cum_b is not None

    def body(*refs):
        if bias:
            (q_ref, k_ref, v_ref, do_ref, lse_ref, dl_ref, cb_ref, ct_ref,
             dq_ref, dk_ref, dv_ref, dcq_ref, dck_ref, dqt_ref) = refs
        else:
            q_ref, k_ref, v_ref, do_ref, lse_ref, dl_ref, dq_ref, dk_ref, dv_ref, dqt_ref = refs
        j = pl.program_id(1)

        @pl.when(j == 0)
        def _():
            dqt_ref[...] = jnp.zeros_like(dqt_ref)
            if bias:
                dcq_ref[...] = jnp.zeros_like(dcq_ref)

        kts = [k_ref[:, _hs(hh)] for hh in range(hg)]
        vts = [v_ref[:, _hs(hh)] for hh in range(hg)]
        k_trs = [kt.astype(F32).T.astype(kt.dtype) for kt in kts]
        cks = [cb_ref[:, hh * HP:hh * HP + 1] for hh in range(hg)] if bias else None
        diff = lax.broadcasted_iota(jnp.int32, (tk, tq), 0) - lax.broadcasted_iota(jnp.int32, (tk, tq), 1)

        def step(i, carry, masked):
            rows = pl.ds(pl.multiple_of(i * tq, tq), tq)
            out = []
            for hh in range(hg):
                dk_acc, dv_acc, dck_acc = carry[hh]
                qt = q_ref[rows, _hs(hh)]
                dot = do_ref[rows, _hs(hh)]
                s = lax.dot_general(kts[hh], qt, NT, preferred_element_type=F32) * scale
                if bias:
                    s = s + (ct_ref[hh, i] - cks[hh])
                if masked:
                    s = jnp.where(diff <= i * tq - j * tk, s, NEG_INF)
                p = jnp.exp(s - lse_ref[hh, i])
                dp = lax.dot_general(vts[hh], dot, NT, preferred_element_type=F32)
                ds = p * (dp - dl_ref[hh, i])
                pb = p.astype(dot.dtype)
                dsb = ds.astype(qt.dtype)
                dv_acc = dv_acc + jnp.dot(pb, dot, preferred_element_type=F32)
                dk_acc = dk_acc + jnp.dot(dsb, qt, preferred_element_type=F32)
                dqt_ref[hh, i] += jnp.dot(k_trs[hh], dsb, preferred_element_type=F32)
                if bias:
                    dcq_ref[hh, i] += jnp.sum(ds, axis=0, keepdims=True)
                    dck_acc = dck_acc - jnp.sum(ds, axis=1, keepdims=True)
                out.append((dk_acc, dv_acc, dck_acc))
            return tuple(out)

        i_first = (j * tk) // tq
        i_clear = jnp.minimum(((j + 1) * tk + tq - 2) // tq, nq)
        carry = tuple((jnp.zeros((tk, HP), F32), jnp.zeros((tk, HP), F32), jnp.zeros((tk, 1), F32)) for _ in range(hg))
        carry = lax.fori_loop(i_first, i_clear, lambda i, c: step(i, c, True), carry)
        carry = lax.fori_loop(i_clear, nq, lambda i, c: step(i, c, False), carry)
        for hh in range(hg):
            dk_acc, dv_acc, dck_acc = carry[hh]
            dk_ref[:, _hs(hh)] = dk_acc * scale
            dv_ref[:, _hs(hh)] = dv_acc
            if bias:
                dck_ref[:, _hs(hh)] = jnp.broadcast_to(dck_acc, (tk, HP))

        @pl.when(j == nk - 1)
        def _():
            for hh in range(hg):
                for i in range(nq):
                    dq_ref[i * tq:(i + 1) * tq, _hs(hh)] = dqt_ref[hh, i].T * scale

    hg = ATT_HEADS
    w = hg * HP
    gpw = HW // w
    res = lambda g: pl.BlockSpec((r, w), lambda h, j: (0, g * gpw + h))
    tile = lambda g: pl.BlockSpec((tk, w), lambda h, j: (j, g * gpw + h))
    rowv = pl.BlockSpec((hg, nq, 1, tq), lambda h, j: (h, 0, 0, 0))
    as_rows = lambda a: a.reshape(HEADS, nq, 1, tq)
    ins = [(qa, res(qg)), (ka, tile(kg)), (va, tile(vg)), (do_b, res(0)), (as_rows(lse_t), rowv), (as_rows(delta_t), rowv)]
    outs = [(_sds((r, HW), F32), res(0)), (_sds((r, HW), F32), tile(0)), (_sds((r, HW), F32), tile(0))]
    if bias:
        ins += [(cum_b, tile(0)), (as_rows(cum_t), rowv)]
        outs += [(_sds((HEADS, nq, 1, tq), F32), rowv), (_sds((r, HW), F32), tile(0))]
    res_out = _call(name, body, (gpw, nk), ins, outs, scratch=[pltpu.VMEM((hg, nq, HP, tq), F32)],
                    sem=("parallel", "arbitrary"))
    if bias:
        dq, dk, dv, dcq, dck = res_out
        return dq, dk, dv, dcq.reshape(HEADS, r), dck
    return res_out


MESH_ID = pl.DeviceIdType.MESH
ANY = pl.BlockSpec(memory_space=pl.ANY)


def _allgather(name, shards):
    n = len(shards)

    def body(*refs):
        x_refs, out_refs = refs[:n], refs[n:2 * n]
        send_sems, recv_sems, local_sems = refs[2 * n:]
        x, y, c = lax.axis_index("x"), lax.axis_index("y"), lax.axis_index("c")
        me, sibling = (x, y, c), (x, y, 1 - c)
        chips = [(1 - x, y), (x, 1 - y), (1 - x, 1 - y)]

        def slot(ti, px, py, pc):
            return out_refs[ti].at[4 * px + 2 * py + pc]

        def copy(ti, k, block, to, src=None):
            return pltpu.make_async_remote_copy(
                src_ref=slot(ti, *block) if src is None else src, dst_ref=slot(ti, *block),
                send_sem=send_sems.at[ti, k], recv_sem=recv_sems.at[ti, k], device_id=to, device_id_type=MESH_ID)

        mine = [pltpu.make_async_copy(x_refs[ti], slot(ti, *me), local_sems.at[ti]) for ti in range(n)]
        for cp in mine:
            cp.start()
        started = []
        for ti in range(n):
            first = [copy(ti, 0, me, sibling, src=x_refs[ti])]
            first += [copy(ti, 1 + j, me, (*chip, c), src=x_refs[ti]) for j, chip in enumerate(chips)]
            for cp in first:
                cp.start()
            started += first
        for ti in range(n):
            for j, chip in enumerate(chips):
                copy(ti, 1 + j, (*chip, c), me).wait_recv()
                fwd = copy(ti, 4 + j, (*chip, c), sibling)
                fwd.start()
                started.append(fwd)
        for ti in range(n):
            copy(ti, 0, sibling, me).wait_recv()
            for j, chip in enumerate(chips):
                copy(ti, 4 + j, (*chip, 1 - c), me).wait_recv()
        for cp in started:
            cp.wait_send()
        for cp in mine:
            cp.wait()

    return pl.pallas_call(
        body, name=name, out_shape=[_sds((N_DEV,) + s.shape, s.dtype) for s in shards],
        in_specs=[ANY] * n, out_specs=[ANY] * n,
        scratch_shapes=[pltpu.SemaphoreType.DMA((n, 7)), pltpu.SemaphoreType.DMA((n, 7)), pltpu.SemaphoreType.DMA((n,))],
    )(*shards)


def _exchange(name, parts):
    n = len(parts)

    def body(*refs):
        g_refs, out_refs = refs[:n], refs[n:2 * n]
        send_sems, recv_sems, local_sems = refs[2 * n:]
        x, y, c = lax.axis_index("x"), lax.axis_index("y"), lax.axis_index("c")
        my_id = 4 * x + 2 * y + c

        def peer(k):
            return (1 - x if k & 4 else x, 1 - y if k & 2 else y, 1 - c if k & 1 else c)

        def copy(ti, k, landing):
            px, py, pc = peer(k)
            pid = 4 * px + 2 * py + pc
            return pltpu.make_async_remote_copy(
                src_ref=g_refs[ti].at[pid], dst_ref=out_refs[ti].at[pid if landing else my_id],
                send_sem=send_sems.at[ti, k - 1], recv_sem=recv_sems.at[ti, k - 1],
                device_id=(px, py, pc), device_id_type=MESH_ID)

        mine = [pltpu.make_async_copy(g_refs[ti].at[my_id], out_refs[ti].at[my_id], local_sems.at[ti]) for ti in range(n)]
        for cp in mine:
            cp.start()
        sends = [copy(ti, k, False) for ti in range(n) for k in range(1, N_DEV)]
        for cp in sends:
            cp.start()
        for ti in range(n):
            for k in range(1, N_DEV):
                copy(ti, k, True).wait_recv()
        for cp in sends:
            cp.wait_send()
        for cp in mine:
            cp.wait()

    return pl.pallas_call(
        body, name=name, out_shape=[_sds(p.shape, p.dtype) for p in parts],
        in_specs=[ANY] * n, out_specs=[ANY] * n,
        scratch_shapes=[pltpu.SemaphoreType.DMA((n, 7)), pltpu.SemaphoreType.DMA((n, 7)), pltpu.SemaphoreType.DMA((n,))],
    )(*parts)


def _adamw(name, parts, w, m, v):
    r, c = w.shape
    br = _pick(r, 256, 16)

    def body(p_ref, w_ref, m_ref, v_ref, g_ref, d_ref, nm_ref, nv_ref):
        g = p_ref[0].astype(F32)
        for k in range(1, N_DEV):
            g = g + p_ref[k].astype(F32)
        mm = ADAM_B1 * m_ref[...] + (1.0 - ADAM_B1) * g
        vv = ADAM_B2 * v_ref[...] + (1.0 - ADAM_B2) * (g * g)
        m_hat = mm / (1.0 - ADAM_B1 ** ADAM_STEP)
        v_hat = vv / (1.0 - ADAM_B2 ** ADAM_STEP)
        g_ref[...] = g
        d_ref[...] = -ADAM_LR * (m_hat / (jnp.sqrt(v_hat) + ADAM_EPS) + ADAM_WD * w_ref[...])
        nm_ref[...] = mm
        nv_ref[...] = vv

    spec = _rows(br, c)
    out = (_sds((r, c), F32), spec)
    return _call(name, body, (r // br,),
                 [(parts, pl.BlockSpec((N_DEV, br, c), lambda i: (0, i, 0))), (w, spec), (m, spec), (v, spec)],
                 [out] * 4, sem=("parallel",))


def _pad_head_cols(w, d):
    k = w.shape[0]
    return jnp.pad(w.reshape(k, HEADS, d), ((0, 0), (0, 0), (0, HP - d))).reshape(k, HW)


def _unpad_head_cols(wp, d):
    k = wp.shape[0]
    return wp.reshape(k, HEADS, HP)[:, :, :d].reshape(k, HEADS * d)


def _pad_head_rows(w, d):
    n = w.shape[1]
    return jnp.pad(w.reshape(HEADS, d, n), ((0, 0), (0, HP - d), (0, 0))).reshape(HW, n)


def _unpad_head_rows(wp, d):
    n = wp.shape[1]
    return wp.reshape(HEADS, HP, n)[:, :d, :].reshape(HEADS * d, n)


IN_SEGS = (("q", Q_RANK), ("kv", KV_RANK), ("kr", ROPE), ("fq", FOX_W), ("fk", FOX_W), ("fv", FOX_W),
           ("fl", HEADS), ("gate", 2 * D_MODEL))


def _split_w_in(w):
    seg = {}
    o = 0
    for nm, wd in IN_SEGS:
        seg[nm] = w[:, o:o + wd]
        o += wd
    d = w.shape[0]
    z = lambda n: jnp.zeros((d, n), w.dtype)
    fused = jnp.concatenate([_pad_head_cols(seg[nm], FOX_DIM) for nm in ("fq", "fk", "fv")], axis=1)
    last = jnp.concatenate([seg["fl"], z(LANE_PE - HEADS), seg["kr"], z(HP - LANE_PE - ROPE)], axis=1)
    rest = jnp.concatenate([seg["q"], seg["kv"], last, z(R_GATE - R_LAST - HP), seg["gate"]], axis=1)
    return fused, rest


def _merge_w_in(fused, rest):
    f = [_unpad_head_cols(fused[:, i * HW:(i + 1) * HW], FOX_DIM) for i in range(3)]
    last = rest[:, R_LAST:R_LAST + HP]
    return jnp.concatenate([rest[:, R_QLAT:R_LAST], last[:, LANE_PE:LANE_PE + ROPE], f[0], f[1], f[2],
                            last[:, LANE_FL:LANE_FL + HEADS], rest[:, R_GATE:]], axis=1)


def _split_w_kv(w):
    k = w.shape[0]
    w3 = w.reshape(k, HEADS, NOPE + V_DIM)
    padl = lambda a: jnp.pad(a, ((0, 0), (0, 0), (0, HP - a.shape[-1]))).reshape(k, HW)
    return padl(w3[..., :NOPE]), padl(w3[..., NOPE:])


def _merge_w_kv(wk, wv):
    k = wk.shape[0]
    return jnp.concatenate([wk.reshape(k, HEADS, HP)[..., :NOPE], wv.reshape(k, HEADS, HP)[..., :V_DIM]],
                           axis=-1).reshape(k, HEADS * (NOPE + V_DIM))


def _local_step(xcat, tgt, seq, p):
    r = xcat.shape[0]
    cd = MXU_DTYPE

    w_f, w_r = _split_w_in(p["w_in"])
    w_q = _pad_head_cols(p["w_q_up"], QK_DIM)
    w_k, w_v = _split_w_kv(p["w_kv_up"])
    w_bm = _pad_head_rows(p["w_branch_mla"], V_DIM)
    w_bf = _pad_head_rows(p["w_branch_fox"], FOX_DIM)

    pos = jnp.arange(r, dtype=F32)
    inv_freq = ROPE_THETA ** (-jnp.arange(HALF, dtype=F32) / HALF)
    ang = pos[:, None] * inv_freq[None, :]
    cos_t = jnp.tile(jnp.cos(ang), (1, HP // HALF))
    sin_t = jnp.tile(jnp.sin(ang), (1, HP // HALF))
    bf_row = jnp.zeros((1, HP), F32).at[0, LANE_FL:LANE_FL + HEADS].set(p["b_forget"])

    h0, h0b = _ln_fwd("ln_emb_fwd", xcat, None, p["ln_emb_g"], p["ln_emb_b"])
    proj_f = _matmul("in_proj_f", h0b, w_f, out_dtype=cd)
    proj_r = _matmul("in_proj_r", h0b, w_r)
    ql = _rms_fwd("q_norm_fwd", proj_r, R_QLAT // Q_RANK, Q_RANK, p["q_norm_g"])
    kvl = _rms_fwd("kv_norm_fwd", proj_r, R_KVLAT // KV_RANK, KV_RANK, p["kv_norm_g"])
    q_raw = _matmul("q_up", ql, w_q)
    k_part = _matmul("k_up", kvl, w_k)
    v_mla = _matmul("v_up", kvl, w_v, out_dtype=cd)
    q_mla, k_mla = _rope_fwd(q_raw, k_part, proj_r, cos_t, sin_t)
    o_mla, o_mla_b, lse_mla = _attn_fwd("mla_fwd", (q_mla, 0), (k_mla, 0), (v_mla, 0), QK_DIM ** -0.5)

    cum, cum_t = _forget_fwd(proj_r, bf_row)
    o_fox, o_fox_b, lse_fox = _attn_fwd("fox_fwd", (proj_f, 0), (proj_f, 1), (proj_f, 2), FOX_DIM ** -0.5, cum, cum_t)

    bm = _matmul("branch_mla", o_mla_b, w_bm)
    bfx = _matmul("branch_fox", o_fox_b, w_bf)
    merged = _gate_fwd(proj_r, p["b_gate"], bm, bfx)
    mix = _matmul("out_proj", merged, p["w_out"])
    h1, h1b = _ln_fwd("ln_mix_fwd", h0, mix, p["ln_mix_g"], p["ln_mix_b"])
    up = _matmul("ffn_up", h1b, p["w_ffn_up"])
    act = _glu_fwd(up, p["conv_w"], p["conv_b"])
    f = _matmul("ffn_down", act, p["w_ffn_down"])
    h2, _ = _ln_fwd("ln_ffn_fwd", h1, f, p["ln_ffn_g"], p["ln_ffn_b"])
    loss, dh2 = _loss(h2, tgt, seq)

    g = {}
    dz2, dz2b, g["ln_ffn_g"], g["ln_ffn_b"] = _ln_bwd("ln_ffn_bwd", h1, f, dh2, p["ln_ffn_g"])
    d_act = _matmul("ffn_down_dx", dz2b, p["w_ffn_down"], tb=True)
    g["w_ffn_down"] = _matmul("ffn_down_dw", act, dz2b, ta=True)
    dgate, dval, dcw, g["conv_b"] = _glu_bwd_gate(up, p["conv_w"], p["conv_b"], d_act)
    g["conv_w"] = dcw[:3]
    d_up = _glu_bwd_conv(dgate, dval, p["conv_w"])
    dh1 = _matmul("ffn_up_dx", d_up, p["w_ffn_up"], tb=True, addend=dz2, alpha=ALPHA)
    g["w_ffn_up"] = _matmul("ffn_up_dw", h1b, d_up, ta=True)
    dz1, dz1b, g["ln_mix_g"], g["ln_mix_b"] = _ln_bwd("ln_mix_bwd", h0, mix, dh1, p["ln_mix_g"])
    dmerged = _matmul("out_proj_dx", dz1b, p["w_out"], tb=True)
    g["w_out"] = _matmul("out_proj_dw", merged, dz1b, ta=True)
    d_bm, d_bf, d_gl, g["b_gate"] = _gate_bwd(proj_r, p["b_gate"], bm, bfx, dmerged)
    d_o_mla = _matmul("branch_mla_dx", d_bm, w_bm, tb=True)
    g["w_branch_mla"] = _unpad_head_rows(_matmul("branch_mla_dw", o_mla_b, d_bm, ta=True), V_DIM)
    d_o_fox = _matmul("branch_fox_dx", d_bf, w_bf, tb=True)
    g["w_branch_fox"] = _unpad_head_rows(_matmul("branch_fox_dw", o_fox_b, d_bf, ta=True), FOX_DIM)

    dl_mla, do_mla_b = _attn_delta("mla_delta", d_o_mla, o_mla)
    dq_m, dk_m, dv_m = _attn_bwd("mla_bwd", (q_mla, 0), (k_mla, 0), (v_mla, 0), do_mla_b, lse_mla, dl_mla, QK_DIM ** -0.5)
    dl_fox, do_fox_b = _attn_delta("fox_delta", d_o_fox, o_fox)
    dfq, dfk, dfv, dcq, dck = _attn_bwd("fox_bwd", (proj_f, 0), (proj_f, 1), (proj_f, 2), do_fox_b, lse_fox, dl_fox,
                                        FOX_DIM ** -0.5, cum, cum_t)
    dfl, dbf = _forget_bwd(proj_r, bf_row, dcq, dck)
    g["b_forget"] = dbf[:, LANE_FL:LANE_FL + HEADS]

    dq_b, dk_b, dlast = _rope_bwd(dq_m, dk_m, dfl, cos_t, sin_t)
    dv_b = dv_m.astype(cd)
    d_ql = _matmul("q_up_dx", dq_b, w_q, tb=True)
    g["w_q_up"] = _unpad_head_cols(_matmul("q_up_dw", ql, dq_b, ta=True), QK_DIM)
    d_kvl = _matmul("k_up_dx", dk_b, w_k, tb=True)
    d_kvl = _matmul("v_up_dx", dv_b, w_v, tb=True, addend=d_kvl)
    g["w_kv_up"] = _merge_w_kv(_matmul("k_up_dw", kvl, dk_b, ta=True), _matmul("v_up_dw", kvl, dv_b, ta=True))
    d_qlat, g["q_norm_g"] = _rms_bwd("q_norm_bwd", proj_r, R_QLAT // Q_RANK, Q_RANK, d_ql, p["q_norm_g"])
    d_kvlat, g["kv_norm_g"] = _rms_bwd("kv_norm_bwd", proj_r, R_KVLAT // KV_RANK, KV_RANK, d_kvl, p["kv_norm_g"])
    dproj_f = jnp.concatenate([dfq.astype(cd), dfk.astype(cd), dfv.astype(cd)], axis=1)
    dproj_r = jnp.concatenate([d_qlat, d_kvlat, dlast, jnp.zeros((r, R_GATE - R_LAST - HP), cd), d_gl], axis=1)
    dh0 = _matmul("in_proj_f_dx", dproj_f, w_f, tb=True, addend=dz1, alpha=ALPHA)
    dh0 = _matmul("in_proj_r_dx", dproj_r, w_r, tb=True, addend=dh0)
    g["w_in"] = _merge_w_in(_matmul("in_proj_f_dw", h0b, dproj_f, ta=True), _matmul("in_proj_r_dw", h0b, dproj_r, ta=True))
    dxcat, _, g["ln_emb_g"], g["ln_emb_b"] = _ln_bwd("ln_emb_bwd", xcat, None, dh0, p["ln_emb_g"])
    return loss, dxcat, g


BIG = (("w_in", 1), ("w_q_up", 1), ("w_kv_up", 1), ("w_branch_mla", 1), ("w_branch_fox", 1), ("w_out", 0),
       ("w_ffn_up", 1), ("w_ffn_down", 0))
SMALL_SHARDED = (("meta_tokens", 1), ("conv_w", 1))
REPLICATED = ("ln_emb_g", "ln_emb_b", "b_gate", "b_forget", "q_norm_g", "kv_norm_g", "ln_mix_g", "ln_mix_b",
              "conv_b", "ln_ffn_g", "ln_ffn_b")
PACK_COLS = 1024


def _pack(flat_list):
    cat = jnp.concatenate(flat_list)
    n = cat.shape[0]
    rows = -(-n // (8 * PACK_COLS)) * 8
    return jnp.pad(cat, (0, rows * PACK_COLS - n)).reshape(rows, PACK_COLS)


def _gathered_full(g3, axis):
    n, r, c = g3.shape
    if axis == 0:
        return g3.reshape(n * r, c)
    return g3.transpose(1, 0, 2).reshape(r, n * c)


def _shard_major(full, axis):
    r, c = full.shape
    if axis == 0:
        return full.reshape(N_DEV, r // N_DEV, c)
    return full.reshape(r, N_DEV, c // N_DEV).transpose(1, 0, 2)


def kernel(x, meta_tokens, ln_emb_g, ln_emb_b, w_in, b_gate, b_forget, q_norm_g, w_q_up, kv_norm_g, w_kv_up, w_branch_mla, w_branch_fox, w_out, ln_mix_g, ln_mix_b, w_ffn_up, conv_w, conv_b, w_ffn_down, ln_ffn_g, ln_ffn_b, loss_target, m_meta_tokens, m_ln_emb_g, m_ln_emb_b, m_w_in, m_b_gate, m_b_forget, m_q_norm_g, m_w_q_up, m_kv_norm_g, m_w_kv_up, m_w_branch_mla, m_w_branch_fox, m_w_out, m_ln_mix_g, m_ln_mix_b, m_w_ffn_up, m_conv_w, m_conv_b, m_w_ffn_down, m_ln_ffn_g, m_ln_ffn_b, v_meta_tokens, v_ln_emb_g, v_ln_emb_b, v_w_in, v_b_gate, v_b_forget, v_q_norm_g, v_w_q_up, v_kv_norm_g, v_w_kv_up, v_w_branch_mla, v_w_branch_fox, v_w_out, v_ln_mix_g, v_ln_mix_b, v_w_ffn_up, v_conv_w, v_conv_b, v_w_ffn_down, v_ln_ffn_g, v_ln_ffn_b):
    names = ("meta_tokens", "ln_emb_g", "ln_emb_b", "w_in", "b_gate", "b_forget", "q_norm_g", "w_q_up", "kv_norm_g",
             "w_kv_up", "w_branch_mla", "w_branch_fox", "w_out", "ln_mix_g", "ln_mix_b", "w_ffn_up", "conv_w", "conv_b",
             "w_ffn_down", "ln_ffn_g", "ln_ffn_b")
    w_args = (meta_tokens, ln_emb_g, ln_emb_b, w_in, b_gate, b_forget, q_norm_g, w_q_up, kv_norm_g, w_kv_up,
              w_branch_mla, w_branch_fox, w_out, ln_mix_g, ln_mix_b, w_ffn_up, conv_w, conv_b, w_ffn_down, ln_ffn_g, ln_ffn_b)
    m_args = (m_meta_tokens, m_ln_emb_g, m_ln_emb_b, m_w_in, m_b_gate, m_b_forget, m_q_norm_g, m_w_q_up, m_kv_norm_g,
              m_w_kv_up, m_w_branch_mla, m_w_branch_fox, m_w_out, m_ln_mix_g, m_ln_mix_b, m_w_ffn_up, m_conv_w, m_conv_b,
              m_w_ffn_down, m_ln_ffn_g, m_ln_ffn_b)
    v_args = (v_meta_tokens, v_ln_emb_g, v_ln_emb_b, v_w_in, v_b_gate, v_b_forget, v_q_norm_g, v_w_q_up, v_kv_norm_g,
              v_w_kv_up, v_w_branch_mla, v_w_branch_fox, v_w_out, v_ln_mix_g, v_ln_mix_b, v_w_ffn_up, v_conv_w, v_conv_b,
              v_w_ffn_down, v_ln_ffn_g, v_ln_ffn_b)
    as2d = lambda a: a.reshape((-1, a.shape[-1])) if a.ndim != 1 else a.reshape(1, -1)
    w = {n: as2d(a) for n, a in zip(names, w_args)}
    m = {n: as2d(a) for n, a in zip(names, m_args)}
    v = {n: as2d(a) for n, a in zip(names, v_args)}
    out_shape = {n: a.shape for n, a in zip(names, w_args)}

    seq = x.shape[1]
    rows = -(-(N_META + seq) // ROW_ALIGN) * ROW_ALIGN
    sharded = BIG + SMALL_SHARDED

    shards = [w[n].astype(MXU_DTYPE) for n, _ in BIG] + [w[n] for n, _ in SMALL_SHARDED]
    gathered = _allgather("gather_weights", shards)
    p = {n: _gathered_full(g3, ax) for (n, ax), g3 in zip(sharded, gathered)}
    for n in REPLICATED:
        p[n] = w[n].reshape(-1)

    zpad = jnp.zeros((rows - N_META - seq, D_MODEL), F32)
    xcat = jnp.concatenate([p["meta_tokens"], x[0], zpad], axis=0)
    tgt = jnp.concatenate([jnp.zeros((N_META, D_MODEL), F32), loss_target[0], zpad], axis=0)
    loss_part, dxcat, g = _local_step(xcat, tgt, seq, p)
    grad_x = dxcat[N_META:N_META + seq][None]
    g["meta_tokens"] = dxcat[:N_META]
    loss = lax.psum(loss_part[0, 0], ("x", "y", "c"))

    g_recv = _exchange("exchange_grads", [_shard_major(g[n], ax).astype(MXU_DTYPE) for n, ax in sharded])
    rep_all = _allgather("gather_small_grads", [_pack([g[n].reshape(-1) for n in REPLICATED])])[0]

    res = {}
    for (n, _), parts in zip(sharded, g_recv):
        res[n] = _adamw("adamw_" + n, parts, w[n], m[n], v[n])
    rep_w = _pack([w[n].reshape(-1) for n in REPLICATED])
    rep_m = _pack([m[n].reshape(-1) for n in REPLICATED])
    rep_v = _pack([v[n].reshape(-1) for n in REPLICATED])
    rep_res = _adamw("adamw_replicated", rep_all, rep_w, rep_m, rep_v)
    off = 0
    for n in REPLICATED:
        sz = w[n].size
        res[n] = tuple(a.reshape(-1)[off:off + sz] for a in rep_res)
        off += sz

    outs = [loss, grad_x]
    for idx in range(4):
        outs += [res[n][idx].reshape(out_shape[n]) for n in names]
    return tuple(outs)
```

```python
import jax
import jax.numpy as jnp
from jax import lax
from jax.experimental import pallas as pl
from jax.experimental.pallas import tpu as pltpu

F32 = jnp.float32
BF16 = jnp.bfloat16
MXU_DTYPE = BF16

N_DEV = 8
N_META = 16
D_MODEL = 1024
HEADS = 8
Q_RANK = 384
KV_RANK = 128
NOPE = 64
ROPE = 32
HALF = ROPE // 2
QK_DIM = NOPE + ROPE
V_DIM = 64
FOX_DIM = 64
FOX_W = HEADS * FOX_DIM
D_FF = 2816
ROPE_THETA = 10000.0
LN_EPS = 1e-5
RMS_EPS = 1e-6
ALPHA = 2.0 ** 0.25
NEG_INF = -1e30

HP = 128
HW = HEADS * HP
F_W = 3 * HW
R_QLAT = 0
R_KVLAT = Q_RANK
R_LAST = R_KVLAT + KV_RANK
R_GATE = D_MODEL
R_W = R_GATE + 2 * D_MODEL
LANE_FL = 0
LANE_PE = NOPE

ADAM_LR = 0.001
ADAM_B1 = 0.9
ADAM_B2 = 0.999
ADAM_EPS = 1e-08
ADAM_WD = 0.01
ADAM_STEP = 10

ROW_BLOCK = 128
ATT_TQ = 256
ATT_TK = 256
ATT_HEADS = 2
ROW_ALIGN = 256
VMEM_LIMIT = 56 * 1024 * 1024
HIGHEST = lax.Precision.HIGHEST
NT = (((1,), (1,)), ((), ()))
TN = (((0,), (0,)), ((), ()))


def _params(sem=None):
    return pltpu.CompilerParams(dimension_semantics=sem, vmem_limit_bytes=VMEM_LIMIT)


def _call(name, body, grid, ins, outs, scratch=(), sem=None, after=()):
    n_in = len(ins)
    n_tok = len(after)

    def run(*refs):
        body(*refs[:n_in], *refs[n_in + n_tok:])

    tok_spec = pl.BlockSpec((8, 128), lambda *_: (0, 0))
    return pl.pallas_call(
        run, name=name, grid=grid,
        in_specs=[s for _, s in ins] + [tok_spec] * n_tok,
        out_specs=[s for _, s in outs],
        out_shape=[o for o, _ in outs],
        scratch_shapes=list(scratch),
        compiler_params=_params(sem),
    )(*[a for a, _ in ins], *after)


def _sds(shape, dtype):
    return jax.ShapeDtypeStruct(shape, dtype)


def _rows(br, c, cb=0):
    return pl.BlockSpec((br, c), lambda i: (i, cb))


def _whole(shape):
    n = len(shape)
    return pl.BlockSpec(shape, lambda i: (0,) * n)


def _pick(dim, cap, mult):
    best = None
    d = mult
    while d <= min(dim, cap):
        if dim % d == 0:
            best = d
        d += mult
    return best if best is not None else dim


def _hs(h):
    return slice(h * HP, (h + 1) * HP)


def _matmul(name, a, b, *, ta=False, tb=False, out_dtype=F32, addend=None, alpha=1.0):
    if ta:
        k, m = a.shape
    else:
        m, k = a.shape
    if tb:
        n, k2 = b.shape
    else:
        k2, n = b.shape
    assert k == k2, (name, a.shape, b.shape)
    bm = _pick(m, 1152, 128 if ta else 16)
    bn = _pick(n, 512, 128)
    bk = _pick(k, 1152, 128 if (not ta or tb) else 16)
    nk = k // bk
    dims = (((0 if ta else 1,), (1 if tb else 0,)), ((), ()))
    has_add = addend is not None

    def body(*refs):
        if has_add:
            a_ref, b_ref, add_ref, o_ref, acc_ref = refs
        else:
            a_ref, b_ref, o_ref, acc_ref = refs
        kk = pl.program_id(2)

        @pl.when(kk == 0)
        def _():
            acc_ref[...] = jnp.zeros_like(acc_ref)

        acc_ref[...] += lax.dot_general(a_ref[...], b_ref[...], dims, preferred_element_type=F32)

        @pl.when(kk == nk - 1)
        def _():
            r = acc_ref[...]
            if has_add:
                r = r + alpha * add_ref[...]
            o_ref[...] = r.astype(o_ref.dtype)

    a_spec = pl.BlockSpec((bk, bm), lambda i, j, l: (l, i)) if ta else pl.BlockSpec((bm, bk), lambda i, j, l: (i, l))
    b_spec = pl.BlockSpec((bn, bk), lambda i, j, l: (j, l)) if tb else pl.BlockSpec((bk, bn), lambda i, j, l: (l, j))
    o_spec = pl.BlockSpec((bm, bn), lambda i, j, l: (i, j))
    ins = [(a, a_spec), (b, b_spec)]
    if has_add:
        ins.append((addend, o_spec))
    return _call(name, body, (m // bm, n // bn, nk), ins, [(_sds((m, n), out_dtype), o_spec)],
                 scratch=[pltpu.VMEM((bm, bn), F32)], sem=("parallel", "parallel", "arbitrary"))[0]


def _ln_stats(z):
    mu = jnp.mean(z, axis=-1, keepdims=True)
    zc = z - mu
    var = jnp.mean(zc * zc, axis=-1, keepdims=True)
    rstd = lax.rsqrt(var + LN_EPS)
    return zc * rstd, rstd


def _ln_fwd(name, a, res, g, b, after=()):
    r, d = a.shape
    br = ROW_BLOCK
    has_res = res is not None

    def body(*refs):
        if has_res:
            a_ref, r_ref, g_ref, b_ref, y_ref, yb_ref = refs
            z = ALPHA * a_ref[...] + r_ref[...]
        else:
            a_ref, g_ref, b_ref, y_ref, yb_ref = refs
            z = a_ref[...]
        xhat, _ = _ln_stats(z)
        y = xhat * g_ref[...] + b_ref[...]
        y_ref[...] = y
        yb_ref[...] = y.astype(yb_ref.dtype)

    ins = [(a, _rows(br, d))]
    if has_res:
        ins.append((res, _rows(br, d)))
    ins += [(g.reshape(1, d), _whole((1, d))), (b.reshape(1, d), _whole((1, d)))]
    outs = [(_sds((r, d), F32), _rows(br, d)), (_sds((r, d), MXU_DTYPE), _rows(br, d))]
    return _call(name, body, (r // br,), ins, outs, sem=("parallel",), after=after)


def _ln_bwd(name, a, res, dy, g, after=()):
    r, d = a.shape
    br = ROW_BLOCK
    has_res = res is not None

    def body(*refs):
        if has_res:
            a_ref, r_ref, dy_ref, g_ref, dz_ref, dzb_ref, dg_ref, db_ref = refs
            z = ALPHA * a_ref[...] + r_ref[...]
        else:
            a_ref, dy_ref, g_ref, dz_ref, dzb_ref, dg_ref, db_ref = refs
            z = a_ref[...]
        xhat, rstd = _ln_stats(z)
        dyv = dy_ref[...]
        dyg = dyv * g_ref[...]
        m1 = jnp.mean(dyg, axis=-1, keepdims=True)
        m2 = jnp.mean(dyg * xhat, axis=-1, keepdims=True)
        dz = rstd * (dyg - m1 - xhat * m2)
        dz_ref[...] = dz
        dzb_ref[...] = dz.astype(dzb_ref.dtype)

        @pl.when(pl.program_id(0) == 0)
        def _():
            dg_ref[...] = jnp.zeros_like(dg_ref)
            db_ref[...] = jnp.zeros_like(db_ref)

        dg_ref[...] += jnp.sum(dyv * xhat, axis=0, keepdims=True)
        db_ref[...] += jnp.sum(dyv, axis=0, keepdims=True)

    ins = [(a, _rows(br, d))]
    if has_res:
        ins.append((res, _rows(br, d)))
    ins += [(dy, _rows(br, d)), (g.reshape(1, d), _whole((1, d)))]
    outs = [(_sds((r, d), F32), _rows(br, d)), (_sds((r, d), MXU_DTYPE), _rows(br, d)),
            (_sds((1, d), F32), _whole((1, d))), (_sds((1, d), F32), _whole((1, d)))]
    return _call(name, body, (r // br,), ins, outs, sem=("arbitrary",), after=after)


def _rms_fwd(name, proj, cb, width, g):
    r = proj.shape[0]
    br = ROW_BLOCK

    def body(x_ref, g_ref, y_ref):
        x = x_ref[...]
        rstd = lax.rsqrt(jnp.mean(x * x, axis=-1, keepdims=True) + RMS_EPS)
        y_ref[...] = (x * rstd * g_ref[...]).astype(y_ref.dtype)

    return _call(name, body, (r // br,), [(proj, _rows(br, width, cb)), (g.reshape(1, width), _whole((1, width)))],
                 [(_sds((r, width), MXU_DTYPE), _rows(br, width))], sem=("parallel",))[0]


def _rms_bwd(name, proj, cb, width, dy, g):
    r = proj.shape[0]
    br = ROW_BLOCK

    def body(x_ref, dy_ref, g_ref, dx_ref, dg_ref):
        x = x_ref[...]
        rstd = lax.rsqrt(jnp.mean(x * x, axis=-1, keepdims=True) + RMS_EPS)
        nrm = x * rstd
        dyv = dy_ref[...]
        dyg = dyv * g_ref[...]
        dx = rstd * (dyg - nrm * jnp.mean(dyg * nrm, axis=-1, keepdims=True))
        dx_ref[...] = dx.astype(dx_ref.dtype)

        @pl.when(pl.program_id(0) == 0)
        def _():
            dg_ref[...] = jnp.zeros_like(dg_ref)

        dg_ref[...] += jnp.sum(dyv * nrm, axis=0, keepdims=True)

    return _call(name, body, (r // br,),
                 [(proj, _rows(br, width, cb)), (dy, _rows(br, width)), (g.reshape(1, width), _whole((1, width)))],
                 [(_sds((r, width), MXU_DTYPE), _rows(br, width)), (_sds((1, width), F32), _whole((1, width)))],
                 sem=("arbitrary",))


def _lane_iota(shape):
    return lax.broadcasted_iota(jnp.int32, shape, 1)


def _rotary(t, c, s, lane, sign):
    second = pltpu.roll(t, HP - HALF, axis=1)
    first = pltpu.roll(t, HALF, axis=1)
    lo = (lane >= LANE_PE) & (lane < LANE_PE + HALF)
    hi = (lane >= LANE_PE + HALF) & (lane < LANE_PE + ROPE)
    return jnp.where(lo, t * c - sign * second * s, jnp.where(hi, t * c + sign * first * s, t))


def _rope_fwd(q_raw, k_part, proj_r, cos_t, sin_t):
    r = q_raw.shape[0]
    br = ROW_BLOCK

    def body(q_ref, k_ref, t_ref, c_ref, s_ref, qo_ref, ko_ref):
        c = c_ref[...]
        s = s_ref[...]
        lane = _lane_iota((br, HP))
        pe = (lane >= LANE_PE) & (lane < LANE_PE + ROPE)
        kp = jnp.where(pe, _rotary(t_ref[...], c, s, lane, 1.0), 0.0)
        for h in range(HEADS):
            qo_ref[:, _hs(h)] = _rotary(q_ref[:, _hs(h)], c, s, lane, 1.0).astype(qo_ref.dtype)
            ko_ref[:, _hs(h)] = (k_ref[:, _hs(h)] + kp).astype(ko_ref.dtype)

    blk = _rows(br, HP)
    wide = _rows(br, HW)
    return _call("rope_fwd", body, (r // br,),
                 [(q_raw, wide), (k_part, wide), (proj_r, _rows(br, HP, R_LAST // HP)), (cos_t, blk), (sin_t, blk)],
                 [(_sds((r, HW), MXU_DTYPE), wide)] * 2, sem=("parallel",))


def _rope_bwd(dq, dk, dfl, cos_t, sin_t):
    r = dq.shape[0]
    br = ROW_BLOCK

    def body(dq_ref, dk_ref, fl_ref, c_ref, s_ref, dqo_ref, dko_ref, dl_ref):
        c = c_ref[...]
        s = s_ref[...]
        lane = _lane_iota((br, HP))
        pe = (lane >= LANE_PE) & (lane < LANE_PE + ROPE)
        acc = jnp.zeros((br, HP), F32)
        for h in range(HEADS):
            dqo_ref[:, _hs(h)] = _rotary(dq_ref[:, _hs(h)], c, s, lane, -1.0).astype(dqo_ref.dtype)
            dkh = dk_ref[:, _hs(h)]
            acc = acc + dkh
            dko_ref[:, _hs(h)] = dkh.astype(dko_ref.dtype)
        dl_ref[...] = (jnp.where(pe, _rotary(acc, c, s, lane, -1.0), 0.0) + fl_ref[...]).astype(dl_ref.dtype)

    blk = _rows(br, HP)
    wide = _rows(br, HW)
    return _call("rope_bwd", body, (r // br,),
                 [(dq, wide), (dk, wide), (dfl, blk), (cos_t, blk), (sin_t, blk)],
                 [(_sds((r, HW), MXU_DTYPE), wide), (_sds((r, HW), MXU_DTYPE), wide), (_sds((r, HP), MXU_DTYPE), blk)],
                 sem=("parallel",))


def _log_sigmoid(x):
    return jnp.minimum(x, 0.0) - jnp.log(1.0 + jnp.exp(-jnp.abs(x)))


def _head_lane(x, h, lane):
    return jnp.sum(jnp.where(lane == h, x, 0.0), axis=1, keepdims=True)


def _forget_fwd(proj_r, bf_row):
    r = proj_r.shape[0]
    br = ROW_BLOCK

    def body(t_ref, b_ref, ob_ref, ot_ref, carry_ref):
        @pl.when(pl.program_id(0) == 0)
        def _():
            carry_ref[...] = jnp.zeros_like(carry_ref)

        x = t_ref[...] + b_ref[...]
        lane = _lane_iota(x.shape)
        lf = jnp.where((lane >= LANE_FL) & (lane < LANE_FL + HEADS), _log_sigmoid(x), 0.0)
        tri = (lax.broadcasted_iota(jnp.int32, (br, br), 0) >= lax.broadcasted_iota(jnp.int32, (br, br), 1)).astype(F32)
        cum = jnp.dot(tri, lf, precision=HIGHEST, preferred_element_type=F32) + carry_ref[0:1, :]
        for h in range(HEADS):
            ob_ref[:, _hs(h)] = jnp.broadcast_to(_head_lane(cum, LANE_FL + h, lane), (br, HP))
        ot_ref[...] = cum.T[LANE_FL:LANE_FL + HEADS, :]
        carry_ref[...] = jnp.broadcast_to(cum[br - 1:br, :], carry_ref.shape)

    return _call("forget_fwd", body, (r // br,),
                 [(proj_r, _rows(br, HP, R_LAST // HP)), (bf_row, _whole((1, HP)))],
                 [(_sds((r, HW), F32), _rows(br, HW)), (_sds((HEADS, r), F32), pl.BlockSpec((HEADS, br), lambda i: (0, i)))],
                 scratch=[pltpu.VMEM((8, HP), F32)], sem=("arbitrary",))


def _forget_bwd(proj_r, bf_row, dcq_t, dck_b):
    r = proj_r.shape[0]
    br = ROW_BLOCK
    nb = r // br

    def body(t_ref, b_ref, dcq_ref, dck_ref, o_ref, db_ref, carry_ref):
        @pl.when(pl.program_id(0) == 0)
        def _():
            carry_ref[...] = jnp.zeros_like(carry_ref)
            db_ref[...] = jnp.zeros_like(db_ref)

        lane = _lane_iota((br, HP))
        dc = jnp.concatenate([dcq_ref[...], jnp.zeros((br - HEADS, br), F32)], axis=0).T
        for h in range(HEADS):
            dc = dc + jnp.where(lane == LANE_FL + h, dck_ref[:, h * HP:h * HP + 1], 0.0)
        triu = (lax.broadcasted_iota(jnp.int32, (br, br), 0) <= lax.broadcasted_iota(jnp.int32, (br, br), 1)).astype(F32)
        dlf = jnp.dot(triu, dc, precision=HIGHEST, preferred_element_type=F32) + carry_ref[0:1, :]
        carry_ref[...] = jnp.broadcast_to(dlf[0:1, :], carry_ref.shape)
        x = t_ref[...] + b_ref[...]
        dfl = jnp.where((lane >= LANE_FL) & (lane < LANE_FL + HEADS), dlf * jax.nn.sigmoid(-x), 0.0)
        o_ref[...] = dfl
        db_ref[...] += jnp.sum(dfl, axis=0, keepdims=True)

    rev = pl.BlockSpec((br, HP), lambda i: (nb - 1 - i, 0))
    return _call("forget_bwd", body, (nb,),
                 [(proj_r, pl.BlockSpec((br, HP), lambda i: (nb - 1 - i, R_LAST // HP))), (bf_row, _whole((1, HP))),
                  (dcq_t, pl.BlockSpec((HEADS, br), lambda i: (0, nb - 1 - i))),
                  (dck_b, pl.BlockSpec((br, HW), lambda i: (nb - 1 - i, 0)))],
                 [(_sds((r, HP), F32), rev), (_sds((1, HP), F32), _whole((1, HP)))],
                 scratch=[pltpu.VMEM((8, HP), F32)], sem=("arbitrary",))


def _gate_fwd(proj_r, b_gate, bm, bfx):
    r, d = bm.shape
    br = ROW_BLOCK
    cb = R_GATE // d

    def body(gm_ref, gf_ref, b1_ref, b2_ref, bm_ref, bf_ref, o_ref):
        g1 = jax.nn.sigmoid(gm_ref[...] + b1_ref[...])
        g2 = jax.nn.sigmoid(gf_ref[...] + b2_ref[...])
        o_ref[...] = (g1 * bm_ref[...] + g2 * bf_ref[...]).astype(o_ref.dtype)

    b1 = b_gate[:d].reshape(1, d)
    b2 = b_gate[d:].reshape(1, d)
    return _call("gate_fwd", body, (r // br,),
                 [(proj_r, _rows(br, d, cb)), (proj_r, _rows(br, d, cb + 1)), (b1, _whole((1, d))), (b2, _whole((1, d))),
                  (bm, _rows(br, d)), (bfx, _rows(br, d))],
                 [(_sds((r, d), MXU_DTYPE), _rows(br, d))], sem=("parallel",))[0]


def _gate_bwd(proj_r, b_gate, bm, bfx, dmerged):
    r, d = bm.shape
    br = ROW_BLOCK
    cb = R_GATE // d

    def body(gm_ref, gf_ref, b1_ref, b2_ref, bm_ref, bf_ref, dm_ref, dbm_ref, dbf_ref, dgl_ref, dbg_ref):
        g1 = jax.nn.sigmoid(gm_ref[...] + b1_ref[...])
        g2 = jax.nn.sigmoid(gf_ref[...] + b2_ref[...])
        dm = dm_ref[...]
        dbm_ref[...] = (dm * g1).astype(dbm_ref.dtype)
        dbf_ref[...] = (dm * g2).astype(dbf_ref.dtype)
        dl1 = dm * bm_ref[...] * (g1 * (1.0 - g1))
        dl2 = dm * bf_ref[...] * (g2 * (1.0 - g2))
        dgl_ref[:, 0:d] = dl1.astype(dgl_ref.dtype)
        dgl_ref[:, d:2 * d] = dl2.astype(dgl_ref.dtype)

        @pl.when(pl.program_id(0) == 0)
        def _():
            dbg_ref[...] = jnp.zeros_like(dbg_ref)

        dbg_ref[:, 0:d] += jnp.sum(dl1, axis=0, keepdims=True)
        dbg_ref[:, d:2 * d] += jnp.sum(dl2, axis=0, keepdims=True)

    b1 = b_gate[:d].reshape(1, d)
    b2 = b_gate[d:].reshape(1, d)
    return _call("gate_bwd", body, (r // br,),
                 [(proj_r, _rows(br, d, cb)), (proj_r, _rows(br, d, cb + 1)), (b1, _whole((1, d))), (b2, _whole((1, d))),
                  (bm, _rows(br, d)), (bfx, _rows(br, d)), (dmerged, _rows(br, d))],
                 [(_sds((r, d), MXU_DTYPE), _rows(br, d)), (_sds((r, d), MXU_DTYPE), _rows(br, d)),
                  (_sds((r, 2 * d), MXU_DTYPE), _rows(br, 2 * d)), (_sds((1, 2 * d), F32), _whole((1, 2 * d)))],
                 sem=("arbitrary",))


def _conv_taps(gp, halo, first_block):
    halo = jnp.where(first_block, 0.0, halo)
    rid = lax.broadcasted_iota(jnp.int32, gp.shape, 0)
    g1 = jnp.where(rid == 0, halo[7:8, :], pltpu.roll(gp, 1, axis=0))
    g2 = jnp.where(rid == 0, halo[6:7, :], jnp.where(rid == 1, halo[7:8, :], pltpu.roll(gp, 2, axis=0)))
    return g1, g2


def _prev_halo(br, c):
    return pl.BlockSpec((8, c), lambda i: (jnp.maximum(i * (br // 8) - 1, 0), 0))


def _glu_fwd(up, conv_w, conv_b):
    r = up.shape[0]
    c = D_FF
    br = ROW_BLOCK

    def body(gp_ref, halo_ref, val_ref, w_ref, b_ref, o_ref):
        gp = gp_ref[...]
        g1, g2 = _conv_taps(gp, halo_ref[...], pl.program_id(0) == 0)
        gate = w_ref[0:1, :] * g2 + w_ref[1:2, :] * g1 + w_ref[2:3, :] * gp + b_ref[...]
        o_ref[...] = (gate * jax.nn.sigmoid(gate) * val_ref[...]).astype(o_ref.dtype)

    return _call("glu_fwd", body, (r // br,),
                 [(up, _rows(br, c, 0)), (up, _prev_halo(br, c)), (up, _rows(br, c, 1)),
                  (conv_w, _whole((3, c))), (conv_b.reshape(1, c), _whole((1, c)))],
                 [(_sds((r, c), MXU_DTYPE), _rows(br, c))], sem=("parallel",))[0]


def _glu_bwd_gate(up, conv_w, conv_b, d_act):
    r = up.shape[0]
    c = D_FF
    br = ROW_BLOCK

    def body(gp_ref, halo_ref, val_ref, w_ref, b_ref, da_ref, dg_ref, dv_ref, dw_ref, db_ref):
        gp = gp_ref[...]
        g1, g2 = _conv_taps(gp, halo_ref[...], pl.program_id(0) == 0)
        gate = w_ref[0:1, :] * g2 + w_ref[1:2, :] * g1 + w_ref[2:3, :] * gp + b_ref[...]
        sg = jax.nn.sigmoid(gate)
        da = da_ref[...]
        dv_ref[...] = (da * (gate * sg)).astype(dv_ref.dtype)
        dg = da * val_ref[...] * (sg * (1.0 + gate * (1.0 - sg)))
        dg_ref[...] = dg

        @pl.when(pl.program_id(0) == 0)
        def _():
            dw_ref[...] = jnp.zeros_like(dw_ref)
            db_ref[...] = jnp.zeros_like(db_ref)

        dw_ref[0:1, :] += jnp.sum(dg * g2, axis=0, keepdims=True)
        dw_ref[1:2, :] += jnp.sum(dg * g1, axis=0, keepdims=True)
        dw_ref[2:3, :] += jnp.sum(dg * gp, axis=0, keepdims=True)
        db_ref[...] += jnp.sum(dg, axis=0, keepdims=True)

    return _call("glu_bwd_gate", body, (r // br,),
                 [(up, _rows(br, c, 0)), (up, _prev_halo(br, c)), (up, _rows(br, c, 1)),
                  (conv_w, _whole((3, c))), (conv_b.reshape(1, c), _whole((1, c))), (d_act, _rows(br, c))],
                 [(_sds((r, c), F32), _rows(br, c)), (_sds((r, c), MXU_DTYPE), _rows(br, c)),
                  (_sds((8, c), F32), _whole((8, c))), (_sds((1, c), F32), _whole((1, c)))],
                 sem=("arbitrary",))


def _glu_bwd_conv(dg, dval, conv_w):
    r, c = dg.shape
    br = ROW_BLOCK
    nb = r // br

    def body(dg_ref, nxt_ref, dv_ref, w_ref, o_ref):
        x = dg_ref[...]
        nxt = jnp.where(pl.program_id(0) == nb - 1, 0.0, nxt_ref[...])
        rid = lax.broadcasted_iota(jnp.int32, x.shape, 0)
        u1 = jnp.where(rid == br - 1, nxt[0:1, :], pltpu.roll(x, br - 1, axis=0))
        u2 = jnp.where(rid == br - 1, nxt[1:2, :], jnp.where(rid == br - 2, nxt[0:1, :], pltpu.roll(x, br - 2, axis=0)))
        dgp = w_ref[2:3, :] * x + w_ref[1:2, :] * u1 + w_ref[0:1, :] * u2
        o_ref[:, 0:c] = dgp.astype(o_ref.dtype)
        o_ref[:, c:2 * c] = dv_ref[...]

    nxt_spec = pl.BlockSpec((8, c), lambda i: (jnp.minimum((i + 1) * (br // 8), r // 8 - 1), 0))
    return _call("glu_bwd_conv", body, (nb,),
                 [(dg, _rows(br, c)), (dg, nxt_spec), (dval, _rows(br, c)), (conv_w, _whole((3, c)))],
                 [(_sds((r, 2 * c), MXU_DTYPE), _rows(br, 2 * c))], sem=("parallel",))[0]


def _loss(h2, tgt, seq):
    r, d = h2.shape
    br = ROW_BLOCK

    def body(h_ref, t_ref, l_ref, d_ref):
        rid = lax.broadcasted_iota(jnp.int32, (br, d), 0) + pl.program_id(0) * br
        valid = (rid >= N_META) & (rid < N_META + seq)
        err = jnp.where(valid, h_ref[...] - t_ref[...], 0.0)
        d_ref[...] = err * (1.0 / d)

        @pl.when(pl.program_id(0) == 0)
        def _():
            l_ref[...] = jnp.zeros_like(l_ref)

        l_ref[...] += jnp.sum(jnp.sum(err * err, axis=1, keepdims=True), axis=0, keepdims=True) * (0.5 / d)

    return _call("loss", body, (r // br,), [(h2, _rows(br, d)), (tgt, _rows(br, d))],
                 [(_sds((1, 1), F32), _whole((1, 1))), (_sds((r, d), F32), _rows(br, d))], sem=("arbitrary",))


def _attn_fwd(name, q, k, v, scale, cum_b=None, cum_t=None):
    (qa, qg), (ka, kg), (va, vg) = q, k, v
    r = qa.shape[0]
    tq, tk = ATT_TQ, ATT_TK
    nq, nk = r // tq, r // tk
    bias = cum_b is not None

    def body(*refs):
        if bias:
            q_ref, k_ref, v_ref, cb_ref, ct_ref, o_ref, ob_ref, lse_ref = refs
        else:
            q_ref, k_ref, v_ref, o_ref, ob_ref, lse_ref = refs
        i = pl.program_id(1)
        qs = [q_ref[:, _hs(hh)] for hh in range(hg)]
        cqs = [cb_ref[:, hh * HP:hh * HP + 1] for hh in range(hg)] if bias else None
        diff = lax.broadcasted_iota(jnp.int32, (tq, tk), 1) - lax.broadcasted_iota(jnp.int32, (tq, tk), 0)

        def step(j, carry, masked):
            off = pl.multiple_of(j * tk, tk)
            out = []
            for hh in range(hg):
                m, l, acc = carry[hh]
                kt = k_ref[pl.ds(off, tk), _hs(hh)]
                vt = v_ref[pl.ds(off, tk), _hs(hh)]
                s = lax.dot_general(qs[hh], kt, NT, preferred_element_type=F32) * scale
                if bias:
                    s = s + (cqs[hh] - ct_ref[hh, j])
                if masked:
                    s = jnp.where(diff <= i * tq - j * tk, s, NEG_INF)
                m_new = jnp.maximum(m, jnp.max(s, axis=1, keepdims=True))
                p = jnp.exp(s - m_new)
                a = jnp.exp(m - m_new)
                l = a * l + jnp.sum(p, axis=1, keepdims=True)
                acc = a * acc + jnp.dot(p.astype(vt.dtype), vt, preferred_element_type=F32)
                out.append((m_new, l, acc))
            return tuple(out)

        n_clear = (i * tq + 1) // tk
        n_all = ((i + 1) * tq - 1) // tk + 1
        carry = tuple((jnp.full((tq, 1), NEG_INF, F32), jnp.zeros((tq, 1), F32), jnp.zeros((tq, HP), F32))
                      for _ in range(hg))
        carry = lax.fori_loop(0, n_clear, lambda j, c: step(j, c, False), carry)
        carry = lax.fori_loop(n_clear, n_all, lambda j, c: step(j, c, True), carry)
        for hh in range(hg):
            m, l, acc = carry[hh]
            o = acc / l
            o_ref[:, _hs(hh)] = o
            ob_ref[:, _hs(hh)] = o.astype(ob_ref.dtype)
            lse_ref[hh] = jnp.broadcast_to(m + jnp.log(l), (tq, HP)).T[0:1, :]

    hg = ATT_HEADS
    w = hg * HP
    gpw = HW // w
    tile = lambda g: pl.BlockSpec((tq, w), lambda h, i: (i, g * gpw + h))
    res = lambda g: pl.BlockSpec((r, w), lambda h, i: (0, g * gpw + h))
    ins = [(qa, tile(qg)), (ka, res(kg)), (va, res(vg))]
    if bias:
        ins += [(cum_b, tile(0)),
                (cum_t.reshape(HEADS, nk, 1, tk), pl.BlockSpec((hg, nk, 1, tk), lambda h, i: (h, 0, 0, 0)))]
    outs = [(_sds((r, HW), F32), tile(0)), (_sds((r, HW), MXU_DTYPE), tile(0)),
            (_sds((HEADS, nq, 1, tq), F32), pl.BlockSpec((hg, None, 1, tq), lambda h, i: (h, i, 0, 0)))]
    o, ob, lse = _call(name, body, (gpw, nq), ins, outs, sem=("parallel", "parallel"))
    return o, ob, lse.reshape(HEADS, r)


def _attn_delta(name, do, o, after=()):
    r = do.shape[0]
    br = ROW_BLOCK

    def body(do_ref, o_ref, d_ref, dob_ref):
        lane = _lane_iota((br, HP))
        d = jnp.zeros((br, HP), F32)
        for h in range(HEADS):
            dh = do_ref[:, _hs(h)]
            d = jnp.where(lane == h, jnp.sum(dh * o_ref[:, _hs(h)], axis=1, keepdims=True), d)
            dob_ref[:, _hs(h)] = dh.astype(dob_ref.dtype)
        d_ref[...] = d.T[0:HEADS, :]

    wide = _rows(br, HW)
    return _call(name, body, (r // br,), [(do, wide), (o, wide)],
                 [(_sds((HEADS, r), F32), pl.BlockSpec((HEADS, br), lambda i: (0, i))), (_sds((r, HW), MXU_DTYPE), wide)],
                 sem=("parallel",), after=after)


def _attn_bwd(name, q, k, v, do_b, lse_t, delta_t, scale, cum_b=None, cum_t=None):
    (qa, qg), (ka, kg), (va, vg) = q, k, v
    r = qa.shape[0]
    tq, tk = ATT_TQ, ATT_TK
    nq, nk = r // tq, r // tk
    bias = cum_b is not None

    def body(*refs):
        if bias:
            (q_ref, k_ref, v_ref, do_ref, lse_ref, dl_ref, cb_ref, ct_ref,
             dq_ref, dk_ref, dv_ref, dcq_ref, dck_ref, dqt_ref) = refs
        else:
            q_ref, k_ref, v_ref, do_ref, lse_ref, dl_ref, dq_ref, dk_ref, dv_ref, dqt_ref = refs
        j = pl.program_id(1)

        @pl.when(j == 0)
        def _():
            dqt_ref[...] = jnp.zeros_like(dqt_ref)
            if bias:
                dcq_ref[...] = jnp.zeros_like(dcq_ref)

        kts = [k_ref[:, _hs(hh)] for hh in range(hg)]
        vts = [v_ref[:, _hs(hh)] for hh in range(hg)]
        k_trs = [kt.astype(F32).T.astype(kt.dtype) for kt in kts]
        cks = [cb_ref[:, hh * HP:hh * HP + 1] for hh in range(hg)] if bias else None
        diff = lax.broadcasted_iota(jnp.int32, (tk, tq), 0) - lax.broadcasted_iota(jnp.int32, (tk, tq), 1)

        def step(i, carry, masked):
            rows = pl.ds(pl.multiple_of(i * tq, tq), tq)
            out = []
            for hh in range(hg):
                dk_acc, dv_acc, dck_acc = carry[hh]
                qt = q_ref[rows, _hs(hh)]
                dot = do_ref[rows, _hs(hh)]
                s = lax.dot_general(kts[hh], qt, NT, preferred_element_type=F32) * scale
                if bias:
                    s = s + (ct_ref[hh, i] - cks[hh])
                if masked:
                    s = jnp.where(diff <= i * tq - j * tk, s, NEG_INF)
                p = jnp.exp(s - lse_ref[hh, i])
                dp = lax.dot_general(vts[hh], dot, NT, preferred_element_type=F32)
                ds = p * (dp - dl_ref[hh, i])
                pb = p.astype(dot.dtype)
                dsb = ds.astype(qt.dtype)
                dv_acc = dv_acc + jnp.dot(pb, dot, preferred_element_type=F32)
                dk_acc = dk_acc + jnp.dot(dsb, qt, preferred_element_type=F32)
                dqt_ref[hh, i] += jnp.dot(k_trs[hh], dsb, preferred_element_type=F32)
                if bias:
                    dcq_ref[hh, i] += jnp.sum(ds, axis=0, keepdims=True)
                    dck_acc = dck_acc - jnp.sum(ds, axis=1, keepdims=True)
                out.append((dk_acc, dv_acc, dck_acc))
            return tuple(out)

        i_first = (j * tk) // tq
        i_clear = jnp.minimum(((j + 1) * tk + tq - 2) // tq, nq)
        carry = tuple((jnp.zeros((tk, HP), F32), jnp.zeros((tk, HP), F32), jnp.zeros((tk, 1), F32)) for _ in range(hg))
        carry = lax.fori_loop(i_first, i_clear, lambda i, c: step(i, c, True), carry)
        carry = lax.fori_loop(i_clear, nq, lambda i, c: step(i, c, False), carry)
        for hh in range(hg):
            dk_acc, dv_acc, dck_acc = carry[hh]
            dk_ref[:, _hs(hh)] = dk_acc * scale
            dv_ref[:, _hs(hh)] = dv_acc
            if bias:
                dck_ref[:, _hs(hh)] = jnp.broadcast_to(dck_acc, (tk, HP))

        @pl.when(j == nk - 1)
        def _():
            for hh in range(hg):
                for i in range(nq):
                    dq_ref[i * tq:(i + 1) * tq, _hs(hh)] = dqt_ref[hh, i].T * scale

    hg = ATT_HEADS
    w = hg * HP
    gpw = HW // w
    res = lambda g: pl.BlockSpec((r, w), lambda h, j: (0, g * gpw + h))
    tile = lambda g: pl.BlockSpec((tk, w), lambda h, j: (j, g * gpw + h))
    rowv = pl.BlockSpec((hg, nq, 1, tq), lambda h, j: (h, 0, 0, 0))
    as_rows = lambda a: a.reshape(HEADS, nq, 1, tq)
    ins = [(qa, res(qg)), (ka, tile(kg)), (va, tile(vg)), (do_b, res(0)), (as_rows(lse_t), rowv), (as_rows(delta_t), rowv)]
    outs = [(_sds((r, HW), F32), res(0)), (_sds((r, HW), F32), tile(0)), (_sds((r, HW), F32), tile(0))]
    if bias:
        ins += [(cum_b, tile(0)), (as_rows(cum_t), rowv)]
        outs += [(_sds((HEADS, nq, 1, tq), F32), rowv), (_sds((r, HW), F32), tile(0))]
    res_out = _call(name, body, (gpw, nk), ins, outs, scratch=[pltpu.VMEM((hg, nq, HP, tq), F32)],
                    sem=("parallel", "arbitrary"))
    if bias:
        dq, dk, dv, dcq, dck = res_out
        return dq, dk, dv, dcq.reshape(HEADS, r), dck
    return res_out


MESH_ID = pl.DeviceIdType.MESH
ANY = pl.BlockSpec(memory_space=pl.ANY)


def _allgather(name, shards):
    n = len(shards)

    def body(*refs):
        x_refs, out_refs = refs[:n], refs[n:2 * n]
        send_sems, recv_sems, local_sems = refs[2 * n:]
        x, y, c = lax.axis_index("x"), lax.axis_index("y"), lax.axis_index("c")
        me, sibling = (x, y, c), (x, y, 1 - c)
        chips = [(1 - x, y), (x, 1 - y), (1 - x, 1 - y)]

        def slot(ti, px, py, pc):
            return out_refs[ti].at[4 * px + 2 * py + pc]

        def copy(ti, k, block, to, src=None):
            return pltpu.make_async_remote_copy(
                src_ref=slot(ti, *block) if src is None else src, dst_ref=slot(ti, *block),
                send_sem=send_sems.at[ti, k], recv_sem=recv_sems.at[ti, k], device_id=to, device_id_type=MESH_ID)

        mine = [pltpu.make_async_copy(x_refs[ti], slot(ti, *me), local_sems.at[ti]) for ti in range(n)]
        for cp in mine:
            cp.start()
        started = []
        for ti in range(n):
            first = [copy(ti, 0, me, sibling, src=x_refs[ti])]
            first += [copy(ti, 1 + j, me, (*chip, c), src=x_refs[ti]) for j, chip in enumerate(chips)]
            for cp in first:
                cp.start()
            started += first
        for ti in range(n):
            for j, chip in enumerate(chips):
                copy(ti, 1 + j, (*chip, c), me).wait_recv()
                fwd = copy(ti, 4 + j, (*chip, c), sibling)
                fwd.start()
                started.append(fwd)
        for ti in range(n):
            copy(ti, 0, sibling, me).wait_recv()
            for j, chip in enumerate(chips):
                copy(ti, 4 + j, (*chip, 1 - c), me).wait_recv()
        for cp in started:
            cp.wait_send()
        for cp in mine:
            cp.wait()

    return pl.pallas_call(
        body, name=name, out_shape=[_sds((N_DEV,) + s.shape, s.dtype) for s in shards],
        in_specs=[ANY] * n, out_specs=[ANY] * n,
        scratch_shapes=[pltpu.SemaphoreType.DMA((n, 7)), pltpu.SemaphoreType.DMA((n, 7)), pltpu.SemaphoreType.DMA((n,))],
    )(*shards)


def _exchange(name, parts):
    n = len(parts)

    def body(*refs):
        g_refs, out_refs = refs[:n], refs[n:2 * n]
        send_sems, recv_sems, local_sems = refs[2 * n:]
        x, y, c = lax.axis_index("x"), lax.axis_index("y"), lax.axis_index("c")
        my_id = 4 * x + 2 * y + c

        def peer(k):
            return (1 - x if k & 4 else x, 1 - y if k & 2 else y, 1 - c if k & 1 else c)

        def copy(ti, k, landing):
            px, py, pc = peer(k)
            pid = 4 * px + 2 * py + pc
            return pltpu.make_async_remote_copy(
                src_ref=g_refs[ti].at[pid], dst_ref=out_refs[ti].at[pid if landing else my_id],
                send_sem=send_sems.at[ti, k - 1], recv_sem=recv_sems.at[ti, k - 1],
                device_id=(px, py, pc), device_id_type=MESH_ID)

        mine = [pltpu.make_async_copy(g_refs[ti].at[my_id], out_refs[ti].at[my_id], local_sems.at[ti]) for ti in range(n)]
        for cp in mine:
            cp.start()
        sends = [copy(ti, k, False) for ti in range(n) for k in range(1, N_DEV)]
        for cp in sends:
            cp.start()
        for ti in range(n):
            for k in range(1, N_DEV):
                copy(ti, k, True).wait_recv()
        for cp in sends:
            cp.wait_send()
        for cp in mine:
            cp.wait()

    return pl.pallas_call(
        body, name=name, out_shape=[_sds(p.shape, p.dtype) for p in parts],
        in_specs=[ANY] * n, out_specs=[ANY] * n,
        scratch_shapes=[pltpu.SemaphoreType.DMA((n, 7)), pltpu.SemaphoreType.DMA((n, 7)), pltpu.SemaphoreType.DMA((n,))],
    )(*parts)


HBM = pl.BlockSpec(memory_space=pltpu.HBM)
SEM = pl.BlockSpec(memory_space=pltpu.SEMAPHORE)
EFFECT = pltpu.SideEffectType.DATAFLOW_SIDE_EFFECTING
N_PEER = N_DEV - 1


def _my_id():
    return 4 * lax.axis_index("x") + 2 * lax.axis_index("y") + lax.axis_index("c")


def _peers():
    x, y, c = lax.axis_index("x"), lax.axis_index("y"), lax.axis_index("c")
    out = []
    for k in range(1, N_DEV):
        px, py, pc = (1 - x if k & 4 else x, 1 - y if k & 2 else y, 1 - c if k & 1 else c)
        out.append(((px, py, pc), 4 * px + 2 * py + pc))
    return out


def _push_copies(src_refs, land_refs, send_sems, recv_sems, scatter, landing):
    me = _my_id()
    out = []
    for ti, (src, land) in enumerate(zip(src_refs, land_refs)):
        for k, (dev, pid) in enumerate(_peers()):
            out.append(pltpu.make_async_remote_copy(
                src_ref=src.at[pid] if scatter else src, dst_ref=land.at[pid if landing else me],
                send_sem=send_sems.at[ti * N_PEER + k], recv_sem=recv_sems.at[ti * N_PEER + k],
                device_id=dev, device_id_type=MESH_ID))
    return out


def _push_start(name, srcs, scatter, after=None):
    n = len(srcs)
    slot = lambda s: s.shape[1:] if scatter else s.shape
    lands = [lax.empty((N_DEV,) + slot(s), s.dtype) for s in srcs]
    n_after = 0 if after is None else 1

    def body(*refs):
        src_refs, land_refs = refs[:n], refs[n:2 * n]
        send_sems, recv_sems = refs[2 * n + n_after], refs[2 * n + n_after + 1]
        token = refs[-1]
        for cp in _push_copies(src_refs, land_refs, send_sems, recv_sems, scatter, False):
            cp.start()
        token[...] = jnp.zeros_like(token)

    hbm = lambda a: pltpu.with_memory_space_constraint(a, pltpu.HBM)
    operands = [hbm(a) for a in srcs + lands] + ([after] if n_after else [])
    res = pl.pallas_call(
        body, name=name,
        out_shape=[pltpu.SemaphoreType.DMA((n * N_PEER,)), pltpu.SemaphoreType.DMA((n * N_PEER,))]
        + [pltpu.HBM(a.shape, a.dtype) for a in srcs + lands] + [_sds((8, 128), F32)],
        in_specs=[HBM] * (2 * n) + [ANY] * n_after,
        out_specs=[SEM, SEM] + [HBM] * (2 * n) + [pl.BlockSpec(memory_space=pltpu.VMEM)],
        input_output_aliases={i: 2 + i for i in range(2 * n)},
        compiler_params=pltpu.CompilerParams(has_side_effects=EFFECT),
    )(*operands)
    return (res[0], res[1], list(res[2:2 + n]), list(res[2 + n:2 + 2 * n]), scatter), res[-1]


def _push_wait(name, handle, after):
    send_sems, recv_sems, srcs, lands, scatter = handle
    n = len(srcs)

    def body(*refs):
        src_refs, land_refs = refs[:n], refs[n:2 * n]
        s_sems, r_sems = refs[2 * n], refs[2 * n + 1]
        for cp in _push_copies(src_refs, land_refs, s_sems, r_sems, scatter, True):
            cp.wait_send()
            cp.wait_recv()

    res = pl.pallas_call(
        body, name=name,
        out_shape=[pltpu.HBM(a.shape, a.dtype) for a in srcs + lands],
        in_specs=[HBM] * (2 * n) + [SEM, SEM, ANY], out_specs=[HBM] * (2 * n),
        input_output_aliases={i: i for i in range(2 * n)},
        compiler_params=pltpu.CompilerParams(has_side_effects=EFFECT),
    )(*srcs, *lands, send_sems, recv_sems, after)
    return list(res[n:])


def _adamw(name, parts, w, m, v, own=None):
    r, c = w.shape
    br = _pick(r, 256, 16)
    has_own = own is not None

    def body(*refs):
        if has_own:
            p_ref, own_ref, w_ref, m_ref, v_ref, g_ref, d_ref, nm_ref, nv_ref = refs
            me = _my_id()
            mine = own_ref[...].astype(F32)
        else:
            p_ref, w_ref, m_ref, v_ref, g_ref, d_ref, nm_ref, nv_ref = refs
        g = None
        for k in range(N_DEV):
            t = p_ref[k].astype(F32)
            if has_own:
                t = jnp.where(me == k, mine, t)
            g = t if g is None else g + t
        mm = ADAM_B1 * m_ref[...] + (1.0 - ADAM_B1) * g
        vv = ADAM_B2 * v_ref[...] + (1.0 - ADAM_B2) * (g * g)
        m_hat = mm / (1.0 - ADAM_B1 ** ADAM_STEP)
        v_hat = vv / (1.0 - ADAM_B2 ** ADAM_STEP)
        g_ref[...] = g
        d_ref[...] = -ADAM_LR * (m_hat / (jnp.sqrt(v_hat) + ADAM_EPS) + ADAM_WD * w_ref[...])
        nm_ref[...] = mm
        nv_ref[...] = vv

    spec = _rows(br, c)
    out = (_sds((r, c), F32), spec)
    ins = [(parts, pl.BlockSpec((N_DEV, br, c), lambda i: (0, i, 0)))] + ([(own, spec)] if has_own else [])
    return _call(name, body, (r // br,), ins + [(w, spec), (m, spec), (v, spec)], [out] * 4, sem=("parallel",))


def _pad_head_cols(w, d):
    k = w.shape[0]
    return jnp.pad(w.reshape(k, HEADS, d), ((0, 0), (0, 0), (0, HP - d))).reshape(k, HW)


def _unpad_head_cols(wp, d):
    k = wp.shape[0]
    return wp.reshape(k, HEADS, HP)[:, :, :d].reshape(k, HEADS * d)


def _pad_head_rows(w, d):
    n = w.shape[1]
    return jnp.pad(w.reshape(HEADS, d, n), ((0, 0), (0, HP - d), (0, 0))).reshape(HW, n)


def _unpad_head_rows(wp, d):
    n = wp.shape[1]
    return wp.reshape(HEADS, HP, n)[:, :d, :].reshape(HEADS * d, n)


IN_SEGS = (("q", Q_RANK), ("kv", KV_RANK), ("kr", ROPE), ("fq", FOX_W), ("fk", FOX_W), ("fv", FOX_W),
           ("fl", HEADS), ("gate", 2 * D_MODEL))


def _split_w_in(w):
    seg = {}
    o = 0
    for nm, wd in IN_SEGS:
        seg[nm] = w[:, o:o + wd]
        o += wd
    d = w.shape[0]
    z = lambda n: jnp.zeros((d, n), w.dtype)
    fused = jnp.concatenate([_pad_head_cols(seg[nm], FOX_DIM) for nm in ("fq", "fk", "fv")], axis=1)
    last = jnp.concatenate([seg["fl"], z(LANE_PE - HEADS), seg["kr"], z(HP - LANE_PE - ROPE)], axis=1)
    rest = jnp.concatenate([seg["q"], seg["kv"], last, z(R_GATE - R_LAST - HP), seg["gate"]], axis=1)
    return fused, rest


def _merge_w_in(fused, rest):
    f = [_unpad_head_cols(fused[:, i * HW:(i + 1) * HW], FOX_DIM) for i in range(3)]
    last = rest[:, R_LAST:R_LAST + HP]
    return jnp.concatenate([rest[:, R_QLAT:R_LAST], last[:, LANE_PE:LANE_PE + ROPE], f[0], f[1], f[2],
                            last[:, LANE_FL:LANE_FL + HEADS], rest[:, R_GATE:]], axis=1)


def _split_w_kv(w):
    k = w.shape[0]
    w3 = w.reshape(k, HEADS, NOPE + V_DIM)
    padl = lambda a: jnp.pad(a, ((0, 0), (0, 0), (0, HP - a.shape[-1]))).reshape(k, HW)
    return padl(w3[..., :NOPE]), padl(w3[..., NOPE:])


def _merge_w_kv(wk, wv):
    k = wk.shape[0]
    return jnp.concatenate([wk.reshape(k, HEADS, HP)[..., :NOPE], wv.reshape(k, HEADS, HP)[..., :V_DIM]],
                           axis=-1).reshape(k, HEADS * (NOPE + V_DIM))


class _NoComm:
    first_token = ()

    def late_weights(self, after):
        return {}

    def send(self, name, grads):
        return ()


def _local_step(xcat, tgt, seq, p, comm=_NoComm()):
    r = xcat.shape[0]
    cd = MXU_DTYPE
    p = dict(p)

    w_f, w_r = _split_w_in(p["w_in"])
    w_q = _pad_head_cols(p["w_q_up"], QK_DIM)
    w_k, w_v = _split_w_kv(p["w_kv_up"])

    pos = jnp.arange(r, dtype=F32)
    inv_freq = ROPE_THETA ** (-jnp.arange(HALF, dtype=F32) / HALF)
    ang = pos[:, None] * inv_freq[None, :]
    cos_t = jnp.tile(jnp.cos(ang), (1, HP // HALF))
    sin_t = jnp.tile(jnp.sin(ang), (1, HP // HALF))
    bf_row = jnp.zeros((1, HP), F32).at[0, LANE_FL:LANE_FL + HEADS].set(p["b_forget"])

    h0, h0b = _ln_fwd("ln_emb_fwd", xcat, None, p["ln_emb_g"], p["ln_emb_b"], after=comm.first_token)
    proj_f = _matmul("in_proj_f", h0b, w_f, out_dtype=cd)
    proj_r = _matmul("in_proj_r", h0b, w_r)
    ql = _rms_fwd("q_norm_fwd", proj_r, R_QLAT // Q_RANK, Q_RANK, p["q_norm_g"])
    kvl = _rms_fwd("kv_norm_fwd", proj_r, R_KVLAT // KV_RANK, KV_RANK, p["kv_norm_g"])
    q_raw = _matmul("q_up", ql, w_q)
    k_part = _matmul("k_up", kvl, w_k)
    v_mla = _matmul("v_up", kvl, w_v, out_dtype=cd)
    q_mla, k_mla = _rope_fwd(q_raw, k_part, proj_r, cos_t, sin_t)
    o_mla, o_mla_b, lse_mla = _attn_fwd("mla_fwd", (q_mla, 0), (k_mla, 0), (v_mla, 0), QK_DIM ** -0.5)

    cum, cum_t = _forget_fwd(proj_r, bf_row)
    o_fox, o_fox_b, lse_fox = _attn_fwd("fox_fwd", (proj_f, 0), (proj_f, 1), (proj_f, 2), FOX_DIM ** -0.5, cum, cum_t)

    p.update(comm.late_weights(o_fox_b))
    w_bm = _pad_head_rows(p["w_branch_mla"], V_DIM)
    w_bf = _pad_head_rows(p["w_branch_fox"], FOX_DIM)
    bm = _matmul("branch_mla", o_mla_b, w_bm)
    bfx = _matmul("branch_fox", o_fox_b, w_bf)
    merged = _gate_fwd(proj_r, p["b_gate"], bm, bfx)
    mix = _matmul("out_proj", merged, p["w_out"])
    h1, h1b = _ln_fwd("ln_mix_fwd", h0, mix, p["ln_mix_g"], p["ln_mix_b"])
    up = _matmul("ffn_up", h1b, p["w_ffn_up"])
    act = _glu_fwd(up, p["conv_w"], p["conv_b"])
    f = _matmul("ffn_down", act, p["w_ffn_down"])
    h2, _ = _ln_fwd("ln_ffn_fwd", h1, f, p["ln_ffn_g"], p["ln_ffn_b"])
    loss, dh2 = _loss(h2, tgt, seq)

    g = {}
    dz2, dz2b, g["ln_ffn_g"], g["ln_ffn_b"] = _ln_bwd("ln_ffn_bwd", h1, f, dh2, p["ln_ffn_g"])
    d_act = _matmul("ffn_down_dx", dz2b, p["w_ffn_down"], tb=True)
    g["w_ffn_down"] = _matmul("ffn_down_dw", act, dz2b, ta=True)
    dgate, dval, dcw, g["conv_b"] = _glu_bwd_gate(up, p["conv_w"], p["conv_b"], d_act)
    g["conv_w"] = dcw[:3]
    d_up = _glu_bwd_conv(dgate, dval, p["conv_w"])
    dh1 = _matmul("ffn_up_dx", d_up, p["w_ffn_up"], tb=True, addend=dz2, alpha=ALPHA)
    g["w_ffn_up"] = _matmul("ffn_up_dw", h1b, d_up, ta=True)
    sent = comm.send("ffn", {n: g[n] for n in ("w_ffn_down", "w_ffn_up", "conv_w")})
    dz1, dz1b, g["ln_mix_g"], g["ln_mix_b"] = _ln_bwd("ln_mix_bwd", h0, mix, dh1, p["ln_mix_g"], after=sent)
    dmerged = _matmul("out_proj_dx", dz1b, p["w_out"], tb=True)
    g["w_out"] = _matmul("out_proj_dw", merged, dz1b, ta=True)
    d_bm, d_bf, d_gl, g["b_gate"] = _gate_bwd(proj_r, p["b_gate"], bm, bfx, dmerged)
    d_o_mla = _matmul("branch_mla_dx", d_bm, w_bm, tb=True)
    g["w_branch_mla"] = _unpad_head_rows(_matmul("branch_mla_dw", o_mla_b, d_bm, ta=True), V_DIM)
    d_o_fox = _matmul("branch_fox_dx", d_bf, w_bf, tb=True)
    g["w_branch_fox"] = _unpad_head_rows(_matmul("branch_fox_dw", o_fox_b, d_bf, ta=True), FOX_DIM)

    sent = comm.send("mix", {n: g[n] for n in ("w_out", "w_branch_mla", "w_branch_fox")})
    dl_mla, do_mla_b = _attn_delta("mla_delta", d_o_mla, o_mla, after=sent)
    dq_m, dk_m, dv_m = _attn_bwd("mla_bwd", (q_mla, 0), (k_mla, 0), (v_mla, 0), do_mla_b, lse_mla, dl_mla, QK_DIM ** -0.5)
    dl_fox, do_fox_b = _attn_delta("fox_delta", d_o_fox, o_fox)
    dfq, dfk, dfv, dcq, dck = _attn_bwd("fox_bwd", (proj_f, 0), (proj_f, 1), (proj_f, 2), do_fox_b, lse_fox, dl_fox,
                                        FOX_DIM ** -0.5, cum, cum_t)
    dfl, dbf = _forget_bwd(proj_r, bf_row, dcq, dck)
    g["b_forget"] = dbf[:, LANE_FL:LANE_FL + HEADS]

    dq_b, dk_b, dlast = _rope_bwd(dq_m, dk_m, dfl, cos_t, sin_t)
    dv_b = dv_m.astype(cd)
    d_ql = _matmul("q_up_dx", dq_b, w_q, tb=True)
    g["w_q_up"] = _unpad_head_cols(_matmul("q_up_dw", ql, dq_b, ta=True), QK_DIM)
    d_kvl = _matmul("k_up_dx", dk_b, w_k, tb=True)
    d_kvl = _matmul("v_up_dx", dv_b, w_v, tb=True, addend=d_kvl)
    g["w_kv_up"] = _merge_w_kv(_matmul("k_up_dw", kvl, dk_b, ta=True), _matmul("v_up_dw", kvl, dv_b, ta=True))
    d_qlat, g["q_norm_g"] = _rms_bwd("q_norm_bwd", proj_r, R_QLAT // Q_RANK, Q_RANK, d_ql, p["q_norm_g"])
    d_kvlat, g["kv_norm_g"] = _rms_bwd("kv_norm_bwd", proj_r, R_KVLAT // KV_RANK, KV_RANK, d_kvl, p["kv_norm_g"])
    dproj_f = jnp.concatenate([dfq.astype(cd), dfk.astype(cd), dfv.astype(cd)], axis=1)
    dproj_r = jnp.concatenate([d_qlat, d_kvlat, dlast, jnp.zeros((r, R_GATE - R_LAST - HP), cd), d_gl], axis=1)
    dh0 = _matmul("in_proj_f_dx", dproj_f, w_f, tb=True, addend=dz1, alpha=ALPHA)
    dh0 = _matmul("in_proj_r_dx", dproj_r, w_r, tb=True, addend=dh0)
    g["w_in"] = _merge_w_in(_matmul("in_proj_f_dw", h0b, dproj_f, ta=True), _matmul("in_proj_r_dw", h0b, dproj_r, ta=True))
    dxcat, _, g["ln_emb_g"], g["ln_emb_b"] = _ln_bwd("ln_emb_bwd", xcat, None, dh0, p["ln_emb_g"])
    return loss, dxcat, g


BIG = (("w_in", 1), ("w_q_up", 1), ("w_kv_up", 1), ("w_branch_mla", 1), ("w_branch_fox", 1), ("w_out", 0),
       ("w_ffn_up", 1), ("w_ffn_down", 0))
SMALL_SHARDED = (("meta_tokens", 1), ("conv_w", 1))
EARLY = ("w_in", "w_q_up", "w_kv_up", "meta_tokens", "conv_w")
LATE = ("w_branch_mla", "w_branch_fox", "w_out", "w_ffn_up", "w_ffn_down")
LAST = ("w_in", "w_q_up", "w_kv_up", "meta_tokens")
REPLICATED = ("ln_emb_g", "ln_emb_b", "b_gate", "b_forget", "q_norm_g", "kv_norm_g", "ln_mix_g", "ln_mix_b",
              "conv_b", "ln_ffn_g", "ln_ffn_b")
PACK_COLS = 1024


def _pack(flat_list):
    cat = jnp.concatenate(flat_list)
    n = cat.shape[0]
    rows = -(-n // (8 * PACK_COLS)) * 8
    return jnp.pad(cat, (0, rows * PACK_COLS - n)).reshape(rows, PACK_COLS)


def _gathered_full(g3, axis):
    n, r, c = g3.shape
    if axis == 0:
        return g3.reshape(n * r, c)
    return g3.transpose(1, 0, 2).reshape(r, n * c)


def _shard_major(full, axis):
    r, c = full.shape
    if axis == 0:
        return full.reshape(N_DEV, r // N_DEV, c)
    return full.reshape(r, N_DEV, c // N_DEV).transpose(1, 0, 2)


def kernel(x, meta_tokens, ln_emb_g, ln_emb_b, w_in, b_gate, b_forget, q_norm_g, w_q_up, kv_norm_g, w_kv_up, w_branch_mla, w_branch_fox, w_out, ln_mix_g, ln_mix_b, w_ffn_up, conv_w, conv_b, w_ffn_down, ln_ffn_g, ln_ffn_b, loss_target, m_meta_tokens, m_ln_emb_g, m_ln_emb_b, m_w_in, m_b_gate, m_b_forget, m_q_norm_g, m_w_q_up, m_kv_norm_g, m_w_kv_up, m_w_branch_mla, m_w_branch_fox, m_w_out, m_ln_mix_g, m_ln_mix_b, m_w_ffn_up, m_conv_w, m_conv_b, m_w_ffn_down, m_ln_ffn_g, m_ln_ffn_b, v_meta_tokens, v_ln_emb_g, v_ln_emb_b, v_w_in, v_b_gate, v_b_forget, v_q_norm_g, v_w_q_up, v_kv_norm_g, v_w_kv_up, v_w_branch_mla, v_w_branch_fox, v_w_out, v_ln_mix_g, v_ln_mix_b, v_w_ffn_up, v_conv_w, v_conv_b, v_w_ffn_down, v_ln_ffn_g, v_ln_ffn_b):
    names = ("meta_tokens", "ln_emb_g", "ln_emb_b", "w_in", "b_gate", "b_forget", "q_norm_g", "w_q_up", "kv_norm_g",
             "w_kv_up", "w_branch_mla", "w_branch_fox", "w_out", "ln_mix_g", "ln_mix_b", "w_ffn_up", "conv_w", "conv_b",
             "w_ffn_down", "ln_ffn_g", "ln_ffn_b")
    w_args = (meta_tokens, ln_emb_g, ln_emb_b, w_in, b_gate, b_forget, q_norm_g, w_q_up, kv_norm_g, w_kv_up,
              w_branch_mla, w_branch_fox, w_out, ln_mix_g, ln_mix_b, w_ffn_up, conv_w, conv_b, w_ffn_down, ln_ffn_g, ln_ffn_b)
    m_args = (m_meta_tokens, m_ln_emb_g, m_ln_emb_b, m_w_in, m_b_gate, m_b_forget, m_q_norm_g, m_w_q_up, m_kv_norm_g,
              m_w_kv_up, m_w_branch_mla, m_w_branch_fox, m_w_out, m_ln_mix_g, m_ln_mix_b, m_w_ffn_up, m_conv_w, m_conv_b,
              m_w_ffn_down, m_ln_ffn_g, m_ln_ffn_b)
    v_args = (v_meta_tokens, v_ln_emb_g, v_ln_emb_b, v_w_in, v_b_gate, v_b_forget, v_q_norm_g, v_w_q_up, v_kv_norm_g,
              v_w_kv_up, v_w_branch_mla, v_w_branch_fox, v_w_out, v_ln_mix_g, v_ln_mix_b, v_w_ffn_up, v_conv_w, v_conv_b,
              v_w_ffn_down, v_ln_ffn_g, v_ln_ffn_b)
    as2d = lambda a: a.reshape((-1, a.shape[-1])) if a.ndim != 1 else a.reshape(1, -1)
    w = {n: as2d(a) for n, a in zip(names, w_args)}
    m = {n: as2d(a) for n, a in zip(names, m_args)}
    v = {n: as2d(a) for n, a in zip(names, v_args)}
    out_shape = {n: a.shape for n, a in zip(names, w_args)}

    seq = x.shape[1]
    rows = -(-(N_META + seq) // ROW_ALIGN) * ROW_ALIGN
    axis_of = dict(BIG + SMALL_SHARDED)
    big = set(n for n, _ in BIG)
    wire = lambda n, a: a.astype(MXU_DTYPE) if n in big else a
    my_id = _my_id()
    slot_is_mine = lax.broadcasted_iota(jnp.int32, (N_DEV, 1, 1), 0) == my_id

    early = _allgather("gather_early", [wire(n, w[n]) for n in EARLY])
    p = {n: _gathered_full(g3, axis_of[n]) for n, g3 in zip(EARLY, early)}
    for n in REPLICATED:
        p[n] = w[n].reshape(-1)
    late_src = [wire(n, w[n]) for n in LATE]
    late_handle, late_token = _push_start("gather_late_start", late_src, False, after=early[0])
    sent = {}

    class Comm:
        first_token = (late_token,)

        def late_weights(self, after):
            lands = _push_wait("gather_late_wait", late_handle, after)
            return {n: _gathered_full(jnp.where(slot_is_mine, own[None], land), axis_of[n])
                    for n, own, land in zip(LATE, late_src, lands)}

        def send(self, name, grads):
            names_ = tuple(grads)
            parts = [_shard_major(grads[n], axis_of[n]).astype(MXU_DTYPE) for n in names_]
            handle, token = _push_start("send_" + name + "_start", parts, True)
            sent[name] = (names_, parts, handle)
            return (token,)

    zpad = jnp.zeros((rows - N_META - seq, D_MODEL), F32)
    xcat = jnp.concatenate([p["meta_tokens"], x[0], zpad], axis=0)
    tgt = jnp.concatenate([jnp.zeros((N_META, D_MODEL), F32), loss_target[0], zpad], axis=0)
    loss_part, dxcat, g = _local_step(xcat, tgt, seq, p, Comm())
    grad_x = dxcat[N_META:N_META + seq][None]
    g["meta_tokens"] = dxcat[:N_META]
    loss = lax.psum(loss_part[0, 0], ("x", "y", "c"))

    g_last = _exchange("exchange_last", [_shard_major(g[n], axis_of[n]).astype(MXU_DTYPE) for n in LAST])
    rep_all = _allgather("gather_small_grads", [_pack([g[n].reshape(-1) for n in REPLICATED])])[0]

    res = {}
    for n, parts in zip(LAST, g_last):
        res[n] = _adamw("adamw_" + n, parts, w[n], m[n], v[n])
    for name, (names_, parts, handle) in sent.items():
        lands = _push_wait("send_" + name + "_wait", handle, rep_all)
        for n, part, land in zip(names_, parts, lands):
            own = lax.dynamic_index_in_dim(part, my_id, axis=0, keepdims=False)
            res[n] = _adamw("adamw_" + n, land, w[n], m[n], v[n], own=own)
    rep_w = _pack([w[n].reshape(-1) for n in REPLICATED])
    rep_m = _pack([m[n].reshape(-1) for n in REPLICATED])
    rep_v = _pack([v[n].reshape(-1) for n in REPLICATED])
    rep_res = _adamw("adamw_replicated", rep_all, rep_w, rep_m, rep_v)
    off = 0
    for n in REPLICATED:
        sz = w[n].size
        res[n] = tuple(a.reshape(-1)[off:off + sz] for a in rep_res)
        off += sz

    outs = [loss, grad_x]
    for idx in range(4):
        outs += [res[n][idx].reshape(out_shape[n]) for n in names]
    return tuple(outs)
```

```python
import jax
import jax.numpy as jnp
from jax import lax
from jax.experimental import pallas as pl
from jax.experimental.pallas import tpu as pltpu

F32 = jnp.float32
BF16 = jnp.bfloat16
MXU_DTYPE = BF16

N_DEV = 8
N_META = 16
D_MODEL = 1024
HEADS = 8
Q_RANK = 384
KV_RANK = 128
NOPE = 64
ROPE = 32
HALF = ROPE // 2
QK_DIM = NOPE + ROPE
V_DIM = 64
FOX_DIM = 64
FOX_W = HEADS * FOX_DIM
D_FF = 2816
ROPE_THETA = 10000.0
LN_EPS = 1e-5
RMS_EPS = 1e-6
ALPHA = 2.0 ** 0.25
NEG_INF = -1e30

HP = 128
HW = HEADS * HP
F_W = 3 * HW
R_QLAT = 0
R_KVLAT = Q_RANK
R_LAST = R_KVLAT + KV_RANK
R_GATE = D_MODEL
R_W = R_GATE + 2 * D_MODEL
LANE_FL = 0
LANE_PE = NOPE

ADAM_LR = 0.001
ADAM_B1 = 0.9
ADAM_B2 = 0.999
ADAM_EPS = 1e-08
ADAM_WD = 0.01
ADAM_STEP = 10

ROW_BLOCK = 128
ATT_TQ = 256
ATT_TK = 256
ATT_HEADS = 2
ROW_ALIGN = 256
MM_BLOCK_CAP = 1408
VMEM_LIMIT = 56 * 1024 * 1024
HIGHEST = lax.Precision.HIGHEST
NT = (((1,), (1,)), ((), ()))
TN = (((0,), (0,)), ((), ()))


def _params(sem=None):
    return pltpu.CompilerParams(dimension_semantics=sem, vmem_limit_bytes=VMEM_LIMIT)


def _call(name, body, grid, ins, outs, scratch=(), sem=None, after=()):
    n_in = len(ins)
    n_tok = len(after)

    def run(*refs):
        body(*refs[:n_in], *refs[n_in + n_tok:])

    tok_spec = pl.BlockSpec((8, 128), lambda *_: (0, 0))
    return pl.pallas_call(
        run, name=name, grid=grid,
        in_specs=[s for _, s in ins] + [tok_spec] * n_tok,
        out_specs=[s for _, s in outs],
        out_shape=[o for o, _ in outs],
        scratch_shapes=list(scratch),
        compiler_params=_params(sem),
    )(*[a for a, _ in ins], *after)


def _sds(shape, dtype):
    return jax.ShapeDtypeStruct(shape, dtype)


def _rows(br, c, cb=0):
    return pl.BlockSpec((br, c), lambda i: (i, cb))


def _whole(shape):
    n = len(shape)
    return pl.BlockSpec(shape, lambda i: (0,) * n)


def _pick(dim, cap, mult):
    best = None
    d = mult
    while d <= min(dim, cap):
        if dim % d == 0:
            best = d
        d += mult
    return best if best is not None else dim


def _hs(h):
    return slice(h * HP, (h + 1) * HP)


def _matmul(name, a, b, *, ta=False, tb=False, out_dtype=F32, addend=None, alpha=1.0, after=()):
    if ta:
        k, m = a.shape
    else:
        m, k = a.shape
    if tb:
        n, k2 = b.shape
    else:
        k2, n = b.shape
    assert k == k2, (name, a.shape, b.shape)
    bm = _pick(m, MM_BLOCK_CAP, 128 if ta else 16)
    bn = _pick(n, MM_BLOCK_CAP, 128)
    bk = _pick(k, MM_BLOCK_CAP, 128 if (not ta or tb) else 16)
    nk = k // bk
    dims = (((0 if ta else 1,), (1 if tb else 0,)), ((), ()))
    has_add = addend is not None

    def body(*refs):
        a_ref, b_ref = refs[:2]
        add_ref = refs[2] if has_add else None
        o_ref = refs[3 if has_add else 2]

        def finish(r):
            if has_add:
                r = r + alpha * add_ref[...]
            o_ref[...] = r.astype(o_ref.dtype)

        part = lax.dot_general(a_ref[...], b_ref[...], dims, preferred_element_type=F32)
        if nk == 1:
            finish(part)
            return
        acc_ref = refs[-1]
        kk = pl.program_id(2)

        @pl.when(kk == 0)
        def _():
            acc_ref[...] = part

        @pl.when(kk > 0)
        def _():
            acc_ref[...] += part

        @pl.when(kk == nk - 1)
        def _():
            finish(acc_ref[...])

    a_spec = pl.BlockSpec((bk, bm), lambda i, j, l: (l, i)) if ta else pl.BlockSpec((bm, bk), lambda i, j, l: (i, l))
    b_spec = pl.BlockSpec((bn, bk), lambda i, j, l: (j, l)) if tb else pl.BlockSpec((bk, bn), lambda i, j, l: (l, j))
    o_spec = pl.BlockSpec((bm, bn), lambda i, j, l: (i, j))
    ins = [(a, a_spec), (b, b_spec)]
    if has_add:
        ins.append((addend, o_spec))
    return _call(name, body, (m // bm, n // bn, nk), ins, [(_sds((m, n), out_dtype), o_spec)],
                 scratch=[pltpu.VMEM((bm, bn), F32)] if nk > 1 else [],
                 sem=("parallel", "parallel", "arbitrary"), after=after)[0]


def _ln_stats(z):
    mu = jnp.mean(z, axis=-1, keepdims=True)
    zc = z - mu
    var = jnp.mean(zc * zc, axis=-1, keepdims=True)
    rstd = lax.rsqrt(var + LN_EPS)
    return zc * rstd, rstd


def _ln_fwd(name, a, res, g, b, after=()):
    r, d = a.shape
    br = ROW_BLOCK
    has_res = res is not None

    def body(*refs):
        if has_res:
            a_ref, r_ref, g_ref, b_ref, y_ref, yb_ref = refs
            z = ALPHA * a_ref[...] + r_ref[...]
        else:
            a_ref, g_ref, b_ref, y_ref, yb_ref = refs
            z = a_ref[...]
        xhat, _ = _ln_stats(z)
        y = xhat * g_ref[...] + b_ref[...]
        y_ref[...] = y
        yb_ref[...] = y.astype(yb_ref.dtype)

    ins = [(a, _rows(br, d))]
    if has_res:
        ins.append((res, _rows(br, d)))
    ins += [(g.reshape(1, d), _whole((1, d))), (b.reshape(1, d), _whole((1, d)))]
    outs = [(_sds((r, d), F32), _rows(br, d)), (_sds((r, d), MXU_DTYPE), _rows(br, d))]
    return _call(name, body, (r // br,), ins, outs, sem=("parallel",), after=after)


def _ln_bwd(name, a, res, dy, g, after=()):
    r, d = a.shape
    br = ROW_BLOCK
    has_res = res is not None

    def body(*refs):
        if has_res:
            a_ref, r_ref, dy_ref, g_ref, dz_ref, dzb_ref, dg_ref, db_ref = refs
            z = ALPHA * a_ref[...] + r_ref[...]
        else:
            a_ref, dy_ref, g_ref, dz_ref, dzb_ref, dg_ref, db_ref = refs
            z = a_ref[...]
        xhat, rstd = _ln_stats(z)
        dyv = dy_ref[...]
        dyg = dyv * g_ref[...]
        m1 = jnp.mean(dyg, axis=-1, keepdims=True)
        m2 = jnp.mean(dyg * xhat, axis=-1, keepdims=True)
        dz = rstd * (dyg - m1 - xhat * m2)
        dz_ref[...] = dz
        dzb_ref[...] = dz.astype(dzb_ref.dtype)

        @pl.when(pl.program_id(0) == 0)
        def _():
            dg_ref[...] = jnp.zeros_like(dg_ref)
            db_ref[...] = jnp.zeros_like(db_ref)

        dg_ref[...] += jnp.sum(dyv * xhat, axis=0, keepdims=True)
        db_ref[...] += jnp.sum(dyv, axis=0, keepdims=True)

    ins = [(a, _rows(br, d))]
    if has_res:
        ins.append((res, _rows(br, d)))
    ins += [(dy, _rows(br, d)), (g.reshape(1, d), _whole((1, d)))]
    outs = [(_sds((r, d), F32), _rows(br, d)), (_sds((r, d), MXU_DTYPE), _rows(br, d)),
            (_sds((1, d), F32), _whole((1, d))), (_sds((1, d), F32), _whole((1, d)))]
    return _call(name, body, (r // br,), ins, outs, sem=("arbitrary",), after=after)


def _rms_fwd(name, proj, cb, width, g):
    r = proj.shape[0]
    br = ROW_BLOCK

    def body(x_ref, g_ref, y_ref):
        x = x_ref[...]
        rstd = lax.rsqrt(jnp.mean(x * x, axis=-1, keepdims=True) + RMS_EPS)
        y_ref[...] = (x * rstd * g_ref[...]).astype(y_ref.dtype)

    return _call(name, body, (r // br,), [(proj, _rows(br, width, cb)), (g.reshape(1, width), _whole((1, width)))],
                 [(_sds((r, width), MXU_DTYPE), _rows(br, width))], sem=("parallel",))[0]


def _rms_bwd(name, proj, cb, width, dy, g):
    r = proj.shape[0]
    br = ROW_BLOCK

    def body(x_ref, dy_ref, g_ref, dx_ref, dg_ref):
        x = x_ref[...]
        rstd = lax.rsqrt(jnp.mean(x * x, axis=-1, keepdims=True) + RMS_EPS)
        nrm = x * rstd
        dyv = dy_ref[...]
        dyg = dyv * g_ref[...]
        dx = rstd * (dyg - nrm * jnp.mean(dyg * nrm, axis=-1, keepdims=True))
        dx_ref[...] = dx.astype(dx_ref.dtype)

        @pl.when(pl.program_id(0) == 0)
        def _():
            dg_ref[...] = jnp.zeros_like(dg_ref)

        dg_ref[...] += jnp.sum(dyv * nrm, axis=0, keepdims=True)

    return _call(name, body, (r // br,),
                 [(proj, _rows(br, width, cb)), (dy, _rows(br, width)), (g.reshape(1, width), _whole((1, width)))],
                 [(_sds((r, width), MXU_DTYPE), _rows(br, width)), (_sds((1, width), F32), _whole((1, width)))],
                 sem=("arbitrary",))


def _lane_iota(shape):
    return lax.broadcasted_iota(jnp.int32, shape, 1)


def _rotary(t, c, s, lane, sign):
    second = pltpu.roll(t, HP - HALF, axis=1)
    first = pltpu.roll(t, HALF, axis=1)
    lo = (lane >= LANE_PE) & (lane < LANE_PE + HALF)
    hi = (lane >= LANE_PE + HALF) & (lane < LANE_PE + ROPE)
    return jnp.where(lo, t * c - sign * second * s, jnp.where(hi, t * c + sign * first * s, t))


def _rope_fwd(q_raw, k_part, proj_r, cos_t, sin_t):
    r = q_raw.shape[0]
    br = ROW_BLOCK

    def body(q_ref, k_ref, t_ref, c_ref, s_ref, qo_ref, ko_ref):
        c = c_ref[...]
        s = s_ref[...]
        lane = _lane_iota((br, HP))
        pe = (lane >= LANE_PE) & (lane < LANE_PE + ROPE)
        kp = jnp.where(pe, _rotary(t_ref[...], c, s, lane, 1.0), 0.0)
        for h in range(HEADS):
            qo_ref[:, _hs(h)] = _rotary(q_ref[:, _hs(h)], c, s, lane, 1.0).astype(qo_ref.dtype)
            ko_ref[:, _hs(h)] = (k_ref[:, _hs(h)] + kp).astype(ko_ref.dtype)

    blk = _rows(br, HP)
    wide = _rows(br, HW)
    return _call("rope_fwd", body, (r // br,),
                 [(q_raw, wide), (k_part, wide), (proj_r, _rows(br, HP, R_LAST // HP)), (cos_t, blk), (sin_t, blk)],
                 [(_sds((r, HW), MXU_DTYPE), wide)] * 2, sem=("parallel",))


def _rope_bwd(dq, dk, dfl, cos_t, sin_t):
    r = dq.shape[0]
    br = ROW_BLOCK

    def body(dq_ref, dk_ref, fl_ref, c_ref, s_ref, dqo_ref, dko_ref, dl_ref):
        c = c_ref[...]
        s = s_ref[...]
        lane = _lane_iota((br, HP))
        pe = (lane >= LANE_PE) & (lane < LANE_PE + ROPE)
        acc = jnp.zeros((br, HP), F32)
        for h in range(HEADS):
            dqo_ref[:, _hs(h)] = _rotary(dq_ref[:, _hs(h)], c, s, lane, -1.0).astype(dqo_ref.dtype)
            dkh = dk_ref[:, _hs(h)]
            acc = acc + dkh
            dko_ref[:, _hs(h)] = dkh.astype(dko_ref.dtype)
        dl_ref[...] = (jnp.where(pe, _rotary(acc, c, s, lane, -1.0), 0.0) + fl_ref[...]).astype(dl_ref.dtype)

    blk = _rows(br, HP)
    wide = _rows(br, HW)
    return _call("rope_bwd", body, (r // br,),
                 [(dq, wide), (dk, wide), (dfl, blk), (cos_t, blk), (sin_t, blk)],
                 [(_sds((r, HW), MXU_DTYPE), wide), (_sds((r, HW), MXU_DTYPE), wide), (_sds((r, HP), MXU_DTYPE), blk)],
                 sem=("parallel",))


def _log_sigmoid(x):
    return jnp.minimum(x, 0.0) - jnp.log(1.0 + jnp.exp(-jnp.abs(x)))


def _head_lane(x, h, lane):
    return jnp.sum(jnp.where(lane == h, x, 0.0), axis=1, keepdims=True)


def _forget_fwd(proj_r, bf_row):
    r = proj_r.shape[0]
    br = ROW_BLOCK

    def body(t_ref, b_ref, ob_ref, ot_ref, carry_ref):
        @pl.when(pl.program_id(0) == 0)
        def _():
            carry_ref[...] = jnp.zeros_like(carry_ref)

        x = t_ref[...] + b_ref[...]
        lane = _lane_iota(x.shape)
        lf = jnp.where((lane >= LANE_FL) & (lane < LANE_FL + HEADS), _log_sigmoid(x), 0.0)
        tri = (lax.broadcasted_iota(jnp.int32, (br, br), 0) >= lax.broadcasted_iota(jnp.int32, (br, br), 1)).astype(F32)
        cum = jnp.dot(tri, lf, precision=HIGHEST, preferred_element_type=F32) + carry_ref[0:1, :]
        for h in range(HEADS):
            ob_ref[:, _hs(h)] = jnp.broadcast_to(_head_lane(cum, LANE_FL + h, lane), (br, HP))
        ot_ref[...] = cum.T[LANE_FL:LANE_FL + HEADS, :]
        carry_ref[...] = jnp.broadcast_to(cum[br - 1:br, :], carry_ref.shape)

    return _call("forget_fwd", body, (r // br,),
                 [(proj_r, _rows(br, HP, R_LAST // HP)), (bf_row, _whole((1, HP)))],
                 [(_sds((r, HW), F32), _rows(br, HW)), (_sds((HEADS, r), F32), pl.BlockSpec((HEADS, br), lambda i: (0, i)))],
                 scratch=[pltpu.VMEM((8, HP), F32)], sem=("arbitrary",))


def _forget_bwd(proj_r, bf_row, dcq_t, dck_b):
    r = proj_r.shape[0]
    br = ROW_BLOCK
    nb = r // br

    def body(t_ref, b_ref, dcq_ref, dck_ref, o_ref, db_ref, carry_ref):
        @pl.when(pl.program_id(0) == 0)
        def _():
            carry_ref[...] = jnp.zeros_like(carry_ref)
            db_ref[...] = jnp.zeros_like(db_ref)

        lane = _lane_iota((br, HP))
        dc = jnp.concatenate([dcq_ref[...], jnp.zeros((br - HEADS, br), F32)], axis=0).T
        for h in range(HEADS):
            dc = dc + jnp.where(lane == LANE_FL + h, dck_ref[:, h * HP:h * HP + 1], 0.0)
        triu = (lax.broadcasted_iota(jnp.int32, (br, br), 0) <= lax.broadcasted_iota(jnp.int32, (br, br), 1)).astype(F32)
        dlf = jnp.dot(triu, dc, precision=HIGHEST, preferred_element_type=F32) + carry_ref[0:1, :]
        carry_ref[...] = jnp.broadcast_to(dlf[0:1, :], carry_ref.shape)
        x = t_ref[...] + b_ref[...]
        dfl = jnp.where((lane >= LANE_FL) & (lane < LANE_FL + HEADS), dlf * jax.nn.sigmoid(-x), 0.0)
        o_ref[...] = dfl
        db_ref[...] += jnp.sum(dfl, axis=0, keepdims=True)

    rev = pl.BlockSpec((br, HP), lambda i: (nb - 1 - i, 0))
    return _call("forget_bwd", body, (nb,),
                 [(proj_r, pl.BlockSpec((br, HP), lambda i: (nb - 1 - i, R_LAST // HP))), (bf_row, _whole((1, HP))),
                  (dcq_t, pl.BlockSpec((HEADS, br), lambda i: (0, nb - 1 - i))),
                  (dck_b, pl.BlockSpec((br, HW), lambda i: (nb - 1 - i, 0)))],
                 [(_sds((r, HP), F32), rev), (_sds((1, HP), F32), _whole((1, HP)))],
                 scratch=[pltpu.VMEM((8, HP), F32)], sem=("arbitrary",))


def _gate_fwd(proj_r, b_gate, bm, bfx):
    r, d = bm.shape
    br = ROW_BLOCK
    cb = R_GATE // d

    def body(gm_ref, gf_ref, b1_ref, b2_ref, bm_ref, bf_ref, o_ref):
        g1 = jax.nn.sigmoid(gm_ref[...] + b1_ref[...])
        g2 = jax.nn.sigmoid(gf_ref[...] + b2_ref[...])
        o_ref[...] = (g1 * bm_ref[...] + g2 * bf_ref[...]).astype(o_ref.dtype)

    b1 = b_gate[:d].reshape(1, d)
    b2 = b_gate[d:].reshape(1, d)
    return _call("gate_fwd", body, (r // br,),
                 [(proj_r, _rows(br, d, cb)), (proj_r, _rows(br, d, cb + 1)), (b1, _whole((1, d))), (b2, _whole((1, d))),
                  (bm, _rows(br, d)), (bfx, _rows(br, d))],
                 [(_sds((r, d), MXU_DTYPE), _rows(br, d))], sem=("parallel",))[0]


def _gate_bwd(proj_r, b_gate, bm, bfx, dmerged):
    r, d = bm.shape
    br = ROW_BLOCK
    cb = R_GATE // d

    def body(gm_ref, gf_ref, b1_ref, b2_ref, bm_ref, bf_ref, dm_ref, dbm_ref, dbf_ref, dgl_ref, dbg_ref):
        g1 = jax.nn.sigmoid(gm_ref[...] + b1_ref[...])
        g2 = jax.nn.sigmoid(gf_ref[...] + b2_ref[...])
        dm = dm_ref[...]
        dbm_ref[...] = (dm * g1).astype(dbm_ref.dtype)
        dbf_ref[...] = (dm * g2).astype(dbf_ref.dtype)
        dl1 = dm * bm_ref[...] * (g1 * (1.0 - g1))
        dl2 = dm * bf_ref[...] * (g2 * (1.0 - g2))
        dgl_ref[:, 0:d] = dl1.astype(dgl_ref.dtype)
        dgl_ref[:, d:2 * d] = dl2.astype(dgl_ref.dtype)

        @pl.when(pl.program_id(0) == 0)
        def _():
            dbg_ref[...] = jnp.zeros_like(dbg_ref)

        dbg_ref[:, 0:d] += jnp.sum(dl1, axis=0, keepdims=True)
        dbg_ref[:, d:2 * d] += jnp.sum(dl2, axis=0, keepdims=True)

    b1 = b_gate[:d].reshape(1, d)
    b2 = b_gate[d:].reshape(1, d)
    return _call("gate_bwd", body, (r // br,),
                 [(proj_r, _rows(br, d, cb)), (proj_r, _rows(br, d, cb + 1)), (b1, _whole((1, d))), (b2, _whole((1, d))),
                  (bm, _rows(br, d)), (bfx, _rows(br, d)), (dmerged, _rows(br, d))],
                 [(_sds((r, d), MXU_DTYPE), _rows(br, d)), (_sds((r, d), MXU_DTYPE), _rows(br, d)),
                  (_sds((r, 2 * d), MXU_DTYPE), _rows(br, 2 * d)), (_sds((1, 2 * d), F32), _whole((1, 2 * d)))],
                 sem=("arbitrary",))


def _conv_taps(gp, halo, first_block):
    halo = jnp.where(first_block, 0.0, halo)
    rid = lax.broadcasted_iota(jnp.int32, gp.shape, 0)
    g1 = jnp.where(rid == 0, halo[7:8, :], pltpu.roll(gp, 1, axis=0))
    g2 = jnp.where(rid == 0, halo[6:7, :], jnp.where(rid == 1, halo[7:8, :], pltpu.roll(gp, 2, axis=0)))
    return g1, g2


def _prev_halo(br, c):
    return pl.BlockSpec((8, c), lambda i: (jnp.maximum(i * (br // 8) - 1, 0), 0))


def _glu_fwd(up, conv_w, conv_b):
    r = up.shape[0]
    c = D_FF
    br = ROW_BLOCK

    def body(gp_ref, halo_ref, val_ref, w_ref, b_ref, o_ref):
        gp = gp_ref[...]
        g1, g2 = _conv_taps(gp, halo_ref[...], pl.program_id(0) == 0)
        gate = w_ref[0:1, :] * g2 + w_ref[1:2, :] * g1 + w_ref[2:3, :] * gp + b_ref[...]
        o_ref[...] = (gate * jax.nn.sigmoid(gate) * val_ref[...]).astype(o_ref.dtype)

    return _call("glu_fwd", body, (r // br,),
                 [(up, _rows(br, c, 0)), (up, _prev_halo(br, c)), (up, _rows(br, c, 1)),
                  (conv_w, _whole((3, c))), (conv_b.reshape(1, c), _whole((1, c)))],
                 [(_sds((r, c), MXU_DTYPE), _rows(br, c))], sem=("parallel",))[0]


def _glu_bwd_gate(up, conv_w, conv_b, d_act):
    r = up.shape[0]
    c = D_FF
    br = ROW_BLOCK

    def body(gp_ref, halo_ref, val_ref, w_ref, b_ref, da_ref, dg_ref, dv_ref, dw_ref, db_ref):
        gp = gp_ref[...]
        g1, g2 = _conv_taps(gp, halo_ref[...], pl.program_id(0) == 0)
        gate = w_ref[0:1, :] * g2 + w_ref[1:2, :] * g1 + w_ref[2:3, :] * gp + b_ref[...]
        sg = jax.nn.sigmoid(gate)
        da = da_ref[...]
        dv_ref[...] = (da * (gate * sg)).astype(dv_ref.dtype)
        dg = da * val_ref[...] * (sg * (1.0 + gate * (1.0 - sg)))
        dg_ref[...] = dg

        @pl.when(pl.program_id(0) == 0)
        def _():
            dw_ref[...] = jnp.zeros_like(dw_ref)
            db_ref[...] = jnp.zeros_like(db_ref)

        dw_ref[0:1, :] += jnp.sum(dg * g2, axis=0, keepdims=True)
        dw_ref[1:2, :] += jnp.sum(dg * g1, axis=0, keepdims=True)
        dw_ref[2:3, :] += jnp.sum(dg * gp, axis=0, keepdims=True)
        db_ref[...] += jnp.sum(dg, axis=0, keepdims=True)

    return _call("glu_bwd_gate", body, (r // br,),
                 [(up, _rows(br, c, 0)), (up, _prev_halo(br, c)), (up, _rows(br, c, 1)),
                  (conv_w, _whole((3, c))), (conv_b.reshape(1, c), _whole((1, c))), (d_act, _rows(br, c))],
                 [(_sds((r, c), F32), _rows(br, c)), (_sds((r, c), MXU_DTYPE), _rows(br, c)),
                  (_sds((8, c), F32), _whole((8, c))), (_sds((1, c), F32), _whole((1, c)))],
                 sem=("arbitrary",))


def _glu_bwd_conv(dg, dval, conv_w):
    r, c = dg.shape
    br = ROW_BLOCK
    nb = r // br

    def body(dg_ref, nxt_ref, dv_ref, w_ref, o_ref):
        x = dg_ref[...]
        nxt = jnp.where(pl.program_id(0) == nb - 1, 0.0, nxt_ref[...])
        rid = lax.broadcasted_iota(jnp.int32, x.shape, 0)
        u1 = jnp.where(rid == br - 1, nxt[0:1, :], pltpu.roll(x, br - 1, axis=0))
        u2 = jnp.where(rid == br - 1, nxt[1:2, :], jnp.where(rid == br - 2, nxt[0:1, :], pltpu.roll(x, br - 2, axis=0)))
        dgp = w_ref[2:3, :] * x + w_ref[1:2, :] * u1 + w_ref[0:1, :] * u2
        o_ref[:, 0:c] = dgp.astype(o_ref.dtype)
        o_ref[:, c:2 * c] = dv_ref[...]

    nxt_spec = pl.BlockSpec((8, c), lambda i: (jnp.minimum((i + 1) * (br // 8), r // 8 - 1), 0))
    return _call("glu_bwd_conv", body, (nb,),
                 [(dg, _rows(br, c)), (dg, nxt_spec), (dval, _rows(br, c)), (conv_w, _whole((3, c)))],
                 [(_sds((r, 2 * c), MXU_DTYPE), _rows(br, 2 * c))], sem=("parallel",))[0]


def _loss(h2, tgt, seq):
    r, d = h2.shape
    br = ROW_BLOCK

    def body(h_ref, t_ref, l_ref, d_ref):
        rid = lax.broadcasted_iota(jnp.int32, (br, d), 0) + pl.program_id(0) * br
        valid = (rid >= N_META) & (rid < N_META + seq)
        err = jnp.where(valid, h_ref[...] - t_ref[...], 0.0)
        d_ref[...] = err * (1.0 / d)

        @pl.when(pl.program_id(0) == 0)
        def _():
            l_ref[...] = jnp.zeros_like(l_ref)

        l_ref[...] += jnp.sum(jnp.sum(err * err, axis=1, keepdims=True), axis=0, keepdims=True) * (0.5 / d)

    return _call("loss", body, (r // br,), [(h2, _rows(br, d)), (tgt, _rows(br, d))],
                 [(_sds((1, 1), F32), _whole((1, 1))), (_sds((r, d), F32), _rows(br, d))], sem=("arbitrary",))


def _attn_fwd(name, q, k, v, scale, cum_b=None, cum_t=None):
    (qa, qg), (ka, kg), (va, vg) = q, k, v
    r = qa.shape[0]
    tq, tk = ATT_TQ, ATT_TK
    nq, nk = r // tq, r // tk
    bias = cum_b is not None

    def body(*refs):
        if bias:
            q_ref, k_ref, v_ref, cb_ref, ct_ref, o_ref, ob_ref, lse_ref = refs
        else:
            q_ref, k_ref, v_ref, o_ref, ob_ref, lse_ref = refs
        i = pl.program_id(1)
        qs = [q_ref[:, _hs(hh)] for hh in range(hg)]
        cqs = [cb_ref[:, hh * HP:hh * HP + 1] for hh in range(hg)] if bias else None
        diff = lax.broadcasted_iota(jnp.int32, (tq, tk), 1) - lax.broadcasted_iota(jnp.int32, (tq, tk), 0)

        def step(j, carry, masked):
            off = pl.multiple_of(j * tk, tk)
            out = []
            for hh in range(hg):
                m, l, acc = carry[hh]
                kt = k_ref[pl.ds(off, tk), _hs(hh)]
                vt = v_ref[pl.ds(off, tk), _hs(hh)]
                s = lax.dot_general(qs[hh], kt, NT, preferred_element_type=F32) * scale
                if bias:
                    s = s + (cqs[hh] - ct_ref[hh, j])
                if masked:
                    s = jnp.where(diff <= i * tq - j * tk, s, NEG_INF)
                m_new = jnp.maximum(m, jnp.max(s, axis=1, keepdims=True))
                p = jnp.exp(s - m_new)
                a = jnp.exp(m - m_new)
                l = a * l + jnp.sum(p, axis=1, keepdims=True)
                acc = a * acc + jnp.dot(p.astype(vt.dtype), vt, preferred_element_type=F32)
                out.append((m_new, l, acc))
            return tuple(out)

        n_clear = (i * tq + 1) // tk
        n_all = ((i + 1) * tq - 1) // tk + 1
        carry = tuple((jnp.full((tq, 1), NEG_INF, F32), jnp.zeros((tq, 1), F32), jnp.zeros((tq, HP), F32))
                      for _ in range(hg))
        carry = lax.fori_loop(0, n_clear, lambda j, c: step(j, c, False), carry)
        carry = lax.fori_loop(n_clear, n_all, lambda j, c: step(j, c, True), carry)
        for hh in range(hg):
            m, l, acc = carry[hh]
            o = acc / l
            o_ref[:, _hs(hh)] = o
            ob_ref[:, _hs(hh)] = o.astype(ob_ref.dtype)
            lse_ref[hh] = jnp.broadcast_to(m + jnp.log(l), (tq, HP)).T[0:1, :]

    hg = ATT_HEADS
    w = hg * HP
    gpw = HW // w
    tile = lambda g: pl.BlockSpec((tq, w), lambda h, i: (i, g * gpw + h))
    res = lambda g: pl.BlockSpec((r, w), lambda h, i: (0, g * gpw + h))
    ins = [(qa, tile(qg)), (ka, res(kg)), (va, res(vg))]
    if bias:
        ins += [(cum_b, tile(0)),
                (cum_t.reshape(HEADS, nk, 1, tk), pl.BlockSpec((hg, nk, 1, tk), lambda h, i: (h, 0, 0, 0)))]
    outs = [(_sds((r, HW), F32), tile(0)), (_sds((r, HW), MXU_DTYPE), tile(0)),
            (_sds((HEADS, nq, 1, tq), F32), pl.BlockSpec((hg, None, 1, tq), lambda h, i: (h, i, 0, 0)))]
    o, ob, lse = _call(name, body, (gpw, nq), ins, outs, sem=("parallel", "parallel"))
    return o, ob, lse.reshape(HEADS, r)


def _attn_delta(name, do, o, after=()):
    r = do.shape[0]
    br = ROW_BLOCK

    def body(do_ref, o_ref, d_ref, dob_ref):
        lane = _lane_iota((br, HP))
        d = jnp.zeros((br, HP), F32)
        for h in range(HEADS):
            dh = do_ref[:, _hs(h)]
            d = jnp.where(lane == h, jnp.sum(dh * o_ref[:, _hs(h)], axis=1, keepdims=True), d)
            dob_ref[:, _hs(h)] = dh.astype(dob_ref.dtype)
        d_ref[...] = d.T[0:HEADS, :]

    wide = _rows(br, HW)
    return _call(name, body, (r // br,), [(do, wide), (o, wide)],
                 [(_sds((HEADS, r), F32), pl.BlockSpec((HEADS, br), lambda i: (0, i))), (_sds((r, HW), MXU_DTYPE), wide)],
                 sem=("parallel",), after=after)


def _attn_bwd(name, q, k, v, do_b, lse_t, delta_t, scale, cum_b=None, cum_t=None):
    (qa, qg), (ka, kg), (va, vg) = q, k, v
    r = qa.shape[0]
    tq, tk = ATT_TQ, ATT_TK
    nq, nk = r // tq, r // tk
    bias = cum_b is not None

    def body(*refs):
        if bias:
            (q_ref, k_ref, v_ref, do_ref, lse_ref, dl_ref, cb_ref, ct_ref,
             dq_ref, dk_ref, dv_ref, dcq_ref, dck_ref, dqt_ref) = refs
        else:
            q_ref, k_ref, v_ref, do_ref, lse_ref, dl_ref, dq_ref, dk_ref, dv_ref, dqt_ref = refs
        j = pl.program_id(1)

        @pl.when(j == 0)
        def _():
            dqt_ref[...] = jnp.zeros_like(dqt_ref)
            if bias:
                dcq_ref[...] = jnp.zeros_like(dcq_ref)

        kts = [k_ref[:, _hs(hh)] for hh in range(hg)]
        vts = [v_ref[:, _hs(hh)] for hh in range(hg)]
        k_trs = [kt.astype(F32).T.astype(kt.dtype) for kt in kts]
        cks = [cb_ref[:, hh * HP:hh * HP + 1] for hh in range(hg)] if bias else None
        diff = lax.broadcasted_iota(jnp.int32, (tk, tq), 0) - lax.broadcasted_iota(jnp.int32, (tk, tq), 1)

        def step(i, carry, masked):
            rows = pl.ds(pl.multiple_of(i * tq, tq), tq)
            out = []
            for hh in range(hg):
                dk_acc, dv_acc, dck_acc = carry[hh]
                qt = q_ref[rows, _hs(hh)]
                dot = do_ref[rows, _hs(hh)]
                s = lax.dot_general(kts[hh], qt, NT, preferred_element_type=F32) * scale
                if bias:
                    s = s + (ct_ref[hh, i] - cks[hh])
                if masked:
                    s = jnp.where(diff <= i * tq - j * tk, s, NEG_INF)
                p = jnp.exp(s - lse_ref[hh, i])
                dp = lax.dot_general(vts[hh], dot, NT, preferred_element_type=F32)
                ds = p * (dp - dl_ref[hh, i])
                pb = p.astype(dot.dtype)
                dsb = ds.astype(qt.dtype)
                dv_acc = dv_acc + jnp.dot(pb, dot, preferred_element_type=F32)
                dk_acc = dk_acc + jnp.dot(dsb, qt, preferred_element_type=F32)
                dqt_ref[hh, i] += jnp.dot(k_trs[hh], dsb, preferred_element_type=F32)
                if bias:
                    dcq_ref[hh, i] += jnp.sum(ds, axis=0, keepdims=True)
                    dck_acc = dck_acc - jnp.sum(ds, axis=1, keepdims=True)
                out.append((dk_acc, dv_acc, dck_acc))
            return tuple(out)

        i_first = (j * tk) // tq
        i_clear = jnp.minimum(((j + 1) * tk + tq - 2) // tq, nq)
        carry = tuple((jnp.zeros((tk, HP), F32), jnp.zeros((tk, HP), F32), jnp.zeros((tk, 1), F32)) for _ in range(hg))
        carry = lax.fori_loop(i_first, i_clear, lambda i, c: step(i, c, True), carry)
        carry = lax.fori_loop(i_clear, nq, lambda i, c: step(i, c, False), carry)
        for hh in range(hg):
            dk_acc, dv_acc, dck_acc = carry[hh]
            dk_ref[:, _hs(hh)] = dk_acc * scale
            dv_ref[:, _hs(hh)] = dv_acc
            if bias:
                dck_ref[:, _hs(hh)] = jnp.broadcast_to(dck_acc, (tk, HP))

        @pl.when(j == nk - 1)
        def _():
            for hh in range(hg):
                for i in range(nq):
                    dq_ref[i * tq:(i + 1) * tq, _hs(hh)] = dqt_ref[hh, i].T * scale

    hg = ATT_HEADS
    w = hg * HP
    gpw = HW // w
    res = lambda g: pl.BlockSpec((r, w), lambda h, j: (0, g * gpw + h))
    tile = lambda g: pl.BlockSpec((tk, w), lambda h, j: (j, g * gpw + h))
    rowv = pl.BlockSpec((hg, nq, 1, tq), lambda h, j: (h, 0, 0, 0))
    as_rows = lambda a: a.reshape(HEADS, nq, 1, tq)
    ins = [(qa, res(qg)), (ka, tile(kg)), (va, tile(vg)), (do_b, res(0)), (as_rows(lse_t), rowv), (as_rows(delta_t), rowv)]
    outs = [(_sds((r, HW), F32), res(0)), (_sds((r, HW), F32), tile(0)), (_sds((r, HW), F32), tile(0))]
    if bias:
        ins += [(cum_b, tile(0)), (as_rows(cum_t), rowv)]
        outs += [(_sds((HEADS, nq, 1, tq), F32), rowv), (_sds((r, HW), F32), tile(0))]
    res_out = _call(name, body, (gpw, nk), ins, outs, scratch=[pltpu.VMEM((hg, nq, HP, tq), F32)],
                    sem=("parallel", "arbitrary"))
    if bias:
        dq, dk, dv, dcq, dck = res_out
        return dq, dk, dv, dcq.reshape(HEADS, r), dck
    return res_out


MESH_ID = pl.DeviceIdType.MESH
ANY = pl.BlockSpec(memory_space=pl.ANY)


def _allgather(name, shards):
    n = len(shards)

    def body(*refs):
        x_refs, out_refs = refs[:n], refs[n:2 * n]
        send_sems, recv_sems, local_sems = refs[2 * n:]
        x, y, c = lax.axis_index("x"), lax.axis_index("y"), lax.axis_index("c")
        me, sibling = (x, y, c), (x, y, 1 - c)
        chips = [(1 - x, y), (x, 1 - y), (1 - x, 1 - y)]

        def slot(ti, px, py, pc):
            return out_refs[ti].at[4 * px + 2 * py + pc]

        def copy(ti, k, block, to, src=None):
            return pltpu.make_async_remote_copy(
                src_ref=slot(ti, *block) if src is None else src, dst_ref=slot(ti, *block),
                send_sem=send_sems.at[ti, k], recv_sem=recv_sems.at[ti, k], device_id=to, device_id_type=MESH_ID)

        mine = [pltpu.make_async_copy(x_refs[ti], slot(ti, *me), local_sems.at[ti]) for ti in range(n)]
        for cp in mine:
            cp.start()
        started = []
        for ti in range(n):
            first = [copy(ti, 0, me, sibling, src=x_refs[ti])]
            first += [copy(ti, 1 + j, me, (*chip, c), src=x_refs[ti]) for j, chip in enumerate(chips)]
            for cp in first:
                cp.start()
            started += first
        for ti in range(n):
            for j, chip in enumerate(chips):
                copy(ti, 1 + j, (*chip, c), me).wait_recv()
                fwd = copy(ti, 4 + j, (*chip, c), sibling)
                fwd.start()
                started.append(fwd)
        for ti in range(n):
            copy(ti, 0, sibling, me).wait_recv()
            for j, chip in enumerate(chips):
                copy(ti, 4 + j, (*chip, 1 - c), me).wait_recv()
        for cp in started:
            cp.wait_send()
        for cp in mine:
            cp.wait()

    return pl.pallas_call(
        body, name=name, out_shape=[_sds((N_DEV,) + s.shape, s.dtype) for s in shards],
        in_specs=[ANY] * n, out_specs=[ANY] * n,
        scratch_shapes=[pltpu.SemaphoreType.DMA((n, 7)), pltpu.SemaphoreType.DMA((n, 7)), pltpu.SemaphoreType.DMA((n,))],
    )(*shards)


def _exchange(name, parts):
    n = len(parts)

    def body(*refs):
        g_refs, out_refs = refs[:n], refs[n:2 * n]
        send_sems, recv_sems, local_sems = refs[2 * n:]
        x, y, c = lax.axis_index("x"), lax.axis_index("y"), lax.axis_index("c")
        my_id = 4 * x + 2 * y + c

        def peer(k):
            return (1 - x if k & 4 else x, 1 - y if k & 2 else y, 1 - c if k & 1 else c)

        def copy(ti, k, landing):
            px, py, pc = peer(k)
            pid = 4 * px + 2 * py + pc
            return pltpu.make_async_remote_copy(
                src_ref=g_refs[ti].at[pid], dst_ref=out_refs[ti].at[pid if landing else my_id],
                send_sem=send_sems.at[ti, k - 1], recv_sem=recv_sems.at[ti, k - 1],
                device_id=(px, py, pc), device_id_type=MESH_ID)

        mine = [pltpu.make_async_copy(g_refs[ti].at[my_id], out_refs[ti].at[my_id], local_sems.at[ti]) for ti in range(n)]
        for cp in mine:
            cp.start()
        sends = [copy(ti, k, False) for ti in range(n) for k in range(1, N_DEV)]
        for cp in sends:
            cp.start()
        for ti in range(n):
            for k in range(1, N_DEV):
                copy(ti, k, True).wait_recv()
        for cp in sends:
            cp.wait_send()
        for cp in mine:
            cp.wait()

    return pl.pallas_call(
        body, name=name, out_shape=[_sds(p.shape, p.dtype) for p in parts],
        in_specs=[ANY] * n, out_specs=[ANY] * n,
        scratch_shapes=[pltpu.SemaphoreType.DMA((n, 7)), pltpu.SemaphoreType.DMA((n, 7)), pltpu.SemaphoreType.DMA((n,))],
    )(*parts)


HBM = pl.BlockSpec(memory_space=pltpu.HBM)
SEM = pl.BlockSpec(memory_space=pltpu.SEMAPHORE)
EFFECT = pltpu.SideEffectType.DATAFLOW_SIDE_EFFECTING
N_PEER = N_DEV - 1


def _my_id():
    return 4 * lax.axis_index("x") + 2 * lax.axis_index("y") + lax.axis_index("c")


def _peers():
    x, y, c = lax.axis_index("x"), lax.axis_index("y"), lax.axis_index("c")
    out = []
    for k in range(1, N_DEV):
        px, py, pc = (1 - x if k & 4 else x, 1 - y if k & 2 else y, 1 - c if k & 1 else c)
        out.append(((px, py, pc), 4 * px + 2 * py + pc))
    return out


def _push_copies(src_refs, land_refs, send_sems, recv_sems, scatter, landing):
    me = _my_id()
    out = []
    for ti, (src, land) in enumerate(zip(src_refs, land_refs)):
        for k, (dev, pid) in enumerate(_peers()):
            out.append(pltpu.make_async_remote_copy(
                src_ref=src.at[pid] if scatter else src, dst_ref=land.at[pid if landing else me],
                send_sem=send_sems.at[ti * N_PEER + k], recv_sem=recv_sems.at[ti * N_PEER + k],
                device_id=dev, device_id_type=MESH_ID))
    return out


def _push_start(name, srcs, scatter, after=None):
    n = len(srcs)
    slot = lambda s: s.shape[1:] if scatter else s.shape
    lands = [lax.empty((N_DEV,) + slot(s), s.dtype) for s in srcs]
    n_after = 0 if after is None else 1

    def body(*refs):
        src_refs, land_refs = refs[:n], refs[n:2 * n]
        send_sems, recv_sems = refs[2 * n + n_after], refs[2 * n + n_after + 1]
        token = refs[-1]
        for cp in _push_copies(src_refs, land_refs, send_sems, recv_sems, scatter, False):
            cp.start()
        token[...] = jnp.zeros_like(token)

    hbm = lambda a: pltpu.with_memory_space_constraint(a, pltpu.HBM)
    operands = [hbm(a) for a in srcs + lands] + ([after] if n_after else [])
    res = pl.pallas_call(
        body, name=name,
        out_shape=[pltpu.SemaphoreType.DMA((n * N_PEER,)), pltpu.SemaphoreType.DMA((n * N_PEER,))]
        + [pltpu.HBM(a.shape, a.dtype) for a in srcs + lands] + [_sds((8, 128), F32)],
        in_specs=[HBM] * (2 * n) + [ANY] * n_after,
        out_specs=[SEM, SEM] + [HBM] * (2 * n) + [pl.BlockSpec(memory_space=pltpu.VMEM)],
        input_output_aliases={i: 2 + i for i in range(2 * n)},
        compiler_params=pltpu.CompilerParams(has_side_effects=EFFECT),
    )(*operands)
    return (res[0], res[1], list(res[2:2 + n]), list(res[2 + n:2 + 2 * n]), scatter), res[-1]


def _push_wait(name, handle, after):
    send_sems, recv_sems, srcs, lands, scatter = handle
    n = len(srcs)

    def body(*refs):
        src_refs, land_refs = refs[:n], refs[n:2 * n]
        s_sems, r_sems = refs[2 * n], refs[2 * n + 1]
        for cp in _push_copies(src_refs, land_refs, s_sems, r_sems, scatter, True):
            cp.wait_send()
            cp.wait_recv()

    res = pl.pallas_call(
        body, name=name,
        out_shape=[pltpu.HBM(a.shape, a.dtype) for a in srcs + lands],
        in_specs=[HBM] * (2 * n) + [SEM, SEM, ANY], out_specs=[HBM] * (2 * n),
        input_output_aliases={i: i for i in range(2 * n)},
        compiler_params=pltpu.CompilerParams(has_side_effects=EFFECT),
    )(*srcs, *lands, send_sems, recv_sems, after)
    return list(res[n:])


def _adamw(name, parts, w, m, v, own=None):
    r, c = w.shape
    br = _pick(r, 256, 16)
    has_own = own is not None

    def body(*refs):
        if has_own:
            p_ref, own_ref, w_ref, m_ref, v_ref, g_ref, d_ref, nm_ref, nv_ref = refs
            me = _my_id()
            mine = own_ref[...].astype(F32)
        else:
            p_ref, w_ref, m_ref, v_ref, g_ref, d_ref, nm_ref, nv_ref = refs
        g = None
        for k in range(N_DEV):
            t = p_ref[k].astype(F32)
            if has_own:
                t = jnp.where(me == k, mine, t)
            g = t if g is None else g + t
        mm = ADAM_B1 * m_ref[...] + (1.0 - ADAM_B1) * g
        vv = ADAM_B2 * v_ref[...] + (1.0 - ADAM_B2) * (g * g)
        m_hat = mm / (1.0 - ADAM_B1 ** ADAM_STEP)
        v_hat = vv / (1.0 - ADAM_B2 ** ADAM_STEP)
        g_ref[...] = g
        d_ref[...] = -ADAM_LR * (m_hat / (jnp.sqrt(v_hat) + ADAM_EPS) + ADAM_WD * w_ref[...])
        nm_ref[...] = mm
        nv_ref[...] = vv

    spec = _rows(br, c)
    out = (_sds((r, c), F32), spec)
    ins = [(parts, pl.BlockSpec((N_DEV, br, c), lambda i: (0, i, 0)))] + ([(own, spec)] if has_own else [])
    return _call(name, body, (r // br,), ins + [(w, spec), (m, spec), (v, spec)], [out] * 4, sem=("parallel",))


def _pad_head_cols(w, d):
    k = w.shape[0]
    return jnp.pad(w.reshape(k, HEADS, d), ((0, 0), (0, 0), (0, HP - d))).reshape(k, HW)


def _unpad_head_cols(wp, d):
    k = wp.shape[0]
    return wp.reshape(k, HEADS, HP)[:, :, :d].reshape(k, HEADS * d)


def _pad_head_rows(w, d):
    n = w.shape[1]
    return jnp.pad(w.reshape(HEADS, d, n), ((0, 0), (0, HP - d), (0, 0))).reshape(HW, n)


def _unpad_head_rows(wp, d):
    n = wp.shape[1]
    return wp.reshape(HEADS, HP, n)[:, :d, :].reshape(HEADS * d, n)


IN_SEGS = (("q", Q_RANK), ("kv", KV_RANK), ("kr", ROPE), ("fq", FOX_W), ("fk", FOX_W), ("fv", FOX_W),
           ("fl", HEADS), ("gate", 2 * D_MODEL))


def _split_w_in(w):
    seg = {}
    o = 0
    for nm, wd in IN_SEGS:
        seg[nm] = w[:, o:o + wd]
        o += wd
    d = w.shape[0]
    z = lambda n: jnp.zeros((d, n), w.dtype)
    fused = jnp.concatenate([_pad_head_cols(seg[nm], FOX_DIM) for nm in ("fq", "fk", "fv")], axis=1)
    last = jnp.concatenate([seg["fl"], z(LANE_PE - HEADS), seg["kr"], z(HP - LANE_PE - ROPE)], axis=1)
    rest = jnp.concatenate([seg["q"], seg["kv"], last, z(R_GATE - R_LAST - HP), seg["gate"]], axis=1)
    return fused, rest


def _merge_w_in(fused, rest):
    f = [_unpad_head_cols(fused[:, i * HW:(i + 1) * HW], FOX_DIM) for i in range(3)]
    last = rest[:, R_LAST:R_LAST + HP]
    return jnp.concatenate([rest[:, R_QLAT:R_LAST], last[:, LANE_PE:LANE_PE + ROPE], f[0], f[1], f[2],
                            last[:, LANE_FL:LANE_FL + HEADS], rest[:, R_GATE:]], axis=1)


def _split_w_kv(w):
    k = w.shape[0]
    w3 = w.reshape(k, HEADS, NOPE + V_DIM)
    padl = lambda a: jnp.pad(a, ((0, 0), (0, 0), (0, HP - a.shape[-1]))).reshape(k, HW)
    return padl(w3[..., :NOPE]), padl(w3[..., NOPE:])


def _merge_w_kv(wk, wv):
    k = wk.shape[0]
    return jnp.concatenate([wk.reshape(k, HEADS, HP)[..., :NOPE], wv.reshape(k, HEADS, HP)[..., :V_DIM]],
                           axis=-1).reshape(k, HEADS * (NOPE + V_DIM))


class _NoComm:
    first_token = ()

    def late_weights(self, after):
        return {}

    def send(self, name, grads):
        return ()


def _local_step(xcat, tgt, seq, p, comm=_NoComm()):
    r = xcat.shape[0]
    cd = MXU_DTYPE
    p = dict(p)

    w_f, w_r = _split_w_in(p["w_in"])
    w_q = _pad_head_cols(p["w_q_up"], QK_DIM)
    w_k, w_v = _split_w_kv(p["w_kv_up"])

    pos = jnp.arange(r, dtype=F32)
    inv_freq = ROPE_THETA ** (-jnp.arange(HALF, dtype=F32) / HALF)
    ang = pos[:, None] * inv_freq[None, :]
    cos_t = jnp.tile(jnp.cos(ang), (1, HP // HALF))
    sin_t = jnp.tile(jnp.sin(ang), (1, HP // HALF))
    bf_row = jnp.zeros((1, HP), F32).at[0, LANE_FL:LANE_FL + HEADS].set(p["b_forget"])

    h0, h0b = _ln_fwd("ln_emb_fwd", xcat, None, p["ln_emb_g"], p["ln_emb_b"], after=comm.first_token)
    proj_f = _matmul("in_proj_f", h0b, w_f, out_dtype=cd)
    proj_r = _matmul("in_proj_r", h0b, w_r)
    ql = _rms_fwd("q_norm_fwd", proj_r, R_QLAT // Q_RANK, Q_RANK, p["q_norm_g"])
    kvl = _rms_fwd("kv_norm_fwd", proj_r, R_KVLAT // KV_RANK, KV_RANK, p["kv_norm_g"])
    q_raw = _matmul("q_up", ql, w_q)
    k_part = _matmul("k_up", kvl, w_k)
    v_mla = _matmul("v_up", kvl, w_v, out_dtype=cd)
    q_mla, k_mla = _rope_fwd(q_raw, k_part, proj_r, cos_t, sin_t)
    o_mla, o_mla_b, lse_mla = _attn_fwd("mla_fwd", (q_mla, 0), (k_mla, 0), (v_mla, 0), QK_DIM ** -0.5)

    cum, cum_t = _forget_fwd(proj_r, bf_row)
    o_fox, o_fox_b, lse_fox = _attn_fwd("fox_fwd", (proj_f, 0), (proj_f, 1), (proj_f, 2), FOX_DIM ** -0.5, cum, cum_t)

    p.update(comm.late_weights(o_fox_b))
    w_bm = _pad_head_rows(p["w_branch_mla"], V_DIM)
    w_bf = _pad_head_rows(p["w_branch_fox"], FOX_DIM)
    bm = _matmul("branch_mla", o_mla_b, w_bm)
    bfx = _matmul("branch_fox", o_fox_b, w_bf)
    merged = _gate_fwd(proj_r, p["b_gate"], bm, bfx)
    mix = _matmul("out_proj", merged, p["w_out"])
    h1, h1b = _ln_fwd("ln_mix_fwd", h0, mix, p["ln_mix_g"], p["ln_mix_b"])
    up = _matmul("ffn_up", h1b, p["w_ffn_up"])
    act = _glu_fwd(up, p["conv_w"], p["conv_b"])
    f = _matmul("ffn_down", act, p["w_ffn_down"])
    h2, _ = _ln_fwd("ln_ffn_fwd", h1, f, p["ln_ffn_g"], p["ln_ffn_b"])
    loss, dh2 = _loss(h2, tgt, seq)

    g = {}
    dz2, dz2b, g["ln_ffn_g"], g["ln_ffn_b"] = _ln_bwd("ln_ffn_bwd", h1, f, dh2, p["ln_ffn_g"])
    d_act = _matmul("ffn_down_dx", dz2b, p["w_ffn_down"], tb=True)
    g["w_ffn_down"] = _matmul("ffn_down_dw", act, dz2b, ta=True)
    dgate, dval, dcw, g["conv_b"] = _glu_bwd_gate(up, p["conv_w"], p["conv_b"], d_act)
    g["conv_w"] = dcw[:3]
    d_up = _glu_bwd_conv(dgate, dval, p["conv_w"])
    dh1 = _matmul("ffn_up_dx", d_up, p["w_ffn_up"], tb=True, addend=dz2, alpha=ALPHA)
    g["w_ffn_up"] = _matmul("ffn_up_dw", h1b, d_up, ta=True)
    sent = comm.send("ffn", {n: g[n] for n in ("w_ffn_down", "w_ffn_up", "conv_w")})
    dz1, dz1b, g["ln_mix_g"], g["ln_mix_b"] = _ln_bwd("ln_mix_bwd", h0, mix, dh1, p["ln_mix_g"], after=sent)
    dmerged = _matmul("out_proj_dx", dz1b, p["w_out"], tb=True)
    g["w_out"] = _matmul("out_proj_dw", merged, dz1b, ta=True)
    d_bm, d_bf, d_gl, g["b_gate"] = _gate_bwd(proj_r, p["b_gate"], bm, bfx, dmerged)
    d_o_mla = _matmul("branch_mla_dx", d_bm, w_bm, tb=True)
    g["w_branch_mla"] = _unpad_head_rows(_matmul("branch_mla_dw", o_mla_b, d_bm, ta=True), V_DIM)
    d_o_fox = _matmul("branch_fox_dx", d_bf, w_bf, tb=True)
    g["w_branch_fox"] = _unpad_head_rows(_matmul("branch_fox_dw", o_fox_b, d_bf, ta=True), FOX_DIM)

    sent = comm.send("mix", {n: g[n] for n in ("w_out", "w_branch_mla", "w_branch_fox")})
    dl_mla, do_mla_b = _attn_delta("mla_delta", d_o_mla, o_mla, after=sent)
    dq_m, dk_m, dv_m = _attn_bwd("mla_bwd", (q_mla, 0), (k_mla, 0), (v_mla, 0), do_mla_b, lse_mla, dl_mla, QK_DIM ** -0.5)
    dl_fox, do_fox_b = _attn_delta("fox_delta", d_o_fox, o_fox)
    dfq, dfk, dfv, dcq, dck = _attn_bwd("fox_bwd", (proj_f, 0), (proj_f, 1), (proj_f, 2), do_fox_b, lse_fox, dl_fox,
                                        FOX_DIM ** -0.5, cum, cum_t)
    dfl, dbf = _forget_bwd(proj_r, bf_row, dcq, dck)
    g["b_forget"] = dbf[:, LANE_FL:LANE_FL + HEADS]

    dq_b, dk_b, dlast = _rope_bwd(dq_m, dk_m, dfl, cos_t, sin_t)
    dv_b = dv_m.astype(cd)
    d_ql = _matmul("q_up_dx", dq_b, w_q, tb=True)
    g["w_q_up"] = _unpad_head_cols(_matmul("q_up_dw", ql, dq_b, ta=True), QK_DIM)
    d_kvl = _matmul("k_up_dx", dk_b, w_k, tb=True)
    d_kvl = _matmul("v_up_dx", dv_b, w_v, tb=True, addend=d_kvl)
    g["w_kv_up"] = _merge_w_kv(_matmul("k_up_dw", kvl, dk_b, ta=True), _matmul("v_up_dw", kvl, dv_b, ta=True))
    d_qlat, g["q_norm_g"] = _rms_bwd("q_norm_bwd", proj_r, R_QLAT // Q_RANK, Q_RANK, d_ql, p["q_norm_g"])
    d_kvlat, g["kv_norm_g"] = _rms_bwd("kv_norm_bwd", proj_r, R_KVLAT // KV_RANK, KV_RANK, d_kvl, p["kv_norm_g"])
    dproj_f = jnp.concatenate([dfq.astype(cd), dfk.astype(cd), dfv.astype(cd)], axis=1)
    dproj_r = jnp.concatenate([d_qlat, d_kvlat, dlast, jnp.zeros((r, R_GATE - R_LAST - HP), cd), d_gl], axis=1)
    g["w_in"] = _merge_w_in(_matmul("in_proj_f_dw", h0b, dproj_f, ta=True), _matmul("in_proj_r_dw", h0b, dproj_r, ta=True))
    sent = comm.send("in", {n: g[n] for n in ("w_in", "w_q_up", "w_kv_up")})
    dh0 = _matmul("in_proj_f_dx", dproj_f, w_f, tb=True, addend=dz1, alpha=ALPHA, after=sent)
    dh0 = _matmul("in_proj_r_dx", dproj_r, w_r, tb=True, addend=dh0)
    dxcat, _, g["ln_emb_g"], g["ln_emb_b"] = _ln_bwd("ln_emb_bwd", xcat, None, dh0, p["ln_emb_g"])
    return loss, dxcat, g


BIG = (("w_in", 1), ("w_q_up", 1), ("w_kv_up", 1), ("w_branch_mla", 1), ("w_branch_fox", 1), ("w_out", 0),
       ("w_ffn_up", 1), ("w_ffn_down", 0))
SMALL_SHARDED = (("meta_tokens", 1), ("conv_w", 1))
EARLY = ("w_in", "w_q_up", "w_kv_up", "meta_tokens", "conv_w")
LATE = ("w_branch_mla", "w_branch_fox", "w_out", "w_ffn_up", "w_ffn_down")
LAST = ("meta_tokens",)
REPLICATED = ("ln_emb_g", "ln_emb_b", "b_gate", "b_forget", "q_norm_g", "kv_norm_g", "ln_mix_g", "ln_mix_b",
              "conv_b", "ln_ffn_g", "ln_ffn_b")
PACK_COLS = 1024


def _pack(flat_list):
    cat = jnp.concatenate(flat_list)
    n = cat.shape[0]
    rows = -(-n // (8 * PACK_COLS)) * 8
    return jnp.pad(cat, (0, rows * PACK_COLS - n)).reshape(rows, PACK_COLS)


def _gathered_full(g3, axis):
    n, r, c = g3.shape
    if axis == 0:
        return g3.reshape(n * r, c)
    return g3.transpose(1, 0, 2).reshape(r, n * c)


def _shard_major(full, axis):
    r, c = full.shape
    if axis == 0:
        return full.reshape(N_DEV, r // N_DEV, c)
    return full.reshape(r, N_DEV, c // N_DEV).transpose(1, 0, 2)


def kernel(x, meta_tokens, ln_emb_g, ln_emb_b, w_in, b_gate, b_forget, q_norm_g, w_q_up, kv_norm_g, w_kv_up, w_branch_mla, w_branch_fox, w_out, ln_mix_g, ln_mix_b, w_ffn_up, conv_w, conv_b, w_ffn_down, ln_ffn_g, ln_ffn_b, loss_target, m_meta_tokens, m_ln_emb_g, m_ln_emb_b, m_w_in, m_b_gate, m_b_forget, m_q_norm_g, m_w_q_up, m_kv_norm_g, m_w_kv_up, m_w_branch_mla, m_w_branch_fox, m_w_out, m_ln_mix_g, m_ln_mix_b, m_w_ffn_up, m_conv_w, m_conv_b, m_w_ffn_down, m_ln_ffn_g, m_ln_ffn_b, v_meta_tokens, v_ln_emb_g, v_ln_emb_b, v_w_in, v_b_gate, v_b_forget, v_q_norm_g, v_w_q_up, v_kv_norm_g, v_w_kv_up, v_w_branch_mla, v_w_branch_fox, v_w_out, v_ln_mix_g, v_ln_mix_b, v_w_ffn_up, v_conv_w, v_conv_b, v_w_ffn_down, v_ln_ffn_g, v_ln_ffn_b):
    names = ("meta_tokens", "ln_emb_g", "ln_emb_b", "w_in", "b_gate", "b_forget", "q_norm_g", "w_q_up", "kv_norm_g",
             "w_kv_up", "w_branch_mla", "w_branch_fox", "w_out", "ln_mix_g", "ln_mix_b", "w_ffn_up", "conv_w", "conv_b",
             "w_ffn_down", "ln_ffn_g", "ln_ffn_b")
    w_args = (meta_tokens, ln_emb_g, ln_emb_b, w_in, b_gate, b_forget, q_norm_g, w_q_up, kv_norm_g, w_kv_up,
              w_branch_mla, w_branch_fox, w_out, ln_mix_g, ln_mix_b, w_ffn_up, conv_w, conv_b, w_ffn_down, ln_ffn_g, ln_ffn_b)
    m_args = (m_meta_tokens, m_ln_emb_g, m_ln_emb_b, m_w_in, m_b_gate, m_b_forget, m_q_norm_g, m_w_q_up, m_kv_norm_g,
              m_w_kv_up, m_w_branch_mla, m_w_branch_fox, m_w_out, m_ln_mix_g, m_ln_mix_b, m_w_ffn_up, m_conv_w, m_conv_b,
              m_w_ffn_down, m_ln_ffn_g, m_ln_ffn_b)
    v_args = (v_meta_tokens, v_ln_emb_g, v_ln_emb_b, v_w_in, v_b_gate, v_b_forget, v_q_norm_g, v_w_q_up, v_kv_norm_g,
              v_w_kv_up, v_w_branch_mla, v_w_branch_fox, v_w_out, v_ln_mix_g, v_ln_mix_b, v_w_ffn_up, v_conv_w, v_conv_b,
              v_w_ffn_down, v_ln_ffn_g, v_ln_ffn_b)
    as2d = lambda a: a.reshape((-1, a.shape[-1])) if a.ndim != 1 else a.reshape(1, -1)
    w = {n: as2d(a) for n, a in zip(names, w_args)}
    m = {n: as2d(a) for n, a in zip(names, m_args)}
    v = {n: as2d(a) for n, a in zip(names, v_args)}
    out_shape = {n: a.shape for n, a in zip(names, w_args)}

    seq = x.shape[1]
    rows = -(-(N_META + seq) // ROW_ALIGN) * ROW_ALIGN
    axis_of = dict(BIG + SMALL_SHARDED)
    big = set(n for n, _ in BIG)
    wire = lambda n, a: a.astype(MXU_DTYPE) if n in big else a
    my_id = _my_id()
    slot_is_mine = lax.broadcasted_iota(jnp.int32, (N_DEV, 1, 1), 0) == my_id

    early = _allgather("gather_early", [wire(n, w[n]) for n in EARLY])
    p = {n: _gathered_full(g3, axis_of[n]) for n, g3 in zip(EARLY, early)}
    for n in REPLICATED:
        p[n] = w[n].reshape(-1)
    late_src = [wire(n, w[n]) for n in LATE]
    late_handle, late_token = _push_start("gather_late_start", late_src, False, after=early[0])
    sent = {}

    class Comm:
        first_token = (late_token,)

        def late_weights(self, after):
            lands = _push_wait("gather_late_wait", late_handle, after)
            return {n: _gathered_full(jnp.where(slot_is_mine, own[None], land), axis_of[n])
                    for n, own, land in zip(LATE, late_src, lands)}

        def send(self, name, grads):
            names_ = tuple(grads)
            parts = [_shard_major(grads[n], axis_of[n]).astype(MXU_DTYPE) for n in names_]
            handle, token = _push_start("send_" + name + "_start", parts, True)
            sent[name] = (names_, parts, handle)
            return (token,)

    zpad = jnp.zeros((rows - N_META - seq, D_MODEL), F32)
    xcat = jnp.concatenate([p["meta_tokens"], x[0], zpad], axis=0)
    tgt = jnp.concatenate([jnp.zeros((N_META, D_MODEL), F32), loss_target[0], zpad], axis=0)
    loss_part, dxcat, g = _local_step(xcat, tgt, seq, p, Comm())
    grad_x = dxcat[N_META:N_META + seq][None]
    g["meta_tokens"] = dxcat[:N_META]
    loss = lax.psum(loss_part[0, 0], ("x", "y", "c"))

    g_last = _exchange("exchange_last", [_shard_major(g[n], axis_of[n]).astype(MXU_DTYPE) for n in LAST])
    rep_all = _allgather("gather_small_grads", [_pack([g[n].reshape(-1) for n in REPLICATED])])[0]

    res = {}
    for n, parts in zip(LAST, g_last):
        res[n] = _adamw("adamw_" + n, parts, w[n], m[n], v[n])
    prev = rep_all
    for name, (names_, parts, handle) in sent.items():
        lands = _push_wait("send_" + name + "_wait", handle, prev)
        for n, part, land in zip(names_, parts, lands):
            own = lax.dynamic_index_in_dim(part, my_id, axis=0, keepdims=False)
            res[n] = _adamw("adamw_" + n, land, w[n], m[n], v[n], own=own)
            prev = res[n][0]
    rep_w = _pack([w[n].reshape(-1) for n in REPLICATED])
    rep_m = _pack([m[n].reshape(-1) for n in REPLICATED])
    rep_v = _pack([v[n].reshape(-1) for n in REPLICATED])
    rep_res = _adamw("adamw_replicated", rep_all, rep_w, rep_m, rep_v)
    off = 0
    for n in REPLICATED:
        sz = w[n].size
        res[n] = tuple(a.reshape(-1)[off:off + sz] for a in rep_res)
        off += sz

    outs = [loss, grad_x]
    for idx in range(4):
        outs += [res[n][idx].reshape(out_shape[n]) for n in names]
    return tuple(outs)
```

```python
import jax
import jax.numpy as jnp
from jax import lax
from jax.experimental import pallas as pl
from jax.experimental.pallas import tpu as pltpu

F32 = jnp.float32
BF16 = jnp.bfloat16
MXU_DTYPE = BF16

N_DEV = 8
N_META = 16
D_MODEL = 1024
HEADS = 8
Q_RANK = 384
KV_RANK = 128
NOPE = 64
ROPE = 32
HALF = ROPE // 2
QK_DIM = NOPE + ROPE
V_DIM = 64
FOX_DIM = 64
FOX_W = HEADS * FOX_DIM
D_FF = 2816
ROPE_THETA = 10000.0
LN_EPS = 1e-5
RMS_EPS = 1e-6
ALPHA = 2.0 ** 0.25
NEG_INF = -1e30

HP = 128
HW = HEADS * HP
F_W = 3 * HW
R_QLAT = 0
R_KVLAT = Q_RANK
R_LAST = R_KVLAT + KV_RANK
R_GATE = D_MODEL
R_W = R_GATE + 2 * D_MODEL
LANE_FL = 0
LANE_PE = NOPE

ADAM_LR = 0.001
ADAM_B1 = 0.9
ADAM_B2 = 0.999
ADAM_EPS = 1e-08
ADAM_WD = 0.01
ADAM_STEP = 10

ROW_BLOCK = 256
ATT_TQ = 768
ATT_TK = 256
ATT_HEADS = 2
ROW_ALIGN = 768
MM_BLOCK_CAP = 1408
VMEM_LIMIT = 56 * 1024 * 1024
HIGHEST = lax.Precision.HIGHEST
NT = (((1,), (1,)), ((), ()))
TN = (((0,), (0,)), ((), ()))


def _params(sem=None):
    return pltpu.CompilerParams(dimension_semantics=sem, vmem_limit_bytes=VMEM_LIMIT)


def _call(name, body, grid, ins, outs, scratch=(), sem=None, after=()):
    n_in = len(ins)
    n_tok = len(after)

    def run(*refs):
        body(*refs[:n_in], *refs[n_in + n_tok:])

    tok_spec = pl.BlockSpec((8, 128), lambda *_: (0, 0))
    return pl.pallas_call(
        run, name=name, grid=grid,
        in_specs=[s for _, s in ins] + [tok_spec] * n_tok,
        out_specs=[s for _, s in outs],
        out_shape=[o for o, _ in outs],
        scratch_shapes=list(scratch),
        compiler_params=_params(sem),
    )(*[a for a, _ in ins], *after)


def _sds(shape, dtype):
    return jax.ShapeDtypeStruct(shape, dtype)


def _rows(br, c, cb=0):
    return pl.BlockSpec((br, c), lambda i: (i, cb))


def _whole(shape):
    n = len(shape)
    return pl.BlockSpec(shape, lambda i: (0,) * n)


def _pick(dim, cap, mult):
    best = None
    d = mult
    while d <= min(dim, cap):
        if dim % d == 0:
            best = d
        d += mult
    return best if best is not None else dim


def _hs(h):
    return slice(h * HP, (h + 1) * HP)


def _matmul(name, a, b, *, ta=False, tb=False, out_dtype=F32, addend=None, alpha=1.0, after=()):
    if ta:
        k, m = a.shape
    else:
        m, k = a.shape
    if tb:
        n, k2 = b.shape
    else:
        k2, n = b.shape
    assert k == k2, (name, a.shape, b.shape)
    bm = _pick(m, MM_BLOCK_CAP, 128 if ta else 16)
    bn = _pick(n, MM_BLOCK_CAP, 128)
    bk = _pick(k, MM_BLOCK_CAP, 128 if (not ta or tb) else 16)
    nk = k // bk
    dims = (((0 if ta else 1,), (1 if tb else 0,)), ((), ()))
    has_add = addend is not None

    def body(*refs):
        a_ref, b_ref = refs[:2]
        add_ref = refs[2] if has_add else None
        o_ref = refs[3 if has_add else 2]

        def finish(r):
            if has_add:
                r = r + alpha * add_ref[...]
            o_ref[...] = r.astype(o_ref.dtype)

        part = lax.dot_general(a_ref[...], b_ref[...], dims, preferred_element_type=F32)
        if nk == 1:
            finish(part)
            return
        acc_ref = refs[-1]
        kk = pl.program_id(2)

        @pl.when(kk == 0)
        def _():
            acc_ref[...] = part

        @pl.when(kk > 0)
        def _():
            acc_ref[...] += part

        @pl.when(kk == nk - 1)
        def _():
            finish(acc_ref[...])

    a_spec = pl.BlockSpec((bk, bm), lambda i, j, l: (l, i)) if ta else pl.BlockSpec((bm, bk), lambda i, j, l: (i, l))
    b_spec = pl.BlockSpec((bn, bk), lambda i, j, l: (j, l)) if tb else pl.BlockSpec((bk, bn), lambda i, j, l: (l, j))
    o_spec = pl.BlockSpec((bm, bn), lambda i, j, l: (i, j))
    ins = [(a, a_spec), (b, b_spec)]
    if has_add:
        ins.append((addend, o_spec))
    return _call(name, body, (m // bm, n // bn, nk), ins, [(_sds((m, n), out_dtype), o_spec)],
                 scratch=[pltpu.VMEM((bm, bn), F32)] if nk > 1 else [],
                 sem=("parallel", "parallel", "arbitrary"), after=after)[0]


def _ln_stats(z):
    mu = jnp.mean(z, axis=-1, keepdims=True)
    zc = z - mu
    var = jnp.mean(zc * zc, axis=-1, keepdims=True)
    rstd = lax.rsqrt(var + LN_EPS)
    return zc * rstd, rstd


def _ln_fwd(name, a, res, g, b, after=()):
    r, d = a.shape
    br = ROW_BLOCK
    has_res = res is not None

    def body(*refs):
        if has_res:
            a_ref, r_ref, g_ref, b_ref, y_ref, yb_ref = refs
            z = ALPHA * a_ref[...] + r_ref[...]
        else:
            a_ref, g_ref, b_ref, y_ref, yb_ref = refs
            z = a_ref[...]
        xhat, _ = _ln_stats(z)
        y = xhat * g_ref[...] + b_ref[...]
        y_ref[...] = y
        yb_ref[...] = y.astype(yb_ref.dtype)

    ins = [(a, _rows(br, d))]
    if has_res:
        ins.append((res, _rows(br, d)))
    ins += [(g.reshape(1, d), _whole((1, d))), (b.reshape(1, d), _whole((1, d)))]
    outs = [(_sds((r, d), F32), _rows(br, d)), (_sds((r, d), MXU_DTYPE), _rows(br, d))]
    return _call(name, body, (r // br,), ins, outs, sem=("parallel",), after=after)


def _ln_bwd(name, a, res, dy, g, after=()):
    r, d = a.shape
    br = ROW_BLOCK
    has_res = res is not None

    def body(*refs):
        if has_res:
            a_ref, r_ref, dy_ref, g_ref, dz_ref, dzb_ref, dg_ref, db_ref = refs
            z = ALPHA * a_ref[...] + r_ref[...]
        else:
            a_ref, dy_ref, g_ref, dz_ref, dzb_ref, dg_ref, db_ref = refs
            z = a_ref[...]
        xhat, rstd = _ln_stats(z)
        dyv = dy_ref[...]
        dyg = dyv * g_ref[...]
        m1 = jnp.mean(dyg, axis=-1, keepdims=True)
        m2 = jnp.mean(dyg * xhat, axis=-1, keepdims=True)
        dz = rstd * (dyg - m1 - xhat * m2)
        dz_ref[...] = dz
        dzb_ref[...] = dz.astype(dzb_ref.dtype)

        @pl.when(pl.program_id(0) == 0)
        def _():
            dg_ref[...] = jnp.zeros_like(dg_ref)
            db_ref[...] = jnp.zeros_like(db_ref)

        dg_ref[...] += jnp.sum(dyv * xhat, axis=0, keepdims=True)
        db_ref[...] += jnp.sum(dyv, axis=0, keepdims=True)

    ins = [(a, _rows(br, d))]
    if has_res:
        ins.append((res, _rows(br, d)))
    ins += [(dy, _rows(br, d)), (g.reshape(1, d), _whole((1, d)))]
    outs = [(_sds((r, d), F32), _rows(br, d)), (_sds((r, d), MXU_DTYPE), _rows(br, d)),
            (_sds((1, d), F32), _whole((1, d))), (_sds((1, d), F32), _whole((1, d)))]
    return _call(name, body, (r // br,), ins, outs, sem=("arbitrary",), after=after)


def _rms_fwd(name, proj, cb, width, g):
    r = proj.shape[0]
    br = ROW_BLOCK

    def body(x_ref, g_ref, y_ref):
        x = x_ref[...]
        rstd = lax.rsqrt(jnp.mean(x * x, axis=-1, keepdims=True) + RMS_EPS)
        y_ref[...] = (x * rstd * g_ref[...]).astype(y_ref.dtype)

    return _call(name, body, (r // br,), [(proj, _rows(br, width, cb)), (g.reshape(1, width), _whole((1, width)))],
                 [(_sds((r, width), MXU_DTYPE), _rows(br, width))], sem=("parallel",))[0]


def _rms_bwd(name, proj, cb, width, dy, g):
    r = proj.shape[0]
    br = ROW_BLOCK

    def body(x_ref, dy_ref, g_ref, dx_ref, dg_ref):
        x = x_ref[...]
        rstd = lax.rsqrt(jnp.mean(x * x, axis=-1, keepdims=True) + RMS_EPS)
        nrm = x * rstd
        dyv = dy_ref[...]
        dyg = dyv * g_ref[...]
        dx = rstd * (dyg - nrm * jnp.mean(dyg * nrm, axis=-1, keepdims=True))
        dx_ref[...] = dx.astype(dx_ref.dtype)

        @pl.when(pl.program_id(0) == 0)
        def _():
            dg_ref[...] = jnp.zeros_like(dg_ref)

        dg_ref[...] += jnp.sum(dyv * nrm, axis=0, keepdims=True)

    return _call(name, body, (r // br,),
                 [(proj, _rows(br, width, cb)), (dy, _rows(br, width)), (g.reshape(1, width), _whole((1, width)))],
                 [(_sds((r, width), MXU_DTYPE), _rows(br, width)), (_sds((1, width), F32), _whole((1, width)))],
                 sem=("arbitrary",))


def _lane_iota(shape):
    return lax.broadcasted_iota(jnp.int32, shape, 1)


def _rotary(t, c, s, lane, sign):
    second = pltpu.roll(t, HP - HALF, axis=1)
    first = pltpu.roll(t, HALF, axis=1)
    lo = (lane >= LANE_PE) & (lane < LANE_PE + HALF)
    hi = (lane >= LANE_PE + HALF) & (lane < LANE_PE + ROPE)
    return jnp.where(lo, t * c - sign * second * s, jnp.where(hi, t * c + sign * first * s, t))


def _rope_fwd(q_raw, k_part, proj_r, cos_t, sin_t):
    r = q_raw.shape[0]
    br = ROW_BLOCK

    def body(q_ref, k_ref, t_ref, c_ref, s_ref, qo_ref, ko_ref):
        c = c_ref[...]
        s = s_ref[...]
        lane = _lane_iota((br, HP))
        pe = (lane >= LANE_PE) & (lane < LANE_PE + ROPE)
        kp = jnp.where(pe, _rotary(t_ref[...], c, s, lane, 1.0), 0.0)
        for h in range(HEADS):
            qo_ref[:, _hs(h)] = _rotary(q_ref[:, _hs(h)], c, s, lane, 1.0).astype(qo_ref.dtype)
            ko_ref[:, _hs(h)] = (k_ref[:, _hs(h)] + kp).astype(ko_ref.dtype)

    blk = _rows(br, HP)
    wide = _rows(br, HW)
    return _call("rope_fwd", body, (r // br,),
                 [(q_raw, wide), (k_part, wide), (proj_r, _rows(br, HP, R_LAST // HP)), (cos_t, blk), (sin_t, blk)],
                 [(_sds((r, HW), MXU_DTYPE), wide)] * 2, sem=("parallel",))


def _rope_bwd(dq, dk, dfl, cos_t, sin_t):
    r = dq.shape[0]
    br = ROW_BLOCK

    def body(dq_ref, dk_ref, fl_ref, c_ref, s_ref, dqo_ref, dko_ref, dl_ref):
        c = c_ref[...]
        s = s_ref[...]
        lane = _lane_iota((br, HP))
        pe = (lane >= LANE_PE) & (lane < LANE_PE + ROPE)
        acc = jnp.zeros((br, HP), F32)
        for h in range(HEADS):
            dqo_ref[:, _hs(h)] = _rotary(dq_ref[:, _hs(h)], c, s, lane, -1.0).astype(dqo_ref.dtype)
            dkh = dk_ref[:, _hs(h)]
            acc = acc + dkh
            dko_ref[:, _hs(h)] = dkh.astype(dko_ref.dtype)
        dl_ref[...] = (jnp.where(pe, _rotary(acc, c, s, lane, -1.0), 0.0) + fl_ref[...]).astype(dl_ref.dtype)

    blk = _rows(br, HP)
    wide = _rows(br, HW)
    return _call("rope_bwd", body, (r // br,),
                 [(dq, wide), (dk, wide), (dfl, blk), (cos_t, blk), (sin_t, blk)],
                 [(_sds((r, HW), MXU_DTYPE), wide), (_sds((r, HW), MXU_DTYPE), wide), (_sds((r, HP), MXU_DTYPE), blk)],
                 sem=("parallel",))


def _log_sigmoid(x):
    return jnp.minimum(x, 0.0) - jnp.log(1.0 + jnp.exp(-jnp.abs(x)))


def _head_lane(x, h, lane):
    return jnp.sum(jnp.where(lane == h, x, 0.0), axis=1, keepdims=True)


def _forget_fwd(proj_r, bf_row):
    r = proj_r.shape[0]
    br = ROW_BLOCK

    def body(t_ref, b_ref, ob_ref, ot_ref, carry_ref):
        @pl.when(pl.program_id(0) == 0)
        def _():
            carry_ref[...] = jnp.zeros_like(carry_ref)

        x = t_ref[...] + b_ref[...]
        lane = _lane_iota(x.shape)
        lf = jnp.where((lane >= LANE_FL) & (lane < LANE_FL + HEADS), _log_sigmoid(x), 0.0)
        tri = (lax.broadcasted_iota(jnp.int32, (br, br), 0) >= lax.broadcasted_iota(jnp.int32, (br, br), 1)).astype(F32)
        cum = jnp.dot(tri, lf, precision=HIGHEST, preferred_element_type=F32) + carry_ref[0:1, :]
        for h in range(HEADS):
            ob_ref[:, _hs(h)] = jnp.broadcast_to(_head_lane(cum, LANE_FL + h, lane), (br, HP))
        ot_ref[...] = cum.T[LANE_FL:LANE_FL + HEADS, :]
        carry_ref[...] = jnp.broadcast_to(cum[br - 1:br, :], carry_ref.shape)

    return _call("forget_fwd", body, (r // br,),
                 [(proj_r, _rows(br, HP, R_LAST // HP)), (bf_row, _whole((1, HP)))],
                 [(_sds((r, HW), F32), _rows(br, HW)), (_sds((HEADS, r), F32), pl.BlockSpec((HEADS, br), lambda i: (0, i)))],
                 scratch=[pltpu.VMEM((8, HP), F32)], sem=("arbitrary",))


def _forget_bwd(proj_r, bf_row, dcq_t, dck_b):
    r = proj_r.shape[0]
    br = ROW_BLOCK
    nb = r // br

    def body(t_ref, b_ref, dcq_ref, dck_ref, o_ref, db_ref, carry_ref):
        @pl.when(pl.program_id(0) == 0)
        def _():
            carry_ref[...] = jnp.zeros_like(carry_ref)
            db_ref[...] = jnp.zeros_like(db_ref)

        lane = _lane_iota((br, HP))
        dc = jnp.concatenate([dcq_ref[...], jnp.zeros((HP - HEADS, br), F32)], axis=0).T
        for h in range(HEADS):
            dc = dc + jnp.where(lane == LANE_FL + h, dck_ref[:, h * HP:h * HP + 1], 0.0)
        triu = (lax.broadcasted_iota(jnp.int32, (br, br), 0) <= lax.broadcasted_iota(jnp.int32, (br, br), 1)).astype(F32)
        dlf = jnp.dot(triu, dc, precision=HIGHEST, preferred_element_type=F32) + carry_ref[0:1, :]
        carry_ref[...] = jnp.broadcast_to(dlf[0:1, :], carry_ref.shape)
        x = t_ref[...] + b_ref[...]
        dfl = jnp.where((lane >= LANE_FL) & (lane < LANE_FL + HEADS), dlf * jax.nn.sigmoid(-x), 0.0)
        o_ref[...] = dfl
        db_ref[...] += jnp.sum(dfl, axis=0, keepdims=True)

    rev = pl.BlockSpec((br, HP), lambda i: (nb - 1 - i, 0))
    return _call("forget_bwd", body, (nb,),
                 [(proj_r, pl.BlockSpec((br, HP), lambda i: (nb - 1 - i, R_LAST // HP))), (bf_row, _whole((1, HP))),
                  (dcq_t, pl.BlockSpec((HEADS, br), lambda i: (0, nb - 1 - i))),
                  (dck_b, pl.BlockSpec((br, HW), lambda i: (nb - 1 - i, 0)))],
                 [(_sds((r, HP), F32), rev), (_sds((1, HP), F32), _whole((1, HP)))],
                 scratch=[pltpu.VMEM((8, HP), F32)], sem=("arbitrary",))


def _gate_fwd(proj_r, b_gate, bm, bfx):
    r, d = bm.shape
    br = ROW_BLOCK
    cb = R_GATE // d

    def body(gm_ref, gf_ref, b1_ref, b2_ref, bm_ref, bf_ref, o_ref):
        g1 = jax.nn.sigmoid(gm_ref[...] + b1_ref[...])
        g2 = jax.nn.sigmoid(gf_ref[...] + b2_ref[...])
        o_ref[...] = (g1 * bm_ref[...] + g2 * bf_ref[...]).astype(o_ref.dtype)

    b1 = b_gate[:d].reshape(1, d)
    b2 = b_gate[d:].reshape(1, d)
    return _call("gate_fwd", body, (r // br,),
                 [(proj_r, _rows(br, d, cb)), (proj_r, _rows(br, d, cb + 1)), (b1, _whole((1, d))), (b2, _whole((1, d))),
                  (bm, _rows(br, d)), (bfx, _rows(br, d))],
                 [(_sds((r, d), MXU_DTYPE), _rows(br, d))], sem=("parallel",))[0]


def _gate_bwd(proj_r, b_gate, bm, bfx, dmerged):
    r, d = bm.shape
    br = ROW_BLOCK
    cb = R_GATE // d

    def body(gm_ref, gf_ref, b1_ref, b2_ref, bm_ref, bf_ref, dm_ref, dbm_ref, dbf_ref, dgl_ref, dbg_ref):
        g1 = jax.nn.sigmoid(gm_ref[...] + b1_ref[...])
        g2 = jax.nn.sigmoid(gf_ref[...] + b2_ref[...])
        dm = dm_ref[...]
        dbm_ref[...] = (dm * g1).astype(dbm_ref.dtype)
        dbf_ref[...] = (dm * g2).astype(dbf_ref.dtype)
        dl1 = dm * bm_ref[...] * (g1 * (1.0 - g1))
        dl2 = dm * bf_ref[...] * (g2 * (1.0 - g2))
        dgl_ref[:, 0:d] = dl1.astype(dgl_ref.dtype)
        dgl_ref[:, d:2 * d] = dl2.astype(dgl_ref.dtype)

        @pl.when(pl.program_id(0) == 0)
        def _():
            dbg_ref[...] = jnp.zeros_like(dbg_ref)

        dbg_ref[:, 0:d] += jnp.sum(dl1, axis=0, keepdims=True)
        dbg_ref[:, d:2 * d] += jnp.sum(dl2, axis=0, keepdims=True)

    b1 = b_gate[:d].reshape(1, d)
    b2 = b_gate[d:].reshape(1, d)
    return _call("gate_bwd", body, (r // br,),
                 [(proj_r, _rows(br, d, cb)), (proj_r, _rows(br, d, cb + 1)), (b1, _whole((1, d))), (b2, _whole((1, d))),
                  (bm, _rows(br, d)), (bfx, _rows(br, d)), (dmerged, _rows(br, d))],
                 [(_sds((r, d), MXU_DTYPE), _rows(br, d)), (_sds((r, d), MXU_DTYPE), _rows(br, d)),
                  (_sds((r, 2 * d), MXU_DTYPE), _rows(br, 2 * d)), (_sds((1, 2 * d), F32), _whole((1, 2 * d)))],
                 sem=("arbitrary",))


def _conv_taps(gp, halo, first_block):
    halo = jnp.where(first_block, 0.0, halo)
    rid = lax.broadcasted_iota(jnp.int32, gp.shape, 0)
    g1 = jnp.where(rid == 0, halo[7:8, :], pltpu.roll(gp, 1, axis=0))
    g2 = jnp.where(rid == 0, halo[6:7, :], jnp.where(rid == 1, halo[7:8, :], pltpu.roll(gp, 2, axis=0)))
    return g1, g2


def _prev_halo(br, c):
    return pl.BlockSpec((8, c), lambda i: (jnp.maximum(i * (br // 8) - 1, 0), 0))


def _glu_fwd(up, conv_w, conv_b):
    r = up.shape[0]
    c = D_FF
    br = ROW_BLOCK

    def body(gp_ref, halo_ref, val_ref, w_ref, b_ref, o_ref):
        gp = gp_ref[...]
        g1, g2 = _conv_taps(gp, halo_ref[...], pl.program_id(0) == 0)
        gate = w_ref[0:1, :] * g2 + w_ref[1:2, :] * g1 + w_ref[2:3, :] * gp + b_ref[...]
        o_ref[...] = (gate * jax.nn.sigmoid(gate) * val_ref[...]).astype(o_ref.dtype)

    return _call("glu_fwd", body, (r // br,),
                 [(up, _rows(br, c, 0)), (up, _prev_halo(br, c)), (up, _rows(br, c, 1)),
                  (conv_w, _whole((3, c))), (conv_b.reshape(1, c), _whole((1, c)))],
                 [(_sds((r, c), MXU_DTYPE), _rows(br, c))], sem=("parallel",))[0]


def _glu_bwd_gate(up, conv_w, conv_b, d_act):
    r = up.shape[0]
    c = D_FF
    br = ROW_BLOCK

    def body(gp_ref, halo_ref, val_ref, w_ref, b_ref, da_ref, dg_ref, dv_ref, dw_ref, db_ref):
        gp = gp_ref[...]
        g1, g2 = _conv_taps(gp, halo_ref[...], pl.program_id(0) == 0)
        gate = w_ref[0:1, :] * g2 + w_ref[1:2, :] * g1 + w_ref[2:3, :] * gp + b_ref[...]
        sg = jax.nn.sigmoid(gate)
        da = da_ref[...]
        dv_ref[...] = (da * (gate * sg)).astype(dv_ref.dtype)
        dg = da * val_ref[...] * (sg * (1.0 + gate * (1.0 - sg)))
        dg_ref[...] = dg

        @pl.when(pl.program_id(0) == 0)
        def _():
            dw_ref[...] = jnp.zeros_like(dw_ref)
            db_ref[...] = jnp.zeros_like(db_ref)

        dw_ref[0:1, :] += jnp.sum(dg * g2, axis=0, keepdims=True)
        dw_ref[1:2, :] += jnp.sum(dg * g1, axis=0, keepdims=True)
        dw_ref[2:3, :] += jnp.sum(dg * gp, axis=0, keepdims=True)
        db_ref[...] += jnp.sum(dg, axis=0, keepdims=True)

    return _call("glu_bwd_gate", body, (r // br,),
                 [(up, _rows(br, c, 0)), (up, _prev_halo(br, c)), (up, _rows(br, c, 1)),
                  (conv_w, _whole((3, c))), (conv_b.reshape(1, c), _whole((1, c))), (d_act, _rows(br, c))],
                 [(_sds((r, c), F32), _rows(br, c)), (_sds((r, c), MXU_DTYPE), _rows(br, c)),
                  (_sds((8, c), F32), _whole((8, c))), (_sds((1, c), F32), _whole((1, c)))],
                 sem=("arbitrary",))


def _glu_bwd_conv(dg, dval, conv_w):
    r, c = dg.shape
    br = ROW_BLOCK
    nb = r // br

    def body(dg_ref, nxt_ref, dv_ref, w_ref, o_ref):
        x = dg_ref[...]
        nxt = jnp.where(pl.program_id(0) == nb - 1, 0.0, nxt_ref[...])
        rid = lax.broadcasted_iota(jnp.int32, x.shape, 0)
        u1 = jnp.where(rid == br - 1, nxt[0:1, :], pltpu.roll(x, br - 1, axis=0))
        u2 = jnp.where(rid == br - 1, nxt[1:2, :], jnp.where(rid == br - 2, nxt[0:1, :], pltpu.roll(x, br - 2, axis=0)))
        dgp = w_ref[2:3, :] * x + w_ref[1:2, :] * u1 + w_ref[0:1, :] * u2
        o_ref[:, 0:c] = dgp.astype(o_ref.dtype)
        o_ref[:, c:2 * c] = dv_ref[...]

    nxt_spec = pl.BlockSpec((8, c), lambda i: (jnp.minimum((i + 1) * (br // 8), r // 8 - 1), 0))
    return _call("glu_bwd_conv", body, (nb,),
                 [(dg, _rows(br, c)), (dg, nxt_spec), (dval, _rows(br, c)), (conv_w, _whole((3, c)))],
                 [(_sds((r, 2 * c), MXU_DTYPE), _rows(br, 2 * c))], sem=("parallel",))[0]


def _loss(h2, tgt, seq):
    r, d = h2.shape
    br = ROW_BLOCK

    def body(h_ref, t_ref, l_ref, d_ref):
        rid = lax.broadcasted_iota(jnp.int32, (br, d), 0) + pl.program_id(0) * br
        valid = (rid >= N_META) & (rid < N_META + seq)
        err = jnp.where(valid, h_ref[...] - t_ref[...], 0.0)
        d_ref[...] = err * (1.0 / d)

        @pl.when(pl.program_id(0) == 0)
        def _():
            l_ref[...] = jnp.zeros_like(l_ref)

        l_ref[...] += jnp.sum(jnp.sum(err * err, axis=1, keepdims=True), axis=0, keepdims=True) * (0.5 / d)

    return _call("loss", body, (r // br,), [(h2, _rows(br, d)), (tgt, _rows(br, d))],
                 [(_sds((1, 1), F32), _whole((1, 1))), (_sds((r, d), F32), _rows(br, d))], sem=("arbitrary",))


def _attn_fwd(name, q, k, v, scale, cum_b=None, cum_t=None):
    (qa, qg), (ka, kg), (va, vg) = q, k, v
    r = qa.shape[0]
    tq, tk = ATT_TQ, ATT_TK
    nq, nk = r // tq, r // tk
    bias = cum_b is not None

    def body(*refs):
        if bias:
            q_ref, k_ref, vt_ref, cb_ref, ct_ref, o_ref, ob_ref, lse_ref = refs
        else:
            q_ref, k_ref, vt_ref, o_ref, ob_ref, lse_ref = refs
        i = pl.program_id(1)
        qs = [q_ref[:, _hs(hh)] for hh in range(hg)]
        cqs = [ct_ref[hh] for hh in range(hg)] if bias else None
        diff = lax.broadcasted_iota(jnp.int32, (tk, tq), 0) - lax.broadcasted_iota(jnp.int32, (tk, tq), 1)

        def step(j, carry, masked):
            keys = pl.ds(pl.multiple_of(j * tk, tk), tk)
            out = []
            for hh in range(hg):
                m, l, acc = carry[hh]
                kt = k_ref[keys, _hs(hh)]
                s = lax.dot_general(kt, qs[hh], NT, preferred_element_type=F32) * scale
                if bias:
                    s = s + (cqs[hh] - cb_ref[keys, hh * HP:hh * HP + 1])
                if masked:
                    s = jnp.where(diff <= i * tq - j * tk, s, NEG_INF)
                m_new = jnp.maximum(m, jnp.max(s, axis=0, keepdims=True))
                p = jnp.exp(s - m_new)
                a = jnp.exp(m - m_new)
                l = a * l + jnp.sum(p, axis=0, keepdims=True)
                acc = a * acc + jnp.dot(vt_ref[j, _hs(hh), :], p.astype(kt.dtype), preferred_element_type=F32)
                out.append((m_new, l, acc))
            return tuple(out)

        n_clear = (i * tq + 1) // tk
        n_all = ((i + 1) * tq - 1) // tk + 1
        carry = tuple((jnp.full((1, tq), NEG_INF, F32), jnp.zeros((1, tq), F32), jnp.zeros((HP, tq), F32))
                      for _ in range(hg))
        carry = lax.fori_loop(0, n_clear, lambda j, c: step(j, c, False), carry)
        carry = lax.fori_loop(n_clear, n_all, lambda j, c: step(j, c, True), carry)
        for hh in range(hg):
            m, l, acc = carry[hh]
            o = (acc / l).T
            o_ref[:, _hs(hh)] = o
            ob_ref[:, _hs(hh)] = o.astype(ob_ref.dtype)
            lse_ref[hh] = m + jnp.log(l)

    hg = ATT_HEADS
    w = hg * HP
    gpw = HW // w
    tile = lambda g: pl.BlockSpec((tq, w), lambda h, i: (i, g * gpw + h))
    res = lambda g: pl.BlockSpec((r, w), lambda h, i: (0, g * gpw + h))
    v_t = _key_tiles_transposed(name + "_vt", va, vg)
    ins = [(qa, tile(qg)), (ka, res(kg)), (v_t, pl.BlockSpec((nk, w, tk), lambda h, i: (0, h, 0)))]
    if bias:
        ins += [(cum_b, res(0)),
                (cum_t.reshape(HEADS, nq, 1, tq), pl.BlockSpec((hg, None, 1, tq), lambda h, i: (h, i, 0, 0)))]
    outs = [(_sds((r, HW), F32), tile(0)), (_sds((r, HW), MXU_DTYPE), tile(0)),
            (_sds((HEADS, nq, 1, tq), F32), pl.BlockSpec((hg, None, 1, tq), lambda h, i: (h, i, 0, 0)))]
    o, ob, lse = _call(name, body, (gpw, nq), ins, outs, sem=("parallel", "parallel"))
    return o, ob, lse.reshape(HEADS, r)


def _key_tiles_transposed(name, a, group):
    r = a.shape[0]
    tk = ATT_TK

    def body(x_ref, o_ref):
        for h in range(HEADS):
            o_ref[_hs(h), :] = x_ref[:, _hs(h)].astype(F32).T.astype(o_ref.dtype)

    return _call(name, body, (r // tk,),
                 [(a, pl.BlockSpec((tk, HW), lambda j: (j, group)))],
                 [(_sds((r // tk, HW, tk), a.dtype), pl.BlockSpec((None, HW, tk), lambda j: (j, 0, 0)))],
                 sem=("parallel",))[0]


def _attn_delta(name, do, o, after=()):
    r = do.shape[0]
    br = ROW_BLOCK

    def body(do_ref, o_ref, d_ref, dob_ref):
        lane = _lane_iota((br, HP))
        d = jnp.zeros((br, HP), F32)
        for h in range(HEADS):
            dh = do_ref[:, _hs(h)]
            d = jnp.where(lane == h, jnp.sum(dh * o_ref[:, _hs(h)], axis=1, keepdims=True), d)
            dob_ref[:, _hs(h)] = dh.astype(dob_ref.dtype)
        d_ref[...] = d.T[0:HEADS, :]

    wide = _rows(br, HW)
    return _call(name, body, (r // br,), [(do, wide), (o, wide)],
                 [(_sds((HEADS, r), F32), pl.BlockSpec((HEADS, br), lambda i: (0, i))), (_sds((r, HW), MXU_DTYPE), wide)],
                 sem=("parallel",), after=after)


def _attn_bwd(name, q, k, v, do_b, lse_t, delta_t, scale, cum_b=None, cum_t=None):
    (qa, qg), (ka, kg), (va, vg) = q, k, v
    r = qa.shape[0]
    tq, tk = ATT_TQ, ATT_TK
    nq, nk = r // tq, r // tk
    bias = cum_b is not None

    def body(*refs):
        if bias:
            (q_ref, k_ref, v_ref, do_ref, lse_ref, dl_ref, cb_ref, ct_ref,
             dq_ref, dk_ref, dv_ref, dcq_ref, dck_ref, dqt_ref) = refs
        else:
            q_ref, k_ref, v_ref, do_ref, lse_ref, dl_ref, dq_ref, dk_ref, dv_ref, dqt_ref = refs
        j = pl.program_id(1)

        @pl.when(j == 0)
        def _():
            dqt_ref[...] = jnp.zeros_like(dqt_ref)
            if bias:
                dcq_ref[...] = jnp.zeros_like(dcq_ref)

        kts = [k_ref[:, _hs(hh)] for hh in range(hg)]
        vts = [v_ref[:, _hs(hh)] for hh in range(hg)]
        k_trs = [kt.astype(F32).T.astype(kt.dtype) for kt in kts]
        cks = [cb_ref[:, hh * HP:hh * HP + 1] for hh in range(hg)] if bias else None
        diff = lax.broadcasted_iota(jnp.int32, (tk, tq), 0) - lax.broadcasted_iota(jnp.int32, (tk, tq), 1)

        def step(i, carry, masked):
            rows = pl.ds(pl.multiple_of(i * tq, tq), tq)
            out = []
            for hh in range(hg):
                dk_acc, dv_acc, dck_acc = carry[hh]
                qt = q_ref[rows, _hs(hh)]
                dot = do_ref[rows, _hs(hh)]
                s = lax.dot_general(kts[hh], qt, NT, preferred_element_type=F32) * scale
                if bias:
                    s = s + (ct_ref[hh, i] - cks[hh])
                if masked:
                    s = jnp.where(diff <= i * tq - j * tk, s, NEG_INF)
                p = jnp.exp(s - lse_ref[hh, i])
                dp = lax.dot_general(vts[hh], dot, NT, preferred_element_type=F32)
                ds = p * (dp - dl_ref[hh, i])
                pb = p.astype(dot.dtype)
                dsb = ds.astype(qt.dtype)
                dv_acc = dv_acc + jnp.dot(pb, dot, preferred_element_type=F32)
                dk_acc = dk_acc + jnp.dot(dsb, qt, preferred_element_type=F32)
                dqt_ref[hh, i] += jnp.dot(k_trs[hh], dsb, preferred_element_type=F32)
                if bias:
                    dcq_ref[hh, i] += jnp.sum(ds, axis=0, keepdims=True)
                    dck_acc = dck_acc - jnp.sum(ds, axis=1, keepdims=True)
                out.append((dk_acc, dv_acc, dck_acc))
            return tuple(out)

        i_first = (j * tk) // tq
        i_clear = jnp.minimum(((j + 1) * tk + tq - 2) // tq, nq)
        carry = tuple((jnp.zeros((tk, HP), F32), jnp.zeros((tk, HP), F32), jnp.zeros((tk, 1), F32)) for _ in range(hg))
        carry = lax.fori_loop(i_first, i_clear, lambda i, c: step(i, c, True), carry)
        carry = lax.fori_loop(i_clear, nq, lambda i, c: step(i, c, False), carry)
        for hh in range(hg):
            dk_acc, dv_acc, dck_acc = carry[hh]
            dk_ref[:, _hs(hh)] = dk_acc * scale
            dv_ref[:, _hs(hh)] = dv_acc
            if bias:
                dck_ref[:, _hs(hh)] = jnp.broadcast_to(dck_acc, (tk, HP))

        @pl.when(j == nk - 1)
        def _():
            for hh in range(hg):
                for i in range(nq):
                    dq_ref[i * tq:(i + 1) * tq, _hs(hh)] = dqt_ref[hh, i].T * scale

    hg = ATT_HEADS
    w = hg * HP
    gpw = HW // w
    res = lambda g: pl.BlockSpec((r, w), lambda h, j: (0, g * gpw + h))
    tile = lambda g: pl.BlockSpec((tk, w), lambda h, j: (j, g * gpw + h))
    rowv = pl.BlockSpec((hg, nq, 1, tq), lambda h, j: (h, 0, 0, 0))
    as_rows = lambda a: a.reshape(HEADS, nq, 1, tq)
    ins = [(qa, res(qg)), (ka, tile(kg)), (va, tile(vg)), (do_b, res(0)), (as_rows(lse_t), rowv), (as_rows(delta_t), rowv)]
    outs = [(_sds((r, HW), F32), res(0)), (_sds((r, HW), F32), tile(0)), (_sds((r, HW), F32), tile(0))]
    if bias:
        ins += [(cum_b, tile(0)), (as_rows(cum_t), rowv)]
        outs += [(_sds((HEADS, nq, 1, tq), F32), rowv), (_sds((r, HW), F32), tile(0))]
    res_out = _call(name, body, (gpw, nk), ins, outs, scratch=[pltpu.VMEM((hg, nq, HP, tq), F32)],
                    sem=("parallel", "arbitrary"))
    if bias:
        dq, dk, dv, dcq, dck = res_out
        return dq, dk, dv, dcq.reshape(HEADS, r), dck
    return res_out


MESH_ID = pl.DeviceIdType.MESH
ANY = pl.BlockSpec(memory_space=pl.ANY)


def _allgather(name, shards):
    n = len(shards)

    def body(*refs):
        x_refs, out_refs = refs[:n], refs[n:2 * n]
        send_sems, recv_sems, local_sems = refs[2 * n:]
        x, y, c = lax.axis_index("x"), lax.axis_index("y"), lax.axis_index("c")
        me, sibling = (x, y, c), (x, y, 1 - c)
        chips = [(1 - x, y), (x, 1 - y), (1 - x, 1 - y)]

        def slot(ti, px, py, pc):
            return out_refs[ti].at[4 * px + 2 * py + pc]

        def copy(ti, k, block, to, src=None):
            return pltpu.make_async_remote_copy(
                src_ref=slot(ti, *block) if src is None else src, dst_ref=slot(ti, *block),
                send_sem=send_sems.at[ti, k], recv_sem=recv_sems.at[ti, k], device_id=to, device_id_type=MESH_ID)

        mine = [pltpu.make_async_copy(x_refs[ti], slot(ti, *me), local_sems.at[ti]) for ti in range(n)]
        for cp in mine:
            cp.start()
        started = []
        for ti in range(n):
            first = [copy(ti, 0, me, sibling, src=x_refs[ti])]
            first += [copy(ti, 1 + j, me, (*chip, c), src=x_refs[ti]) for j, chip in enumerate(chips)]
            for cp in first:
                cp.start()
            started += first
        for ti in range(n):
            for j, chip in enumerate(chips):
                copy(ti, 1 + j, (*chip, c), me).wait_recv()
                fwd = copy(ti, 4 + j, (*chip, c), sibling)
                fwd.start()
                started.append(fwd)
        for ti in range(n):
            copy(ti, 0, sibling, me).wait_recv()
            for j, chip in enumerate(chips):
                copy(ti, 4 + j, (*chip, 1 - c), me).wait_recv()
        for cp in started:
            cp.wait_send()
        for cp in mine:
            cp.wait()

    return pl.pallas_call(
        body, name=name, out_shape=[_sds((N_DEV,) + s.shape, s.dtype) for s in shards],
        in_specs=[ANY] * n, out_specs=[ANY] * n,
        scratch_shapes=[pltpu.SemaphoreType.DMA((n, 7)), pltpu.SemaphoreType.DMA((n, 7)), pltpu.SemaphoreType.DMA((n,))],
    )(*shards)


def _exchange(name, parts):
    n = len(parts)

    def body(*refs):
        g_refs, out_refs = refs[:n], refs[n:2 * n]
        send_sems, recv_sems, local_sems = refs[2 * n:]
        x, y, c = lax.axis_index("x"), lax.axis_index("y"), lax.axis_index("c")
        my_id = 4 * x + 2 * y + c

        def peer(k):
            return (1 - x if k & 4 else x, 1 - y if k & 2 else y, 1 - c if k & 1 else c)

        def copy(ti, k, landing):
            px, py, pc = peer(k)
            pid = 4 * px + 2 * py + pc
            return pltpu.make_async_remote_copy(
                src_ref=g_refs[ti].at[pid], dst_ref=out_refs[ti].at[pid if landing else my_id],
                send_sem=send_sems.at[ti, k - 1], recv_sem=recv_sems.at[ti, k - 1],
                device_id=(px, py, pc), device_id_type=MESH_ID)

        mine = [pltpu.make_async_copy(g_refs[ti].at[my_id], out_refs[ti].at[my_id], local_sems.at[ti]) for ti in range(n)]
        for cp in mine:
            cp.start()
        sends = [copy(ti, k, False) for ti in range(n) for k in range(1, N_DEV)]
        for cp in sends:
            cp.start()
        for ti in range(n):
            for k in range(1, N_DEV):
                copy(ti, k, True).wait_recv()
        for cp in sends:
            cp.wait_send()
        for cp in mine:
            cp.wait()

    return pl.pallas_call(
        body, name=name, out_shape=[_sds(p.shape, p.dtype) for p in parts],
        in_specs=[ANY] * n, out_specs=[ANY] * n,
        scratch_shapes=[pltpu.SemaphoreType.DMA((n, 7)), pltpu.SemaphoreType.DMA((n, 7)), pltpu.SemaphoreType.DMA((n,))],
    )(*parts)


HBM = pl.BlockSpec(memory_space=pltpu.HBM)
SEM = pl.BlockSpec(memory_space=pltpu.SEMAPHORE)
EFFECT = pltpu.SideEffectType.DATAFLOW_SIDE_EFFECTING
N_PEER = N_DEV - 1


def _my_id():
    return 4 * lax.axis_index("x") + 2 * lax.axis_index("y") + lax.axis_index("c")


def _peers():
    x, y, c = lax.axis_index("x"), lax.axis_index("y"), lax.axis_index("c")
    out = []
    for k in range(1, N_DEV):
        px, py, pc = (1 - x if k & 4 else x, 1 - y if k & 2 else y, 1 - c if k & 1 else c)
        out.append(((px, py, pc), 4 * px + 2 * py + pc))
    return out


def _push_copies(src_refs, land_refs, send_sems, recv_sems, scatter, landing):
    me = _my_id()
    out = []
    for ti, (src, land) in enumerate(zip(src_refs, land_refs)):
        for k, (dev, pid) in enumerate(_peers()):
            out.append(pltpu.make_async_remote_copy(
                src_ref=src.at[pid] if scatter else src, dst_ref=land.at[pid if landing else me],
                send_sem=send_sems.at[ti * N_PEER + k], recv_sem=recv_sems.at[ti * N_PEER + k],
                device_id=dev, device_id_type=MESH_ID))
    return out


def _push_start(name, srcs, scatter, after=None):
    n = len(srcs)
    slot = lambda s: s.shape[1:] if scatter else s.shape
    lands = [lax.empty((N_DEV,) + slot(s), s.dtype) for s in srcs]
    n_after = 0 if after is None else 1

    def body(*refs):
        src_refs, land_refs = refs[:n], refs[n:2 * n]
        send_sems, recv_sems = refs[2 * n + n_after], refs[2 * n + n_after + 1]
        token = refs[-1]
        for cp in _push_copies(src_refs, land_refs, send_sems, recv_sems, scatter, False):
            cp.start()
        token[...] = jnp.zeros_like(token)

    hbm = lambda a: pltpu.with_memory_space_constraint(a, pltpu.HBM)
    operands = [hbm(a) for a in srcs + lands] + ([after] if n_after else [])
    res = pl.pallas_call(
        body, name=name,
        out_shape=[pltpu.SemaphoreType.DMA((n * N_PEER,)), pltpu.SemaphoreType.DMA((n * N_PEER,))]
        + [pltpu.HBM(a.shape, a.dtype) for a in srcs + lands] + [_sds((8, 128), F32)],
        in_specs=[HBM] * (2 * n) + [ANY] * n_after,
        out_specs=[SEM, SEM] + [HBM] * (2 * n) + [pl.BlockSpec(memory_space=pltpu.VMEM)],
        input_output_aliases={i: 2 + i for i in range(2 * n)},
        compiler_params=pltpu.CompilerParams(has_side_effects=EFFECT),
    )(*operands)
    return (res[0], res[1], list(res[2:2 + n]), list(res[2 + n:2 + 2 * n]), scatter), res[-1]


def _push_wait(name, handle, after):
    send_sems, recv_sems, srcs, lands, scatter = handle
    n = len(srcs)

    def body(*refs):
        src_refs, land_refs = refs[:n], refs[n:2 * n]
        s_sems, r_sems = refs[2 * n], refs[2 * n + 1]
        for cp in _push_copies(src_refs, land_refs, s_sems, r_sems, scatter, True):
            cp.wait_send()
            cp.wait_recv()

    res = pl.pallas_call(
        body, name=name,
        out_shape=[pltpu.HBM(a.shape, a.dtype) for a in srcs + lands],
        in_specs=[HBM] * (2 * n) + [SEM, SEM, ANY], out_specs=[HBM] * (2 * n),
        input_output_aliases={i: i for i in range(2 * n)},
        compiler_params=pltpu.CompilerParams(has_side_effects=EFFECT),
    )(*srcs, *lands, send_sems, recv_sems, after)
    return list(res[n:])


def _adamw(name, parts, w, m, v, own=None):
    r, c = w.shape
    br = _pick(r, 256, 16)
    has_own = own is not None

    def body(*refs):
        if has_own:
            p_ref, own_ref, w_ref, m_ref, v_ref, g_ref, d_ref, nm_ref, nv_ref = refs
            me = _my_id()
            mine = own_ref[...].astype(F32)
        else:
            p_ref, w_ref, m_ref, v_ref, g_ref, d_ref, nm_ref, nv_ref = refs
        g = None
        for k in range(N_DEV):
            t = p_ref[k].astype(F32)
            if has_own:
                t = jnp.where(me == k, mine, t)
            g = t if g is None else g + t
        mm = ADAM_B1 * m_ref[...] + (1.0 - ADAM_B1) * g
        vv = ADAM_B2 * v_ref[...] + (1.0 - ADAM_B2) * (g * g)
        m_hat = mm / (1.0 - ADAM_B1 ** ADAM_STEP)
        v_hat = vv / (1.0 - ADAM_B2 ** ADAM_STEP)
        g_ref[...] = g
        d_ref[...] = -ADAM_LR * (m_hat / (jnp.sqrt(v_hat) + ADAM_EPS) + ADAM_WD * w_ref[...])
        nm_ref[...] = mm
        nv_ref[...] = vv

    spec = _rows(br, c)
    out = (_sds((r, c), F32), spec)
    ins = [(parts, pl.BlockSpec((N_DEV, br, c), lambda i: (0, i, 0)))] + ([(own, spec)] if has_own else [])
    return _call(name, body, (r // br,), ins + [(w, spec), (m, spec), (v, spec)], [out] * 4, sem=("parallel",))


def _pad_head_cols(w, d):
    k = w.shape[0]
    return jnp.pad(w.reshape(k, HEADS, d), ((0, 0), (0, 0), (0, HP - d))).reshape(k, HW)


def _unpad_head_cols(wp, d):
    k = wp.shape[0]
    return wp.reshape(k, HEADS, HP)[:, :, :d].reshape(k, HEADS * d)


def _pad_head_rows(w, d):
    n = w.shape[1]
    return jnp.pad(w.reshape(HEADS, d, n), ((0, 0), (0, HP - d), (0, 0))).reshape(HW, n)


def _unpad_head_rows(wp, d):
    n = wp.shape[1]
    return wp.reshape(HEADS, HP, n)[:, :d, :].reshape(HEADS * d, n)


IN_SEGS = (("q", Q_RANK), ("kv", KV_RANK), ("kr", ROPE), ("fq", FOX_W), ("fk", FOX_W), ("fv", FOX_W),
           ("fl", HEADS), ("gate", 2 * D_MODEL))


def _split_w_in(w):
    seg = {}
    o = 0
    for nm, wd in IN_SEGS:
        seg[nm] = w[:, o:o + wd]
        o += wd
    d = w.shape[0]
    z = lambda n: jnp.zeros((d, n), w.dtype)
    fused = jnp.concatenate([_pad_head_cols(seg[nm], FOX_DIM) for nm in ("fq", "fk", "fv")], axis=1)
    last = jnp.concatenate([seg["fl"], z(LANE_PE - HEADS), seg["kr"], z(HP - LANE_PE - ROPE)], axis=1)
    rest = jnp.concatenate([seg["q"], seg["kv"], last, z(R_GATE - R_LAST - HP), seg["gate"]], axis=1)
    return fused, rest


def _merge_w_in(fused, rest):
    f = [_unpad_head_cols(fused[:, i * HW:(i + 1) * HW], FOX_DIM) for i in range(3)]
    last = rest[:, R_LAST:R_LAST + HP]
    return jnp.concatenate([rest[:, R_QLAT:R_LAST], last[:, LANE_PE:LANE_PE + ROPE], f[0], f[1], f[2],
                            last[:, LANE_FL:LANE_FL + HEADS], rest[:, R_GATE:]], axis=1)


def _split_w_kv(w):
    k = w.shape[0]
    w3 = w.reshape(k, HEADS, NOPE + V_DIM)
    padl = lambda a: jnp.pad(a, ((0, 0), (0, 0), (0, HP - a.shape[-1]))).reshape(k, HW)
    return padl(w3[..., :NOPE]), padl(w3[..., NOPE:])


def _merge_w_kv(wk, wv):
    k = wk.shape[0]
    return jnp.concatenate([wk.reshape(k, HEADS, HP)[..., :NOPE], wv.reshape(k, HEADS, HP)[..., :V_DIM]],
                           axis=-1).reshape(k, HEADS * (NOPE + V_DIM))


class _NoComm:
    first_token = ()

    def late_weights(self, after):
        return {}

    def send(self, name, grads):
        return ()


def _local_step(xcat, tgt, seq, p, comm=_NoComm()):
    r = xcat.shape[0]
    cd = MXU_DTYPE
    p = dict(p)

    w_f, w_r = _split_w_in(p["w_in"])
    w_q = _pad_head_cols(p["w_q_up"], QK_DIM)
    w_k, w_v = _split_w_kv(p["w_kv_up"])

    pos = jnp.arange(r, dtype=F32)
    inv_freq = ROPE_THETA ** (-jnp.arange(HALF, dtype=F32) / HALF)
    ang = pos[:, None] * inv_freq[None, :]
    cos_t = jnp.tile(jnp.cos(ang), (1, HP // HALF))
    sin_t = jnp.tile(jnp.sin(ang), (1, HP // HALF))
    bf_row = jnp.zeros((1, HP), F32).at[0, LANE_FL:LANE_FL + HEADS].set(p["b_forget"])

    h0, h0b = _ln_fwd("ln_emb_fwd", xcat, None, p["ln_emb_g"], p["ln_emb_b"], after=comm.first_token)
    proj_f = _matmul("in_proj_f", h0b, w_f, out_dtype=cd)
    proj_r = _matmul("in_proj_r", h0b, w_r)
    ql = _rms_fwd("q_norm_fwd", proj_r, R_QLAT // Q_RANK, Q_RANK, p["q_norm_g"])
    kvl = _rms_fwd("kv_norm_fwd", proj_r, R_KVLAT // KV_RANK, KV_RANK, p["kv_norm_g"])
    q_raw = _matmul("q_up", ql, w_q)
    k_part = _matmul("k_up", kvl, w_k)
    v_mla = _matmul("v_up", kvl, w_v, out_dtype=cd)
    q_mla, k_mla = _rope_fwd(q_raw, k_part, proj_r, cos_t, sin_t)
    o_mla, o_mla_b, lse_mla = _attn_fwd("mla_fwd", (q_mla, 0), (k_mla, 0), (v_mla, 0), QK_DIM ** -0.5)

    cum, cum_t = _forget_fwd(proj_r, bf_row)
    o_fox, o_fox_b, lse_fox = _attn_fwd("fox_fwd", (proj_f, 0), (proj_f, 1), (proj_f, 2), FOX_DIM ** -0.5, cum, cum_t)

    p.update(comm.late_weights(o_fox_b))
    w_bm = _pad_head_rows(p["w_branch_mla"], V_DIM)
    w_bf = _pad_head_rows(p["w_branch_fox"], FOX_DIM)
    bm = _matmul("branch_mla", o_mla_b, w_bm)
    bfx = _matmul("branch_fox", o_fox_b, w_bf)
    merged = _gate_fwd(proj_r, p["b_gate"], bm, bfx)
    mix = _matmul("out_proj", merged, p["w_out"])
    h1, h1b = _ln_fwd("ln_mix_fwd", h0, mix, p["ln_mix_g"], p["ln_mix_b"])
    up = _matmul("ffn_up", h1b, p["w_ffn_up"])
    act = _glu_fwd(up, p["conv_w"], p["conv_b"])
    f = _matmul("ffn_down", act, p["w_ffn_down"])
    h2, _ = _ln_fwd("ln_ffn_fwd", h1, f, p["ln_ffn_g"], p["ln_ffn_b"])
    loss, dh2 = _loss(h2, tgt, seq)

    g = {}
    dz2, dz2b, g["ln_ffn_g"], g["ln_ffn_b"] = _ln_bwd("ln_ffn_bwd", h1, f, dh2, p["ln_ffn_g"])
    d_act = _matmul("ffn_down_dx", dz2b, p["w_ffn_down"], tb=True)
    g["w_ffn_down"] = _matmul("ffn_down_dw", act, dz2b, ta=True, out_dtype=cd)
    dgate, dval, dcw, g["conv_b"] = _glu_bwd_gate(up, p["conv_w"], p["conv_b"], d_act)
    g["conv_w"] = dcw[:3]
    d_up = _glu_bwd_conv(dgate, dval, p["conv_w"])
    dh1 = _matmul("ffn_up_dx", d_up, p["w_ffn_up"], tb=True, addend=dz2, alpha=ALPHA)
    g["w_ffn_up"] = _matmul("ffn_up_dw", h1b, d_up, ta=True, out_dtype=cd)
    sent = comm.send("ffn", {n: g[n] for n in ("w_ffn_down", "w_ffn_up", "conv_w")})
    dz1, dz1b, g["ln_mix_g"], g["ln_mix_b"] = _ln_bwd("ln_mix_bwd", h0, mix, dh1, p["ln_mix_g"], after=sent)
    dmerged = _matmul("out_proj_dx", dz1b, p["w_out"], tb=True)
    g["w_out"] = _matmul("out_proj_dw", merged, dz1b, ta=True, out_dtype=cd)
    d_bm, d_bf, d_gl, g["b_gate"] = _gate_bwd(proj_r, p["b_gate"], bm, bfx, dmerged)
    d_o_mla = _matmul("branch_mla_dx", d_bm, w_bm, tb=True)
    g["w_branch_mla"] = _unpad_head_rows(_matmul("branch_mla_dw", o_mla_b, d_bm, ta=True, out_dtype=cd), V_DIM)
    d_o_fox = _matmul("branch_fox_dx", d_bf, w_bf, tb=True)
    g["w_branch_fox"] = _unpad_head_rows(_matmul("branch_fox_dw", o_fox_b, d_bf, ta=True, out_dtype=cd), FOX_DIM)

    sent = comm.send("mix", {n: g[n] for n in ("w_out", "w_branch_mla", "w_branch_fox")})
    dl_mla, do_mla_b = _attn_delta("mla_delta", d_o_mla, o_mla, after=sent)
    dq_m, dk_m, dv_m = _attn_bwd("mla_bwd", (q_mla, 0), (k_mla, 0), (v_mla, 0), do_mla_b, lse_mla, dl_mla, QK_DIM ** -0.5)
    dl_fox, do_fox_b = _attn_delta("fox_delta", d_o_fox, o_fox)
    dfq, dfk, dfv, dcq, dck = _attn_bwd("fox_bwd", (proj_f, 0), (proj_f, 1), (proj_f, 2), do_fox_b, lse_fox, dl_fox,
                                        FOX_DIM ** -0.5, cum, cum_t)
    dfl, dbf = _forget_bwd(proj_r, bf_row, dcq, dck)
    g["b_forget"] = dbf[:, LANE_FL:LANE_FL + HEADS]

    dq_b, dk_b, dlast = _rope_bwd(dq_m, dk_m, dfl, cos_t, sin_t)
    dv_b = dv_m.astype(cd)
    d_ql = _matmul("q_up_dx", dq_b, w_q, tb=True)
    g["w_q_up"] = _unpad_head_cols(_matmul("q_up_dw", ql, dq_b, ta=True, out_dtype=cd), QK_DIM)
    d_kvl = _matmul("k_up_dx", dk_b, w_k, tb=True)
    d_kvl = _matmul("v_up_dx", dv_b, w_v, tb=True, addend=d_kvl)
    g["w_kv_up"] = _merge_w_kv(_matmul("k_up_dw", kvl, dk_b, ta=True, out_dtype=cd), _matmul("v_up_dw", kvl, dv_b, ta=True, out_dtype=cd))
    d_qlat, g["q_norm_g"] = _rms_bwd("q_norm_bwd", proj_r, R_QLAT // Q_RANK, Q_RANK, d_ql, p["q_norm_g"])
    d_kvlat, g["kv_norm_g"] = _rms_bwd("kv_norm_bwd", proj_r, R_KVLAT // KV_RANK, KV_RANK, d_kvl, p["kv_norm_g"])
    dproj_f = jnp.concatenate([dfq.astype(cd), dfk.astype(cd), dfv.astype(cd)], axis=1)
    dproj_r = jnp.concatenate([d_qlat, d_kvlat, dlast, jnp.zeros((r, R_GATE - R_LAST - HP), cd), d_gl], axis=1)
    g["w_in"] = _merge_w_in(_matmul("in_proj_f_dw", h0b, dproj_f, ta=True, out_dtype=cd), _matmul("in_proj_r_dw", h0b, dproj_r, ta=True, out_dtype=cd))
    sent = comm.send("in", {n: g[n] for n in ("w_in", "w_q_up", "w_kv_up")})
    dh0 = _matmul("in_proj_f_dx", dproj_f, w_f, tb=True, addend=dz1, alpha=ALPHA, after=sent)
    dh0 = _matmul("in_proj_r_dx", dproj_r, w_r, tb=True, addend=dh0)
    dxcat, _, g["ln_emb_g"], g["ln_emb_b"] = _ln_bwd("ln_emb_bwd", xcat, None, dh0, p["ln_emb_g"])
    return loss, dxcat, g


BIG = (("w_in", 1), ("w_q_up", 1), ("w_kv_up", 1), ("w_branch_mla", 1), ("w_branch_fox", 1), ("w_out", 0),
       ("w_ffn_up", 1), ("w_ffn_down", 0))
SMALL_SHARDED = (("meta_tokens", 1), ("conv_w", 1))
EARLY = ("w_in", "w_q_up", "w_kv_up", "meta_tokens", "conv_w")
LATE = ("w_branch_mla", "w_branch_fox", "w_out", "w_ffn_up", "w_ffn_down")
LAST = ("meta_tokens",)
REPLICATED = ("ln_emb_g", "ln_emb_b", "b_gate", "b_forget", "q_norm_g", "kv_norm_g", "ln_mix_g", "ln_mix_b",
              "conv_b", "ln_ffn_g", "ln_ffn_b")
PACK_COLS = 1024


def _pack(flat_list):
    cat = jnp.concatenate(flat_list)
    n = cat.shape[0]
    rows = -(-n // (8 * PACK_COLS)) * 8
    return jnp.pad(cat, (0, rows * PACK_COLS - n)).reshape(rows, PACK_COLS)


def _gathered_full(g3, axis):
    n, r, c = g3.shape
    if axis == 0:
        return g3.reshape(n * r, c)
    return g3.transpose(1, 0, 2).reshape(r, n * c)


def _shard_major(full, axis):
    r, c = full.shape
    if axis == 0:
        return full.reshape(N_DEV, r // N_DEV, c)
    return full.reshape(r, N_DEV, c // N_DEV).transpose(1, 0, 2)


def kernel(x, meta_tokens, ln_emb_g, ln_emb_b, w_in, b_gate, b_forget, q_norm_g, w_q_up, kv_norm_g, w_kv_up, w_branch_mla, w_branch_fox, w_out, ln_mix_g, ln_mix_b, w_ffn_up, conv_w, conv_b, w_ffn_down, ln_ffn_g, ln_ffn_b, loss_target, m_meta_tokens, m_ln_emb_g, m_ln_emb_b, m_w_in, m_b_gate, m_b_forget, m_q_norm_g, m_w_q_up, m_kv_norm_g, m_w_kv_up, m_w_branch_mla, m_w_branch_fox, m_w_out, m_ln_mix_g, m_ln_mix_b, m_w_ffn_up, m_conv_w, m_conv_b, m_w_ffn_down, m_ln_ffn_g, m_ln_ffn_b, v_meta_tokens, v_ln_emb_g, v_ln_emb_b, v_w_in, v_b_gate, v_b_forget, v_q_norm_g, v_w_q_up, v_kv_norm_g, v_w_kv_up, v_w_branch_mla, v_w_branch_fox, v_w_out, v_ln_mix_g, v_ln_mix_b, v_w_ffn_up, v_conv_w, v_conv_b, v_w_ffn_down, v_ln_ffn_g, v_ln_ffn_b):
    names = ("meta_tokens", "ln_emb_g", "ln_emb_b", "w_in", "b_gate", "b_forget", "q_norm_g", "w_q_up", "kv_norm_g",
             "w_kv_up", "w_branch_mla", "w_branch_fox", "w_out", "ln_mix_g", "ln_mix_b", "w_ffn_up", "conv_w", "conv_b",
             "w_ffn_down", "ln_ffn_g", "ln_ffn_b")
    w_args = (meta_tokens, ln_emb_g, ln_emb_b, w_in, b_gate, b_forget, q_norm_g, w_q_up, kv_norm_g, w_kv_up,
              w_branch_mla, w_branch_fox, w_out, ln_mix_g, ln_mix_b, w_ffn_up, conv_w, conv_b, w_ffn_down, ln_ffn_g, ln_ffn_b)
    m_args = (m_meta_tokens, m_ln_emb_g, m_ln_emb_b, m_w_in, m_b_gate, m_b_forget, m_q_norm_g, m_w_q_up, m_kv_norm_g,
              m_w_kv_up, m_w_branch_mla, m_w_branch_fox, m_w_out, m_ln_mix_g, m_ln_mix_b, m_w_ffn_up, m_conv_w, m_conv_b,
              m_w_ffn_down, m_ln_ffn_g, m_ln_ffn_b)
    v_args = (v_meta_tokens, v_ln_emb_g, v_ln_emb_b, v_w_in, v_b_gate, v_b_forget, v_q_norm_g, v_w_q_up, v_kv_norm_g,
              v_w_kv_up, v_w_branch_mla, v_w_branch_fox, v_w_out, v_ln_mix_g, v_ln_mix_b, v_w_ffn_up, v_conv_w, v_conv_b,
              v_w_ffn_down, v_ln_ffn_g, v_ln_ffn_b)
    as2d = lambda a: a.reshape((-1, a.shape[-1])) if a.ndim != 1 else a.reshape(1, -1)
    w = {n: as2d(a) for n, a in zip(names, w_args)}
    m = {n: as2d(a) for n, a in zip(names, m_args)}
    v = {n: as2d(a) for n, a in zip(names, v_args)}
    out_shape = {n: a.shape for n, a in zip(names, w_args)}

    seq = x.shape[1]
    rows = -(-(N_META + seq) // ROW_ALIGN) * ROW_ALIGN
    axis_of = dict(BIG + SMALL_SHARDED)
    big = set(n for n, _ in BIG)
    wire = lambda n, a: a.astype(MXU_DTYPE) if n in big else a
    my_id = _my_id()
    slot_is_mine = lax.broadcasted_iota(jnp.int32, (N_DEV, 1, 1), 0) == my_id

    early = _allgather("gather_early", [wire(n, w[n]) for n in EARLY])
    p = {n: _gathered_full(g3, axis_of[n]) for n, g3 in zip(EARLY, early)}
    for n in REPLICATED:
        p[n] = w[n].reshape(-1)
    late_src = [wire(n, w[n]) for n in LATE]
    late_handle, late_token = _push_start("gather_late_start", late_src, False, after=early[0])
    sent = {}

    class Comm:
        first_token = (late_token,)

        def late_weights(self, after):
            lands = _push_wait("gather_late_wait", late_handle, after)
            return {n: _gathered_full(jnp.where(slot_is_mine, own[None], land), axis_of[n])
                    for n, own, land in zip(LATE, late_src, lands)}

        def send(self, name, grads):
            names_ = tuple(grads)
            parts = [_shard_major(grads[n], axis_of[n]).astype(MXU_DTYPE) for n in names_]
            handle, token = _push_start("send_" + name + "_start", parts, True)
            sent[name] = (names_, parts, handle)
            return (token,)

    zpad = jnp.zeros((rows - N_META - seq, D_MODEL), F32)
    xcat = jnp.concatenate([p["meta_tokens"], x[0], zpad], axis=0)
    tgt = jnp.concatenate([jnp.zeros((N_META, D_MODEL), F32), loss_target[0], zpad], axis=0)
    loss_part, dxcat, g = _local_step(xcat, tgt, seq, p, Comm())
    grad_x = dxcat[N_META:N_META + seq][None]
    g["meta_tokens"] = dxcat[:N_META]
    loss = lax.psum(loss_part[0, 0], ("x", "y", "c"))

    g_last = _exchange("exchange_last", [_shard_major(g[n], axis_of[n]).astype(MXU_DTYPE) for n in LAST])
    rep_all = _allgather("gather_small_grads", [_pack([g[n].reshape(-1) for n in REPLICATED])])[0]

    res = {}
    for n, parts in zip(LAST, g_last):
        res[n] = _adamw("adamw_" + n, parts, w[n], m[n], v[n])
    prev = rep_all
    for name, (names_, parts, handle) in sent.items():
        lands = _push_wait("send_" + name + "_wait", handle, prev)
        for n, part, land in zip(names_, parts, lands):
            own = lax.dynamic_index_in_dim(part, my_id, axis=0, keepdims=False)
            res[n] = _adamw("adamw_" + n, land, w[n], m[n], v[n], own=own)
            prev = res[n][0]
    rep_w = _pack([w[n].reshape(-1) for n in REPLICATED])
    rep_m = _pack([m[n].reshape(-1) for n in REPLICATED])
    rep_v = _pack([v[n].reshape(-1) for n in REPLICATED])
    rep_res = _adamw("adamw_replicated", rep_all, rep_w, rep_m, rep_v)
    off = 0
    for n in REPLICATED:
        sz = w[n].size
        res[n] = tuple(a.reshape(-1)[off:off + sz] for a in rep_res)
        off += sz

    outs = [loss, grad_x]
    for idx in range(4):
        outs += [res[n][idx].reshape(out_shape[n]) for n in names]
    return tuple(outs)
```

```python
import jax
import jax.numpy as jnp
from jax import lax
from jax.experimental import pallas as pl
from jax.experimental.pallas import tpu as pltpu

F32 = jnp.float32
BF16 = jnp.bfloat16
MXU_DTYPE = BF16

N_DEV = 8
N_META = 16
D_MODEL = 1024
HEADS = 8
Q_RANK = 384
KV_RANK = 128
NOPE = 64
ROPE = 32
HALF = ROPE // 2
QK_DIM = NOPE + ROPE
V_DIM = 64
FOX_DIM = 64
FOX_W = HEADS * FOX_DIM
D_FF = 2816
ROPE_THETA = 10000.0
LN_EPS = 1e-5
RMS_EPS = 1e-6
ALPHA = 2.0 ** 0.25
NEG_INF = -1e30

HP = 128
HW = HEADS * HP
F_W = 3 * HW
R_QLAT = 0
R_KVLAT = Q_RANK
R_LAST = R_KVLAT + KV_RANK
R_GATE = D_MODEL
R_W = R_GATE + 2 * D_MODEL
LANE_FL = 0
LANE_PE = NOPE

ADAM_LR = 0.001
ADAM_B1 = 0.9
ADAM_B2 = 0.999
ADAM_EPS = 1e-08
ADAM_WD = 0.01
ADAM_STEP = 10

ROW_BLOCK = 256
ATT_TQ = 768
ATT_TK = 256
ATT_HEADS = 2
ROW_ALIGN = 768
MM_BLOCK_CAP = 1408
VMEM_LIMIT = 56 * 1024 * 1024
HIGHEST = lax.Precision.HIGHEST
NT = (((1,), (1,)), ((), ()))
TN = (((0,), (0,)), ((), ()))


def _params(sem=None):
    return pltpu.CompilerParams(dimension_semantics=sem, vmem_limit_bytes=VMEM_LIMIT)


def _call(name, body, grid, ins, outs, scratch=(), sem=None, after=()):
    n_in = len(ins)
    n_tok = len(after)

    def run(*refs):
        body(*refs[:n_in], *refs[n_in + n_tok:])

    tok_spec = pl.BlockSpec((8, 128), lambda *_: (0, 0))
    return pl.pallas_call(
        run, name=name, grid=grid,
        in_specs=[s for _, s in ins] + [tok_spec] * n_tok,
        out_specs=[s for _, s in outs],
        out_shape=[o for o, _ in outs],
        scratch_shapes=list(scratch),
        compiler_params=_params(sem),
    )(*[a for a, _ in ins], *after)


def _sds(shape, dtype):
    return jax.ShapeDtypeStruct(shape, dtype)


def _rows(br, c, cb=0):
    return pl.BlockSpec((br, c), lambda i: (i, cb))


def _whole(shape):
    n = len(shape)
    return pl.BlockSpec(shape, lambda i: (0,) * n)


def _pick(dim, cap, mult):
    best = None
    d = mult
    while d <= min(dim, cap):
        if dim % d == 0:
            best = d
        d += mult
    return best if best is not None else dim


def _hs(h):
    return slice(h * HP, (h + 1) * HP)


def _matmul(name, a, b, *, ta=False, tb=False, out_dtype=F32, addend=None, alpha=1.0, after=()):
    if ta:
        k, m = a.shape
    else:
        m, k = a.shape
    if tb:
        n, k2 = b.shape
    else:
        k2, n = b.shape
    assert k == k2, (name, a.shape, b.shape)
    bm = _pick(m, MM_BLOCK_CAP, 128 if ta else 16)
    bn = _pick(n, MM_BLOCK_CAP, 128)
    bk = _pick(k, MM_BLOCK_CAP, 128 if (not ta or tb) else 16)
    nk = k // bk
    dims = (((0 if ta else 1,), (1 if tb else 0,)), ((), ()))
    has_add = addend is not None

    def body(*refs):
        a_ref, b_ref = refs[:2]
        add_ref = refs[2] if has_add else None
        o_ref = refs[3 if has_add else 2]

        def finish(r):
            if has_add:
                r = r + alpha * add_ref[...]
            o_ref[...] = r.astype(o_ref.dtype)

        part = lax.dot_general(a_ref[...], b_ref[...], dims, preferred_element_type=F32)
        if nk == 1:
            finish(part)
            return
        acc_ref = refs[-1]
        kk = pl.program_id(2)

        @pl.when(kk == 0)
        def _():
            acc_ref[...] = part

        @pl.when(kk > 0)
        def _():
            acc_ref[...] += part

        @pl.when(kk == nk - 1)
        def _():
            finish(acc_ref[...])

    a_spec = pl.BlockSpec((bk, bm), lambda i, j, l: (l, i)) if ta else pl.BlockSpec((bm, bk), lambda i, j, l: (i, l))
    b_spec = pl.BlockSpec((bn, bk), lambda i, j, l: (j, l)) if tb else pl.BlockSpec((bk, bn), lambda i, j, l: (l, j))
    o_spec = pl.BlockSpec((bm, bn), lambda i, j, l: (i, j))
    ins = [(a, a_spec), (b, b_spec)]
    if has_add:
        ins.append((addend, o_spec))
    return _call(name, body, (m // bm, n // bn, nk), ins, [(_sds((m, n), out_dtype), o_spec)],
                 scratch=[pltpu.VMEM((bm, bn), F32)] if nk > 1 else [],
                 sem=("parallel", "parallel", "arbitrary"), after=after)[0]


def _ln_stats(z):
    mu = jnp.mean(z, axis=-1, keepdims=True)
    zc = z - mu
    var = jnp.mean(zc * zc, axis=-1, keepdims=True)
    rstd = lax.rsqrt(var + LN_EPS)
    return zc * rstd, rstd


def _ln_fwd(name, a, res, g, b, after=()):
    r, d = a.shape
    br = ROW_BLOCK
    has_res = res is not None

    def body(*refs):
        if has_res:
            a_ref, r_ref, g_ref, b_ref, y_ref, yb_ref = refs
            z = ALPHA * a_ref[...] + r_ref[...]
        else:
            a_ref, g_ref, b_ref, y_ref, yb_ref = refs
            z = a_ref[...]
        xhat, _ = _ln_stats(z)
        y = xhat * g_ref[...] + b_ref[...]
        y_ref[...] = y
        yb_ref[...] = y.astype(yb_ref.dtype)

    ins = [(a, _rows(br, d))]
    if has_res:
        ins.append((res, _rows(br, d)))
    ins += [(g.reshape(1, d), _whole((1, d))), (b.reshape(1, d), _whole((1, d)))]
    outs = [(_sds((r, d), F32), _rows(br, d)), (_sds((r, d), MXU_DTYPE), _rows(br, d))]
    return _call(name, body, (r // br,), ins, outs, sem=("parallel",), after=after)


def _ln_bwd(name, a, res, dy, g, after=()):
    r, d = a.shape
    br = ROW_BLOCK
    has_res = res is not None

    def body(*refs):
        if has_res:
            a_ref, r_ref, dy_ref, g_ref, dz_ref, dzb_ref, dg_ref, db_ref = refs
            z = ALPHA * a_ref[...] + r_ref[...]
        else:
            a_ref, dy_ref, g_ref, dz_ref, dzb_ref, dg_ref, db_ref = refs
            z = a_ref[...]
        xhat, rstd = _ln_stats(z)
        dyv = dy_ref[...]
        dyg = dyv * g_ref[...]
        m1 = jnp.mean(dyg, axis=-1, keepdims=True)
        m2 = jnp.mean(dyg * xhat, axis=-1, keepdims=True)
        dz = rstd * (dyg - m1 - xhat * m2)
        dz_ref[...] = dz
        dzb_ref[...] = dz.astype(dzb_ref.dtype)

        @pl.when(pl.program_id(0) == 0)
        def _():
            dg_ref[...] = jnp.zeros_like(dg_ref)
            db_ref[...] = jnp.zeros_like(db_ref)

        dg_ref[...] += jnp.sum(dyv * xhat, axis=0, keepdims=True)
        db_ref[...] += jnp.sum(dyv, axis=0, keepdims=True)

    ins = [(a, _rows(br, d))]
    if has_res:
        ins.append((res, _rows(br, d)))
    ins += [(dy, _rows(br, d)), (g.reshape(1, d), _whole((1, d)))]
    outs = [(_sds((r, d), F32), _rows(br, d)), (_sds((r, d), MXU_DTYPE), _rows(br, d)),
            (_sds((1, d), F32), _whole((1, d))), (_sds((1, d), F32), _whole((1, d)))]
    return _call(name, body, (r // br,), ins, outs, sem=("arbitrary",), after=after)


def _rms_fwd(name, proj, cb, width, g):
    r = proj.shape[0]
    br = ROW_BLOCK

    def body(x_ref, g_ref, y_ref):
        x = x_ref[...]
        rstd = lax.rsqrt(jnp.mean(x * x, axis=-1, keepdims=True) + RMS_EPS)
        y_ref[...] = (x * rstd * g_ref[...]).astype(y_ref.dtype)

    return _call(name, body, (r // br,), [(proj, _rows(br, width, cb)), (g.reshape(1, width), _whole((1, width)))],
                 [(_sds((r, width), MXU_DTYPE), _rows(br, width))], sem=("parallel",))[0]


def _rms_bwd(name, proj, cb, width, dy, g):
    r = proj.shape[0]
    br = ROW_BLOCK

    def body(x_ref, dy_ref, g_ref, dx_ref, dg_ref):
        x = x_ref[...]
        rstd = lax.rsqrt(jnp.mean(x * x, axis=-1, keepdims=True) + RMS_EPS)
        nrm = x * rstd
        dyv = dy_ref[...]
        dyg = dyv * g_ref[...]
        dx = rstd * (dyg - nrm * jnp.mean(dyg * nrm, axis=-1, keepdims=True))
        dx_ref[...] = dx.astype(dx_ref.dtype)

        @pl.when(pl.program_id(0) == 0)
        def _():
            dg_ref[...] = jnp.zeros_like(dg_ref)

        dg_ref[...] += jnp.sum(dyv * nrm, axis=0, keepdims=True)

    return _call(name, body, (r // br,),
                 [(proj, _rows(br, width, cb)), (dy, _rows(br, width)), (g.reshape(1, width), _whole((1, width)))],
                 [(_sds((r, width), MXU_DTYPE), _rows(br, width)), (_sds((1, width), F32), _whole((1, width)))],
                 sem=("arbitrary",))


def _lane_iota(shape):
    return lax.broadcasted_iota(jnp.int32, shape, 1)


def _rotary(t, c, s, lane, sign):
    second = pltpu.roll(t, HP - HALF, axis=1)
    first = pltpu.roll(t, HALF, axis=1)
    lo = (lane >= LANE_PE) & (lane < LANE_PE + HALF)
    hi = (lane >= LANE_PE + HALF) & (lane < LANE_PE + ROPE)
    return jnp.where(lo, t * c - sign * second * s, jnp.where(hi, t * c + sign * first * s, t))


def _rope_fwd(q_raw, k_part, proj_r, cos_t, sin_t):
    r = q_raw.shape[0]
    br = ROW_BLOCK

    def body(q_ref, k_ref, t_ref, c_ref, s_ref, qo_ref, ko_ref):
        c = c_ref[...]
        s = s_ref[...]
        lane = _lane_iota((br, HP))
        pe = (lane >= LANE_PE) & (lane < LANE_PE + ROPE)
        kp = jnp.where(pe, _rotary(t_ref[...], c, s, lane, 1.0), 0.0)
        for h in range(HEADS):
            qo_ref[:, _hs(h)] = _rotary(q_ref[:, _hs(h)], c, s, lane, 1.0).astype(qo_ref.dtype)
            ko_ref[:, _hs(h)] = (k_ref[:, _hs(h)] + kp).astype(ko_ref.dtype)

    blk = _rows(br, HP)
    wide = _rows(br, HW)
    return _call("rope_fwd", body, (r // br,),
                 [(q_raw, wide), (k_part, wide), (proj_r, _rows(br, HP, R_LAST // HP)), (cos_t, blk), (sin_t, blk)],
                 [(_sds((r, HW), MXU_DTYPE), wide)] * 2, sem=("parallel",))


def _rope_bwd(dq, dk, dfl, cos_t, sin_t):
    r = dq.shape[0]
    br = ROW_BLOCK

    def body(dq_ref, dk_ref, fl_ref, c_ref, s_ref, dqo_ref, dko_ref, dl_ref):
        c = c_ref[...]
        s = s_ref[...]
        lane = _lane_iota((br, HP))
        pe = (lane >= LANE_PE) & (lane < LANE_PE + ROPE)
        acc = jnp.zeros((br, HP), F32)
        for h in range(HEADS):
            dqo_ref[:, _hs(h)] = _rotary(dq_ref[:, _hs(h)], c, s, lane, -1.0).astype(dqo_ref.dtype)
            dkh = dk_ref[:, _hs(h)]
            acc = acc + dkh
            dko_ref[:, _hs(h)] = dkh.astype(dko_ref.dtype)
        dl_ref[...] = (jnp.where(pe, _rotary(acc, c, s, lane, -1.0), 0.0) + fl_ref[...]).astype(dl_ref.dtype)

    blk = _rows(br, HP)
    wide = _rows(br, HW)
    return _call("rope_bwd", body, (r // br,),
                 [(dq, wide), (dk, wide), (dfl, blk), (cos_t, blk), (sin_t, blk)],
                 [(_sds((r, HW), MXU_DTYPE), wide), (_sds((r, HW), MXU_DTYPE), wide), (_sds((r, HP), MXU_DTYPE), blk)],
                 sem=("parallel",))


def _log_sigmoid(x):
    return jnp.minimum(x, 0.0) - jnp.log(1.0 + jnp.exp(-jnp.abs(x)))


def _head_lane(x, h, lane):
    return jnp.sum(jnp.where(lane == h, x, 0.0), axis=1, keepdims=True)


def _forget_fwd(proj_r, bf_row):
    r = proj_r.shape[0]
    br = ROW_BLOCK

    def body(t_ref, b_ref, ob_ref, ot_ref, carry_ref):
        @pl.when(pl.program_id(0) == 0)
        def _():
            carry_ref[...] = jnp.zeros_like(carry_ref)

        x = t_ref[...] + b_ref[...]
        lane = _lane_iota(x.shape)
        lf = jnp.where((lane >= LANE_FL) & (lane < LANE_FL + HEADS), _log_sigmoid(x), 0.0)
        tri = (lax.broadcasted_iota(jnp.int32, (br, br), 0) >= lax.broadcasted_iota(jnp.int32, (br, br), 1)).astype(F32)
        cum = jnp.dot(tri, lf, precision=HIGHEST, preferred_element_type=F32) + carry_ref[0:1, :]
        for h in range(HEADS):
            ob_ref[:, _hs(h)] = jnp.broadcast_to(_head_lane(cum, LANE_FL + h, lane), (br, HP))
        ot_ref[...] = cum.T[LANE_FL:LANE_FL + HEADS, :]
        carry_ref[...] = jnp.broadcast_to(cum[br - 1:br, :], carry_ref.shape)

    return _call("forget_fwd", body, (r // br,),
                 [(proj_r, _rows(br, HP, R_LAST // HP)), (bf_row, _whole((1, HP)))],
                 [(_sds((r, HW), F32), _rows(br, HW)), (_sds((HEADS, r), F32), pl.BlockSpec((HEADS, br), lambda i: (0, i)))],
                 scratch=[pltpu.VMEM((8, HP), F32)], sem=("arbitrary",))


def _forget_bwd(proj_r, bf_row, dcq_t, dck_b):
    r = proj_r.shape[0]
    br = ROW_BLOCK
    nb = r // br

    def body(t_ref, b_ref, dcq_ref, dck_ref, o_ref, db_ref, carry_ref):
        @pl.when(pl.program_id(0) == 0)
        def _():
            carry_ref[...] = jnp.zeros_like(carry_ref)
            db_ref[...] = jnp.zeros_like(db_ref)

        lane = _lane_iota((br, HP))
        dc = jnp.concatenate([dcq_ref[...], jnp.zeros((HP - HEADS, br), F32)], axis=0).T
        for h in range(HEADS):
            dc = dc + jnp.where(lane == LANE_FL + h, dck_ref[:, h * HP:h * HP + 1], 0.0)
        triu = (lax.broadcasted_iota(jnp.int32, (br, br), 0) <= lax.broadcasted_iota(jnp.int32, (br, br), 1)).astype(F32)
        dlf = jnp.dot(triu, dc, precision=HIGHEST, preferred_element_type=F32) + carry_ref[0:1, :]
        carry_ref[...] = jnp.broadcast_to(dlf[0:1, :], carry_ref.shape)
        x = t_ref[...] + b_ref[...]
        dfl = jnp.where((lane >= LANE_FL) & (lane < LANE_FL + HEADS), dlf * jax.nn.sigmoid(-x), 0.0)
        o_ref[...] = dfl
        db_ref[...] += jnp.sum(dfl, axis=0, keepdims=True)

    rev = pl.BlockSpec((br, HP), lambda i: (nb - 1 - i, 0))
    return _call("forget_bwd", body, (nb,),
                 [(proj_r, pl.BlockSpec((br, HP), lambda i: (nb - 1 - i, R_LAST // HP))), (bf_row, _whole((1, HP))),
                  (dcq_t, pl.BlockSpec((HEADS, br), lambda i: (0, nb - 1 - i))),
                  (dck_b, pl.BlockSpec((br, HW), lambda i: (nb - 1 - i, 0)))],
                 [(_sds((r, HP), F32), rev), (_sds((1, HP), F32), _whole((1, HP)))],
                 scratch=[pltpu.VMEM((8, HP), F32)], sem=("arbitrary",))


def _gate_fwd(proj_r, b_gate, bm, bfx):
    r, d = bm.shape
    br = ROW_BLOCK
    cb = R_GATE // d

    def body(gm_ref, gf_ref, b1_ref, b2_ref, bm_ref, bf_ref, o_ref):
        g1 = jax.nn.sigmoid(gm_ref[...] + b1_ref[...])
        g2 = jax.nn.sigmoid(gf_ref[...] + b2_ref[...])
        o_ref[...] = (g1 * bm_ref[...] + g2 * bf_ref[...]).astype(o_ref.dtype)

    b1 = b_gate[:d].reshape(1, d)
    b2 = b_gate[d:].reshape(1, d)
    return _call("gate_fwd", body, (r // br,),
                 [(proj_r, _rows(br, d, cb)), (proj_r, _rows(br, d, cb + 1)), (b1, _whole((1, d))), (b2, _whole((1, d))),
                  (bm, _rows(br, d)), (bfx, _rows(br, d))],
                 [(_sds((r, d), MXU_DTYPE), _rows(br, d))], sem=("parallel",))[0]


def _gate_bwd(proj_r, b_gate, bm, bfx, dmerged):
    r, d = bm.shape
    br = ROW_BLOCK
    cb = R_GATE // d

    def body(gm_ref, gf_ref, b1_ref, b2_ref, bm_ref, bf_ref, dm_ref, dbm_ref, dbf_ref, dgl_ref, dbg_ref):
        g1 = jax.nn.sigmoid(gm_ref[...] + b1_ref[...])
        g2 = jax.nn.sigmoid(gf_ref[...] + b2_ref[...])
        dm = dm_ref[...]
        dbm_ref[...] = (dm * g1).astype(dbm_ref.dtype)
        dbf_ref[...] = (dm * g2).astype(dbf_ref.dtype)
        dl1 = dm * bm_ref[...] * (g1 * (1.0 - g1))
        dl2 = dm * bf_ref[...] * (g2 * (1.0 - g2))
        dgl_ref[:, 0:d] = dl1.astype(dgl_ref.dtype)
        dgl_ref[:, d:2 * d] = dl2.astype(dgl_ref.dtype)

        @pl.when(pl.program_id(0) == 0)
        def _():
            dbg_ref[...] = jnp.zeros_like(dbg_ref)

        dbg_ref[:, 0:d] += jnp.sum(dl1, axis=0, keepdims=True)
        dbg_ref[:, d:2 * d] += jnp.sum(dl2, axis=0, keepdims=True)

    b1 = b_gate[:d].reshape(1, d)
    b2 = b_gate[d:].reshape(1, d)
    return _call("gate_bwd", body, (r // br,),
                 [(proj_r, _rows(br, d, cb)), (proj_r, _rows(br, d, cb + 1)), (b1, _whole((1, d))), (b2, _whole((1, d))),
                  (bm, _rows(br, d)), (bfx, _rows(br, d)), (dmerged, _rows(br, d))],
                 [(_sds((r, d), MXU_DTYPE), _rows(br, d)), (_sds((r, d), MXU_DTYPE), _rows(br, d)),
                  (_sds((r, 2 * d), MXU_DTYPE), _rows(br, 2 * d)), (_sds((1, 2 * d), F32), _whole((1, 2 * d)))],
                 sem=("arbitrary",))


def _conv_taps(gp, halo, first_block):
    halo = jnp.where(first_block, 0.0, halo)
    rid = lax.broadcasted_iota(jnp.int32, gp.shape, 0)
    g1 = jnp.where(rid == 0, halo[7:8, :], pltpu.roll(gp, 1, axis=0))
    g2 = jnp.where(rid == 0, halo[6:7, :], jnp.where(rid == 1, halo[7:8, :], pltpu.roll(gp, 2, axis=0)))
    return g1, g2


def _prev_halo(br, c):
    return pl.BlockSpec((8, c), lambda i: (jnp.maximum(i * (br // 8) - 1, 0), 0))


def _glu_fwd(up, conv_w, conv_b):
    r = up.shape[0]
    c = D_FF
    br = ROW_BLOCK

    def body(gp_ref, halo_ref, val_ref, w_ref, b_ref, o_ref):
        gp = gp_ref[...]
        g1, g2 = _conv_taps(gp, halo_ref[...], pl.program_id(0) == 0)
        gate = w_ref[0:1, :] * g2 + w_ref[1:2, :] * g1 + w_ref[2:3, :] * gp + b_ref[...]
        o_ref[...] = (gate * jax.nn.sigmoid(gate) * val_ref[...]).astype(o_ref.dtype)

    return _call("glu_fwd", body, (r // br,),
                 [(up, _rows(br, c, 0)), (up, _prev_halo(br, c)), (up, _rows(br, c, 1)),
                  (conv_w, _whole((3, c))), (conv_b.reshape(1, c), _whole((1, c)))],
                 [(_sds((r, c), MXU_DTYPE), _rows(br, c))], sem=("parallel",))[0]


def _glu_bwd_gate(up, conv_w, conv_b, d_act):
    r = up.shape[0]
    c = D_FF
    br = ROW_BLOCK

    def body(gp_ref, halo_ref, val_ref, w_ref, b_ref, da_ref, dg_ref, dv_ref, dw_ref, db_ref):
        gp = gp_ref[...]
        g1, g2 = _conv_taps(gp, halo_ref[...], pl.program_id(0) == 0)
        gate = w_ref[0:1, :] * g2 + w_ref[1:2, :] * g1 + w_ref[2:3, :] * gp + b_ref[...]
        sg = jax.nn.sigmoid(gate)
        da = da_ref[...]
        dv_ref[...] = (da * (gate * sg)).astype(dv_ref.dtype)
        dg = da * val_ref[...] * (sg * (1.0 + gate * (1.0 - sg)))
        dg_ref[...] = dg

        @pl.when(pl.program_id(0) == 0)
        def _():
            dw_ref[...] = jnp.zeros_like(dw_ref)
            db_ref[...] = jnp.zeros_like(db_ref)

        dw_ref[0:1, :] += jnp.sum(dg * g2, axis=0, keepdims=True)
        dw_ref[1:2, :] += jnp.sum(dg * g1, axis=0, keepdims=True)
        dw_ref[2:3, :] += jnp.sum(dg * gp, axis=0, keepdims=True)
        db_ref[...] += jnp.sum(dg, axis=0, keepdims=True)

    return _call("glu_bwd_gate", body, (r // br,),
                 [(up, _rows(br, c, 0)), (up, _prev_halo(br, c)), (up, _rows(br, c, 1)),
                  (conv_w, _whole((3, c))), (conv_b.reshape(1, c), _whole((1, c))), (d_act, _rows(br, c))],
                 [(_sds((r, c), F32), _rows(br, c)), (_sds((r, c), MXU_DTYPE), _rows(br, c)),
                  (_sds((8, c), F32), _whole((8, c))), (_sds((1, c), F32), _whole((1, c)))],
                 sem=("arbitrary",))


def _glu_bwd_conv(dg, dval, conv_w):
    r, c = dg.shape
    br = ROW_BLOCK
    nb = r // br

    def body(dg_ref, nxt_ref, dv_ref, w_ref, o_ref):
        x = dg_ref[...]
        nxt = jnp.where(pl.program_id(0) == nb - 1, 0.0, nxt_ref[...])
        rid = lax.broadcasted_iota(jnp.int32, x.shape, 0)
        u1 = jnp.where(rid == br - 1, nxt[0:1, :], pltpu.roll(x, br - 1, axis=0))
        u2 = jnp.where(rid == br - 1, nxt[1:2, :], jnp.where(rid == br - 2, nxt[0:1, :], pltpu.roll(x, br - 2, axis=0)))
        dgp = w_ref[2:3, :] * x + w_ref[1:2, :] * u1 + w_ref[0:1, :] * u2
        o_ref[:, 0:c] = dgp.astype(o_ref.dtype)
        o_ref[:, c:2 * c] = dv_ref[...]

    nxt_spec = pl.BlockSpec((8, c), lambda i: (jnp.minimum((i + 1) * (br // 8), r // 8 - 1), 0))
    return _call("glu_bwd_conv", body, (nb,),
                 [(dg, _rows(br, c)), (dg, nxt_spec), (dval, _rows(br, c)), (conv_w, _whole((3, c)))],
                 [(_sds((r, 2 * c), MXU_DTYPE), _rows(br, 2 * c))], sem=("parallel",))[0]


def _token_specs(seq, d):
    br = ROW_BLOCK
    nxb = seq // br
    main = pl.BlockSpec((br, d), lambda i: (jnp.minimum(i, nxb - 1), 0))
    tail = pl.BlockSpec((N_META, d), lambda i: (jnp.clip(i * (br // N_META) - 1, 0, seq // N_META - 1), 0))
    return main, tail


def _padded_block(main_ref, tail_ref, first, seq):
    br = ROW_BLOCK
    i = pl.program_id(0)
    nxb = seq // br
    main = jnp.where(i < nxb, main_ref[...], 0.0)
    head = jnp.where(i == 0, first, jnp.where(i <= nxb, tail_ref[...], 0.0))
    return jnp.concatenate([head, main[:br - N_META]], axis=0)


def _ln_emb_fwd(x, meta, g, b, rows, after=()):
    seq, d = x.shape
    br = ROW_BLOCK
    assert seq % br == 0 and br % N_META == 0 and rows % br == 0

    def body(x_ref, tail_ref, meta_ref, g_ref, b_ref, y_ref, yb_ref):
        z = _padded_block(x_ref, tail_ref, meta_ref[...], seq)
        xhat, _ = _ln_stats(z)
        y = xhat * g_ref[...] + b_ref[...]
        y_ref[...] = y
        yb_ref[...] = y.astype(yb_ref.dtype)

    main, tail = _token_specs(seq, d)
    return _call("ln_emb_fwd", body, (rows // br,),
                 [(x, main), (x, tail), (meta, _whole((N_META, d))), (g.reshape(1, d), _whole((1, d))),
                  (b.reshape(1, d), _whole((1, d)))],
                 [(_sds((rows, d), F32), _rows(br, d)), (_sds((rows, d), MXU_DTYPE), _rows(br, d))],
                 sem=("parallel",), after=after)


def _ln_emb_bwd(x, meta, dh0, g):
    seq, d = x.shape
    br = ROW_BLOCK
    step = br // N_META

    def ln_bwd(z, dy, gv):
        xhat, rstd = _ln_stats(z)
        dyg = dy * gv
        m1 = jnp.mean(dyg, axis=-1, keepdims=True)
        m2 = jnp.mean(dyg * xhat, axis=-1, keepdims=True)
        dz = rstd * (dyg - m1 - xhat * m2)
        return dz, jnp.sum(dy * xhat, axis=0, keepdims=True), jnp.sum(dy, axis=0, keepdims=True)

    def body(x_ref, dh_ref, nxt_ref, meta_ref, top_ref, g_ref, dx_ref, dm_ref, dg_ref, db_ref):
        gv = g_ref[...]
        dy = jnp.concatenate([dh_ref[N_META:, :], nxt_ref[...]], axis=0)
        dz, dg, db = ln_bwd(x_ref[...], dy, gv)
        dx_ref[...] = dz

        @pl.when(pl.program_id(0) == 0)
        def _():
            dzm, dgm, dbm = ln_bwd(meta_ref[...], top_ref[...], gv)
            dm_ref[...] = dzm
            dg_ref[...] = dgm
            db_ref[...] = dbm

        dg_ref[...] += dg
        db_ref[...] += db

    small = _whole((N_META, d))
    return _call("ln_emb_bwd", body, (seq // br,),
                 [(x, _rows(br, d)), (dh0, _rows(br, d)), (dh0, pl.BlockSpec((N_META, d), lambda i: ((i + 1) * step, 0))),
                  (meta, small), (dh0, small), (g.reshape(1, d), _whole((1, d)))],
                 [(_sds((seq, d), F32), _rows(br, d)), (_sds((N_META, d), F32), small),
                  (_sds((1, d), F32), _whole((1, d))), (_sds((1, d), F32), _whole((1, d)))], sem=("arbitrary",))


def _ln_ffn_loss(h1, f, tgt, g, b):
    r, d = h1.shape
    seq = tgt.shape[0]
    br = ROW_BLOCK

    def body(a_ref, r_ref, t_ref, tail_ref, g_ref, b_ref, l_ref):
        err = _loss_err(a_ref, r_ref, t_ref, tail_ref, g_ref, b_ref, seq)[0]

        @pl.when(pl.program_id(0) == 0)
        def _():
            l_ref[...] = jnp.zeros_like(l_ref)

        l_ref[...] += jnp.sum(jnp.sum(err * err, axis=1, keepdims=True), axis=0, keepdims=True) * (0.5 / d)

    main, tail = _token_specs(seq, d)
    return _call("ln_ffn_loss", body, (r // br,),
                 [(h1, _rows(br, d)), (f, _rows(br, d)), (tgt, main), (tgt, tail),
                  (g.reshape(1, d), _whole((1, d))), (b.reshape(1, d), _whole((1, d)))],
                 [(_sds((1, 1), F32), _whole((1, 1)))], sem=("arbitrary",))[0]


def _loss_err(a_ref, r_ref, t_ref, tail_ref, g_ref, b_ref, seq):
    br, d = a_ref.shape
    xhat, rstd = _ln_stats(ALPHA * a_ref[...] + r_ref[...])
    y = xhat * g_ref[...] + b_ref[...]
    t = _padded_block(t_ref, tail_ref, jnp.zeros((N_META, d), F32), seq)
    rid = lax.broadcasted_iota(jnp.int32, (br, d), 0) + pl.program_id(0) * br
    valid = (rid >= N_META) & (rid < N_META + seq)
    return jnp.where(valid, y - t, 0.0), xhat, rstd


def _ln_ffn_bwd(h1, f, tgt, g, b):
    r, d = h1.shape
    seq = tgt.shape[0]
    br = ROW_BLOCK

    def body(a_ref, r_ref, t_ref, tail_ref, g_ref, b_ref, dz_ref, dzb_ref, dg_ref, db_ref):
        err, xhat, rstd = _loss_err(a_ref, r_ref, t_ref, tail_ref, g_ref, b_ref, seq)
        dyv = err * (1.0 / d)
        dyg = dyv * g_ref[...]
        m1 = jnp.mean(dyg, axis=-1, keepdims=True)
        m2 = jnp.mean(dyg * xhat, axis=-1, keepdims=True)
        dz = rstd * (dyg - m1 - xhat * m2)
        dz_ref[...] = dz
        dzb_ref[...] = dz.astype(dzb_ref.dtype)

        @pl.when(pl.program_id(0) == 0)
        def _():
            dg_ref[...] = jnp.zeros_like(dg_ref)
            db_ref[...] = jnp.zeros_like(db_ref)

        dg_ref[...] += jnp.sum(dyv * xhat, axis=0, keepdims=True)
        db_ref[...] += jnp.sum(dyv, axis=0, keepdims=True)

    main, tail = _token_specs(seq, d)
    return _call("ln_ffn_bwd", body, (r // br,),
                 [(h1, _rows(br, d)), (f, _rows(br, d)), (tgt, main), (tgt, tail),
                  (g.reshape(1, d), _whole((1, d))), (b.reshape(1, d), _whole((1, d)))],
                 [(_sds((r, d), F32), _rows(br, d)), (_sds((r, d), MXU_DTYPE), _rows(br, d)),
                  (_sds((1, d), F32), _whole((1, d))), (_sds((1, d), F32), _whole((1, d)))], sem=("arbitrary",))


def _attn_fwd(name, q, k, v, scale, cum_b=None, cum_t=None):
    (qa, qg), (ka, kg), (va, vg) = q, k, v
    r = qa.shape[0]
    tq, tk = ATT_TQ, ATT_TK
    nq, nk = r // tq, r // tk
    bias = cum_b is not None

    def body(*refs):
        if bias:
            q_ref, k_ref, vt_ref, cb_ref, ct_ref, o_ref, ob_ref, lse_ref = refs
        else:
            q_ref, k_ref, vt_ref, o_ref, ob_ref, lse_ref = refs
        i = pl.program_id(1)
        qs = [q_ref[:, _hs(hh)] for hh in range(hg)]
        cqs = [ct_ref[hh] for hh in range(hg)] if bias else None
        diff = lax.broadcasted_iota(jnp.int32, (tk, tq), 0) - lax.broadcasted_iota(jnp.int32, (tk, tq), 1)

        def step(j, carry, masked):
            keys = pl.ds(pl.multiple_of(j * tk, tk), tk)
            out = []
            for hh in range(hg):
                m, l, acc = carry[hh]
                kt = k_ref[keys, _hs(hh)]
                s = lax.dot_general(kt, qs[hh], NT, preferred_element_type=F32) * scale
                if bias:
                    s = s + (cqs[hh] - cb_ref[keys, hh * HP:hh * HP + 1])
                if masked:
                    s = jnp.where(diff <= i * tq - j * tk, s, NEG_INF)
                m_new = jnp.maximum(m, jnp.max(s, axis=0, keepdims=True))
                p = jnp.exp(s - m_new)
                a = jnp.exp(m - m_new)
                l = a * l + jnp.sum(p, axis=0, keepdims=True)
                acc = a * acc + jnp.dot(vt_ref[j, _hs(hh), :], p.astype(kt.dtype), preferred_element_type=F32)
                out.append((m_new, l, acc))
            return tuple(out)

        n_clear = (i * tq + 1) // tk
        n_all = ((i + 1) * tq - 1) // tk + 1
        carry = tuple((jnp.full((1, tq), NEG_INF, F32), jnp.zeros((1, tq), F32), jnp.zeros((HP, tq), F32))
                      for _ in range(hg))
        carry = lax.fori_loop(0, n_clear, lambda j, c: step(j, c, False), carry)
        carry = lax.fori_loop(n_clear, n_all, lambda j, c: step(j, c, True), carry)
        for hh in range(hg):
            m, l, acc = carry[hh]
            o = (acc / l).T
            o_ref[:, _hs(hh)] = o
            ob_ref[:, _hs(hh)] = o.astype(ob_ref.dtype)
            lse_ref[hh] = m + jnp.log(l)

    hg = ATT_HEADS
    w = hg * HP
    gpw = HW // w
    tile = lambda g: pl.BlockSpec((tq, w), lambda h, i: (i, g * gpw + h))
    res = lambda g: pl.BlockSpec((r, w), lambda h, i: (0, g * gpw + h))
    v_t = _key_tiles_transposed(name + "_vt", va, vg)
    ins = [(qa, tile(qg)), (ka, res(kg)), (v_t, pl.BlockSpec((nk, w, tk), lambda h, i: (0, h, 0)))]
    if bias:
        ins += [(cum_b, res(0)),
                (cum_t.reshape(HEADS, nq, 1, tq), pl.BlockSpec((hg, None, 1, tq), lambda h, i: (h, i, 0, 0)))]
    outs = [(_sds((r, HW), F32), tile(0)), (_sds((r, HW), MXU_DTYPE), tile(0)),
            (_sds((HEADS, nq, 1, tq), F32), pl.BlockSpec((hg, None, 1, tq), lambda h, i: (h, i, 0, 0)))]
    o, ob, lse = _call(name, body, (gpw, nq), ins, outs, sem=("parallel", "parallel"))
    return o, ob, lse.reshape(HEADS, r)


def _key_tiles_transposed(name, a, group):
    r = a.shape[0]
    tk = ATT_TK

    def body(x_ref, o_ref):
        for h in range(HEADS):
            o_ref[_hs(h), :] = x_ref[:, _hs(h)].astype(F32).T.astype(o_ref.dtype)

    return _call(name, body, (r // tk,),
                 [(a, pl.BlockSpec((tk, HW), lambda j: (j, group)))],
                 [(_sds((r // tk, HW, tk), a.dtype), pl.BlockSpec((None, HW, tk), lambda j: (j, 0, 0)))],
                 sem=("parallel",))[0]


def _attn_delta(name, do, o, after=()):
    r = do.shape[0]
    br = ROW_BLOCK

    def body(do_ref, o_ref, d_ref, dob_ref):
        lane = _lane_iota((br, HP))
        d = jnp.zeros((br, HP), F32)
        for h in range(HEADS):
            dh = do_ref[:, _hs(h)]
            d = jnp.where(lane == h, jnp.sum(dh * o_ref[:, _hs(h)], axis=1, keepdims=True), d)
            dob_ref[:, _hs(h)] = dh.astype(dob_ref.dtype)
        d_ref[...] = d.T[0:HEADS, :]

    wide = _rows(br, HW)
    return _call(name, body, (r // br,), [(do, wide), (o, wide)],
                 [(_sds((HEADS, r), F32), pl.BlockSpec((HEADS, br), lambda i: (0, i))), (_sds((r, HW), MXU_DTYPE), wide)],
                 sem=("parallel",), after=after)


def _attn_bwd(name, q, k, v, do_b, lse_t, delta_t, scale, cum_b=None, cum_t=None):
    (qa, qg), (ka, kg), (va, vg) = q, k, v
    r = qa.shape[0]
    tq, tk = ATT_TQ, ATT_TK
    nq, nk = r // tq, r // tk
    bias = cum_b is not None

    def body(*refs):
        if bias:
            (q_ref, k_ref, v_ref, do_ref, lse_ref, dl_ref, cb_ref, ct_ref,
             dq_ref, dk_ref, dv_ref, dcq_ref, dck_ref, dqt_ref) = refs
        else:
            q_ref, k_ref, v_ref, do_ref, lse_ref, dl_ref, dq_ref, dk_ref, dv_ref, dqt_ref = refs
        j = pl.program_id(1)

        @pl.when(j == 0)
        def _():
            dqt_ref[...] = jnp.zeros_like(dqt_ref)
            if bias:
                dcq_ref[...] = jnp.zeros_like(dcq_ref)

        kts = [k_ref[:, _hs(hh)] for hh in range(hg)]
        vts = [v_ref[:, _hs(hh)] for hh in range(hg)]
        k_trs = [kt.astype(F32).T.astype(kt.dtype) for kt in kts]
        cks = [cb_ref[:, hh * HP:hh * HP + 1] for hh in range(hg)] if bias else None
        diff = lax.broadcasted_iota(jnp.int32, (tk, tq), 0) - lax.broadcasted_iota(jnp.int32, (tk, tq), 1)

        def step(i, carry, masked):
            rows = pl.ds(pl.multiple_of(i * tq, tq), tq)
            out = []
            for hh in range(hg):
                dk_acc, dv_acc, dck_acc = carry[hh]
                qt = q_ref[rows, _hs(hh)]
                dot = do_ref[rows, _hs(hh)]
                s = lax.dot_general(kts[hh], qt, NT, preferred_element_type=F32) * scale
                if bias:
                    s = s + (ct_ref[hh, i] - cks[hh])
                if masked:
                    s = jnp.where(diff <= i * tq - j * tk, s, NEG_INF)
                p = jnp.exp(s - lse_ref[hh, i])
                dp = lax.dot_general(vts[hh], dot, NT, preferred_element_type=F32)
                ds = p * (dp - dl_ref[hh, i])
                pb = p.astype(dot.dtype)
                dsb = ds.astype(qt.dtype)
                dv_acc = dv_acc + jnp.dot(pb, dot, preferred_element_type=F32)
                dk_acc = dk_acc + jnp.dot(dsb, qt, preferred_element_type=F32)
                dqt_ref[hh, i] += jnp.dot(k_trs[hh], dsb, preferred_element_type=F32)
                if bias:
                    dcq_ref[hh, i] += jnp.sum(ds, axis=0, keepdims=True)
                    dck_acc = dck_acc - jnp.sum(ds, axis=1, keepdims=True)
                out.append((dk_acc, dv_acc, dck_acc))
            return tuple(out)

        i_first = (j * tk) // tq
        i_clear = jnp.minimum(((j + 1) * tk + tq - 2) // tq, nq)
        carry = tuple((jnp.zeros((tk, HP), F32), jnp.zeros((tk, HP), F32), jnp.zeros((tk, 1), F32)) for _ in range(hg))
        carry = lax.fori_loop(i_first, i_clear, lambda i, c: step(i, c, True), carry)
        carry = lax.fori_loop(i_clear, nq, lambda i, c: step(i, c, False), carry)
        for hh in range(hg):
            dk_acc, dv_acc, dck_acc = carry[hh]
            dk_ref[:, _hs(hh)] = dk_acc * scale
            dv_ref[:, _hs(hh)] = dv_acc
            if bias:
                dck_ref[:, _hs(hh)] = jnp.broadcast_to(dck_acc, (tk, HP))

        @pl.when(j == nk - 1)
        def _():
            for hh in range(hg):
                for i in range(nq):
                    dq_ref[i * tq:(i + 1) * tq, _hs(hh)] = dqt_ref[hh, i].T * scale

    hg = ATT_HEADS
    w = hg * HP
    gpw = HW // w
    res = lambda g: pl.BlockSpec((r, w), lambda h, j: (0, g * gpw + h))
    tile = lambda g: pl.BlockSpec((tk, w), lambda h, j: (j, g * gpw + h))
    rowv = pl.BlockSpec((hg, nq, 1, tq), lambda h, j: (h, 0, 0, 0))
    as_rows = lambda a: a.reshape(HEADS, nq, 1, tq)
    ins = [(qa, res(qg)), (ka, tile(kg)), (va, tile(vg)), (do_b, res(0)), (as_rows(lse_t), rowv), (as_rows(delta_t), rowv)]
    outs = [(_sds((r, HW), F32), res(0)), (_sds((r, HW), F32), tile(0)), (_sds((r, HW), F32), tile(0))]
    if bias:
        ins += [(cum_b, tile(0)), (as_rows(cum_t), rowv)]
        outs += [(_sds((HEADS, nq, 1, tq), F32), rowv), (_sds((r, HW), F32), tile(0))]
    res_out = _call(name, body, (gpw, nk), ins, outs, scratch=[pltpu.VMEM((hg, nq, HP, tq), F32)],
                    sem=("parallel", "arbitrary"))
    if bias:
        dq, dk, dv, dcq, dck = res_out
        return dq, dk, dv, dcq.reshape(HEADS, r), dck
    return res_out


MESH_ID = pl.DeviceIdType.MESH
ANY = pl.BlockSpec(memory_space=pl.ANY)


def _allgather(name, shards):
    n = len(shards)

    def body(*refs):
        x_refs, out_refs = refs[:n], refs[n:2 * n]
        send_sems, recv_sems, local_sems = refs[2 * n:]
        x, y, c = lax.axis_index("x"), lax.axis_index("y"), lax.axis_index("c")
        me, sibling = (x, y, c), (x, y, 1 - c)
        chips = [(1 - x, y), (x, 1 - y), (1 - x, 1 - y)]

        def slot(ti, px, py, pc):
            return out_refs[ti].at[4 * px + 2 * py + pc]

        def copy(ti, k, block, to, src=None):
            return pltpu.make_async_remote_copy(
                src_ref=slot(ti, *block) if src is None else src, dst_ref=slot(ti, *block),
                send_sem=send_sems.at[ti, k], recv_sem=recv_sems.at[ti, k], device_id=to, device_id_type=MESH_ID)

        mine = [pltpu.make_async_copy(x_refs[ti], slot(ti, *me), local_sems.at[ti]) for ti in range(n)]
        for cp in mine:
            cp.start()
        started = []
        for ti in range(n):
            first = [copy(ti, 0, me, sibling, src=x_refs[ti])]
            first += [copy(ti, 1 + j, me, (*chip, c), src=x_refs[ti]) for j, chip in enumerate(chips)]
            for cp in first:
                cp.start()
            started += first
        for ti in range(n):
            for j, chip in enumerate(chips):
                copy(ti, 1 + j, (*chip, c), me).wait_recv()
                fwd = copy(ti, 4 + j, (*chip, c), sibling)
                fwd.start()
                started.append(fwd)
        for ti in range(n):
            copy(ti, 0, sibling, me).wait_recv()
            for j, chip in enumerate(chips):
                copy(ti, 4 + j, (*chip, 1 - c), me).wait_recv()
        for cp in started:
            cp.wait_send()
        for cp in mine:
            cp.wait()

    return pl.pallas_call(
        body, name=name, out_shape=[_sds((N_DEV,) + s.shape, s.dtype) for s in shards],
        in_specs=[ANY] * n, out_specs=[ANY] * n,
        scratch_shapes=[pltpu.SemaphoreType.DMA((n, 7)), pltpu.SemaphoreType.DMA((n, 7)), pltpu.SemaphoreType.DMA((n,))],
    )(*shards)


HBM = pl.BlockSpec(memory_space=pltpu.HBM)
SEM = pl.BlockSpec(memory_space=pltpu.SEMAPHORE)
EFFECT = pltpu.SideEffectType.DATAFLOW_SIDE_EFFECTING
N_PEER = N_DEV - 1


def _my_id():
    return 4 * lax.axis_index("x") + 2 * lax.axis_index("y") + lax.axis_index("c")


def _peers():
    x, y, c = lax.axis_index("x"), lax.axis_index("y"), lax.axis_index("c")
    out = []
    for k in range(1, N_DEV):
        px, py, pc = (1 - x if k & 4 else x, 1 - y if k & 2 else y, 1 - c if k & 1 else c)
        out.append(((px, py, pc), 4 * px + 2 * py + pc))
    return out


def _push_copies(src_refs, land_refs, send_sems, recv_sems, scatter, landing):
    me = _my_id()
    out = []
    for ti, (src, land) in enumerate(zip(src_refs, land_refs)):
        for k, (dev, pid) in enumerate(_peers()):
            out.append(pltpu.make_async_remote_copy(
                src_ref=src.at[pid] if scatter else src, dst_ref=land.at[pid if landing else me],
                send_sem=send_sems.at[ti * N_PEER + k], recv_sem=recv_sems.at[ti * N_PEER + k],
                device_id=dev, device_id_type=MESH_ID))
    return out


def _push_start(name, srcs, scatter, after=None):
    n = len(srcs)
    slot = lambda s: s.shape[1:] if scatter else s.shape
    lands = [lax.empty((N_DEV,) + slot(s), s.dtype) for s in srcs]
    n_after = 0 if after is None else 1

    def body(*refs):
        src_refs, land_refs = refs[:n], refs[n:2 * n]
        send_sems, recv_sems = refs[2 * n + n_after], refs[2 * n + n_after + 1]
        token = refs[-1]
        for cp in _push_copies(src_refs, land_refs, send_sems, recv_sems, scatter, False):
            cp.start()
        token[...] = jnp.zeros_like(token)

    hbm = lambda a: pltpu.with_memory_space_constraint(a, pltpu.HBM)
    operands = [hbm(a) for a in srcs + lands] + ([after] if n_after else [])
    res = pl.pallas_call(
        body, name=name,
        out_shape=[pltpu.SemaphoreType.DMA((n * N_PEER,)), pltpu.SemaphoreType.DMA((n * N_PEER,))]
        + [pltpu.HBM(a.shape, a.dtype) for a in srcs + lands] + [_sds((8, 128), F32)],
        in_specs=[HBM] * (2 * n) + [ANY] * n_after,
        out_specs=[SEM, SEM] + [HBM] * (2 * n) + [pl.BlockSpec(memory_space=pltpu.VMEM)],
        input_output_aliases={i: 2 + i for i in range(2 * n)},
        compiler_params=pltpu.CompilerParams(has_side_effects=EFFECT),
    )(*operands)
    return (res[0], res[1], list(res[2:2 + n]), list(res[2 + n:2 + 2 * n]), scatter), res[-1]


def _push_wait(name, handle, after):
    send_sems, recv_sems, srcs, lands, scatter = handle
    n = len(srcs)

    def body(*refs):
        src_refs, land_refs = refs[:n], refs[n:2 * n]
        s_sems, r_sems = refs[2 * n], refs[2 * n + 1]
        for cp in _push_copies(src_refs, land_refs, s_sems, r_sems, scatter, True):
            cp.wait_send()
            cp.wait_recv()

    res = pl.pallas_call(
        body, name=name,
        out_shape=[pltpu.HBM(a.shape, a.dtype) for a in srcs + lands],
        in_specs=[HBM] * (2 * n) + [SEM, SEM, ANY], out_specs=[HBM] * (2 * n),
        input_output_aliases={i: i for i in range(2 * n)},
        compiler_params=pltpu.CompilerParams(has_side_effects=EFFECT),
    )(*srcs, *lands, send_sems, recv_sems, after)
    return list(res[n:])


def _adamw(name, parts, w, m, v, own=None):
    r, c = w.shape
    br = _pick(r, 256, 16)
    has_own = own is not None

    def body(*refs):
        if has_own:
            p_ref, own_ref, w_ref, m_ref, v_ref, g_ref, d_ref, nm_ref, nv_ref = refs
            me = _my_id()
            mine = own_ref[...].astype(F32)
        else:
            p_ref, w_ref, m_ref, v_ref, g_ref, d_ref, nm_ref, nv_ref = refs
        g = None
        for k in range(N_DEV):
            t = p_ref[k].astype(F32)
            if has_own:
                t = jnp.where(me == k, mine, t)
            g = t if g is None else g + t
        mm = ADAM_B1 * m_ref[...] + (1.0 - ADAM_B1) * g
        vv = ADAM_B2 * v_ref[...] + (1.0 - ADAM_B2) * (g * g)
        m_hat = mm / (1.0 - ADAM_B1 ** ADAM_STEP)
        v_hat = vv / (1.0 - ADAM_B2 ** ADAM_STEP)
        g_ref[...] = g
        d_ref[...] = -ADAM_LR * (m_hat / (jnp.sqrt(v_hat) + ADAM_EPS) + ADAM_WD * w_ref[...])
        nm_ref[...] = mm
        nv_ref[...] = vv

    spec = _rows(br, c)
    out = (_sds((r, c), F32), spec)
    ins = [(parts, pl.BlockSpec((N_DEV, br, c), lambda i: (0, i, 0)))] + ([(own, spec)] if has_own else [])
    return _call(name, body, (r // br,), ins + [(w, spec), (m, spec), (v, spec)], [out] * 4, sem=("parallel",))


def _pad_head_cols(w, d):
    k = w.shape[0]
    return jnp.pad(w.reshape(k, HEADS, d), ((0, 0), (0, 0), (0, HP - d))).reshape(k, HW)


def _unpad_head_cols(wp, d):
    k = wp.shape[0]
    return wp.reshape(k, HEADS, HP)[:, :, :d].reshape(k, HEADS * d)


def _pad_head_rows(w, d):
    n = w.shape[1]
    return jnp.pad(w.reshape(HEADS, d, n), ((0, 0), (0, HP - d), (0, 0))).reshape(HW, n)


def _unpad_head_rows(wp, d):
    n = wp.shape[1]
    return wp.reshape(HEADS, HP, n)[:, :d, :].reshape(HEADS * d, n)


IN_SEGS = (("q", Q_RANK), ("kv", KV_RANK), ("kr", ROPE), ("fq", FOX_W), ("fk", FOX_W), ("fv", FOX_W),
           ("fl", HEADS), ("gate", 2 * D_MODEL))


def _split_w_in(w):
    seg = {}
    o = 0
    for nm, wd in IN_SEGS:
        seg[nm] = w[:, o:o + wd]
        o += wd
    d = w.shape[0]
    z = lambda n: jnp.zeros((d, n), w.dtype)
    fused = jnp.concatenate([_pad_head_cols(seg[nm], FOX_DIM) for nm in ("fq", "fk", "fv")], axis=1)
    last = jnp.concatenate([seg["fl"], z(LANE_PE - HEADS), seg["kr"], z(HP - LANE_PE - ROPE)], axis=1)
    rest = jnp.concatenate([seg["q"], seg["kv"], last, z(R_GATE - R_LAST - HP), seg["gate"]], axis=1)
    return fused, rest


def _merge_w_in(fused, rest):
    f = [_unpad_head_cols(fused[:, i * HW:(i + 1) * HW], FOX_DIM) for i in range(3)]
    last = rest[:, R_LAST:R_LAST + HP]
    return jnp.concatenate([rest[:, R_QLAT:R_LAST], last[:, LANE_PE:LANE_PE + ROPE], f[0], f[1], f[2],
                            last[:, LANE_FL:LANE_FL + HEADS], rest[:, R_GATE:]], axis=1)


def _split_w_kv(w):
    k = w.shape[0]
    w3 = w.reshape(k, HEADS, NOPE + V_DIM)
    padl = lambda a: jnp.pad(a, ((0, 0), (0, 0), (0, HP - a.shape[-1]))).reshape(k, HW)
    return padl(w3[..., :NOPE]), padl(w3[..., NOPE:])


def _merge_w_kv(wk, wv):
    k = wk.shape[0]
    return jnp.concatenate([wk.reshape(k, HEADS, HP)[..., :NOPE], wv.reshape(k, HEADS, HP)[..., :V_DIM]],
                           axis=-1).reshape(k, HEADS * (NOPE + V_DIM))


class _NoComm:
    first_token = ()

    def late_weights(self, after):
        return {}

    def send(self, name, grads):
        return ()


def _local_step(x, tgt, p, comm=_NoComm()):
    seq = x.shape[0]
    r = -(-(N_META + seq) // ROW_ALIGN) * ROW_ALIGN
    cd = MXU_DTYPE
    p = dict(p)

    w_f, w_r = _split_w_in(p["w_in"])
    w_q = _pad_head_cols(p["w_q_up"], QK_DIM)
    w_k, w_v = _split_w_kv(p["w_kv_up"])

    pos = jnp.arange(r, dtype=F32)
    inv_freq = ROPE_THETA ** (-jnp.arange(HALF, dtype=F32) / HALF)
    ang = pos[:, None] * inv_freq[None, :]
    cos_t = jnp.tile(jnp.cos(ang), (1, HP // HALF))
    sin_t = jnp.tile(jnp.sin(ang), (1, HP // HALF))
    bf_row = jnp.zeros((1, HP), F32).at[0, LANE_FL:LANE_FL + HEADS].set(p["b_forget"])

    h0, h0b = _ln_emb_fwd(x, p["meta_tokens"], p["ln_emb_g"], p["ln_emb_b"], r, after=comm.first_token)
    proj_f = _matmul("in_proj_f", h0b, w_f, out_dtype=cd)
    proj_r = _matmul("in_proj_r", h0b, w_r)
    ql = _rms_fwd("q_norm_fwd", proj_r, R_QLAT // Q_RANK, Q_RANK, p["q_norm_g"])
    kvl = _rms_fwd("kv_norm_fwd", proj_r, R_KVLAT // KV_RANK, KV_RANK, p["kv_norm_g"])
    q_raw = _matmul("q_up", ql, w_q)
    k_part = _matmul("k_up", kvl, w_k)
    v_mla = _matmul("v_up", kvl, w_v, out_dtype=cd)
    q_mla, k_mla = _rope_fwd(q_raw, k_part, proj_r, cos_t, sin_t)
    o_mla, o_mla_b, lse_mla = _attn_fwd("mla_fwd", (q_mla, 0), (k_mla, 0), (v_mla, 0), QK_DIM ** -0.5)

    cum, cum_t = _forget_fwd(proj_r, bf_row)
    o_fox, o_fox_b, lse_fox = _attn_fwd("fox_fwd", (proj_f, 0), (proj_f, 1), (proj_f, 2), FOX_DIM ** -0.5, cum, cum_t)

    p.update(comm.late_weights(o_fox_b))
    w_bm = _pad_head_rows(p["w_branch_mla"], V_DIM)
    w_bf = _pad_head_rows(p["w_branch_fox"], FOX_DIM)
    bm = _matmul("branch_mla", o_mla_b, w_bm)
    bfx = _matmul("branch_fox", o_fox_b, w_bf)
    merged = _gate_fwd(proj_r, p["b_gate"], bm, bfx)
    mix = _matmul("out_proj", merged, p["w_out"])
    h1, h1b = _ln_fwd("ln_mix_fwd", h0, mix, p["ln_mix_g"], p["ln_mix_b"])
    up = _matmul("ffn_up", h1b, p["w_ffn_up"])
    act = _glu_fwd(up, p["conv_w"], p["conv_b"])
    f = _matmul("ffn_down", act, p["w_ffn_down"])
    loss = _ln_ffn_loss(h1, f, tgt, p["ln_ffn_g"], p["ln_ffn_b"])

    g = {}
    dz2, dz2b, g["ln_ffn_g"], g["ln_ffn_b"] = _ln_ffn_bwd(h1, f, tgt, p["ln_ffn_g"], p["ln_ffn_b"])
    d_act = _matmul("ffn_down_dx", dz2b, p["w_ffn_down"], tb=True)
    g["w_ffn_down"] = _matmul("ffn_down_dw", act, dz2b, ta=True, out_dtype=cd)
    dgate, dval, dcw, g["conv_b"] = _glu_bwd_gate(up, p["conv_w"], p["conv_b"], d_act)
    g["conv_w"] = dcw[:3]
    d_up = _glu_bwd_conv(dgate, dval, p["conv_w"])
    dh1 = _matmul("ffn_up_dx", d_up, p["w_ffn_up"], tb=True, addend=dz2, alpha=ALPHA)
    g["w_ffn_up"] = _matmul("ffn_up_dw", h1b, d_up, ta=True, out_dtype=cd)
    sent = comm.send("ffn", {n: g[n] for n in ("w_ffn_down", "w_ffn_up", "conv_w")})
    dz1, dz1b, g["ln_mix_g"], g["ln_mix_b"] = _ln_bwd("ln_mix_bwd", h0, mix, dh1, p["ln_mix_g"], after=sent)
    dmerged = _matmul("out_proj_dx", dz1b, p["w_out"], tb=True)
    g["w_out"] = _matmul("out_proj_dw", merged, dz1b, ta=True, out_dtype=cd)
    d_bm, d_bf, d_gl, g["b_gate"] = _gate_bwd(proj_r, p["b_gate"], bm, bfx, dmerged)
    d_o_mla = _matmul("branch_mla_dx", d_bm, w_bm, tb=True)
    g["w_branch_mla"] = _unpad_head_rows(_matmul("branch_mla_dw", o_mla_b, d_bm, ta=True, out_dtype=cd), V_DIM)
    d_o_fox = _matmul("branch_fox_dx", d_bf, w_bf, tb=True)
    g["w_branch_fox"] = _unpad_head_rows(_matmul("branch_fox_dw", o_fox_b, d_bf, ta=True, out_dtype=cd), FOX_DIM)

    sent = comm.send("mix", {n: g[n] for n in ("w_out", "w_branch_mla", "w_branch_fox")})
    dl_mla, do_mla_b = _attn_delta("mla_delta", d_o_mla, o_mla, after=sent)
    dq_m, dk_m, dv_m = _attn_bwd("mla_bwd", (q_mla, 0), (k_mla, 0), (v_mla, 0), do_mla_b, lse_mla, dl_mla, QK_DIM ** -0.5)
    dl_fox, do_fox_b = _attn_delta("fox_delta", d_o_fox, o_fox)
    dfq, dfk, dfv, dcq, dck = _attn_bwd("fox_bwd", (proj_f, 0), (proj_f, 1), (proj_f, 2), do_fox_b, lse_fox, dl_fox,
                                        FOX_DIM ** -0.5, cum, cum_t)
    dfl, dbf = _forget_bwd(proj_r, bf_row, dcq, dck)
    g["b_forget"] = dbf[:, LANE_FL:LANE_FL + HEADS]

    dq_b, dk_b, dlast = _rope_bwd(dq_m, dk_m, dfl, cos_t, sin_t)
    dv_b = dv_m.astype(cd)
    d_ql = _matmul("q_up_dx", dq_b, w_q, tb=True)
    g["w_q_up"] = _unpad_head_cols(_matmul("q_up_dw", ql, dq_b, ta=True, out_dtype=cd), QK_DIM)
    d_kvl = _matmul("k_up_dx", dk_b, w_k, tb=True)
    d_kvl = _matmul("v_up_dx", dv_b, w_v, tb=True, addend=d_kvl)
    g["w_kv_up"] = _merge_w_kv(_matmul("k_up_dw", kvl, dk_b, ta=True, out_dtype=cd), _matmul("v_up_dw", kvl, dv_b, ta=True, out_dtype=cd))
    d_qlat, g["q_norm_g"] = _rms_bwd("q_norm_bwd", proj_r, R_QLAT // Q_RANK, Q_RANK, d_ql, p["q_norm_g"])
    d_kvlat, g["kv_norm_g"] = _rms_bwd("kv_norm_bwd", proj_r, R_KVLAT // KV_RANK, KV_RANK, d_kvl, p["kv_norm_g"])
    dproj_f = jnp.concatenate([dfq.astype(cd), dfk.astype(cd), dfv.astype(cd)], axis=1)
    dproj_r = jnp.concatenate([d_qlat, d_kvlat, dlast, jnp.zeros((r, R_GATE - R_LAST - HP), cd), d_gl], axis=1)
    g["w_in"] = _merge_w_in(_matmul("in_proj_f_dw", h0b, dproj_f, ta=True, out_dtype=cd), _matmul("in_proj_r_dw", h0b, dproj_r, ta=True, out_dtype=cd))
    sent = comm.send("in", {n: g[n] for n in ("w_in", "w_q_up", "w_kv_up")})
    dh0 = _matmul("in_proj_f_dx", dproj_f, w_f, tb=True, addend=dz1, alpha=ALPHA, after=sent)
    dh0 = _matmul("in_proj_r_dx", dproj_r, w_r, tb=True, addend=dh0)
    grad_x, d_meta, g["ln_emb_g"], g["ln_emb_b"] = _ln_emb_bwd(x, p["meta_tokens"], dh0, p["ln_emb_g"])
    return loss, grad_x, d_meta, g


BIG = (("w_in", 1), ("w_q_up", 1), ("w_kv_up", 1), ("w_branch_mla", 1), ("w_branch_fox", 1), ("w_out", 0),
       ("w_ffn_up", 1), ("w_ffn_down", 0))
SMALL_SHARDED = (("meta_tokens", 1), ("conv_w", 1))
EARLY = ("w_in", "w_q_up", "w_kv_up", "meta_tokens", "conv_w")
LATE = ("w_branch_mla", "w_branch_fox", "w_out", "w_ffn_up", "w_ffn_down")
REPLICATED = ("ln_emb_g", "ln_emb_b", "b_gate", "b_forget", "q_norm_g", "kv_norm_g", "ln_mix_g", "ln_mix_b",
              "conv_b", "ln_ffn_g", "ln_ffn_b")
PACK_COLS = 1024


def _pack(flat_list):
    cat = jnp.concatenate(flat_list)
    n = cat.shape[0]
    rows = -(-n // (8 * PACK_COLS)) * 8
    return jnp.pad(cat, (0, rows * PACK_COLS - n)).reshape(rows, PACK_COLS)


def _gathered_full(g3, axis):
    n, r, c = g3.shape
    if axis == 0:
        return g3.reshape(n * r, c)
    return g3.transpose(1, 0, 2).reshape(r, n * c)


def _shard_major(full, axis):
    r, c = full.shape
    if axis == 0:
        return full.reshape(N_DEV, r // N_DEV, c)
    return full.reshape(r, N_DEV, c // N_DEV).transpose(1, 0, 2)


def kernel(x, meta_tokens, ln_emb_g, ln_emb_b, w_in, b_gate, b_forget, q_norm_g, w_q_up, kv_norm_g, w_kv_up, w_branch_mla, w_branch_fox, w_out, ln_mix_g, ln_mix_b, w_ffn_up, conv_w, conv_b, w_ffn_down, ln_ffn_g, ln_ffn_b, loss_target, m_meta_tokens, m_ln_emb_g, m_ln_emb_b, m_w_in, m_b_gate, m_b_forget, m_q_norm_g, m_w_q_up, m_kv_norm_g, m_w_kv_up, m_w_branch_mla, m_w_branch_fox, m_w_out, m_ln_mix_g, m_ln_mix_b, m_w_ffn_up, m_conv_w, m_conv_b, m_w_ffn_down, m_ln_ffn_g, m_ln_ffn_b, v_meta_tokens, v_ln_emb_g, v_ln_emb_b, v_w_in, v_b_gate, v_b_forget, v_q_norm_g, v_w_q_up, v_kv_norm_g, v_w_kv_up, v_w_branch_mla, v_w_branch_fox, v_w_out, v_ln_mix_g, v_ln_mix_b, v_w_ffn_up, v_conv_w, v_conv_b, v_w_ffn_down, v_ln_ffn_g, v_ln_ffn_b):
    names = ("meta_tokens", "ln_emb_g", "ln_emb_b", "w_in", "b_gate", "b_forget", "q_norm_g", "w_q_up", "kv_norm_g",
             "w_kv_up", "w_branch_mla", "w_branch_fox", "w_out", "ln_mix_g", "ln_mix_b", "w_ffn_up", "conv_w", "conv_b",
             "w_ffn_down", "ln_ffn_g", "ln_ffn_b")
    w_args = (meta_tokens, ln_emb_g, ln_emb_b, w_in, b_gate, b_forget, q_norm_g, w_q_up, kv_norm_g, w_kv_up,
              w_branch_mla, w_branch_fox, w_out, ln_mix_g, ln_mix_b, w_ffn_up, conv_w, conv_b, w_ffn_down, ln_ffn_g, ln_ffn_b)
    m_args = (m_meta_tokens, m_ln_emb_g, m_ln_emb_b, m_w_in, m_b_gate, m_b_forget, m_q_norm_g, m_w_q_up, m_kv_norm_g,
              m_w_kv_up, m_w_branch_mla, m_w_branch_fox, m_w_out, m_ln_mix_g, m_ln_mix_b, m_w_ffn_up, m_conv_w, m_conv_b,
              m_w_ffn_down, m_ln_ffn_g, m_ln_ffn_b)
    v_args = (v_meta_tokens, v_ln_emb_g, v_ln_emb_b, v_w_in, v_b_gate, v_b_forget, v_q_norm_g, v_w_q_up, v_kv_norm_g,
              v_w_kv_up, v_w_branch_mla, v_w_branch_fox, v_w_out, v_ln_mix_g, v_ln_mix_b, v_w_ffn_up, v_conv_w, v_conv_b,
              v_w_ffn_down, v_ln_ffn_g, v_ln_ffn_b)
    as2d = lambda a: a.reshape((-1, a.shape[-1])) if a.ndim != 1 else a.reshape(1, -1)
    w = {n: as2d(a) for n, a in zip(names, w_args)}
    m = {n: as2d(a) for n, a in zip(names, m_args)}
    v = {n: as2d(a) for n, a in zip(names, v_args)}
    out_shape = {n: a.shape for n, a in zip(names, w_args)}

    axis_of = dict(BIG + SMALL_SHARDED)
    big = set(n for n, _ in BIG)
    wire = lambda n, a: a.astype(MXU_DTYPE) if n in big else a
    my_id = _my_id()
    slot_is_mine = lax.broadcasted_iota(jnp.int32, (N_DEV, 1, 1), 0) == my_id

    early = _allgather("gather_early", [wire(n, w[n]) for n in EARLY])
    p = {n: _gathered_full(g3, axis_of[n]) for n, g3 in zip(EARLY, early)}
    for n in REPLICATED:
        p[n] = w[n].reshape(-1)
    late_src = [wire(n, w[n]) for n in LATE]
    late_handle, late_token = _push_start("gather_late_start", late_src, False, after=early[0])
    sent = {}

    class Comm:
        first_token = (late_token,)

        def late_weights(self, after):
            lands = _push_wait("gather_late_wait", late_handle, after)
            return {n: _gathered_full(jnp.where(slot_is_mine, own[None], land), axis_of[n])
                    for n, own, land in zip(LATE, late_src, lands)}

        def send(self, name, grads):
            names_ = tuple(grads)
            parts = [_shard_major(grads[n], axis_of[n]).astype(MXU_DTYPE) for n in names_]
            handle, token = _push_start("send_" + name + "_start", parts, True)
            sent[name] = (names_, parts, handle)
            return (token,)

    loss_part, grad_x, d_meta, g = _local_step(x[0], loss_target[0], p, Comm())
    grad_x = grad_x[None]

    small = _pack([d_meta.reshape(-1)] + [g[n].reshape(-1) for n in REPLICATED] + [loss_part.reshape(-1)])
    small_handle, small_token = _push_start("send_small_start", [small], False)

    res = {}
    prev = small_token
    for name, (names_, parts, handle) in sent.items():
        lands = _push_wait("send_" + name + "_wait", handle, prev)
        for n, part, land in zip(names_, parts, lands):
            own = lax.dynamic_index_in_dim(part, my_id, axis=0, keepdims=False)
            res[n] = _adamw("adamw_" + n, land, w[n], m[n], v[n], own=own)
            prev = res[n][0]
    small_all = _push_wait("send_small_wait", small_handle, prev)[0]
    head = jnp.zeros((d_meta.size,), F32)
    rep_w = _pack([head] + [w[n].reshape(-1) for n in REPLICATED])
    rep_m = _pack([head] + [m[n].reshape(-1) for n in REPLICATED])
    rep_v = _pack([head] + [v[n].reshape(-1) for n in REPLICATED])
    rep_res = _adamw("adamw_replicated", small_all, rep_w, rep_m, rep_v, own=small)
    off = d_meta.size
    for n in REPLICATED:
        sz = w[n].size
        res[n] = tuple(a.reshape(-1)[off:off + sz] for a in rep_res)
        off += sz
    loss = rep_res[0].reshape(-1)[off]
    cols = w["meta_tokens"].shape[1]
    meta_rows = lambda a: a.reshape(a.shape[:-2] + (-1,))[..., :d_meta.size].reshape(a.shape[:-2] + d_meta.shape)
    my_cols = lambda a: lax.dynamic_slice_in_dim(a, my_id * cols, cols, axis=a.ndim - 1)
    res["meta_tokens"] = _adamw("adamw_meta_tokens", my_cols(meta_rows(small_all)), w["meta_tokens"],
                                m["meta_tokens"], v["meta_tokens"], own=my_cols(d_meta))

    outs = [loss, grad_x]
    for idx in range(4):
        outs += [res[n][idx].reshape(out_shape[n]) for n in names]
    return tuple(outs)
```

```python
import jax
import jax.numpy as jnp
from jax import lax
from jax.experimental import pallas as pl
from jax.experimental.pallas import tpu as pltpu

F32 = jnp.float32
BF16 = jnp.bfloat16
MXU_DTYPE = BF16

N_DEV = 8
N_META = 16
D_MODEL = 1024
HEADS = 8
Q_RANK = 384
KV_RANK = 128
NOPE = 64
ROPE = 32
HALF = ROPE // 2
QK_DIM = NOPE + ROPE
V_DIM = 64
FOX_DIM = 64
FOX_W = HEADS * FOX_DIM
D_FF = 2816
ROPE_THETA = 10000.0
LN_EPS = 1e-5
RMS_EPS = 1e-6
ALPHA = 2.0 ** 0.25
MLA_SCALE = QK_DIM ** -0.5
FOX_SCALE = FOX_DIM ** -0.5
NEG_INF = -1e30

HP = 128
HW = HEADS * HP
F_W = 3 * HW
R_QLAT = 0
R_KVLAT = Q_RANK
R_LAST = R_KVLAT + KV_RANK
R_GATE = D_MODEL
R_W = R_GATE + 2 * D_MODEL
LANE_FL = 0
LANE_PE = NOPE

ADAM_LR = 0.001
ADAM_B1 = 0.9
ADAM_B2 = 0.999
ADAM_EPS = 1e-08
ADAM_WD = 0.01
ADAM_STEP = 10

ROW_BLOCK = 256
ATT_TQ = 768
ATT_TK = 256
ATT_HEADS = 2
ROW_ALIGN = 768
MM_BLOCK_CAP = 1408
VMEM_LIMIT = 56 * 1024 * 1024
HIGHEST = lax.Precision.HIGHEST
NT = (((1,), (1,)), ((), ()))
TN = (((0,), (0,)), ((), ()))


def _params(sem=None):
    return pltpu.CompilerParams(dimension_semantics=sem, vmem_limit_bytes=VMEM_LIMIT)


def _call(name, body, grid, ins, outs, scratch=(), sem=None, after=()):
    n_in = len(ins)
    n_tok = len(after)

    def run(*refs):
        body(*refs[:n_in], *refs[n_in + n_tok:])

    tok_spec = pl.BlockSpec((8, 128), lambda *_: (0, 0))
    return pl.pallas_call(
        run, name=name, grid=grid,
        in_specs=[s for _, s in ins] + [tok_spec] * n_tok,
        out_specs=[s for _, s in outs],
        out_shape=[o for o, _ in outs],
        scratch_shapes=list(scratch),
        compiler_params=_params(sem),
    )(*[a for a, _ in ins], *after)


def _sds(shape, dtype):
    return jax.ShapeDtypeStruct(shape, dtype)


def _rows(br, c, cb=0):
    return pl.BlockSpec((br, c), lambda i: (i, cb))


def _whole(shape):
    n = len(shape)
    return pl.BlockSpec(shape, lambda i: (0,) * n)


def _pick(dim, cap, mult):
    best = None
    d = mult
    while d <= min(dim, cap):
        if dim % d == 0:
            best = d
        d += mult
    return best if best is not None else dim


def _hs(h):
    return slice(h * HP, (h + 1) * HP)


def _matmul(name, a, b, *, ta=False, tb=False, out_dtype=F32, addend=None, alpha=1.0, after=()):
    if ta:
        k, m = a.shape
    else:
        m, k = a.shape
    if tb:
        n, k2 = b.shape
    else:
        k2, n = b.shape
    assert k == k2, (name, a.shape, b.shape)
    bm = _pick(m, MM_BLOCK_CAP, 128 if ta else 16)
    bn = _pick(n, MM_BLOCK_CAP, 128)
    bk = _pick(k, MM_BLOCK_CAP, 128 if (not ta or tb) else 16)
    nk = k // bk
    dims = (((0 if ta else 1,), (1 if tb else 0,)), ((), ()))
    has_add = addend is not None

    def body(*refs):
        a_ref, b_ref = refs[:2]
        add_ref = refs[2] if has_add else None
        o_ref = refs[3 if has_add else 2]

        def finish(r):
            if has_add:
                r = r + alpha * add_ref[...]
            o_ref[...] = r.astype(o_ref.dtype)

        part = lax.dot_general(a_ref[...], b_ref[...], dims, preferred_element_type=F32)
        if nk == 1:
            finish(part)
            return
        acc_ref = refs[-1]
        kk = pl.program_id(2)

        @pl.when(kk == 0)
        def _():
            acc_ref[...] = part

        @pl.when(kk > 0)
        def _():
            acc_ref[...] += part

        @pl.when(kk == nk - 1)
        def _():
            finish(acc_ref[...])

    a_spec = pl.BlockSpec((bk, bm), lambda i, j, l: (l, i)) if ta else pl.BlockSpec((bm, bk), lambda i, j, l: (i, l))
    b_spec = pl.BlockSpec((bn, bk), lambda i, j, l: (j, l)) if tb else pl.BlockSpec((bk, bn), lambda i, j, l: (l, j))
    o_spec = pl.BlockSpec((bm, bn), lambda i, j, l: (i, j))
    ins = [(a, a_spec), (b, b_spec)]
    if has_add:
        ins.append((addend, o_spec))
    return _call(name, body, (m // bm, n // bn, nk), ins, [(_sds((m, n), out_dtype), o_spec)],
                 scratch=[pltpu.VMEM((bm, bn), F32)] if nk > 1 else [],
                 sem=("parallel", "parallel", "arbitrary"), after=after)[0]


def _ln_stats(z):
    mu = jnp.mean(z, axis=-1, keepdims=True)
    zc = z - mu
    var = jnp.mean(zc * zc, axis=-1, keepdims=True)
    rstd = lax.rsqrt(var + LN_EPS)
    return zc * rstd, rstd


def _ln_fwd(name, a, res, g, b, after=()):
    r, d = a.shape
    br = ROW_BLOCK
    has_res = res is not None

    def body(*refs):
        if has_res:
            a_ref, r_ref, g_ref, b_ref, y_ref, yb_ref = refs
            z = ALPHA * a_ref[...] + r_ref[...]
        else:
            a_ref, g_ref, b_ref, y_ref, yb_ref = refs
            z = a_ref[...]
        xhat, _ = _ln_stats(z)
        y = xhat * g_ref[...] + b_ref[...]
        y_ref[...] = y
        yb_ref[...] = y.astype(yb_ref.dtype)

    ins = [(a, _rows(br, d))]
    if has_res:
        ins.append((res, _rows(br, d)))
    ins += [(g.reshape(1, d), _whole((1, d))), (b.reshape(1, d), _whole((1, d)))]
    outs = [(_sds((r, d), F32), _rows(br, d)), (_sds((r, d), MXU_DTYPE), _rows(br, d))]
    return _call(name, body, (r // br,), ins, outs, sem=("parallel",), after=after)


def _ln_bwd(name, a, res, dy, g, after=()):
    r, d = a.shape
    br = ROW_BLOCK
    has_res = res is not None

    def body(*refs):
        if has_res:
            a_ref, r_ref, dy_ref, g_ref, dz_ref, dzb_ref, dg_ref, db_ref = refs
            z = ALPHA * a_ref[...] + r_ref[...]
        else:
            a_ref, dy_ref, g_ref, dz_ref, dzb_ref, dg_ref, db_ref = refs
            z = a_ref[...]
        xhat, rstd = _ln_stats(z)
        dyv = dy_ref[...]
        dyg = dyv * g_ref[...]
        m1 = jnp.mean(dyg, axis=-1, keepdims=True)
        m2 = jnp.mean(dyg * xhat, axis=-1, keepdims=True)
        dz = rstd * (dyg - m1 - xhat * m2)
        dz_ref[...] = dz
        dzb_ref[...] = dz.astype(dzb_ref.dtype)

        @pl.when(pl.program_id(0) == 0)
        def _():
            dg_ref[...] = jnp.zeros_like(dg_ref)
            db_ref[...] = jnp.zeros_like(db_ref)

        dg_ref[...] += jnp.sum(dyv * xhat, axis=0, keepdims=True)
        db_ref[...] += jnp.sum(dyv, axis=0, keepdims=True)

    ins = [(a, _rows(br, d))]
    if has_res:
        ins.append((res, _rows(br, d)))
    ins += [(dy, _rows(br, d)), (g.reshape(1, d), _whole((1, d)))]
    outs = [(_sds((r, d), F32), _rows(br, d)), (_sds((r, d), MXU_DTYPE), _rows(br, d)),
            (_sds((1, d), F32), _whole((1, d))), (_sds((1, d), F32), _whole((1, d)))]
    return _call(name, body, (r // br,), ins, outs, sem=("arbitrary",), after=after)


def _rms_fwd(name, proj, cb, width, g):
    r = proj.shape[0]
    br = ROW_BLOCK

    def body(x_ref, g_ref, y_ref):
        x = x_ref[...]
        rstd = lax.rsqrt(jnp.mean(x * x, axis=-1, keepdims=True) + RMS_EPS)
        y_ref[...] = (x * rstd * g_ref[...]).astype(y_ref.dtype)

    return _call(name, body, (r // br,), [(proj, _rows(br, width, cb)), (g.reshape(1, width), _whole((1, width)))],
                 [(_sds((r, width), MXU_DTYPE), _rows(br, width))], sem=("parallel",))[0]


def _rms_bwd(name, proj, cb, width, dy, g):
    r = proj.shape[0]
    br = ROW_BLOCK

    def body(x_ref, dy_ref, g_ref, dx_ref, dg_ref):
        x = x_ref[...]
        rstd = lax.rsqrt(jnp.mean(x * x, axis=-1, keepdims=True) + RMS_EPS)
        nrm = x * rstd
        dyv = dy_ref[...]
        dyg = dyv * g_ref[...]
        dx = rstd * (dyg - nrm * jnp.mean(dyg * nrm, axis=-1, keepdims=True))
        dx_ref[...] = dx.astype(dx_ref.dtype)

        @pl.when(pl.program_id(0) == 0)
        def _():
            dg_ref[...] = jnp.zeros_like(dg_ref)

        dg_ref[...] += jnp.sum(dyv * nrm, axis=0, keepdims=True)

    return _call(name, body, (r // br,),
                 [(proj, _rows(br, width, cb)), (dy, _rows(br, width)), (g.reshape(1, width), _whole((1, width)))],
                 [(_sds((r, width), MXU_DTYPE), _rows(br, width)), (_sds((1, width), F32), _whole((1, width)))],
                 sem=("arbitrary",))


def _lane_iota(shape):
    return lax.broadcasted_iota(jnp.int32, shape, 1)


def _rotary(t, c, s, lane, sign):
    second = pltpu.roll(t, HP - HALF, axis=1)
    first = pltpu.roll(t, HALF, axis=1)
    lo = (lane >= LANE_PE) & (lane < LANE_PE + HALF)
    hi = (lane >= LANE_PE + HALF) & (lane < LANE_PE + ROPE)
    return jnp.where(lo, t * c - sign * second * s, jnp.where(hi, t * c + sign * first * s, t))


def _rope_fwd(q_raw, k_part, proj_r, cos_t, sin_t):
    r = q_raw.shape[0]
    br = ROW_BLOCK

    def body(q_ref, k_ref, t_ref, c_ref, s_ref, qo_ref, ko_ref):
        c = c_ref[...]
        s = s_ref[...]
        lane = _lane_iota((br, HP))
        pe = (lane >= LANE_PE) & (lane < LANE_PE + ROPE)
        kp = jnp.where(pe, _rotary(t_ref[...], c, s, lane, 1.0), 0.0)
        for h in range(HEADS):
            qo_ref[:, _hs(h)] = (_rotary(q_ref[:, _hs(h)], c, s, lane, 1.0) * MLA_SCALE).astype(qo_ref.dtype)
            ko_ref[:, _hs(h)] = (k_ref[:, _hs(h)] + kp).astype(ko_ref.dtype)

    blk = _rows(br, HP)
    wide = _rows(br, HW)
    return _call("rope_fwd", body, (r // br,),
                 [(q_raw, wide), (k_part, wide), (proj_r, _rows(br, HP, R_LAST // HP)), (cos_t, blk), (sin_t, blk)],
                 [(_sds((r, HW), MXU_DTYPE), wide)] * 2, sem=("parallel",))


def _rope_bwd(dq, dk, dfl, cos_t, sin_t):
    r = dq.shape[0]
    br = ROW_BLOCK

    def body(dq_ref, dk_ref, fl_ref, c_ref, s_ref, dqo_ref, dko_ref, dl_ref):
        c = c_ref[...]
        s = s_ref[...]
        lane = _lane_iota((br, HP))
        pe = (lane >= LANE_PE) & (lane < LANE_PE + ROPE)
        acc = jnp.zeros((br, HP), F32)
        for h in range(HEADS):
            dqo_ref[:, _hs(h)] = (_rotary(dq_ref[:, _hs(h)], c, s, lane, -1.0) * MLA_SCALE).astype(dqo_ref.dtype)
            dkh = dk_ref[:, _hs(h)]
            acc = acc + dkh
            dko_ref[:, _hs(h)] = dkh.astype(dko_ref.dtype)
        dl_ref[...] = (jnp.where(pe, _rotary(acc, c, s, lane, -1.0), 0.0) + fl_ref[...]).astype(dl_ref.dtype)

    blk = _rows(br, HP)
    wide = _rows(br, HW)
    return _call("rope_bwd", body, (r // br,),
                 [(dq, wide), (dk, wide), (dfl, blk), (cos_t, blk), (sin_t, blk)],
                 [(_sds((r, HW), MXU_DTYPE), wide), (_sds((r, HW), MXU_DTYPE), wide), (_sds((r, HP), MXU_DTYPE), blk)],
                 sem=("parallel",))


def _log_sigmoid(x):
    return jnp.minimum(x, 0.0) - jnp.log(1.0 + jnp.exp(-jnp.abs(x)))


def _head_lane(x, h, lane):
    return jnp.sum(jnp.where(lane == h, x, 0.0), axis=1, keepdims=True)


def _forget_fwd(proj_r, bf_row):
    r = proj_r.shape[0]
    br = ROW_BLOCK

    def body(t_ref, b_ref, ob_ref, ot_ref, carry_ref):
        @pl.when(pl.program_id(0) == 0)
        def _():
            carry_ref[...] = jnp.zeros_like(carry_ref)

        x = t_ref[...] + b_ref[...]
        lane = _lane_iota(x.shape)
        lf = jnp.where((lane >= LANE_FL) & (lane < LANE_FL + HEADS), _log_sigmoid(x), 0.0)
        tri = (lax.broadcasted_iota(jnp.int32, (br, br), 0) >= lax.broadcasted_iota(jnp.int32, (br, br), 1)).astype(F32)
        cum = jnp.dot(tri, lf, precision=HIGHEST, preferred_element_type=F32) + carry_ref[0:1, :]
        for h in range(HEADS):
            ob_ref[:, _hs(h)] = jnp.broadcast_to(_head_lane(cum, LANE_FL + h, lane), (br, HP))
        ot_ref[...] = cum.T[LANE_FL:LANE_FL + HEADS, :]
        carry_ref[...] = jnp.broadcast_to(cum[br - 1:br, :], carry_ref.shape)

    return _call("forget_fwd", body, (r // br,),
                 [(proj_r, _rows(br, HP, R_LAST // HP)), (bf_row, _whole((1, HP)))],
                 [(_sds((r, HW), F32), _rows(br, HW)), (_sds((HEADS, r), F32), pl.BlockSpec((HEADS, br), lambda i: (0, i)))],
                 scratch=[pltpu.VMEM((8, HP), F32)], sem=("arbitrary",))


def _forget_bwd(proj_r, bf_row, dcq_t, dck_b):
    r = proj_r.shape[0]
    br = ROW_BLOCK
    nb = r // br

    def body(t_ref, b_ref, dcq_ref, dck_ref, o_ref, db_ref, carry_ref):
        @pl.when(pl.program_id(0) == 0)
        def _():
            carry_ref[...] = jnp.zeros_like(carry_ref)
            db_ref[...] = jnp.zeros_like(db_ref)

        lane = _lane_iota((br, HP))
        dc = jnp.concatenate([dcq_ref[...], jnp.zeros((HP - HEADS, br), F32)], axis=0).T
        for h in range(HEADS):
            dc = dc + jnp.where(lane == LANE_FL + h, dck_ref[:, h * HP:h * HP + 1], 0.0)
        triu = (lax.broadcasted_iota(jnp.int32, (br, br), 0) <= lax.broadcasted_iota(jnp.int32, (br, br), 1)).astype(F32)
        dlf = jnp.dot(triu, dc, precision=HIGHEST, preferred_element_type=F32) + carry_ref[0:1, :]
        carry_ref[...] = jnp.broadcast_to(dlf[0:1, :], carry_ref.shape)
        x = t_ref[...] + b_ref[...]
        dfl = jnp.where((lane >= LANE_FL) & (lane < LANE_FL + HEADS), dlf * jax.nn.sigmoid(-x), 0.0)
        o_ref[...] = dfl
        db_ref[...] += jnp.sum(dfl, axis=0, keepdims=True)

    rev = pl.BlockSpec((br, HP), lambda i: (nb - 1 - i, 0))
    return _call("forget_bwd", body, (nb,),
                 [(proj_r, pl.BlockSpec((br, HP), lambda i: (nb - 1 - i, R_LAST // HP))), (bf_row, _whole((1, HP))),
                  (dcq_t, pl.BlockSpec((HEADS, br), lambda i: (0, nb - 1 - i))),
                  (dck_b, pl.BlockSpec((br, HW), lambda i: (nb - 1 - i, 0)))],
                 [(_sds((r, HP), F32), rev), (_sds((1, HP), F32), _whole((1, HP)))],
                 scratch=[pltpu.VMEM((8, HP), F32)], sem=("arbitrary",))


def _gate_fwd(proj_r, b_gate, bm, bfx):
    r, d = bm.shape
    br = ROW_BLOCK
    cb = R_GATE // d

    def body(gm_ref, gf_ref, b1_ref, b2_ref, bm_ref, bf_ref, o_ref):
        g1 = jax.nn.sigmoid(gm_ref[...] + b1_ref[...])
        g2 = jax.nn.sigmoid(gf_ref[...] + b2_ref[...])
        o_ref[...] = (g1 * bm_ref[...] + g2 * bf_ref[...]).astype(o_ref.dtype)

    b1 = b_gate[:d].reshape(1, d)
    b2 = b_gate[d:].reshape(1, d)
    return _call("gate_fwd", body, (r // br,),
                 [(proj_r, _rows(br, d, cb)), (proj_r, _rows(br, d, cb + 1)), (b1, _whole((1, d))), (b2, _whole((1, d))),
                  (bm, _rows(br, d)), (bfx, _rows(br, d))],
                 [(_sds((r, d), MXU_DTYPE), _rows(br, d))], sem=("parallel",))[0]


def _gate_bwd(proj_r, b_gate, bm, bfx, dmerged):
    r, d = bm.shape
    br = ROW_BLOCK
    cb = R_GATE // d

    def body(gm_ref, gf_ref, b1_ref, b2_ref, bm_ref, bf_ref, dm_ref, dbm_ref, dbf_ref, dgl_ref, dbg_ref):
        g1 = jax.nn.sigmoid(gm_ref[...] + b1_ref[...])
        g2 = jax.nn.sigmoid(gf_ref[...] + b2_ref[...])
        dm = dm_ref[...]
        dbm_ref[...] = (dm * g1).astype(dbm_ref.dtype)
        dbf_ref[...] = (dm * g2).astype(dbf_ref.dtype)
        dl1 = dm * bm_ref[...] * (g1 * (1.0 - g1))
        dl2 = dm * bf_ref[...] * (g2 * (1.0 - g2))
        dgl_ref[:, 0:d] = dl1.astype(dgl_ref.dtype)
        dgl_ref[:, d:2 * d] = dl2.astype(dgl_ref.dtype)

        @pl.when(pl.program_id(0) == 0)
        def _():
            dbg_ref[...] = jnp.zeros_like(dbg_ref)

        dbg_ref[:, 0:d] += jnp.sum(dl1, axis=0, keepdims=True)
        dbg_ref[:, d:2 * d] += jnp.sum(dl2, axis=0, keepdims=True)

    b1 = b_gate[:d].reshape(1, d)
    b2 = b_gate[d:].reshape(1, d)
    return _call("gate_bwd", body, (r // br,),
                 [(proj_r, _rows(br, d, cb)), (proj_r, _rows(br, d, cb + 1)), (b1, _whole((1, d))), (b2, _whole((1, d))),
                  (bm, _rows(br, d)), (bfx, _rows(br, d)), (dmerged, _rows(br, d))],
                 [(_sds((r, d), MXU_DTYPE), _rows(br, d)), (_sds((r, d), MXU_DTYPE), _rows(br, d)),
                  (_sds((r, 2 * d), MXU_DTYPE), _rows(br, 2 * d)), (_sds((1, 2 * d), F32), _whole((1, 2 * d)))],
                 sem=("arbitrary",))


def _conv_taps(gp, halo, first_block):
    halo = jnp.where(first_block, 0.0, halo)
    rid = lax.broadcasted_iota(jnp.int32, gp.shape, 0)
    g1 = jnp.where(rid == 0, halo[7:8, :], pltpu.roll(gp, 1, axis=0))
    g2 = jnp.where(rid == 0, halo[6:7, :], jnp.where(rid == 1, halo[7:8, :], pltpu.roll(gp, 2, axis=0)))
    return g1, g2


def _prev_halo(br, c):
    return pl.BlockSpec((8, c), lambda i: (jnp.maximum(i * (br // 8) - 1, 0), 0))


def _glu_fwd(up, conv_w, conv_b):
    r = up.shape[0]
    c = D_FF
    br = ROW_BLOCK

    def body(gp_ref, halo_ref, val_ref, w_ref, b_ref, o_ref):
        gp = gp_ref[...]
        g1, g2 = _conv_taps(gp, halo_ref[...], pl.program_id(0) == 0)
        gate = w_ref[0:1, :] * g2 + w_ref[1:2, :] * g1 + w_ref[2:3, :] * gp + b_ref[...]
        o_ref[...] = (gate * jax.nn.sigmoid(gate) * val_ref[...]).astype(o_ref.dtype)

    return _call("glu_fwd", body, (r // br,),
                 [(up, _rows(br, c, 0)), (up, _prev_halo(br, c)), (up, _rows(br, c, 1)),
                  (conv_w, _whole((3, c))), (conv_b.reshape(1, c), _whole((1, c)))],
                 [(_sds((r, c), MXU_DTYPE), _rows(br, c))], sem=("parallel",))[0]


def _glu_bwd_gate(up, conv_w, conv_b, d_act):
    r = up.shape[0]
    c = D_FF
    br = ROW_BLOCK

    def body(gp_ref, halo_ref, val_ref, w_ref, b_ref, da_ref, dg_ref, dv_ref, dw_ref, db_ref):
        gp = gp_ref[...]
        g1, g2 = _conv_taps(gp, halo_ref[...], pl.program_id(0) == 0)
        gate = w_ref[0:1, :] * g2 + w_ref[1:2, :] * g1 + w_ref[2:3, :] * gp + b_ref[...]
        sg = jax.nn.sigmoid(gate)
        da = da_ref[...]
        dv_ref[...] = (da * (gate * sg)).astype(dv_ref.dtype)
        dg = da * val_ref[...] * (sg * (1.0 + gate * (1.0 - sg)))
        dg_ref[...] = dg

        @pl.when(pl.program_id(0) == 0)
        def _():
            dw_ref[...] = jnp.zeros_like(dw_ref)
            db_ref[...] = jnp.zeros_like(db_ref)

        dw_ref[0:1, :] += jnp.sum(dg * g2, axis=0, keepdims=True)
        dw_ref[1:2, :] += jnp.sum(dg * g1, axis=0, keepdims=True)
        dw_ref[2:3, :] += jnp.sum(dg * gp, axis=0, keepdims=True)
        db_ref[...] += jnp.sum(dg, axis=0, keepdims=True)

    return _call("glu_bwd_gate", body, (r // br,),
                 [(up, _rows(br, c, 0)), (up, _prev_halo(br, c)), (up, _rows(br, c, 1)),
                  (conv_w, _whole((3, c))), (conv_b.reshape(1, c), _whole((1, c))), (d_act, _rows(br, c))],
                 [(_sds((r, c), F32), _rows(br, c)), (_sds((r, c), MXU_DTYPE), _rows(br, c)),
                  (_sds((8, c), F32), _whole((8, c))), (_sds((1, c), F32), _whole((1, c)))],
                 sem=("arbitrary",))


def _glu_bwd_conv(dg, dval, conv_w):
    r, c = dg.shape
    br = ROW_BLOCK
    nb = r // br

    def body(dg_ref, nxt_ref, dv_ref, w_ref, o_ref):
        x = dg_ref[...]
        nxt = jnp.where(pl.program_id(0) == nb - 1, 0.0, nxt_ref[...])
        rid = lax.broadcasted_iota(jnp.int32, x.shape, 0)
        u1 = jnp.where(rid == br - 1, nxt[0:1, :], pltpu.roll(x, br - 1, axis=0))
        u2 = jnp.where(rid == br - 1, nxt[1:2, :], jnp.where(rid == br - 2, nxt[0:1, :], pltpu.roll(x, br - 2, axis=0)))
        dgp = w_ref[2:3, :] * x + w_ref[1:2, :] * u1 + w_ref[0:1, :] * u2
        o_ref[:, 0:c] = dgp.astype(o_ref.dtype)
        o_ref[:, c:2 * c] = dv_ref[...]

    nxt_spec = pl.BlockSpec((8, c), lambda i: (jnp.minimum((i + 1) * (br // 8), r // 8 - 1), 0))
    return _call("glu_bwd_conv", body, (nb,),
                 [(dg, _rows(br, c)), (dg, nxt_spec), (dval, _rows(br, c)), (conv_w, _whole((3, c)))],
                 [(_sds((r, 2 * c), MXU_DTYPE), _rows(br, 2 * c))], sem=("parallel",))[0]


def _token_specs(seq, d):
    br = ROW_BLOCK
    nxb = seq // br
    main = pl.BlockSpec((br, d), lambda i: (jnp.minimum(i, nxb - 1), 0))
    tail = pl.BlockSpec((N_META, d), lambda i: (jnp.clip(i * (br // N_META) - 1, 0, seq // N_META - 1), 0))
    return main, tail


def _padded_block(main_ref, tail_ref, first, seq):
    br = ROW_BLOCK
    i = pl.program_id(0)
    nxb = seq // br
    main = jnp.where(i < nxb, main_ref[...], 0.0)
    head = jnp.where(i == 0, first, jnp.where(i <= nxb, tail_ref[...], 0.0))
    return jnp.concatenate([head, main[:br - N_META]], axis=0)


def _ln_emb_fwd(x, meta, g, b, rows, after=()):
    seq, d = x.shape
    br = ROW_BLOCK
    assert seq % br == 0 and br % N_META == 0 and rows % br == 0

    def body(x_ref, tail_ref, meta_ref, g_ref, b_ref, y_ref, yb_ref):
        z = _padded_block(x_ref, tail_ref, meta_ref[...], seq)
        xhat, _ = _ln_stats(z)
        y = xhat * g_ref[...] + b_ref[...]
        y_ref[...] = y
        yb_ref[...] = y.astype(yb_ref.dtype)

    main, tail = _token_specs(seq, d)
    return _call("ln_emb_fwd", body, (rows // br,),
                 [(x, main), (x, tail), (meta, _whole((N_META, d))), (g.reshape(1, d), _whole((1, d))),
                  (b.reshape(1, d), _whole((1, d)))],
                 [(_sds((rows, d), F32), _rows(br, d)), (_sds((rows, d), MXU_DTYPE), _rows(br, d))],
                 sem=("parallel",), after=after)


def _ln_emb_bwd(x, meta, dh0, g):
    seq, d = x.shape
    br = ROW_BLOCK
    step = br // N_META

    def ln_bwd(z, dy, gv):
        xhat, rstd = _ln_stats(z)
        dyg = dy * gv
        m1 = jnp.mean(dyg, axis=-1, keepdims=True)
        m2 = jnp.mean(dyg * xhat, axis=-1, keepdims=True)
        dz = rstd * (dyg - m1 - xhat * m2)
        return dz, jnp.sum(dy * xhat, axis=0, keepdims=True), jnp.sum(dy, axis=0, keepdims=True)

    def body(x_ref, dh_ref, nxt_ref, meta_ref, top_ref, g_ref, dx_ref, dm_ref, dg_ref, db_ref):
        gv = g_ref[...]
        dy = jnp.concatenate([dh_ref[N_META:, :], nxt_ref[...]], axis=0)
        dz, dg, db = ln_bwd(x_ref[...], dy, gv)
        dx_ref[...] = dz

        @pl.when(pl.program_id(0) == 0)
        def _():
            dzm, dgm, dbm = ln_bwd(meta_ref[...], top_ref[...], gv)
            dm_ref[...] = dzm
            dg_ref[...] = dgm
            db_ref[...] = dbm

        dg_ref[...] += dg
        db_ref[...] += db

    small = _whole((N_META, d))
    return _call("ln_emb_bwd", body, (seq // br,),
                 [(x, _rows(br, d)), (dh0, _rows(br, d)), (dh0, pl.BlockSpec((N_META, d), lambda i: ((i + 1) * step, 0))),
                  (meta, small), (dh0, small), (g.reshape(1, d), _whole((1, d)))],
                 [(_sds((seq, d), F32), _rows(br, d)), (_sds((N_META, d), F32), small),
                  (_sds((1, d), F32), _whole((1, d))), (_sds((1, d), F32), _whole((1, d)))], sem=("arbitrary",))


def _ln_ffn_loss(h1, f, tgt, g, b):
    r, d = h1.shape
    seq = tgt.shape[0]
    br = ROW_BLOCK

    def body(a_ref, r_ref, t_ref, tail_ref, g_ref, b_ref, l_ref):
        err = _loss_err(a_ref, r_ref, t_ref, tail_ref, g_ref, b_ref, seq)[0]

        @pl.when(pl.program_id(0) == 0)
        def _():
            l_ref[...] = jnp.zeros_like(l_ref)

        l_ref[...] += jnp.sum(jnp.sum(err * err, axis=1, keepdims=True), axis=0, keepdims=True) * (0.5 / d)

    main, tail = _token_specs(seq, d)
    return _call("ln_ffn_loss", body, (r // br,),
                 [(h1, _rows(br, d)), (f, _rows(br, d)), (tgt, main), (tgt, tail),
                  (g.reshape(1, d), _whole((1, d))), (b.reshape(1, d), _whole((1, d)))],
                 [(_sds((1, 1), F32), _whole((1, 1)))], sem=("arbitrary",))[0]


def _loss_err(a_ref, r_ref, t_ref, tail_ref, g_ref, b_ref, seq):
    br, d = a_ref.shape
    xhat, rstd = _ln_stats(ALPHA * a_ref[...] + r_ref[...])
    y = xhat * g_ref[...] + b_ref[...]
    t = _padded_block(t_ref, tail_ref, jnp.zeros((N_META, d), F32), seq)
    rid = lax.broadcasted_iota(jnp.int32, (br, d), 0) + pl.program_id(0) * br
    valid = (rid >= N_META) & (rid < N_META + seq)
    return jnp.where(valid, y - t, 0.0), xhat, rstd


def _ln_ffn_bwd(h1, f, tgt, g, b):
    r, d = h1.shape
    seq = tgt.shape[0]
    br = ROW_BLOCK

    def body(a_ref, r_ref, t_ref, tail_ref, g_ref, b_ref, dz_ref, dzb_ref, dg_ref, db_ref):
        err, xhat, rstd = _loss_err(a_ref, r_ref, t_ref, tail_ref, g_ref, b_ref, seq)
        dyv = err * (1.0 / d)
        dyg = dyv * g_ref[...]
        m1 = jnp.mean(dyg, axis=-1, keepdims=True)
        m2 = jnp.mean(dyg * xhat, axis=-1, keepdims=True)
        dz = rstd * (dyg - m1 - xhat * m2)
        dz_ref[...] = dz
        dzb_ref[...] = dz.astype(dzb_ref.dtype)

        @pl.when(pl.program_id(0) == 0)
        def _():
            dg_ref[...] = jnp.zeros_like(dg_ref)
            db_ref[...] = jnp.zeros_like(db_ref)

        dg_ref[...] += jnp.sum(dyv * xhat, axis=0, keepdims=True)
        db_ref[...] += jnp.sum(dyv, axis=0, keepdims=True)

    main, tail = _token_specs(seq, d)
    return _call("ln_ffn_bwd", body, (r // br,),
                 [(h1, _rows(br, d)), (f, _rows(br, d)), (tgt, main), (tgt, tail),
                  (g.reshape(1, d), _whole((1, d))), (b.reshape(1, d), _whole((1, d)))],
                 [(_sds((r, d), F32), _rows(br, d)), (_sds((r, d), MXU_DTYPE), _rows(br, d)),
                  (_sds((1, d), F32), _whole((1, d))), (_sds((1, d), F32), _whole((1, d)))], sem=("arbitrary",))


def _attn_fwd(name, q, k, v, cum_b=None, cum_t=None):
    (qa, qg), (ka, kg), (va, vg) = q, k, v
    r = qa.shape[0]
    tq, tk = ATT_TQ, ATT_TK
    nq, nk = r // tq, r // tk
    bias = cum_b is not None

    def body(*refs):
        if bias:
            q_ref, k_ref, vt_ref, cb_ref, ct_ref, o_ref, ob_ref, lse_ref = refs
        else:
            q_ref, k_ref, vt_ref, o_ref, ob_ref, lse_ref = refs
        i = pl.program_id(1)
        qs = [q_ref[:, _hs(hh)] for hh in range(hg)]
        cqs = [ct_ref[hh] for hh in range(hg)] if bias else None
        diff = lax.broadcasted_iota(jnp.int32, (tk, tq), 0) - lax.broadcasted_iota(jnp.int32, (tk, tq), 1)

        def step(j, carry, masked):
            keys = pl.ds(pl.multiple_of(j * tk, tk), tk)
            out = []
            for hh in range(hg):
                m, l, acc = carry[hh]
                kt = k_ref[keys, _hs(hh)]
                s = lax.dot_general(kt, qs[hh], NT, preferred_element_type=F32)
                if bias:
                    s = s + (cqs[hh] - cb_ref[keys, hh * HP:hh * HP + 1])
                if masked:
                    s = jnp.where(diff <= i * tq - j * tk, s, NEG_INF)
                m_new = jnp.maximum(m, jnp.max(s, axis=0, keepdims=True))
                p = jnp.exp(s - m_new)
                a = jnp.exp(m - m_new)
                l = a * l + jnp.sum(p, axis=0, keepdims=True)
                acc = a * acc + jnp.dot(vt_ref[j, _hs(hh), :], p.astype(kt.dtype), preferred_element_type=F32)
                out.append((m_new, l, acc))
            return tuple(out)

        n_clear = (i * tq + 1) // tk
        n_all = ((i + 1) * tq - 1) // tk + 1
        carry = tuple((jnp.full((1, tq), NEG_INF, F32), jnp.zeros((1, tq), F32), jnp.zeros((HP, tq), F32))
                      for _ in range(hg))
        carry = lax.fori_loop(0, n_clear, lambda j, c: step(j, c, False), carry)
        carry = lax.fori_loop(n_clear, n_all, lambda j, c: step(j, c, True), carry)
        for hh in range(hg):
            m, l, acc = carry[hh]
            o = (acc / l).T
            o_ref[:, _hs(hh)] = o
            ob_ref[:, _hs(hh)] = o.astype(ob_ref.dtype)
            lse_ref[hh] = m + jnp.log(l)

    hg = ATT_HEADS
    w = hg * HP
    gpw = HW // w
    tile = lambda g: pl.BlockSpec((tq, w), lambda h, i: (i, g * gpw + h))
    res = lambda g: pl.BlockSpec((r, w), lambda h, i: (0, g * gpw + h))
    v_t = _key_tiles_transposed(name + "_vt", va, vg)
    ins = [(qa, tile(qg)), (ka, res(kg)), (v_t, pl.BlockSpec((nk, w, tk), lambda h, i: (0, h, 0)))]
    if bias:
        ins += [(cum_b, res(0)),
                (cum_t.reshape(HEADS, nq, 1, tq), pl.BlockSpec((hg, None, 1, tq), lambda h, i: (h, i, 0, 0)))]
    outs = [(_sds((r, HW), F32), tile(0)), (_sds((r, HW), MXU_DTYPE), tile(0)),
            (_sds((HEADS, nq, 1, tq), F32), pl.BlockSpec((hg, None, 1, tq), lambda h, i: (h, i, 0, 0)))]
    o, ob, lse = _call(name, body, (gpw, nq), ins, outs, sem=("parallel", "parallel"))
    return o, ob, lse.reshape(HEADS, r)


def _key_tiles_transposed(name, a, group):
    r = a.shape[0]
    tk = ATT_TK

    def body(x_ref, o_ref):
        for h in range(HEADS):
            o_ref[_hs(h), :] = x_ref[:, _hs(h)].astype(F32).T.astype(o_ref.dtype)

    return _call(name, body, (r // tk,),
                 [(a, pl.BlockSpec((tk, HW), lambda j: (j, group)))],
                 [(_sds((r // tk, HW, tk), a.dtype), pl.BlockSpec((None, HW, tk), lambda j: (j, 0, 0)))],
                 sem=("parallel",))[0]


def _attn_delta(name, do, o, after=()):
    r = do.shape[0]
    br = ROW_BLOCK

    def body(do_ref, o_ref, d_ref, dob_ref):
        lane = _lane_iota((br, HP))
        d = jnp.zeros((br, HP), F32)
        for h in range(HEADS):
            dh = do_ref[:, _hs(h)]
            d = jnp.where(lane == h, jnp.sum(dh * o_ref[:, _hs(h)], axis=1, keepdims=True), d)
            dob_ref[:, _hs(h)] = dh.astype(dob_ref.dtype)
        d_ref[...] = d.T[0:HEADS, :]

    wide = _rows(br, HW)
    return _call(name, body, (r // br,), [(do, wide), (o, wide)],
                 [(_sds((HEADS, r), F32), pl.BlockSpec((HEADS, br), lambda i: (0, i))), (_sds((r, HW), MXU_DTYPE), wide)],
                 sem=("parallel",), after=after)


def _attn_bwd(name, q, k, v, do_b, lse_t, delta_t, cum_b=None, cum_t=None):
    (qa, qg), (ka, kg), (va, vg) = q, k, v
    r = qa.shape[0]
    tq, tk = ATT_TQ, ATT_TK
    nq, nk = r // tq, r // tk
    bias = cum_b is not None

    def body(*refs):
        if bias:
            (q_ref, k_ref, v_ref, do_ref, lse_ref, dl_ref, cb_ref, ct_ref,
             dq_ref, dk_ref, dv_ref, dcq_ref, dck_ref, dqt_ref) = refs
        else:
            q_ref, k_ref, v_ref, do_ref, lse_ref, dl_ref, dq_ref, dk_ref, dv_ref, dqt_ref = refs
        j = pl.program_id(1)

        @pl.when(j == 0)
        def _():
            dqt_ref[...] = jnp.zeros_like(dqt_ref)
            if bias:
                dcq_ref[...] = jnp.zeros_like(dcq_ref)

        kts = [k_ref[:, _hs(hh)] for hh in range(hg)]
        vts = [v_ref[:, _hs(hh)] for hh in range(hg)]
        k_trs = [kt.astype(F32).T.astype(kt.dtype) for kt in kts]
        cks = [cb_ref[:, hh * HP:hh * HP + 1] for hh in range(hg)] if bias else None
        diff = lax.broadcasted_iota(jnp.int32, (tk, tq), 0) - lax.broadcasted_iota(jnp.int32, (tk, tq), 1)

        def step(i, carry, masked):
            rows = pl.ds(pl.multiple_of(i * tq, tq), tq)
            out = []
            for hh in range(hg):
                dk_acc, dv_acc, dck_acc = carry[hh]
                qt = q_ref[rows, _hs(hh)]
                dot = do_ref[rows, _hs(hh)]
                s = lax.dot_general(kts[hh], qt, NT, preferred_element_type=F32)
                if bias:
                    s = s + (ct_ref[hh, i] - cks[hh])
                if masked:
                    s = jnp.where(diff <= i * tq - j * tk, s, NEG_INF)
                p = jnp.exp(s - lse_ref[hh, i])
                dp = lax.dot_general(vts[hh], dot, NT, preferred_element_type=F32)
                ds = p * (dp - dl_ref[hh, i])
                pb = p.astype(dot.dtype)
                dsb = ds.astype(qt.dtype)
                dv_acc = dv_acc + jnp.dot(pb, dot, preferred_element_type=F32)
                dk_acc = dk_acc + jnp.dot(dsb, qt, preferred_element_type=F32)
                dqt_ref[hh, i] += jnp.dot(k_trs[hh], dsb, preferred_element_type=F32)
                if bias:
                    dcq_ref[hh, i] += jnp.sum(ds, axis=0, keepdims=True)
                    dck_acc = dck_acc - jnp.sum(ds, axis=1, keepdims=True)
                out.append((dk_acc, dv_acc, dck_acc))
            return tuple(out)

        i_first = (j * tk) // tq
        i_clear = jnp.minimum(((j + 1) * tk + tq - 2) // tq, nq)
        carry = tuple((jnp.zeros((tk, HP), F32), jnp.zeros((tk, HP), F32), jnp.zeros((tk, 1), F32)) for _ in range(hg))
        carry = lax.fori_loop(i_first, i_clear, lambda i, c: step(i, c, True), carry)
        carry = lax.fori_loop(i_clear, nq, lambda i, c: step(i, c, False), carry)
        for hh in range(hg):
            dk_acc, dv_acc, dck_acc = carry[hh]
            dk_ref[:, _hs(hh)] = dk_acc
            dv_ref[:, _hs(hh)] = dv_acc
            if bias:
                dck_ref[:, _hs(hh)] = jnp.broadcast_to(dck_acc, (tk, HP))

        @pl.when(j == nk - 1)
        def _():
            for hh in range(hg):
                for i in range(nq):
                    dq_ref[i * tq:(i + 1) * tq, _hs(hh)] = dqt_ref[hh, i].T

    hg = ATT_HEADS
    w = hg * HP
    gpw = HW // w
    res = lambda g: pl.BlockSpec((r, w), lambda h, j: (0, g * gpw + h))
    tile = lambda g: pl.BlockSpec((tk, w), lambda h, j: (j, g * gpw + h))
    rowv = pl.BlockSpec((hg, nq, 1, tq), lambda h, j: (h, 0, 0, 0))
    as_rows = lambda a: a.reshape(HEADS, nq, 1, tq)
    ins = [(qa, res(qg)), (ka, tile(kg)), (va, tile(vg)), (do_b, res(0)), (as_rows(lse_t), rowv), (as_rows(delta_t), rowv)]
    outs = [(_sds((r, HW), F32), res(0)), (_sds((r, HW), F32), tile(0)), (_sds((r, HW), F32), tile(0))]
    if bias:
        ins += [(cum_b, tile(0)), (as_rows(cum_t), rowv)]
        outs += [(_sds((HEADS, nq, 1, tq), F32), rowv), (_sds((r, HW), F32), tile(0))]
    res_out = _call(name, body, (gpw, nk), ins, outs, scratch=[pltpu.VMEM((hg, nq, HP, tq), F32)],
                    sem=("parallel", "arbitrary"))
    if bias:
        dq, dk, dv, dcq, dck = res_out
        return dq, dk, dv, dcq.reshape(HEADS, r), dck
    return res_out


MESH_ID = pl.DeviceIdType.MESH
ANY = pl.BlockSpec(memory_space=pl.ANY)


def _allgather(name, shards):
    n = len(shards)

    def body(*refs):
        x_refs, out_refs = refs[:n], refs[n:2 * n]
        send_sems, recv_sems, local_sems = refs[2 * n:]
        x, y, c = lax.axis_index("x"), lax.axis_index("y"), lax.axis_index("c")
        me, sibling = (x, y, c), (x, y, 1 - c)
        chips = [(1 - x, y), (x, 1 - y), (1 - x, 1 - y)]

        def slot(ti, px, py, pc):
            return out_refs[ti].at[4 * px + 2 * py + pc]

        def copy(ti, k, block, to, src=None):
            return pltpu.make_async_remote_copy(
                src_ref=slot(ti, *block) if src is None else src, dst_ref=slot(ti, *block),
                send_sem=send_sems.at[ti, k], recv_sem=recv_sems.at[ti, k], device_id=to, device_id_type=MESH_ID)

        mine = [pltpu.make_async_copy(x_refs[ti], slot(ti, *me), local_sems.at[ti]) for ti in range(n)]
        for cp in mine:
            cp.start()
        started = []
        for ti in range(n):
            first = [copy(ti, 0, me, sibling, src=x_refs[ti])]
            first += [copy(ti, 1 + j, me, (*chip, c), src=x_refs[ti]) for j, chip in enumerate(chips)]
            for cp in first:
                cp.start()
            started += first
        for ti in range(n):
            for j, chip in enumerate(chips):
                copy(ti, 1 + j, (*chip, c), me).wait_recv()
                fwd = copy(ti, 4 + j, (*chip, c), sibling)
                fwd.start()
                started.append(fwd)
        for ti in range(n):
            copy(ti, 0, sibling, me).wait_recv()
            for j, chip in enumerate(chips):
                copy(ti, 4 + j, (*chip, 1 - c), me).wait_recv()
        for cp in started:
            cp.wait_send()
        for cp in mine:
            cp.wait()

    return pl.pallas_call(
        body, name=name, out_shape=[_sds((N_DEV,) + s.shape, s.dtype) for s in shards],
        in_specs=[ANY] * n, out_specs=[ANY] * n,
        scratch_shapes=[pltpu.SemaphoreType.DMA((n, 7)), pltpu.SemaphoreType.DMA((n, 7)), pltpu.SemaphoreType.DMA((n,))],
    )(*shards)


HBM = pl.BlockSpec(memory_space=pltpu.HBM)
SEM = pl.BlockSpec(memory_space=pltpu.SEMAPHORE)
EFFECT = pltpu.SideEffectType.DATAFLOW_SIDE_EFFECTING
N_PEER = N_DEV - 1


def _my_id():
    return 4 * lax.axis_index("x") + 2 * lax.axis_index("y") + lax.axis_index("c")


def _peers():
    x, y, c = lax.axis_index("x"), lax.axis_index("y"), lax.axis_index("c")
    out = []
    for k in range(1, N_DEV):
        px, py, pc = (1 - x if k & 4 else x, 1 - y if k & 2 else y, 1 - c if k & 1 else c)
        out.append(((px, py, pc), 4 * px + 2 * py + pc))
    return out


def _push_copies(src_refs, land_refs, send_sems, recv_sems, scatter, landing):
    me = _my_id()
    out = []
    for ti, (src, land) in enumerate(zip(src_refs, land_refs)):
        for k, (dev, pid) in enumerate(_peers()):
            out.append(pltpu.make_async_remote_copy(
                src_ref=src.at[pid] if scatter else src, dst_ref=land.at[pid if landing else me],
                send_sem=send_sems.at[ti * N_PEER + k], recv_sem=recv_sems.at[ti * N_PEER + k],
                device_id=dev, device_id_type=MESH_ID))
    return out


def _push_start(name, srcs, scatter, after=None):
    n = len(srcs)
    slot = lambda s: s.shape[1:] if scatter else s.shape
    lands = [lax.empty((N_DEV,) + slot(s), s.dtype) for s in srcs]
    n_after = 0 if after is None else 1

    def body(*refs):
        src_refs, land_refs = refs[:n], refs[n:2 * n]
        send_sems, recv_sems = refs[2 * n + n_after], refs[2 * n + n_after + 1]
        token = refs[-1]
        for cp in _push_copies(src_refs, land_refs, send_sems, recv_sems, scatter, False):
            cp.start()
        token[...] = jnp.zeros_like(token)

    hbm = lambda a: pltpu.with_memory_space_constraint(a, pltpu.HBM)
    operands = [hbm(a) for a in srcs + lands] + ([after] if n_after else [])
    res = pl.pallas_call(
        body, name=name,
        out_shape=[pltpu.SemaphoreType.DMA((n * N_PEER,)), pltpu.SemaphoreType.DMA((n * N_PEER,))]
        + [pltpu.HBM(a.shape, a.dtype) for a in srcs + lands] + [_sds((8, 128), F32)],
        in_specs=[HBM] * (2 * n) + [ANY] * n_after,
        out_specs=[SEM, SEM] + [HBM] * (2 * n) + [pl.BlockSpec(memory_space=pltpu.VMEM)],
        input_output_aliases={i: 2 + i for i in range(2 * n)},
        compiler_params=pltpu.CompilerParams(has_side_effects=EFFECT),
    )(*operands)
    return (res[0], res[1], list(res[2:2 + n]), list(res[2 + n:2 + 2 * n]), scatter), res[-1]


def _push_wait(name, handle, after):
    send_sems, recv_sems, srcs, lands, scatter = handle
    n = len(srcs)

    def body(*refs):
        src_refs, land_refs = refs[:n], refs[n:2 * n]
        s_sems, r_sems = refs[2 * n], refs[2 * n + 1]
        for cp in _push_copies(src_refs, land_refs, s_sems, r_sems, scatter, True):
            cp.wait_send()
            cp.wait_recv()

    res = pl.pallas_call(
        body, name=name,
        out_shape=[pltpu.HBM(a.shape, a.dtype) for a in srcs + lands],
        in_specs=[HBM] * (2 * n) + [SEM, SEM, ANY], out_specs=[HBM] * (2 * n),
        input_output_aliases={i: i for i in range(2 * n)},
        compiler_params=pltpu.CompilerParams(has_side_effects=EFFECT),
    )(*srcs, *lands, send_sems, recv_sems, after)
    return list(res[n:])


def _adamw(name, parts, w, m, v, own=None):
    r, c = w.shape
    br = _pick(r, 256, 16)
    has_own = own is not None

    def body(*refs):
        if has_own:
            p_ref, own_ref, w_ref, m_ref, v_ref, g_ref, d_ref, nm_ref, nv_ref = refs
            me = _my_id()
            mine = own_ref[...].astype(F32)
        else:
            p_ref, w_ref, m_ref, v_ref, g_ref, d_ref, nm_ref, nv_ref = refs
        g = None
        for k in range(N_DEV):
            t = p_ref[k].astype(F32)
            if has_own:
                t = jnp.where(me == k, mine, t)
            g = t if g is None else g + t
        mm = ADAM_B1 * m_ref[...] + (1.0 - ADAM_B1) * g
        vv = ADAM_B2 * v_ref[...] + (1.0 - ADAM_B2) * (g * g)
        m_hat = mm / (1.0 - ADAM_B1 ** ADAM_STEP)
        v_hat = vv / (1.0 - ADAM_B2 ** ADAM_STEP)
        g_ref[...] = g
        d_ref[...] = -ADAM_LR * (m_hat / (jnp.sqrt(v_hat) + ADAM_EPS) + ADAM_WD * w_ref[...])
        nm_ref[...] = mm
        nv_ref[...] = vv

    spec = _rows(br, c)
    out = (_sds((r, c), F32), spec)
    ins = [(parts, pl.BlockSpec((N_DEV, br, c), lambda i: (0, i, 0)))] + ([(own, spec)] if has_own else [])
    return _call(name, body, (r // br,), ins + [(w, spec), (m, spec), (v, spec)], [out] * 4, sem=("parallel",))


def _pad_head_cols(w, d):
    k = w.shape[0]
    return jnp.pad(w.reshape(k, HEADS, d), ((0, 0), (0, 0), (0, HP - d))).reshape(k, HW)


def _unpad_head_cols(wp, d):
    k = wp.shape[0]
    return wp.reshape(k, HEADS, HP)[:, :, :d].reshape(k, HEADS * d)


def _pad_head_rows(w, d):
    n = w.shape[1]
    return jnp.pad(w.reshape(HEADS, d, n), ((0, 0), (0, HP - d), (0, 0))).reshape(HW, n)


def _unpad_head_rows(wp, d):
    n = wp.shape[1]
    return wp.reshape(HEADS, HP, n)[:, :d, :].reshape(HEADS * d, n)


IN_SEGS = (("q", Q_RANK), ("kv", KV_RANK), ("kr", ROPE), ("fq", FOX_W), ("fk", FOX_W), ("fv", FOX_W),
           ("fl", HEADS), ("gate", 2 * D_MODEL))


def _split_w_in(w):
    seg = {}
    o = 0
    for nm, wd in IN_SEGS:
        seg[nm] = w[:, o:o + wd]
        o += wd
    d = w.shape[0]
    z = lambda n: jnp.zeros((d, n), w.dtype)
    seg["fq"] = seg["fq"] * jnp.asarray(FOX_SCALE, w.dtype)
    fused = jnp.concatenate([_pad_head_cols(seg[nm], FOX_DIM) for nm in ("fq", "fk", "fv")], axis=1)
    last = jnp.concatenate([seg["fl"], z(LANE_PE - HEADS), seg["kr"], z(HP - LANE_PE - ROPE)], axis=1)
    rest = jnp.concatenate([seg["q"], seg["kv"], last, z(R_GATE - R_LAST - HP), seg["gate"]], axis=1)
    return fused, rest


def _merge_w_in(fused, rest):
    f = [_unpad_head_cols(fused[:, i * HW:(i + 1) * HW], FOX_DIM) for i in range(3)]
    f[0] = f[0] * jnp.asarray(FOX_SCALE, fused.dtype)
    last = rest[:, R_LAST:R_LAST + HP]
    return jnp.concatenate([rest[:, R_QLAT:R_LAST], last[:, LANE_PE:LANE_PE + ROPE], f[0], f[1], f[2],
                            last[:, LANE_FL:LANE_FL + HEADS], rest[:, R_GATE:]], axis=1)


def _split_w_kv(w):
    k = w.shape[0]
    w3 = w.reshape(k, HEADS, NOPE + V_DIM)
    padl = lambda a: jnp.pad(a, ((0, 0), (0, 0), (0, HP - a.shape[-1]))).reshape(k, HW)
    return padl(w3[..., :NOPE]), padl(w3[..., NOPE:])


def _merge_w_kv(wk, wv):
    k = wk.shape[0]
    return jnp.concatenate([wk.reshape(k, HEADS, HP)[..., :NOPE], wv.reshape(k, HEADS, HP)[..., :V_DIM]],
                           axis=-1).reshape(k, HEADS * (NOPE + V_DIM))


class _NoComm:
    first_token = ()

    def late_weights(self, after):
        return {}

    def send(self, name, grads):
        return ()


def _local_step(x, tgt, p, comm=_NoComm()):
    seq = x.shape[0]
    r = -(-(N_META + seq) // ROW_ALIGN) * ROW_ALIGN
    cd = MXU_DTYPE
    p = dict(p)

    w_f, w_r = _split_w_in(p["w_in"])
    w_q = _pad_head_cols(p["w_q_up"], QK_DIM)
    w_k, w_v = _split_w_kv(p["w_kv_up"])

    pos = jnp.arange(r, dtype=F32)
    inv_freq = ROPE_THETA ** (-jnp.arange(HALF, dtype=F32) / HALF)
    ang = pos[:, None] * inv_freq[None, :]
    cos_t = jnp.tile(jnp.cos(ang), (1, HP // HALF))
    sin_t = jnp.tile(jnp.sin(ang), (1, HP // HALF))
    bf_row = jnp.zeros((1, HP), F32).at[0, LANE_FL:LANE_FL + HEADS].set(p["b_forget"])

    h0, h0b = _ln_emb_fwd(x, p["meta_tokens"], p["ln_emb_g"], p["ln_emb_b"], r, after=comm.first_token)
    proj_f = _matmul("in_proj_f", h0b, w_f, out_dtype=cd)
    proj_r = _matmul("in_proj_r", h0b, w_r)
    ql = _rms_fwd("q_norm_fwd", proj_r, R_QLAT // Q_RANK, Q_RANK, p["q_norm_g"])
    kvl = _rms_fwd("kv_norm_fwd", proj_r, R_KVLAT // KV_RANK, KV_RANK, p["kv_norm_g"])
    q_raw = _matmul("q_up", ql, w_q)
    k_part = _matmul("k_up", kvl, w_k)
    v_mla = _matmul("v_up", kvl, w_v, out_dtype=cd)
    q_mla, k_mla = _rope_fwd(q_raw, k_part, proj_r, cos_t, sin_t)
    o_mla, o_mla_b, lse_mla = _attn_fwd("mla_fwd", (q_mla, 0), (k_mla, 0), (v_mla, 0))

    cum, cum_t = _forget_fwd(proj_r, bf_row)
    o_fox, o_fox_b, lse_fox = _attn_fwd("fox_fwd", (proj_f, 0), (proj_f, 1), (proj_f, 2), cum, cum_t)

    p.update(comm.late_weights(o_fox_b))
    w_bm = _pad_head_rows(p["w_branch_mla"], V_DIM)
    w_bf = _pad_head_rows(p["w_branch_fox"], FOX_DIM)
    bm = _matmul("branch_mla", o_mla_b, w_bm)
    bfx = _matmul("branch_fox", o_fox_b, w_bf)
    merged = _gate_fwd(proj_r, p["b_gate"], bm, bfx)
    mix = _matmul("out_proj", merged, p["w_out"])
    h1, h1b = _ln_fwd("ln_mix_fwd", h0, mix, p["ln_mix_g"], p["ln_mix_b"])
    up = _matmul("ffn_up", h1b, p["w_ffn_up"])
    act = _glu_fwd(up, p["conv_w"], p["conv_b"])
    f = _matmul("ffn_down", act, p["w_ffn_down"])
    loss = _ln_ffn_loss(h1, f, tgt, p["ln_ffn_g"], p["ln_ffn_b"])

    g = {}
    dz2, dz2b, g["ln_ffn_g"], g["ln_ffn_b"] = _ln_ffn_bwd(h1, f, tgt, p["ln_ffn_g"], p["ln_ffn_b"])
    d_act = _matmul("ffn_down_dx", dz2b, p["w_ffn_down"], tb=True)
    g["w_ffn_down"] = _matmul("ffn_down_dw", act, dz2b, ta=True, out_dtype=cd)
    dgate, dval, dcw, g["conv_b"] = _glu_bwd_gate(up, p["conv_w"], p["conv_b"], d_act)
    g["conv_w"] = dcw[:3]
    d_up = _glu_bwd_conv(dgate, dval, p["conv_w"])
    dh1 = _matmul("ffn_up_dx", d_up, p["w_ffn_up"], tb=True, addend=dz2, alpha=ALPHA)
    g["w_ffn_up"] = _matmul("ffn_up_dw", h1b, d_up, ta=True, out_dtype=cd)
    sent = comm.send("ffn", {n: g[n] for n in ("w_ffn_down", "w_ffn_up", "conv_w")})
    dz1, dz1b, g["ln_mix_g"], g["ln_mix_b"] = _ln_bwd("ln_mix_bwd", h0, mix, dh1, p["ln_mix_g"], after=sent)
    dmerged = _matmul("out_proj_dx", dz1b, p["w_out"], tb=True)
    g["w_out"] = _matmul("out_proj_dw", merged, dz1b, ta=True, out_dtype=cd)
    d_bm, d_bf, d_gl, g["b_gate"] = _gate_bwd(proj_r, p["b_gate"], bm, bfx, dmerged)
    d_o_mla = _matmul("branch_mla_dx", d_bm, w_bm, tb=True)
    g["w_branch_mla"] = _unpad_head_rows(_matmul("branch_mla_dw", o_mla_b, d_bm, ta=True, out_dtype=cd), V_DIM)
    d_o_fox = _matmul("branch_fox_dx", d_bf, w_bf, tb=True)
    g["w_branch_fox"] = _unpad_head_rows(_matmul("branch_fox_dw", o_fox_b, d_bf, ta=True, out_dtype=cd), FOX_DIM)

    sent = comm.send("mix", {n: g[n] for n in ("w_out", "w_branch_mla", "w_branch_fox")})
    dl_mla, do_mla_b = _attn_delta("mla_delta", d_o_mla, o_mla, after=sent)
    dq_m, dk_m, dv_m = _attn_bwd("mla_bwd", (q_mla, 0), (k_mla, 0), (v_mla, 0), do_mla_b, lse_mla, dl_mla)
    dl_fox, do_fox_b = _attn_delta("fox_delta", d_o_fox, o_fox)
    dfq, dfk, dfv, dcq, dck = _attn_bwd("fox_bwd", (proj_f, 0), (proj_f, 1), (proj_f, 2), do_fox_b, lse_fox, dl_fox,
                                        cum, cum_t)
    dfl, dbf = _forget_bwd(proj_r, bf_row, dcq, dck)
    g["b_forget"] = dbf[:, LANE_FL:LANE_FL + HEADS]

    dq_b, dk_b, dlast = _rope_bwd(dq_m, dk_m, dfl, cos_t, sin_t)
    dv_b = dv_m.astype(cd)
    d_ql = _matmul("q_up_dx", dq_b, w_q, tb=True)
    g["w_q_up"] = _unpad_head_cols(_matmul("q_up_dw", ql, dq_b, ta=True, out_dtype=cd), QK_DIM)
    d_kvl = _matmul("k_up_dx", dk_b, w_k, tb=True)
    d_kvl = _matmul("v_up_dx", dv_b, w_v, tb=True, addend=d_kvl)
    g["w_kv_up"] = _merge_w_kv(_matmul("k_up_dw", kvl, dk_b, ta=True, out_dtype=cd), _matmul("v_up_dw", kvl, dv_b, ta=True, out_dtype=cd))
    d_qlat, g["q_norm_g"] = _rms_bwd("q_norm_bwd", proj_r, R_QLAT // Q_RANK, Q_RANK, d_ql, p["q_norm_g"])
    d_kvlat, g["kv_norm_g"] = _rms_bwd("kv_norm_bwd", proj_r, R_KVLAT // KV_RANK, KV_RANK, d_kvl, p["kv_norm_g"])
    dproj_f = jnp.concatenate([dfq.astype(cd), dfk.astype(cd), dfv.astype(cd)], axis=1)
    dproj_r = jnp.concatenate([d_qlat, d_kvlat, dlast, jnp.zeros((r, R_GATE - R_LAST - HP), cd), d_gl], axis=1)
    g["w_in"] = _merge_w_in(_matmul("in_proj_f_dw", h0b, dproj_f, ta=True, out_dtype=cd), _matmul("in_proj_r_dw", h0b, dproj_r, ta=True, out_dtype=cd))
    sent = comm.send("in", {n: g[n] for n in ("w_in", "w_q_up", "w_kv_up")})
    dh0 = _matmul("in_proj_f_dx", dproj_f, w_f, tb=True, addend=dz1, alpha=ALPHA, after=sent)
    dh0 = _matmul("in_proj_r_dx", dproj_r, w_r, tb=True, addend=dh0)
    grad_x, d_meta, g["ln_emb_g"], g["ln_emb_b"] = _ln_emb_bwd(x, p["meta_tokens"], dh0, p["ln_emb_g"])
    return loss, grad_x, d_meta, g


BIG = (("w_in", 1), ("w_q_up", 1), ("w_kv_up", 1), ("w_branch_mla", 1), ("w_branch_fox", 1), ("w_out", 0),
       ("w_ffn_up", 1), ("w_ffn_down", 0))
SMALL_SHARDED = (("meta_tokens", 1), ("conv_w", 1))
EARLY = ("w_in", "w_q_up", "w_kv_up", "meta_tokens", "conv_w")
LATE = ("w_branch_mla", "w_branch_fox", "w_out", "w_ffn_up", "w_ffn_down")
REPLICATED = ("ln_emb_g", "ln_emb_b", "b_gate", "b_forget", "q_norm_g", "kv_norm_g", "ln_mix_g", "ln_mix_b",
              "conv_b", "ln_ffn_g", "ln_ffn_b")
PACK_COLS = 1024


def _pack(flat_list):
    cat = jnp.concatenate(flat_list)
    n = cat.shape[0]
    rows = -(-n // (8 * PACK_COLS)) * 8
    return jnp.pad(cat, (0, rows * PACK_COLS - n)).reshape(rows, PACK_COLS)


def _gathered_full(g3, axis):
    n, r, c = g3.shape
    if axis == 0:
        return g3.reshape(n * r, c)
    return g3.transpose(1, 0, 2).reshape(r, n * c)


def _shard_major(full, axis):
    r, c = full.shape
    if axis == 0:
        return full.reshape(N_DEV, r // N_DEV, c)
    return full.reshape(r, N_DEV, c // N_DEV).transpose(1, 0, 2)


def kernel(x, meta_tokens, ln_emb_g, ln_emb_b, w_in, b_gate, b_forget, q_norm_g, w_q_up, kv_norm_g, w_kv_up, w_branch_mla, w_branch_fox, w_out, ln_mix_g, ln_mix_b, w_ffn_up, conv_w, conv_b, w_ffn_down, ln_ffn_g, ln_ffn_b, loss_target, m_meta_tokens, m_ln_emb_g, m_ln_emb_b, m_w_in, m_b_gate, m_b_forget, m_q_norm_g, m_w_q_up, m_kv_norm_g, m_w_kv_up, m_w_branch_mla, m_w_branch_fox, m_w_out, m_ln_mix_g, m_ln_mix_b, m_w_ffn_up, m_conv_w, m_conv_b, m_w_ffn_down, m_ln_ffn_g, m_ln_ffn_b, v_meta_tokens, v_ln_emb_g, v_ln_emb_b, v_w_in, v_b_gate, v_b_forget, v_q_norm_g, v_w_q_up, v_kv_norm_g, v_w_kv_up, v_w_branch_mla, v_w_branch_fox, v_w_out, v_ln_mix_g, v_ln_mix_b, v_w_ffn_up, v_conv_w, v_conv_b, v_w_ffn_down, v_ln_ffn_g, v_ln_ffn_b):
    names = ("meta_tokens", "ln_emb_g", "ln_emb_b", "w_in", "b_gate", "b_forget", "q_norm_g", "w_q_up", "kv_norm_g",
             "w_kv_up", "w_branch_mla", "w_branch_fox", "w_out", "ln_mix_g", "ln_mix_b", "w_ffn_up", "conv_w", "conv_b",
             "w_ffn_down", "ln_ffn_g", "ln_ffn_b")
    w_args = (meta_tokens, ln_emb_g, ln_emb_b, w_in, b_gate, b_forget, q_norm_g, w_q_up, kv_norm_g, w_kv_up,
              w_branch_mla, w_branch_fox, w_out, ln_mix_g, ln_mix_b, w_ffn_up, conv_w, conv_b, w_ffn_down, ln_ffn_g, ln_ffn_b)
    m_args = (m_meta_tokens, m_ln_emb_g, m_ln_emb_b, m_w_in, m_b_gate, m_b_forget, m_q_norm_g, m_w_q_up, m_kv_norm_g,
              m_w_kv_up, m_w_branch_mla, m_w_branch_fox, m_w_out, m_ln_mix_g, m_ln_mix_b, m_w_ffn_up, m_conv_w, m_conv_b,
              m_w_ffn_down, m_ln_ffn_g, m_ln_ffn_b)
    v_args = (v_meta_tokens, v_ln_emb_g, v_ln_emb_b, v_w_in, v_b_gate, v_b_forget, v_q_norm_g, v_w_q_up, v_kv_norm_g,
              v_w_kv_up, v_w_branch_mla, v_w_branch_fox, v_w_out, v_ln_mix_g, v_ln_mix_b, v_w_ffn_up, v_conv_w, v_conv_b,
              v_w_ffn_down, v_ln_ffn_g, v_ln_ffn_b)
    as2d = lambda a: a.reshape((-1, a.shape[-1])) if a.ndim != 1 else a.reshape(1, -1)
    w = {n: as2d(a) for n, a in zip(names, w_args)}
    m = {n: as2d(a) for n, a in zip(names, m_args)}
    v = {n: as2d(a) for n, a in zip(names, v_args)}
    out_shape = {n: a.shape for n, a in zip(names, w_args)}

    axis_of = dict(BIG + SMALL_SHARDED)
    big = set(n for n, _ in BIG)
    wire = lambda n, a: a.astype(MXU_DTYPE) if n in big else a
    my_id = _my_id()
    slot_is_mine = lax.broadcasted_iota(jnp.int32, (N_DEV, 1, 1), 0) == my_id

    early = _allgather("gather_early", [wire(n, w[n]) for n in EARLY])
    p = {n: _gathered_full(g3, axis_of[n]) for n, g3 in zip(EARLY, early)}
    for n in REPLICATED:
        p[n] = w[n].reshape(-1)
    late_src = [wire(n, w[n]) for n in LATE]
    late_handle, late_token = _push_start("gather_late_start", late_src, False, after=early[0])
    sent = {}

    class Comm:
        first_token = (late_token,)

        def late_weights(self, after):
            lands = _push_wait("gather_late_wait", late_handle, after)
            return {n: _gathered_full(jnp.where(slot_is_mine, own[None], land), axis_of[n])
                    for n, own, land in zip(LATE, late_src, lands)}

        def send(self, name, grads):
            names_ = tuple(grads)
            parts = [_shard_major(grads[n], axis_of[n]).astype(MXU_DTYPE) for n in names_]
            handle, token = _push_start("send_" + name + "_start", parts, True)
            sent[name] = (names_, parts, handle)
            return (token,)

    loss_part, grad_x, d_meta, g = _local_step(x[0], loss_target[0], p, Comm())
    grad_x = grad_x[None]

    small = _pack([d_meta.reshape(-1)] + [g[n].reshape(-1) for n in REPLICATED] + [loss_part.reshape(-1)])
    small_handle, small_token = _push_start("send_small_start", [small], False)

    res = {}
    prev = small_token
    for name, (names_, parts, handle) in sent.items():
        lands = _push_wait("send_" + name + "_wait", handle, prev)
        for n, part, land in zip(names_, parts, lands):
            own = lax.dynamic_index_in_dim(part, my_id, axis=0, keepdims=False)
            res[n] = _adamw("adamw_" + n, land, w[n], m[n], v[n], own=own)
            prev = res[n][0]
    small_all = _push_wait("send_small_wait", small_handle, prev)[0]
    head = jnp.zeros((d_meta.size,), F32)
    rep_w = _pack([head] + [w[n].reshape(-1) for n in REPLICATED])
    rep_m = _pack([head] + [m[n].reshape(-1) for n in REPLICATED])
    rep_v = _pack([head] + [v[n].reshape(-1) for n in REPLICATED])
    rep_res = _adamw("adamw_replicated", small_all, rep_w, rep_m, rep_v, own=small)
    off = d_meta.size
    for n in REPLICATED:
        sz = w[n].size
        res[n] = tuple(a.reshape(-1)[off:off + sz] for a in rep_res)
        off += sz
    loss = rep_res[0].reshape(-1)[off]
    cols = w["meta_tokens"].shape[1]
    meta_rows = lambda a: a.reshape(a.shape[:-2] + (-1,))[..., :d_meta.size].reshape(a.shape[:-2] + d_meta.shape)
    my_cols = lambda a: lax.dynamic_slice_in_dim(a, my_id * cols, cols, axis=a.ndim - 1)
    res["meta_tokens"] = _adamw("adamw_meta_tokens", my_cols(meta_rows(small_all)), w["meta_tokens"],
                                m["meta_tokens"], v["meta_tokens"], own=my_cols(d_meta))

    outs = [loss, grad_x]
    for idx in range(4):
        outs += [res[n][idx].reshape(out_shape[n]) for n in names]
    return tuple(outs)
```

```python
import jax
import jax.numpy as jnp
from jax import lax
from jax.experimental import pallas as pl
from jax.experimental.pallas import tpu as pltpu

F32 = jnp.float32
BF16 = jnp.bfloat16
MXU_DTYPE = BF16

N_DEV = 8
N_META = 16
D_MODEL = 1024
HEADS = 8
Q_RANK = 384
KV_RANK = 128
NOPE = 64
ROPE = 32
HALF = ROPE // 2
QK_DIM = NOPE + ROPE
V_DIM = 64
FOX_DIM = 64
FOX_W = HEADS * FOX_DIM
D_FF = 2816
ROPE_THETA = 10000.0
LN_EPS = 1e-5
RMS_EPS = 1e-6
ALPHA = 2.0 ** 0.25
MLA_SCALE = QK_DIM ** -0.5
FOX_SCALE = FOX_DIM ** -0.5
NEG_INF = -1e30

HP = 128
HW = HEADS * HP
F_W = 3 * HW
R_QLAT = 0
R_KVLAT = Q_RANK
R_LAST = R_KVLAT + KV_RANK
R_GATE = D_MODEL
R_W = R_GATE + 2 * D_MODEL
LANE_FL = 0
LANE_PE = NOPE

ADAM_LR = 0.001
ADAM_B1 = 0.9
ADAM_B2 = 0.999
ADAM_EPS = 1e-08
ADAM_WD = 0.01
ADAM_STEP = 10

ROW_BLOCK = 256
ATT_TQ = 768
ATT_TK = 256
ATT_HEADS = 2
ROW_ALIGN = 768
MM_BLOCK_CAP = 1408
VMEM_LIMIT = 56 * 1024 * 1024
HIGHEST = lax.Precision.HIGHEST
NT = (((1,), (1,)), ((), ()))
TN = (((0,), (0,)), ((), ()))


def _params(sem=None):
    return pltpu.CompilerParams(dimension_semantics=sem, vmem_limit_bytes=VMEM_LIMIT)


def _call(name, body, grid, ins, outs, scratch=(), sem=None, after=()):
    n_in = len(ins)
    n_tok = len(after)

    def run(*refs):
        body(*refs[:n_in], *refs[n_in + n_tok:])

    tok_spec = pl.BlockSpec((8, 128), lambda *_: (0, 0))
    return pl.pallas_call(
        run, name=name, grid=grid,
        in_specs=[s for _, s in ins] + [tok_spec] * n_tok,
        out_specs=[s for _, s in outs],
        out_shape=[o for o, _ in outs],
        scratch_shapes=list(scratch),
        compiler_params=_params(sem),
    )(*[a for a, _ in ins], *after)


def _sds(shape, dtype):
    return jax.ShapeDtypeStruct(shape, dtype)


def _rows(br, c, cb=0):
    return pl.BlockSpec((br, c), lambda i: (i, cb))


def _whole(shape):
    n = len(shape)
    return pl.BlockSpec(shape, lambda i: (0,) * n)


def _pick(dim, cap, mult):
    best = None
    d = mult
    while d <= min(dim, cap):
        if dim % d == 0:
            best = d
        d += mult
    return best if best is not None else dim


def _hs(h):
    return slice(h * HP, (h + 1) * HP)


def _matmul(name, a, b, *, ta=False, tb=False, out_dtype=F32, addend=None, alpha=1.0, after=()):
    if ta:
        k, m = a.shape
    else:
        m, k = a.shape
    if tb:
        n, k2 = b.shape
    else:
        k2, n = b.shape
    assert k == k2, (name, a.shape, b.shape)
    bm = _pick(m, MM_BLOCK_CAP, 128 if ta else 16)
    bn = _pick(n, MM_BLOCK_CAP, 128)
    bk = _pick(k, MM_BLOCK_CAP, 128 if (not ta or tb) else 16)
    nk = k // bk
    dims = (((0 if ta else 1,), (1 if tb else 0,)), ((), ()))
    has_add = addend is not None

    def body(*refs):
        a_ref, b_ref = refs[:2]
        add_ref = refs[2] if has_add else None
        o_ref = refs[3 if has_add else 2]

        def finish(r):
            if has_add:
                r = r + alpha * add_ref[...]
            o_ref[...] = r.astype(o_ref.dtype)

        part = lax.dot_general(a_ref[...], b_ref[...], dims, preferred_element_type=F32)
        if nk == 1:
            finish(part)
            return
        acc_ref = refs[-1]
        kk = pl.program_id(2)

        @pl.when(kk == 0)
        def _():
            acc_ref[...] = part

        @pl.when(kk > 0)
        def _():
            acc_ref[...] += part

        @pl.when(kk == nk - 1)
        def _():
            finish(acc_ref[...])

    a_spec = pl.BlockSpec((bk, bm), lambda i, j, l: (l, i)) if ta else pl.BlockSpec((bm, bk), lambda i, j, l: (i, l))
    b_spec = pl.BlockSpec((bn, bk), lambda i, j, l: (j, l)) if tb else pl.BlockSpec((bk, bn), lambda i, j, l: (l, j))
    o_spec = pl.BlockSpec((bm, bn), lambda i, j, l: (i, j))
    ins = [(a, a_spec), (b, b_spec)]
    if has_add:
        ins.append((addend, o_spec))
    return _call(name, body, (m // bm, n // bn, nk), ins, [(_sds((m, n), out_dtype), o_spec)],
                 scratch=[pltpu.VMEM((bm, bn), F32)] if nk > 1 else [],
                 sem=("parallel", "parallel", "arbitrary"), after=after)[0]


def _ln_stats(z):
    mu = jnp.mean(z, axis=-1, keepdims=True)
    zc = z - mu
    var = jnp.mean(zc * zc, axis=-1, keepdims=True)
    rstd = lax.rsqrt(var + LN_EPS)
    return zc * rstd, rstd


def _ln_fwd(name, a, res, g, b, after=()):
    r, d = a.shape
    br = ROW_BLOCK
    has_res = res is not None

    def body(*refs):
        if has_res:
            a_ref, r_ref, g_ref, b_ref, y_ref, yb_ref = refs
            z = ALPHA * a_ref[...] + r_ref[...]
        else:
            a_ref, g_ref, b_ref, y_ref, yb_ref = refs
            z = a_ref[...]
        xhat, _ = _ln_stats(z)
        y = xhat * g_ref[...] + b_ref[...]
        y_ref[...] = y
        yb_ref[...] = y.astype(yb_ref.dtype)

    ins = [(a, _rows(br, d))]
    if has_res:
        ins.append((res, _rows(br, d)))
    ins += [(g.reshape(1, d), _whole((1, d))), (b.reshape(1, d), _whole((1, d)))]
    outs = [(_sds((r, d), F32), _rows(br, d)), (_sds((r, d), MXU_DTYPE), _rows(br, d))]
    return _call(name, body, (r // br,), ins, outs, sem=("parallel",), after=after)


def _ln_bwd(name, a, res, dy, g, after=()):
    r, d = a.shape
    br = ROW_BLOCK
    has_res = res is not None

    def body(*refs):
        if has_res:
            a_ref, r_ref, dy_ref, g_ref, dz_ref, dzb_ref, dg_ref, db_ref = refs
            z = ALPHA * a_ref[...] + r_ref[...]
        else:
            a_ref, dy_ref, g_ref, dz_ref, dzb_ref, dg_ref, db_ref = refs
            z = a_ref[...]
        xhat, rstd = _ln_stats(z)
        dyv = dy_ref[...]
        dyg = dyv * g_ref[...]
        m1 = jnp.mean(dyg, axis=-1, keepdims=True)
        m2 = jnp.mean(dyg * xhat, axis=-1, keepdims=True)
        dz = rstd * (dyg - m1 - xhat * m2)
        dz_ref[...] = dz
        dzb_ref[...] = dz.astype(dzb_ref.dtype)

        @pl.when(pl.program_id(0) == 0)
        def _():
            dg_ref[...] = jnp.zeros_like(dg_ref)
            db_ref[...] = jnp.zeros_like(db_ref)

        dg_ref[...] += jnp.sum(dyv * xhat, axis=0, keepdims=True)
        db_ref[...] += jnp.sum(dyv, axis=0, keepdims=True)

    ins = [(a, _rows(br, d))]
    if has_res:
        ins.append((res, _rows(br, d)))
    ins += [(dy, _rows(br, d)), (g.reshape(1, d), _whole((1, d)))]
    outs = [(_sds((r, d), F32), _rows(br, d)), (_sds((r, d), MXU_DTYPE), _rows(br, d)),
            (_sds((1, d), F32), _whole((1, d))), (_sds((1, d), F32), _whole((1, d)))]
    return _call(name, body, (r // br,), ins, outs, sem=("arbitrary",), after=after)


def _rms_fwd(name, proj, cb, width, g):
    r = proj.shape[0]
    br = ROW_BLOCK

    def body(x_ref, g_ref, y_ref):
        x = x_ref[...]
        rstd = lax.rsqrt(jnp.mean(x * x, axis=-1, keepdims=True) + RMS_EPS)
        y_ref[...] = (x * rstd * g_ref[...]).astype(y_ref.dtype)

    return _call(name, body, (r // br,), [(proj, _rows(br, width, cb)), (g.reshape(1, width), _whole((1, width)))],
                 [(_sds((r, width), MXU_DTYPE), _rows(br, width))], sem=("parallel",))[0]


def _rms_bwd(name, proj, cb, width, dy, g):
    r = proj.shape[0]
    br = ROW_BLOCK

    def body(x_ref, dy_ref, g_ref, dx_ref, dg_ref):
        x = x_ref[...]
        rstd = lax.rsqrt(jnp.mean(x * x, axis=-1, keepdims=True) + RMS_EPS)
        nrm = x * rstd
        dyv = dy_ref[...]
        dyg = dyv * g_ref[...]
        dx = rstd * (dyg - nrm * jnp.mean(dyg * nrm, axis=-1, keepdims=True))
        dx_ref[...] = dx.astype(dx_ref.dtype)

        @pl.when(pl.program_id(0) == 0)
        def _():
            dg_ref[...] = jnp.zeros_like(dg_ref)

        dg_ref[...] += jnp.sum(dyv * nrm, axis=0, keepdims=True)

    return _call(name, body, (r // br,),
                 [(proj, _rows(br, width, cb)), (dy, _rows(br, width)), (g.reshape(1, width), _whole((1, width)))],
                 [(_sds((r, width), MXU_DTYPE), _rows(br, width)), (_sds((1, width), F32), _whole((1, width)))],
                 sem=("arbitrary",))


def _lane_iota(shape):
    return lax.broadcasted_iota(jnp.int32, shape, 1)


def _rotary(t, c, s, lane, sign):
    second = pltpu.roll(t, HP - HALF, axis=1)
    first = pltpu.roll(t, HALF, axis=1)
    lo = (lane >= LANE_PE) & (lane < LANE_PE + HALF)
    hi = (lane >= LANE_PE + HALF) & (lane < LANE_PE + ROPE)
    return jnp.where(lo, t * c - sign * second * s, jnp.where(hi, t * c + sign * first * s, t))


def _rope_fwd(q_raw, k_part, proj_r, cos_t, sin_t):
    r = q_raw.shape[0]
    br = ROW_BLOCK

    def body(q_ref, k_ref, t_ref, c_ref, s_ref, qo_ref, ko_ref):
        c = c_ref[...]
        s = s_ref[...]
        lane = _lane_iota((br, HP))
        pe = (lane >= LANE_PE) & (lane < LANE_PE + ROPE)
        kp = jnp.where(pe, _rotary(t_ref[...], c, s, lane, 1.0), 0.0)
        for h in range(HEADS):
            qo_ref[:, _hs(h)] = (_rotary(q_ref[:, _hs(h)], c, s, lane, 1.0) * MLA_SCALE).astype(qo_ref.dtype)
            ko_ref[:, _hs(h)] = (k_ref[:, _hs(h)] + kp).astype(ko_ref.dtype)

    blk = _rows(br, HP)
    wide = _rows(br, HW)
    return _call("rope_fwd", body, (r // br,),
                 [(q_raw, wide), (k_part, wide), (proj_r, _rows(br, HP, R_LAST // HP)), (cos_t, blk), (sin_t, blk)],
                 [(_sds((r, HW), MXU_DTYPE), wide)] * 2, sem=("parallel",))


def _rope_bwd(dq, dk, dfl, cos_t, sin_t):
    r = dq.shape[0]
    br = ROW_BLOCK

    def body(dq_ref, dk_ref, fl_ref, c_ref, s_ref, dqo_ref, dko_ref, dl_ref):
        c = c_ref[...]
        s = s_ref[...]
        lane = _lane_iota((br, HP))
        pe = (lane >= LANE_PE) & (lane < LANE_PE + ROPE)
        acc = jnp.zeros((br, HP), F32)
        for h in range(HEADS):
            dqo_ref[:, _hs(h)] = (_rotary(dq_ref[:, _hs(h)], c, s, lane, -1.0) * MLA_SCALE).astype(dqo_ref.dtype)
            dkh = dk_ref[:, _hs(h)]
            acc = acc + dkh
            dko_ref[:, _hs(h)] = dkh.astype(dko_ref.dtype)
        dl_ref[...] = (jnp.where(pe, _rotary(acc, c, s, lane, -1.0), 0.0) + fl_ref[...]).astype(dl_ref.dtype)

    blk = _rows(br, HP)
    wide = _rows(br, HW)
    return _call("rope_bwd", body, (r // br,),
                 [(dq, wide), (dk, wide), (dfl, blk), (cos_t, blk), (sin_t, blk)],
                 [(_sds((r, HW), MXU_DTYPE), wide), (_sds((r, HW), MXU_DTYPE), wide), (_sds((r, HP), MXU_DTYPE), blk)],
                 sem=("parallel",))


def _log_sigmoid(x):
    return jnp.minimum(x, 0.0) - jnp.log(1.0 + jnp.exp(-jnp.abs(x)))


def _head_lane(x, h, lane):
    return jnp.sum(jnp.where(lane == h, x, 0.0), axis=1, keepdims=True)


def _forget_fwd(proj_r, bf_row):
    r = proj_r.shape[0]
    br = ROW_BLOCK

    def body(t_ref, b_ref, ob_ref, ot_ref, carry_ref):
        @pl.when(pl.program_id(0) == 0)
        def _():
            carry_ref[...] = jnp.zeros_like(carry_ref)

        x = t_ref[...] + b_ref[...]
        lane = _lane_iota(x.shape)
        lf = jnp.where((lane >= LANE_FL) & (lane < LANE_FL + HEADS), _log_sigmoid(x), 0.0)
        tri = (lax.broadcasted_iota(jnp.int32, (br, br), 0) >= lax.broadcasted_iota(jnp.int32, (br, br), 1)).astype(F32)
        cum = jnp.dot(tri, lf, precision=HIGHEST, preferred_element_type=F32) + carry_ref[0:1, :]
        for h in range(HEADS):
            ob_ref[:, _hs(h)] = jnp.broadcast_to(_head_lane(cum, LANE_FL + h, lane), (br, HP))
        ot_ref[...] = cum.T[LANE_FL:LANE_FL + HEADS, :]
        carry_ref[...] = jnp.broadcast_to(cum[br - 1:br, :], carry_ref.shape)

    return _call("forget_fwd", body, (r // br,),
                 [(proj_r, _rows(br, HP, R_LAST // HP)), (bf_row, _whole((1, HP)))],
                 [(_sds((r, HW), F32), _rows(br, HW)), (_sds((HEADS, r), F32), pl.BlockSpec((HEADS, br), lambda i: (0, i)))],
                 scratch=[pltpu.VMEM((8, HP), F32)], sem=("arbitrary",))


def _forget_bwd(proj_r, bf_row, dcq_t, dck_b):
    r = proj_r.shape[0]
    br = ROW_BLOCK
    nb = r // br

    def body(t_ref, b_ref, dcq_ref, dck_ref, o_ref, db_ref, carry_ref):
        @pl.when(pl.program_id(0) == 0)
        def _():
            carry_ref[...] = jnp.zeros_like(carry_ref)
            db_ref[...] = jnp.zeros_like(db_ref)

        lane = _lane_iota((br, HP))
        dc = jnp.concatenate([dcq_ref[...], jnp.zeros((HP - HEADS, br), F32)], axis=0).T
        for h in range(HEADS):
            dc = dc + jnp.where(lane == LANE_FL + h, dck_ref[:, h * HP:h * HP + 1], 0.0)
        triu = (lax.broadcasted_iota(jnp.int32, (br, br), 0) <= lax.broadcasted_iota(jnp.int32, (br, br), 1)).astype(F32)
        dlf = jnp.dot(triu, dc, precision=HIGHEST, preferred_element_type=F32) + carry_ref[0:1, :]
        carry_ref[...] = jnp.broadcast_to(dlf[0:1, :], carry_ref.shape)
        x = t_ref[...] + b_ref[...]
        dfl = jnp.where((lane >= LANE_FL) & (lane < LANE_FL + HEADS), dlf * jax.nn.sigmoid(-x), 0.0)
        o_ref[...] = dfl
        db_ref[...] += jnp.sum(dfl, axis=0, keepdims=True)

    rev = pl.BlockSpec((br, HP), lambda i: (nb - 1 - i, 0))
    return _call("forget_bwd", body, (nb,),
                 [(proj_r, pl.BlockSpec((br, HP), lambda i: (nb - 1 - i, R_LAST // HP))), (bf_row, _whole((1, HP))),
                  (dcq_t, pl.BlockSpec((HEADS, br), lambda i: (0, nb - 1 - i))),
                  (dck_b, pl.BlockSpec((br, HW), lambda i: (nb - 1 - i, 0)))],
                 [(_sds((r, HP), F32), rev), (_sds((1, HP), F32), _whole((1, HP)))],
                 scratch=[pltpu.VMEM((8, HP), F32)], sem=("arbitrary",))


def _gate_fwd(proj_r, b_gate, bm, bfx):
    r, d = bm.shape
    br = ROW_BLOCK
    cb = R_GATE // d

    def body(gm_ref, gf_ref, b1_ref, b2_ref, bm_ref, bf_ref, o_ref):
        g1 = jax.nn.sigmoid(gm_ref[...] + b1_ref[...])
        g2 = jax.nn.sigmoid(gf_ref[...] + b2_ref[...])
        o_ref[...] = (g1 * bm_ref[...] + g2 * bf_ref[...]).astype(o_ref.dtype)

    b1 = b_gate[:d].reshape(1, d)
    b2 = b_gate[d:].reshape(1, d)
    return _call("gate_fwd", body, (r // br,),
                 [(proj_r, _rows(br, d, cb)), (proj_r, _rows(br, d, cb + 1)), (b1, _whole((1, d))), (b2, _whole((1, d))),
                  (bm, _rows(br, d)), (bfx, _rows(br, d))],
                 [(_sds((r, d), MXU_DTYPE), _rows(br, d))], sem=("parallel",))[0]


def _gate_bwd(proj_r, b_gate, bm, bfx, dmerged):
    r, d = bm.shape
    br = ROW_BLOCK
    cb = R_GATE // d

    def body(gm_ref, gf_ref, b1_ref, b2_ref, bm_ref, bf_ref, dm_ref, dbm_ref, dbf_ref, dgl_ref, dbg_ref):
        g1 = jax.nn.sigmoid(gm_ref[...] + b1_ref[...])
        g2 = jax.nn.sigmoid(gf_ref[...] + b2_ref[...])
        dm = dm_ref[...]
        dbm_ref[...] = (dm * g1).astype(dbm_ref.dtype)
        dbf_ref[...] = (dm * g2).astype(dbf_ref.dtype)
        dl1 = dm * bm_ref[...] * (g1 * (1.0 - g1))
        dl2 = dm * bf_ref[...] * (g2 * (1.0 - g2))
        dgl_ref[:, 0:d] = dl1.astype(dgl_ref.dtype)
        dgl_ref[:, d:2 * d] = dl2.astype(dgl_ref.dtype)

        @pl.when(pl.program_id(0) == 0)
        def _():
            dbg_ref[...] = jnp.zeros_like(dbg_ref)

        dbg_ref[:, 0:d] += jnp.sum(dl1, axis=0, keepdims=True)
        dbg_ref[:, d:2 * d] += jnp.sum(dl2, axis=0, keepdims=True)

    b1 = b_gate[:d].reshape(1, d)
    b2 = b_gate[d:].reshape(1, d)
    return _call("gate_bwd", body, (r // br,),
                 [(proj_r, _rows(br, d, cb)), (proj_r, _rows(br, d, cb + 1)), (b1, _whole((1, d))), (b2, _whole((1, d))),
                  (bm, _rows(br, d)), (bfx, _rows(br, d)), (dmerged, _rows(br, d))],
                 [(_sds((r, d), MXU_DTYPE), _rows(br, d)), (_sds((r, d), MXU_DTYPE), _rows(br, d)),
                  (_sds((r, 2 * d), MXU_DTYPE), _rows(br, 2 * d)), (_sds((1, 2 * d), F32), _whole((1, 2 * d)))],
                 sem=("arbitrary",))


def _conv_taps(gp, halo, first_block):
    halo = jnp.where(first_block, 0.0, halo)
    rid = lax.broadcasted_iota(jnp.int32, gp.shape, 0)
    g1 = jnp.where(rid == 0, halo[7:8, :], pltpu.roll(gp, 1, axis=0))
    g2 = jnp.where(rid == 0, halo[6:7, :], jnp.where(rid == 1, halo[7:8, :], pltpu.roll(gp, 2, axis=0)))
    return g1, g2


def _prev_halo(br, c):
    return pl.BlockSpec((8, c), lambda i: (jnp.maximum(i * (br // 8) - 1, 0), 0))


def _glu_fwd(up, conv_w, conv_b):
    r = up.shape[0]
    c = D_FF
    br = ROW_BLOCK

    def body(gp_ref, halo_ref, val_ref, w_ref, b_ref, o_ref):
        gp = gp_ref[...]
        g1, g2 = _conv_taps(gp, halo_ref[...], pl.program_id(0) == 0)
        gate = w_ref[0:1, :] * g2 + w_ref[1:2, :] * g1 + w_ref[2:3, :] * gp + b_ref[...]
        o_ref[...] = (gate * jax.nn.sigmoid(gate) * val_ref[...]).astype(o_ref.dtype)

    return _call("glu_fwd", body, (r // br,),
                 [(up, _rows(br, c, 0)), (up, _prev_halo(br, c)), (up, _rows(br, c, 1)),
                  (conv_w, _whole((3, c))), (conv_b.reshape(1, c), _whole((1, c)))],
                 [(_sds((r, c), MXU_DTYPE), _rows(br, c))], sem=("parallel",))[0]


def _glu_bwd_gate(up, conv_w, conv_b, d_act):
    r = up.shape[0]
    c = D_FF
    br = ROW_BLOCK

    def body(gp_ref, halo_ref, val_ref, w_ref, b_ref, da_ref, dg_ref, dv_ref, dw_ref, db_ref):
        gp = gp_ref[...]
        g1, g2 = _conv_taps(gp, halo_ref[...], pl.program_id(0) == 0)
        gate = w_ref[0:1, :] * g2 + w_ref[1:2, :] * g1 + w_ref[2:3, :] * gp + b_ref[...]
        sg = jax.nn.sigmoid(gate)
        da = da_ref[...]
        dv_ref[...] = (da * (gate * sg)).astype(dv_ref.dtype)
        dg = da * val_ref[...] * (sg * (1.0 + gate * (1.0 - sg)))
        dg_ref[...] = dg

        @pl.when(pl.program_id(0) == 0)
        def _():
            dw_ref[...] = jnp.zeros_like(dw_ref)
            db_ref[...] = jnp.zeros_like(db_ref)

        dw_ref[0:1, :] += jnp.sum(dg * g2, axis=0, keepdims=True)
        dw_ref[1:2, :] += jnp.sum(dg * g1, axis=0, keepdims=True)
        dw_ref[2:3, :] += jnp.sum(dg * gp, axis=0, keepdims=True)
        db_ref[...] += jnp.sum(dg, axis=0, keepdims=True)

    return _call("glu_bwd_gate", body, (r // br,),
                 [(up, _rows(br, c, 0)), (up, _prev_halo(br, c)), (up, _rows(br, c, 1)),
                  (conv_w, _whole((3, c))), (conv_b.reshape(1, c), _whole((1, c))), (d_act, _rows(br, c))],
                 [(_sds((r, c), F32), _rows(br, c)), (_sds((r, c), MXU_DTYPE), _rows(br, c)),
                  (_sds((8, c), F32), _whole((8, c))), (_sds((1, c), F32), _whole((1, c)))],
                 sem=("arbitrary",))


def _glu_bwd_conv(dg, dval, conv_w):
    r, c = dg.shape
    br = ROW_BLOCK
    nb = r // br

    def body(dg_ref, nxt_ref, dv_ref, w_ref, o_ref):
        x = dg_ref[...]
        nxt = jnp.where(pl.program_id(0) == nb - 1, 0.0, nxt_ref[...])
        rid = lax.broadcasted_iota(jnp.int32, x.shape, 0)
        u1 = jnp.where(rid == br - 1, nxt[0:1, :], pltpu.roll(x, br - 1, axis=0))
        u2 = jnp.where(rid == br - 1, nxt[1:2, :], jnp.where(rid == br - 2, nxt[0:1, :], pltpu.roll(x, br - 2, axis=0)))
        dgp = w_ref[2:3, :] * x + w_ref[1:2, :] * u1 + w_ref[0:1, :] * u2
        o_ref[:, 0:c] = dgp.astype(o_ref.dtype)
        o_ref[:, c:2 * c] = dv_ref[...]

    nxt_spec = pl.BlockSpec((8, c), lambda i: (jnp.minimum((i + 1) * (br // 8), r // 8 - 1), 0))
    return _call("glu_bwd_conv", body, (nb,),
                 [(dg, _rows(br, c)), (dg, nxt_spec), (dval, _rows(br, c)), (conv_w, _whole((3, c)))],
                 [(_sds((r, 2 * c), MXU_DTYPE), _rows(br, 2 * c))], sem=("parallel",))[0]


def _token_specs(seq, d):
    br = ROW_BLOCK
    nxb = seq // br
    main = pl.BlockSpec((br, d), lambda i: (jnp.minimum(i, nxb - 1), 0))
    tail = pl.BlockSpec((N_META, d), lambda i: (jnp.clip(i * (br // N_META) - 1, 0, seq // N_META - 1), 0))
    return main, tail


def _padded_block(main_ref, tail_ref, first, seq):
    br = ROW_BLOCK
    i = pl.program_id(0)
    nxb = seq // br
    main = jnp.where(i < nxb, main_ref[...], 0.0)
    head = jnp.where(i == 0, first, jnp.where(i <= nxb, tail_ref[...], 0.0))
    return jnp.concatenate([head, main[:br - N_META]], axis=0)


def _ln_emb_fwd(x, meta, g, b, rows, after=()):
    seq, d = x.shape
    br = ROW_BLOCK
    assert seq % br == 0 and br % N_META == 0 and rows % br == 0

    def body(x_ref, tail_ref, meta_ref, g_ref, b_ref, y_ref, yb_ref):
        z = _padded_block(x_ref, tail_ref, meta_ref[...], seq)
        xhat, _ = _ln_stats(z)
        y = xhat * g_ref[...] + b_ref[...]
        y_ref[...] = y
        yb_ref[...] = y.astype(yb_ref.dtype)

    main, tail = _token_specs(seq, d)
    return _call("ln_emb_fwd", body, (rows // br,),
                 [(x, main), (x, tail), (meta, _whole((N_META, d))), (g.reshape(1, d), _whole((1, d))),
                  (b.reshape(1, d), _whole((1, d)))],
                 [(_sds((rows, d), F32), _rows(br, d)), (_sds((rows, d), MXU_DTYPE), _rows(br, d))],
                 sem=("parallel",), after=after)


def _ln_emb_bwd(x, meta, dh0, g):
    seq, d = x.shape
    br = ROW_BLOCK
    step = br // N_META

    def ln_bwd(z, dy, gv):
        xhat, rstd = _ln_stats(z)
        dyg = dy * gv
        m1 = jnp.mean(dyg, axis=-1, keepdims=True)
        m2 = jnp.mean(dyg * xhat, axis=-1, keepdims=True)
        dz = rstd * (dyg - m1 - xhat * m2)
        return dz, jnp.sum(dy * xhat, axis=0, keepdims=True), jnp.sum(dy, axis=0, keepdims=True)

    def body(x_ref, dh_ref, nxt_ref, meta_ref, top_ref, g_ref, dx_ref, dm_ref, dg_ref, db_ref):
        gv = g_ref[...]
        dy = jnp.concatenate([dh_ref[N_META:, :], nxt_ref[...]], axis=0)
        dz, dg, db = ln_bwd(x_ref[...], dy, gv)
        dx_ref[...] = dz

        @pl.when(pl.program_id(0) == 0)
        def _():
            dzm, dgm, dbm = ln_bwd(meta_ref[...], top_ref[...], gv)
            dm_ref[...] = dzm
            dg_ref[...] = dgm
            db_ref[...] = dbm

        dg_ref[...] += dg
        db_ref[...] += db

    small = _whole((N_META, d))
    return _call("ln_emb_bwd", body, (seq // br,),
                 [(x, _rows(br, d)), (dh0, _rows(br, d)), (dh0, pl.BlockSpec((N_META, d), lambda i: ((i + 1) * step, 0))),
                  (meta, small), (dh0, small), (g.reshape(1, d), _whole((1, d)))],
                 [(_sds((seq, d), F32), _rows(br, d)), (_sds((N_META, d), F32), small),
                  (_sds((1, d), F32), _whole((1, d))), (_sds((1, d), F32), _whole((1, d)))], sem=("arbitrary",))


def _ln_ffn_loss(h1, f, tgt, g, b):
    r, d = h1.shape
    seq = tgt.shape[0]
    br = ROW_BLOCK

    def body(a_ref, r_ref, t_ref, tail_ref, g_ref, b_ref, l_ref):
        err = _loss_err(a_ref, r_ref, t_ref, tail_ref, g_ref, b_ref, seq)[0]

        @pl.when(pl.program_id(0) == 0)
        def _():
            l_ref[...] = jnp.zeros_like(l_ref)

        l_ref[...] += jnp.sum(jnp.sum(err * err, axis=1, keepdims=True), axis=0, keepdims=True) * (0.5 / d)

    main, tail = _token_specs(seq, d)
    return _call("ln_ffn_loss", body, (r // br,),
                 [(h1, _rows(br, d)), (f, _rows(br, d)), (tgt, main), (tgt, tail),
                  (g.reshape(1, d), _whole((1, d))), (b.reshape(1, d), _whole((1, d)))],
                 [(_sds((1, 1), F32), _whole((1, 1)))], sem=("arbitrary",))[0]


def _loss_err(a_ref, r_ref, t_ref, tail_ref, g_ref, b_ref, seq):
    br, d = a_ref.shape
    xhat, rstd = _ln_stats(ALPHA * a_ref[...] + r_ref[...])
    y = xhat * g_ref[...] + b_ref[...]
    t = _padded_block(t_ref, tail_ref, jnp.zeros((N_META, d), F32), seq)
    rid = lax.broadcasted_iota(jnp.int32, (br, d), 0) + pl.program_id(0) * br
    valid = (rid >= N_META) & (rid < N_META + seq)
    return jnp.where(valid, y - t, 0.0), xhat, rstd


def _ln_ffn_bwd(h1, f, tgt, g, b):
    r, d = h1.shape
    seq = tgt.shape[0]
    br = ROW_BLOCK

    def body(a_ref, r_ref, t_ref, tail_ref, g_ref, b_ref, dz_ref, dzb_ref, dg_ref, db_ref):
        err, xhat, rstd = _loss_err(a_ref, r_ref, t_ref, tail_ref, g_ref, b_ref, seq)
        dyv = err * (1.0 / d)
        dyg = dyv * g_ref[...]
        m1 = jnp.mean(dyg, axis=-1, keepdims=True)
        m2 = jnp.mean(dyg * xhat, axis=-1, keepdims=True)
        dz = rstd * (dyg - m1 - xhat * m2)
        dz_ref[...] = dz
        dzb_ref[...] = dz.astype(dzb_ref.dtype)

        @pl.when(pl.program_id(0) == 0)
        def _():
            dg_ref[...] = jnp.zeros_like(dg_ref)
            db_ref[...] = jnp.zeros_like(db_ref)

        dg_ref[...] += jnp.sum(dyv * xhat, axis=0, keepdims=True)
        db_ref[...] += jnp.sum(dyv, axis=0, keepdims=True)

    main, tail = _token_specs(seq, d)
    return _call("ln_ffn_bwd", body, (r // br,),
                 [(h1, _rows(br, d)), (f, _rows(br, d)), (tgt, main), (tgt, tail),
                  (g.reshape(1, d), _whole((1, d))), (b.reshape(1, d), _whole((1, d)))],
                 [(_sds((r, d), F32), _rows(br, d)), (_sds((r, d), MXU_DTYPE), _rows(br, d)),
                  (_sds((1, d), F32), _whole((1, d))), (_sds((1, d), F32), _whole((1, d)))], sem=("arbitrary",))


def _attn_fwd(name, q, k, v, cum_b=None, cum_t=None):
    (qa, qg), (ka, kg), (va, vg) = q, k, v
    r = qa.shape[0]
    tq, tk = ATT_TQ, ATT_TK
    nq, nk = r // tq, r // tk
    bias = cum_b is not None

    def body(*refs):
        if bias:
            q_ref, k_ref, vt_ref, cb_ref, ct_ref, o_ref, ob_ref, lse_ref = refs
        else:
            q_ref, k_ref, vt_ref, o_ref, ob_ref, lse_ref = refs
        i = pl.program_id(1)
        qs = [q_ref[:, _hs(hh)] for hh in range(hg)]
        cqs = [ct_ref[hh] for hh in range(hg)] if bias else None
        diff = lax.broadcasted_iota(jnp.int32, (tk, tq), 0) - lax.broadcasted_iota(jnp.int32, (tk, tq), 1)

        def step(j, carry, masked):
            keys = pl.ds(pl.multiple_of(j * tk, tk), tk)
            out = []
            for hh in range(hg):
                m, l, acc = carry[hh]
                kt = k_ref[keys, _hs(hh)]
                s = lax.dot_general(kt, qs[hh], NT, preferred_element_type=F32)
                if bias:
                    s = s + (cqs[hh] - cb_ref[keys, hh * HP:hh * HP + 1])
                if masked:
                    s = jnp.where(diff <= i * tq - j * tk, s, NEG_INF)
                m_new = jnp.maximum(m, jnp.max(s, axis=0, keepdims=True))
                p = jnp.exp(s - m_new)
                a = jnp.exp(m - m_new)
                l = a * l + jnp.sum(p, axis=0, keepdims=True)
                acc = a * acc + jnp.dot(vt_ref[j, _hs(hh), :], p.astype(kt.dtype), preferred_element_type=F32)
                out.append((m_new, l, acc))
            return tuple(out)

        n_clear = (i * tq + 1) // tk
        n_all = ((i + 1) * tq - 1) // tk + 1
        carry = tuple((jnp.full((1, tq), NEG_INF, F32), jnp.zeros((1, tq), F32), jnp.zeros((HP, tq), F32))
                      for _ in range(hg))
        carry = lax.fori_loop(0, n_clear, lambda j, c: step(j, c, False), carry)
        carry = lax.fori_loop(n_clear, n_all, lambda j, c: step(j, c, True), carry)
        for hh in range(hg):
            m, l, acc = carry[hh]
            o = (acc / l).T
            o_ref[:, _hs(hh)] = o
            ob_ref[:, _hs(hh)] = o.astype(ob_ref.dtype)
            lse_ref[hh] = m + jnp.log(l)

    hg = ATT_HEADS
    w = hg * HP
    gpw = HW // w
    tile = lambda g: pl.BlockSpec((tq, w), lambda h, i: (i, g * gpw + h))
    res = lambda g: pl.BlockSpec((r, w), lambda h, i: (0, g * gpw + h))
    v_t = _key_tiles_transposed(name + "_vt", va, vg)
    ins = [(qa, tile(qg)), (ka, res(kg)), (v_t, pl.BlockSpec((nk, w, tk), lambda h, i: (0, h, 0)))]
    if bias:
        ins += [(cum_b, res(0)),
                (cum_t.reshape(HEADS, nq, 1, tq), pl.BlockSpec((hg, None, 1, tq), lambda h, i: (h, i, 0, 0)))]
    outs = [(_sds((r, HW), F32), tile(0)), (_sds((r, HW), MXU_DTYPE), tile(0)),
            (_sds((HEADS, nq, 1, tq), F32), pl.BlockSpec((hg, None, 1, tq), lambda h, i: (h, i, 0, 0)))]
    o, ob, lse = _call(name, body, (gpw, nq), ins, outs, sem=("parallel", "parallel"))
    return o, ob, lse.reshape(HEADS, r)


def _key_tiles_transposed(name, a, group):
    r = a.shape[0]
    tk = ATT_TK

    def body(x_ref, o_ref):
        for h in range(HEADS):
            o_ref[_hs(h), :] = x_ref[:, _hs(h)].astype(F32).T.astype(o_ref.dtype)

    return _call(name, body, (r // tk,),
                 [(a, pl.BlockSpec((tk, HW), lambda j: (j, group)))],
                 [(_sds((r // tk, HW, tk), a.dtype), pl.BlockSpec((None, HW, tk), lambda j: (j, 0, 0)))],
                 sem=("parallel",))[0]


def _attn_delta(name, do_b, o, after=()):
    r = do_b.shape[0]
    br = ROW_BLOCK

    def body(do_ref, o_ref, d_ref):
        lane = _lane_iota((br, HP))
        d = jnp.zeros((br, HP), F32)
        for h in range(HEADS):
            dh = do_ref[:, _hs(h)].astype(F32)
            d = jnp.where(lane == h, jnp.sum(dh * o_ref[:, _hs(h)], axis=1, keepdims=True), d)
        d_ref[...] = d.T[0:HEADS, :]

    wide = _rows(br, HW)
    return _call(name, body, (r // br,), [(do_b, wide), (o, wide)],
                 [(_sds((HEADS, r), F32), pl.BlockSpec((HEADS, br), lambda i: (0, i)))],
                 sem=("parallel",), after=after)[0]


def _attn_bwd(name, q, k, v, do_b, lse_t, delta_t, cum_b=None, cum_t=None, out_dtype=F32):
    (qa, qg), (ka, kg), (va, vg) = q, k, v
    r = qa.shape[0]
    tq, tk = ATT_TQ, ATT_TK
    nq, nk = r // tq, r // tk
    bias = cum_b is not None

    def body(*refs):
        if bias:
            (q_ref, k_ref, v_ref, do_ref, lse_ref, dl_ref, cb_ref, ct_ref,
             dq_ref, dk_ref, dv_ref, dcq_ref, dck_ref, dqt_ref) = refs
        else:
            q_ref, k_ref, v_ref, do_ref, lse_ref, dl_ref, dq_ref, dk_ref, dv_ref, dqt_ref = refs
        j = pl.program_id(1)

        @pl.when(j == 0)
        def _():
            dqt_ref[...] = jnp.zeros_like(dqt_ref)
            if bias:
                dcq_ref[...] = jnp.zeros_like(dcq_ref)

        kts = [k_ref[:, _hs(hh)] for hh in range(hg)]
        vts = [v_ref[:, _hs(hh)] for hh in range(hg)]
        k_trs = [kt.astype(F32).T.astype(kt.dtype) for kt in kts]
        cks = [cb_ref[:, hh * HP:hh * HP + 1] for hh in range(hg)] if bias else None
        diff = lax.broadcasted_iota(jnp.int32, (tk, tq), 0) - lax.broadcasted_iota(jnp.int32, (tk, tq), 1)

        def step(i, carry, masked):
            rows = pl.ds(pl.multiple_of(i * tq, tq), tq)
            out = []
            for hh in range(hg):
                dk_acc, dv_acc, dck_acc = carry[hh]
                qt = q_ref[rows, _hs(hh)]
                dot = do_ref[rows, _hs(hh)]
                s = lax.dot_general(kts[hh], qt, NT, preferred_element_type=F32)
                if bias:
                    s = s + (ct_ref[hh, i] - cks[hh])
                if masked:
                    s = jnp.where(diff <= i * tq - j * tk, s, NEG_INF)
                p = jnp.exp(s - lse_ref[hh, i])
                dp = lax.dot_general(vts[hh], dot, NT, preferred_element_type=F32)
                ds = p * (dp - dl_ref[hh, i])
                pb = p.astype(dot.dtype)
                dsb = ds.astype(qt.dtype)
                dv_acc = dv_acc + jnp.dot(pb, dot, preferred_element_type=F32)
                dk_acc = dk_acc + jnp.dot(dsb, qt, preferred_element_type=F32)
                dqt_ref[hh, i] += jnp.dot(k_trs[hh], dsb, preferred_element_type=F32)
                if bias:
                    dcq_ref[hh, i] += jnp.sum(ds, axis=0, keepdims=True)
                    dck_acc = dck_acc - jnp.sum(ds, axis=1, keepdims=True)
                out.append((dk_acc, dv_acc, dck_acc))
            return tuple(out)

        i_first = (j * tk) // tq
        i_clear = jnp.minimum(((j + 1) * tk + tq - 2) // tq, nq)
        carry = tuple((jnp.zeros((tk, HP), F32), jnp.zeros((tk, HP), F32), jnp.zeros((tk, 1), F32)) for _ in range(hg))
        carry = lax.fori_loop(i_first, i_clear, lambda i, c: step(i, c, True), carry)
        carry = lax.fori_loop(i_clear, nq, lambda i, c: step(i, c, False), carry)
        for hh in range(hg):
            dk_acc, dv_acc, dck_acc = carry[hh]
            dk_ref[:, _hs(hh)] = dk_acc.astype(dk_ref.dtype)
            dv_ref[:, _hs(hh)] = dv_acc.astype(dv_ref.dtype)
            if bias:
                dck_ref[:, _hs(hh)] = jnp.broadcast_to(dck_acc, (tk, HP))

        @pl.when(j == nk - 1)
        def _():
            for hh in range(hg):
                for i in range(nq):
                    dq_ref[i * tq:(i + 1) * tq, _hs(hh)] = dqt_ref[hh, i].T.astype(dq_ref.dtype)

    hg = ATT_HEADS
    w = hg * HP
    gpw = HW // w
    res = lambda g: pl.BlockSpec((r, w), lambda h, j: (0, g * gpw + h))
    tile = lambda g: pl.BlockSpec((tk, w), lambda h, j: (j, g * gpw + h))
    rowv = pl.BlockSpec((hg, nq, 1, tq), lambda h, j: (h, 0, 0, 0))
    as_rows = lambda a: a.reshape(HEADS, nq, 1, tq)
    ins = [(qa, res(qg)), (ka, tile(kg)), (va, tile(vg)), (do_b, res(0)), (as_rows(lse_t), rowv), (as_rows(delta_t), rowv)]
    outs = [(_sds((r, HW), out_dtype), res(0)), (_sds((r, HW), out_dtype), tile(0)), (_sds((r, HW), out_dtype), tile(0))]
    if bias:
        ins += [(cum_b, tile(0)), (as_rows(cum_t), rowv)]
        outs += [(_sds((HEADS, nq, 1, tq), F32), rowv), (_sds((r, HW), F32), tile(0))]
    res_out = _call(name, body, (gpw, nk), ins, outs, scratch=[pltpu.VMEM((hg, nq, HP, tq), F32)],
                    sem=("parallel", "arbitrary"))
    if bias:
        dq, dk, dv, dcq, dck = res_out
        return dq, dk, dv, dcq.reshape(HEADS, r), dck
    return res_out


MESH_ID = pl.DeviceIdType.MESH
ANY = pl.BlockSpec(memory_space=pl.ANY)


def _allgather(name, shards):
    n = len(shards)

    def body(*refs):
        x_refs, out_refs = refs[:n], refs[n:2 * n]
        send_sems, recv_sems, local_sems = refs[2 * n:]
        x, y, c = lax.axis_index("x"), lax.axis_index("y"), lax.axis_index("c")
        me, sibling = (x, y, c), (x, y, 1 - c)
        chips = [(1 - x, y), (x, 1 - y), (1 - x, 1 - y)]

        def slot(ti, px, py, pc):
            return out_refs[ti].at[4 * px + 2 * py + pc]

        def copy(ti, k, block, to, src=None):
            return pltpu.make_async_remote_copy(
                src_ref=slot(ti, *block) if src is None else src, dst_ref=slot(ti, *block),
                send_sem=send_sems.at[ti, k], recv_sem=recv_sems.at[ti, k], device_id=to, device_id_type=MESH_ID)

        mine = [pltpu.make_async_copy(x_refs[ti], slot(ti, *me), local_sems.at[ti]) for ti in range(n)]
        for cp in mine:
            cp.start()
        started = []
        for ti in range(n):
            first = [copy(ti, 0, me, sibling, src=x_refs[ti])]
            first += [copy(ti, 1 + j, me, (*chip, c), src=x_refs[ti]) for j, chip in enumerate(chips)]
            for cp in first:
                cp.start()
            started += first
        for ti in range(n):
            for j, chip in enumerate(chips):
                copy(ti, 1 + j, (*chip, c), me).wait_recv()
                fwd = copy(ti, 4 + j, (*chip, c), sibling)
                fwd.start()
                started.append(fwd)
        for ti in range(n):
            copy(ti, 0, sibling, me).wait_recv()
            for j, chip in enumerate(chips):
                copy(ti, 4 + j, (*chip, 1 - c), me).wait_recv()
        for cp in started:
            cp.wait_send()
        for cp in mine:
            cp.wait()

    return pl.pallas_call(
        body, name=name, out_shape=[_sds((N_DEV,) + s.shape, s.dtype) for s in shards],
        in_specs=[ANY] * n, out_specs=[ANY] * n,
        scratch_shapes=[pltpu.SemaphoreType.DMA((n, 7)), pltpu.SemaphoreType.DMA((n, 7)), pltpu.SemaphoreType.DMA((n,))],
    )(*shards)


HBM = pl.BlockSpec(memory_space=pltpu.HBM)
SEM = pl.BlockSpec(memory_space=pltpu.SEMAPHORE)
EFFECT = pltpu.SideEffectType.DATAFLOW_SIDE_EFFECTING
N_PEER = N_DEV - 1


def _my_id():
    return 4 * lax.axis_index("x") + 2 * lax.axis_index("y") + lax.axis_index("c")


def _peers():
    x, y, c = lax.axis_index("x"), lax.axis_index("y"), lax.axis_index("c")
    out = []
    for k in range(1, N_DEV):
        px, py, pc = (1 - x if k & 4 else x, 1 - y if k & 2 else y, 1 - c if k & 1 else c)
        out.append(((px, py, pc), 4 * px + 2 * py + pc))
    return out


def _push_copies(src_refs, land_refs, send_sems, recv_sems, scatter, landing):
    me = _my_id()
    out = []
    for ti, (src, land) in enumerate(zip(src_refs, land_refs)):
        for k, (dev, pid) in enumerate(_peers()):
            out.append(pltpu.make_async_remote_copy(
                src_ref=src.at[pid] if scatter else src, dst_ref=land.at[pid if landing else me],
                send_sem=send_sems.at[ti * N_PEER + k], recv_sem=recv_sems.at[ti * N_PEER + k],
                device_id=dev, device_id_type=MESH_ID))
    return out


def _push_start(name, srcs, scatter, after=None):
    n = len(srcs)
    slot = lambda s: s.shape[1:] if scatter else s.shape
    lands = [lax.empty((N_DEV,) + slot(s), s.dtype) for s in srcs]
    n_after = 0 if after is None else 1

    def body(*refs):
        src_refs, land_refs = refs[:n], refs[n:2 * n]
        send_sems, recv_sems = refs[2 * n + n_after], refs[2 * n + n_after + 1]
        token = refs[-1]
        for cp in _push_copies(src_refs, land_refs, send_sems, recv_sems, scatter, False):
            cp.start()
        token[...] = jnp.zeros_like(token)

    hbm = lambda a: pltpu.with_memory_space_constraint(a, pltpu.HBM)
    operands = [hbm(a) for a in srcs + lands] + ([after] if n_after else [])
    res = pl.pallas_call(
        body, name=name,
        out_shape=[pltpu.SemaphoreType.DMA((n * N_PEER,)), pltpu.SemaphoreType.DMA((n * N_PEER,))]
        + [pltpu.HBM(a.shape, a.dtype) for a in srcs + lands] + [_sds((8, 128), F32)],
        in_specs=[HBM] * (2 * n) + [ANY] * n_after,
        out_specs=[SEM, SEM] + [HBM] * (2 * n) + [pl.BlockSpec(memory_space=pltpu.VMEM)],
        input_output_aliases={i: 2 + i for i in range(2 * n)},
        compiler_params=pltpu.CompilerParams(has_side_effects=EFFECT),
    )(*operands)
    return (res[0], res[1], list(res[2:2 + n]), list(res[2 + n:2 + 2 * n]), scatter), res[-1]


def _push_wait(name, handle, after):
    send_sems, recv_sems, srcs, lands, scatter = handle
    n = len(srcs)

    def body(*refs):
        src_refs, land_refs = refs[:n], refs[n:2 * n]
        s_sems, r_sems = refs[2 * n], refs[2 * n + 1]
        for cp in _push_copies(src_refs, land_refs, s_sems, r_sems, scatter, True):
            cp.wait_send()
            cp.wait_recv()

    res = pl.pallas_call(
        body, name=name,
        out_shape=[pltpu.HBM(a.shape, a.dtype) for a in srcs + lands],
        in_specs=[HBM] * (2 * n) + [SEM, SEM, ANY], out_specs=[HBM] * (2 * n),
        input_output_aliases={i: i for i in range(2 * n)},
        compiler_params=pltpu.CompilerParams(has_side_effects=EFFECT),
    )(*srcs, *lands, send_sems, recv_sems, after)
    return list(res[n:])


def _adamw(name, parts, w, m, v, own=None):
    r, c = w.shape
    br = _pick(r, 256, 16)
    has_own = own is not None

    def body(*refs):
        if has_own:
            p_ref, own_ref, w_ref, m_ref, v_ref, g_ref, d_ref, nm_ref, nv_ref = refs
            me = _my_id()
            mine = own_ref[...].astype(F32)
        else:
            p_ref, w_ref, m_ref, v_ref, g_ref, d_ref, nm_ref, nv_ref = refs
        g = None
        for k in range(N_DEV):
            t = p_ref[k].astype(F32)
            if has_own:
                t = jnp.where(me == k, mine, t)
            g = t if g is None else g + t
        mm = ADAM_B1 * m_ref[...] + (1.0 - ADAM_B1) * g
        vv = ADAM_B2 * v_ref[...] + (1.0 - ADAM_B2) * (g * g)
        m_hat = mm / (1.0 - ADAM_B1 ** ADAM_STEP)
        v_hat = vv / (1.0 - ADAM_B2 ** ADAM_STEP)
        g_ref[...] = g
        d_ref[...] = -ADAM_LR * (m_hat / (jnp.sqrt(v_hat) + ADAM_EPS) + ADAM_WD * w_ref[...])
        nm_ref[...] = mm
        nv_ref[...] = vv

    spec = _rows(br, c)
    out = (_sds((r, c), F32), spec)
    ins = [(parts, pl.BlockSpec((N_DEV, br, c), lambda i: (0, i, 0)))] + ([(own, spec)] if has_own else [])
    return _call(name, body, (r // br,), ins + [(w, spec), (m, spec), (v, spec)], [out] * 4, sem=("parallel",))


def _pad_head_cols(w, d):
    k = w.shape[0]
    return jnp.pad(w.reshape(k, HEADS, d), ((0, 0), (0, 0), (0, HP - d))).reshape(k, HW)


def _unpad_head_cols(wp, d):
    k = wp.shape[0]
    return wp.reshape(k, HEADS, HP)[:, :, :d].reshape(k, HEADS * d)


def _pad_head_rows(w, d):
    n = w.shape[1]
    return jnp.pad(w.reshape(HEADS, d, n), ((0, 0), (0, HP - d), (0, 0))).reshape(HW, n)


def _unpad_head_rows(wp, d):
    n = wp.shape[1]
    return wp.reshape(HEADS, HP, n)[:, :d, :].reshape(HEADS * d, n)


IN_SEGS = (("q", Q_RANK), ("kv", KV_RANK), ("kr", ROPE), ("fq", FOX_W), ("fk", FOX_W), ("fv", FOX_W),
           ("fl", HEADS), ("gate", 2 * D_MODEL))


def _split_w_in(w):
    seg = {}
    o = 0
    for nm, wd in IN_SEGS:
        seg[nm] = w[:, o:o + wd]
        o += wd
    d = w.shape[0]
    z = lambda n: jnp.zeros((d, n), w.dtype)
    seg["fq"] = seg["fq"] * jnp.asarray(FOX_SCALE, w.dtype)
    fused = jnp.concatenate([_pad_head_cols(seg[nm], FOX_DIM) for nm in ("fq", "fk", "fv")], axis=1)
    last = jnp.concatenate([seg["fl"], z(LANE_PE - HEADS), seg["kr"], z(HP - LANE_PE - ROPE)], axis=1)
    rest = jnp.concatenate([seg["q"], seg["kv"], last, z(R_GATE - R_LAST - HP), seg["gate"]], axis=1)
    return fused, rest


def _merge_w_in(fused, rest):
    f = [_unpad_head_cols(fused[:, i * HW:(i + 1) * HW], FOX_DIM) for i in range(3)]
    f[0] = f[0] * jnp.asarray(FOX_SCALE, fused.dtype)
    last = rest[:, R_LAST:R_LAST + HP]
    return jnp.concatenate([rest[:, R_QLAT:R_LAST], last[:, LANE_PE:LANE_PE + ROPE], f[0], f[1], f[2],
                            last[:, LANE_FL:LANE_FL + HEADS], rest[:, R_GATE:]], axis=1)


def _split_w_kv(w):
    k = w.shape[0]
    w3 = w.reshape(k, HEADS, NOPE + V_DIM)
    padl = lambda a: jnp.pad(a, ((0, 0), (0, 0), (0, HP - a.shape[-1]))).reshape(k, HW)
    return padl(w3[..., :NOPE]), padl(w3[..., NOPE:])


def _merge_w_kv(wk, wv):
    k = wk.shape[0]
    return jnp.concatenate([wk.reshape(k, HEADS, HP)[..., :NOPE], wv.reshape(k, HEADS, HP)[..., :V_DIM]],
                           axis=-1).reshape(k, HEADS * (NOPE + V_DIM))


class _NoComm:
    first_token = ()

    def late_weights(self, after):
        return {}

    def send(self, name, grads):
        return ()


def _local_step(x, tgt, p, comm=_NoComm()):
    seq = x.shape[0]
    r = -(-(N_META + seq) // ROW_ALIGN) * ROW_ALIGN
    cd = MXU_DTYPE
    p = dict(p)

    w_f, w_r = _split_w_in(p["w_in"])
    w_q = _pad_head_cols(p["w_q_up"], QK_DIM)
    w_k, w_v = _split_w_kv(p["w_kv_up"])

    pos = jnp.arange(r, dtype=F32)
    inv_freq = ROPE_THETA ** (-jnp.arange(HALF, dtype=F32) / HALF)
    ang = pos[:, None] * inv_freq[None, :]
    cos_t = jnp.tile(jnp.cos(ang), (1, HP // HALF))
    sin_t = jnp.tile(jnp.sin(ang), (1, HP // HALF))
    bf_row = jnp.zeros((1, HP), F32).at[0, LANE_FL:LANE_FL + HEADS].set(p["b_forget"])

    h0, h0b = _ln_emb_fwd(x, p["meta_tokens"], p["ln_emb_g"], p["ln_emb_b"], r, after=comm.first_token)
    proj_f = _matmul("in_proj_f", h0b, w_f, out_dtype=cd)
    proj_r = _matmul("in_proj_r", h0b, w_r)
    ql = _rms_fwd("q_norm_fwd", proj_r, R_QLAT // Q_RANK, Q_RANK, p["q_norm_g"])
    kvl = _rms_fwd("kv_norm_fwd", proj_r, R_KVLAT // KV_RANK, KV_RANK, p["kv_norm_g"])
    q_raw = _matmul("q_up", ql, w_q)
    k_part = _matmul("k_up", kvl, w_k)
    v_mla = _matmul("v_up", kvl, w_v, out_dtype=cd)
    q_mla, k_mla = _rope_fwd(q_raw, k_part, proj_r, cos_t, sin_t)
    o_mla, o_mla_b, lse_mla = _attn_fwd("mla_fwd", (q_mla, 0), (k_mla, 0), (v_mla, 0))

    cum, cum_t = _forget_fwd(proj_r, bf_row)
    o_fox, o_fox_b, lse_fox = _attn_fwd("fox_fwd", (proj_f, 0), (proj_f, 1), (proj_f, 2), cum, cum_t)

    p.update(comm.late_weights(o_fox_b))
    w_bm = _pad_head_rows(p["w_branch_mla"], V_DIM)
    w_bf = _pad_head_rows(p["w_branch_fox"], FOX_DIM)
    bm = _matmul("branch_mla", o_mla_b, w_bm)
    bfx = _matmul("branch_fox", o_fox_b, w_bf)
    merged = _gate_fwd(proj_r, p["b_gate"], bm, bfx)
    mix = _matmul("out_proj", merged, p["w_out"])
    h1, h1b = _ln_fwd("ln_mix_fwd", h0, mix, p["ln_mix_g"], p["ln_mix_b"])
    up = _matmul("ffn_up", h1b, p["w_ffn_up"])
    act = _glu_fwd(up, p["conv_w"], p["conv_b"])
    f = _matmul("ffn_down", act, p["w_ffn_down"])
    loss = _ln_ffn_loss(h1, f, tgt, p["ln_ffn_g"], p["ln_ffn_b"])

    g = {}
    dz2, dz2b, g["ln_ffn_g"], g["ln_ffn_b"] = _ln_ffn_bwd(h1, f, tgt, p["ln_ffn_g"], p["ln_ffn_b"])
    d_act = _matmul("ffn_down_dx", dz2b, p["w_ffn_down"], tb=True)
    g["w_ffn_down"] = _matmul("ffn_down_dw", act, dz2b, ta=True, out_dtype=cd)
    dgate, dval, dcw, g["conv_b"] = _glu_bwd_gate(up, p["conv_w"], p["conv_b"], d_act)
    g["conv_w"] = dcw[:3]
    d_up = _glu_bwd_conv(dgate, dval, p["conv_w"])
    dh1 = _matmul("ffn_up_dx", d_up, p["w_ffn_up"], tb=True, addend=dz2, alpha=ALPHA)
    g["w_ffn_up"] = _matmul("ffn_up_dw", h1b, d_up, ta=True, out_dtype=cd)
    sent = comm.send("ffn", {n: g[n] for n in ("w_ffn_down", "w_ffn_up", "conv_w")})
    dz1, dz1b, g["ln_mix_g"], g["ln_mix_b"] = _ln_bwd("ln_mix_bwd", h0, mix, dh1, p["ln_mix_g"], after=sent)
    dmerged = _matmul("out_proj_dx", dz1b, p["w_out"], tb=True)
    g["w_out"] = _matmul("out_proj_dw", merged, dz1b, ta=True, out_dtype=cd)
    d_bm, d_bf, d_gl, g["b_gate"] = _gate_bwd(proj_r, p["b_gate"], bm, bfx, dmerged)
    do_mla_b = _matmul("branch_mla_dx", d_bm, w_bm, tb=True, out_dtype=cd)
    g["w_branch_mla"] = _unpad_head_rows(_matmul("branch_mla_dw", o_mla_b, d_bm, ta=True, out_dtype=cd), V_DIM)
    do_fox_b = _matmul("branch_fox_dx", d_bf, w_bf, tb=True, out_dtype=cd)
    g["w_branch_fox"] = _unpad_head_rows(_matmul("branch_fox_dw", o_fox_b, d_bf, ta=True, out_dtype=cd), FOX_DIM)

    sent = comm.send("mix", {n: g[n] for n in ("w_out", "w_branch_mla", "w_branch_fox")})
    dl_mla = _attn_delta("mla_delta", do_mla_b, o_mla, after=sent)
    dq_m, dk_m, dv_m = _attn_bwd("mla_bwd", (q_mla, 0), (k_mla, 0), (v_mla, 0), do_mla_b, lse_mla, dl_mla)
    dl_fox = _attn_delta("fox_delta", do_fox_b, o_fox)
    dfq, dfk, dfv, dcq, dck = _attn_bwd("fox_bwd", (proj_f, 0), (proj_f, 1), (proj_f, 2), do_fox_b, lse_fox, dl_fox,
                                        cum, cum_t, out_dtype=cd)
    dfl, dbf = _forget_bwd(proj_r, bf_row, dcq, dck)
    g["b_forget"] = dbf[:, LANE_FL:LANE_FL + HEADS]

    dq_b, dk_b, dlast = _rope_bwd(dq_m, dk_m, dfl, cos_t, sin_t)
    dv_b = dv_m.astype(cd)
    d_ql = _matmul("q_up_dx", dq_b, w_q, tb=True)
    d_kvl = _matmul("k_up_dx", dk_b, w_k, tb=True)
    d_kvl = _matmul("v_up_dx", dv_b, w_v, tb=True, addend=d_kvl)
    d_qlat, g["q_norm_g"] = _rms_bwd("q_norm_bwd", proj_r, R_QLAT // Q_RANK, Q_RANK, d_ql, p["q_norm_g"])
    d_kvlat, g["kv_norm_g"] = _rms_bwd("kv_norm_bwd", proj_r, R_KVLAT // KV_RANK, KV_RANK, d_kvl, p["kv_norm_g"])
    dproj_f = jnp.concatenate([dfq, dfk, dfv], axis=1)
    dproj_r = jnp.concatenate([d_qlat, d_kvlat, dlast, jnp.zeros((r, R_GATE - R_LAST - HP), cd), d_gl], axis=1)
    g["w_in"] = _merge_w_in(_matmul("in_proj_f_dw", h0b, dproj_f, ta=True, out_dtype=cd), _matmul("in_proj_r_dw", h0b, dproj_r, ta=True, out_dtype=cd))
    sent = comm.send("in", {"w_in": g["w_in"]})
    dh0 = _matmul("in_proj_f_dx", dproj_f, w_f, tb=True, addend=dz1, alpha=ALPHA, after=sent)
    g["w_q_up"] = _unpad_head_cols(_matmul("q_up_dw", ql, dq_b, ta=True, out_dtype=cd, after=sent), QK_DIM)
    g["w_kv_up"] = _merge_w_kv(_matmul("k_up_dw", kvl, dk_b, ta=True, out_dtype=cd, after=sent),
                               _matmul("v_up_dw", kvl, dv_b, ta=True, out_dtype=cd, after=sent))
    sent = comm.send("qkv", {n: g[n] for n in ("w_q_up", "w_kv_up")})
    dh0 = _matmul("in_proj_r_dx", dproj_r, w_r, tb=True, addend=dh0, after=sent)
    grad_x, d_meta, g["ln_emb_g"], g["ln_emb_b"] = _ln_emb_bwd(x, p["meta_tokens"], dh0, p["ln_emb_g"])
    return loss, grad_x, d_meta, g


BIG = (("w_in", 1), ("w_q_up", 1), ("w_kv_up", 1), ("w_branch_mla", 1), ("w_branch_fox", 1), ("w_out", 0),
       ("w_ffn_up", 1), ("w_ffn_down", 0))
SMALL_SHARDED = (("meta_tokens", 1), ("conv_w", 1))
EARLY = ("w_in", "w_q_up", "w_kv_up", "meta_tokens", "conv_w")
LATE = ("w_branch_mla", "w_branch_fox", "w_out", "w_ffn_up", "w_ffn_down")
REPLICATED = ("ln_emb_g", "ln_emb_b", "b_gate", "b_forget", "q_norm_g", "kv_norm_g", "ln_mix_g", "ln_mix_b",
              "conv_b", "ln_ffn_g", "ln_ffn_b")
PACK_COLS = 1024


def _pack(flat_list):
    cat = jnp.concatenate(flat_list)
    n = cat.shape[0]
    rows = -(-n // (8 * PACK_COLS)) * 8
    return jnp.pad(cat, (0, rows * PACK_COLS - n)).reshape(rows, PACK_COLS)


def _gathered_full(g3, axis):
    n, r, c = g3.shape
    if axis == 0:
        return g3.reshape(n * r, c)
    return g3.transpose(1, 0, 2).reshape(r, n * c)


def _shard_major(full, axis):
    r, c = full.shape
    if axis == 0:
        return full.reshape(N_DEV, r // N_DEV, c)
    return full.reshape(r, N_DEV, c // N_DEV).transpose(1, 0, 2)


def kernel(x, meta_tokens, ln_emb_g, ln_emb_b, w_in, b_gate, b_forget, q_norm_g, w_q_up, kv_norm_g, w_kv_up, w_branch_mla, w_branch_fox, w_out, ln_mix_g, ln_mix_b, w_ffn_up, conv_w, conv_b, w_ffn_down, ln_ffn_g, ln_ffn_b, loss_target, m_meta_tokens, m_ln_emb_g, m_ln_emb_b, m_w_in, m_b_gate, m_b_forget, m_q_norm_g, m_w_q_up, m_kv_norm_g, m_w_kv_up, m_w_branch_mla, m_w_branch_fox, m_w_out, m_ln_mix_g, m_ln_mix_b, m_w_ffn_up, m_conv_w, m_conv_b, m_w_ffn_down, m_ln_ffn_g, m_ln_ffn_b, v_meta_tokens, v_ln_emb_g, v_ln_emb_b, v_w_in, v_b_gate, v_b_forget, v_q_norm_g, v_w_q_up, v_kv_norm_g, v_w_kv_up, v_w_branch_mla, v_w_branch_fox, v_w_out, v_ln_mix_g, v_ln_mix_b, v_w_ffn_up, v_conv_w, v_conv_b, v_w_ffn_down, v_ln_ffn_g, v_ln_ffn_b):
    names = ("meta_tokens", "ln_emb_g", "ln_emb_b", "w_in", "b_gate", "b_forget", "q_norm_g", "w_q_up", "kv_norm_g",
             "w_kv_up", "w_branch_mla", "w_branch_fox", "w_out", "ln_mix_g", "ln_mix_b", "w_ffn_up", "conv_w", "conv_b",
             "w_ffn_down", "ln_ffn_g", "ln_ffn_b")
    w_args = (meta_tokens, ln_emb_g, ln_emb_b, w_in, b_gate, b_forget, q_norm_g, w_q_up, kv_norm_g, w_kv_up,
              w_branch_mla, w_branch_fox, w_out, ln_mix_g, ln_mix_b, w_ffn_up, conv_w, conv_b, w_ffn_down, ln_ffn_g, ln_ffn_b)
    m_args = (m_meta_tokens, m_ln_emb_g, m_ln_emb_b, m_w_in, m_b_gate, m_b_forget, m_q_norm_g, m_w_q_up, m_kv_norm_g,
              m_w_kv_up, m_w_branch_mla, m_w_branch_fox, m_w_out, m_ln_mix_g, m_ln_mix_b, m_w_ffn_up, m_conv_w, m_conv_b,
              m_w_ffn_down, m_ln_ffn_g, m_ln_ffn_b)
    v_args = (v_meta_tokens, v_ln_emb_g, v_ln_emb_b, v_w_in, v_b_gate, v_b_forget, v_q_norm_g, v_w_q_up, v_kv_norm_g,
              v_w_kv_up, v_w_branch_mla, v_w_branch_fox, v_w_out, v_ln_mix_g, v_ln_mix_b, v_w_ffn_up, v_conv_w, v_conv_b,
              v_w_ffn_down, v_ln_ffn_g, v_ln_ffn_b)
    as2d = lambda a: a.reshape((-1, a.shape[-1])) if a.ndim != 1 else a.reshape(1, -1)
    w = {n: as2d(a) for n, a in zip(names, w_args)}
    m = {n: as2d(a) for n, a in zip(names, m_args)}
    v = {n: as2d(a) for n, a in zip(names, v_args)}
    out_shape = {n: a.shape for n, a in zip(names, w_args)}

    axis_of = dict(BIG + SMALL_SHARDED)
    big = set(n for n, _ in BIG)
    wire = lambda n, a: a.astype(MXU_DTYPE) if n in big else a
    my_id = _my_id()

    early = _allgather("gather_early", [wire(n, w[n]) for n in EARLY])
    p = {n: _gathered_full(g3, axis_of[n]) for n, g3 in zip(EARLY, early)}
    for n in REPLICATED:
        p[n] = w[n].reshape(-1)
    late_src = [wire(n, w[n]) for n in LATE]
    late_handle, late_token = _push_start("gather_late_start", late_src, False, after=early[0])
    sent = {}

    class Comm:
        first_token = (late_token,)

        def late_weights(self, after):
            lands = _push_wait("gather_late_wait", late_handle, after)
            return {n: _gathered_full(lax.dynamic_update_index_in_dim(land, own, my_id, 0), axis_of[n])
                    for n, own, land in zip(LATE, late_src, lands)}

        def send(self, name, grads):
            names_ = tuple(grads)
            parts = [_shard_major(grads[n], axis_of[n]).astype(MXU_DTYPE) for n in names_]
            handle, token = _push_start("send_" + name + "_start", parts, True)
            sent[name] = (names_, parts, handle)
            return (token,)

    loss_part, grad_x, d_meta, g = _local_step(x[0], loss_target[0], p, Comm())
    grad_x = grad_x[None]

    small = _pack([d_meta.reshape(-1)] + [g[n].reshape(-1) for n in REPLICATED] + [loss_part.reshape(-1)])
    small_handle, small_token = _push_start("send_small_start", [small], False)

    res = {}
    prev = small_token
    for name, (names_, parts, handle) in sent.items():
        lands = _push_wait("send_" + name + "_wait", handle, prev)
        for n, part, land in zip(names_, parts, lands):
            own = lax.dynamic_index_in_dim(part, my_id, axis=0, keepdims=False)
            res[n] = _adamw("adamw_" + n, land, w[n], m[n], v[n], own=own)
            prev = res[n][0]
    small_all = _push_wait("send_small_wait", small_handle, prev)[0]
    head = jnp.zeros((d_meta.size,), F32)
    rep_w = _pack([head] + [w[n].reshape(-1) for n in REPLICATED])
    rep_m = _pack([head] + [m[n].reshape(-1) for n in REPLICATED])
    rep_v = _pack([head] + [v[n].reshape(-1) for n in REPLICATED])
    rep_res = _adamw("adamw_replicated", small_all, rep_w, rep_m, rep_v, own=small)
    off = d_meta.size
    for n in REPLICATED:
        sz = w[n].size
        res[n] = tuple(a.reshape(-1)[off:off + sz] for a in rep_res)
        off += sz
    loss = rep_res[0].reshape(-1)[off]
    cols = w["meta_tokens"].shape[1]
    meta_rows = lambda a: a.reshape(a.shape[:-2] + (-1,))[..., :d_meta.size].reshape(a.shape[:-2] + d_meta.shape)
    my_cols = lambda a: lax.dynamic_slice_in_dim(a, my_id * cols, cols, axis=a.ndim - 1)
    res["meta_tokens"] = _adamw("adamw_meta_tokens", my_cols(meta_rows(small_all)), w["meta_tokens"],
                                m["meta_tokens"], v["meta_tokens"], own=my_cols(d_meta))

    outs = [loss, grad_x]
    for idx in range(4):
        outs += [res[n][idx].reshape(out_shape[n]) for n in names]
    return tuple(outs)
```

```python
import jax
import jax.numpy as jnp
from jax import lax
from jax.experimental import pallas as pl
from jax.experimental.pallas import tpu as pltpu

F32 = jnp.float32
BF16 = jnp.bfloat16
MXU_DTYPE = BF16

N_DEV = 8
N_META = 16
D_MODEL = 1024
HEADS = 8
Q_RANK = 384
KV_RANK = 128
NOPE = 64
ROPE = 32
HALF = ROPE // 2
QK_DIM = NOPE + ROPE
V_DIM = 64
FOX_DIM = 64
FOX_W = HEADS * FOX_DIM
D_FF = 2816
ROPE_THETA = 10000.0
LN_EPS = 1e-5
RMS_EPS = 1e-6
ALPHA = 2.0 ** 0.25
MLA_SCALE = QK_DIM ** -0.5
FOX_SCALE = FOX_DIM ** -0.5
NEG_INF = -1e30

HP = 128
HW = HEADS * HP
F_W = 3 * HW
R_QLAT = 0
R_KVLAT = Q_RANK
R_LAST = R_KVLAT + KV_RANK
R_GATE = D_MODEL
R_W = R_GATE + 2 * D_MODEL
LANE_FL = 0
LANE_PE = NOPE

ADAM_LR = 0.001
ADAM_B1 = 0.9
ADAM_B2 = 0.999
ADAM_EPS = 1e-08
ADAM_WD = 0.01
ADAM_STEP = 10

ROW_BLOCK = 256
ATT_TQ = 768
ATT_TK = 256
ATT_HEADS = 2
ROW_ALIGN = 768
MM_BLOCK_CAP = 1408
VMEM_LIMIT = 56 * 1024 * 1024
HIGHEST = lax.Precision.HIGHEST
NT = (((1,), (1,)), ((), ()))
TN = (((0,), (0,)), ((), ()))


def _params(sem=None):
    return pltpu.CompilerParams(dimension_semantics=sem, vmem_limit_bytes=VMEM_LIMIT)


def _call(name, body, grid, ins, outs, scratch=(), sem=None, after=()):
    n_in = len(ins)
    n_tok = len(after)

    def run(*refs):
        body(*refs[:n_in], *refs[n_in + n_tok:])

    tok_spec = pl.BlockSpec((8, 128), lambda *_: (0, 0))
    return pl.pallas_call(
        run, name=name, grid=grid,
        in_specs=[s for _, s in ins] + [tok_spec] * n_tok,
        out_specs=[s for _, s in outs],
        out_shape=[o for o, _ in outs],
        scratch_shapes=list(scratch),
        compiler_params=_params(sem),
    )(*[a for a, _ in ins], *after)


def _sds(shape, dtype):
    return jax.ShapeDtypeStruct(shape, dtype)


def _rows(br, c, cb=0):
    return pl.BlockSpec((br, c), lambda i: (i, cb))


def _whole(shape):
    n = len(shape)
    return pl.BlockSpec(shape, lambda i: (0,) * n)


def _pick(dim, cap, mult):
    best = None
    d = mult
    while d <= min(dim, cap):
        if dim % d == 0:
            best = d
        d += mult
    return best if best is not None else dim


def _hs(h):
    return slice(h * HP, (h + 1) * HP)


def _matmul(name, a, b, *, ta=False, tb=False, out_dtype=F32, addend=None, alpha=1.0, after=()):
    if ta:
        k, m = a.shape
    else:
        m, k = a.shape
    if tb:
        n, k2 = b.shape
    else:
        k2, n = b.shape
    assert k == k2, (name, a.shape, b.shape)
    bm = _pick(m, MM_BLOCK_CAP, 128 if ta else 16)
    bn = _pick(n, MM_BLOCK_CAP, 128)
    bk = _pick(k, MM_BLOCK_CAP, 128 if (not ta or tb) else 16)
    nk = k // bk
    dims = (((0 if ta else 1,), (1 if tb else 0,)), ((), ()))
    has_add = addend is not None

    def body(*refs):
        a_ref, b_ref = refs[:2]
        add_ref = refs[2] if has_add else None
        o_ref = refs[3 if has_add else 2]

        def finish(r):
            if has_add:
                r = r + alpha * add_ref[...]
            o_ref[...] = r.astype(o_ref.dtype)

        part = lax.dot_general(a_ref[...], b_ref[...], dims, preferred_element_type=F32)
        if nk == 1:
            finish(part)
            return
        acc_ref = refs[-1]
        kk = pl.program_id(2)

        @pl.when(kk == 0)
        def _():
            acc_ref[...] = part

        @pl.when(kk > 0)
        def _():
            acc_ref[...] += part

        @pl.when(kk == nk - 1)
        def _():
            finish(acc_ref[...])

    a_spec = pl.BlockSpec((bk, bm), lambda i, j, l: (l, i)) if ta else pl.BlockSpec((bm, bk), lambda i, j, l: (i, l))
    b_spec = pl.BlockSpec((bn, bk), lambda i, j, l: (j, l)) if tb else pl.BlockSpec((bk, bn), lambda i, j, l: (l, j))
    o_spec = pl.BlockSpec((bm, bn), lambda i, j, l: (i, j))
    ins = [(a, a_spec), (b, b_spec)]
    if has_add:
        ins.append((addend, o_spec))
    return _call(name, body, (m // bm, n // bn, nk), ins, [(_sds((m, n), out_dtype), o_spec)],
                 scratch=[pltpu.VMEM((bm, bn), F32)] if nk > 1 else [],
                 sem=("parallel", "parallel", "arbitrary"), after=after)[0]


def _ln_stats(z):
    mu = jnp.mean(z, axis=-1, keepdims=True)
    zc = z - mu
    var = jnp.mean(zc * zc, axis=-1, keepdims=True)
    rstd = lax.rsqrt(var + LN_EPS)
    return zc * rstd, rstd


def _ln_fwd(name, a, res, g, b, after=()):
    r, d = a.shape
    br = ROW_BLOCK
    has_res = res is not None

    def body(*refs):
        if has_res:
            a_ref, r_ref, g_ref, b_ref, y_ref, yb_ref = refs
            z = ALPHA * a_ref[...] + r_ref[...]
        else:
            a_ref, g_ref, b_ref, y_ref, yb_ref = refs
            z = a_ref[...]
        xhat, _ = _ln_stats(z)
        y = xhat * g_ref[...] + b_ref[...]
        y_ref[...] = y
        yb_ref[...] = y.astype(yb_ref.dtype)

    ins = [(a, _rows(br, d))]
    if has_res:
        ins.append((res, _rows(br, d)))
    ins += [(g.reshape(1, d), _whole((1, d))), (b.reshape(1, d), _whole((1, d)))]
    outs = [(_sds((r, d), F32), _rows(br, d)), (_sds((r, d), MXU_DTYPE), _rows(br, d))]
    return _call(name, body, (r // br,), ins, outs, sem=("parallel",), after=after)


def _ln_bwd(name, a, res, dy, g, after=()):
    r, d = a.shape
    br = ROW_BLOCK
    has_res = res is not None

    def body(*refs):
        if has_res:
            a_ref, r_ref, dy_ref, g_ref, dz_ref, dzb_ref, dg_ref, db_ref = refs
            z = ALPHA * a_ref[...] + r_ref[...]
        else:
            a_ref, dy_ref, g_ref, dz_ref, dzb_ref, dg_ref, db_ref = refs
            z = a_ref[...]
        xhat, rstd = _ln_stats(z)
        dyv = dy_ref[...]
        dyg = dyv * g_ref[...]
        m1 = jnp.mean(dyg, axis=-1, keepdims=True)
        m2 = jnp.mean(dyg * xhat, axis=-1, keepdims=True)
        dz = rstd * (dyg - m1 - xhat * m2)
        dz_ref[...] = dz
        dzb_ref[...] = dz.astype(dzb_ref.dtype)

        @pl.when(pl.program_id(0) == 0)
        def _():
            dg_ref[...] = jnp.zeros_like(dg_ref)
            db_ref[...] = jnp.zeros_like(db_ref)

        dg_ref[...] += jnp.sum(dyv * xhat, axis=0, keepdims=True)
        db_ref[...] += jnp.sum(dyv, axis=0, keepdims=True)

    ins = [(a, _rows(br, d))]
    if has_res:
        ins.append((res, _rows(br, d)))
    ins += [(dy, _rows(br, d)), (g.reshape(1, d), _whole((1, d)))]
    outs = [(_sds((r, d), F32), _rows(br, d)), (_sds((r, d), MXU_DTYPE), _rows(br, d)),
            (_sds((1, d), F32), _whole((1, d))), (_sds((1, d), F32), _whole((1, d)))]
    return _call(name, body, (r // br,), ins, outs, sem=("arbitrary",), after=after)


def _rms_fwd(name, proj, cb, width, g):
    r = proj.shape[0]
    br = ROW_BLOCK

    def body(x_ref, g_ref, y_ref):
        x = x_ref[...]
        rstd = lax.rsqrt(jnp.mean(x * x, axis=-1, keepdims=True) + RMS_EPS)
        y_ref[...] = (x * rstd * g_ref[...]).astype(y_ref.dtype)

    return _call(name, body, (r // br,), [(proj, _rows(br, width, cb)), (g.reshape(1, width), _whole((1, width)))],
                 [(_sds((r, width), MXU_DTYPE), _rows(br, width))], sem=("parallel",))[0]


def _rms_bwd(name, proj, cb, width, dy, g):
    r = proj.shape[0]
    br = ROW_BLOCK

    def body(x_ref, dy_ref, g_ref, dx_ref, dg_ref):
        x = x_ref[...]
        rstd = lax.rsqrt(jnp.mean(x * x, axis=-1, keepdims=True) + RMS_EPS)
        nrm = x * rstd
        dyv = dy_ref[...]
        dyg = dyv * g_ref[...]
        dx = rstd * (dyg - nrm * jnp.mean(dyg * nrm, axis=-1, keepdims=True))
        dx_ref[...] = dx.astype(dx_ref.dtype)

        @pl.when(pl.program_id(0) == 0)
        def _():
            dg_ref[...] = jnp.zeros_like(dg_ref)

        dg_ref[...] += jnp.sum(dyv * nrm, axis=0, keepdims=True)

    return _call(name, body, (r // br,),
                 [(proj, _rows(br, width, cb)), (dy, _rows(br, width)), (g.reshape(1, width), _whole((1, width)))],
                 [(_sds((r, width), MXU_DTYPE), _rows(br, width)), (_sds((1, width), F32), _whole((1, width)))],
                 sem=("arbitrary",))


def _lane_iota(shape):
    return lax.broadcasted_iota(jnp.int32, shape, 1)


def _rotary(t, c, s, lane, sign):
    second = pltpu.roll(t, HP - HALF, axis=1)
    first = pltpu.roll(t, HALF, axis=1)
    lo = (lane >= LANE_PE) & (lane < LANE_PE + HALF)
    hi = (lane >= LANE_PE + HALF) & (lane < LANE_PE + ROPE)
    return jnp.where(lo, t * c - sign * second * s, jnp.where(hi, t * c + sign * first * s, t))


def _rope_fwd(q_raw, k_part, proj_r, cos_t, sin_t):
    r = q_raw.shape[0]
    br = ROW_BLOCK

    def body(q_ref, k_ref, t_ref, c_ref, s_ref, qo_ref, ko_ref):
        c = c_ref[...]
        s = s_ref[...]
        lane = _lane_iota((br, HP))
        pe = (lane >= LANE_PE) & (lane < LANE_PE + ROPE)
        kp = jnp.where(pe, _rotary(t_ref[...], c, s, lane, 1.0), 0.0)
        for h in range(HEADS):
            qo_ref[:, _hs(h)] = (_rotary(q_ref[:, _hs(h)], c, s, lane, 1.0) * MLA_SCALE).astype(qo_ref.dtype)
            ko_ref[:, _hs(h)] = (k_ref[:, _hs(h)] + kp).astype(ko_ref.dtype)

    blk = _rows(br, HP)
    wide = _rows(br, HW)
    return _call("rope_fwd", body, (r // br,),
                 [(q_raw, wide), (k_part, wide), (proj_r, _rows(br, HP, R_LAST // HP)), (cos_t, blk), (sin_t, blk)],
                 [(_sds((r, HW), MXU_DTYPE), wide)] * 2, sem=("parallel",))


def _rope_bwd(dq, dk, dfl, cos_t, sin_t):
    r = dq.shape[0]
    br = ROW_BLOCK

    def body(dq_ref, dk_ref, fl_ref, c_ref, s_ref, dqo_ref, dko_ref, dl_ref):
        c = c_ref[...]
        s = s_ref[...]
        lane = _lane_iota((br, HP))
        pe = (lane >= LANE_PE) & (lane < LANE_PE + ROPE)
        acc = jnp.zeros((br, HP), F32)
        for h in range(HEADS):
            dqo_ref[:, _hs(h)] = (_rotary(dq_ref[:, _hs(h)], c, s, lane, -1.0) * MLA_SCALE).astype(dqo_ref.dtype)
            dkh = dk_ref[:, _hs(h)]
            acc = acc + dkh
            dko_ref[:, _hs(h)] = dkh.astype(dko_ref.dtype)
        dl_ref[...] = (jnp.where(pe, _rotary(acc, c, s, lane, -1.0), 0.0) + fl_ref[...]).astype(dl_ref.dtype)

    blk = _rows(br, HP)
    wide = _rows(br, HW)
    return _call("rope_bwd", body, (r // br,),
                 [(dq, wide), (dk, wide), (dfl, blk), (cos_t, blk), (sin_t, blk)],
                 [(_sds((r, HW), MXU_DTYPE), wide), (_sds((r, HW), MXU_DTYPE), wide), (_sds((r, HP), MXU_DTYPE), blk)],
                 sem=("parallel",))


def _log_sigmoid(x):
    return jnp.minimum(x, 0.0) - jnp.log(1.0 + jnp.exp(-jnp.abs(x)))


def _head_lane(x, h, lane):
    return jnp.sum(jnp.where(lane == h, x, 0.0), axis=1, keepdims=True)


def _forget_fwd(proj_r, bf_row):
    r = proj_r.shape[0]
    br = ROW_BLOCK

    def body(t_ref, b_ref, ob_ref, ot_ref, carry_ref):
        @pl.when(pl.program_id(0) == 0)
        def _():
            carry_ref[...] = jnp.zeros_like(carry_ref)

        x = t_ref[...] + b_ref[...]
        lane = _lane_iota(x.shape)
        lf = jnp.where((lane >= LANE_FL) & (lane < LANE_FL + HEADS), _log_sigmoid(x), 0.0)
        tri = (lax.broadcasted_iota(jnp.int32, (br, br), 0) >= lax.broadcasted_iota(jnp.int32, (br, br), 1)).astype(F32)
        cum = jnp.dot(tri, lf, precision=HIGHEST, preferred_element_type=F32) + carry_ref[0:1, :]
        for h in range(HEADS):
            ob_ref[:, _hs(h)] = jnp.broadcast_to(_head_lane(cum, LANE_FL + h, lane), (br, HP))
        ot_ref[...] = cum.T[LANE_FL:LANE_FL + HEADS, :]
        carry_ref[...] = jnp.broadcast_to(cum[br - 1:br, :], carry_ref.shape)

    return _call("forget_fwd", body, (r // br,),
                 [(proj_r, _rows(br, HP, R_LAST // HP)), (bf_row, _whole((1, HP)))],
                 [(_sds((r, HW), F32), _rows(br, HW)), (_sds((HEADS, r), F32), pl.BlockSpec((HEADS, br), lambda i: (0, i)))],
                 scratch=[pltpu.VMEM((8, HP), F32)], sem=("arbitrary",))


def _forget_bwd(proj_r, bf_row, dcq_t, dck_b):
    r = proj_r.shape[0]
    br = ROW_BLOCK
    nb = r // br

    def body(t_ref, b_ref, dcq_ref, dck_ref, o_ref, db_ref, carry_ref):
        @pl.when(pl.program_id(0) == 0)
        def _():
            carry_ref[...] = jnp.zeros_like(carry_ref)
            db_ref[...] = jnp.zeros_like(db_ref)

        lane = _lane_iota((br, HP))
        dc = jnp.concatenate([dcq_ref[...], jnp.zeros((HP - HEADS, br), F32)], axis=0).T
        for h in range(HEADS):
            dc = dc + jnp.where(lane == LANE_FL + h, dck_ref[:, h * HP:h * HP + 1], 0.0)
        triu = (lax.broadcasted_iota(jnp.int32, (br, br), 0) <= lax.broadcasted_iota(jnp.int32, (br, br), 1)).astype(F32)
        dlf = jnp.dot(triu, dc, precision=HIGHEST, preferred_element_type=F32) + carry_ref[0:1, :]
        carry_ref[...] = jnp.broadcast_to(dlf[0:1, :], carry_ref.shape)
        x = t_ref[...] + b_ref[...]
        dfl = jnp.where((lane >= LANE_FL) & (lane < LANE_FL + HEADS), dlf * jax.nn.sigmoid(-x), 0.0)
        o_ref[...] = dfl
        db_ref[...] += jnp.sum(dfl, axis=0, keepdims=True)

    rev = pl.BlockSpec((br, HP), lambda i: (nb - 1 - i, 0))
    return _call("forget_bwd", body, (nb,),
                 [(proj_r, pl.BlockSpec((br, HP), lambda i: (nb - 1 - i, R_LAST // HP))), (bf_row, _whole((1, HP))),
                  (dcq_t, pl.BlockSpec((HEADS, br), lambda i: (0, nb - 1 - i))),
                  (dck_b, pl.BlockSpec((br, HW), lambda i: (nb - 1 - i, 0)))],
                 [(_sds((r, HP), F32), rev), (_sds((1, HP), F32), _whole((1, HP)))],
                 scratch=[pltpu.VMEM((8, HP), F32)], sem=("arbitrary",))


def _gate_fwd(proj_r, b_gate, bm, bfx):
    r, d = bm.shape
    br = ROW_BLOCK
    cb = R_GATE // d

    def body(gm_ref, gf_ref, b1_ref, b2_ref, bm_ref, bf_ref, o_ref):
        g1 = jax.nn.sigmoid(gm_ref[...] + b1_ref[...])
        g2 = jax.nn.sigmoid(gf_ref[...] + b2_ref[...])
        o_ref[...] = (g1 * bm_ref[...].astype(F32) + g2 * bf_ref[...].astype(F32)).astype(o_ref.dtype)

    b1 = b_gate[:d].reshape(1, d)
    b2 = b_gate[d:].reshape(1, d)
    return _call("gate_fwd", body, (r // br,),
                 [(proj_r, _rows(br, d, cb)), (proj_r, _rows(br, d, cb + 1)), (b1, _whole((1, d))), (b2, _whole((1, d))),
                  (bm, _rows(br, d)), (bfx, _rows(br, d))],
                 [(_sds((r, d), MXU_DTYPE), _rows(br, d))], sem=("parallel",))[0]


def _gate_bwd(proj_r, b_gate, bm, bfx, dmerged):
    r, d = bm.shape
    br = ROW_BLOCK
    cb = R_GATE // d

    def body(gm_ref, gf_ref, b1_ref, b2_ref, bm_ref, bf_ref, dm_ref, dbm_ref, dbf_ref, dgl_ref, dbg_ref):
        g1 = jax.nn.sigmoid(gm_ref[...] + b1_ref[...])
        g2 = jax.nn.sigmoid(gf_ref[...] + b2_ref[...])
        dm = dm_ref[...].astype(F32)
        dbm_ref[...] = (dm * g1).astype(dbm_ref.dtype)
        dbf_ref[...] = (dm * g2).astype(dbf_ref.dtype)
        dl1 = dm * bm_ref[...].astype(F32) * (g1 * (1.0 - g1))
        dl2 = dm * bf_ref[...].astype(F32) * (g2 * (1.0 - g2))
        dgl_ref[:, 0:d] = dl1.astype(dgl_ref.dtype)
        dgl_ref[:, d:2 * d] = dl2.astype(dgl_ref.dtype)

        @pl.when(pl.program_id(0) == 0)
        def _():
            dbg_ref[...] = jnp.zeros_like(dbg_ref)

        dbg_ref[:, 0:d] += jnp.sum(dl1, axis=0, keepdims=True)
        dbg_ref[:, d:2 * d] += jnp.sum(dl2, axis=0, keepdims=True)

    b1 = b_gate[:d].reshape(1, d)
    b2 = b_gate[d:].reshape(1, d)
    return _call("gate_bwd", body, (r // br,),
                 [(proj_r, _rows(br, d, cb)), (proj_r, _rows(br, d, cb + 1)), (b1, _whole((1, d))), (b2, _whole((1, d))),
                  (bm, _rows(br, d)), (bfx, _rows(br, d)), (dmerged, _rows(br, d))],
                 [(_sds((r, d), MXU_DTYPE), _rows(br, d)), (_sds((r, d), MXU_DTYPE), _rows(br, d)),
                  (_sds((r, 2 * d), MXU_DTYPE), _rows(br, 2 * d)), (_sds((1, 2 * d), F32), _whole((1, 2 * d)))],
                 sem=("arbitrary",))


HALO = 16


def _conv_taps(gp, halo, first_block):
    halo = jnp.where(first_block, 0.0, halo.astype(F32))
    rid = lax.broadcasted_iota(jnp.int32, gp.shape, 0)
    last, prev = halo[HALO - 1:HALO, :], halo[HALO - 2:HALO - 1, :]
    g1 = jnp.where(rid == 0, last, pltpu.roll(gp, 1, axis=0))
    g2 = jnp.where(rid == 0, prev, jnp.where(rid == 1, last, pltpu.roll(gp, 2, axis=0)))
    return g1, g2


def _prev_halo(br, c):
    return pl.BlockSpec((HALO, c), lambda i: (jnp.maximum(i * (br // HALO) - 1, 0), 0))


def _glu_fwd(up, conv_w, conv_b):
    r = up.shape[0]
    c = D_FF
    br = ROW_BLOCK

    def body(gp_ref, halo_ref, val_ref, w_ref, b_ref, o_ref):
        gp = gp_ref[...].astype(F32)
        g1, g2 = _conv_taps(gp, halo_ref[...], pl.program_id(0) == 0)
        gate = w_ref[0:1, :] * g2 + w_ref[1:2, :] * g1 + w_ref[2:3, :] * gp + b_ref[...]
        o_ref[...] = (gate * jax.nn.sigmoid(gate) * val_ref[...].astype(F32)).astype(o_ref.dtype)

    return _call("glu_fwd", body, (r // br,),
                 [(up, _rows(br, c, 0)), (up, _prev_halo(br, c)), (up, _rows(br, c, 1)),
                  (conv_w, _whole((3, c))), (conv_b.reshape(1, c), _whole((1, c)))],
                 [(_sds((r, c), MXU_DTYPE), _rows(br, c))], sem=("parallel",))[0]


def _glu_bwd_gate(up, conv_w, conv_b, d_act):
    r = up.shape[0]
    c = D_FF
    br = ROW_BLOCK

    def body(gp_ref, halo_ref, val_ref, w_ref, b_ref, da_ref, dg_ref, dv_ref, dw_ref, db_ref):
        gp = gp_ref[...].astype(F32)
        g1, g2 = _conv_taps(gp, halo_ref[...], pl.program_id(0) == 0)
        gate = w_ref[0:1, :] * g2 + w_ref[1:2, :] * g1 + w_ref[2:3, :] * gp + b_ref[...]
        sg = jax.nn.sigmoid(gate)
        da = da_ref[...].astype(F32)
        dv_ref[...] = (da * (gate * sg)).astype(dv_ref.dtype)
        dg = da * val_ref[...].astype(F32) * (sg * (1.0 + gate * (1.0 - sg)))
        dg_ref[...] = dg.astype(dg_ref.dtype)

        @pl.when(pl.program_id(0) == 0)
        def _():
            dw_ref[...] = jnp.zeros_like(dw_ref)
            db_ref[...] = jnp.zeros_like(db_ref)

        dw_ref[0:1, :] += jnp.sum(dg * g2, axis=0, keepdims=True)
        dw_ref[1:2, :] += jnp.sum(dg * g1, axis=0, keepdims=True)
        dw_ref[2:3, :] += jnp.sum(dg * gp, axis=0, keepdims=True)
        db_ref[...] += jnp.sum(dg, axis=0, keepdims=True)

    return _call("glu_bwd_gate", body, (r // br,),
                 [(up, _rows(br, c, 0)), (up, _prev_halo(br, c)), (up, _rows(br, c, 1)),
                  (conv_w, _whole((3, c))), (conv_b.reshape(1, c), _whole((1, c))), (d_act, _rows(br, c))],
                 [(_sds((r, c), MXU_DTYPE), _rows(br, c)), (_sds((r, c), MXU_DTYPE), _rows(br, c)),
                  (_sds((8, c), F32), _whole((8, c))), (_sds((1, c), F32), _whole((1, c)))],
                 sem=("arbitrary",))


def _glu_bwd_conv(dg, dval, conv_w):
    r, c = dg.shape
    br = ROW_BLOCK
    nb = r // br

    def body(dg_ref, nxt_ref, dv_ref, w_ref, o_ref):
        x = dg_ref[...].astype(F32)
        nxt = jnp.where(pl.program_id(0) == nb - 1, 0.0, nxt_ref[...].astype(F32))
        rid = lax.broadcasted_iota(jnp.int32, x.shape, 0)
        u1 = jnp.where(rid == br - 1, nxt[0:1, :], pltpu.roll(x, br - 1, axis=0))
        u2 = jnp.where(rid == br - 1, nxt[1:2, :], jnp.where(rid == br - 2, nxt[0:1, :], pltpu.roll(x, br - 2, axis=0)))
        dgp = w_ref[2:3, :] * x + w_ref[1:2, :] * u1 + w_ref[0:1, :] * u2
        o_ref[:, 0:c] = dgp.astype(o_ref.dtype)
        o_ref[:, c:2 * c] = dv_ref[...]

    nxt_spec = pl.BlockSpec((HALO, c), lambda i: (jnp.minimum((i + 1) * (br // HALO), r // HALO - 1), 0))
    return _call("glu_bwd_conv", body, (nb,),
                 [(dg, _rows(br, c)), (dg, nxt_spec), (dval, _rows(br, c)), (conv_w, _whole((3, c)))],
                 [(_sds((r, 2 * c), MXU_DTYPE), _rows(br, 2 * c))], sem=("parallel",))[0]


def _token_specs(seq, d):
    br = ROW_BLOCK
    nxb = seq // br
    main = pl.BlockSpec((br, d), lambda i: (jnp.minimum(i, nxb - 1), 0))
    tail = pl.BlockSpec((N_META, d), lambda i: (jnp.clip(i * (br // N_META) - 1, 0, seq // N_META - 1), 0))
    return main, tail


def _padded_block(main_ref, tail_ref, first, seq):
    br = ROW_BLOCK
    i = pl.program_id(0)
    nxb = seq // br
    main = jnp.where(i < nxb, main_ref[...], 0.0)
    head = jnp.where(i == 0, first, jnp.where(i <= nxb, tail_ref[...], 0.0))
    return jnp.concatenate([head, main[:br - N_META]], axis=0)


def _ln_emb_fwd(x, meta, g, b, rows, after=()):
    seq, d = x.shape
    br = ROW_BLOCK
    assert seq % br == 0 and br % N_META == 0 and rows % br == 0

    def body(x_ref, tail_ref, meta_ref, g_ref, b_ref, y_ref, yb_ref):
        z = _padded_block(x_ref, tail_ref, meta_ref[...], seq)
        xhat, _ = _ln_stats(z)
        y = xhat * g_ref[...] + b_ref[...]
        y_ref[...] = y
        yb_ref[...] = y.astype(yb_ref.dtype)

    main, tail = _token_specs(seq, d)
    return _call("ln_emb_fwd", body, (rows // br,),
                 [(x, main), (x, tail), (meta, _whole((N_META, d))), (g.reshape(1, d), _whole((1, d))),
                  (b.reshape(1, d), _whole((1, d)))],
                 [(_sds((rows, d), F32), _rows(br, d)), (_sds((rows, d), MXU_DTYPE), _rows(br, d))],
                 sem=("parallel",), after=after)


def _ln_emb_bwd(x, meta, dh0, g):
    seq, d = x.shape
    br = ROW_BLOCK
    step = br // N_META

    def ln_bwd(z, dy, gv):
        xhat, rstd = _ln_stats(z)
        dyg = dy * gv
        m1 = jnp.mean(dyg, axis=-1, keepdims=True)
        m2 = jnp.mean(dyg * xhat, axis=-1, keepdims=True)
        dz = rstd * (dyg - m1 - xhat * m2)
        return dz, jnp.sum(dy * xhat, axis=0, keepdims=True), jnp.sum(dy, axis=0, keepdims=True)

    def body(x_ref, dh_ref, nxt_ref, meta_ref, top_ref, g_ref, dx_ref, dm_ref, dg_ref, db_ref):
        gv = g_ref[...]
        dy = jnp.concatenate([dh_ref[N_META:, :], nxt_ref[...]], axis=0)
        dz, dg, db = ln_bwd(x_ref[...], dy, gv)
        dx_ref[...] = dz

        @pl.when(pl.program_id(0) == 0)
        def _():
            dzm, dgm, dbm = ln_bwd(meta_ref[...], top_ref[...], gv)
            dm_ref[...] = dzm
            dg_ref[...] = dgm
            db_ref[...] = dbm

        dg_ref[...] += dg
        db_ref[...] += db

    small = _whole((N_META, d))
    return _call("ln_emb_bwd", body, (seq // br,),
                 [(x, _rows(br, d)), (dh0, _rows(br, d)), (dh0, pl.BlockSpec((N_META, d), lambda i: ((i + 1) * step, 0))),
                  (meta, small), (dh0, small), (g.reshape(1, d), _whole((1, d)))],
                 [(_sds((seq, d), F32), _rows(br, d)), (_sds((N_META, d), F32), small),
                  (_sds((1, d), F32), _whole((1, d))), (_sds((1, d), F32), _whole((1, d)))], sem=("arbitrary",))


def _ln_ffn_loss(h1, f, tgt, g, b):
    r, d = h1.shape
    seq = tgt.shape[0]
    br = ROW_BLOCK

    def body(a_ref, r_ref, t_ref, tail_ref, g_ref, b_ref, l_ref):
        err = _loss_err(a_ref, r_ref, t_ref, tail_ref, g_ref, b_ref, seq)[0]

        @pl.when(pl.program_id(0) == 0)
        def _():
            l_ref[...] = jnp.zeros_like(l_ref)

        l_ref[...] += jnp.sum(jnp.sum(err * err, axis=1, keepdims=True), axis=0, keepdims=True) * (0.5 / d)

    main, tail = _token_specs(seq, d)
    return _call("ln_ffn_loss", body, (r // br,),
                 [(h1, _rows(br, d)), (f, _rows(br, d)), (tgt, main), (tgt, tail),
                  (g.reshape(1, d), _whole((1, d))), (b.reshape(1, d), _whole((1, d)))],
                 [(_sds((1, 1), F32), _whole((1, 1)))], sem=("arbitrary",))[0]


def _loss_err(a_ref, r_ref, t_ref, tail_ref, g_ref, b_ref, seq):
    br, d = a_ref.shape
    xhat, rstd = _ln_stats(ALPHA * a_ref[...] + r_ref[...])
    y = xhat * g_ref[...] + b_ref[...]
    t = _padded_block(t_ref, tail_ref, jnp.zeros((N_META, d), F32), seq)
    rid = lax.broadcasted_iota(jnp.int32, (br, d), 0) + pl.program_id(0) * br
    valid = (rid >= N_META) & (rid < N_META + seq)
    return jnp.where(valid, y - t, 0.0), xhat, rstd


def _ln_ffn_bwd(h1, f, tgt, g, b):
    r, d = h1.shape
    seq = tgt.shape[0]
    br = ROW_BLOCK

    def body(a_ref, r_ref, t_ref, tail_ref, g_ref, b_ref, dz_ref, dzb_ref, dg_ref, db_ref):
        err, xhat, rstd = _loss_err(a_ref, r_ref, t_ref, tail_ref, g_ref, b_ref, seq)
        dyv = err * (1.0 / d)
        dyg = dyv * g_ref[...]
        m1 = jnp.mean(dyg, axis=-1, keepdims=True)
        m2 = jnp.mean(dyg * xhat, axis=-1, keepdims=True)
        dz = rstd * (dyg - m1 - xhat * m2)
        dz_ref[...] = dz
        dzb_ref[...] = dz.astype(dzb_ref.dtype)

        @pl.when(pl.program_id(0) == 0)
        def _():
            dg_ref[...] = jnp.zeros_like(dg_ref)
            db_ref[...] = jnp.zeros_like(db_ref)

        dg_ref[...] += jnp.sum(dyv * xhat, axis=0, keepdims=True)
        db_ref[...] += jnp.sum(dyv, axis=0, keepdims=True)

    main, tail = _token_specs(seq, d)
    return _call("ln_ffn_bwd", body, (r // br,),
                 [(h1, _rows(br, d)), (f, _rows(br, d)), (tgt, main), (tgt, tail),
                  (g.reshape(1, d), _whole((1, d))), (b.reshape(1, d), _whole((1, d)))],
                 [(_sds((r, d), F32), _rows(br, d)), (_sds((r, d), MXU_DTYPE), _rows(br, d)),
                  (_sds((1, d), F32), _whole((1, d))), (_sds((1, d), F32), _whole((1, d)))], sem=("arbitrary",))


def _attn_fwd(name, q, k, v, cum_b=None, cum_t=None):
    (qa, qg), (ka, kg), (va, vg) = q, k, v
    r = qa.shape[0]
    tq, tk = ATT_TQ, ATT_TK
    nq, nk = r // tq, r // tk
    bias = cum_b is not None

    def body(*refs):
        if bias:
            q_ref, k_ref, vt_ref, cb_ref, ct_ref, o_ref, ob_ref, lse_ref = refs
        else:
            q_ref, k_ref, vt_ref, o_ref, ob_ref, lse_ref = refs
        i = pl.program_id(1)
        qs = [q_ref[:, _hs(hh)] for hh in range(hg)]
        cqs = [ct_ref[hh] for hh in range(hg)] if bias else None
        diff = lax.broadcasted_iota(jnp.int32, (tk, tq), 0) - lax.broadcasted_iota(jnp.int32, (tk, tq), 1)

        def step(j, carry, masked):
            keys = pl.ds(pl.multiple_of(j * tk, tk), tk)
            out = []
            for hh in range(hg):
                m, l, acc = carry[hh]
                kt = k_ref[keys, _hs(hh)]
                s = lax.dot_general(kt, qs[hh], NT, preferred_element_type=F32)
                if bias:
                    s = s + (cqs[hh] - cb_ref[keys, hh * HP:hh * HP + 1])
                if masked:
                    s = jnp.where(diff <= i * tq - j * tk, s, NEG_INF)
                m_new = jnp.maximum(m, jnp.max(s, axis=0, keepdims=True))
                p = jnp.exp(s - m_new)
                a = jnp.exp(m - m_new)
                l = a * l + jnp.sum(p, axis=0, keepdims=True)
                acc = a * acc + jnp.dot(vt_ref[j, _hs(hh), :], p.astype(kt.dtype), preferred_element_type=F32)
                out.append((m_new, l, acc))
            return tuple(out)

        n_clear = (i * tq + 1) // tk
        n_all = ((i + 1) * tq - 1) // tk + 1
        carry = tuple((jnp.full((1, tq), NEG_INF, F32), jnp.zeros((1, tq), F32), jnp.zeros((HP, tq), F32))
                      for _ in range(hg))
        carry = lax.fori_loop(0, n_clear, lambda j, c: step(j, c, False), carry)
        carry = lax.fori_loop(n_clear, n_all, lambda j, c: step(j, c, True), carry)
        for hh in range(hg):
            m, l, acc = carry[hh]
            o = (acc / l).T
            o_ref[:, _hs(hh)] = o
            ob_ref[:, _hs(hh)] = o.astype(ob_ref.dtype)
            lse_ref[hh] = m + jnp.log(l)

    hg = ATT_HEADS
    w = hg * HP
    gpw = HW // w
    tile = lambda g: pl.BlockSpec((tq, w), lambda h, i: (i, g * gpw + h))
    res = lambda g: pl.BlockSpec((r, w), lambda h, i: (0, g * gpw + h))
    v_t = _key_tiles_transposed(name + "_vt", va, vg)
    ins = [(qa, tile(qg)), (ka, res(kg)), (v_t, pl.BlockSpec((nk, w, tk), lambda h, i: (0, h, 0)))]
    if bias:
        ins += [(cum_b, res(0)),
                (cum_t.reshape(HEADS, nq, 1, tq), pl.BlockSpec((hg, None, 1, tq), lambda h, i: (h, i, 0, 0)))]
    outs = [(_sds((r, HW), F32), tile(0)), (_sds((r, HW), MXU_DTYPE), tile(0)),
            (_sds((HEADS, nq, 1, tq), F32), pl.BlockSpec((hg, None, 1, tq), lambda h, i: (h, i, 0, 0)))]
    o, ob, lse = _call(name, body, (gpw, nq), ins, outs, sem=("parallel", "parallel"))
    return o, ob, lse.reshape(HEADS, r)


def _key_tiles_transposed(name, a, group):
    r = a.shape[0]
    tk = ATT_TK

    def body(x_ref, o_ref):
        for h in range(HEADS):
            o_ref[_hs(h), :] = x_ref[:, _hs(h)].astype(F32).T.astype(o_ref.dtype)

    return _call(name, body, (r // tk,),
                 [(a, pl.BlockSpec((tk, HW), lambda j: (j, group)))],
                 [(_sds((r // tk, HW, tk), a.dtype), pl.BlockSpec((None, HW, tk), lambda j: (j, 0, 0)))],
                 sem=("parallel",))[0]


def _attn_delta(name, do_b, o, after=()):
    r = do_b.shape[0]
    br = ROW_BLOCK

    def body(do_ref, o_ref, d_ref):
        lane = _lane_iota((br, HP))
        d = jnp.zeros((br, HP), F32)
        for h in range(HEADS):
            dh = do_ref[:, _hs(h)].astype(F32)
            d = jnp.where(lane == h, jnp.sum(dh * o_ref[:, _hs(h)], axis=1, keepdims=True), d)
        d_ref[...] = d.T[0:HEADS, :]

    wide = _rows(br, HW)
    return _call(name, body, (r // br,), [(do_b, wide), (o, wide)],
                 [(_sds((HEADS, r), F32), pl.BlockSpec((HEADS, br), lambda i: (0, i)))],
                 sem=("parallel",), after=after)[0]


def _attn_bwd(name, q, k, v, do_b, lse_t, delta_t, cum_b=None, cum_t=None, out_dtype=F32):
    (qa, qg), (ka, kg), (va, vg) = q, k, v
    r = qa.shape[0]
    tq, tk = ATT_TQ, ATT_TK
    nq, nk = r // tq, r // tk
    bias = cum_b is not None

    def body(*refs):
        if bias:
            (q_ref, k_ref, v_ref, do_ref, lse_ref, dl_ref, cb_ref, ct_ref,
             dq_ref, dk_ref, dv_ref, dcq_ref, dck_ref, dqt_ref) = refs
        else:
            q_ref, k_ref, v_ref, do_ref, lse_ref, dl_ref, dq_ref, dk_ref, dv_ref, dqt_ref = refs
        j = pl.program_id(1)

        @pl.when(j == 0)
        def _():
            dqt_ref[...] = jnp.zeros_like(dqt_ref)
            if bias:
                dcq_ref[...] = jnp.zeros_like(dcq_ref)

        kts = [k_ref[:, _hs(hh)] for hh in range(hg)]
        vts = [v_ref[:, _hs(hh)] for hh in range(hg)]
        k_trs = [kt.astype(F32).T.astype(kt.dtype) for kt in kts]
        cks = [cb_ref[:, hh * HP:hh * HP + 1] for hh in range(hg)] if bias else None
        diff = lax.broadcasted_iota(jnp.int32, (tk, tq), 0) - lax.broadcasted_iota(jnp.int32, (tk, tq), 1)

        def step(i, carry, masked):
            rows = pl.ds(pl.multiple_of(i * tq, tq), tq)
            out = []
            for hh in range(hg):
                dk_acc, dv_acc, dck_acc = carry[hh]
                qt = q_ref[rows, _hs(hh)]
                dot = do_ref[rows, _hs(hh)]
                s = lax.dot_general(kts[hh], qt, NT, preferred_element_type=F32)
                if bias:
                    s = s + (ct_ref[hh, i] - cks[hh])
                if masked:
                    s = jnp.where(diff <= i * tq - j * tk, s, NEG_INF)
                p = jnp.exp(s - lse_ref[hh, i])
                dp = lax.dot_general(vts[hh], dot, NT, preferred_element_type=F32)
                ds = p * (dp - dl_ref[hh, i])
                pb = p.astype(dot.dtype)
                dsb = ds.astype(qt.dtype)
                dv_acc = dv_acc + jnp.dot(pb, dot, preferred_element_type=F32)
                dk_acc = dk_acc + jnp.dot(dsb, qt, preferred_element_type=F32)
                dqt_ref[hh, i] += jnp.dot(k_trs[hh], dsb, preferred_element_type=F32)
                if bias:
                    dcq_ref[hh, i] += jnp.sum(ds, axis=0, keepdims=True)
                    dck_acc = dck_acc - jnp.sum(ds, axis=1, keepdims=True)
                out.append((dk_acc, dv_acc, dck_acc))
            return tuple(out)

        i_first = (j * tk) // tq
        i_clear = jnp.minimum(((j + 1) * tk + tq - 2) // tq, nq)
        carry = tuple((jnp.zeros((tk, HP), F32), jnp.zeros((tk, HP), F32), jnp.zeros((tk, 1), F32)) for _ in range(hg))
        carry = lax.fori_loop(i_first, i_clear, lambda i, c: step(i, c, True), carry)
        carry = lax.fori_loop(i_clear, nq, lambda i, c: step(i, c, False), carry)
        for hh in range(hg):
            dk_acc, dv_acc, dck_acc = carry[hh]
            dk_ref[:, _hs(hh)] = dk_acc.astype(dk_ref.dtype)
            dv_ref[:, _hs(hh)] = dv_acc.astype(dv_ref.dtype)
            if bias:
                dck_ref[:, _hs(hh)] = jnp.broadcast_to(dck_acc, (tk, HP))

        @pl.when(j == nk - 1)
        def _():
            for hh in range(hg):
                for i in range(nq):
                    dq_ref[i * tq:(i + 1) * tq, _hs(hh)] = dqt_ref[hh, i].T.astype(dq_ref.dtype)

    hg = ATT_HEADS
    w = hg * HP
    gpw = HW // w
    res = lambda g: pl.BlockSpec((r, w), lambda h, j: (0, g * gpw + h))
    tile = lambda g: pl.BlockSpec((tk, w), lambda h, j: (j, g * gpw + h))
    rowv = pl.BlockSpec((hg, nq, 1, tq), lambda h, j: (h, 0, 0, 0))
    as_rows = lambda a: a.reshape(HEADS, nq, 1, tq)
    ins = [(qa, res(qg)), (ka, tile(kg)), (va, tile(vg)), (do_b, res(0)), (as_rows(lse_t), rowv), (as_rows(delta_t), rowv)]
    outs = [(_sds((r, HW), out_dtype), res(0)), (_sds((r, HW), out_dtype), tile(0)), (_sds((r, HW), out_dtype), tile(0))]
    if bias:
        ins += [(cum_b, tile(0)), (as_rows(cum_t), rowv)]
        outs += [(_sds((HEADS, nq, 1, tq), F32), rowv), (_sds((r, HW), F32), tile(0))]
    res_out = _call(name, body, (gpw, nk), ins, outs, scratch=[pltpu.VMEM((hg, nq, HP, tq), F32)],
                    sem=("parallel", "arbitrary"))
    if bias:
        dq, dk, dv, dcq, dck = res_out
        return dq, dk, dv, dcq.reshape(HEADS, r), dck
    return res_out


MESH_ID = pl.DeviceIdType.MESH
ANY = pl.BlockSpec(memory_space=pl.ANY)


def _allgather(name, shards):
    n = len(shards)

    def body(*refs):
        x_refs, out_refs = refs[:n], refs[n:2 * n]
        send_sems, recv_sems, local_sems = refs[2 * n:]
        x, y, c = lax.axis_index("x"), lax.axis_index("y"), lax.axis_index("c")
        me, sibling = (x, y, c), (x, y, 1 - c)
        chips = [(1 - x, y), (x, 1 - y), (1 - x, 1 - y)]

        def slot(ti, px, py, pc):
            return out_refs[ti].at[4 * px + 2 * py + pc]

        def copy(ti, k, block, to, src=None):
            return pltpu.make_async_remote_copy(
                src_ref=slot(ti, *block) if src is None else src, dst_ref=slot(ti, *block),
                send_sem=send_sems.at[ti, k], recv_sem=recv_sems.at[ti, k], device_id=to, device_id_type=MESH_ID)

        mine = [pltpu.make_async_copy(x_refs[ti], slot(ti, *me), local_sems.at[ti]) for ti in range(n)]
        for cp in mine:
            cp.start()
        started = []
        for ti in range(n):
            first = [copy(ti, 0, me, sibling, src=x_refs[ti])]
            first += [copy(ti, 1 + j, me, (*chip, c), src=x_refs[ti]) for j, chip in enumerate(chips)]
            for cp in first:
                cp.start()
            started += first
        for ti in range(n):
            for j, chip in enumerate(chips):
                copy(ti, 1 + j, (*chip, c), me).wait_recv()
                fwd = copy(ti, 4 + j, (*chip, c), sibling)
                fwd.start()
                started.append(fwd)
        for ti in range(n):
            copy(ti, 0, sibling, me).wait_recv()
            for j, chip in enumerate(chips):
                copy(ti, 4 + j, (*chip, 1 - c), me).wait_recv()
        for cp in started:
            cp.wait_send()
        for cp in mine:
            cp.wait()

    return pl.pallas_call(
        body, name=name, out_shape=[_sds((N_DEV,) + s.shape, s.dtype) for s in shards],
        in_specs=[ANY] * n, out_specs=[ANY] * n,
        scratch_shapes=[pltpu.SemaphoreType.DMA((n, 7)), pltpu.SemaphoreType.DMA((n, 7)), pltpu.SemaphoreType.DMA((n,))],
    )(*shards)


HBM = pl.BlockSpec(memory_space=pltpu.HBM)
SEM = pl.BlockSpec(memory_space=pltpu.SEMAPHORE)
EFFECT = pltpu.SideEffectType.DATAFLOW_SIDE_EFFECTING
N_PEER = N_DEV - 1


def _my_id():
    return 4 * lax.axis_index("x") + 2 * lax.axis_index("y") + lax.axis_index("c")


def _peers():
    x, y, c = lax.axis_index("x"), lax.axis_index("y"), lax.axis_index("c")
    out = []
    for k in range(1, N_DEV):
        px, py, pc = (1 - x if k & 4 else x, 1 - y if k & 2 else y, 1 - c if k & 1 else c)
        out.append(((px, py, pc), 4 * px + 2 * py + pc))
    return out


def _push_copies(src_refs, land_refs, send_sems, recv_sems, scatter, landing):
    me = _my_id()
    out = []
    for ti, (src, land) in enumerate(zip(src_refs, land_refs)):
        for k, (dev, pid) in enumerate(_peers()):
            out.append(pltpu.make_async_remote_copy(
                src_ref=src.at[pid] if scatter else src, dst_ref=land.at[pid if landing else me],
                send_sem=send_sems.at[ti * N_PEER + k], recv_sem=recv_sems.at[ti * N_PEER + k],
                device_id=dev, device_id_type=MESH_ID))
    return out


def _push_start(name, srcs, scatter, after=None):
    n = len(srcs)
    slot = lambda s: s.shape[1:] if scatter else s.shape
    lands = [lax.empty((N_DEV,) + slot(s), s.dtype) for s in srcs]
    n_after = 0 if after is None else 1

    def body(*refs):
        src_refs, land_refs = refs[:n], refs[n:2 * n]
        send_sems, recv_sems = refs[2 * n + n_after], refs[2 * n + n_after + 1]
        token = refs[-1]
        for cp in _push_copies(src_refs, land_refs, send_sems, recv_sems, scatter, False):
            cp.start()
        token[...] = jnp.zeros_like(token)

    hbm = lambda a: pltpu.with_memory_space_constraint(a, pltpu.HBM)
    operands = [hbm(a) for a in srcs + lands] + ([after] if n_after else [])
    res = pl.pallas_call(
        body, name=name,
        out_shape=[pltpu.SemaphoreType.DMA((n * N_PEER,)), pltpu.SemaphoreType.DMA((n * N_PEER,))]
        + [pltpu.HBM(a.shape, a.dtype) for a in srcs + lands] + [_sds((8, 128), F32)],
        in_specs=[HBM] * (2 * n) + [ANY] * n_after,
        out_specs=[SEM, SEM] + [HBM] * (2 * n) + [pl.BlockSpec(memory_space=pltpu.VMEM)],
        input_output_aliases={i: 2 + i for i in range(2 * n)},
        compiler_params=pltpu.CompilerParams(has_side_effects=EFFECT),
    )(*operands)
    return (res[0], res[1], list(res[2:2 + n]), list(res[2 + n:2 + 2 * n]), scatter), res[-1]


def _push_wait(name, handle, after):
    send_sems, recv_sems, srcs, lands, scatter = handle
    n = len(srcs)

    def body(*refs):
        src_refs, land_refs = refs[:n], refs[n:2 * n]
        s_sems, r_sems = refs[2 * n], refs[2 * n + 1]
        for cp in _push_copies(src_refs, land_refs, s_sems, r_sems, scatter, True):
            cp.wait_send()
            cp.wait_recv()

    res = pl.pallas_call(
        body, name=name,
        out_shape=[pltpu.HBM(a.shape, a.dtype) for a in srcs + lands],
        in_specs=[HBM] * (2 * n) + [SEM, SEM, ANY], out_specs=[HBM] * (2 * n),
        input_output_aliases={i: i for i in range(2 * n)},
        compiler_params=pltpu.CompilerParams(has_side_effects=EFFECT),
    )(*srcs, *lands, send_sems, recv_sems, after)
    return list(res[n:])


def _adamw(name, parts, w, m, v, own=None):
    r, c = w.shape
    br = _pick(r, 256, 16)
    has_own = own is not None

    def body(*refs):
        if has_own:
            p_ref, own_ref, w_ref, m_ref, v_ref, g_ref, d_ref, nm_ref, nv_ref = refs
            me = _my_id()
            mine = own_ref[...].astype(F32)
        else:
            p_ref, w_ref, m_ref, v_ref, g_ref, d_ref, nm_ref, nv_ref = refs
        g = None
        for k in range(N_DEV):
            t = p_ref[k].astype(F32)
            if has_own:
                t = jnp.where(me == k, mine, t)
            g = t if g is None else g + t
        mm = ADAM_B1 * m_ref[...] + (1.0 - ADAM_B1) * g
        vv = ADAM_B2 * v_ref[...] + (1.0 - ADAM_B2) * (g * g)
        m_hat = mm / (1.0 - ADAM_B1 ** ADAM_STEP)
        v_hat = vv / (1.0 - ADAM_B2 ** ADAM_STEP)
        g_ref[...] = g
        d_ref[...] = -ADAM_LR * (m_hat / (jnp.sqrt(v_hat) + ADAM_EPS) + ADAM_WD * w_ref[...])
        nm_ref[...] = mm
        nv_ref[...] = vv

    spec = _rows(br, c)
    out = (_sds((r, c), F32), spec)
    ins = [(parts, pl.BlockSpec((N_DEV, br, c), lambda i: (0, i, 0)))] + ([(own, spec)] if has_own else [])
    return _call(name, body, (r // br,), ins + [(w, spec), (m, spec), (v, spec)], [out] * 4, sem=("parallel",))


def _pad_head_cols(w, d):
    k = w.shape[0]
    return jnp.pad(w.reshape(k, HEADS, d), ((0, 0), (0, 0), (0, HP - d))).reshape(k, HW)


def _unpad_head_cols(wp, d):
    k = wp.shape[0]
    return wp.reshape(k, HEADS, HP)[:, :, :d].reshape(k, HEADS * d)


def _pad_head_rows(w, d):
    n = w.shape[1]
    return jnp.pad(w.reshape(HEADS, d, n), ((0, 0), (0, HP - d), (0, 0))).reshape(HW, n)


def _unpad_head_rows(wp, d):
    n = wp.shape[1]
    return wp.reshape(HEADS, HP, n)[:, :d, :].reshape(HEADS * d, n)


IN_SEGS = (("q", Q_RANK), ("kv", KV_RANK), ("kr", ROPE), ("fq", FOX_W), ("fk", FOX_W), ("fv", FOX_W),
           ("fl", HEADS), ("gate", 2 * D_MODEL))


def _split_w_in(w):
    seg = {}
    o = 0
    for nm, wd in IN_SEGS:
        seg[nm] = w[:, o:o + wd]
        o += wd
    d = w.shape[0]
    z = lambda n: jnp.zeros((d, n), w.dtype)
    seg["fq"] = seg["fq"] * jnp.asarray(FOX_SCALE, w.dtype)
    fused = jnp.concatenate([_pad_head_cols(seg[nm], FOX_DIM) for nm in ("fq", "fk", "fv")], axis=1)
    last = jnp.concatenate([seg["fl"], z(LANE_PE - HEADS), seg["kr"], z(HP - LANE_PE - ROPE)], axis=1)
    rest = jnp.concatenate([seg["q"], seg["kv"], last, z(R_GATE - R_LAST - HP), seg["gate"]], axis=1)
    return fused, rest


def _merge_w_in(fused, rest):
    f = [_unpad_head_cols(fused[:, i * HW:(i + 1) * HW], FOX_DIM) for i in range(3)]
    f[0] = f[0] * jnp.asarray(FOX_SCALE, fused.dtype)
    last = rest[:, R_LAST:R_LAST + HP]
    return jnp.concatenate([rest[:, R_QLAT:R_LAST], last[:, LANE_PE:LANE_PE + ROPE], f[0], f[1], f[2],
                            last[:, LANE_FL:LANE_FL + HEADS], rest[:, R_GATE:]], axis=1)


def _split_w_kv(w):
    k = w.shape[0]
    w3 = w.reshape(k, HEADS, NOPE + V_DIM)
    padl = lambda a: jnp.pad(a, ((0, 0), (0, 0), (0, HP - a.shape[-1]))).reshape(k, HW)
    return padl(w3[..., :NOPE]), padl(w3[..., NOPE:])


def _merge_w_kv(wk, wv):
    k = wk.shape[0]
    return jnp.concatenate([wk.reshape(k, HEADS, HP)[..., :NOPE], wv.reshape(k, HEADS, HP)[..., :V_DIM]],
                           axis=-1).reshape(k, HEADS * (NOPE + V_DIM))


class _NoComm:
    first_token = ()

    def late_weights(self, group, after):
        return {}

    def send(self, name, grads):
        return ()


def _local_step(x, tgt, p, comm=_NoComm()):
    seq = x.shape[0]
    r = -(-(N_META + seq) // ROW_ALIGN) * ROW_ALIGN
    cd = MXU_DTYPE
    p = dict(p)

    w_f, w_r = _split_w_in(p["w_in"])
    w_q = _pad_head_cols(p["w_q_up"], QK_DIM)
    w_k, w_v = _split_w_kv(p["w_kv_up"])

    pos = jnp.arange(r, dtype=F32)
    inv_freq = ROPE_THETA ** (-jnp.arange(HALF, dtype=F32) / HALF)
    ang = pos[:, None] * inv_freq[None, :]
    cos_t = jnp.tile(jnp.cos(ang), (1, HP // HALF))
    sin_t = jnp.tile(jnp.sin(ang), (1, HP // HALF))
    bf_row = jnp.zeros((1, HP), F32).at[0, LANE_FL:LANE_FL + HEADS].set(p["b_forget"])

    h0, h0b = _ln_emb_fwd(x, p["meta_tokens"], p["ln_emb_g"], p["ln_emb_b"], r, after=comm.first_token)
    proj_f = _matmul("in_proj_f", h0b, w_f, out_dtype=cd)
    proj_r = _matmul("in_proj_r", h0b, w_r)
    ql = _rms_fwd("q_norm_fwd", proj_r, R_QLAT // Q_RANK, Q_RANK, p["q_norm_g"])
    kvl = _rms_fwd("kv_norm_fwd", proj_r, R_KVLAT // KV_RANK, KV_RANK, p["kv_norm_g"])
    q_raw = _matmul("q_up", ql, w_q)
    k_part = _matmul("k_up", kvl, w_k)
    v_mla = _matmul("v_up", kvl, w_v, out_dtype=cd)
    q_mla, k_mla = _rope_fwd(q_raw, k_part, proj_r, cos_t, sin_t)
    o_mla, o_mla_b, lse_mla = _attn_fwd("mla_fwd", (q_mla, 0), (k_mla, 0), (v_mla, 0))

    cum, cum_t = _forget_fwd(proj_r, bf_row)
    o_fox, o_fox_b, lse_fox = _attn_fwd("fox_fwd", (proj_f, 0), (proj_f, 1), (proj_f, 2), cum, cum_t)

    p.update(comm.late_weights("mix", o_fox_b))
    w_bm = _pad_head_rows(p["w_branch_mla"], V_DIM)
    w_bf = _pad_head_rows(p["w_branch_fox"], FOX_DIM)
    bm = _matmul("branch_mla", o_mla_b, w_bm, out_dtype=cd)
    bfx = _matmul("branch_fox", o_fox_b, w_bf, out_dtype=cd)
    merged = _gate_fwd(proj_r, p["b_gate"], bm, bfx)
    mix = _matmul("out_proj", merged, p["w_out"])
    h1, h1b = _ln_fwd("ln_mix_fwd", h0, mix, p["ln_mix_g"], p["ln_mix_b"])
    p.update(comm.late_weights("ffn", h1b))
    up = _matmul("ffn_up", h1b, p["w_ffn_up"], out_dtype=cd)
    act = _glu_fwd(up, p["conv_w"], p["conv_b"])
    f = _matmul("ffn_down", act, p["w_ffn_down"])
    loss = _ln_ffn_loss(h1, f, tgt, p["ln_ffn_g"], p["ln_ffn_b"])

    g = {}
    dz2, dz2b, g["ln_ffn_g"], g["ln_ffn_b"] = _ln_ffn_bwd(h1, f, tgt, p["ln_ffn_g"], p["ln_ffn_b"])
    d_act = _matmul("ffn_down_dx", dz2b, p["w_ffn_down"], tb=True, out_dtype=cd)
    g["w_ffn_down"] = _matmul("ffn_down_dw", act, dz2b, ta=True, out_dtype=cd)
    dgate, dval, dcw, g["conv_b"] = _glu_bwd_gate(up, p["conv_w"], p["conv_b"], d_act)
    g["conv_w"] = dcw[:3]
    d_up = _glu_bwd_conv(dgate, dval, p["conv_w"])
    dh1 = _matmul("ffn_up_dx", d_up, p["w_ffn_up"], tb=True, addend=dz2, alpha=ALPHA)
    g["w_ffn_up"] = _matmul("ffn_up_dw", h1b, d_up, ta=True, out_dtype=cd)
    sent = comm.send("ffn", {n: g[n] for n in ("w_ffn_down", "w_ffn_up", "conv_w")})
    dz1, dz1b, g["ln_mix_g"], g["ln_mix_b"] = _ln_bwd("ln_mix_bwd", h0, mix, dh1, p["ln_mix_g"], after=sent)
    dmerged = _matmul("out_proj_dx", dz1b, p["w_out"], tb=True, out_dtype=cd)
    g["w_out"] = _matmul("out_proj_dw", merged, dz1b, ta=True, out_dtype=cd)
    d_bm, d_bf, d_gl, g["b_gate"] = _gate_bwd(proj_r, p["b_gate"], bm, bfx, dmerged)
    do_mla_b = _matmul("branch_mla_dx", d_bm, w_bm, tb=True, out_dtype=cd)
    g["w_branch_mla"] = _unpad_head_rows(_matmul("branch_mla_dw", o_mla_b, d_bm, ta=True, out_dtype=cd), V_DIM)
    do_fox_b = _matmul("branch_fox_dx", d_bf, w_bf, tb=True, out_dtype=cd)
    g["w_branch_fox"] = _unpad_head_rows(_matmul("branch_fox_dw", o_fox_b, d_bf, ta=True, out_dtype=cd), FOX_DIM)

    sent = comm.send("mix", {n: g[n] for n in ("w_out", "w_branch_mla", "w_branch_fox")})
    dl_mla = _attn_delta("mla_delta", do_mla_b, o_mla, after=sent)
    dq_m, dk_m, dv_m = _attn_bwd("mla_bwd", (q_mla, 0), (k_mla, 0), (v_mla, 0), do_mla_b, lse_mla, dl_mla)
    dl_fox = _attn_delta("fox_delta", do_fox_b, o_fox)
    dfq, dfk, dfv, dcq, dck = _attn_bwd("fox_bwd", (proj_f, 0), (proj_f, 1), (proj_f, 2), do_fox_b, lse_fox, dl_fox,
                                        cum, cum_t, out_dtype=cd)
    dfl, dbf = _forget_bwd(proj_r, bf_row, dcq, dck)
    g["b_forget"] = dbf[:, LANE_FL:LANE_FL + HEADS]

    dq_b, dk_b, dlast = _rope_bwd(dq_m, dk_m, dfl, cos_t, sin_t)
    dv_b = dv_m.astype(cd)
    d_ql = _matmul("q_up_dx", dq_b, w_q, tb=True)
    d_kvl = _matmul("k_up_dx", dk_b, w_k, tb=True)
    d_kvl = _matmul("v_up_dx", dv_b, w_v, tb=True, addend=d_kvl)
    d_qlat, g["q_norm_g"] = _rms_bwd("q_norm_bwd", proj_r, R_QLAT // Q_RANK, Q_RANK, d_ql, p["q_norm_g"])
    d_kvlat, g["kv_norm_g"] = _rms_bwd("kv_norm_bwd", proj_r, R_KVLAT // KV_RANK, KV_RANK, d_kvl, p["kv_norm_g"])
    dproj_f = jnp.concatenate([dfq, dfk, dfv], axis=1)
    dproj_r = jnp.concatenate([d_qlat, d_kvlat, dlast, jnp.zeros((r, R_GATE - R_LAST - HP), cd), d_gl], axis=1)
    g["w_in"] = _merge_w_in(_matmul("in_proj_f_dw", h0b, dproj_f, ta=True, out_dtype=cd), _matmul("in_proj_r_dw", h0b, dproj_r, ta=True, out_dtype=cd))
    sent = comm.send("in", {"w_in": g["w_in"]})
    dh0 = _matmul("in_proj_f_dx", dproj_f, w_f, tb=True, addend=dz1, alpha=ALPHA, after=sent)
    g["w_q_up"] = _unpad_head_cols(_matmul("q_up_dw", ql, dq_b, ta=True, out_dtype=cd, after=sent), QK_DIM)
    g["w_kv_up"] = _merge_w_kv(_matmul("k_up_dw", kvl, dk_b, ta=True, out_dtype=cd, after=sent),
                               _matmul("v_up_dw", kvl, dv_b, ta=True, out_dtype=cd, after=sent))
    sent = comm.send("qkv", {n: g[n] for n in ("w_q_up", "w_kv_up")})
    dh0 = _matmul("in_proj_r_dx", dproj_r, w_r, tb=True, addend=dh0, after=sent)
    grad_x, d_meta, g["ln_emb_g"], g["ln_emb_b"] = _ln_emb_bwd(x, p["meta_tokens"], dh0, p["ln_emb_g"])
    return loss, grad_x, d_meta, g


BIG = (("w_in", 1), ("w_q_up", 1), ("w_kv_up", 1), ("w_branch_mla", 1), ("w_branch_fox", 1), ("w_out", 0),
       ("w_ffn_up", 1), ("w_ffn_down", 0))
SMALL_SHARDED = (("meta_tokens", 1), ("conv_w", 1))
EARLY = ("w_in", "w_q_up", "w_kv_up", "meta_tokens", "conv_w")
LATE = {"mix": ("w_branch_mla", "w_branch_fox", "w_out"),
        "ffn": ("w_ffn_up", "w_ffn_down")}
REPLICATED = ("ln_emb_g", "ln_emb_b", "b_gate", "b_forget", "q_norm_g", "kv_norm_g", "ln_mix_g", "ln_mix_b",
              "conv_b", "ln_ffn_g", "ln_ffn_b")
PACK_COLS = 1024


def _pack(flat_list):
    cat = jnp.concatenate(flat_list)
    n = cat.shape[0]
    rows = -(-n // (8 * PACK_COLS)) * 8
    return jnp.pad(cat, (0, rows * PACK_COLS - n)).reshape(rows, PACK_COLS)


def _gathered_full(g3, axis):
    n, r, c = g3.shape
    if axis == 0:
        return g3.reshape(n * r, c)
    return g3.transpose(1, 0, 2).reshape(r, n * c)


def _shard_major(full, axis):
    r, c = full.shape
    if axis == 0:
        return full.reshape(N_DEV, r // N_DEV, c)
    return full.reshape(r, N_DEV, c // N_DEV).transpose(1, 0, 2)


def kernel(x, meta_tokens, ln_emb_g, ln_emb_b, w_in, b_gate, b_forget, q_norm_g, w_q_up, kv_norm_g, w_kv_up, w_branch_mla, w_branch_fox, w_out, ln_mix_g, ln_mix_b, w_ffn_up, conv_w, conv_b, w_ffn_down, ln_ffn_g, ln_ffn_b, loss_target, m_meta_tokens, m_ln_emb_g, m_ln_emb_b, m_w_in, m_b_gate, m_b_forget, m_q_norm_g, m_w_q_up, m_kv_norm_g, m_w_kv_up, m_w_branch_mla, m_w_branch_fox, m_w_out, m_ln_mix_g, m_ln_mix_b, m_w_ffn_up, m_conv_w, m_conv_b, m_w_ffn_down, m_ln_ffn_g, m_ln_ffn_b, v_meta_tokens, v_ln_emb_g, v_ln_emb_b, v_w_in, v_b_gate, v_b_forget, v_q_norm_g, v_w_q_up, v_kv_norm_g, v_w_kv_up, v_w_branch_mla, v_w_branch_fox, v_w_out, v_ln_mix_g, v_ln_mix_b, v_w_ffn_up, v_conv_w, v_conv_b, v_w_ffn_down, v_ln_ffn_g, v_ln_ffn_b):
    names = ("meta_tokens", "ln_emb_g", "ln_emb_b", "w_in", "b_gate", "b_forget", "q_norm_g", "w_q_up", "kv_norm_g",
             "w_kv_up", "w_branch_mla", "w_branch_fox", "w_out", "ln_mix_g", "ln_mix_b", "w_ffn_up", "conv_w", "conv_b",
             "w_ffn_down", "ln_ffn_g", "ln_ffn_b")
    w_args = (meta_tokens, ln_emb_g, ln_emb_b, w_in, b_gate, b_forget, q_norm_g, w_q_up, kv_norm_g, w_kv_up,
              w_branch_mla, w_branch_fox, w_out, ln_mix_g, ln_mix_b, w_ffn_up, conv_w, conv_b, w_ffn_down, ln_ffn_g, ln_ffn_b)
    m_args = (m_meta_tokens, m_ln_emb_g, m_ln_emb_b, m_w_in, m_b_gate, m_b_forget, m_q_norm_g, m_w_q_up, m_kv_norm_g,
              m_w_kv_up, m_w_branch_mla, m_w_branch_fox, m_w_out, m_ln_mix_g, m_ln_mix_b, m_w_ffn_up, m_conv_w, m_conv_b,
              m_w_ffn_down, m_ln_ffn_g, m_ln_ffn_b)
    v_args = (v_meta_tokens, v_ln_emb_g, v_ln_emb_b, v_w_in, v_b_gate, v_b_forget, v_q_norm_g, v_w_q_up, v_kv_norm_g,
              v_w_kv_up, v_w_branch_mla, v_w_branch_fox, v_w_out, v_ln_mix_g, v_ln_mix_b, v_w_ffn_up, v_conv_w, v_conv_b,
              v_w_ffn_down, v_ln_ffn_g, v_ln_ffn_b)
    as2d = lambda a: a.reshape((-1, a.shape[-1])) if a.ndim != 1 else a.reshape(1, -1)
    w = {n: as2d(a) for n, a in zip(names, w_args)}
    m = {n: as2d(a) for n, a in zip(names, m_args)}
    v = {n: as2d(a) for n, a in zip(names, v_args)}
    out_shape = {n: a.shape for n, a in zip(names, w_args)}

    axis_of = dict(BIG + SMALL_SHARDED)
    big = set(n for n, _ in BIG)
    wire = lambda n, a: a.astype(MXU_DTYPE) if n in big else a
    my_id = _my_id()

    early = _allgather("gather_early", [wire(n, w[n]) for n in EARLY])
    p = {n: _gathered_full(g3, axis_of[n]) for n, g3 in zip(EARLY, early)}
    for n in REPLICATED:
        p[n] = w[n].reshape(-1)
    late, tokens, prev = {}, [], early[0]
    for group, members in LATE.items():
        src = [wire(n, w[n]) for n in members]
        handle, token = _push_start("gather_" + group + "_start", src, False, after=prev)
        late[group] = (members, src, handle)
        tokens.append(token)
        prev = token
    sent = {}

    class Comm:
        first_token = tuple(tokens)

        def late_weights(self, group, after):
            members, src, handle = late[group]
            lands = _push_wait("gather_" + group + "_wait", handle, after)
            return {n: _gathered_full(lax.dynamic_update_index_in_dim(land, own, my_id, 0), axis_of[n])
                    for n, own, land in zip(members, src, lands)}

        def send(self, name, grads):
            names_ = tuple(grads)
            parts = [_shard_major(grads[n], axis_of[n]).astype(MXU_DTYPE) for n in names_]
            handle, token = _push_start("send_" + name + "_start", parts, True)
            sent[name] = (names_, parts, handle)
            return (token,)

    loss_part, grad_x, d_meta, g = _local_step(x[0], loss_target[0], p, Comm())
    grad_x = grad_x[None]

    small = _pack([d_meta.reshape(-1)] + [g[n].reshape(-1) for n in REPLICATED] + [loss_part.reshape(-1)])
    small_handle, small_token = _push_start("send_small_start", [small], False)

    res = {}
    prev = small_token
    for name, (names_, parts, handle) in sent.items():
        lands = _push_wait("send_" + name + "_wait", handle, prev)
        for n, part, land in zip(names_, parts, lands):
            own = lax.dynamic_index_in_dim(part, my_id, axis=0, keepdims=False)
            res[n] = _adamw("adamw_" + n, land, w[n], m[n], v[n], own=own)
            prev = res[n][0]
    small_all = _push_wait("send_small_wait", small_handle, prev)[0]
    head = jnp.zeros((d_meta.size,), F32)
    rep_w = _pack([head] + [w[n].reshape(-1) for n in REPLICATED])
    rep_m = _pack([head] + [m[n].reshape(-1) for n in REPLICATED])
    rep_v = _pack([head] + [v[n].reshape(-1) for n in REPLICATED])
    rep_res = _adamw("adamw_replicated", small_all, rep_w, rep_m, rep_v, own=small)
    off = d_meta.size
    for n in REPLICATED:
        sz = w[n].size
        res[n] = tuple(a.reshape(-1)[off:off + sz] for a in rep_res)
        off += sz
    loss = rep_res[0].reshape(-1)[off]
    cols = w["meta_tokens"].shape[1]
    meta_rows = lambda a: a.reshape(a.shape[:-2] + (-1,))[..., :d_meta.size].reshape(a.shape[:-2] + d_meta.shape)
    my_cols = lambda a: lax.dynamic_slice_in_dim(a, my_id * cols, cols, axis=a.ndim - 1)
    res["meta_tokens"] = _adamw("adamw_meta_tokens", my_cols(meta_rows(small_all)), w["meta_tokens"],
                                m["meta_tokens"], v["meta_tokens"], own=my_cols(d_meta))

    outs = [loss, grad_x]
    for idx in range(4):
        outs += [res[n][idx].reshape(out_shape[n]) for n in names]
    return tuple(outs)
```

```python
import jax
import jax.numpy as jnp
from jax import lax
from jax.experimental import pallas as pl
from jax.experimental.pallas import tpu as pltpu

F32 = jnp.float32
BF16 = jnp.bfloat16
MXU_DTYPE = BF16

N_DEV = 8
N_META = 16
D_MODEL = 1024
HEADS = 8
Q_RANK = 384
KV_RANK = 128
NOPE = 64
ROPE = 32
HALF = ROPE // 2
QK_DIM = NOPE + ROPE
V_DIM = 64
FOX_DIM = 64
FOX_W = HEADS * FOX_DIM
D_FF = 2816
ROPE_THETA = 10000.0
LN_EPS = 1e-5
RMS_EPS = 1e-6
ALPHA = 2.0 ** 0.25
MLA_SCALE = QK_DIM ** -0.5
FOX_SCALE = FOX_DIM ** -0.5
NEG_INF = -1e30

HP = 128
HW = HEADS * HP
F_W = 3 * HW
R_QLAT = 0
R_KVLAT = Q_RANK
R_LAST = R_KVLAT + KV_RANK
R_GATE = D_MODEL
R_W = R_GATE + 2 * D_MODEL
LANE_FL = 0
LANE_PE = NOPE

ADAM_LR = 0.001
ADAM_B1 = 0.9
ADAM_B2 = 0.999
ADAM_EPS = 1e-08
ADAM_WD = 0.01
ADAM_STEP = 10

ROW_BLOCK = 256
ATT_TQ = 768
ATT_TK = 256
ATT_HEADS = 2
ROW_ALIGN = 768
MM_BLOCK_CAP = 1408
VMEM_LIMIT = 56 * 1024 * 1024
HIGHEST = lax.Precision.HIGHEST
NT = (((1,), (1,)), ((), ()))
TN = (((0,), (0,)), ((), ()))


def _params(sem=None):
    return pltpu.CompilerParams(dimension_semantics=sem, vmem_limit_bytes=VMEM_LIMIT)


def _call(name, body, grid, ins, outs, scratch=(), sem=None, after=()):
    n_in = len(ins)
    n_tok = len(after)

    def run(*refs):
        body(*refs[:n_in], *refs[n_in + n_tok:])

    tok_spec = pl.BlockSpec((8, 128), lambda *_: (0, 0))
    return pl.pallas_call(
        run, name=name, grid=grid,
        in_specs=[s for _, s in ins] + [tok_spec] * n_tok,
        out_specs=[s for _, s in outs],
        out_shape=[o for o, _ in outs],
        scratch_shapes=list(scratch),
        compiler_params=_params(sem),
    )(*[a for a, _ in ins], *after)


def _sds(shape, dtype):
    return jax.ShapeDtypeStruct(shape, dtype)


def _rows(br, c, cb=0):
    return pl.BlockSpec((br, c), lambda i: (i, cb))


def _whole(shape):
    n = len(shape)
    return pl.BlockSpec(shape, lambda i: (0,) * n)


def _pick(dim, cap, mult):
    best = None
    d = mult
    while d <= min(dim, cap):
        if dim % d == 0:
            best = d
        d += mult
    return best if best is not None else dim


def _hs(h):
    return slice(h * HP, (h + 1) * HP)


def _matmul(name, a, b, *, ta=False, tb=False, out_dtype=F32, addend=None, alpha=1.0, after=()):
    if ta:
        k, m = a.shape
    else:
        m, k = a.shape
    if tb:
        n, k2 = b.shape
    else:
        k2, n = b.shape
    assert k == k2, (name, a.shape, b.shape)
    bm = _pick(m, MM_BLOCK_CAP, 128 if ta else 16)
    bn = _pick(n, MM_BLOCK_CAP, 128)
    bk = _pick(k, MM_BLOCK_CAP, 128 if (not ta or tb) else 16)
    nk = k // bk
    dims = (((0 if ta else 1,), (1 if tb else 0,)), ((), ()))
    has_add = addend is not None

    def body(*refs):
        a_ref, b_ref = refs[:2]
        add_ref = refs[2] if has_add else None
        o_ref = refs[3 if has_add else 2]

        def finish(r):
            if has_add:
                r = r + alpha * add_ref[...]
            o_ref[...] = r.astype(o_ref.dtype)

        part = lax.dot_general(a_ref[...], b_ref[...], dims, preferred_element_type=F32)
        if nk == 1:
            finish(part)
            return
        acc_ref = refs[-1]
        kk = pl.program_id(2)

        @pl.when(kk == 0)
        def _():
            acc_ref[...] = part

        @pl.when(kk > 0)
        def _():
            acc_ref[...] += part

        @pl.when(kk == nk - 1)
        def _():
            finish(acc_ref[...])

    a_spec = pl.BlockSpec((bk, bm), lambda i, j, l: (l, i)) if ta else pl.BlockSpec((bm, bk), lambda i, j, l: (i, l))
    b_spec = pl.BlockSpec((bn, bk), lambda i, j, l: (j, l)) if tb else pl.BlockSpec((bk, bn), lambda i, j, l: (l, j))
    o_spec = pl.BlockSpec((bm, bn), lambda i, j, l: (i, j))
    ins = [(a, a_spec), (b, b_spec)]
    if has_add:
        ins.append((addend, o_spec))
    return _call(name, body, (m // bm, n // bn, nk), ins, [(_sds((m, n), out_dtype), o_spec)],
                 scratch=[pltpu.VMEM((bm, bn), F32)] if nk > 1 else [],
                 sem=("parallel", "parallel", "arbitrary"), after=after)[0]


def _ln_stats(z):
    mu = jnp.mean(z, axis=-1, keepdims=True)
    zc = z - mu
    var = jnp.mean(zc * zc, axis=-1, keepdims=True)
    rstd = lax.rsqrt(var + LN_EPS)
    return zc * rstd, rstd


def _ln_fwd(name, a, res, g, b, after=()):
    r, d = a.shape
    br = ROW_BLOCK
    has_res = res is not None

    def body(*refs):
        if has_res:
            a_ref, r_ref, g_ref, b_ref, y_ref, yb_ref = refs
            z = ALPHA * a_ref[...] + r_ref[...]
        else:
            a_ref, g_ref, b_ref, y_ref, yb_ref = refs
            z = a_ref[...]
        xhat, _ = _ln_stats(z)
        y = xhat * g_ref[...] + b_ref[...]
        y_ref[...] = y
        yb_ref[...] = y.astype(yb_ref.dtype)

    ins = [(a, _rows(br, d))]
    if has_res:
        ins.append((res, _rows(br, d)))
    ins += [(g.reshape(1, d), _whole((1, d))), (b.reshape(1, d), _whole((1, d)))]
    outs = [(_sds((r, d), F32), _rows(br, d)), (_sds((r, d), MXU_DTYPE), _rows(br, d))]
    return _call(name, body, (r // br,), ins, outs, sem=("parallel",), after=after)


def _ln_bwd(name, a, res, dy, g, after=()):
    r, d = a.shape
    br = ROW_BLOCK
    has_res = res is not None

    def body(*refs):
        if has_res:
            a_ref, r_ref, dy_ref, g_ref, dz_ref, dzb_ref, dg_ref, db_ref = refs
            z = ALPHA * a_ref[...] + r_ref[...]
        else:
            a_ref, dy_ref, g_ref, dz_ref, dzb_ref, dg_ref, db_ref = refs
            z = a_ref[...]
        xhat, rstd = _ln_stats(z)
        dyv = dy_ref[...]
        dyg = dyv * g_ref[...]
        m1 = jnp.mean(dyg, axis=-1, keepdims=True)
        m2 = jnp.mean(dyg * xhat, axis=-1, keepdims=True)
        dz = rstd * (dyg - m1 - xhat * m2)
        dz_ref[...] = dz
        dzb_ref[...] = dz.astype(dzb_ref.dtype)

        @pl.when(pl.program_id(0) == 0)
        def _():
            dg_ref[...] = jnp.zeros_like(dg_ref)
            db_ref[...] = jnp.zeros_like(db_ref)

        dg_ref[...] += jnp.sum(dyv * xhat, axis=0, keepdims=True)
        db_ref[...] += jnp.sum(dyv, axis=0, keepdims=True)

    ins = [(a, _rows(br, d))]
    if has_res:
        ins.append((res, _rows(br, d)))
    ins += [(dy, _rows(br, d)), (g.reshape(1, d), _whole((1, d)))]
    outs = [(_sds((r, d), F32), _rows(br, d)), (_sds((r, d), MXU_DTYPE), _rows(br, d)),
            (_sds((1, d), F32), _whole((1, d))), (_sds((1, d), F32), _whole((1, d)))]
    return _call(name, body, (r // br,), ins, outs, sem=("arbitrary",), after=after)


def _rms_fwd(name, proj, cb, width, g):
    r = proj.shape[0]
    br = ROW_BLOCK

    def body(x_ref, g_ref, y_ref):
        x = x_ref[...]
        rstd = lax.rsqrt(jnp.mean(x * x, axis=-1, keepdims=True) + RMS_EPS)
        y_ref[...] = (x * rstd * g_ref[...]).astype(y_ref.dtype)

    return _call(name, body, (r // br,), [(proj, _rows(br, width, cb)), (g.reshape(1, width), _whole((1, width)))],
                 [(_sds((r, width), MXU_DTYPE), _rows(br, width))], sem=("parallel",))[0]


def _rms_bwd(name, proj, cb, width, dy, g):
    r = proj.shape[0]
    br = ROW_BLOCK

    def body(x_ref, dy_ref, g_ref, dx_ref, dg_ref):
        x = x_ref[...]
        rstd = lax.rsqrt(jnp.mean(x * x, axis=-1, keepdims=True) + RMS_EPS)
        nrm = x * rstd
        dyv = dy_ref[...]
        dyg = dyv * g_ref[...]
        dx = rstd * (dyg - nrm * jnp.mean(dyg * nrm, axis=-1, keepdims=True))
        dx_ref[...] = dx.astype(dx_ref.dtype)

        @pl.when(pl.program_id(0) == 0)
        def _():
            dg_ref[...] = jnp.zeros_like(dg_ref)

        dg_ref[...] += jnp.sum(dyv * nrm, axis=0, keepdims=True)

    return _call(name, body, (r // br,),
                 [(proj, _rows(br, width, cb)), (dy, _rows(br, width)), (g.reshape(1, width), _whole((1, width)))],
                 [(_sds((r, width), MXU_DTYPE), _rows(br, width)), (_sds((1, width), F32), _whole((1, width)))],
                 sem=("arbitrary",))


def _lane_iota(shape):
    return lax.broadcasted_iota(jnp.int32, shape, 1)


def _rotary(t, c, s, lane, sign):
    second = pltpu.roll(t, HP - HALF, axis=1)
    first = pltpu.roll(t, HALF, axis=1)
    lo = (lane >= LANE_PE) & (lane < LANE_PE + HALF)
    hi = (lane >= LANE_PE + HALF) & (lane < LANE_PE + ROPE)
    return jnp.where(lo, t * c - sign * second * s, jnp.where(hi, t * c + sign * first * s, t))


def _rope_fwd(q_raw, k_part, proj_r, cos_t, sin_t):
    r = q_raw.shape[0]
    br = ROW_BLOCK

    def body(q_ref, k_ref, t_ref, c_ref, s_ref, qo_ref, ko_ref):
        c = c_ref[...]
        s = s_ref[...]
        lane = _lane_iota((br, HP))
        pe = (lane >= LANE_PE) & (lane < LANE_PE + ROPE)
        kp = jnp.where(pe, _rotary(t_ref[...], c, s, lane, 1.0), 0.0)
        for h in range(HEADS):
            qo_ref[:, _hs(h)] = (_rotary(q_ref[:, _hs(h)], c, s, lane, 1.0) * MLA_SCALE).astype(qo_ref.dtype)
            ko_ref[:, _hs(h)] = (k_ref[:, _hs(h)] + kp).astype(ko_ref.dtype)

    blk = _rows(br, HP)
    wide = _rows(br, HW)
    return _call("rope_fwd", body, (r // br,),
                 [(q_raw, wide), (k_part, wide), (proj_r, _rows(br, HP, R_LAST // HP)), (cos_t, blk), (sin_t, blk)],
                 [(_sds((r, HW), MXU_DTYPE), wide)] * 2, sem=("parallel",))


def _rope_bwd(dq, dk, dfl, cos_t, sin_t):
    r = dq.shape[0]
    br = ROW_BLOCK

    def body(dq_ref, dk_ref, fl_ref, c_ref, s_ref, dqo_ref, dko_ref, dl_ref):
        c = c_ref[...]
        s = s_ref[...]
        lane = _lane_iota((br, HP))
        pe = (lane >= LANE_PE) & (lane < LANE_PE + ROPE)
        acc = jnp.zeros((br, HP), F32)
        for h in range(HEADS):
            dqo_ref[:, _hs(h)] = (_rotary(dq_ref[:, _hs(h)], c, s, lane, -1.0) * MLA_SCALE).astype(dqo_ref.dtype)
            dkh = dk_ref[:, _hs(h)]
            acc = acc + dkh
            dko_ref[:, _hs(h)] = dkh.astype(dko_ref.dtype)
        dl_ref[...] = (jnp.where(pe, _rotary(acc, c, s, lane, -1.0), 0.0) + fl_ref[...]).astype(dl_ref.dtype)

    blk = _rows(br, HP)
    wide = _rows(br, HW)
    return _call("rope_bwd", body, (r // br,),
                 [(dq, wide), (dk, wide), (dfl, blk), (cos_t, blk), (sin_t, blk)],
                 [(_sds((r, HW), MXU_DTYPE), wide), (_sds((r, HW), MXU_DTYPE), wide), (_sds((r, HP), MXU_DTYPE), blk)],
                 sem=("parallel",))


def _log_sigmoid(x):
    return jnp.minimum(x, 0.0) - jnp.log(1.0 + jnp.exp(-jnp.abs(x)))


def _head_lane(x, h, lane):
    return jnp.sum(jnp.where(lane == h, x, 0.0), axis=1, keepdims=True)


def _forget_fwd(proj_r, bf_row):
    r = proj_r.shape[0]
    br = ROW_BLOCK

    def body(t_ref, b_ref, ob_ref, ot_ref, carry_ref):
        @pl.when(pl.program_id(0) == 0)
        def _():
            carry_ref[...] = jnp.zeros_like(carry_ref)

        x = t_ref[...] + b_ref[...]
        lane = _lane_iota(x.shape)
        lf = jnp.where((lane >= LANE_FL) & (lane < LANE_FL + HEADS), _log_sigmoid(x), 0.0)
        tri = (lax.broadcasted_iota(jnp.int32, (br, br), 0) >= lax.broadcasted_iota(jnp.int32, (br, br), 1)).astype(F32)
        cum = jnp.dot(tri, lf, precision=HIGHEST, preferred_element_type=F32) + carry_ref[0:1, :]
        for h in range(HEADS):
            ob_ref[:, _hs(h)] = jnp.broadcast_to(_head_lane(cum, LANE_FL + h, lane), (br, HP))
        ot_ref[...] = cum.T[LANE_FL:LANE_FL + HEADS, :]
        carry_ref[...] = jnp.broadcast_to(cum[br - 1:br, :], carry_ref.shape)

    return _call("forget_fwd", body, (r // br,),
                 [(proj_r, _rows(br, HP, R_LAST // HP)), (bf_row, _whole((1, HP)))],
                 [(_sds((r, HW), F32), _rows(br, HW)), (_sds((HEADS, r), F32), pl.BlockSpec((HEADS, br), lambda i: (0, i)))],
                 scratch=[pltpu.VMEM((8, HP), F32)], sem=("arbitrary",))


def _forget_bwd(proj_r, bf_row, dcq_t, dck_b):
    r = proj_r.shape[0]
    br = ROW_BLOCK
    nb = r // br

    def body(t_ref, b_ref, dcq_ref, dck_ref, o_ref, db_ref, carry_ref):
        @pl.when(pl.program_id(0) == 0)
        def _():
            carry_ref[...] = jnp.zeros_like(carry_ref)
            db_ref[...] = jnp.zeros_like(db_ref)

        lane = _lane_iota((br, HP))
        dc = jnp.concatenate([dcq_ref[...], jnp.zeros((HP - HEADS, br), F32)], axis=0).T
        for h in range(HEADS):
            dc = dc + jnp.where(lane == LANE_FL + h, dck_ref[:, h * HP:h * HP + 1], 0.0)
        triu = (lax.broadcasted_iota(jnp.int32, (br, br), 0) <= lax.broadcasted_iota(jnp.int32, (br, br), 1)).astype(F32)
        dlf = jnp.dot(triu, dc, precision=HIGHEST, preferred_element_type=F32) + carry_ref[0:1, :]
        carry_ref[...] = jnp.broadcast_to(dlf[0:1, :], carry_ref.shape)
        x = t_ref[...] + b_ref[...]
        dfl = jnp.where((lane >= LANE_FL) & (lane < LANE_FL + HEADS), dlf * jax.nn.sigmoid(-x), 0.0)
        o_ref[...] = dfl
        db_ref[...] += jnp.sum(dfl, axis=0, keepdims=True)

    rev = pl.BlockSpec((br, HP), lambda i: (nb - 1 - i, 0))
    return _call("forget_bwd", body, (nb,),
                 [(proj_r, pl.BlockSpec((br, HP), lambda i: (nb - 1 - i, R_LAST // HP))), (bf_row, _whole((1, HP))),
                  (dcq_t, pl.BlockSpec((HEADS, br), lambda i: (0, nb - 1 - i))),
                  (dck_b, pl.BlockSpec((br, HW), lambda i: (nb - 1 - i, 0)))],
                 [(_sds((r, HP), F32), rev), (_sds((1, HP), F32), _whole((1, HP)))],
                 scratch=[pltpu.VMEM((8, HP), F32)], sem=("arbitrary",))


def _gate_fwd(proj_r, b_gate, bm, bfx):
    r, d = bm.shape
    br = ROW_BLOCK
    cb = R_GATE // d

    def body(gm_ref, gf_ref, b1_ref, b2_ref, bm_ref, bf_ref, o_ref):
        g1 = jax.nn.sigmoid(gm_ref[...] + b1_ref[...])
        g2 = jax.nn.sigmoid(gf_ref[...] + b2_ref[...])
        o_ref[...] = (g1 * bm_ref[...].astype(F32) + g2 * bf_ref[...].astype(F32)).astype(o_ref.dtype)

    b1 = b_gate[:d].reshape(1, d)
    b2 = b_gate[d:].reshape(1, d)
    return _call("gate_fwd", body, (r // br,),
                 [(proj_r, _rows(br, d, cb)), (proj_r, _rows(br, d, cb + 1)), (b1, _whole((1, d))), (b2, _whole((1, d))),
                  (bm, _rows(br, d)), (bfx, _rows(br, d))],
                 [(_sds((r, d), MXU_DTYPE), _rows(br, d))], sem=("parallel",))[0]


def _gate_bwd(proj_r, b_gate, bm, bfx, dmerged):
    r, d = bm.shape
    br = ROW_BLOCK
    cb = R_GATE // d

    def body(gm_ref, gf_ref, b1_ref, b2_ref, bm_ref, bf_ref, dm_ref, dbm_ref, dbf_ref, dgl_ref, dbg_ref):
        g1 = jax.nn.sigmoid(gm_ref[...] + b1_ref[...])
        g2 = jax.nn.sigmoid(gf_ref[...] + b2_ref[...])
        dm = dm_ref[...].astype(F32)
        dbm_ref[...] = (dm * g1).astype(dbm_ref.dtype)
        dbf_ref[...] = (dm * g2).astype(dbf_ref.dtype)
        dl1 = dm * bm_ref[...].astype(F32) * (g1 * (1.0 - g1))
        dl2 = dm * bf_ref[...].astype(F32) * (g2 * (1.0 - g2))
        dgl_ref[:, 0:d] = dl1.astype(dgl_ref.dtype)
        dgl_ref[:, d:2 * d] = dl2.astype(dgl_ref.dtype)

        @pl.when(pl.program_id(0) == 0)
        def _():
            dbg_ref[...] = jnp.zeros_like(dbg_ref)

        dbg_ref[:, 0:d] += jnp.sum(dl1, axis=0, keepdims=True)
        dbg_ref[:, d:2 * d] += jnp.sum(dl2, axis=0, keepdims=True)

    b1 = b_gate[:d].reshape(1, d)
    b2 = b_gate[d:].reshape(1, d)
    return _call("gate_bwd", body, (r // br,),
                 [(proj_r, _rows(br, d, cb)), (proj_r, _rows(br, d, cb + 1)), (b1, _whole((1, d))), (b2, _whole((1, d))),
                  (bm, _rows(br, d)), (bfx, _rows(br, d)), (dmerged, _rows(br, d))],
                 [(_sds((r, d), MXU_DTYPE), _rows(br, d)), (_sds((r, d), MXU_DTYPE), _rows(br, d)),
                  (_sds((r, 2 * d), MXU_DTYPE), _rows(br, 2 * d)), (_sds((1, 2 * d), F32), _whole((1, 2 * d)))],
                 sem=("arbitrary",))


HALO = 16


def _conv_taps(gp, halo, first_block):
    halo = jnp.where(first_block, 0.0, halo.astype(F32))
    rid = lax.broadcasted_iota(jnp.int32, gp.shape, 0)
    last, prev = halo[HALO - 1:HALO, :], halo[HALO - 2:HALO - 1, :]
    g1 = jnp.where(rid == 0, last, pltpu.roll(gp, 1, axis=0))
    g2 = jnp.where(rid == 0, prev, jnp.where(rid == 1, last, pltpu.roll(gp, 2, axis=0)))
    return g1, g2


def _prev_halo(br, c):
    return pl.BlockSpec((HALO, c), lambda i: (jnp.maximum(i * (br // HALO) - 1, 0), 0))


def _glu_fwd(up, conv_w, conv_b):
    r = up.shape[0]
    c = D_FF
    br = ROW_BLOCK

    def body(gp_ref, halo_ref, val_ref, w_ref, b_ref, o_ref):
        gp = gp_ref[...].astype(F32)
        g1, g2 = _conv_taps(gp, halo_ref[...], pl.program_id(0) == 0)
        gate = w_ref[0:1, :] * g2 + w_ref[1:2, :] * g1 + w_ref[2:3, :] * gp + b_ref[...]
        o_ref[...] = (gate * jax.nn.sigmoid(gate) * val_ref[...].astype(F32)).astype(o_ref.dtype)

    return _call("glu_fwd", body, (r // br,),
                 [(up, _rows(br, c, 0)), (up, _prev_halo(br, c)), (up, _rows(br, c, 1)),
                  (conv_w, _whole((3, c))), (conv_b.reshape(1, c), _whole((1, c)))],
                 [(_sds((r, c), MXU_DTYPE), _rows(br, c))], sem=("parallel",))[0]


def _glu_bwd_gate(up, conv_w, conv_b, d_act):
    r = up.shape[0]
    c = D_FF
    br = ROW_BLOCK

    def body(gp_ref, halo_ref, val_ref, w_ref, b_ref, da_ref, dg_ref, dv_ref, dw_ref, db_ref):
        gp = gp_ref[...].astype(F32)
        g1, g2 = _conv_taps(gp, halo_ref[...], pl.program_id(0) == 0)
        gate = w_ref[0:1, :] * g2 + w_ref[1:2, :] * g1 + w_ref[2:3, :] * gp + b_ref[...]
        sg = jax.nn.sigmoid(gate)
        da = da_ref[...].astype(F32)
        dv_ref[...] = (da * (gate * sg)).astype(dv_ref.dtype)
        dg = da * val_ref[...].astype(F32) * (sg * (1.0 + gate * (1.0 - sg)))
        dg_ref[...] = dg.astype(dg_ref.dtype)

        @pl.when(pl.program_id(0) == 0)
        def _():
            dw_ref[...] = jnp.zeros_like(dw_ref)
            db_ref[...] = jnp.zeros_like(db_ref)

        dw_ref[0:1, :] += jnp.sum(dg * g2, axis=0, keepdims=True)
        dw_ref[1:2, :] += jnp.sum(dg * g1, axis=0, keepdims=True)
        dw_ref[2:3, :] += jnp.sum(dg * gp, axis=0, keepdims=True)
        db_ref[...] += jnp.sum(dg, axis=0, keepdims=True)

    return _call("glu_bwd_gate", body, (r // br,),
                 [(up, _rows(br, c, 0)), (up, _prev_halo(br, c)), (up, _rows(br, c, 1)),
                  (conv_w, _whole((3, c))), (conv_b.reshape(1, c), _whole((1, c))), (d_act, _rows(br, c))],
                 [(_sds((r, c), MXU_DTYPE), _rows(br, c)), (_sds((r, c), MXU_DTYPE), _rows(br, c)),
                  (_sds((8, c), F32), _whole((8, c))), (_sds((1, c), F32), _whole((1, c)))],
                 sem=("arbitrary",))


def _glu_bwd_conv(dg, dval, conv_w):
    r, c = dg.shape
    br = ROW_BLOCK
    nb = r // br

    def body(dg_ref, nxt_ref, dv_ref, w_ref, o_ref):
        x = dg_ref[...].astype(F32)
        nxt = jnp.where(pl.program_id(0) == nb - 1, 0.0, nxt_ref[...].astype(F32))
        rid = lax.broadcasted_iota(jnp.int32, x.shape, 0)
        u1 = jnp.where(rid == br - 1, nxt[0:1, :], pltpu.roll(x, br - 1, axis=0))
        u2 = jnp.where(rid == br - 1, nxt[1:2, :], jnp.where(rid == br - 2, nxt[0:1, :], pltpu.roll(x, br - 2, axis=0)))
        dgp = w_ref[2:3, :] * x + w_ref[1:2, :] * u1 + w_ref[0:1, :] * u2
        o_ref[:, 0:c] = dgp.astype(o_ref.dtype)
        o_ref[:, c:2 * c] = dv_ref[...]

    nxt_spec = pl.BlockSpec((HALO, c), lambda i: (jnp.minimum((i + 1) * (br // HALO), r // HALO - 1), 0))
    return _call("glu_bwd_conv", body, (nb,),
                 [(dg, _rows(br, c)), (dg, nxt_spec), (dval, _rows(br, c)), (conv_w, _whole((3, c)))],
                 [(_sds((r, 2 * c), MXU_DTYPE), _rows(br, 2 * c))], sem=("parallel",))[0]


def _token_specs(seq, d):
    br = ROW_BLOCK
    nxb = seq // br
    main = pl.BlockSpec((br, d), lambda i: (jnp.minimum(i, nxb - 1), 0))
    tail = pl.BlockSpec((N_META, d), lambda i: (jnp.clip(i * (br // N_META) - 1, 0, seq // N_META - 1), 0))
    return main, tail


def _padded_block(main_ref, tail_ref, first, seq):
    br = ROW_BLOCK
    i = pl.program_id(0)
    nxb = seq // br
    main = jnp.where(i < nxb, main_ref[...], 0.0)
    head = jnp.where(i == 0, first, jnp.where(i <= nxb, tail_ref[...], 0.0))
    return jnp.concatenate([head, main[:br - N_META]], axis=0)


def _ln_emb_fwd(x, meta, g, b, rows, after=()):
    seq, d = x.shape
    br = ROW_BLOCK
    assert seq % br == 0 and br % N_META == 0 and rows % br == 0

    def body(x_ref, tail_ref, meta_ref, g_ref, b_ref, y_ref, yb_ref):
        z = _padded_block(x_ref, tail_ref, meta_ref[...], seq)
        xhat, _ = _ln_stats(z)
        y = xhat * g_ref[...] + b_ref[...]
        y_ref[...] = y
        yb_ref[...] = y.astype(yb_ref.dtype)

    main, tail = _token_specs(seq, d)
    return _call("ln_emb_fwd", body, (rows // br,),
                 [(x, main), (x, tail), (meta, _whole((N_META, d))), (g.reshape(1, d), _whole((1, d))),
                  (b.reshape(1, d), _whole((1, d)))],
                 [(_sds((rows, d), F32), _rows(br, d)), (_sds((rows, d), MXU_DTYPE), _rows(br, d))],
                 sem=("parallel",), after=after)


def _ln_emb_bwd(x, meta, dh0, g):
    seq, d = x.shape
    br = ROW_BLOCK
    step = br // N_META

    def ln_bwd(z, dy, gv):
        xhat, rstd = _ln_stats(z)
        dyg = dy * gv
        m1 = jnp.mean(dyg, axis=-1, keepdims=True)
        m2 = jnp.mean(dyg * xhat, axis=-1, keepdims=True)
        dz = rstd * (dyg - m1 - xhat * m2)
        return dz, jnp.sum(dy * xhat, axis=0, keepdims=True), jnp.sum(dy, axis=0, keepdims=True)

    def body(x_ref, dh_ref, nxt_ref, meta_ref, top_ref, g_ref, dx_ref, dm_ref, dg_ref, db_ref):
        gv = g_ref[...]
        dy = jnp.concatenate([dh_ref[N_META:, :], nxt_ref[...]], axis=0)
        dz, dg, db = ln_bwd(x_ref[...], dy, gv)
        dx_ref[...] = dz

        @pl.when(pl.program_id(0) == 0)
        def _():
            dzm, dgm, dbm = ln_bwd(meta_ref[...], top_ref[...], gv)
            dm_ref[...] = dzm
            dg_ref[...] = dgm
            db_ref[...] = dbm

        dg_ref[...] += dg
        db_ref[...] += db

    small = _whole((N_META, d))
    return _call("ln_emb_bwd", body, (seq // br,),
                 [(x, _rows(br, d)), (dh0, _rows(br, d)), (dh0, pl.BlockSpec((N_META, d), lambda i: ((i + 1) * step, 0))),
                  (meta, small), (dh0, small), (g.reshape(1, d), _whole((1, d)))],
                 [(_sds((seq, d), F32), _rows(br, d)), (_sds((N_META, d), F32), small),
                  (_sds((1, d), F32), _whole((1, d))), (_sds((1, d), F32), _whole((1, d)))], sem=("arbitrary",))


def _ln_ffn_loss(h1, f, tgt, g, b):
    r, d = h1.shape
    seq = tgt.shape[0]
    br = ROW_BLOCK

    def body(a_ref, r_ref, t_ref, tail_ref, g_ref, b_ref, l_ref):
        err = _loss_err(a_ref, r_ref, t_ref, tail_ref, g_ref, b_ref, seq)[0]

        @pl.when(pl.program_id(0) == 0)
        def _():
            l_ref[...] = jnp.zeros_like(l_ref)

        l_ref[...] += jnp.sum(jnp.sum(err * err, axis=1, keepdims=True), axis=0, keepdims=True) * (0.5 / d)

    main, tail = _token_specs(seq, d)
    return _call("ln_ffn_loss", body, (r // br,),
                 [(h1, _rows(br, d)), (f, _rows(br, d)), (tgt, main), (tgt, tail),
                  (g.reshape(1, d), _whole((1, d))), (b.reshape(1, d), _whole((1, d)))],
                 [(_sds((1, 1), F32), _whole((1, 1)))], sem=("arbitrary",))[0]


def _loss_err(a_ref, r_ref, t_ref, tail_ref, g_ref, b_ref, seq):
    br, d = a_ref.shape
    xhat, rstd = _ln_stats(ALPHA * a_ref[...] + r_ref[...])
    y = xhat * g_ref[...] + b_ref[...]
    t = _padded_block(t_ref, tail_ref, jnp.zeros((N_META, d), F32), seq)
    rid = lax.broadcasted_iota(jnp.int32, (br, d), 0) + pl.program_id(0) * br
    valid = (rid >= N_META) & (rid < N_META + seq)
    return jnp.where(valid, y - t, 0.0), xhat, rstd


def _ln_ffn_bwd(h1, f, tgt, g, b):
    r, d = h1.shape
    seq = tgt.shape[0]
    br = ROW_BLOCK

    def body(a_ref, r_ref, t_ref, tail_ref, g_ref, b_ref, dz_ref, dzb_ref, dg_ref, db_ref):
        err, xhat, rstd = _loss_err(a_ref, r_ref, t_ref, tail_ref, g_ref, b_ref, seq)
        dyv = err * (1.0 / d)
        dyg = dyv * g_ref[...]
        m1 = jnp.mean(dyg, axis=-1, keepdims=True)
        m2 = jnp.mean(dyg * xhat, axis=-1, keepdims=True)
        dz = rstd * (dyg - m1 - xhat * m2)
        dz_ref[...] = dz
        dzb_ref[...] = dz.astype(dzb_ref.dtype)

        @pl.when(pl.program_id(0) == 0)
        def _():
            dg_ref[...] = jnp.zeros_like(dg_ref)
            db_ref[...] = jnp.zeros_like(db_ref)

        dg_ref[...] += jnp.sum(dyv * xhat, axis=0, keepdims=True)
        db_ref[...] += jnp.sum(dyv, axis=0, keepdims=True)

    main, tail = _token_specs(seq, d)
    return _call("ln_ffn_bwd", body, (r // br,),
                 [(h1, _rows(br, d)), (f, _rows(br, d)), (tgt, main), (tgt, tail),
                  (g.reshape(1, d), _whole((1, d))), (b.reshape(1, d), _whole((1, d)))],
                 [(_sds((r, d), F32), _rows(br, d)), (_sds((r, d), MXU_DTYPE), _rows(br, d)),
                  (_sds((1, d), F32), _whole((1, d))), (_sds((1, d), F32), _whole((1, d)))], sem=("arbitrary",))


def _attn_fwd(name, q, k, v, cum_b=None, cum_t=None):
    (qa, qg), (ka, kg), (va, vg) = q, k, v
    r = qa.shape[0]
    tq, tk = ATT_TQ, ATT_TK
    nq, nk = r // tq, r // tk
    bias = cum_b is not None

    def body(*refs):
        if bias:
            q_ref, k_ref, vt_ref, cb_ref, ct_ref, o_ref, ob_ref, lse_ref = refs
        else:
            q_ref, k_ref, vt_ref, o_ref, ob_ref, lse_ref = refs
        i = pl.program_id(1)
        qs = [q_ref[:, _hs(hh)] for hh in range(hg)]
        cqs = [ct_ref[hh] for hh in range(hg)] if bias else None
        diff = lax.broadcasted_iota(jnp.int32, (tk, tq), 0) - lax.broadcasted_iota(jnp.int32, (tk, tq), 1)

        def step(j, carry, masked):
            keys = pl.ds(pl.multiple_of(j * tk, tk), tk)
            out = []
            for hh in range(hg):
                m, l, acc = carry[hh]
                kt = k_ref[keys, _hs(hh)]
                s = lax.dot_general(kt, qs[hh], NT, preferred_element_type=F32)
                if bias:
                    s = s + (cqs[hh] - cb_ref[keys, hh * HP:hh * HP + 1])
                if masked:
                    s = jnp.where(diff <= i * tq - j * tk, s, NEG_INF)
                m_new = jnp.maximum(m, jnp.max(s, axis=0, keepdims=True))
                p = jnp.exp(s - m_new)
                a = jnp.exp(m - m_new)
                l = a * l + jnp.sum(p, axis=0, keepdims=True)
                acc = a * acc + jnp.dot(vt_ref[j, _hs(hh), :], p.astype(kt.dtype), preferred_element_type=F32)
                out.append((m_new, l, acc))
            return tuple(out)

        n_clear = (i * tq + 1) // tk
        n_all = ((i + 1) * tq - 1) // tk + 1
        carry = tuple((jnp.full((1, tq), NEG_INF, F32), jnp.zeros((1, tq), F32), jnp.zeros((HP, tq), F32))
                      for _ in range(hg))
        carry = lax.fori_loop(0, n_clear, lambda j, c: step(j, c, False), carry)
        carry = lax.fori_loop(n_clear, n_all, lambda j, c: step(j, c, True), carry)
        for hh in range(hg):
            m, l, acc = carry[hh]
            o = (acc / l).T
            o_ref[:, _hs(hh)] = o
            ob_ref[:, _hs(hh)] = o.astype(ob_ref.dtype)
            lse_ref[hh] = m + jnp.log(l)

    hg = ATT_HEADS
    w = hg * HP
    gpw = HW // w
    tile = lambda g: pl.BlockSpec((tq, w), lambda h, i: (i, g * gpw + h))
    res = lambda g: pl.BlockSpec((r, w), lambda h, i: (0, g * gpw + h))
    v_t = _key_tiles_transposed(name + "_vt", va, vg)
    ins = [(qa, tile(qg)), (ka, res(kg)), (v_t, pl.BlockSpec((nk, w, tk), lambda h, i: (0, h, 0)))]
    if bias:
        ins += [(cum_b, res(0)),
                (cum_t.reshape(HEADS, nq, 1, tq), pl.BlockSpec((hg, None, 1, tq), lambda h, i: (h, i, 0, 0)))]
    outs = [(_sds((r, HW), F32), tile(0)), (_sds((r, HW), MXU_DTYPE), tile(0)),
            (_sds((HEADS, nq, 1, tq), F32), pl.BlockSpec((hg, None, 1, tq), lambda h, i: (h, i, 0, 0)))]
    o, ob, lse = _call(name, body, (gpw, nq), ins, outs, sem=("parallel", "parallel"))
    return o, ob, lse.reshape(HEADS, r)


def _key_tiles_transposed(name, a, group):
    r = a.shape[0]
    tk = ATT_TK

    def body(x_ref, o_ref):
        for h in range(HEADS):
            o_ref[_hs(h), :] = x_ref[:, _hs(h)].astype(F32).T.astype(o_ref.dtype)

    return _call(name, body, (r // tk,),
                 [(a, pl.BlockSpec((tk, HW), lambda j: (j, group)))],
                 [(_sds((r // tk, HW, tk), a.dtype), pl.BlockSpec((None, HW, tk), lambda j: (j, 0, 0)))],
                 sem=("parallel",))[0]


def _attn_delta(name, do_b, o, after=()):
    r = do_b.shape[0]
    br = ROW_BLOCK

    def body(do_ref, o_ref, d_ref):
        lane = _lane_iota((br, HP))
        d = jnp.zeros((br, HP), F32)
        for h in range(HEADS):
            dh = do_ref[:, _hs(h)].astype(F32)
            d = jnp.where(lane == h, jnp.sum(dh * o_ref[:, _hs(h)], axis=1, keepdims=True), d)
        d_ref[...] = d.T[0:HEADS, :]

    wide = _rows(br, HW)
    return _call(name, body, (r // br,), [(do_b, wide), (o, wide)],
                 [(_sds((HEADS, r), F32), pl.BlockSpec((HEADS, br), lambda i: (0, i)))],
                 sem=("parallel",), after=after)[0]


def _attn_bwd(name, q, k, v, do_b, lse_t, delta_t, cum_b=None, cum_t=None, out_dtype=F32):
    (qa, qg), (ka, kg), (va, vg) = q, k, v
    r = qa.shape[0]
    tq, tk = ATT_TQ, ATT_TK
    nq, nk = r // tq, r // tk
    bias = cum_b is not None

    def body(*refs):
        if bias:
            (q_ref, k_ref, v_ref, do_ref, lse_ref, dl_ref, cb_ref, ct_ref,
             dq_ref, dk_ref, dv_ref, dcq_ref, dck_ref, dqt_ref) = refs
        else:
            q_ref, k_ref, v_ref, do_ref, lse_ref, dl_ref, dq_ref, dk_ref, dv_ref, dqt_ref = refs
        j = pl.program_id(1)

        @pl.when(j == 0)
        def _():
            dqt_ref[...] = jnp.zeros_like(dqt_ref)
            if bias:
                dcq_ref[...] = jnp.zeros_like(dcq_ref)

        kts = [k_ref[:, _hs(hh)] for hh in range(hg)]
        vts = [v_ref[:, _hs(hh)] for hh in range(hg)]
        k_trs = [kt.astype(F32).T.astype(kt.dtype) for kt in kts]
        cks = [cb_ref[:, hh * HP:hh * HP + 1] for hh in range(hg)] if bias else None
        diff = lax.broadcasted_iota(jnp.int32, (tk, tq), 0) - lax.broadcasted_iota(jnp.int32, (tk, tq), 1)

        def step(i, carry, masked):
            rows = pl.ds(pl.multiple_of(i * tq, tq), tq)
            out = []
            for hh in range(hg):
                dk_acc, dv_acc, dck_acc = carry[hh]
                qt = q_ref[rows, _hs(hh)]
                dot = do_ref[rows, _hs(hh)]
                s = lax.dot_general(kts[hh], qt, NT, preferred_element_type=F32)
                if bias:
                    s = s + (ct_ref[hh, i] - cks[hh])
                if masked:
                    s = jnp.where(diff <= i * tq - j * tk, s, NEG_INF)
                p = jnp.exp(s - lse_ref[hh, i])
                dp = lax.dot_general(vts[hh], dot, NT, preferred_element_type=F32)
                ds = p * (dp - dl_ref[hh, i])
                pb = p.astype(dot.dtype)
                dsb = ds.astype(qt.dtype)
                dv_acc = dv_acc + jnp.dot(pb, dot, preferred_element_type=F32)
                dk_acc = dk_acc + jnp.dot(dsb, qt, preferred_element_type=F32)
                dqt_ref[hh, i] += jnp.dot(k_trs[hh], dsb, preferred_element_type=F32)
                if bias:
                    dcq_ref[hh, i] += jnp.sum(ds, axis=0, keepdims=True)
                    dck_acc = dck_acc - jnp.sum(ds, axis=1, keepdims=True)
                out.append((dk_acc, dv_acc, dck_acc))
            return tuple(out)

        i_first = (j * tk) // tq
        i_clear = jnp.minimum(((j + 1) * tk + tq - 2) // tq, nq)
        carry = tuple((jnp.zeros((tk, HP), F32), jnp.zeros((tk, HP), F32), jnp.zeros((tk, 1), F32)) for _ in range(hg))
        carry = lax.fori_loop(i_first, i_clear, lambda i, c: step(i, c, True), carry)
        carry = lax.fori_loop(i_clear, nq, lambda i, c: step(i, c, False), carry)
        for hh in range(hg):
            dk_acc, dv_acc, dck_acc = carry[hh]
            dk_ref[:, _hs(hh)] = dk_acc.astype(dk_ref.dtype)
            dv_ref[:, _hs(hh)] = dv_acc.astype(dv_ref.dtype)
            if bias:
                dck_ref[:, _hs(hh)] = jnp.broadcast_to(dck_acc, (tk, HP))

        @pl.when(j == nk - 1)
        def _():
            for hh in range(hg):
                for i in range(nq):
                    dq_ref[i * tq:(i + 1) * tq, _hs(hh)] = dqt_ref[hh, i].T.astype(dq_ref.dtype)

    hg = ATT_HEADS
    w = hg * HP
    gpw = HW // w
    res = lambda g: pl.BlockSpec((r, w), lambda h, j: (0, g * gpw + h))
    tile = lambda g: pl.BlockSpec((tk, w), lambda h, j: (j, g * gpw + h))
    rowv = pl.BlockSpec((hg, nq, 1, tq), lambda h, j: (h, 0, 0, 0))
    as_rows = lambda a: a.reshape(HEADS, nq, 1, tq)
    ins = [(qa, res(qg)), (ka, tile(kg)), (va, tile(vg)), (do_b, res(0)), (as_rows(lse_t), rowv), (as_rows(delta_t), rowv)]
    outs = [(_sds((r, HW), out_dtype), res(0)), (_sds((r, HW), out_dtype), tile(0)), (_sds((r, HW), out_dtype), tile(0))]
    if bias:
        ins += [(cum_b, tile(0)), (as_rows(cum_t), rowv)]
        outs += [(_sds((HEADS, nq, 1, tq), F32), rowv), (_sds((r, HW), F32), tile(0))]
    res_out = _call(name, body, (gpw, nk), ins, outs, scratch=[pltpu.VMEM((hg, nq, HP, tq), F32)],
                    sem=("parallel", "arbitrary"))
    if bias:
        dq, dk, dv, dcq, dck = res_out
        return dq, dk, dv, dcq.reshape(HEADS, r), dck
    return res_out


MESH_ID = pl.DeviceIdType.MESH
ANY = pl.BlockSpec(memory_space=pl.ANY)


def _allgather(name, shards):
    n = len(shards)

    def body(*refs):
        x_refs, out_refs = refs[:n], refs[n:2 * n]
        send_sems, recv_sems, local_sems = refs[2 * n:]
        x, y, c = lax.axis_index("x"), lax.axis_index("y"), lax.axis_index("c")
        me, sibling = (x, y, c), (x, y, 1 - c)
        chips = [(1 - x, y), (x, 1 - y), (1 - x, 1 - y)]

        def slot(ti, px, py, pc):
            return out_refs[ti].at[4 * px + 2 * py + pc]

        def copy(ti, k, block, to, src=None):
            return pltpu.make_async_remote_copy(
                src_ref=slot(ti, *block) if src is None else src, dst_ref=slot(ti, *block),
                send_sem=send_sems.at[ti, k], recv_sem=recv_sems.at[ti, k], device_id=to, device_id_type=MESH_ID)

        mine = [pltpu.make_async_copy(x_refs[ti], slot(ti, *me), local_sems.at[ti]) for ti in range(n)]
        for cp in mine:
            cp.start()
        started = []
        for ti in range(n):
            first = [copy(ti, 0, me, sibling, src=x_refs[ti])]
            first += [copy(ti, 1 + j, me, (*chip, c), src=x_refs[ti]) for j, chip in enumerate(chips)]
            for cp in first:
                cp.start()
            started += first
        for ti in range(n):
            for j, chip in enumerate(chips):
                copy(ti, 1 + j, (*chip, c), me).wait_recv()
                fwd = copy(ti, 4 + j, (*chip, c), sibling)
                fwd.start()
                started.append(fwd)
        for ti in range(n):
            copy(ti, 0, sibling, me).wait_recv()
            for j, chip in enumerate(chips):
                copy(ti, 4 + j, (*chip, 1 - c), me).wait_recv()
        for cp in started:
            cp.wait_send()
        for cp in mine:
            cp.wait()

    return pl.pallas_call(
        body, name=name, out_shape=[_sds((N_DEV,) + s.shape, s.dtype) for s in shards],
        in_specs=[ANY] * n, out_specs=[ANY] * n,
        scratch_shapes=[pltpu.SemaphoreType.DMA((n, 7)), pltpu.SemaphoreType.DMA((n, 7)), pltpu.SemaphoreType.DMA((n,))],
    )(*shards)


HBM = pl.BlockSpec(memory_space=pltpu.HBM)
SEM = pl.BlockSpec(memory_space=pltpu.SEMAPHORE)
EFFECT = pltpu.SideEffectType.DATAFLOW_SIDE_EFFECTING
N_PEER = N_DEV - 1


def _my_id():
    return 4 * lax.axis_index("x") + 2 * lax.axis_index("y") + lax.axis_index("c")


def _peers():
    x, y, c = lax.axis_index("x"), lax.axis_index("y"), lax.axis_index("c")
    out = []
    for k in range(1, N_DEV):
        px, py, pc = (1 - x if k & 4 else x, 1 - y if k & 2 else y, 1 - c if k & 1 else c)
        out.append(((px, py, pc), 4 * px + 2 * py + pc))
    return out


def _push_copies(src_refs, land_refs, send_sems, recv_sems, scatter, landing):
    me = _my_id()
    out = []
    for ti, (src, land) in enumerate(zip(src_refs, land_refs)):
        for k, (dev, pid) in enumerate(_peers()):
            out.append(pltpu.make_async_remote_copy(
                src_ref=src.at[pid] if scatter else src, dst_ref=land.at[pid if landing else me],
                send_sem=send_sems.at[ti * N_PEER + k], recv_sem=recv_sems.at[ti * N_PEER + k],
                device_id=dev, device_id_type=MESH_ID))
    return out


def _push_start(name, srcs, scatter, after=None):
    n = len(srcs)
    slot = lambda s: s.shape[1:] if scatter else s.shape
    lands = [lax.empty((N_DEV,) + slot(s), s.dtype) for s in srcs]
    n_after = 0 if after is None else 1

    def body(*refs):
        src_refs, land_refs = refs[:n], refs[n:2 * n]
        send_sems, recv_sems = refs[2 * n + n_after], refs[2 * n + n_after + 1]
        token = refs[-1]
        for cp in _push_copies(src_refs, land_refs, send_sems, recv_sems, scatter, False):
            cp.start()
        token[...] = jnp.zeros_like(token)

    hbm = lambda a: pltpu.with_memory_space_constraint(a, pltpu.HBM)
    operands = [hbm(a) for a in srcs + lands] + ([after] if n_after else [])
    res = pl.pallas_call(
        body, name=name,
        out_shape=[pltpu.SemaphoreType.DMA((n * N_PEER,)), pltpu.SemaphoreType.DMA((n * N_PEER,))]
        + [pltpu.HBM(a.shape, a.dtype) for a in srcs + lands] + [_sds((8, 128), F32)],
        in_specs=[HBM] * (2 * n) + [ANY] * n_after,
        out_specs=[SEM, SEM] + [HBM] * (2 * n) + [pl.BlockSpec(memory_space=pltpu.VMEM)],
        input_output_aliases={i: 2 + i for i in range(2 * n)},
        compiler_params=pltpu.CompilerParams(has_side_effects=EFFECT),
    )(*operands)
    return (res[0], res[1], list(res[2:2 + n]), list(res[2 + n:2 + 2 * n]), scatter), res[-1]


def _push_wait(name, handle, after):
    send_sems, recv_sems, srcs, lands, scatter = handle
    n = len(srcs)

    def body(*refs):
        src_refs, land_refs = refs[:n], refs[n:2 * n]
        s_sems, r_sems = refs[2 * n], refs[2 * n + 1]
        for cp in _push_copies(src_refs, land_refs, s_sems, r_sems, scatter, True):
            cp.wait_send()
            cp.wait_recv()

    res = pl.pallas_call(
        body, name=name,
        out_shape=[pltpu.HBM(a.shape, a.dtype) for a in srcs + lands],
        in_specs=[HBM] * (2 * n) + [SEM, SEM, ANY], out_specs=[HBM] * (2 * n),
        input_output_aliases={i: i for i in range(2 * n)},
        compiler_params=pltpu.CompilerParams(has_side_effects=EFFECT),
    )(*srcs, *lands, send_sems, recv_sems, after)
    return list(res[n:])


def _adamw(name, parts, w, m, v, own=None):
    r, c = w.shape
    br = _pick(r, 256, 16)
    has_own = own is not None

    def body(*refs):
        if has_own:
            p_ref, own_ref, w_ref, m_ref, v_ref, g_ref, d_ref, nm_ref, nv_ref = refs
            me = _my_id()
            mine = own_ref[...].astype(F32)
        else:
            p_ref, w_ref, m_ref, v_ref, g_ref, d_ref, nm_ref, nv_ref = refs
        g = None
        for k in range(N_DEV):
            t = p_ref[k].astype(F32)
            if has_own:
                t = jnp.where(me == k, mine, t)
            g = t if g is None else g + t
        mm = ADAM_B1 * m_ref[...] + (1.0 - ADAM_B1) * g
        vv = ADAM_B2 * v_ref[...] + (1.0 - ADAM_B2) * (g * g)
        m_hat = mm / (1.0 - ADAM_B1 ** ADAM_STEP)
        v_hat = vv / (1.0 - ADAM_B2 ** ADAM_STEP)
        g_ref[...] = g
        d_ref[...] = -ADAM_LR * (m_hat / (jnp.sqrt(v_hat) + ADAM_EPS) + ADAM_WD * w_ref[...])
        nm_ref[...] = mm
        nv_ref[...] = vv

    spec = _rows(br, c)
    out = (_sds((r, c), F32), spec)
    ins = [(parts, pl.BlockSpec((N_DEV, br, c), lambda i: (0, i, 0)))] + ([(own, spec)] if has_own else [])
    return _call(name, body, (r // br,), ins + [(w, spec), (m, spec), (v, spec)], [out] * 4, sem=("parallel",))


def _pad_head_cols(w, d):
    k = w.shape[0]
    return jnp.pad(w.reshape(k, HEADS, d), ((0, 0), (0, 0), (0, HP - d))).reshape(k, HW)


def _unpad_head_cols(wp, d):
    k = wp.shape[0]
    return wp.reshape(k, HEADS, HP)[:, :, :d].reshape(k, HEADS * d)


def _pad_head_rows(w, d):
    n = w.shape[1]
    return jnp.pad(w.reshape(HEADS, d, n), ((0, 0), (0, HP - d), (0, 0))).reshape(HW, n)


def _unpad_head_rows(wp, d):
    n = wp.shape[1]
    return wp.reshape(HEADS, HP, n)[:, :d, :].reshape(HEADS * d, n)


def _w_in_runs():
    nat = {}
    o = 0
    for nm, wd in (("q", Q_RANK), ("kv", KV_RANK), ("kr", ROPE), ("fq", FOX_W), ("fk", FOX_W), ("fv", FOX_W),
                   ("fl", HEADS), ("gate", 2 * D_MODEL)):
        nat[nm] = o
        o += wd
    runs = [(1, R_QLAT, nat["q"], Q_RANK, 1.0), (1, R_KVLAT, nat["kv"], KV_RANK, 1.0),
            (1, R_LAST + LANE_FL, nat["fl"], HEADS, 1.0), (1, R_LAST + LANE_PE, nat["kr"], ROPE, 1.0),
            (1, R_GATE, nat["gate"], 2 * D_MODEL, 1.0)]
    for grp, (nm, sc) in enumerate((("fq", FOX_SCALE), ("fk", 1.0), ("fv", 1.0))):
        runs += [(0, grp * HW + h * HP, nat[nm] + h * FOX_DIM, FOX_DIM, sc) for h in range(HEADS)]
    return runs


def _sharded_runs(runs, shard_cols):
    out = []
    for half, col, ncol, width, sc in runs:
        while width > 0:
            d, local = divmod(ncol, shard_cols)
            wd = min(width, shard_cols - local)
            out.append((half, col, d, local, wd, sc))
            col, ncol, width = col + wd, ncol + wd, width - wd
    return out


def _remap(name, srcs, out_shapes, moves):
    rows = srcs[0].shape[-2]
    br = _pick(rows, 256, 16)
    ns = len(srcs)

    def spec(shape):
        if len(shape) == 2:
            return pl.BlockSpec((br, shape[1]), lambda i: (i, 0))
        return pl.BlockSpec((shape[0], br, shape[2]), lambda i: (0, i, 0))

    def body(*refs):
        s_refs, o_refs = refs[:ns], refs[ns:]
        for o in o_refs:
            o[...] = jnp.zeros_like(o)
        for di, dl, dc, si, sl, sc0, wd, scale in moves:
            v = s_refs[si][:, sc0:sc0 + wd] if sl is None else s_refs[si][sl, :, sc0:sc0 + wd]
            if scale != 1.0:
                v = v * jnp.asarray(scale, v.dtype)
            v = v.astype(o_refs[di].dtype)
            if dl is None:
                o_refs[di][:, dc:dc + wd] = v
            else:
                o_refs[di][dl, :, dc:dc + wd] = v

    return _call(name, body, (rows // br,), [(a, spec(a.shape)) for a in srcs],
                 [(_sds(shape, dt), spec(shape)) for shape, dt in out_shapes], sem=("parallel",))


def _w_in_from_shards(g3):
    n, rows, c = g3.shape
    moves = [(half, None, col, 0, d, local, wd, sc) for half, col, d, local, wd, sc in _sharded_runs(_w_in_runs(), c)]
    return _remap("w_in_repack", [g3], [((rows, F_W), g3.dtype), ((rows, R_W), g3.dtype)], moves)


def _w_in_grad_to_shards(d_fused, d_rest, n, c):
    rows = d_fused.shape[0]
    moves = [(0, d, local, half, None, col, wd, sc) for half, col, d, local, wd, sc in _sharded_runs(_w_in_runs(), c)]
    return _remap("w_in_grad_unpack", [d_fused, d_rest], [((n, rows, c), d_fused.dtype)], moves)[0]


def _cols_from_shards(name, g3):
    n, rows, c = g3.shape
    return _remap(name, [g3], [((rows, n * c), g3.dtype)], [(0, None, c * d, 0, d, 0, c, 1.0) for d in range(n)])[0]


def _cols_to_shards(name, full, n):
    rows, nc = full.shape
    c = nc // n
    return _remap(name, [full], [((n, rows, c), full.dtype)], [(0, d, 0, 0, None, c * d, c, 1.0) for d in range(n)])[0]


def _split_w_kv(w):
    k = w.shape[0]
    w3 = w.reshape(k, HEADS, NOPE + V_DIM)
    padl = lambda a: jnp.pad(a, ((0, 0), (0, 0), (0, HP - a.shape[-1]))).reshape(k, HW)
    return padl(w3[..., :NOPE]), padl(w3[..., NOPE:])


def _merge_w_kv(wk, wv):
    k = wk.shape[0]
    return jnp.concatenate([wk.reshape(k, HEADS, HP)[..., :NOPE], wv.reshape(k, HEADS, HP)[..., :V_DIM]],
                           axis=-1).reshape(k, HEADS * (NOPE + V_DIM))


class _NoComm:
    first_token = ()

    def late_weights(self, group, after):
        return {}

    def send(self, name, grads):
        return ()


def _local_step(x, tgt, p, comm=_NoComm()):
    seq = x.shape[0]
    r = -(-(N_META + seq) // ROW_ALIGN) * ROW_ALIGN
    cd = MXU_DTYPE
    p = dict(p)

    w_f, w_r = p["w_in"]
    w_q = _pad_head_cols(p["w_q_up"], QK_DIM)
    w_k, w_v = _split_w_kv(p["w_kv_up"])

    pos = jnp.arange(r, dtype=F32)
    inv_freq = ROPE_THETA ** (-jnp.arange(HALF, dtype=F32) / HALF)
    ang = pos[:, None] * inv_freq[None, :]
    cos_t = jnp.tile(jnp.cos(ang), (1, HP // HALF))
    sin_t = jnp.tile(jnp.sin(ang), (1, HP // HALF))
    bf_row = jnp.zeros((1, HP), F32).at[0, LANE_FL:LANE_FL + HEADS].set(p["b_forget"])

    h0, h0b = _ln_emb_fwd(x, p["meta_tokens"], p["ln_emb_g"], p["ln_emb_b"], r, after=comm.first_token)
    proj_f = _matmul("in_proj_f", h0b, w_f, out_dtype=cd)
    proj_r = _matmul("in_proj_r", h0b, w_r)
    ql = _rms_fwd("q_norm_fwd", proj_r, R_QLAT // Q_RANK, Q_RANK, p["q_norm_g"])
    kvl = _rms_fwd("kv_norm_fwd", proj_r, R_KVLAT // KV_RANK, KV_RANK, p["kv_norm_g"])
    q_raw = _matmul("q_up", ql, w_q)
    k_part = _matmul("k_up", kvl, w_k)
    v_mla = _matmul("v_up", kvl, w_v, out_dtype=cd)
    q_mla, k_mla = _rope_fwd(q_raw, k_part, proj_r, cos_t, sin_t)
    o_mla, o_mla_b, lse_mla = _attn_fwd("mla_fwd", (q_mla, 0), (k_mla, 0), (v_mla, 0))

    cum, cum_t = _forget_fwd(proj_r, bf_row)
    o_fox, o_fox_b, lse_fox = _attn_fwd("fox_fwd", (proj_f, 0), (proj_f, 1), (proj_f, 2), cum, cum_t)

    p.update(comm.late_weights("mix", o_fox_b))
    w_bm = _pad_head_rows(p["w_branch_mla"], V_DIM)
    w_bf = _pad_head_rows(p["w_branch_fox"], FOX_DIM)
    bm = _matmul("branch_mla", o_mla_b, w_bm, out_dtype=cd)
    bfx = _matmul("branch_fox", o_fox_b, w_bf, out_dtype=cd)
    merged = _gate_fwd(proj_r, p["b_gate"], bm, bfx)
    mix = _matmul("out_proj", merged, p["w_out"])
    h1, h1b = _ln_fwd("ln_mix_fwd", h0, mix, p["ln_mix_g"], p["ln_mix_b"])
    p.update(comm.late_weights("ffn", h1b))
    up = _matmul("ffn_up", h1b, p["w_ffn_up"], out_dtype=cd)
    act = _glu_fwd(up, p["conv_w"], p["conv_b"])
    f = _matmul("ffn_down", act, p["w_ffn_down"])
    loss = _ln_ffn_loss(h1, f, tgt, p["ln_ffn_g"], p["ln_ffn_b"])

    g = {}
    dz2, dz2b, g["ln_ffn_g"], g["ln_ffn_b"] = _ln_ffn_bwd(h1, f, tgt, p["ln_ffn_g"], p["ln_ffn_b"])
    d_act = _matmul("ffn_down_dx", dz2b, p["w_ffn_down"], tb=True, out_dtype=cd)
    g["w_ffn_down"] = _matmul("ffn_down_dw", act, dz2b, ta=True, out_dtype=cd)
    dgate, dval, dcw, g["conv_b"] = _glu_bwd_gate(up, p["conv_w"], p["conv_b"], d_act)
    g["conv_w"] = dcw[:3]
    d_up = _glu_bwd_conv(dgate, dval, p["conv_w"])
    dh1 = _matmul("ffn_up_dx", d_up, p["w_ffn_up"], tb=True, addend=dz2, alpha=ALPHA)
    g["w_ffn_up"] = _matmul("ffn_up_dw", h1b, d_up, ta=True, out_dtype=cd)
    sent = comm.send("ffn", {n: g[n] for n in ("w_ffn_down", "w_ffn_up", "conv_w")})
    dz1, dz1b, g["ln_mix_g"], g["ln_mix_b"] = _ln_bwd("ln_mix_bwd", h0, mix, dh1, p["ln_mix_g"], after=sent)
    dmerged = _matmul("out_proj_dx", dz1b, p["w_out"], tb=True, out_dtype=cd)
    g["w_out"] = _matmul("out_proj_dw", merged, dz1b, ta=True, out_dtype=cd)
    d_bm, d_bf, d_gl, g["b_gate"] = _gate_bwd(proj_r, p["b_gate"], bm, bfx, dmerged)
    do_mla_b = _matmul("branch_mla_dx", d_bm, w_bm, tb=True, out_dtype=cd)
    g["w_branch_mla"] = _unpad_head_rows(_matmul("branch_mla_dw", o_mla_b, d_bm, ta=True, out_dtype=cd), V_DIM)
    do_fox_b = _matmul("branch_fox_dx", d_bf, w_bf, tb=True, out_dtype=cd)
    g["w_branch_fox"] = _unpad_head_rows(_matmul("branch_fox_dw", o_fox_b, d_bf, ta=True, out_dtype=cd), FOX_DIM)

    sent = comm.send("mix", {n: g[n] for n in ("w_out", "w_branch_mla", "w_branch_fox")})
    dl_mla = _attn_delta("mla_delta", do_mla_b, o_mla, after=sent)
    dq_m, dk_m, dv_m = _attn_bwd("mla_bwd", (q_mla, 0), (k_mla, 0), (v_mla, 0), do_mla_b, lse_mla, dl_mla)
    dl_fox = _attn_delta("fox_delta", do_fox_b, o_fox)
    dfq, dfk, dfv, dcq, dck = _attn_bwd("fox_bwd", (proj_f, 0), (proj_f, 1), (proj_f, 2), do_fox_b, lse_fox, dl_fox,
                                        cum, cum_t, out_dtype=cd)
    dfl, dbf = _forget_bwd(proj_r, bf_row, dcq, dck)
    g["b_forget"] = dbf[:, LANE_FL:LANE_FL + HEADS]

    dq_b, dk_b, dlast = _rope_bwd(dq_m, dk_m, dfl, cos_t, sin_t)
    dv_b = dv_m.astype(cd)
    d_ql = _matmul("q_up_dx", dq_b, w_q, tb=True)
    d_kvl = _matmul("k_up_dx", dk_b, w_k, tb=True)
    d_kvl = _matmul("v_up_dx", dv_b, w_v, tb=True, addend=d_kvl)
    d_qlat, g["q_norm_g"] = _rms_bwd("q_norm_bwd", proj_r, R_QLAT // Q_RANK, Q_RANK, d_ql, p["q_norm_g"])
    d_kvlat, g["kv_norm_g"] = _rms_bwd("kv_norm_bwd", proj_r, R_KVLAT // KV_RANK, KV_RANK, d_kvl, p["kv_norm_g"])
    dproj_f = jnp.concatenate([dfq, dfk, dfv], axis=1)
    dproj_r = jnp.concatenate([d_qlat, d_kvlat, dlast, jnp.zeros((r, R_GATE - R_LAST - HP), cd), d_gl], axis=1)
    g["w_in"] = (_matmul("in_proj_f_dw", h0b, dproj_f, ta=True, out_dtype=cd),
                 _matmul("in_proj_r_dw", h0b, dproj_r, ta=True, out_dtype=cd))
    sent = comm.send("in", {"w_in": g["w_in"]})
    dh0 = _matmul("in_proj_f_dx", dproj_f, w_f, tb=True, addend=dz1, alpha=ALPHA, after=sent)
    g["w_q_up"] = _unpad_head_cols(_matmul("q_up_dw", ql, dq_b, ta=True, out_dtype=cd, after=sent), QK_DIM)
    g["w_kv_up"] = _merge_w_kv(_matmul("k_up_dw", kvl, dk_b, ta=True, out_dtype=cd, after=sent),
                               _matmul("v_up_dw", kvl, dv_b, ta=True, out_dtype=cd, after=sent))
    sent = comm.send("qkv", {n: g[n] for n in ("w_q_up", "w_kv_up")})
    dh0 = _matmul("in_proj_r_dx", dproj_r, w_r, tb=True, addend=dh0, after=sent)
    grad_x, d_meta, g["ln_emb_g"], g["ln_emb_b"] = _ln_emb_bwd(x, p["meta_tokens"], dh0, p["ln_emb_g"])
    return loss, grad_x, d_meta, g


BIG = (("w_in", 1), ("w_q_up", 1), ("w_kv_up", 1), ("w_branch_mla", 1), ("w_branch_fox", 1), ("w_out", 0),
       ("w_ffn_up", 1), ("w_ffn_down", 0))
SMALL_SHARDED = (("meta_tokens", 1), ("conv_w", 1))
EARLY = ("w_in", "w_q_up", "w_kv_up", "meta_tokens", "conv_w")
LATE = {"mix": ("w_branch_mla", "w_branch_fox", "w_out"),
        "ffn": ("w_ffn_up", "w_ffn_down")}
REPLICATED = ("ln_emb_g", "ln_emb_b", "b_gate", "b_forget", "q_norm_g", "kv_norm_g", "ln_mix_g", "ln_mix_b",
              "conv_b", "ln_ffn_g", "ln_ffn_b")
PACK_COLS = 1024


def _pack(flat_list):
    cat = jnp.concatenate(flat_list)
    n = cat.shape[0]
    rows = -(-n // (8 * PACK_COLS)) * 8
    return jnp.pad(cat, (0, rows * PACK_COLS - n)).reshape(rows, PACK_COLS)


def _gathered_full(g3, axis):
    n, r, c = g3.shape
    if axis == 0:
        return g3.reshape(n * r, c)
    return g3.transpose(1, 0, 2).reshape(r, n * c)


def _shard_major(full, axis):
    r, c = full.shape
    if axis == 0:
        return full.reshape(N_DEV, r // N_DEV, c)
    return full.reshape(r, N_DEV, c // N_DEV).transpose(1, 0, 2)


def kernel(x, meta_tokens, ln_emb_g, ln_emb_b, w_in, b_gate, b_forget, q_norm_g, w_q_up, kv_norm_g, w_kv_up, w_branch_mla, w_branch_fox, w_out, ln_mix_g, ln_mix_b, w_ffn_up, conv_w, conv_b, w_ffn_down, ln_ffn_g, ln_ffn_b, loss_target, m_meta_tokens, m_ln_emb_g, m_ln_emb_b, m_w_in, m_b_gate, m_b_forget, m_q_norm_g, m_w_q_up, m_kv_norm_g, m_w_kv_up, m_w_branch_mla, m_w_branch_fox, m_w_out, m_ln_mix_g, m_ln_mix_b, m_w_ffn_up, m_conv_w, m_conv_b, m_w_ffn_down, m_ln_ffn_g, m_ln_ffn_b, v_meta_tokens, v_ln_emb_g, v_ln_emb_b, v_w_in, v_b_gate, v_b_forget, v_q_norm_g, v_w_q_up, v_kv_norm_g, v_w_kv_up, v_w_branch_mla, v_w_branch_fox, v_w_out, v_ln_mix_g, v_ln_mix_b, v_w_ffn_up, v_conv_w, v_conv_b, v_w_ffn_down, v_ln_ffn_g, v_ln_ffn_b):
    names = ("meta_tokens", "ln_emb_g", "ln_emb_b", "w_in", "b_gate", "b_forget", "q_norm_g", "w_q_up", "kv_norm_g",
             "w_kv_up", "w_branch_mla", "w_branch_fox", "w_out", "ln_mix_g", "ln_mix_b", "w_ffn_up", "conv_w", "conv_b",
             "w_ffn_down", "ln_ffn_g", "ln_ffn_b")
    w_args = (meta_tokens, ln_emb_g, ln_emb_b, w_in, b_gate, b_forget, q_norm_g, w_q_up, kv_norm_g, w_kv_up,
              w_branch_mla, w_branch_fox, w_out, ln_mix_g, ln_mix_b, w_ffn_up, conv_w, conv_b, w_ffn_down, ln_ffn_g, ln_ffn_b)
    m_args = (m_meta_tokens, m_ln_emb_g, m_ln_emb_b, m_w_in, m_b_gate, m_b_forget, m_q_norm_g, m_w_q_up, m_kv_norm_g,
              m_w_kv_up, m_w_branch_mla, m_w_branch_fox, m_w_out, m_ln_mix_g, m_ln_mix_b, m_w_ffn_up, m_conv_w, m_conv_b,
              m_w_ffn_down, m_ln_ffn_g, m_ln_ffn_b)
    v_args = (v_meta_tokens, v_ln_emb_g, v_ln_emb_b, v_w_in, v_b_gate, v_b_forget, v_q_norm_g, v_w_q_up, v_kv_norm_g,
              v_w_kv_up, v_w_branch_mla, v_w_branch_fox, v_w_out, v_ln_mix_g, v_ln_mix_b, v_w_ffn_up, v_conv_w, v_conv_b,
              v_w_ffn_down, v_ln_ffn_g, v_ln_ffn_b)
    as2d = lambda a: a.reshape((-1, a.shape[-1])) if a.ndim != 1 else a.reshape(1, -1)
    w = {n: as2d(a) for n, a in zip(names, w_args)}
    m = {n: as2d(a) for n, a in zip(names, m_args)}
    v = {n: as2d(a) for n, a in zip(names, v_args)}
    out_shape = {n: a.shape for n, a in zip(names, w_args)}

    axis_of = dict(BIG + SMALL_SHARDED)
    big = set(n for n, _ in BIG)
    wire = lambda n, a: a.astype(MXU_DTYPE) if n in big else a
    my_id = _my_id()

    early = _allgather("gather_early", [wire(n, w[n]) for n in EARLY])
    p = {n: _gathered_full(g3, axis_of[n]) for n, g3 in zip(EARLY, early) if n != "w_in"}
    p["w_in"] = _w_in_from_shards(early[EARLY.index("w_in")])
    for n in REPLICATED:
        p[n] = w[n].reshape(-1)
    late, tokens, prev = {}, [], early[0]
    for group, members in LATE.items():
        src = [wire(n, w[n]) for n in members]
        handle, token = _push_start("gather_" + group + "_start", src, False, after=prev)
        late[group] = (members, src, handle)
        tokens.append(token)
        prev = token
    sent = {}

    class Comm:
        first_token = tuple(tokens)

        def late_weights(self, group, after):
            members, src, handle = late[group]
            lands = _push_wait("gather_" + group + "_wait", handle, after)
            out = {}
            for n, own, land in zip(members, src, lands):
                g3 = lax.dynamic_update_index_in_dim(land, own, my_id, 0)
                out[n] = _cols_from_shards(n + "_repack", g3) if n == "w_ffn_up" else _gathered_full(g3, axis_of[n])
            return out

        def send(self, name, grads):
            names_ = tuple(grads)
            parts = []
            for n in names_:
                if n == "w_in":
                    parts.append(_w_in_grad_to_shards(*grads[n], N_DEV, w[n].shape[1]))
                elif n == "w_ffn_up":
                    parts.append(_cols_to_shards(n + "_grad_unpack", grads[n], N_DEV))
                else:
                    parts.append(_shard_major(grads[n], axis_of[n]).astype(MXU_DTYPE))
            handle, token = _push_start("send_" + name + "_start", parts, True)
            sent[name] = (names_, parts, handle)
            return (token,)

    loss_part, grad_x, d_meta, g = _local_step(x[0], loss_target[0], p, Comm())
    grad_x = grad_x[None]

    small = _pack([d_meta.reshape(-1)] + [g[n].reshape(-1) for n in REPLICATED] + [loss_part.reshape(-1)])
    small_handle, small_token = _push_start("send_small_start", [small], False)

    res = {}
    prev = small_token
    for name, (names_, parts, handle) in sent.items():
        lands = _push_wait("send_" + name + "_wait", handle, prev)
        for n, part, land in zip(names_, parts, lands):
            own = lax.dynamic_index_in_dim(part, my_id, axis=0, keepdims=False)
            res[n] = _adamw("adamw_" + n, land, w[n], m[n], v[n], own=own)
            prev = res[n][0]
    small_all = _push_wait("send_small_wait", small_handle, prev)[0]
    head = jnp.zeros((d_meta.size,), F32)
    rep_w = _pack([head] + [w[n].reshape(-1) for n in REPLICATED])
    rep_m = _pack([head] + [m[n].reshape(-1) for n in REPLICATED])
    rep_v = _pack([head] + [v[n].reshape(-1) for n in REPLICATED])
    rep_res = _adamw("adamw_replicated", small_all, rep_w, rep_m, rep_v, own=small)
    off = d_meta.size
    for n in REPLICATED:
        sz = w[n].size
        res[n] = tuple(a.reshape(-1)[off:off + sz] for a in rep_res)
        off += sz
    loss = rep_res[0].reshape(-1)[off]
    cols = w["meta_tokens"].shape[1]
    meta_rows = lambda a: a.reshape(a.shape[:-2] + (-1,))[..., :d_meta.size].reshape(a.shape[:-2] + d_meta.shape)
    my_cols = lambda a: lax.dynamic_slice_in_dim(a, my_id * cols, cols, axis=a.ndim - 1)
    res["meta_tokens"] = _adamw("adamw_meta_tokens", my_cols(meta_rows(small_all)), w["meta_tokens"],
                                m["meta_tokens"], v["meta_tokens"], own=my_cols(d_meta))

    outs = [loss, grad_x]
    for idx in range(4):
        outs += [res[n][idx].reshape(out_shape[n]) for n in names]
    return tuple(outs)
```

```python
import jax
import jax.numpy as jnp
from jax import lax
from jax.experimental import pallas as pl
from jax.experimental.pallas import tpu as pltpu

F32 = jnp.float32
BF16 = jnp.bfloat16
MXU_DTYPE = BF16

N_DEV = 8
N_META = 16
D_MODEL = 1024
HEADS = 8
Q_RANK = 384
KV_RANK = 128
NOPE = 64
ROPE = 32
HALF = ROPE // 2
QK_DIM = NOPE + ROPE
V_DIM = 64
FOX_DIM = 64
FOX_W = HEADS * FOX_DIM
D_FF = 2816
ROPE_THETA = 10000.0
LN_EPS = 1e-5
RMS_EPS = 1e-6
ALPHA = 2.0 ** 0.25
MLA_SCALE = QK_DIM ** -0.5
FOX_SCALE = FOX_DIM ** -0.5
NEG_INF = -1e30

HP = 128
HW = HEADS * HP
F_W = 3 * HW
R_QLAT = 0
R_KVLAT = Q_RANK
R_LAST = R_KVLAT + KV_RANK
R_GATE = D_MODEL
R_W = R_GATE + 2 * D_MODEL
LANE_FL = 0
LANE_PE = NOPE

ADAM_LR = 0.001
ADAM_B1 = 0.9
ADAM_B2 = 0.999
ADAM_EPS = 1e-08
ADAM_WD = 0.01
ADAM_STEP = 10

ROW_BLOCK = 256
ATT_TQ = 768
ATT_TK = 256
ATT_HEADS = 2
ROW_ALIGN = 768
MM_BLOCK_CAP = 1408
VMEM_LIMIT = 56 * 1024 * 1024
HIGHEST = lax.Precision.HIGHEST
NT = (((1,), (1,)), ((), ()))
TN = (((0,), (0,)), ((), ()))


def _params(sem=None):
    return pltpu.CompilerParams(dimension_semantics=sem, vmem_limit_bytes=VMEM_LIMIT)


def _call(name, body, grid, ins, outs, scratch=(), sem=None, after=()):
    n_in = len(ins)
    n_tok = len(after)

    def run(*refs):
        body(*refs[:n_in], *refs[n_in + n_tok:])

    tok_spec = pl.BlockSpec((8, 128), lambda *_: (0, 0))
    return pl.pallas_call(
        run, name=name, grid=grid,
        in_specs=[s for _, s in ins] + [tok_spec] * n_tok,
        out_specs=[s for _, s in outs],
        out_shape=[o for o, _ in outs],
        scratch_shapes=list(scratch),
        compiler_params=_params(sem),
    )(*[a for a, _ in ins], *after)


def _sds(shape, dtype):
    return jax.ShapeDtypeStruct(shape, dtype)


def _rows(br, c, cb=0):
    return pl.BlockSpec((br, c), lambda i: (i, cb))


def _whole(shape):
    n = len(shape)
    return pl.BlockSpec(shape, lambda i: (0,) * n)


def _pick(dim, cap, mult):
    best = None
    d = mult
    while d <= min(dim, cap):
        if dim % d == 0:
            best = d
        d += mult
    return best if best is not None else dim


def _hs(h):
    return slice(h * HP, (h + 1) * HP)


def _matmul(name, a, b, *, ta=False, tb=False, out_dtype=F32, addend=None, alpha=1.0, after=()):
    if ta:
        k, m = a.shape
    else:
        m, k = a.shape
    if tb:
        n, k2 = b.shape
    else:
        k2, n = b.shape
    assert k == k2, (name, a.shape, b.shape)
    bm = _pick(m, MM_BLOCK_CAP, 128 if ta else 16)
    bn = _pick(n, MM_BLOCK_CAP, 128)
    bk = _pick(k, MM_BLOCK_CAP, 128 if (not ta or tb) else 16)
    nk = k // bk
    dims = (((0 if ta else 1,), (1 if tb else 0,)), ((), ()))
    has_add = addend is not None

    def body(*refs):
        a_ref, b_ref = refs[:2]
        add_ref = refs[2] if has_add else None
        o_ref = refs[3 if has_add else 2]

        def finish(r):
            if has_add:
                r = r + alpha * add_ref[...]
            o_ref[...] = r.astype(o_ref.dtype)

        part = lax.dot_general(a_ref[...], b_ref[...], dims, preferred_element_type=F32)
        if nk == 1:
            finish(part)
            return
        acc_ref = refs[-1]
        kk = pl.program_id(2)

        @pl.when(kk == 0)
        def _():
            acc_ref[...] = part

        @pl.when(kk > 0)
        def _():
            acc_ref[...] += part

        @pl.when(kk == nk - 1)
        def _():
            finish(acc_ref[...])

    a_spec = pl.BlockSpec((bk, bm), lambda i, j, l: (l, i)) if ta else pl.BlockSpec((bm, bk), lambda i, j, l: (i, l))
    b_spec = pl.BlockSpec((bn, bk), lambda i, j, l: (j, l)) if tb else pl.BlockSpec((bk, bn), lambda i, j, l: (l, j))
    o_spec = pl.BlockSpec((bm, bn), lambda i, j, l: (i, j))
    ins = [(a, a_spec), (b, b_spec)]
    if has_add:
        ins.append((addend, o_spec))
    return _call(name, body, (m // bm, n // bn, nk), ins, [(_sds((m, n), out_dtype), o_spec)],
                 scratch=[pltpu.VMEM((bm, bn), F32)] if nk > 1 else [],
                 sem=("parallel", "parallel", "arbitrary"), after=after)[0]


def _ln_stats(z):
    mu = jnp.mean(z, axis=-1, keepdims=True)
    zc = z - mu
    var = jnp.mean(zc * zc, axis=-1, keepdims=True)
    rstd = lax.rsqrt(var + LN_EPS)
    return zc * rstd, rstd


def _ln_fwd(name, a, res, g, b, after=()):
    r, d = a.shape
    br = ROW_BLOCK
    has_res = res is not None

    def body(*refs):
        if has_res:
            a_ref, r_ref, g_ref, b_ref, y_ref, yb_ref = refs
            z = ALPHA * a_ref[...] + r_ref[...]
        else:
            a_ref, g_ref, b_ref, y_ref, yb_ref = refs
            z = a_ref[...]
        xhat, _ = _ln_stats(z)
        y = xhat * g_ref[...] + b_ref[...]
        y_ref[...] = y
        yb_ref[...] = y.astype(yb_ref.dtype)

    ins = [(a, _rows(br, d))]
    if has_res:
        ins.append((res, _rows(br, d)))
    ins += [(g.reshape(1, d), _whole((1, d))), (b.reshape(1, d), _whole((1, d)))]
    outs = [(_sds((r, d), F32), _rows(br, d)), (_sds((r, d), MXU_DTYPE), _rows(br, d))]
    return _call(name, body, (r // br,), ins, outs, sem=("parallel",), after=after)


def _ln_bwd(name, a, res, dy, g, after=()):
    r, d = a.shape
    br = ROW_BLOCK
    has_res = res is not None

    def body(*refs):
        if has_res:
            a_ref, r_ref, dy_ref, g_ref, dz_ref, dzb_ref, dg_ref, db_ref = refs
            z = ALPHA * a_ref[...] + r_ref[...]
        else:
            a_ref, dy_ref, g_ref, dz_ref, dzb_ref, dg_ref, db_ref = refs
            z = a_ref[...]
        xhat, rstd = _ln_stats(z)
        dyv = dy_ref[...]
        dyg = dyv * g_ref[...]
        m1 = jnp.mean(dyg, axis=-1, keepdims=True)
        m2 = jnp.mean(dyg * xhat, axis=-1, keepdims=True)
        dz = rstd * (dyg - m1 - xhat * m2)
        dz_ref[...] = dz
        dzb_ref[...] = dz.astype(dzb_ref.dtype)

        @pl.when(pl.program_id(0) == 0)
        def _():
            dg_ref[...] = jnp.zeros_like(dg_ref)
            db_ref[...] = jnp.zeros_like(db_ref)

        dg_ref[...] += jnp.sum(dyv * xhat, axis=0, keepdims=True)
        db_ref[...] += jnp.sum(dyv, axis=0, keepdims=True)

    ins = [(a, _rows(br, d))]
    if has_res:
        ins.append((res, _rows(br, d)))
    ins += [(dy, _rows(br, d)), (g.reshape(1, d), _whole((1, d)))]
    outs = [(_sds((r, d), F32), _rows(br, d)), (_sds((r, d), MXU_DTYPE), _rows(br, d)),
            (_sds((1, d), F32), _whole((1, d))), (_sds((1, d), F32), _whole((1, d)))]
    return _call(name, body, (r // br,), ins, outs, sem=("arbitrary",), after=after)


def _rms_fwd(name, proj, cb, width, g):
    r = proj.shape[0]
    br = ROW_BLOCK

    def body(x_ref, g_ref, y_ref):
        x = x_ref[...]
        rstd = lax.rsqrt(jnp.mean(x * x, axis=-1, keepdims=True) + RMS_EPS)
        y_ref[...] = (x * rstd * g_ref[...]).astype(y_ref.dtype)

    return _call(name, body, (r // br,), [(proj, _rows(br, width, cb)), (g.reshape(1, width), _whole((1, width)))],
                 [(_sds((r, width), MXU_DTYPE), _rows(br, width))], sem=("parallel",))[0]


def _rms_bwd(name, proj, cb, width, dy, g):
    r = proj.shape[0]
    br = ROW_BLOCK

    def body(x_ref, dy_ref, g_ref, dx_ref, dg_ref):
        x = x_ref[...]
        rstd = lax.rsqrt(jnp.mean(x * x, axis=-1, keepdims=True) + RMS_EPS)
        nrm = x * rstd
        dyv = dy_ref[...]
        dyg = dyv * g_ref[...]
        dx = rstd * (dyg - nrm * jnp.mean(dyg * nrm, axis=-1, keepdims=True))
        dx_ref[...] = dx.astype(dx_ref.dtype)

        @pl.when(pl.program_id(0) == 0)
        def _():
            dg_ref[...] = jnp.zeros_like(dg_ref)

        dg_ref[...] += jnp.sum(dyv * nrm, axis=0, keepdims=True)

    return _call(name, body, (r // br,),
                 [(proj, _rows(br, width, cb)), (dy, _rows(br, width)), (g.reshape(1, width), _whole((1, width)))],
                 [(_sds((r, width), MXU_DTYPE), _rows(br, width)), (_sds((1, width), F32), _whole((1, width)))],
                 sem=("arbitrary",))


def _lane_iota(shape):
    return lax.broadcasted_iota(jnp.int32, shape, 1)


def _rotary(t, c, s, lane, sign):
    second = pltpu.roll(t, HP - HALF, axis=1)
    first = pltpu.roll(t, HALF, axis=1)
    lo = (lane >= LANE_PE) & (lane < LANE_PE + HALF)
    hi = (lane >= LANE_PE + HALF) & (lane < LANE_PE + ROPE)
    return jnp.where(lo, t * c - sign * second * s, jnp.where(hi, t * c + sign * first * s, t))


def _rope_fwd(q_raw, k_part, proj_r, cos_t, sin_t):
    r = q_raw.shape[0]
    br = ROW_BLOCK

    def body(q_ref, k_ref, t_ref, c_ref, s_ref, qo_ref, ko_ref):
        c = c_ref[...]
        s = s_ref[...]
        lane = _lane_iota((br, HP))
        pe = (lane >= LANE_PE) & (lane < LANE_PE + ROPE)
        kp = jnp.where(pe, _rotary(t_ref[...], c, s, lane, 1.0), 0.0)
        for h in range(HEADS):
            qo_ref[:, _hs(h)] = (_rotary(q_ref[:, _hs(h)], c, s, lane, 1.0) * MLA_SCALE).astype(qo_ref.dtype)
            ko_ref[:, _hs(h)] = (k_ref[:, _hs(h)] + kp).astype(ko_ref.dtype)

    blk = _rows(br, HP)
    wide = _rows(br, HW)
    return _call("rope_fwd", body, (r // br,),
                 [(q_raw, wide), (k_part, wide), (proj_r, _rows(br, HP, R_LAST // HP)), (cos_t, blk), (sin_t, blk)],
                 [(_sds((r, HW), MXU_DTYPE), wide)] * 2, sem=("parallel",))


def _rope_bwd(dq, dk, dv, dfl, cos_t, sin_t):
    r = dq.shape[0]
    br = ROW_BLOCK

    def body(dq_ref, dk_ref, dv_ref, fl_ref, c_ref, s_ref, dqo_ref, dkv_ref, dl_ref):
        c = c_ref[...]
        s = s_ref[...]
        lane = _lane_iota((br, HP))
        pe = (lane >= LANE_PE) & (lane < LANE_PE + ROPE)
        acc = jnp.zeros((br, HP), F32)
        for h in range(HEADS):
            dqo_ref[:, _hs(h)] = (_rotary(dq_ref[:, _hs(h)], c, s, lane, -1.0) * MLA_SCALE).astype(dqo_ref.dtype)
            dkh = dk_ref[:, _hs(h)]
            acc = acc + dkh
            dkv_ref[:, _hs(h)] = dkh.astype(dkv_ref.dtype)
            dkv_ref[:, _hs(HEADS + h)] = dv_ref[:, _hs(h)].astype(dkv_ref.dtype)
        dl_ref[...] = (jnp.where(pe, _rotary(acc, c, s, lane, -1.0), 0.0) + fl_ref[...]).astype(dl_ref.dtype)

    blk = _rows(br, HP)
    wide = _rows(br, HW)
    return _call("rope_bwd", body, (r // br,),
                 [(dq, wide), (dk, wide), (dv, wide), (dfl, blk), (cos_t, blk), (sin_t, blk)],
                 [(_sds((r, HW), MXU_DTYPE), wide), (_sds((r, 2 * HW), MXU_DTYPE), _rows(br, 2 * HW)),
                  (_sds((r, HP), MXU_DTYPE), blk)],
                 sem=("parallel",))


def _log_sigmoid(x):
    return jnp.minimum(x, 0.0) - jnp.log(1.0 + jnp.exp(-jnp.abs(x)))


def _head_lane(x, h, lane):
    return jnp.sum(jnp.where(lane == h, x, 0.0), axis=1, keepdims=True)


def _forget_fwd(proj_r, bf_row):
    r = proj_r.shape[0]
    br = ROW_BLOCK

    def body(t_ref, b_ref, ob_ref, ot_ref, carry_ref):
        @pl.when(pl.program_id(0) == 0)
        def _():
            carry_ref[...] = jnp.zeros_like(carry_ref)

        x = t_ref[...] + b_ref[...]
        lane = _lane_iota(x.shape)
        lf = jnp.where((lane >= LANE_FL) & (lane < LANE_FL + HEADS), _log_sigmoid(x), 0.0)
        tri = (lax.broadcasted_iota(jnp.int32, (br, br), 0) >= lax.broadcasted_iota(jnp.int32, (br, br), 1)).astype(F32)
        cum = jnp.dot(tri, lf, precision=HIGHEST, preferred_element_type=F32) + carry_ref[0:1, :]
        for h in range(HEADS):
            ob_ref[:, _hs(h)] = jnp.broadcast_to(_head_lane(cum, LANE_FL + h, lane), (br, HP))
        ot_ref[...] = cum.T[LANE_FL:LANE_FL + HEADS, :]
        carry_ref[...] = jnp.broadcast_to(cum[br - 1:br, :], carry_ref.shape)

    return _call("forget_fwd", body, (r // br,),
                 [(proj_r, _rows(br, HP, R_LAST // HP)), (bf_row, _whole((1, HP)))],
                 [(_sds((r, HW), F32), _rows(br, HW)), (_sds((HEADS, r), F32), pl.BlockSpec((HEADS, br), lambda i: (0, i)))],
                 scratch=[pltpu.VMEM((8, HP), F32)], sem=("arbitrary",))


def _forget_bwd(proj_r, bf_row, dcq_t, dck_b):
    r = proj_r.shape[0]
    br = ROW_BLOCK
    nb = r // br

    def body(t_ref, b_ref, dcq_ref, dck_ref, o_ref, db_ref, carry_ref):
        @pl.when(pl.program_id(0) == 0)
        def _():
            carry_ref[...] = jnp.zeros_like(carry_ref)
            db_ref[...] = jnp.zeros_like(db_ref)

        lane = _lane_iota((br, HP))
        dc = jnp.concatenate([dcq_ref[...], jnp.zeros((HP - HEADS, br), F32)], axis=0).T
        for h in range(HEADS):
            dc = dc + jnp.where(lane == LANE_FL + h, dck_ref[:, h * HP:h * HP + 1], 0.0)
        triu = (lax.broadcasted_iota(jnp.int32, (br, br), 0) <= lax.broadcasted_iota(jnp.int32, (br, br), 1)).astype(F32)
        dlf = jnp.dot(triu, dc, precision=HIGHEST, preferred_element_type=F32) + carry_ref[0:1, :]
        carry_ref[...] = jnp.broadcast_to(dlf[0:1, :], carry_ref.shape)
        x = t_ref[...] + b_ref[...]
        dfl = jnp.where((lane >= LANE_FL) & (lane < LANE_FL + HEADS), dlf * jax.nn.sigmoid(-x), 0.0)
        o_ref[...] = dfl
        db_ref[...] += jnp.sum(dfl, axis=0, keepdims=True)

    rev = pl.BlockSpec((br, HP), lambda i: (nb - 1 - i, 0))
    return _call("forget_bwd", body, (nb,),
                 [(proj_r, pl.BlockSpec((br, HP), lambda i: (nb - 1 - i, R_LAST // HP))), (bf_row, _whole((1, HP))),
                  (dcq_t, pl.BlockSpec((HEADS, br), lambda i: (0, nb - 1 - i))),
                  (dck_b, pl.BlockSpec((br, HW), lambda i: (nb - 1 - i, 0)))],
                 [(_sds((r, HP), F32), rev), (_sds((1, HP), F32), _whole((1, HP)))],
                 scratch=[pltpu.VMEM((8, HP), F32)], sem=("arbitrary",))


def _gate_fwd(proj_r, b_gate, bm, bfx):
    r, d = bm.shape
    br = ROW_BLOCK
    cb = R_GATE // d

    def body(gm_ref, gf_ref, b1_ref, b2_ref, bm_ref, bf_ref, o_ref):
        g1 = jax.nn.sigmoid(gm_ref[...] + b1_ref[...])
        g2 = jax.nn.sigmoid(gf_ref[...] + b2_ref[...])
        o_ref[...] = (g1 * bm_ref[...].astype(F32) + g2 * bf_ref[...].astype(F32)).astype(o_ref.dtype)

    b1 = b_gate[:d].reshape(1, d)
    b2 = b_gate[d:].reshape(1, d)
    return _call("gate_fwd", body, (r // br,),
                 [(proj_r, _rows(br, d, cb)), (proj_r, _rows(br, d, cb + 1)), (b1, _whole((1, d))), (b2, _whole((1, d))),
                  (bm, _rows(br, d)), (bfx, _rows(br, d))],
                 [(_sds((r, d), MXU_DTYPE), _rows(br, d))], sem=("parallel",))[0]


def _gate_bwd(proj_r, b_gate, bm, bfx, dmerged):
    r, d = bm.shape
    br = ROW_BLOCK
    cb = R_GATE // d

    def body(gm_ref, gf_ref, b1_ref, b2_ref, bm_ref, bf_ref, dm_ref, dbm_ref, dbf_ref, dgl_ref, dbg_ref):
        g1 = jax.nn.sigmoid(gm_ref[...] + b1_ref[...])
        g2 = jax.nn.sigmoid(gf_ref[...] + b2_ref[...])
        dm = dm_ref[...].astype(F32)
        dbm_ref[...] = (dm * g1).astype(dbm_ref.dtype)
        dbf_ref[...] = (dm * g2).astype(dbf_ref.dtype)
        dl1 = dm * bm_ref[...].astype(F32) * (g1 * (1.0 - g1))
        dl2 = dm * bf_ref[...].astype(F32) * (g2 * (1.0 - g2))
        dgl_ref[:, 0:d] = dl1.astype(dgl_ref.dtype)
        dgl_ref[:, d:2 * d] = dl2.astype(dgl_ref.dtype)

        @pl.when(pl.program_id(0) == 0)
        def _():
            dbg_ref[...] = jnp.zeros_like(dbg_ref)

        dbg_ref[:, 0:d] += jnp.sum(dl1, axis=0, keepdims=True)
        dbg_ref[:, d:2 * d] += jnp.sum(dl2, axis=0, keepdims=True)

    b1 = b_gate[:d].reshape(1, d)
    b2 = b_gate[d:].reshape(1, d)
    return _call("gate_bwd", body, (r // br,),
                 [(proj_r, _rows(br, d, cb)), (proj_r, _rows(br, d, cb + 1)), (b1, _whole((1, d))), (b2, _whole((1, d))),
                  (bm, _rows(br, d)), (bfx, _rows(br, d)), (dmerged, _rows(br, d))],
                 [(_sds((r, d), MXU_DTYPE), _rows(br, d)), (_sds((r, d), MXU_DTYPE), _rows(br, d)),
                  (_sds((r, 2 * d), MXU_DTYPE), _rows(br, 2 * d)), (_sds((1, 2 * d), F32), _whole((1, 2 * d)))],
                 sem=("arbitrary",))


HALO = 16
GLU_BWD_BLOCK = 128


def _conv_taps(gp, halo, first_block):
    halo = jnp.where(first_block, 0.0, halo.astype(F32))
    rid = lax.broadcasted_iota(jnp.int32, gp.shape, 0)
    last, prev = halo[HALO - 1:HALO, :], halo[HALO - 2:HALO - 1, :]
    g1 = jnp.where(rid == 0, last, pltpu.roll(gp, 1, axis=0))
    g2 = jnp.where(rid == 0, prev, jnp.where(rid == 1, last, pltpu.roll(gp, 2, axis=0)))
    return g1, g2


def _prev_halo(br, c):
    return pl.BlockSpec((HALO, c), lambda i: (jnp.maximum(i * (br // HALO) - 1, 0), 0))


def _glu_fwd(up, conv_w, conv_b):
    r = up.shape[0]
    c = D_FF
    br = ROW_BLOCK

    def body(gp_ref, halo_ref, val_ref, w_ref, b_ref, o_ref):
        gp = gp_ref[...].astype(F32)
        g1, g2 = _conv_taps(gp, halo_ref[...], pl.program_id(0) == 0)
        gate = w_ref[0:1, :] * g2 + w_ref[1:2, :] * g1 + w_ref[2:3, :] * gp + b_ref[...]
        o_ref[...] = (gate * jax.nn.sigmoid(gate) * val_ref[...].astype(F32)).astype(o_ref.dtype)

    return _call("glu_fwd", body, (r // br,),
                 [(up, _rows(br, c, 0)), (up, _prev_halo(br, c)), (up, _rows(br, c, 1)),
                  (conv_w, _whole((3, c))), (conv_b.reshape(1, c), _whole((1, c)))],
                 [(_sds((r, c), MXU_DTYPE), _rows(br, c))], sem=("parallel",))[0]


def _glu_bwd(up, conv_w, conv_b, d_act):
    r = up.shape[0]
    c = D_FF
    br = GLU_BWD_BLOCK
    nb = r // br

    def body(gp_ref, halo_ref, val_ref, da_ref, gpn_ref, valn_ref, dan_ref, w_ref, b_ref, o_ref, dw_ref, db_ref):
        i = pl.program_id(0)
        w0, w1, w2, bias = w_ref[0:1, :], w_ref[1:2, :], w_ref[2:3, :], b_ref[...]

        def d_gate(gp, g1, g2, val, da):
            gate = w0 * g2 + w1 * g1 + w2 * gp + bias
            sg = jax.nn.sigmoid(gate)
            return da * val * (sg * (1.0 + gate * (1.0 - sg))), da * (gate * sg)

        gp = gp_ref[...].astype(F32)
        g1, g2 = _conv_taps(gp, halo_ref[...], i == 0)
        dg, dv = d_gate(gp, g1, g2, val_ref[...].astype(F32), da_ref[...].astype(F32))
        gpn = gpn_ref[...].astype(F32)
        g1n, g2n = _conv_taps(gpn, gp[br - HALO:, :], False)
        dgn, _ = d_gate(gpn, g1n, g2n, valn_ref[...].astype(F32), dan_ref[...].astype(F32))
        dgn = jnp.where(i == nb - 1, 0.0, dgn)
        rid = lax.broadcasted_iota(jnp.int32, dg.shape, 0)
        u1 = jnp.where(rid == br - 1, dgn[0:1, :], pltpu.roll(dg, br - 1, axis=0))
        u2 = jnp.where(rid == br - 1, dgn[1:2, :], jnp.where(rid == br - 2, dgn[0:1, :], pltpu.roll(dg, br - 2, axis=0)))
        o_ref[:, 0:c] = (w2 * dg + w1 * u1 + w0 * u2).astype(o_ref.dtype)
        o_ref[:, c:2 * c] = dv.astype(o_ref.dtype)

        @pl.when(i == 0)
        def _():
            dw_ref[...] = jnp.zeros_like(dw_ref)
            db_ref[...] = jnp.zeros_like(db_ref)

        dw_ref[0:1, :] += jnp.sum(dg * g2, axis=0, keepdims=True)
        dw_ref[1:2, :] += jnp.sum(dg * g1, axis=0, keepdims=True)
        dw_ref[2:3, :] += jnp.sum(dg * gp, axis=0, keepdims=True)
        db_ref[...] += jnp.sum(dg, axis=0, keepdims=True)

    nxt = lambda cb: pl.BlockSpec((HALO, c), lambda i: (jnp.minimum((i + 1) * (br // HALO), r // HALO - 1), cb))
    return _call("glu_bwd", body, (nb,),
                 [(up, _rows(br, c, 0)), (up, _prev_halo(br, c)), (up, _rows(br, c, 1)), (d_act, _rows(br, c)),
                  (up, nxt(0)), (up, nxt(1)), (d_act, nxt(0)),
                  (conv_w, _whole((3, c))), (conv_b.reshape(1, c), _whole((1, c)))],
                 [(_sds((r, 2 * c), MXU_DTYPE), _rows(br, 2 * c)),
                  (_sds((8, c), F32), _whole((8, c))), (_sds((1, c), F32), _whole((1, c)))],
                 sem=("arbitrary",))


def _token_specs(seq, d):
    br = ROW_BLOCK
    nxb = seq // br
    main = pl.BlockSpec((br, d), lambda i: (jnp.minimum(i, nxb - 1), 0))
    tail = pl.BlockSpec((N_META, d), lambda i: (jnp.clip(i * (br // N_META) - 1, 0, seq // N_META - 1), 0))
    return main, tail


def _padded_block(main_ref, tail_ref, first, seq):
    br = ROW_BLOCK
    i = pl.program_id(0)
    nxb = seq // br
    main = jnp.where(i < nxb, main_ref[...], 0.0)
    head = jnp.where(i == 0, first, jnp.where(i <= nxb, tail_ref[...], 0.0))
    return jnp.concatenate([head, main[:br - N_META]], axis=0)


def _ln_emb_fwd(x, meta, g, b, rows, after=()):
    seq, d = x.shape
    br = ROW_BLOCK
    assert seq % br == 0 and br % N_META == 0 and rows % br == 0

    def body(x_ref, tail_ref, meta_ref, g_ref, b_ref, y_ref, yb_ref):
        z = _padded_block(x_ref, tail_ref, meta_ref[...], seq)
        xhat, _ = _ln_stats(z)
        y = xhat * g_ref[...] + b_ref[...]
        y_ref[...] = y
        yb_ref[...] = y.astype(yb_ref.dtype)

    main, tail = _token_specs(seq, d)
    return _call("ln_emb_fwd", body, (rows // br,),
                 [(x, main), (x, tail), (meta, _whole((N_META, d))), (g.reshape(1, d), _whole((1, d))),
                  (b.reshape(1, d), _whole((1, d)))],
                 [(_sds((rows, d), F32), _rows(br, d)), (_sds((rows, d), MXU_DTYPE), _rows(br, d))],
                 sem=("parallel",), after=after)


def _ln_emb_bwd(x, meta, dh0, g):
    seq, d = x.shape
    br = ROW_BLOCK
    step = br // N_META

    def ln_bwd(z, dy, gv):
        xhat, rstd = _ln_stats(z)
        dyg = dy * gv
        m1 = jnp.mean(dyg, axis=-1, keepdims=True)
        m2 = jnp.mean(dyg * xhat, axis=-1, keepdims=True)
        dz = rstd * (dyg - m1 - xhat * m2)
        return dz, jnp.sum(dy * xhat, axis=0, keepdims=True), jnp.sum(dy, axis=0, keepdims=True)

    def body(x_ref, dh_ref, nxt_ref, meta_ref, top_ref, g_ref, dx_ref, dm_ref, dg_ref, db_ref):
        gv = g_ref[...]
        dy = jnp.concatenate([dh_ref[N_META:, :], nxt_ref[...]], axis=0)
        dz, dg, db = ln_bwd(x_ref[...], dy, gv)
        dx_ref[...] = dz

        @pl.when(pl.program_id(0) == 0)
        def _():
            dzm, dgm, dbm = ln_bwd(meta_ref[...], top_ref[...], gv)
            dm_ref[...] = dzm
            dg_ref[...] = dgm
            db_ref[...] = dbm

        dg_ref[...] += dg
        db_ref[...] += db

    small = _whole((N_META, d))
    return _call("ln_emb_bwd", body, (seq // br,),
                 [(x, _rows(br, d)), (dh0, _rows(br, d)), (dh0, pl.BlockSpec((N_META, d), lambda i: ((i + 1) * step, 0))),
                  (meta, small), (dh0, small), (g.reshape(1, d), _whole((1, d)))],
                 [(_sds((seq, d), F32), _rows(br, d)), (_sds((N_META, d), F32), small),
                  (_sds((1, d), F32), _whole((1, d))), (_sds((1, d), F32), _whole((1, d)))], sem=("arbitrary",))


def _ln_ffn_loss(h1, f, tgt, g, b):
    r, d = h1.shape
    seq = tgt.shape[0]
    br = ROW_BLOCK

    def body(a_ref, r_ref, t_ref, tail_ref, g_ref, b_ref, l_ref):
        err = _loss_err(a_ref, r_ref, t_ref, tail_ref, g_ref, b_ref, seq)[0]

        @pl.when(pl.program_id(0) == 0)
        def _():
            l_ref[...] = jnp.zeros_like(l_ref)

        l_ref[...] += jnp.sum(jnp.sum(err * err, axis=1, keepdims=True), axis=0, keepdims=True) * (0.5 / d)

    main, tail = _token_specs(seq, d)
    return _call("ln_ffn_loss", body, (r // br,),
                 [(h1, _rows(br, d)), (f, _rows(br, d)), (tgt, main), (tgt, tail),
                  (g.reshape(1, d), _whole((1, d))), (b.reshape(1, d), _whole((1, d)))],
                 [(_sds((1, 1), F32), _whole((1, 1)))], sem=("arbitrary",))[0]


def _loss_err(a_ref, r_ref, t_ref, tail_ref, g_ref, b_ref, seq):
    br, d = a_ref.shape
    xhat, rstd = _ln_stats(ALPHA * a_ref[...] + r_ref[...])
    y = xhat * g_ref[...] + b_ref[...]
    t = _padded_block(t_ref, tail_ref, jnp.zeros((N_META, d), F32), seq)
    rid = lax.broadcasted_iota(jnp.int32, (br, d), 0) + pl.program_id(0) * br
    valid = (rid >= N_META) & (rid < N_META + seq)
    return jnp.where(valid, y - t, 0.0), xhat, rstd


def _ln_ffn_bwd(h1, f, tgt, g, b):
    r, d = h1.shape
    seq = tgt.shape[0]
    br = ROW_BLOCK

    def body(a_ref, r_ref, t_ref, tail_ref, g_ref, b_ref, dz_ref, dzb_ref, dg_ref, db_ref):
        err, xhat, rstd = _loss_err(a_ref, r_ref, t_ref, tail_ref, g_ref, b_ref, seq)
        dyv = err * (1.0 / d)
        dyg = dyv * g_ref[...]
        m1 = jnp.mean(dyg, axis=-1, keepdims=True)
        m2 = jnp.mean(dyg * xhat, axis=-1, keepdims=True)
        dz = rstd * (dyg - m1 - xhat * m2)
        dz_ref[...] = dz
        dzb_ref[...] = dz.astype(dzb_ref.dtype)

        @pl.when(pl.program_id(0) == 0)
        def _():
            dg_ref[...] = jnp.zeros_like(dg_ref)
            db_ref[...] = jnp.zeros_like(db_ref)

        dg_ref[...] += jnp.sum(dyv * xhat, axis=0, keepdims=True)
        db_ref[...] += jnp.sum(dyv, axis=0, keepdims=True)

    main, tail = _token_specs(seq, d)
    return _call("ln_ffn_bwd", body, (r // br,),
                 [(h1, _rows(br, d)), (f, _rows(br, d)), (tgt, main), (tgt, tail),
                  (g.reshape(1, d), _whole((1, d))), (b.reshape(1, d), _whole((1, d)))],
                 [(_sds((r, d), F32), _rows(br, d)), (_sds((r, d), MXU_DTYPE), _rows(br, d)),
                  (_sds((1, d), F32), _whole((1, d))), (_sds((1, d), F32), _whole((1, d)))], sem=("arbitrary",))


def _attn_fwd(name, q, k, v, cum_b=None, cum_t=None):
    (qa, qg), (ka, kg), (va, vg) = q, k, v
    r = qa.shape[0]
    tq, tk = ATT_TQ, ATT_TK
    nq, nk = r // tq, r // tk
    bias = cum_b is not None

    def body(*refs):
        if bias:
            q_ref, k_ref, vt_ref, cb_ref, ct_ref, o_ref, ob_ref, lse_ref = refs
        else:
            q_ref, k_ref, vt_ref, o_ref, ob_ref, lse_ref = refs
        i = pl.program_id(1)
        qs = [q_ref[:, _hs(hh)] for hh in range(hg)]
        cqs = [ct_ref[hh] for hh in range(hg)] if bias else None
        diff = lax.broadcasted_iota(jnp.int32, (tk, tq), 0) - lax.broadcasted_iota(jnp.int32, (tk, tq), 1)

        def step(j, carry, masked):
            keys = pl.ds(pl.multiple_of(j * tk, tk), tk)
            out = []
            for hh in range(hg):
                m, l, acc = carry[hh]
                kt = k_ref[keys, _hs(hh)]
                s = lax.dot_general(kt, qs[hh], NT, preferred_element_type=F32)
                if bias:
                    s = s + (cqs[hh] - cb_ref[keys, hh * HP:hh * HP + 1])
                if masked:
                    s = jnp.where(diff <= i * tq - j * tk, s, NEG_INF)
                m_new = jnp.maximum(m, jnp.max(s, axis=0, keepdims=True))
                p = jnp.exp(s - m_new)
                a = jnp.exp(m - m_new)
                l = a * l + jnp.sum(p, axis=0, keepdims=True)
                acc = a * acc + jnp.dot(vt_ref[j, _hs(hh), :], p.astype(kt.dtype), preferred_element_type=F32)
                out.append((m_new, l, acc))
            return tuple(out)

        n_clear = (i * tq + 1) // tk
        n_all = ((i + 1) * tq - 1) // tk + 1
        carry = tuple((jnp.full((1, tq), NEG_INF, F32), jnp.zeros((1, tq), F32), jnp.zeros((HP, tq), F32))
                      for _ in range(hg))
        carry = lax.fori_loop(0, n_clear, lambda j, c: step(j, c, False), carry)
        carry = lax.fori_loop(n_clear, n_all, lambda j, c: step(j, c, True), carry)
        for hh in range(hg):
            m, l, acc = carry[hh]
            o = (acc / l).T
            o_ref[:, _hs(hh)] = o
            ob_ref[:, _hs(hh)] = o.astype(ob_ref.dtype)
            lse_ref[hh] = m + jnp.log(l)

    hg = ATT_HEADS
    w = hg * HP
    gpw = HW // w
    tile = lambda g: pl.BlockSpec((tq, w), lambda h, i: (i, g * gpw + h))
    res = lambda g: pl.BlockSpec((r, w), lambda h, i: (0, g * gpw + h))
    v_t = _key_tiles_transposed(name + "_vt", va, vg)
    ins = [(qa, tile(qg)), (ka, res(kg)), (v_t, pl.BlockSpec((nk, w, tk), lambda h, i: (0, h, 0)))]
    if bias:
        ins += [(cum_b, res(0)),
                (cum_t.reshape(HEADS, nq, 1, tq), pl.BlockSpec((hg, None, 1, tq), lambda h, i: (h, i, 0, 0)))]
    outs = [(_sds((r, HW), F32), tile(0)), (_sds((r, HW), MXU_DTYPE), tile(0)),
            (_sds((HEADS, nq, 1, tq), F32), pl.BlockSpec((hg, None, 1, tq), lambda h, i: (h, i, 0, 0)))]
    o, ob, lse = _call(name, body, (gpw, nq), ins, outs, sem=("parallel", "parallel"))
    return o, ob, lse.reshape(HEADS, r)


def _key_tiles_transposed(name, a, group):
    r = a.shape[0]
    tk = ATT_TK

    def body(x_ref, o_ref):
        for h in range(HEADS):
            o_ref[_hs(h), :] = x_ref[:, _hs(h)].astype(F32).T.astype(o_ref.dtype)

    return _call(name, body, (r // tk,),
                 [(a, pl.BlockSpec((tk, HW), lambda j: (j, group)))],
                 [(_sds((r // tk, HW, tk), a.dtype), pl.BlockSpec((None, HW, tk), lambda j: (j, 0, 0)))],
                 sem=("parallel",))[0]


def _attn_delta(name, do_b, o, after=()):
    r = do_b.shape[0]
    br = ROW_BLOCK

    def body(do_ref, o_ref, d_ref):
        lane = _lane_iota((br, HP))
        d = jnp.zeros((br, HP), F32)
        for h in range(HEADS):
            dh = do_ref[:, _hs(h)].astype(F32)
            d = jnp.where(lane == h, jnp.sum(dh * o_ref[:, _hs(h)], axis=1, keepdims=True), d)
        d_ref[...] = d.T[0:HEADS, :]

    wide = _rows(br, HW)
    return _call(name, body, (r // br,), [(do_b, wide), (o, wide)],
                 [(_sds((HEADS, r), F32), pl.BlockSpec((HEADS, br), lambda i: (0, i)))],
                 sem=("parallel",), after=after)[0]


def _attn_bwd(name, q, k, v, do_b, lse_t, delta_t, cum_b=None, cum_t=None, out_dtype=F32):
    (qa, qg), (ka, kg), (va, vg) = q, k, v
    r = qa.shape[0]
    tq, tk = ATT_TQ, ATT_TK
    nq, nk = r // tq, r // tk
    bias = cum_b is not None

    def body(*refs):
        if bias:
            (q_ref, k_ref, v_ref, do_ref, lse_ref, dl_ref, cb_ref, ct_ref,
             dq_ref, dk_ref, dv_ref, dcq_ref, dck_ref, dqt_ref) = refs
        else:
            q_ref, k_ref, v_ref, do_ref, lse_ref, dl_ref, dq_ref, dk_ref, dv_ref, dqt_ref = refs
        j = pl.program_id(1)

        @pl.when(j == 0)
        def _():
            dqt_ref[...] = jnp.zeros_like(dqt_ref)
            if bias:
                dcq_ref[...] = jnp.zeros_like(dcq_ref)

        kts = [k_ref[:, _hs(hh)] for hh in range(hg)]
        vts = [v_ref[:, _hs(hh)] for hh in range(hg)]
        k_trs = [kt.astype(F32).T.astype(kt.dtype) for kt in kts]
        cks = [cb_ref[:, hh * HP:hh * HP + 1] for hh in range(hg)] if bias else None
        diff = lax.broadcasted_iota(jnp.int32, (tk, tq), 0) - lax.broadcasted_iota(jnp.int32, (tk, tq), 1)

        def step(i, carry, masked):
            rows = pl.ds(pl.multiple_of(i * tq, tq), tq)
            out = []
            for hh in range(hg):
                dk_acc, dv_acc, dck_acc = carry[hh]
                qt = q_ref[rows, _hs(hh)]
                dot = do_ref[rows, _hs(hh)]
                s = lax.dot_general(kts[hh], qt, NT, preferred_element_type=F32)
                if bias:
                    s = s + (ct_ref[hh, i] - cks[hh])
                if masked:
                    s = jnp.where(diff <= i * tq - j * tk, s, NEG_INF)
                p = jnp.exp(s - lse_ref[hh, i])
                dp = lax.dot_general(vts[hh], dot, NT, preferred_element_type=F32)
                ds = p * (dp - dl_ref[hh, i])
                pb = p.astype(dot.dtype)
                dsb = ds.astype(qt.dtype)
                dv_acc = dv_acc + jnp.dot(pb, dot, preferred_element_type=F32)
                dk_acc = dk_acc + jnp.dot(dsb, qt, preferred_element_type=F32)
                dqt_ref[hh, i] += jnp.dot(k_trs[hh], dsb, preferred_element_type=F32)
                if bias:
                    dcq_ref[hh, i] += jnp.sum(ds, axis=0, keepdims=True)
                    dck_acc = dck_acc - jnp.sum(ds, axis=1, keepdims=True)
                out.append((dk_acc, dv_acc, dck_acc))
            return tuple(out)

        i_first = (j * tk) // tq
        i_clear = jnp.minimum(((j + 1) * tk + tq - 2) // tq, nq)
        carry = tuple((jnp.zeros((tk, HP), F32), jnp.zeros((tk, HP), F32), jnp.zeros((tk, 1), F32)) for _ in range(hg))
        carry = lax.fori_loop(i_first, i_clear, lambda i, c: step(i, c, True), carry)
        carry = lax.fori_loop(i_clear, nq, lambda i, c: step(i, c, False), carry)
        for hh in range(hg):
            dk_acc, dv_acc, dck_acc = carry[hh]
            dk_ref[:, _hs(hh)] = dk_acc.astype(dk_ref.dtype)
            dv_ref[:, _hs(hh)] = dv_acc.astype(dv_ref.dtype)
            if bias:
                dck_ref[:, _hs(hh)] = jnp.broadcast_to(dck_acc, (tk, HP))

        @pl.when(j == nk - 1)
        def _():
            for hh in range(hg):
                for i in range(nq):
                    dq_ref[i * tq:(i + 1) * tq, _hs(hh)] = dqt_ref[hh, i].T.astype(dq_ref.dtype)

    hg = ATT_HEADS
    w = hg * HP
    gpw = HW // w
    res = lambda g: pl.BlockSpec((r, w), lambda h, j: (0, g * gpw + h))
    tile = lambda g: pl.BlockSpec((tk, w), lambda h, j: (j, g * gpw + h))
    rowv = pl.BlockSpec((hg, nq, 1, tq), lambda h, j: (h, 0, 0, 0))
    as_rows = lambda a: a.reshape(HEADS, nq, 1, tq)
    ins = [(qa, res(qg)), (ka, tile(kg)), (va, tile(vg)), (do_b, res(0)), (as_rows(lse_t), rowv), (as_rows(delta_t), rowv)]
    outs = [(_sds((r, HW), out_dtype), res(0)), (_sds((r, HW), out_dtype), tile(0)), (_sds((r, HW), out_dtype), tile(0))]
    if bias:
        ins += [(cum_b, tile(0)), (as_rows(cum_t), rowv)]
        outs += [(_sds((HEADS, nq, 1, tq), F32), rowv), (_sds((r, HW), F32), tile(0))]
    res_out = _call(name, body, (gpw, nk), ins, outs, scratch=[pltpu.VMEM((hg, nq, HP, tq), F32)],
                    sem=("parallel", "arbitrary"))
    if bias:
        dq, dk, dv, dcq, dck = res_out
        return dq, dk, dv, dcq.reshape(HEADS, r), dck
    return res_out


MESH_ID = pl.DeviceIdType.MESH
ANY = pl.BlockSpec(memory_space=pl.ANY)


def _allgather(name, shards):
    n = len(shards)

    def body(*refs):
        x_refs, out_refs = refs[:n], refs[n:2 * n]
        send_sems, recv_sems, local_sems = refs[2 * n:]
        x, y, c = lax.axis_index("x"), lax.axis_index("y"), lax.axis_index("c")
        me, sibling = (x, y, c), (x, y, 1 - c)
        chips = [(1 - x, y), (x, 1 - y), (1 - x, 1 - y)]

        def slot(ti, px, py, pc):
            return out_refs[ti].at[4 * px + 2 * py + pc]

        def copy(ti, k, block, to, src=None):
            return pltpu.make_async_remote_copy(
                src_ref=slot(ti, *block) if src is None else src, dst_ref=slot(ti, *block),
                send_sem=send_sems.at[ti, k], recv_sem=recv_sems.at[ti, k], device_id=to, device_id_type=MESH_ID)

        mine = [pltpu.make_async_copy(x_refs[ti], slot(ti, *me), local_sems.at[ti]) for ti in range(n)]
        for cp in mine:
            cp.start()
        started = []
        for ti in range(n):
            first = [copy(ti, 0, me, sibling, src=x_refs[ti])]
            first += [copy(ti, 1 + j, me, (*chip, c), src=x_refs[ti]) for j, chip in enumerate(chips)]
            for cp in first:
                cp.start()
            started += first
        for ti in range(n):
            for j, chip in enumerate(chips):
                copy(ti, 1 + j, (*chip, c), me).wait_recv()
                fwd = copy(ti, 4 + j, (*chip, c), sibling)
                fwd.start()
                started.append(fwd)
        for ti in range(n):
            copy(ti, 0, sibling, me).wait_recv()
            for j, chip in enumerate(chips):
                copy(ti, 4 + j, (*chip, 1 - c), me).wait_recv()
        for cp in started:
            cp.wait_send()
        for cp in mine:
            cp.wait()

    return pl.pallas_call(
        body, name=name, out_shape=[_sds((N_DEV,) + s.shape, s.dtype) for s in shards],
        in_specs=[ANY] * n, out_specs=[ANY] * n,
        scratch_shapes=[pltpu.SemaphoreType.DMA((n, 7)), pltpu.SemaphoreType.DMA((n, 7)), pltpu.SemaphoreType.DMA((n,))],
    )(*shards)


HBM = pl.BlockSpec(memory_space=pltpu.HBM)
SEM = pl.BlockSpec(memory_space=pltpu.SEMAPHORE)
EFFECT = pltpu.SideEffectType.DATAFLOW_SIDE_EFFECTING
N_PEER = N_DEV - 1


def _my_id():
    return 4 * lax.axis_index("x") + 2 * lax.axis_index("y") + lax.axis_index("c")


def _peers():
    x, y, c = lax.axis_index("x"), lax.axis_index("y"), lax.axis_index("c")
    out = []
    for k in range(1, N_DEV):
        px, py, pc = (1 - x if k & 4 else x, 1 - y if k & 2 else y, 1 - c if k & 1 else c)
        out.append(((px, py, pc), 4 * px + 2 * py + pc))
    return out


def _push_copies(src_refs, land_refs, send_sems, recv_sems, scatter, landing):
    me = _my_id()
    out = []
    for ti, (src, land) in enumerate(zip(src_refs, land_refs)):
        for k, (dev, pid) in enumerate(_peers()):
            out.append(pltpu.make_async_remote_copy(
                src_ref=src.at[pid] if scatter else src, dst_ref=land.at[pid if landing else me],
                send_sem=send_sems.at[ti * N_PEER + k], recv_sem=recv_sems.at[ti * N_PEER + k],
                device_id=dev, device_id_type=MESH_ID))
    return out


def _push_start(name, srcs, scatter, after=None):
    n = len(srcs)
    slot = lambda s: s.shape[1:] if scatter else s.shape
    lands = [lax.empty((N_DEV,) + slot(s), s.dtype) for s in srcs]
    n_after = 0 if after is None else 1

    def body(*refs):
        src_refs, land_refs = refs[:n], refs[n:2 * n]
        send_sems, recv_sems = refs[2 * n + n_after], refs[2 * n + n_after + 1]
        token = refs[-1]
        for cp in _push_copies(src_refs, land_refs, send_sems, recv_sems, scatter, False):
            cp.start()
        token[...] = jnp.zeros_like(token)

    hbm = lambda a: pltpu.with_memory_space_constraint(a, pltpu.HBM)
    operands = [hbm(a) for a in srcs + lands] + ([after] if n_after else [])
    res = pl.pallas_call(
        body, name=name,
        out_shape=[pltpu.SemaphoreType.DMA((n * N_PEER,)), pltpu.SemaphoreType.DMA((n * N_PEER,))]
        + [pltpu.HBM(a.shape, a.dtype) for a in srcs + lands] + [_sds((8, 128), F32)],
        in_specs=[HBM] * (2 * n) + [ANY] * n_after,
        out_specs=[SEM, SEM] + [HBM] * (2 * n) + [pl.BlockSpec(memory_space=pltpu.VMEM)],
        input_output_aliases={i: 2 + i for i in range(2 * n)},
        compiler_params=pltpu.CompilerParams(has_side_effects=EFFECT),
    )(*operands)
    return (res[0], res[1], list(res[2:2 + n]), list(res[2 + n:2 + 2 * n]), scatter), res[-1]


def _push_wait(name, handle, after):
    send_sems, recv_sems, srcs, lands, scatter = handle
    n = len(srcs)

    def body(*refs):
        src_refs, land_refs = refs[:n], refs[n:2 * n]
        s_sems, r_sems = refs[2 * n], refs[2 * n + 1]
        for cp in _push_copies(src_refs, land_refs, s_sems, r_sems, scatter, True):
            cp.wait_send()
            cp.wait_recv()

    res = pl.pallas_call(
        body, name=name,
        out_shape=[pltpu.HBM(a.shape, a.dtype) for a in srcs + lands],
        in_specs=[HBM] * (2 * n) + [SEM, SEM, ANY], out_specs=[HBM] * (2 * n),
        input_output_aliases={i: i for i in range(2 * n)},
        compiler_params=pltpu.CompilerParams(has_side_effects=EFFECT),
    )(*srcs, *lands, send_sems, recv_sems, after)
    return list(res[n:])


def _adamw(name, parts, w, m, v, own=None):
    r, c = w.shape
    br = _pick(r, 256, 16)
    has_own = own is not None

    def body(*refs):
        if has_own:
            p_ref, own_ref, w_ref, m_ref, v_ref, g_ref, d_ref, nm_ref, nv_ref = refs
            me = _my_id()
            mine = own_ref[...].astype(F32)
        else:
            p_ref, w_ref, m_ref, v_ref, g_ref, d_ref, nm_ref, nv_ref = refs
        g = None
        for k in range(N_DEV):
            t = p_ref[k].astype(F32)
            if has_own:
                t = jnp.where(me == k, mine, t)
            g = t if g is None else g + t
        mm = ADAM_B1 * m_ref[...] + (1.0 - ADAM_B1) * g
        vv = ADAM_B2 * v_ref[...] + (1.0 - ADAM_B2) * (g * g)
        m_hat = mm / (1.0 - ADAM_B1 ** ADAM_STEP)
        v_hat = vv / (1.0 - ADAM_B2 ** ADAM_STEP)
        g_ref[...] = g
        d_ref[...] = -ADAM_LR * (m_hat / (jnp.sqrt(v_hat) + ADAM_EPS) + ADAM_WD * w_ref[...])
        nm_ref[...] = mm
        nv_ref[...] = vv

    spec = _rows(br, c)
    out = (_sds((r, c), F32), spec)
    ins = [(parts, pl.BlockSpec((N_DEV, br, c), lambda i: (0, i, 0)))] + ([(own, spec)] if has_own else [])
    return _call(name, body, (r // br,), ins + [(w, spec), (m, spec), (v, spec)], [out] * 4, sem=("parallel",))


def _pad_head_cols(w, d):
    k = w.shape[0]
    return jnp.pad(w.reshape(k, HEADS, d), ((0, 0), (0, 0), (0, HP - d))).reshape(k, HW)


def _unpad_head_cols(wp, d):
    k = wp.shape[0]
    return wp.reshape(k, HEADS, HP)[:, :, :d].reshape(k, HEADS * d)


def _pad_head_rows(w, d):
    n = w.shape[1]
    return jnp.pad(w.reshape(HEADS, d, n), ((0, 0), (0, HP - d), (0, 0))).reshape(HW, n)


def _unpad_head_rows(wp, d):
    n = wp.shape[1]
    return wp.reshape(HEADS, HP, n)[:, :d, :].reshape(HEADS * d, n)


def _w_in_runs():
    nat = {}
    o = 0
    for nm, wd in (("q", Q_RANK), ("kv", KV_RANK), ("kr", ROPE), ("fq", FOX_W), ("fk", FOX_W), ("fv", FOX_W),
                   ("fl", HEADS), ("gate", 2 * D_MODEL)):
        nat[nm] = o
        o += wd
    runs = [(1, R_QLAT, nat["q"], Q_RANK, 1.0), (1, R_KVLAT, nat["kv"], KV_RANK, 1.0),
            (1, R_LAST + LANE_FL, nat["fl"], HEADS, 1.0), (1, R_LAST + LANE_PE, nat["kr"], ROPE, 1.0),
            (1, R_GATE, nat["gate"], 2 * D_MODEL, 1.0)]
    for grp, (nm, sc) in enumerate((("fq", FOX_SCALE), ("fk", 1.0), ("fv", 1.0))):
        runs += [(0, grp * HW + h * HP, nat[nm] + h * FOX_DIM, FOX_DIM, sc) for h in range(HEADS)]
    return runs


def _sharded_runs(runs, shard_cols):
    out = []
    for half, col, ncol, width, sc in runs:
        while width > 0:
            d, local = divmod(ncol, shard_cols)
            wd = min(width, shard_cols - local)
            out.append((half, col, d, local, wd, sc))
            col, ncol, width = col + wd, ncol + wd, width - wd
    return out


def _remap(name, srcs, out_shapes, moves):
    rows = srcs[0].shape[-2]
    br = _pick(rows, 256, 16)
    ns = len(srcs)

    def spec(shape):
        if len(shape) == 2:
            return pl.BlockSpec((br, shape[1]), lambda i: (i, 0))
        return pl.BlockSpec((shape[0], br, shape[2]), lambda i: (0, i, 0))

    def body(*refs):
        s_refs, o_refs = refs[:ns], refs[ns:]
        for o in o_refs:
            o[...] = jnp.zeros_like(o)
        for di, dl, dc, si, sl, sc0, wd, scale in moves:
            v = s_refs[si][:, sc0:sc0 + wd] if sl is None else s_refs[si][sl, :, sc0:sc0 + wd]
            if scale != 1.0:
                v = v * jnp.asarray(scale, v.dtype)
            v = v.astype(o_refs[di].dtype)
            if dl is None:
                o_refs[di][:, dc:dc + wd] = v
            else:
                o_refs[di][dl, :, dc:dc + wd] = v

    return _call(name, body, (rows // br,), [(a, spec(a.shape)) for a in srcs],
                 [(_sds(shape, dt), spec(shape)) for shape, dt in out_shapes], sem=("parallel",))


def _w_in_from_shards(g3):
    n, rows, c = g3.shape
    moves = [(half, None, col, 0, d, local, wd, sc) for half, col, d, local, wd, sc in _sharded_runs(_w_in_runs(), c)]
    return _remap("w_in_repack", [g3], [((rows, F_W), g3.dtype), ((rows, R_W), g3.dtype)], moves)


def _w_in_grad_to_shards(d_fused, d_rest, n, c):
    rows = d_fused.shape[0]
    moves = [(0, d, local, half, None, col, wd, sc) for half, col, d, local, wd, sc in _sharded_runs(_w_in_runs(), c)]
    return _remap("w_in_grad_unpack", [d_fused, d_rest], [((n, rows, c), d_fused.dtype)], moves)[0]


def _cols_from_shards(name, land, own):
    n, rows, c = land.shape
    br = _pick(rows, 256, 16)

    def body(land_ref, own_ref, o_ref):
        me = _my_id()
        for d in range(n):
            o_ref[:, c * d:c * (d + 1)] = jnp.where(me == d, own_ref[...], land_ref[d])

    return _call(name, body, (rows // br,),
                 [(land, pl.BlockSpec((n, br, c), lambda i: (0, i, 0))), (own, _rows(br, c))],
                 [(_sds((rows, n * c), land.dtype), _rows(br, n * c))], sem=("parallel",))[0]


def _cols_to_shards(name, full, n):
    rows, nc = full.shape
    c = nc // n
    return _remap(name, [full], [((n, rows, c), full.dtype)], [(0, d, 0, 0, None, c * d, c, 1.0) for d in range(n)])[0]


def _split_w_kv(w):
    k = w.shape[0]
    w3 = w.reshape(k, HEADS, NOPE + V_DIM)
    padl = lambda a: jnp.pad(a, ((0, 0), (0, 0), (0, HP - a.shape[-1]))).reshape(k, HW)
    return padl(w3[..., :NOPE]), padl(w3[..., NOPE:])


def _merge_w_kv(wk, wv):
    k = wk.shape[0]
    return jnp.concatenate([wk.reshape(k, HEADS, HP)[..., :NOPE], wv.reshape(k, HEADS, HP)[..., :V_DIM]],
                           axis=-1).reshape(k, HEADS * (NOPE + V_DIM))


class _NoComm:
    first_token = ()

    def late_weights(self, group, after):
        return {}

    def send(self, name, grads):
        return ()


def _local_step(x, tgt, p, comm=_NoComm()):
    seq = x.shape[0]
    r = -(-(N_META + seq) // ROW_ALIGN) * ROW_ALIGN
    cd = MXU_DTYPE
    p = dict(p)

    w_f, w_r = p["w_in"]
    w_q = _pad_head_cols(p["w_q_up"], QK_DIM)
    w_kv = jnp.concatenate(_split_w_kv(p["w_kv_up"]), axis=1)

    pos = jnp.arange(r, dtype=F32)
    inv_freq = ROPE_THETA ** (-jnp.arange(HALF, dtype=F32) / HALF)
    ang = pos[:, None] * inv_freq[None, :]
    cos_t = jnp.tile(jnp.cos(ang), (1, HP // HALF))
    sin_t = jnp.tile(jnp.sin(ang), (1, HP // HALF))
    bf_row = jnp.zeros((1, HP), F32).at[0, LANE_FL:LANE_FL + HEADS].set(p["b_forget"])

    h0, h0b = _ln_emb_fwd(x, p["meta_tokens"], p["ln_emb_g"], p["ln_emb_b"], r, after=comm.first_token)
    proj_f = _matmul("in_proj_f", h0b, w_f, out_dtype=cd)
    proj_r = _matmul("in_proj_r", h0b, w_r)
    ql = _rms_fwd("q_norm_fwd", proj_r, R_QLAT // Q_RANK, Q_RANK, p["q_norm_g"])
    kvl = _rms_fwd("kv_norm_fwd", proj_r, R_KVLAT // KV_RANK, KV_RANK, p["kv_norm_g"])
    q_raw = _matmul("q_up", ql, w_q)
    kv = _matmul("kv_up", kvl, w_kv, out_dtype=cd)
    q_mla, k_mla = _rope_fwd(q_raw, kv, proj_r, cos_t, sin_t)
    o_mla, o_mla_b, lse_mla = _attn_fwd("mla_fwd", (q_mla, 0), (k_mla, 0), (kv, 1))

    cum, cum_t = _forget_fwd(proj_r, bf_row)
    o_fox, o_fox_b, lse_fox = _attn_fwd("fox_fwd", (proj_f, 0), (proj_f, 1), (proj_f, 2), cum, cum_t)

    p.update(comm.late_weights("mix", o_fox_b))
    w_bm = _pad_head_rows(p["w_branch_mla"], V_DIM)
    w_bf = _pad_head_rows(p["w_branch_fox"], FOX_DIM)
    bm = _matmul("branch_mla", o_mla_b, w_bm, out_dtype=cd)
    bfx = _matmul("branch_fox", o_fox_b, w_bf, out_dtype=cd)
    merged = _gate_fwd(proj_r, p["b_gate"], bm, bfx)
    mix = _matmul("out_proj", merged, p["w_out"])
    h1, h1b = _ln_fwd("ln_mix_fwd", h0, mix, p["ln_mix_g"], p["ln_mix_b"])
    p.update(comm.late_weights("ffn", h1b))
    up = _matmul("ffn_up", h1b, p["w_ffn_up"], out_dtype=cd)
    act = _glu_fwd(up, p["conv_w"], p["conv_b"])
    f = _matmul("ffn_down", act, p["w_ffn_down"])
    loss = _ln_ffn_loss(h1, f, tgt, p["ln_ffn_g"], p["ln_ffn_b"])

    g = {}
    dz2, dz2b, g["ln_ffn_g"], g["ln_ffn_b"] = _ln_ffn_bwd(h1, f, tgt, p["ln_ffn_g"], p["ln_ffn_b"])
    d_act = _matmul("ffn_down_dx", dz2b, p["w_ffn_down"], tb=True, out_dtype=cd)
    g["w_ffn_down"] = _matmul("ffn_down_dw", act, dz2b, ta=True, out_dtype=cd)
    d_up, dcw, g["conv_b"] = _glu_bwd(up, p["conv_w"], p["conv_b"], d_act)
    g["conv_w"] = dcw[:3]
    dh1 = _matmul("ffn_up_dx", d_up, p["w_ffn_up"], tb=True, addend=dz2, alpha=ALPHA)
    g["w_ffn_up"] = _matmul("ffn_up_dw", h1b, d_up, ta=True, out_dtype=cd)
    sent = comm.send("ffn", {n: g[n] for n in ("w_ffn_down", "w_ffn_up", "conv_w")})
    dz1, dz1b, g["ln_mix_g"], g["ln_mix_b"] = _ln_bwd("ln_mix_bwd", h0, mix, dh1, p["ln_mix_g"], after=sent)
    dmerged = _matmul("out_proj_dx", dz1b, p["w_out"], tb=True, out_dtype=cd)
    g["w_out"] = _matmul("out_proj_dw", merged, dz1b, ta=True, out_dtype=cd)
    d_bm, d_bf, d_gl, g["b_gate"] = _gate_bwd(proj_r, p["b_gate"], bm, bfx, dmerged)
    do_mla_b = _matmul("branch_mla_dx", d_bm, w_bm, tb=True, out_dtype=cd)
    g["w_branch_mla"] = _unpad_head_rows(_matmul("branch_mla_dw", o_mla_b, d_bm, ta=True, out_dtype=cd), V_DIM)
    do_fox_b = _matmul("branch_fox_dx", d_bf, w_bf, tb=True, out_dtype=cd)
    g["w_branch_fox"] = _unpad_head_rows(_matmul("branch_fox_dw", o_fox_b, d_bf, ta=True, out_dtype=cd), FOX_DIM)

    sent = comm.send("mix", {n: g[n] for n in ("w_out", "w_branch_mla", "w_branch_fox")})
    dl_mla = _attn_delta("mla_delta", do_mla_b, o_mla, after=sent)
    dq_m, dk_m, dv_m = _attn_bwd("mla_bwd", (q_mla, 0), (k_mla, 0), (kv, 1), do_mla_b, lse_mla, dl_mla)
    dl_fox = _attn_delta("fox_delta", do_fox_b, o_fox)
    dfq, dfk, dfv, dcq, dck = _attn_bwd("fox_bwd", (proj_f, 0), (proj_f, 1), (proj_f, 2), do_fox_b, lse_fox, dl_fox,
                                        cum, cum_t, out_dtype=cd)
    dfl, dbf = _forget_bwd(proj_r, bf_row, dcq, dck)
    g["b_forget"] = dbf[:, LANE_FL:LANE_FL + HEADS]

    dq_b, dkv_b, dlast = _rope_bwd(dq_m, dk_m, dv_m, dfl, cos_t, sin_t)
    d_ql = _matmul("q_up_dx", dq_b, w_q, tb=True)
    d_kvl = _matmul("kv_up_dx", dkv_b, w_kv, tb=True)
    d_qlat, g["q_norm_g"] = _rms_bwd("q_norm_bwd", proj_r, R_QLAT // Q_RANK, Q_RANK, d_ql, p["q_norm_g"])
    d_kvlat, g["kv_norm_g"] = _rms_bwd("kv_norm_bwd", proj_r, R_KVLAT // KV_RANK, KV_RANK, d_kvl, p["kv_norm_g"])
    dproj_f = jnp.concatenate([dfq, dfk, dfv], axis=1)
    dproj_r = jnp.concatenate([d_qlat, d_kvlat, dlast, jnp.zeros((r, R_GATE - R_LAST - HP), cd), d_gl], axis=1)
    g["w_in"] = (_matmul("in_proj_f_dw", h0b, dproj_f, ta=True, out_dtype=cd),
                 _matmul("in_proj_r_dw", h0b, dproj_r, ta=True, out_dtype=cd))
    sent = comm.send("in", {"w_in": g["w_in"]})
    dh0 = _matmul("in_proj_f_dx", dproj_f, w_f, tb=True, addend=dz1, alpha=ALPHA, after=sent)
    g["w_q_up"] = _unpad_head_cols(_matmul("q_up_dw", ql, dq_b, ta=True, out_dtype=cd, after=sent), QK_DIM)
    dw_kv = _matmul("kv_up_dw", kvl, dkv_b, ta=True, out_dtype=cd, after=sent)
    g["w_kv_up"] = _merge_w_kv(dw_kv[:, :HW], dw_kv[:, HW:])
    sent = comm.send("qkv", {n: g[n] for n in ("w_q_up", "w_kv_up")})
    dh0 = _matmul("in_proj_r_dx", dproj_r, w_r, tb=True, addend=dh0, after=sent)
    grad_x, d_meta, g["ln_emb_g"], g["ln_emb_b"] = _ln_emb_bwd(x, p["meta_tokens"], dh0, p["ln_emb_g"])
    return loss, grad_x, d_meta, g


BIG = (("w_in", 1), ("w_q_up", 1), ("w_kv_up", 1), ("w_branch_mla", 1), ("w_branch_fox", 1), ("w_out", 0),
       ("w_ffn_up", 1), ("w_ffn_down", 0))
SMALL_SHARDED = (("meta_tokens", 1), ("conv_w", 1))
EARLY = ("w_in", "w_q_up", "w_kv_up", "meta_tokens", "conv_w")
LATE = {"mix": ("w_branch_mla", "w_branch_fox", "w_out"),
        "ffn": ("w_ffn_up", "w_ffn_down")}
REPLICATED = ("ln_emb_g", "ln_emb_b", "b_gate", "b_forget", "q_norm_g", "kv_norm_g", "ln_mix_g", "ln_mix_b",
              "conv_b", "ln_ffn_g", "ln_ffn_b")
PACK_COLS = 1024


def _pack(flat_list):
    cat = jnp.concatenate(flat_list)
    n = cat.shape[0]
    rows = -(-n // (8 * PACK_COLS)) * 8
    return jnp.pad(cat, (0, rows * PACK_COLS - n)).reshape(rows, PACK_COLS)


def _gathered_full(g3, axis):
    n, r, c = g3.shape
    if axis == 0:
        return g3.reshape(n * r, c)
    return g3.transpose(1, 0, 2).reshape(r, n * c)


def _shard_major(full, axis):
    r, c = full.shape
    if axis == 0:
        return full.reshape(N_DEV, r // N_DEV, c)
    return full.reshape(r, N_DEV, c // N_DEV).transpose(1, 0, 2)


def kernel(x, meta_tokens, ln_emb_g, ln_emb_b, w_in, b_gate, b_forget, q_norm_g, w_q_up, kv_norm_g, w_kv_up, w_branch_mla, w_branch_fox, w_out, ln_mix_g, ln_mix_b, w_ffn_up, conv_w, conv_b, w_ffn_down, ln_ffn_g, ln_ffn_b, loss_target, m_meta_tokens, m_ln_emb_g, m_ln_emb_b, m_w_in, m_b_gate, m_b_forget, m_q_norm_g, m_w_q_up, m_kv_norm_g, m_w_kv_up, m_w_branch_mla, m_w_branch_fox, m_w_out, m_ln_mix_g, m_ln_mix_b, m_w_ffn_up, m_conv_w, m_conv_b, m_w_ffn_down, m_ln_ffn_g, m_ln_ffn_b, v_meta_tokens, v_ln_emb_g, v_ln_emb_b, v_w_in, v_b_gate, v_b_forget, v_q_norm_g, v_w_q_up, v_kv_norm_g, v_w_kv_up, v_w_branch_mla, v_w_branch_fox, v_w_out, v_ln_mix_g, v_ln_mix_b, v_w_ffn_up, v_conv_w, v_conv_b, v_w_ffn_down, v_ln_ffn_g, v_ln_ffn_b):
    names = ("meta_tokens", "ln_emb_g", "ln_emb_b", "w_in", "b_gate", "b_forget", "q_norm_g", "w_q_up", "kv_norm_g",
             "w_kv_up", "w_branch_mla", "w_branch_fox", "w_out", "ln_mix_g", "ln_mix_b", "w_ffn_up", "conv_w", "conv_b",
             "w_ffn_down", "ln_ffn_g", "ln_ffn_b")
    w_args = (meta_tokens, ln_emb_g, ln_emb_b, w_in, b_gate, b_forget, q_norm_g, w_q_up, kv_norm_g, w_kv_up,
              w_branch_mla, w_branch_fox, w_out, ln_mix_g, ln_mix_b, w_ffn_up, conv_w, conv_b, w_ffn_down, ln_ffn_g, ln_ffn_b)
    m_args = (m_meta_tokens, m_ln_emb_g, m_ln_emb_b, m_w_in, m_b_gate, m_b_forget, m_q_norm_g, m_w_q_up, m_kv_norm_g,
              m_w_kv_up, m_w_branch_mla, m_w_branch_fox, m_w_out, m_ln_mix_g, m_ln_mix_b, m_w_ffn_up, m_conv_w, m_conv_b,
              m_w_ffn_down, m_ln_ffn_g, m_ln_ffn_b)
    v_args = (v_meta_tokens, v_ln_emb_g, v_ln_emb_b, v_w_in, v_b_gate, v_b_forget, v_q_norm_g, v_w_q_up, v_kv_norm_g,
              v_w_kv_up, v_w_branch_mla, v_w_branch_fox, v_w_out, v_ln_mix_g, v_ln_mix_b, v_w_ffn_up, v_conv_w, v_conv_b,
              v_w_ffn_down, v_ln_ffn_g, v_ln_ffn_b)
    as2d = lambda a: a.reshape((-1, a.shape[-1])) if a.ndim != 1 else a.reshape(1, -1)
    w = {n: as2d(a) for n, a in zip(names, w_args)}
    m = {n: as2d(a) for n, a in zip(names, m_args)}
    v = {n: as2d(a) for n, a in zip(names, v_args)}
    out_shape = {n: a.shape for n, a in zip(names, w_args)}

    axis_of = dict(BIG + SMALL_SHARDED)
    big = set(n for n, _ in BIG)
    wire = lambda n, a: a.astype(MXU_DTYPE) if n in big else a
    my_id = _my_id()

    early = _allgather("gather_early", [wire(n, w[n]) for n in EARLY])
    p = {n: _gathered_full(g3, axis_of[n]) for n, g3 in zip(EARLY, early) if n != "w_in"}
    p["w_in"] = _w_in_from_shards(early[EARLY.index("w_in")])
    for n in REPLICATED:
        p[n] = w[n].reshape(-1)
    late, tokens, prev = {}, [], early[0]
    for group, members in LATE.items():
        src = [wire(n, w[n]) for n in members]
        handle, token = _push_start("gather_" + group + "_start", src, False, after=prev)
        late[group] = (members, src, handle)
        tokens.append(token)
        prev = token
    sent = {}

    class Comm:
        first_token = tuple(tokens)

        def late_weights(self, group, after):
            members, src, handle = late[group]
            lands = _push_wait("gather_" + group + "_wait", handle, after)
            out = {}
            for n, own, land in zip(members, src, lands):
                if n == "w_ffn_up":
                    out[n] = _cols_from_shards(n + "_repack", land, own)
                else:
                    out[n] = _gathered_full(lax.dynamic_update_index_in_dim(land, own, my_id, 0), axis_of[n])
            return out

        def send(self, name, grads):
            names_ = tuple(grads)
            parts = []
            for n in names_:
                if n == "w_in":
                    parts.append(_w_in_grad_to_shards(*grads[n], N_DEV, w[n].shape[1]))
                elif n == "w_ffn_up":
                    parts.append(_cols_to_shards(n + "_grad_unpack", grads[n], N_DEV))
                else:
                    parts.append(_shard_major(grads[n], axis_of[n]).astype(MXU_DTYPE))
            handle, token = _push_start("send_" + name + "_start", parts, True)
            sent[name] = (names_, parts, handle)
            return (token,)

    loss_part, grad_x, d_meta, g = _local_step(x[0], loss_target[0], p, Comm())
    grad_x = grad_x[None]

    small = _pack([d_meta.reshape(-1)] + [g[n].reshape(-1) for n in REPLICATED] + [loss_part.reshape(-1)])
    small_handle, small_token = _push_start("send_small_start", [small], False)

    res = {}
    prev = small_token
    for name, (names_, parts, handle) in sent.items():
        lands = _push_wait("send_" + name + "_wait", handle, prev)
        for n, part, land in zip(names_, parts, lands):
            own = lax.dynamic_index_in_dim(part, my_id, axis=0, keepdims=False)
            res[n] = _adamw("adamw_" + n, land, w[n], m[n], v[n], own=own)
            prev = res[n][0]
    small_all = _push_wait("send_small_wait", small_handle, prev)[0]
    head = jnp.zeros((d_meta.size,), F32)
    rep_w = _pack([head] + [w[n].reshape(-1) for n in REPLICATED])
    rep_m = _pack([head] + [m[n].reshape(-1) for n in REPLICATED])
    rep_v = _pack([head] + [v[n].reshape(-1) for n in REPLICATED])
    rep_res = _adamw("adamw_replicated", small_all, rep_w, rep_m, rep_v, own=small)
    off = d_meta.size
    for n in REPLICATED:
        sz = w[n].size
        res[n] = tuple(a.reshape(-1)[off:off + sz] for a in rep_res)
        off += sz
    loss = rep_res[0].reshape(-1)[off]
    cols = w["meta_tokens"].shape[1]
    meta_rows = lambda a: a.reshape(a.shape[:-2] + (-1,))[..., :d_meta.size].reshape(a.shape[:-2] + d_meta.shape)
    my_cols = lambda a: lax.dynamic_slice_in_dim(a, my_id * cols, cols, axis=a.ndim - 1)
    res["meta_tokens"] = _adamw("adamw_meta_tokens", my_cols(meta_rows(small_all)), w["meta_tokens"],
                                m["meta_tokens"], v["meta_tokens"], own=my_cols(d_meta))

    outs = [loss, grad_x]
    for idx in range(4):
        outs += [res[n][idx].reshape(out_shape[n]) for n in names]
    return tuple(outs)
```

```python
import jax
import jax.numpy as jnp
from jax import lax
from jax.experimental import pallas as pl
from jax.experimental.pallas import tpu as pltpu

F32 = jnp.float32
BF16 = jnp.bfloat16
MXU_DTYPE = BF16

N_DEV = 8
N_META = 16
D_MODEL = 1024
HEADS = 8
Q_RANK = 384
KV_RANK = 128
NOPE = 64
ROPE = 32
HALF = ROPE // 2
QK_DIM = NOPE + ROPE
V_DIM = 64
FOX_DIM = 64
FOX_W = HEADS * FOX_DIM
D_FF = 2816
ROPE_THETA = 10000.0
LN_EPS = 1e-5
RMS_EPS = 1e-6
ALPHA = 2.0 ** 0.25
MLA_SCALE = QK_DIM ** -0.5
FOX_SCALE = FOX_DIM ** -0.5
NEG_INF = -1e30

HP = 128
HW = HEADS * HP
F_W = 3 * HW
R_QLAT = 0
R_KVLAT = Q_RANK
R_LAST = R_KVLAT + KV_RANK
R_GATE = D_MODEL
R_W = R_GATE + 2 * D_MODEL
LANE_FL = 0
LANE_PE = NOPE

ADAM_LR = 0.001
ADAM_B1 = 0.9
ADAM_B2 = 0.999
ADAM_EPS = 1e-08
ADAM_WD = 0.01
ADAM_STEP = 10

ROW_BLOCK = 256
ATT_TQ = 768
ATT_TK = 768
ATT_HEADS = 2
ROW_ALIGN = 768
MM_BLOCK_CAP = 1408
VMEM_LIMIT = 56 * 1024 * 1024
HIGHEST = lax.Precision.HIGHEST
NT = (((1,), (1,)), ((), ()))
TN = (((0,), (0,)), ((), ()))


def _params(sem=None):
    return pltpu.CompilerParams(dimension_semantics=sem, vmem_limit_bytes=VMEM_LIMIT)


def _call(name, body, grid, ins, outs, scratch=(), sem=None, after=()):
    n_in = len(ins)
    n_tok = len(after)

    def run(*refs):
        body(*refs[:n_in], *refs[n_in + n_tok:])

    tok_spec = pl.BlockSpec((8, 128), lambda *_: (0, 0))
    return pl.pallas_call(
        run, name=name, grid=grid,
        in_specs=[s for _, s in ins] + [tok_spec] * n_tok,
        out_specs=[s for _, s in outs],
        out_shape=[o for o, _ in outs],
        scratch_shapes=list(scratch),
        compiler_params=_params(sem),
    )(*[a for a, _ in ins], *after)


def _sds(shape, dtype):
    return jax.ShapeDtypeStruct(shape, dtype)


def _rows(br, c, cb=0):
    return pl.BlockSpec((br, c), lambda i: (i, cb))


def _whole(shape):
    n = len(shape)
    return pl.BlockSpec(shape, lambda i: (0,) * n)


def _pick(dim, cap, mult):
    best = None
    d = mult
    while d <= min(dim, cap):
        if dim % d == 0:
            best = d
        d += mult
    return best if best is not None else dim


def _hs(h):
    return slice(h * HP, (h + 1) * HP)


def _matmul(name, a, b, *, ta=False, tb=False, out_dtype=F32, addend=None, alpha=1.0, after=()):
    if ta:
        k, m = a.shape
    else:
        m, k = a.shape
    if tb:
        n, k2 = b.shape
    else:
        k2, n = b.shape
    assert k == k2, (name, a.shape, b.shape)
    bm = _pick(m, MM_BLOCK_CAP, 128 if ta else 16)
    bn = _pick(n, MM_BLOCK_CAP, 128)
    bk = _pick(k, MM_BLOCK_CAP, 128 if (not ta or tb) else 16)
    nk = k // bk
    dims = (((0 if ta else 1,), (1 if tb else 0,)), ((), ()))
    has_add = addend is not None

    def body(*refs):
        a_ref, b_ref = refs[:2]
        add_ref = refs[2] if has_add else None
        o_ref = refs[3 if has_add else 2]

        def finish(r):
            if has_add:
                r = r + alpha * add_ref[...]
            o_ref[...] = r.astype(o_ref.dtype)

        part = lax.dot_general(a_ref[...], b_ref[...], dims, preferred_element_type=F32)
        if nk == 1:
            finish(part)
            return
        acc_ref = refs[-1]
        kk = pl.program_id(2)

        @pl.when(kk == 0)
        def _():
            acc_ref[...] = part

        @pl.when(kk > 0)
        def _():
            acc_ref[...] += part

        @pl.when(kk == nk - 1)
        def _():
            finish(acc_ref[...])

    a_spec = pl.BlockSpec((bk, bm), lambda i, j, l: (l, i)) if ta else pl.BlockSpec((bm, bk), lambda i, j, l: (i, l))
    b_spec = pl.BlockSpec((bn, bk), lambda i, j, l: (j, l)) if tb else pl.BlockSpec((bk, bn), lambda i, j, l: (l, j))
    o_spec = pl.BlockSpec((bm, bn), lambda i, j, l: (i, j))
    ins = [(a, a_spec), (b, b_spec)]
    if has_add:
        ins.append((addend, o_spec))
    return _call(name, body, (m // bm, n // bn, nk), ins, [(_sds((m, n), out_dtype), o_spec)],
                 scratch=[pltpu.VMEM((bm, bn), F32)] if nk > 1 else [],
                 sem=("parallel", "parallel", "arbitrary"), after=after)[0]


def _ln_stats(z):
    mu = jnp.mean(z, axis=-1, keepdims=True)
    zc = z - mu
    var = jnp.mean(zc * zc, axis=-1, keepdims=True)
    rstd = lax.rsqrt(var + LN_EPS)
    return zc * rstd, rstd


def _ln_fwd(name, a, res, g, b, after=()):
    r, d = a.shape
    br = ROW_BLOCK
    has_res = res is not None

    def body(*refs):
        if has_res:
            a_ref, r_ref, g_ref, b_ref, y_ref, yb_ref = refs
            z = ALPHA * a_ref[...] + r_ref[...]
        else:
            a_ref, g_ref, b_ref, y_ref, yb_ref = refs
            z = a_ref[...]
        xhat, _ = _ln_stats(z)
        y = xhat * g_ref[...] + b_ref[...]
        y_ref[...] = y
        yb_ref[...] = y.astype(yb_ref.dtype)

    ins = [(a, _rows(br, d))]
    if has_res:
        ins.append((res, _rows(br, d)))
    ins += [(g.reshape(1, d), _whole((1, d))), (b.reshape(1, d), _whole((1, d)))]
    outs = [(_sds((r, d), F32), _rows(br, d)), (_sds((r, d), MXU_DTYPE), _rows(br, d))]
    return _call(name, body, (r // br,), ins, outs, sem=("parallel",), after=after)


def _ln_bwd(name, a, res, dy, g, after=()):
    r, d = a.shape
    br = ROW_BLOCK
    has_res = res is not None

    def body(*refs):
        if has_res:
            a_ref, r_ref, dy_ref, g_ref, dz_ref, dzb_ref, dg_ref, db_ref = refs
            z = ALPHA * a_ref[...] + r_ref[...]
        else:
            a_ref, dy_ref, g_ref, dz_ref, dzb_ref, dg_ref, db_ref = refs
            z = a_ref[...]
        xhat, rstd = _ln_stats(z)
        dyv = dy_ref[...]
        dyg = dyv * g_ref[...]
        m1 = jnp.mean(dyg, axis=-1, keepdims=True)
        m2 = jnp.mean(dyg * xhat, axis=-1, keepdims=True)
        dz = rstd * (dyg - m1 - xhat * m2)
        dz_ref[...] = dz
        dzb_ref[...] = dz.astype(dzb_ref.dtype)

        @pl.when(pl.program_id(0) == 0)
        def _():
            dg_ref[...] = jnp.zeros_like(dg_ref)
            db_ref[...] = jnp.zeros_like(db_ref)

        dg_ref[...] += jnp.sum(dyv * xhat, axis=0, keepdims=True)
        db_ref[...] += jnp.sum(dyv, axis=0, keepdims=True)

    ins = [(a, _rows(br, d))]
    if has_res:
        ins.append((res, _rows(br, d)))
    ins += [(dy, _rows(br, d)), (g.reshape(1, d), _whole((1, d)))]
    outs = [(_sds((r, d), F32), _rows(br, d)), (_sds((r, d), MXU_DTYPE), _rows(br, d)),
            (_sds((1, d), F32), _whole((1, d))), (_sds((1, d), F32), _whole((1, d)))]
    return _call(name, body, (r // br,), ins, outs, sem=("arbitrary",), after=after)


def _rms_fwd(name, proj, cb, width, g):
    r = proj.shape[0]
    br = ROW_BLOCK

    def body(x_ref, g_ref, y_ref):
        x = x_ref[...]
        rstd = lax.rsqrt(jnp.mean(x * x, axis=-1, keepdims=True) + RMS_EPS)
        y_ref[...] = (x * rstd * g_ref[...]).astype(y_ref.dtype)

    return _call(name, body, (r // br,), [(proj, _rows(br, width, cb)), (g.reshape(1, width), _whole((1, width)))],
                 [(_sds((r, width), MXU_DTYPE), _rows(br, width))], sem=("parallel",))[0]


def _rms_bwd(name, proj, cb, width, dy, g):
    r = proj.shape[0]
    br = ROW_BLOCK

    def body(x_ref, dy_ref, g_ref, dx_ref, dg_ref):
        x = x_ref[...]
        rstd = lax.rsqrt(jnp.mean(x * x, axis=-1, keepdims=True) + RMS_EPS)
        nrm = x * rstd
        dyv = dy_ref[...]
        dyg = dyv * g_ref[...]
        dx = rstd * (dyg - nrm * jnp.mean(dyg * nrm, axis=-1, keepdims=True))
        dx_ref[...] = dx.astype(dx_ref.dtype)

        @pl.when(pl.program_id(0) == 0)
        def _():
            dg_ref[...] = jnp.zeros_like(dg_ref)

        dg_ref[...] += jnp.sum(dyv * nrm, axis=0, keepdims=True)

    return _call(name, body, (r // br,),
                 [(proj, _rows(br, width, cb)), (dy, _rows(br, width)), (g.reshape(1, width), _whole((1, width)))],
                 [(_sds((r, width), MXU_DTYPE), _rows(br, width)), (_sds((1, width), F32), _whole((1, width)))],
                 sem=("arbitrary",))


def _lane_iota(shape):
    return lax.broadcasted_iota(jnp.int32, shape, 1)


def _rotary(t, c, s, lane, sign):
    second = pltpu.roll(t, HP - HALF, axis=1)
    first = pltpu.roll(t, HALF, axis=1)
    lo = (lane >= LANE_PE) & (lane < LANE_PE + HALF)
    hi = (lane >= LANE_PE + HALF) & (lane < LANE_PE + ROPE)
    return jnp.where(lo, t * c - sign * second * s, jnp.where(hi, t * c + sign * first * s, t))


def _rope_fwd(q_raw, k_part, proj_r, cos_t, sin_t):
    r = q_raw.shape[0]
    br = ROW_BLOCK

    def body(q_ref, k_ref, t_ref, c_ref, s_ref, qo_ref, ko_ref):
        c = c_ref[...]
        s = s_ref[...]
        lane = _lane_iota((br, HP))
        pe = (lane >= LANE_PE) & (lane < LANE_PE + ROPE)
        kp = jnp.where(pe, _rotary(t_ref[...], c, s, lane, 1.0), 0.0)
        for h in range(HEADS):
            qo_ref[:, _hs(h)] = (_rotary(q_ref[:, _hs(h)], c, s, lane, 1.0) * MLA_SCALE).astype(qo_ref.dtype)
            ko_ref[:, _hs(h)] = (k_ref[:, _hs(h)] + kp).astype(ko_ref.dtype)

    blk = _rows(br, HP)
    wide = _rows(br, HW)
    return _call("rope_fwd", body, (r // br,),
                 [(q_raw, wide), (k_part, wide), (proj_r, _rows(br, HP, R_LAST // HP)), (cos_t, blk), (sin_t, blk)],
                 [(_sds((r, HW), MXU_DTYPE), wide)] * 2, sem=("parallel",))


def _rope_bwd(dq, dk, dv, dfl, cos_t, sin_t):
    r = dq.shape[0]
    br = ROW_BLOCK

    def body(dq_ref, dk_ref, dv_ref, fl_ref, c_ref, s_ref, dqo_ref, dkv_ref, dl_ref):
        c = c_ref[...]
        s = s_ref[...]
        lane = _lane_iota((br, HP))
        pe = (lane >= LANE_PE) & (lane < LANE_PE + ROPE)
        acc = jnp.zeros((br, HP), F32)
        for h in range(HEADS):
            dqo_ref[:, _hs(h)] = (_rotary(dq_ref[:, _hs(h)], c, s, lane, -1.0) * MLA_SCALE).astype(dqo_ref.dtype)
            dkh = dk_ref[:, _hs(h)]
            acc = acc + dkh
            dkv_ref[:, _hs(h)] = dkh.astype(dkv_ref.dtype)
            dkv_ref[:, _hs(HEADS + h)] = dv_ref[:, _hs(h)].astype(dkv_ref.dtype)
        dl_ref[...] = (jnp.where(pe, _rotary(acc, c, s, lane, -1.0), 0.0) + fl_ref[...]).astype(dl_ref.dtype)

    blk = _rows(br, HP)
    wide = _rows(br, HW)
    return _call("rope_bwd", body, (r // br,),
                 [(dq, wide), (dk, wide), (dv, wide), (dfl, blk), (cos_t, blk), (sin_t, blk)],
                 [(_sds((r, HW), MXU_DTYPE), wide), (_sds((r, 2 * HW), MXU_DTYPE), _rows(br, 2 * HW)),
                  (_sds((r, HP), MXU_DTYPE), blk)],
                 sem=("parallel",))


def _log_sigmoid(x):
    return jnp.minimum(x, 0.0) - jnp.log(1.0 + jnp.exp(-jnp.abs(x)))


def _head_lane(x, h, lane):
    return jnp.sum(jnp.where(lane == h, x, 0.0), axis=1, keepdims=True)


def _forget_fwd(proj_r, bf_row):
    r = proj_r.shape[0]
    br = ROW_BLOCK

    def body(t_ref, b_ref, ob_ref, ot_ref, carry_ref):
        @pl.when(pl.program_id(0) == 0)
        def _():
            carry_ref[...] = jnp.zeros_like(carry_ref)

        x = t_ref[...] + b_ref[...]
        lane = _lane_iota(x.shape)
        lf = jnp.where((lane >= LANE_FL) & (lane < LANE_FL + HEADS), _log_sigmoid(x), 0.0)
        tri = (lax.broadcasted_iota(jnp.int32, (br, br), 0) >= lax.broadcasted_iota(jnp.int32, (br, br), 1)).astype(F32)
        cum = jnp.dot(tri, lf, precision=HIGHEST, preferred_element_type=F32) + carry_ref[0:1, :]
        for h in range(HEADS):
            ob_ref[:, _hs(h)] = jnp.broadcast_to(_head_lane(cum, LANE_FL + h, lane), (br, HP))
        ot_ref[...] = cum.T[LANE_FL:LANE_FL + HEADS, :]
        carry_ref[...] = jnp.broadcast_to(cum[br - 1:br, :], carry_ref.shape)

    return _call("forget_fwd", body, (r // br,),
                 [(proj_r, _rows(br, HP, R_LAST // HP)), (bf_row, _whole((1, HP)))],
                 [(_sds((r, HW), F32), _rows(br, HW)), (_sds((HEADS, r), F32), pl.BlockSpec((HEADS, br), lambda i: (0, i)))],
                 scratch=[pltpu.VMEM((8, HP), F32)], sem=("arbitrary",))


def _forget_bwd(proj_r, bf_row, dcq_t, dck_b):
    r = proj_r.shape[0]
    br = ROW_BLOCK
    nb = r // br

    def body(t_ref, b_ref, dcq_ref, dck_ref, o_ref, db_ref, carry_ref):
        @pl.when(pl.program_id(0) == 0)
        def _():
            carry_ref[...] = jnp.zeros_like(carry_ref)
            db_ref[...] = jnp.zeros_like(db_ref)

        lane = _lane_iota((br, HP))
        dc = jnp.concatenate([dcq_ref[...], jnp.zeros((HP - HEADS, br), F32)], axis=0).T
        for h in range(HEADS):
            dc = dc + jnp.where(lane == LANE_FL + h, dck_ref[:, h * HP:h * HP + 1], 0.0)
        triu = (lax.broadcasted_iota(jnp.int32, (br, br), 0) <= lax.broadcasted_iota(jnp.int32, (br, br), 1)).astype(F32)
        dlf = jnp.dot(triu, dc, precision=HIGHEST, preferred_element_type=F32) + carry_ref[0:1, :]
        carry_ref[...] = jnp.broadcast_to(dlf[0:1, :], carry_ref.shape)
        x = t_ref[...] + b_ref[...]
        dfl = jnp.where((lane >= LANE_FL) & (lane < LANE_FL + HEADS), dlf * jax.nn.sigmoid(-x), 0.0)
        o_ref[...] = dfl
        db_ref[...] += jnp.sum(dfl, axis=0, keepdims=True)

    rev = pl.BlockSpec((br, HP), lambda i: (nb - 1 - i, 0))
    return _call("forget_bwd", body, (nb,),
                 [(proj_r, pl.BlockSpec((br, HP), lambda i: (nb - 1 - i, R_LAST // HP))), (bf_row, _whole((1, HP))),
                  (dcq_t, pl.BlockSpec((HEADS, br), lambda i: (0, nb - 1 - i))),
                  (dck_b, pl.BlockSpec((br, HW), lambda i: (nb - 1 - i, 0)))],
                 [(_sds((r, HP), F32), rev), (_sds((1, HP), F32), _whole((1, HP)))],
                 scratch=[pltpu.VMEM((8, HP), F32)], sem=("arbitrary",))


def _gate_fwd(proj_r, b_gate, bm, bfx):
    r, d = bm.shape
    br = ROW_BLOCK
    cb = R_GATE // d

    def body(gm_ref, gf_ref, b1_ref, b2_ref, bm_ref, bf_ref, o_ref):
        g1 = jax.nn.sigmoid(gm_ref[...] + b1_ref[...])
        g2 = jax.nn.sigmoid(gf_ref[...] + b2_ref[...])
        o_ref[...] = (g1 * bm_ref[...].astype(F32) + g2 * bf_ref[...].astype(F32)).astype(o_ref.dtype)

    b1 = b_gate[:d].reshape(1, d)
    b2 = b_gate[d:].reshape(1, d)
    return _call("gate_fwd", body, (r // br,),
                 [(proj_r, _rows(br, d, cb)), (proj_r, _rows(br, d, cb + 1)), (b1, _whole((1, d))), (b2, _whole((1, d))),
                  (bm, _rows(br, d)), (bfx, _rows(br, d))],
                 [(_sds((r, d), MXU_DTYPE), _rows(br, d))], sem=("parallel",))[0]


def _gate_bwd(proj_r, b_gate, bm, bfx, dmerged):
    r, d = bm.shape
    br = ROW_BLOCK
    cb = R_GATE // d

    def body(gm_ref, gf_ref, b1_ref, b2_ref, bm_ref, bf_ref, dm_ref, dbm_ref, dbf_ref, dgl_ref, dbg_ref):
        g1 = jax.nn.sigmoid(gm_ref[...] + b1_ref[...])
        g2 = jax.nn.sigmoid(gf_ref[...] + b2_ref[...])
        dm = dm_ref[...].astype(F32)
        dbm_ref[...] = (dm * g1).astype(dbm_ref.dtype)
        dbf_ref[...] = (dm * g2).astype(dbf_ref.dtype)
        dl1 = dm * bm_ref[...].astype(F32) * (g1 * (1.0 - g1))
        dl2 = dm * bf_ref[...].astype(F32) * (g2 * (1.0 - g2))
        dgl_ref[:, 0:d] = dl1.astype(dgl_ref.dtype)
        dgl_ref[:, d:2 * d] = dl2.astype(dgl_ref.dtype)

        @pl.when(pl.program_id(0) == 0)
        def _():
            dbg_ref[...] = jnp.zeros_like(dbg_ref)

        dbg_ref[:, 0:d] += jnp.sum(dl1, axis=0, keepdims=True)
        dbg_ref[:, d:2 * d] += jnp.sum(dl2, axis=0, keepdims=True)

    b1 = b_gate[:d].reshape(1, d)
    b2 = b_gate[d:].reshape(1, d)
    return _call("gate_bwd", body, (r // br,),
                 [(proj_r, _rows(br, d, cb)), (proj_r, _rows(br, d, cb + 1)), (b1, _whole((1, d))), (b2, _whole((1, d))),
                  (bm, _rows(br, d)), (bfx, _rows(br, d)), (dmerged, _rows(br, d))],
                 [(_sds((r, d), MXU_DTYPE), _rows(br, d)), (_sds((r, d), MXU_DTYPE), _rows(br, d)),
                  (_sds((r, 2 * d), MXU_DTYPE), _rows(br, 2 * d)), (_sds((1, 2 * d), F32), _whole((1, 2 * d)))],
                 sem=("arbitrary",))


HALO = 16
GLU_BWD_BLOCK = 128


def _conv_taps(gp, halo, first_block):
    halo = jnp.where(first_block, 0.0, halo.astype(F32))
    rid = lax.broadcasted_iota(jnp.int32, gp.shape, 0)
    last, prev = halo[HALO - 1:HALO, :], halo[HALO - 2:HALO - 1, :]
    g1 = jnp.where(rid == 0, last, pltpu.roll(gp, 1, axis=0))
    g2 = jnp.where(rid == 0, prev, jnp.where(rid == 1, last, pltpu.roll(gp, 2, axis=0)))
    return g1, g2


def _prev_halo(br, c):
    return pl.BlockSpec((HALO, c), lambda i: (jnp.maximum(i * (br // HALO) - 1, 0), 0))


def _glu_fwd(up, conv_w, conv_b):
    r = up.shape[0]
    c = D_FF
    br = ROW_BLOCK

    def body(gp_ref, halo_ref, val_ref, w_ref, b_ref, o_ref):
        gp = gp_ref[...].astype(F32)
        g1, g2 = _conv_taps(gp, halo_ref[...], pl.program_id(0) == 0)
        gate = w_ref[0:1, :] * g2 + w_ref[1:2, :] * g1 + w_ref[2:3, :] * gp + b_ref[...]
        o_ref[...] = (gate * jax.nn.sigmoid(gate) * val_ref[...].astype(F32)).astype(o_ref.dtype)

    return _call("glu_fwd", body, (r // br,),
                 [(up, _rows(br, c, 0)), (up, _prev_halo(br, c)), (up, _rows(br, c, 1)),
                  (conv_w, _whole((3, c))), (conv_b.reshape(1, c), _whole((1, c)))],
                 [(_sds((r, c), MXU_DTYPE), _rows(br, c))], sem=("parallel",))[0]


def _glu_bwd(up, conv_w, conv_b, d_act):
    r = up.shape[0]
    c = D_FF
    br = GLU_BWD_BLOCK
    nb = r // br

    def body(gp_ref, halo_ref, val_ref, da_ref, gpn_ref, valn_ref, dan_ref, w_ref, b_ref, o_ref, dw_ref, db_ref):
        i = pl.program_id(0)
        w0, w1, w2, bias = w_ref[0:1, :], w_ref[1:2, :], w_ref[2:3, :], b_ref[...]

        def d_gate(gp, g1, g2, val, da):
            gate = w0 * g2 + w1 * g1 + w2 * gp + bias
            sg = jax.nn.sigmoid(gate)
            return da * val * (sg * (1.0 + gate * (1.0 - sg))), da * (gate * sg)

        gp = gp_ref[...].astype(F32)
        g1, g2 = _conv_taps(gp, halo_ref[...], i == 0)
        dg, dv = d_gate(gp, g1, g2, val_ref[...].astype(F32), da_ref[...].astype(F32))
        gpn = gpn_ref[...].astype(F32)
        g1n, g2n = _conv_taps(gpn, gp[br - HALO:, :], False)
        dgn, _ = d_gate(gpn, g1n, g2n, valn_ref[...].astype(F32), dan_ref[...].astype(F32))
        dgn = jnp.where(i == nb - 1, 0.0, dgn)
        rid = lax.broadcasted_iota(jnp.int32, dg.shape, 0)
        u1 = jnp.where(rid == br - 1, dgn[0:1, :], pltpu.roll(dg, br - 1, axis=0))
        u2 = jnp.where(rid == br - 1, dgn[1:2, :], jnp.where(rid == br - 2, dgn[0:1, :], pltpu.roll(dg, br - 2, axis=0)))
        o_ref[:, 0:c] = (w2 * dg + w1 * u1 + w0 * u2).astype(o_ref.dtype)
        o_ref[:, c:2 * c] = dv.astype(o_ref.dtype)

        @pl.when(i == 0)
        def _():
            dw_ref[...] = jnp.zeros_like(dw_ref)
            db_ref[...] = jnp.zeros_like(db_ref)

        dw_ref[0:1, :] += jnp.sum(dg * g2, axis=0, keepdims=True)
        dw_ref[1:2, :] += jnp.sum(dg * g1, axis=0, keepdims=True)
        dw_ref[2:3, :] += jnp.sum(dg * gp, axis=0, keepdims=True)
        db_ref[...] += jnp.sum(dg, axis=0, keepdims=True)

    nxt = lambda cb: pl.BlockSpec((HALO, c), lambda i: (jnp.minimum((i + 1) * (br // HALO), r // HALO - 1), cb))
    return _call("glu_bwd", body, (nb,),
                 [(up, _rows(br, c, 0)), (up, _prev_halo(br, c)), (up, _rows(br, c, 1)), (d_act, _rows(br, c)),
                  (up, nxt(0)), (up, nxt(1)), (d_act, nxt(0)),
                  (conv_w, _whole((3, c))), (conv_b.reshape(1, c), _whole((1, c)))],
                 [(_sds((r, 2 * c), MXU_DTYPE), _rows(br, 2 * c)),
                  (_sds((8, c), F32), _whole((8, c))), (_sds((1, c), F32), _whole((1, c)))],
                 sem=("arbitrary",))


def _token_specs(seq, d):
    br = ROW_BLOCK
    nxb = seq // br
    main = pl.BlockSpec((br, d), lambda i: (jnp.minimum(i, nxb - 1), 0))
    tail = pl.BlockSpec((N_META, d), lambda i: (jnp.clip(i * (br // N_META) - 1, 0, seq // N_META - 1), 0))
    return main, tail


def _padded_block(main_ref, tail_ref, first, seq):
    br = ROW_BLOCK
    i = pl.program_id(0)
    nxb = seq // br
    main = jnp.where(i < nxb, main_ref[...], 0.0)
    head = jnp.where(i == 0, first, jnp.where(i <= nxb, tail_ref[...], 0.0))
    return jnp.concatenate([head, main[:br - N_META]], axis=0)


def _ln_emb_fwd(x, meta, g, b, rows, after=()):
    seq, d = x.shape
    br = ROW_BLOCK
    assert seq % br == 0 and br % N_META == 0 and rows % br == 0

    def body(x_ref, tail_ref, meta_ref, g_ref, b_ref, y_ref, yb_ref):
        z = _padded_block(x_ref, tail_ref, meta_ref[...], seq)
        xhat, _ = _ln_stats(z)
        y = xhat * g_ref[...] + b_ref[...]
        y_ref[...] = y
        yb_ref[...] = y.astype(yb_ref.dtype)

    main, tail = _token_specs(seq, d)
    return _call("ln_emb_fwd", body, (rows // br,),
                 [(x, main), (x, tail), (meta, _whole((N_META, d))), (g.reshape(1, d), _whole((1, d))),
                  (b.reshape(1, d), _whole((1, d)))],
                 [(_sds((rows, d), F32), _rows(br, d)), (_sds((rows, d), MXU_DTYPE), _rows(br, d))],
                 sem=("parallel",), after=after)


def _ln_emb_bwd(x, meta, dh0, g):
    seq, d = x.shape
    br = ROW_BLOCK
    step = br // N_META

    def ln_bwd(z, dy, gv):
        xhat, rstd = _ln_stats(z)
        dyg = dy * gv
        m1 = jnp.mean(dyg, axis=-1, keepdims=True)
        m2 = jnp.mean(dyg * xhat, axis=-1, keepdims=True)
        dz = rstd * (dyg - m1 - xhat * m2)
        return dz, jnp.sum(dy * xhat, axis=0, keepdims=True), jnp.sum(dy, axis=0, keepdims=True)

    def body(x_ref, dh_ref, nxt_ref, meta_ref, top_ref, g_ref, dx_ref, dm_ref, dg_ref, db_ref):
        gv = g_ref[...]
        dy = jnp.concatenate([dh_ref[N_META:, :], nxt_ref[...]], axis=0)
        dz, dg, db = ln_bwd(x_ref[...], dy, gv)
        dx_ref[...] = dz

        @pl.when(pl.program_id(0) == 0)
        def _():
            dzm, dgm, dbm = ln_bwd(meta_ref[...], top_ref[...], gv)
            dm_ref[...] = dzm
            dg_ref[...] = dgm
            db_ref[...] = dbm

        dg_ref[...] += dg
        db_ref[...] += db

    small = _whole((N_META, d))
    return _call("ln_emb_bwd", body, (seq // br,),
                 [(x, _rows(br, d)), (dh0, _rows(br, d)), (dh0, pl.BlockSpec((N_META, d), lambda i: ((i + 1) * step, 0))),
                  (meta, small), (dh0, small), (g.reshape(1, d), _whole((1, d)))],
                 [(_sds((seq, d), F32), _rows(br, d)), (_sds((N_META, d), F32), small),
                  (_sds((1, d), F32), _whole((1, d))), (_sds((1, d), F32), _whole((1, d)))], sem=("arbitrary",))


def _ln_ffn_loss(h1, f, tgt, g, b):
    r, d = h1.shape
    seq = tgt.shape[0]
    br = ROW_BLOCK

    def body(a_ref, r_ref, t_ref, tail_ref, g_ref, b_ref, l_ref):
        err = _loss_err(a_ref, r_ref, t_ref, tail_ref, g_ref, b_ref, seq)[0]

        @pl.when(pl.program_id(0) == 0)
        def _():
            l_ref[...] = jnp.zeros_like(l_ref)

        l_ref[...] += jnp.sum(jnp.sum(err * err, axis=1, keepdims=True), axis=0, keepdims=True) * (0.5 / d)

    main, tail = _token_specs(seq, d)
    return _call("ln_ffn_loss", body, (r // br,),
                 [(h1, _rows(br, d)), (f, _rows(br, d)), (tgt, main), (tgt, tail),
                  (g.reshape(1, d), _whole((1, d))), (b.reshape(1, d), _whole((1, d)))],
                 [(_sds((1, 1), F32), _whole((1, 1)))], sem=("arbitrary",))[0]


def _loss_err(a_ref, r_ref, t_ref, tail_ref, g_ref, b_ref, seq):
    br, d = a_ref.shape
    xhat, rstd = _ln_stats(ALPHA * a_ref[...] + r_ref[...])
    y = xhat * g_ref[...] + b_ref[...]
    t = _padded_block(t_ref, tail_ref, jnp.zeros((N_META, d), F32), seq)
    rid = lax.broadcasted_iota(jnp.int32, (br, d), 0) + pl.program_id(0) * br
    valid = (rid >= N_META) & (rid < N_META + seq)
    return jnp.where(valid, y - t, 0.0), xhat, rstd


def _ln_ffn_bwd(h1, f, tgt, g, b):
    r, d = h1.shape
    seq = tgt.shape[0]
    br = ROW_BLOCK

    def body(a_ref, r_ref, t_ref, tail_ref, g_ref, b_ref, dz_ref, dzb_ref, dg_ref, db_ref):
        err, xhat, rstd = _loss_err(a_ref, r_ref, t_ref, tail_ref, g_ref, b_ref, seq)
        dyv = err * (1.0 / d)
        dyg = dyv * g_ref[...]
        m1 = jnp.mean(dyg, axis=-1, keepdims=True)
        m2 = jnp.mean(dyg * xhat, axis=-1, keepdims=True)
        dz = rstd * (dyg - m1 - xhat * m2)
        dz_ref[...] = dz
        dzb_ref[...] = dz.astype(dzb_ref.dtype)

        @pl.when(pl.program_id(0) == 0)
        def _():
            dg_ref[...] = jnp.zeros_like(dg_ref)
            db_ref[...] = jnp.zeros_like(db_ref)

        dg_ref[...] += jnp.sum(dyv * xhat, axis=0, keepdims=True)
        db_ref[...] += jnp.sum(dyv, axis=0, keepdims=True)

    main, tail = _token_specs(seq, d)
    return _call("ln_ffn_bwd", body, (r // br,),
                 [(h1, _rows(br, d)), (f, _rows(br, d)), (tgt, main), (tgt, tail),
                  (g.reshape(1, d), _whole((1, d))), (b.reshape(1, d), _whole((1, d)))],
                 [(_sds((r, d), F32), _rows(br, d)), (_sds((r, d), MXU_DTYPE), _rows(br, d)),
                  (_sds((1, d), F32), _whole((1, d))), (_sds((1, d), F32), _whole((1, d)))], sem=("arbitrary",))


def _attn_fwd(name, q, k, v, cum_b=None, cum_t=None):
    (qa, qg), (ka, kg), (va, vg) = q, k, v
    r = qa.shape[0]
    tq, tk = ATT_TQ, ATT_TK
    nq, nk = r // tq, r // tk
    bias = cum_b is not None

    def body(*refs):
        if bias:
            q_ref, k_ref, vt_ref, cb_ref, ct_ref, o_ref, ob_ref, lse_ref = refs
        else:
            q_ref, k_ref, vt_ref, o_ref, ob_ref, lse_ref = refs
        i = pl.program_id(1)
        qs = [q_ref[:, _hs(hh)] for hh in range(hg)]
        cqs = [ct_ref[hh] for hh in range(hg)] if bias else None
        diff = lax.broadcasted_iota(jnp.int32, (tk, tq), 0) - lax.broadcasted_iota(jnp.int32, (tk, tq), 1)

        def step(j, carry, masked):
            keys = pl.ds(pl.multiple_of(j * tk, tk), tk)
            out = []
            for hh in range(hg):
                m, l, acc = carry[hh]
                kt = k_ref[keys, _hs(hh)]
                s = lax.dot_general(kt, qs[hh], NT, preferred_element_type=F32)
                if bias:
                    s = s + (cqs[hh] - cb_ref[keys, hh * HP:hh * HP + 1])
                if masked:
                    s = jnp.where(diff <= i * tq - j * tk, s, NEG_INF)
                m_new = jnp.maximum(m, jnp.max(s, axis=0, keepdims=True))
                p = jnp.exp(s - m_new)
                a = jnp.exp(m - m_new)
                l = a * l + jnp.sum(p, axis=0, keepdims=True)
                acc = a * acc + jnp.dot(vt_ref[j, _hs(hh), :], p.astype(kt.dtype), preferred_element_type=F32)
                out.append((m_new, l, acc))
            return tuple(out)

        n_clear = (i * tq + 1) // tk
        n_all = ((i + 1) * tq - 1) // tk + 1
        carry = tuple((jnp.full((1, tq), NEG_INF, F32), jnp.zeros((1, tq), F32), jnp.zeros((HP, tq), F32))
                      for _ in range(hg))
        carry = lax.fori_loop(0, n_clear, lambda j, c: step(j, c, False), carry)
        carry = lax.fori_loop(n_clear, n_all, lambda j, c: step(j, c, True), carry)
        for hh in range(hg):
            m, l, acc = carry[hh]
            o = (acc / l).T
            o_ref[:, _hs(hh)] = o
            ob_ref[:, _hs(hh)] = o.astype(ob_ref.dtype)
            lse_ref[hh] = m + jnp.log(l)

    hg = ATT_HEADS
    w = hg * HP
    gpw = HW // w
    tile = lambda g: pl.BlockSpec((tq, w), lambda h, i: (i, g * gpw + h))
    res = lambda g: pl.BlockSpec((r, w), lambda h, i: (0, g * gpw + h))
    v_t = _key_tiles_transposed(name + "_vt", va, vg)
    ins = [(qa, tile(qg)), (ka, res(kg)), (v_t, pl.BlockSpec((nk, w, tk), lambda h, i: (0, h, 0)))]
    if bias:
        ins += [(cum_b, res(0)),
                (cum_t.reshape(HEADS, nq, 1, tq), pl.BlockSpec((hg, None, 1, tq), lambda h, i: (h, i, 0, 0)))]
    outs = [(_sds((r, HW), F32), tile(0)), (_sds((r, HW), MXU_DTYPE), tile(0)),
            (_sds((HEADS, nq, 1, tq), F32), pl.BlockSpec((hg, None, 1, tq), lambda h, i: (h, i, 0, 0)))]
    o, ob, lse = _call(name, body, (gpw, nq), ins, outs, sem=("parallel", "parallel"))
    return o, ob, lse.reshape(HEADS, r)


def _key_tiles_transposed(name, a, group):
    r = a.shape[0]
    tk = ATT_TK

    def body(x_ref, o_ref):
        for h in range(HEADS):
            o_ref[_hs(h), :] = x_ref[:, _hs(h)].astype(F32).T.astype(o_ref.dtype)

    return _call(name, body, (r // tk,),
                 [(a, pl.BlockSpec((tk, HW), lambda j: (j, group)))],
                 [(_sds((r // tk, HW, tk), a.dtype), pl.BlockSpec((None, HW, tk), lambda j: (j, 0, 0)))],
                 sem=("parallel",))[0]


def _attn_delta(name, do_b, o, after=()):
    r = do_b.shape[0]
    br = ROW_BLOCK

    def body(do_ref, o_ref, d_ref):
        lane = _lane_iota((br, HP))
        d = jnp.zeros((br, HP), F32)
        for h in range(HEADS):
            dh = do_ref[:, _hs(h)].astype(F32)
            d = jnp.where(lane == h, jnp.sum(dh * o_ref[:, _hs(h)], axis=1, keepdims=True), d)
        d_ref[...] = d.T[0:HEADS, :]

    wide = _rows(br, HW)
    return _call(name, body, (r // br,), [(do_b, wide), (o, wide)],
                 [(_sds((HEADS, r), F32), pl.BlockSpec((HEADS, br), lambda i: (0, i)))],
                 sem=("parallel",), after=after)[0]


def _attn_bwd(name, q, k, v, do_b, lse_t, delta_t, cum_b=None, cum_t=None, out_dtype=F32):
    (qa, qg), (ka, kg), (va, vg) = q, k, v
    r = qa.shape[0]
    tq, tk = ATT_TQ, ATT_TK
    nq, nk = r // tq, r // tk
    bias = cum_b is not None

    def body(*refs):
        if bias:
            (q_ref, k_ref, v_ref, do_ref, lse_ref, dl_ref, cb_ref, ct_ref,
             dq_ref, dk_ref, dv_ref, dcq_ref, dck_ref, dqt_ref) = refs
        else:
            q_ref, k_ref, v_ref, do_ref, lse_ref, dl_ref, dq_ref, dk_ref, dv_ref, dqt_ref = refs
        j = pl.program_id(1)

        @pl.when(j == 0)
        def _():
            dqt_ref[...] = jnp.zeros_like(dqt_ref)
            if bias:
                dcq_ref[...] = jnp.zeros_like(dcq_ref)

        kts = [k_ref[:, _hs(hh)] for hh in range(hg)]
        vts = [v_ref[:, _hs(hh)] for hh in range(hg)]
        k_trs = [kt.astype(F32).T.astype(kt.dtype) for kt in kts]
        cks = [cb_ref[:, hh * HP:hh * HP + 1] for hh in range(hg)] if bias else None
        diff = lax.broadcasted_iota(jnp.int32, (tk, tq), 0) - lax.broadcasted_iota(jnp.int32, (tk, tq), 1)

        def step(i, carry, masked):
            rows = pl.ds(pl.multiple_of(i * tq, tq), tq)
            out = []
            for hh in range(hg):
                dk_acc, dv_acc, dck_acc = carry[hh]
                qt = q_ref[rows, _hs(hh)]
                dot = do_ref[rows, _hs(hh)]
                s = lax.dot_general(kts[hh], qt, NT, preferred_element_type=F32)
                if bias:
                    s = s + (ct_ref[hh, i] - cks[hh])
                if masked:
                    s = jnp.where(diff <= i * tq - j * tk, s, NEG_INF)
                p = jnp.exp(s - lse_ref[hh, i])
                dp = lax.dot_general(vts[hh], dot, NT, preferred_element_type=F32)
                ds = p * (dp - dl_ref[hh, i])
                pb = p.astype(dot.dtype)
                dsb = ds.astype(qt.dtype)
                dv_acc = dv_acc + jnp.dot(pb, dot, preferred_element_type=F32)
                dk_acc = dk_acc + jnp.dot(dsb, qt, preferred_element_type=F32)
                dqt_ref[hh, i] += jnp.dot(k_trs[hh], dsb, preferred_element_type=F32)
                if bias:
                    dcq_ref[hh, i] += jnp.sum(ds, axis=0, keepdims=True)
                    dck_acc = dck_acc - jnp.sum(ds, axis=1, keepdims=True)
                out.append((dk_acc, dv_acc, dck_acc))
            return tuple(out)

        i_first = (j * tk) // tq
        i_clear = jnp.minimum(((j + 1) * tk + tq - 2) // tq, nq)
        carry = tuple((jnp.zeros((tk, HP), F32), jnp.zeros((tk, HP), F32), jnp.zeros((tk, 1), F32)) for _ in range(hg))
        carry = lax.fori_loop(i_first, i_clear, lambda i, c: step(i, c, True), carry)
        carry = lax.fori_loop(i_clear, nq, lambda i, c: step(i, c, False), carry)
        for hh in range(hg):
            dk_acc, dv_acc, dck_acc = carry[hh]
            dk_ref[:, _hs(hh)] = dk_acc.astype(dk_ref.dtype)
            dv_ref[:, _hs(hh)] = dv_acc.astype(dv_ref.dtype)
            if bias:
                dck_ref[:, _hs(hh)] = jnp.broadcast_to(dck_acc, (tk, HP))

        @pl.when(j == nk - 1)
        def _():
            for hh in range(hg):
                for i in range(nq):
                    dq_ref[i * tq:(i + 1) * tq, _hs(hh)] = dqt_ref[hh, i].T.astype(dq_ref.dtype)

    hg = ATT_HEADS
    w = hg * HP
    gpw = HW // w
    res = lambda g: pl.BlockSpec((r, w), lambda h, j: (0, g * gpw + h))
    tile = lambda g: pl.BlockSpec((tk, w), lambda h, j: (j, g * gpw + h))
    rowv = pl.BlockSpec((hg, nq, 1, tq), lambda h, j: (h, 0, 0, 0))
    as_rows = lambda a: a.reshape(HEADS, nq, 1, tq)
    ins = [(qa, res(qg)), (ka, tile(kg)), (va, tile(vg)), (do_b, res(0)), (as_rows(lse_t), rowv), (as_rows(delta_t), rowv)]
    outs = [(_sds((r, HW), out_dtype), res(0)), (_sds((r, HW), out_dtype), tile(0)), (_sds((r, HW), out_dtype), tile(0))]
    if bias:
        ins += [(cum_b, tile(0)), (as_rows(cum_t), rowv)]
        outs += [(_sds((HEADS, nq, 1, tq), F32), rowv), (_sds((r, HW), F32), tile(0))]
    res_out = _call(name, body, (gpw, nk), ins, outs, scratch=[pltpu.VMEM((hg, nq, HP, tq), F32)],
                    sem=("parallel", "arbitrary"))
    if bias:
        dq, dk, dv, dcq, dck = res_out
        return dq, dk, dv, dcq.reshape(HEADS, r), dck
    return res_out


MESH_ID = pl.DeviceIdType.MESH
ANY = pl.BlockSpec(memory_space=pl.ANY)


def _allgather(name, shards):
    n = len(shards)

    def body(*refs):
        x_refs, out_refs = refs[:n], refs[n:2 * n]
        send_sems, recv_sems, local_sems = refs[2 * n:]
        x, y, c = lax.axis_index("x"), lax.axis_index("y"), lax.axis_index("c")
        me, sibling = (x, y, c), (x, y, 1 - c)
        chips = [(1 - x, y), (x, 1 - y), (1 - x, 1 - y)]

        def slot(ti, px, py, pc):
            return out_refs[ti].at[4 * px + 2 * py + pc]

        def copy(ti, k, block, to, src=None):
            return pltpu.make_async_remote_copy(
                src_ref=slot(ti, *block) if src is None else src, dst_ref=slot(ti, *block),
                send_sem=send_sems.at[ti, k], recv_sem=recv_sems.at[ti, k], device_id=to, device_id_type=MESH_ID)

        mine = [pltpu.make_async_copy(x_refs[ti], slot(ti, *me), local_sems.at[ti]) for ti in range(n)]
        for cp in mine:
            cp.start()
        started = []
        for ti in range(n):
            first = [copy(ti, 0, me, sibling, src=x_refs[ti])]
            first += [copy(ti, 1 + j, me, (*chip, c), src=x_refs[ti]) for j, chip in enumerate(chips)]
            for cp in first:
                cp.start()
            started += first
        for ti in range(n):
            for j, chip in enumerate(chips):
                copy(ti, 1 + j, (*chip, c), me).wait_recv()
                fwd = copy(ti, 4 + j, (*chip, c), sibling)
                fwd.start()
                started.append(fwd)
        for ti in range(n):
            copy(ti, 0, sibling, me).wait_recv()
            for j, chip in enumerate(chips):
                copy(ti, 4 + j, (*chip, 1 - c), me).wait_recv()
        for cp in started:
            cp.wait_send()
        for cp in mine:
            cp.wait()

    return pl.pallas_call(
        body, name=name, out_shape=[_sds((N_DEV,) + s.shape, s.dtype) for s in shards],
        in_specs=[ANY] * n, out_specs=[ANY] * n,
        scratch_shapes=[pltpu.SemaphoreType.DMA((n, 7)), pltpu.SemaphoreType.DMA((n, 7)), pltpu.SemaphoreType.DMA((n,))],
    )(*shards)


HBM = pl.BlockSpec(memory_space=pltpu.HBM)
SEM = pl.BlockSpec(memory_space=pltpu.SEMAPHORE)
EFFECT = pltpu.SideEffectType.DATAFLOW_SIDE_EFFECTING
N_PEER = N_DEV - 1


def _my_id():
    return 4 * lax.axis_index("x") + 2 * lax.axis_index("y") + lax.axis_index("c")


def _peers():
    x, y, c = lax.axis_index("x"), lax.axis_index("y"), lax.axis_index("c")
    out = []
    for k in range(1, N_DEV):
        px, py, pc = (1 - x if k & 4 else x, 1 - y if k & 2 else y, 1 - c if k & 1 else c)
        out.append(((px, py, pc), 4 * px + 2 * py + pc))
    return out


def _push_copies(src_refs, land_refs, send_sems, recv_sems, scatter, landing):
    me = _my_id()
    out = []
    for ti, (src, land) in enumerate(zip(src_refs, land_refs)):
        for k, (dev, pid) in enumerate(_peers()):
            out.append(pltpu.make_async_remote_copy(
                src_ref=src.at[pid] if scatter else src, dst_ref=land.at[pid if landing else me],
                send_sem=send_sems.at[ti * N_PEER + k], recv_sem=recv_sems.at[ti * N_PEER + k],
                device_id=dev, device_id_type=MESH_ID))
    return out


def _push_start(name, srcs, scatter, after=None):
    n = len(srcs)
    slot = lambda s: s.shape[1:] if scatter else s.shape
    lands = [lax.empty((N_DEV,) + slot(s), s.dtype) for s in srcs]
    n_after = 0 if after is None else 1

    def body(*refs):
        src_refs, land_refs = refs[:n], refs[n:2 * n]
        send_sems, recv_sems = refs[2 * n + n_after], refs[2 * n + n_after + 1]
        token = refs[-1]
        for cp in _push_copies(src_refs, land_refs, send_sems, recv_sems, scatter, False):
            cp.start()
        token[...] = jnp.zeros_like(token)

    hbm = lambda a: pltpu.with_memory_space_constraint(a, pltpu.HBM)
    operands = [hbm(a) for a in srcs + lands] + ([after] if n_after else [])
    res = pl.pallas_call(
        body, name=name,
        out_shape=[pltpu.SemaphoreType.DMA((n * N_PEER,)), pltpu.SemaphoreType.DMA((n * N_PEER,))]
        + [pltpu.HBM(a.shape, a.dtype) for a in srcs + lands] + [_sds((8, 128), F32)],
        in_specs=[HBM] * (2 * n) + [ANY] * n_after,
        out_specs=[SEM, SEM] + [HBM] * (2 * n) + [pl.BlockSpec(memory_space=pltpu.VMEM)],
        input_output_aliases={i: 2 + i for i in range(2 * n)},
        compiler_params=pltpu.CompilerParams(has_side_effects=EFFECT),
    )(*operands)
    return (res[0], res[1], list(res[2:2 + n]), list(res[2 + n:2 + 2 * n]), scatter), res[-1]


def _push_wait(name, handle, after):
    send_sems, recv_sems, srcs, lands, scatter = handle
    n = len(srcs)

    def body(*refs):
        src_refs, land_refs = refs[:n], refs[n:2 * n]
        s_sems, r_sems = refs[2 * n], refs[2 * n + 1]
        for cp in _push_copies(src_refs, land_refs, s_sems, r_sems, scatter, True):
            cp.wait_send()
            cp.wait_recv()

    res = pl.pallas_call(
        body, name=name,
        out_shape=[pltpu.HBM(a.shape, a.dtype) for a in srcs + lands],
        in_specs=[HBM] * (2 * n) + [SEM, SEM, ANY], out_specs=[HBM] * (2 * n),
        input_output_aliases={i: i for i in range(2 * n)},
        compiler_params=pltpu.CompilerParams(has_side_effects=EFFECT),
    )(*srcs, *lands, send_sems, recv_sems, after)
    return list(res[n:])


def _adamw(name, parts, w, m, v, own=None):
    r, c = w.shape
    br = _pick(r, 256, 16)
    has_own = own is not None

    def body(*refs):
        if has_own:
            p_ref, own_ref, w_ref, m_ref, v_ref, g_ref, d_ref, nm_ref, nv_ref = refs
            me = _my_id()
            mine = own_ref[...].astype(F32)
        else:
            p_ref, w_ref, m_ref, v_ref, g_ref, d_ref, nm_ref, nv_ref = refs
        g = None
        for k in range(N_DEV):
            t = p_ref[k].astype(F32)
            if has_own:
                t = jnp.where(me == k, mine, t)
            g = t if g is None else g + t
        mm = ADAM_B1 * m_ref[...] + (1.0 - ADAM_B1) * g
        vv = ADAM_B2 * v_ref[...] + (1.0 - ADAM_B2) * (g * g)
        m_hat = mm / (1.0 - ADAM_B1 ** ADAM_STEP)
        v_hat = vv / (1.0 - ADAM_B2 ** ADAM_STEP)
        g_ref[...] = g
        d_ref[...] = -ADAM_LR * (m_hat / (jnp.sqrt(v_hat) + ADAM_EPS) + ADAM_WD * w_ref[...])
        nm_ref[...] = mm
        nv_ref[...] = vv

    spec = _rows(br, c)
    out = (_sds((r, c), F32), spec)
    ins = [(parts, pl.BlockSpec((N_DEV, br, c), lambda i: (0, i, 0)))] + ([(own, spec)] if has_own else [])
    return _call(name, body, (r // br,), ins + [(w, spec), (m, spec), (v, spec)], [out] * 4, sem=("parallel",))


def _pad_head_cols(w, d):
    k = w.shape[0]
    return jnp.pad(w.reshape(k, HEADS, d), ((0, 0), (0, 0), (0, HP - d))).reshape(k, HW)


def _unpad_head_cols(wp, d):
    k = wp.shape[0]
    return wp.reshape(k, HEADS, HP)[:, :, :d].reshape(k, HEADS * d)


def _pad_head_rows(w, d):
    n = w.shape[1]
    return jnp.pad(w.reshape(HEADS, d, n), ((0, 0), (0, HP - d), (0, 0))).reshape(HW, n)


def _unpad_head_rows(wp, d):
    n = wp.shape[1]
    return wp.reshape(HEADS, HP, n)[:, :d, :].reshape(HEADS * d, n)


def _w_in_runs():
    nat = {}
    o = 0
    for nm, wd in (("q", Q_RANK), ("kv", KV_RANK), ("kr", ROPE), ("fq", FOX_W), ("fk", FOX_W), ("fv", FOX_W),
                   ("fl", HEADS), ("gate", 2 * D_MODEL)):
        nat[nm] = o
        o += wd
    runs = [(1, R_QLAT, nat["q"], Q_RANK, 1.0), (1, R_KVLAT, nat["kv"], KV_RANK, 1.0),
            (1, R_LAST + LANE_FL, nat["fl"], HEADS, 1.0), (1, R_LAST + LANE_PE, nat["kr"], ROPE, 1.0),
            (1, R_GATE, nat["gate"], 2 * D_MODEL, 1.0)]
    for grp, (nm, sc) in enumerate((("fq", FOX_SCALE), ("fk", 1.0), ("fv", 1.0))):
        runs += [(0, grp * HW + h * HP, nat[nm] + h * FOX_DIM, FOX_DIM, sc) for h in range(HEADS)]
    return runs


def _sharded_runs(runs, shard_cols):
    out = []
    for half, col, ncol, width, sc in runs:
        while width > 0:
            d, local = divmod(ncol, shard_cols)
            wd = min(width, shard_cols - local)
            out.append((half, col, d, local, wd, sc))
            col, ncol, width = col + wd, ncol + wd, width - wd
    return out


def _remap(name, srcs, out_shapes, moves):
    rows = srcs[0].shape[-2]
    br = _pick(rows, 256, 16)
    ns = len(srcs)

    def spec(shape):
        if len(shape) == 2:
            return pl.BlockSpec((br, shape[1]), lambda i: (i, 0))
        return pl.BlockSpec((shape[0], br, shape[2]), lambda i: (0, i, 0))

    def body(*refs):
        s_refs, o_refs = refs[:ns], refs[ns:]
        for o in o_refs:
            o[...] = jnp.zeros_like(o)
        for di, dl, dc, si, sl, sc0, wd, scale in moves:
            v = s_refs[si][:, sc0:sc0 + wd] if sl is None else s_refs[si][sl, :, sc0:sc0 + wd]
            if scale != 1.0:
                v = v * jnp.asarray(scale, v.dtype)
            v = v.astype(o_refs[di].dtype)
            if dl is None:
                o_refs[di][:, dc:dc + wd] = v
            else:
                o_refs[di][dl, :, dc:dc + wd] = v

    return _call(name, body, (rows // br,), [(a, spec(a.shape)) for a in srcs],
                 [(_sds(shape, dt), spec(shape)) for shape, dt in out_shapes], sem=("parallel",))


def _w_in_from_shards(g3):
    n, rows, c = g3.shape
    moves = [(half, None, col, 0, d, local, wd, sc) for half, col, d, local, wd, sc in _sharded_runs(_w_in_runs(), c)]
    return _remap("w_in_repack", [g3], [((rows, F_W), g3.dtype), ((rows, R_W), g3.dtype)], moves)


def _w_in_grad_to_shards(d_fused, d_rest, n, c):
    rows = d_fused.shape[0]
    moves = [(0, d, local, half, None, col, wd, sc) for half, col, d, local, wd, sc in _sharded_runs(_w_in_runs(), c)]
    return _remap("w_in_grad_unpack", [d_fused, d_rest], [((n, rows, c), d_fused.dtype)], moves)[0]


def _cols_from_shards(name, land, own):
    n, rows, c = land.shape
    br = _pick(rows, 256, 16)

    def body(land_ref, own_ref, o_ref):
        me = _my_id()
        for d in range(n):
            o_ref[:, c * d:c * (d + 1)] = jnp.where(me == d, own_ref[...], land_ref[d])

    return _call(name, body, (rows // br,),
                 [(land, pl.BlockSpec((n, br, c), lambda i: (0, i, 0))), (own, _rows(br, c))],
                 [(_sds((rows, n * c), land.dtype), _rows(br, n * c))], sem=("parallel",))[0]


def _cols_to_shards(name, full, n):
    rows, nc = full.shape
    c = nc // n
    return _remap(name, [full], [((n, rows, c), full.dtype)], [(0, d, 0, 0, None, c * d, c, 1.0) for d in range(n)])[0]


def _split_w_kv(w):
    k = w.shape[0]
    w3 = w.reshape(k, HEADS, NOPE + V_DIM)
    padl = lambda a: jnp.pad(a, ((0, 0), (0, 0), (0, HP - a.shape[-1]))).reshape(k, HW)
    return padl(w3[..., :NOPE]), padl(w3[..., NOPE:])


def _merge_w_kv(wk, wv):
    k = wk.shape[0]
    return jnp.concatenate([wk.reshape(k, HEADS, HP)[..., :NOPE], wv.reshape(k, HEADS, HP)[..., :V_DIM]],
                           axis=-1).reshape(k, HEADS * (NOPE + V_DIM))


class _NoComm:
    first_token = ()

    def late_weights(self, group, after):
        return {}

    def send(self, name, grads):
        return ()


def _local_step(x, tgt, p, comm=_NoComm()):
    seq = x.shape[0]
    r = -(-(N_META + seq) // ROW_ALIGN) * ROW_ALIGN
    cd = MXU_DTYPE
    p = dict(p)

    w_f, w_r = p["w_in"]
    w_q = _pad_head_cols(p["w_q_up"], QK_DIM)
    w_kv = jnp.concatenate(_split_w_kv(p["w_kv_up"]), axis=1)

    pos = jnp.arange(r, dtype=F32)
    inv_freq = ROPE_THETA ** (-jnp.arange(HALF, dtype=F32) / HALF)
    ang = pos[:, None] * inv_freq[None, :]
    cos_t = jnp.tile(jnp.cos(ang), (1, HP // HALF))
    sin_t = jnp.tile(jnp.sin(ang), (1, HP // HALF))
    bf_row = jnp.zeros((1, HP), F32).at[0, LANE_FL:LANE_FL + HEADS].set(p["b_forget"])

    h0, h0b = _ln_emb_fwd(x, p["meta_tokens"], p["ln_emb_g"], p["ln_emb_b"], r, after=comm.first_token)
    proj_f = _matmul("in_proj_f", h0b, w_f, out_dtype=cd)
    proj_r = _matmul("in_proj_r", h0b, w_r)
    ql = _rms_fwd("q_norm_fwd", proj_r, R_QLAT // Q_RANK, Q_RANK, p["q_norm_g"])
    kvl = _rms_fwd("kv_norm_fwd", proj_r, R_KVLAT // KV_RANK, KV_RANK, p["kv_norm_g"])
    q_raw = _matmul("q_up", ql, w_q)
    kv = _matmul("kv_up", kvl, w_kv, out_dtype=cd)
    q_mla, k_mla = _rope_fwd(q_raw, kv, proj_r, cos_t, sin_t)
    o_mla, o_mla_b, lse_mla = _attn_fwd("mla_fwd", (q_mla, 0), (k_mla, 0), (kv, 1))

    cum, cum_t = _forget_fwd(proj_r, bf_row)
    o_fox, o_fox_b, lse_fox = _attn_fwd("fox_fwd", (proj_f, 0), (proj_f, 1), (proj_f, 2), cum, cum_t)

    p.update(comm.late_weights("mix", o_fox_b))
    w_bm = _pad_head_rows(p["w_branch_mla"], V_DIM)
    w_bf = _pad_head_rows(p["w_branch_fox"], FOX_DIM)
    bm = _matmul("branch_mla", o_mla_b, w_bm, out_dtype=cd)
    bfx = _matmul("branch_fox", o_fox_b, w_bf, out_dtype=cd)
    merged = _gate_fwd(proj_r, p["b_gate"], bm, bfx)
    mix = _matmul("out_proj", merged, p["w_out"])
    h1, h1b = _ln_fwd("ln_mix_fwd", h0, mix, p["ln_mix_g"], p["ln_mix_b"])
    p.update(comm.late_weights("ffn", h1b))
    up = _matmul("ffn_up", h1b, p["w_ffn_up"], out_dtype=cd)
    act = _glu_fwd(up, p["conv_w"], p["conv_b"])
    f = _matmul("ffn_down", act, p["w_ffn_down"])
    loss = _ln_ffn_loss(h1, f, tgt, p["ln_ffn_g"], p["ln_ffn_b"])

    g = {}
    dz2, dz2b, g["ln_ffn_g"], g["ln_ffn_b"] = _ln_ffn_bwd(h1, f, tgt, p["ln_ffn_g"], p["ln_ffn_b"])
    d_act = _matmul("ffn_down_dx", dz2b, p["w_ffn_down"], tb=True, out_dtype=cd)
    g["w_ffn_down"] = _matmul("ffn_down_dw", act, dz2b, ta=True, out_dtype=cd)
    d_up, dcw, g["conv_b"] = _glu_bwd(up, p["conv_w"], p["conv_b"], d_act)
    g["conv_w"] = dcw[:3]
    dh1 = _matmul("ffn_up_dx", d_up, p["w_ffn_up"], tb=True, addend=dz2, alpha=ALPHA)
    g["w_ffn_up"] = _matmul("ffn_up_dw", h1b, d_up, ta=True, out_dtype=cd)
    sent = comm.send("ffn", {n: g[n] for n in ("w_ffn_down", "w_ffn_up", "conv_w")})
    dz1, dz1b, g["ln_mix_g"], g["ln_mix_b"] = _ln_bwd("ln_mix_bwd", h0, mix, dh1, p["ln_mix_g"], after=sent)
    dmerged = _matmul("out_proj_dx", dz1b, p["w_out"], tb=True, out_dtype=cd)
    g["w_out"] = _matmul("out_proj_dw", merged, dz1b, ta=True, out_dtype=cd)
    d_bm, d_bf, d_gl, g["b_gate"] = _gate_bwd(proj_r, p["b_gate"], bm, bfx, dmerged)
    do_mla_b = _matmul("branch_mla_dx", d_bm, w_bm, tb=True, out_dtype=cd)
    g["w_branch_mla"] = _unpad_head_rows(_matmul("branch_mla_dw", o_mla_b, d_bm, ta=True, out_dtype=cd), V_DIM)
    do_fox_b = _matmul("branch_fox_dx", d_bf, w_bf, tb=True, out_dtype=cd)
    g["w_branch_fox"] = _unpad_head_rows(_matmul("branch_fox_dw", o_fox_b, d_bf, ta=True, out_dtype=cd), FOX_DIM)

    sent = comm.send("mix", {n: g[n] for n in ("w_out", "w_branch_mla", "w_branch_fox")})
    dl_mla = _attn_delta("mla_delta", do_mla_b, o_mla, after=sent)
    dq_m, dk_m, dv_m = _attn_bwd("mla_bwd", (q_mla, 0), (k_mla, 0), (kv, 1), do_mla_b, lse_mla, dl_mla)
    dl_fox = _attn_delta("fox_delta", do_fox_b, o_fox)
    dfq, dfk, dfv, dcq, dck = _attn_bwd("fox_bwd", (proj_f, 0), (proj_f, 1), (proj_f, 2), do_fox_b, lse_fox, dl_fox,
                                        cum, cum_t, out_dtype=cd)
    dfl, dbf = _forget_bwd(proj_r, bf_row, dcq, dck)
    g["b_forget"] = dbf[:, LANE_FL:LANE_FL + HEADS]

    dq_b, dkv_b, dlast = _rope_bwd(dq_m, dk_m, dv_m, dfl, cos_t, sin_t)
    d_ql = _matmul("q_up_dx", dq_b, w_q, tb=True)
    d_kvl = _matmul("kv_up_dx", dkv_b, w_kv, tb=True)
    d_qlat, g["q_norm_g"] = _rms_bwd("q_norm_bwd", proj_r, R_QLAT // Q_RANK, Q_RANK, d_ql, p["q_norm_g"])
    d_kvlat, g["kv_norm_g"] = _rms_bwd("kv_norm_bwd", proj_r, R_KVLAT // KV_RANK, KV_RANK, d_kvl, p["kv_norm_g"])
    dproj_f = jnp.concatenate([dfq, dfk, dfv], axis=1)
    dproj_r = jnp.concatenate([d_qlat, d_kvlat, dlast, jnp.zeros((r, R_GATE - R_LAST - HP), cd), d_gl], axis=1)
    g["w_in"] = (_matmul("in_proj_f_dw", h0b, dproj_f, ta=True, out_dtype=cd),
                 _matmul("in_proj_r_dw", h0b, dproj_r, ta=True, out_dtype=cd))
    sent = comm.send("in", {"w_in": g["w_in"]})
    dh0 = _matmul("in_proj_f_dx", dproj_f, w_f, tb=True, addend=dz1, alpha=ALPHA, after=sent)
    g["w_q_up"] = _unpad_head_cols(_matmul("q_up_dw", ql, dq_b, ta=True, out_dtype=cd, after=sent), QK_DIM)
    dw_kv = _matmul("kv_up_dw", kvl, dkv_b, ta=True, out_dtype=cd, after=sent)
    g["w_kv_up"] = _merge_w_kv(dw_kv[:, :HW], dw_kv[:, HW:])
    sent = comm.send("qkv", {n: g[n] for n in ("w_q_up", "w_kv_up")})
    dh0 = _matmul("in_proj_r_dx", dproj_r, w_r, tb=True, addend=dh0, after=sent)
    grad_x, d_meta, g["ln_emb_g"], g["ln_emb_b"] = _ln_emb_bwd(x, p["meta_tokens"], dh0, p["ln_emb_g"])
    return loss, grad_x, d_meta, g


BIG = (("w_in", 1), ("w_q_up", 1), ("w_kv_up", 1), ("w_branch_mla", 1), ("w_branch_fox", 1), ("w_out", 0),
       ("w_ffn_up", 1), ("w_ffn_down", 0))
SMALL_SHARDED = (("meta_tokens", 1), ("conv_w", 1))
EARLY = ("w_in", "w_q_up", "w_kv_up", "meta_tokens", "conv_w")
LATE = {"mix": ("w_branch_mla", "w_branch_fox", "w_out"),
        "ffn": ("w_ffn_up", "w_ffn_down")}
REPLICATED = ("ln_emb_g", "ln_emb_b", "b_gate", "b_forget", "q_norm_g", "kv_norm_g", "ln_mix_g", "ln_mix_b",
              "conv_b", "ln_ffn_g", "ln_ffn_b")
PACK_COLS = 1024


def _pack(flat_list):
    cat = jnp.concatenate(flat_list)
    n = cat.shape[0]
    rows = -(-n // (8 * PACK_COLS)) * 8
    return jnp.pad(cat, (0, rows * PACK_COLS - n)).reshape(rows, PACK_COLS)


def _gathered_full(g3, axis):
    n, r, c = g3.shape
    if axis == 0:
        return g3.reshape(n * r, c)
    return g3.transpose(1, 0, 2).reshape(r, n * c)


def _shard_major(full, axis):
    r, c = full.shape
    if axis == 0:
        return full.reshape(N_DEV, r // N_DEV, c)
    return full.reshape(r, N_DEV, c // N_DEV).transpose(1, 0, 2)


def kernel(x, meta_tokens, ln_emb_g, ln_emb_b, w_in, b_gate, b_forget, q_norm_g, w_q_up, kv_norm_g, w_kv_up, w_branch_mla, w_branch_fox, w_out, ln_mix_g, ln_mix_b, w_ffn_up, conv_w, conv_b, w_ffn_down, ln_ffn_g, ln_ffn_b, loss_target, m_meta_tokens, m_ln_emb_g, m_ln_emb_b, m_w_in, m_b_gate, m_b_forget, m_q_norm_g, m_w_q_up, m_kv_norm_g, m_w_kv_up, m_w_branch_mla, m_w_branch_fox, m_w_out, m_ln_mix_g, m_ln_mix_b, m_w_ffn_up, m_conv_w, m_conv_b, m_w_ffn_down, m_ln_ffn_g, m_ln_ffn_b, v_meta_tokens, v_ln_emb_g, v_ln_emb_b, v_w_in, v_b_gate, v_b_forget, v_q_norm_g, v_w_q_up, v_kv_norm_g, v_w_kv_up, v_w_branch_mla, v_w_branch_fox, v_w_out, v_ln_mix_g, v_ln_mix_b, v_w_ffn_up, v_conv_w, v_conv_b, v_w_ffn_down, v_ln_ffn_g, v_ln_ffn_b):
    names = ("meta_tokens", "ln_emb_g", "ln_emb_b", "w_in", "b_gate", "b_forget", "q_norm_g", "w_q_up", "kv_norm_g",
             "w_kv_up", "w_branch_mla", "w_branch_fox", "w_out", "ln_mix_g", "ln_mix_b", "w_ffn_up", "conv_w", "conv_b",
             "w_ffn_down", "ln_ffn_g", "ln_ffn_b")
    w_args = (meta_tokens, ln_emb_g, ln_emb_b, w_in, b_gate, b_forget, q_norm_g, w_q_up, kv_norm_g, w_kv_up,
              w_branch_mla, w_branch_fox, w_out, ln_mix_g, ln_mix_b, w_ffn_up, conv_w, conv_b, w_ffn_down, ln_ffn_g, ln_ffn_b)
    m_args = (m_meta_tokens, m_ln_emb_g, m_ln_emb_b, m_w_in, m_b_gate, m_b_forget, m_q_norm_g, m_w_q_up, m_kv_norm_g,
              m_w_kv_up, m_w_branch_mla, m_w_branch_fox, m_w_out, m_ln_mix_g, m_ln_mix_b, m_w_ffn_up, m_conv_w, m_conv_b,
              m_w_ffn_down, m_ln_ffn_g, m_ln_ffn_b)
    v_args = (v_meta_tokens, v_ln_emb_g, v_ln_emb_b, v_w_in, v_b_gate, v_b_forget, v_q_norm_g, v_w_q_up, v_kv_norm_g,
              v_w_kv_up, v_w_branch_mla, v_w_branch_fox, v_w_out, v_ln_mix_g, v_ln_mix_b, v_w_ffn_up, v_conv_w, v_conv_b,
              v_w_ffn_down, v_ln_ffn_g, v_ln_ffn_b)
    as2d = lambda a: a.reshape((-1, a.shape[-1])) if a.ndim != 1 else a.reshape(1, -1)
    w = {n: as2d(a) for n, a in zip(names, w_args)}
    m = {n: as2d(a) for n, a in zip(names, m_args)}
    v = {n: as2d(a) for n, a in zip(names, v_args)}
    out_shape = {n: a.shape for n, a in zip(names, w_args)}

    axis_of = dict(BIG + SMALL_SHARDED)
    big = set(n for n, _ in BIG)
    wire = lambda n, a: a.astype(MXU_DTYPE) if n in big else a
    my_id = _my_id()

    early = _allgather("gather_early", [wire(n, w[n]) for n in EARLY])
    p = {n: _gathered_full(g3, axis_of[n]) for n, g3 in zip(EARLY, early) if n != "w_in"}
    p["w_in"] = _w_in_from_shards(early[EARLY.index("w_in")])
    for n in REPLICATED:
        p[n] = w[n].reshape(-1)
    late, tokens, prev = {}, [], early[0]
    for group, members in LATE.items():
        src = [wire(n, w[n]) for n in members]
        handle, token = _push_start("gather_" + group + "_start", src, False, after=prev)
        late[group] = (members, src, handle)
        tokens.append(token)
        prev = token
    sent = {}

    class Comm:
        first_token = tuple(tokens)

        def late_weights(self, group, after):
            members, src, handle = late[group]
            lands = _push_wait("gather_" + group + "_wait", handle, after)
            out = {}
            for n, own, land in zip(members, src, lands):
                if n == "w_ffn_up":
                    out[n] = _cols_from_shards(n + "_repack", land, own)
                else:
                    out[n] = _gathered_full(lax.dynamic_update_index_in_dim(land, own, my_id, 0), axis_of[n])
            return out

        def send(self, name, grads):
            names_ = tuple(grads)
            parts = []
            for n in names_:
                if n == "w_in":
                    parts.append(_w_in_grad_to_shards(*grads[n], N_DEV, w[n].shape[1]))
                elif n == "w_ffn_up":
                    parts.append(_cols_to_shards(n + "_grad_unpack", grads[n], N_DEV))
                else:
                    parts.append(_shard_major(grads[n], axis_of[n]).astype(MXU_DTYPE))
            handle, token = _push_start("send_" + name + "_start", parts, True)
            sent[name] = (names_, parts, handle)
            return (token,)

    loss_part, grad_x, d_meta, g = _local_step(x[0], loss_target[0], p, Comm())
    grad_x = grad_x[None]

    small = _pack([d_meta.reshape(-1)] + [g[n].reshape(-1) for n in REPLICATED] + [loss_part.reshape(-1)])
    small_handle, small_token = _push_start("send_small_start", [small], False)

    res = {}
    prev = small_token
    for name, (names_, parts, handle) in sent.items():
        lands = _push_wait("send_" + name + "_wait", handle, prev)
        for n, part, land in zip(names_, parts, lands):
            own = lax.dynamic_index_in_dim(part, my_id, axis=0, keepdims=False)
            res[n] = _adamw("adamw_" + n, land, w[n], m[n], v[n], own=own)
            prev = res[n][0]
    small_all = _push_wait("send_small_wait", small_handle, prev)[0]
    head = jnp.zeros((d_meta.size,), F32)
    rep_w = _pack([head] + [w[n].reshape(-1) for n in REPLICATED])
    rep_m = _pack([head] + [m[n].reshape(-1) for n in REPLICATED])
    rep_v = _pack([head] + [v[n].reshape(-1) for n in REPLICATED])
    rep_res = _adamw("adamw_replicated", small_all, rep_w, rep_m, rep_v, own=small)
    off = d_meta.size
    for n in REPLICATED:
        sz = w[n].size
        res[n] = tuple(a.reshape(-1)[off:off + sz] for a in rep_res)
        off += sz
    loss = rep_res[0].reshape(-1)[off]
    cols = w["meta_tokens"].shape[1]
    meta_rows = lambda a: a.reshape(a.shape[:-2] + (-1,))[..., :d_meta.size].reshape(a.shape[:-2] + d_meta.shape)
    my_cols = lambda a: lax.dynamic_slice_in_dim(a, my_id * cols, cols, axis=a.ndim - 1)
    res["meta_tokens"] = _adamw("adamw_meta_tokens", my_cols(meta_rows(small_all)), w["meta_tokens"],
                                m["meta_tokens"], v["meta_tokens"], own=my_cols(d_meta))

    outs = [loss, grad_x]
    for idx in range(4):
        outs += [res[n][idx].reshape(out_shape[n]) for n in names]
    return tuple(outs)
```

```python
import jax
import jax.numpy as jnp
from jax import lax
from jax.experimental import pallas as pl
from jax.experimental.pallas import tpu as pltpu

F32 = jnp.float32
BF16 = jnp.bfloat16
MXU_DTYPE = BF16

N_DEV = 8
N_META = 16
D_MODEL = 1024
HEADS = 8
Q_RANK = 384
KV_RANK = 128
NOPE = 64
ROPE = 32
HALF = ROPE // 2
QK_DIM = NOPE + ROPE
V_DIM = 64
FOX_DIM = 64
FOX_W = HEADS * FOX_DIM
D_FF = 2816
ROPE_THETA = 10000.0
LN_EPS = 1e-5
RMS_EPS = 1e-6
ALPHA = 2.0 ** 0.25
MLA_SCALE = QK_DIM ** -0.5
FOX_SCALE = FOX_DIM ** -0.5
NEG_INF = -1e30

HP = 128
HW = HEADS * HP
F_W = 3 * HW
R_QLAT = 0
R_KVLAT = Q_RANK
R_LAST = R_KVLAT + KV_RANK
R_GATE = D_MODEL
R_W = R_GATE + 2 * D_MODEL
LANE_FL = 0
LANE_PE = NOPE

ADAM_LR = 0.001
ADAM_B1 = 0.9
ADAM_B2 = 0.999
ADAM_EPS = 1e-08
ADAM_WD = 0.01
ADAM_STEP = 10

ROW_BLOCK = 256
ATT_TQ = 768
ATT_TK = 768
ATT_HEADS = 2
ROW_ALIGN = 768
MM_BLOCK_CAP = 1408
VMEM_LIMIT = 56 * 1024 * 1024
HIGHEST = lax.Precision.HIGHEST
NT = (((1,), (1,)), ((), ()))
TN = (((0,), (0,)), ((), ()))


def _params(sem=None):
    return pltpu.CompilerParams(dimension_semantics=sem, vmem_limit_bytes=VMEM_LIMIT)


def _call(name, body, grid, ins, outs, scratch=(), sem=None, after=()):
    n_in = len(ins)
    n_tok = len(after)

    def run(*refs):
        body(*refs[:n_in], *refs[n_in + n_tok:])

    tok_spec = pl.BlockSpec((8, 128), lambda *_: (0, 0))
    return pl.pallas_call(
        run, name=name, grid=grid,
        in_specs=[s for _, s in ins] + [tok_spec] * n_tok,
        out_specs=[s for _, s in outs],
        out_shape=[o for o, _ in outs],
        scratch_shapes=list(scratch),
        compiler_params=_params(sem),
    )(*[a for a, _ in ins], *after)


def _sds(shape, dtype):
    return jax.ShapeDtypeStruct(shape, dtype)


def _rows(br, c, cb=0):
    return pl.BlockSpec((br, c), lambda i: (i, cb))


def _whole(shape):
    n = len(shape)
    return pl.BlockSpec(shape, lambda i: (0,) * n)


def _pick(dim, cap, mult):
    best = None
    d = mult
    while d <= min(dim, cap):
        if dim % d == 0:
            best = d
        d += mult
    return best if best is not None else dim


def _hs(h):
    return slice(h * HP, (h + 1) * HP)


def _matmul(name, a, b, *, ta=False, tb=False, out_dtype=F32, addend=None, alpha=1.0, after=()):
    if ta:
        k, m = a.shape
    else:
        m, k = a.shape
    if tb:
        n, k2 = b.shape
    else:
        k2, n = b.shape
    assert k == k2, (name, a.shape, b.shape)
    bm = _pick(m, MM_BLOCK_CAP, 128 if ta else 16)
    bn = _pick(n, MM_BLOCK_CAP, 128)
    bk = _pick(k, MM_BLOCK_CAP, 128 if (not ta or tb) else 16)
    nk = k // bk
    dims = (((0 if ta else 1,), (1 if tb else 0,)), ((), ()))
    has_add = addend is not None

    def body(*refs):
        a_ref, b_ref = refs[:2]
        add_ref = refs[2] if has_add else None
        o_ref = refs[3 if has_add else 2]

        def finish(r):
            if has_add:
                r = r + alpha * add_ref[...]
            o_ref[...] = r.astype(o_ref.dtype)

        part = lax.dot_general(a_ref[...], b_ref[...], dims, preferred_element_type=F32)
        if nk == 1:
            finish(part)
            return
        acc_ref = refs[-1]
        kk = pl.program_id(2)

        @pl.when(kk == 0)
        def _():
            acc_ref[...] = part

        @pl.when(kk > 0)
        def _():
            acc_ref[...] += part

        @pl.when(kk == nk - 1)
        def _():
            finish(acc_ref[...])

    a_spec = pl.BlockSpec((bk, bm), lambda i, j, l: (l, i)) if ta else pl.BlockSpec((bm, bk), lambda i, j, l: (i, l))
    b_spec = pl.BlockSpec((bn, bk), lambda i, j, l: (j, l)) if tb else pl.BlockSpec((bk, bn), lambda i, j, l: (l, j))
    o_spec = pl.BlockSpec((bm, bn), lambda i, j, l: (i, j))
    ins = [(a, a_spec), (b, b_spec)]
    if has_add:
        ins.append((addend, o_spec))
    return _call(name, body, (m // bm, n // bn, nk), ins, [(_sds((m, n), out_dtype), o_spec)],
                 scratch=[pltpu.VMEM((bm, bn), F32)] if nk > 1 else [],
                 sem=("parallel", "parallel", "arbitrary"), after=after)[0]


def _ln_stats(z):
    mu = jnp.mean(z, axis=-1, keepdims=True)
    zc = z - mu
    var = jnp.mean(zc * zc, axis=-1, keepdims=True)
    rstd = lax.rsqrt(var + LN_EPS)
    return zc * rstd, rstd


def _ln_fwd(name, a, res, g, b, after=()):
    r, d = a.shape
    br = ROW_BLOCK
    has_res = res is not None

    def body(*refs):
        if has_res:
            a_ref, r_ref, g_ref, b_ref, y_ref, yb_ref = refs
            z = ALPHA * a_ref[...] + r_ref[...]
        else:
            a_ref, g_ref, b_ref, y_ref, yb_ref = refs
            z = a_ref[...]
        xhat, _ = _ln_stats(z)
        y = xhat * g_ref[...] + b_ref[...]
        y_ref[...] = y
        yb_ref[...] = y.astype(yb_ref.dtype)

    ins = [(a, _rows(br, d))]
    if has_res:
        ins.append((res, _rows(br, d)))
    ins += [(g.reshape(1, d), _whole((1, d))), (b.reshape(1, d), _whole((1, d)))]
    outs = [(_sds((r, d), F32), _rows(br, d)), (_sds((r, d), MXU_DTYPE), _rows(br, d))]
    return _call(name, body, (r // br,), ins, outs, sem=("parallel",), after=after)


def _ln_bwd(name, a, res, dy, g, after=()):
    r, d = a.shape
    br = ROW_BLOCK
    has_res = res is not None

    def body(*refs):
        if has_res:
            a_ref, r_ref, dy_ref, g_ref, dz_ref, dzb_ref, dg_ref, db_ref = refs
            z = ALPHA * a_ref[...] + r_ref[...]
        else:
            a_ref, dy_ref, g_ref, dz_ref, dzb_ref, dg_ref, db_ref = refs
            z = a_ref[...]
        xhat, rstd = _ln_stats(z)
        dyv = dy_ref[...]
        dyg = dyv * g_ref[...]
        m1 = jnp.mean(dyg, axis=-1, keepdims=True)
        m2 = jnp.mean(dyg * xhat, axis=-1, keepdims=True)
        dz = rstd * (dyg - m1 - xhat * m2)
        dz_ref[...] = dz
        dzb_ref[...] = dz.astype(dzb_ref.dtype)

        @pl.when(pl.program_id(0) == 0)
        def _():
            dg_ref[...] = jnp.zeros_like(dg_ref)
            db_ref[...] = jnp.zeros_like(db_ref)

        dg_ref[...] += jnp.sum(dyv * xhat, axis=0, keepdims=True)
        db_ref[...] += jnp.sum(dyv, axis=0, keepdims=True)

    ins = [(a, _rows(br, d))]
    if has_res:
        ins.append((res, _rows(br, d)))
    ins += [(dy, _rows(br, d)), (g.reshape(1, d), _whole((1, d)))]
    outs = [(_sds((r, d), F32), _rows(br, d)), (_sds((r, d), MXU_DTYPE), _rows(br, d)),
            (_sds((1, d), F32), _whole((1, d))), (_sds((1, d), F32), _whole((1, d)))]
    return _call(name, body, (r // br,), ins, outs, sem=("arbitrary",), after=after)


def _rms_fwd(name, proj, cb, width, g):
    r = proj.shape[0]
    br = ROW_BLOCK

    def body(x_ref, g_ref, y_ref):
        x = x_ref[...]
        rstd = lax.rsqrt(jnp.mean(x * x, axis=-1, keepdims=True) + RMS_EPS)
        y_ref[...] = (x * rstd * g_ref[...]).astype(y_ref.dtype)

    return _call(name, body, (r // br,), [(proj, _rows(br, width, cb)), (g.reshape(1, width), _whole((1, width)))],
                 [(_sds((r, width), MXU_DTYPE), _rows(br, width))], sem=("parallel",))[0]


def _rms_bwd(name, proj, cb, width, dy, g):
    r = proj.shape[0]
    br = ROW_BLOCK

    def body(x_ref, dy_ref, g_ref, dx_ref, dg_ref):
        x = x_ref[...]
        rstd = lax.rsqrt(jnp.mean(x * x, axis=-1, keepdims=True) + RMS_EPS)
        nrm = x * rstd
        dyv = dy_ref[...]
        dyg = dyv * g_ref[...]
        dx = rstd * (dyg - nrm * jnp.mean(dyg * nrm, axis=-1, keepdims=True))
        dx_ref[...] = dx.astype(dx_ref.dtype)

        @pl.when(pl.program_id(0) == 0)
        def _():
            dg_ref[...] = jnp.zeros_like(dg_ref)

        dg_ref[...] += jnp.sum(dyv * nrm, axis=0, keepdims=True)

    return _call(name, body, (r // br,),
                 [(proj, _rows(br, width, cb)), (dy, _rows(br, width)), (g.reshape(1, width), _whole((1, width)))],
                 [(_sds((r, width), MXU_DTYPE), _rows(br, width)), (_sds((1, width), F32), _whole((1, width)))],
                 sem=("arbitrary",))


def _lane_iota(shape):
    return lax.broadcasted_iota(jnp.int32, shape, 1)


def _rotary(t, c, s, lane, sign):
    second = pltpu.roll(t, HP - HALF, axis=1)
    first = pltpu.roll(t, HALF, axis=1)
    lo = (lane >= LANE_PE) & (lane < LANE_PE + HALF)
    hi = (lane >= LANE_PE + HALF) & (lane < LANE_PE + ROPE)
    return jnp.where(lo, t * c - sign * second * s, jnp.where(hi, t * c + sign * first * s, t))


def _rope_fwd(q_raw, k_part, proj_r, cos_t, sin_t):
    r = q_raw.shape[0]
    br = ROW_BLOCK

    def body(q_ref, k_ref, t_ref, c_ref, s_ref, qo_ref, ko_ref):
        c = c_ref[...]
        s = s_ref[...]
        lane = _lane_iota((br, HP))
        pe = (lane >= LANE_PE) & (lane < LANE_PE + ROPE)
        kp = jnp.where(pe, _rotary(t_ref[...], c, s, lane, 1.0), 0.0)
        for h in range(HEADS):
            qo_ref[:, _hs(h)] = (_rotary(q_ref[:, _hs(h)], c, s, lane, 1.0) * MLA_SCALE).astype(qo_ref.dtype)
            ko_ref[:, _hs(h)] = (k_ref[:, _hs(h)] + kp).astype(ko_ref.dtype)

    blk = _rows(br, HP)
    wide = _rows(br, HW)
    return _call("rope_fwd", body, (r // br,),
                 [(q_raw, wide), (k_part, wide), (proj_r, _rows(br, HP, R_LAST // HP)), (cos_t, blk), (sin_t, blk)],
                 [(_sds((r, HW), MXU_DTYPE), wide)] * 2, sem=("parallel",))


def _rope_bwd(dq, dk, dv, dfl, cos_t, sin_t):
    r = dq.shape[0]
    br = ROW_BLOCK

    def body(dq_ref, dk_ref, dv_ref, fl_ref, c_ref, s_ref, dqo_ref, dkv_ref, dl_ref):
        c = c_ref[...]
        s = s_ref[...]
        lane = _lane_iota((br, HP))
        pe = (lane >= LANE_PE) & (lane < LANE_PE + ROPE)
        acc = jnp.zeros((br, HP), F32)
        for h in range(HEADS):
            dqo_ref[:, _hs(h)] = (_rotary(dq_ref[:, _hs(h)], c, s, lane, -1.0) * MLA_SCALE).astype(dqo_ref.dtype)
            dkh = dk_ref[:, _hs(h)]
            acc = acc + dkh
            dkv_ref[:, _hs(h)] = dkh.astype(dkv_ref.dtype)
            dkv_ref[:, _hs(HEADS + h)] = dv_ref[:, _hs(h)].astype(dkv_ref.dtype)
        dl_ref[...] = (jnp.where(pe, _rotary(acc, c, s, lane, -1.0), 0.0) + fl_ref[...]).astype(dl_ref.dtype)

    blk = _rows(br, HP)
    wide = _rows(br, HW)
    return _call("rope_bwd", body, (r // br,),
                 [(dq, wide), (dk, wide), (dv, wide), (dfl, blk), (cos_t, blk), (sin_t, blk)],
                 [(_sds((r, HW), MXU_DTYPE), wide), (_sds((r, 2 * HW), MXU_DTYPE), _rows(br, 2 * HW)),
                  (_sds((r, HP), MXU_DTYPE), blk)],
                 sem=("parallel",))


def _log_sigmoid(x):
    return jnp.minimum(x, 0.0) - jnp.log(1.0 + jnp.exp(-jnp.abs(x)))


def _head_lane(x, h, lane):
    return jnp.sum(jnp.where(lane == h, x, 0.0), axis=1, keepdims=True)


def _forget_fwd(proj_r, bf_row):
    r = proj_r.shape[0]
    br = ROW_BLOCK

    def body(t_ref, b_ref, ob_ref, ot_ref, carry_ref):
        @pl.when(pl.program_id(0) == 0)
        def _():
            carry_ref[...] = jnp.zeros_like(carry_ref)

        x = t_ref[...] + b_ref[...]
        lane = _lane_iota(x.shape)
        lf = jnp.where((lane >= LANE_FL) & (lane < LANE_FL + HEADS), _log_sigmoid(x), 0.0)
        tri = (lax.broadcasted_iota(jnp.int32, (br, br), 0) >= lax.broadcasted_iota(jnp.int32, (br, br), 1)).astype(F32)
        cum = jnp.dot(tri, lf, precision=HIGHEST, preferred_element_type=F32) + carry_ref[0:1, :]
        for h in range(HEADS):
            ob_ref[:, _hs(h)] = jnp.broadcast_to(_head_lane(cum, LANE_FL + h, lane), (br, HP))
        ot_ref[...] = cum.T[LANE_FL:LANE_FL + HEADS, :]
        carry_ref[...] = jnp.broadcast_to(cum[br - 1:br, :], carry_ref.shape)

    return _call("forget_fwd", body, (r // br,),
                 [(proj_r, _rows(br, HP, R_LAST // HP)), (bf_row, _whole((1, HP)))],
                 [(_sds((r, HW), F32), _rows(br, HW)), (_sds((HEADS, r), F32), pl.BlockSpec((HEADS, br), lambda i: (0, i)))],
                 scratch=[pltpu.VMEM((8, HP), F32)], sem=("arbitrary",))


def _forget_bwd(proj_r, bf_row, dcq_t, dck_b):
    r = proj_r.shape[0]
    br = ROW_BLOCK
    nb = r // br

    def body(t_ref, b_ref, dcq_ref, dck_ref, o_ref, db_ref, carry_ref):
        @pl.when(pl.program_id(0) == 0)
        def _():
            carry_ref[...] = jnp.zeros_like(carry_ref)
            db_ref[...] = jnp.zeros_like(db_ref)

        lane = _lane_iota((br, HP))
        dc = jnp.concatenate([dcq_ref[...], jnp.zeros((HP - HEADS, br), F32)], axis=0).T
        for h in range(HEADS):
            dc = dc + jnp.where(lane == LANE_FL + h, dck_ref[:, h * HP:h * HP + 1], 0.0)
        triu = (lax.broadcasted_iota(jnp.int32, (br, br), 0) <= lax.broadcasted_iota(jnp.int32, (br, br), 1)).astype(F32)
        dlf = jnp.dot(triu, dc, precision=HIGHEST, preferred_element_type=F32) + carry_ref[0:1, :]
        carry_ref[...] = jnp.broadcast_to(dlf[0:1, :], carry_ref.shape)
        x = t_ref[...] + b_ref[...]
        dfl = jnp.where((lane >= LANE_FL) & (lane < LANE_FL + HEADS), dlf * jax.nn.sigmoid(-x), 0.0)
        o_ref[...] = dfl
        db_ref[...] += jnp.sum(dfl, axis=0, keepdims=True)

    rev = pl.BlockSpec((br, HP), lambda i: (nb - 1 - i, 0))
    return _call("forget_bwd", body, (nb,),
                 [(proj_r, pl.BlockSpec((br, HP), lambda i: (nb - 1 - i, R_LAST // HP))), (bf_row, _whole((1, HP))),
                  (dcq_t, pl.BlockSpec((HEADS, br), lambda i: (0, nb - 1 - i))),
                  (dck_b, pl.BlockSpec((br, HW), lambda i: (nb - 1 - i, 0)))],
                 [(_sds((r, HP), F32), rev), (_sds((1, HP), F32), _whole((1, HP)))],
                 scratch=[pltpu.VMEM((8, HP), F32)], sem=("arbitrary",))


def _gate_fwd(proj_r, b_gate, bm, bfx):
    r, d = bm.shape
    br = ROW_BLOCK
    cb = R_GATE // d

    def body(gm_ref, gf_ref, b1_ref, b2_ref, bm_ref, bf_ref, o_ref):
        g1 = jax.nn.sigmoid(gm_ref[...] + b1_ref[...])
        g2 = jax.nn.sigmoid(gf_ref[...] + b2_ref[...])
        o_ref[...] = (g1 * bm_ref[...].astype(F32) + g2 * bf_ref[...].astype(F32)).astype(o_ref.dtype)

    b1 = b_gate[:d].reshape(1, d)
    b2 = b_gate[d:].reshape(1, d)
    return _call("gate_fwd", body, (r // br,),
                 [(proj_r, _rows(br, d, cb)), (proj_r, _rows(br, d, cb + 1)), (b1, _whole((1, d))), (b2, _whole((1, d))),
                  (bm, _rows(br, d)), (bfx, _rows(br, d))],
                 [(_sds((r, d), MXU_DTYPE), _rows(br, d))], sem=("parallel",))[0]


def _gate_bwd(proj_r, b_gate, bm, bfx, dmerged):
    r, d = bm.shape
    br = ROW_BLOCK
    cb = R_GATE // d

    def body(gm_ref, gf_ref, b1_ref, b2_ref, bm_ref, bf_ref, dm_ref, dbm_ref, dbf_ref, dgl_ref, dbg_ref):
        g1 = jax.nn.sigmoid(gm_ref[...] + b1_ref[...])
        g2 = jax.nn.sigmoid(gf_ref[...] + b2_ref[...])
        dm = dm_ref[...].astype(F32)
        dbm_ref[...] = (dm * g1).astype(dbm_ref.dtype)
        dbf_ref[...] = (dm * g2).astype(dbf_ref.dtype)
        dl1 = dm * bm_ref[...].astype(F32) * (g1 * (1.0 - g1))
        dl2 = dm * bf_ref[...].astype(F32) * (g2 * (1.0 - g2))
        dgl_ref[:, 0:d] = dl1.astype(dgl_ref.dtype)
        dgl_ref[:, d:2 * d] = dl2.astype(dgl_ref.dtype)

        @pl.when(pl.program_id(0) == 0)
        def _():
            dbg_ref[...] = jnp.zeros_like(dbg_ref)

        dbg_ref[:, 0:d] += jnp.sum(dl1, axis=0, keepdims=True)
        dbg_ref[:, d:2 * d] += jnp.sum(dl2, axis=0, keepdims=True)

    b1 = b_gate[:d].reshape(1, d)
    b2 = b_gate[d:].reshape(1, d)
    return _call("gate_bwd", body, (r // br,),
                 [(proj_r, _rows(br, d, cb)), (proj_r, _rows(br, d, cb + 1)), (b1, _whole((1, d))), (b2, _whole((1, d))),
                  (bm, _rows(br, d)), (bfx, _rows(br, d)), (dmerged, _rows(br, d))],
                 [(_sds((r, d), MXU_DTYPE), _rows(br, d)), (_sds((r, d), MXU_DTYPE), _rows(br, d)),
                  (_sds((r, 2 * d), MXU_DTYPE), _rows(br, 2 * d)), (_sds((1, 2 * d), F32), _whole((1, 2 * d)))],
                 sem=("arbitrary",))


HALO = 16
GLU_BWD_BLOCK = 128


def _conv_taps(gp, halo, first_block):
    halo = jnp.where(first_block, 0.0, halo.astype(F32))
    rid = lax.broadcasted_iota(jnp.int32, gp.shape, 0)
    last, prev = halo[HALO - 1:HALO, :], halo[HALO - 2:HALO - 1, :]
    g1 = jnp.where(rid == 0, last, pltpu.roll(gp, 1, axis=0))
    g2 = jnp.where(rid == 0, prev, jnp.where(rid == 1, last, pltpu.roll(gp, 2, axis=0)))
    return g1, g2


def _prev_halo(br, c):
    return pl.BlockSpec((HALO, c), lambda i: (jnp.maximum(i * (br // HALO) - 1, 0), 0))


def _glu_fwd(up, conv_w, conv_b):
    r = up.shape[0]
    c = D_FF
    br = ROW_BLOCK

    def body(gp_ref, halo_ref, val_ref, w_ref, b_ref, o_ref):
        gp = gp_ref[...].astype(F32)
        g1, g2 = _conv_taps(gp, halo_ref[...], pl.program_id(0) == 0)
        gate = w_ref[0:1, :] * g2 + w_ref[1:2, :] * g1 + w_ref[2:3, :] * gp + b_ref[...]
        o_ref[...] = (gate * jax.nn.sigmoid(gate) * val_ref[...].astype(F32)).astype(o_ref.dtype)

    return _call("glu_fwd", body, (r // br,),
                 [(up, _rows(br, c, 0)), (up, _prev_halo(br, c)), (up, _rows(br, c, 1)),
                  (conv_w, _whole((3, c))), (conv_b.reshape(1, c), _whole((1, c)))],
                 [(_sds((r, c), MXU_DTYPE), _rows(br, c))], sem=("parallel",))[0]


def _glu_bwd(up, conv_w, conv_b, d_act):
    r = up.shape[0]
    c = D_FF
    br = GLU_BWD_BLOCK
    nb = r // br

    def body(gp_ref, halo_ref, val_ref, da_ref, gpn_ref, valn_ref, dan_ref, w_ref, b_ref, o_ref, dw_ref, db_ref):
        i = pl.program_id(0)
        w0, w1, w2, bias = w_ref[0:1, :], w_ref[1:2, :], w_ref[2:3, :], b_ref[...]

        def d_gate(gp, g1, g2, val, da):
            gate = w0 * g2 + w1 * g1 + w2 * gp + bias
            sg = jax.nn.sigmoid(gate)
            return da * val * (sg * (1.0 + gate * (1.0 - sg))), da * (gate * sg)

        gp = gp_ref[...].astype(F32)
        g1, g2 = _conv_taps(gp, halo_ref[...], i == 0)
        dg, dv = d_gate(gp, g1, g2, val_ref[...].astype(F32), da_ref[...].astype(F32))
        gpn = gpn_ref[...].astype(F32)
        g1n, g2n = _conv_taps(gpn, gp[br - HALO:, :], False)
        dgn, _ = d_gate(gpn, g1n, g2n, valn_ref[...].astype(F32), dan_ref[...].astype(F32))
        dgn = jnp.where(i == nb - 1, 0.0, dgn)
        rid = lax.broadcasted_iota(jnp.int32, dg.shape, 0)
        u1 = jnp.where(rid == br - 1, dgn[0:1, :], pltpu.roll(dg, br - 1, axis=0))
        u2 = jnp.where(rid == br - 1, dgn[1:2, :], jnp.where(rid == br - 2, dgn[0:1, :], pltpu.roll(dg, br - 2, axis=0)))
        o_ref[:, 0:c] = (w2 * dg + w1 * u1 + w0 * u2).astype(o_ref.dtype)
        o_ref[:, c:2 * c] = dv.astype(o_ref.dtype)

        @pl.when(i == 0)
        def _():
            dw_ref[...] = jnp.zeros_like(dw_ref)
            db_ref[...] = jnp.zeros_like(db_ref)

        dw_ref[0:1, :] += jnp.sum(dg * g2, axis=0, keepdims=True)
        dw_ref[1:2, :] += jnp.sum(dg * g1, axis=0, keepdims=True)
        dw_ref[2:3, :] += jnp.sum(dg * gp, axis=0, keepdims=True)
        db_ref[...] += jnp.sum(dg, axis=0, keepdims=True)

    nxt = lambda cb: pl.BlockSpec((HALO, c), lambda i: (jnp.minimum((i + 1) * (br // HALO), r // HALO - 1), cb))
    return _call("glu_bwd", body, (nb,),
                 [(up, _rows(br, c, 0)), (up, _prev_halo(br, c)), (up, _rows(br, c, 1)), (d_act, _rows(br, c)),
                  (up, nxt(0)), (up, nxt(1)), (d_act, nxt(0)),
                  (conv_w, _whole((3, c))), (conv_b.reshape(1, c), _whole((1, c)))],
                 [(_sds((r, 2 * c), MXU_DTYPE), _rows(br, 2 * c)),
                  (_sds((8, c), F32), _whole((8, c))), (_sds((1, c), F32), _whole((1, c)))],
                 sem=("arbitrary",))


def _token_specs(seq, d):
    br = ROW_BLOCK
    nxb = seq // br
    main = pl.BlockSpec((br, d), lambda i: (jnp.minimum(i, nxb - 1), 0))
    tail = pl.BlockSpec((N_META, d), lambda i: (jnp.clip(i * (br // N_META) - 1, 0, seq // N_META - 1), 0))
    return main, tail


def _padded_block(main_ref, tail_ref, first, seq):
    br = ROW_BLOCK
    i = pl.program_id(0)
    nxb = seq // br
    main = jnp.where(i < nxb, main_ref[...], 0.0)
    head = jnp.where(i == 0, first, jnp.where(i <= nxb, tail_ref[...], 0.0))
    return jnp.concatenate([head, main[:br - N_META]], axis=0)


def _ln_emb_fwd(x, meta, g, b, rows, after=()):
    seq, d = x.shape
    br = ROW_BLOCK
    assert seq % br == 0 and br % N_META == 0 and rows % br == 0

    def body(x_ref, tail_ref, meta_ref, g_ref, b_ref, y_ref, yb_ref):
        z = _padded_block(x_ref, tail_ref, meta_ref[...], seq)
        xhat, _ = _ln_stats(z)
        y = xhat * g_ref[...] + b_ref[...]
        y_ref[...] = y
        yb_ref[...] = y.astype(yb_ref.dtype)

    main, tail = _token_specs(seq, d)
    return _call("ln_emb_fwd", body, (rows // br,),
                 [(x, main), (x, tail), (meta, _whole((N_META, d))), (g.reshape(1, d), _whole((1, d))),
                  (b.reshape(1, d), _whole((1, d)))],
                 [(_sds((rows, d), F32), _rows(br, d)), (_sds((rows, d), MXU_DTYPE), _rows(br, d))],
                 sem=("parallel",), after=after)


def _ln_emb_bwd(x, meta, dh0, g):
    seq, d = x.shape
    br = ROW_BLOCK
    step = br // N_META

    def ln_bwd(z, dy, gv):
        xhat, rstd = _ln_stats(z)
        dyg = dy * gv
        m1 = jnp.mean(dyg, axis=-1, keepdims=True)
        m2 = jnp.mean(dyg * xhat, axis=-1, keepdims=True)
        dz = rstd * (dyg - m1 - xhat * m2)
        return dz, jnp.sum(dy * xhat, axis=0, keepdims=True), jnp.sum(dy, axis=0, keepdims=True)

    def body(x_ref, dh_ref, nxt_ref, meta_ref, top_ref, g_ref, dx_ref, dm_ref, dg_ref, db_ref):
        gv = g_ref[...]
        dy = jnp.concatenate([dh_ref[N_META:, :], nxt_ref[...]], axis=0)
        dz, dg, db = ln_bwd(x_ref[...], dy, gv)
        dx_ref[...] = dz

        @pl.when(pl.program_id(0) == 0)
        def _():
            dzm, dgm, dbm = ln_bwd(meta_ref[...], top_ref[...], gv)
            dm_ref[...] = dzm
            dg_ref[...] = dgm
            db_ref[...] = dbm

        dg_ref[...] += dg
        db_ref[...] += db

    small = _whole((N_META, d))
    return _call("ln_emb_bwd", body, (seq // br,),
                 [(x, _rows(br, d)), (dh0, _rows(br, d)), (dh0, pl.BlockSpec((N_META, d), lambda i: ((i + 1) * step, 0))),
                  (meta, small), (dh0, small), (g.reshape(1, d), _whole((1, d)))],
                 [(_sds((seq, d), F32), _rows(br, d)), (_sds((N_META, d), F32), small),
                  (_sds((1, d), F32), _whole((1, d))), (_sds((1, d), F32), _whole((1, d)))], sem=("arbitrary",))


def _ln_ffn_loss(h1, f, tgt, g, b):
    r, d = h1.shape
    seq = tgt.shape[0]
    br = ROW_BLOCK

    def body(a_ref, r_ref, t_ref, tail_ref, g_ref, b_ref, l_ref):
        err = _loss_err(a_ref, r_ref, t_ref, tail_ref, g_ref, b_ref, seq)[0]

        @pl.when(pl.program_id(0) == 0)
        def _():
            l_ref[...] = jnp.zeros_like(l_ref)

        l_ref[...] += jnp.sum(jnp.sum(err * err, axis=1, keepdims=True), axis=0, keepdims=True) * (0.5 / d)

    main, tail = _token_specs(seq, d)
    return _call("ln_ffn_loss", body, (r // br,),
                 [(h1, _rows(br, d)), (f, _rows(br, d)), (tgt, main), (tgt, tail),
                  (g.reshape(1, d), _whole((1, d))), (b.reshape(1, d), _whole((1, d)))],
                 [(_sds((1, 1), F32), _whole((1, 1)))], sem=("arbitrary",))[0]


def _loss_err(a_ref, r_ref, t_ref, tail_ref, g_ref, b_ref, seq):
    br, d = a_ref.shape
    xhat, rstd = _ln_stats(ALPHA * a_ref[...] + r_ref[...])
    y = xhat * g_ref[...] + b_ref[...]
    t = _padded_block(t_ref, tail_ref, jnp.zeros((N_META, d), F32), seq)
    rid = lax.broadcasted_iota(jnp.int32, (br, d), 0) + pl.program_id(0) * br
    valid = (rid >= N_META) & (rid < N_META + seq)
    return jnp.where(valid, y - t, 0.0), xhat, rstd


def _ln_ffn_bwd(h1, f, tgt, g, b):
    r, d = h1.shape
    seq = tgt.shape[0]
    br = ROW_BLOCK

    def body(a_ref, r_ref, t_ref, tail_ref, g_ref, b_ref, dz_ref, dzb_ref, dg_ref, db_ref):
        err, xhat, rstd = _loss_err(a_ref, r_ref, t_ref, tail_ref, g_ref, b_ref, seq)
        dyv = err * (1.0 / d)
        dyg = dyv * g_ref[...]
        m1 = jnp.mean(dyg, axis=-1, keepdims=True)
        m2 = jnp.mean(dyg * xhat, axis=-1, keepdims=True)
        dz = rstd * (dyg - m1 - xhat * m2)
        dz_ref[...] = dz
        dzb_ref[...] = dz.astype(dzb_ref.dtype)

        @pl.when(pl.program_id(0) == 0)
        def _():
            dg_ref[...] = jnp.zeros_like(dg_ref)
            db_ref[...] = jnp.zeros_like(db_ref)

        dg_ref[...] += jnp.sum(dyv * xhat, axis=0, keepdims=True)
        db_ref[...] += jnp.sum(dyv, axis=0, keepdims=True)

    main, tail = _token_specs(seq, d)
    return _call("ln_ffn_bwd", body, (r // br,),
                 [(h1, _rows(br, d)), (f, _rows(br, d)), (tgt, main), (tgt, tail),
                  (g.reshape(1, d), _whole((1, d))), (b.reshape(1, d), _whole((1, d)))],
                 [(_sds((r, d), F32), _rows(br, d)), (_sds((r, d), MXU_DTYPE), _rows(br, d)),
                  (_sds((1, d), F32), _whole((1, d))), (_sds((1, d), F32), _whole((1, d)))], sem=("arbitrary",))


def _attn_fwd(name, q, k, v, cum_b=None, cum_t=None):
    (qa, qg), (ka, kg), (va, vg) = q, k, v
    r = qa.shape[0]
    tq, tk = ATT_TQ, ATT_TK
    nq, nk = r // tq, r // tk
    bias = cum_b is not None

    def body(*refs):
        if bias:
            q_ref, k_ref, vt_ref, cb_ref, ct_ref, o_ref, ob_ref, lse_ref = refs
        else:
            q_ref, k_ref, vt_ref, o_ref, ob_ref, lse_ref = refs
        i = pl.program_id(1)
        qs = [q_ref[:, _hs(hh)] for hh in range(hg)]
        cqs = [ct_ref[hh] for hh in range(hg)] if bias else None
        diff = lax.broadcasted_iota(jnp.int32, (tk, tq), 0) - lax.broadcasted_iota(jnp.int32, (tk, tq), 1)

        def step(j, carry, masked):
            keys = pl.ds(pl.multiple_of(j * tk, tk), tk)
            out = []
            for hh in range(hg):
                m, l, acc = carry[hh]
                kt = k_ref[keys, _hs(hh)]
                s = lax.dot_general(kt, qs[hh], NT, preferred_element_type=F32)
                if bias:
                    s = s + (cqs[hh] - cb_ref[keys, hh * HP:hh * HP + 1])
                if masked:
                    s = jnp.where(diff <= i * tq - j * tk, s, NEG_INF)
                m_new = jnp.maximum(m, jnp.max(s, axis=0, keepdims=True))
                p = jnp.exp(s - m_new)
                a = jnp.exp(m - m_new)
                l = a * l + jnp.sum(p, axis=0, keepdims=True)
                acc = a * acc + jnp.dot(vt_ref[j, _hs(hh), :], p.astype(kt.dtype), preferred_element_type=F32)
                out.append((m_new, l, acc))
            return tuple(out)

        n_clear = (i * tq + 1) // tk
        n_all = ((i + 1) * tq - 1) // tk + 1
        carry = tuple((jnp.full((1, tq), NEG_INF, F32), jnp.zeros((1, tq), F32), jnp.zeros((HP, tq), F32))
                      for _ in range(hg))
        carry = lax.fori_loop(0, n_clear, lambda j, c: step(j, c, False), carry)
        carry = lax.fori_loop(n_clear, n_all, lambda j, c: step(j, c, True), carry)
        for hh in range(hg):
            m, l, acc = carry[hh]
            o = (acc / l).T
            o_ref[:, _hs(hh)] = o
            ob_ref[:, _hs(hh)] = o.astype(ob_ref.dtype)
            lse_ref[hh] = m + jnp.log(l)

    hg = ATT_HEADS
    w = hg * HP
    gpw = HW // w
    tile = lambda g: pl.BlockSpec((tq, w), lambda h, i: (i, g * gpw + h))
    res = lambda g: pl.BlockSpec((r, w), lambda h, i: (0, g * gpw + h))
    v_t = _key_tiles_transposed(name + "_vt", va, vg)
    ins = [(qa, tile(qg)), (ka, res(kg)), (v_t, pl.BlockSpec((nk, w, tk), lambda h, i: (0, h, 0)))]
    if bias:
        ins += [(cum_b, res(0)),
                (cum_t.reshape(HEADS, nq, 1, tq), pl.BlockSpec((hg, None, 1, tq), lambda h, i: (h, i, 0, 0)))]
    outs = [(_sds((r, HW), F32), tile(0)), (_sds((r, HW), MXU_DTYPE), tile(0)),
            (_sds((HEADS, nq, 1, tq), F32), pl.BlockSpec((hg, None, 1, tq), lambda h, i: (h, i, 0, 0)))]
    o, ob, lse = _call(name, body, (gpw, nq), ins, outs, sem=("parallel", "parallel"))
    return o, ob, lse.reshape(HEADS, r)


def _key_tiles_transposed(name, a, group):
    r = a.shape[0]
    tk = ATT_TK

    def body(x_ref, o_ref):
        for h in range(HEADS):
            o_ref[_hs(h), :] = x_ref[:, _hs(h)].astype(F32).T.astype(o_ref.dtype)

    return _call(name, body, (r // tk,),
                 [(a, pl.BlockSpec((tk, HW), lambda j: (j, group)))],
                 [(_sds((r // tk, HW, tk), a.dtype), pl.BlockSpec((None, HW, tk), lambda j: (j, 0, 0)))],
                 sem=("parallel",))[0]


def _attn_delta(name, do_b, o, after=()):
    r = do_b.shape[0]
    br = ROW_BLOCK

    def body(do_ref, o_ref, d_ref):
        lane = _lane_iota((br, HP))
        d = jnp.zeros((br, HP), F32)
        for h in range(HEADS):
            dh = do_ref[:, _hs(h)].astype(F32)
            d = jnp.where(lane == h, jnp.sum(dh * o_ref[:, _hs(h)], axis=1, keepdims=True), d)
        d_ref[...] = d.T[0:HEADS, :]

    wide = _rows(br, HW)
    return _call(name, body, (r // br,), [(do_b, wide), (o, wide)],
                 [(_sds((HEADS, r), F32), pl.BlockSpec((HEADS, br), lambda i: (0, i)))],
                 sem=("parallel",), after=after)[0]


def _attn_bwd(name, q, k, v, do_b, lse_t, delta_t, cum_b=None, cum_t=None, out_dtype=F32):
    (qa, qg), (ka, kg), (va, vg) = q, k, v
    r = qa.shape[0]
    tq, tk = ATT_TQ, ATT_TK
    nq, nk = r // tq, r // tk
    bias = cum_b is not None

    def body(*refs):
        if bias:
            (q_ref, k_ref, v_ref, do_ref, lse_ref, dl_ref, cb_ref, ct_ref,
             dq_ref, dk_ref, dv_ref, dcq_ref, dck_ref, dqt_ref) = refs
        else:
            q_ref, k_ref, v_ref, do_ref, lse_ref, dl_ref, dq_ref, dk_ref, dv_ref, dqt_ref = refs
        j = pl.program_id(1)

        @pl.when(j == 0)
        def _():
            dqt_ref[...] = jnp.zeros_like(dqt_ref)
            if bias:
                dcq_ref[...] = jnp.zeros_like(dcq_ref)

        kts = [k_ref[:, _hs(hh)] for hh in range(hg)]
        vts = [v_ref[:, _hs(hh)] for hh in range(hg)]
        k_trs = [kt.astype(F32).T.astype(kt.dtype) for kt in kts]
        cks = [cb_ref[:, hh * HP:hh * HP + 1] for hh in range(hg)] if bias else None
        diff = lax.broadcasted_iota(jnp.int32, (tk, tq), 0) - lax.broadcasted_iota(jnp.int32, (tk, tq), 1)

        def step(i, carry, masked):
            rows = pl.ds(pl.multiple_of(i * tq, tq), tq)
            out = []
            for hh in range(hg):
                dk_acc, dv_acc, dck_acc = carry[hh]
                qt = q_ref[rows, _hs(hh)]
                dot = do_ref[rows, _hs(hh)]
                s = lax.dot_general(kts[hh], qt, NT, preferred_element_type=F32)
                if bias:
                    s = s + (ct_ref[hh, i] - cks[hh])
                if masked:
                    s = jnp.where(diff <= i * tq - j * tk, s, NEG_INF)
                p = jnp.exp(s - lse_ref[hh, i])
                dp = lax.dot_general(vts[hh], dot, NT, preferred_element_type=F32)
                ds = p * (dp - dl_ref[hh, i])
                pb = p.astype(dot.dtype)
                dsb = ds.astype(qt.dtype)
                dv_acc = dv_acc + jnp.dot(pb, dot, preferred_element_type=F32)
                dk_acc = dk_acc + jnp.dot(dsb, qt, preferred_element_type=F32)
                dqt_ref[hh, i] += jnp.dot(k_trs[hh], dsb, preferred_element_type=F32)
                if bias:
                    dcq_ref[hh, i] += jnp.sum(ds, axis=0, keepdims=True)
                    dck_acc = dck_acc - jnp.sum(ds, axis=1, keepdims=True)
                out.append((dk_acc, dv_acc, dck_acc))
            return tuple(out)

        i_first = (j * tk) // tq
        i_clear = jnp.minimum(((j + 1) * tk + tq - 2) // tq, nq)
        carry = tuple((jnp.zeros((tk, HP), F32), jnp.zeros((tk, HP), F32), jnp.zeros((tk, 1), F32)) for _ in range(hg))
        carry = lax.fori_loop(i_first, i_clear, lambda i, c: step(i, c, True), carry)
        carry = lax.fori_loop(i_clear, nq, lambda i, c: step(i, c, False), carry)
        for hh in range(hg):
            dk_acc, dv_acc, dck_acc = carry[hh]
            dk_ref[:, _hs(hh)] = dk_acc.astype(dk_ref.dtype)
            dv_ref[:, _hs(hh)] = dv_acc.astype(dv_ref.dtype)
            if bias:
                dck_ref[:, _hs(hh)] = jnp.broadcast_to(dck_acc, (tk, HP))

        @pl.when(j == nk - 1)
        def _():
            for hh in range(hg):
                for i in range(nq):
                    dq_ref[i * tq:(i + 1) * tq, _hs(hh)] = dqt_ref[hh, i].T.astype(dq_ref.dtype)

    hg = ATT_HEADS
    w = hg * HP
    gpw = HW // w
    res = lambda g: pl.BlockSpec((r, w), lambda h, j: (0, g * gpw + h))
    tile = lambda g: pl.BlockSpec((tk, w), lambda h, j: (j, g * gpw + h))
    rowv = pl.BlockSpec((hg, nq, 1, tq), lambda h, j: (h, 0, 0, 0))
    as_rows = lambda a: a.reshape(HEADS, nq, 1, tq)
    ins = [(qa, res(qg)), (ka, tile(kg)), (va, tile(vg)), (do_b, res(0)), (as_rows(lse_t), rowv), (as_rows(delta_t), rowv)]
    outs = [(_sds((r, HW), out_dtype), res(0)), (_sds((r, HW), out_dtype), tile(0)), (_sds((r, HW), out_dtype), tile(0))]
    if bias:
        ins += [(cum_b, tile(0)), (as_rows(cum_t), rowv)]
        outs += [(_sds((HEADS, nq, 1, tq), F32), rowv), (_sds((r, HW), F32), tile(0))]
    res_out = _call(name, body, (gpw, nk), ins, outs, scratch=[pltpu.VMEM((hg, nq, HP, tq), F32)],
                    sem=("parallel", "arbitrary"))
    if bias:
        dq, dk, dv, dcq, dck = res_out
        return dq, dk, dv, dcq.reshape(HEADS, r), dck
    return res_out


MESH_ID = pl.DeviceIdType.MESH
ANY = pl.BlockSpec(memory_space=pl.ANY)


N_GATHER_COPIES = 8


def _allgather(name, shards):
    n = len(shards)

    def body(*refs):
        x_refs, out_refs = refs[:n], refs[n:2 * n]
        send_sems, recv_sems, local_sems = refs[2 * n:]
        x, y, c = lax.axis_index("x"), lax.axis_index("y"), lax.axis_index("c")
        me, sibling = (x, y, c), (x, y, 1 - c)
        xn, yn, dg = (1 - x, y, c), (x, 1 - y, c), (1 - x, 1 - y, c)
        other = lambda dev: (dev[0], dev[1], 1 - c)

        def slot(ti, dev, half=None):
            ref = out_refs[ti].at[4 * dev[0] + 2 * dev[1] + dev[2]]
            if half is None:
                return ref
            rows = shards[ti].shape[0] // 2
            return ref.at[pl.ds(half * rows, rows)]

        def copy(ti, k, block, to, half=None, src=None):
            return pltpu.make_async_remote_copy(
                src_ref=slot(ti, block, half) if src is None else src, dst_ref=slot(ti, block, half),
                send_sem=send_sems.at[ti, k], recv_sem=recv_sems.at[ti, k], device_id=to, device_id_type=MESH_ID)

        mine = [pltpu.make_async_copy(x_refs[ti], slot(ti, me), local_sems.at[ti]) for ti in range(n)]
        for cp in mine:
            cp.start()
        started = []

        def go(cp):
            cp.start()
            started.append(cp)

        for ti in range(n):
            go(copy(ti, 0, me, sibling, src=x_refs[ti]))
            go(copy(ti, 1, me, xn, src=x_refs[ti]))
            go(copy(ti, 2, me, yn, src=x_refs[ti]))
        for ti in range(n):
            copy(ti, 1, xn, me).wait_recv()
            go(copy(ti, 3, xn, yn, half=0))
            go(copy(ti, 5, xn, sibling))
            copy(ti, 2, yn, me).wait_recv()
            go(copy(ti, 4, yn, xn, half=1))
            go(copy(ti, 6, yn, sibling))
        for ti in range(n):
            copy(ti, 3, dg, me, half=0).wait_recv()
            copy(ti, 4, dg, me, half=1).wait_recv()
            go(copy(ti, 7, dg, sibling))
        for ti in range(n):
            copy(ti, 0, sibling, me).wait_recv()
            for k, dev in ((5, xn), (6, yn), (7, dg)):
                copy(ti, k, other(dev), me).wait_recv()
        for cp in started:
            cp.wait_send()
        for cp in mine:
            cp.wait()

    sems = pltpu.SemaphoreType.DMA((n, N_GATHER_COPIES))
    return pl.pallas_call(
        body, name=name, out_shape=[_sds((N_DEV,) + s.shape, s.dtype) for s in shards],
        in_specs=[ANY] * n, out_specs=[ANY] * n,
        scratch_shapes=[sems, sems, pltpu.SemaphoreType.DMA((n,))],
    )(*shards)


HBM = pl.BlockSpec(memory_space=pltpu.HBM)
SEM = pl.BlockSpec(memory_space=pltpu.SEMAPHORE)
EFFECT = pltpu.SideEffectType.DATAFLOW_SIDE_EFFECTING
N_PEER = N_DEV - 1


def _my_id():
    return 4 * lax.axis_index("x") + 2 * lax.axis_index("y") + lax.axis_index("c")


def _peers():
    x, y, c = lax.axis_index("x"), lax.axis_index("y"), lax.axis_index("c")
    out = []
    for k in range(1, N_DEV):
        px, py, pc = (1 - x if k & 4 else x, 1 - y if k & 2 else y, 1 - c if k & 1 else c)
        out.append(((px, py, pc), 4 * px + 2 * py + pc))
    return out


def _push_copies(src_refs, land_refs, send_sems, recv_sems, scatter, landing):
    me = _my_id()
    out = []
    for ti, (src, land) in enumerate(zip(src_refs, land_refs)):
        for k, (dev, pid) in enumerate(_peers()):
            out.append(pltpu.make_async_remote_copy(
                src_ref=src.at[pid] if scatter else src, dst_ref=land.at[pid if landing else me],
                send_sem=send_sems.at[ti * N_PEER + k], recv_sem=recv_sems.at[ti * N_PEER + k],
                device_id=dev, device_id_type=MESH_ID))
    return out


def _push_start(name, srcs, scatter, after=None):
    n = len(srcs)
    slot = lambda s: s.shape[1:] if scatter else s.shape
    lands = [lax.empty((N_DEV,) + slot(s), s.dtype) for s in srcs]
    n_after = 0 if after is None else 1

    def body(*refs):
        src_refs, land_refs = refs[:n], refs[n:2 * n]
        send_sems, recv_sems = refs[2 * n + n_after], refs[2 * n + n_after + 1]
        token = refs[-1]
        for cp in _push_copies(src_refs, land_refs, send_sems, recv_sems, scatter, False):
            cp.start()
        token[...] = jnp.zeros_like(token)

    hbm = lambda a: pltpu.with_memory_space_constraint(a, pltpu.HBM)
    operands = [hbm(a) for a in srcs + lands] + ([after] if n_after else [])
    res = pl.pallas_call(
        body, name=name,
        out_shape=[pltpu.SemaphoreType.DMA((n * N_PEER,)), pltpu.SemaphoreType.DMA((n * N_PEER,))]
        + [pltpu.HBM(a.shape, a.dtype) for a in srcs + lands] + [_sds((8, 128), F32)],
        in_specs=[HBM] * (2 * n) + [ANY] * n_after,
        out_specs=[SEM, SEM] + [HBM] * (2 * n) + [pl.BlockSpec(memory_space=pltpu.VMEM)],
        input_output_aliases={i: 2 + i for i in range(2 * n)},
        compiler_params=pltpu.CompilerParams(has_side_effects=EFFECT),
    )(*operands)
    return (res[0], res[1], list(res[2:2 + n]), list(res[2 + n:2 + 2 * n]), scatter), res[-1]


def _push_wait(name, handle, after):
    send_sems, recv_sems, srcs, lands, scatter = handle
    n = len(srcs)

    def body(*refs):
        src_refs, land_refs = refs[:n], refs[n:2 * n]
        s_sems, r_sems = refs[2 * n], refs[2 * n + 1]
        for cp in _push_copies(src_refs, land_refs, s_sems, r_sems, scatter, True):
            cp.wait_send()
            cp.wait_recv()

    res = pl.pallas_call(
        body, name=name,
        out_shape=[pltpu.HBM(a.shape, a.dtype) for a in srcs + lands],
        in_specs=[HBM] * (2 * n) + [SEM, SEM, ANY], out_specs=[HBM] * (2 * n),
        input_output_aliases={i: i for i in range(2 * n)},
        compiler_params=pltpu.CompilerParams(has_side_effects=EFFECT),
    )(*srcs, *lands, send_sems, recv_sems, after)
    return list(res[n:])


def _adamw(name, parts, w, m, v, own=None):
    r, c = w.shape
    br = _pick(r, 256, 16)
    has_own = own is not None

    def body(*refs):
        if has_own:
            p_ref, own_ref, w_ref, m_ref, v_ref, g_ref, d_ref, nm_ref, nv_ref = refs
            me = _my_id()
            mine = own_ref[...].astype(F32)
        else:
            p_ref, w_ref, m_ref, v_ref, g_ref, d_ref, nm_ref, nv_ref = refs
        g = None
        for k in range(N_DEV):
            t = p_ref[k].astype(F32)
            if has_own:
                t = jnp.where(me == k, mine, t)
            g = t if g is None else g + t
        mm = ADAM_B1 * m_ref[...] + (1.0 - ADAM_B1) * g
        vv = ADAM_B2 * v_ref[...] + (1.0 - ADAM_B2) * (g * g)
        m_hat = mm / (1.0 - ADAM_B1 ** ADAM_STEP)
        v_hat = vv / (1.0 - ADAM_B2 ** ADAM_STEP)
        g_ref[...] = g
        d_ref[...] = -ADAM_LR * (m_hat / (jnp.sqrt(v_hat) + ADAM_EPS) + ADAM_WD * w_ref[...])
        nm_ref[...] = mm
        nv_ref[...] = vv

    spec = _rows(br, c)
    out = (_sds((r, c), F32), spec)
    ins = [(parts, pl.BlockSpec((N_DEV, br, c), lambda i: (0, i, 0)))] + ([(own, spec)] if has_own else [])
    return _call(name, body, (r // br,), ins + [(w, spec), (m, spec), (v, spec)], [out] * 4, sem=("parallel",))


def _pad_head_cols(w, d):
    k = w.shape[0]
    return jnp.pad(w.reshape(k, HEADS, d), ((0, 0), (0, 0), (0, HP - d))).reshape(k, HW)


def _unpad_head_cols(wp, d):
    k = wp.shape[0]
    return wp.reshape(k, HEADS, HP)[:, :, :d].reshape(k, HEADS * d)


def _pad_head_rows(w, d):
    n = w.shape[1]
    return jnp.pad(w.reshape(HEADS, d, n), ((0, 0), (0, HP - d), (0, 0))).reshape(HW, n)


def _unpad_head_rows(wp, d):
    n = wp.shape[1]
    return wp.reshape(HEADS, HP, n)[:, :d, :].reshape(HEADS * d, n)


def _w_in_runs():
    nat = {}
    o = 0
    for nm, wd in (("q", Q_RANK), ("kv", KV_RANK), ("kr", ROPE), ("fq", FOX_W), ("fk", FOX_W), ("fv", FOX_W),
                   ("fl", HEADS), ("gate", 2 * D_MODEL)):
        nat[nm] = o
        o += wd
    runs = [(1, R_QLAT, nat["q"], Q_RANK, 1.0), (1, R_KVLAT, nat["kv"], KV_RANK, 1.0),
            (1, R_LAST + LANE_FL, nat["fl"], HEADS, 1.0), (1, R_LAST + LANE_PE, nat["kr"], ROPE, 1.0),
            (1, R_GATE, nat["gate"], 2 * D_MODEL, 1.0)]
    for grp, (nm, sc) in enumerate((("fq", FOX_SCALE), ("fk", 1.0), ("fv", 1.0))):
        runs += [(0, grp * HW + h * HP, nat[nm] + h * FOX_DIM, FOX_DIM, sc) for h in range(HEADS)]
    return runs


def _sharded_runs(runs, shard_cols):
    out = []
    for half, col, ncol, width, sc in runs:
        while width > 0:
            d, local = divmod(ncol, shard_cols)
            wd = min(width, shard_cols - local)
            out.append((half, col, d, local, wd, sc))
            col, ncol, width = col + wd, ncol + wd, width - wd
    return out


def _remap(name, srcs, out_shapes, moves):
    rows = srcs[0].shape[-2]
    br = _pick(rows, 256, 16)
    ns = len(srcs)

    def spec(shape):
        if len(shape) == 2:
            return pl.BlockSpec((br, shape[1]), lambda i: (i, 0))
        return pl.BlockSpec((shape[0], br, shape[2]), lambda i: (0, i, 0))

    def body(*refs):
        s_refs, o_refs = refs[:ns], refs[ns:]
        for o in o_refs:
            o[...] = jnp.zeros_like(o)
        for di, dl, dc, si, sl, sc0, wd, scale in moves:
            v = s_refs[si][:, sc0:sc0 + wd] if sl is None else s_refs[si][sl, :, sc0:sc0 + wd]
            if scale != 1.0:
                v = v * jnp.asarray(scale, v.dtype)
            v = v.astype(o_refs[di].dtype)
            if dl is None:
                o_refs[di][:, dc:dc + wd] = v
            else:
                o_refs[di][dl, :, dc:dc + wd] = v

    return _call(name, body, (rows // br,), [(a, spec(a.shape)) for a in srcs],
                 [(_sds(shape, dt), spec(shape)) for shape, dt in out_shapes], sem=("parallel",))


def _w_in_from_shards(g3):
    n, rows, c = g3.shape
    moves = [(half, None, col, 0, d, local, wd, sc) for half, col, d, local, wd, sc in _sharded_runs(_w_in_runs(), c)]
    return _remap("w_in_repack", [g3], [((rows, F_W), g3.dtype), ((rows, R_W), g3.dtype)], moves)


def _w_in_grad_to_shards(d_fused, d_rest, n, c):
    rows = d_fused.shape[0]
    moves = [(0, d, local, half, None, col, wd, sc) for half, col, d, local, wd, sc in _sharded_runs(_w_in_runs(), c)]
    return _remap("w_in_grad_unpack", [d_fused, d_rest], [((n, rows, c), d_fused.dtype)], moves)[0]


def _cols_from_shards(name, land, own):
    n, rows, c = land.shape
    br = _pick(rows, 256, 16)

    def body(land_ref, own_ref, o_ref):
        me = _my_id()
        for d in range(n):
            o_ref[:, c * d:c * (d + 1)] = jnp.where(me == d, own_ref[...], land_ref[d])

    return _call(name, body, (rows // br,),
                 [(land, pl.BlockSpec((n, br, c), lambda i: (0, i, 0))), (own, _rows(br, c))],
                 [(_sds((rows, n * c), land.dtype), _rows(br, n * c))], sem=("parallel",))[0]


def _cols_to_shards(name, full, n):
    rows, nc = full.shape
    c = nc // n
    return _remap(name, [full], [((n, rows, c), full.dtype)], [(0, d, 0, 0, None, c * d, c, 1.0) for d in range(n)])[0]


def _split_w_kv(w):
    k = w.shape[0]
    w3 = w.reshape(k, HEADS, NOPE + V_DIM)
    padl = lambda a: jnp.pad(a, ((0, 0), (0, 0), (0, HP - a.shape[-1]))).reshape(k, HW)
    return padl(w3[..., :NOPE]), padl(w3[..., NOPE:])


def _merge_w_kv(wk, wv):
    k = wk.shape[0]
    return jnp.concatenate([wk.reshape(k, HEADS, HP)[..., :NOPE], wv.reshape(k, HEADS, HP)[..., :V_DIM]],
                           axis=-1).reshape(k, HEADS * (NOPE + V_DIM))


class _NoComm:
    first_token = ()

    def late_weights(self, group, after):
        return {}

    def send(self, name, grads):
        return ()


def _local_step(x, tgt, p, comm=_NoComm()):
    seq = x.shape[0]
    r = -(-(N_META + seq) // ROW_ALIGN) * ROW_ALIGN
    cd = MXU_DTYPE
    p = dict(p)

    w_f, w_r = p["w_in"]

    pos = jnp.arange(r, dtype=F32)
    inv_freq = ROPE_THETA ** (-jnp.arange(HALF, dtype=F32) / HALF)
    ang = pos[:, None] * inv_freq[None, :]
    cos_t = jnp.tile(jnp.cos(ang), (1, HP // HALF))
    sin_t = jnp.tile(jnp.sin(ang), (1, HP // HALF))
    bf_row = jnp.zeros((1, HP), F32).at[0, LANE_FL:LANE_FL + HEADS].set(p["b_forget"])

    h0, h0b = _ln_emb_fwd(x, p["meta_tokens"], p["ln_emb_g"], p["ln_emb_b"], r, after=comm.first_token)
    proj_f = _matmul("in_proj_f", h0b, w_f, out_dtype=cd)
    proj_r = _matmul("in_proj_r", h0b, w_r)
    ql = _rms_fwd("q_norm_fwd", proj_r, R_QLAT // Q_RANK, Q_RANK, p["q_norm_g"])
    kvl = _rms_fwd("kv_norm_fwd", proj_r, R_KVLAT // KV_RANK, KV_RANK, p["kv_norm_g"])
    p.update(comm.late_weights("qkv", ql))
    w_q = _pad_head_cols(p["w_q_up"], QK_DIM)
    w_kv = jnp.concatenate(_split_w_kv(p["w_kv_up"]), axis=1)
    q_raw = _matmul("q_up", ql, w_q)
    kv = _matmul("kv_up", kvl, w_kv, out_dtype=cd)
    q_mla, k_mla = _rope_fwd(q_raw, kv, proj_r, cos_t, sin_t)
    o_mla, o_mla_b, lse_mla = _attn_fwd("mla_fwd", (q_mla, 0), (k_mla, 0), (kv, 1))

    cum, cum_t = _forget_fwd(proj_r, bf_row)
    o_fox, o_fox_b, lse_fox = _attn_fwd("fox_fwd", (proj_f, 0), (proj_f, 1), (proj_f, 2), cum, cum_t)

    p.update(comm.late_weights("mix", o_fox_b))
    w_bm = _pad_head_rows(p["w_branch_mla"], V_DIM)
    w_bf = _pad_head_rows(p["w_branch_fox"], FOX_DIM)
    bm = _matmul("branch_mla", o_mla_b, w_bm, out_dtype=cd)
    bfx = _matmul("branch_fox", o_fox_b, w_bf, out_dtype=cd)
    merged = _gate_fwd(proj_r, p["b_gate"], bm, bfx)
    mix = _matmul("out_proj", merged, p["w_out"])
    h1, h1b = _ln_fwd("ln_mix_fwd", h0, mix, p["ln_mix_g"], p["ln_mix_b"])
    p.update(comm.late_weights("ffn", h1b))
    up = _matmul("ffn_up", h1b, p["w_ffn_up"], out_dtype=cd)
    act = _glu_fwd(up, p["conv_w"], p["conv_b"])
    f = _matmul("ffn_down", act, p["w_ffn_down"])
    loss = _ln_ffn_loss(h1, f, tgt, p["ln_ffn_g"], p["ln_ffn_b"])

    g = {}
    dz2, dz2b, g["ln_ffn_g"], g["ln_ffn_b"] = _ln_ffn_bwd(h1, f, tgt, p["ln_ffn_g"], p["ln_ffn_b"])
    d_act = _matmul("ffn_down_dx", dz2b, p["w_ffn_down"], tb=True, out_dtype=cd)
    g["w_ffn_down"] = _matmul("ffn_down_dw", act, dz2b, ta=True, out_dtype=cd)
    d_up, dcw, g["conv_b"] = _glu_bwd(up, p["conv_w"], p["conv_b"], d_act)
    g["conv_w"] = dcw[:3]
    dh1 = _matmul("ffn_up_dx", d_up, p["w_ffn_up"], tb=True, addend=dz2, alpha=ALPHA)
    g["w_ffn_up"] = _matmul("ffn_up_dw", h1b, d_up, ta=True, out_dtype=cd)
    sent = comm.send("ffn", {n: g[n] for n in ("w_ffn_down", "w_ffn_up", "conv_w")})
    dz1, dz1b, g["ln_mix_g"], g["ln_mix_b"] = _ln_bwd("ln_mix_bwd", h0, mix, dh1, p["ln_mix_g"], after=sent)
    dmerged = _matmul("out_proj_dx", dz1b, p["w_out"], tb=True, out_dtype=cd)
    g["w_out"] = _matmul("out_proj_dw", merged, dz1b, ta=True, out_dtype=cd)
    d_bm, d_bf, d_gl, g["b_gate"] = _gate_bwd(proj_r, p["b_gate"], bm, bfx, dmerged)
    do_mla_b = _matmul("branch_mla_dx", d_bm, w_bm, tb=True, out_dtype=cd)
    g["w_branch_mla"] = _unpad_head_rows(_matmul("branch_mla_dw", o_mla_b, d_bm, ta=True, out_dtype=cd), V_DIM)
    do_fox_b = _matmul("branch_fox_dx", d_bf, w_bf, tb=True, out_dtype=cd)
    g["w_branch_fox"] = _unpad_head_rows(_matmul("branch_fox_dw", o_fox_b, d_bf, ta=True, out_dtype=cd), FOX_DIM)

    sent = comm.send("mix", {n: g[n] for n in ("w_out", "w_branch_mla", "w_branch_fox")})
    dl_mla = _attn_delta("mla_delta", do_mla_b, o_mla, after=sent)
    dq_m, dk_m, dv_m = _attn_bwd("mla_bwd", (q_mla, 0), (k_mla, 0), (kv, 1), do_mla_b, lse_mla, dl_mla)
    dl_fox = _attn_delta("fox_delta", do_fox_b, o_fox)
    dfq, dfk, dfv, dcq, dck = _attn_bwd("fox_bwd", (proj_f, 0), (proj_f, 1), (proj_f, 2), do_fox_b, lse_fox, dl_fox,
                                        cum, cum_t, out_dtype=cd)
    dfl, dbf = _forget_bwd(proj_r, bf_row, dcq, dck)
    g["b_forget"] = dbf[:, LANE_FL:LANE_FL + HEADS]

    dq_b, dkv_b, dlast = _rope_bwd(dq_m, dk_m, dv_m, dfl, cos_t, sin_t)
    d_ql = _matmul("q_up_dx", dq_b, w_q, tb=True)
    d_kvl = _matmul("kv_up_dx", dkv_b, w_kv, tb=True)
    d_qlat, g["q_norm_g"] = _rms_bwd("q_norm_bwd", proj_r, R_QLAT // Q_RANK, Q_RANK, d_ql, p["q_norm_g"])
    d_kvlat, g["kv_norm_g"] = _rms_bwd("kv_norm_bwd", proj_r, R_KVLAT // KV_RANK, KV_RANK, d_kvl, p["kv_norm_g"])
    dproj_f = jnp.concatenate([dfq, dfk, dfv], axis=1)
    dproj_r = jnp.concatenate([d_qlat, d_kvlat, dlast, jnp.zeros((r, R_GATE - R_LAST - HP), cd), d_gl], axis=1)
    g["w_in"] = (_matmul("in_proj_f_dw", h0b, dproj_f, ta=True, out_dtype=cd),
                 _matmul("in_proj_r_dw", h0b, dproj_r, ta=True, out_dtype=cd))
    sent = comm.send("in", {"w_in": g["w_in"]})
    dh0 = _matmul("in_proj_f_dx", dproj_f, w_f, tb=True, addend=dz1, alpha=ALPHA, after=sent)
    g["w_q_up"] = _unpad_head_cols(_matmul("q_up_dw", ql, dq_b, ta=True, out_dtype=cd, after=sent), QK_DIM)
    dw_kv = _matmul("kv_up_dw", kvl, dkv_b, ta=True, out_dtype=cd, after=sent)
    g["w_kv_up"] = _merge_w_kv(dw_kv[:, :HW], dw_kv[:, HW:])
    sent = comm.send("qkv", {n: g[n] for n in ("w_q_up", "w_kv_up")})
    dh0 = _matmul("in_proj_r_dx", dproj_r, w_r, tb=True, addend=dh0, after=sent)
    grad_x, d_meta, g["ln_emb_g"], g["ln_emb_b"] = _ln_emb_bwd(x, p["meta_tokens"], dh0, p["ln_emb_g"])
    return loss, grad_x, d_meta, g


BIG = (("w_in", 1), ("w_q_up", 1), ("w_kv_up", 1), ("w_branch_mla", 1), ("w_branch_fox", 1), ("w_out", 0),
       ("w_ffn_up", 1), ("w_ffn_down", 0))
SMALL_SHARDED = (("meta_tokens", 1), ("conv_w", 1))
EARLY = ("w_in", "meta_tokens")
LATE = {"qkv": ("w_q_up", "w_kv_up", "conv_w"),
        "mix": ("w_branch_mla", "w_branch_fox", "w_out"),
        "ffn": ("w_ffn_up", "w_ffn_down")}
REPLICATED = ("ln_emb_g", "ln_emb_b", "b_gate", "b_forget", "q_norm_g", "kv_norm_g", "ln_mix_g", "ln_mix_b",
              "conv_b", "ln_ffn_g", "ln_ffn_b")
PACK_COLS = 1024


def _pack(flat_list):
    cat = jnp.concatenate(flat_list)
    n = cat.shape[0]
    rows = -(-n // (8 * PACK_COLS)) * 8
    return jnp.pad(cat, (0, rows * PACK_COLS - n)).reshape(rows, PACK_COLS)


def _gathered_full(g3, axis):
    n, r, c = g3.shape
    if axis == 0:
        return g3.reshape(n * r, c)
    return g3.transpose(1, 0, 2).reshape(r, n * c)


def _shard_major(full, axis):
    r, c = full.shape
    if axis == 0:
        return full.reshape(N_DEV, r // N_DEV, c)
    return full.reshape(r, N_DEV, c // N_DEV).transpose(1, 0, 2)


def kernel(x, meta_tokens, ln_emb_g, ln_emb_b, w_in, b_gate, b_forget, q_norm_g, w_q_up, kv_norm_g, w_kv_up, w_branch_mla, w_branch_fox, w_out, ln_mix_g, ln_mix_b, w_ffn_up, conv_w, conv_b, w_ffn_down, ln_ffn_g, ln_ffn_b, loss_target, m_meta_tokens, m_ln_emb_g, m_ln_emb_b, m_w_in, m_b_gate, m_b_forget, m_q_norm_g, m_w_q_up, m_kv_norm_g, m_w_kv_up, m_w_branch_mla, m_w_branch_fox, m_w_out, m_ln_mix_g, m_ln_mix_b, m_w_ffn_up, m_conv_w, m_conv_b, m_w_ffn_down, m_ln_ffn_g, m_ln_ffn_b, v_meta_tokens, v_ln_emb_g, v_ln_emb_b, v_w_in, v_b_gate, v_b_forget, v_q_norm_g, v_w_q_up, v_kv_norm_g, v_w_kv_up, v_w_branch_mla, v_w_branch_fox, v_w_out, v_ln_mix_g, v_ln_mix_b, v_w_ffn_up, v_conv_w, v_conv_b, v_w_ffn_down, v_ln_ffn_g, v_ln_ffn_b):
    names = ("meta_tokens", "ln_emb_g", "ln_emb_b", "w_in", "b_gate", "b_forget", "q_norm_g", "w_q_up", "kv_norm_g",
             "w_kv_up", "w_branch_mla", "w_branch_fox", "w_out", "ln_mix_g", "ln_mix_b", "w_ffn_up", "conv_w", "conv_b",
             "w_ffn_down", "ln_ffn_g", "ln_ffn_b")
    w_args = (meta_tokens, ln_emb_g, ln_emb_b, w_in, b_gate, b_forget, q_norm_g, w_q_up, kv_norm_g, w_kv_up,
              w_branch_mla, w_branch_fox, w_out, ln_mix_g, ln_mix_b, w_ffn_up, conv_w, conv_b, w_ffn_down, ln_ffn_g, ln_ffn_b)
    m_args = (m_meta_tokens, m_ln_emb_g, m_ln_emb_b, m_w_in, m_b_gate, m_b_forget, m_q_norm_g, m_w_q_up, m_kv_norm_g,
              m_w_kv_up, m_w_branch_mla, m_w_branch_fox, m_w_out, m_ln_mix_g, m_ln_mix_b, m_w_ffn_up, m_conv_w, m_conv_b,
              m_w_ffn_down, m_ln_ffn_g, m_ln_ffn_b)
    v_args = (v_meta_tokens, v_ln_emb_g, v_ln_emb_b, v_w_in, v_b_gate, v_b_forget, v_q_norm_g, v_w_q_up, v_kv_norm_g,
              v_w_kv_up, v_w_branch_mla, v_w_branch_fox, v_w_out, v_ln_mix_g, v_ln_mix_b, v_w_ffn_up, v_conv_w, v_conv_b,
              v_w_ffn_down, v_ln_ffn_g, v_ln_ffn_b)
    as2d = lambda a: a.reshape((-1, a.shape[-1])) if a.ndim != 1 else a.reshape(1, -1)
    w = {n: as2d(a) for n, a in zip(names, w_args)}
    m = {n: as2d(a) for n, a in zip(names, m_args)}
    v = {n: as2d(a) for n, a in zip(names, v_args)}
    out_shape = {n: a.shape for n, a in zip(names, w_args)}

    axis_of = dict(BIG + SMALL_SHARDED)
    big = set(n for n, _ in BIG)
    wire = lambda n, a: a.astype(MXU_DTYPE) if n in big else a
    my_id = _my_id()

    early = _allgather("gather_early", [wire(n, w[n]) for n in EARLY])
    p = {n: _gathered_full(g3, axis_of[n]) for n, g3 in zip(EARLY, early) if n != "w_in"}
    p["w_in"] = _w_in_from_shards(early[EARLY.index("w_in")])
    for n in REPLICATED:
        p[n] = w[n].reshape(-1)
    late, tokens, prev = {}, [], early[0]
    for group, members in LATE.items():
        src = [wire(n, w[n]) for n in members]
        handle, token = _push_start("gather_" + group + "_start", src, False, after=prev)
        late[group] = (members, src, handle)
        tokens.append(token)
        prev = token
    sent = {}

    class Comm:
        first_token = tuple(tokens)

        def late_weights(self, group, after):
            members, src, handle = late[group]
            lands = _push_wait("gather_" + group + "_wait", handle, after)
            out = {}
            for n, own, land in zip(members, src, lands):
                if n == "w_ffn_up":
                    out[n] = _cols_from_shards(n + "_repack", land, own)
                else:
                    out[n] = _gathered_full(lax.dynamic_update_index_in_dim(land, own, my_id, 0), axis_of[n])
            return out

        def send(self, name, grads):
            names_ = tuple(grads)
            parts = []
            for n in names_:
                if n == "w_in":
                    parts.append(_w_in_grad_to_shards(*grads[n], N_DEV, w[n].shape[1]))
                elif n == "w_ffn_up":
                    parts.append(_cols_to_shards(n + "_grad_unpack", grads[n], N_DEV))
                else:
                    parts.append(_shard_major(grads[n], axis_of[n]).astype(MXU_DTYPE))
            handle, token = _push_start("send_" + name + "_start", parts, True)
            sent[name] = (names_, parts, handle)
            return (token,)

    loss_part, grad_x, d_meta, g = _local_step(x[0], loss_target[0], p, Comm())
    grad_x = grad_x[None]

    small = _pack([d_meta.reshape(-1)] + [g[n].reshape(-1) for n in REPLICATED] + [loss_part.reshape(-1)])
    small_handle, small_token = _push_start("send_small_start", [small], False)

    res = {}
    prev = small_token
    for name, (names_, parts, handle) in sent.items():
        lands = _push_wait("send_" + name + "_wait", handle, prev)
        for n, part, land in zip(names_, parts, lands):
            own = lax.dynamic_index_in_dim(part, my_id, axis=0, keepdims=False)
            res[n] = _adamw("adamw_" + n, land, w[n], m[n], v[n], own=own)
            prev = res[n][0]
    small_all = _push_wait("send_small_wait", small_handle, prev)[0]
    head = jnp.zeros((d_meta.size,), F32)
    rep_w = _pack([head] + [w[n].reshape(-1) for n in REPLICATED])
    rep_m = _pack([head] + [m[n].reshape(-1) for n in REPLICATED])
    rep_v = _pack([head] + [v[n].reshape(-1) for n in REPLICATED])
    rep_res = _adamw("adamw_replicated", small_all, rep_w, rep_m, rep_v, own=small)
    off = d_meta.size
    for n in REPLICATED:
        sz = w[n].size
        res[n] = tuple(a.reshape(-1)[off:off + sz] for a in rep_res)
        off += sz
    loss = rep_res[0].reshape(-1)[off]
    cols = w["meta_tokens"].shape[1]
    meta_rows = lambda a: a.reshape(a.shape[:-2] + (-1,))[..., :d_meta.size].reshape(a.shape[:-2] + d_meta.shape)
    my_cols = lambda a: lax.dynamic_slice_in_dim(a, my_id * cols, cols, axis=a.ndim - 1)
    res["meta_tokens"] = _adamw("adamw_meta_tokens", my_cols(meta_rows(small_all)), w["meta_tokens"],
                                m["meta_tokens"], v["meta_tokens"], own=my_cols(d_meta))

    outs = [loss, grad_x]
    for idx in range(4):
        outs += [res[n][idx].reshape(out_shape[n]) for n in names]
    return tuple(outs)
```

```python
import jax
import jax.numpy as jnp
from jax import lax
from jax.experimental import pallas as pl
from jax.experimental.pallas import tpu as pltpu

F32 = jnp.float32
BF16 = jnp.bfloat16
MXU_DTYPE = BF16

N_DEV = 8
N_META = 16
D_MODEL = 1024
HEADS = 8
Q_RANK = 384
KV_RANK = 128
NOPE = 64
ROPE = 32
HALF = ROPE // 2
QK_DIM = NOPE + ROPE
V_DIM = 64
FOX_DIM = 64
FOX_W = HEADS * FOX_DIM
D_FF = 2816
ROPE_THETA = 10000.0
LN_EPS = 1e-5
RMS_EPS = 1e-6
ALPHA = 2.0 ** 0.25
MLA_SCALE = QK_DIM ** -0.5
FOX_SCALE = FOX_DIM ** -0.5
NEG_INF = -1e30

HP = 128
HW = HEADS * HP
F_W = 3 * HW
R_QLAT = 0
R_KVLAT = Q_RANK
R_LAST = R_KVLAT + KV_RANK
R_GATE = D_MODEL
R_W = R_GATE + 2 * D_MODEL
LANE_FL = 0
LANE_PE = NOPE

ADAM_LR = 0.001
ADAM_B1 = 0.9
ADAM_B2 = 0.999
ADAM_EPS = 1e-08
ADAM_WD = 0.01
ADAM_STEP = 10

ROW_BLOCK = 256
ATT_TQ = 768
ATT_TK = 768
ATT_HEADS = 2
ROW_ALIGN = 768
MM_BLOCK_CAP = 1408
VMEM_LIMIT = 56 * 1024 * 1024
HIGHEST = lax.Precision.HIGHEST
NT = (((1,), (1,)), ((), ()))
TN = (((0,), (0,)), ((), ()))


def _params(sem=None):
    return pltpu.CompilerParams(dimension_semantics=sem, vmem_limit_bytes=VMEM_LIMIT)


def _call(name, body, grid, ins, outs, scratch=(), sem=None, after=()):
    n_in = len(ins)
    n_tok = len(after)

    def run(*refs):
        body(*refs[:n_in], *refs[n_in + n_tok:])

    tok_spec = pl.BlockSpec((8, 128), lambda *_: (0, 0))
    return pl.pallas_call(
        run, name=name, grid=grid,
        in_specs=[s for _, s in ins] + [tok_spec] * n_tok,
        out_specs=[s for _, s in outs],
        out_shape=[o for o, _ in outs],
        scratch_shapes=list(scratch),
        compiler_params=_params(sem),
    )(*[a for a, _ in ins], *after)


def _sds(shape, dtype):
    return jax.ShapeDtypeStruct(shape, dtype)


def _rows(br, c, cb=0):
    return pl.BlockSpec((br, c), lambda i: (i, cb))


def _whole(shape):
    n = len(shape)
    return pl.BlockSpec(shape, lambda i: (0,) * n)


def _pick(dim, cap, mult):
    best = None
    d = mult
    while d <= min(dim, cap):
        if dim % d == 0:
            best = d
        d += mult
    return best if best is not None else dim


def _hs(h):
    return slice(h * HP, (h + 1) * HP)


def _matmul(name, a, b, *, ta=False, tb=False, out_dtype=F32, addend=None, alpha=1.0, after=()):
    if ta:
        k, m = a.shape
    else:
        m, k = a.shape
    if tb:
        n, k2 = b.shape
    else:
        k2, n = b.shape
    assert k == k2, (name, a.shape, b.shape)
    bm = _pick(m, MM_BLOCK_CAP, 128 if ta else 16)
    bn = _pick(n, MM_BLOCK_CAP, 128)
    bk = _pick(k, MM_BLOCK_CAP, 128 if (not ta or tb) else 16)
    nk = k // bk
    dims = (((0 if ta else 1,), (1 if tb else 0,)), ((), ()))
    has_add = addend is not None

    def body(*refs):
        a_ref, b_ref = refs[:2]
        add_ref = refs[2] if has_add else None
        o_ref = refs[3 if has_add else 2]

        def finish(r):
            if has_add:
                r = r + alpha * add_ref[...]
            o_ref[...] = r.astype(o_ref.dtype)

        part = lax.dot_general(a_ref[...], b_ref[...], dims, preferred_element_type=F32)
        if nk == 1:
            finish(part)
            return
        acc_ref = refs[-1]
        kk = pl.program_id(2)

        @pl.when(kk == 0)
        def _():
            acc_ref[...] = part

        @pl.when(kk > 0)
        def _():
            acc_ref[...] += part

        @pl.when(kk == nk - 1)
        def _():
            finish(acc_ref[...])

    a_spec = pl.BlockSpec((bk, bm), lambda i, j, l: (l, i)) if ta else pl.BlockSpec((bm, bk), lambda i, j, l: (i, l))
    b_spec = pl.BlockSpec((bn, bk), lambda i, j, l: (j, l)) if tb else pl.BlockSpec((bk, bn), lambda i, j, l: (l, j))
    o_spec = pl.BlockSpec((bm, bn), lambda i, j, l: (i, j))
    ins = [(a, a_spec), (b, b_spec)]
    if has_add:
        ins.append((addend, o_spec))
    return _call(name, body, (m // bm, n // bn, nk), ins, [(_sds((m, n), out_dtype), o_spec)],
                 scratch=[pltpu.VMEM((bm, bn), F32)] if nk > 1 else [],
                 sem=("parallel", "parallel", "arbitrary"), after=after)[0]


def _ln_stats(z):
    mu = jnp.mean(z, axis=-1, keepdims=True)
    zc = z - mu
    var = jnp.mean(zc * zc, axis=-1, keepdims=True)
    rstd = lax.rsqrt(var + LN_EPS)
    return zc * rstd, rstd


def _ln_fwd(name, a, res, g, b, after=()):
    r, d = a.shape
    br = ROW_BLOCK
    has_res = res is not None

    def body(*refs):
        if has_res:
            a_ref, r_ref, g_ref, b_ref, y_ref, yb_ref = refs
            z = ALPHA * a_ref[...] + r_ref[...]
        else:
            a_ref, g_ref, b_ref, y_ref, yb_ref = refs
            z = a_ref[...]
        xhat, _ = _ln_stats(z)
        y = xhat * g_ref[...] + b_ref[...]
        y_ref[...] = y
        yb_ref[...] = y.astype(yb_ref.dtype)

    ins = [(a, _rows(br, d))]
    if has_res:
        ins.append((res, _rows(br, d)))
    ins += [(g.reshape(1, d), _whole((1, d))), (b.reshape(1, d), _whole((1, d)))]
    outs = [(_sds((r, d), F32), _rows(br, d)), (_sds((r, d), MXU_DTYPE), _rows(br, d))]
    return _call(name, body, (r // br,), ins, outs, sem=("parallel",), after=after)


def _ln_bwd(name, a, res, dy, g, after=()):
    r, d = a.shape
    br = ROW_BLOCK
    has_res = res is not None

    def body(*refs):
        if has_res:
            a_ref, r_ref, dy_ref, g_ref, dz_ref, dzb_ref, dg_ref, db_ref = refs
            z = ALPHA * a_ref[...] + r_ref[...]
        else:
            a_ref, dy_ref, g_ref, dz_ref, dzb_ref, dg_ref, db_ref = refs
            z = a_ref[...]
        xhat, rstd = _ln_stats(z)
        dyv = dy_ref[...]
        dyg = dyv * g_ref[...]
        m1 = jnp.mean(dyg, axis=-1, keepdims=True)
        m2 = jnp.mean(dyg * xhat, axis=-1, keepdims=True)
        dz = rstd * (dyg - m1 - xhat * m2)
        dz_ref[...] = dz
        dzb_ref[...] = dz.astype(dzb_ref.dtype)

        @pl.when(pl.program_id(0) == 0)
        def _():
            dg_ref[...] = jnp.zeros_like(dg_ref)
            db_ref[...] = jnp.zeros_like(db_ref)

        dg_ref[...] += jnp.sum(dyv * xhat, axis=0, keepdims=True)
        db_ref[...] += jnp.sum(dyv, axis=0, keepdims=True)

    ins = [(a, _rows(br, d))]
    if has_res:
        ins.append((res, _rows(br, d)))
    ins += [(dy, _rows(br, d)), (g.reshape(1, d), _whole((1, d)))]
    outs = [(_sds((r, d), F32), _rows(br, d)), (_sds((r, d), MXU_DTYPE), _rows(br, d)),
            (_sds((1, d), F32), _whole((1, d))), (_sds((1, d), F32), _whole((1, d)))]
    return _call(name, body, (r // br,), ins, outs, sem=("arbitrary",), after=after)


LATENTS = ((R_QLAT // Q_RANK, Q_RANK), (R_KVLAT // KV_RANK, KV_RANK))


def _latent_norm_fwd(proj_r, gains):
    r = proj_r.shape[0]
    br = ROW_BLOCK

    def body(xq_ref, xk_ref, gq_ref, gk_ref, yq_ref, yk_ref):
        for x_ref, g_ref, y_ref in ((xq_ref, gq_ref, yq_ref), (xk_ref, gk_ref, yk_ref)):
            x = x_ref[...]
            rstd = lax.rsqrt(jnp.mean(x * x, axis=-1, keepdims=True) + RMS_EPS)
            y_ref[...] = (x * rstd * g_ref[...]).astype(y_ref.dtype)

    return _call("latent_norm_fwd", body, (r // br,),
                 [(proj_r, _rows(br, wd, cb)) for cb, wd in LATENTS]
                 + [(g.reshape(1, wd), _whole((1, wd))) for g, (_, wd) in zip(gains, LATENTS)],
                 [(_sds((r, wd), MXU_DTYPE), _rows(br, wd)) for _, wd in LATENTS], sem=("parallel",))


def _latent_norm_bwd(proj_r, dys, gains):
    r = proj_r.shape[0]
    br = ROW_BLOCK

    def body(xq_ref, xk_ref, dq_ref, dk_ref, gq_ref, gk_ref, oq_ref, ok_ref, dgq_ref, dgk_ref):
        @pl.when(pl.program_id(0) == 0)
        def _():
            dgq_ref[...] = jnp.zeros_like(dgq_ref)
            dgk_ref[...] = jnp.zeros_like(dgk_ref)

        for x_ref, dy_ref, g_ref, dx_ref, dg_ref in ((xq_ref, dq_ref, gq_ref, oq_ref, dgq_ref),
                                                     (xk_ref, dk_ref, gk_ref, ok_ref, dgk_ref)):
            x = x_ref[...]
            rstd = lax.rsqrt(jnp.mean(x * x, axis=-1, keepdims=True) + RMS_EPS)
            nrm = x * rstd
            dyv = dy_ref[...]
            dyg = dyv * g_ref[...]
            dx_ref[...] = (rstd * (dyg - nrm * jnp.mean(dyg * nrm, axis=-1, keepdims=True))).astype(dx_ref.dtype)
            dg_ref[...] += jnp.sum(dyv * nrm, axis=0, keepdims=True)

    return _call("latent_norm_bwd", body, (r // br,),
                 [(proj_r, _rows(br, wd, cb)) for cb, wd in LATENTS]
                 + [(dy, _rows(br, wd)) for dy, (_, wd) in zip(dys, LATENTS)]
                 + [(g.reshape(1, wd), _whole((1, wd))) for g, (_, wd) in zip(gains, LATENTS)],
                 [(_sds((r, wd), MXU_DTYPE), _rows(br, wd)) for _, wd in LATENTS]
                 + [(_sds((1, wd), F32), _whole((1, wd))) for _, wd in LATENTS], sem=("arbitrary",))


def _lane_iota(shape):
    return lax.broadcasted_iota(jnp.int32, shape, 1)


def _rotary(t, c, s, lane, sign):
    second = pltpu.roll(t, HP - HALF, axis=1)
    first = pltpu.roll(t, HALF, axis=1)
    lo = (lane >= LANE_PE) & (lane < LANE_PE + HALF)
    hi = (lane >= LANE_PE + HALF) & (lane < LANE_PE + ROPE)
    return jnp.where(lo, t * c - sign * second * s, jnp.where(hi, t * c + sign * first * s, t))


def _rope_fwd(q_raw, k_part, proj_r, cos_t, sin_t):
    r = q_raw.shape[0]
    br = ROW_BLOCK

    def body(q_ref, k_ref, t_ref, c_ref, s_ref, qo_ref, ko_ref):
        c = c_ref[...]
        s = s_ref[...]
        lane = _lane_iota((br, HP))
        pe = (lane >= LANE_PE) & (lane < LANE_PE + ROPE)
        kp = jnp.where(pe, _rotary(t_ref[...], c, s, lane, 1.0), 0.0)
        for h in range(HEADS):
            qo_ref[:, _hs(h)] = (_rotary(q_ref[:, _hs(h)], c, s, lane, 1.0) * MLA_SCALE).astype(qo_ref.dtype)
            ko_ref[:, _hs(h)] = (k_ref[:, _hs(h)] + kp).astype(ko_ref.dtype)

    blk = _rows(br, HP)
    wide = _rows(br, HW)
    return _call("rope_fwd", body, (r // br,),
                 [(q_raw, wide), (k_part, wide), (proj_r, _rows(br, HP, R_LAST // HP)), (cos_t, blk), (sin_t, blk)],
                 [(_sds((r, HW), MXU_DTYPE), wide)] * 2, sem=("parallel",))


def _rope_bwd(dq, dk, dv, dfl, cos_t, sin_t):
    r = dq.shape[0]
    br = ROW_BLOCK

    def body(dq_ref, dk_ref, dv_ref, fl_ref, c_ref, s_ref, dqo_ref, dkv_ref, dl_ref):
        c = c_ref[...]
        s = s_ref[...]
        lane = _lane_iota((br, HP))
        pe = (lane >= LANE_PE) & (lane < LANE_PE + ROPE)
        acc = jnp.zeros((br, HP), F32)
        for h in range(HEADS):
            dqo_ref[:, _hs(h)] = (_rotary(dq_ref[:, _hs(h)], c, s, lane, -1.0) * MLA_SCALE).astype(dqo_ref.dtype)
            dkh = dk_ref[:, _hs(h)]
            acc = acc + dkh
            dkv_ref[:, _hs(h)] = dkh.astype(dkv_ref.dtype)
            dkv_ref[:, _hs(HEADS + h)] = dv_ref[:, _hs(h)].astype(dkv_ref.dtype)
        dl_ref[...] = (jnp.where(pe, _rotary(acc, c, s, lane, -1.0), 0.0) + fl_ref[...]).astype(dl_ref.dtype)

    blk = _rows(br, HP)
    wide = _rows(br, HW)
    return _call("rope_bwd", body, (r // br,),
                 [(dq, wide), (dk, wide), (dv, wide), (dfl, blk), (cos_t, blk), (sin_t, blk)],
                 [(_sds((r, HW), MXU_DTYPE), wide), (_sds((r, 2 * HW), MXU_DTYPE), _rows(br, 2 * HW)),
                  (_sds((r, HP), MXU_DTYPE), blk)],
                 sem=("parallel",))


def _log_sigmoid(x):
    return jnp.minimum(x, 0.0) - jnp.log(1.0 + jnp.exp(-jnp.abs(x)))


def _head_lane(x, h, lane):
    return jnp.sum(jnp.where(lane == h, x, 0.0), axis=1, keepdims=True)


def _forget_fwd(proj_r, bf_row):
    r = proj_r.shape[0]
    br = ROW_BLOCK

    def body(t_ref, b_ref, ob_ref, ot_ref, carry_ref):
        @pl.when(pl.program_id(0) == 0)
        def _():
            carry_ref[...] = jnp.zeros_like(carry_ref)

        x = t_ref[...] + b_ref[...]
        lane = _lane_iota(x.shape)
        lf = jnp.where((lane >= LANE_FL) & (lane < LANE_FL + HEADS), _log_sigmoid(x), 0.0)
        tri = (lax.broadcasted_iota(jnp.int32, (br, br), 0) >= lax.broadcasted_iota(jnp.int32, (br, br), 1)).astype(F32)
        cum = jnp.dot(tri, lf, precision=HIGHEST, preferred_element_type=F32) + carry_ref[0:1, :]
        for h in range(HEADS):
            ob_ref[:, _hs(h)] = jnp.broadcast_to(_head_lane(cum, LANE_FL + h, lane), (br, HP))
        ot_ref[...] = cum.T[LANE_FL:LANE_FL + HEADS, :]
        carry_ref[...] = jnp.broadcast_to(cum[br - 1:br, :], carry_ref.shape)

    return _call("forget_fwd", body, (r // br,),
                 [(proj_r, _rows(br, HP, R_LAST // HP)), (bf_row, _whole((1, HP)))],
                 [(_sds((r, HW), F32), _rows(br, HW)), (_sds((HEADS, r), F32), pl.BlockSpec((HEADS, br), lambda i: (0, i)))],
                 scratch=[pltpu.VMEM((8, HP), F32)], sem=("arbitrary",))


def _forget_bwd(proj_r, bf_row, dcq_t, dck_b):
    r = proj_r.shape[0]
    br = ROW_BLOCK
    nb = r // br

    def body(t_ref, b_ref, dcq_ref, dck_ref, o_ref, db_ref, carry_ref):
        @pl.when(pl.program_id(0) == 0)
        def _():
            carry_ref[...] = jnp.zeros_like(carry_ref)
            db_ref[...] = jnp.zeros_like(db_ref)

        lane = _lane_iota((br, HP))
        dc = jnp.concatenate([dcq_ref[...], jnp.zeros((HP - HEADS, br), F32)], axis=0).T
        for h in range(HEADS):
            dc = dc + jnp.where(lane == LANE_FL + h, dck_ref[:, h * HP:h * HP + 1], 0.0)
        triu = (lax.broadcasted_iota(jnp.int32, (br, br), 0) <= lax.broadcasted_iota(jnp.int32, (br, br), 1)).astype(F32)
        dlf = jnp.dot(triu, dc, precision=HIGHEST, preferred_element_type=F32) + carry_ref[0:1, :]
        carry_ref[...] = jnp.broadcast_to(dlf[0:1, :], carry_ref.shape)
        x = t_ref[...] + b_ref[...]
        dfl = jnp.where((lane >= LANE_FL) & (lane < LANE_FL + HEADS), dlf * jax.nn.sigmoid(-x), 0.0)
        o_ref[...] = dfl
        db_ref[...] += jnp.sum(dfl, axis=0, keepdims=True)

    rev = pl.BlockSpec((br, HP), lambda i: (nb - 1 - i, 0))
    return _call("forget_bwd", body, (nb,),
                 [(proj_r, pl.BlockSpec((br, HP), lambda i: (nb - 1 - i, R_LAST // HP))), (bf_row, _whole((1, HP))),
                  (dcq_t, pl.BlockSpec((HEADS, br), lambda i: (0, nb - 1 - i))),
                  (dck_b, pl.BlockSpec((br, HW), lambda i: (nb - 1 - i, 0)))],
                 [(_sds((r, HP), F32), rev), (_sds((1, HP), F32), _whole((1, HP)))],
                 scratch=[pltpu.VMEM((8, HP), F32)], sem=("arbitrary",))


def _gate_fwd(proj_r, b_gate, bm, bfx):
    r, d = bm.shape
    br = ROW_BLOCK
    cb = R_GATE // d

    def body(gm_ref, gf_ref, b1_ref, b2_ref, bm_ref, bf_ref, o_ref):
        g1 = jax.nn.sigmoid(gm_ref[...] + b1_ref[...])
        g2 = jax.nn.sigmoid(gf_ref[...] + b2_ref[...])
        o_ref[...] = (g1 * bm_ref[...].astype(F32) + g2 * bf_ref[...].astype(F32)).astype(o_ref.dtype)

    b1 = b_gate[:d].reshape(1, d)
    b2 = b_gate[d:].reshape(1, d)
    return _call("gate_fwd", body, (r // br,),
                 [(proj_r, _rows(br, d, cb)), (proj_r, _rows(br, d, cb + 1)), (b1, _whole((1, d))), (b2, _whole((1, d))),
                  (bm, _rows(br, d)), (bfx, _rows(br, d))],
                 [(_sds((r, d), MXU_DTYPE), _rows(br, d))], sem=("parallel",))[0]


def _gate_bwd(proj_r, b_gate, bm, bfx, dmerged):
    r, d = bm.shape
    br = ROW_BLOCK
    cb = R_GATE // d

    def body(gm_ref, gf_ref, b1_ref, b2_ref, bm_ref, bf_ref, dm_ref, dbm_ref, dbf_ref, dgl_ref, dbg_ref):
        g1 = jax.nn.sigmoid(gm_ref[...] + b1_ref[...])
        g2 = jax.nn.sigmoid(gf_ref[...] + b2_ref[...])
        dm = dm_ref[...].astype(F32)
        dbm_ref[...] = (dm * g1).astype(dbm_ref.dtype)
        dbf_ref[...] = (dm * g2).astype(dbf_ref.dtype)
        dl1 = dm * bm_ref[...].astype(F32) * (g1 * (1.0 - g1))
        dl2 = dm * bf_ref[...].astype(F32) * (g2 * (1.0 - g2))
        dgl_ref[:, 0:d] = dl1.astype(dgl_ref.dtype)
        dgl_ref[:, d:2 * d] = dl2.astype(dgl_ref.dtype)

        @pl.when(pl.program_id(0) == 0)
        def _():
            dbg_ref[...] = jnp.zeros_like(dbg_ref)

        dbg_ref[:, 0:d] += jnp.sum(dl1, axis=0, keepdims=True)
        dbg_ref[:, d:2 * d] += jnp.sum(dl2, axis=0, keepdims=True)

    b1 = b_gate[:d].reshape(1, d)
    b2 = b_gate[d:].reshape(1, d)
    return _call("gate_bwd", body, (r // br,),
                 [(proj_r, _rows(br, d, cb)), (proj_r, _rows(br, d, cb + 1)), (b1, _whole((1, d))), (b2, _whole((1, d))),
                  (bm, _rows(br, d)), (bfx, _rows(br, d)), (dmerged, _rows(br, d))],
                 [(_sds((r, d), MXU_DTYPE), _rows(br, d)), (_sds((r, d), MXU_DTYPE), _rows(br, d)),
                  (_sds((r, 2 * d), MXU_DTYPE), _rows(br, 2 * d)), (_sds((1, 2 * d), F32), _whole((1, 2 * d)))],
                 sem=("arbitrary",))


HALO = 16
GLU_BWD_BLOCK = 128


def _conv_taps(gp, halo, first_block):
    halo = jnp.where(first_block, 0.0, halo.astype(F32))
    rid = lax.broadcasted_iota(jnp.int32, gp.shape, 0)
    last, prev = halo[HALO - 1:HALO, :], halo[HALO - 2:HALO - 1, :]
    g1 = jnp.where(rid == 0, last, pltpu.roll(gp, 1, axis=0))
    g2 = jnp.where(rid == 0, prev, jnp.where(rid == 1, last, pltpu.roll(gp, 2, axis=0)))
    return g1, g2


def _prev_halo(br, c):
    return pl.BlockSpec((HALO, c), lambda i: (jnp.maximum(i * (br // HALO) - 1, 0), 0))


def _glu_fwd(up, conv_w, conv_b):
    r = up.shape[0]
    c = D_FF
    br = ROW_BLOCK

    def body(gp_ref, halo_ref, val_ref, w_ref, b_ref, o_ref):
        gp = gp_ref[...].astype(F32)
        g1, g2 = _conv_taps(gp, halo_ref[...], pl.program_id(0) == 0)
        gate = w_ref[0:1, :] * g2 + w_ref[1:2, :] * g1 + w_ref[2:3, :] * gp + b_ref[...]
        o_ref[...] = (gate * jax.nn.sigmoid(gate) * val_ref[...].astype(F32)).astype(o_ref.dtype)

    return _call("glu_fwd", body, (r // br,),
                 [(up, _rows(br, c, 0)), (up, _prev_halo(br, c)), (up, _rows(br, c, 1)),
                  (conv_w, _whole((3, c))), (conv_b.reshape(1, c), _whole((1, c)))],
                 [(_sds((r, c), MXU_DTYPE), _rows(br, c))], sem=("parallel",))[0]


def _glu_bwd(up, conv_w, conv_b, d_act):
    r = up.shape[0]
    c = D_FF
    br = GLU_BWD_BLOCK
    nb = r // br

    def body(gp_ref, halo_ref, val_ref, da_ref, gpn_ref, valn_ref, dan_ref, w_ref, b_ref, o_ref, dw_ref, db_ref):
        i = pl.program_id(0)
        w0, w1, w2, bias = w_ref[0:1, :], w_ref[1:2, :], w_ref[2:3, :], b_ref[...]

        def d_gate(gp, g1, g2, val, da):
            gate = w0 * g2 + w1 * g1 + w2 * gp + bias
            sg = jax.nn.sigmoid(gate)
            return da * val * (sg * (1.0 + gate * (1.0 - sg))), da * (gate * sg)

        gp = gp_ref[...].astype(F32)
        g1, g2 = _conv_taps(gp, halo_ref[...], i == 0)
        dg, dv = d_gate(gp, g1, g2, val_ref[...].astype(F32), da_ref[...].astype(F32))
        gpn = gpn_ref[...].astype(F32)
        g1n, g2n = _conv_taps(gpn, gp[br - HALO:, :], False)
        dgn, _ = d_gate(gpn, g1n, g2n, valn_ref[...].astype(F32), dan_ref[...].astype(F32))
        dgn = jnp.where(i == nb - 1, 0.0, dgn)
        rid = lax.broadcasted_iota(jnp.int32, dg.shape, 0)
        u1 = jnp.where(rid == br - 1, dgn[0:1, :], pltpu.roll(dg, br - 1, axis=0))
        u2 = jnp.where(rid == br - 1, dgn[1:2, :], jnp.where(rid == br - 2, dgn[0:1, :], pltpu.roll(dg, br - 2, axis=0)))
        o_ref[:, 0:c] = (w2 * dg + w1 * u1 + w0 * u2).astype(o_ref.dtype)
        o_ref[:, c:2 * c] = dv.astype(o_ref.dtype)

        @pl.when(i == 0)
        def _():
            dw_ref[...] = jnp.zeros_like(dw_ref)
            db_ref[...] = jnp.zeros_like(db_ref)

        dw_ref[0:1, :] += jnp.sum(dg * g2, axis=0, keepdims=True)
        dw_ref[1:2, :] += jnp.sum(dg * g1, axis=0, keepdims=True)
        dw_ref[2:3, :] += jnp.sum(dg * gp, axis=0, keepdims=True)
        db_ref[...] += jnp.sum(dg, axis=0, keepdims=True)

    nxt = lambda cb: pl.BlockSpec((HALO, c), lambda i: (jnp.minimum((i + 1) * (br // HALO), r // HALO - 1), cb))
    return _call("glu_bwd", body, (nb,),
                 [(up, _rows(br, c, 0)), (up, _prev_halo(br, c)), (up, _rows(br, c, 1)), (d_act, _rows(br, c)),
                  (up, nxt(0)), (up, nxt(1)), (d_act, nxt(0)),
                  (conv_w, _whole((3, c))), (conv_b.reshape(1, c), _whole((1, c)))],
                 [(_sds((r, 2 * c), MXU_DTYPE), _rows(br, 2 * c)),
                  (_sds((8, c), F32), _whole((8, c))), (_sds((1, c), F32), _whole((1, c)))],
                 sem=("arbitrary",))


def _token_specs(seq, d):
    br = ROW_BLOCK
    nxb = seq // br
    main = pl.BlockSpec((br, d), lambda i: (jnp.minimum(i, nxb - 1), 0))
    tail = pl.BlockSpec((N_META, d), lambda i: (jnp.clip(i * (br // N_META) - 1, 0, seq // N_META - 1), 0))
    return main, tail


def _padded_block(main_ref, tail_ref, first, seq):
    br = ROW_BLOCK
    i = pl.program_id(0)
    nxb = seq // br
    main = jnp.where(i < nxb, main_ref[...], 0.0)
    head = jnp.where(i == 0, first, jnp.where(i <= nxb, tail_ref[...], 0.0))
    return jnp.concatenate([head, main[:br - N_META]], axis=0)


def _ln_emb_fwd(x, meta, g, b, rows, after=()):
    seq, d = x.shape
    br = ROW_BLOCK
    assert seq % br == 0 and br % N_META == 0 and rows % br == 0

    def body(x_ref, tail_ref, meta_ref, g_ref, b_ref, y_ref, yb_ref):
        z = _padded_block(x_ref, tail_ref, meta_ref[...], seq)
        xhat, _ = _ln_stats(z)
        y = xhat * g_ref[...] + b_ref[...]
        y_ref[...] = y
        yb_ref[...] = y.astype(yb_ref.dtype)

    main, tail = _token_specs(seq, d)
    return _call("ln_emb_fwd", body, (rows // br,),
                 [(x, main), (x, tail), (meta, _whole((N_META, d))), (g.reshape(1, d), _whole((1, d))),
                  (b.reshape(1, d), _whole((1, d)))],
                 [(_sds((rows, d), F32), _rows(br, d)), (_sds((rows, d), MXU_DTYPE), _rows(br, d))],
                 sem=("parallel",), after=after)


def _ln_emb_bwd(x, meta, dh0, g):
    seq, d = x.shape
    br = ROW_BLOCK
    step = br // N_META

    def ln_bwd(z, dy, gv):
        xhat, rstd = _ln_stats(z)
        dyg = dy * gv
        m1 = jnp.mean(dyg, axis=-1, keepdims=True)
        m2 = jnp.mean(dyg * xhat, axis=-1, keepdims=True)
        dz = rstd * (dyg - m1 - xhat * m2)
        return dz, jnp.sum(dy * xhat, axis=0, keepdims=True), jnp.sum(dy, axis=0, keepdims=True)

    def body(x_ref, dh_ref, nxt_ref, meta_ref, top_ref, g_ref, dx_ref, dm_ref, dg_ref, db_ref):
        gv = g_ref[...]
        dy = jnp.concatenate([dh_ref[N_META:, :], nxt_ref[...]], axis=0)
        dz, dg, db = ln_bwd(x_ref[...], dy, gv)
        dx_ref[...] = dz

        @pl.when(pl.program_id(0) == 0)
        def _():
            dzm, dgm, dbm = ln_bwd(meta_ref[...], top_ref[...], gv)
            dm_ref[...] = dzm
            dg_ref[...] = dgm
            db_ref[...] = dbm

        dg_ref[...] += dg
        db_ref[...] += db

    small = _whole((N_META, d))
    return _call("ln_emb_bwd", body, (seq // br,),
                 [(x, _rows(br, d)), (dh0, _rows(br, d)), (dh0, pl.BlockSpec((N_META, d), lambda i: ((i + 1) * step, 0))),
                  (meta, small), (dh0, small), (g.reshape(1, d), _whole((1, d)))],
                 [(_sds((seq, d), F32), _rows(br, d)), (_sds((N_META, d), F32), small),
                  (_sds((1, d), F32), _whole((1, d))), (_sds((1, d), F32), _whole((1, d)))], sem=("arbitrary",))


def _ln_ffn_loss(h1, f, tgt, g, b):
    r, d = h1.shape
    seq = tgt.shape[0]
    br = ROW_BLOCK

    def body(a_ref, r_ref, t_ref, tail_ref, g_ref, b_ref, l_ref):
        err = _loss_err(a_ref, r_ref, t_ref, tail_ref, g_ref, b_ref, seq)[0]

        @pl.when(pl.program_id(0) == 0)
        def _():
            l_ref[...] = jnp.zeros_like(l_ref)

        l_ref[...] += jnp.sum(jnp.sum(err * err, axis=1, keepdims=True), axis=0, keepdims=True) * (0.5 / d)

    main, tail = _token_specs(seq, d)
    return _call("ln_ffn_loss", body, (r // br,),
                 [(h1, _rows(br, d)), (f, _rows(br, d)), (tgt, main), (tgt, tail),
                  (g.reshape(1, d), _whole((1, d))), (b.reshape(1, d), _whole((1, d)))],
                 [(_sds((1, 1), F32), _whole((1, 1)))], sem=("arbitrary",))[0]


def _loss_err(a_ref, r_ref, t_ref, tail_ref, g_ref, b_ref, seq):
    br, d = a_ref.shape
    xhat, rstd = _ln_stats(ALPHA * a_ref[...] + r_ref[...])
    y = xhat * g_ref[...] + b_ref[...]
    t = _padded_block(t_ref, tail_ref, jnp.zeros((N_META, d), F32), seq)
    rid = lax.broadcasted_iota(jnp.int32, (br, d), 0) + pl.program_id(0) * br
    valid = (rid >= N_META) & (rid < N_META + seq)
    return jnp.where(valid, y - t, 0.0), xhat, rstd


def _ln_ffn_bwd(h1, f, tgt, g, b):
    r, d = h1.shape
    seq = tgt.shape[0]
    br = ROW_BLOCK

    def body(a_ref, r_ref, t_ref, tail_ref, g_ref, b_ref, dz_ref, dzb_ref, dg_ref, db_ref):
        err, xhat, rstd = _loss_err(a_ref, r_ref, t_ref, tail_ref, g_ref, b_ref, seq)
        dyv = err * (1.0 / d)
        dyg = dyv * g_ref[...]
        m1 = jnp.mean(dyg, axis=-1, keepdims=True)
        m2 = jnp.mean(dyg * xhat, axis=-1, keepdims=True)
        dz = rstd * (dyg - m1 - xhat * m2)
        dz_ref[...] = dz
        dzb_ref[...] = dz.astype(dzb_ref.dtype)

        @pl.when(pl.program_id(0) == 0)
        def _():
            dg_ref[...] = jnp.zeros_like(dg_ref)
            db_ref[...] = jnp.zeros_like(db_ref)

        dg_ref[...] += jnp.sum(dyv * xhat, axis=0, keepdims=True)
        db_ref[...] += jnp.sum(dyv, axis=0, keepdims=True)

    main, tail = _token_specs(seq, d)
    return _call("ln_ffn_bwd", body, (r // br,),
                 [(h1, _rows(br, d)), (f, _rows(br, d)), (tgt, main), (tgt, tail),
                  (g.reshape(1, d), _whole((1, d))), (b.reshape(1, d), _whole((1, d)))],
                 [(_sds((r, d), F32), _rows(br, d)), (_sds((r, d), MXU_DTYPE), _rows(br, d)),
                  (_sds((1, d), F32), _whole((1, d))), (_sds((1, d), F32), _whole((1, d)))], sem=("arbitrary",))


def _attn_fwd(name, q, k, v, cum_b=None, cum_t=None):
    (qa, qg), (ka, kg), (va, vg) = q, k, v
    r = qa.shape[0]
    tq, tk = ATT_TQ, ATT_TK
    nq, nk = r // tq, r // tk
    bias = cum_b is not None

    def body(*refs):
        if bias:
            q_ref, k_ref, vt_ref, cb_ref, ct_ref, o_ref, ob_ref, lse_ref = refs
        else:
            q_ref, k_ref, vt_ref, o_ref, ob_ref, lse_ref = refs
        i = pl.program_id(1)
        qs = [q_ref[:, _hs(hh)] for hh in range(hg)]
        cqs = [ct_ref[hh] for hh in range(hg)] if bias else None
        diff = lax.broadcasted_iota(jnp.int32, (tk, tq), 0) - lax.broadcasted_iota(jnp.int32, (tk, tq), 1)

        def step(j, carry, masked):
            keys = pl.ds(pl.multiple_of(j * tk, tk), tk)
            out = []
            for hh in range(hg):
                m, l, acc = carry[hh]
                kt = k_ref[keys, _hs(hh)]
                s = lax.dot_general(kt, qs[hh], NT, preferred_element_type=F32)
                if bias:
                    s = s + (cqs[hh] - cb_ref[keys, hh * HP:hh * HP + 1])
                if masked:
                    s = jnp.where(diff <= i * tq - j * tk, s, NEG_INF)
                m_new = jnp.maximum(m, jnp.max(s, axis=0, keepdims=True))
                p = jnp.exp(s - m_new)
                a = jnp.exp(m - m_new)
                l = a * l + jnp.sum(p, axis=0, keepdims=True)
                acc = a * acc + jnp.dot(vt_ref[j, _hs(hh), :], p.astype(kt.dtype), preferred_element_type=F32)
                out.append((m_new, l, acc))
            return tuple(out)

        n_clear = (i * tq + 1) // tk
        n_all = ((i + 1) * tq - 1) // tk + 1
        carry = tuple((jnp.full((1, tq), NEG_INF, F32), jnp.zeros((1, tq), F32), jnp.zeros((HP, tq), F32))
                      for _ in range(hg))
        carry = lax.fori_loop(0, n_clear, lambda j, c: step(j, c, False), carry)
        carry = lax.fori_loop(n_clear, n_all, lambda j, c: step(j, c, True), carry)
        for hh in range(hg):
            m, l, acc = carry[hh]
            o = (acc / l).T
            o_ref[:, _hs(hh)] = o
            ob_ref[:, _hs(hh)] = o.astype(ob_ref.dtype)
            lse_ref[hh] = m + jnp.log(l)

    hg = ATT_HEADS
    w = hg * HP
    gpw = HW // w
    tile = lambda g: pl.BlockSpec((tq, w), lambda h, i: (i, g * gpw + h))
    res = lambda g: pl.BlockSpec((r, w), lambda h, i: (0, g * gpw + h))
    v_t = _key_tiles_transposed(name + "_vt", va, vg)
    ins = [(qa, tile(qg)), (ka, res(kg)), (v_t, pl.BlockSpec((nk, w, tk), lambda h, i: (0, h, 0)))]
    if bias:
        ins += [(cum_b, res(0)),
                (cum_t.reshape(HEADS, nq, 1, tq), pl.BlockSpec((hg, None, 1, tq), lambda h, i: (h, i, 0, 0)))]
    outs = [(_sds((r, HW), F32), tile(0)), (_sds((r, HW), MXU_DTYPE), tile(0)),
            (_sds((HEADS, nq, 1, tq), F32), pl.BlockSpec((hg, None, 1, tq), lambda h, i: (h, i, 0, 0)))]
    o, ob, lse = _call(name, body, (gpw, nq), ins, outs, sem=("parallel", "parallel"))
    return o, ob, lse.reshape(HEADS, r)


def _key_tiles_transposed(name, a, group):
    r = a.shape[0]
    tk = ATT_TK

    def body(x_ref, o_ref):
        for h in range(HEADS):
            o_ref[_hs(h), :] = x_ref[:, _hs(h)].astype(F32).T.astype(o_ref.dtype)

    return _call(name, body, (r // tk,),
                 [(a, pl.BlockSpec((tk, HW), lambda j: (j, group)))],
                 [(_sds((r // tk, HW, tk), a.dtype), pl.BlockSpec((None, HW, tk), lambda j: (j, 0, 0)))],
                 sem=("parallel",))[0]


def _attn_delta(name, do_b, o, after=()):
    r = do_b.shape[0]
    br = ROW_BLOCK

    def body(do_ref, o_ref, d_ref):
        lane = _lane_iota((br, HP))
        d = jnp.zeros((br, HP), F32)
        for h in range(HEADS):
            dh = do_ref[:, _hs(h)].astype(F32)
            d = jnp.where(lane == h, jnp.sum(dh * o_ref[:, _hs(h)], axis=1, keepdims=True), d)
        d_ref[...] = d.T[0:HEADS, :]

    wide = _rows(br, HW)
    return _call(name, body, (r // br,), [(do_b, wide), (o, wide)],
                 [(_sds((HEADS, r), F32), pl.BlockSpec((HEADS, br), lambda i: (0, i)))],
                 sem=("parallel",), after=after)[0]


def _attn_bwd(name, q, k, v, do_b, lse_t, delta_t, cum_b=None, cum_t=None, out_dtype=F32):
    (qa, qg), (ka, kg), (va, vg) = q, k, v
    r = qa.shape[0]
    tq, tk = ATT_TQ, ATT_TK
    nq, nk = r // tq, r // tk
    bias = cum_b is not None

    def body(*refs):
        if bias:
            (q_ref, k_ref, v_ref, do_ref, lse_ref, dl_ref, cb_ref, ct_ref,
             dq_ref, dk_ref, dv_ref, dcq_ref, dck_ref, dqt_ref) = refs
        else:
            q_ref, k_ref, v_ref, do_ref, lse_ref, dl_ref, dq_ref, dk_ref, dv_ref, dqt_ref = refs
        j = pl.program_id(1)

        @pl.when(j == 0)
        def _():
            dqt_ref[...] = jnp.zeros_like(dqt_ref)
            if bias:
                dcq_ref[...] = jnp.zeros_like(dcq_ref)

        kts = [k_ref[:, _hs(hh)] for hh in range(hg)]
        vts = [v_ref[:, _hs(hh)] for hh in range(hg)]
        k_trs = [kt.astype(F32).T.astype(kt.dtype) for kt in kts]
        cks = [cb_ref[:, hh * HP:hh * HP + 1] for hh in range(hg)] if bias else None
        diff = lax.broadcasted_iota(jnp.int32, (tk, tq), 0) - lax.broadcasted_iota(jnp.int32, (tk, tq), 1)

        def step(i, carry, masked):
            rows = pl.ds(pl.multiple_of(i * tq, tq), tq)
            out = []
            for hh in range(hg):
                dk_acc, dv_acc, dck_acc = carry[hh]
                qt = q_ref[rows, _hs(hh)]
                dot = do_ref[rows, _hs(hh)]
                s = lax.dot_general(kts[hh], qt, NT, preferred_element_type=F32)
                if bias:
                    s = s + (ct_ref[hh, i] - cks[hh])
                if masked:
                    s = jnp.where(diff <= i * tq - j * tk, s, NEG_INF)
                p = jnp.exp(s - lse_ref[hh, i])
                dp = lax.dot_general(vts[hh], dot, NT, preferred_element_type=F32)
                ds = p * (dp - dl_ref[hh, i])
                pb = p.astype(dot.dtype)
                dsb = ds.astype(qt.dtype)
                dv_acc = dv_acc + jnp.dot(pb, dot, preferred_element_type=F32)
                dk_acc = dk_acc + jnp.dot(dsb, qt, preferred_element_type=F32)
                dqt_ref[hh, i] += jnp.dot(k_trs[hh], dsb, preferred_element_type=F32)
                if bias:
                    dcq_ref[hh, i] += jnp.sum(ds, axis=0, keepdims=True)
                    dck_acc = dck_acc - jnp.sum(ds, axis=1, keepdims=True)
                out.append((dk_acc, dv_acc, dck_acc))
            return tuple(out)

        i_first = (j * tk) // tq
        i_clear = jnp.minimum(((j + 1) * tk + tq - 2) // tq, nq)
        carry = tuple((jnp.zeros((tk, HP), F32), jnp.zeros((tk, HP), F32), jnp.zeros((tk, 1), F32)) for _ in range(hg))
        carry = lax.fori_loop(i_first, i_clear, lambda i, c: step(i, c, True), carry)
        carry = lax.fori_loop(i_clear, nq, lambda i, c: step(i, c, False), carry)
        for hh in range(hg):
            dk_acc, dv_acc, dck_acc = carry[hh]
            dk_ref[:, _hs(hh)] = dk_acc.astype(dk_ref.dtype)
            dv_ref[:, _hs(hh)] = dv_acc.astype(dv_ref.dtype)
            if bias:
                dck_ref[:, _hs(hh)] = jnp.broadcast_to(dck_acc, (tk, HP))

        @pl.when(j == nk - 1)
        def _():
            for hh in range(hg):
                for i in range(nq):
                    dq_ref[i * tq:(i + 1) * tq, _hs(hh)] = dqt_ref[hh, i].T.astype(dq_ref.dtype)

    hg = ATT_HEADS
    w = hg * HP
    gpw = HW // w
    res = lambda g: pl.BlockSpec((r, w), lambda h, j: (0, g * gpw + h))
    tile = lambda g: pl.BlockSpec((tk, w), lambda h, j: (j, g * gpw + h))
    rowv = pl.BlockSpec((hg, nq, 1, tq), lambda h, j: (h, 0, 0, 0))
    as_rows = lambda a: a.reshape(HEADS, nq, 1, tq)
    ins = [(qa, res(qg)), (ka, tile(kg)), (va, tile(vg)), (do_b, res(0)), (as_rows(lse_t), rowv), (as_rows(delta_t), rowv)]
    outs = [(_sds((r, HW), out_dtype), res(0)), (_sds((r, HW), out_dtype), tile(0)), (_sds((r, HW), out_dtype), tile(0))]
    if bias:
        ins += [(cum_b, tile(0)), (as_rows(cum_t), rowv)]
        outs += [(_sds((HEADS, nq, 1, tq), F32), rowv), (_sds((r, HW), F32), tile(0))]
    res_out = _call(name, body, (gpw, nk), ins, outs, scratch=[pltpu.VMEM((hg, nq, HP, tq), F32)],
                    sem=("parallel", "arbitrary"))
    if bias:
        dq, dk, dv, dcq, dck = res_out
        return dq, dk, dv, dcq.reshape(HEADS, r), dck
    return res_out


MESH_ID = pl.DeviceIdType.MESH
ANY = pl.BlockSpec(memory_space=pl.ANY)


N_GATHER_COPIES = 8


def _allgather(name, shards):
    n = len(shards)

    def body(*refs):
        x_refs, out_refs = refs[:n], refs[n:2 * n]
        send_sems, recv_sems, local_sems = refs[2 * n:]
        x, y, c = lax.axis_index("x"), lax.axis_index("y"), lax.axis_index("c")
        me, sibling = (x, y, c), (x, y, 1 - c)
        xn, yn, dg = (1 - x, y, c), (x, 1 - y, c), (1 - x, 1 - y, c)
        other = lambda dev: (dev[0], dev[1], 1 - c)

        def slot(ti, dev, half=None):
            ref = out_refs[ti].at[4 * dev[0] + 2 * dev[1] + dev[2]]
            if half is None:
                return ref
            rows = shards[ti].shape[0] // 2
            return ref.at[pl.ds(half * rows, rows)]

        def copy(ti, k, block, to, half=None, src=None):
            return pltpu.make_async_remote_copy(
                src_ref=slot(ti, block, half) if src is None else src, dst_ref=slot(ti, block, half),
                send_sem=send_sems.at[ti, k], recv_sem=recv_sems.at[ti, k], device_id=to, device_id_type=MESH_ID)

        mine = [pltpu.make_async_copy(x_refs[ti], slot(ti, me), local_sems.at[ti]) for ti in range(n)]
        for cp in mine:
            cp.start()
        started = []

        def go(cp):
            cp.start()
            started.append(cp)

        for ti in range(n):
            go(copy(ti, 0, me, sibling, src=x_refs[ti]))
            go(copy(ti, 1, me, xn, src=x_refs[ti]))
            go(copy(ti, 2, me, yn, src=x_refs[ti]))
        for ti in range(n):
            copy(ti, 1, xn, me).wait_recv()
            go(copy(ti, 3, xn, yn, half=0))
            go(copy(ti, 5, xn, sibling))
            copy(ti, 2, yn, me).wait_recv()
            go(copy(ti, 4, yn, xn, half=1))
            go(copy(ti, 6, yn, sibling))
        for ti in range(n):
            copy(ti, 3, dg, me, half=0).wait_recv()
            copy(ti, 4, dg, me, half=1).wait_recv()
            go(copy(ti, 7, dg, sibling))
        for ti in range(n):
            copy(ti, 0, sibling, me).wait_recv()
            for k, dev in ((5, xn), (6, yn), (7, dg)):
                copy(ti, k, other(dev), me).wait_recv()
        for cp in started:
            cp.wait_send()
        for cp in mine:
            cp.wait()

    sems = pltpu.SemaphoreType.DMA((n, N_GATHER_COPIES))
    return pl.pallas_call(
        body, name=name, out_shape=[_sds((N_DEV,) + s.shape, s.dtype) for s in shards],
        in_specs=[ANY] * n, out_specs=[ANY] * n,
        scratch_shapes=[sems, sems, pltpu.SemaphoreType.DMA((n,))],
    )(*shards)


HBM = pl.BlockSpec(memory_space=pltpu.HBM)
SEM = pl.BlockSpec(memory_space=pltpu.SEMAPHORE)
EFFECT = pltpu.SideEffectType.DATAFLOW_SIDE_EFFECTING
N_PEER = N_DEV - 1


def _my_id():
    return 4 * lax.axis_index("x") + 2 * lax.axis_index("y") + lax.axis_index("c")


def _peers():
    x, y, c = lax.axis_index("x"), lax.axis_index("y"), lax.axis_index("c")
    out = []
    for k in range(1, N_DEV):
        px, py, pc = (1 - x if k & 4 else x, 1 - y if k & 2 else y, 1 - c if k & 1 else c)
        out.append(((px, py, pc), 4 * px + 2 * py + pc))
    return out


def _push_copies(src_refs, land_refs, send_sems, recv_sems, scatter, landing):
    me = _my_id()
    out = []
    for ti, (src, land) in enumerate(zip(src_refs, land_refs)):
        for k, (dev, pid) in enumerate(_peers()):
            out.append(pltpu.make_async_remote_copy(
                src_ref=src.at[pid] if scatter else src, dst_ref=land.at[pid if landing else me],
                send_sem=send_sems.at[ti * N_PEER + k], recv_sem=recv_sems.at[ti * N_PEER + k],
                device_id=dev, device_id_type=MESH_ID))
    return out


def _push_start(name, groups, scatter, after=None):
    sizes = [len(g) for g in groups]
    srcs = [a for g in groups for a in g]
    n = len(srcs)
    slot = lambda s: s.shape[1:] if scatter else s.shape
    lands = [lax.empty((N_DEV,) + slot(s), s.dtype) for s in srcs]
    n_after = 0 if after is None else 1
    n_grp = len(groups)

    def body(*refs):
        src_refs, land_refs = refs[:n], refs[n:2 * n]
        sems = refs[2 * n + n_after:2 * n + n_after + 2 * n_grp]
        token = refs[-1]
        lo = 0
        for gi, sz in enumerate(sizes):
            for cp in _push_copies(src_refs[lo:lo + sz], land_refs[lo:lo + sz], sems[2 * gi], sems[2 * gi + 1], scatter, False):
                cp.start()
            lo += sz
        token[...] = jnp.zeros_like(token)

    hbm = lambda a: pltpu.with_memory_space_constraint(a, pltpu.HBM)
    operands = [hbm(a) for a in srcs + lands] + ([after] if n_after else [])
    sem_shapes = [pltpu.SemaphoreType.DMA((sz * N_PEER,)) for sz in sizes for _ in range(2)]
    res = pl.pallas_call(
        body, name=name,
        out_shape=sem_shapes + [pltpu.HBM(a.shape, a.dtype) for a in srcs + lands] + [_sds((8, 128), F32)],
        in_specs=[HBM] * (2 * n) + [ANY] * n_after,
        out_specs=[SEM] * (2 * n_grp) + [HBM] * (2 * n) + [pl.BlockSpec(memory_space=pltpu.VMEM)],
        input_output_aliases={i: 2 * n_grp + i for i in range(2 * n)},
        compiler_params=pltpu.CompilerParams(has_side_effects=EFFECT),
    )(*operands)
    thru = res[2 * n_grp:2 * n_grp + 2 * n]
    handles, lo = [], 0
    for gi, sz in enumerate(sizes):
        handles.append((res[2 * gi], res[2 * gi + 1], list(thru[lo:lo + sz]), list(thru[n + lo:n + lo + sz]), scatter))
        lo += sz
    return handles, res[-1]


def _push_wait(name, handle, after):
    send_sems, recv_sems, srcs, lands, scatter = handle
    n = len(srcs)

    def body(*refs):
        src_refs, land_refs = refs[:n], refs[n:2 * n]
        s_sems, r_sems = refs[2 * n], refs[2 * n + 1]
        for cp in _push_copies(src_refs, land_refs, s_sems, r_sems, scatter, True):
            cp.wait_send()
            cp.wait_recv()

    res = pl.pallas_call(
        body, name=name,
        out_shape=[pltpu.HBM(a.shape, a.dtype) for a in srcs + lands],
        in_specs=[HBM] * (2 * n) + [SEM, SEM, ANY], out_specs=[HBM] * (2 * n),
        input_output_aliases={i: i for i in range(2 * n)},
        compiler_params=pltpu.CompilerParams(has_side_effects=EFFECT),
    )(*srcs, *lands, send_sems, recv_sems, after)
    return list(res[n:])


def _adamw(name, parts, w, m, v, own=None):
    r, c = w.shape
    br = _pick(r, 256, 16)
    has_own = own is not None

    def body(*refs):
        if has_own:
            p_ref, own_ref, w_ref, m_ref, v_ref, g_ref, d_ref, nm_ref, nv_ref = refs
            me = _my_id()
            mine = own_ref[...].astype(F32)
        else:
            p_ref, w_ref, m_ref, v_ref, g_ref, d_ref, nm_ref, nv_ref = refs
        g = None
        for k in range(N_DEV):
            t = p_ref[k].astype(F32)
            if has_own:
                t = jnp.where(me == k, mine, t)
            g = t if g is None else g + t
        mm = ADAM_B1 * m_ref[...] + (1.0 - ADAM_B1) * g
        vv = ADAM_B2 * v_ref[...] + (1.0 - ADAM_B2) * (g * g)
        m_hat = mm / (1.0 - ADAM_B1 ** ADAM_STEP)
        v_hat = vv / (1.0 - ADAM_B2 ** ADAM_STEP)
        g_ref[...] = g
        d_ref[...] = -ADAM_LR * (m_hat / (jnp.sqrt(v_hat) + ADAM_EPS) + ADAM_WD * w_ref[...])
        nm_ref[...] = mm
        nv_ref[...] = vv

    spec = _rows(br, c)
    out = (_sds((r, c), F32), spec)
    ins = [(parts, pl.BlockSpec((N_DEV, br, c), lambda i: (0, i, 0)))] + ([(own, spec)] if has_own else [])
    return _call(name, body, (r // br,), ins + [(w, spec), (m, spec), (v, spec)], [out] * 4, sem=("parallel",))


def _pad_head_cols(w, d):
    k = w.shape[0]
    return jnp.pad(w.reshape(k, HEADS, d), ((0, 0), (0, 0), (0, HP - d))).reshape(k, HW)


def _unpad_head_cols(wp, d):
    k = wp.shape[0]
    return wp.reshape(k, HEADS, HP)[:, :, :d].reshape(k, HEADS * d)


def _pad_head_rows(w, d):
    n = w.shape[1]
    return jnp.pad(w.reshape(HEADS, d, n), ((0, 0), (0, HP - d), (0, 0))).reshape(HW, n)


def _unpad_head_rows(wp, d):
    n = wp.shape[1]
    return wp.reshape(HEADS, HP, n)[:, :d, :].reshape(HEADS * d, n)


def _w_in_runs():
    nat = {}
    o = 0
    for nm, wd in (("q", Q_RANK), ("kv", KV_RANK), ("kr", ROPE), ("fq", FOX_W), ("fk", FOX_W), ("fv", FOX_W),
                   ("fl", HEADS), ("gate", 2 * D_MODEL)):
        nat[nm] = o
        o += wd
    runs = [(1, R_QLAT, nat["q"], Q_RANK, 1.0), (1, R_KVLAT, nat["kv"], KV_RANK, 1.0),
            (1, R_LAST + LANE_FL, nat["fl"], HEADS, 1.0), (1, R_LAST + LANE_PE, nat["kr"], ROPE, 1.0),
            (1, R_GATE, nat["gate"], 2 * D_MODEL, 1.0)]
    for grp, (nm, sc) in enumerate((("fq", FOX_SCALE), ("fk", 1.0), ("fv", 1.0))):
        runs += [(0, grp * HW + h * HP, nat[nm] + h * FOX_DIM, FOX_DIM, sc) for h in range(HEADS)]
    return runs


def _sharded_runs(runs, shard_cols):
    out = []
    for half, col, ncol, width, sc in runs:
        while width > 0:
            d, local = divmod(ncol, shard_cols)
            wd = min(width, shard_cols - local)
            out.append((half, col, d, local, wd, sc))
            col, ncol, width = col + wd, ncol + wd, width - wd
    return out


def _remap(name, srcs, out_shapes, moves):
    rows = srcs[0].shape[-2]
    br = _pick(rows, 256, 16)
    ns = len(srcs)

    def spec(shape):
        if len(shape) == 2:
            return pl.BlockSpec((br, shape[1]), lambda i: (i, 0))
        return pl.BlockSpec((shape[0], br, shape[2]), lambda i: (0, i, 0))

    def body(*refs):
        s_refs, o_refs = refs[:ns], refs[ns:]
        for o in o_refs:
            o[...] = jnp.zeros_like(o)
        for di, dl, dc, si, sl, sc0, wd, scale in moves:
            v = s_refs[si][:, sc0:sc0 + wd] if sl is None else s_refs[si][sl, :, sc0:sc0 + wd]
            if scale != 1.0:
                v = v * jnp.asarray(scale, v.dtype)
            v = v.astype(o_refs[di].dtype)
            if dl is None:
                o_refs[di][:, dc:dc + wd] = v
            else:
                o_refs[di][dl, :, dc:dc + wd] = v

    return _call(name, body, (rows // br,), [(a, spec(a.shape)) for a in srcs],
                 [(_sds(shape, dt), spec(shape)) for shape, dt in out_shapes], sem=("parallel",))


def _w_in_from_shards(g3):
    n, rows, c = g3.shape
    moves = [(half, None, col, 0, d, local, wd, sc) for half, col, d, local, wd, sc in _sharded_runs(_w_in_runs(), c)]
    return _remap("w_in_repack", [g3], [((rows, F_W), g3.dtype), ((rows, R_W), g3.dtype)], moves)


def _w_in_grad_to_shards(d_fused, d_rest, n, c):
    rows = d_fused.shape[0]
    moves = [(0, d, local, half, None, col, wd, sc) for half, col, d, local, wd, sc in _sharded_runs(_w_in_runs(), c)]
    return _remap("w_in_grad_unpack", [d_fused, d_rest], [((n, rows, c), d_fused.dtype)], moves)[0]


def _rows_from_shards(name, land, own):
    n, rows, c = land.shape

    def body(land_ref, own_ref, o_ref):
        o_ref[...] = jnp.where(_my_id() == pl.program_id(0), own_ref[...], land_ref[...])

    return _call(name, body, (n,),
                 [(land, pl.BlockSpec((None, rows, c), lambda d: (d, 0, 0))), (own, _whole((rows, c)))],
                 [(_sds((n * rows, c), land.dtype), pl.BlockSpec((rows, c), lambda d: (d, 0)))], sem=("parallel",))[0]


def _cols_from_shards(name, land, own):
    n, rows, c = land.shape
    br = _pick(rows, 256, 16)

    def body(land_ref, own_ref, o_ref):
        me = _my_id()
        for d in range(n):
            o_ref[:, c * d:c * (d + 1)] = jnp.where(me == d, own_ref[...], land_ref[d])

    return _call(name, body, (rows // br,),
                 [(land, pl.BlockSpec((n, br, c), lambda i: (0, i, 0))), (own, _rows(br, c))],
                 [(_sds((rows, n * c), land.dtype), _rows(br, n * c))], sem=("parallel",))[0]


def _cols_to_shards(name, full, n):
    rows, nc = full.shape
    c = nc // n
    return _remap(name, [full], [((n, rows, c), full.dtype)], [(0, d, 0, 0, None, c * d, c, 1.0) for d in range(n)])[0]


def _split_w_kv(w):
    k = w.shape[0]
    w3 = w.reshape(k, HEADS, NOPE + V_DIM)
    padl = lambda a: jnp.pad(a, ((0, 0), (0, 0), (0, HP - a.shape[-1]))).reshape(k, HW)
    return padl(w3[..., :NOPE]), padl(w3[..., NOPE:])


def _merge_w_kv(wk, wv):
    k = wk.shape[0]
    return jnp.concatenate([wk.reshape(k, HEADS, HP)[..., :NOPE], wv.reshape(k, HEADS, HP)[..., :V_DIM]],
                           axis=-1).reshape(k, HEADS * (NOPE + V_DIM))


class _NoComm:
    first_token = ()

    def late_weights(self, group, after):
        return {}

    def send(self, name, grads):
        return ()


def _local_step(x, tgt, p, comm=_NoComm()):
    seq = x.shape[0]
    r = -(-(N_META + seq) // ROW_ALIGN) * ROW_ALIGN
    cd = MXU_DTYPE
    p = dict(p)

    w_f, w_r = p["w_in"]

    pos = jnp.arange(r, dtype=F32)
    inv_freq = ROPE_THETA ** (-jnp.arange(HALF, dtype=F32) / HALF)
    ang = pos[:, None] * inv_freq[None, :]
    cos_t = jnp.tile(jnp.cos(ang), (1, HP // HALF))
    sin_t = jnp.tile(jnp.sin(ang), (1, HP // HALF))
    bf_row = jnp.zeros((1, HP), F32).at[0, LANE_FL:LANE_FL + HEADS].set(p["b_forget"])

    h0, h0b = _ln_emb_fwd(x, p["meta_tokens"], p["ln_emb_g"], p["ln_emb_b"], r, after=comm.first_token)
    proj_f = _matmul("in_proj_f", h0b, w_f, out_dtype=cd)
    proj_r = _matmul("in_proj_r", h0b, w_r)
    latent_gains = (p["q_norm_g"], p["kv_norm_g"])
    ql, kvl = _latent_norm_fwd(proj_r, latent_gains)
    p.update(comm.late_weights("qkv", ql))
    w_q = _pad_head_cols(p["w_q_up"], QK_DIM)
    w_kv = jnp.concatenate(_split_w_kv(p["w_kv_up"]), axis=1)
    q_raw = _matmul("q_up", ql, w_q)
    kv = _matmul("kv_up", kvl, w_kv, out_dtype=cd)
    q_mla, k_mla = _rope_fwd(q_raw, kv, proj_r, cos_t, sin_t)
    o_mla, o_mla_b, lse_mla = _attn_fwd("mla_fwd", (q_mla, 0), (k_mla, 0), (kv, 1))

    cum, cum_t = _forget_fwd(proj_r, bf_row)
    o_fox, o_fox_b, lse_fox = _attn_fwd("fox_fwd", (proj_f, 0), (proj_f, 1), (proj_f, 2), cum, cum_t)

    p.update(comm.late_weights("mix", o_fox_b))
    w_bm = _pad_head_rows(p["w_branch_mla"], V_DIM)
    w_bf = _pad_head_rows(p["w_branch_fox"], FOX_DIM)
    bm = _matmul("branch_mla", o_mla_b, w_bm, out_dtype=cd)
    bfx = _matmul("branch_fox", o_fox_b, w_bf, out_dtype=cd)
    merged = _gate_fwd(proj_r, p["b_gate"], bm, bfx)
    mix = _matmul("out_proj", merged, p["w_out"])
    h1, h1b = _ln_fwd("ln_mix_fwd", h0, mix, p["ln_mix_g"], p["ln_mix_b"])
    p.update(comm.late_weights("ffn", h1b))
    up = _matmul("ffn_up", h1b, p["w_ffn_up"], out_dtype=cd)
    act = _glu_fwd(up, p["conv_w"], p["conv_b"])
    f = _matmul("ffn_down", act, p["w_ffn_down"])
    loss = _ln_ffn_loss(h1, f, tgt, p["ln_ffn_g"], p["ln_ffn_b"])

    g = {}
    dz2, dz2b, g["ln_ffn_g"], g["ln_ffn_b"] = _ln_ffn_bwd(h1, f, tgt, p["ln_ffn_g"], p["ln_ffn_b"])
    d_act = _matmul("ffn_down_dx", dz2b, p["w_ffn_down"], tb=True, out_dtype=cd)
    g["w_ffn_down"] = _matmul("ffn_down_dw", act, dz2b, ta=True, out_dtype=cd)
    d_up, dcw, g["conv_b"] = _glu_bwd(up, p["conv_w"], p["conv_b"], d_act)
    g["conv_w"] = dcw[:3]
    dh1 = _matmul("ffn_up_dx", d_up, p["w_ffn_up"], tb=True, addend=dz2, alpha=ALPHA)
    g["w_ffn_up"] = _matmul("ffn_up_dw", h1b, d_up, ta=True, out_dtype=cd)
    sent = comm.send("ffn", {n: g[n] for n in ("w_ffn_down", "w_ffn_up", "conv_w")})
    dz1, dz1b, g["ln_mix_g"], g["ln_mix_b"] = _ln_bwd("ln_mix_bwd", h0, mix, dh1, p["ln_mix_g"], after=sent)
    dmerged = _matmul("out_proj_dx", dz1b, p["w_out"], tb=True, out_dtype=cd)
    g["w_out"] = _matmul("out_proj_dw", merged, dz1b, ta=True, out_dtype=cd)
    d_bm, d_bf, d_gl, g["b_gate"] = _gate_bwd(proj_r, p["b_gate"], bm, bfx, dmerged)
    do_mla_b = _matmul("branch_mla_dx", d_bm, w_bm, tb=True, out_dtype=cd)
    g["w_branch_mla"] = _unpad_head_rows(_matmul("branch_mla_dw", o_mla_b, d_bm, ta=True, out_dtype=cd), V_DIM)
    do_fox_b = _matmul("branch_fox_dx", d_bf, w_bf, tb=True, out_dtype=cd)
    g["w_branch_fox"] = _unpad_head_rows(_matmul("branch_fox_dw", o_fox_b, d_bf, ta=True, out_dtype=cd), FOX_DIM)

    sent = comm.send("mix", {n: g[n] for n in ("w_out", "w_branch_mla", "w_branch_fox")})
    dl_mla = _attn_delta("mla_delta", do_mla_b, o_mla, after=sent)
    dq_m, dk_m, dv_m = _attn_bwd("mla_bwd", (q_mla, 0), (k_mla, 0), (kv, 1), do_mla_b, lse_mla, dl_mla)
    dl_fox = _attn_delta("fox_delta", do_fox_b, o_fox)
    dfq, dfk, dfv, dcq, dck = _attn_bwd("fox_bwd", (proj_f, 0), (proj_f, 1), (proj_f, 2), do_fox_b, lse_fox, dl_fox,
                                        cum, cum_t, out_dtype=cd)
    dfl, dbf = _forget_bwd(proj_r, bf_row, dcq, dck)
    g["b_forget"] = dbf[:, LANE_FL:LANE_FL + HEADS]

    dq_b, dkv_b, dlast = _rope_bwd(dq_m, dk_m, dv_m, dfl, cos_t, sin_t)
    d_ql = _matmul("q_up_dx", dq_b, w_q, tb=True)
    d_kvl = _matmul("kv_up_dx", dkv_b, w_kv, tb=True)
    d_qlat, d_kvlat, g["q_norm_g"], g["kv_norm_g"] = _latent_norm_bwd(proj_r, (d_ql, d_kvl), latent_gains)
    dproj_f = jnp.concatenate([dfq, dfk, dfv], axis=1)
    dproj_r = jnp.concatenate([d_qlat, d_kvlat, dlast, jnp.zeros((r, R_GATE - R_LAST - HP), cd), d_gl], axis=1)
    g["w_in"] = (_matmul("in_proj_f_dw", h0b, dproj_f, ta=True, out_dtype=cd),
                 _matmul("in_proj_r_dw", h0b, dproj_r, ta=True, out_dtype=cd))
    sent = comm.send("in", {"w_in": g["w_in"]})
    dh0 = _matmul("in_proj_f_dx", dproj_f, w_f, tb=True, addend=dz1, alpha=ALPHA, after=sent)
    g["w_q_up"] = _unpad_head_cols(_matmul("q_up_dw", ql, dq_b, ta=True, out_dtype=cd, after=sent), QK_DIM)
    dw_kv = _matmul("kv_up_dw", kvl, dkv_b, ta=True, out_dtype=cd, after=sent)
    g["w_kv_up"] = _merge_w_kv(dw_kv[:, :HW], dw_kv[:, HW:])
    sent = comm.send("qkv", {n: g[n] for n in ("w_q_up", "w_kv_up")})
    dh0 = _matmul("in_proj_r_dx", dproj_r, w_r, tb=True, addend=dh0, after=sent)
    grad_x, d_meta, g["ln_emb_g"], g["ln_emb_b"] = _ln_emb_bwd(x, p["meta_tokens"], dh0, p["ln_emb_g"])
    return loss, grad_x, d_meta, g


BIG = (("w_in", 1), ("w_q_up", 1), ("w_kv_up", 1), ("w_branch_mla", 1), ("w_branch_fox", 1), ("w_out", 0),
       ("w_ffn_up", 1), ("w_ffn_down", 0))
SMALL_SHARDED = (("meta_tokens", 1), ("conv_w", 1))
EARLY = ("w_in", "meta_tokens")
LATE = {"qkv": ("w_q_up", "w_kv_up", "conv_w"),
        "mix": ("w_branch_mla", "w_branch_fox", "w_out"),
        "ffn": ("w_ffn_up", "w_ffn_down")}
REPLICATED = ("ln_emb_g", "ln_emb_b", "b_gate", "b_forget", "q_norm_g", "kv_norm_g", "ln_mix_g", "ln_mix_b",
              "conv_b", "ln_ffn_g", "ln_ffn_b")
PACK_COLS = 1024


def _pack(flat_list):
    cat = jnp.concatenate(flat_list)
    n = cat.shape[0]
    rows = -(-n // (8 * PACK_COLS)) * 8
    return jnp.pad(cat, (0, rows * PACK_COLS - n)).reshape(rows, PACK_COLS)


def _gathered_full(g3, axis):
    n, r, c = g3.shape
    if axis == 0:
        return g3.reshape(n * r, c)
    return g3.transpose(1, 0, 2).reshape(r, n * c)


def _shard_major(full, axis):
    r, c = full.shape
    if axis == 0:
        return full.reshape(N_DEV, r // N_DEV, c)
    return full.reshape(r, N_DEV, c // N_DEV).transpose(1, 0, 2)


def kernel(x, meta_tokens, ln_emb_g, ln_emb_b, w_in, b_gate, b_forget, q_norm_g, w_q_up, kv_norm_g, w_kv_up, w_branch_mla, w_branch_fox, w_out, ln_mix_g, ln_mix_b, w_ffn_up, conv_w, conv_b, w_ffn_down, ln_ffn_g, ln_ffn_b, loss_target, m_meta_tokens, m_ln_emb_g, m_ln_emb_b, m_w_in, m_b_gate, m_b_forget, m_q_norm_g, m_w_q_up, m_kv_norm_g, m_w_kv_up, m_w_branch_mla, m_w_branch_fox, m_w_out, m_ln_mix_g, m_ln_mix_b, m_w_ffn_up, m_conv_w, m_conv_b, m_w_ffn_down, m_ln_ffn_g, m_ln_ffn_b, v_meta_tokens, v_ln_emb_g, v_ln_emb_b, v_w_in, v_b_gate, v_b_forget, v_q_norm_g, v_w_q_up, v_kv_norm_g, v_w_kv_up, v_w_branch_mla, v_w_branch_fox, v_w_out, v_ln_mix_g, v_ln_mix_b, v_w_ffn_up, v_conv_w, v_conv_b, v_w_ffn_down, v_ln_ffn_g, v_ln_ffn_b):
    names = ("meta_tokens", "ln_emb_g", "ln_emb_b", "w_in", "b_gate", "b_forget", "q_norm_g", "w_q_up", "kv_norm_g",
             "w_kv_up", "w_branch_mla", "w_branch_fox", "w_out", "ln_mix_g", "ln_mix_b", "w_ffn_up", "conv_w", "conv_b",
             "w_ffn_down", "ln_ffn_g", "ln_ffn_b")
    w_args = (meta_tokens, ln_emb_g, ln_emb_b, w_in, b_gate, b_forget, q_norm_g, w_q_up, kv_norm_g, w_kv_up,
              w_branch_mla, w_branch_fox, w_out, ln_mix_g, ln_mix_b, w_ffn_up, conv_w, conv_b, w_ffn_down, ln_ffn_g, ln_ffn_b)
    m_args = (m_meta_tokens, m_ln_emb_g, m_ln_emb_b, m_w_in, m_b_gate, m_b_forget, m_q_norm_g, m_w_q_up, m_kv_norm_g,
              m_w_kv_up, m_w_branch_mla, m_w_branch_fox, m_w_out, m_ln_mix_g, m_ln_mix_b, m_w_ffn_up, m_conv_w, m_conv_b,
              m_w_ffn_down, m_ln_ffn_g, m_ln_ffn_b)
    v_args = (v_meta_tokens, v_ln_emb_g, v_ln_emb_b, v_w_in, v_b_gate, v_b_forget, v_q_norm_g, v_w_q_up, v_kv_norm_g,
              v_w_kv_up, v_w_branch_mla, v_w_branch_fox, v_w_out, v_ln_mix_g, v_ln_mix_b, v_w_ffn_up, v_conv_w, v_conv_b,
              v_w_ffn_down, v_ln_ffn_g, v_ln_ffn_b)
    as2d = lambda a: a.reshape((-1, a.shape[-1])) if a.ndim != 1 else a.reshape(1, -1)
    w = {n: as2d(a) for n, a in zip(names, w_args)}
    m = {n: as2d(a) for n, a in zip(names, m_args)}
    v = {n: as2d(a) for n, a in zip(names, v_args)}
    out_shape = {n: a.shape for n, a in zip(names, w_args)}

    axis_of = dict(BIG + SMALL_SHARDED)
    big = set(n for n, _ in BIG)
    wire = lambda n, a: a.astype(MXU_DTYPE) if n in big else a
    my_id = _my_id()

    early = _allgather("gather_early", [wire(n, w[n]) for n in EARLY])
    p = {n: _gathered_full(g3, axis_of[n]) for n, g3 in zip(EARLY, early) if n != "w_in"}
    p["w_in"] = _w_in_from_shards(early[EARLY.index("w_in")])
    for n in REPLICATED:
        p[n] = w[n].reshape(-1)
    late_src = [[wire(n, w[n]) for n in members] for members in LATE.values()]
    late_handles, late_token = _push_start("gather_late_start", late_src, False, after=early[0])
    late = {group: (members, src, handle)
            for (group, members), src, handle in zip(LATE.items(), late_src, late_handles)}
    sent = {}

    class Comm:
        first_token = (late_token,)

        def late_weights(self, group, after):
            members, src, handle = late[group]
            lands = _push_wait("gather_" + group + "_wait", handle, after)
            out = {}
            for n, own, land in zip(members, src, lands):
                if own.shape[0] % 16:
                    out[n] = _gathered_full(lax.dynamic_update_index_in_dim(land, own, my_id, 0), axis_of[n])
                elif axis_of[n] == 1:
                    out[n] = _cols_from_shards(n + "_repack", land, own)
                else:
                    out[n] = _rows_from_shards(n + "_repack", land, own)
            return out

        def send(self, name, grads):
            names_ = tuple(grads)
            parts = []
            for n in names_:
                if n == "w_in":
                    parts.append(_w_in_grad_to_shards(*grads[n], N_DEV, w[n].shape[1]))
                elif n == "w_ffn_up":
                    parts.append(_cols_to_shards(n + "_grad_unpack", grads[n], N_DEV))
                else:
                    parts.append(_shard_major(grads[n], axis_of[n]).astype(MXU_DTYPE))
            (handle,), token = _push_start("send_" + name + "_start", [parts], True)
            sent[name] = (names_, parts, handle)
            return (token,)

    loss_part, grad_x, d_meta, g = _local_step(x[0], loss_target[0], p, Comm())
    grad_x = grad_x[None]

    small = _pack([d_meta.reshape(-1)] + [g[n].reshape(-1) for n in REPLICATED] + [loss_part.reshape(-1)])
    (small_handle,), small_token = _push_start("send_small_start", [[small]], False)

    res = {}
    prev = small_token
    for name, (names_, parts, handle) in sent.items():
        lands = _push_wait("send_" + name + "_wait", handle, prev)
        for n, part, land in zip(names_, parts, lands):
            own = lax.dynamic_index_in_dim(part, my_id, axis=0, keepdims=False)
            res[n] = _adamw("adamw_" + n, land, w[n], m[n], v[n], own=own)
            prev = res[n][0]
    small_all = _push_wait("send_small_wait", small_handle, prev)[0]
    head = jnp.zeros((d_meta.size,), F32)
    rep_w = _pack([head] + [w[n].reshape(-1) for n in REPLICATED])
    rep_m = _pack([head] + [m[n].reshape(-1) for n in REPLICATED])
    rep_v = _pack([head] + [v[n].reshape(-1) for n in REPLICATED])
    rep_res = _adamw("adamw_replicated", small_all, rep_w, rep_m, rep_v, own=small)
    off = d_meta.size
    for n in REPLICATED:
        sz = w[n].size
        res[n] = tuple(a.reshape(-1)[off:off + sz] for a in rep_res)
        off += sz
    loss = rep_res[0].reshape(-1)[off]
    cols = w["meta_tokens"].shape[1]
    meta_rows = lambda a: a.reshape(a.shape[:-2] + (-1,))[..., :d_meta.size].reshape(a.shape[:-2] + d_meta.shape)
    my_cols = lambda a: lax.dynamic_slice_in_dim(a, my_id * cols, cols, axis=a.ndim - 1)
    res["meta_tokens"] = _adamw("adamw_meta_tokens", my_cols(meta_rows(small_all)), w["meta_tokens"],
                                m["meta_tokens"], v["meta_tokens"], own=my_cols(d_meta))

    outs = [loss, grad_x]
    for idx in range(4):
        outs += [res[n][idx].reshape(out_shape[n]) for n in names]
    return tuple(outs)
```

```python
import jax
import jax.numpy as jnp
from jax import lax
from jax.experimental import pallas as pl
from jax.experimental.pallas import tpu as pltpu

F32 = jnp.float32
BF16 = jnp.bfloat16
MXU_DTYPE = BF16

N_DEV = 8
N_META = 16
D_MODEL = 1024
HEADS = 8
Q_RANK = 384
KV_RANK = 128
NOPE = 64
ROPE = 32
HALF = ROPE // 2
QK_DIM = NOPE + ROPE
V_DIM = 64
FOX_DIM = 64
FOX_W = HEADS * FOX_DIM
D_FF = 2816
ROPE_THETA = 10000.0
LN_EPS = 1e-5
RMS_EPS = 1e-6
ALPHA = 2.0 ** 0.25
MLA_SCALE = QK_DIM ** -0.5
FOX_SCALE = FOX_DIM ** -0.5
NEG_INF = -1e30

HP = 128
HW = HEADS * HP
F_W = 3 * HW
R_QLAT = 0
R_KVLAT = Q_RANK
R_LAST = R_KVLAT + KV_RANK
R_GATE = D_MODEL
R_W = R_GATE + 2 * D_MODEL
LANE_FL = 0
LANE_PE = NOPE

ADAM_LR = 0.001
ADAM_B1 = 0.9
ADAM_B2 = 0.999
ADAM_EPS = 1e-08
ADAM_WD = 0.01
ADAM_STEP = 10

ROW_BLOCK = 256
ATT_TQ = 768
ATT_TK = 768
ATT_HEADS = 4
ROW_ALIGN = 768
MM_BLOCK_CAP = 1408
VMEM_LIMIT = 56 * 1024 * 1024
HIGHEST = lax.Precision.HIGHEST
NT = (((1,), (1,)), ((), ()))
TN = (((0,), (0,)), ((), ()))


def _params(sem=None):
    return pltpu.CompilerParams(dimension_semantics=sem, vmem_limit_bytes=VMEM_LIMIT)


def _call(name, body, grid, ins, outs, scratch=(), sem=None, after=()):
    n_in = len(ins)
    n_tok = len(after)

    def run(*refs):
        body(*refs[:n_in], *refs[n_in + n_tok:])

    tok_spec = pl.BlockSpec((8, 128), lambda *_: (0, 0))
    return pl.pallas_call(
        run, name=name, grid=grid,
        in_specs=[s for _, s in ins] + [tok_spec] * n_tok,
        out_specs=[s for _, s in outs],
        out_shape=[o for o, _ in outs],
        scratch_shapes=list(scratch),
        compiler_params=_params(sem),
    )(*[a for a, _ in ins], *after)


def _sds(shape, dtype):
    return jax.ShapeDtypeStruct(shape, dtype)


def _rows(br, c, cb=0):
    return pl.BlockSpec((br, c), lambda i: (i, cb))


def _whole(shape):
    n = len(shape)
    return pl.BlockSpec(shape, lambda i: (0,) * n)


def _pick(dim, cap, mult):
    best = None
    d = mult
    while d <= min(dim, cap):
        if dim % d == 0:
            best = d
        d += mult
    return best if best is not None else dim


def _hs(h):
    return slice(h * HP, (h + 1) * HP)


def _matmul(name, a, b, *, ta=False, tb=False, out_dtype=F32, addend=None, alpha=1.0, after=()):
    if ta:
        k, m = a.shape
    else:
        m, k = a.shape
    if tb:
        n, k2 = b.shape
    else:
        k2, n = b.shape
    assert k == k2, (name, a.shape, b.shape)
    bm = _pick(m, MM_BLOCK_CAP, 128 if ta else 16)
    bn = _pick(n, MM_BLOCK_CAP, 128)
    bk = _pick(k, MM_BLOCK_CAP, 128 if (not ta or tb) else 16)
    nk = k // bk
    dims = (((0 if ta else 1,), (1 if tb else 0,)), ((), ()))
    has_add = addend is not None

    def body(*refs):
        a_ref, b_ref = refs[:2]
        add_ref = refs[2] if has_add else None
        o_ref = refs[3 if has_add else 2]

        def finish(r):
            if has_add:
                r = r + alpha * add_ref[...]
            o_ref[...] = r.astype(o_ref.dtype)

        part = lax.dot_general(a_ref[...], b_ref[...], dims, preferred_element_type=F32)
        if nk == 1:
            finish(part)
            return
        acc_ref = refs[-1]
        kk = pl.program_id(2)

        @pl.when(kk == 0)
        def _():
            acc_ref[...] = part

        @pl.when(kk > 0)
        def _():
            acc_ref[...] += part

        @pl.when(kk == nk - 1)
        def _():
            finish(acc_ref[...])

    a_spec = pl.BlockSpec((bk, bm), lambda i, j, l: (l, i)) if ta else pl.BlockSpec((bm, bk), lambda i, j, l: (i, l))
    b_spec = pl.BlockSpec((bn, bk), lambda i, j, l: (j, l)) if tb else pl.BlockSpec((bk, bn), lambda i, j, l: (l, j))
    o_spec = pl.BlockSpec((bm, bn), lambda i, j, l: (i, j))
    ins = [(a, a_spec), (b, b_spec)]
    if has_add:
        ins.append((addend, o_spec))
    return _call(name, body, (m // bm, n // bn, nk), ins, [(_sds((m, n), out_dtype), o_spec)],
                 scratch=[pltpu.VMEM((bm, bn), F32)] if nk > 1 else [],
                 sem=("parallel", "parallel", "arbitrary"), after=after)[0]


def _ln_stats(z):
    mu = jnp.mean(z, axis=-1, keepdims=True)
    zc = z - mu
    var = jnp.mean(zc * zc, axis=-1, keepdims=True)
    rstd = lax.rsqrt(var + LN_EPS)
    return zc * rstd, rstd


def _ln_fwd(name, a, res, g, b, after=()):
    r, d = a.shape
    br = ROW_BLOCK
    has_res = res is not None

    def body(*refs):
        if has_res:
            a_ref, r_ref, g_ref, b_ref, y_ref, yb_ref = refs
            z = ALPHA * a_ref[...] + r_ref[...]
        else:
            a_ref, g_ref, b_ref, y_ref, yb_ref = refs
            z = a_ref[...]
        xhat, _ = _ln_stats(z)
        y = xhat * g_ref[...] + b_ref[...]
        y_ref[...] = y
        yb_ref[...] = y.astype(yb_ref.dtype)

    ins = [(a, _rows(br, d))]
    if has_res:
        ins.append((res, _rows(br, d)))
    ins += [(g.reshape(1, d), _whole((1, d))), (b.reshape(1, d), _whole((1, d)))]
    outs = [(_sds((r, d), F32), _rows(br, d)), (_sds((r, d), MXU_DTYPE), _rows(br, d))]
    return _call(name, body, (r // br,), ins, outs, sem=("parallel",), after=after)


def _ln_bwd(name, a, res, dy, g, after=()):
    r, d = a.shape
    br = ROW_BLOCK
    has_res = res is not None

    def body(*refs):
        if has_res:
            a_ref, r_ref, dy_ref, g_ref, dz_ref, dzb_ref, dg_ref, db_ref = refs
            z = ALPHA * a_ref[...] + r_ref[...]
        else:
            a_ref, dy_ref, g_ref, dz_ref, dzb_ref, dg_ref, db_ref = refs
            z = a_ref[...]
        xhat, rstd = _ln_stats(z)
        dyv = dy_ref[...]
        dyg = dyv * g_ref[...]
        m1 = jnp.mean(dyg, axis=-1, keepdims=True)
        m2 = jnp.mean(dyg * xhat, axis=-1, keepdims=True)
        dz = rstd * (dyg - m1 - xhat * m2)
        dz_ref[...] = dz
        dzb_ref[...] = dz.astype(dzb_ref.dtype)

        @pl.when(pl.program_id(0) == 0)
        def _():
            dg_ref[...] = jnp.zeros_like(dg_ref)
            db_ref[...] = jnp.zeros_like(db_ref)

        dg_ref[...] += jnp.sum(dyv * xhat, axis=0, keepdims=True)
        db_ref[...] += jnp.sum(dyv, axis=0, keepdims=True)

    ins = [(a, _rows(br, d))]
    if has_res:
        ins.append((res, _rows(br, d)))
    ins += [(dy, _rows(br, d)), (g.reshape(1, d), _whole((1, d)))]
    outs = [(_sds((r, d), F32), _rows(br, d)), (_sds((r, d), MXU_DTYPE), _rows(br, d)),
            (_sds((1, d), F32), _whole((1, d))), (_sds((1, d), F32), _whole((1, d)))]
    return _call(name, body, (r // br,), ins, outs, sem=("arbitrary",), after=after)


LATENTS = ((R_QLAT // Q_RANK, Q_RANK), (R_KVLAT // KV_RANK, KV_RANK))


def _latent_norm_fwd(proj_r, gains):
    r = proj_r.shape[0]
    br = ROW_BLOCK

    def body(xq_ref, xk_ref, gq_ref, gk_ref, yq_ref, yk_ref):
        for x_ref, g_ref, y_ref in ((xq_ref, gq_ref, yq_ref), (xk_ref, gk_ref, yk_ref)):
            x = x_ref[...]
            rstd = lax.rsqrt(jnp.mean(x * x, axis=-1, keepdims=True) + RMS_EPS)
            y_ref[...] = (x * rstd * g_ref[...]).astype(y_ref.dtype)

    return _call("latent_norm_fwd", body, (r // br,),
                 [(proj_r, _rows(br, wd, cb)) for cb, wd in LATENTS]
                 + [(g.reshape(1, wd), _whole((1, wd))) for g, (_, wd) in zip(gains, LATENTS)],
                 [(_sds((r, wd), MXU_DTYPE), _rows(br, wd)) for _, wd in LATENTS], sem=("parallel",))


def _latent_norm_bwd(proj_r, dys, gains):
    r = proj_r.shape[0]
    br = ROW_BLOCK

    def body(xq_ref, xk_ref, dq_ref, dk_ref, gq_ref, gk_ref, oq_ref, ok_ref, dgq_ref, dgk_ref):
        @pl.when(pl.program_id(0) == 0)
        def _():
            dgq_ref[...] = jnp.zeros_like(dgq_ref)
            dgk_ref[...] = jnp.zeros_like(dgk_ref)

        for x_ref, dy_ref, g_ref, dx_ref, dg_ref in ((xq_ref, dq_ref, gq_ref, oq_ref, dgq_ref),
                                                     (xk_ref, dk_ref, gk_ref, ok_ref, dgk_ref)):
            x = x_ref[...]
            rstd = lax.rsqrt(jnp.mean(x * x, axis=-1, keepdims=True) + RMS_EPS)
            nrm = x * rstd
            dyv = dy_ref[...]
            dyg = dyv * g_ref[...]
            dx_ref[...] = (rstd * (dyg - nrm * jnp.mean(dyg * nrm, axis=-1, keepdims=True))).astype(dx_ref.dtype)
            dg_ref[...] += jnp.sum(dyv * nrm, axis=0, keepdims=True)

    return _call("latent_norm_bwd", body, (r // br,),
                 [(proj_r, _rows(br, wd, cb)) for cb, wd in LATENTS]
                 + [(dy, _rows(br, wd)) for dy, (_, wd) in zip(dys, LATENTS)]
                 + [(g.reshape(1, wd), _whole((1, wd))) for g, (_, wd) in zip(gains, LATENTS)],
                 [(_sds((r, wd), MXU_DTYPE), _rows(br, wd)) for _, wd in LATENTS]
                 + [(_sds((1, wd), F32), _whole((1, wd))) for _, wd in LATENTS], sem=("arbitrary",))


def _lane_iota(shape):
    return lax.broadcasted_iota(jnp.int32, shape, 1)


def _rotary(t, c, s, lane, sign):
    second = pltpu.roll(t, HP - HALF, axis=1)
    first = pltpu.roll(t, HALF, axis=1)
    lo = (lane >= LANE_PE) & (lane < LANE_PE + HALF)
    hi = (lane >= LANE_PE + HALF) & (lane < LANE_PE + ROPE)
    return jnp.where(lo, t * c - sign * second * s, jnp.where(hi, t * c + sign * first * s, t))


def _rope_fwd(q_raw, k_part, proj_r, cos_t, sin_t):
    r = q_raw.shape[0]
    br = ROW_BLOCK

    def body(q_ref, k_ref, t_ref, c_ref, s_ref, qo_ref, ko_ref):
        c = c_ref[...]
        s = s_ref[...]
        lane = _lane_iota((br, HP))
        pe = (lane >= LANE_PE) & (lane < LANE_PE + ROPE)
        kp = jnp.where(pe, _rotary(t_ref[...], c, s, lane, 1.0), 0.0)
        for h in range(HEADS):
            qo_ref[:, _hs(h)] = (_rotary(q_ref[:, _hs(h)], c, s, lane, 1.0) * MLA_SCALE).astype(qo_ref.dtype)
            ko_ref[:, _hs(h)] = (k_ref[:, _hs(h)] + kp).astype(ko_ref.dtype)

    blk = _rows(br, HP)
    wide = _rows(br, HW)
    return _call("rope_fwd", body, (r // br,),
                 [(q_raw, wide), (k_part, wide), (proj_r, _rows(br, HP, R_LAST // HP)), (cos_t, blk), (sin_t, blk)],
                 [(_sds((r, HW), MXU_DTYPE), wide)] * 2, sem=("parallel",))


def _rope_bwd(dq, dk, dv, dfl, cos_t, sin_t):
    r = dq.shape[0]
    br = ROW_BLOCK

    def body(dq_ref, dk_ref, dv_ref, fl_ref, c_ref, s_ref, dqo_ref, dkv_ref, dl_ref):
        c = c_ref[...]
        s = s_ref[...]
        lane = _lane_iota((br, HP))
        pe = (lane >= LANE_PE) & (lane < LANE_PE + ROPE)
        acc = jnp.zeros((br, HP), F32)
        for h in range(HEADS):
            dqo_ref[:, _hs(h)] = (_rotary(dq_ref[:, _hs(h)], c, s, lane, -1.0) * MLA_SCALE).astype(dqo_ref.dtype)
            dkh = dk_ref[:, _hs(h)]
            acc = acc + dkh
            dkv_ref[:, _hs(h)] = dkh.astype(dkv_ref.dtype)
            dkv_ref[:, _hs(HEADS + h)] = dv_ref[:, _hs(h)].astype(dkv_ref.dtype)
        dl_ref[...] = (jnp.where(pe, _rotary(acc, c, s, lane, -1.0), 0.0) + fl_ref[...]).astype(dl_ref.dtype)

    blk = _rows(br, HP)
    wide = _rows(br, HW)
    return _call("rope_bwd", body, (r // br,),
                 [(dq, wide), (dk, wide), (dv, wide), (dfl, blk), (cos_t, blk), (sin_t, blk)],
                 [(_sds((r, HW), MXU_DTYPE), wide), (_sds((r, 2 * HW), MXU_DTYPE), _rows(br, 2 * HW)),
                  (_sds((r, HP), MXU_DTYPE), blk)],
                 sem=("parallel",))


def _log_sigmoid(x):
    return jnp.minimum(x, 0.0) - jnp.log(1.0 + jnp.exp(-jnp.abs(x)))


def _head_lane(x, h, lane):
    return jnp.sum(jnp.where(lane == h, x, 0.0), axis=1, keepdims=True)


def _forget_fwd(proj_r, bf_row):
    r = proj_r.shape[0]
    br = ROW_BLOCK

    def body(t_ref, b_ref, ob_ref, ot_ref, carry_ref):
        @pl.when(pl.program_id(0) == 0)
        def _():
            carry_ref[...] = jnp.zeros_like(carry_ref)

        x = t_ref[...] + b_ref[...]
        lane = _lane_iota(x.shape)
        lf = jnp.where((lane >= LANE_FL) & (lane < LANE_FL + HEADS), _log_sigmoid(x), 0.0)
        tri = (lax.broadcasted_iota(jnp.int32, (br, br), 0) >= lax.broadcasted_iota(jnp.int32, (br, br), 1)).astype(F32)
        cum = jnp.dot(tri, lf, precision=HIGHEST, preferred_element_type=F32) + carry_ref[0:1, :]
        for h in range(HEADS):
            ob_ref[:, _hs(h)] = jnp.broadcast_to(_head_lane(cum, LANE_FL + h, lane), (br, HP))
        ot_ref[...] = cum.T[LANE_FL:LANE_FL + HEADS, :]
        carry_ref[...] = jnp.broadcast_to(cum[br - 1:br, :], carry_ref.shape)

    return _call("forget_fwd", body, (r // br,),
                 [(proj_r, _rows(br, HP, R_LAST // HP)), (bf_row, _whole((1, HP)))],
                 [(_sds((r, HW), F32), _rows(br, HW)), (_sds((HEADS, r), F32), pl.BlockSpec((HEADS, br), lambda i: (0, i)))],
                 scratch=[pltpu.VMEM((8, HP), F32)], sem=("arbitrary",))


def _forget_bwd(proj_r, bf_row, dcq_t, dck_b):
    r = proj_r.shape[0]
    br = ROW_BLOCK
    nb = r // br

    def body(t_ref, b_ref, dcq_ref, dck_ref, o_ref, db_ref, carry_ref):
        @pl.when(pl.program_id(0) == 0)
        def _():
            carry_ref[...] = jnp.zeros_like(carry_ref)
            db_ref[...] = jnp.zeros_like(db_ref)

        lane = _lane_iota((br, HP))
        dc = jnp.concatenate([dcq_ref[...], jnp.zeros((HP - HEADS, br), F32)], axis=0).T
        for h in range(HEADS):
            dc = dc + jnp.where(lane == LANE_FL + h, dck_ref[:, h * HP:h * HP + 1], 0.0)
        triu = (lax.broadcasted_iota(jnp.int32, (br, br), 0) <= lax.broadcasted_iota(jnp.int32, (br, br), 1)).astype(F32)
        dlf = jnp.dot(triu, dc, precision=HIGHEST, preferred_element_type=F32) + carry_ref[0:1, :]
        carry_ref[...] = jnp.broadcast_to(dlf[0:1, :], carry_ref.shape)
        x = t_ref[...] + b_ref[...]
        dfl = jnp.where((lane >= LANE_FL) & (lane < LANE_FL + HEADS), dlf * jax.nn.sigmoid(-x), 0.0)
        o_ref[...] = dfl
        db_ref[...] += jnp.sum(dfl, axis=0, keepdims=True)

    rev = pl.BlockSpec((br, HP), lambda i: (nb - 1 - i, 0))
    return _call("forget_bwd", body, (nb,),
                 [(proj_r, pl.BlockSpec((br, HP), lambda i: (nb - 1 - i, R_LAST // HP))), (bf_row, _whole((1, HP))),
                  (dcq_t, pl.BlockSpec((HEADS, br), lambda i: (0, nb - 1 - i))),
                  (dck_b, pl.BlockSpec((br, HW), lambda i: (nb - 1 - i, 0)))],
                 [(_sds((r, HP), F32), rev), (_sds((1, HP), F32), _whole((1, HP)))],
                 scratch=[pltpu.VMEM((8, HP), F32)], sem=("arbitrary",))


def _gate_fwd(proj_r, b_gate, bm, bfx):
    r, d = bm.shape
    br = ROW_BLOCK
    cb = R_GATE // d

    def body(gm_ref, gf_ref, b1_ref, b2_ref, bm_ref, bf_ref, o_ref):
        g1 = jax.nn.sigmoid(gm_ref[...] + b1_ref[...])
        g2 = jax.nn.sigmoid(gf_ref[...] + b2_ref[...])
        o_ref[...] = (g1 * bm_ref[...].astype(F32) + g2 * bf_ref[...].astype(F32)).astype(o_ref.dtype)

    b1 = b_gate[:d].reshape(1, d)
    b2 = b_gate[d:].reshape(1, d)
    return _call("gate_fwd", body, (r // br,),
                 [(proj_r, _rows(br, d, cb)), (proj_r, _rows(br, d, cb + 1)), (b1, _whole((1, d))), (b2, _whole((1, d))),
                  (bm, _rows(br, d)), (bfx, _rows(br, d))],
                 [(_sds((r, d), MXU_DTYPE), _rows(br, d))], sem=("parallel",))[0]


def _gate_bwd(proj_r, b_gate, bm, bfx, dmerged):
    r, d = bm.shape
    br = ROW_BLOCK
    cb = R_GATE // d

    def body(gm_ref, gf_ref, b1_ref, b2_ref, bm_ref, bf_ref, dm_ref, dbm_ref, dbf_ref, dgl_ref, dbg_ref):
        g1 = jax.nn.sigmoid(gm_ref[...] + b1_ref[...])
        g2 = jax.nn.sigmoid(gf_ref[...] + b2_ref[...])
        dm = dm_ref[...].astype(F32)
        dbm_ref[...] = (dm * g1).astype(dbm_ref.dtype)
        dbf_ref[...] = (dm * g2).astype(dbf_ref.dtype)
        dl1 = dm * bm_ref[...].astype(F32) * (g1 * (1.0 - g1))
        dl2 = dm * bf_ref[...].astype(F32) * (g2 * (1.0 - g2))
        dgl_ref[:, 0:d] = dl1.astype(dgl_ref.dtype)
        dgl_ref[:, d:2 * d] = dl2.astype(dgl_ref.dtype)

        @pl.when(pl.program_id(0) == 0)
        def _():
            dbg_ref[...] = jnp.zeros_like(dbg_ref)

        dbg_ref[:, 0:d] += jnp.sum(dl1, axis=0, keepdims=True)
        dbg_ref[:, d:2 * d] += jnp.sum(dl2, axis=0, keepdims=True)

    b1 = b_gate[:d].reshape(1, d)
    b2 = b_gate[d:].reshape(1, d)
    return _call("gate_bwd", body, (r // br,),
                 [(proj_r, _rows(br, d, cb)), (proj_r, _rows(br, d, cb + 1)), (b1, _whole((1, d))), (b2, _whole((1, d))),
                  (bm, _rows(br, d)), (bfx, _rows(br, d)), (dmerged, _rows(br, d))],
                 [(_sds((r, d), MXU_DTYPE), _rows(br, d)), (_sds((r, d), MXU_DTYPE), _rows(br, d)),
                  (_sds((r, 2 * d), MXU_DTYPE), _rows(br, 2 * d)), (_sds((1, 2 * d), F32), _whole((1, 2 * d)))],
                 sem=("arbitrary",))


HALO = 16
GLU_BWD_BLOCK = 128


def _conv_taps(gp, halo, first_block):
    halo = jnp.where(first_block, 0.0, halo.astype(F32))
    rid = lax.broadcasted_iota(jnp.int32, gp.shape, 0)
    last, prev = halo[HALO - 1:HALO, :], halo[HALO - 2:HALO - 1, :]
    g1 = jnp.where(rid == 0, last, pltpu.roll(gp, 1, axis=0))
    g2 = jnp.where(rid == 0, prev, jnp.where(rid == 1, last, pltpu.roll(gp, 2, axis=0)))
    return g1, g2


def _prev_halo(br, c):
    return pl.BlockSpec((HALO, c), lambda i: (jnp.maximum(i * (br // HALO) - 1, 0), 0))


def _glu_fwd(up, conv_w, conv_b):
    r = up.shape[0]
    c = D_FF
    br = ROW_BLOCK

    def body(gp_ref, halo_ref, val_ref, w_ref, b_ref, o_ref):
        gp = gp_ref[...].astype(F32)
        g1, g2 = _conv_taps(gp, halo_ref[...], pl.program_id(0) == 0)
        gate = w_ref[0:1, :] * g2 + w_ref[1:2, :] * g1 + w_ref[2:3, :] * gp + b_ref[...]
        o_ref[...] = (gate * jax.nn.sigmoid(gate) * val_ref[...].astype(F32)).astype(o_ref.dtype)

    return _call("glu_fwd", body, (r // br,),
                 [(up, _rows(br, c, 0)), (up, _prev_halo(br, c)), (up, _rows(br, c, 1)),
                  (conv_w, _whole((3, c))), (conv_b.reshape(1, c), _whole((1, c)))],
                 [(_sds((r, c), MXU_DTYPE), _rows(br, c))], sem=("parallel",))[0]


def _glu_bwd(up, conv_w, conv_b, d_act):
    r = up.shape[0]
    c = D_FF
    br = GLU_BWD_BLOCK
    nb = r // br

    def body(gp_ref, halo_ref, val_ref, da_ref, gpn_ref, valn_ref, dan_ref, w_ref, b_ref, o_ref, dw_ref, db_ref):
        i = pl.program_id(0)
        w0, w1, w2, bias = w_ref[0:1, :], w_ref[1:2, :], w_ref[2:3, :], b_ref[...]

        def d_gate(gp, g1, g2, val, da):
            gate = w0 * g2 + w1 * g1 + w2 * gp + bias
            sg = jax.nn.sigmoid(gate)
            return da * val * (sg * (1.0 + gate * (1.0 - sg))), da * (gate * sg)

        gp = gp_ref[...].astype(F32)
        g1, g2 = _conv_taps(gp, halo_ref[...], i == 0)
        dg, dv = d_gate(gp, g1, g2, val_ref[...].astype(F32), da_ref[...].astype(F32))
        gpn = gpn_ref[...].astype(F32)
        g1n, g2n = _conv_taps(gpn, gp[br - HALO:, :], False)
        dgn, _ = d_gate(gpn, g1n, g2n, valn_ref[...].astype(F32), dan_ref[...].astype(F32))
        dgn = jnp.where(i == nb - 1, 0.0, dgn)
        rid = lax.broadcasted_iota(jnp.int32, dg.shape, 0)
        u1 = jnp.where(rid == br - 1, dgn[0:1, :], pltpu.roll(dg, br - 1, axis=0))
        u2 = jnp.where(rid == br - 1, dgn[1:2, :], jnp.where(rid == br - 2, dgn[0:1, :], pltpu.roll(dg, br - 2, axis=0)))
        o_ref[:, 0:c] = (w2 * dg + w1 * u1 + w0 * u2).astype(o_ref.dtype)
        o_ref[:, c:2 * c] = dv.astype(o_ref.dtype)

        @pl.when(i == 0)
        def _():
            dw_ref[...] = jnp.zeros_like(dw_ref)
            db_ref[...] = jnp.zeros_like(db_ref)

        dw_ref[0:1, :] += jnp.sum(dg * g2, axis=0, keepdims=True)
        dw_ref[1:2, :] += jnp.sum(dg * g1, axis=0, keepdims=True)
        dw_ref[2:3, :] += jnp.sum(dg * gp, axis=0, keepdims=True)
        db_ref[...] += jnp.sum(dg, axis=0, keepdims=True)

    nxt = lambda cb: pl.BlockSpec((HALO, c), lambda i: (jnp.minimum((i + 1) * (br // HALO), r // HALO - 1), cb))
    return _call("glu_bwd", body, (nb,),
                 [(up, _rows(br, c, 0)), (up, _prev_halo(br, c)), (up, _rows(br, c, 1)), (d_act, _rows(br, c)),
                  (up, nxt(0)), (up, nxt(1)), (d_act, nxt(0)),
                  (conv_w, _whole((3, c))), (conv_b.reshape(1, c), _whole((1, c)))],
                 [(_sds((r, 2 * c), MXU_DTYPE), _rows(br, 2 * c)),
                  (_sds((8, c), F32), _whole((8, c))), (_sds((1, c), F32), _whole((1, c)))],
                 sem=("arbitrary",))


def _token_specs(seq, d):
    br = ROW_BLOCK
    nxb = seq // br
    main = pl.BlockSpec((br, d), lambda i: (jnp.minimum(i, nxb - 1), 0))
    tail = pl.BlockSpec((N_META, d), lambda i: (jnp.clip(i * (br // N_META) - 1, 0, seq // N_META - 1), 0))
    return main, tail


def _padded_block(main_ref, tail_ref, first, seq):
    br = ROW_BLOCK
    i = pl.program_id(0)
    nxb = seq // br
    main = jnp.where(i < nxb, main_ref[...], 0.0)
    head = jnp.where(i == 0, first, jnp.where(i <= nxb, tail_ref[...], 0.0))
    return jnp.concatenate([head, main[:br - N_META]], axis=0)


def _ln_emb_fwd(x, meta, g, b, rows, after=()):
    seq, d = x.shape
    br = ROW_BLOCK
    assert seq % br == 0 and br % N_META == 0 and rows % br == 0

    def body(x_ref, tail_ref, meta_ref, g_ref, b_ref, y_ref, yb_ref):
        z = _padded_block(x_ref, tail_ref, meta_ref[...], seq)
        xhat, _ = _ln_stats(z)
        y = xhat * g_ref[...] + b_ref[...]
        y_ref[...] = y
        yb_ref[...] = y.astype(yb_ref.dtype)

    main, tail = _token_specs(seq, d)
    return _call("ln_emb_fwd", body, (rows // br,),
                 [(x, main), (x, tail), (meta, _whole((N_META, d))), (g.reshape(1, d), _whole((1, d))),
                  (b.reshape(1, d), _whole((1, d)))],
                 [(_sds((rows, d), F32), _rows(br, d)), (_sds((rows, d), MXU_DTYPE), _rows(br, d))],
                 sem=("parallel",), after=after)


def _ln_emb_bwd(x, meta, dh0, g):
    seq, d = x.shape
    br = ROW_BLOCK
    step = br // N_META

    def ln_bwd(z, dy, gv):
        xhat, rstd = _ln_stats(z)
        dyg = dy * gv
        m1 = jnp.mean(dyg, axis=-1, keepdims=True)
        m2 = jnp.mean(dyg * xhat, axis=-1, keepdims=True)
        dz = rstd * (dyg - m1 - xhat * m2)
        return dz, jnp.sum(dy * xhat, axis=0, keepdims=True), jnp.sum(dy, axis=0, keepdims=True)

    def body(x_ref, dh_ref, nxt_ref, meta_ref, top_ref, g_ref, dx_ref, dm_ref, dg_ref, db_ref):
        gv = g_ref[...]
        dy = jnp.concatenate([dh_ref[N_META:, :], nxt_ref[...]], axis=0)
        dz, dg, db = ln_bwd(x_ref[...], dy, gv)
        dx_ref[...] = dz

        @pl.when(pl.program_id(0) == 0)
        def _():
            dzm, dgm, dbm = ln_bwd(meta_ref[...], top_ref[...], gv)
            dm_ref[...] = dzm
            dg_ref[...] = dgm
            db_ref[...] = dbm

        dg_ref[...] += dg
        db_ref[...] += db

    small = _whole((N_META, d))
    return _call("ln_emb_bwd", body, (seq // br,),
                 [(x, _rows(br, d)), (dh0, _rows(br, d)), (dh0, pl.BlockSpec((N_META, d), lambda i: ((i + 1) * step, 0))),
                  (meta, small), (dh0, small), (g.reshape(1, d), _whole((1, d)))],
                 [(_sds((seq, d), F32), _rows(br, d)), (_sds((N_META, d), F32), small),
                  (_sds((1, d), F32), _whole((1, d))), (_sds((1, d), F32), _whole((1, d)))], sem=("arbitrary",))


def _ln_ffn_loss(h1, f, tgt, g, b):
    r, d = h1.shape
    seq = tgt.shape[0]
    br = ROW_BLOCK

    def body(a_ref, r_ref, t_ref, tail_ref, g_ref, b_ref, l_ref):
        err = _loss_err(a_ref, r_ref, t_ref, tail_ref, g_ref, b_ref, seq)[0]

        @pl.when(pl.program_id(0) == 0)
        def _():
            l_ref[...] = jnp.zeros_like(l_ref)

        l_ref[...] += jnp.sum(jnp.sum(err * err, axis=1, keepdims=True), axis=0, keepdims=True) * (0.5 / d)

    main, tail = _token_specs(seq, d)
    return _call("ln_ffn_loss", body, (r // br,),
                 [(h1, _rows(br, d)), (f, _rows(br, d)), (tgt, main), (tgt, tail),
                  (g.reshape(1, d), _whole((1, d))), (b.reshape(1, d), _whole((1, d)))],
                 [(_sds((1, 1), F32), _whole((1, 1)))], sem=("arbitrary",))[0]


def _loss_err(a_ref, r_ref, t_ref, tail_ref, g_ref, b_ref, seq):
    br, d = a_ref.shape
    xhat, rstd = _ln_stats(ALPHA * a_ref[...] + r_ref[...])
    y = xhat * g_ref[...] + b_ref[...]
    t = _padded_block(t_ref, tail_ref, jnp.zeros((N_META, d), F32), seq)
    rid = lax.broadcasted_iota(jnp.int32, (br, d), 0) + pl.program_id(0) * br
    valid = (rid >= N_META) & (rid < N_META + seq)
    return jnp.where(valid, y - t, 0.0), xhat, rstd


def _ln_ffn_bwd(h1, f, tgt, g, b):
    r, d = h1.shape
    seq = tgt.shape[0]
    br = ROW_BLOCK

    def body(a_ref, r_ref, t_ref, tail_ref, g_ref, b_ref, dz_ref, dzb_ref, dg_ref, db_ref):
        err, xhat, rstd = _loss_err(a_ref, r_ref, t_ref, tail_ref, g_ref, b_ref, seq)
        dyv = err * (1.0 / d)
        dyg = dyv * g_ref[...]
        m1 = jnp.mean(dyg, axis=-1, keepdims=True)
        m2 = jnp.mean(dyg * xhat, axis=-1, keepdims=True)
        dz = rstd * (dyg - m1 - xhat * m2)
        dz_ref[...] = dz
        dzb_ref[...] = dz.astype(dzb_ref.dtype)

        @pl.when(pl.program_id(0) == 0)
        def _():
            dg_ref[...] = jnp.zeros_like(dg_ref)
            db_ref[...] = jnp.zeros_like(db_ref)

        dg_ref[...] += jnp.sum(dyv * xhat, axis=0, keepdims=True)
        db_ref[...] += jnp.sum(dyv, axis=0, keepdims=True)

    main, tail = _token_specs(seq, d)
    return _call("ln_ffn_bwd", body, (r // br,),
                 [(h1, _rows(br, d)), (f, _rows(br, d)), (tgt, main), (tgt, tail),
                  (g.reshape(1, d), _whole((1, d))), (b.reshape(1, d), _whole((1, d)))],
                 [(_sds((r, d), F32), _rows(br, d)), (_sds((r, d), MXU_DTYPE), _rows(br, d)),
                  (_sds((1, d), F32), _whole((1, d))), (_sds((1, d), F32), _whole((1, d)))], sem=("arbitrary",))


def _attn_fwd(name, q, k, v, cum_b=None, cum_t=None):
    (qa, qg), (ka, kg), (va, vg) = q, k, v
    r = qa.shape[0]
    tq, tk = ATT_TQ, ATT_TK
    nq, nk = r // tq, r // tk
    bias = cum_b is not None

    def body(*refs):
        if bias:
            q_ref, k_ref, vt_ref, cb_ref, ct_ref, o_ref, ob_ref, lse_ref = refs
        else:
            q_ref, k_ref, vt_ref, o_ref, ob_ref, lse_ref = refs
        i = pl.program_id(1)
        qs = [q_ref[:, _hs(hh)] for hh in range(hg)]
        cqs = [ct_ref[hh] for hh in range(hg)] if bias else None
        diff = lax.broadcasted_iota(jnp.int32, (tk, tq), 0) - lax.broadcasted_iota(jnp.int32, (tk, tq), 1)

        def step(j, carry, masked):
            keys = pl.ds(pl.multiple_of(j * tk, tk), tk)
            out = []
            for hh in range(hg):
                m, l, acc = carry[hh]
                kt = k_ref[keys, _hs(hh)]
                s = lax.dot_general(kt, qs[hh], NT, preferred_element_type=F32)
                if bias:
                    s = s + (cqs[hh] - cb_ref[keys, hh * HP:hh * HP + 1])
                if masked:
                    s = jnp.where(diff <= i * tq - j * tk, s, NEG_INF)
                m_new = jnp.maximum(m, jnp.max(s, axis=0, keepdims=True))
                p = jnp.exp(s - m_new)
                a = jnp.exp(m - m_new)
                l = a * l + jnp.sum(p, axis=0, keepdims=True)
                acc = a * acc + jnp.dot(vt_ref[j, _hs(hh), :], p.astype(kt.dtype), preferred_element_type=F32)
                out.append((m_new, l, acc))
            return tuple(out)

        n_clear = (i * tq + 1) // tk
        n_all = ((i + 1) * tq - 1) // tk + 1
        carry = tuple((jnp.full((1, tq), NEG_INF, F32), jnp.zeros((1, tq), F32), jnp.zeros((HP, tq), F32))
                      for _ in range(hg))
        carry = lax.fori_loop(0, n_clear, lambda j, c: step(j, c, False), carry)
        carry = lax.fori_loop(n_clear, n_all, lambda j, c: step(j, c, True), carry)
        for hh in range(hg):
            m, l, acc = carry[hh]
            o = (acc / l).T
            o_ref[:, _hs(hh)] = o
            ob_ref[:, _hs(hh)] = o.astype(ob_ref.dtype)
            lse_ref[hh] = m + jnp.log(l)

    hg = ATT_HEADS
    w = hg * HP
    gpw = HW // w
    tile = lambda g: pl.BlockSpec((tq, w), lambda h, i: (i, g * gpw + h))
    res = lambda g: pl.BlockSpec((r, w), lambda h, i: (0, g * gpw + h))
    v_t = _key_tiles_transposed(name + "_vt", va, vg)
    ins = [(qa, tile(qg)), (ka, res(kg)), (v_t, pl.BlockSpec((nk, w, tk), lambda h, i: (0, h, 0)))]
    if bias:
        ins += [(cum_b, res(0)),
                (cum_t.reshape(HEADS, nq, 1, tq), pl.BlockSpec((hg, None, 1, tq), lambda h, i: (h, i, 0, 0)))]
    outs = [(_sds((r, HW), F32), tile(0)), (_sds((r, HW), MXU_DTYPE), tile(0)),
            (_sds((HEADS, nq, 1, tq), F32), pl.BlockSpec((hg, None, 1, tq), lambda h, i: (h, i, 0, 0)))]
    o, ob, lse = _call(name, body, (gpw, nq), ins, outs, sem=("parallel", "parallel"))
    return o, ob, lse.reshape(HEADS, r)


def _key_tiles_transposed(name, a, group):
    r = a.shape[0]
    tk = ATT_TK

    def body(x_ref, o_ref):
        for h in range(HEADS):
            o_ref[_hs(h), :] = x_ref[:, _hs(h)].astype(F32).T.astype(o_ref.dtype)

    return _call(name, body, (r // tk,),
                 [(a, pl.BlockSpec((tk, HW), lambda j: (j, group)))],
                 [(_sds((r // tk, HW, tk), a.dtype), pl.BlockSpec((None, HW, tk), lambda j: (j, 0, 0)))],
                 sem=("parallel",))[0]


def _attn_delta(name, do_b, o, after=()):
    r = do_b.shape[0]
    br = ROW_BLOCK

    def body(do_ref, o_ref, d_ref):
        lane = _lane_iota((br, HP))
        d = jnp.zeros((br, HP), F32)
        for h in range(HEADS):
            dh = do_ref[:, _hs(h)].astype(F32)
            d = jnp.where(lane == h, jnp.sum(dh * o_ref[:, _hs(h)], axis=1, keepdims=True), d)
        d_ref[...] = d.T[0:HEADS, :]

    wide = _rows(br, HW)
    return _call(name, body, (r // br,), [(do_b, wide), (o, wide)],
                 [(_sds((HEADS, r), F32), pl.BlockSpec((HEADS, br), lambda i: (0, i)))],
                 sem=("parallel",), after=after)[0]


def _attn_bwd(name, q, k, v, do_b, lse_t, delta_t, cum_b=None, cum_t=None, out_dtype=F32):
    (qa, qg), (ka, kg), (va, vg) = q, k, v
    r = qa.shape[0]
    tq, tk = ATT_TQ, ATT_TK
    nq, nk = r // tq, r // tk
    bias = cum_b is not None

    def body(*refs):
        if bias:
            (q_ref, k_ref, v_ref, do_ref, lse_ref, dl_ref, cb_ref, ct_ref,
             dq_ref, dk_ref, dv_ref, dcq_ref, dck_ref, dqt_ref) = refs
        else:
            q_ref, k_ref, v_ref, do_ref, lse_ref, dl_ref, dq_ref, dk_ref, dv_ref, dqt_ref = refs
        j = pl.program_id(1)

        @pl.when(j == 0)
        def _():
            dqt_ref[...] = jnp.zeros_like(dqt_ref)
            if bias:
                dcq_ref[...] = jnp.zeros_like(dcq_ref)

        kts = [k_ref[:, _hs(hh)] for hh in range(hg)]
        vts = [v_ref[:, _hs(hh)] for hh in range(hg)]
        k_trs = [kt.astype(F32).T.astype(kt.dtype) for kt in kts]
        cks = [cb_ref[:, hh * HP:hh * HP + 1] for hh in range(hg)] if bias else None
        diff = lax.broadcasted_iota(jnp.int32, (tk, tq), 0) - lax.broadcasted_iota(jnp.int32, (tk, tq), 1)

        def step(i, carry, masked):
            rows = pl.ds(pl.multiple_of(i * tq, tq), tq)
            out = []
            for hh in range(hg):
                dk_acc, dv_acc, dck_acc = carry[hh]
                qt = q_ref[rows, _hs(hh)]
                dot = do_ref[rows, _hs(hh)]
                s = lax.dot_general(kts[hh], qt, NT, preferred_element_type=F32)
                if bias:
                    s = s + (ct_ref[hh, i] - cks[hh])
                if masked:
                    s = jnp.where(diff <= i * tq - j * tk, s, NEG_INF)
                p = jnp.exp(s - lse_ref[hh, i])
                dp = lax.dot_general(vts[hh], dot, NT, preferred_element_type=F32)
                ds = p * (dp - dl_ref[hh, i])
                pb = p.astype(dot.dtype)
                dsb = ds.astype(qt.dtype)
                dv_acc = dv_acc + jnp.dot(pb, dot, preferred_element_type=F32)
                dk_acc = dk_acc + jnp.dot(dsb, qt, preferred_element_type=F32)
                dqt_ref[hh, i] += jnp.dot(k_trs[hh], dsb, preferred_element_type=F32)
                if bias:
                    dcq_ref[hh, i] += jnp.sum(ds, axis=0, keepdims=True)
                    dck_acc = dck_acc - jnp.sum(ds, axis=1, keepdims=True)
                out.append((dk_acc, dv_acc, dck_acc))
            return tuple(out)

        i_first = (j * tk) // tq
        i_clear = jnp.minimum(((j + 1) * tk + tq - 2) // tq, nq)
        carry = tuple((jnp.zeros((tk, HP), F32), jnp.zeros((tk, HP), F32), jnp.zeros((tk, 1), F32)) for _ in range(hg))
        carry = lax.fori_loop(i_first, i_clear, lambda i, c: step(i, c, True), carry)
        carry = lax.fori_loop(i_clear, nq, lambda i, c: step(i, c, False), carry)
        for hh in range(hg):
            dk_acc, dv_acc, dck_acc = carry[hh]
            dk_ref[:, _hs(hh)] = dk_acc.astype(dk_ref.dtype)
            dv_ref[:, _hs(hh)] = dv_acc.astype(dv_ref.dtype)
            if bias:
                dck_ref[:, _hs(hh)] = jnp.broadcast_to(dck_acc, (tk, HP))

        @pl.when(j == nk - 1)
        def _():
            for hh in range(hg):
                for i in range(nq):
                    dq_ref[i * tq:(i + 1) * tq, _hs(hh)] = dqt_ref[hh, i].T.astype(dq_ref.dtype)

    hg = ATT_HEADS
    w = hg * HP
    gpw = HW // w
    res = lambda g: pl.BlockSpec((r, w), lambda h, j: (0, g * gpw + h))
    tile = lambda g: pl.BlockSpec((tk, w), lambda h, j: (j, g * gpw + h))
    rowv = pl.BlockSpec((hg, nq, 1, tq), lambda h, j: (h, 0, 0, 0))
    as_rows = lambda a: a.reshape(HEADS, nq, 1, tq)
    ins = [(qa, res(qg)), (ka, tile(kg)), (va, tile(vg)), (do_b, res(0)), (as_rows(lse_t), rowv), (as_rows(delta_t), rowv)]
    outs = [(_sds((r, HW), out_dtype), res(0)), (_sds((r, HW), out_dtype), tile(0)), (_sds((r, HW), out_dtype), tile(0))]
    if bias:
        ins += [(cum_b, tile(0)), (as_rows(cum_t), rowv)]
        outs += [(_sds((HEADS, nq, 1, tq), F32), rowv), (_sds((r, HW), F32), tile(0))]
    res_out = _call(name, body, (gpw, nk), ins, outs, scratch=[pltpu.VMEM((hg, nq, HP, tq), F32)],
                    sem=("parallel", "arbitrary"))
    if bias:
        dq, dk, dv, dcq, dck = res_out
        return dq, dk, dv, dcq.reshape(HEADS, r), dck
    return res_out


MESH_ID = pl.DeviceIdType.MESH
ANY = pl.BlockSpec(memory_space=pl.ANY)


N_GATHER_COPIES = 8


def _allgather(name, shards):
    n = len(shards)

    def body(*refs):
        x_refs, out_refs = refs[:n], refs[n:2 * n]
        send_sems, recv_sems, local_sems = refs[2 * n:]
        x, y, c = lax.axis_index("x"), lax.axis_index("y"), lax.axis_index("c")
        me, sibling = (x, y, c), (x, y, 1 - c)
        xn, yn, dg = (1 - x, y, c), (x, 1 - y, c), (1 - x, 1 - y, c)
        other = lambda dev: (dev[0], dev[1], 1 - c)

        def slot(ti, dev, half=None):
            ref = out_refs[ti].at[4 * dev[0] + 2 * dev[1] + dev[2]]
            if half is None:
                return ref
            rows = shards[ti].shape[0] // 2
            return ref.at[pl.ds(half * rows, rows)]

        def copy(ti, k, block, to, half=None, src=None):
            return pltpu.make_async_remote_copy(
                src_ref=slot(ti, block, half) if src is None else src, dst_ref=slot(ti, block, half),
                send_sem=send_sems.at[ti, k], recv_sem=recv_sems.at[ti, k], device_id=to, device_id_type=MESH_ID)

        mine = [pltpu.make_async_copy(x_refs[ti], slot(ti, me), local_sems.at[ti]) for ti in range(n)]
        for cp in mine:
            cp.start()
        started = []

        def go(cp):
            cp.start()
            started.append(cp)

        for ti in range(n):
            go(copy(ti, 0, me, sibling, src=x_refs[ti]))
            go(copy(ti, 1, me, xn, src=x_refs[ti]))
            go(copy(ti, 2, me, yn, src=x_refs[ti]))
        for ti in range(n):
            copy(ti, 1, xn, me).wait_recv()
            go(copy(ti, 3, xn, yn, half=0))
            go(copy(ti, 5, xn, sibling))
            copy(ti, 2, yn, me).wait_recv()
            go(copy(ti, 4, yn, xn, half=1))
            go(copy(ti, 6, yn, sibling))
        for ti in range(n):
            copy(ti, 3, dg, me, half=0).wait_recv()
            copy(ti, 4, dg, me, half=1).wait_recv()
            go(copy(ti, 7, dg, sibling))
        for ti in range(n):
            copy(ti, 0, sibling, me).wait_recv()
            for k, dev in ((5, xn), (6, yn), (7, dg)):
                copy(ti, k, other(dev), me).wait_recv()
        for cp in started:
            cp.wait_send()
        for cp in mine:
            cp.wait()

    sems = pltpu.SemaphoreType.DMA((n, N_GATHER_COPIES))
    return pl.pallas_call(
        body, name=name, out_shape=[_sds((N_DEV,) + s.shape, s.dtype) for s in shards],
        in_specs=[ANY] * n, out_specs=[ANY] * n,
        scratch_shapes=[sems, sems, pltpu.SemaphoreType.DMA((n,))],
    )(*shards)


HBM = pl.BlockSpec(memory_space=pltpu.HBM)
SEM = pl.BlockSpec(memory_space=pltpu.SEMAPHORE)
EFFECT = pltpu.SideEffectType.DATAFLOW_SIDE_EFFECTING
N_PEER = N_DEV - 1


def _my_id():
    return 4 * lax.axis_index("x") + 2 * lax.axis_index("y") + lax.axis_index("c")


def _peers():
    x, y, c = lax.axis_index("x"), lax.axis_index("y"), lax.axis_index("c")
    out = []
    for k in range(1, N_DEV):
        px, py, pc = (1 - x if k & 4 else x, 1 - y if k & 2 else y, 1 - c if k & 1 else c)
        out.append(((px, py, pc), 4 * px + 2 * py + pc))
    return out


def _push_copies(src_refs, land_refs, send_sems, recv_sems, scatter, landing):
    me = _my_id()
    out = []
    for ti, (src, land) in enumerate(zip(src_refs, land_refs)):
        for k, (dev, pid) in enumerate(_peers()):
            out.append(pltpu.make_async_remote_copy(
                src_ref=src.at[pid] if scatter else src, dst_ref=land.at[pid if landing else me],
                send_sem=send_sems.at[ti * N_PEER + k], recv_sem=recv_sems.at[ti * N_PEER + k],
                device_id=dev, device_id_type=MESH_ID))
    return out


def _push_start(name, groups, scatter, after=None):
    sizes = [len(g) for g in groups]
    srcs = [a for g in groups for a in g]
    n = len(srcs)
    slot = lambda s: s.shape[1:] if scatter else s.shape
    lands = [lax.empty((N_DEV,) + slot(s), s.dtype) for s in srcs]
    n_after = 0 if after is None else 1
    n_grp = len(groups)

    def body(*refs):
        src_refs, land_refs = refs[:n], refs[n:2 * n]
        sems = refs[2 * n + n_after:2 * n + n_after + 2 * n_grp]
        token = refs[-1]
        lo = 0
        for gi, sz in enumerate(sizes):
            for cp in _push_copies(src_refs[lo:lo + sz], land_refs[lo:lo + sz], sems[2 * gi], sems[2 * gi + 1], scatter, False):
                cp.start()
            lo += sz
        token[...] = jnp.zeros_like(token)

    hbm = lambda a: pltpu.with_memory_space_constraint(a, pltpu.HBM)
    operands = [hbm(a) for a in srcs + lands] + ([after] if n_after else [])
    sem_shapes = [pltpu.SemaphoreType.DMA((sz * N_PEER,)) for sz in sizes for _ in range(2)]
    res = pl.pallas_call(
        body, name=name,
        out_shape=sem_shapes + [pltpu.HBM(a.shape, a.dtype) for a in srcs + lands] + [_sds((8, 128), F32)],
        in_specs=[HBM] * (2 * n) + [ANY] * n_after,
        out_specs=[SEM] * (2 * n_grp) + [HBM] * (2 * n) + [pl.BlockSpec(memory_space=pltpu.VMEM)],
        input_output_aliases={i: 2 * n_grp + i for i in range(2 * n)},
        compiler_params=pltpu.CompilerParams(has_side_effects=EFFECT),
    )(*operands)
    thru = res[2 * n_grp:2 * n_grp + 2 * n]
    handles, lo = [], 0
    for gi, sz in enumerate(sizes):
        handles.append((res[2 * gi], res[2 * gi + 1], list(thru[lo:lo + sz]), list(thru[n + lo:n + lo + sz]), scatter))
        lo += sz
    return handles, res[-1]


def _push_wait(name, handle, after):
    send_sems, recv_sems, srcs, lands, scatter = handle
    n = len(srcs)

    def body(*refs):
        src_refs, land_refs = refs[:n], refs[n:2 * n]
        s_sems, r_sems = refs[2 * n], refs[2 * n + 1]
        for cp in _push_copies(src_refs, land_refs, s_sems, r_sems, scatter, True):
            cp.wait_send()
            cp.wait_recv()

    res = pl.pallas_call(
        body, name=name,
        out_shape=[pltpu.HBM(a.shape, a.dtype) for a in srcs + lands],
        in_specs=[HBM] * (2 * n) + [SEM, SEM, ANY], out_specs=[HBM] * (2 * n),
        input_output_aliases={i: i for i in range(2 * n)},
        compiler_params=pltpu.CompilerParams(has_side_effects=EFFECT),
    )(*srcs, *lands, send_sems, recv_sems, after)
    return list(res[n:])


def _adamw(name, parts, w, m, v, own=None):
    r, c = w.shape
    br = _pick(r, 256, 16)
    has_own = own is not None

    def body(*refs):
        if has_own:
            p_ref, own_ref, w_ref, m_ref, v_ref, g_ref, d_ref, nm_ref, nv_ref = refs
            me = _my_id()
            mine = own_ref[...].astype(F32)
        else:
            p_ref, w_ref, m_ref, v_ref, g_ref, d_ref, nm_ref, nv_ref = refs
        g = None
        for k in range(N_DEV):
            t = p_ref[k].astype(F32)
            if has_own:
                t = jnp.where(me == k, mine, t)
            g = t if g is None else g + t
        mm = ADAM_B1 * m_ref[...] + (1.0 - ADAM_B1) * g
        vv = ADAM_B2 * v_ref[...] + (1.0 - ADAM_B2) * (g * g)
        m_hat = mm / (1.0 - ADAM_B1 ** ADAM_STEP)
        v_hat = vv / (1.0 - ADAM_B2 ** ADAM_STEP)
        g_ref[...] = g
        d_ref[...] = -ADAM_LR * (m_hat / (jnp.sqrt(v_hat) + ADAM_EPS) + ADAM_WD * w_ref[...])
        nm_ref[...] = mm
        nv_ref[...] = vv

    spec = _rows(br, c)
    out = (_sds((r, c), F32), spec)
    ins = [(parts, pl.BlockSpec((N_DEV, br, c), lambda i: (0, i, 0)))] + ([(own, spec)] if has_own else [])
    return _call(name, body, (r // br,), ins + [(w, spec), (m, spec), (v, spec)], [out] * 4, sem=("parallel",))


def _pad_head_cols(w, d):
    k = w.shape[0]
    return jnp.pad(w.reshape(k, HEADS, d), ((0, 0), (0, 0), (0, HP - d))).reshape(k, HW)


def _unpad_head_cols(wp, d):
    k = wp.shape[0]
    return wp.reshape(k, HEADS, HP)[:, :, :d].reshape(k, HEADS * d)


def _pad_head_rows(w, d):
    n = w.shape[1]
    return jnp.pad(w.reshape(HEADS, d, n), ((0, 0), (0, HP - d), (0, 0))).reshape(HW, n)


def _unpad_head_rows(wp, d):
    n = wp.shape[1]
    return wp.reshape(HEADS, HP, n)[:, :d, :].reshape(HEADS * d, n)


def _w_in_runs():
    nat = {}
    o = 0
    for nm, wd in (("q", Q_RANK), ("kv", KV_RANK), ("kr", ROPE), ("fq", FOX_W), ("fk", FOX_W), ("fv", FOX_W),
                   ("fl", HEADS), ("gate", 2 * D_MODEL)):
        nat[nm] = o
        o += wd
    runs = [(1, R_QLAT, nat["q"], Q_RANK, 1.0), (1, R_KVLAT, nat["kv"], KV_RANK, 1.0),
            (1, R_LAST + LANE_FL, nat["fl"], HEADS, 1.0), (1, R_LAST + LANE_PE, nat["kr"], ROPE, 1.0),
            (1, R_GATE, nat["gate"], 2 * D_MODEL, 1.0)]
    for grp, (nm, sc) in enumerate((("fq", FOX_SCALE), ("fk", 1.0), ("fv", 1.0))):
        runs += [(0, grp * HW + h * HP, nat[nm] + h * FOX_DIM, FOX_DIM, sc) for h in range(HEADS)]
    return runs


def _sharded_runs(runs, shard_cols):
    out = []
    for half, col, ncol, width, sc in runs:
        while width > 0:
            d, local = divmod(ncol, shard_cols)
            wd = min(width, shard_cols - local)
            out.append((half, col, d, local, wd, sc))
            col, ncol, width = col + wd, ncol + wd, width - wd
    return out


def _remap(name, srcs, out_shapes, moves):
    rows = srcs[0].shape[-2]
    br = _pick(rows, 256, 16)
    ns = len(srcs)

    def spec(shape):
        if len(shape) == 2:
            return pl.BlockSpec((br, shape[1]), lambda i: (i, 0))
        return pl.BlockSpec((shape[0], br, shape[2]), lambda i: (0, i, 0))

    def body(*refs):
        s_refs, o_refs = refs[:ns], refs[ns:]
        for o in o_refs:
            o[...] = jnp.zeros_like(o)
        for di, dl, dc, si, sl, sc0, wd, scale in moves:
            v = s_refs[si][:, sc0:sc0 + wd] if sl is None else s_refs[si][sl, :, sc0:sc0 + wd]
            if scale != 1.0:
                v = v * jnp.asarray(scale, v.dtype)
            v = v.astype(o_refs[di].dtype)
            if dl is None:
                o_refs[di][:, dc:dc + wd] = v
            else:
                o_refs[di][dl, :, dc:dc + wd] = v

    return _call(name, body, (rows // br,), [(a, spec(a.shape)) for a in srcs],
                 [(_sds(shape, dt), spec(shape)) for shape, dt in out_shapes], sem=("parallel",))


def _w_in_from_shards(g3):
    n, rows, c = g3.shape
    moves = [(half, None, col, 0, d, local, wd, sc) for half, col, d, local, wd, sc in _sharded_runs(_w_in_runs(), c)]
    return _remap("w_in_repack", [g3], [((rows, F_W), g3.dtype), ((rows, R_W), g3.dtype)], moves)


def _w_in_grad_to_shards(d_fused, d_rest, n, c):
    rows = d_fused.shape[0]
    moves = [(0, d, local, half, None, col, wd, sc) for half, col, d, local, wd, sc in _sharded_runs(_w_in_runs(), c)]
    return _remap("w_in_grad_unpack", [d_fused, d_rest], [((n, rows, c), d_fused.dtype)], moves)[0]


def _rows_from_shards(name, land, own):
    n, rows, c = land.shape

    def body(land_ref, own_ref, o_ref):
        o_ref[...] = jnp.where(_my_id() == pl.program_id(0), own_ref[...], land_ref[...])

    return _call(name, body, (n,),
                 [(land, pl.BlockSpec((None, rows, c), lambda d: (d, 0, 0))), (own, _whole((rows, c)))],
                 [(_sds((n * rows, c), land.dtype), pl.BlockSpec((rows, c), lambda d: (d, 0)))], sem=("parallel",))[0]


def _cols_from_shards(name, land, own):
    n, rows, c = land.shape
    br = _pick(rows, 256, 16)

    def body(land_ref, own_ref, o_ref):
        me = _my_id()
        for d in range(n):
            o_ref[:, c * d:c * (d + 1)] = jnp.where(me == d, own_ref[...], land_ref[d])

    return _call(name, body, (rows // br,),
                 [(land, pl.BlockSpec((n, br, c), lambda i: (0, i, 0))), (own, _rows(br, c))],
                 [(_sds((rows, n * c), land.dtype), _rows(br, n * c))], sem=("parallel",))[0]


def _cols_to_shards(name, full, n):
    rows, nc = full.shape
    c = nc // n
    return _remap(name, [full], [((n, rows, c), full.dtype)], [(0, d, 0, 0, None, c * d, c, 1.0) for d in range(n)])[0]


def _split_w_kv(w):
    k = w.shape[0]
    w3 = w.reshape(k, HEADS, NOPE + V_DIM)
    padl = lambda a: jnp.pad(a, ((0, 0), (0, 0), (0, HP - a.shape[-1]))).reshape(k, HW)
    return padl(w3[..., :NOPE]), padl(w3[..., NOPE:])


def _merge_w_kv(wk, wv):
    k = wk.shape[0]
    return jnp.concatenate([wk.reshape(k, HEADS, HP)[..., :NOPE], wv.reshape(k, HEADS, HP)[..., :V_DIM]],
                           axis=-1).reshape(k, HEADS * (NOPE + V_DIM))


class _NoComm:
    first_token = ()

    def late_weights(self, group, after):
        return {}

    def send(self, name, grads):
        return ()


def _local_step(x, tgt, p, comm=_NoComm()):
    seq = x.shape[0]
    r = -(-(N_META + seq) // ROW_ALIGN) * ROW_ALIGN
    cd = MXU_DTYPE
    p = dict(p)

    w_f, w_r = p["w_in"]

    pos = jnp.arange(r, dtype=F32)
    inv_freq = ROPE_THETA ** (-jnp.arange(HALF, dtype=F32) / HALF)
    ang = pos[:, None] * inv_freq[None, :]
    cos_t = jnp.tile(jnp.cos(ang), (1, HP // HALF))
    sin_t = jnp.tile(jnp.sin(ang), (1, HP // HALF))
    bf_row = jnp.zeros((1, HP), F32).at[0, LANE_FL:LANE_FL + HEADS].set(p["b_forget"])

    h0, h0b = _ln_emb_fwd(x, p["meta_tokens"], p["ln_emb_g"], p["ln_emb_b"], r, after=comm.first_token)
    proj_f = _matmul("in_proj_f", h0b, w_f, out_dtype=cd)
    proj_r = _matmul("in_proj_r", h0b, w_r)
    latent_gains = (p["q_norm_g"], p["kv_norm_g"])
    ql, kvl = _latent_norm_fwd(proj_r, latent_gains)
    p.update(comm.late_weights("qkv", ql))
    w_q = _pad_head_cols(p["w_q_up"], QK_DIM)
    w_kv = jnp.concatenate(_split_w_kv(p["w_kv_up"]), axis=1)
    q_raw = _matmul("q_up", ql, w_q)
    kv = _matmul("kv_up", kvl, w_kv, out_dtype=cd)
    q_mla, k_mla = _rope_fwd(q_raw, kv, proj_r, cos_t, sin_t)
    o_mla, o_mla_b, lse_mla = _attn_fwd("mla_fwd", (q_mla, 0), (k_mla, 0), (kv, 1))

    cum, cum_t = _forget_fwd(proj_r, bf_row)
    o_fox, o_fox_b, lse_fox = _attn_fwd("fox_fwd", (proj_f, 0), (proj_f, 1), (proj_f, 2), cum, cum_t)

    p.update(comm.late_weights("mix", o_fox_b))
    w_bm = _pad_head_rows(p["w_branch_mla"], V_DIM)
    w_bf = _pad_head_rows(p["w_branch_fox"], FOX_DIM)
    bm = _matmul("branch_mla", o_mla_b, w_bm, out_dtype=cd)
    bfx = _matmul("branch_fox", o_fox_b, w_bf, out_dtype=cd)
    merged = _gate_fwd(proj_r, p["b_gate"], bm, bfx)
    mix = _matmul("out_proj", merged, p["w_out"])
    h1, h1b = _ln_fwd("ln_mix_fwd", h0, mix, p["ln_mix_g"], p["ln_mix_b"])
    p.update(comm.late_weights("ffn", h1b))
    up = _matmul("ffn_up", h1b, p["w_ffn_up"], out_dtype=cd)
    act = _glu_fwd(up, p["conv_w"], p["conv_b"])
    f = _matmul("ffn_down", act, p["w_ffn_down"])
    loss = _ln_ffn_loss(h1, f, tgt, p["ln_ffn_g"], p["ln_ffn_b"])

    g = {}
    dz2, dz2b, g["ln_ffn_g"], g["ln_ffn_b"] = _ln_ffn_bwd(h1, f, tgt, p["ln_ffn_g"], p["ln_ffn_b"])
    d_act = _matmul("ffn_down_dx", dz2b, p["w_ffn_down"], tb=True, out_dtype=cd)
    g["w_ffn_down"] = _matmul("ffn_down_dw", act, dz2b, ta=True, out_dtype=cd)
    d_up, dcw, g["conv_b"] = _glu_bwd(up, p["conv_w"], p["conv_b"], d_act)
    g["conv_w"] = dcw[:3]
    dh1 = _matmul("ffn_up_dx", d_up, p["w_ffn_up"], tb=True, addend=dz2, alpha=ALPHA)
    g["w_ffn_up"] = _matmul("ffn_up_dw", h1b, d_up, ta=True, out_dtype=cd)
    sent = comm.send("ffn", {n: g[n] for n in ("w_ffn_down", "w_ffn_up", "conv_w")})
    dz1, dz1b, g["ln_mix_g"], g["ln_mix_b"] = _ln_bwd("ln_mix_bwd", h0, mix, dh1, p["ln_mix_g"], after=sent)
    dmerged = _matmul("out_proj_dx", dz1b, p["w_out"], tb=True, out_dtype=cd)
    g["w_out"] = _matmul("out_proj_dw", merged, dz1b, ta=True, out_dtype=cd)
    d_bm, d_bf, d_gl, g["b_gate"] = _gate_bwd(proj_r, p["b_gate"], bm, bfx, dmerged)
    do_mla_b = _matmul("branch_mla_dx", d_bm, w_bm, tb=True, out_dtype=cd)
    g["w_branch_mla"] = _unpad_head_rows(_matmul("branch_mla_dw", o_mla_b, d_bm, ta=True, out_dtype=cd), V_DIM)
    do_fox_b = _matmul("branch_fox_dx", d_bf, w_bf, tb=True, out_dtype=cd)
    g["w_branch_fox"] = _unpad_head_rows(_matmul("branch_fox_dw", o_fox_b, d_bf, ta=True, out_dtype=cd), FOX_DIM)

    sent = comm.send("mix", {n: g[n] for n in ("w_out", "w_branch_mla", "w_branch_fox")})
    dl_mla = _attn_delta("mla_delta", do_mla_b, o_mla, after=sent)
    dq_m, dk_m, dv_m = _attn_bwd("mla_bwd", (q_mla, 0), (k_mla, 0), (kv, 1), do_mla_b, lse_mla, dl_mla)
    dl_fox = _attn_delta("fox_delta", do_fox_b, o_fox)
    dfq, dfk, dfv, dcq, dck = _attn_bwd("fox_bwd", (proj_f, 0), (proj_f, 1), (proj_f, 2), do_fox_b, lse_fox, dl_fox,
                                        cum, cum_t, out_dtype=cd)
    dfl, dbf = _forget_bwd(proj_r, bf_row, dcq, dck)
    g["b_forget"] = dbf[:, LANE_FL:LANE_FL + HEADS]

    dq_b, dkv_b, dlast = _rope_bwd(dq_m, dk_m, dv_m, dfl, cos_t, sin_t)
    d_ql = _matmul("q_up_dx", dq_b, w_q, tb=True)
    d_kvl = _matmul("kv_up_dx", dkv_b, w_kv, tb=True)
    d_qlat, d_kvlat, g["q_norm_g"], g["kv_norm_g"] = _latent_norm_bwd(proj_r, (d_ql, d_kvl), latent_gains)
    dproj_f = jnp.concatenate([dfq, dfk, dfv], axis=1)
    dproj_r = jnp.concatenate([d_qlat, d_kvlat, dlast, jnp.zeros((r, R_GATE - R_LAST - HP), cd), d_gl], axis=1)
    g["w_in"] = (_matmul("in_proj_f_dw", h0b, dproj_f, ta=True, out_dtype=cd),
                 _matmul("in_proj_r_dw", h0b, dproj_r, ta=True, out_dtype=cd))
    sent = comm.send("in", {"w_in": g["w_in"]})
    dh0 = _matmul("in_proj_f_dx", dproj_f, w_f, tb=True, addend=dz1, alpha=ALPHA, after=sent)
    g["w_q_up"] = _unpad_head_cols(_matmul("q_up_dw", ql, dq_b, ta=True, out_dtype=cd, after=sent), QK_DIM)
    dw_kv = _matmul("kv_up_dw", kvl, dkv_b, ta=True, out_dtype=cd, after=sent)
    g["w_kv_up"] = _merge_w_kv(dw_kv[:, :HW], dw_kv[:, HW:])
    sent = comm.send("qkv", {n: g[n] for n in ("w_q_up", "w_kv_up")})
    dh0 = _matmul("in_proj_r_dx", dproj_r, w_r, tb=True, addend=dh0, after=sent)
    grad_x, d_meta, g["ln_emb_g"], g["ln_emb_b"] = _ln_emb_bwd(x, p["meta_tokens"], dh0, p["ln_emb_g"])
    return loss, grad_x, d_meta, g


BIG = (("w_in", 1), ("w_q_up", 1), ("w_kv_up", 1), ("w_branch_mla", 1), ("w_branch_fox", 1), ("w_out", 0),
       ("w_ffn_up", 1), ("w_ffn_down", 0))
SMALL_SHARDED = (("meta_tokens", 1), ("conv_w", 1))
EARLY = ("w_in", "meta_tokens")
LATE = {"qkv": ("w_q_up", "w_kv_up", "conv_w"),
        "mix": ("w_branch_mla", "w_branch_fox", "w_out"),
        "ffn": ("w_ffn_up", "w_ffn_down")}
REPLICATED = ("ln_emb_g", "ln_emb_b", "b_gate", "b_forget", "q_norm_g", "kv_norm_g", "ln_mix_g", "ln_mix_b",
              "conv_b", "ln_ffn_g", "ln_ffn_b")
PACK_COLS = 1024


def _pack(flat_list):
    cat = jnp.concatenate(flat_list)
    n = cat.shape[0]
    rows = -(-n // (8 * PACK_COLS)) * 8
    return jnp.pad(cat, (0, rows * PACK_COLS - n)).reshape(rows, PACK_COLS)


def _gathered_full(g3, axis):
    n, r, c = g3.shape
    if axis == 0:
        return g3.reshape(n * r, c)
    return g3.transpose(1, 0, 2).reshape(r, n * c)


def _shard_major(full, axis):
    r, c = full.shape
    if axis == 0:
        return full.reshape(N_DEV, r // N_DEV, c)
    return full.reshape(r, N_DEV, c // N_DEV).transpose(1, 0, 2)


def kernel(x, meta_tokens, ln_emb_g, ln_emb_b, w_in, b_gate, b_forget, q_norm_g, w_q_up, kv_norm_g, w_kv_up, w_branch_mla, w_branch_fox, w_out, ln_mix_g, ln_mix_b, w_ffn_up, conv_w, conv_b, w_ffn_down, ln_ffn_g, ln_ffn_b, loss_target, m_meta_tokens, m_ln_emb_g, m_ln_emb_b, m_w_in, m_b_gate, m_b_forget, m_q_norm_g, m_w_q_up, m_kv_norm_g, m_w_kv_up, m_w_branch_mla, m_w_branch_fox, m_w_out, m_ln_mix_g, m_ln_mix_b, m_w_ffn_up, m_conv_w, m_conv_b, m_w_ffn_down, m_ln_ffn_g, m_ln_ffn_b, v_meta_tokens, v_ln_emb_g, v_ln_emb_b, v_w_in, v_b_gate, v_b_forget, v_q_norm_g, v_w_q_up, v_kv_norm_g, v_w_kv_up, v_w_branch_mla, v_w_branch_fox, v_w_out, v_ln_mix_g, v_ln_mix_b, v_w_ffn_up, v_conv_w, v_conv_b, v_w_ffn_down, v_ln_ffn_g, v_ln_ffn_b):
    names = ("meta_tokens", "ln_emb_g", "ln_emb_b", "w_in", "b_gate", "b_forget", "q_norm_g", "w_q_up", "kv_norm_g",
             "w_kv_up", "w_branch_mla", "w_branch_fox", "w_out", "ln_mix_g", "ln_mix_b", "w_ffn_up", "conv_w", "conv_b",
             "w_ffn_down", "ln_ffn_g", "ln_ffn_b")
    w_args = (meta_tokens, ln_emb_g, ln_emb_b, w_in, b_gate, b_forget, q_norm_g, w_q_up, kv_norm_g, w_kv_up,
              w_branch_mla, w_branch_fox, w_out, ln_mix_g, ln_mix_b, w_ffn_up, conv_w, conv_b, w_ffn_down, ln_ffn_g, ln_ffn_b)
    m_args = (m_meta_tokens, m_ln_emb_g, m_ln_emb_b, m_w_in, m_b_gate, m_b_forget, m_q_norm_g, m_w_q_up, m_kv_norm_g,
              m_w_kv_up, m_w_branch_mla, m_w_branch_fox, m_w_out, m_ln_mix_g, m_ln_mix_b, m_w_ffn_up, m_conv_w, m_conv_b,
              m_w_ffn_down, m_ln_ffn_g, m_ln_ffn_b)
    v_args = (v_meta_tokens, v_ln_emb_g, v_ln_emb_b, v_w_in, v_b_gate, v_b_forget, v_q_norm_g, v_w_q_up, v_kv_norm_g,
              v_w_kv_up, v_w_branch_mla, v_w_branch_fox, v_w_out, v_ln_mix_g, v_ln_mix_b, v_w_ffn_up, v_conv_w, v_conv_b,
              v_w_ffn_down, v_ln_ffn_g, v_ln_ffn_b)
    as2d = lambda a: a.reshape((-1, a.shape[-1])) if a.ndim != 1 else a.reshape(1, -1)
    w = {n: as2d(a) for n, a in zip(names, w_args)}
    m = {n: as2d(a) for n, a in zip(names, m_args)}
    v = {n: as2d(a) for n, a in zip(names, v_args)}
    out_shape = {n: a.shape for n, a in zip(names, w_args)}

    axis_of = dict(BIG + SMALL_SHARDED)
    big = set(n for n, _ in BIG)
    wire = lambda n, a: a.astype(MXU_DTYPE) if n in big else a
    my_id = _my_id()

    early = _allgather("gather_early", [wire(n, w[n]) for n in EARLY])
    p = {n: _gathered_full(g3, axis_of[n]) for n, g3 in zip(EARLY, early) if n != "w_in"}
    p["w_in"] = _w_in_from_shards(early[EARLY.index("w_in")])
    for n in REPLICATED:
        p[n] = w[n].reshape(-1)
    late_src = [[wire(n, w[n]) for n in members] for members in LATE.values()]
    late_handles, late_token = _push_start("gather_late_start", late_src, False, after=early[0])
    late = {group: (members, src, handle)
            for (group, members), src, handle in zip(LATE.items(), late_src, late_handles)}
    sent = {}

    class Comm:
        first_token = (late_token,)

        def late_weights(self, group, after):
            members, src, handle = late[group]
            lands = _push_wait("gather_" + group + "_wait", handle, after)
            out = {}
            for n, own, land in zip(members, src, lands):
                if own.shape[0] % 16:
                    out[n] = _gathered_full(lax.dynamic_update_index_in_dim(land, own, my_id, 0), axis_of[n])
                elif axis_of[n] == 1:
                    out[n] = _cols_from_shards(n + "_repack", land, own)
                else:
                    out[n] = _rows_from_shards(n + "_repack", land, own)
            return out

        def send(self, name, grads):
            names_ = tuple(grads)
            parts = []
            for n in names_:
                if n == "w_in":
                    parts.append(_w_in_grad_to_shards(*grads[n], N_DEV, w[n].shape[1]))
                elif n == "w_ffn_up":
                    parts.append(_cols_to_shards(n + "_grad_unpack", grads[n], N_DEV))
                else:
                    parts.append(_shard_major(grads[n], axis_of[n]).astype(MXU_DTYPE))
            (handle,), token = _push_start("send_" + name + "_start", [parts], True)
            sent[name] = (names_, parts, handle)
            return (token,)

    loss_part, grad_x, d_meta, g = _local_step(x[0], loss_target[0], p, Comm())
    grad_x = grad_x[None]

    small = _pack([d_meta.reshape(-1)] + [g[n].reshape(-1) for n in REPLICATED] + [loss_part.reshape(-1)])
    (small_handle,), small_token = _push_start("send_small_start", [[small]], False)

    res = {}
    prev = small_token
    for name, (names_, parts, handle) in sent.items():
        lands = _push_wait("send_" + name + "_wait", handle, prev)
        for n, part, land in zip(names_, parts, lands):
            own = lax.dynamic_index_in_dim(part, my_id, axis=0, keepdims=False)
            res[n] = _adamw("adamw_" + n, land, w[n], m[n], v[n], own=own)
            prev = res[n][0]
    small_all = _push_wait("send_small_wait", small_handle, prev)[0]
    head = jnp.zeros((d_meta.size,), F32)
    rep_w = _pack([head] + [w[n].reshape(-1) for n in REPLICATED])
    rep_m = _pack([head] + [m[n].reshape(-1) for n in REPLICATED])
    rep_v = _pack([head] + [v[n].reshape(-1) for n in REPLICATED])
    rep_res = _adamw("adamw_replicated", small_all, rep_w, rep_m, rep_v, own=small)
    off = d_meta.size
    for n in REPLICATED:
        sz = w[n].size
        res[n] = tuple(a.reshape(-1)[off:off + sz] for a in rep_res)
        off += sz
    loss = rep_res[0].reshape(-1)[off]
    cols = w["meta_tokens"].shape[1]
    meta_rows = lambda a: a.reshape(a.shape[:-2] + (-1,))[..., :d_meta.size].reshape(a.shape[:-2] + d_meta.shape)
    my_cols = lambda a: lax.dynamic_slice_in_dim(a, my_id * cols, cols, axis=a.ndim - 1)
    res["meta_tokens"] = _adamw("adamw_meta_tokens", my_cols(meta_rows(small_all)), w["meta_tokens"],
                                m["meta_tokens"], v["meta_tokens"], own=my_cols(d_meta))

    outs = [loss, grad_x]
    for idx in range(4):
        outs += [res[n][idx].reshape(out_shape[n]) for n in names]
    return tuple(outs)
```

```python
import jax
import jax.numpy as jnp
from jax import lax
from jax.experimental import pallas as pl
from jax.experimental.pallas import tpu as pltpu

F32 = jnp.float32
BF16 = jnp.bfloat16
MXU_DTYPE = BF16

N_DEV = 8
N_META = 16
D_MODEL = 1024
HEADS = 8
Q_RANK = 384
KV_RANK = 128
NOPE = 64
ROPE = 32
HALF = ROPE // 2
QK_DIM = NOPE + ROPE
V_DIM = 64
FOX_DIM = 64
FOX_W = HEADS * FOX_DIM
D_FF = 2816
ROPE_THETA = 10000.0
LN_EPS = 1e-5
RMS_EPS = 1e-6
ALPHA = 2.0 ** 0.25
MLA_SCALE = QK_DIM ** -0.5
FOX_SCALE = FOX_DIM ** -0.5
NEG_INF = -1e30

HP = 128
HW = HEADS * HP
F_W = 3 * HW
R_QLAT = 0
R_KVLAT = Q_RANK
R_LAST = R_KVLAT + KV_RANK
R_GATE = D_MODEL
R_W = R_GATE + 2 * D_MODEL
LANE_FL = 0
LANE_PE = NOPE

ADAM_LR = 0.001
ADAM_B1 = 0.9
ADAM_B2 = 0.999
ADAM_EPS = 1e-08
ADAM_WD = 0.01
ADAM_STEP = 10

ROW_BLOCK = 256
ATT_TQ = 768
ATT_TK = 768
ATT_HEADS = 4
ROW_ALIGN = 768
MM_BLOCK_CAP = 1408
VMEM_LIMIT = 56 * 1024 * 1024
HIGHEST = lax.Precision.HIGHEST
NT = (((1,), (1,)), ((), ()))
TN = (((0,), (0,)), ((), ()))


def _params(sem=None):
    return pltpu.CompilerParams(dimension_semantics=sem, vmem_limit_bytes=VMEM_LIMIT)


def _call(name, body, grid, ins, outs, scratch=(), sem=None, after=()):
    n_in = len(ins)
    n_tok = len(after)

    def run(*refs):
        body(*refs[:n_in], *refs[n_in + n_tok:])

    tok_spec = pl.BlockSpec((8, 128), lambda *_: (0, 0))
    return pl.pallas_call(
        run, name=name, grid=grid,
        in_specs=[s for _, s in ins] + [tok_spec] * n_tok,
        out_specs=[s for _, s in outs],
        out_shape=[o for o, _ in outs],
        scratch_shapes=list(scratch),
        compiler_params=_params(sem),
    )(*[a for a, _ in ins], *after)


def _sds(shape, dtype):
    return jax.ShapeDtypeStruct(shape, dtype)


def _rows(br, c, cb=0):
    return pl.BlockSpec((br, c), lambda i: (i, cb))


def _whole(shape):
    n = len(shape)
    return pl.BlockSpec(shape, lambda i: (0,) * n)


def _pick(dim, cap, mult):
    best = None
    d = mult
    while d <= min(dim, cap):
        if dim % d == 0:
            best = d
        d += mult
    return best if best is not None else dim


def _hs(h):
    return slice(h * HP, (h + 1) * HP)


def _matmul(name, a, b, *, ta=False, tb=False, out_dtype=F32, addend=None, alpha=1.0, after=()):
    if ta:
        k, m = a.shape
    else:
        m, k = a.shape
    if tb:
        n, k2 = b.shape
    else:
        k2, n = b.shape
    assert k == k2, (name, a.shape, b.shape)
    bm = _pick(m, MM_BLOCK_CAP, 128 if ta else 16)
    bn = _pick(n, MM_BLOCK_CAP, 128)
    bk = _pick(k, MM_BLOCK_CAP, 128 if (not ta or tb) else 16)
    nk = k // bk
    dims = (((0 if ta else 1,), (1 if tb else 0,)), ((), ()))
    has_add = addend is not None

    def body(*refs):
        a_ref, b_ref = refs[:2]
        add_ref = refs[2] if has_add else None
        o_ref = refs[3 if has_add else 2]

        def finish(r):
            if has_add:
                r = r + alpha * add_ref[...]
            o_ref[...] = r.astype(o_ref.dtype)

        part = lax.dot_general(a_ref[...], b_ref[...], dims, preferred_element_type=F32)
        if nk == 1:
            finish(part)
            return
        acc_ref = refs[-1]
        kk = pl.program_id(2)

        @pl.when(kk == 0)
        def _():
            acc_ref[...] = part

        @pl.when(kk > 0)
        def _():
            acc_ref[...] += part

        @pl.when(kk == nk - 1)
        def _():
            finish(acc_ref[...])

    a_spec = pl.BlockSpec((bk, bm), lambda i, j, l: (l, i)) if ta else pl.BlockSpec((bm, bk), lambda i, j, l: (i, l))
    b_spec = pl.BlockSpec((bn, bk), lambda i, j, l: (j, l)) if tb else pl.BlockSpec((bk, bn), lambda i, j, l: (l, j))
    o_spec = pl.BlockSpec((bm, bn), lambda i, j, l: (i, j))
    ins = [(a, a_spec), (b, b_spec)]
    if has_add:
        ins.append((addend, o_spec))
    return _call(name, body, (m // bm, n // bn, nk), ins, [(_sds((m, n), out_dtype), o_spec)],
                 scratch=[pltpu.VMEM((bm, bn), F32)] if nk > 1 else [],
                 sem=("parallel", "parallel", "arbitrary"), after=after)[0]


def _ln_stats(z):
    mu = jnp.mean(z, axis=-1, keepdims=True)
    zc = z - mu
    var = jnp.mean(zc * zc, axis=-1, keepdims=True)
    rstd = lax.rsqrt(var + LN_EPS)
    return zc * rstd, rstd


def _ln_fwd(name, a, res, g, b, after=()):
    r, d = a.shape
    br = ROW_BLOCK
    has_res = res is not None

    def body(*refs):
        if has_res:
            a_ref, r_ref, g_ref, b_ref, y_ref, yb_ref = refs
            z = ALPHA * a_ref[...] + r_ref[...]
        else:
            a_ref, g_ref, b_ref, y_ref, yb_ref = refs
            z = a_ref[...]
        xhat, _ = _ln_stats(z)
        y = xhat * g_ref[...] + b_ref[...]
        y_ref[...] = y
        yb_ref[...] = y.astype(yb_ref.dtype)

    ins = [(a, _rows(br, d))]
    if has_res:
        ins.append((res, _rows(br, d)))
    ins += [(g.reshape(1, d), _whole((1, d))), (b.reshape(1, d), _whole((1, d)))]
    outs = [(_sds((r, d), F32), _rows(br, d)), (_sds((r, d), MXU_DTYPE), _rows(br, d))]
    return _call(name, body, (r // br,), ins, outs, sem=("parallel",), after=after)


def _ln_bwd(name, a, res, dy, g, after=()):
    r, d = a.shape
    br = ROW_BLOCK
    has_res = res is not None

    def body(*refs):
        if has_res:
            a_ref, r_ref, dy_ref, g_ref, dz_ref, dzb_ref, dg_ref, db_ref = refs
            z = ALPHA * a_ref[...] + r_ref[...]
        else:
            a_ref, dy_ref, g_ref, dz_ref, dzb_ref, dg_ref, db_ref = refs
            z = a_ref[...]
        xhat, rstd = _ln_stats(z)
        dyv = dy_ref[...]
        dyg = dyv * g_ref[...]
        m1 = jnp.mean(dyg, axis=-1, keepdims=True)
        m2 = jnp.mean(dyg * xhat, axis=-1, keepdims=True)
        dz = rstd * (dyg - m1 - xhat * m2)
        dz_ref[...] = dz
        dzb_ref[...] = dz.astype(dzb_ref.dtype)

        @pl.when(pl.program_id(0) == 0)
        def _():
            dg_ref[...] = jnp.zeros_like(dg_ref)
            db_ref[...] = jnp.zeros_like(db_ref)

        dg_ref[...] += jnp.sum(dyv * xhat, axis=0, keepdims=True)
        db_ref[...] += jnp.sum(dyv, axis=0, keepdims=True)

    ins = [(a, _rows(br, d))]
    if has_res:
        ins.append((res, _rows(br, d)))
    ins += [(dy, _rows(br, d)), (g.reshape(1, d), _whole((1, d)))]
    outs = [(_sds((r, d), F32), _rows(br, d)), (_sds((r, d), MXU_DTYPE), _rows(br, d)),
            (_sds((1, d), F32), _whole((1, d))), (_sds((1, d), F32), _whole((1, d)))]
    return _call(name, body, (r // br,), ins, outs, sem=("arbitrary",), after=after)


LATENTS = ((R_QLAT // Q_RANK, Q_RANK), (R_KVLAT // KV_RANK, KV_RANK))


def _latent_norm_fwd(proj_r, gains):
    r = proj_r.shape[0]
    br = ROW_BLOCK

    def body(xq_ref, xk_ref, gq_ref, gk_ref, yq_ref, yk_ref):
        for x_ref, g_ref, y_ref in ((xq_ref, gq_ref, yq_ref), (xk_ref, gk_ref, yk_ref)):
            x = x_ref[...]
            rstd = lax.rsqrt(jnp.mean(x * x, axis=-1, keepdims=True) + RMS_EPS)
            y_ref[...] = (x * rstd * g_ref[...]).astype(y_ref.dtype)

    return _call("latent_norm_fwd", body, (r // br,),
                 [(proj_r, _rows(br, wd, cb)) for cb, wd in LATENTS]
                 + [(g.reshape(1, wd), _whole((1, wd))) for g, (_, wd) in zip(gains, LATENTS)],
                 [(_sds((r, wd), MXU_DTYPE), _rows(br, wd)) for _, wd in LATENTS], sem=("parallel",))


def _latent_norm_bwd(proj_r, dys, gains):
    r = proj_r.shape[0]
    br = ROW_BLOCK

    def body(xq_ref, xk_ref, dq_ref, dk_ref, gq_ref, gk_ref, oq_ref, ok_ref, dgq_ref, dgk_ref):
        @pl.when(pl.program_id(0) == 0)
        def _():
            dgq_ref[...] = jnp.zeros_like(dgq_ref)
            dgk_ref[...] = jnp.zeros_like(dgk_ref)

        for x_ref, dy_ref, g_ref, dx_ref, dg_ref in ((xq_ref, dq_ref, gq_ref, oq_ref, dgq_ref),
                                                     (xk_ref, dk_ref, gk_ref, ok_ref, dgk_ref)):
            x = x_ref[...]
            rstd = lax.rsqrt(jnp.mean(x * x, axis=-1, keepdims=True) + RMS_EPS)
            nrm = x * rstd
            dyv = dy_ref[...]
            dyg = dyv * g_ref[...]
            dx_ref[...] = (rstd * (dyg - nrm * jnp.mean(dyg * nrm, axis=-1, keepdims=True))).astype(dx_ref.dtype)
            dg_ref[...] += jnp.sum(dyv * nrm, axis=0, keepdims=True)

    return _call("latent_norm_bwd", body, (r // br,),
                 [(proj_r, _rows(br, wd, cb)) for cb, wd in LATENTS]
                 + [(dy, _rows(br, wd)) for dy, (_, wd) in zip(dys, LATENTS)]
                 + [(g.reshape(1, wd), _whole((1, wd))) for g, (_, wd) in zip(gains, LATENTS)],
                 [(_sds((r, wd), MXU_DTYPE), _rows(br, wd)) for _, wd in LATENTS]
                 + [(_sds((1, wd), F32), _whole((1, wd))) for _, wd in LATENTS], sem=("arbitrary",))


def _lane_iota(shape):
    return lax.broadcasted_iota(jnp.int32, shape, 1)


def _rotary(t, c, s, lane, sign):
    second = pltpu.roll(t, HP - HALF, axis=1)
    first = pltpu.roll(t, HALF, axis=1)
    lo = (lane >= LANE_PE) & (lane < LANE_PE + HALF)
    hi = (lane >= LANE_PE + HALF) & (lane < LANE_PE + ROPE)
    return jnp.where(lo, t * c - sign * second * s, jnp.where(hi, t * c + sign * first * s, t))


def _rope_fwd(q_raw, k_part, proj_r, cos_t, sin_t):
    r = q_raw.shape[0]
    br = ROW_BLOCK

    def body(q_ref, k_ref, t_ref, c_ref, s_ref, qo_ref, ko_ref):
        c = c_ref[...]
        s = s_ref[...]
        lane = _lane_iota((br, HP))
        pe = (lane >= LANE_PE) & (lane < LANE_PE + ROPE)
        kp = jnp.where(pe, _rotary(t_ref[...], c, s, lane, 1.0), 0.0)
        for h in range(HEADS):
            qo_ref[:, _hs(h)] = (_rotary(q_ref[:, _hs(h)], c, s, lane, 1.0) * MLA_SCALE).astype(qo_ref.dtype)
            ko_ref[:, _hs(h)] = (k_ref[:, _hs(h)] + kp).astype(ko_ref.dtype)

    blk = _rows(br, HP)
    wide = _rows(br, HW)
    return _call("rope_fwd", body, (r // br,),
                 [(q_raw, wide), (k_part, wide), (proj_r, _rows(br, HP, R_LAST // HP)), (cos_t, blk), (sin_t, blk)],
                 [(_sds((r, HW), MXU_DTYPE), wide)] * 2, sem=("parallel",))


def _rope_bwd(dq, dk, dv, dfl, cos_t, sin_t):
    r = dq.shape[0]
    br = ROW_BLOCK

    def body(dq_ref, dk_ref, dv_ref, fl_ref, c_ref, s_ref, dqo_ref, dkv_ref, dl_ref):
        c = c_ref[...]
        s = s_ref[...]
        lane = _lane_iota((br, HP))
        pe = (lane >= LANE_PE) & (lane < LANE_PE + ROPE)
        acc = jnp.zeros((br, HP), F32)
        for h in range(HEADS):
            dqo_ref[:, _hs(h)] = (_rotary(dq_ref[:, _hs(h)], c, s, lane, -1.0) * MLA_SCALE).astype(dqo_ref.dtype)
            dkh = dk_ref[:, _hs(h)]
            acc = acc + dkh
            dkv_ref[:, _hs(h)] = dkh.astype(dkv_ref.dtype)
            dkv_ref[:, _hs(HEADS + h)] = dv_ref[:, _hs(h)].astype(dkv_ref.dtype)
        dl_ref[...] = (jnp.where(pe, _rotary(acc, c, s, lane, -1.0), 0.0) + fl_ref[...]).astype(dl_ref.dtype)

    blk = _rows(br, HP)
    wide = _rows(br, HW)
    return _call("rope_bwd", body, (r // br,),
                 [(dq, wide), (dk, wide), (dv, wide), (dfl, blk), (cos_t, blk), (sin_t, blk)],
                 [(_sds((r, HW), MXU_DTYPE), wide), (_sds((r, 2 * HW), MXU_DTYPE), _rows(br, 2 * HW)),
                  (_sds((r, HP), MXU_DTYPE), blk)],
                 sem=("parallel",))


def _log_sigmoid(x):
    return jnp.minimum(x, 0.0) - jnp.log(1.0 + jnp.exp(-jnp.abs(x)))


def _head_lane(x, h, lane):
    return jnp.sum(jnp.where(lane == h, x, 0.0), axis=1, keepdims=True)


def _forget_fwd(proj_r, bf_row):
    r = proj_r.shape[0]
    br = ROW_BLOCK

    def body(t_ref, b_ref, ob_ref, ot_ref, carry_ref):
        @pl.when(pl.program_id(0) == 0)
        def _():
            carry_ref[...] = jnp.zeros_like(carry_ref)

        x = t_ref[...] + b_ref[...]
        lane = _lane_iota(x.shape)
        lf = jnp.where((lane >= LANE_FL) & (lane < LANE_FL + HEADS), _log_sigmoid(x), 0.0)
        tri = (lax.broadcasted_iota(jnp.int32, (br, br), 0) >= lax.broadcasted_iota(jnp.int32, (br, br), 1)).astype(F32)
        cum = jnp.dot(tri, lf, precision=HIGHEST, preferred_element_type=F32) + carry_ref[0:1, :]
        for h in range(HEADS):
            ob_ref[:, _hs(h)] = jnp.broadcast_to(_head_lane(cum, LANE_FL + h, lane), (br, HP))
        ot_ref[...] = cum.T[LANE_FL:LANE_FL + HEADS, :]
        carry_ref[...] = jnp.broadcast_to(cum[br - 1:br, :], carry_ref.shape)

    return _call("forget_fwd", body, (r // br,),
                 [(proj_r, _rows(br, HP, R_LAST // HP)), (bf_row, _whole((1, HP)))],
                 [(_sds((r, HW), F32), _rows(br, HW)), (_sds((HEADS, r), F32), pl.BlockSpec((HEADS, br), lambda i: (0, i)))],
                 scratch=[pltpu.VMEM((8, HP), F32)], sem=("arbitrary",))


def _forget_bwd(proj_r, bf_row, dcq_t, dck_b):
    r = proj_r.shape[0]
    br = ROW_BLOCK
    nb = r // br

    def body(t_ref, b_ref, dcq_ref, dck_ref, o_ref, db_ref, carry_ref):
        @pl.when(pl.program_id(0) == 0)
        def _():
            carry_ref[...] = jnp.zeros_like(carry_ref)
            db_ref[...] = jnp.zeros_like(db_ref)

        lane = _lane_iota((br, HP))
        dc = jnp.concatenate([dcq_ref[...], jnp.zeros((HP - HEADS, br), F32)], axis=0).T
        for h in range(HEADS):
            dc = dc + jnp.where(lane == LANE_FL + h, dck_ref[:, h * HP:h * HP + 1], 0.0)
        triu = (lax.broadcasted_iota(jnp.int32, (br, br), 0) <= lax.broadcasted_iota(jnp.int32, (br, br), 1)).astype(F32)
        dlf = jnp.dot(triu, dc, precision=HIGHEST, preferred_element_type=F32) + carry_ref[0:1, :]
        carry_ref[...] = jnp.broadcast_to(dlf[0:1, :], carry_ref.shape)
        x = t_ref[...] + b_ref[...]
        dfl = jnp.where((lane >= LANE_FL) & (lane < LANE_FL + HEADS), dlf * jax.nn.sigmoid(-x), 0.0)
        o_ref[...] = dfl
        db_ref[...] += jnp.sum(dfl, axis=0, keepdims=True)

    rev = pl.BlockSpec((br, HP), lambda i: (nb - 1 - i, 0))
    return _call("forget_bwd", body, (nb,),
                 [(proj_r, pl.BlockSpec((br, HP), lambda i: (nb - 1 - i, R_LAST // HP))), (bf_row, _whole((1, HP))),
                  (dcq_t, pl.BlockSpec((HEADS, br), lambda i: (0, nb - 1 - i))),
                  (dck_b, pl.BlockSpec((br, HW), lambda i: (nb - 1 - i, 0)))],
                 [(_sds((r, HP), F32), rev), (_sds((1, HP), F32), _whole((1, HP)))],
                 scratch=[pltpu.VMEM((8, HP), F32)], sem=("arbitrary",))


def _gate_fwd(proj_r, b_gate, bm, bfx):
    r, d = bm.shape
    br = ROW_BLOCK
    cb = R_GATE // d

    def body(gm_ref, gf_ref, b1_ref, b2_ref, bm_ref, bf_ref, o_ref):
        g1 = jax.nn.sigmoid(gm_ref[...] + b1_ref[...])
        g2 = jax.nn.sigmoid(gf_ref[...] + b2_ref[...])
        o_ref[...] = (g1 * bm_ref[...].astype(F32) + g2 * bf_ref[...].astype(F32)).astype(o_ref.dtype)

    b1 = b_gate[:d].reshape(1, d)
    b2 = b_gate[d:].reshape(1, d)
    return _call("gate_fwd", body, (r // br,),
                 [(proj_r, _rows(br, d, cb)), (proj_r, _rows(br, d, cb + 1)), (b1, _whole((1, d))), (b2, _whole((1, d))),
                  (bm, _rows(br, d)), (bfx, _rows(br, d))],
                 [(_sds((r, d), MXU_DTYPE), _rows(br, d))], sem=("parallel",))[0]


def _gate_bwd(proj_r, b_gate, bm, bfx, dmerged):
    r, d = bm.shape
    br = ROW_BLOCK
    cb = R_GATE // d

    def body(gm_ref, gf_ref, b1_ref, b2_ref, bm_ref, bf_ref, dm_ref, dbm_ref, dbf_ref, dgl_ref, dbg_ref):
        g1 = jax.nn.sigmoid(gm_ref[...] + b1_ref[...])
        g2 = jax.nn.sigmoid(gf_ref[...] + b2_ref[...])
        dm = dm_ref[...].astype(F32)
        dbm_ref[...] = (dm * g1).astype(dbm_ref.dtype)
        dbf_ref[...] = (dm * g2).astype(dbf_ref.dtype)
        dl1 = dm * bm_ref[...].astype(F32) * (g1 * (1.0 - g1))
        dl2 = dm * bf_ref[...].astype(F32) * (g2 * (1.0 - g2))
        dgl_ref[:, 0:d] = dl1.astype(dgl_ref.dtype)
        dgl_ref[:, d:2 * d] = dl2.astype(dgl_ref.dtype)

        @pl.when(pl.program_id(0) == 0)
        def _():
            dbg_ref[...] = jnp.zeros_like(dbg_ref)

        dbg_ref[:, 0:d] += jnp.sum(dl1, axis=0, keepdims=True)
        dbg_ref[:, d:2 * d] += jnp.sum(dl2, axis=0, keepdims=True)

    b1 = b_gate[:d].reshape(1, d)
    b2 = b_gate[d:].reshape(1, d)
    return _call("gate_bwd", body, (r // br,),
                 [(proj_r, _rows(br, d, cb)), (proj_r, _rows(br, d, cb + 1)), (b1, _whole((1, d))), (b2, _whole((1, d))),
                  (bm, _rows(br, d)), (bfx, _rows(br, d)), (dmerged, _rows(br, d))],
                 [(_sds((r, d), MXU_DTYPE), _rows(br, d)), (_sds((r, d), MXU_DTYPE), _rows(br, d)),
                  (_sds((r, 2 * d), MXU_DTYPE), _rows(br, 2 * d)), (_sds((1, 2 * d), F32), _whole((1, 2 * d)))],
                 sem=("arbitrary",))


HALO = 16
GLU_BWD_BLOCK = 128


def _conv_taps(gp, halo, first_block):
    halo = jnp.where(first_block, 0.0, halo.astype(F32))
    rid = lax.broadcasted_iota(jnp.int32, gp.shape, 0)
    last, prev = halo[HALO - 1:HALO, :], halo[HALO - 2:HALO - 1, :]
    g1 = jnp.where(rid == 0, last, pltpu.roll(gp, 1, axis=0))
    g2 = jnp.where(rid == 0, prev, jnp.where(rid == 1, last, pltpu.roll(gp, 2, axis=0)))
    return g1, g2


def _prev_halo(br, c):
    return pl.BlockSpec((HALO, c), lambda i: (jnp.maximum(i * (br // HALO) - 1, 0), 0))


def _glu_fwd(up, conv_w, conv_b):
    r = up.shape[0]
    c = D_FF
    br = ROW_BLOCK

    def body(gp_ref, halo_ref, val_ref, w_ref, b_ref, o_ref):
        gp = gp_ref[...].astype(F32)
        g1, g2 = _conv_taps(gp, halo_ref[...], pl.program_id(0) == 0)
        gate = w_ref[0:1, :] * g2 + w_ref[1:2, :] * g1 + w_ref[2:3, :] * gp + b_ref[...]
        o_ref[...] = (gate * jax.nn.sigmoid(gate) * val_ref[...].astype(F32)).astype(o_ref.dtype)

    return _call("glu_fwd", body, (r // br,),
                 [(up, _rows(br, c, 0)), (up, _prev_halo(br, c)), (up, _rows(br, c, 1)),
                  (conv_w, _whole((3, c))), (conv_b.reshape(1, c), _whole((1, c)))],
                 [(_sds((r, c), MXU_DTYPE), _rows(br, c))], sem=("parallel",))[0]


def _glu_bwd(up, conv_w, conv_b, d_act):
    r = up.shape[0]
    c = D_FF
    br = GLU_BWD_BLOCK
    nb = r // br

    def body(gp_ref, halo_ref, val_ref, da_ref, gpn_ref, valn_ref, dan_ref, w_ref, b_ref, o_ref, dw_ref, db_ref):
        i = pl.program_id(0)
        w0, w1, w2, bias = w_ref[0:1, :], w_ref[1:2, :], w_ref[2:3, :], b_ref[...]

        def d_gate(gp, g1, g2, val, da):
            gate = w0 * g2 + w1 * g1 + w2 * gp + bias
            sg = jax.nn.sigmoid(gate)
            return da * val * (sg * (1.0 + gate * (1.0 - sg))), da * (gate * sg)

        gp = gp_ref[...].astype(F32)
        g1, g2 = _conv_taps(gp, halo_ref[...], i == 0)
        dg, dv = d_gate(gp, g1, g2, val_ref[...].astype(F32), da_ref[...].astype(F32))
        gpn = gpn_ref[...].astype(F32)
        g1n, g2n = _conv_taps(gpn, gp[br - HALO:, :], False)
        dgn, _ = d_gate(gpn, g1n, g2n, valn_ref[...].astype(F32), dan_ref[...].astype(F32))
        dgn = jnp.where(i == nb - 1, 0.0, dgn)
        rid = lax.broadcasted_iota(jnp.int32, dg.shape, 0)
        u1 = jnp.where(rid == br - 1, dgn[0:1, :], pltpu.roll(dg, br - 1, axis=0))
        u2 = jnp.where(rid == br - 1, dgn[1:2, :], jnp.where(rid == br - 2, dgn[0:1, :], pltpu.roll(dg, br - 2, axis=0)))
        o_ref[:, 0:c] = (w2 * dg + w1 * u1 + w0 * u2).astype(o_ref.dtype)
        o_ref[:, c:2 * c] = dv.astype(o_ref.dtype)

        @pl.when(i == 0)
        def _():
            dw_ref[...] = jnp.zeros_like(dw_ref)
            db_ref[...] = jnp.zeros_like(db_ref)

        dw_ref[0:1, :] += jnp.sum(dg * g2, axis=0, keepdims=True)
        dw_ref[1:2, :] += jnp.sum(dg * g1, axis=0, keepdims=True)
        dw_ref[2:3, :] += jnp.sum(dg * gp, axis=0, keepdims=True)
        db_ref[...] += jnp.sum(dg, axis=0, keepdims=True)

    nxt = lambda cb: pl.BlockSpec((HALO, c), lambda i: (jnp.minimum((i + 1) * (br // HALO), r // HALO - 1), cb))
    return _call("glu_bwd", body, (nb,),
                 [(up, _rows(br, c, 0)), (up, _prev_halo(br, c)), (up, _rows(br, c, 1)), (d_act, _rows(br, c)),
                  (up, nxt(0)), (up, nxt(1)), (d_act, nxt(0)),
                  (conv_w, _whole((3, c))), (conv_b.reshape(1, c), _whole((1, c)))],
                 [(_sds((r, 2 * c), MXU_DTYPE), _rows(br, 2 * c)),
                  (_sds((8, c), F32), _whole((8, c))), (_sds((1, c), F32), _whole((1, c)))],
                 sem=("arbitrary",))


def _token_specs(seq, d):
    br = ROW_BLOCK
    nxb = seq // br
    main = pl.BlockSpec((br, d), lambda i: (jnp.minimum(i, nxb - 1), 0))
    tail = pl.BlockSpec((N_META, d), lambda i: (jnp.clip(i * (br // N_META) - 1, 0, seq // N_META - 1), 0))
    return main, tail


def _padded_block(main_ref, tail_ref, first, seq):
    br = ROW_BLOCK
    i = pl.program_id(0)
    nxb = seq // br
    main = jnp.where(i < nxb, main_ref[...], 0.0)
    head = jnp.where(i == 0, first, jnp.where(i <= nxb, tail_ref[...], 0.0))
    return jnp.concatenate([head, main[:br - N_META]], axis=0)


def _ln_emb_fwd(x, meta, g, b, rows, after=()):
    seq, d = x.shape
    br = ROW_BLOCK
    assert seq % br == 0 and br % N_META == 0 and rows % br == 0

    def body(x_ref, tail_ref, meta_ref, g_ref, b_ref, y_ref, yb_ref):
        z = _padded_block(x_ref, tail_ref, meta_ref[...], seq)
        xhat, _ = _ln_stats(z)
        y = xhat * g_ref[...] + b_ref[...]
        y_ref[...] = y
        yb_ref[...] = y.astype(yb_ref.dtype)

    main, tail = _token_specs(seq, d)
    return _call("ln_emb_fwd", body, (rows // br,),
                 [(x, main), (x, tail), (meta, _whole((N_META, d))), (g.reshape(1, d), _whole((1, d))),
                  (b.reshape(1, d), _whole((1, d)))],
                 [(_sds((rows, d), F32), _rows(br, d)), (_sds((rows, d), MXU_DTYPE), _rows(br, d))],
                 sem=("parallel",), after=after)


def _ln_emb_bwd(x, meta, dh0, g):
    seq, d = x.shape
    br = ROW_BLOCK
    step = br // N_META

    def ln_bwd(z, dy, gv):
        xhat, rstd = _ln_stats(z)
        dyg = dy * gv
        m1 = jnp.mean(dyg, axis=-1, keepdims=True)
        m2 = jnp.mean(dyg * xhat, axis=-1, keepdims=True)
        dz = rstd * (dyg - m1 - xhat * m2)
        return dz, jnp.sum(dy * xhat, axis=0, keepdims=True), jnp.sum(dy, axis=0, keepdims=True)

    def body(x_ref, dh_ref, nxt_ref, meta_ref, top_ref, g_ref, dx_ref, dm_ref, dg_ref, db_ref):
        gv = g_ref[...]
        dy = jnp.concatenate([dh_ref[N_META:, :], nxt_ref[...]], axis=0)
        dz, dg, db = ln_bwd(x_ref[...], dy, gv)
        dx_ref[...] = dz

        @pl.when(pl.program_id(0) == 0)
        def _():
            dzm, dgm, dbm = ln_bwd(meta_ref[...], top_ref[...], gv)
            dm_ref[...] = dzm
            dg_ref[...] = dgm
            db_ref[...] = dbm

        dg_ref[...] += dg
        db_ref[...] += db

    small = _whole((N_META, d))
    return _call("ln_emb_bwd", body, (seq // br,),
                 [(x, _rows(br, d)), (dh0, _rows(br, d)), (dh0, pl.BlockSpec((N_META, d), lambda i: ((i + 1) * step, 0))),
                  (meta, small), (dh0, small), (g.reshape(1, d), _whole((1, d)))],
                 [(_sds((seq, d), F32), _rows(br, d)), (_sds((N_META, d), F32), small),
                  (_sds((1, d), F32), _whole((1, d))), (_sds((1, d), F32), _whole((1, d)))], sem=("arbitrary",))


def _ln_ffn_loss(h1, f, tgt, g, b):
    r, d = h1.shape
    seq = tgt.shape[0]
    br = ROW_BLOCK

    def body(a_ref, r_ref, t_ref, tail_ref, g_ref, b_ref, l_ref):
        err = _loss_err(a_ref, r_ref, t_ref, tail_ref, g_ref, b_ref, seq)[0]

        @pl.when(pl.program_id(0) == 0)
        def _():
            l_ref[...] = jnp.zeros_like(l_ref)

        l_ref[...] += jnp.sum(jnp.sum(err * err, axis=1, keepdims=True), axis=0, keepdims=True) * (0.5 / d)

    main, tail = _token_specs(seq, d)
    return _call("ln_ffn_loss", body, (r // br,),
                 [(h1, _rows(br, d)), (f, _rows(br, d)), (tgt, main), (tgt, tail),
                  (g.reshape(1, d), _whole((1, d))), (b.reshape(1, d), _whole((1, d)))],
                 [(_sds((1, 1), F32), _whole((1, 1)))], sem=("arbitrary",))[0]


def _loss_err(a_ref, r_ref, t_ref, tail_ref, g_ref, b_ref, seq):
    br, d = a_ref.shape
    xhat, rstd = _ln_stats(ALPHA * a_ref[...] + r_ref[...])
    y = xhat * g_ref[...] + b_ref[...]
    t = _padded_block(t_ref, tail_ref, jnp.zeros((N_META, d), F32), seq)
    rid = lax.broadcasted_iota(jnp.int32, (br, d), 0) + pl.program_id(0) * br
    valid = (rid >= N_META) & (rid < N_META + seq)
    return jnp.where(valid, y - t, 0.0), xhat, rstd


def _ln_ffn_bwd(h1, f, tgt, g, b):
    r, d = h1.shape
    seq = tgt.shape[0]
    br = ROW_BLOCK

    def body(a_ref, r_ref, t_ref, tail_ref, g_ref, b_ref, dz_ref, dzb_ref, dg_ref, db_ref):
        err, xhat, rstd = _loss_err(a_ref, r_ref, t_ref, tail_ref, g_ref, b_ref, seq)
        dyv = err * (1.0 / d)
        dyg = dyv * g_ref[...]
        m1 = jnp.mean(dyg, axis=-1, keepdims=True)
        m2 = jnp.mean(dyg * xhat, axis=-1, keepdims=True)
        dz = rstd * (dyg - m1 - xhat * m2)
        dz_ref[...] = dz
        dzb_ref[...] = dz.astype(dzb_ref.dtype)

        @pl.when(pl.program_id(0) == 0)
        def _():
            dg_ref[...] = jnp.zeros_like(dg_ref)
            db_ref[...] = jnp.zeros_like(db_ref)

        dg_ref[...] += jnp.sum(dyv * xhat, axis=0, keepdims=True)
        db_ref[...] += jnp.sum(dyv, axis=0, keepdims=True)

    main, tail = _token_specs(seq, d)
    return _call("ln_ffn_bwd", body, (r // br,),
                 [(h1, _rows(br, d)), (f, _rows(br, d)), (tgt, main), (tgt, tail),
                  (g.reshape(1, d), _whole((1, d))), (b.reshape(1, d), _whole((1, d)))],
                 [(_sds((r, d), F32), _rows(br, d)), (_sds((r, d), MXU_DTYPE), _rows(br, d)),
                  (_sds((1, d), F32), _whole((1, d))), (_sds((1, d), F32), _whole((1, d)))], sem=("arbitrary",))


def _attn_fwd(name, q, k, v, cum_b=None, cum_t=None):
    (qa, qg), (ka, kg), (va, vg) = q, k, v
    r = qa.shape[0]
    tq, tk = ATT_TQ, ATT_TK
    nq, nk = r // tq, r // tk
    bias = cum_b is not None

    def body(*refs):
        if bias:
            q_ref, k_ref, vt_ref, cb_ref, ct_ref, o_ref, ob_ref, lse_ref = refs
        else:
            q_ref, k_ref, vt_ref, o_ref, ob_ref, lse_ref = refs
        i = pl.program_id(1)
        qs = [q_ref[:, _hs(hh)] for hh in range(hg)]
        cqs = [ct_ref[hh] for hh in range(hg)] if bias else None
        diff = lax.broadcasted_iota(jnp.int32, (tk, tq), 0) - lax.broadcasted_iota(jnp.int32, (tk, tq), 1)

        def step(j, carry, masked):
            keys = pl.ds(pl.multiple_of(j * tk, tk), tk)
            out = []
            for hh in range(hg):
                m, l, acc = carry[hh]
                kt = k_ref[keys, _hs(hh)]
                s = lax.dot_general(kt, qs[hh], NT, preferred_element_type=F32)
                if bias:
                    s = s + (cqs[hh] - cb_ref[keys, hh * HP:hh * HP + 1])
                if masked:
                    s = jnp.where(diff <= i * tq - j * tk, s, NEG_INF)
                m_new = jnp.maximum(m, jnp.max(s, axis=0, keepdims=True))
                p = jnp.exp(s - m_new)
                a = jnp.exp(m - m_new)
                l = a * l + jnp.sum(p, axis=0, keepdims=True)
                acc = a * acc + jnp.dot(vt_ref[j, _hs(hh), :], p.astype(kt.dtype), preferred_element_type=F32)
                out.append((m_new, l, acc))
            return tuple(out)

        n_clear = (i * tq + 1) // tk
        n_all = ((i + 1) * tq - 1) // tk + 1
        carry = tuple((jnp.full((1, tq), NEG_INF, F32), jnp.zeros((1, tq), F32), jnp.zeros((HP, tq), F32))
                      for _ in range(hg))
        carry = lax.fori_loop(0, n_clear, lambda j, c: step(j, c, False), carry)
        carry = lax.fori_loop(n_clear, n_all, lambda j, c: step(j, c, True), carry)
        for hh in range(hg):
            m, l, acc = carry[hh]
            o = (acc / l).T
            o_ref[:, _hs(hh)] = o
            ob_ref[:, _hs(hh)] = o.astype(ob_ref.dtype)
            lse_ref[hh] = m + jnp.log(l)

    hg = ATT_HEADS
    w = hg * HP
    gpw = HW // w
    tile = lambda g: pl.BlockSpec((tq, w), lambda h, i: (i, g * gpw + h))
    res = lambda g: pl.BlockSpec((r, w), lambda h, i: (0, g * gpw + h))
    v_t = _key_tiles_transposed(name + "_vt", va, vg)
    ins = [(qa, tile(qg)), (ka, res(kg)), (v_t, pl.BlockSpec((nk, w, tk), lambda h, i: (0, h, 0)))]
    if bias:
        ins += [(cum_b, res(0)),
                (cum_t.reshape(HEADS, nq, 1, tq), pl.BlockSpec((hg, None, 1, tq), lambda h, i: (h, i, 0, 0)))]
    outs = [(_sds((r, HW), F32), tile(0)), (_sds((r, HW), MXU_DTYPE), tile(0)),
            (_sds((HEADS, nq, 1, tq), F32), pl.BlockSpec((hg, None, 1, tq), lambda h, i: (h, i, 0, 0)))]
    o, ob, lse = _call(name, body, (gpw, nq), ins, outs, sem=("parallel", "parallel"))
    return o, ob, lse.reshape(HEADS, r)


def _key_tiles_transposed(name, a, group):
    r = a.shape[0]
    tk = ATT_TK

    def body(x_ref, o_ref):
        for h in range(HEADS):
            o_ref[_hs(h), :] = x_ref[:, _hs(h)].astype(F32).T.astype(o_ref.dtype)

    return _call(name, body, (r // tk,),
                 [(a, pl.BlockSpec((tk, HW), lambda j: (j, group)))],
                 [(_sds((r // tk, HW, tk), a.dtype), pl.BlockSpec((None, HW, tk), lambda j: (j, 0, 0)))],
                 sem=("parallel",))[0]


def _attn_delta(name, do_b, o, after=()):
    r = do_b.shape[0]
    br = ROW_BLOCK

    def body(do_ref, o_ref, d_ref):
        lane = _lane_iota((br, HP))
        d = jnp.zeros((br, HP), F32)
        for h in range(HEADS):
            dh = do_ref[:, _hs(h)].astype(F32)
            d = jnp.where(lane == h, jnp.sum(dh * o_ref[:, _hs(h)], axis=1, keepdims=True), d)
        d_ref[...] = d.T[0:HEADS, :]

    wide = _rows(br, HW)
    return _call(name, body, (r // br,), [(do_b, wide), (o, wide)],
                 [(_sds((HEADS, r), F32), pl.BlockSpec((HEADS, br), lambda i: (0, i)))],
                 sem=("parallel",), after=after)[0]


def _attn_bwd(name, q, k, v, do_b, lse_t, delta_t, cum_b=None, cum_t=None, out_dtype=F32):
    (qa, qg), (ka, kg), (va, vg) = q, k, v
    r = qa.shape[0]
    tq, tk = ATT_TQ, ATT_TK
    nq, nk = r // tq, r // tk
    bias = cum_b is not None

    def body(*refs):
        if bias:
            (q_ref, k_ref, v_ref, do_ref, lse_ref, dl_ref, cb_ref, ct_ref,
             dq_ref, dk_ref, dv_ref, dcq_ref, dck_ref, dqt_ref) = refs
        else:
            q_ref, k_ref, v_ref, do_ref, lse_ref, dl_ref, dq_ref, dk_ref, dv_ref, dqt_ref = refs
        j = pl.program_id(1)

        @pl.when(j == 0)
        def _():
            dqt_ref[...] = jnp.zeros_like(dqt_ref)
            if bias:
                dcq_ref[...] = jnp.zeros_like(dcq_ref)

        kts = [k_ref[:, _hs(hh)] for hh in range(hg)]
        vts = [v_ref[:, _hs(hh)] for hh in range(hg)]
        k_trs = [kt.astype(F32).T.astype(kt.dtype) for kt in kts]
        cks = [cb_ref[:, hh * HP:hh * HP + 1] for hh in range(hg)] if bias else None
        diff = lax.broadcasted_iota(jnp.int32, (tk, tq), 0) - lax.broadcasted_iota(jnp.int32, (tk, tq), 1)

        def step(i, carry, masked):
            rows = pl.ds(pl.multiple_of(i * tq, tq), tq)
            out = []
            for hh in range(hg):
                dk_acc, dv_acc, dck_acc = carry[hh]
                qt = q_ref[rows, _hs(hh)]
                dot = do_ref[rows, _hs(hh)]
                s = lax.dot_general(kts[hh], qt, NT, preferred_element_type=F32)
                if bias:
                    s = s + (ct_ref[hh, i] - cks[hh])
                if masked:
                    s = jnp.where(diff <= i * tq - j * tk, s, NEG_INF)
                p = jnp.exp(s - lse_ref[hh, i])
                dp = lax.dot_general(vts[hh], dot, NT, preferred_element_type=F32)
                ds = p * (dp - dl_ref[hh, i])
                pb = p.astype(dot.dtype)
                dsb = ds.astype(qt.dtype)
                dv_acc = dv_acc + jnp.dot(pb, dot, preferred_element_type=F32)
                dk_acc = dk_acc + jnp.dot(dsb, qt, preferred_element_type=F32)
                dqt_ref[hh, i] += jnp.dot(k_trs[hh], dsb, preferred_element_type=F32)
                if bias:
                    dcq_ref[hh, i] += jnp.sum(ds, axis=0, keepdims=True)
                    dck_acc = dck_acc - jnp.sum(ds, axis=1, keepdims=True)
                out.append((dk_acc, dv_acc, dck_acc))
            return tuple(out)

        i_first = (j * tk) // tq
        i_clear = jnp.minimum(((j + 1) * tk + tq - 2) // tq, nq)
        carry = tuple((jnp.zeros((tk, HP), F32), jnp.zeros((tk, HP), F32), jnp.zeros((tk, 1), F32)) for _ in range(hg))
        carry = lax.fori_loop(i_first, i_clear, lambda i, c: step(i, c, True), carry)
        carry = lax.fori_loop(i_clear, nq, lambda i, c: step(i, c, False), carry)
        for hh in range(hg):
            dk_acc, dv_acc, dck_acc = carry[hh]
            dk_ref[:, _hs(hh)] = dk_acc.astype(dk_ref.dtype)
            dv_ref[:, _hs(hh)] = dv_acc.astype(dv_ref.dtype)
            if bias:
                dck_ref[:, _hs(hh)] = jnp.broadcast_to(dck_acc, (tk, HP))

        @pl.when(j == nk - 1)
        def _():
            for hh in range(hg):
                for i in range(nq):
                    dq_ref[i * tq:(i + 1) * tq, _hs(hh)] = dqt_ref[hh, i].T.astype(dq_ref.dtype)

    hg = ATT_HEADS
    w = hg * HP
    gpw = HW // w
    res = lambda g: pl.BlockSpec((r, w), lambda h, j: (0, g * gpw + h))
    tile = lambda g: pl.BlockSpec((tk, w), lambda h, j: (j, g * gpw + h))
    rowv = pl.BlockSpec((hg, nq, 1, tq), lambda h, j: (h, 0, 0, 0))
    as_rows = lambda a: a.reshape(HEADS, nq, 1, tq)
    ins = [(qa, res(qg)), (ka, tile(kg)), (va, tile(vg)), (do_b, res(0)), (as_rows(lse_t), rowv), (as_rows(delta_t), rowv)]
    outs = [(_sds((r, HW), out_dtype), res(0)), (_sds((r, HW), out_dtype), tile(0)), (_sds((r, HW), out_dtype), tile(0))]
    if bias:
        ins += [(cum_b, tile(0)), (as_rows(cum_t), rowv)]
        outs += [(_sds((HEADS, nq, 1, tq), F32), rowv), (_sds((r, HW), F32), tile(0))]
    res_out = _call(name, body, (gpw, nk), ins, outs, scratch=[pltpu.VMEM((hg, nq, HP, tq), F32)],
                    sem=("parallel", "arbitrary"))
    if bias:
        dq, dk, dv, dcq, dck = res_out
        return dq, dk, dv, dcq.reshape(HEADS, r), dck
    return res_out


MESH_ID = pl.DeviceIdType.MESH
ANY = pl.BlockSpec(memory_space=pl.ANY)


N_GATHER_COPIES = 8


def _allgather(name, shards):
    n = len(shards)

    def body(*refs):
        x_refs, out_refs = refs[:n], refs[n:2 * n]
        send_sems, recv_sems, local_sems = refs[2 * n:]
        x, y, c = lax.axis_index("x"), lax.axis_index("y"), lax.axis_index("c")
        me, sibling = (x, y, c), (x, y, 1 - c)
        xn, yn, dg = (1 - x, y, c), (x, 1 - y, c), (1 - x, 1 - y, c)
        other = lambda dev: (dev[0], dev[1], 1 - c)

        def slot(ti, dev, half=None):
            ref = out_refs[ti].at[4 * dev[0] + 2 * dev[1] + dev[2]]
            if half is None:
                return ref
            rows = shards[ti].shape[0] // 2
            return ref.at[pl.ds(half * rows, rows)]

        def copy(ti, k, block, to, half=None, src=None):
            return pltpu.make_async_remote_copy(
                src_ref=slot(ti, block, half) if src is None else src, dst_ref=slot(ti, block, half),
                send_sem=send_sems.at[ti, k], recv_sem=recv_sems.at[ti, k], device_id=to, device_id_type=MESH_ID)

        mine = [pltpu.make_async_copy(x_refs[ti], slot(ti, me), local_sems.at[ti]) for ti in range(n)]
        for cp in mine:
            cp.start()
        started = []

        def go(cp):
            cp.start()
            started.append(cp)

        for ti in range(n):
            go(copy(ti, 0, me, sibling, src=x_refs[ti]))
            go(copy(ti, 1, me, xn, src=x_refs[ti]))
            go(copy(ti, 2, me, yn, src=x_refs[ti]))
        for ti in range(n):
            copy(ti, 1, xn, me).wait_recv()
            go(copy(ti, 3, xn, yn, half=0))
            go(copy(ti, 5, xn, sibling))
            copy(ti, 2, yn, me).wait_recv()
            go(copy(ti, 4, yn, xn, half=1))
            go(copy(ti, 6, yn, sibling))
        for ti in range(n):
            copy(ti, 3, dg, me, half=0).wait_recv()
            copy(ti, 4, dg, me, half=1).wait_recv()
            go(copy(ti, 7, dg, sibling))
        for ti in range(n):
            copy(ti, 0, sibling, me).wait_recv()
            for k, dev in ((5, xn), (6, yn), (7, dg)):
                copy(ti, k, other(dev), me).wait_recv()
        for cp in started:
            cp.wait_send()
        for cp in mine:
            cp.wait()

    sems = pltpu.SemaphoreType.DMA((n, N_GATHER_COPIES))
    return pl.pallas_call(
        body, name=name, out_shape=[_sds((N_DEV,) + s.shape, s.dtype) for s in shards],
        in_specs=[ANY] * n, out_specs=[ANY] * n,
        scratch_shapes=[sems, sems, pltpu.SemaphoreType.DMA((n,))],
    )(*shards)


HBM = pl.BlockSpec(memory_space=pltpu.HBM)
SEM = pl.BlockSpec(memory_space=pltpu.SEMAPHORE)
EFFECT = pltpu.SideEffectType.DATAFLOW_SIDE_EFFECTING
N_PEER = N_DEV - 1


def _my_id():
    return 4 * lax.axis_index("x") + 2 * lax.axis_index("y") + lax.axis_index("c")


def _peers():
    x, y, c = lax.axis_index("x"), lax.axis_index("y"), lax.axis_index("c")
    out = []
    for k in range(1, N_DEV):
        px, py, pc = (1 - x if k & 4 else x, 1 - y if k & 2 else y, 1 - c if k & 1 else c)
        out.append(((px, py, pc), 4 * px + 2 * py + pc))
    return out


def _push_copies(src_refs, land_refs, send_sems, recv_sems, scatter, landing):
    me = _my_id()
    out = []
    for ti, (src, land) in enumerate(zip(src_refs, land_refs)):
        for k, (dev, pid) in enumerate(_peers()):
            out.append(pltpu.make_async_remote_copy(
                src_ref=src.at[pid] if scatter else src, dst_ref=land.at[pid if landing else me],
                send_sem=send_sems.at[ti * N_PEER + k], recv_sem=recv_sems.at[ti * N_PEER + k],
                device_id=dev, device_id_type=MESH_ID))
    return out


def _push_start(name, groups, scatter, after=None):
    sizes = [len(g) for g in groups]
    srcs = [a for g in groups for a in g]
    n = len(srcs)
    slot = lambda s: s.shape[1:] if scatter else s.shape
    lands = [lax.empty((N_DEV,) + slot(s), s.dtype) for s in srcs]
    n_after = 0 if after is None else 1
    n_grp = len(groups)

    def body(*refs):
        src_refs, land_refs = refs[:n], refs[n:2 * n]
        sems = refs[2 * n + n_after:2 * n + n_after + 2 * n_grp]
        token = refs[-1]
        lo = 0
        for gi, sz in enumerate(sizes):
            for cp in _push_copies(src_refs[lo:lo + sz], land_refs[lo:lo + sz], sems[2 * gi], sems[2 * gi + 1], scatter, False):
                cp.start()
            lo += sz
        token[...] = jnp.zeros_like(token)

    hbm = lambda a: pltpu.with_memory_space_constraint(a, pltpu.HBM)
    operands = [hbm(a) for a in srcs + lands] + ([after] if n_after else [])
    sem_shapes = [pltpu.SemaphoreType.DMA((sz * N_PEER,)) for sz in sizes for _ in range(2)]
    res = pl.pallas_call(
        body, name=name,
        out_shape=sem_shapes + [pltpu.HBM(a.shape, a.dtype) for a in srcs + lands] + [_sds((8, 128), F32)],
        in_specs=[HBM] * (2 * n) + [ANY] * n_after,
        out_specs=[SEM] * (2 * n_grp) + [HBM] * (2 * n) + [pl.BlockSpec(memory_space=pltpu.VMEM)],
        input_output_aliases={i: 2 * n_grp + i for i in range(2 * n)},
        compiler_params=pltpu.CompilerParams(has_side_effects=EFFECT),
    )(*operands)
    thru = res[2 * n_grp:2 * n_grp + 2 * n]
    handles, lo = [], 0
    for gi, sz in enumerate(sizes):
        handles.append((res[2 * gi], res[2 * gi + 1], list(thru[lo:lo + sz]), list(thru[n + lo:n + lo + sz]), scatter))
        lo += sz
    return handles, res[-1]


def _push_wait(name, handle, after):
    send_sems, recv_sems, srcs, lands, scatter = handle
    n = len(srcs)

    def body(*refs):
        src_refs, land_refs = refs[:n], refs[n:2 * n]
        s_sems, r_sems = refs[2 * n], refs[2 * n + 1]
        for cp in _push_copies(src_refs, land_refs, s_sems, r_sems, scatter, True):
            cp.wait_send()
            cp.wait_recv()

    res = pl.pallas_call(
        body, name=name,
        out_shape=[pltpu.HBM(a.shape, a.dtype) for a in srcs + lands],
        in_specs=[HBM] * (2 * n) + [SEM, SEM, ANY], out_specs=[HBM] * (2 * n),
        input_output_aliases={i: i for i in range(2 * n)},
        compiler_params=pltpu.CompilerParams(has_side_effects=EFFECT),
    )(*srcs, *lands, send_sems, recv_sems, after)
    return list(res[n:])


def _adamw(name, parts, w, m, v, own=None):
    r, c = w.shape
    br = _pick(r, 256, 16)
    has_own = own is not None

    def body(*refs):
        if has_own:
            p_ref, own_ref, w_ref, m_ref, v_ref, g_ref, d_ref, nm_ref, nv_ref = refs
            me = _my_id()
            mine = own_ref[...].astype(F32)
        else:
            p_ref, w_ref, m_ref, v_ref, g_ref, d_ref, nm_ref, nv_ref = refs
        g = None
        for k in range(N_DEV):
            t = p_ref[k].astype(F32)
            if has_own:
                t = jnp.where(me == k, mine, t)
            g = t if g is None else g + t
        mm = ADAM_B1 * m_ref[...] + (1.0 - ADAM_B1) * g
        vv = ADAM_B2 * v_ref[...] + (1.0 - ADAM_B2) * (g * g)
        m_hat = mm / (1.0 - ADAM_B1 ** ADAM_STEP)
        v_hat = vv / (1.0 - ADAM_B2 ** ADAM_STEP)
        g_ref[...] = g
        d_ref[...] = -ADAM_LR * (m_hat / (jnp.sqrt(v_hat) + ADAM_EPS) + ADAM_WD * w_ref[...])
        nm_ref[...] = mm
        nv_ref[...] = vv

    spec = _rows(br, c)
    out = (_sds((r, c), F32), spec)
    ins = [(parts, pl.BlockSpec((N_DEV, br, c), lambda i: (0, i, 0)))] + ([(own, spec)] if has_own else [])
    return _call(name, body, (r // br,), ins + [(w, spec), (m, spec), (v, spec)], [out] * 4, sem=("parallel",))


def _pad_head_cols(w, d):
    k = w.shape[0]
    return jnp.pad(w.reshape(k, HEADS, d), ((0, 0), (0, 0), (0, HP - d))).reshape(k, HW)


def _unpad_head_cols(wp, d):
    k = wp.shape[0]
    return wp.reshape(k, HEADS, HP)[:, :, :d].reshape(k, HEADS * d)


def _pad_head_rows(w, d):
    n = w.shape[1]
    return jnp.pad(w.reshape(HEADS, d, n), ((0, 0), (0, HP - d), (0, 0))).reshape(HW, n)


def _unpad_head_rows(wp, d):
    n = wp.shape[1]
    return wp.reshape(HEADS, HP, n)[:, :d, :].reshape(HEADS * d, n)


def _w_in_runs():
    nat = {}
    o = 0
    for nm, wd in (("q", Q_RANK), ("kv", KV_RANK), ("kr", ROPE), ("fq", FOX_W), ("fk", FOX_W), ("fv", FOX_W),
                   ("fl", HEADS), ("gate", 2 * D_MODEL)):
        nat[nm] = o
        o += wd
    runs = [(1, R_QLAT, nat["q"], Q_RANK, 1.0), (1, R_KVLAT, nat["kv"], KV_RANK, 1.0),
            (1, R_LAST + LANE_FL, nat["fl"], HEADS, 1.0), (1, R_LAST + LANE_PE, nat["kr"], ROPE, 1.0),
            (1, R_GATE, nat["gate"], 2 * D_MODEL, 1.0)]
    for grp, (nm, sc) in enumerate((("fq", FOX_SCALE), ("fk", 1.0), ("fv", 1.0))):
        runs += [(0, grp * HW + h * HP, nat[nm] + h * FOX_DIM, FOX_DIM, sc) for h in range(HEADS)]
    return runs


def _sharded_runs(runs, shard_cols):
    out = []
    for half, col, ncol, width, sc in runs:
        while width > 0:
            d, local = divmod(ncol, shard_cols)
            wd = min(width, shard_cols - local)
            out.append((half, col, d, local, wd, sc))
            col, ncol, width = col + wd, ncol + wd, width - wd
    return out


def _remap(name, srcs, out_shapes, moves):
    rows = srcs[0].shape[-2]
    br = _pick(rows, 256, 16)
    ns = len(srcs)

    def spec(shape):
        if len(shape) == 2:
            return pl.BlockSpec((br, shape[1]), lambda i: (i, 0))
        return pl.BlockSpec((shape[0], br, shape[2]), lambda i: (0, i, 0))

    def body(*refs):
        s_refs, o_refs = refs[:ns], refs[ns:]
        for o in o_refs:
            o[...] = jnp.zeros_like(o)
        for di, dl, dc, si, sl, sc0, wd, scale in moves:
            v = s_refs[si][:, sc0:sc0 + wd] if sl is None else s_refs[si][sl, :, sc0:sc0 + wd]
            if scale != 1.0:
                v = v * jnp.asarray(scale, v.dtype)
            v = v.astype(o_refs[di].dtype)
            if dl is None:
                o_refs[di][:, dc:dc + wd] = v
            else:
                o_refs[di][dl, :, dc:dc + wd] = v

    return _call(name, body, (rows // br,), [(a, spec(a.shape)) for a in srcs],
                 [(_sds(shape, dt), spec(shape)) for shape, dt in out_shapes], sem=("parallel",))


def _w_in_from_shards(g3):
    n, rows, c = g3.shape
    moves = [(half, None, col, 0, d, local, wd, sc) for half, col, d, local, wd, sc in _sharded_runs(_w_in_runs(), c)]
    return _remap("w_in_repack", [g3], [((rows, F_W), g3.dtype), ((rows, R_W), g3.dtype)], moves)


def _w_in_grad_to_shards(d_fused, d_rest, n, c):
    rows = d_fused.shape[0]
    moves = [(0, d, local, half, None, col, wd, sc) for half, col, d, local, wd, sc in _sharded_runs(_w_in_runs(), c)]
    return _remap("w_in_grad_unpack", [d_fused, d_rest], [((n, rows, c), d_fused.dtype)], moves)[0]


def _rows_from_shards(name, land, own):
    n, rows, c = land.shape

    def body(land_ref, own_ref, o_ref):
        o_ref[...] = jnp.where(_my_id() == pl.program_id(0), own_ref[...], land_ref[...])

    return _call(name, body, (n,),
                 [(land, pl.BlockSpec((None, rows, c), lambda d: (d, 0, 0))), (own, _whole((rows, c)))],
                 [(_sds((n * rows, c), land.dtype), pl.BlockSpec((rows, c), lambda d: (d, 0)))], sem=("parallel",))[0]


def _cols_from_shards(name, land, own):
    n, rows, c = land.shape
    br = _pick(rows, 256, 16)

    def body(land_ref, own_ref, o_ref):
        me = _my_id()
        for d in range(n):
            o_ref[:, c * d:c * (d + 1)] = jnp.where(me == d, own_ref[...], land_ref[d])

    return _call(name, body, (rows // br,),
                 [(land, pl.BlockSpec((n, br, c), lambda i: (0, i, 0))), (own, _rows(br, c))],
                 [(_sds((rows, n * c), land.dtype), _rows(br, n * c))], sem=("parallel",))[0]


def _cols_to_shards(name, full, n):
    rows, nc = full.shape
    c = nc // n
    return _remap(name, [full], [((n, rows, c), full.dtype)], [(0, d, 0, 0, None, c * d, c, 1.0) for d in range(n)])[0]


def _split_w_kv(w):
    k = w.shape[0]
    w3 = w.reshape(k, HEADS, NOPE + V_DIM)
    padl = lambda a: jnp.pad(a, ((0, 0), (0, 0), (0, HP - a.shape[-1]))).reshape(k, HW)
    return padl(w3[..., :NOPE]), padl(w3[..., NOPE:])


def _merge_w_kv(wk, wv):
    k = wk.shape[0]
    return jnp.concatenate([wk.reshape(k, HEADS, HP)[..., :NOPE], wv.reshape(k, HEADS, HP)[..., :V_DIM]],
                           axis=-1).reshape(k, HEADS * (NOPE + V_DIM))


class _NoComm:
    first_token = ()

    def late_weights(self, group, after):
        return {}

    def send(self, name, grads):
        return ()


def _local_step(x, tgt, p, comm=_NoComm()):
    seq = x.shape[0]
    r = -(-(N_META + seq) // ROW_ALIGN) * ROW_ALIGN
    cd = MXU_DTYPE
    p = dict(p)

    w_f, w_r = p["w_in"]

    pos = jnp.arange(r, dtype=F32)
    inv_freq = ROPE_THETA ** (-jnp.arange(HALF, dtype=F32) / HALF)
    ang = pos[:, None] * inv_freq[None, :]
    cos_t = jnp.tile(jnp.cos(ang), (1, HP // HALF))
    sin_t = jnp.tile(jnp.sin(ang), (1, HP // HALF))
    bf_row = jnp.zeros((1, HP), F32).at[0, LANE_FL:LANE_FL + HEADS].set(p["b_forget"])

    h0, h0b = _ln_emb_fwd(x, p["meta_tokens"], p["ln_emb_g"], p["ln_emb_b"], r, after=comm.first_token)
    proj_f = _matmul("in_proj_f", h0b, w_f, out_dtype=cd)
    proj_r = _matmul("in_proj_r", h0b, w_r)
    latent_gains = (p["q_norm_g"], p["kv_norm_g"])
    ql, kvl = _latent_norm_fwd(proj_r, latent_gains)
    p.update(comm.late_weights("qkv", ql))
    w_q = _pad_head_cols(p["w_q_up"], QK_DIM)
    w_kv = jnp.concatenate(_split_w_kv(p["w_kv_up"]), axis=1)
    q_raw = _matmul("q_up", ql, w_q)
    kv = _matmul("kv_up", kvl, w_kv, out_dtype=cd)
    q_mla, k_mla = _rope_fwd(q_raw, kv, proj_r, cos_t, sin_t)
    o_mla, o_mla_b, lse_mla = _attn_fwd("mla_fwd", (q_mla, 0), (k_mla, 0), (kv, 1))

    cum, cum_t = _forget_fwd(proj_r, bf_row)
    o_fox, o_fox_b, lse_fox = _attn_fwd("fox_fwd", (proj_f, 0), (proj_f, 1), (proj_f, 2), cum, cum_t)

    p.update(comm.late_weights("mix", o_fox_b))
    w_bm = _pad_head_rows(p["w_branch_mla"], V_DIM)
    w_bf = _pad_head_rows(p["w_branch_fox"], FOX_DIM)
    bm = _matmul("branch_mla", o_mla_b, w_bm, out_dtype=cd)
    bfx = _matmul("branch_fox", o_fox_b, w_bf, out_dtype=cd)
    merged = _gate_fwd(proj_r, p["b_gate"], bm, bfx)
    mix = _matmul("out_proj", merged, p["w_out"])
    h1, h1b = _ln_fwd("ln_mix_fwd", h0, mix, p["ln_mix_g"], p["ln_mix_b"])
    p.update(comm.late_weights("ffn", h1b))
    up = _matmul("ffn_up", h1b, p["w_ffn_up"], out_dtype=cd)
    act = _glu_fwd(up, p["conv_w"], p["conv_b"])
    f = _matmul("ffn_down", act, p["w_ffn_down"])
    loss = _ln_ffn_loss(h1, f, tgt, p["ln_ffn_g"], p["ln_ffn_b"])

    g = {}
    dz2, dz2b, g["ln_ffn_g"], g["ln_ffn_b"] = _ln_ffn_bwd(h1, f, tgt, p["ln_ffn_g"], p["ln_ffn_b"])
    d_act = _matmul("ffn_down_dx", dz2b, p["w_ffn_down"], tb=True, out_dtype=cd)
    g["w_ffn_down"] = _matmul("ffn_down_dw", act, dz2b, ta=True, out_dtype=cd)
    d_up, dcw, g["conv_b"] = _glu_bwd(up, p["conv_w"], p["conv_b"], d_act)
    g["conv_w"] = dcw[:3]
    dh1 = _matmul("ffn_up_dx", d_up, p["w_ffn_up"], tb=True, addend=dz2, alpha=ALPHA)
    g["w_ffn_up"] = _matmul("ffn_up_dw", h1b, d_up, ta=True, out_dtype=cd)
    sent = comm.send("ffn", {n: g[n] for n in ("w_ffn_down", "w_ffn_up", "conv_w")})
    dz1, dz1b, g["ln_mix_g"], g["ln_mix_b"] = _ln_bwd("ln_mix_bwd", h0, mix, dh1, p["ln_mix_g"], after=sent)
    dmerged = _matmul("out_proj_dx", dz1b, p["w_out"], tb=True, out_dtype=cd)
    g["w_out"] = _matmul("out_proj_dw", merged, dz1b, ta=True, out_dtype=cd)
    d_bm, d_bf, d_gl, g["b_gate"] = _gate_bwd(proj_r, p["b_gate"], bm, bfx, dmerged)
    do_mla_b = _matmul("branch_mla_dx", d_bm, w_bm, tb=True, out_dtype=cd)
    g["w_branch_mla"] = _unpad_head_rows(_matmul("branch_mla_dw", o_mla_b, d_bm, ta=True, out_dtype=cd), V_DIM)
    do_fox_b = _matmul("branch_fox_dx", d_bf, w_bf, tb=True, out_dtype=cd)
    g["w_branch_fox"] = _unpad_head_rows(_matmul("branch_fox_dw", o_fox_b, d_bf, ta=True, out_dtype=cd), FOX_DIM)

    sent = comm.send("mix", {n: g[n] for n in ("w_out", "w_branch_mla", "w_branch_fox")})
    dl_mla = _attn_delta("mla_delta", do_mla_b, o_mla, after=sent)
    dq_m, dk_m, dv_m = _attn_bwd("mla_bwd", (q_mla, 0), (k_mla, 0), (kv, 1), do_mla_b, lse_mla, dl_mla)
    dl_fox = _attn_delta("fox_delta", do_fox_b, o_fox)
    dfq, dfk, dfv, dcq, dck = _attn_bwd("fox_bwd", (proj_f, 0), (proj_f, 1), (proj_f, 2), do_fox_b, lse_fox, dl_fox,
                                        cum, cum_t, out_dtype=cd)
    dfl, dbf = _forget_bwd(proj_r, bf_row, dcq, dck)
    g["b_forget"] = dbf[:, LANE_FL:LANE_FL + HEADS]

    dq_b, dkv_b, dlast = _rope_bwd(dq_m, dk_m, dv_m, dfl, cos_t, sin_t)
    d_ql = _matmul("q_up_dx", dq_b, w_q, tb=True)
    d_kvl = _matmul("kv_up_dx", dkv_b, w_kv, tb=True)
    d_qlat, d_kvlat, g["q_norm_g"], g["kv_norm_g"] = _latent_norm_bwd(proj_r, (d_ql, d_kvl), latent_gains)
    side_by_side = lambda parts, cols: [(0, None, c0, si, None, 0, a.shape[1], 1.0) for si, (a, c0) in enumerate(zip(parts, cols))]
    dproj_f = _remap("dproj_f_pack", [dfq, dfk, dfv], [((r, F_W), cd)], side_by_side([dfq, dfk, dfv], (0, HW, 2 * HW)))[0]
    rest_parts = [d_qlat, d_kvlat, dlast, d_gl]
    dproj_r = _remap("dproj_r_pack", rest_parts, [((r, R_W), cd)],
                     side_by_side(rest_parts, (R_QLAT, R_KVLAT, R_LAST, R_GATE)))[0]
    g["w_in"] = (_matmul("in_proj_f_dw", h0b, dproj_f, ta=True, out_dtype=cd),
                 _matmul("in_proj_r_dw", h0b, dproj_r, ta=True, out_dtype=cd))
    sent = comm.send("in", {"w_in": g["w_in"]})
    dh0 = _matmul("in_proj_f_dx", dproj_f, w_f, tb=True, addend=dz1, alpha=ALPHA, after=sent)
    g["w_q_up"] = _unpad_head_cols(_matmul("q_up_dw", ql, dq_b, ta=True, out_dtype=cd, after=sent), QK_DIM)
    dw_kv = _matmul("kv_up_dw", kvl, dkv_b, ta=True, out_dtype=cd, after=sent)
    g["w_kv_up"] = _merge_w_kv(dw_kv[:, :HW], dw_kv[:, HW:])
    sent = comm.send("qkv", {n: g[n] for n in ("w_q_up", "w_kv_up")})
    dh0 = _matmul("in_proj_r_dx", dproj_r, w_r, tb=True, addend=dh0, after=sent)
    grad_x, d_meta, g["ln_emb_g"], g["ln_emb_b"] = _ln_emb_bwd(x, p["meta_tokens"], dh0, p["ln_emb_g"])
    return loss, grad_x, d_meta, g


BIG = (("w_in", 1), ("w_q_up", 1), ("w_kv_up", 1), ("w_branch_mla", 1), ("w_branch_fox", 1), ("w_out", 0),
       ("w_ffn_up", 1), ("w_ffn_down", 0))
SMALL_SHARDED = (("meta_tokens", 1), ("conv_w", 1))
EARLY = ("w_in", "meta_tokens")
LATE = {"qkv": ("w_q_up", "w_kv_up", "conv_w"),
        "mix": ("w_branch_mla", "w_branch_fox", "w_out"),
        "ffn": ("w_ffn_up", "w_ffn_down")}
REPLICATED = ("ln_emb_g", "ln_emb_b", "b_gate", "b_forget", "q_norm_g", "kv_norm_g", "ln_mix_g", "ln_mix_b",
              "conv_b", "ln_ffn_g", "ln_ffn_b")
PACK_COLS = 1024


def _pack(flat_list):
    cat = jnp.concatenate(flat_list)
    n = cat.shape[0]
    rows = -(-n // (8 * PACK_COLS)) * 8
    return jnp.pad(cat, (0, rows * PACK_COLS - n)).reshape(rows, PACK_COLS)


def _gathered_full(g3, axis):
    n, r, c = g3.shape
    if axis == 0:
        return g3.reshape(n * r, c)
    return g3.transpose(1, 0, 2).reshape(r, n * c)


def _shard_major(full, axis):
    r, c = full.shape
    if axis == 0:
        return full.reshape(N_DEV, r // N_DEV, c)
    return full.reshape(r, N_DEV, c // N_DEV).transpose(1, 0, 2)


def kernel(x, meta_tokens, ln_emb_g, ln_emb_b, w_in, b_gate, b_forget, q_norm_g, w_q_up, kv_norm_g, w_kv_up, w_branch_mla, w_branch_fox, w_out, ln_mix_g, ln_mix_b, w_ffn_up, conv_w, conv_b, w_ffn_down, ln_ffn_g, ln_ffn_b, loss_target, m_meta_tokens, m_ln_emb_g, m_ln_emb_b, m_w_in, m_b_gate, m_b_forget, m_q_norm_g, m_w_q_up, m_kv_norm_g, m_w_kv_up, m_w_branch_mla, m_w_branch_fox, m_w_out, m_ln_mix_g, m_ln_mix_b, m_w_ffn_up, m_conv_w, m_conv_b, m_w_ffn_down, m_ln_ffn_g, m_ln_ffn_b, v_meta_tokens, v_ln_emb_g, v_ln_emb_b, v_w_in, v_b_gate, v_b_forget, v_q_norm_g, v_w_q_up, v_kv_norm_g, v_w_kv_up, v_w_branch_mla, v_w_branch_fox, v_w_out, v_ln_mix_g, v_ln_mix_b, v_w_ffn_up, v_conv_w, v_conv_b, v_w_ffn_down, v_ln_ffn_g, v_ln_ffn_b):
    names = ("meta_tokens", "ln_emb_g", "ln_emb_b", "w_in", "b_gate", "b_forget", "q_norm_g", "w_q_up", "kv_norm_g",
             "w_kv_up", "w_branch_mla", "w_branch_fox", "w_out", "ln_mix_g", "ln_mix_b", "w_ffn_up", "conv_w", "conv_b",
             "w_ffn_down", "ln_ffn_g", "ln_ffn_b")
    w_args = (meta_tokens, ln_emb_g, ln_emb_b, w_in, b_gate, b_forget, q_norm_g, w_q_up, kv_norm_g, w_kv_up,
              w_branch_mla, w_branch_fox, w_out, ln_mix_g, ln_mix_b, w_ffn_up, conv_w, conv_b, w_ffn_down, ln_ffn_g, ln_ffn_b)
    m_args = (m_meta_tokens, m_ln_emb_g, m_ln_emb_b, m_w_in, m_b_gate, m_b_forget, m_q_norm_g, m_w_q_up, m_kv_norm_g,
              m_w_kv_up, m_w_branch_mla, m_w_branch_fox, m_w_out, m_ln_mix_g, m_ln_mix_b, m_w_ffn_up, m_conv_w, m_conv_b,
              m_w_ffn_down, m_ln_ffn_g, m_ln_ffn_b)
    v_args = (v_meta_tokens, v_ln_emb_g, v_ln_emb_b, v_w_in, v_b_gate, v_b_forget, v_q_norm_g, v_w_q_up, v_kv_norm_g,
              v_w_kv_up, v_w_branch_mla, v_w_branch_fox, v_w_out, v_ln_mix_g, v_ln_mix_b, v_w_ffn_up, v_conv_w, v_conv_b,
              v_w_ffn_down, v_ln_ffn_g, v_ln_ffn_b)
    as2d = lambda a: a.reshape((-1, a.shape[-1])) if a.ndim != 1 else a.reshape(1, -1)
    w = {n: as2d(a) for n, a in zip(names, w_args)}
    m = {n: as2d(a) for n, a in zip(names, m_args)}
    v = {n: as2d(a) for n, a in zip(names, v_args)}
    out_shape = {n: a.shape for n, a in zip(names, w_args)}

    axis_of = dict(BIG + SMALL_SHARDED)
    big = set(n for n, _ in BIG)
    wire = lambda n, a: a.astype(MXU_DTYPE) if n in big else a
    my_id = _my_id()

    early = _allgather("gather_early", [wire(n, w[n]) for n in EARLY])
    p = {n: _gathered_full(g3, axis_of[n]) for n, g3 in zip(EARLY, early) if n != "w_in"}
    p["w_in"] = _w_in_from_shards(early[EARLY.index("w_in")])
    for n in REPLICATED:
        p[n] = w[n].reshape(-1)
    late_src = [[wire(n, w[n]) for n in members] for members in LATE.values()]
    late_handles, late_token = _push_start("gather_late_start", late_src, False, after=early[0])
    late = {group: (members, src, handle)
            for (group, members), src, handle in zip(LATE.items(), late_src, late_handles)}
    sent = {}

    class Comm:
        first_token = (late_token,)

        def late_weights(self, group, after):
            members, src, handle = late[group]
            lands = _push_wait("gather_" + group + "_wait", handle, after)
            out = {}
            for n, own, land in zip(members, src, lands):
                if own.shape[0] % 16:
                    out[n] = _gathered_full(lax.dynamic_update_index_in_dim(land, own, my_id, 0), axis_of[n])
                elif axis_of[n] == 1:
                    out[n] = _cols_from_shards(n + "_repack", land, own)
                else:
                    out[n] = _rows_from_shards(n + "_repack", land, own)
            return out

        def send(self, name, grads):
            names_ = tuple(grads)
            parts = []
            for n in names_:
                if n == "w_in":
                    parts.append(_w_in_grad_to_shards(*grads[n], N_DEV, w[n].shape[1]))
                elif n == "w_ffn_up":
                    parts.append(_cols_to_shards(n + "_grad_unpack", grads[n], N_DEV))
                else:
                    parts.append(_shard_major(grads[n], axis_of[n]).astype(MXU_DTYPE))
            (handle,), token = _push_start("send_" + name + "_start", [parts], True)
            sent[name] = (names_, parts, handle)
            return (token,)

    loss_part, grad_x, d_meta, g = _local_step(x[0], loss_target[0], p, Comm())
    grad_x = grad_x[None]

    small = _pack([d_meta.reshape(-1)] + [g[n].reshape(-1) for n in REPLICATED] + [loss_part.reshape(-1)])
    (small_handle,), small_token = _push_start("send_small_start", [[small]], False)

    res = {}
    prev = small_token
    for name, (names_, parts, handle) in sent.items():
        lands = _push_wait("send_" + name + "_wait", handle, prev)
        for n, part, land in zip(names_, parts, lands):
            own = lax.dynamic_index_in_dim(part, my_id, axis=0, keepdims=False)
            res[n] = _adamw("adamw_" + n, land, w[n], m[n], v[n], own=own)
            prev = res[n][0]
    small_all = _push_wait("send_small_wait", small_handle, prev)[0]
    head = jnp.zeros((d_meta.size,), F32)
    rep_w = _pack([head] + [w[n].reshape(-1) for n in REPLICATED])
    rep_m = _pack([head] + [m[n].reshape(-1) for n in REPLICATED])
    rep_v = _pack([head] + [v[n].reshape(-1) for n in REPLICATED])
    rep_res = _adamw("adamw_replicated", small_all, rep_w, rep_m, rep_v, own=small)
    off = d_meta.size
    for n in REPLICATED:
        sz = w[n].size
        res[n] = tuple(a.reshape(-1)[off:off + sz] for a in rep_res)
        off += sz
    loss = rep_res[0].reshape(-1)[off]
    cols = w["meta_tokens"].shape[1]
    meta_rows = lambda a: a.reshape(a.shape[:-2] + (-1,))[..., :d_meta.size].reshape(a.shape[:-2] + d_meta.shape)
    my_cols = lambda a: lax.dynamic_slice_in_dim(a, my_id * cols, cols, axis=a.ndim - 1)
    res["meta_tokens"] = _adamw("adamw_meta_tokens", my_cols(meta_rows(small_all)), w["meta_tokens"],
                                m["meta_tokens"], v["meta_tokens"], own=my_cols(d_meta))

    outs = [loss, grad_x]
    for idx in range(4):
        outs += [res[n][idx].reshape(out_shape[n]) for n in names]
    return tuple(outs)
```

```python
import jax
import jax.numpy as jnp
from jax import lax
from jax.experimental import pallas as pl
from jax.experimental.pallas import tpu as pltpu

F32 = jnp.float32
BF16 = jnp.bfloat16
MXU_DTYPE = BF16

N_DEV = 8
N_META = 16
D_MODEL = 1024
HEADS = 8
Q_RANK = 384
KV_RANK = 128
NOPE = 64
ROPE = 32
HALF = ROPE // 2
QK_DIM = NOPE + ROPE
V_DIM = 64
FOX_DIM = 64
FOX_W = HEADS * FOX_DIM
D_FF = 2816
ROPE_THETA = 10000.0
LN_EPS = 1e-5
RMS_EPS = 1e-6
ALPHA = 2.0 ** 0.25
MLA_SCALE = QK_DIM ** -0.5
FOX_SCALE = FOX_DIM ** -0.5
NEG_INF = -1e30

HP = 128
HW = HEADS * HP
F_W = 3 * HW
R_QLAT = 0
R_KVLAT = Q_RANK
R_LAST = R_KVLAT + KV_RANK
R_GATE = D_MODEL
R_W = R_GATE + 2 * D_MODEL
LANE_FL = 0
LANE_PE = NOPE

ADAM_LR = 0.001
ADAM_B1 = 0.9
ADAM_B2 = 0.999
ADAM_EPS = 1e-08
ADAM_WD = 0.01
ADAM_STEP = 10

ROW_BLOCK = 256
ATT_TQ = 768
ATT_TK = 768
ATT_HEADS = 2
ROW_ALIGN = 768
MM_BLOCK_CAP = 1408
VMEM_LIMIT = 56 * 1024 * 1024
HIGHEST = lax.Precision.HIGHEST
NT = (((1,), (1,)), ((), ()))
TN = (((0,), (0,)), ((), ()))


def _params(sem=None):
    return pltpu.CompilerParams(dimension_semantics=sem, vmem_limit_bytes=VMEM_LIMIT)


def _call(name, body, grid, ins, outs, scratch=(), sem=None, after=()):
    n_in = len(ins)
    n_tok = len(after)

    def run(*refs):
        body(*refs[:n_in], *refs[n_in + n_tok:])

    tok_spec = pl.BlockSpec((8, 128), lambda *_: (0, 0))
    return pl.pallas_call(
        run, name=name, grid=grid,
        in_specs=[s for _, s in ins] + [tok_spec] * n_tok,
        out_specs=[s for _, s in outs],
        out_shape=[o for o, _ in outs],
        scratch_shapes=list(scratch),
        compiler_params=_params(sem),
    )(*[a for a, _ in ins], *after)


def _sds(shape, dtype):
    return jax.ShapeDtypeStruct(shape, dtype)


def _rows(br, c, cb=0):
    return pl.BlockSpec((br, c), lambda i: (i, cb))


def _whole(shape):
    n = len(shape)
    return pl.BlockSpec(shape, lambda i: (0,) * n)


def _pick(dim, cap, mult):
    best = None
    d = mult
    while d <= min(dim, cap):
        if dim % d == 0:
            best = d
        d += mult
    return best if best is not None else dim


def _hs(h):
    return slice(h * HP, (h + 1) * HP)


def _matmul(name, a, b, *, ta=False, tb=False, out_dtype=F32, addend=None, alpha=1.0, after=()):
    if ta:
        k, m = a.shape
    else:
        m, k = a.shape
    if tb:
        n, k2 = b.shape
    else:
        k2, n = b.shape
    assert k == k2, (name, a.shape, b.shape)
    bm = _pick(m, MM_BLOCK_CAP, 128 if ta else 16)
    bn = _pick(n, MM_BLOCK_CAP, 128)
    bk = _pick(k, MM_BLOCK_CAP, 128 if (not ta or tb) else 16)
    nk = k // bk
    dims = (((0 if ta else 1,), (1 if tb else 0,)), ((), ()))
    has_add = addend is not None

    def body(*refs):
        a_ref, b_ref = refs[:2]
        add_ref = refs[2] if has_add else None
        o_ref = refs[3 if has_add else 2]

        def finish(r):
            if has_add:
                r = r + alpha * add_ref[...]
            o_ref[...] = r.astype(o_ref.dtype)

        part = lax.dot_general(a_ref[...], b_ref[...], dims, preferred_element_type=F32)
        if nk == 1:
            finish(part)
            return
        acc_ref = refs[-1]
        kk = pl.program_id(2)

        @pl.when(kk == 0)
        def _():
            acc_ref[...] = part

        @pl.when(kk > 0)
        def _():
            acc_ref[...] += part

        @pl.when(kk == nk - 1)
        def _():
            finish(acc_ref[...])

    a_spec = pl.BlockSpec((bk, bm), lambda i, j, l: (l, i)) if ta else pl.BlockSpec((bm, bk), lambda i, j, l: (i, l))
    b_spec = pl.BlockSpec((bn, bk), lambda i, j, l: (j, l)) if tb else pl.BlockSpec((bk, bn), lambda i, j, l: (l, j))
    o_spec = pl.BlockSpec((bm, bn), lambda i, j, l: (i, j))
    ins = [(a, a_spec), (b, b_spec)]
    if has_add:
        ins.append((addend, o_spec))
    return _call(name, body, (m // bm, n // bn, nk), ins, [(_sds((m, n), out_dtype), o_spec)],
                 scratch=[pltpu.VMEM((bm, bn), F32)] if nk > 1 else [],
                 sem=("parallel", "parallel", "arbitrary"), after=after)[0]


def _ln_stats(z):
    mu = jnp.mean(z, axis=-1, keepdims=True)
    zc = z - mu
    var = jnp.mean(zc * zc, axis=-1, keepdims=True)
    rstd = lax.rsqrt(var + LN_EPS)
    return zc * rstd, rstd


def _ln_fwd(name, a, res, g, b, after=()):
    r, d = a.shape
    br = ROW_BLOCK
    has_res = res is not None

    def body(*refs):
        if has_res:
            a_ref, r_ref, g_ref, b_ref, y_ref, yb_ref = refs
            z = ALPHA * a_ref[...] + r_ref[...]
        else:
            a_ref, g_ref, b_ref, y_ref, yb_ref = refs
            z = a_ref[...]
        xhat, _ = _ln_stats(z)
        y = xhat * g_ref[...] + b_ref[...]
        y_ref[...] = y
        yb_ref[...] = y.astype(yb_ref.dtype)

    ins = [(a, _rows(br, d))]
    if has_res:
        ins.append((res, _rows(br, d)))
    ins += [(g.reshape(1, d), _whole((1, d))), (b.reshape(1, d), _whole((1, d)))]
    outs = [(_sds((r, d), F32), _rows(br, d)), (_sds((r, d), MXU_DTYPE), _rows(br, d))]
    return _call(name, body, (r // br,), ins, outs, sem=("parallel",), after=after)


def _ln_bwd(name, a, res, dy, g, after=()):
    r, d = a.shape
    br = ROW_BLOCK
    has_res = res is not None

    def body(*refs):
        if has_res:
            a_ref, r_ref, dy_ref, g_ref, dz_ref, dzb_ref, dg_ref, db_ref = refs
            z = ALPHA * a_ref[...] + r_ref[...]
        else:
            a_ref, dy_ref, g_ref, dz_ref, dzb_ref, dg_ref, db_ref = refs
            z = a_ref[...]
        xhat, rstd = _ln_stats(z)
        dyv = dy_ref[...]
        dyg = dyv * g_ref[...]
        m1 = jnp.mean(dyg, axis=-1, keepdims=True)
        m2 = jnp.mean(dyg * xhat, axis=-1, keepdims=True)
        dz = rstd * (dyg - m1 - xhat * m2)
        dz_ref[...] = dz
        dzb_ref[...] = dz.astype(dzb_ref.dtype)

        @pl.when(pl.program_id(0) == 0)
        def _():
            dg_ref[...] = jnp.zeros_like(dg_ref)
            db_ref[...] = jnp.zeros_like(db_ref)

        dg_ref[...] += jnp.sum(dyv * xhat, axis=0, keepdims=True)
        db_ref[...] += jnp.sum(dyv, axis=0, keepdims=True)

    ins = [(a, _rows(br, d))]
    if has_res:
        ins.append((res, _rows(br, d)))
    ins += [(dy, _rows(br, d)), (g.reshape(1, d), _whole((1, d)))]
    outs = [(_sds((r, d), F32), _rows(br, d)), (_sds((r, d), MXU_DTYPE), _rows(br, d)),
            (_sds((1, d), F32), _whole((1, d))), (_sds((1, d), F32), _whole((1, d)))]
    return _call(name, body, (r // br,), ins, outs, sem=("arbitrary",), after=after)


LATENTS = ((R_QLAT // Q_RANK, Q_RANK), (R_KVLAT // KV_RANK, KV_RANK))


def _latent_norm_fwd(proj_r, gains):
    r = proj_r.shape[0]
    br = ROW_BLOCK

    def body(xq_ref, xk_ref, gq_ref, gk_ref, yq_ref, yk_ref):
        for x_ref, g_ref, y_ref in ((xq_ref, gq_ref, yq_ref), (xk_ref, gk_ref, yk_ref)):
            x = x_ref[...]
            rstd = lax.rsqrt(jnp.mean(x * x, axis=-1, keepdims=True) + RMS_EPS)
            y_ref[...] = (x * rstd * g_ref[...]).astype(y_ref.dtype)

    return _call("latent_norm_fwd", body, (r // br,),
                 [(proj_r, _rows(br, wd, cb)) for cb, wd in LATENTS]
                 + [(g.reshape(1, wd), _whole((1, wd))) for g, (_, wd) in zip(gains, LATENTS)],
                 [(_sds((r, wd), MXU_DTYPE), _rows(br, wd)) for _, wd in LATENTS], sem=("parallel",))


def _latent_norm_bwd(proj_r, dys, gains):
    r = proj_r.shape[0]
    br = ROW_BLOCK

    def body(xq_ref, xk_ref, dq_ref, dk_ref, gq_ref, gk_ref, oq_ref, ok_ref, dgq_ref, dgk_ref):
        @pl.when(pl.program_id(0) == 0)
        def _():
            dgq_ref[...] = jnp.zeros_like(dgq_ref)
            dgk_ref[...] = jnp.zeros_like(dgk_ref)

        for x_ref, dy_ref, g_ref, dx_ref, dg_ref in ((xq_ref, dq_ref, gq_ref, oq_ref, dgq_ref),
                                                     (xk_ref, dk_ref, gk_ref, ok_ref, dgk_ref)):
            x = x_ref[...]
            rstd = lax.rsqrt(jnp.mean(x * x, axis=-1, keepdims=True) + RMS_EPS)
            nrm = x * rstd
            dyv = dy_ref[...]
            dyg = dyv * g_ref[...]
            dx_ref[...] = (rstd * (dyg - nrm * jnp.mean(dyg * nrm, axis=-1, keepdims=True))).astype(dx_ref.dtype)
            dg_ref[...] += jnp.sum(dyv * nrm, axis=0, keepdims=True)

    return _call("latent_norm_bwd", body, (r // br,),
                 [(proj_r, _rows(br, wd, cb)) for cb, wd in LATENTS]
                 + [(dy, _rows(br, wd)) for dy, (_, wd) in zip(dys, LATENTS)]
                 + [(g.reshape(1, wd), _whole((1, wd))) for g, (_, wd) in zip(gains, LATENTS)],
                 [(_sds((r, wd), MXU_DTYPE), _rows(br, wd)) for _, wd in LATENTS]
                 + [(_sds((1, wd), F32), _whole((1, wd))) for _, wd in LATENTS], sem=("arbitrary",))


def _lane_iota(shape):
    return lax.broadcasted_iota(jnp.int32, shape, 1)


def _rotary(t, c, s, lane, sign):
    second = pltpu.roll(t, HP - HALF, axis=1)
    first = pltpu.roll(t, HALF, axis=1)
    lo = (lane >= LANE_PE) & (lane < LANE_PE + HALF)
    hi = (lane >= LANE_PE + HALF) & (lane < LANE_PE + ROPE)
    return jnp.where(lo, t * c - sign * second * s, jnp.where(hi, t * c + sign * first * s, t))


def _rope_fwd(q_raw, k_part, proj_r, cos_t, sin_t):
    r = q_raw.shape[0]
    br = ROW_BLOCK

    def body(q_ref, k_ref, t_ref, c_ref, s_ref, qo_ref, ko_ref):
        c = c_ref[...]
        s = s_ref[...]
        lane = _lane_iota((br, HP))
        pe = (lane >= LANE_PE) & (lane < LANE_PE + ROPE)
        kp = jnp.where(pe, _rotary(t_ref[...], c, s, lane, 1.0), 0.0)
        for h in range(HEADS):
            qo_ref[:, _hs(h)] = (_rotary(q_ref[:, _hs(h)], c, s, lane, 1.0) * MLA_SCALE).astype(qo_ref.dtype)
            ko_ref[:, _hs(h)] = (k_ref[:, _hs(h)] + kp).astype(ko_ref.dtype)

    blk = _rows(br, HP)
    wide = _rows(br, HW)
    return _call("rope_fwd", body, (r // br,),
                 [(q_raw, wide), (k_part, wide), (proj_r, _rows(br, HP, R_LAST // HP)), (cos_t, blk), (sin_t, blk)],
                 [(_sds((r, HW), MXU_DTYPE), wide)] * 2, sem=("parallel",))


def _rope_bwd(dq, dk, dv, dfl, cos_t, sin_t):
    r = dq.shape[0]
    br = ROW_BLOCK

    def body(dq_ref, dk_ref, dv_ref, fl_ref, c_ref, s_ref, dqo_ref, dkv_ref, dl_ref):
        c = c_ref[...]
        s = s_ref[...]
        lane = _lane_iota((br, HP))
        pe = (lane >= LANE_PE) & (lane < LANE_PE + ROPE)
        acc = jnp.zeros((br, HP), F32)
        for h in range(HEADS):
            dqo_ref[:, _hs(h)] = (_rotary(dq_ref[:, _hs(h)], c, s, lane, -1.0) * MLA_SCALE).astype(dqo_ref.dtype)
            dkh = dk_ref[:, _hs(h)]
            acc = acc + dkh
            dkv_ref[:, _hs(h)] = dkh.astype(dkv_ref.dtype)
            dkv_ref[:, _hs(HEADS + h)] = dv_ref[:, _hs(h)].astype(dkv_ref.dtype)
        dl_ref[...] = (jnp.where(pe, _rotary(acc, c, s, lane, -1.0), 0.0) + fl_ref[...]).astype(dl_ref.dtype)

    blk = _rows(br, HP)
    wide = _rows(br, HW)
    return _call("rope_bwd", body, (r // br,),
                 [(dq, wide), (dk, wide), (dv, wide), (dfl, blk), (cos_t, blk), (sin_t, blk)],
                 [(_sds((r, HW), MXU_DTYPE), wide), (_sds((r, 2 * HW), MXU_DTYPE), _rows(br, 2 * HW)),
                  (_sds((r, HP), MXU_DTYPE), blk)],
                 sem=("parallel",))


def _log_sigmoid(x):
    return jnp.minimum(x, 0.0) - jnp.log(1.0 + jnp.exp(-jnp.abs(x)))


def _head_lane(x, h, lane):
    return jnp.sum(jnp.where(lane == h, x, 0.0), axis=1, keepdims=True)


def _forget_fwd(proj_r, bf_row):
    r = proj_r.shape[0]
    br = ROW_BLOCK

    def body(t_ref, b_ref, ob_ref, ot_ref, carry_ref):
        @pl.when(pl.program_id(0) == 0)
        def _():
            carry_ref[...] = jnp.zeros_like(carry_ref)

        x = t_ref[...] + b_ref[...]
        lane = _lane_iota(x.shape)
        lf = jnp.where((lane >= LANE_FL) & (lane < LANE_FL + HEADS), _log_sigmoid(x), 0.0)
        tri = (lax.broadcasted_iota(jnp.int32, (br, br), 0) >= lax.broadcasted_iota(jnp.int32, (br, br), 1)).astype(F32)
        cum = jnp.dot(tri, lf, precision=HIGHEST, preferred_element_type=F32) + carry_ref[0:1, :]
        for h in range(HEADS):
            ob_ref[:, _hs(h)] = jnp.broadcast_to(_head_lane(cum, LANE_FL + h, lane), (br, HP))
        ot_ref[...] = cum.T[LANE_FL:LANE_FL + HEADS, :]
        carry_ref[...] = jnp.broadcast_to(cum[br - 1:br, :], carry_ref.shape)

    return _call("forget_fwd", body, (r // br,),
                 [(proj_r, _rows(br, HP, R_LAST // HP)), (bf_row, _whole((1, HP)))],
                 [(_sds((r, HW), F32), _rows(br, HW)), (_sds((HEADS, r), F32), pl.BlockSpec((HEADS, br), lambda i: (0, i)))],
                 scratch=[pltpu.VMEM((8, HP), F32)], sem=("arbitrary",))


def _forget_bwd(proj_r, bf_row, dcq_t, dck_b):
    r = proj_r.shape[0]
    br = ROW_BLOCK
    nb = r // br

    def body(t_ref, b_ref, dcq_ref, dck_ref, o_ref, db_ref, carry_ref):
        @pl.when(pl.program_id(0) == 0)
        def _():
            carry_ref[...] = jnp.zeros_like(carry_ref)
            db_ref[...] = jnp.zeros_like(db_ref)

        lane = _lane_iota((br, HP))
        dc = jnp.concatenate([dcq_ref[...], jnp.zeros((HP - HEADS, br), F32)], axis=0).T
        for h in range(HEADS):
            dc = dc + jnp.where(lane == LANE_FL + h, dck_ref[:, h * HP:h * HP + 1], 0.0)
        triu = (lax.broadcasted_iota(jnp.int32, (br, br), 0) <= lax.broadcasted_iota(jnp.int32, (br, br), 1)).astype(F32)
        dlf = jnp.dot(triu, dc, precision=HIGHEST, preferred_element_type=F32) + carry_ref[0:1, :]
        carry_ref[...] = jnp.broadcast_to(dlf[0:1, :], carry_ref.shape)
        x = t_ref[...] + b_ref[...]
        dfl = jnp.where((lane >= LANE_FL) & (lane < LANE_FL + HEADS), dlf * jax.nn.sigmoid(-x), 0.0)
        o_ref[...] = dfl
        db_ref[...] += jnp.sum(dfl, axis=0, keepdims=True)

    rev = pl.BlockSpec((br, HP), lambda i: (nb - 1 - i, 0))
    return _call("forget_bwd", body, (nb,),
                 [(proj_r, pl.BlockSpec((br, HP), lambda i: (nb - 1 - i, R_LAST // HP))), (bf_row, _whole((1, HP))),
                  (dcq_t, pl.BlockSpec((HEADS, br), lambda i: (0, nb - 1 - i))),
                  (dck_b, pl.BlockSpec((br, HW), lambda i: (nb - 1 - i, 0)))],
                 [(_sds((r, HP), F32), rev), (_sds((1, HP), F32), _whole((1, HP)))],
                 scratch=[pltpu.VMEM((8, HP), F32)], sem=("arbitrary",))


def _gate_fwd(proj_r, b_gate, bm, bfx):
    r, d = bm.shape
    br = ROW_BLOCK
    cb = R_GATE // d

    def body(gm_ref, gf_ref, b1_ref, b2_ref, bm_ref, bf_ref, o_ref):
        g1 = jax.nn.sigmoid(gm_ref[...] + b1_ref[...])
        g2 = jax.nn.sigmoid(gf_ref[...] + b2_ref[...])
        o_ref[...] = (g1 * bm_ref[...].astype(F32) + g2 * bf_ref[...].astype(F32)).astype(o_ref.dtype)

    b1 = b_gate[:d].reshape(1, d)
    b2 = b_gate[d:].reshape(1, d)
    return _call("gate_fwd", body, (r // br,),
                 [(proj_r, _rows(br, d, cb)), (proj_r, _rows(br, d, cb + 1)), (b1, _whole((1, d))), (b2, _whole((1, d))),
                  (bm, _rows(br, d)), (bfx, _rows(br, d))],
                 [(_sds((r, d), MXU_DTYPE), _rows(br, d))], sem=("parallel",))[0]


def _gate_bwd(proj_r, b_gate, bm, bfx, dmerged):
    r, d = bm.shape
    br = ROW_BLOCK
    cb = R_GATE // d

    def body(gm_ref, gf_ref, b1_ref, b2_ref, bm_ref, bf_ref, dm_ref, dbm_ref, dbf_ref, dgl_ref, dbg_ref):
        g1 = jax.nn.sigmoid(gm_ref[...] + b1_ref[...])
        g2 = jax.nn.sigmoid(gf_ref[...] + b2_ref[...])
        dm = dm_ref[...].astype(F32)
        dbm_ref[...] = (dm * g1).astype(dbm_ref.dtype)
        dbf_ref[...] = (dm * g2).astype(dbf_ref.dtype)
        dl1 = dm * bm_ref[...].astype(F32) * (g1 * (1.0 - g1))
        dl2 = dm * bf_ref[...].astype(F32) * (g2 * (1.0 - g2))
        dgl_ref[:, 0:d] = dl1.astype(dgl_ref.dtype)
        dgl_ref[:, d:2 * d] = dl2.astype(dgl_ref.dtype)

        @pl.when(pl.program_id(0) == 0)
        def _():
            dbg_ref[...] = jnp.zeros_like(dbg_ref)

        dbg_ref[:, 0:d] += jnp.sum(dl1, axis=0, keepdims=True)
        dbg_ref[:, d:2 * d] += jnp.sum(dl2, axis=0, keepdims=True)

    b1 = b_gate[:d].reshape(1, d)
    b2 = b_gate[d:].reshape(1, d)
    return _call("gate_bwd", body, (r // br,),
                 [(proj_r, _rows(br, d, cb)), (proj_r, _rows(br, d, cb + 1)), (b1, _whole((1, d))), (b2, _whole((1, d))),
                  (bm, _rows(br, d)), (bfx, _rows(br, d)), (dmerged, _rows(br, d))],
                 [(_sds((r, d), MXU_DTYPE), _rows(br, d)), (_sds((r, d), MXU_DTYPE), _rows(br, d)),
                  (_sds((r, 2 * d), MXU_DTYPE), _rows(br, 2 * d)), (_sds((1, 2 * d), F32), _whole((1, 2 * d)))],
                 sem=("arbitrary",))


HALO = 16
GLU_BWD_BLOCK = 128


def _conv_taps(gp, halo, first_block):
    halo = jnp.where(first_block, 0.0, halo.astype(F32))
    rid = lax.broadcasted_iota(jnp.int32, gp.shape, 0)
    last, prev = halo[HALO - 1:HALO, :], halo[HALO - 2:HALO - 1, :]
    g1 = jnp.where(rid == 0, last, pltpu.roll(gp, 1, axis=0))
    g2 = jnp.where(rid == 0, prev, jnp.where(rid == 1, last, pltpu.roll(gp, 2, axis=0)))
    return g1, g2


def _prev_halo(br, c):
    return pl.BlockSpec((HALO, c), lambda i: (jnp.maximum(i * (br // HALO) - 1, 0), 0))


def _glu_fwd(up, conv_w, conv_b):
    r = up.shape[0]
    c = D_FF
    br = ROW_BLOCK

    def body(gp_ref, halo_ref, val_ref, w_ref, b_ref, o_ref):
        gp = gp_ref[...].astype(F32)
        g1, g2 = _conv_taps(gp, halo_ref[...], pl.program_id(0) == 0)
        gate = w_ref[0:1, :] * g2 + w_ref[1:2, :] * g1 + w_ref[2:3, :] * gp + b_ref[...]
        o_ref[...] = (gate * jax.nn.sigmoid(gate) * val_ref[...].astype(F32)).astype(o_ref.dtype)

    return _call("glu_fwd", body, (r // br,),
                 [(up, _rows(br, c, 0)), (up, _prev_halo(br, c)), (up, _rows(br, c, 1)),
                  (conv_w, _whole((3, c))), (conv_b.reshape(1, c), _whole((1, c)))],
                 [(_sds((r, c), MXU_DTYPE), _rows(br, c))], sem=("parallel",))[0]


def _glu_bwd(up, conv_w, conv_b, d_act):
    r = up.shape[0]
    c = D_FF
    br = GLU_BWD_BLOCK
    nb = r // br

    def body(gp_ref, halo_ref, val_ref, da_ref, gpn_ref, valn_ref, dan_ref, w_ref, b_ref, o_ref, dw_ref, db_ref):
        i = pl.program_id(0)
        w0, w1, w2, bias = w_ref[0:1, :], w_ref[1:2, :], w_ref[2:3, :], b_ref[...]

        def d_gate(gp, g1, g2, val, da):
            gate = w0 * g2 + w1 * g1 + w2 * gp + bias
            sg = jax.nn.sigmoid(gate)
            return da * val * (sg * (1.0 + gate * (1.0 - sg))), da * (gate * sg)

        gp = gp_ref[...].astype(F32)
        g1, g2 = _conv_taps(gp, halo_ref[...], i == 0)
        dg, dv = d_gate(gp, g1, g2, val_ref[...].astype(F32), da_ref[...].astype(F32))
        gpn = gpn_ref[...].astype(F32)
        g1n, g2n = _conv_taps(gpn, gp[br - HALO:, :], False)
        dgn, _ = d_gate(gpn, g1n, g2n, valn_ref[...].astype(F32), dan_ref[...].astype(F32))
        dgn = jnp.where(i == nb - 1, 0.0, dgn)
        rid = lax.broadcasted_iota(jnp.int32, dg.shape, 0)
        u1 = jnp.where(rid == br - 1, dgn[0:1, :], pltpu.roll(dg, br - 1, axis=0))
        u2 = jnp.where(rid == br - 1, dgn[1:2, :], jnp.where(rid == br - 2, dgn[0:1, :], pltpu.roll(dg, br - 2, axis=0)))
        o_ref[:, 0:c] = (w2 * dg + w1 * u1 + w0 * u2).astype(o_ref.dtype)
        o_ref[:, c:2 * c] = dv.astype(o_ref.dtype)

        @pl.when(i == 0)
        def _():
            dw_ref[...] = jnp.zeros_like(dw_ref)
            db_ref[...] = jnp.zeros_like(db_ref)

        dw_ref[0:1, :] += jnp.sum(dg * g2, axis=0, keepdims=True)
        dw_ref[1:2, :] += jnp.sum(dg * g1, axis=0, keepdims=True)
        dw_ref[2:3, :] += jnp.sum(dg * gp, axis=0, keepdims=True)
        db_ref[...] += jnp.sum(dg, axis=0, keepdims=True)

    nxt = lambda cb: pl.BlockSpec((HALO, c), lambda i: (jnp.minimum((i + 1) * (br // HALO), r // HALO - 1), cb))
    return _call("glu_bwd", body, (nb,),
                 [(up, _rows(br, c, 0)), (up, _prev_halo(br, c)), (up, _rows(br, c, 1)), (d_act, _rows(br, c)),
                  (up, nxt(0)), (up, nxt(1)), (d_act, nxt(0)),
                  (conv_w, _whole((3, c))), (conv_b.reshape(1, c), _whole((1, c)))],
                 [(_sds((r, 2 * c), MXU_DTYPE), _rows(br, 2 * c)),
                  (_sds((8, c), F32), _whole((8, c))), (_sds((1, c), F32), _whole((1, c)))],
                 sem=("arbitrary",))


def _token_specs(seq, d):
    br = ROW_BLOCK
    nxb = seq // br
    main = pl.BlockSpec((br, d), lambda i: (jnp.minimum(i, nxb - 1), 0))
    tail = pl.BlockSpec((N_META, d), lambda i: (jnp.clip(i * (br // N_META) - 1, 0, seq // N_META - 1), 0))
    return main, tail


def _padded_block(main_ref, tail_ref, first, seq):
    br = ROW_BLOCK
    i = pl.program_id(0)
    nxb = seq // br
    main = jnp.where(i < nxb, main_ref[...], 0.0)
    head = jnp.where(i == 0, first, jnp.where(i <= nxb, tail_ref[...], 0.0))
    return jnp.concatenate([head, main[:br - N_META]], axis=0)


def _ln_emb_fwd(x, meta, g, b, rows, after=()):
    seq, d = x.shape
    br = ROW_BLOCK
    assert seq % br == 0 and br % N_META == 0 and rows % br == 0

    def body(x_ref, tail_ref, meta_ref, g_ref, b_ref, y_ref, yb_ref):
        z = _padded_block(x_ref, tail_ref, meta_ref[...], seq)
        xhat, _ = _ln_stats(z)
        y = xhat * g_ref[...] + b_ref[...]
        y_ref[...] = y
        yb_ref[...] = y.astype(yb_ref.dtype)

    main, tail = _token_specs(seq, d)
    return _call("ln_emb_fwd", body, (rows // br,),
                 [(x, main), (x, tail), (meta, _whole((N_META, d))), (g.reshape(1, d), _whole((1, d))),
                  (b.reshape(1, d), _whole((1, d)))],
                 [(_sds((rows, d), F32), _rows(br, d)), (_sds((rows, d), MXU_DTYPE), _rows(br, d))],
                 sem=("parallel",), after=after)


def _ln_emb_bwd(x, meta, dh0, g):
    seq, d = x.shape
    br = ROW_BLOCK
    step = br // N_META

    def ln_bwd(z, dy, gv):
        xhat, rstd = _ln_stats(z)
        dyg = dy * gv
        m1 = jnp.mean(dyg, axis=-1, keepdims=True)
        m2 = jnp.mean(dyg * xhat, axis=-1, keepdims=True)
        dz = rstd * (dyg - m1 - xhat * m2)
        return dz, jnp.sum(dy * xhat, axis=0, keepdims=True), jnp.sum(dy, axis=0, keepdims=True)

    def body(x_ref, dh_ref, nxt_ref, meta_ref, top_ref, g_ref, dx_ref, dm_ref, dg_ref, db_ref):
        gv = g_ref[...]
        dy = jnp.concatenate([dh_ref[N_META:, :], nxt_ref[...]], axis=0)
        dz, dg, db = ln_bwd(x_ref[...], dy, gv)
        dx_ref[...] = dz

        @pl.when(pl.program_id(0) == 0)
        def _():
            dzm, dgm, dbm = ln_bwd(meta_ref[...], top_ref[...], gv)
            dm_ref[...] = dzm
            dg_ref[...] = dgm
            db_ref[...] = dbm

        dg_ref[...] += dg
        db_ref[...] += db

    small = _whole((N_META, d))
    return _call("ln_emb_bwd", body, (seq // br,),
                 [(x, _rows(br, d)), (dh0, _rows(br, d)), (dh0, pl.BlockSpec((N_META, d), lambda i: ((i + 1) * step, 0))),
                  (meta, small), (dh0, small), (g.reshape(1, d), _whole((1, d)))],
                 [(_sds((seq, d), F32), _rows(br, d)), (_sds((N_META, d), F32), small),
                  (_sds((1, d), F32), _whole((1, d))), (_sds((1, d), F32), _whole((1, d)))], sem=("arbitrary",))


def _ln_ffn_loss(h1, f, tgt, g, b):
    r, d = h1.shape
    seq = tgt.shape[0]
    br = ROW_BLOCK

    def body(a_ref, r_ref, t_ref, tail_ref, g_ref, b_ref, l_ref):
        err = _loss_err(a_ref, r_ref, t_ref, tail_ref, g_ref, b_ref, seq)[0]

        @pl.when(pl.program_id(0) == 0)
        def _():
            l_ref[...] = jnp.zeros_like(l_ref)

        l_ref[...] += jnp.sum(jnp.sum(err * err, axis=1, keepdims=True), axis=0, keepdims=True) * (0.5 / d)

    main, tail = _token_specs(seq, d)
    return _call("ln_ffn_loss", body, (r // br,),
                 [(h1, _rows(br, d)), (f, _rows(br, d)), (tgt, main), (tgt, tail),
                  (g.reshape(1, d), _whole((1, d))), (b.reshape(1, d), _whole((1, d)))],
                 [(_sds((1, 1), F32), _whole((1, 1)))], sem=("arbitrary",))[0]


def _loss_err(a_ref, r_ref, t_ref, tail_ref, g_ref, b_ref, seq):
    br, d = a_ref.shape
    xhat, rstd = _ln_stats(ALPHA * a_ref[...] + r_ref[...])
    y = xhat * g_ref[...] + b_ref[...]
    t = _padded_block(t_ref, tail_ref, jnp.zeros((N_META, d), F32), seq)
    rid = lax.broadcasted_iota(jnp.int32, (br, d), 0) + pl.program_id(0) * br
    valid = (rid >= N_META) & (rid < N_META + seq)
    return jnp.where(valid, y - t, 0.0), xhat, rstd


def _ln_ffn_bwd(h1, f, tgt, g, b):
    r, d = h1.shape
    seq = tgt.shape[0]
    br = ROW_BLOCK

    def body(a_ref, r_ref, t_ref, tail_ref, g_ref, b_ref, dz_ref, dzb_ref, dg_ref, db_ref):
        err, xhat, rstd = _loss_err(a_ref, r_ref, t_ref, tail_ref, g_ref, b_ref, seq)
        dyv = err * (1.0 / d)
        dyg = dyv * g_ref[...]
        m1 = jnp.mean(dyg, axis=-1, keepdims=True)
        m2 = jnp.mean(dyg * xhat, axis=-1, keepdims=True)
        dz = rstd * (dyg - m1 - xhat * m2)
        dz_ref[...] = dz
        dzb_ref[...] = dz.astype(dzb_ref.dtype)

        @pl.when(pl.program_id(0) == 0)
        def _():
            dg_ref[...] = jnp.zeros_like(dg_ref)
            db_ref[...] = jnp.zeros_like(db_ref)

        dg_ref[...] += jnp.sum(dyv * xhat, axis=0, keepdims=True)
        db_ref[...] += jnp.sum(dyv, axis=0, keepdims=True)

    main, tail = _token_specs(seq, d)
    return _call("ln_ffn_bwd", body, (r // br,),
                 [(h1, _rows(br, d)), (f, _rows(br, d)), (tgt, main), (tgt, tail),
                  (g.reshape(1, d), _whole((1, d))), (b.reshape(1, d), _whole((1, d)))],
                 [(_sds((r, d), F32), _rows(br, d)), (_sds((r, d), MXU_DTYPE), _rows(br, d)),
                  (_sds((1, d), F32), _whole((1, d))), (_sds((1, d), F32), _whole((1, d)))], sem=("arbitrary",))


def _attn_fwd(name, q, k, v, cum_b=None, cum_t=None):
    (qa, qg), (ka, kg), (va, vg) = q, k, v
    r = qa.shape[0]
    tq, tk = ATT_TQ, ATT_TK
    nq, nk = r // tq, r // tk
    bias = cum_b is not None

    def body(*refs):
        if bias:
            q_ref, k_ref, vt_ref, cb_ref, ct_ref, o_ref, ob_ref, lse_ref = refs
        else:
            q_ref, k_ref, vt_ref, o_ref, ob_ref, lse_ref = refs
        i = pl.program_id(1)
        qs = [q_ref[:, _hs(hh)] for hh in range(hg)]
        cqs = [ct_ref[hh] for hh in range(hg)] if bias else None
        diff = lax.broadcasted_iota(jnp.int32, (tk, tq), 0) - lax.broadcasted_iota(jnp.int32, (tk, tq), 1)

        def step(j, carry, masked):
            keys = pl.ds(pl.multiple_of(j * tk, tk), tk)
            out = []
            for hh in range(hg):
                m, l, acc = carry[hh]
                kt = k_ref[keys, _hs(hh)]
                s = lax.dot_general(kt, qs[hh], NT, preferred_element_type=F32)
                if bias:
                    s = s + (cqs[hh] - cb_ref[keys, hh * HP:hh * HP + 1])
                if masked:
                    s = jnp.where(diff <= i * tq - j * tk, s, NEG_INF)
                m_new = jnp.maximum(m, jnp.max(s, axis=0, keepdims=True))
                p = jnp.exp(s - m_new)
                a = jnp.exp(m - m_new)
                l = a * l + jnp.sum(p, axis=0, keepdims=True)
                acc = a * acc + jnp.dot(vt_ref[j, _hs(hh), :], p.astype(kt.dtype), preferred_element_type=F32)
                out.append((m_new, l, acc))
            return tuple(out)

        n_clear = (i * tq + 1) // tk
        n_all = ((i + 1) * tq - 1) // tk + 1
        carry = tuple((jnp.full((1, tq), NEG_INF, F32), jnp.zeros((1, tq), F32), jnp.zeros((HP, tq), F32))
                      for _ in range(hg))
        carry = lax.fori_loop(0, n_clear, lambda j, c: step(j, c, False), carry)
        carry = lax.fori_loop(n_clear, n_all, lambda j, c: step(j, c, True), carry)
        for hh in range(hg):
            m, l, acc = carry[hh]
            o = (acc / l).T
            o_ref[:, _hs(hh)] = o
            ob_ref[:, _hs(hh)] = o.astype(ob_ref.dtype)
            lse_ref[hh] = m + jnp.log(l)

    hg = ATT_HEADS
    w = hg * HP
    gpw = HW // w
    tile = lambda g: pl.BlockSpec((tq, w), lambda h, i: (i, g * gpw + h))
    res = lambda g: pl.BlockSpec((r, w), lambda h, i: (0, g * gpw + h))
    v_t = _key_tiles_transposed(name + "_vt", va, vg)
    ins = [(qa, tile(qg)), (ka, res(kg)), (v_t, pl.BlockSpec((nk, w, tk), lambda h, i: (0, h, 0)))]
    if bias:
        ins += [(cum_b, res(0)),
                (cum_t.reshape(HEADS, nq, 1, tq), pl.BlockSpec((hg, None, 1, tq), lambda h, i: (h, i, 0, 0)))]
    outs = [(_sds((r, HW), F32), tile(0)), (_sds((r, HW), MXU_DTYPE), tile(0)),
            (_sds((HEADS, nq, 1, tq), F32), pl.BlockSpec((hg, None, 1, tq), lambda h, i: (h, i, 0, 0)))]
    o, ob, lse = _call(name, body, (gpw, nq), ins, outs, sem=("parallel", "parallel"))
    return o, ob, lse.reshape(HEADS, r)


def _key_tiles_transposed(name, a, group):
    r = a.shape[0]
    tk = ATT_TK

    def body(x_ref, o_ref):
        for h in range(HEADS):
            o_ref[_hs(h), :] = x_ref[:, _hs(h)].astype(F32).T.astype(o_ref.dtype)

    return _call(name, body, (r // tk,),
                 [(a, pl.BlockSpec((tk, HW), lambda j: (j, group)))],
                 [(_sds((r // tk, HW, tk), a.dtype), pl.BlockSpec((None, HW, tk), lambda j: (j, 0, 0)))],
                 sem=("parallel",))[0]


def _attn_bwd(name, q, k, v, do_b, o, lse_t, cum_b=None, cum_t=None, out_dtype=F32, after=()):
    (qa, qg), (ka, kg), (va, vg) = q, k, v
    r = qa.shape[0]
    tq, tk = ATT_TQ, ATT_TK
    nq, nk = r // tq, r // tk
    bias = cum_b is not None

    def body(*refs):
        if bias:
            (q_ref, k_ref, v_ref, do_ref, o_ref, lse_ref, cb_ref, ct_ref,
             dq_ref, dk_ref, dv_ref, dcq_ref, dck_ref, dqt_ref, dl_ref) = refs
        else:
            q_ref, k_ref, v_ref, do_ref, o_ref, lse_ref, dq_ref, dk_ref, dv_ref, dqt_ref, dl_ref = refs
        j = pl.program_id(1)

        @pl.when(j == 0)
        def _():
            dqt_ref[...] = jnp.zeros_like(dqt_ref)
            if bias:
                dcq_ref[...] = jnp.zeros_like(dcq_ref)
            for hh in range(hg):
                for i in range(nq):
                    rows = slice(i * tq, (i + 1) * tq)
                    prod = do_ref[rows, _hs(hh)].astype(F32) * o_ref[rows, _hs(hh)]
                    dl_ref[hh, i] = jnp.sum(prod.T, axis=0, keepdims=True)

        kts = [k_ref[:, _hs(hh)] for hh in range(hg)]
        vts = [v_ref[:, _hs(hh)] for hh in range(hg)]
        k_trs = [kt.astype(F32).T.astype(kt.dtype) for kt in kts]
        cks = [cb_ref[:, hh * HP:hh * HP + 1] for hh in range(hg)] if bias else None
        diff = lax.broadcasted_iota(jnp.int32, (tk, tq), 0) - lax.broadcasted_iota(jnp.int32, (tk, tq), 1)

        def step(i, carry, masked):
            rows = pl.ds(pl.multiple_of(i * tq, tq), tq)
            out = []
            for hh in range(hg):
                dk_acc, dv_acc, dck_acc = carry[hh]
                qt = q_ref[rows, _hs(hh)]
                dot = do_ref[rows, _hs(hh)]
                s = lax.dot_general(kts[hh], qt, NT, preferred_element_type=F32)
                if bias:
                    s = s + (ct_ref[hh, i] - cks[hh])
                if masked:
                    s = jnp.where(diff <= i * tq - j * tk, s, NEG_INF)
                p = jnp.exp(s - lse_ref[hh, i])
                dp = lax.dot_general(vts[hh], dot, NT, preferred_element_type=F32)
                ds = p * (dp - dl_ref[hh, i])
                pb = p.astype(dot.dtype)
                dsb = ds.astype(qt.dtype)
                dv_acc = dv_acc + jnp.dot(pb, dot, preferred_element_type=F32)
                dk_acc = dk_acc + jnp.dot(dsb, qt, preferred_element_type=F32)
                dqt_ref[hh, i] += jnp.dot(k_trs[hh], dsb, preferred_element_type=F32)
                if bias:
                    dcq_ref[hh, i] += jnp.sum(ds, axis=0, keepdims=True)
                    dck_acc = dck_acc - jnp.sum(ds, axis=1, keepdims=True)
                out.append((dk_acc, dv_acc, dck_acc))
            return tuple(out)

        i_first = (j * tk) // tq
        i_clear = jnp.minimum(((j + 1) * tk + tq - 2) // tq, nq)
        carry = tuple((jnp.zeros((tk, HP), F32), jnp.zeros((tk, HP), F32), jnp.zeros((tk, 1), F32)) for _ in range(hg))
        carry = lax.fori_loop(i_first, i_clear, lambda i, c: step(i, c, True), carry)
        carry = lax.fori_loop(i_clear, nq, lambda i, c: step(i, c, False), carry)
        for hh in range(hg):
            dk_acc, dv_acc, dck_acc = carry[hh]
            dk_ref[:, _hs(hh)] = dk_acc.astype(dk_ref.dtype)
            dv_ref[:, _hs(hh)] = dv_acc.astype(dv_ref.dtype)
            if bias:
                dck_ref[:, _hs(hh)] = jnp.broadcast_to(dck_acc, (tk, HP))

        @pl.when(j == nk - 1)
        def _():
            for hh in range(hg):
                for i in range(nq):
                    dq_ref[i * tq:(i + 1) * tq, _hs(hh)] = dqt_ref[hh, i].T.astype(dq_ref.dtype)

    hg = ATT_HEADS
    w = hg * HP
    gpw = HW // w
    res = lambda g: pl.BlockSpec((r, w), lambda h, j: (0, g * gpw + h))
    tile = lambda g: pl.BlockSpec((tk, w), lambda h, j: (j, g * gpw + h))
    rowv = pl.BlockSpec((hg, nq, 1, tq), lambda h, j: (h, 0, 0, 0))
    as_rows = lambda a: a.reshape(HEADS, nq, 1, tq)
    ins = [(qa, res(qg)), (ka, tile(kg)), (va, tile(vg)), (do_b, res(0)), (o, res(0)), (as_rows(lse_t), rowv)]
    outs = [(_sds((r, HW), out_dtype), res(0)), (_sds((r, HW), out_dtype), tile(0)), (_sds((r, HW), out_dtype), tile(0))]
    if bias:
        ins += [(cum_b, tile(0)), (as_rows(cum_t), rowv)]
        outs += [(_sds((HEADS, nq, 1, tq), F32), rowv), (_sds((r, HW), F32), tile(0))]
    res_out = _call(name, body, (gpw, nk), ins, outs,
                    scratch=[pltpu.VMEM((hg, nq, HP, tq), F32), pltpu.VMEM((hg, nq, 1, tq), F32)],
                    sem=("parallel", "arbitrary"), after=after)
    if bias:
        dq, dk, dv, dcq, dck = res_out
        return dq, dk, dv, dcq.reshape(HEADS, r), dck
    return res_out


MESH_ID = pl.DeviceIdType.MESH
ANY = pl.BlockSpec(memory_space=pl.ANY)


N_GATHER_COPIES = 8


def _allgather(name, shards):
    n = len(shards)

    def body(*refs):
        x_refs, out_refs = refs[:n], refs[n:2 * n]
        send_sems, recv_sems, local_sems = refs[2 * n:]
        x, y, c = lax.axis_index("x"), lax.axis_index("y"), lax.axis_index("c")
        me, sibling = (x, y, c), (x, y, 1 - c)
        xn, yn, dg = (1 - x, y, c), (x, 1 - y, c), (1 - x, 1 - y, c)
        other = lambda dev: (dev[0], dev[1], 1 - c)

        def slot(ti, dev, half=None):
            ref = out_refs[ti].at[4 * dev[0] + 2 * dev[1] + dev[2]]
            if half is None:
                return ref
            rows = shards[ti].shape[0] // 2
            return ref.at[pl.ds(half * rows, rows)]

        def copy(ti, k, block, to, half=None, src=None):
            return pltpu.make_async_remote_copy(
                src_ref=slot(ti, block, half) if src is None else src, dst_ref=slot(ti, block, half),
                send_sem=send_sems.at[ti, k], recv_sem=recv_sems.at[ti, k], device_id=to, device_id_type=MESH_ID)

        mine = [pltpu.make_async_copy(x_refs[ti], slot(ti, me), local_sems.at[ti]) for ti in range(n)]
        for cp in mine:
            cp.start()
        started = []

        def go(cp):
            cp.start()
            started.append(cp)

        for ti in range(n):
            go(copy(ti, 0, me, sibling, src=x_refs[ti]))
            go(copy(ti, 1, me, xn, src=x_refs[ti]))
            go(copy(ti, 2, me, yn, src=x_refs[ti]))
        for ti in range(n):
            copy(ti, 1, xn, me).wait_recv()
            go(copy(ti, 3, xn, yn, half=0))
            go(copy(ti, 5, xn, sibling))
            copy(ti, 2, yn, me).wait_recv()
            go(copy(ti, 4, yn, xn, half=1))
            go(copy(ti, 6, yn, sibling))
        for ti in range(n):
            copy(ti, 3, dg, me, half=0).wait_recv()
            copy(ti, 4, dg, me, half=1).wait_recv()
            go(copy(ti, 7, dg, sibling))
        for ti in range(n):
            copy(ti, 0, sibling, me).wait_recv()
            for k, dev in ((5, xn), (6, yn), (7, dg)):
                copy(ti, k, other(dev), me).wait_recv()
        for cp in started:
            cp.wait_send()
        for cp in mine:
            cp.wait()

    sems = pltpu.SemaphoreType.DMA((n, N_GATHER_COPIES))
    return pl.pallas_call(
        body, name=name, out_shape=[_sds((N_DEV,) + s.shape, s.dtype) for s in shards],
        in_specs=[ANY] * n, out_specs=[ANY] * n,
        scratch_shapes=[sems, sems, pltpu.SemaphoreType.DMA((n,))],
    )(*shards)


HBM = pl.BlockSpec(memory_space=pltpu.HBM)
SEM = pl.BlockSpec(memory_space=pltpu.SEMAPHORE)
EFFECT = pltpu.SideEffectType.DATAFLOW_SIDE_EFFECTING
N_PEER = N_DEV - 1


def _my_id():
    return 4 * lax.axis_index("x") + 2 * lax.axis_index("y") + lax.axis_index("c")


def _peers():
    x, y, c = lax.axis_index("x"), lax.axis_index("y"), lax.axis_index("c")
    out = []
    for k in range(1, N_DEV):
        px, py, pc = (1 - x if k & 4 else x, 1 - y if k & 2 else y, 1 - c if k & 1 else c)
        out.append(((px, py, pc), 4 * px + 2 * py + pc))
    return out


def _push_copies(src_refs, land_refs, send_sems, recv_sems, scatter, landing):
    me = _my_id()
    out = []
    for ti, (src, land) in enumerate(zip(src_refs, land_refs)):
        for k, (dev, pid) in enumerate(_peers()):
            out.append(pltpu.make_async_remote_copy(
                src_ref=src.at[pid] if scatter else src, dst_ref=land.at[pid if landing else me],
                send_sem=send_sems.at[ti * N_PEER + k], recv_sem=recv_sems.at[ti * N_PEER + k],
                device_id=dev, device_id_type=MESH_ID))
    return out


def _push_start(name, groups, scatter, after=None):
    sizes = [len(g) for g in groups]
    srcs = [a for g in groups for a in g]
    n = len(srcs)
    slot = lambda s: s.shape[1:] if scatter else s.shape
    lands = [lax.empty((N_DEV,) + slot(s), s.dtype) for s in srcs]
    n_after = 0 if after is None else 1
    n_grp = len(groups)

    def body(*refs):
        src_refs, land_refs = refs[:n], refs[n:2 * n]
        sems = refs[2 * n + n_after:2 * n + n_after + 2 * n_grp]
        token = refs[-1]
        lo = 0
        for gi, sz in enumerate(sizes):
            for cp in _push_copies(src_refs[lo:lo + sz], land_refs[lo:lo + sz], sems[2 * gi], sems[2 * gi + 1], scatter, False):
                cp.start()
            lo += sz
        token[...] = jnp.zeros_like(token)

    hbm = lambda a: pltpu.with_memory_space_constraint(a, pltpu.HBM)
    operands = [hbm(a) for a in srcs + lands] + ([after] if n_after else [])
    sem_shapes = [pltpu.SemaphoreType.DMA((sz * N_PEER,)) for sz in sizes for _ in range(2)]
    res = pl.pallas_call(
        body, name=name,
        out_shape=sem_shapes + [pltpu.HBM(a.shape, a.dtype) for a in srcs + lands] + [_sds((8, 128), F32)],
        in_specs=[HBM] * (2 * n) + [ANY] * n_after,
        out_specs=[SEM] * (2 * n_grp) + [HBM] * (2 * n) + [pl.BlockSpec(memory_space=pltpu.VMEM)],
        input_output_aliases={i: 2 * n_grp + i for i in range(2 * n)},
        compiler_params=pltpu.CompilerParams(has_side_effects=EFFECT),
    )(*operands)
    thru = res[2 * n_grp:2 * n_grp + 2 * n]
    handles, lo = [], 0
    for gi, sz in enumerate(sizes):
        handles.append((res[2 * gi], res[2 * gi + 1], list(thru[lo:lo + sz]), list(thru[n + lo:n + lo + sz]), scatter))
        lo += sz
    return handles, res[-1]


def _push_wait(name, handle, after):
    send_sems, recv_sems, srcs, lands, scatter = handle
    n = len(srcs)

    def body(*refs):
        src_refs, land_refs = refs[:n], refs[n:2 * n]
        s_sems, r_sems = refs[2 * n], refs[2 * n + 1]
        for cp in _push_copies(src_refs, land_refs, s_sems, r_sems, scatter, True):
            cp.wait_send()
            cp.wait_recv()

    res = pl.pallas_call(
        body, name=name,
        out_shape=[pltpu.HBM(a.shape, a.dtype) for a in srcs + lands],
        in_specs=[HBM] * (2 * n) + [SEM, SEM, ANY], out_specs=[HBM] * (2 * n),
        input_output_aliases={i: i for i in range(2 * n)},
        compiler_params=pltpu.CompilerParams(has_side_effects=EFFECT),
    )(*srcs, *lands, send_sems, recv_sems, after)
    return list(res[n:])


def _adamw(name, parts, w, m, v, own=None):
    r, c = w.shape
    br = _pick(r, 256, 16)
    has_own = own is not None

    def body(*refs):
        if has_own:
            p_ref, own_ref, w_ref, m_ref, v_ref, g_ref, d_ref, nm_ref, nv_ref = refs
            me = _my_id()
            mine = own_ref[...].astype(F32)
        else:
            p_ref, w_ref, m_ref, v_ref, g_ref, d_ref, nm_ref, nv_ref = refs
        g = None
        for k in range(N_DEV):
            t = p_ref[k].astype(F32)
            if has_own:
                t = jnp.where(me == k, mine, t)
            g = t if g is None else g + t
        mm = ADAM_B1 * m_ref[...] + (1.0 - ADAM_B1) * g
        vv = ADAM_B2 * v_ref[...] + (1.0 - ADAM_B2) * (g * g)
        m_hat = mm / (1.0 - ADAM_B1 ** ADAM_STEP)
        v_hat = vv / (1.0 - ADAM_B2 ** ADAM_STEP)
        g_ref[...] = g
        d_ref[...] = -ADAM_LR * (m_hat / (jnp.sqrt(v_hat) + ADAM_EPS) + ADAM_WD * w_ref[...])
        nm_ref[...] = mm
        nv_ref[...] = vv

    spec = _rows(br, c)
    out = (_sds((r, c), F32), spec)
    ins = [(parts, pl.BlockSpec((N_DEV, br, c), lambda i: (0, i, 0)))] + ([(own, spec)] if has_own else [])
    return _call(name, body, (r // br,), ins + [(w, spec), (m, spec), (v, spec)], [out] * 4, sem=("parallel",))


def _pad_head_cols(w, d):
    k = w.shape[0]
    return jnp.pad(w.reshape(k, HEADS, d), ((0, 0), (0, 0), (0, HP - d))).reshape(k, HW)


def _unpad_head_cols(wp, d):
    k = wp.shape[0]
    return wp.reshape(k, HEADS, HP)[:, :, :d].reshape(k, HEADS * d)


def _pad_head_rows(w, d):
    n = w.shape[1]
    return jnp.pad(w.reshape(HEADS, d, n), ((0, 0), (0, HP - d), (0, 0))).reshape(HW, n)


def _unpad_head_rows(wp, d):
    n = wp.shape[1]
    return wp.reshape(HEADS, HP, n)[:, :d, :].reshape(HEADS * d, n)


def _w_in_runs():
    nat = {}
    o = 0
    for nm, wd in (("q", Q_RANK), ("kv", KV_RANK), ("kr", ROPE), ("fq", FOX_W), ("fk", FOX_W), ("fv", FOX_W),
                   ("fl", HEADS), ("gate", 2 * D_MODEL)):
        nat[nm] = o
        o += wd
    runs = [(1, R_QLAT, nat["q"], Q_RANK, 1.0), (1, R_KVLAT, nat["kv"], KV_RANK, 1.0),
            (1, R_LAST + LANE_FL, nat["fl"], HEADS, 1.0), (1, R_LAST + LANE_PE, nat["kr"], ROPE, 1.0),
            (1, R_GATE, nat["gate"], 2 * D_MODEL, 1.0)]
    for grp, (nm, sc) in enumerate((("fq", FOX_SCALE), ("fk", 1.0), ("fv", 1.0))):
        runs += [(0, grp * HW + h * HP, nat[nm] + h * FOX_DIM, FOX_DIM, sc) for h in range(HEADS)]
    return runs


def _sharded_runs(runs, shard_cols):
    out = []
    for half, col, ncol, width, sc in runs:
        while width > 0:
            d, local = divmod(ncol, shard_cols)
            wd = min(width, shard_cols - local)
            out.append((half, col, d, local, wd, sc))
            col, ncol, width = col + wd, ncol + wd, width - wd
    return out


def _remap(name, srcs, out_shapes, moves):
    rows = srcs[0].shape[-2]
    br = _pick(rows, 256, 16)
    ns = len(srcs)

    def spec(shape):
        if len(shape) == 2:
            return pl.BlockSpec((br, shape[1]), lambda i: (i, 0))
        return pl.BlockSpec((shape[0], br, shape[2]), lambda i: (0, i, 0))

    def body(*refs):
        s_refs, o_refs = refs[:ns], refs[ns:]
        for o in o_refs:
            o[...] = jnp.zeros_like(o)
        for di, dl, dc, si, sl, sc0, wd, scale in moves:
            v = s_refs[si][:, sc0:sc0 + wd] if sl is None else s_refs[si][sl, :, sc0:sc0 + wd]
            if scale != 1.0:
                v = v * jnp.asarray(scale, v.dtype)
            v = v.astype(o_refs[di].dtype)
            if dl is None:
                o_refs[di][:, dc:dc + wd] = v
            else:
                o_refs[di][dl, :, dc:dc + wd] = v

    return _call(name, body, (rows // br,), [(a, spec(a.shape)) for a in srcs],
                 [(_sds(shape, dt), spec(shape)) for shape, dt in out_shapes], sem=("parallel",))


def _w_in_from_shards(g3):
    n, rows, c = g3.shape
    moves = [(half, None, col, 0, d, local, wd, sc) for half, col, d, local, wd, sc in _sharded_runs(_w_in_runs(), c)]
    return _remap("w_in_repack", [g3], [((rows, F_W), g3.dtype), ((rows, R_W), g3.dtype)], moves)


def _w_in_grad_to_shards(d_fused, d_rest, n, c):
    rows = d_fused.shape[0]
    moves = [(0, d, local, half, None, col, wd, sc) for half, col, d, local, wd, sc in _sharded_runs(_w_in_runs(), c)]
    return _remap("w_in_grad_unpack", [d_fused, d_rest], [((n, rows, c), d_fused.dtype)], moves)[0]


def _rows_from_shards(name, land, own):
    n, rows, c = land.shape

    def body(land_ref, own_ref, o_ref):
        o_ref[...] = jnp.where(_my_id() == pl.program_id(0), own_ref[...], land_ref[...])

    return _call(name, body, (n,),
                 [(land, pl.BlockSpec((None, rows, c), lambda d: (d, 0, 0))), (own, _whole((rows, c)))],
                 [(_sds((n * rows, c), land.dtype), pl.BlockSpec((rows, c), lambda d: (d, 0)))], sem=("parallel",))[0]


def _cols_from_shards(name, land, own):
    n, rows, c = land.shape
    br = _pick(rows, 256, 16)

    def body(land_ref, own_ref, o_ref):
        me = _my_id()
        for d in range(n):
            o_ref[:, c * d:c * (d + 1)] = jnp.where(me == d, own_ref[...], land_ref[d])

    return _call(name, body, (rows // br,),
                 [(land, pl.BlockSpec((n, br, c), lambda i: (0, i, 0))), (own, _rows(br, c))],
                 [(_sds((rows, n * c), land.dtype), _rows(br, n * c))], sem=("parallel",))[0]


def _cols_to_shards(name, full, n):
    rows, nc = full.shape
    c = nc // n
    return _remap(name, [full], [((n, rows, c), full.dtype)], [(0, d, 0, 0, None, c * d, c, 1.0) for d in range(n)])[0]


def _split_w_kv(w):
    k = w.shape[0]
    w3 = w.reshape(k, HEADS, NOPE + V_DIM)
    padl = lambda a: jnp.pad(a, ((0, 0), (0, 0), (0, HP - a.shape[-1]))).reshape(k, HW)
    return padl(w3[..., :NOPE]), padl(w3[..., NOPE:])


def _merge_w_kv(wk, wv):
    k = wk.shape[0]
    return jnp.concatenate([wk.reshape(k, HEADS, HP)[..., :NOPE], wv.reshape(k, HEADS, HP)[..., :V_DIM]],
                           axis=-1).reshape(k, HEADS * (NOPE + V_DIM))


class _NoComm:
    first_token = ()

    def late_weights(self, group, after):
        return {}

    def send(self, name, grads):
        return ()


def _local_step(x, tgt, p, comm=_NoComm()):
    seq = x.shape[0]
    r = -(-(N_META + seq) // ROW_ALIGN) * ROW_ALIGN
    cd = MXU_DTYPE
    p = dict(p)

    w_f, w_r = p["w_in"]

    pos = jnp.arange(r, dtype=F32)
    inv_freq = ROPE_THETA ** (-jnp.arange(HALF, dtype=F32) / HALF)
    ang = pos[:, None] * inv_freq[None, :]
    cos_t = jnp.tile(jnp.cos(ang), (1, HP // HALF))
    sin_t = jnp.tile(jnp.sin(ang), (1, HP // HALF))
    bf_row = jnp.zeros((1, HP), F32).at[0, LANE_FL:LANE_FL + HEADS].set(p["b_forget"])

    h0, h0b = _ln_emb_fwd(x, p["meta_tokens"], p["ln_emb_g"], p["ln_emb_b"], r, after=comm.first_token)
    proj_f = _matmul("in_proj_f", h0b, w_f, out_dtype=cd)
    proj_r = _matmul("in_proj_r", h0b, w_r)
    latent_gains = (p["q_norm_g"], p["kv_norm_g"])
    ql, kvl = _latent_norm_fwd(proj_r, latent_gains)
    p.update(comm.late_weights("qkv", ql))
    w_q = _pad_head_cols(p["w_q_up"], QK_DIM)
    w_kv = jnp.concatenate(_split_w_kv(p["w_kv_up"]), axis=1)
    q_raw = _matmul("q_up", ql, w_q)
    kv = _matmul("kv_up", kvl, w_kv, out_dtype=cd)
    q_mla, k_mla = _rope_fwd(q_raw, kv, proj_r, cos_t, sin_t)
    o_mla, o_mla_b, lse_mla = _attn_fwd("mla_fwd", (q_mla, 0), (k_mla, 0), (kv, 1))

    cum, cum_t = _forget_fwd(proj_r, bf_row)
    o_fox, o_fox_b, lse_fox = _attn_fwd("fox_fwd", (proj_f, 0), (proj_f, 1), (proj_f, 2), cum, cum_t)

    p.update(comm.late_weights("mix", o_fox_b))
    w_bm = _pad_head_rows(p["w_branch_mla"], V_DIM)
    w_bf = _pad_head_rows(p["w_branch_fox"], FOX_DIM)
    bm = _matmul("branch_mla", o_mla_b, w_bm, out_dtype=cd)
    bfx = _matmul("branch_fox", o_fox_b, w_bf, out_dtype=cd)
    merged = _gate_fwd(proj_r, p["b_gate"], bm, bfx)
    mix = _matmul("out_proj", merged, p["w_out"])
    h1, h1b = _ln_fwd("ln_mix_fwd", h0, mix, p["ln_mix_g"], p["ln_mix_b"])
    p.update(comm.late_weights("ffn", h1b))
    up = _matmul("ffn_up", h1b, p["w_ffn_up"], out_dtype=cd)
    act = _glu_fwd(up, p["conv_w"], p["conv_b"])
    f = _matmul("ffn_down", act, p["w_ffn_down"])
    loss = _ln_ffn_loss(h1, f, tgt, p["ln_ffn_g"], p["ln_ffn_b"])

    g = {}
    dz2, dz2b, g["ln_ffn_g"], g["ln_ffn_b"] = _ln_ffn_bwd(h1, f, tgt, p["ln_ffn_g"], p["ln_ffn_b"])
    d_act = _matmul("ffn_down_dx", dz2b, p["w_ffn_down"], tb=True, out_dtype=cd)
    g["w_ffn_down"] = _matmul("ffn_down_dw", act, dz2b, ta=True, out_dtype=cd)
    d_up, dcw, g["conv_b"] = _glu_bwd(up, p["conv_w"], p["conv_b"], d_act)
    g["conv_w"] = dcw[:3]
    dh1 = _matmul("ffn_up_dx", d_up, p["w_ffn_up"], tb=True, addend=dz2, alpha=ALPHA)
    g["w_ffn_up"] = _matmul("ffn_up_dw", h1b, d_up, ta=True, out_dtype=cd)
    sent = comm.send("ffn", {n: g[n] for n in ("w_ffn_down", "w_ffn_up", "conv_w")})
    dz1, dz1b, g["ln_mix_g"], g["ln_mix_b"] = _ln_bwd("ln_mix_bwd", h0, mix, dh1, p["ln_mix_g"], after=sent)
    dmerged = _matmul("out_proj_dx", dz1b, p["w_out"], tb=True, out_dtype=cd)
    g["w_out"] = _matmul("out_proj_dw", merged, dz1b, ta=True, out_dtype=cd)
    d_bm, d_bf, d_gl, g["b_gate"] = _gate_bwd(proj_r, p["b_gate"], bm, bfx, dmerged)
    do_mla_b = _matmul("branch_mla_dx", d_bm, w_bm, tb=True, out_dtype=cd)
    g["w_branch_mla"] = _unpad_head_rows(_matmul("branch_mla_dw", o_mla_b, d_bm, ta=True, out_dtype=cd), V_DIM)
    do_fox_b = _matmul("branch_fox_dx", d_bf, w_bf, tb=True, out_dtype=cd)
    g["w_branch_fox"] = _unpad_head_rows(_matmul("branch_fox_dw", o_fox_b, d_bf, ta=True, out_dtype=cd), FOX_DIM)

    sent = comm.send("mix", {n: g[n] for n in ("w_out", "w_branch_mla", "w_branch_fox")})
    dq_m, dk_m, dv_m = _attn_bwd("mla_bwd", (q_mla, 0), (k_mla, 0), (kv, 1), do_mla_b, o_mla, lse_mla, after=sent)
    dfq, dfk, dfv, dcq, dck = _attn_bwd("fox_bwd", (proj_f, 0), (proj_f, 1), (proj_f, 2), do_fox_b, o_fox, lse_fox,
                                        cum, cum_t, out_dtype=cd)
    dfl, dbf = _forget_bwd(proj_r, bf_row, dcq, dck)
    g["b_forget"] = dbf[:, LANE_FL:LANE_FL + HEADS]

    dq_b, dkv_b, dlast = _rope_bwd(dq_m, dk_m, dv_m, dfl, cos_t, sin_t)
    d_ql = _matmul("q_up_dx", dq_b, w_q, tb=True)
    d_kvl = _matmul("kv_up_dx", dkv_b, w_kv, tb=True)
    d_qlat, d_kvlat, g["q_norm_g"], g["kv_norm_g"] = _latent_norm_bwd(proj_r, (d_ql, d_kvl), latent_gains)
    side_by_side = lambda parts, cols: [(0, None, c0, si, None, 0, a.shape[1], 1.0) for si, (a, c0) in enumerate(zip(parts, cols))]
    dproj_f = _remap("dproj_f_pack", [dfq, dfk, dfv], [((r, F_W), cd)], side_by_side([dfq, dfk, dfv], (0, HW, 2 * HW)))[0]
    rest_parts = [d_qlat, d_kvlat, dlast, d_gl]
    dproj_r = _remap("dproj_r_pack", rest_parts, [((r, R_W), cd)],
                     side_by_side(rest_parts, (R_QLAT, R_KVLAT, R_LAST, R_GATE)))[0]
    g["w_in"] = (_matmul("in_proj_f_dw", h0b, dproj_f, ta=True, out_dtype=cd),
                 _matmul("in_proj_r_dw", h0b, dproj_r, ta=True, out_dtype=cd))
    sent = comm.send("in", {"w_in": g["w_in"]})
    dh0 = _matmul("in_proj_f_dx", dproj_f, w_f, tb=True, addend=dz1, alpha=ALPHA, after=sent)
    g["w_q_up"] = _unpad_head_cols(_matmul("q_up_dw", ql, dq_b, ta=True, out_dtype=cd, after=sent), QK_DIM)
    dw_kv = _matmul("kv_up_dw", kvl, dkv_b, ta=True, out_dtype=cd, after=sent)
    g["w_kv_up"] = _merge_w_kv(dw_kv[:, :HW], dw_kv[:, HW:])
    sent = comm.send("qkv", {n: g[n] for n in ("w_q_up", "w_kv_up")})
    dh0 = _matmul("in_proj_r_dx", dproj_r, w_r, tb=True, addend=dh0, after=sent)
    grad_x, d_meta, g["ln_emb_g"], g["ln_emb_b"] = _ln_emb_bwd(x, p["meta_tokens"], dh0, p["ln_emb_g"])
    return loss, grad_x, d_meta, g


BIG = (("w_in", 1), ("w_q_up", 1), ("w_kv_up", 1), ("w_branch_mla", 1), ("w_branch_fox", 1), ("w_out", 0),
       ("w_ffn_up", 1), ("w_ffn_down", 0))
SMALL_SHARDED = (("meta_tokens", 1), ("conv_w", 1))
EARLY = ("w_in", "meta_tokens")
LATE = {"qkv": ("w_q_up", "w_kv_up", "conv_w"),
        "mix": ("w_branch_mla", "w_branch_fox", "w_out"),
        "ffn": ("w_ffn_up", "w_ffn_down")}
REPLICATED = ("ln_emb_g", "ln_emb_b", "b_gate", "b_forget", "q_norm_g", "kv_norm_g", "ln_mix_g", "ln_mix_b",
              "conv_b", "ln_ffn_g", "ln_ffn_b")
PACK_COLS = 1024


def _pack(flat_list):
    cat = jnp.concatenate(flat_list)
    n = cat.shape[0]
    rows = -(-n // (8 * PACK_COLS)) * 8
    return jnp.pad(cat, (0, rows * PACK_COLS - n)).reshape(rows, PACK_COLS)


def _gathered_full(g3, axis):
    n, r, c = g3.shape
    if axis == 0:
        return g3.reshape(n * r, c)
    return g3.transpose(1, 0, 2).reshape(r, n * c)


def _shard_major(full, axis):
    r, c = full.shape
    if axis == 0:
        return full.reshape(N_DEV, r // N_DEV, c)
    return full.reshape(r, N_DEV, c // N_DEV).transpose(1, 0, 2)


def kernel(x, meta_tokens, ln_emb_g, ln_emb_b, w_in, b_gate, b_forget, q_norm_g, w_q_up, kv_norm_g, w_kv_up, w_branch_mla, w_branch_fox, w_out, ln_mix_g, ln_mix_b, w_ffn_up, conv_w, conv_b, w_ffn_down, ln_ffn_g, ln_ffn_b, loss_target, m_meta_tokens, m_ln_emb_g, m_ln_emb_b, m_w_in, m_b_gate, m_b_forget, m_q_norm_g, m_w_q_up, m_kv_norm_g, m_w_kv_up, m_w_branch_mla, m_w_branch_fox, m_w_out, m_ln_mix_g, m_ln_mix_b, m_w_ffn_up, m_conv_w, m_conv_b, m_w_ffn_down, m_ln_ffn_g, m_ln_ffn_b, v_meta_tokens, v_ln_emb_g, v_ln_emb_b, v_w_in, v_b_gate, v_b_forget, v_q_norm_g, v_w_q_up, v_kv_norm_g, v_w_kv_up, v_w_branch_mla, v_w_branch_fox, v_w_out, v_ln_mix_g, v_ln_mix_b, v_w_ffn_up, v_conv_w, v_conv_b, v_w_ffn_down, v_ln_ffn_g, v_ln_ffn_b):
    names = ("meta_tokens", "ln_emb_g", "ln_emb_b", "w_in", "b_gate", "b_forget", "q_norm_g", "w_q_up", "kv_norm_g",
             "w_kv_up", "w_branch_mla", "w_branch_fox", "w_out", "ln_mix_g", "ln_mix_b", "w_ffn_up", "conv_w", "conv_b",
             "w_ffn_down", "ln_ffn_g", "ln_ffn_b")
    w_args = (meta_tokens, ln_emb_g, ln_emb_b, w_in, b_gate, b_forget, q_norm_g, w_q_up, kv_norm_g, w_kv_up,
              w_branch_mla, w_branch_fox, w_out, ln_mix_g, ln_mix_b, w_ffn_up, conv_w, conv_b, w_ffn_down, ln_ffn_g, ln_ffn_b)
    m_args = (m_meta_tokens, m_ln_emb_g, m_ln_emb_b, m_w_in, m_b_gate, m_b_forget, m_q_norm_g, m_w_q_up, m_kv_norm_g,
              m_w_kv_up, m_w_branch_mla, m_w_branch_fox, m_w_out, m_ln_mix_g, m_ln_mix_b, m_w_ffn_up, m_conv_w, m_conv_b,
              m_w_ffn_down, m_ln_ffn_g, m_ln_ffn_b)
    v_args = (v_meta_tokens, v_ln_emb_g, v_ln_emb_b, v_w_in, v_b_gate, v_b_forget, v_q_norm_g, v_w_q_up, v_kv_norm_g,
              v_w_kv_up, v_w_branch_mla, v_w_branch_fox, v_w_out, v_ln_mix_g, v_ln_mix_b, v_w_ffn_up, v_conv_w, v_conv_b,
              v_w_ffn_down, v_ln_ffn_g, v_ln_ffn_b)
    as2d = lambda a: a.reshape((-1, a.shape[-1])) if a.ndim != 1 else a.reshape(1, -1)
    w = {n: as2d(a) for n, a in zip(names, w_args)}
    m = {n: as2d(a) for n, a in zip(names, m_args)}
    v = {n: as2d(a) for n, a in zip(names, v_args)}
    out_shape = {n: a.shape for n, a in zip(names, w_args)}

    axis_of = dict(BIG + SMALL_SHARDED)
    big = set(n for n, _ in BIG)
    wire = lambda n, a: a.astype(MXU_DTYPE) if n in big else a
    my_id = _my_id()

    early = _allgather("gather_early", [wire(n, w[n]) for n in EARLY])
    p = {n: _gathered_full(g3, axis_of[n]) for n, g3 in zip(EARLY, early) if n != "w_in"}
    p["w_in"] = _w_in_from_shards(early[EARLY.index("w_in")])
    for n in REPLICATED:
        p[n] = w[n].reshape(-1)
    late_src = [[wire(n, w[n]) for n in members] for members in LATE.values()]
    late_handles, late_token = _push_start("gather_late_start", late_src, False, after=early[0])
    late = {group: (members, src, handle)
            for (group, members), src, handle in zip(LATE.items(), late_src, late_handles)}
    sent = {}

    class Comm:
        first_token = (late_token,)

        def late_weights(self, group, after):
            members, src, handle = late[group]
            lands = _push_wait("gather_" + group + "_wait", handle, after)
            out = {}
            for n, own, land in zip(members, src, lands):
                if own.shape[0] % 16:
                    out[n] = _gathered_full(lax.dynamic_update_index_in_dim(land, own, my_id, 0), axis_of[n])
                elif axis_of[n] == 1:
                    out[n] = _cols_from_shards(n + "_repack", land, own)
                else:
                    out[n] = _rows_from_shards(n + "_repack", land, own)
            return out

        def send(self, name, grads):
            names_ = tuple(grads)
            parts = []
            for n in names_:
                if n == "w_in":
                    parts.append(_w_in_grad_to_shards(*grads[n], N_DEV, w[n].shape[1]))
                elif n == "w_ffn_up":
                    parts.append(_cols_to_shards(n + "_grad_unpack", grads[n], N_DEV))
                else:
                    parts.append(_shard_major(grads[n], axis_of[n]).astype(MXU_DTYPE))
            (handle,), token = _push_start("send_" + name + "_start", [parts], True)
            sent[name] = (names_, parts, handle)
            return (token,)

    loss_part, grad_x, d_meta, g = _local_step(x[0], loss_target[0], p, Comm())
    grad_x = grad_x[None]

    small = _pack([d_meta.reshape(-1)] + [g[n].reshape(-1) for n in REPLICATED] + [loss_part.reshape(-1)])
    (small_handle,), small_token = _push_start("send_small_start", [[small]], False)

    res = {}
    prev = small_token
    for name, (names_, parts, handle) in sent.items():
        lands = _push_wait("send_" + name + "_wait", handle, prev)
        for n, part, land in zip(names_, parts, lands):
            own = lax.dynamic_index_in_dim(part, my_id, axis=0, keepdims=False)
            res[n] = _adamw("adamw_" + n, land, w[n], m[n], v[n], own=own)
            prev = res[n][0]
    small_all = _push_wait("send_small_wait", small_handle, prev)[0]
    head = jnp.zeros((d_meta.size,), F32)
    rep_w = _pack([head] + [w[n].reshape(-1) for n in REPLICATED])
    rep_m = _pack([head] + [m[n].reshape(-1) for n in REPLICATED])
    rep_v = _pack([head] + [v[n].reshape(-1) for n in REPLICATED])
    rep_res = _adamw("adamw_replicated", small_all, rep_w, rep_m, rep_v, own=small)
    off = d_meta.size
    for n in REPLICATED:
        sz = w[n].size
        res[n] = tuple(a.reshape(-1)[off:off + sz] for a in rep_res)
        off += sz
    loss = rep_res[0].reshape(-1)[off]
    cols = w["meta_tokens"].shape[1]
    meta_rows = lambda a: a.reshape(a.shape[:-2] + (-1,))[..., :d_meta.size].reshape(a.shape[:-2] + d_meta.shape)
    my_cols = lambda a: lax.dynamic_slice_in_dim(a, my_id * cols, cols, axis=a.ndim - 1)
    res["meta_tokens"] = _adamw("adamw_meta_tokens", my_cols(meta_rows(small_all)), w["meta_tokens"],
                                m["meta_tokens"], v["meta_tokens"], own=my_cols(d_meta))

    outs = [loss, grad_x]
    for idx in range(4):
        outs += [res[n][idx].reshape(out_shape[n]) for n in names]
    return tuple(outs)
```

```python
import jax
import jax.numpy as jnp
from jax import lax
from jax.experimental import pallas as pl
from jax.experimental.pallas import tpu as pltpu

F32 = jnp.float32
BF16 = jnp.bfloat16
MXU_DTYPE = BF16

N_DEV = 8
N_META = 16
D_MODEL = 1024
HEADS = 8
Q_RANK = 384
KV_RANK = 128
NOPE = 64
ROPE = 32
HALF = ROPE // 2
QK_DIM = NOPE + ROPE
V_DIM = 64
FOX_DIM = 64
FOX_W = HEADS * FOX_DIM
D_FF = 2816
ROPE_THETA = 10000.0
LN_EPS = 1e-5
RMS_EPS = 1e-6
ALPHA = 2.0 ** 0.25
MLA_SCALE = QK_DIM ** -0.5
FOX_SCALE = FOX_DIM ** -0.5
NEG_INF = -1e30

HP = 128
HW = HEADS * HP
F_W = 3 * HW
R_GATE = 0
R_KVLAT = R_GATE + 2 * D_MODEL
R_LAST = R_KVLAT + KV_RANK
R_QLAT = R_LAST + HP
R_W = R_QLAT + Q_RANK
assert R_QLAT % Q_RANK == 0 and R_KVLAT % KV_RANK == 0 and R_GATE % D_MODEL == 0 and R_W % HP == 0
LANE_FL = 0
LANE_PE = NOPE

ADAM_LR = 0.001
ADAM_B1 = 0.9
ADAM_B2 = 0.999
ADAM_EPS = 1e-08
ADAM_WD = 0.01
ADAM_STEP = 10

ROW_BLOCK = 256
ATT_TQ = 768
ATT_TK = 768
ATT_HEADS = 2
ROW_ALIGN = 768
MM_BLOCK_CAP = 1408
VMEM_LIMIT = 56 * 1024 * 1024
HIGHEST = lax.Precision.HIGHEST
NT = (((1,), (1,)), ((), ()))
TN = (((0,), (0,)), ((), ()))


def _params(sem=None):
    return pltpu.CompilerParams(dimension_semantics=sem, vmem_limit_bytes=VMEM_LIMIT)


def _call(name, body, grid, ins, outs, scratch=(), sem=None, after=()):
    n_in = len(ins)
    n_tok = len(after)

    def run(*refs):
        body(*refs[:n_in], *refs[n_in + n_tok:])

    tok_spec = pl.BlockSpec((8, 128), lambda *_: (0, 0))
    return pl.pallas_call(
        run, name=name, grid=grid,
        in_specs=[s for _, s in ins] + [tok_spec] * n_tok,
        out_specs=[s for _, s in outs],
        out_shape=[o for o, _ in outs],
        scratch_shapes=list(scratch),
        compiler_params=_params(sem),
    )(*[a for a, _ in ins], *after)


def _sds(shape, dtype):
    return jax.ShapeDtypeStruct(shape, dtype)


def _rows(br, c, cb=0):
    return pl.BlockSpec((br, c), lambda i: (i, cb))


def _whole(shape):
    n = len(shape)
    return pl.BlockSpec(shape, lambda i: (0,) * n)


def _pick(dim, cap, mult):
    best = None
    d = mult
    while d <= min(dim, cap):
        if dim % d == 0:
            best = d
        d += mult
    return best if best is not None else dim


def _hs(h):
    return slice(h * HP, (h + 1) * HP)


def _matmul(name, a, b, *, ta=False, tb=False, out_dtype=F32, addend=None, alpha=1.0, after=()):
    if ta:
        k, m = a.shape
    else:
        m, k = a.shape
    if tb:
        n, k2 = b.shape
    else:
        k2, n = b.shape
    assert k == k2, (name, a.shape, b.shape)
    bm = _pick(m, MM_BLOCK_CAP, 128 if ta else 16)
    bn = _pick(n, MM_BLOCK_CAP, 128)
    bk = _pick(k, MM_BLOCK_CAP, 128 if (not ta or tb) else 16)
    nk = k // bk
    dims = (((0 if ta else 1,), (1 if tb else 0,)), ((), ()))
    has_add = addend is not None

    def body(*refs):
        a_ref, b_ref = refs[:2]
        add_ref = refs[2] if has_add else None
        o_ref = refs[3 if has_add else 2]

        def finish(r):
            if has_add:
                r = r + alpha * add_ref[...]
            o_ref[...] = r.astype(o_ref.dtype)

        part = lax.dot_general(a_ref[...], b_ref[...], dims, preferred_element_type=F32)
        if nk == 1:
            finish(part)
            return
        acc_ref = refs[-1]
        kk = pl.program_id(2)

        @pl.when(kk == 0)
        def _():
            acc_ref[...] = part

        @pl.when(kk > 0)
        def _():
            acc_ref[...] += part

        @pl.when(kk == nk - 1)
        def _():
            finish(acc_ref[...])

    a_spec = pl.BlockSpec((bk, bm), lambda i, j, l: (l, i)) if ta else pl.BlockSpec((bm, bk), lambda i, j, l: (i, l))
    b_spec = pl.BlockSpec((bn, bk), lambda i, j, l: (j, l)) if tb else pl.BlockSpec((bk, bn), lambda i, j, l: (l, j))
    o_spec = pl.BlockSpec((bm, bn), lambda i, j, l: (i, j))
    ins = [(a, a_spec), (b, b_spec)]
    if has_add:
        ins.append((addend, o_spec))
    return _call(name, body, (m // bm, n // bn, nk), ins, [(_sds((m, n), out_dtype), o_spec)],
                 scratch=[pltpu.VMEM((bm, bn), F32)] if nk > 1 else [],
                 sem=("parallel", "parallel", "arbitrary"), after=after)[0]


def _ln_stats(z):
    mu = jnp.mean(z, axis=-1, keepdims=True)
    zc = z - mu
    var = jnp.mean(zc * zc, axis=-1, keepdims=True)
    rstd = lax.rsqrt(var + LN_EPS)
    return zc * rstd, rstd


def _ln_fwd(name, a, res, g, b, after=()):
    r, d = a.shape
    br = ROW_BLOCK
    has_res = res is not None

    def body(*refs):
        if has_res:
            a_ref, r_ref, g_ref, b_ref, y_ref, yb_ref = refs
            z = ALPHA * a_ref[...] + r_ref[...]
        else:
            a_ref, g_ref, b_ref, y_ref, yb_ref = refs
            z = a_ref[...]
        xhat, _ = _ln_stats(z)
        y = xhat * g_ref[...] + b_ref[...]
        y_ref[...] = y
        yb_ref[...] = y.astype(yb_ref.dtype)

    ins = [(a, _rows(br, d))]
    if has_res:
        ins.append((res, _rows(br, d)))
    ins += [(g.reshape(1, d), _whole((1, d))), (b.reshape(1, d), _whole((1, d)))]
    outs = [(_sds((r, d), F32), _rows(br, d)), (_sds((r, d), MXU_DTYPE), _rows(br, d))]
    return _call(name, body, (r // br,), ins, outs, sem=("parallel",), after=after)


def _ln_bwd(name, a, res, dy, g, after=()):
    r, d = a.shape
    br = ROW_BLOCK
    has_res = res is not None

    def body(*refs):
        if has_res:
            a_ref, r_ref, dy_ref, g_ref, dz_ref, dzb_ref, dg_ref, db_ref = refs
            z = ALPHA * a_ref[...] + r_ref[...]
        else:
            a_ref, dy_ref, g_ref, dz_ref, dzb_ref, dg_ref, db_ref = refs
            z = a_ref[...]
        xhat, rstd = _ln_stats(z)
        dyv = dy_ref[...]
        dyg = dyv * g_ref[...]
        m1 = jnp.mean(dyg, axis=-1, keepdims=True)
        m2 = jnp.mean(dyg * xhat, axis=-1, keepdims=True)
        dz = rstd * (dyg - m1 - xhat * m2)
        dz_ref[...] = dz
        dzb_ref[...] = dz.astype(dzb_ref.dtype)

        @pl.when(pl.program_id(0) == 0)
        def _():
            dg_ref[...] = jnp.zeros_like(dg_ref)
            db_ref[...] = jnp.zeros_like(db_ref)

        dg_ref[...] += jnp.sum(dyv * xhat, axis=0, keepdims=True)
        db_ref[...] += jnp.sum(dyv, axis=0, keepdims=True)

    ins = [(a, _rows(br, d))]
    if has_res:
        ins.append((res, _rows(br, d)))
    ins += [(dy, _rows(br, d)), (g.reshape(1, d), _whole((1, d)))]
    outs = [(_sds((r, d), F32), _rows(br, d)), (_sds((r, d), MXU_DTYPE), _rows(br, d)),
            (_sds((1, d), F32), _whole((1, d))), (_sds((1, d), F32), _whole((1, d)))]
    return _call(name, body, (r // br,), ins, outs, sem=("arbitrary",), after=after)


LATENTS = ((R_QLAT // Q_RANK, Q_RANK), (R_KVLAT // KV_RANK, KV_RANK))


def _latent_norm_fwd(proj_r, gains):
    r = proj_r.shape[0]
    br = ROW_BLOCK

    def body(xq_ref, xk_ref, gq_ref, gk_ref, yq_ref, yk_ref):
        for x_ref, g_ref, y_ref in ((xq_ref, gq_ref, yq_ref), (xk_ref, gk_ref, yk_ref)):
            x = x_ref[...]
            rstd = lax.rsqrt(jnp.mean(x * x, axis=-1, keepdims=True) + RMS_EPS)
            y_ref[...] = (x * rstd * g_ref[...]).astype(y_ref.dtype)

    return _call("latent_norm_fwd", body, (r // br,),
                 [(proj_r, _rows(br, wd, cb)) for cb, wd in LATENTS]
                 + [(g.reshape(1, wd), _whole((1, wd))) for g, (_, wd) in zip(gains, LATENTS)],
                 [(_sds((r, wd), MXU_DTYPE), _rows(br, wd)) for _, wd in LATENTS], sem=("parallel",))


def _latent_norm_bwd(proj_r, dys, gains):
    r = proj_r.shape[0]
    br = ROW_BLOCK

    def body(xq_ref, xk_ref, dq_ref, dk_ref, gq_ref, gk_ref, oq_ref, ok_ref, dgq_ref, dgk_ref):
        @pl.when(pl.program_id(0) == 0)
        def _():
            dgq_ref[...] = jnp.zeros_like(dgq_ref)
            dgk_ref[...] = jnp.zeros_like(dgk_ref)

        for x_ref, dy_ref, g_ref, dx_ref, dg_ref in ((xq_ref, dq_ref, gq_ref, oq_ref, dgq_ref),
                                                     (xk_ref, dk_ref, gk_ref, ok_ref, dgk_ref)):
            x = x_ref[...]
            rstd = lax.rsqrt(jnp.mean(x * x, axis=-1, keepdims=True) + RMS_EPS)
            nrm = x * rstd
            dyv = dy_ref[...]
            dyg = dyv * g_ref[...]
            dx_ref[...] = (rstd * (dyg - nrm * jnp.mean(dyg * nrm, axis=-1, keepdims=True))).astype(dx_ref.dtype)
            dg_ref[...] += jnp.sum(dyv * nrm, axis=0, keepdims=True)

    return _call("latent_norm_bwd", body, (r // br,),
                 [(proj_r, _rows(br, wd, cb)) for cb, wd in LATENTS]
                 + [(dy, _rows(br, wd)) for dy, (_, wd) in zip(dys, LATENTS)]
                 + [(g.reshape(1, wd), _whole((1, wd))) for g, (_, wd) in zip(gains, LATENTS)],
                 [(_sds((r, wd), MXU_DTYPE), _rows(br, wd)) for _, wd in LATENTS]
                 + [(_sds((1, wd), F32), _whole((1, wd))) for _, wd in LATENTS], sem=("arbitrary",))


def _lane_iota(shape):
    return lax.broadcasted_iota(jnp.int32, shape, 1)


def _rotary(t, c, s, lane, sign):
    second = pltpu.roll(t, HP - HALF, axis=1)
    first = pltpu.roll(t, HALF, axis=1)
    lo = (lane >= LANE_PE) & (lane < LANE_PE + HALF)
    hi = (lane >= LANE_PE + HALF) & (lane < LANE_PE + ROPE)
    return jnp.where(lo, t * c - sign * second * s, jnp.where(hi, t * c + sign * first * s, t))


def _rope_fwd(q_raw, k_part, proj_r, cos_t, sin_t):
    r = q_raw.shape[0]
    br = ROW_BLOCK

    def body(q_ref, k_ref, t_ref, c_ref, s_ref, qo_ref, ko_ref):
        c = c_ref[...]
        s = s_ref[...]
        lane = _lane_iota((br, HP))
        pe = (lane >= LANE_PE) & (lane < LANE_PE + ROPE)
        kp = jnp.where(pe, _rotary(t_ref[...], c, s, lane, 1.0), 0.0)
        for h in range(HEADS):
            qo_ref[:, _hs(h)] = (_rotary(q_ref[:, _hs(h)], c, s, lane, 1.0) * MLA_SCALE).astype(qo_ref.dtype)
            ko_ref[:, _hs(h)] = (k_ref[:, _hs(h)] + kp).astype(ko_ref.dtype)

    blk = _rows(br, HP)
    wide = _rows(br, HW)
    return _call("rope_fwd", body, (r // br,),
                 [(q_raw, wide), (k_part, wide), (proj_r, _rows(br, HP, R_LAST // HP)), (cos_t, blk), (sin_t, blk)],
                 [(_sds((r, HW), MXU_DTYPE), wide)] * 2, sem=("parallel",))


def _rope_bwd(dq, dk, dv, dfl, cos_t, sin_t):
    r = dq.shape[0]
    br = ROW_BLOCK

    def body(dq_ref, dk_ref, dv_ref, fl_ref, c_ref, s_ref, dqo_ref, dkv_ref, dl_ref):
        c = c_ref[...]
        s = s_ref[...]
        lane = _lane_iota((br, HP))
        pe = (lane >= LANE_PE) & (lane < LANE_PE + ROPE)
        acc = jnp.zeros((br, HP), F32)
        for h in range(HEADS):
            dqo_ref[:, _hs(h)] = (_rotary(dq_ref[:, _hs(h)], c, s, lane, -1.0) * MLA_SCALE).astype(dqo_ref.dtype)
            dkh = dk_ref[:, _hs(h)]
            acc = acc + dkh
            dkv_ref[:, _hs(h)] = dkh.astype(dkv_ref.dtype)
            dkv_ref[:, _hs(HEADS + h)] = dv_ref[:, _hs(h)].astype(dkv_ref.dtype)
        dl_ref[...] = (jnp.where(pe, _rotary(acc, c, s, lane, -1.0), 0.0) + fl_ref[...]).astype(dl_ref.dtype)

    blk = _rows(br, HP)
    wide = _rows(br, HW)
    return _call("rope_bwd", body, (r // br,),
                 [(dq, wide), (dk, wide), (dv, wide), (dfl, blk), (cos_t, blk), (sin_t, blk)],
                 [(_sds((r, HW), MXU_DTYPE), wide), (_sds((r, 2 * HW), MXU_DTYPE), _rows(br, 2 * HW)),
                  (_sds((r, HP), MXU_DTYPE), blk)],
                 sem=("parallel",))


def _log_sigmoid(x):
    return jnp.minimum(x, 0.0) - jnp.log(1.0 + jnp.exp(-jnp.abs(x)))


def _head_lane(x, h, lane):
    return jnp.sum(jnp.where(lane == h, x, 0.0), axis=1, keepdims=True)


def _forget_fwd(proj_r, bf_row):
    r = proj_r.shape[0]
    br = ROW_BLOCK

    def body(t_ref, b_ref, ob_ref, ot_ref, carry_ref):
        @pl.when(pl.program_id(0) == 0)
        def _():
            carry_ref[...] = jnp.zeros_like(carry_ref)

        x = t_ref[...] + b_ref[...]
        lane = _lane_iota(x.shape)
        lf = jnp.where((lane >= LANE_FL) & (lane < LANE_FL + HEADS), _log_sigmoid(x), 0.0)
        tri = (lax.broadcasted_iota(jnp.int32, (br, br), 0) >= lax.broadcasted_iota(jnp.int32, (br, br), 1)).astype(F32)
        cum = jnp.dot(tri, lf, precision=HIGHEST, preferred_element_type=F32) + carry_ref[0:1, :]
        for h in range(HEADS):
            ob_ref[:, _hs(h)] = jnp.broadcast_to(_head_lane(cum, LANE_FL + h, lane), (br, HP))
        ot_ref[...] = cum.T[LANE_FL:LANE_FL + HEADS, :]
        carry_ref[...] = jnp.broadcast_to(cum[br - 1:br, :], carry_ref.shape)

    return _call("forget_fwd", body, (r // br,),
                 [(proj_r, _rows(br, HP, R_LAST // HP)), (bf_row, _whole((1, HP)))],
                 [(_sds((r, HW), F32), _rows(br, HW)), (_sds((HEADS, r), F32), pl.BlockSpec((HEADS, br), lambda i: (0, i)))],
                 scratch=[pltpu.VMEM((8, HP), F32)], sem=("arbitrary",))


def _forget_bwd(proj_r, bf_row, dcq_t, dck_b):
    r = proj_r.shape[0]
    br = ROW_BLOCK
    nb = r // br

    def body(t_ref, b_ref, dcq_ref, dck_ref, o_ref, db_ref, carry_ref):
        @pl.when(pl.program_id(0) == 0)
        def _():
            carry_ref[...] = jnp.zeros_like(carry_ref)
            db_ref[...] = jnp.zeros_like(db_ref)

        lane = _lane_iota((br, HP))
        dc = jnp.concatenate([dcq_ref[...], jnp.zeros((HP - HEADS, br), F32)], axis=0).T
        for h in range(HEADS):
            dc = dc + jnp.where(lane == LANE_FL + h, dck_ref[:, h * HP:h * HP + 1], 0.0)
        triu = (lax.broadcasted_iota(jnp.int32, (br, br), 0) <= lax.broadcasted_iota(jnp.int32, (br, br), 1)).astype(F32)
        dlf = jnp.dot(triu, dc, precision=HIGHEST, preferred_element_type=F32) + carry_ref[0:1, :]
        carry_ref[...] = jnp.broadcast_to(dlf[0:1, :], carry_ref.shape)
        x = t_ref[...] + b_ref[...]
        dfl = jnp.where((lane >= LANE_FL) & (lane < LANE_FL + HEADS), dlf * jax.nn.sigmoid(-x), 0.0)
        o_ref[...] = dfl
        db_ref[...] += jnp.sum(dfl, axis=0, keepdims=True)

    rev = pl.BlockSpec((br, HP), lambda i: (nb - 1 - i, 0))
    return _call("forget_bwd", body, (nb,),
                 [(proj_r, pl.BlockSpec((br, HP), lambda i: (nb - 1 - i, R_LAST // HP))), (bf_row, _whole((1, HP))),
                  (dcq_t, pl.BlockSpec((HEADS, br), lambda i: (0, nb - 1 - i))),
                  (dck_b, pl.BlockSpec((br, HW), lambda i: (nb - 1 - i, 0)))],
                 [(_sds((r, HP), F32), rev), (_sds((1, HP), F32), _whole((1, HP)))],
                 scratch=[pltpu.VMEM((8, HP), F32)], sem=("arbitrary",))


def _gate_fwd(proj_r, b_gate, bm, bfx):
    r, d = bm.shape
    br = ROW_BLOCK
    cb = R_GATE // d

    def body(gm_ref, gf_ref, b1_ref, b2_ref, bm_ref, bf_ref, o_ref):
        g1 = jax.nn.sigmoid(gm_ref[...] + b1_ref[...])
        g2 = jax.nn.sigmoid(gf_ref[...] + b2_ref[...])
        o_ref[...] = (g1 * bm_ref[...].astype(F32) + g2 * bf_ref[...].astype(F32)).astype(o_ref.dtype)

    b1 = b_gate[:d].reshape(1, d)
    b2 = b_gate[d:].reshape(1, d)
    return _call("gate_fwd", body, (r // br,),
                 [(proj_r, _rows(br, d, cb)), (proj_r, _rows(br, d, cb + 1)), (b1, _whole((1, d))), (b2, _whole((1, d))),
                  (bm, _rows(br, d)), (bfx, _rows(br, d))],
                 [(_sds((r, d), MXU_DTYPE), _rows(br, d))], sem=("parallel",))[0]


def _gate_bwd(proj_r, b_gate, bm, bfx, dmerged):
    r, d = bm.shape
    br = ROW_BLOCK
    cb = R_GATE // d

    def body(gm_ref, gf_ref, b1_ref, b2_ref, bm_ref, bf_ref, dm_ref, dbm_ref, dbf_ref, dgl_ref, dbg_ref):
        g1 = jax.nn.sigmoid(gm_ref[...] + b1_ref[...])
        g2 = jax.nn.sigmoid(gf_ref[...] + b2_ref[...])
        dm = dm_ref[...].astype(F32)
        dbm_ref[...] = (dm * g1).astype(dbm_ref.dtype)
        dbf_ref[...] = (dm * g2).astype(dbf_ref.dtype)
        dl1 = dm * bm_ref[...].astype(F32) * (g1 * (1.0 - g1))
        dl2 = dm * bf_ref[...].astype(F32) * (g2 * (1.0 - g2))
        dgl_ref[:, 0:d] = dl1.astype(dgl_ref.dtype)
        dgl_ref[:, d:2 * d] = dl2.astype(dgl_ref.dtype)

        @pl.when(pl.program_id(0) == 0)
        def _():
            dbg_ref[...] = jnp.zeros_like(dbg_ref)

        dbg_ref[:, 0:d] += jnp.sum(dl1, axis=0, keepdims=True)
        dbg_ref[:, d:2 * d] += jnp.sum(dl2, axis=0, keepdims=True)

    b1 = b_gate[:d].reshape(1, d)
    b2 = b_gate[d:].reshape(1, d)
    return _call("gate_bwd", body, (r // br,),
                 [(proj_r, _rows(br, d, cb)), (proj_r, _rows(br, d, cb + 1)), (b1, _whole((1, d))), (b2, _whole((1, d))),
                  (bm, _rows(br, d)), (bfx, _rows(br, d)), (dmerged, _rows(br, d))],
                 [(_sds((r, d), MXU_DTYPE), _rows(br, d)), (_sds((r, d), MXU_DTYPE), _rows(br, d)),
                  (_sds((r, 2 * d), MXU_DTYPE), _rows(br, 2 * d)), (_sds((1, 2 * d), F32), _whole((1, 2 * d)))],
                 sem=("arbitrary",))


HALO = 16
GLU_BWD_BLOCK = 128


def _conv_taps(gp, halo, first_block):
    halo = jnp.where(first_block, 0.0, halo.astype(F32))
    rid = lax.broadcasted_iota(jnp.int32, gp.shape, 0)
    last, prev = halo[HALO - 1:HALO, :], halo[HALO - 2:HALO - 1, :]
    g1 = jnp.where(rid == 0, last, pltpu.roll(gp, 1, axis=0))
    g2 = jnp.where(rid == 0, prev, jnp.where(rid == 1, last, pltpu.roll(gp, 2, axis=0)))
    return g1, g2


def _prev_halo(br, c):
    return pl.BlockSpec((HALO, c), lambda i: (jnp.maximum(i * (br // HALO) - 1, 0), 0))


def _glu_fwd(up, conv_w, conv_b):
    r = up.shape[0]
    c = D_FF
    br = ROW_BLOCK

    def body(gp_ref, halo_ref, val_ref, w_ref, b_ref, o_ref):
        gp = gp_ref[...].astype(F32)
        g1, g2 = _conv_taps(gp, halo_ref[...], pl.program_id(0) == 0)
        gate = w_ref[0:1, :] * g2 + w_ref[1:2, :] * g1 + w_ref[2:3, :] * gp + b_ref[...]
        o_ref[...] = (gate * jax.nn.sigmoid(gate) * val_ref[...].astype(F32)).astype(o_ref.dtype)

    return _call("glu_fwd", body, (r // br,),
                 [(up, _rows(br, c, 0)), (up, _prev_halo(br, c)), (up, _rows(br, c, 1)),
                  (conv_w, _whole((3, c))), (conv_b.reshape(1, c), _whole((1, c)))],
                 [(_sds((r, c), MXU_DTYPE), _rows(br, c))], sem=("parallel",))[0]


def _glu_bwd(up, conv_w, conv_b, d_act):
    r = up.shape[0]
    c = D_FF
    br = GLU_BWD_BLOCK
    nb = r // br

    def body(gp_ref, halo_ref, val_ref, da_ref, gpn_ref, valn_ref, dan_ref, w_ref, b_ref, o_ref, dw_ref, db_ref):
        i = pl.program_id(0)
        w0, w1, w2, bias = w_ref[0:1, :], w_ref[1:2, :], w_ref[2:3, :], b_ref[...]

        def d_gate(gp, g1, g2, val, da):
            gate = w0 * g2 + w1 * g1 + w2 * gp + bias
            sg = jax.nn.sigmoid(gate)
            return da * val * (sg * (1.0 + gate * (1.0 - sg))), da * (gate * sg)

        gp = gp_ref[...].astype(F32)
        g1, g2 = _conv_taps(gp, halo_ref[...], i == 0)
        dg, dv = d_gate(gp, g1, g2, val_ref[...].astype(F32), da_ref[...].astype(F32))
        gpn = gpn_ref[...].astype(F32)
        g1n, g2n = _conv_taps(gpn, gp[br - HALO:, :], False)
        dgn, _ = d_gate(gpn, g1n, g2n, valn_ref[...].astype(F32), dan_ref[...].astype(F32))
        dgn = jnp.where(i == nb - 1, 0.0, dgn)
        rid = lax.broadcasted_iota(jnp.int32, dg.shape, 0)
        u1 = jnp.where(rid == br - 1, dgn[0:1, :], pltpu.roll(dg, br - 1, axis=0))
        u2 = jnp.where(rid == br - 1, dgn[1:2, :], jnp.where(rid == br - 2, dgn[0:1, :], pltpu.roll(dg, br - 2, axis=0)))
        o_ref[:, 0:c] = (w2 * dg + w1 * u1 + w0 * u2).astype(o_ref.dtype)
        o_ref[:, c:2 * c] = dv.astype(o_ref.dtype)

        @pl.when(i == 0)
        def _():
            dw_ref[...] = jnp.zeros_like(dw_ref)
            db_ref[...] = jnp.zeros_like(db_ref)

        dw_ref[0:1, :] += jnp.sum(dg * g2, axis=0, keepdims=True)
        dw_ref[1:2, :] += jnp.sum(dg * g1, axis=0, keepdims=True)
        dw_ref[2:3, :] += jnp.sum(dg * gp, axis=0, keepdims=True)
        db_ref[...] += jnp.sum(dg, axis=0, keepdims=True)

    nxt = lambda cb: pl.BlockSpec((HALO, c), lambda i: (jnp.minimum((i + 1) * (br // HALO), r // HALO - 1), cb))
    return _call("glu_bwd", body, (nb,),
                 [(up, _rows(br, c, 0)), (up, _prev_halo(br, c)), (up, _rows(br, c, 1)), (d_act, _rows(br, c)),
                  (up, nxt(0)), (up, nxt(1)), (d_act, nxt(0)),
                  (conv_w, _whole((3, c))), (conv_b.reshape(1, c), _whole((1, c)))],
                 [(_sds((r, 2 * c), MXU_DTYPE), _rows(br, 2 * c)),
                  (_sds((8, c), F32), _whole((8, c))), (_sds((1, c), F32), _whole((1, c)))],
                 sem=("arbitrary",))


def _token_specs(seq, d):
    br = ROW_BLOCK
    nxb = seq // br
    main = pl.BlockSpec((br, d), lambda i: (jnp.minimum(i, nxb - 1), 0))
    tail = pl.BlockSpec((N_META, d), lambda i: (jnp.clip(i * (br // N_META) - 1, 0, seq // N_META - 1), 0))
    return main, tail


def _padded_block(main_ref, tail_ref, first, seq):
    br = ROW_BLOCK
    i = pl.program_id(0)
    nxb = seq // br
    main = jnp.where(i < nxb, main_ref[...], 0.0)
    head = jnp.where(i == 0, first, jnp.where(i <= nxb, tail_ref[...], 0.0))
    return jnp.concatenate([head, main[:br - N_META]], axis=0)


def _ln_emb_fwd(x, meta, g, b, rows, after=()):
    seq, d = x.shape
    br = ROW_BLOCK
    assert seq % br == 0 and br % N_META == 0 and rows % br == 0

    def body(x_ref, tail_ref, meta_ref, g_ref, b_ref, y_ref, yb_ref):
        z = _padded_block(x_ref, tail_ref, meta_ref[...], seq)
        xhat, _ = _ln_stats(z)
        y = xhat * g_ref[...] + b_ref[...]
        y_ref[...] = y
        yb_ref[...] = y.astype(yb_ref.dtype)

    main, tail = _token_specs(seq, d)
    return _call("ln_emb_fwd", body, (rows // br,),
                 [(x, main), (x, tail), (meta, _whole((N_META, d))), (g.reshape(1, d), _whole((1, d))),
                  (b.reshape(1, d), _whole((1, d)))],
                 [(_sds((rows, d), F32), _rows(br, d)), (_sds((rows, d), MXU_DTYPE), _rows(br, d))],
                 sem=("parallel",), after=after)


def _ln_emb_bwd(x, meta, dh0, g):
    seq, d = x.shape
    br = ROW_BLOCK
    step = br // N_META

    def ln_bwd(z, dy, gv):
        xhat, rstd = _ln_stats(z)
        dyg = dy * gv
        m1 = jnp.mean(dyg, axis=-1, keepdims=True)
        m2 = jnp.mean(dyg * xhat, axis=-1, keepdims=True)
        dz = rstd * (dyg - m1 - xhat * m2)
        return dz, jnp.sum(dy * xhat, axis=0, keepdims=True), jnp.sum(dy, axis=0, keepdims=True)

    def body(x_ref, dh_ref, nxt_ref, meta_ref, top_ref, g_ref, dx_ref, dm_ref, dg_ref, db_ref):
        gv = g_ref[...]
        dy = jnp.concatenate([dh_ref[N_META:, :], nxt_ref[...]], axis=0)
        dz, dg, db = ln_bwd(x_ref[...], dy, gv)
        dx_ref[...] = dz

        @pl.when(pl.program_id(0) == 0)
        def _():
            dzm, dgm, dbm = ln_bwd(meta_ref[...], top_ref[...], gv)
            dm_ref[...] = dzm
            dg_ref[...] = dgm
            db_ref[...] = dbm

        dg_ref[...] += dg
        db_ref[...] += db

    small = _whole((N_META, d))
    return _call("ln_emb_bwd", body, (seq // br,),
                 [(x, _rows(br, d)), (dh0, _rows(br, d)), (dh0, pl.BlockSpec((N_META, d), lambda i: ((i + 1) * step, 0))),
                  (meta, small), (dh0, small), (g.reshape(1, d), _whole((1, d)))],
                 [(_sds((seq, d), F32), _rows(br, d)), (_sds((N_META, d), F32), small),
                  (_sds((1, d), F32), _whole((1, d))), (_sds((1, d), F32), _whole((1, d)))], sem=("arbitrary",))


def _ln_ffn_loss(h1, f, tgt, g, b):
    r, d = h1.shape
    seq = tgt.shape[0]
    br = ROW_BLOCK

    def body(a_ref, r_ref, t_ref, tail_ref, g_ref, b_ref, l_ref):
        err = _loss_err(a_ref, r_ref, t_ref, tail_ref, g_ref, b_ref, seq)[0]

        @pl.when(pl.program_id(0) == 0)
        def _():
            l_ref[...] = jnp.zeros_like(l_ref)

        l_ref[...] += jnp.sum(jnp.sum(err * err, axis=1, keepdims=True), axis=0, keepdims=True) * (0.5 / d)

    main, tail = _token_specs(seq, d)
    return _call("ln_ffn_loss", body, (r // br,),
                 [(h1, _rows(br, d)), (f, _rows(br, d)), (tgt, main), (tgt, tail),
                  (g.reshape(1, d), _whole((1, d))), (b.reshape(1, d), _whole((1, d)))],
                 [(_sds((1, 1), F32), _whole((1, 1)))], sem=("arbitrary",))[0]


def _loss_err(a_ref, r_ref, t_ref, tail_ref, g_ref, b_ref, seq):
    br, d = a_ref.shape
    xhat, rstd = _ln_stats(ALPHA * a_ref[...] + r_ref[...])
    y = xhat * g_ref[...] + b_ref[...]
    t = _padded_block(t_ref, tail_ref, jnp.zeros((N_META, d), F32), seq)
    rid = lax.broadcasted_iota(jnp.int32, (br, d), 0) + pl.program_id(0) * br
    valid = (rid >= N_META) & (rid < N_META + seq)
    return jnp.where(valid, y - t, 0.0), xhat, rstd


def _ln_ffn_bwd(h1, f, tgt, g, b):
    r, d = h1.shape
    seq = tgt.shape[0]
    br = ROW_BLOCK

    def body(a_ref, r_ref, t_ref, tail_ref, g_ref, b_ref, dz_ref, dzb_ref, dg_ref, db_ref):
        err, xhat, rstd = _loss_err(a_ref, r_ref, t_ref, tail_ref, g_ref, b_ref, seq)
        dyv = err * (1.0 / d)
        dyg = dyv * g_ref[...]
        m1 = jnp.mean(dyg, axis=-1, keepdims=True)
        m2 = jnp.mean(dyg * xhat, axis=-1, keepdims=True)
        dz = rstd * (dyg - m1 - xhat * m2)
        dz_ref[...] = dz
        dzb_ref[...] = dz.astype(dzb_ref.dtype)

        @pl.when(pl.program_id(0) == 0)
        def _():
            dg_ref[...] = jnp.zeros_like(dg_ref)
            db_ref[...] = jnp.zeros_like(db_ref)

        dg_ref[...] += jnp.sum(dyv * xhat, axis=0, keepdims=True)
        db_ref[...] += jnp.sum(dyv, axis=0, keepdims=True)

    main, tail = _token_specs(seq, d)
    return _call("ln_ffn_bwd", body, (r // br,),
                 [(h1, _rows(br, d)), (f, _rows(br, d)), (tgt, main), (tgt, tail),
                  (g.reshape(1, d), _whole((1, d))), (b.reshape(1, d), _whole((1, d)))],
                 [(_sds((r, d), F32), _rows(br, d)), (_sds((r, d), MXU_DTYPE), _rows(br, d)),
                  (_sds((1, d), F32), _whole((1, d))), (_sds((1, d), F32), _whole((1, d)))], sem=("arbitrary",))


def _attn_fwd(name, q, k, v, cum_b=None, cum_t=None):
    (qa, qg), (ka, kg), (va, vg) = q, k, v
    r = qa.shape[0]
    tq, tk = ATT_TQ, ATT_TK
    nq, nk = r // tq, r // tk
    bias = cum_b is not None

    def body(*refs):
        if bias:
            q_ref, k_ref, vt_ref, cb_ref, ct_ref, o_ref, ob_ref, lse_ref = refs
        else:
            q_ref, k_ref, vt_ref, o_ref, ob_ref, lse_ref = refs
        i = pl.program_id(1)
        qs = [q_ref[:, _hs(hh)] for hh in range(hg)]
        cqs = [ct_ref[hh] for hh in range(hg)] if bias else None
        diff = lax.broadcasted_iota(jnp.int32, (tk, tq), 0) - lax.broadcasted_iota(jnp.int32, (tk, tq), 1)

        def step(j, carry, masked):
            keys = pl.ds(pl.multiple_of(j * tk, tk), tk)
            out = []
            for hh in range(hg):
                m, l, acc = carry[hh]
                kt = k_ref[keys, _hs(hh)]
                s = lax.dot_general(kt, qs[hh], NT, preferred_element_type=F32)
                if bias:
                    s = s + (cqs[hh] - cb_ref[keys, hh * HP:hh * HP + 1])
                if masked:
                    s = jnp.where(diff <= i * tq - j * tk, s, NEG_INF)
                m_new = jnp.maximum(m, jnp.max(s, axis=0, keepdims=True))
                p = jnp.exp(s - m_new)
                a = jnp.exp(m - m_new)
                l = a * l + jnp.sum(p, axis=0, keepdims=True)
                acc = a * acc + jnp.dot(vt_ref[j, _hs(hh), :], p.astype(kt.dtype), preferred_element_type=F32)
                out.append((m_new, l, acc))
            return tuple(out)

        n_clear = (i * tq + 1) // tk
        n_all = ((i + 1) * tq - 1) // tk + 1
        carry = tuple((jnp.full((1, tq), NEG_INF, F32), jnp.zeros((1, tq), F32), jnp.zeros((HP, tq), F32))
                      for _ in range(hg))
        carry = lax.fori_loop(0, n_clear, lambda j, c: step(j, c, False), carry)
        carry = lax.fori_loop(n_clear, n_all, lambda j, c: step(j, c, True), carry)
        for hh in range(hg):
            m, l, acc = carry[hh]
            o = (acc / l).T
            o_ref[:, _hs(hh)] = o
            ob_ref[:, _hs(hh)] = o.astype(ob_ref.dtype)
            lse_ref[hh] = m + jnp.log(l)

    hg = ATT_HEADS
    w = hg * HP
    gpw = HW // w
    tile = lambda g: pl.BlockSpec((tq, w), lambda h, i: (i, g * gpw + h))
    res = lambda g: pl.BlockSpec((r, w), lambda h, i: (0, g * gpw + h))
    v_t = _key_tiles_transposed(name + "_vt", va, vg)
    ins = [(qa, tile(qg)), (ka, res(kg)), (v_t, pl.BlockSpec((nk, w, tk), lambda h, i: (0, h, 0)))]
    if bias:
        ins += [(cum_b, res(0)),
                (cum_t.reshape(HEADS, nq, 1, tq), pl.BlockSpec((hg, None, 1, tq), lambda h, i: (h, i, 0, 0)))]
    outs = [(_sds((r, HW), F32), tile(0)), (_sds((r, HW), MXU_DTYPE), tile(0)),
            (_sds((HEADS, nq, 1, tq), F32), pl.BlockSpec((hg, None, 1, tq), lambda h, i: (h, i, 0, 0)))]
    o, ob, lse = _call(name, body, (gpw, nq), ins, outs, sem=("parallel", "parallel"))
    return o, ob, lse.reshape(HEADS, r)


def _key_tiles_transposed(name, a, group):
    r = a.shape[0]
    tk = ATT_TK

    def body(x_ref, o_ref):
        for h in range(HEADS):
            o_ref[_hs(h), :] = x_ref[:, _hs(h)].astype(F32).T.astype(o_ref.dtype)

    return _call(name, body, (r // tk,),
                 [(a, pl.BlockSpec((tk, HW), lambda j: (j, group)))],
                 [(_sds((r // tk, HW, tk), a.dtype), pl.BlockSpec((None, HW, tk), lambda j: (j, 0, 0)))],
                 sem=("parallel",))[0]


def _attn_bwd(name, q, k, v, do_b, o, lse_t, cum_b=None, cum_t=None, out_dtype=F32, after=()):
    (qa, qg), (ka, kg), (va, vg) = q, k, v
    r = qa.shape[0]
    tq, tk = ATT_TQ, ATT_TK
    nq, nk = r // tq, r // tk
    bias = cum_b is not None

    def body(*refs):
        if bias:
            (q_ref, k_ref, v_ref, do_ref, o_ref, lse_ref, cb_ref, ct_ref,
             dq_ref, dk_ref, dv_ref, dcq_ref, dck_ref, dqt_ref, dl_ref) = refs
        else:
            q_ref, k_ref, v_ref, do_ref, o_ref, lse_ref, dq_ref, dk_ref, dv_ref, dqt_ref, dl_ref = refs
        j = pl.program_id(1)

        @pl.when(j == 0)
        def _():
            dqt_ref[...] = jnp.zeros_like(dqt_ref)
            if bias:
                dcq_ref[...] = jnp.zeros_like(dcq_ref)
            for hh in range(hg):
                for i in range(nq):
                    rows = slice(i * tq, (i + 1) * tq)
                    prod = do_ref[rows, _hs(hh)].astype(F32) * o_ref[rows, _hs(hh)]
                    dl_ref[hh, i] = jnp.sum(prod.T, axis=0, keepdims=True)

        kts = [k_ref[:, _hs(hh)] for hh in range(hg)]
        vts = [v_ref[:, _hs(hh)] for hh in range(hg)]
        k_trs = [kt.astype(F32).T.astype(kt.dtype) for kt in kts]
        cks = [cb_ref[:, hh * HP:hh * HP + 1] for hh in range(hg)] if bias else None
        diff = lax.broadcasted_iota(jnp.int32, (tk, tq), 0) - lax.broadcasted_iota(jnp.int32, (tk, tq), 1)

        def step(i, carry, masked):
            rows = pl.ds(pl.multiple_of(i * tq, tq), tq)
            out = []
            for hh in range(hg):
                dk_acc, dv_acc, dck_acc = carry[hh]
                qt = q_ref[rows, _hs(hh)]
                dot = do_ref[rows, _hs(hh)]
                s = lax.dot_general(kts[hh], qt, NT, preferred_element_type=F32)
                if bias:
                    s = s + (ct_ref[hh, i] - cks[hh])
                if masked:
                    s = jnp.where(diff <= i * tq - j * tk, s, NEG_INF)
                p = jnp.exp(s - lse_ref[hh, i])
                dp = lax.dot_general(vts[hh], dot, NT, preferred_element_type=F32)
                ds = p * (dp - dl_ref[hh, i])
                pb = p.astype(dot.dtype)
                dsb = ds.astype(qt.dtype)
                dv_acc = dv_acc + jnp.dot(pb, dot, preferred_element_type=F32)
                dk_acc = dk_acc + jnp.dot(dsb, qt, preferred_element_type=F32)
                dqt_ref[hh, i] += jnp.dot(k_trs[hh], dsb, preferred_element_type=F32)
                if bias:
                    dcq_ref[hh, i] += jnp.sum(ds, axis=0, keepdims=True)
                    dck_acc = dck_acc - jnp.sum(ds, axis=1, keepdims=True)
                out.append((dk_acc, dv_acc, dck_acc))
            return tuple(out)

        i_first = (j * tk) // tq
        i_clear = jnp.minimum(((j + 1) * tk + tq - 2) // tq, nq)
        carry = tuple((jnp.zeros((tk, HP), F32), jnp.zeros((tk, HP), F32), jnp.zeros((tk, 1), F32)) for _ in range(hg))
        carry = lax.fori_loop(i_first, i_clear, lambda i, c: step(i, c, True), carry)
        carry = lax.fori_loop(i_clear, nq, lambda i, c: step(i, c, False), carry)
        for hh in range(hg):
            dk_acc, dv_acc, dck_acc = carry[hh]
            dk_ref[:, _hs(hh)] = dk_acc.astype(dk_ref.dtype)
            dv_ref[:, _hs(hh)] = dv_acc.astype(dv_ref.dtype)
            if bias:
                dck_ref[:, _hs(hh)] = jnp.broadcast_to(dck_acc, (tk, HP))

        @pl.when(j == nk - 1)
        def _():
            for hh in range(hg):
                for i in range(nq):
                    dq_ref[i * tq:(i + 1) * tq, _hs(hh)] = dqt_ref[hh, i].T.astype(dq_ref.dtype)

    hg = ATT_HEADS
    w = hg * HP
    gpw = HW // w
    res = lambda g: pl.BlockSpec((r, w), lambda h, j: (0, g * gpw + h))
    tile = lambda g: pl.BlockSpec((tk, w), lambda h, j: (j, g * gpw + h))
    rowv = pl.BlockSpec((hg, nq, 1, tq), lambda h, j: (h, 0, 0, 0))
    as_rows = lambda a: a.reshape(HEADS, nq, 1, tq)
    ins = [(qa, res(qg)), (ka, tile(kg)), (va, tile(vg)), (do_b, res(0)), (o, res(0)), (as_rows(lse_t), rowv)]
    outs = [(_sds((r, HW), out_dtype), res(0)), (_sds((r, HW), out_dtype), tile(0)), (_sds((r, HW), out_dtype), tile(0))]
    if bias:
        ins += [(cum_b, tile(0)), (as_rows(cum_t), rowv)]
        outs += [(_sds((HEADS, nq, 1, tq), F32), rowv), (_sds((r, HW), F32), tile(0))]
    res_out = _call(name, body, (gpw, nk), ins, outs,
                    scratch=[pltpu.VMEM((hg, nq, HP, tq), F32), pltpu.VMEM((hg, nq, 1, tq), F32)],
                    sem=("parallel", "arbitrary"), after=after)
    if bias:
        dq, dk, dv, dcq, dck = res_out
        return dq, dk, dv, dcq.reshape(HEADS, r), dck
    return res_out


MESH_ID = pl.DeviceIdType.MESH
ANY = pl.BlockSpec(memory_space=pl.ANY)


N_GATHER_COPIES = 8


def _allgather(name, shards):
    n = len(shards)

    def body(*refs):
        x_refs, out_refs = refs[:n], refs[n:2 * n]
        send_sems, recv_sems, local_sems = refs[2 * n:]
        x, y, c = lax.axis_index("x"), lax.axis_index("y"), lax.axis_index("c")
        me, sibling = (x, y, c), (x, y, 1 - c)
        xn, yn, dg = (1 - x, y, c), (x, 1 - y, c), (1 - x, 1 - y, c)
        other = lambda dev: (dev[0], dev[1], 1 - c)

        def slot(ti, dev, half=None):
            ref = out_refs[ti].at[4 * dev[0] + 2 * dev[1] + dev[2]]
            if half is None:
                return ref
            rows = shards[ti].shape[0] // 2
            return ref.at[pl.ds(half * rows, rows)]

        def copy(ti, k, block, to, half=None, src=None):
            return pltpu.make_async_remote_copy(
                src_ref=slot(ti, block, half) if src is None else src, dst_ref=slot(ti, block, half),
                send_sem=send_sems.at[ti, k], recv_sem=recv_sems.at[ti, k], device_id=to, device_id_type=MESH_ID)

        mine = [pltpu.make_async_copy(x_refs[ti], slot(ti, me), local_sems.at[ti]) for ti in range(n)]
        for cp in mine:
            cp.start()
        started = []

        def go(cp):
            cp.start()
            started.append(cp)

        for ti in range(n):
            go(copy(ti, 0, me, sibling, src=x_refs[ti]))
            go(copy(ti, 1, me, xn, src=x_refs[ti]))
            go(copy(ti, 2, me, yn, src=x_refs[ti]))
        for ti in range(n):
            copy(ti, 1, xn, me).wait_recv()
            go(copy(ti, 3, xn, yn, half=0))
            go(copy(ti, 5, xn, sibling))
            copy(ti, 2, yn, me).wait_recv()
            go(copy(ti, 4, yn, xn, half=1))
            go(copy(ti, 6, yn, sibling))
        for ti in range(n):
            copy(ti, 3, dg, me, half=0).wait_recv()
            copy(ti, 4, dg, me, half=1).wait_recv()
            go(copy(ti, 7, dg, sibling))
        for ti in range(n):
            copy(ti, 0, sibling, me).wait_recv()
            for k, dev in ((5, xn), (6, yn), (7, dg)):
                copy(ti, k, other(dev), me).wait_recv()
        for cp in started:
            cp.wait_send()
        for cp in mine:
            cp.wait()

    sems = pltpu.SemaphoreType.DMA((n, N_GATHER_COPIES))
    return pl.pallas_call(
        body, name=name, out_shape=[_sds((N_DEV,) + s.shape, s.dtype) for s in shards],
        in_specs=[ANY] * n, out_specs=[ANY] * n,
        scratch_shapes=[sems, sems, pltpu.SemaphoreType.DMA((n,))],
    )(*shards)


HBM = pl.BlockSpec(memory_space=pltpu.HBM)
SEM = pl.BlockSpec(memory_space=pltpu.SEMAPHORE)
EFFECT = pltpu.SideEffectType.DATAFLOW_SIDE_EFFECTING
N_PEER = N_DEV - 1


def _my_id():
    return 4 * lax.axis_index("x") + 2 * lax.axis_index("y") + lax.axis_index("c")


def _peers():
    x, y, c = lax.axis_index("x"), lax.axis_index("y"), lax.axis_index("c")
    out = []
    for k in range(1, N_DEV):
        px, py, pc = (1 - x if k & 4 else x, 1 - y if k & 2 else y, 1 - c if k & 1 else c)
        out.append(((px, py, pc), 4 * px + 2 * py + pc))
    return out


def _push_copies(src_refs, land_refs, send_sems, recv_sems, scatter, landing):
    me = _my_id()
    out = []
    for ti, (src, land) in enumerate(zip(src_refs, land_refs)):
        for k, (dev, pid) in enumerate(_peers()):
            out.append(pltpu.make_async_remote_copy(
                src_ref=src.at[pid] if scatter else src, dst_ref=land.at[pid if landing else me],
                send_sem=send_sems.at[ti * N_PEER + k], recv_sem=recv_sems.at[ti * N_PEER + k],
                device_id=dev, device_id_type=MESH_ID))
    return out


def _push_start(name, groups, scatter, after=None):
    sizes = [len(g) for g in groups]
    srcs = [a for g in groups for a in g]
    n = len(srcs)
    slot = lambda s: s.shape[1:] if scatter else s.shape
    lands = [lax.empty((N_DEV,) + slot(s), s.dtype) for s in srcs]
    n_after = 0 if after is None else 1
    n_grp = len(groups)

    def body(*refs):
        src_refs, land_refs = refs[:n], refs[n:2 * n]
        sems = refs[2 * n + n_after:2 * n + n_after + 2 * n_grp]
        token = refs[-1]
        lo = 0
        for gi, sz in enumerate(sizes):
            for cp in _push_copies(src_refs[lo:lo + sz], land_refs[lo:lo + sz], sems[2 * gi], sems[2 * gi + 1], scatter, False):
                cp.start()
            lo += sz
        token[...] = jnp.zeros_like(token)

    hbm = lambda a: pltpu.with_memory_space_constraint(a, pltpu.HBM)
    operands = [hbm(a) for a in srcs + lands] + ([after] if n_after else [])
    sem_shapes = [pltpu.SemaphoreType.DMA((sz * N_PEER,)) for sz in sizes for _ in range(2)]
    res = pl.pallas_call(
        body, name=name,
        out_shape=sem_shapes + [pltpu.HBM(a.shape, a.dtype) for a in srcs + lands] + [_sds((8, 128), F32)],
        in_specs=[HBM] * (2 * n) + [ANY] * n_after,
        out_specs=[SEM] * (2 * n_grp) + [HBM] * (2 * n) + [pl.BlockSpec(memory_space=pltpu.VMEM)],
        input_output_aliases={i: 2 * n_grp + i for i in range(2 * n)},
        compiler_params=pltpu.CompilerParams(has_side_effects=EFFECT),
    )(*operands)
    thru = res[2 * n_grp:2 * n_grp + 2 * n]
    handles, lo = [], 0
    for gi, sz in enumerate(sizes):
        handles.append((res[2 * gi], res[2 * gi + 1], list(thru[lo:lo + sz]), list(thru[n + lo:n + lo + sz]), scatter))
        lo += sz
    return handles, res[-1]


def _push_wait(name, handle, after):
    send_sems, recv_sems, srcs, lands, scatter = handle
    n = len(srcs)

    def body(*refs):
        src_refs, land_refs = refs[:n], refs[n:2 * n]
        s_sems, r_sems = refs[2 * n], refs[2 * n + 1]
        for cp in _push_copies(src_refs, land_refs, s_sems, r_sems, scatter, True):
            cp.wait_send()
            cp.wait_recv()

    res = pl.pallas_call(
        body, name=name,
        out_shape=[pltpu.HBM(a.shape, a.dtype) for a in srcs + lands],
        in_specs=[HBM] * (2 * n) + [SEM, SEM, ANY], out_specs=[HBM] * (2 * n),
        input_output_aliases={i: i for i in range(2 * n)},
        compiler_params=pltpu.CompilerParams(has_side_effects=EFFECT),
    )(*srcs, *lands, send_sems, recv_sems, after)
    return list(res[n:])


def _adamw(name, parts, w, m, v, own=None):
    r, c = w.shape
    br = _pick(r, 256, 16)
    has_own = own is not None

    def body(*refs):
        if has_own:
            p_ref, own_ref, w_ref, m_ref, v_ref, g_ref, d_ref, nm_ref, nv_ref = refs
            me = _my_id()
            mine = own_ref[...].astype(F32)
        else:
            p_ref, w_ref, m_ref, v_ref, g_ref, d_ref, nm_ref, nv_ref = refs
        g = None
        for k in range(N_DEV):
            t = p_ref[k].astype(F32)
            if has_own:
                t = jnp.where(me == k, mine, t)
            g = t if g is None else g + t
        mm = ADAM_B1 * m_ref[...] + (1.0 - ADAM_B1) * g
        vv = ADAM_B2 * v_ref[...] + (1.0 - ADAM_B2) * (g * g)
        m_hat = mm / (1.0 - ADAM_B1 ** ADAM_STEP)
        v_hat = vv / (1.0 - ADAM_B2 ** ADAM_STEP)
        g_ref[...] = g
        d_ref[...] = -ADAM_LR * (m_hat / (jnp.sqrt(v_hat) + ADAM_EPS) + ADAM_WD * w_ref[...])
        nm_ref[...] = mm
        nv_ref[...] = vv

    spec = _rows(br, c)
    out = (_sds((r, c), F32), spec)
    ins = [(parts, pl.BlockSpec((N_DEV, br, c), lambda i: (0, i, 0)))] + ([(own, spec)] if has_own else [])
    return _call(name, body, (r // br,), ins + [(w, spec), (m, spec), (v, spec)], [out] * 4, sem=("parallel",))


def _pad_head_cols(w, d):
    k = w.shape[0]
    return jnp.pad(w.reshape(k, HEADS, d), ((0, 0), (0, 0), (0, HP - d))).reshape(k, HW)


def _unpad_head_cols(wp, d):
    k = wp.shape[0]
    return wp.reshape(k, HEADS, HP)[:, :, :d].reshape(k, HEADS * d)


def _pad_head_rows(w, d):
    n = w.shape[1]
    return jnp.pad(w.reshape(HEADS, d, n), ((0, 0), (0, HP - d), (0, 0))).reshape(HW, n)


def _unpad_head_rows(wp, d):
    n = wp.shape[1]
    return wp.reshape(HEADS, HP, n)[:, :d, :].reshape(HEADS * d, n)


def _w_in_runs():
    nat = {}
    o = 0
    for nm, wd in (("q", Q_RANK), ("kv", KV_RANK), ("kr", ROPE), ("fq", FOX_W), ("fk", FOX_W), ("fv", FOX_W),
                   ("fl", HEADS), ("gate", 2 * D_MODEL)):
        nat[nm] = o
        o += wd
    runs = [(1, R_QLAT, nat["q"], Q_RANK, 1.0), (1, R_KVLAT, nat["kv"], KV_RANK, 1.0),
            (1, R_LAST + LANE_FL, nat["fl"], HEADS, 1.0), (1, R_LAST + LANE_PE, nat["kr"], ROPE, 1.0),
            (1, R_GATE, nat["gate"], 2 * D_MODEL, 1.0)]
    for grp, (nm, sc) in enumerate((("fq", FOX_SCALE), ("fk", 1.0), ("fv", 1.0))):
        runs += [(0, grp * HW + h * HP, nat[nm] + h * FOX_DIM, FOX_DIM, sc) for h in range(HEADS)]
    return runs


def _sharded_runs(runs, shard_cols):
    out = []
    for half, col, ncol, width, sc in runs:
        while width > 0:
            d, local = divmod(ncol, shard_cols)
            wd = min(width, shard_cols - local)
            out.append((half, col, d, local, wd, sc))
            col, ncol, width = col + wd, ncol + wd, width - wd
    return out


def _remap(name, srcs, out_shapes, moves):
    rows = srcs[0].shape[-2]
    br = _pick(rows, 256, 16)
    ns = len(srcs)

    def spec(shape):
        if len(shape) == 2:
            return pl.BlockSpec((br, shape[1]), lambda i: (i, 0))
        return pl.BlockSpec((shape[0], br, shape[2]), lambda i: (0, i, 0))

    def body(*refs):
        s_refs, o_refs = refs[:ns], refs[ns:]
        for o in o_refs:
            o[...] = jnp.zeros_like(o)
        for di, dl, dc, si, sl, sc0, wd, scale in moves:
            v = s_refs[si][:, sc0:sc0 + wd] if sl is None else s_refs[si][sl, :, sc0:sc0 + wd]
            if scale != 1.0:
                v = v * jnp.asarray(scale, v.dtype)
            v = v.astype(o_refs[di].dtype)
            if dl is None:
                o_refs[di][:, dc:dc + wd] = v
            else:
                o_refs[di][dl, :, dc:dc + wd] = v

    return _call(name, body, (rows // br,), [(a, spec(a.shape)) for a in srcs],
                 [(_sds(shape, dt), spec(shape)) for shape, dt in out_shapes], sem=("parallel",))


def _w_in_from_shards(g3):
    n, rows, c = g3.shape
    moves = [(half, None, col, 0, d, local, wd, sc) for half, col, d, local, wd, sc in _sharded_runs(_w_in_runs(), c)]
    return _remap("w_in_repack", [g3], [((rows, F_W), g3.dtype), ((rows, R_W), g3.dtype)], moves)


def _w_in_grad_to_shards(d_fused, d_rest, n, c):
    rows = d_fused.shape[0]
    moves = [(0, d, local, half, None, col, wd, sc) for half, col, d, local, wd, sc in _sharded_runs(_w_in_runs(), c)]
    return _remap("w_in_grad_unpack", [d_fused, d_rest], [((n, rows, c), d_fused.dtype)], moves)[0]


def _rows_from_shards(name, land, own):
    n, rows, c = land.shape

    def body(land_ref, own_ref, o_ref):
        o_ref[...] = jnp.where(_my_id() == pl.program_id(0), own_ref[...], land_ref[...])

    return _call(name, body, (n,),
                 [(land, pl.BlockSpec((None, rows, c), lambda d: (d, 0, 0))), (own, _whole((rows, c)))],
                 [(_sds((n * rows, c), land.dtype), pl.BlockSpec((rows, c), lambda d: (d, 0)))], sem=("parallel",))[0]


def _cols_from_shards(name, land, own):
    n, rows, c = land.shape
    br = _pick(rows, 256, 16)

    def body(land_ref, own_ref, o_ref):
        me = _my_id()
        for d in range(n):
            o_ref[:, c * d:c * (d + 1)] = jnp.where(me == d, own_ref[...], land_ref[d])

    return _call(name, body, (rows // br,),
                 [(land, pl.BlockSpec((n, br, c), lambda i: (0, i, 0))), (own, _rows(br, c))],
                 [(_sds((rows, n * c), land.dtype), _rows(br, n * c))], sem=("parallel",))[0]


def _cols_to_shards(name, full, n):
    rows, nc = full.shape
    c = nc // n
    return _remap(name, [full], [((n, rows, c), full.dtype)], [(0, d, 0, 0, None, c * d, c, 1.0) for d in range(n)])[0]


def _split_w_kv(w):
    k = w.shape[0]
    w3 = w.reshape(k, HEADS, NOPE + V_DIM)
    padl = lambda a: jnp.pad(a, ((0, 0), (0, 0), (0, HP - a.shape[-1]))).reshape(k, HW)
    return padl(w3[..., :NOPE]), padl(w3[..., NOPE:])


def _merge_w_kv(wk, wv):
    k = wk.shape[0]
    return jnp.concatenate([wk.reshape(k, HEADS, HP)[..., :NOPE], wv.reshape(k, HEADS, HP)[..., :V_DIM]],
                           axis=-1).reshape(k, HEADS * (NOPE + V_DIM))


class _NoComm:
    first_token = ()

    def late_weights(self, group, after):
        return {}

    def send(self, name, grads):
        return ()


def _local_step(x, tgt, p, comm=_NoComm()):
    seq = x.shape[0]
    r = -(-(N_META + seq) // ROW_ALIGN) * ROW_ALIGN
    cd = MXU_DTYPE
    p = dict(p)

    w_f, w_r = p["w_in"]

    pos = jnp.arange(r, dtype=F32)
    inv_freq = ROPE_THETA ** (-jnp.arange(HALF, dtype=F32) / HALF)
    ang = pos[:, None] * inv_freq[None, :]
    cos_t = jnp.tile(jnp.cos(ang), (1, HP // HALF))
    sin_t = jnp.tile(jnp.sin(ang), (1, HP // HALF))
    bf_row = jnp.zeros((1, HP), F32).at[0, LANE_FL:LANE_FL + HEADS].set(p["b_forget"])

    h0, h0b = _ln_emb_fwd(x, p["meta_tokens"], p["ln_emb_g"], p["ln_emb_b"], r, after=comm.first_token)
    proj_f = _matmul("in_proj_f", h0b, w_f, out_dtype=cd)
    proj_r = _matmul("in_proj_r", h0b, w_r)
    latent_gains = (p["q_norm_g"], p["kv_norm_g"])
    ql, kvl = _latent_norm_fwd(proj_r, latent_gains)
    p.update(comm.late_weights("qkv", ql))
    w_q = _pad_head_cols(p["w_q_up"], QK_DIM)
    w_kv = jnp.concatenate(_split_w_kv(p["w_kv_up"]), axis=1)
    q_raw = _matmul("q_up", ql, w_q)
    kv = _matmul("kv_up", kvl, w_kv, out_dtype=cd)
    q_mla, k_mla = _rope_fwd(q_raw, kv, proj_r, cos_t, sin_t)
    o_mla, o_mla_b, lse_mla = _attn_fwd("mla_fwd", (q_mla, 0), (k_mla, 0), (kv, 1))

    cum, cum_t = _forget_fwd(proj_r, bf_row)
    o_fox, o_fox_b, lse_fox = _attn_fwd("fox_fwd", (proj_f, 0), (proj_f, 1), (proj_f, 2), cum, cum_t)

    p.update(comm.late_weights("mix", o_fox_b))
    w_bm = _pad_head_rows(p["w_branch_mla"], V_DIM)
    w_bf = _pad_head_rows(p["w_branch_fox"], FOX_DIM)
    bm = _matmul("branch_mla", o_mla_b, w_bm, out_dtype=cd)
    bfx = _matmul("branch_fox", o_fox_b, w_bf, out_dtype=cd)
    merged = _gate_fwd(proj_r, p["b_gate"], bm, bfx)
    mix = _matmul("out_proj", merged, p["w_out"])
    h1, h1b = _ln_fwd("ln_mix_fwd", h0, mix, p["ln_mix_g"], p["ln_mix_b"])
    p.update(comm.late_weights("ffn", h1b))
    up = _matmul("ffn_up", h1b, p["w_ffn_up"], out_dtype=cd)
    act = _glu_fwd(up, p["conv_w"], p["conv_b"])
    f = _matmul("ffn_down", act, p["w_ffn_down"])
    loss = _ln_ffn_loss(h1, f, tgt, p["ln_ffn_g"], p["ln_ffn_b"])

    g = {}
    dz2, dz2b, g["ln_ffn_g"], g["ln_ffn_b"] = _ln_ffn_bwd(h1, f, tgt, p["ln_ffn_g"], p["ln_ffn_b"])
    d_act = _matmul("ffn_down_dx", dz2b, p["w_ffn_down"], tb=True, out_dtype=cd)
    g["w_ffn_down"] = _matmul("ffn_down_dw", act, dz2b, ta=True, out_dtype=cd)
    d_up, dcw, g["conv_b"] = _glu_bwd(up, p["conv_w"], p["conv_b"], d_act)
    g["conv_w"] = dcw[:3]
    dh1 = _matmul("ffn_up_dx", d_up, p["w_ffn_up"], tb=True, addend=dz2, alpha=ALPHA)
    g["w_ffn_up"] = _matmul("ffn_up_dw", h1b, d_up, ta=True, out_dtype=cd)
    sent = comm.send("ffn", {n: g[n] for n in ("w_ffn_down", "w_ffn_up", "conv_w")})
    dz1, dz1b, g["ln_mix_g"], g["ln_mix_b"] = _ln_bwd("ln_mix_bwd", h0, mix, dh1, p["ln_mix_g"], after=sent)
    dmerged = _matmul("out_proj_dx", dz1b, p["w_out"], tb=True, out_dtype=cd)
    g["w_out"] = _matmul("out_proj_dw", merged, dz1b, ta=True, out_dtype=cd)
    d_bm, d_bf, d_gl, g["b_gate"] = _gate_bwd(proj_r, p["b_gate"], bm, bfx, dmerged)
    do_mla_b = _matmul("branch_mla_dx", d_bm, w_bm, tb=True, out_dtype=cd)
    g["w_branch_mla"] = _unpad_head_rows(_matmul("branch_mla_dw", o_mla_b, d_bm, ta=True, out_dtype=cd), V_DIM)
    do_fox_b = _matmul("branch_fox_dx", d_bf, w_bf, tb=True, out_dtype=cd)
    g["w_branch_fox"] = _unpad_head_rows(_matmul("branch_fox_dw", o_fox_b, d_bf, ta=True, out_dtype=cd), FOX_DIM)

    sent = comm.send("mix", {n: g[n] for n in ("w_out", "w_branch_mla", "w_branch_fox")})
    dq_m, dk_m, dv_m = _attn_bwd("mla_bwd", (q_mla, 0), (k_mla, 0), (kv, 1), do_mla_b, o_mla, lse_mla, after=sent)
    dfq, dfk, dfv, dcq, dck = _attn_bwd("fox_bwd", (proj_f, 0), (proj_f, 1), (proj_f, 2), do_fox_b, o_fox, lse_fox,
                                        cum, cum_t, out_dtype=cd)
    dfl, dbf = _forget_bwd(proj_r, bf_row, dcq, dck)
    g["b_forget"] = dbf[:, LANE_FL:LANE_FL + HEADS]

    dq_b, dkv_b, dlast = _rope_bwd(dq_m, dk_m, dv_m, dfl, cos_t, sin_t)
    d_ql = _matmul("q_up_dx", dq_b, w_q, tb=True)
    d_kvl = _matmul("kv_up_dx", dkv_b, w_kv, tb=True)
    d_qlat, d_kvlat, g["q_norm_g"], g["kv_norm_g"] = _latent_norm_bwd(proj_r, (d_ql, d_kvl), latent_gains)
    side_by_side = lambda parts, cols: [(0, None, c0, si, None, 0, a.shape[1], 1.0) for si, (a, c0) in enumerate(zip(parts, cols))]
    dproj_f = _remap("dproj_f_pack", [dfq, dfk, dfv], [((r, F_W), cd)], side_by_side([dfq, dfk, dfv], (0, HW, 2 * HW)))[0]
    rest_parts = [d_qlat, d_kvlat, dlast, d_gl]
    dproj_r = _remap("dproj_r_pack", rest_parts, [((r, R_W), cd)],
                     side_by_side(rest_parts, (R_QLAT, R_KVLAT, R_LAST, R_GATE)))[0]
    g["w_in"] = (_matmul("in_proj_f_dw", h0b, dproj_f, ta=True, out_dtype=cd),
                 _matmul("in_proj_r_dw", h0b, dproj_r, ta=True, out_dtype=cd))
    sent = comm.send("in", {"w_in": g["w_in"]})
    dh0 = _matmul("in_proj_f_dx", dproj_f, w_f, tb=True, addend=dz1, alpha=ALPHA, after=sent)
    g["w_q_up"] = _unpad_head_cols(_matmul("q_up_dw", ql, dq_b, ta=True, out_dtype=cd, after=sent), QK_DIM)
    dw_kv = _matmul("kv_up_dw", kvl, dkv_b, ta=True, out_dtype=cd, after=sent)
    g["w_kv_up"] = _merge_w_kv(dw_kv[:, :HW], dw_kv[:, HW:])
    sent = comm.send("qkv", {n: g[n] for n in ("w_q_up", "w_kv_up")})
    dh0 = _matmul("in_proj_r_dx", dproj_r, w_r, tb=True, addend=dh0, after=sent)
    grad_x, d_meta, g["ln_emb_g"], g["ln_emb_b"] = _ln_emb_bwd(x, p["meta_tokens"], dh0, p["ln_emb_g"])
    return loss, grad_x, d_meta, g


BIG = (("w_in", 1), ("w_q_up", 1), ("w_kv_up", 1), ("w_branch_mla", 1), ("w_branch_fox", 1), ("w_out", 0),
       ("w_ffn_up", 1), ("w_ffn_down", 0))
SMALL_SHARDED = (("meta_tokens", 1), ("conv_w", 1))
EARLY = ("w_in", "meta_tokens")
LATE = {"qkv": ("w_q_up", "w_kv_up", "conv_w"),
        "mix": ("w_branch_mla", "w_branch_fox", "w_out"),
        "ffn": ("w_ffn_up", "w_ffn_down")}
REPLICATED = ("ln_emb_g", "ln_emb_b", "b_gate", "b_forget", "q_norm_g", "kv_norm_g", "ln_mix_g", "ln_mix_b",
              "conv_b", "ln_ffn_g", "ln_ffn_b")
PACK_COLS = 1024


def _pack(flat_list):
    cat = jnp.concatenate(flat_list)
    n = cat.shape[0]
    rows = -(-n // (8 * PACK_COLS)) * 8
    return jnp.pad(cat, (0, rows * PACK_COLS - n)).reshape(rows, PACK_COLS)


def _gathered_full(g3, axis):
    n, r, c = g3.shape
    if axis == 0:
        return g3.reshape(n * r, c)
    return g3.transpose(1, 0, 2).reshape(r, n * c)


def _shard_major(full, axis):
    r, c = full.shape
    if axis == 0:
        return full.reshape(N_DEV, r // N_DEV, c)
    return full.reshape(r, N_DEV, c // N_DEV).transpose(1, 0, 2)


def kernel(x, meta_tokens, ln_emb_g, ln_emb_b, w_in, b_gate, b_forget, q_norm_g, w_q_up, kv_norm_g, w_kv_up, w_branch_mla, w_branch_fox, w_out, ln_mix_g, ln_mix_b, w_ffn_up, conv_w, conv_b, w_ffn_down, ln_ffn_g, ln_ffn_b, loss_target, m_meta_tokens, m_ln_emb_g, m_ln_emb_b, m_w_in, m_b_gate, m_b_forget, m_q_norm_g, m_w_q_up, m_kv_norm_g, m_w_kv_up, m_w_branch_mla, m_w_branch_fox, m_w_out, m_ln_mix_g, m_ln_mix_b, m_w_ffn_up, m_conv_w, m_conv_b, m_w_ffn_down, m_ln_ffn_g, m_ln_ffn_b, v_meta_tokens, v_ln_emb_g, v_ln_emb_b, v_w_in, v_b_gate, v_b_forget, v_q_norm_g, v_w_q_up, v_kv_norm_g, v_w_kv_up, v_w_branch_mla, v_w_branch_fox, v_w_out, v_ln_mix_g, v_ln_mix_b, v_w_ffn_up, v_conv_w, v_conv_b, v_w_ffn_down, v_ln_ffn_g, v_ln_ffn_b):
    names = ("meta_tokens", "ln_emb_g", "ln_emb_b", "w_in", "b_gate", "b_forget", "q_norm_g", "w_q_up", "kv_norm_g",
             "w_kv_up", "w_branch_mla", "w_branch_fox", "w_out", "ln_mix_g", "ln_mix_b", "w_ffn_up", "conv_w", "conv_b",
             "w_ffn_down", "ln_ffn_g", "ln_ffn_b")
    w_args = (meta_tokens, ln_emb_g, ln_emb_b, w_in, b_gate, b_forget, q_norm_g, w_q_up, kv_norm_g, w_kv_up,
              w_branch_mla, w_branch_fox, w_out, ln_mix_g, ln_mix_b, w_ffn_up, conv_w, conv_b, w_ffn_down, ln_ffn_g, ln_ffn_b)
    m_args = (m_meta_tokens, m_ln_emb_g, m_ln_emb_b, m_w_in, m_b_gate, m_b_forget, m_q_norm_g, m_w_q_up, m_kv_norm_g,
              m_w_kv_up, m_w_branch_mla, m_w_branch_fox, m_w_out, m_ln_mix_g, m_ln_mix_b, m_w_ffn_up, m_conv_w, m_conv_b,
              m_w_ffn_down, m_ln_ffn_g, m_ln_ffn_b)
    v_args = (v_meta_tokens, v_ln_emb_g, v_ln_emb_b, v_w_in, v_b_gate, v_b_forget, v_q_norm_g, v_w_q_up, v_kv_norm_g,
              v_w_kv_up, v_w_branch_mla, v_w_branch_fox, v_w_out, v_ln_mix_g, v_ln_mix_b, v_w_ffn_up, v_conv_w, v_conv_b,
              v_w_ffn_down, v_ln_ffn_g, v_ln_ffn_b)
    as2d = lambda a: a.reshape((-1, a.shape[-1])) if a.ndim != 1 else a.reshape(1, -1)
    w = {n: as2d(a) for n, a in zip(names, w_args)}
    m = {n: as2d(a) for n, a in zip(names, m_args)}
    v = {n: as2d(a) for n, a in zip(names, v_args)}
    out_shape = {n: a.shape for n, a in zip(names, w_args)}

    axis_of = dict(BIG + SMALL_SHARDED)
    big = set(n for n, _ in BIG)
    wire = lambda n, a: a.astype(MXU_DTYPE) if n in big else a
    my_id = _my_id()

    early = _allgather("gather_early", [wire(n, w[n]) for n in EARLY])
    p = {n: _gathered_full(g3, axis_of[n]) for n, g3 in zip(EARLY, early) if n != "w_in"}
    p["w_in"] = _w_in_from_shards(early[EARLY.index("w_in")])
    for n in REPLICATED:
        p[n] = w[n].reshape(-1)
    late_src = [[wire(n, w[n]) for n in members] for members in LATE.values()]
    late_handles, late_token = _push_start("gather_late_start", late_src, False, after=early[0])
    late = {group: (members, src, handle)
            for (group, members), src, handle in zip(LATE.items(), late_src, late_handles)}
    sent = {}

    class Comm:
        first_token = (late_token,)

        def late_weights(self, group, after):
            members, src, handle = late[group]
            lands = _push_wait("gather_" + group + "_wait", handle, after)
            out = {}
            for n, own, land in zip(members, src, lands):
                if own.shape[0] % 16:
                    out[n] = _gathered_full(lax.dynamic_update_index_in_dim(land, own, my_id, 0), axis_of[n])
                elif axis_of[n] == 1:
                    out[n] = _cols_from_shards(n + "_repack", land, own)
                else:
                    out[n] = _rows_from_shards(n + "_repack", land, own)
            return out

        def send(self, name, grads):
            names_ = tuple(grads)
            parts = []
            for n in names_:
                if n == "w_in":
                    parts.append(_w_in_grad_to_shards(*grads[n], N_DEV, w[n].shape[1]))
                elif n == "w_ffn_up":
                    parts.append(_cols_to_shards(n + "_grad_unpack", grads[n], N_DEV))
                else:
                    parts.append(_shard_major(grads[n], axis_of[n]).astype(MXU_DTYPE))
            (handle,), token = _push_start("send_" + name + "_start", [parts], True)
            sent[name] = (names_, parts, handle)
            return (token,)

    loss_part, grad_x, d_meta, g = _local_step(x[0], loss_target[0], p, Comm())
    grad_x = grad_x[None]

    small = _pack([d_meta.reshape(-1)] + [g[n].reshape(-1) for n in REPLICATED] + [loss_part.reshape(-1)])
    (small_handle,), small_token = _push_start("send_small_start", [[small]], False)

    res = {}
    prev = small_token
    for name, (names_, parts, handle) in sent.items():
        lands = _push_wait("send_" + name + "_wait", handle, prev)
        for n, part, land in zip(names_, parts, lands):
            own = lax.dynamic_index_in_dim(part, my_id, axis=0, keepdims=False)
            res[n] = _adamw("adamw_" + n, land, w[n], m[n], v[n], own=own)
            prev = res[n][0]
    small_all = _push_wait("send_small_wait", small_handle, prev)[0]
    head = jnp.zeros((d_meta.size,), F32)
    rep_w = _pack([head] + [w[n].reshape(-1) for n in REPLICATED])
    rep_m = _pack([head] + [m[n].reshape(-1) for n in REPLICATED])
    rep_v = _pack([head] + [v[n].reshape(-1) for n in REPLICATED])
    rep_res = _adamw("adamw_replicated", small_all, rep_w, rep_m, rep_v, own=small)
    off = d_meta.size
    for n in REPLICATED:
        sz = w[n].size
        res[n] = tuple(a.reshape(-1)[off:off + sz] for a in rep_res)
        off += sz
    loss = rep_res[0].reshape(-1)[off]
    cols = w["meta_tokens"].shape[1]
    meta_rows = lambda a: a.reshape(a.shape[:-2] + (-1,))[..., :d_meta.size].reshape(a.shape[:-2] + d_meta.shape)
    my_cols = lambda a: lax.dynamic_slice_in_dim(a, my_id * cols, cols, axis=a.ndim - 1)
    res["meta_tokens"] = _adamw("adamw_meta_tokens", my_cols(meta_rows(small_all)), w["meta_tokens"],
                                m["meta_tokens"], v["meta_tokens"], own=my_cols(d_meta))

    outs = [loss, grad_x]
    for idx in range(4):
        outs += [res[n][idx].reshape(out_shape[n]) for n in names]
    return tuple(outs)
```

```python
import jax
import jax.numpy as jnp
from jax import lax
from jax.experimental import pallas as pl
from jax.experimental.pallas import tpu as pltpu

F32 = jnp.float32
BF16 = jnp.bfloat16
MXU_DTYPE = BF16

N_DEV = 8
N_META = 16
D_MODEL = 1024
HEADS = 8
Q_RANK = 384
KV_RANK = 128
NOPE = 64
ROPE = 32
HALF = ROPE // 2
QK_DIM = NOPE + ROPE
V_DIM = 64
FOX_DIM = 64
FOX_W = HEADS * FOX_DIM
D_FF = 2816
ROPE_THETA = 10000.0
LN_EPS = 1e-5
RMS_EPS = 1e-6
ALPHA = 2.0 ** 0.25
MLA_SCALE = QK_DIM ** -0.5
FOX_SCALE = FOX_DIM ** -0.5
NEG_INF = -1e30

HP = 128
HW = HEADS * HP
F_W = 3 * HW
R_GATE = 0
R_KVLAT = R_GATE + 2 * D_MODEL
R_LAST = R_KVLAT + KV_RANK
R_QLAT = R_LAST + HP
R_W = R_QLAT + Q_RANK
assert R_QLAT % Q_RANK == 0 and R_KVLAT % KV_RANK == 0 and R_GATE % D_MODEL == 0 and R_W % HP == 0
LANE_FL = 0
LANE_PE = NOPE

ADAM_LR = 0.001
ADAM_B1 = 0.9
ADAM_B2 = 0.999
ADAM_EPS = 1e-08
ADAM_WD = 0.01
ADAM_STEP = 10

ROW_BLOCK = 256
ATT_TQ = 768
ATT_TK = 768
ATT_HEADS = 2
ATT_HEADS_FWD = 4
ROW_ALIGN = 768
MM_BLOCK_CAP = 1408
VMEM_LIMIT = 56 * 1024 * 1024
HIGHEST = lax.Precision.HIGHEST
NT = (((1,), (1,)), ((), ()))
TN = (((0,), (0,)), ((), ()))


def _params(sem=None):
    return pltpu.CompilerParams(dimension_semantics=sem, vmem_limit_bytes=VMEM_LIMIT)


def _call(name, body, grid, ins, outs, scratch=(), sem=None, after=()):
    n_in = len(ins)
    n_tok = len(after)

    def run(*refs):
        body(*refs[:n_in], *refs[n_in + n_tok:])

    tok_spec = pl.BlockSpec((8, 128), lambda *_: (0, 0))
    return pl.pallas_call(
        run, name=name, grid=grid,
        in_specs=[s for _, s in ins] + [tok_spec] * n_tok,
        out_specs=[s for _, s in outs],
        out_shape=[o for o, _ in outs],
        scratch_shapes=list(scratch),
        compiler_params=_params(sem),
    )(*[a for a, _ in ins], *after)


def _sds(shape, dtype):
    return jax.ShapeDtypeStruct(shape, dtype)


def _rows(br, c, cb=0):
    return pl.BlockSpec((br, c), lambda i: (i, cb))


def _whole(shape):
    n = len(shape)
    return pl.BlockSpec(shape, lambda i: (0,) * n)


def _pick(dim, cap, mult):
    best = None
    d = mult
    while d <= min(dim, cap):
        if dim % d == 0:
            best = d
        d += mult
    return best if best is not None else dim


def _hs(h):
    return slice(h * HP, (h + 1) * HP)


def _matmul(name, a, b, *, ta=False, tb=False, out_dtype=F32, addend=None, alpha=1.0, after=()):
    if ta:
        k, m = a.shape
    else:
        m, k = a.shape
    if tb:
        n, k2 = b.shape
    else:
        k2, n = b.shape
    assert k == k2, (name, a.shape, b.shape)
    bm = _pick(m, MM_BLOCK_CAP, 128 if ta else 16)
    bn = _pick(n, MM_BLOCK_CAP, 128)
    bk = _pick(k, MM_BLOCK_CAP, 128 if (not ta or tb) else 16)
    nk = k // bk
    dims = (((0 if ta else 1,), (1 if tb else 0,)), ((), ()))
    has_add = addend is not None

    def body(*refs):
        a_ref, b_ref = refs[:2]
        add_ref = refs[2] if has_add else None
        o_ref = refs[3 if has_add else 2]

        def finish(r):
            if has_add:
                r = r + alpha * add_ref[...]
            o_ref[...] = r.astype(o_ref.dtype)

        part = lax.dot_general(a_ref[...], b_ref[...], dims, preferred_element_type=F32)
        if nk == 1:
            finish(part)
            return
        acc_ref = refs[-1]
        kk = pl.program_id(2)

        @pl.when(kk == 0)
        def _():
            acc_ref[...] = part

        @pl.when(kk > 0)
        def _():
            acc_ref[...] += part

        @pl.when(kk == nk - 1)
        def _():
            finish(acc_ref[...])

    a_spec = pl.BlockSpec((bk, bm), lambda i, j, l: (l, i)) if ta else pl.BlockSpec((bm, bk), lambda i, j, l: (i, l))
    b_spec = pl.BlockSpec((bn, bk), lambda i, j, l: (j, l)) if tb else pl.BlockSpec((bk, bn), lambda i, j, l: (l, j))
    o_spec = pl.BlockSpec((bm, bn), lambda i, j, l: (i, j))
    ins = [(a, a_spec), (b, b_spec)]
    if has_add:
        ins.append((addend, o_spec))
    return _call(name, body, (m // bm, n // bn, nk), ins, [(_sds((m, n), out_dtype), o_spec)],
                 scratch=[pltpu.VMEM((bm, bn), F32)] if nk > 1 else [],
                 sem=("parallel", "parallel", "arbitrary"), after=after)[0]


def _ln_stats(z):
    mu = jnp.mean(z, axis=-1, keepdims=True)
    zc = z - mu
    var = jnp.mean(zc * zc, axis=-1, keepdims=True)
    rstd = lax.rsqrt(var + LN_EPS)
    return zc * rstd, rstd


def _ln_fwd(name, a, res, g, b, after=()):
    r, d = a.shape
    br = ROW_BLOCK
    has_res = res is not None

    def body(*refs):
        if has_res:
            a_ref, r_ref, g_ref, b_ref, y_ref, yb_ref = refs
            z = ALPHA * a_ref[...] + r_ref[...]
        else:
            a_ref, g_ref, b_ref, y_ref, yb_ref = refs
            z = a_ref[...]
        xhat, _ = _ln_stats(z)
        y = xhat * g_ref[...] + b_ref[...]
        y_ref[...] = y
        yb_ref[...] = y.astype(yb_ref.dtype)

    ins = [(a, _rows(br, d))]
    if has_res:
        ins.append((res, _rows(br, d)))
    ins += [(g.reshape(1, d), _whole((1, d))), (b.reshape(1, d), _whole((1, d)))]
    outs = [(_sds((r, d), F32), _rows(br, d)), (_sds((r, d), MXU_DTYPE), _rows(br, d))]
    return _call(name, body, (r // br,), ins, outs, sem=("parallel",), after=after)


def _ln_bwd(name, a, res, dy, g, after=()):
    r, d = a.shape
    br = ROW_BLOCK
    has_res = res is not None

    def body(*refs):
        if has_res:
            a_ref, r_ref, dy_ref, g_ref, dz_ref, dzb_ref, dg_ref, db_ref = refs
            z = ALPHA * a_ref[...] + r_ref[...]
        else:
            a_ref, dy_ref, g_ref, dz_ref, dzb_ref, dg_ref, db_ref = refs
            z = a_ref[...]
        xhat, rstd = _ln_stats(z)
        dyv = dy_ref[...]
        dyg = dyv * g_ref[...]
        m1 = jnp.mean(dyg, axis=-1, keepdims=True)
        m2 = jnp.mean(dyg * xhat, axis=-1, keepdims=True)
        dz = rstd * (dyg - m1 - xhat * m2)
        dz_ref[...] = dz
        dzb_ref[...] = dz.astype(dzb_ref.dtype)

        @pl.when(pl.program_id(0) == 0)
        def _():
            dg_ref[...] = jnp.zeros_like(dg_ref)
            db_ref[...] = jnp.zeros_like(db_ref)

        dg_ref[...] += jnp.sum(dyv * xhat, axis=0, keepdims=True)
        db_ref[...] += jnp.sum(dyv, axis=0, keepdims=True)

    ins = [(a, _rows(br, d))]
    if has_res:
        ins.append((res, _rows(br, d)))
    ins += [(dy, _rows(br, d)), (g.reshape(1, d), _whole((1, d)))]
    outs = [(_sds((r, d), F32), _rows(br, d)), (_sds((r, d), MXU_DTYPE), _rows(br, d)),
            (_sds((1, d), F32), _whole((1, d))), (_sds((1, d), F32), _whole((1, d)))]
    return _call(name, body, (r // br,), ins, outs, sem=("arbitrary",), after=after)


LATENTS = ((R_QLAT // Q_RANK, Q_RANK), (R_KVLAT // KV_RANK, KV_RANK))


def _latent_norm_fwd(proj_r, gains):
    r = proj_r.shape[0]
    br = ROW_BLOCK

    def body(xq_ref, xk_ref, gq_ref, gk_ref, yq_ref, yk_ref):
        for x_ref, g_ref, y_ref in ((xq_ref, gq_ref, yq_ref), (xk_ref, gk_ref, yk_ref)):
            x = x_ref[...]
            rstd = lax.rsqrt(jnp.mean(x * x, axis=-1, keepdims=True) + RMS_EPS)
            y_ref[...] = (x * rstd * g_ref[...]).astype(y_ref.dtype)

    return _call("latent_norm_fwd", body, (r // br,),
                 [(proj_r, _rows(br, wd, cb)) for cb, wd in LATENTS]
                 + [(g.reshape(1, wd), _whole((1, wd))) for g, (_, wd) in zip(gains, LATENTS)],
                 [(_sds((r, wd), MXU_DTYPE), _rows(br, wd)) for _, wd in LATENTS], sem=("parallel",))


def _latent_norm_bwd(proj_r, dys, gains):
    r = proj_r.shape[0]
    br = ROW_BLOCK

    def body(xq_ref, xk_ref, dq_ref, dk_ref, gq_ref, gk_ref, oq_ref, ok_ref, dgq_ref, dgk_ref):
        @pl.when(pl.program_id(0) == 0)
        def _():
            dgq_ref[...] = jnp.zeros_like(dgq_ref)
            dgk_ref[...] = jnp.zeros_like(dgk_ref)

        for x_ref, dy_ref, g_ref, dx_ref, dg_ref in ((xq_ref, dq_ref, gq_ref, oq_ref, dgq_ref),
                                                     (xk_ref, dk_ref, gk_ref, ok_ref, dgk_ref)):
            x = x_ref[...]
            rstd = lax.rsqrt(jnp.mean(x * x, axis=-1, keepdims=True) + RMS_EPS)
            nrm = x * rstd
            dyv = dy_ref[...]
            dyg = dyv * g_ref[...]
            dx_ref[...] = (rstd * (dyg - nrm * jnp.mean(dyg * nrm, axis=-1, keepdims=True))).astype(dx_ref.dtype)
            dg_ref[...] += jnp.sum(dyv * nrm, axis=0, keepdims=True)

    return _call("latent_norm_bwd", body, (r // br,),
                 [(proj_r, _rows(br, wd, cb)) for cb, wd in LATENTS]
                 + [(dy, _rows(br, wd)) for dy, (_, wd) in zip(dys, LATENTS)]
                 + [(g.reshape(1, wd), _whole((1, wd))) for g, (_, wd) in zip(gains, LATENTS)],
                 [(_sds((r, wd), MXU_DTYPE), _rows(br, wd)) for _, wd in LATENTS]
                 + [(_sds((1, wd), F32), _whole((1, wd))) for _, wd in LATENTS], sem=("arbitrary",))


def _lane_iota(shape):
    return lax.broadcasted_iota(jnp.int32, shape, 1)


def _rotary(t, c, s, lane, sign):
    second = pltpu.roll(t, HP - HALF, axis=1)
    first = pltpu.roll(t, HALF, axis=1)
    lo = (lane >= LANE_PE) & (lane < LANE_PE + HALF)
    hi = (lane >= LANE_PE + HALF) & (lane < LANE_PE + ROPE)
    return jnp.where(lo, t * c - sign * second * s, jnp.where(hi, t * c + sign * first * s, t))


def _rope_fwd(q_raw, k_part, proj_r, cos_t, sin_t):
    r = q_raw.shape[0]
    br = ROW_BLOCK

    def body(q_ref, k_ref, t_ref, c_ref, s_ref, qo_ref, ko_ref):
        c = c_ref[...]
        s = s_ref[...]
        lane = _lane_iota((br, HP))
        pe = (lane >= LANE_PE) & (lane < LANE_PE + ROPE)
        kp = jnp.where(pe, _rotary(t_ref[...], c, s, lane, 1.0), 0.0)
        for h in range(HEADS):
            qo_ref[:, _hs(h)] = (_rotary(q_ref[:, _hs(h)], c, s, lane, 1.0) * MLA_SCALE).astype(qo_ref.dtype)
            ko_ref[:, _hs(h)] = (k_ref[:, _hs(h)] + kp).astype(ko_ref.dtype)

    blk = _rows(br, HP)
    wide = _rows(br, HW)
    return _call("rope_fwd", body, (r // br,),
                 [(q_raw, wide), (k_part, wide), (proj_r, _rows(br, HP, R_LAST // HP)), (cos_t, blk), (sin_t, blk)],
                 [(_sds((r, HW), MXU_DTYPE), wide)] * 2, sem=("parallel",))


def _rope_bwd(dq, dk, dv, dfl, cos_t, sin_t):
    r = dq.shape[0]
    br = ROW_BLOCK

    def body(dq_ref, dk_ref, dv_ref, fl_ref, c_ref, s_ref, dqo_ref, dkv_ref, dl_ref):
        c = c_ref[...]
        s = s_ref[...]
        lane = _lane_iota((br, HP))
        pe = (lane >= LANE_PE) & (lane < LANE_PE + ROPE)
        acc = jnp.zeros((br, HP), F32)
        for h in range(HEADS):
            dqo_ref[:, _hs(h)] = (_rotary(dq_ref[:, _hs(h)], c, s, lane, -1.0) * MLA_SCALE).astype(dqo_ref.dtype)
            dkh = dk_ref[:, _hs(h)]
            acc = acc + dkh
            dkv_ref[:, _hs(h)] = dkh.astype(dkv_ref.dtype)
            dkv_ref[:, _hs(HEADS + h)] = dv_ref[:, _hs(h)].astype(dkv_ref.dtype)
        dl_ref[...] = (jnp.where(pe, _rotary(acc, c, s, lane, -1.0), 0.0) + fl_ref[...]).astype(dl_ref.dtype)

    blk = _rows(br, HP)
    wide = _rows(br, HW)
    return _call("rope_bwd", body, (r // br,),
                 [(dq, wide), (dk, wide), (dv, wide), (dfl, blk), (cos_t, blk), (sin_t, blk)],
                 [(_sds((r, HW), MXU_DTYPE), wide), (_sds((r, 2 * HW), MXU_DTYPE), _rows(br, 2 * HW)),
                  (_sds((r, HP), MXU_DTYPE), blk)],
                 sem=("parallel",))


def _log_sigmoid(x):
    return jnp.minimum(x, 0.0) - jnp.log(1.0 + jnp.exp(-jnp.abs(x)))


def _head_lane(x, h, lane):
    return jnp.sum(jnp.where(lane == h, x, 0.0), axis=1, keepdims=True)


def _forget_fwd(proj_r, bf_row):
    r = proj_r.shape[0]
    br = ROW_BLOCK

    def body(t_ref, b_ref, ob_ref, ot_ref, carry_ref):
        @pl.when(pl.program_id(0) == 0)
        def _():
            carry_ref[...] = jnp.zeros_like(carry_ref)

        x = t_ref[...] + b_ref[...]
        lane = _lane_iota(x.shape)
        lf = jnp.where((lane >= LANE_FL) & (lane < LANE_FL + HEADS), _log_sigmoid(x), 0.0)
        tri = (lax.broadcasted_iota(jnp.int32, (br, br), 0) >= lax.broadcasted_iota(jnp.int32, (br, br), 1)).astype(F32)
        cum = jnp.dot(tri, lf, precision=HIGHEST, preferred_element_type=F32) + carry_ref[0:1, :]
        for h in range(HEADS):
            ob_ref[:, _hs(h)] = jnp.broadcast_to(_head_lane(cum, LANE_FL + h, lane), (br, HP))
        ot_ref[...] = cum.T[LANE_FL:LANE_FL + HEADS, :]
        carry_ref[...] = jnp.broadcast_to(cum[br - 1:br, :], carry_ref.shape)

    return _call("forget_fwd", body, (r // br,),
                 [(proj_r, _rows(br, HP, R_LAST // HP)), (bf_row, _whole((1, HP)))],
                 [(_sds((r, HW), F32), _rows(br, HW)), (_sds((HEADS, r), F32), pl.BlockSpec((HEADS, br), lambda i: (0, i)))],
                 scratch=[pltpu.VMEM((8, HP), F32)], sem=("arbitrary",))


def _forget_bwd(proj_r, bf_row, dcq_t, dck_b):
    r = proj_r.shape[0]
    br = ROW_BLOCK
    nb = r // br

    def body(t_ref, b_ref, dcq_ref, dck_ref, o_ref, db_ref, carry_ref):
        @pl.when(pl.program_id(0) == 0)
        def _():
            carry_ref[...] = jnp.zeros_like(carry_ref)
            db_ref[...] = jnp.zeros_like(db_ref)

        lane = _lane_iota((br, HP))
        dc = jnp.concatenate([dcq_ref[...], jnp.zeros((HP - HEADS, br), F32)], axis=0).T
        for h in range(HEADS):
            dc = dc + jnp.where(lane == LANE_FL + h, dck_ref[:, h * HP:h * HP + 1], 0.0)
        triu = (lax.broadcasted_iota(jnp.int32, (br, br), 0) <= lax.broadcasted_iota(jnp.int32, (br, br), 1)).astype(F32)
        dlf = jnp.dot(triu, dc, precision=HIGHEST, preferred_element_type=F32) + carry_ref[0:1, :]
        carry_ref[...] = jnp.broadcast_to(dlf[0:1, :], carry_ref.shape)
        x = t_ref[...] + b_ref[...]
        dfl = jnp.where((lane >= LANE_FL) & (lane < LANE_FL + HEADS), dlf * jax.nn.sigmoid(-x), 0.0)
        o_ref[...] = dfl
        db_ref[...] += jnp.sum(dfl, axis=0, keepdims=True)

    rev = pl.BlockSpec((br, HP), lambda i: (nb - 1 - i, 0))
    return _call("forget_bwd", body, (nb,),
                 [(proj_r, pl.BlockSpec((br, HP), lambda i: (nb - 1 - i, R_LAST // HP))), (bf_row, _whole((1, HP))),
                  (dcq_t, pl.BlockSpec((HEADS, br), lambda i: (0, nb - 1 - i))),
                  (dck_b, pl.BlockSpec((br, HW), lambda i: (nb - 1 - i, 0)))],
                 [(_sds((r, HP), F32), rev), (_sds((1, HP), F32), _whole((1, HP)))],
                 scratch=[pltpu.VMEM((8, HP), F32)], sem=("arbitrary",))


def _gate_fwd(proj_r, b_gate, bm, bfx):
    r, d = bm.shape
    br = ROW_BLOCK
    cb = R_GATE // d

    def body(gm_ref, gf_ref, b1_ref, b2_ref, bm_ref, bf_ref, o_ref):
        g1 = jax.nn.sigmoid(gm_ref[...] + b1_ref[...])
        g2 = jax.nn.sigmoid(gf_ref[...] + b2_ref[...])
        o_ref[...] = (g1 * bm_ref[...].astype(F32) + g2 * bf_ref[...].astype(F32)).astype(o_ref.dtype)

    b1 = b_gate[:d].reshape(1, d)
    b2 = b_gate[d:].reshape(1, d)
    return _call("gate_fwd", body, (r // br,),
                 [(proj_r, _rows(br, d, cb)), (proj_r, _rows(br, d, cb + 1)), (b1, _whole((1, d))), (b2, _whole((1, d))),
                  (bm, _rows(br, d)), (bfx, _rows(br, d))],
                 [(_sds((r, d), MXU_DTYPE), _rows(br, d))], sem=("parallel",))[0]


def _gate_bwd(proj_r, b_gate, bm, bfx, dmerged):
    r, d = bm.shape
    br = ROW_BLOCK
    cb = R_GATE // d

    def body(gm_ref, gf_ref, b1_ref, b2_ref, bm_ref, bf_ref, dm_ref, dbm_ref, dbf_ref, dgl_ref, dbg_ref):
        g1 = jax.nn.sigmoid(gm_ref[...] + b1_ref[...])
        g2 = jax.nn.sigmoid(gf_ref[...] + b2_ref[...])
        dm = dm_ref[...].astype(F32)
        dbm_ref[...] = (dm * g1).astype(dbm_ref.dtype)
        dbf_ref[...] = (dm * g2).astype(dbf_ref.dtype)
        dl1 = dm * bm_ref[...].astype(F32) * (g1 * (1.0 - g1))
        dl2 = dm * bf_ref[...].astype(F32) * (g2 * (1.0 - g2))
        dgl_ref[:, 0:d] = dl1.astype(dgl_ref.dtype)
        dgl_ref[:, d:2 * d] = dl2.astype(dgl_ref.dtype)

        @pl.when(pl.program_id(0) == 0)
        def _():
            dbg_ref[...] = jnp.zeros_like(dbg_ref)

        dbg_ref[:, 0:d] += jnp.sum(dl1, axis=0, keepdims=True)
        dbg_ref[:, d:2 * d] += jnp.sum(dl2, axis=0, keepdims=True)

    b1 = b_gate[:d].reshape(1, d)
    b2 = b_gate[d:].reshape(1, d)
    return _call("gate_bwd", body, (r // br,),
                 [(proj_r, _rows(br, d, cb)), (proj_r, _rows(br, d, cb + 1)), (b1, _whole((1, d))), (b2, _whole((1, d))),
                  (bm, _rows(br, d)), (bfx, _rows(br, d)), (dmerged, _rows(br, d))],
                 [(_sds((r, d), MXU_DTYPE), _rows(br, d)), (_sds((r, d), MXU_DTYPE), _rows(br, d)),
                  (_sds((r, 2 * d), MXU_DTYPE), _rows(br, 2 * d)), (_sds((1, 2 * d), F32), _whole((1, 2 * d)))],
                 sem=("arbitrary",))


HALO = 16
GLU_BWD_BLOCK = 128


def _conv_taps(gp, halo, first_block):
    halo = jnp.where(first_block, 0.0, halo.astype(F32))
    rid = lax.broadcasted_iota(jnp.int32, gp.shape, 0)
    last, prev = halo[HALO - 1:HALO, :], halo[HALO - 2:HALO - 1, :]
    g1 = jnp.where(rid == 0, last, pltpu.roll(gp, 1, axis=0))
    g2 = jnp.where(rid == 0, prev, jnp.where(rid == 1, last, pltpu.roll(gp, 2, axis=0)))
    return g1, g2


def _prev_halo(br, c):
    return pl.BlockSpec((HALO, c), lambda i: (jnp.maximum(i * (br // HALO) - 1, 0), 0))


def _glu_fwd(up, conv_w, conv_b):
    r = up.shape[0]
    c = D_FF
    br = ROW_BLOCK

    def body(gp_ref, halo_ref, val_ref, w_ref, b_ref, o_ref):
        gp = gp_ref[...].astype(F32)
        g1, g2 = _conv_taps(gp, halo_ref[...], pl.program_id(0) == 0)
        gate = w_ref[0:1, :] * g2 + w_ref[1:2, :] * g1 + w_ref[2:3, :] * gp + b_ref[...]
        o_ref[...] = (gate * jax.nn.sigmoid(gate) * val_ref[...].astype(F32)).astype(o_ref.dtype)

    return _call("glu_fwd", body, (r // br,),
                 [(up, _rows(br, c, 0)), (up, _prev_halo(br, c)), (up, _rows(br, c, 1)),
                  (conv_w, _whole((3, c))), (conv_b.reshape(1, c), _whole((1, c)))],
                 [(_sds((r, c), MXU_DTYPE), _rows(br, c))], sem=("parallel",))[0]


def _glu_bwd(up, conv_w, conv_b, d_act):
    r = up.shape[0]
    c = D_FF
    br = GLU_BWD_BLOCK
    nb = r // br

    def body(gp_ref, halo_ref, val_ref, da_ref, gpn_ref, valn_ref, dan_ref, w_ref, b_ref, o_ref, dw_ref, db_ref):
        i = pl.program_id(0)
        w0, w1, w2, bias = w_ref[0:1, :], w_ref[1:2, :], w_ref[2:3, :], b_ref[...]

        def d_gate(gp, g1, g2, val, da):
            gate = w0 * g2 + w1 * g1 + w2 * gp + bias
            sg = jax.nn.sigmoid(gate)
            return da * val * (sg * (1.0 + gate * (1.0 - sg))), da * (gate * sg)

        gp = gp_ref[...].astype(F32)
        g1, g2 = _conv_taps(gp, halo_ref[...], i == 0)
        dg, dv = d_gate(gp, g1, g2, val_ref[...].astype(F32), da_ref[...].astype(F32))
        gpn = gpn_ref[...].astype(F32)
        g1n, g2n = _conv_taps(gpn, gp[br - HALO:, :], False)
        dgn, _ = d_gate(gpn, g1n, g2n, valn_ref[...].astype(F32), dan_ref[...].astype(F32))
        dgn = jnp.where(i == nb - 1, 0.0, dgn)
        rid = lax.broadcasted_iota(jnp.int32, dg.shape, 0)
        u1 = jnp.where(rid == br - 1, dgn[0:1, :], pltpu.roll(dg, br - 1, axis=0))
        u2 = jnp.where(rid == br - 1, dgn[1:2, :], jnp.where(rid == br - 2, dgn[0:1, :], pltpu.roll(dg, br - 2, axis=0)))
        o_ref[:, 0:c] = (w2 * dg + w1 * u1 + w0 * u2).astype(o_ref.dtype)
        o_ref[:, c:2 * c] = dv.astype(o_ref.dtype)

        @pl.when(i == 0)
        def _():
            dw_ref[...] = jnp.zeros_like(dw_ref)
            db_ref[...] = jnp.zeros_like(db_ref)

        dw_ref[0:1, :] += jnp.sum(dg * g2, axis=0, keepdims=True)
        dw_ref[1:2, :] += jnp.sum(dg * g1, axis=0, keepdims=True)
        dw_ref[2:3, :] += jnp.sum(dg * gp, axis=0, keepdims=True)
        db_ref[...] += jnp.sum(dg, axis=0, keepdims=True)

    nxt = lambda cb: pl.BlockSpec((HALO, c), lambda i: (jnp.minimum((i + 1) * (br // HALO), r // HALO - 1), cb))
    return _call("glu_bwd", body, (nb,),
                 [(up, _rows(br, c, 0)), (up, _prev_halo(br, c)), (up, _rows(br, c, 1)), (d_act, _rows(br, c)),
                  (up, nxt(0)), (up, nxt(1)), (d_act, nxt(0)),
                  (conv_w, _whole((3, c))), (conv_b.reshape(1, c), _whole((1, c)))],
                 [(_sds((r, 2 * c), MXU_DTYPE), _rows(br, 2 * c)),
                  (_sds((8, c), F32), _whole((8, c))), (_sds((1, c), F32), _whole((1, c)))],
                 sem=("arbitrary",))


def _token_specs(seq, d):
    br = ROW_BLOCK
    nxb = seq // br
    main = pl.BlockSpec((br, d), lambda i: (jnp.minimum(i, nxb - 1), 0))
    tail = pl.BlockSpec((N_META, d), lambda i: (jnp.clip(i * (br // N_META) - 1, 0, seq // N_META - 1), 0))
    return main, tail


def _padded_block(main_ref, tail_ref, first, seq):
    br = ROW_BLOCK
    i = pl.program_id(0)
    nxb = seq // br
    main = jnp.where(i < nxb, main_ref[...], 0.0)
    head = jnp.where(i == 0, first, jnp.where(i <= nxb, tail_ref[...], 0.0))
    return jnp.concatenate([head, main[:br - N_META]], axis=0)


def _ln_emb_fwd(x, meta, g, b, rows, after=()):
    seq, d = x.shape
    br = ROW_BLOCK
    assert seq % br == 0 and br % N_META == 0 and rows % br == 0

    def body(x_ref, tail_ref, meta_ref, g_ref, b_ref, y_ref, yb_ref):
        z = _padded_block(x_ref, tail_ref, meta_ref[...], seq)
        xhat, _ = _ln_stats(z)
        y = xhat * g_ref[...] + b_ref[...]
        y_ref[...] = y
        yb_ref[...] = y.astype(yb_ref.dtype)

    main, tail = _token_specs(seq, d)
    return _call("ln_emb_fwd", body, (rows // br,),
                 [(x, main), (x, tail), (meta, _whole((N_META, d))), (g.reshape(1, d), _whole((1, d))),
                  (b.reshape(1, d), _whole((1, d)))],
                 [(_sds((rows, d), F32), _rows(br, d)), (_sds((rows, d), MXU_DTYPE), _rows(br, d))],
                 sem=("parallel",), after=after)


def _ln_emb_bwd(x, meta, dh0, g):
    seq, d = x.shape
    br = ROW_BLOCK
    step = br // N_META

    def ln_bwd(z, dy, gv):
        xhat, rstd = _ln_stats(z)
        dyg = dy * gv
        m1 = jnp.mean(dyg, axis=-1, keepdims=True)
        m2 = jnp.mean(dyg * xhat, axis=-1, keepdims=True)
        dz = rstd * (dyg - m1 - xhat * m2)
        return dz, jnp.sum(dy * xhat, axis=0, keepdims=True), jnp.sum(dy, axis=0, keepdims=True)

    def body(x_ref, dh_ref, nxt_ref, meta_ref, top_ref, g_ref, dx_ref, dm_ref, dg_ref, db_ref):
        gv = g_ref[...]
        dy = jnp.concatenate([dh_ref[N_META:, :], nxt_ref[...]], axis=0)
        dz, dg, db = ln_bwd(x_ref[...], dy, gv)
        dx_ref[...] = dz

        @pl.when(pl.program_id(0) == 0)
        def _():
            dzm, dgm, dbm = ln_bwd(meta_ref[...], top_ref[...], gv)
            dm_ref[...] = dzm
            dg_ref[...] = dgm
            db_ref[...] = dbm

        dg_ref[...] += dg
        db_ref[...] += db

    small = _whole((N_META, d))
    return _call("ln_emb_bwd", body, (seq // br,),
                 [(x, _rows(br, d)), (dh0, _rows(br, d)), (dh0, pl.BlockSpec((N_META, d), lambda i: ((i + 1) * step, 0))),
                  (meta, small), (dh0, small), (g.reshape(1, d), _whole((1, d)))],
                 [(_sds((seq, d), F32), _rows(br, d)), (_sds((N_META, d), F32), small),
                  (_sds((1, d), F32), _whole((1, d))), (_sds((1, d), F32), _whole((1, d)))], sem=("arbitrary",))


def _ln_ffn_loss(h1, f, tgt, g, b):
    r, d = h1.shape
    seq = tgt.shape[0]
    br = ROW_BLOCK

    def body(a_ref, r_ref, t_ref, tail_ref, g_ref, b_ref, l_ref):
        err = _loss_err(a_ref, r_ref, t_ref, tail_ref, g_ref, b_ref, seq)[0]

        @pl.when(pl.program_id(0) == 0)
        def _():
            l_ref[...] = jnp.zeros_like(l_ref)

        l_ref[...] += jnp.sum(jnp.sum(err * err, axis=1, keepdims=True), axis=0, keepdims=True) * (0.5 / d)

    main, tail = _token_specs(seq, d)
    return _call("ln_ffn_loss", body, (r // br,),
                 [(h1, _rows(br, d)), (f, _rows(br, d)), (tgt, main), (tgt, tail),
                  (g.reshape(1, d), _whole((1, d))), (b.reshape(1, d), _whole((1, d)))],
                 [(_sds((1, 1), F32), _whole((1, 1)))], sem=("arbitrary",))[0]


def _loss_err(a_ref, r_ref, t_ref, tail_ref, g_ref, b_ref, seq):
    br, d = a_ref.shape
    xhat, rstd = _ln_stats(ALPHA * a_ref[...] + r_ref[...])
    y = xhat * g_ref[...] + b_ref[...]
    t = _padded_block(t_ref, tail_ref, jnp.zeros((N_META, d), F32), seq)
    rid = lax.broadcasted_iota(jnp.int32, (br, d), 0) + pl.program_id(0) * br
    valid = (rid >= N_META) & (rid < N_META + seq)
    return jnp.where(valid, y - t, 0.0), xhat, rstd


def _ln_ffn_bwd(h1, f, tgt, g, b):
    r, d = h1.shape
    seq = tgt.shape[0]
    br = ROW_BLOCK

    def body(a_ref, r_ref, t_ref, tail_ref, g_ref, b_ref, dz_ref, dzb_ref, dg_ref, db_ref):
        err, xhat, rstd = _loss_err(a_ref, r_ref, t_ref, tail_ref, g_ref, b_ref, seq)
        dyv = err * (1.0 / d)
        dyg = dyv * g_ref[...]
        m1 = jnp.mean(dyg, axis=-1, keepdims=True)
        m2 = jnp.mean(dyg * xhat, axis=-1, keepdims=True)
        dz = rstd * (dyg - m1 - xhat * m2)
        dz_ref[...] = dz
        dzb_ref[...] = dz.astype(dzb_ref.dtype)

        @pl.when(pl.program_id(0) == 0)
        def _():
            dg_ref[...] = jnp.zeros_like(dg_ref)
            db_ref[...] = jnp.zeros_like(db_ref)

        dg_ref[...] += jnp.sum(dyv * xhat, axis=0, keepdims=True)
        db_ref[...] += jnp.sum(dyv, axis=0, keepdims=True)

    main, tail = _token_specs(seq, d)
    return _call("ln_ffn_bwd", body, (r // br,),
                 [(h1, _rows(br, d)), (f, _rows(br, d)), (tgt, main), (tgt, tail),
                  (g.reshape(1, d), _whole((1, d))), (b.reshape(1, d), _whole((1, d)))],
                 [(_sds((r, d), F32), _rows(br, d)), (_sds((r, d), MXU_DTYPE), _rows(br, d)),
                  (_sds((1, d), F32), _whole((1, d))), (_sds((1, d), F32), _whole((1, d)))], sem=("arbitrary",))


def _attn_fwd(name, q, k, v, cum_b=None, cum_t=None):
    (qa, qg), (ka, kg), (va, vg) = q, k, v
    r = qa.shape[0]
    tq, tk = ATT_TQ, ATT_TK
    nq, nk = r // tq, r // tk
    bias = cum_b is not None

    def body(*refs):
        if bias:
            q_ref, k_ref, vt_ref, cb_ref, ct_ref, o_ref, ob_ref, lse_ref = refs
        else:
            q_ref, k_ref, vt_ref, o_ref, ob_ref, lse_ref = refs
        i = pl.program_id(1)
        qs = [q_ref[:, _hs(hh)] for hh in range(hg)]
        cqs = [ct_ref[hh] for hh in range(hg)] if bias else None
        diff = lax.broadcasted_iota(jnp.int32, (tk, tq), 0) - lax.broadcasted_iota(jnp.int32, (tk, tq), 1)

        def step(j, carry, masked):
            keys = pl.ds(pl.multiple_of(j * tk, tk), tk)
            out = []
            for hh in range(hg):
                m, l, acc = carry[hh]
                kt = k_ref[keys, _hs(hh)]
                s = lax.dot_general(kt, qs[hh], NT, preferred_element_type=F32)
                if bias:
                    s = s + (cqs[hh] - cb_ref[keys, hh * HP:hh * HP + 1])
                if masked:
                    s = jnp.where(diff <= i * tq - j * tk, s, NEG_INF)
                m_new = jnp.maximum(m, jnp.max(s, axis=0, keepdims=True))
                p = jnp.exp(s - m_new)
                a = jnp.exp(m - m_new)
                l = a * l + jnp.sum(p, axis=0, keepdims=True)
                acc = a * acc + jnp.dot(vt_ref[j, _hs(hh), :], p.astype(kt.dtype), preferred_element_type=F32)
                out.append((m_new, l, acc))
            return tuple(out)

        n_clear = (i * tq + 1) // tk
        n_all = ((i + 1) * tq - 1) // tk + 1
        carry = tuple((jnp.full((1, tq), NEG_INF, F32), jnp.zeros((1, tq), F32), jnp.zeros((HP, tq), F32))
                      for _ in range(hg))
        carry = lax.fori_loop(0, n_clear, lambda j, c: step(j, c, False), carry)
        carry = lax.fori_loop(n_clear, n_all, lambda j, c: step(j, c, True), carry)
        for hh in range(hg):
            m, l, acc = carry[hh]
            o = (acc / l).T
            o_ref[:, _hs(hh)] = o
            ob_ref[:, _hs(hh)] = o.astype(ob_ref.dtype)
            lse_ref[hh] = m + jnp.log(l)

    hg = ATT_HEADS_FWD
    w = hg * HP
    gpw = HW // w
    tile = lambda g: pl.BlockSpec((tq, w), lambda h, i: (i, g * gpw + h))
    res = lambda g: pl.BlockSpec((r, w), lambda h, i: (0, g * gpw + h))
    v_t = _key_tiles_transposed(name + "_vt", va, vg)
    ins = [(qa, tile(qg)), (ka, res(kg)), (v_t, pl.BlockSpec((nk, w, tk), lambda h, i: (0, h, 0)))]
    if bias:
        ins += [(cum_b, res(0)),
                (cum_t.reshape(HEADS, nq, 1, tq), pl.BlockSpec((hg, None, 1, tq), lambda h, i: (h, i, 0, 0)))]
    outs = [(_sds((r, HW), F32), tile(0)), (_sds((r, HW), MXU_DTYPE), tile(0)),
            (_sds((HEADS, nq, 1, tq), F32), pl.BlockSpec((hg, None, 1, tq), lambda h, i: (h, i, 0, 0)))]
    o, ob, lse = _call(name, body, (gpw, nq), ins, outs, sem=("parallel", "parallel"))
    return o, ob, lse.reshape(HEADS, r)


def _key_tiles_transposed(name, a, group):
    r = a.shape[0]
    tk = ATT_TK

    def body(x_ref, o_ref):
        for h in range(HEADS):
            o_ref[_hs(h), :] = x_ref[:, _hs(h)].astype(F32).T.astype(o_ref.dtype)

    return _call(name, body, (r // tk,),
                 [(a, pl.BlockSpec((tk, HW), lambda j: (j, group)))],
                 [(_sds((r // tk, HW, tk), a.dtype), pl.BlockSpec((None, HW, tk), lambda j: (j, 0, 0)))],
                 sem=("parallel",))[0]


def _attn_bwd(name, q, k, v, do_b, o, lse_t, cum_b=None, cum_t=None, out_dtype=F32, after=()):
    (qa, qg), (ka, kg), (va, vg) = q, k, v
    r = qa.shape[0]
    tq, tk = ATT_TQ, ATT_TK
    nq, nk = r // tq, r // tk
    bias = cum_b is not None

    def body(*refs):
        if bias:
            (q_ref, k_ref, v_ref, do_ref, o_ref, lse_ref, cb_ref, ct_ref,
             dq_ref, dk_ref, dv_ref, dcq_ref, dck_ref, dqt_ref, dl_ref) = refs
        else:
            q_ref, k_ref, v_ref, do_ref, o_ref, lse_ref, dq_ref, dk_ref, dv_ref, dqt_ref, dl_ref = refs
        j = pl.program_id(1)

        @pl.when(j == 0)
        def _():
            dqt_ref[...] = jnp.zeros_like(dqt_ref)
            if bias:
                dcq_ref[...] = jnp.zeros_like(dcq_ref)
            for hh in range(hg):
                for i in range(nq):
                    rows = slice(i * tq, (i + 1) * tq)
                    prod = do_ref[rows, _hs(hh)].astype(F32) * o_ref[rows, _hs(hh)]
                    dl_ref[hh, i] = jnp.sum(prod.T, axis=0, keepdims=True)

        kts = [k_ref[:, _hs(hh)] for hh in range(hg)]
        vts = [v_ref[:, _hs(hh)] for hh in range(hg)]
        k_trs = [kt.astype(F32).T.astype(kt.dtype) for kt in kts]
        cks = [cb_ref[:, hh * HP:hh * HP + 1] for hh in range(hg)] if bias else None
        diff = lax.broadcasted_iota(jnp.int32, (tk, tq), 0) - lax.broadcasted_iota(jnp.int32, (tk, tq), 1)

        def step(i, carry, masked):
            rows = pl.ds(pl.multiple_of(i * tq, tq), tq)
            out = []
            for hh in range(hg):
                dk_acc, dv_acc, dck_acc = carry[hh]
                qt = q_ref[rows, _hs(hh)]
                dot = do_ref[rows, _hs(hh)]
                s = lax.dot_general(kts[hh], qt, NT, preferred_element_type=F32)
                if bias:
                    s = s + (ct_ref[hh, i] - cks[hh])
                if masked:
                    s = jnp.where(diff <= i * tq - j * tk, s, NEG_INF)
                p = jnp.exp(s - lse_ref[hh, i])
                dp = lax.dot_general(vts[hh], dot, NT, preferred_element_type=F32)
                ds = p * (dp - dl_ref[hh, i])
                pb = p.astype(dot.dtype)
                dsb = ds.astype(qt.dtype)
                dv_acc = dv_acc + jnp.dot(pb, dot, preferred_element_type=F32)
                dk_acc = dk_acc + jnp.dot(dsb, qt, preferred_element_type=F32)
                dqt_ref[hh, i] += jnp.dot(k_trs[hh], dsb, preferred_element_type=F32)
                if bias:
                    dcq_ref[hh, i] += jnp.sum(ds, axis=0, keepdims=True)
                    dck_acc = dck_acc - jnp.sum(ds, axis=1, keepdims=True)
                out.append((dk_acc, dv_acc, dck_acc))
            return tuple(out)

        i_first = (j * tk) // tq
        i_clear = jnp.minimum(((j + 1) * tk + tq - 2) // tq, nq)
        carry = tuple((jnp.zeros((tk, HP), F32), jnp.zeros((tk, HP), F32), jnp.zeros((tk, 1), F32)) for _ in range(hg))
        carry = lax.fori_loop(i_first, i_clear, lambda i, c: step(i, c, True), carry)
        carry = lax.fori_loop(i_clear, nq, lambda i, c: step(i, c, False), carry)
        for hh in range(hg):
            dk_acc, dv_acc, dck_acc = carry[hh]
            dk_ref[:, _hs(hh)] = dk_acc.astype(dk_ref.dtype)
            dv_ref[:, _hs(hh)] = dv_acc.astype(dv_ref.dtype)
            if bias:
                dck_ref[:, _hs(hh)] = jnp.broadcast_to(dck_acc, (tk, HP))

        @pl.when(j == nk - 1)
        def _():
            for hh in range(hg):
                for i in range(nq):
                    dq_ref[i * tq:(i + 1) * tq, _hs(hh)] = dqt_ref[hh, i].T.astype(dq_ref.dtype)

    hg = ATT_HEADS
    w = hg * HP
    gpw = HW // w
    res = lambda g: pl.BlockSpec((r, w), lambda h, j: (0, g * gpw + h))
    tile = lambda g: pl.BlockSpec((tk, w), lambda h, j: (j, g * gpw + h))
    rowv = pl.BlockSpec((hg, nq, 1, tq), lambda h, j: (h, 0, 0, 0))
    as_rows = lambda a: a.reshape(HEADS, nq, 1, tq)
    ins = [(qa, res(qg)), (ka, tile(kg)), (va, tile(vg)), (do_b, res(0)), (o, res(0)), (as_rows(lse_t), rowv)]
    outs = [(_sds((r, HW), out_dtype), res(0)), (_sds((r, HW), out_dtype), tile(0)), (_sds((r, HW), out_dtype), tile(0))]
    if bias:
        ins += [(cum_b, tile(0)), (as_rows(cum_t), rowv)]
        outs += [(_sds((HEADS, nq, 1, tq), F32), rowv), (_sds((r, HW), F32), tile(0))]
    res_out = _call(name, body, (gpw, nk), ins, outs,
                    scratch=[pltpu.VMEM((hg, nq, HP, tq), F32), pltpu.VMEM((hg, nq, 1, tq), F32)],
                    sem=("parallel", "arbitrary"), after=after)
    if bias:
        dq, dk, dv, dcq, dck = res_out
        return dq, dk, dv, dcq.reshape(HEADS, r), dck
    return res_out


MESH_ID = pl.DeviceIdType.MESH
ANY = pl.BlockSpec(memory_space=pl.ANY)


N_GATHER_COPIES = 8


def _allgather(name, shards):
    n = len(shards)

    def body(*refs):
        x_refs, out_refs = refs[:n], refs[n:2 * n]
        send_sems, recv_sems, local_sems = refs[2 * n:]
        x, y, c = lax.axis_index("x"), lax.axis_index("y"), lax.axis_index("c")
        me, sibling = (x, y, c), (x, y, 1 - c)
        xn, yn, dg = (1 - x, y, c), (x, 1 - y, c), (1 - x, 1 - y, c)
        other = lambda dev: (dev[0], dev[1], 1 - c)

        def slot(ti, dev, half=None):
            ref = out_refs[ti].at[4 * dev[0] + 2 * dev[1] + dev[2]]
            if half is None:
                return ref
            rows = shards[ti].shape[0] // 2
            return ref.at[pl.ds(half * rows, rows)]

        def copy(ti, k, block, to, half=None, src=None):
            return pltpu.make_async_remote_copy(
                src_ref=slot(ti, block, half) if src is None else src, dst_ref=slot(ti, block, half),
                send_sem=send_sems.at[ti, k], recv_sem=recv_sems.at[ti, k], device_id=to, device_id_type=MESH_ID)

        mine = [pltpu.make_async_copy(x_refs[ti], slot(ti, me), local_sems.at[ti]) for ti in range(n)]
        for cp in mine:
            cp.start()
        started = []

        def go(cp):
            cp.start()
            started.append(cp)

        for ti in range(n):
            go(copy(ti, 0, me, sibling, src=x_refs[ti]))
            go(copy(ti, 1, me, xn, src=x_refs[ti]))
            go(copy(ti, 2, me, yn, src=x_refs[ti]))
        for ti in range(n):
            copy(ti, 1, xn, me).wait_recv()
            go(copy(ti, 3, xn, yn, half=0))
            go(copy(ti, 5, xn, sibling))
            copy(ti, 2, yn, me).wait_recv()
            go(copy(ti, 4, yn, xn, half=1))
            go(copy(ti, 6, yn, sibling))
        for ti in range(n):
            copy(ti, 3, dg, me, half=0).wait_recv()
            copy(ti, 4, dg, me, half=1).wait_recv()
            go(copy(ti, 7, dg, sibling))
        for ti in range(n):
            copy(ti, 0, sibling, me).wait_recv()
            for k, dev in ((5, xn), (6, yn), (7, dg)):
                copy(ti, k, other(dev), me).wait_recv()
        for cp in started:
            cp.wait_send()
        for cp in mine:
            cp.wait()

    sems = pltpu.SemaphoreType.DMA((n, N_GATHER_COPIES))
    return pl.pallas_call(
        body, name=name, out_shape=[_sds((N_DEV,) + s.shape, s.dtype) for s in shards],
        in_specs=[ANY] * n, out_specs=[ANY] * n,
        scratch_shapes=[sems, sems, pltpu.SemaphoreType.DMA((n,))],
    )(*shards)


HBM = pl.BlockSpec(memory_space=pltpu.HBM)
SEM = pl.BlockSpec(memory_space=pltpu.SEMAPHORE)
EFFECT = pltpu.SideEffectType.DATAFLOW_SIDE_EFFECTING
N_PEER = N_DEV - 1


def _my_id():
    return 4 * lax.axis_index("x") + 2 * lax.axis_index("y") + lax.axis_index("c")


def _peers():
    x, y, c = lax.axis_index("x"), lax.axis_index("y"), lax.axis_index("c")
    out = []
    for k in range(1, N_DEV):
        px, py, pc = (1 - x if k & 4 else x, 1 - y if k & 2 else y, 1 - c if k & 1 else c)
        out.append(((px, py, pc), 4 * px + 2 * py + pc))
    return out


def _push_copies(src_refs, land_refs, send_sems, recv_sems, scatter, landing):
    me = _my_id()
    out = []
    for ti, (src, land) in enumerate(zip(src_refs, land_refs)):
        for k, (dev, pid) in enumerate(_peers()):
            out.append(pltpu.make_async_remote_copy(
                src_ref=src.at[pid] if scatter else src, dst_ref=land.at[pid if landing else me],
                send_sem=send_sems.at[ti * N_PEER + k], recv_sem=recv_sems.at[ti * N_PEER + k],
                device_id=dev, device_id_type=MESH_ID))
    return out


def _push_start(name, groups, scatter, after=None):
    sizes = [len(g) for g in groups]
    srcs = [a for g in groups for a in g]
    n = len(srcs)
    slot = lambda s: s.shape[1:] if scatter else s.shape
    lands = [lax.empty((N_DEV,) + slot(s), s.dtype) for s in srcs]
    n_after = 0 if after is None else 1
    n_grp = len(groups)

    def body(*refs):
        src_refs, land_refs = refs[:n], refs[n:2 * n]
        sems = refs[2 * n + n_after:2 * n + n_after + 2 * n_grp]
        token = refs[-1]
        lo = 0
        for gi, sz in enumerate(sizes):
            for cp in _push_copies(src_refs[lo:lo + sz], land_refs[lo:lo + sz], sems[2 * gi], sems[2 * gi + 1], scatter, False):
                cp.start()
            lo += sz
        token[...] = jnp.zeros_like(token)

    hbm = lambda a: pltpu.with_memory_space_constraint(a, pltpu.HBM)
    operands = [hbm(a) for a in srcs + lands] + ([after] if n_after else [])
    sem_shapes = [pltpu.SemaphoreType.DMA((sz * N_PEER,)) for sz in sizes for _ in range(2)]
    res = pl.pallas_call(
        body, name=name,
        out_shape=sem_shapes + [pltpu.HBM(a.shape, a.dtype) for a in srcs + lands] + [_sds((8, 128), F32)],
        in_specs=[HBM] * (2 * n) + [ANY] * n_after,
        out_specs=[SEM] * (2 * n_grp) + [HBM] * (2 * n) + [pl.BlockSpec(memory_space=pltpu.VMEM)],
        input_output_aliases={i: 2 * n_grp + i for i in range(2 * n)},
        compiler_params=pltpu.CompilerParams(has_side_effects=EFFECT),
    )(*operands)
    thru = res[2 * n_grp:2 * n_grp + 2 * n]
    handles, lo = [], 0
    for gi, sz in enumerate(sizes):
        handles.append((res[2 * gi], res[2 * gi + 1], list(thru[lo:lo + sz]), list(thru[n + lo:n + lo + sz]), scatter))
        lo += sz
    return handles, res[-1]


def _push_wait(name, handle, after):
    send_sems, recv_sems, srcs, lands, scatter = handle
    n = len(srcs)

    def body(*refs):
        src_refs, land_refs = refs[:n], refs[n:2 * n]
        s_sems, r_sems = refs[2 * n], refs[2 * n + 1]
        for cp in _push_copies(src_refs, land_refs, s_sems, r_sems, scatter, True):
            cp.wait_send()
            cp.wait_recv()

    res = pl.pallas_call(
        body, name=name,
        out_shape=[pltpu.HBM(a.shape, a.dtype) for a in srcs + lands],
        in_specs=[HBM] * (2 * n) + [SEM, SEM, ANY], out_specs=[HBM] * (2 * n),
        input_output_aliases={i: i for i in range(2 * n)},
        compiler_params=pltpu.CompilerParams(has_side_effects=EFFECT),
    )(*srcs, *lands, send_sems, recv_sems, after)
    return list(res[n:])


def _adamw(name, parts, w, m, v, own=None):
    r, c = w.shape
    br = _pick(r, 256, 16)
    has_own = own is not None

    def body(*refs):
        if has_own:
            p_ref, own_ref, w_ref, m_ref, v_ref, g_ref, d_ref, nm_ref, nv_ref = refs
            me = _my_id()
            mine = own_ref[...].astype(F32)
        else:
            p_ref, w_ref, m_ref, v_ref, g_ref, d_ref, nm_ref, nv_ref = refs
        g = None
        for k in range(N_DEV):
            t = p_ref[k].astype(F32)
            if has_own:
                t = jnp.where(me == k, mine, t)
            g = t if g is None else g + t
        mm = ADAM_B1 * m_ref[...] + (1.0 - ADAM_B1) * g
        vv = ADAM_B2 * v_ref[...] + (1.0 - ADAM_B2) * (g * g)
        m_hat = mm / (1.0 - ADAM_B1 ** ADAM_STEP)
        v_hat = vv / (1.0 - ADAM_B2 ** ADAM_STEP)
        g_ref[...] = g
        d_ref[...] = -ADAM_LR * (m_hat / (jnp.sqrt(v_hat) + ADAM_EPS) + ADAM_WD * w_ref[...])
        nm_ref[...] = mm
        nv_ref[...] = vv

    spec = _rows(br, c)
    out = (_sds((r, c), F32), spec)
    ins = [(parts, pl.BlockSpec((N_DEV, br, c), lambda i: (0, i, 0)))] + ([(own, spec)] if has_own else [])
    return _call(name, body, (r // br,), ins + [(w, spec), (m, spec), (v, spec)], [out] * 4, sem=("parallel",))


def _pad_head_cols(w, d):
    k = w.shape[0]
    return jnp.pad(w.reshape(k, HEADS, d), ((0, 0), (0, 0), (0, HP - d))).reshape(k, HW)


def _unpad_head_cols(wp, d):
    k = wp.shape[0]
    return wp.reshape(k, HEADS, HP)[:, :, :d].reshape(k, HEADS * d)


def _pad_head_rows(w, d):
    n = w.shape[1]
    return jnp.pad(w.reshape(HEADS, d, n), ((0, 0), (0, HP - d), (0, 0))).reshape(HW, n)


def _unpad_head_rows(wp, d):
    n = wp.shape[1]
    return wp.reshape(HEADS, HP, n)[:, :d, :].reshape(HEADS * d, n)


def _w_in_runs():
    nat = {}
    o = 0
    for nm, wd in (("q", Q_RANK), ("kv", KV_RANK), ("kr", ROPE), ("fq", FOX_W), ("fk", FOX_W), ("fv", FOX_W),
                   ("fl", HEADS), ("gate", 2 * D_MODEL)):
        nat[nm] = o
        o += wd
    runs = [(1, R_QLAT, nat["q"], Q_RANK, 1.0), (1, R_KVLAT, nat["kv"], KV_RANK, 1.0),
            (1, R_LAST + LANE_FL, nat["fl"], HEADS, 1.0), (1, R_LAST + LANE_PE, nat["kr"], ROPE, 1.0),
            (1, R_GATE, nat["gate"], 2 * D_MODEL, 1.0)]
    for grp, (nm, sc) in enumerate((("fq", FOX_SCALE), ("fk", 1.0), ("fv", 1.0))):
        runs += [(0, grp * HW + h * HP, nat[nm] + h * FOX_DIM, FOX_DIM, sc) for h in range(HEADS)]
    return runs


def _sharded_runs(runs, shard_cols):
    out = []
    for half, col, ncol, width, sc in runs:
        while width > 0:
            d, local = divmod(ncol, shard_cols)
            wd = min(width, shard_cols - local)
            out.append((half, col, d, local, wd, sc))
            col, ncol, width = col + wd, ncol + wd, width - wd
    return out


def _remap(name, srcs, out_shapes, moves):
    rows = srcs[0].shape[-2]
    br = _pick(rows, 256, 16)
    ns = len(srcs)

    def spec(shape):
        if len(shape) == 2:
            return pl.BlockSpec((br, shape[1]), lambda i: (i, 0))
        return pl.BlockSpec((shape[0], br, shape[2]), lambda i: (0, i, 0))

    covered = [sum(m[6] for m in moves if m[0] == di) for di in range(len(out_shapes))]
    has_gaps = [cov < (shape[1] if len(shape) == 2 else shape[0] * shape[2])
                for cov, (shape, _) in zip(covered, out_shapes)]

    def body(*refs):
        s_refs, o_refs = refs[:ns], refs[ns:]
        for o, gaps in zip(o_refs, has_gaps):
            if gaps:
                o[...] = jnp.zeros_like(o)
        for di, dl, dc, si, sl, sc0, wd, scale in moves:
            v = s_refs[si][:, sc0:sc0 + wd] if sl is None else s_refs[si][sl, :, sc0:sc0 + wd]
            if scale != 1.0:
                v = v * jnp.asarray(scale, v.dtype)
            v = v.astype(o_refs[di].dtype)
            if dl is None:
                o_refs[di][:, dc:dc + wd] = v
            else:
                o_refs[di][dl, :, dc:dc + wd] = v

    return _call(name, body, (rows // br,), [(a, spec(a.shape)) for a in srcs],
                 [(_sds(shape, dt), spec(shape)) for shape, dt in out_shapes], sem=("parallel",))


def _w_in_from_shards(g3):
    n, rows, c = g3.shape
    moves = [(half, None, col, 0, d, local, wd, sc) for half, col, d, local, wd, sc in _sharded_runs(_w_in_runs(), c)]
    return _remap("w_in_repack", [g3], [((rows, F_W), g3.dtype), ((rows, R_W), g3.dtype)], moves)


def _w_in_grad_to_shards(d_fused, d_rest, n, c):
    rows = d_fused.shape[0]
    moves = [(0, d, local, half, None, col, wd, sc) for half, col, d, local, wd, sc in _sharded_runs(_w_in_runs(), c)]
    return _remap("w_in_grad_unpack", [d_fused, d_rest], [((n, rows, c), d_fused.dtype)], moves)[0]


def _rows_from_shards(name, land, own):
    n, rows, c = land.shape

    def body(land_ref, own_ref, o_ref):
        o_ref[...] = jnp.where(_my_id() == pl.program_id(0), own_ref[...], land_ref[...])

    return _call(name, body, (n,),
                 [(land, pl.BlockSpec((None, rows, c), lambda d: (d, 0, 0))), (own, _whole((rows, c)))],
                 [(_sds((n * rows, c), land.dtype), pl.BlockSpec((rows, c), lambda d: (d, 0)))], sem=("parallel",))[0]


def _cols_from_shards(name, land, own):
    n, rows, c = land.shape
    br = _pick(rows, 256, 16)

    def body(land_ref, own_ref, o_ref):
        me = _my_id()
        for d in range(n):
            o_ref[:, c * d:c * (d + 1)] = jnp.where(me == d, own_ref[...], land_ref[d])

    return _call(name, body, (rows // br,),
                 [(land, pl.BlockSpec((n, br, c), lambda i: (0, i, 0))), (own, _rows(br, c))],
                 [(_sds((rows, n * c), land.dtype), _rows(br, n * c))], sem=("parallel",))[0]


def _cols_to_shards(name, full, n):
    rows, nc = full.shape
    c = nc // n
    return _remap(name, [full], [((n, rows, c), full.dtype)], [(0, d, 0, 0, None, c * d, c, 1.0) for d in range(n)])[0]


def _split_w_kv(w):
    k = w.shape[0]
    w3 = w.reshape(k, HEADS, NOPE + V_DIM)
    padl = lambda a: jnp.pad(a, ((0, 0), (0, 0), (0, HP - a.shape[-1]))).reshape(k, HW)
    return padl(w3[..., :NOPE]), padl(w3[..., NOPE:])


def _merge_w_kv(wk, wv):
    k = wk.shape[0]
    return jnp.concatenate([wk.reshape(k, HEADS, HP)[..., :NOPE], wv.reshape(k, HEADS, HP)[..., :V_DIM]],
                           axis=-1).reshape(k, HEADS * (NOPE + V_DIM))


class _NoComm:
    first_token = ()

    def late_weights(self, group, after):
        return {}

    def send(self, name, grads):
        return ()


def _local_step(x, tgt, p, comm=_NoComm()):
    seq = x.shape[0]
    r = -(-(N_META + seq) // ROW_ALIGN) * ROW_ALIGN
    cd = MXU_DTYPE
    p = dict(p)

    w_f, w_r = p["w_in"]

    pos = jnp.arange(r, dtype=F32)
    inv_freq = ROPE_THETA ** (-jnp.arange(HALF, dtype=F32) / HALF)
    ang = pos[:, None] * inv_freq[None, :]
    cos_t = jnp.tile(jnp.cos(ang), (1, HP // HALF))
    sin_t = jnp.tile(jnp.sin(ang), (1, HP // HALF))
    bf_row = jnp.zeros((1, HP), F32).at[0, LANE_FL:LANE_FL + HEADS].set(p["b_forget"])

    h0, h0b = _ln_emb_fwd(x, p["meta_tokens"], p["ln_emb_g"], p["ln_emb_b"], r, after=comm.first_token)
    proj_f = _matmul("in_proj_f", h0b, w_f, out_dtype=cd)
    proj_r = _matmul("in_proj_r", h0b, w_r)
    latent_gains = (p["q_norm_g"], p["kv_norm_g"])
    ql, kvl = _latent_norm_fwd(proj_r, latent_gains)
    p.update(comm.late_weights("qkv", ql))
    w_q = _pad_head_cols(p["w_q_up"], QK_DIM)
    w_kv = jnp.concatenate(_split_w_kv(p["w_kv_up"]), axis=1)
    q_raw = _matmul("q_up", ql, w_q)
    kv = _matmul("kv_up", kvl, w_kv, out_dtype=cd)
    q_mla, k_mla = _rope_fwd(q_raw, kv, proj_r, cos_t, sin_t)
    o_mla, o_mla_b, lse_mla = _attn_fwd("mla_fwd", (q_mla, 0), (k_mla, 0), (kv, 1))

    cum, cum_t = _forget_fwd(proj_r, bf_row)
    o_fox, o_fox_b, lse_fox = _attn_fwd("fox_fwd", (proj_f, 0), (proj_f, 1), (proj_f, 2), cum, cum_t)

    p.update(comm.late_weights("mix", o_fox_b))
    w_bm = _pad_head_rows(p["w_branch_mla"], V_DIM)
    w_bf = _pad_head_rows(p["w_branch_fox"], FOX_DIM)
    bm = _matmul("branch_mla", o_mla_b, w_bm, out_dtype=cd)
    bfx = _matmul("branch_fox", o_fox_b, w_bf, out_dtype=cd)
    merged = _gate_fwd(proj_r, p["b_gate"], bm, bfx)
    mix = _matmul("out_proj", merged, p["w_out"])
    h1, h1b = _ln_fwd("ln_mix_fwd", h0, mix, p["ln_mix_g"], p["ln_mix_b"])
    p.update(comm.late_weights("ffn", h1b))
    up = _matmul("ffn_up", h1b, p["w_ffn_up"], out_dtype=cd)
    act = _glu_fwd(up, p["conv_w"], p["conv_b"])
    f = _matmul("ffn_down", act, p["w_ffn_down"])
    loss = _ln_ffn_loss(h1, f, tgt, p["ln_ffn_g"], p["ln_ffn_b"])

    g = {}
    dz2, dz2b, g["ln_ffn_g"], g["ln_ffn_b"] = _ln_ffn_bwd(h1, f, tgt, p["ln_ffn_g"], p["ln_ffn_b"])
    d_act = _matmul("ffn_down_dx", dz2b, p["w_ffn_down"], tb=True, out_dtype=cd)
    g["w_ffn_down"] = _matmul("ffn_down_dw", act, dz2b, ta=True, out_dtype=cd)
    d_up, dcw, g["conv_b"] = _glu_bwd(up, p["conv_w"], p["conv_b"], d_act)
    g["conv_w"] = dcw[:3]
    dh1 = _matmul("ffn_up_dx", d_up, p["w_ffn_up"], tb=True, addend=dz2, alpha=ALPHA)
    g["w_ffn_up"] = _matmul("ffn_up_dw", h1b, d_up, ta=True, out_dtype=cd)
    sent = comm.send("ffn", {n: g[n] for n in ("w_ffn_down", "w_ffn_up", "conv_w")})
    dz1, dz1b, g["ln_mix_g"], g["ln_mix_b"] = _ln_bwd("ln_mix_bwd", h0, mix, dh1, p["ln_mix_g"], after=sent)
    dmerged = _matmul("out_proj_dx", dz1b, p["w_out"], tb=True, out_dtype=cd)
    g["w_out"] = _matmul("out_proj_dw", merged, dz1b, ta=True, out_dtype=cd)
    d_bm, d_bf, d_gl, g["b_gate"] = _gate_bwd(proj_r, p["b_gate"], bm, bfx, dmerged)
    do_mla_b = _matmul("branch_mla_dx", d_bm, w_bm, tb=True, out_dtype=cd)
    g["w_branch_mla"] = _unpad_head_rows(_matmul("branch_mla_dw", o_mla_b, d_bm, ta=True, out_dtype=cd), V_DIM)
    do_fox_b = _matmul("branch_fox_dx", d_bf, w_bf, tb=True, out_dtype=cd)
    g["w_branch_fox"] = _unpad_head_rows(_matmul("branch_fox_dw", o_fox_b, d_bf, ta=True, out_dtype=cd), FOX_DIM)

    sent = comm.send("mix", {n: g[n] for n in ("w_out", "w_branch_mla", "w_branch_fox")})
    dq_m, dk_m, dv_m = _attn_bwd("mla_bwd", (q_mla, 0), (k_mla, 0), (kv, 1), do_mla_b, o_mla, lse_mla, after=sent)
    dfq, dfk, dfv, dcq, dck = _attn_bwd("fox_bwd", (proj_f, 0), (proj_f, 1), (proj_f, 2), do_fox_b, o_fox, lse_fox,
                                        cum, cum_t, out_dtype=cd)
    dfl, dbf = _forget_bwd(proj_r, bf_row, dcq, dck)
    g["b_forget"] = dbf[:, LANE_FL:LANE_FL + HEADS]

    dq_b, dkv_b, dlast = _rope_bwd(dq_m, dk_m, dv_m, dfl, cos_t, sin_t)
    d_ql = _matmul("q_up_dx", dq_b, w_q, tb=True)
    d_kvl = _matmul("kv_up_dx", dkv_b, w_kv, tb=True)
    d_qlat, d_kvlat, g["q_norm_g"], g["kv_norm_g"] = _latent_norm_bwd(proj_r, (d_ql, d_kvl), latent_gains)
    side_by_side = lambda parts, cols: [(0, None, c0, si, None, 0, a.shape[1], 1.0) for si, (a, c0) in enumerate(zip(parts, cols))]
    dproj_f = _remap("dproj_f_pack", [dfq, dfk, dfv], [((r, F_W), cd)], side_by_side([dfq, dfk, dfv], (0, HW, 2 * HW)))[0]
    rest_parts = [d_qlat, d_kvlat, dlast, d_gl]
    dproj_r = _remap("dproj_r_pack", rest_parts, [((r, R_W), cd)],
                     side_by_side(rest_parts, (R_QLAT, R_KVLAT, R_LAST, R_GATE)))[0]
    g["w_in"] = (_matmul("in_proj_f_dw", h0b, dproj_f, ta=True, out_dtype=cd),
                 _matmul("in_proj_r_dw", h0b, dproj_r, ta=True, out_dtype=cd))
    sent = comm.send("in", {"w_in": g["w_in"]})
    dh0 = _matmul("in_proj_f_dx", dproj_f, w_f, tb=True, addend=dz1, alpha=ALPHA, after=sent)
    g["w_q_up"] = _unpad_head_cols(_matmul("q_up_dw", ql, dq_b, ta=True, out_dtype=cd, after=sent), QK_DIM)
    dw_kv = _matmul("kv_up_dw", kvl, dkv_b, ta=True, out_dtype=cd, after=sent)
    g["w_kv_up"] = _merge_w_kv(dw_kv[:, :HW], dw_kv[:, HW:])
    sent = comm.send("qkv", {n: g[n] for n in ("w_q_up", "w_kv_up")})
    dh0 = _matmul("in_proj_r_dx", dproj_r, w_r, tb=True, addend=dh0, after=sent)
    grad_x, d_meta, g["ln_emb_g"], g["ln_emb_b"] = _ln_emb_bwd(x, p["meta_tokens"], dh0, p["ln_emb_g"])
    return loss, grad_x, d_meta, g


BIG = (("w_in", 1), ("w_q_up", 1), ("w_kv_up", 1), ("w_branch_mla", 1), ("w_branch_fox", 1), ("w_out", 0),
       ("w_ffn_up", 1), ("w_ffn_down", 0))
SMALL_SHARDED = (("meta_tokens", 1), ("conv_w", 1))
EARLY = ("w_in", "meta_tokens")
LATE = {"qkv": ("w_q_up", "w_kv_up", "conv_w"),
        "mix": ("w_branch_mla", "w_branch_fox", "w_out"),
        "ffn": ("w_ffn_up", "w_ffn_down")}
REPLICATED = ("ln_emb_g", "ln_emb_b", "b_gate", "b_forget", "q_norm_g", "kv_norm_g", "ln_mix_g", "ln_mix_b",
              "conv_b", "ln_ffn_g", "ln_ffn_b")
PACK_COLS = 1024


def _pack(flat_list):
    cat = jnp.concatenate(flat_list)
    n = cat.shape[0]
    rows = -(-n // (8 * PACK_COLS)) * 8
    return jnp.pad(cat, (0, rows * PACK_COLS - n)).reshape(rows, PACK_COLS)


def _gathered_full(g3, axis):
    n, r, c = g3.shape
    if axis == 0:
        return g3.reshape(n * r, c)
    return g3.transpose(1, 0, 2).reshape(r, n * c)


def _shard_major(full, axis):
    r, c = full.shape
    if axis == 0:
        return full.reshape(N_DEV, r // N_DEV, c)
    return full.reshape(r, N_DEV, c // N_DEV).transpose(1, 0, 2)


def kernel(x, meta_tokens, ln_emb_g, ln_emb_b, w_in, b_gate, b_forget, q_norm_g, w_q_up, kv_norm_g, w_kv_up, w_branch_mla, w_branch_fox, w_out, ln_mix_g, ln_mix_b, w_ffn_up, conv_w, conv_b, w_ffn_down, ln_ffn_g, ln_ffn_b, loss_target, m_meta_tokens, m_ln_emb_g, m_ln_emb_b, m_w_in, m_b_gate, m_b_forget, m_q_norm_g, m_w_q_up, m_kv_norm_g, m_w_kv_up, m_w_branch_mla, m_w_branch_fox, m_w_out, m_ln_mix_g, m_ln_mix_b, m_w_ffn_up, m_conv_w, m_conv_b, m_w_ffn_down, m_ln_ffn_g, m_ln_ffn_b, v_meta_tokens, v_ln_emb_g, v_ln_emb_b, v_w_in, v_b_gate, v_b_forget, v_q_norm_g, v_w_q_up, v_kv_norm_g, v_w_kv_up, v_w_branch_mla, v_w_branch_fox, v_w_out, v_ln_mix_g, v_ln_mix_b, v_w_ffn_up, v_conv_w, v_conv_b, v_w_ffn_down, v_ln_ffn_g, v_ln_ffn_b):
    names = ("meta_tokens", "ln_emb_g", "ln_emb_b", "w_in", "b_gate", "b_forget", "q_norm_g", "w_q_up", "kv_norm_g",
             "w_kv_up", "w_branch_mla", "w_branch_fox", "w_out", "ln_mix_g", "ln_mix_b", "w_ffn_up", "conv_w", "conv_b",
             "w_ffn_down", "ln_ffn_g", "ln_ffn_b")
    w_args = (meta_tokens, ln_emb_g, ln_emb_b, w_in, b_gate, b_forget, q_norm_g, w_q_up, kv_norm_g, w_kv_up,
              w_branch_mla, w_branch_fox, w_out, ln_mix_g, ln_mix_b, w_ffn_up, conv_w, conv_b, w_ffn_down, ln_ffn_g, ln_ffn_b)
    m_args = (m_meta_tokens, m_ln_emb_g, m_ln_emb_b, m_w_in, m_b_gate, m_b_forget, m_q_norm_g, m_w_q_up, m_kv_norm_g,
              m_w_kv_up, m_w_branch_mla, m_w_branch_fox, m_w_out, m_ln_mix_g, m_ln_mix_b, m_w_ffn_up, m_conv_w, m_conv_b,
              m_w_ffn_down, m_ln_ffn_g, m_ln_ffn_b)
    v_args = (v_meta_tokens, v_ln_emb_g, v_ln_emb_b, v_w_in, v_b_gate, v_b_forget, v_q_norm_g, v_w_q_up, v_kv_norm_g,
              v_w_kv_up, v_w_branch_mla, v_w_branch_fox, v_w_out, v_ln_mix_g, v_ln_mix_b, v_w_ffn_up, v_conv_w, v_conv_b,
              v_w_ffn_down, v_ln_ffn_g, v_ln_ffn_b)
    as2d = lambda a: a.reshape((-1, a.shape[-1])) if a.ndim != 1 else a.reshape(1, -1)
    w = {n: as2d(a) for n, a in zip(names, w_args)}
    m = {n: as2d(a) for n, a in zip(names, m_args)}
    v = {n: as2d(a) for n, a in zip(names, v_args)}
    out_shape = {n: a.shape for n, a in zip(names, w_args)}

    axis_of = dict(BIG + SMALL_SHARDED)
    big = set(n for n, _ in BIG)
    wire = lambda n, a: a.astype(MXU_DTYPE) if n in big else a
    my_id = _my_id()

    early = _allgather("gather_early", [wire(n, w[n]) for n in EARLY])
    p = {n: _gathered_full(g3, axis_of[n]) for n, g3 in zip(EARLY, early) if n != "w_in"}
    p["w_in"] = _w_in_from_shards(early[EARLY.index("w_in")])
    for n in REPLICATED:
        p[n] = w[n].reshape(-1)
    late_src = [[wire(n, w[n]) for n in members] for members in LATE.values()]
    late_handles, late_token = _push_start("gather_late_start", late_src, False, after=early[0])
    late = {group: (members, src, handle)
            for (group, members), src, handle in zip(LATE.items(), late_src, late_handles)}
    sent = {}

    class Comm:
        first_token = (late_token,)

        def late_weights(self, group, after):
            members, src, handle = late[group]
            lands = _push_wait("gather_" + group + "_wait", handle, after)
            out = {}
            for n, own, land in zip(members, src, lands):
                if own.shape[0] % 16:
                    out[n] = _gathered_full(lax.dynamic_update_index_in_dim(land, own, my_id, 0), axis_of[n])
                elif axis_of[n] == 1:
                    out[n] = _cols_from_shards(n + "_repack", land, own)
                else:
                    out[n] = _rows_from_shards(n + "_repack", land, own)
            return out

        def send(self, name, grads):
            names_ = tuple(grads)
            parts = []
            for n in names_:
                if n == "w_in":
                    parts.append(_w_in_grad_to_shards(*grads[n], N_DEV, w[n].shape[1]))
                elif n == "w_ffn_up":
                    parts.append(_cols_to_shards(n + "_grad_unpack", grads[n], N_DEV))
                else:
                    parts.append(_shard_major(grads[n], axis_of[n]).astype(MXU_DTYPE))
            (handle,), token = _push_start("send_" + name + "_start", [parts], True)
            sent[name] = (names_, parts, handle)
            return (token,)

    loss_part, grad_x, d_meta, g = _local_step(x[0], loss_target[0], p, Comm())
    grad_x = grad_x[None]

    small = _pack([d_meta.reshape(-1)] + [g[n].reshape(-1) for n in REPLICATED] + [loss_part.reshape(-1)])
    (small_handle,), small_token = _push_start("send_small_start", [[small]], False)

    res = {}
    prev = small_token
    for name, (names_, parts, handle) in sent.items():
        lands = _push_wait("send_" + name + "_wait", handle, prev)
        for n, part, land in zip(names_, parts, lands):
            own = lax.dynamic_index_in_dim(part, my_id, axis=0, keepdims=False)
            res[n] = _adamw("adamw_" + n, land, w[n], m[n], v[n], own=own)
            prev = res[n][0]
    small_all = _push_wait("send_small_wait", small_handle, prev)[0]
    head = jnp.zeros((d_meta.size,), F32)
    rep_w = _pack([head] + [w[n].reshape(-1) for n in REPLICATED])
    rep_m = _pack([head] + [m[n].reshape(-1) for n in REPLICATED])
    rep_v = _pack([head] + [v[n].reshape(-1) for n in REPLICATED])
    rep_res = _adamw("adamw_replicated", small_all, rep_w, rep_m, rep_v, own=small)
    off = d_meta.size
    for n in REPLICATED:
        sz = w[n].size
        res[n] = tuple(a.reshape(-1)[off:off + sz] for a in rep_res)
        off += sz
    loss = rep_res[0].reshape(-1)[off]
    cols = w["meta_tokens"].shape[1]
    meta_rows = lambda a: a.reshape(a.shape[:-2] + (-1,))[..., :d_meta.size].reshape(a.shape[:-2] + d_meta.shape)
    my_cols = lambda a: lax.dynamic_slice_in_dim(a, my_id * cols, cols, axis=a.ndim - 1)
    res["meta_tokens"] = _adamw("adamw_meta_tokens", my_cols(meta_rows(small_all)), w["meta_tokens"],
                                m["meta_tokens"], v["meta_tokens"], own=my_cols(d_meta))

    outs = [loss, grad_x]
    for idx in range(4):
        outs += [res[n][idx].reshape(out_shape[n]) for n in names]
    return tuple(outs)
```

```python
import jax
import jax.numpy as jnp
from jax import lax
from jax.experimental import pallas as pl
from jax.experimental.pallas import tpu as pltpu

F32 = jnp.float32
BF16 = jnp.bfloat16
MXU_DTYPE = BF16

N_DEV = 8
N_META = 16
D_MODEL = 1024
HEADS = 8
Q_RANK = 384
KV_RANK = 128
NOPE = 64
ROPE = 32
HALF = ROPE // 2
QK_DIM = NOPE + ROPE
V_DIM = 64
FOX_DIM = 64
FOX_W = HEADS * FOX_DIM
D_FF = 2816
ROPE_THETA = 10000.0
LN_EPS = 1e-5
RMS_EPS = 1e-6
ALPHA = 2.0 ** 0.25
MLA_SCALE = QK_DIM ** -0.5
FOX_SCALE = FOX_DIM ** -0.5
NEG_INF = -1e30

HP = 128
HW = HEADS * HP
F_W = 3 * FOX_W
R_GATE = 0
R_KVLAT = R_GATE + 2 * D_MODEL
R_LAST = R_KVLAT + KV_RANK
R_QLAT = R_LAST + HP
R_W = R_QLAT + Q_RANK
assert R_QLAT % Q_RANK == 0 and R_KVLAT % KV_RANK == 0 and R_GATE % D_MODEL == 0 and R_W % HP == 0
LANE_FL = 0
LANE_PE = NOPE

ADAM_LR = 0.001
ADAM_B1 = 0.9
ADAM_B2 = 0.999
ADAM_EPS = 1e-08
ADAM_WD = 0.01
ADAM_STEP = 10

ROW_BLOCK = 256
ATT_TQ = 768
ATT_TK = 768
ATT_HEADS = 2
ATT_HEADS_FWD = 4
ROW_ALIGN = 768
MM_BLOCK_CAP = 1408
VMEM_LIMIT = 56 * 1024 * 1024
HIGHEST = lax.Precision.HIGHEST
NT = (((1,), (1,)), ((), ()))
TN = (((0,), (0,)), ((), ()))


def _params(sem=None):
    return pltpu.CompilerParams(dimension_semantics=sem, vmem_limit_bytes=VMEM_LIMIT)


def _call(name, body, grid, ins, outs, scratch=(), sem=None, after=()):
    n_in = len(ins)
    n_tok = len(after)

    def run(*refs):
        body(*refs[:n_in], *refs[n_in + n_tok:])

    tok_spec = pl.BlockSpec((8, 128), lambda *_: (0, 0))
    return pl.pallas_call(
        run, name=name, grid=grid,
        in_specs=[s for _, s in ins] + [tok_spec] * n_tok,
        out_specs=[s for _, s in outs],
        out_shape=[o for o, _ in outs],
        scratch_shapes=list(scratch),
        compiler_params=_params(sem),
    )(*[a for a, _ in ins], *after)


def _sds(shape, dtype):
    return jax.ShapeDtypeStruct(shape, dtype)


def _rows(br, c, cb=0):
    return pl.BlockSpec((br, c), lambda i: (i, cb))


def _whole(shape):
    n = len(shape)
    return pl.BlockSpec(shape, lambda i: (0,) * n)


def _pick(dim, cap, mult):
    best = None
    d = mult
    while d <= min(dim, cap):
        if dim % d == 0:
            best = d
        d += mult
    return best if best is not None else dim


def _hs(h):
    return slice(h * HP, (h + 1) * HP)


def _matmul(name, a, b, *, ta=False, tb=False, out_dtype=F32, addend=None, alpha=1.0, after=()):
    if ta:
        k, m = a.shape
    else:
        m, k = a.shape
    if tb:
        n, k2 = b.shape
    else:
        k2, n = b.shape
    assert k == k2, (name, a.shape, b.shape)
    bm = _pick(m, MM_BLOCK_CAP, 128 if ta else 16)
    bn = _pick(n, MM_BLOCK_CAP, 128)
    bk = _pick(k, MM_BLOCK_CAP, 128 if (not ta or tb) else 16)
    nk = k // bk
    dims = (((0 if ta else 1,), (1 if tb else 0,)), ((), ()))
    has_add = addend is not None

    def body(*refs):
        a_ref, b_ref = refs[:2]
        add_ref = refs[2] if has_add else None
        o_ref = refs[3 if has_add else 2]

        def finish(r):
            if has_add:
                r = r + alpha * add_ref[...]
            o_ref[...] = r.astype(o_ref.dtype)

        part = lax.dot_general(a_ref[...], b_ref[...], dims, preferred_element_type=F32)
        if nk == 1:
            finish(part)
            return
        acc_ref = refs[-1]
        kk = pl.program_id(2)

        @pl.when(kk == 0)
        def _():
            acc_ref[...] = part

        @pl.when(kk > 0)
        def _():
            acc_ref[...] += part

        @pl.when(kk == nk - 1)
        def _():
            finish(acc_ref[...])

    a_spec = pl.BlockSpec((bk, bm), lambda i, j, l: (l, i)) if ta else pl.BlockSpec((bm, bk), lambda i, j, l: (i, l))
    b_spec = pl.BlockSpec((bn, bk), lambda i, j, l: (j, l)) if tb else pl.BlockSpec((bk, bn), lambda i, j, l: (l, j))
    o_spec = pl.BlockSpec((bm, bn), lambda i, j, l: (i, j))
    ins = [(a, a_spec), (b, b_spec)]
    if has_add:
        ins.append((addend, o_spec))
    return _call(name, body, (m // bm, n // bn, nk), ins, [(_sds((m, n), out_dtype), o_spec)],
                 scratch=[pltpu.VMEM((bm, bn), F32)] if nk > 1 else [],
                 sem=("parallel", "parallel", "arbitrary"), after=after)[0]


def _ln_stats(z):
    mu = jnp.mean(z, axis=-1, keepdims=True)
    zc = z - mu
    var = jnp.mean(zc * zc, axis=-1, keepdims=True)
    rstd = lax.rsqrt(var + LN_EPS)
    return zc * rstd, rstd


def _ln_fwd(name, a, res, g, b, after=()):
    r, d = a.shape
    br = ROW_BLOCK
    has_res = res is not None

    def body(*refs):
        if has_res:
            a_ref, r_ref, g_ref, b_ref, y_ref, yb_ref = refs
            z = ALPHA * a_ref[...] + r_ref[...]
        else:
            a_ref, g_ref, b_ref, y_ref, yb_ref = refs
            z = a_ref[...]
        xhat, _ = _ln_stats(z)
        y = xhat * g_ref[...] + b_ref[...]
        y_ref[...] = y
        yb_ref[...] = y.astype(yb_ref.dtype)

    ins = [(a, _rows(br, d))]
    if has_res:
        ins.append((res, _rows(br, d)))
    ins += [(g.reshape(1, d), _whole((1, d))), (b.reshape(1, d), _whole((1, d)))]
    outs = [(_sds((r, d), F32), _rows(br, d)), (_sds((r, d), MXU_DTYPE), _rows(br, d))]
    return _call(name, body, (r // br,), ins, outs, sem=("parallel",), after=after)


def _ln_bwd(name, a, res, dy, g, after=()):
    r, d = a.shape
    br = ROW_BLOCK
    has_res = res is not None

    def body(*refs):
        if has_res:
            a_ref, r_ref, dy_ref, g_ref, dz_ref, dzb_ref, dg_ref, db_ref = refs
            z = ALPHA * a_ref[...] + r_ref[...]
        else:
            a_ref, dy_ref, g_ref, dz_ref, dzb_ref, dg_ref, db_ref = refs
            z = a_ref[...]
        xhat, rstd = _ln_stats(z)
        dyv = dy_ref[...]
        dyg = dyv * g_ref[...]
        m1 = jnp.mean(dyg, axis=-1, keepdims=True)
        m2 = jnp.mean(dyg * xhat, axis=-1, keepdims=True)
        dz = rstd * (dyg - m1 - xhat * m2)
        dz_ref[...] = dz
        dzb_ref[...] = dz.astype(dzb_ref.dtype)

        @pl.when(pl.program_id(0) == 0)
        def _():
            dg_ref[...] = jnp.zeros_like(dg_ref)
            db_ref[...] = jnp.zeros_like(db_ref)

        dg_ref[...] += jnp.sum(dyv * xhat, axis=0, keepdims=True)
        db_ref[...] += jnp.sum(dyv, axis=0, keepdims=True)

    ins = [(a, _rows(br, d))]
    if has_res:
        ins.append((res, _rows(br, d)))
    ins += [(dy, _rows(br, d)), (g.reshape(1, d), _whole((1, d)))]
    outs = [(_sds((r, d), F32), _rows(br, d)), (_sds((r, d), MXU_DTYPE), _rows(br, d)),
            (_sds((1, d), F32), _whole((1, d))), (_sds((1, d), F32), _whole((1, d)))]
    return _call(name, body, (r // br,), ins, outs, sem=("arbitrary",), after=after)


LATENTS = ((R_QLAT // Q_RANK, Q_RANK), (R_KVLAT // KV_RANK, KV_RANK))


def _latent_norm_fwd(proj_r, gains):
    r = proj_r.shape[0]
    br = ROW_BLOCK

    def body(xq_ref, xk_ref, gq_ref, gk_ref, yq_ref, yk_ref):
        for x_ref, g_ref, y_ref in ((xq_ref, gq_ref, yq_ref), (xk_ref, gk_ref, yk_ref)):
            x = x_ref[...]
            rstd = lax.rsqrt(jnp.mean(x * x, axis=-1, keepdims=True) + RMS_EPS)
            y_ref[...] = (x * rstd * g_ref[...]).astype(y_ref.dtype)

    return _call("latent_norm_fwd", body, (r // br,),
                 [(proj_r, _rows(br, wd, cb)) for cb, wd in LATENTS]
                 + [(g.reshape(1, wd), _whole((1, wd))) for g, (_, wd) in zip(gains, LATENTS)],
                 [(_sds((r, wd), MXU_DTYPE), _rows(br, wd)) for _, wd in LATENTS], sem=("parallel",))


def _latent_norm_bwd(proj_r, dys, gains):
    r = proj_r.shape[0]
    br = ROW_BLOCK

    def body(xq_ref, xk_ref, dq_ref, dk_ref, gq_ref, gk_ref, oq_ref, ok_ref, dgq_ref, dgk_ref):
        @pl.when(pl.program_id(0) == 0)
        def _():
            dgq_ref[...] = jnp.zeros_like(dgq_ref)
            dgk_ref[...] = jnp.zeros_like(dgk_ref)

        for x_ref, dy_ref, g_ref, dx_ref, dg_ref in ((xq_ref, dq_ref, gq_ref, oq_ref, dgq_ref),
                                                     (xk_ref, dk_ref, gk_ref, ok_ref, dgk_ref)):
            x = x_ref[...]
            rstd = lax.rsqrt(jnp.mean(x * x, axis=-1, keepdims=True) + RMS_EPS)
            nrm = x * rstd
            dyv = dy_ref[...]
            dyg = dyv * g_ref[...]
            dx_ref[...] = (rstd * (dyg - nrm * jnp.mean(dyg * nrm, axis=-1, keepdims=True))).astype(dx_ref.dtype)
            dg_ref[...] += jnp.sum(dyv * nrm, axis=0, keepdims=True)

    return _call("latent_norm_bwd", body, (r // br,),
                 [(proj_r, _rows(br, wd, cb)) for cb, wd in LATENTS]
                 + [(dy, _rows(br, wd)) for dy, (_, wd) in zip(dys, LATENTS)]
                 + [(g.reshape(1, wd), _whole((1, wd))) for g, (_, wd) in zip(gains, LATENTS)],
                 [(_sds((r, wd), MXU_DTYPE), _rows(br, wd)) for _, wd in LATENTS]
                 + [(_sds((1, wd), F32), _whole((1, wd))) for _, wd in LATENTS], sem=("arbitrary",))


def _lane_iota(shape):
    return lax.broadcasted_iota(jnp.int32, shape, 1)


def _rotary(t, c, s, lane, sign):
    second = pltpu.roll(t, HP - HALF, axis=1)
    first = pltpu.roll(t, HALF, axis=1)
    lo = (lane >= LANE_PE) & (lane < LANE_PE + HALF)
    hi = (lane >= LANE_PE + HALF) & (lane < LANE_PE + ROPE)
    return jnp.where(lo, t * c - sign * second * s, jnp.where(hi, t * c + sign * first * s, t))


def _rope_fwd(q_raw, k_part, proj_r, cos_t, sin_t):
    r = q_raw.shape[0]
    br = ROW_BLOCK

    def body(q_ref, k_ref, t_ref, c_ref, s_ref, qo_ref, ko_ref):
        c = c_ref[...]
        s = s_ref[...]
        lane = _lane_iota((br, HP))
        pe = (lane >= LANE_PE) & (lane < LANE_PE + ROPE)
        kp = jnp.where(pe, _rotary(t_ref[...], c, s, lane, 1.0), 0.0)
        for h in range(HEADS):
            qo_ref[:, _hs(h)] = (_rotary(q_ref[:, _hs(h)], c, s, lane, 1.0) * MLA_SCALE).astype(qo_ref.dtype)
            ko_ref[:, _hs(h)] = (k_ref[:, _hs(h)] + kp).astype(ko_ref.dtype)

    blk = _rows(br, HP)
    wide = _rows(br, HW)
    return _call("rope_fwd", body, (r // br,),
                 [(q_raw, wide), (k_part, wide), (proj_r, _rows(br, HP, R_LAST // HP)), (cos_t, blk), (sin_t, blk)],
                 [(_sds((r, HW), MXU_DTYPE), wide)] * 2, sem=("parallel",))


def _rope_bwd(dq, dk, dv, dfl, cos_t, sin_t):
    r = dq.shape[0]
    br = ROW_BLOCK

    def body(dq_ref, dk_ref, dv_ref, fl_ref, c_ref, s_ref, dqo_ref, dkv_ref, dl_ref):
        c = c_ref[...]
        s = s_ref[...]
        lane = _lane_iota((br, HP))
        pe = (lane >= LANE_PE) & (lane < LANE_PE + ROPE)
        acc = jnp.zeros((br, HP), F32)
        for h in range(HEADS):
            dqo_ref[:, _hs(h)] = (_rotary(dq_ref[:, _hs(h)], c, s, lane, -1.0) * MLA_SCALE).astype(dqo_ref.dtype)
            dkh = dk_ref[:, _hs(h)]
            acc = acc + dkh
            dkv_ref[:, _hs(h)] = dkh.astype(dkv_ref.dtype)
            dkv_ref[:, _hs(HEADS + h)] = dv_ref[:, _hs(h)].astype(dkv_ref.dtype)
        dl_ref[...] = (jnp.where(pe, _rotary(acc, c, s, lane, -1.0), 0.0) + fl_ref[...]).astype(dl_ref.dtype)

    blk = _rows(br, HP)
    wide = _rows(br, HW)
    return _call("rope_bwd", body, (r // br,),
                 [(dq, wide), (dk, wide), (dv, wide), (dfl, blk), (cos_t, blk), (sin_t, blk)],
                 [(_sds((r, HW), MXU_DTYPE), wide), (_sds((r, 2 * HW), MXU_DTYPE), _rows(br, 2 * HW)),
                  (_sds((r, HP), MXU_DTYPE), blk)],
                 sem=("parallel",))


def _log_sigmoid(x):
    return jnp.minimum(x, 0.0) - jnp.log(1.0 + jnp.exp(-jnp.abs(x)))


def _head_lane(x, h, lane):
    return jnp.sum(jnp.where(lane == h, x, 0.0), axis=1, keepdims=True)


def _forget_fwd(proj_r, bf_row):
    r = proj_r.shape[0]
    br = ROW_BLOCK

    def body(t_ref, b_ref, ob_ref, ot_ref, carry_ref):
        @pl.when(pl.program_id(0) == 0)
        def _():
            carry_ref[...] = jnp.zeros_like(carry_ref)

        x = t_ref[...] + b_ref[...]
        lane = _lane_iota(x.shape)
        lf = jnp.where((lane >= LANE_FL) & (lane < LANE_FL + HEADS), _log_sigmoid(x), 0.0)
        tri = (lax.broadcasted_iota(jnp.int32, (br, br), 0) >= lax.broadcasted_iota(jnp.int32, (br, br), 1)).astype(F32)
        cum = jnp.dot(tri, lf, precision=HIGHEST, preferred_element_type=F32) + carry_ref[0:1, :]
        for h in range(HEADS):
            ob_ref[:, _hs(h)] = jnp.broadcast_to(_head_lane(cum, LANE_FL + h, lane), (br, HP))
        ot_ref[...] = cum.T[LANE_FL:LANE_FL + HEADS, :]
        carry_ref[...] = jnp.broadcast_to(cum[br - 1:br, :], carry_ref.shape)

    return _call("forget_fwd", body, (r // br,),
                 [(proj_r, _rows(br, HP, R_LAST // HP)), (bf_row, _whole((1, HP)))],
                 [(_sds((r, HW), F32), _rows(br, HW)), (_sds((HEADS, r), F32), pl.BlockSpec((HEADS, br), lambda i: (0, i)))],
                 scratch=[pltpu.VMEM((8, HP), F32)], sem=("arbitrary",))


def _forget_bwd(proj_r, bf_row, dcq_t, dck_b):
    r = proj_r.shape[0]
    br = ROW_BLOCK
    nb = r // br

    def body(t_ref, b_ref, dcq_ref, dck_ref, o_ref, db_ref, carry_ref):
        @pl.when(pl.program_id(0) == 0)
        def _():
            carry_ref[...] = jnp.zeros_like(carry_ref)
            db_ref[...] = jnp.zeros_like(db_ref)

        lane = _lane_iota((br, HP))
        dc = jnp.concatenate([dcq_ref[...], jnp.zeros((HP - HEADS, br), F32)], axis=0).T
        for h in range(HEADS):
            dc = dc + jnp.where(lane == LANE_FL + h, dck_ref[:, h * HP:h * HP + 1], 0.0)
        triu = (lax.broadcasted_iota(jnp.int32, (br, br), 0) <= lax.broadcasted_iota(jnp.int32, (br, br), 1)).astype(F32)
        dlf = jnp.dot(triu, dc, precision=HIGHEST, preferred_element_type=F32) + carry_ref[0:1, :]
        carry_ref[...] = jnp.broadcast_to(dlf[0:1, :], carry_ref.shape)
        x = t_ref[...] + b_ref[...]
        dfl = jnp.where((lane >= LANE_FL) & (lane < LANE_FL + HEADS), dlf * jax.nn.sigmoid(-x), 0.0)
        o_ref[...] = dfl
        db_ref[...] += jnp.sum(dfl, axis=0, keepdims=True)

    rev = pl.BlockSpec((br, HP), lambda i: (nb - 1 - i, 0))
    return _call("forget_bwd", body, (nb,),
                 [(proj_r, pl.BlockSpec((br, HP), lambda i: (nb - 1 - i, R_LAST // HP))), (bf_row, _whole((1, HP))),
                  (dcq_t, pl.BlockSpec((HEADS, br), lambda i: (0, nb - 1 - i))),
                  (dck_b, pl.BlockSpec((br, HW), lambda i: (nb - 1 - i, 0)))],
                 [(_sds((r, HP), F32), rev), (_sds((1, HP), F32), _whole((1, HP)))],
                 scratch=[pltpu.VMEM((8, HP), F32)], sem=("arbitrary",))


def _gate_fwd(proj_r, b_gate, bm, bfx):
    r, d = bm.shape
    br = ROW_BLOCK
    cb = R_GATE // d

    def body(gm_ref, gf_ref, b1_ref, b2_ref, bm_ref, bf_ref, o_ref):
        g1 = jax.nn.sigmoid(gm_ref[...] + b1_ref[...])
        g2 = jax.nn.sigmoid(gf_ref[...] + b2_ref[...])
        o_ref[...] = (g1 * bm_ref[...].astype(F32) + g2 * bf_ref[...].astype(F32)).astype(o_ref.dtype)

    b1 = b_gate[:d].reshape(1, d)
    b2 = b_gate[d:].reshape(1, d)
    return _call("gate_fwd", body, (r // br,),
                 [(proj_r, _rows(br, d, cb)), (proj_r, _rows(br, d, cb + 1)), (b1, _whole((1, d))), (b2, _whole((1, d))),
                  (bm, _rows(br, d)), (bfx, _rows(br, d))],
                 [(_sds((r, d), MXU_DTYPE), _rows(br, d))], sem=("parallel",))[0]


def _gate_bwd(proj_r, b_gate, bm, bfx, dmerged):
    r, d = bm.shape
    br = ROW_BLOCK
    cb = R_GATE // d

    def body(gm_ref, gf_ref, b1_ref, b2_ref, bm_ref, bf_ref, dm_ref, dbm_ref, dbf_ref, dgl_ref, dbg_ref):
        g1 = jax.nn.sigmoid(gm_ref[...] + b1_ref[...])
        g2 = jax.nn.sigmoid(gf_ref[...] + b2_ref[...])
        dm = dm_ref[...].astype(F32)
        dbm_ref[...] = (dm * g1).astype(dbm_ref.dtype)
        dbf_ref[...] = (dm * g2).astype(dbf_ref.dtype)
        dl1 = dm * bm_ref[...].astype(F32) * (g1 * (1.0 - g1))
        dl2 = dm * bf_ref[...].astype(F32) * (g2 * (1.0 - g2))
        dgl_ref[:, 0:d] = dl1.astype(dgl_ref.dtype)
        dgl_ref[:, d:2 * d] = dl2.astype(dgl_ref.dtype)

        @pl.when(pl.program_id(0) == 0)
        def _():
            dbg_ref[...] = jnp.zeros_like(dbg_ref)

        dbg_ref[:, 0:d] += jnp.sum(dl1, axis=0, keepdims=True)
        dbg_ref[:, d:2 * d] += jnp.sum(dl2, axis=0, keepdims=True)

    b1 = b_gate[:d].reshape(1, d)
    b2 = b_gate[d:].reshape(1, d)
    return _call("gate_bwd", body, (r // br,),
                 [(proj_r, _rows(br, d, cb)), (proj_r, _rows(br, d, cb + 1)), (b1, _whole((1, d))), (b2, _whole((1, d))),
                  (bm, _rows(br, d)), (bfx, _rows(br, d)), (dmerged, _rows(br, d))],
                 [(_sds((r, d), MXU_DTYPE), _rows(br, d)), (_sds((r, d), MXU_DTYPE), _rows(br, d)),
                  (_sds((r, 2 * d), MXU_DTYPE), _rows(br, 2 * d)), (_sds((1, 2 * d), F32), _whole((1, 2 * d)))],
                 sem=("arbitrary",))


HALO = 16
GLU_BWD_BLOCK = 128


def _conv_taps(gp, halo, first_block):
    halo = jnp.where(first_block, 0.0, halo.astype(F32))
    rid = lax.broadcasted_iota(jnp.int32, gp.shape, 0)
    last, prev = halo[HALO - 1:HALO, :], halo[HALO - 2:HALO - 1, :]
    g1 = jnp.where(rid == 0, last, pltpu.roll(gp, 1, axis=0))
    g2 = jnp.where(rid == 0, prev, jnp.where(rid == 1, last, pltpu.roll(gp, 2, axis=0)))
    return g1, g2


def _prev_halo(br, c):
    return pl.BlockSpec((HALO, c), lambda i: (jnp.maximum(i * (br // HALO) - 1, 0), 0))


def _glu_fwd(up, conv_w, conv_b):
    r = up.shape[0]
    c = D_FF
    br = ROW_BLOCK

    def body(gp_ref, halo_ref, val_ref, w_ref, b_ref, o_ref):
        gp = gp_ref[...].astype(F32)
        g1, g2 = _conv_taps(gp, halo_ref[...], pl.program_id(0) == 0)
        gate = w_ref[0:1, :] * g2 + w_ref[1:2, :] * g1 + w_ref[2:3, :] * gp + b_ref[...]
        o_ref[...] = (gate * jax.nn.sigmoid(gate) * val_ref[...].astype(F32)).astype(o_ref.dtype)

    return _call("glu_fwd", body, (r // br,),
                 [(up, _rows(br, c, 0)), (up, _prev_halo(br, c)), (up, _rows(br, c, 1)),
                  (conv_w, _whole((3, c))), (conv_b.reshape(1, c), _whole((1, c)))],
                 [(_sds((r, c), MXU_DTYPE), _rows(br, c))], sem=("parallel",))[0]


def _glu_bwd(up, conv_w, conv_b, d_act):
    r = up.shape[0]
    c = D_FF
    br = GLU_BWD_BLOCK
    nb = r // br

    def body(gp_ref, halo_ref, val_ref, da_ref, gpn_ref, valn_ref, dan_ref, w_ref, b_ref, o_ref, dw_ref, db_ref):
        i = pl.program_id(0)
        w0, w1, w2, bias = w_ref[0:1, :], w_ref[1:2, :], w_ref[2:3, :], b_ref[...]

        def d_gate(gp, g1, g2, val, da):
            gate = w0 * g2 + w1 * g1 + w2 * gp + bias
            sg = jax.nn.sigmoid(gate)
            return da * val * (sg * (1.0 + gate * (1.0 - sg))), da * (gate * sg)

        gp = gp_ref[...].astype(F32)
        g1, g2 = _conv_taps(gp, halo_ref[...], i == 0)
        dg, dv = d_gate(gp, g1, g2, val_ref[...].astype(F32), da_ref[...].astype(F32))
        gpn = gpn_ref[...].astype(F32)
        g1n, g2n = _conv_taps(gpn, gp[br - HALO:, :], False)
        dgn, _ = d_gate(gpn, g1n, g2n, valn_ref[...].astype(F32), dan_ref[...].astype(F32))
        dgn = jnp.where(i == nb - 1, 0.0, dgn)
        rid = lax.broadcasted_iota(jnp.int32, dg.shape, 0)
        u1 = jnp.where(rid == br - 1, dgn[0:1, :], pltpu.roll(dg, br - 1, axis=0))
        u2 = jnp.where(rid == br - 1, dgn[1:2, :], jnp.where(rid == br - 2, dgn[0:1, :], pltpu.roll(dg, br - 2, axis=0)))
        o_ref[:, 0:c] = (w2 * dg + w1 * u1 + w0 * u2).astype(o_ref.dtype)
        o_ref[:, c:2 * c] = dv.astype(o_ref.dtype)

        @pl.when(i == 0)
        def _():
            dw_ref[...] = jnp.zeros_like(dw_ref)
            db_ref[...] = jnp.zeros_like(db_ref)

        dw_ref[0:1, :] += jnp.sum(dg * g2, axis=0, keepdims=True)
        dw_ref[1:2, :] += jnp.sum(dg * g1, axis=0, keepdims=True)
        dw_ref[2:3, :] += jnp.sum(dg * gp, axis=0, keepdims=True)
        db_ref[...] += jnp.sum(dg, axis=0, keepdims=True)

    nxt = lambda cb: pl.BlockSpec((HALO, c), lambda i: (jnp.minimum((i + 1) * (br // HALO), r // HALO - 1), cb))
    return _call("glu_bwd", body, (nb,),
                 [(up, _rows(br, c, 0)), (up, _prev_halo(br, c)), (up, _rows(br, c, 1)), (d_act, _rows(br, c)),
                  (up, nxt(0)), (up, nxt(1)), (d_act, nxt(0)),
                  (conv_w, _whole((3, c))), (conv_b.reshape(1, c), _whole((1, c)))],
                 [(_sds((r, 2 * c), MXU_DTYPE), _rows(br, 2 * c)),
                  (_sds((8, c), F32), _whole((8, c))), (_sds((1, c), F32), _whole((1, c)))],
                 sem=("arbitrary",))


def _token_specs(seq, d):
    br = ROW_BLOCK
    nxb = seq // br
    main = pl.BlockSpec((br, d), lambda i: (jnp.minimum(i, nxb - 1), 0))
    tail = pl.BlockSpec((N_META, d), lambda i: (jnp.clip(i * (br // N_META) - 1, 0, seq // N_META - 1), 0))
    return main, tail


def _padded_block(main_ref, tail_ref, first, seq):
    br = ROW_BLOCK
    i = pl.program_id(0)
    nxb = seq // br
    main = jnp.where(i < nxb, main_ref[...], 0.0)
    head = jnp.where(i == 0, first, jnp.where(i <= nxb, tail_ref[...], 0.0))
    return jnp.concatenate([head, main[:br - N_META]], axis=0)


def _ln_emb_fwd(x, meta, g, b, rows, after=()):
    seq, d = x.shape
    br = ROW_BLOCK
    assert seq % br == 0 and br % N_META == 0 and rows % br == 0

    def body(x_ref, tail_ref, meta_ref, g_ref, b_ref, y_ref, yb_ref):
        z = _padded_block(x_ref, tail_ref, meta_ref[...], seq)
        xhat, _ = _ln_stats(z)
        y = xhat * g_ref[...] + b_ref[...]
        y_ref[...] = y
        yb_ref[...] = y.astype(yb_ref.dtype)

    main, tail = _token_specs(seq, d)
    return _call("ln_emb_fwd", body, (rows // br,),
                 [(x, main), (x, tail), (meta, _whole((N_META, d))), (g.reshape(1, d), _whole((1, d))),
                  (b.reshape(1, d), _whole((1, d)))],
                 [(_sds((rows, d), F32), _rows(br, d)), (_sds((rows, d), MXU_DTYPE), _rows(br, d))],
                 sem=("parallel",), after=after)


def _ln_emb_bwd(x, meta, dh0, g):
    seq, d = x.shape
    br = ROW_BLOCK
    step = br // N_META

    def ln_bwd(z, dy, gv):
        xhat, rstd = _ln_stats(z)
        dyg = dy * gv
        m1 = jnp.mean(dyg, axis=-1, keepdims=True)
        m2 = jnp.mean(dyg * xhat, axis=-1, keepdims=True)
        dz = rstd * (dyg - m1 - xhat * m2)
        return dz, jnp.sum(dy * xhat, axis=0, keepdims=True), jnp.sum(dy, axis=0, keepdims=True)

    def body(x_ref, dh_ref, nxt_ref, meta_ref, top_ref, g_ref, dx_ref, dm_ref, dg_ref, db_ref):
        gv = g_ref[...]
        dy = jnp.concatenate([dh_ref[N_META:, :], nxt_ref[...]], axis=0)
        dz, dg, db = ln_bwd(x_ref[...], dy, gv)
        dx_ref[...] = dz

        @pl.when(pl.program_id(0) == 0)
        def _():
            dzm, dgm, dbm = ln_bwd(meta_ref[...], top_ref[...], gv)
            dm_ref[...] = dzm
            dg_ref[...] = dgm
            db_ref[...] = dbm

        dg_ref[...] += dg
        db_ref[...] += db

    small = _whole((N_META, d))
    return _call("ln_emb_bwd", body, (seq // br,),
                 [(x, _rows(br, d)), (dh0, _rows(br, d)), (dh0, pl.BlockSpec((N_META, d), lambda i: ((i + 1) * step, 0))),
                  (meta, small), (dh0, small), (g.reshape(1, d), _whole((1, d)))],
                 [(_sds((seq, d), F32), _rows(br, d)), (_sds((N_META, d), F32), small),
                  (_sds((1, d), F32), _whole((1, d))), (_sds((1, d), F32), _whole((1, d)))], sem=("arbitrary",))


def _ln_ffn_loss(h1, f, tgt, g, b):
    r, d = h1.shape
    seq = tgt.shape[0]
    br = ROW_BLOCK

    def body(a_ref, r_ref, t_ref, tail_ref, g_ref, b_ref, l_ref):
        err = _loss_err(a_ref, r_ref, t_ref, tail_ref, g_ref, b_ref, seq)[0]

        @pl.when(pl.program_id(0) == 0)
        def _():
            l_ref[...] = jnp.zeros_like(l_ref)

        l_ref[...] += jnp.sum(jnp.sum(err * err, axis=1, keepdims=True), axis=0, keepdims=True) * (0.5 / d)

    main, tail = _token_specs(seq, d)
    return _call("ln_ffn_loss", body, (r // br,),
                 [(h1, _rows(br, d)), (f, _rows(br, d)), (tgt, main), (tgt, tail),
                  (g.reshape(1, d), _whole((1, d))), (b.reshape(1, d), _whole((1, d)))],
                 [(_sds((1, 1), F32), _whole((1, 1)))], sem=("arbitrary",))[0]


def _loss_err(a_ref, r_ref, t_ref, tail_ref, g_ref, b_ref, seq):
    br, d = a_ref.shape
    xhat, rstd = _ln_stats(ALPHA * a_ref[...] + r_ref[...])
    y = xhat * g_ref[...] + b_ref[...]
    t = _padded_block(t_ref, tail_ref, jnp.zeros((N_META, d), F32), seq)
    rid = lax.broadcasted_iota(jnp.int32, (br, d), 0) + pl.program_id(0) * br
    valid = (rid >= N_META) & (rid < N_META + seq)
    return jnp.where(valid, y - t, 0.0), xhat, rstd


def _ln_ffn_bwd(h1, f, tgt, g, b):
    r, d = h1.shape
    seq = tgt.shape[0]
    br = ROW_BLOCK

    def body(a_ref, r_ref, t_ref, tail_ref, g_ref, b_ref, dz_ref, dzb_ref, dg_ref, db_ref):
        err, xhat, rstd = _loss_err(a_ref, r_ref, t_ref, tail_ref, g_ref, b_ref, seq)
        dyv = err * (1.0 / d)
        dyg = dyv * g_ref[...]
        m1 = jnp.mean(dyg, axis=-1, keepdims=True)
        m2 = jnp.mean(dyg * xhat, axis=-1, keepdims=True)
        dz = rstd * (dyg - m1 - xhat * m2)
        dz_ref[...] = dz
        dzb_ref[...] = dz.astype(dzb_ref.dtype)

        @pl.when(pl.program_id(0) == 0)
        def _():
            dg_ref[...] = jnp.zeros_like(dg_ref)
            db_ref[...] = jnp.zeros_like(db_ref)

        dg_ref[...] += jnp.sum(dyv * xhat, axis=0, keepdims=True)
        db_ref[...] += jnp.sum(dyv, axis=0, keepdims=True)

    main, tail = _token_specs(seq, d)
    return _call("ln_ffn_bwd", body, (r // br,),
                 [(h1, _rows(br, d)), (f, _rows(br, d)), (tgt, main), (tgt, tail),
                  (g.reshape(1, d), _whole((1, d))), (b.reshape(1, d), _whole((1, d)))],
                 [(_sds((r, d), F32), _rows(br, d)), (_sds((r, d), MXU_DTYPE), _rows(br, d)),
                  (_sds((1, d), F32), _whole((1, d))), (_sds((1, d), F32), _whole((1, d)))], sem=("arbitrary",))


def _attn_fwd(name, q, k, v, cum_b=None, cum_t=None):
    (qa, qg), (ka, kg), (va, vg) = q, k, v
    r = qa.shape[0]
    tq, tk = ATT_TQ, ATT_TK
    nq, nk = r // tq, r // tk
    bias = cum_b is not None

    def body(*refs):
        if bias:
            q_ref, k_ref, vt_ref, cb_ref, ct_ref, o_ref, ob_ref, lse_ref = refs
        else:
            q_ref, k_ref, vt_ref, o_ref, ob_ref, lse_ref = refs
        i = pl.program_id(1)
        qs = [q_ref[:, _hs(hh)] for hh in range(hg)]
        cqs = [ct_ref[hh] for hh in range(hg)] if bias else None
        diff = lax.broadcasted_iota(jnp.int32, (tk, tq), 0) - lax.broadcasted_iota(jnp.int32, (tk, tq), 1)

        def step(j, carry, masked):
            keys = pl.ds(pl.multiple_of(j * tk, tk), tk)
            out = []
            for hh in range(hg):
                m, l, acc = carry[hh]
                kt = k_ref[keys, _hs(hh)]
                s = lax.dot_general(kt, qs[hh], NT, preferred_element_type=F32)
                if bias:
                    s = s + (cqs[hh] - cb_ref[keys, hh * HP:hh * HP + 1])
                if masked:
                    s = jnp.where(diff <= i * tq - j * tk, s, NEG_INF)
                m_new = jnp.maximum(m, jnp.max(s, axis=0, keepdims=True))
                p = jnp.exp(s - m_new)
                a = jnp.exp(m - m_new)
                l = a * l + jnp.sum(p, axis=0, keepdims=True)
                acc = a * acc + jnp.dot(vt_ref[j, _hs(hh), :], p.astype(kt.dtype), preferred_element_type=F32)
                out.append((m_new, l, acc))
            return tuple(out)

        n_clear = (i * tq + 1) // tk
        n_all = ((i + 1) * tq - 1) // tk + 1
        carry = tuple((jnp.full((1, tq), NEG_INF, F32), jnp.zeros((1, tq), F32), jnp.zeros((HP, tq), F32))
                      for _ in range(hg))
        carry = lax.fori_loop(0, n_clear, lambda j, c: step(j, c, False), carry)
        carry = lax.fori_loop(n_clear, n_all, lambda j, c: step(j, c, True), carry)
        for hh in range(hg):
            m, l, acc = carry[hh]
            o = (acc / l).T
            o_ref[:, _hs(hh)] = o
            ob_ref[:, _hs(hh)] = o.astype(ob_ref.dtype)
            lse_ref[hh] = m + jnp.log(l)

    hg = ATT_HEADS_FWD
    w = hg * HP
    gpw = HW // w
    tile = lambda g: pl.BlockSpec((tq, w), lambda h, i: (i, g * gpw + h))
    res = lambda g: pl.BlockSpec((r, w), lambda h, i: (0, g * gpw + h))
    v_t = _key_tiles_transposed(name + "_vt", va, vg)
    ins = [(qa, tile(qg)), (ka, res(kg)), (v_t, pl.BlockSpec((nk, w, tk), lambda h, i: (0, h, 0)))]
    if bias:
        ins += [(cum_b, res(0)),
                (cum_t.reshape(HEADS, nq, 1, tq), pl.BlockSpec((hg, None, 1, tq), lambda h, i: (h, i, 0, 0)))]
    outs = [(_sds((r, HW), F32), tile(0)), (_sds((r, HW), MXU_DTYPE), tile(0)),
            (_sds((HEADS, nq, 1, tq), F32), pl.BlockSpec((hg, None, 1, tq), lambda h, i: (h, i, 0, 0)))]
    o, ob, lse = _call(name, body, (gpw, nq), ins, outs, sem=("parallel", "parallel"))
    return o, ob, lse.reshape(HEADS, r)


def _key_tiles_transposed(name, a, group):
    r = a.shape[0]
    tk = ATT_TK

    def body(x_ref, o_ref):
        for h in range(HEADS):
            o_ref[_hs(h), :] = x_ref[:, _hs(h)].astype(F32).T.astype(o_ref.dtype)

    return _call(name, body, (r // tk,),
                 [(a, pl.BlockSpec((tk, HW), lambda j: (j, group)))],
                 [(_sds((r // tk, HW, tk), a.dtype), pl.BlockSpec((None, HW, tk), lambda j: (j, 0, 0)))],
                 sem=("parallel",))[0]


def _attn_bwd(name, q, k, v, do_b, o, lse_t, cum_b=None, cum_t=None, out_dtype=F32, after=()):
    (qa, qg), (ka, kg), (va, vg) = q, k, v
    r = qa.shape[0]
    tq, tk = ATT_TQ, ATT_TK
    nq, nk = r // tq, r // tk
    bias = cum_b is not None

    def body(*refs):
        if bias:
            (q_ref, k_ref, v_ref, do_ref, o_ref, lse_ref, cb_ref, ct_ref,
             dq_ref, dk_ref, dv_ref, dcq_ref, dck_ref, dqt_ref, dl_ref) = refs
        else:
            q_ref, k_ref, v_ref, do_ref, o_ref, lse_ref, dq_ref, dk_ref, dv_ref, dqt_ref, dl_ref = refs
        j = pl.program_id(1)

        @pl.when(j == 0)
        def _():
            dqt_ref[...] = jnp.zeros_like(dqt_ref)
            if bias:
                dcq_ref[...] = jnp.zeros_like(dcq_ref)
            for hh in range(hg):
                for i in range(nq):
                    rows = slice(i * tq, (i + 1) * tq)
                    prod = do_ref[rows, _hs(hh)].astype(F32) * o_ref[rows, _hs(hh)]
                    dl_ref[hh, i] = jnp.sum(prod.T, axis=0, keepdims=True)

        kts = [k_ref[:, _hs(hh)] for hh in range(hg)]
        vts = [v_ref[:, _hs(hh)] for hh in range(hg)]
        k_trs = [kt.astype(F32).T.astype(kt.dtype) for kt in kts]
        cks = [cb_ref[:, hh * HP:hh * HP + 1] for hh in range(hg)] if bias else None
        diff = lax.broadcasted_iota(jnp.int32, (tk, tq), 0) - lax.broadcasted_iota(jnp.int32, (tk, tq), 1)

        def step(i, carry, masked):
            rows = pl.ds(pl.multiple_of(i * tq, tq), tq)
            out = []
            for hh in range(hg):
                dk_acc, dv_acc, dck_acc = carry[hh]
                qt = q_ref[rows, _hs(hh)]
                dot = do_ref[rows, _hs(hh)]
                s = lax.dot_general(kts[hh], qt, NT, preferred_element_type=F32)
                if bias:
                    s = s + (ct_ref[hh, i] - cks[hh])
                if masked:
                    s = jnp.where(diff <= i * tq - j * tk, s, NEG_INF)
                p = jnp.exp(s - lse_ref[hh, i])
                dp = lax.dot_general(vts[hh], dot, NT, preferred_element_type=F32)
                ds = p * (dp - dl_ref[hh, i])
                pb = p.astype(dot.dtype)
                dsb = ds.astype(qt.dtype)
                dv_acc = dv_acc + jnp.dot(pb, dot, preferred_element_type=F32)
                dk_acc = dk_acc + jnp.dot(dsb, qt, preferred_element_type=F32)
                dqt_ref[hh, i] += jnp.dot(k_trs[hh], dsb, preferred_element_type=F32)
                if bias:
                    dcq_ref[hh, i] += jnp.sum(ds, axis=0, keepdims=True)
                    dck_acc = dck_acc - jnp.sum(ds, axis=1, keepdims=True)
                out.append((dk_acc, dv_acc, dck_acc))
            return tuple(out)

        i_first = (j * tk) // tq
        i_clear = jnp.minimum(((j + 1) * tk + tq - 2) // tq, nq)
        carry = tuple((jnp.zeros((tk, HP), F32), jnp.zeros((tk, HP), F32), jnp.zeros((tk, 1), F32)) for _ in range(hg))
        carry = lax.fori_loop(i_first, i_clear, lambda i, c: step(i, c, True), carry)
        carry = lax.fori_loop(i_clear, nq, lambda i, c: step(i, c, False), carry)
        for hh in range(hg):
            dk_acc, dv_acc, dck_acc = carry[hh]
            dk_ref[:, _hs(hh)] = dk_acc.astype(dk_ref.dtype)
            dv_ref[:, _hs(hh)] = dv_acc.astype(dv_ref.dtype)
            if bias:
                dck_ref[:, _hs(hh)] = jnp.broadcast_to(dck_acc, (tk, HP))

        @pl.when(j == nk - 1)
        def _():
            for hh in range(hg):
                for i in range(nq):
                    dq_ref[i * tq:(i + 1) * tq, _hs(hh)] = dqt_ref[hh, i].T.astype(dq_ref.dtype)

    hg = ATT_HEADS
    w = hg * HP
    gpw = HW // w
    res = lambda g: pl.BlockSpec((r, w), lambda h, j: (0, g * gpw + h))
    tile = lambda g: pl.BlockSpec((tk, w), lambda h, j: (j, g * gpw + h))
    rowv = pl.BlockSpec((hg, nq, 1, tq), lambda h, j: (h, 0, 0, 0))
    as_rows = lambda a: a.reshape(HEADS, nq, 1, tq)
    ins = [(qa, res(qg)), (ka, tile(kg)), (va, tile(vg)), (do_b, res(0)), (o, res(0)), (as_rows(lse_t), rowv)]
    outs = [(_sds((r, HW), out_dtype), res(0)), (_sds((r, HW), out_dtype), tile(0)), (_sds((r, HW), out_dtype), tile(0))]
    if bias:
        ins += [(cum_b, tile(0)), (as_rows(cum_t), rowv)]
        outs += [(_sds((HEADS, nq, 1, tq), F32), rowv), (_sds((r, HW), F32), tile(0))]
    res_out = _call(name, body, (gpw, nk), ins, outs,
                    scratch=[pltpu.VMEM((hg, nq, HP, tq), F32), pltpu.VMEM((hg, nq, 1, tq), F32)],
                    sem=("parallel", "arbitrary"), after=after)
    if bias:
        dq, dk, dv, dcq, dck = res_out
        return dq, dk, dv, dcq.reshape(HEADS, r), dck
    return res_out


MESH_ID = pl.DeviceIdType.MESH
ANY = pl.BlockSpec(memory_space=pl.ANY)


N_GATHER_COPIES = 8


def _allgather(name, shards):
    n = len(shards)

    def body(*refs):
        x_refs, out_refs = refs[:n], refs[n:2 * n]
        send_sems, recv_sems, local_sems = refs[2 * n:]
        x, y, c = lax.axis_index("x"), lax.axis_index("y"), lax.axis_index("c")
        me, sibling = (x, y, c), (x, y, 1 - c)
        xn, yn, dg = (1 - x, y, c), (x, 1 - y, c), (1 - x, 1 - y, c)
        other = lambda dev: (dev[0], dev[1], 1 - c)

        def slot(ti, dev, half=None):
            ref = out_refs[ti].at[4 * dev[0] + 2 * dev[1] + dev[2]]
            if half is None:
                return ref
            rows = shards[ti].shape[0] // 2
            return ref.at[pl.ds(half * rows, rows)]

        def copy(ti, k, block, to, half=None, src=None):
            return pltpu.make_async_remote_copy(
                src_ref=slot(ti, block, half) if src is None else src, dst_ref=slot(ti, block, half),
                send_sem=send_sems.at[ti, k], recv_sem=recv_sems.at[ti, k], device_id=to, device_id_type=MESH_ID)

        mine = [pltpu.make_async_copy(x_refs[ti], slot(ti, me), local_sems.at[ti]) for ti in range(n)]
        for cp in mine:
            cp.start()
        started = []

        def go(cp):
            cp.start()
            started.append(cp)

        for ti in range(n):
            go(copy(ti, 0, me, sibling, src=x_refs[ti]))
            go(copy(ti, 1, me, xn, src=x_refs[ti]))
            go(copy(ti, 2, me, yn, src=x_refs[ti]))
        for ti in range(n):
            copy(ti, 1, xn, me).wait_recv()
            go(copy(ti, 3, xn, yn, half=0))
            go(copy(ti, 5, xn, sibling))
            copy(ti, 2, yn, me).wait_recv()
            go(copy(ti, 4, yn, xn, half=1))
            go(copy(ti, 6, yn, sibling))
        for ti in range(n):
            copy(ti, 3, dg, me, half=0).wait_recv()
            copy(ti, 4, dg, me, half=1).wait_recv()
            go(copy(ti, 7, dg, sibling))
        for ti in range(n):
            copy(ti, 0, sibling, me).wait_recv()
            for k, dev in ((5, xn), (6, yn), (7, dg)):
                copy(ti, k, other(dev), me).wait_recv()
        for cp in started:
            cp.wait_send()
        for cp in mine:
            cp.wait()

    sems = pltpu.SemaphoreType.DMA((n, N_GATHER_COPIES))
    return pl.pallas_call(
        body, name=name, out_shape=[_sds((N_DEV,) + s.shape, s.dtype) for s in shards],
        in_specs=[ANY] * n, out_specs=[ANY] * n,
        scratch_shapes=[sems, sems, pltpu.SemaphoreType.DMA((n,))],
    )(*shards)


HBM = pl.BlockSpec(memory_space=pltpu.HBM)
SEM = pl.BlockSpec(memory_space=pltpu.SEMAPHORE)
EFFECT = pltpu.SideEffectType.DATAFLOW_SIDE_EFFECTING
N_PEER = N_DEV - 1


def _my_id():
    return 4 * lax.axis_index("x") + 2 * lax.axis_index("y") + lax.axis_index("c")


def _peers():
    x, y, c = lax.axis_index("x"), lax.axis_index("y"), lax.axis_index("c")
    out = []
    for k in range(1, N_DEV):
        px, py, pc = (1 - x if k & 4 else x, 1 - y if k & 2 else y, 1 - c if k & 1 else c)
        out.append(((px, py, pc), 4 * px + 2 * py + pc))
    return out


def _push_copies(src_refs, land_refs, send_sems, recv_sems, scatter, landing):
    me = _my_id()
    out = []
    for ti, (src, land) in enumerate(zip(src_refs, land_refs)):
        for k, (dev, pid) in enumerate(_peers()):
            out.append(pltpu.make_async_remote_copy(
                src_ref=src.at[pid] if scatter else src, dst_ref=land.at[pid if landing else me],
                send_sem=send_sems.at[ti * N_PEER + k], recv_sem=recv_sems.at[ti * N_PEER + k],
                device_id=dev, device_id_type=MESH_ID))
    return out


def _push_start(name, groups, scatter, after=None):
    sizes = [len(g) for g in groups]
    srcs = [a for g in groups for a in g]
    n = len(srcs)
    slot = lambda s: s.shape[1:] if scatter else s.shape
    lands = [lax.empty((N_DEV,) + slot(s), s.dtype) for s in srcs]
    n_after = 0 if after is None else 1
    n_grp = len(groups)

    def body(*refs):
        src_refs, land_refs = refs[:n], refs[n:2 * n]
        sems = refs[2 * n + n_after:2 * n + n_after + 2 * n_grp]
        token = refs[-1]
        lo = 0
        for gi, sz in enumerate(sizes):
            for cp in _push_copies(src_refs[lo:lo + sz], land_refs[lo:lo + sz], sems[2 * gi], sems[2 * gi + 1], scatter, False):
                cp.start()
            lo += sz
        token[...] = jnp.zeros_like(token)

    hbm = lambda a: pltpu.with_memory_space_constraint(a, pltpu.HBM)
    operands = [hbm(a) for a in srcs + lands] + ([after] if n_after else [])
    sem_shapes = [pltpu.SemaphoreType.DMA((sz * N_PEER,)) for sz in sizes for _ in range(2)]
    res = pl.pallas_call(
        body, name=name,
        out_shape=sem_shapes + [pltpu.HBM(a.shape, a.dtype) for a in srcs + lands] + [_sds((8, 128), F32)],
        in_specs=[HBM] * (2 * n) + [ANY] * n_after,
        out_specs=[SEM] * (2 * n_grp) + [HBM] * (2 * n) + [pl.BlockSpec(memory_space=pltpu.VMEM)],
        input_output_aliases={i: 2 * n_grp + i for i in range(2 * n)},
        compiler_params=pltpu.CompilerParams(has_side_effects=EFFECT),
    )(*operands)
    thru = res[2 * n_grp:2 * n_grp + 2 * n]
    handles, lo = [], 0
    for gi, sz in enumerate(sizes):
        handles.append((res[2 * gi], res[2 * gi + 1], list(thru[lo:lo + sz]), list(thru[n + lo:n + lo + sz]), scatter))
        lo += sz
    return handles, res[-1]


def _push_wait(name, handle, after):
    send_sems, recv_sems, srcs, lands, scatter = handle
    n = len(srcs)

    def body(*refs):
        src_refs, land_refs = refs[:n], refs[n:2 * n]
        s_sems, r_sems = refs[2 * n], refs[2 * n + 1]
        for cp in _push_copies(src_refs, land_refs, s_sems, r_sems, scatter, True):
            cp.wait_send()
            cp.wait_recv()

    res = pl.pallas_call(
        body, name=name,
        out_shape=[pltpu.HBM(a.shape, a.dtype) for a in srcs + lands],
        in_specs=[HBM] * (2 * n) + [SEM, SEM, ANY], out_specs=[HBM] * (2 * n),
        input_output_aliases={i: i for i in range(2 * n)},
        compiler_params=pltpu.CompilerParams(has_side_effects=EFFECT),
    )(*srcs, *lands, send_sems, recv_sems, after)
    return list(res[n:])


def _adamw(name, parts, w, m, v, own=None):
    r, c = w.shape
    br = _pick(r, 256, 16)
    has_own = own is not None

    def body(*refs):
        if has_own:
            p_ref, own_ref, w_ref, m_ref, v_ref, g_ref, d_ref, nm_ref, nv_ref = refs
            me = _my_id()
            mine = own_ref[...].astype(F32)
        else:
            p_ref, w_ref, m_ref, v_ref, g_ref, d_ref, nm_ref, nv_ref = refs
        g = None
        for k in range(N_DEV):
            t = p_ref[k].astype(F32)
            if has_own:
                t = jnp.where(me == k, mine, t)
            g = t if g is None else g + t
        mm = ADAM_B1 * m_ref[...] + (1.0 - ADAM_B1) * g
        vv = ADAM_B2 * v_ref[...] + (1.0 - ADAM_B2) * (g * g)
        m_hat = mm / (1.0 - ADAM_B1 ** ADAM_STEP)
        v_hat = vv / (1.0 - ADAM_B2 ** ADAM_STEP)
        g_ref[...] = g
        d_ref[...] = -ADAM_LR * (m_hat / (jnp.sqrt(v_hat) + ADAM_EPS) + ADAM_WD * w_ref[...])
        nm_ref[...] = mm
        nv_ref[...] = vv

    spec = _rows(br, c)
    out = (_sds((r, c), F32), spec)
    ins = [(parts, pl.BlockSpec((N_DEV, br, c), lambda i: (0, i, 0)))] + ([(own, spec)] if has_own else [])
    return _call(name, body, (r // br,), ins + [(w, spec), (m, spec), (v, spec)], [out] * 4, sem=("parallel",))


def _pad_head_cols(w, d):
    k = w.shape[0]
    return jnp.pad(w.reshape(k, HEADS, d), ((0, 0), (0, 0), (0, HP - d))).reshape(k, HW)


def _unpad_head_cols(wp, d):
    k = wp.shape[0]
    return wp.reshape(k, HEADS, HP)[:, :, :d].reshape(k, HEADS * d)


def _pad_head_rows(w, d):
    n = w.shape[1]
    return jnp.pad(w.reshape(HEADS, d, n), ((0, 0), (0, HP - d), (0, 0))).reshape(HW, n)


def _unpad_head_rows(wp, d):
    n = wp.shape[1]
    return wp.reshape(HEADS, HP, n)[:, :d, :].reshape(HEADS * d, n)


def _w_in_runs():
    nat = {}
    o = 0
    for nm, wd in (("q", Q_RANK), ("kv", KV_RANK), ("kr", ROPE), ("fq", FOX_W), ("fk", FOX_W), ("fv", FOX_W),
                   ("fl", HEADS), ("gate", 2 * D_MODEL)):
        nat[nm] = o
        o += wd
    runs = [(1, R_QLAT, nat["q"], Q_RANK, 1.0), (1, R_KVLAT, nat["kv"], KV_RANK, 1.0),
            (1, R_LAST + LANE_FL, nat["fl"], HEADS, 1.0), (1, R_LAST + LANE_PE, nat["kr"], ROPE, 1.0),
            (1, R_GATE, nat["gate"], 2 * D_MODEL, 1.0)]
    for grp, (nm, sc) in enumerate((("fq", FOX_SCALE), ("fk", 1.0), ("fv", 1.0))):
        runs.append((0, grp * FOX_W, nat[nm], FOX_W, sc))
    return runs


def _head_pad_moves(pad):
    moves = []
    for grp in range(3):
        for h in range(HEADS):
            narrow, wide = grp * FOX_W + h * FOX_DIM, h * HP
            if pad:
                moves.append((0, None, grp * HW + wide, 0, None, narrow, FOX_DIM, 1.0))
            else:
                moves.append((0, None, narrow, grp, None, wide, FOX_DIM, 1.0))
    return moves


def _sharded_runs(runs, shard_cols):
    out = []
    for half, col, ncol, width, sc in runs:
        while width > 0:
            d, local = divmod(ncol, shard_cols)
            wd = min(width, shard_cols - local)
            out.append((half, col, d, local, wd, sc))
            col, ncol, width = col + wd, ncol + wd, width - wd
    return out


def _remap(name, srcs, out_shapes, moves):
    rows = srcs[0].shape[-2]
    br = _pick(rows, 256, 16)
    ns = len(srcs)

    def spec(shape):
        if len(shape) == 2:
            return pl.BlockSpec((br, shape[1]), lambda i: (i, 0))
        return pl.BlockSpec((shape[0], br, shape[2]), lambda i: (0, i, 0))

    covered = [sum(m[6] for m in moves if m[0] == di) for di in range(len(out_shapes))]
    has_gaps = [cov < (shape[1] if len(shape) == 2 else shape[0] * shape[2])
                for cov, (shape, _) in zip(covered, out_shapes)]

    def body(*refs):
        s_refs, o_refs = refs[:ns], refs[ns:]
        for o, gaps in zip(o_refs, has_gaps):
            if gaps:
                o[...] = jnp.zeros_like(o)
        for di, dl, dc, si, sl, sc0, wd, scale in moves:
            v = s_refs[si][:, sc0:sc0 + wd] if sl is None else s_refs[si][sl, :, sc0:sc0 + wd]
            if scale != 1.0:
                v = v * jnp.asarray(scale, v.dtype)
            v = v.astype(o_refs[di].dtype)
            if dl is None:
                o_refs[di][:, dc:dc + wd] = v
            else:
                o_refs[di][dl, :, dc:dc + wd] = v

    return _call(name, body, (rows // br,), [(a, spec(a.shape)) for a in srcs],
                 [(_sds(shape, dt), spec(shape)) for shape, dt in out_shapes], sem=("parallel",))


def _w_in_from_shards(g3):
    n, rows, c = g3.shape
    moves = [(half, None, col, 0, d, local, wd, sc) for half, col, d, local, wd, sc in _sharded_runs(_w_in_runs(), c)]
    return _remap("w_in_repack", [g3], [((rows, F_W), g3.dtype), ((rows, R_W), g3.dtype)], moves)


def _w_in_grad_to_shards(d_fused, d_rest, n, c):
    rows = d_fused.shape[0]
    moves = [(0, d, local, half, None, col, wd, sc) for half, col, d, local, wd, sc in _sharded_runs(_w_in_runs(), c)]
    return _remap("w_in_grad_unpack", [d_fused, d_rest], [((n, rows, c), d_fused.dtype)], moves)[0]


def _rows_from_shards(name, land, own):
    n, rows, c = land.shape

    def body(land_ref, own_ref, o_ref):
        o_ref[...] = jnp.where(_my_id() == pl.program_id(0), own_ref[...], land_ref[...])

    return _call(name, body, (n,),
                 [(land, pl.BlockSpec((None, rows, c), lambda d: (d, 0, 0))), (own, _whole((rows, c)))],
                 [(_sds((n * rows, c), land.dtype), pl.BlockSpec((rows, c), lambda d: (d, 0)))], sem=("parallel",))[0]


def _cols_from_shards(name, land, own):
    n, rows, c = land.shape
    br = _pick(rows, 256, 16)

    def body(land_ref, own_ref, o_ref):
        me = _my_id()
        for d in range(n):
            o_ref[:, c * d:c * (d + 1)] = jnp.where(me == d, own_ref[...], land_ref[d])

    return _call(name, body, (rows // br,),
                 [(land, pl.BlockSpec((n, br, c), lambda i: (0, i, 0))), (own, _rows(br, c))],
                 [(_sds((rows, n * c), land.dtype), _rows(br, n * c))], sem=("parallel",))[0]


def _cols_to_shards(name, full, n):
    rows, nc = full.shape
    c = nc // n
    return _remap(name, [full], [((n, rows, c), full.dtype)], [(0, d, 0, 0, None, c * d, c, 1.0) for d in range(n)])[0]


def _split_w_kv(w):
    k = w.shape[0]
    w3 = w.reshape(k, HEADS, NOPE + V_DIM)
    padl = lambda a: jnp.pad(a, ((0, 0), (0, 0), (0, HP - a.shape[-1]))).reshape(k, HW)
    return padl(w3[..., :NOPE]), padl(w3[..., NOPE:])


def _merge_w_kv(wk, wv):
    k = wk.shape[0]
    return jnp.concatenate([wk.reshape(k, HEADS, HP)[..., :NOPE], wv.reshape(k, HEADS, HP)[..., :V_DIM]],
                           axis=-1).reshape(k, HEADS * (NOPE + V_DIM))


class _NoComm:
    first_token = ()

    def late_weights(self, group, after):
        return {}

    def send(self, name, grads):
        return ()


def _local_step(x, tgt, p, comm=_NoComm()):
    seq = x.shape[0]
    r = -(-(N_META + seq) // ROW_ALIGN) * ROW_ALIGN
    cd = MXU_DTYPE
    p = dict(p)

    w_f, w_r = p["w_in"]

    pos = jnp.arange(r, dtype=F32)
    inv_freq = ROPE_THETA ** (-jnp.arange(HALF, dtype=F32) / HALF)
    ang = pos[:, None] * inv_freq[None, :]
    cos_t = jnp.tile(jnp.cos(ang), (1, HP // HALF))
    sin_t = jnp.tile(jnp.sin(ang), (1, HP // HALF))
    bf_row = jnp.zeros((1, HP), F32).at[0, LANE_FL:LANE_FL + HEADS].set(p["b_forget"])

    h0, h0b = _ln_emb_fwd(x, p["meta_tokens"], p["ln_emb_g"], p["ln_emb_b"], r, after=comm.first_token)
    proj_f = _matmul("in_proj_f", h0b, w_f, out_dtype=cd)
    proj_f = _remap("proj_f_pad", [proj_f], [((r, 3 * HW), cd)], _head_pad_moves(True))[0]
    proj_r = _matmul("in_proj_r", h0b, w_r)
    latent_gains = (p["q_norm_g"], p["kv_norm_g"])
    ql, kvl = _latent_norm_fwd(proj_r, latent_gains)
    p.update(comm.late_weights("qkv", ql))
    w_q = _pad_head_cols(p["w_q_up"], QK_DIM)
    w_kv = jnp.concatenate(_split_w_kv(p["w_kv_up"]), axis=1)
    q_raw = _matmul("q_up", ql, w_q)
    kv = _matmul("kv_up", kvl, w_kv, out_dtype=cd)
    q_mla, k_mla = _rope_fwd(q_raw, kv, proj_r, cos_t, sin_t)
    o_mla, o_mla_b, lse_mla = _attn_fwd("mla_fwd", (q_mla, 0), (k_mla, 0), (kv, 1))

    cum, cum_t = _forget_fwd(proj_r, bf_row)
    o_fox, o_fox_b, lse_fox = _attn_fwd("fox_fwd", (proj_f, 0), (proj_f, 1), (proj_f, 2), cum, cum_t)

    p.update(comm.late_weights("mix", o_fox_b))
    w_bm = _pad_head_rows(p["w_branch_mla"], V_DIM)
    w_bf = _pad_head_rows(p["w_branch_fox"], FOX_DIM)
    bm = _matmul("branch_mla", o_mla_b, w_bm, out_dtype=cd)
    bfx = _matmul("branch_fox", o_fox_b, w_bf, out_dtype=cd)
    merged = _gate_fwd(proj_r, p["b_gate"], bm, bfx)
    mix = _matmul("out_proj", merged, p["w_out"])
    h1, h1b = _ln_fwd("ln_mix_fwd", h0, mix, p["ln_mix_g"], p["ln_mix_b"])
    p.update(comm.late_weights("ffn", h1b))
    up = _matmul("ffn_up", h1b, p["w_ffn_up"], out_dtype=cd)
    act = _glu_fwd(up, p["conv_w"], p["conv_b"])
    f = _matmul("ffn_down", act, p["w_ffn_down"])
    loss = _ln_ffn_loss(h1, f, tgt, p["ln_ffn_g"], p["ln_ffn_b"])

    g = {}
    dz2, dz2b, g["ln_ffn_g"], g["ln_ffn_b"] = _ln_ffn_bwd(h1, f, tgt, p["ln_ffn_g"], p["ln_ffn_b"])
    d_act = _matmul("ffn_down_dx", dz2b, p["w_ffn_down"], tb=True, out_dtype=cd)
    g["w_ffn_down"] = _matmul("ffn_down_dw", act, dz2b, ta=True, out_dtype=cd)
    d_up, dcw, g["conv_b"] = _glu_bwd(up, p["conv_w"], p["conv_b"], d_act)
    g["conv_w"] = dcw[:3]
    dh1 = _matmul("ffn_up_dx", d_up, p["w_ffn_up"], tb=True, addend=dz2, alpha=ALPHA)
    g["w_ffn_up"] = _matmul("ffn_up_dw", h1b, d_up, ta=True, out_dtype=cd)
    sent = comm.send("ffn", {n: g[n] for n in ("w_ffn_down", "w_ffn_up", "conv_w")})
    dz1, dz1b, g["ln_mix_g"], g["ln_mix_b"] = _ln_bwd("ln_mix_bwd", h0, mix, dh1, p["ln_mix_g"], after=sent)
    dmerged = _matmul("out_proj_dx", dz1b, p["w_out"], tb=True, out_dtype=cd)
    g["w_out"] = _matmul("out_proj_dw", merged, dz1b, ta=True, out_dtype=cd)
    d_bm, d_bf, d_gl, g["b_gate"] = _gate_bwd(proj_r, p["b_gate"], bm, bfx, dmerged)
    do_mla_b = _matmul("branch_mla_dx", d_bm, w_bm, tb=True, out_dtype=cd)
    g["w_branch_mla"] = _unpad_head_rows(_matmul("branch_mla_dw", o_mla_b, d_bm, ta=True, out_dtype=cd), V_DIM)
    do_fox_b = _matmul("branch_fox_dx", d_bf, w_bf, tb=True, out_dtype=cd)
    g["w_branch_fox"] = _unpad_head_rows(_matmul("branch_fox_dw", o_fox_b, d_bf, ta=True, out_dtype=cd), FOX_DIM)

    sent = comm.send("mix", {n: g[n] for n in ("w_out", "w_branch_mla", "w_branch_fox")})
    dq_m, dk_m, dv_m = _attn_bwd("mla_bwd", (q_mla, 0), (k_mla, 0), (kv, 1), do_mla_b, o_mla, lse_mla, after=sent)
    dfq, dfk, dfv, dcq, dck = _attn_bwd("fox_bwd", (proj_f, 0), (proj_f, 1), (proj_f, 2), do_fox_b, o_fox, lse_fox,
                                        cum, cum_t, out_dtype=cd)
    dfl, dbf = _forget_bwd(proj_r, bf_row, dcq, dck)
    g["b_forget"] = dbf[:, LANE_FL:LANE_FL + HEADS]

    dq_b, dkv_b, dlast = _rope_bwd(dq_m, dk_m, dv_m, dfl, cos_t, sin_t)
    d_ql = _matmul("q_up_dx", dq_b, w_q, tb=True)
    d_kvl = _matmul("kv_up_dx", dkv_b, w_kv, tb=True)
    d_qlat, d_kvlat, g["q_norm_g"], g["kv_norm_g"] = _latent_norm_bwd(proj_r, (d_ql, d_kvl), latent_gains)
    side_by_side = lambda parts, cols: [(0, None, c0, si, None, 0, a.shape[1], 1.0) for si, (a, c0) in enumerate(zip(parts, cols))]
    dproj_f = _remap("dproj_f_pack", [dfq, dfk, dfv], [((r, F_W), cd)], _head_pad_moves(False))[0]
    rest_parts = [d_qlat, d_kvlat, dlast, d_gl]
    dproj_r = _remap("dproj_r_pack", rest_parts, [((r, R_W), cd)],
                     side_by_side(rest_parts, (R_QLAT, R_KVLAT, R_LAST, R_GATE)))[0]
    g["w_in"] = (_matmul("in_proj_f_dw", h0b, dproj_f, ta=True, out_dtype=cd),
                 _matmul("in_proj_r_dw", h0b, dproj_r, ta=True, out_dtype=cd))
    sent = comm.send("in", {"w_in": g["w_in"]})
    dh0 = _matmul("in_proj_f_dx", dproj_f, w_f, tb=True, addend=dz1, alpha=ALPHA, after=sent)
    g["w_q_up"] = _unpad_head_cols(_matmul("q_up_dw", ql, dq_b, ta=True, out_dtype=cd, after=sent), QK_DIM)
    dw_kv = _matmul("kv_up_dw", kvl, dkv_b, ta=True, out_dtype=cd, after=sent)
    g["w_kv_up"] = _merge_w_kv(dw_kv[:, :HW], dw_kv[:, HW:])
    sent = comm.send("qkv", {n: g[n] for n in ("w_q_up", "w_kv_up")})
    dh0 = _matmul("in_proj_r_dx", dproj_r, w_r, tb=True, addend=dh0, after=sent)
    grad_x, d_meta, g["ln_emb_g"], g["ln_emb_b"] = _ln_emb_bwd(x, p["meta_tokens"], dh0, p["ln_emb_g"])
    return loss, grad_x, d_meta, g


BIG = (("w_in", 1), ("w_q_up", 1), ("w_kv_up", 1), ("w_branch_mla", 1), ("w_branch_fox", 1), ("w_out", 0),
       ("w_ffn_up", 1), ("w_ffn_down", 0))
SMALL_SHARDED = (("meta_tokens", 1), ("conv_w", 1))
EARLY = ("w_in", "meta_tokens")
LATE = {"qkv": ("w_q_up", "w_kv_up", "conv_w"),
        "mix": ("w_branch_mla", "w_branch_fox", "w_out"),
        "ffn": ("w_ffn_up", "w_ffn_down")}
REPLICATED = ("ln_emb_g", "ln_emb_b", "b_gate", "b_forget", "q_norm_g", "kv_norm_g", "ln_mix_g", "ln_mix_b",
              "conv_b", "ln_ffn_g", "ln_ffn_b")
PACK_COLS = 1024


def _pack(flat_list):
    cat = jnp.concatenate(flat_list)
    n = cat.shape[0]
    rows = -(-n // (8 * PACK_COLS)) * 8
    return jnp.pad(cat, (0, rows * PACK_COLS - n)).reshape(rows, PACK_COLS)


def _gathered_full(g3, axis):
    n, r, c = g3.shape
    if axis == 0:
        return g3.reshape(n * r, c)
    return g3.transpose(1, 0, 2).reshape(r, n * c)


def _shard_major(full, axis):
    r, c = full.shape
    if axis == 0:
        return full.reshape(N_DEV, r // N_DEV, c)
    return full.reshape(r, N_DEV, c // N_DEV).transpose(1, 0, 2)


def kernel(x, meta_tokens, ln_emb_g, ln_emb_b, w_in, b_gate, b_forget, q_norm_g, w_q_up, kv_norm_g, w_kv_up, w_branch_mla, w_branch_fox, w_out, ln_mix_g, ln_mix_b, w_ffn_up, conv_w, conv_b, w_ffn_down, ln_ffn_g, ln_ffn_b, loss_target, m_meta_tokens, m_ln_emb_g, m_ln_emb_b, m_w_in, m_b_gate, m_b_forget, m_q_norm_g, m_w_q_up, m_kv_norm_g, m_w_kv_up, m_w_branch_mla, m_w_branch_fox, m_w_out, m_ln_mix_g, m_ln_mix_b, m_w_ffn_up, m_conv_w, m_conv_b, m_w_ffn_down, m_ln_ffn_g, m_ln_ffn_b, v_meta_tokens, v_ln_emb_g, v_ln_emb_b, v_w_in, v_b_gate, v_b_forget, v_q_norm_g, v_w_q_up, v_kv_norm_g, v_w_kv_up, v_w_branch_mla, v_w_branch_fox, v_w_out, v_ln_mix_g, v_ln_mix_b, v_w_ffn_up, v_conv_w, v_conv_b, v_w_ffn_down, v_ln_ffn_g, v_ln_ffn_b):
    names = ("meta_tokens", "ln_emb_g", "ln_emb_b", "w_in", "b_gate", "b_forget", "q_norm_g", "w_q_up", "kv_norm_g",
             "w_kv_up", "w_branch_mla", "w_branch_fox", "w_out", "ln_mix_g", "ln_mix_b", "w_ffn_up", "conv_w", "conv_b",
             "w_ffn_down", "ln_ffn_g", "ln_ffn_b")
    w_args = (meta_tokens, ln_emb_g, ln_emb_b, w_in, b_gate, b_forget, q_norm_g, w_q_up, kv_norm_g, w_kv_up,
              w_branch_mla, w_branch_fox, w_out, ln_mix_g, ln_mix_b, w_ffn_up, conv_w, conv_b, w_ffn_down, ln_ffn_g, ln_ffn_b)
    m_args = (m_meta_tokens, m_ln_emb_g, m_ln_emb_b, m_w_in, m_b_gate, m_b_forget, m_q_norm_g, m_w_q_up, m_kv_norm_g,
              m_w_kv_up, m_w_branch_mla, m_w_branch_fox, m_w_out, m_ln_mix_g, m_ln_mix_b, m_w_ffn_up, m_conv_w, m_conv_b,
              m_w_ffn_down, m_ln_ffn_g, m_ln_ffn_b)
    v_args = (v_meta_tokens, v_ln_emb_g, v_ln_emb_b, v_w_in, v_b_gate, v_b_forget, v_q_norm_g, v_w_q_up, v_kv_norm_g,
              v_w_kv_up, v_w_branch_mla, v_w_branch_fox, v_w_out, v_ln_mix_g, v_ln_mix_b, v_w_ffn_up, v_conv_w, v_conv_b,
              v_w_ffn_down, v_ln_ffn_g, v_ln_ffn_b)
    as2d = lambda a: a.reshape((-1, a.shape[-1])) if a.ndim != 1 else a.reshape(1, -1)
    w = {n: as2d(a) for n, a in zip(names, w_args)}
    m = {n: as2d(a) for n, a in zip(names, m_args)}
    v = {n: as2d(a) for n, a in zip(names, v_args)}
    out_shape = {n: a.shape for n, a in zip(names, w_args)}

    axis_of = dict(BIG + SMALL_SHARDED)
    big = set(n for n, _ in BIG)
    wire = lambda n, a: a.astype(MXU_DTYPE) if n in big else a
    my_id = _my_id()

    early = _allgather("gather_early", [wire(n, w[n]) for n in EARLY])
    p = {n: _gathered_full(g3, axis_of[n]) for n, g3 in zip(EARLY, early) if n != "w_in"}
    p["w_in"] = _w_in_from_shards(early[EARLY.index("w_in")])
    for n in REPLICATED:
        p[n] = w[n].reshape(-1)
    late_src = [[wire(n, w[n]) for n in members] for members in LATE.values()]
    late_handles, late_token = _push_start("gather_late_start", late_src, False, after=early[0])
    late = {group: (members, src, handle)
            for (group, members), src, handle in zip(LATE.items(), late_src, late_handles)}
    sent = {}

    class Comm:
        first_token = (late_token,)

        def late_weights(self, group, after):
            members, src, handle = late[group]
            lands = _push_wait("gather_" + group + "_wait", handle, after)
            out = {}
            for n, own, land in zip(members, src, lands):
                if own.shape[0] % 16:
                    out[n] = _gathered_full(lax.dynamic_update_index_in_dim(land, own, my_id, 0), axis_of[n])
                elif axis_of[n] == 1:
                    out[n] = _cols_from_shards(n + "_repack", land, own)
                else:
                    out[n] = _rows_from_shards(n + "_repack", land, own)
            return out

        def send(self, name, grads):
            names_ = tuple(grads)
            parts = []
            for n in names_:
                if n == "w_in":
                    parts.append(_w_in_grad_to_shards(*grads[n], N_DEV, w[n].shape[1]))
                elif n == "w_ffn_up":
                    parts.append(_cols_to_shards(n + "_grad_unpack", grads[n], N_DEV))
                else:
                    parts.append(_shard_major(grads[n], axis_of[n]).astype(MXU_DTYPE))
            (handle,), token = _push_start("send_" + name + "_start", [parts], True)
            sent[name] = (names_, parts, handle)
            return (token,)

    loss_part, grad_x, d_meta, g = _local_step(x[0], loss_target[0], p, Comm())
    grad_x = grad_x[None]

    small = _pack([d_meta.reshape(-1)] + [g[n].reshape(-1) for n in REPLICATED] + [loss_part.reshape(-1)])
    (small_handle,), small_token = _push_start("send_small_start", [[small]], False)

    res = {}
    prev = small_token
    for name, (names_, parts, handle) in sent.items():
        lands = _push_wait("send_" + name + "_wait", handle, prev)
        for n, part, land in zip(names_, parts, lands):
            own = lax.dynamic_index_in_dim(part, my_id, axis=0, keepdims=False)
            res[n] = _adamw("adamw_" + n, land, w[n], m[n], v[n], own=own)
            prev = res[n][0]
    small_all = _push_wait("send_small_wait", small_handle, prev)[0]
    head = jnp.zeros((d_meta.size,), F32)
    rep_w = _pack([head] + [w[n].reshape(-1) for n in REPLICATED])
    rep_m = _pack([head] + [m[n].reshape(-1) for n in REPLICATED])
    rep_v = _pack([head] + [v[n].reshape(-1) for n in REPLICATED])
    rep_res = _adamw("adamw_replicated", small_all, rep_w, rep_m, rep_v, own=small)
    off = d_meta.size
    for n in REPLICATED:
        sz = w[n].size
        res[n] = tuple(a.reshape(-1)[off:off + sz] for a in rep_res)
        off += sz
    loss = rep_res[0].reshape(-1)[off]
    cols = w["meta_tokens"].shape[1]
    meta_rows = lambda a: a.reshape(a.shape[:-2] + (-1,))[..., :d_meta.size].reshape(a.shape[:-2] + d_meta.shape)
    my_cols = lambda a: lax.dynamic_slice_in_dim(a, my_id * cols, cols, axis=a.ndim - 1)
    res["meta_tokens"] = _adamw("adamw_meta_tokens", my_cols(meta_rows(small_all)), w["meta_tokens"],
                                m["meta_tokens"], v["meta_tokens"], own=my_cols(d_meta))

    outs = [loss, grad_x]
    for idx in range(4):
        outs += [res[n][idx].reshape(out_shape[n]) for n in names]
    return tuple(outs)
```

```python
import jax
import jax.numpy as jnp
from jax import lax
from jax.experimental import pallas as pl
from jax.experimental.pallas import tpu as pltpu

F32 = jnp.float32
BF16 = jnp.bfloat16
MXU_DTYPE = BF16

N_DEV = 8
N_META = 16
D_MODEL = 1024
HEADS = 8
Q_RANK = 384
KV_RANK = 128
NOPE = 64
ROPE = 32
HALF = ROPE // 2
QK_DIM = NOPE + ROPE
V_DIM = 64
FOX_DIM = 64
FOX_W = HEADS * FOX_DIM
D_FF = 2816
ROPE_THETA = 10000.0
LN_EPS = 1e-5
RMS_EPS = 1e-6
ALPHA = 2.0 ** 0.25
MLA_SCALE = QK_DIM ** -0.5
FOX_SCALE = FOX_DIM ** -0.5
NEG_INF = -1e30

HP = 128
HW = HEADS * HP
F_W = 3 * FOX_W
R_GATE = 0
R_KVLAT = R_GATE + 2 * D_MODEL
R_LAST = R_KVLAT + KV_RANK
R_QLAT = R_LAST + HP
R_W = R_QLAT + Q_RANK
assert R_QLAT % Q_RANK == 0 and R_KVLAT % KV_RANK == 0 and R_GATE % D_MODEL == 0 and R_W % HP == 0
LANE_FL = 0
LANE_PE = NOPE

ADAM_LR = 0.001
ADAM_B1 = 0.9
ADAM_B2 = 0.999
ADAM_EPS = 1e-08
ADAM_WD = 0.01
ADAM_STEP = 10

ROW_BLOCK = 256
ATT_TQ = 768
ATT_TK = 768
ATT_HEADS = 2
ATT_HEADS_FWD = 4
ROW_ALIGN = 768
MM_BLOCK_CAP = 1408
VMEM_LIMIT = 56 * 1024 * 1024
HIGHEST = lax.Precision.HIGHEST
NT = (((1,), (1,)), ((), ()))
TN = (((0,), (0,)), ((), ()))


def _params(sem=None):
    return pltpu.CompilerParams(dimension_semantics=sem, vmem_limit_bytes=VMEM_LIMIT)


def _call(name, body, grid, ins, outs, scratch=(), sem=None, after=()):
    n_in = len(ins)
    n_tok = len(after)

    def run(*refs):
        body(*refs[:n_in], *refs[n_in + n_tok:])

    tok_spec = pl.BlockSpec((8, 128), lambda *_: (0, 0))
    return pl.pallas_call(
        run, name=name, grid=grid,
        in_specs=[s for _, s in ins] + [tok_spec] * n_tok,
        out_specs=[s for _, s in outs],
        out_shape=[o for o, _ in outs],
        scratch_shapes=list(scratch),
        compiler_params=_params(sem),
    )(*[a for a, _ in ins], *after)


def _sds(shape, dtype):
    return jax.ShapeDtypeStruct(shape, dtype)


def _rows(br, c, cb=0):
    return pl.BlockSpec((br, c), lambda i: (i, cb))


def _whole(shape):
    n = len(shape)
    return pl.BlockSpec(shape, lambda i: (0,) * n)


def _pick(dim, cap, mult):
    best = None
    d = mult
    while d <= min(dim, cap):
        if dim % d == 0:
            best = d
        d += mult
    return best if best is not None else dim


def _hs(h):
    return slice(h * HP, (h + 1) * HP)


def _matmul(name, a, b, *, ta=False, tb=False, out_dtype=F32, addend=None, alpha=1.0, after=()):
    if ta:
        k, m = a.shape
    else:
        m, k = a.shape
    if tb:
        n, k2 = b.shape
    else:
        k2, n = b.shape
    assert k == k2, (name, a.shape, b.shape)
    bm = _pick(m, MM_BLOCK_CAP, 128 if ta else 16)
    bn = _pick(n, MM_BLOCK_CAP, 128)
    bk = _pick(k, MM_BLOCK_CAP, 128 if (not ta or tb) else 16)
    nk = k // bk
    dims = (((0 if ta else 1,), (1 if tb else 0,)), ((), ()))
    has_add = addend is not None

    def body(*refs):
        a_ref, b_ref = refs[:2]
        add_ref = refs[2] if has_add else None
        o_ref = refs[3 if has_add else 2]

        def finish(r):
            if has_add:
                r = r + alpha * add_ref[...]
            o_ref[...] = r.astype(o_ref.dtype)

        part = lax.dot_general(a_ref[...], b_ref[...], dims, preferred_element_type=F32)
        if nk == 1:
            finish(part)
            return
        acc_ref = refs[-1]
        kk = pl.program_id(2)

        @pl.when(kk == 0)
        def _():
            acc_ref[...] = part

        @pl.when(kk > 0)
        def _():
            acc_ref[...] += part

        @pl.when(kk == nk - 1)
        def _():
            finish(acc_ref[...])

    a_spec = pl.BlockSpec((bk, bm), lambda i, j, l: (l, i)) if ta else pl.BlockSpec((bm, bk), lambda i, j, l: (i, l))
    b_spec = pl.BlockSpec((bn, bk), lambda i, j, l: (j, l)) if tb else pl.BlockSpec((bk, bn), lambda i, j, l: (l, j))
    o_spec = pl.BlockSpec((bm, bn), lambda i, j, l: (i, j))
    ins = [(a, a_spec), (b, b_spec)]
    if has_add:
        ins.append((addend, o_spec))
    return _call(name, body, (m // bm, n // bn, nk), ins, [(_sds((m, n), out_dtype), o_spec)],
                 scratch=[pltpu.VMEM((bm, bn), F32)] if nk > 1 else [],
                 sem=("parallel", "parallel", "arbitrary"), after=after)[0]


def _ln_stats(z):
    mu = jnp.mean(z, axis=-1, keepdims=True)
    zc = z - mu
    var = jnp.mean(zc * zc, axis=-1, keepdims=True)
    rstd = lax.rsqrt(var + LN_EPS)
    return zc * rstd, rstd


def _ln_fwd(name, a, res, g, b, after=()):
    r, d = a.shape
    br = ROW_BLOCK
    has_res = res is not None

    def body(*refs):
        if has_res:
            a_ref, r_ref, g_ref, b_ref, y_ref, yb_ref = refs
            z = ALPHA * a_ref[...] + r_ref[...]
        else:
            a_ref, g_ref, b_ref, y_ref, yb_ref = refs
            z = a_ref[...]
        xhat, _ = _ln_stats(z)
        y = xhat * g_ref[...] + b_ref[...]
        y_ref[...] = y
        yb_ref[...] = y.astype(yb_ref.dtype)

    ins = [(a, _rows(br, d))]
    if has_res:
        ins.append((res, _rows(br, d)))
    ins += [(g.reshape(1, d), _whole((1, d))), (b.reshape(1, d), _whole((1, d)))]
    outs = [(_sds((r, d), F32), _rows(br, d)), (_sds((r, d), MXU_DTYPE), _rows(br, d))]
    return _call(name, body, (r // br,), ins, outs, sem=("parallel",), after=after)


def _ln_bwd(name, a, res, dy, g, after=()):
    r, d = a.shape
    br = ROW_BLOCK
    has_res = res is not None

    def body(*refs):
        if has_res:
            a_ref, r_ref, dy_ref, g_ref, dz_ref, dzb_ref, dg_ref, db_ref = refs
            z = ALPHA * a_ref[...] + r_ref[...]
        else:
            a_ref, dy_ref, g_ref, dz_ref, dzb_ref, dg_ref, db_ref = refs
            z = a_ref[...]
        xhat, rstd = _ln_stats(z)
        dyv = dy_ref[...]
        dyg = dyv * g_ref[...]
        m1 = jnp.mean(dyg, axis=-1, keepdims=True)
        m2 = jnp.mean(dyg * xhat, axis=-1, keepdims=True)
        dz = rstd * (dyg - m1 - xhat * m2)
        dz_ref[...] = dz
        dzb_ref[...] = dz.astype(dzb_ref.dtype)

        @pl.when(pl.program_id(0) == 0)
        def _():
            dg_ref[...] = jnp.zeros_like(dg_ref)
            db_ref[...] = jnp.zeros_like(db_ref)

        dg_ref[...] += jnp.sum(dyv * xhat, axis=0, keepdims=True)
        db_ref[...] += jnp.sum(dyv, axis=0, keepdims=True)

    ins = [(a, _rows(br, d))]
    if has_res:
        ins.append((res, _rows(br, d)))
    ins += [(dy, _rows(br, d)), (g.reshape(1, d), _whole((1, d)))]
    outs = [(_sds((r, d), F32), _rows(br, d)), (_sds((r, d), MXU_DTYPE), _rows(br, d)),
            (_sds((1, d), F32), _whole((1, d))), (_sds((1, d), F32), _whole((1, d)))]
    return _call(name, body, (r // br,), ins, outs, sem=("arbitrary",), after=after)


LATENTS = ((R_QLAT // Q_RANK, Q_RANK), (R_KVLAT // KV_RANK, KV_RANK))


def _latent_norm_fwd(proj_r, gains):
    r = proj_r.shape[0]
    br = ROW_BLOCK

    def body(xq_ref, xk_ref, gq_ref, gk_ref, yq_ref, yk_ref):
        for x_ref, g_ref, y_ref in ((xq_ref, gq_ref, yq_ref), (xk_ref, gk_ref, yk_ref)):
            x = x_ref[...]
            rstd = lax.rsqrt(jnp.mean(x * x, axis=-1, keepdims=True) + RMS_EPS)
            y_ref[...] = (x * rstd * g_ref[...]).astype(y_ref.dtype)

    return _call("latent_norm_fwd", body, (r // br,),
                 [(proj_r, _rows(br, wd, cb)) for cb, wd in LATENTS]
                 + [(g.reshape(1, wd), _whole((1, wd))) for g, (_, wd) in zip(gains, LATENTS)],
                 [(_sds((r, wd), MXU_DTYPE), _rows(br, wd)) for _, wd in LATENTS], sem=("parallel",))


def _latent_norm_bwd(proj_r, dys, gains):
    r = proj_r.shape[0]
    br = ROW_BLOCK

    def body(xq_ref, xk_ref, dq_ref, dk_ref, gq_ref, gk_ref, oq_ref, ok_ref, dgq_ref, dgk_ref):
        @pl.when(pl.program_id(0) == 0)
        def _():
            dgq_ref[...] = jnp.zeros_like(dgq_ref)
            dgk_ref[...] = jnp.zeros_like(dgk_ref)

        for x_ref, dy_ref, g_ref, dx_ref, dg_ref in ((xq_ref, dq_ref, gq_ref, oq_ref, dgq_ref),
                                                     (xk_ref, dk_ref, gk_ref, ok_ref, dgk_ref)):
            x = x_ref[...]
            rstd = lax.rsqrt(jnp.mean(x * x, axis=-1, keepdims=True) + RMS_EPS)
            nrm = x * rstd
            dyv = dy_ref[...]
            dyg = dyv * g_ref[...]
            dx_ref[...] = (rstd * (dyg - nrm * jnp.mean(dyg * nrm, axis=-1, keepdims=True))).astype(dx_ref.dtype)
            dg_ref[...] += jnp.sum(dyv * nrm, axis=0, keepdims=True)

    return _call("latent_norm_bwd", body, (r // br,),
                 [(proj_r, _rows(br, wd, cb)) for cb, wd in LATENTS]
                 + [(dy, _rows(br, wd)) for dy, (_, wd) in zip(dys, LATENTS)]
                 + [(g.reshape(1, wd), _whole((1, wd))) for g, (_, wd) in zip(gains, LATENTS)],
                 [(_sds((r, wd), MXU_DTYPE), _rows(br, wd)) for _, wd in LATENTS]
                 + [(_sds((1, wd), F32), _whole((1, wd))) for _, wd in LATENTS], sem=("arbitrary",))


def _lane_iota(shape):
    return lax.broadcasted_iota(jnp.int32, shape, 1)


def _rotary(t, c, s, lane, sign):
    second = pltpu.roll(t, HP - HALF, axis=1)
    first = pltpu.roll(t, HALF, axis=1)
    lo = (lane >= LANE_PE) & (lane < LANE_PE + HALF)
    hi = (lane >= LANE_PE + HALF) & (lane < LANE_PE + ROPE)
    return jnp.where(lo, t * c - sign * second * s, jnp.where(hi, t * c + sign * first * s, t))


def _rope_fwd(q_raw, k_part, proj_r, cos_t, sin_t):
    r = q_raw.shape[0]
    br = ROW_BLOCK

    def body(q_ref, k_ref, t_ref, c_ref, s_ref, qo_ref, ko_ref):
        c = c_ref[...]
        s = s_ref[...]
        lane = _lane_iota((br, HP))
        pe = (lane >= LANE_PE) & (lane < LANE_PE + ROPE)
        kp = jnp.where(pe, _rotary(t_ref[...], c, s, lane, 1.0), 0.0)
        for h in range(HEADS):
            qo_ref[:, _hs(h)] = (_rotary(q_ref[:, _hs(h)], c, s, lane, 1.0) * MLA_SCALE).astype(qo_ref.dtype)
            ko_ref[:, _hs(h)] = (k_ref[:, _hs(h)] + kp).astype(ko_ref.dtype)

    blk = _rows(br, HP)
    wide = _rows(br, HW)
    return _call("rope_fwd", body, (r // br,),
                 [(q_raw, wide), (k_part, wide), (proj_r, _rows(br, HP, R_LAST // HP)), (cos_t, blk), (sin_t, blk)],
                 [(_sds((r, HW), MXU_DTYPE), wide)] * 2, sem=("parallel",))


def _rope_bwd(dq, dk, dv, dfl, cos_t, sin_t):
    r = dq.shape[0]
    br = ROW_BLOCK

    def body(dq_ref, dk_ref, dv_ref, fl_ref, c_ref, s_ref, dqo_ref, dkv_ref, dl_ref):
        c = c_ref[...]
        s = s_ref[...]
        lane = _lane_iota((br, HP))
        pe = (lane >= LANE_PE) & (lane < LANE_PE + ROPE)
        acc = jnp.zeros((br, HP), F32)
        for h in range(HEADS):
            dqo_ref[:, _hs(h)] = (_rotary(dq_ref[:, _hs(h)], c, s, lane, -1.0) * MLA_SCALE).astype(dqo_ref.dtype)
            dkh = dk_ref[:, _hs(h)]
            acc = acc + dkh
            dkv_ref[:, _hs(h)] = dkh.astype(dkv_ref.dtype)
            dkv_ref[:, _hs(HEADS + h)] = dv_ref[:, _hs(h)].astype(dkv_ref.dtype)
        dl_ref[...] = (jnp.where(pe, _rotary(acc, c, s, lane, -1.0), 0.0) + fl_ref[...]).astype(dl_ref.dtype)

    blk = _rows(br, HP)
    wide = _rows(br, HW)
    return _call("rope_bwd", body, (r // br,),
                 [(dq, wide), (dk, wide), (dv, wide), (dfl, blk), (cos_t, blk), (sin_t, blk)],
                 [(_sds((r, HW), MXU_DTYPE), wide), (_sds((r, 2 * HW), MXU_DTYPE), _rows(br, 2 * HW)),
                  (_sds((r, HP), MXU_DTYPE), blk)],
                 sem=("parallel",))


def _log_sigmoid(x):
    return jnp.minimum(x, 0.0) - jnp.log(1.0 + jnp.exp(-jnp.abs(x)))


def _head_lane(x, h, lane):
    return jnp.sum(jnp.where(lane == h, x, 0.0), axis=1, keepdims=True)


def _forget_fwd(proj_r, bf_row):
    r = proj_r.shape[0]
    br = ROW_BLOCK

    def body(t_ref, b_ref, ob_ref, ot_ref, carry_ref):
        @pl.when(pl.program_id(0) == 0)
        def _():
            carry_ref[...] = jnp.zeros_like(carry_ref)

        x = t_ref[...] + b_ref[...]
        lane = _lane_iota(x.shape)
        lf = jnp.where((lane >= LANE_FL) & (lane < LANE_FL + HEADS), _log_sigmoid(x), 0.0)
        tri = (lax.broadcasted_iota(jnp.int32, (br, br), 0) >= lax.broadcasted_iota(jnp.int32, (br, br), 1)).astype(F32)
        cum = jnp.dot(tri, lf, precision=HIGHEST, preferred_element_type=F32) + carry_ref[0:1, :]
        for h in range(HEADS):
            ob_ref[:, _hs(h)] = jnp.broadcast_to(_head_lane(cum, LANE_FL + h, lane), (br, HP))
        ot_ref[...] = cum.T[LANE_FL:LANE_FL + HEADS, :]
        carry_ref[...] = jnp.broadcast_to(cum[br - 1:br, :], carry_ref.shape)

    return _call("forget_fwd", body, (r // br,),
                 [(proj_r, _rows(br, HP, R_LAST // HP)), (bf_row, _whole((1, HP)))],
                 [(_sds((r, HW), F32), _rows(br, HW)), (_sds((HEADS, r), F32), pl.BlockSpec((HEADS, br), lambda i: (0, i)))],
                 scratch=[pltpu.VMEM((8, HP), F32)], sem=("arbitrary",))


def _forget_bwd(proj_r, bf_row, dcq_t, dck_b):
    r = proj_r.shape[0]
    br = ROW_BLOCK
    nb = r // br

    def body(t_ref, b_ref, dcq_ref, dck_ref, o_ref, db_ref, carry_ref):
        @pl.when(pl.program_id(0) == 0)
        def _():
            carry_ref[...] = jnp.zeros_like(carry_ref)
            db_ref[...] = jnp.zeros_like(db_ref)

        lane = _lane_iota((br, HP))
        dc = jnp.concatenate([dcq_ref[...], jnp.zeros((HP - HEADS, br), F32)], axis=0).T
        for h in range(HEADS):
            dc = dc + jnp.where(lane == LANE_FL + h, dck_ref[:, h * HP:h * HP + 1], 0.0)
        triu = (lax.broadcasted_iota(jnp.int32, (br, br), 0) <= lax.broadcasted_iota(jnp.int32, (br, br), 1)).astype(F32)
        dlf = jnp.dot(triu, dc, precision=HIGHEST, preferred_element_type=F32) + carry_ref[0:1, :]
        carry_ref[...] = jnp.broadcast_to(dlf[0:1, :], carry_ref.shape)
        x = t_ref[...] + b_ref[...]
        dfl = jnp.where((lane >= LANE_FL) & (lane < LANE_FL + HEADS), dlf * jax.nn.sigmoid(-x), 0.0)
        o_ref[...] = dfl
        db_ref[...] += jnp.sum(dfl, axis=0, keepdims=True)

    rev = pl.BlockSpec((br, HP), lambda i: (nb - 1 - i, 0))
    return _call("forget_bwd", body, (nb,),
                 [(proj_r, pl.BlockSpec((br, HP), lambda i: (nb - 1 - i, R_LAST // HP))), (bf_row, _whole((1, HP))),
                  (dcq_t, pl.BlockSpec((HEADS, br), lambda i: (0, nb - 1 - i))),
                  (dck_b, pl.BlockSpec((br, HW), lambda i: (nb - 1 - i, 0)))],
                 [(_sds((r, HP), F32), rev), (_sds((1, HP), F32), _whole((1, HP)))],
                 scratch=[pltpu.VMEM((8, HP), F32)], sem=("arbitrary",))


def _gate_fwd(proj_r, b_gate, bm, bfx):
    r, d = bm.shape
    br = ROW_BLOCK
    cb = R_GATE // d

    def body(gm_ref, gf_ref, b1_ref, b2_ref, bm_ref, bf_ref, o_ref):
        g1 = jax.nn.sigmoid(gm_ref[...] + b1_ref[...])
        g2 = jax.nn.sigmoid(gf_ref[...] + b2_ref[...])
        o_ref[...] = (g1 * bm_ref[...].astype(F32) + g2 * bf_ref[...].astype(F32)).astype(o_ref.dtype)

    b1 = b_gate[:d].reshape(1, d)
    b2 = b_gate[d:].reshape(1, d)
    return _call("gate_fwd", body, (r // br,),
                 [(proj_r, _rows(br, d, cb)), (proj_r, _rows(br, d, cb + 1)), (b1, _whole((1, d))), (b2, _whole((1, d))),
                  (bm, _rows(br, d)), (bfx, _rows(br, d))],
                 [(_sds((r, d), MXU_DTYPE), _rows(br, d))], sem=("parallel",))[0]


def _gate_bwd(proj_r, b_gate, bm, bfx, dmerged):
    r, d = bm.shape
    br = ROW_BLOCK
    cb = R_GATE // d

    def body(gm_ref, gf_ref, b1_ref, b2_ref, bm_ref, bf_ref, dm_ref, dbm_ref, dbf_ref, dgl_ref, dbg_ref):
        g1 = jax.nn.sigmoid(gm_ref[...] + b1_ref[...])
        g2 = jax.nn.sigmoid(gf_ref[...] + b2_ref[...])
        dm = dm_ref[...].astype(F32)
        dbm_ref[...] = (dm * g1).astype(dbm_ref.dtype)
        dbf_ref[...] = (dm * g2).astype(dbf_ref.dtype)
        dl1 = dm * bm_ref[...].astype(F32) * (g1 * (1.0 - g1))
        dl2 = dm * bf_ref[...].astype(F32) * (g2 * (1.0 - g2))
        dgl_ref[:, 0:d] = dl1.astype(dgl_ref.dtype)
        dgl_ref[:, d:2 * d] = dl2.astype(dgl_ref.dtype)

        @pl.when(pl.program_id(0) == 0)
        def _():
            dbg_ref[...] = jnp.zeros_like(dbg_ref)

        dbg_ref[:, 0:d] += jnp.sum(dl1, axis=0, keepdims=True)
        dbg_ref[:, d:2 * d] += jnp.sum(dl2, axis=0, keepdims=True)

    b1 = b_gate[:d].reshape(1, d)
    b2 = b_gate[d:].reshape(1, d)
    return _call("gate_bwd", body, (r // br,),
                 [(proj_r, _rows(br, d, cb)), (proj_r, _rows(br, d, cb + 1)), (b1, _whole((1, d))), (b2, _whole((1, d))),
                  (bm, _rows(br, d)), (bfx, _rows(br, d)), (dmerged, _rows(br, d))],
                 [(_sds((r, d), MXU_DTYPE), _rows(br, d)), (_sds((r, d), MXU_DTYPE), _rows(br, d)),
                  (_sds((r, 2 * d), MXU_DTYPE), _rows(br, 2 * d)), (_sds((1, 2 * d), F32), _whole((1, 2 * d)))],
                 sem=("arbitrary",))


HALO = 16
GLU_BWD_BLOCK = 128


def _conv_taps(gp, halo, first_block):
    halo = jnp.where(first_block, 0.0, halo.astype(F32))
    rid = lax.broadcasted_iota(jnp.int32, gp.shape, 0)
    last, prev = halo[HALO - 1:HALO, :], halo[HALO - 2:HALO - 1, :]
    g1 = jnp.where(rid == 0, last, pltpu.roll(gp, 1, axis=0))
    g2 = jnp.where(rid == 0, prev, jnp.where(rid == 1, last, pltpu.roll(gp, 2, axis=0)))
    return g1, g2


def _prev_halo(br, c):
    return pl.BlockSpec((HALO, c), lambda i: (jnp.maximum(i * (br // HALO) - 1, 0), 0))


def _glu_fwd(up, conv_w, conv_b):
    r = up.shape[0]
    c = D_FF
    br = ROW_BLOCK

    def body(gp_ref, halo_ref, val_ref, w_ref, b_ref, o_ref):
        gp = gp_ref[...].astype(F32)
        g1, g2 = _conv_taps(gp, halo_ref[...], pl.program_id(0) == 0)
        gate = w_ref[0:1, :] * g2 + w_ref[1:2, :] * g1 + w_ref[2:3, :] * gp + b_ref[...]
        o_ref[...] = (gate * jax.nn.sigmoid(gate) * val_ref[...].astype(F32)).astype(o_ref.dtype)

    return _call("glu_fwd", body, (r // br,),
                 [(up, _rows(br, c, 0)), (up, _prev_halo(br, c)), (up, _rows(br, c, 1)),
                  (conv_w, _whole((3, c))), (conv_b.reshape(1, c), _whole((1, c)))],
                 [(_sds((r, c), MXU_DTYPE), _rows(br, c))], sem=("parallel",))[0]


def _glu_bwd(up, conv_w, conv_b, d_act):
    r = up.shape[0]
    c = D_FF
    br = GLU_BWD_BLOCK
    nb = r // br

    def body(gp_ref, halo_ref, val_ref, da_ref, gpn_ref, valn_ref, dan_ref, w_ref, b_ref, o_ref, dw_ref, db_ref):
        i = pl.program_id(0)
        w0, w1, w2, bias = w_ref[0:1, :], w_ref[1:2, :], w_ref[2:3, :], b_ref[...]

        def d_gate(gp, g1, g2, val, da):
            gate = w0 * g2 + w1 * g1 + w2 * gp + bias
            sg = jax.nn.sigmoid(gate)
            return da * val * (sg * (1.0 + gate * (1.0 - sg))), da * (gate * sg)

        gp = gp_ref[...].astype(F32)
        g1, g2 = _conv_taps(gp, halo_ref[...], i == 0)
        dg, dv = d_gate(gp, g1, g2, val_ref[...].astype(F32), da_ref[...].astype(F32))
        gpn = gpn_ref[...].astype(F32)
        g1n, g2n = _conv_taps(gpn, gp[br - HALO:, :], False)
        dgn, _ = d_gate(gpn, g1n, g2n, valn_ref[...].astype(F32), dan_ref[...].astype(F32))
        dgn = jnp.where(i == nb - 1, 0.0, dgn)
        rid = lax.broadcasted_iota(jnp.int32, dg.shape, 0)
        u1 = jnp.where(rid == br - 1, dgn[0:1, :], pltpu.roll(dg, br - 1, axis=0))
        u2 = jnp.where(rid == br - 1, dgn[1:2, :], jnp.where(rid == br - 2, dgn[0:1, :], pltpu.roll(dg, br - 2, axis=0)))
        o_ref[:, 0:c] = (w2 * dg + w1 * u1 + w0 * u2).astype(o_ref.dtype)
        o_ref[:, c:2 * c] = dv.astype(o_ref.dtype)

        @pl.when(i == 0)
        def _():
            dw_ref[...] = jnp.zeros_like(dw_ref)
            db_ref[...] = jnp.zeros_like(db_ref)

        dw_ref[0:1, :] += jnp.sum(dg * g2, axis=0, keepdims=True)
        dw_ref[1:2, :] += jnp.sum(dg * g1, axis=0, keepdims=True)
        dw_ref[2:3, :] += jnp.sum(dg * gp, axis=0, keepdims=True)
        db_ref[...] += jnp.sum(dg, axis=0, keepdims=True)

    nxt = lambda cb: pl.BlockSpec((HALO, c), lambda i: (jnp.minimum((i + 1) * (br // HALO), r // HALO - 1), cb))
    return _call("glu_bwd", body, (nb,),
                 [(up, _rows(br, c, 0)), (up, _prev_halo(br, c)), (up, _rows(br, c, 1)), (d_act, _rows(br, c)),
                  (up, nxt(0)), (up, nxt(1)), (d_act, nxt(0)),
                  (conv_w, _whole((3, c))), (conv_b.reshape(1, c), _whole((1, c)))],
                 [(_sds((r, 2 * c), MXU_DTYPE), _rows(br, 2 * c)),
                  (_sds((8, c), F32), _whole((8, c))), (_sds((1, c), F32), _whole((1, c)))],
                 sem=("arbitrary",))


def _token_specs(seq, d):
    br = ROW_BLOCK
    nxb = seq // br
    main = pl.BlockSpec((br, d), lambda i: (jnp.minimum(i, nxb - 1), 0))
    tail = pl.BlockSpec((N_META, d), lambda i: (jnp.clip(i * (br // N_META) - 1, 0, seq // N_META - 1), 0))
    return main, tail


def _padded_block(main_ref, tail_ref, first, seq):
    br = ROW_BLOCK
    i = pl.program_id(0)
    nxb = seq // br
    main = jnp.where(i < nxb, main_ref[...], 0.0)
    head = jnp.where(i == 0, first, jnp.where(i <= nxb, tail_ref[...], 0.0))
    return jnp.concatenate([head, main[:br - N_META]], axis=0)


def _ln_emb_fwd(x, meta, g, b, rows, after=()):
    seq, d = x.shape
    br = ROW_BLOCK
    assert seq % br == 0 and br % N_META == 0 and rows % br == 0

    def body(x_ref, tail_ref, meta_ref, g_ref, b_ref, y_ref, yb_ref):
        z = _padded_block(x_ref, tail_ref, meta_ref[...], seq)
        xhat, _ = _ln_stats(z)
        y = xhat * g_ref[...] + b_ref[...]
        y_ref[...] = y
        yb_ref[...] = y.astype(yb_ref.dtype)

    main, tail = _token_specs(seq, d)
    return _call("ln_emb_fwd", body, (rows // br,),
                 [(x, main), (x, tail), (meta, _whole((N_META, d))), (g.reshape(1, d), _whole((1, d))),
                  (b.reshape(1, d), _whole((1, d)))],
                 [(_sds((rows, d), F32), _rows(br, d)), (_sds((rows, d), MXU_DTYPE), _rows(br, d))],
                 sem=("parallel",), after=after)


def _ln_emb_bwd(x, meta, dh0, g):
    seq, d = x.shape
    br = ROW_BLOCK
    step = br // N_META

    def ln_bwd(z, dy, gv):
        xhat, rstd = _ln_stats(z)
        dyg = dy * gv
        m1 = jnp.mean(dyg, axis=-1, keepdims=True)
        m2 = jnp.mean(dyg * xhat, axis=-1, keepdims=True)
        dz = rstd * (dyg - m1 - xhat * m2)
        return dz, jnp.sum(dy * xhat, axis=0, keepdims=True), jnp.sum(dy, axis=0, keepdims=True)

    def body(x_ref, dh_ref, nxt_ref, meta_ref, top_ref, g_ref, dx_ref, dm_ref, dg_ref, db_ref):
        gv = g_ref[...]
        dy = jnp.concatenate([dh_ref[N_META:, :], nxt_ref[...]], axis=0)
        dz, dg, db = ln_bwd(x_ref[...], dy, gv)
        dx_ref[...] = dz

        @pl.when(pl.program_id(0) == 0)
        def _():
            dzm, dgm, dbm = ln_bwd(meta_ref[...], top_ref[...], gv)
            dm_ref[...] = dzm
            dg_ref[...] = dgm
            db_ref[...] = dbm

        dg_ref[...] += dg
        db_ref[...] += db

    small = _whole((N_META, d))
    return _call("ln_emb_bwd", body, (seq // br,),
                 [(x, _rows(br, d)), (dh0, _rows(br, d)), (dh0, pl.BlockSpec((N_META, d), lambda i: ((i + 1) * step, 0))),
                  (meta, small), (dh0, small), (g.reshape(1, d), _whole((1, d)))],
                 [(_sds((seq, d), F32), _rows(br, d)), (_sds((N_META, d), F32), small),
                  (_sds((1, d), F32), _whole((1, d))), (_sds((1, d), F32), _whole((1, d)))], sem=("arbitrary",))


def _ln_ffn_loss(h1, f, tgt, g, b):
    r, d = h1.shape
    seq = tgt.shape[0]
    br = ROW_BLOCK

    def body(a_ref, r_ref, t_ref, tail_ref, g_ref, b_ref, l_ref):
        err = _loss_err(a_ref, r_ref, t_ref, tail_ref, g_ref, b_ref, seq)[0]

        @pl.when(pl.program_id(0) == 0)
        def _():
            l_ref[...] = jnp.zeros_like(l_ref)

        l_ref[...] += jnp.sum(jnp.sum(err * err, axis=1, keepdims=True), axis=0, keepdims=True) * (0.5 / d)

    main, tail = _token_specs(seq, d)
    return _call("ln_ffn_loss", body, (r // br,),
                 [(h1, _rows(br, d)), (f, _rows(br, d)), (tgt, main), (tgt, tail),
                  (g.reshape(1, d), _whole((1, d))), (b.reshape(1, d), _whole((1, d)))],
                 [(_sds((1, 1), F32), _whole((1, 1)))], sem=("arbitrary",))[0]


def _loss_err(a_ref, r_ref, t_ref, tail_ref, g_ref, b_ref, seq):
    br, d = a_ref.shape
    xhat, rstd = _ln_stats(ALPHA * a_ref[...] + r_ref[...])
    y = xhat * g_ref[...] + b_ref[...]
    t = _padded_block(t_ref, tail_ref, jnp.zeros((N_META, d), F32), seq)
    rid = lax.broadcasted_iota(jnp.int32, (br, d), 0) + pl.program_id(0) * br
    valid = (rid >= N_META) & (rid < N_META + seq)
    return jnp.where(valid, y - t, 0.0), xhat, rstd


def _ln_ffn_bwd(h1, f, tgt, g, b):
    r, d = h1.shape
    seq = tgt.shape[0]
    br = ROW_BLOCK

    def body(a_ref, r_ref, t_ref, tail_ref, g_ref, b_ref, dz_ref, dzb_ref, dg_ref, db_ref):
        err, xhat, rstd = _loss_err(a_ref, r_ref, t_ref, tail_ref, g_ref, b_ref, seq)
        dyv = err * (1.0 / d)
        dyg = dyv * g_ref[...]
        m1 = jnp.mean(dyg, axis=-1, keepdims=True)
        m2 = jnp.mean(dyg * xhat, axis=-1, keepdims=True)
        dz = rstd * (dyg - m1 - xhat * m2)
        dz_ref[...] = dz
        dzb_ref[...] = dz.astype(dzb_ref.dtype)

        @pl.when(pl.program_id(0) == 0)
        def _():
            dg_ref[...] = jnp.zeros_like(dg_ref)
            db_ref[...] = jnp.zeros_like(db_ref)

        dg_ref[...] += jnp.sum(dyv * xhat, axis=0, keepdims=True)
        db_ref[...] += jnp.sum(dyv, axis=0, keepdims=True)

    main, tail = _token_specs(seq, d)
    return _call("ln_ffn_bwd", body, (r // br,),
                 [(h1, _rows(br, d)), (f, _rows(br, d)), (tgt, main), (tgt, tail),
                  (g.reshape(1, d), _whole((1, d))), (b.reshape(1, d), _whole((1, d)))],
                 [(_sds((r, d), F32), _rows(br, d)), (_sds((r, d), MXU_DTYPE), _rows(br, d)),
                  (_sds((1, d), F32), _whole((1, d))), (_sds((1, d), F32), _whole((1, d)))], sem=("arbitrary",))


def _attn_fwd(name, q, k, v, cum_b=None, cum_t=None):
    (qa, qg), (ka, kg), (va, vg) = q, k, v
    r = qa.shape[0]
    tq, tk = ATT_TQ, ATT_TK
    nq, nk = r // tq, r // tk
    bias = cum_b is not None

    def body(*refs):
        if bias:
            q_ref, k_ref, vt_ref, cb_ref, ct_ref, o_ref, ob_ref, lse_ref = refs
        else:
            q_ref, k_ref, vt_ref, o_ref, ob_ref, lse_ref = refs
        i = pl.program_id(1)
        qs = [q_ref[:, _hs(hh)] for hh in range(hg)]
        cqs = [ct_ref[hh] for hh in range(hg)] if bias else None
        diff = lax.broadcasted_iota(jnp.int32, (tk, tq), 0) - lax.broadcasted_iota(jnp.int32, (tk, tq), 1)

        def step(j, carry, masked):
            keys = pl.ds(pl.multiple_of(j * tk, tk), tk)
            out = []
            for hh in range(hg):
                m, l, acc = carry[hh]
                kt = k_ref[keys, _hs(hh)]
                s = lax.dot_general(kt, qs[hh], NT, preferred_element_type=F32)
                if bias:
                    s = s + (cqs[hh] - cb_ref[keys, hh * HP:hh * HP + 1])
                if masked:
                    s = jnp.where(diff <= i * tq - j * tk, s, NEG_INF)
                m_new = jnp.maximum(m, jnp.max(s, axis=0, keepdims=True))
                p = jnp.exp(s - m_new)
                a = jnp.exp(m - m_new)
                l = a * l + jnp.sum(p, axis=0, keepdims=True)
                acc = a * acc + jnp.dot(vt_ref[j, _hs(hh), :], p.astype(kt.dtype), preferred_element_type=F32)
                out.append((m_new, l, acc))
            return tuple(out)

        n_clear = (i * tq + 1) // tk
        n_all = ((i + 1) * tq - 1) // tk + 1
        carry = tuple((jnp.full((1, tq), NEG_INF, F32), jnp.zeros((1, tq), F32), jnp.zeros((HP, tq), F32))
                      for _ in range(hg))
        carry = lax.fori_loop(0, n_clear, lambda j, c: step(j, c, False), carry)
        carry = lax.fori_loop(n_clear, n_all, lambda j, c: step(j, c, True), carry)
        for hh in range(hg):
            m, l, acc = carry[hh]
            o = (acc / l).T
            o_ref[:, _hs(hh)] = o
            ob_ref[:, hh * V_DIM:(hh + 1) * V_DIM] = o[:, :V_DIM].astype(ob_ref.dtype)
            lse_ref[hh] = m + jnp.log(l)

    hg = ATT_HEADS_FWD
    w = hg * HP
    gpw = HW // w
    tile = lambda g: pl.BlockSpec((tq, w), lambda h, i: (i, g * gpw + h))
    res = lambda g: pl.BlockSpec((r, w), lambda h, i: (0, g * gpw + h))
    v_t = _key_tiles_transposed(name + "_vt", va, vg)
    ins = [(qa, tile(qg)), (ka, res(kg)), (v_t, pl.BlockSpec((nk, w, tk), lambda h, i: (0, h, 0)))]
    if bias:
        ins += [(cum_b, res(0)),
                (cum_t.reshape(HEADS, nq, 1, tq), pl.BlockSpec((hg, None, 1, tq), lambda h, i: (h, i, 0, 0)))]
    outs = [(_sds((r, HW), F32), tile(0)),
            (_sds((r, HEADS * V_DIM), MXU_DTYPE), pl.BlockSpec((tq, hg * V_DIM), lambda h, i: (i, h))),
            (_sds((HEADS, nq, 1, tq), F32), pl.BlockSpec((hg, None, 1, tq), lambda h, i: (h, i, 0, 0)))]
    o, ob, lse = _call(name, body, (gpw, nq), ins, outs, sem=("parallel", "parallel"))
    return o, ob, lse.reshape(HEADS, r)


def _key_tiles_transposed(name, a, group):
    r = a.shape[0]
    tk = ATT_TK

    def body(x_ref, o_ref):
        for h in range(HEADS):
            o_ref[_hs(h), :] = x_ref[:, _hs(h)].astype(F32).T.astype(o_ref.dtype)

    return _call(name, body, (r // tk,),
                 [(a, pl.BlockSpec((tk, HW), lambda j: (j, group)))],
                 [(_sds((r // tk, HW, tk), a.dtype), pl.BlockSpec((None, HW, tk), lambda j: (j, 0, 0)))],
                 sem=("parallel",))[0]


def _attn_bwd(name, q, k, v, do_b, o, lse_t, cum_b=None, cum_t=None, out_dtype=F32, after=()):
    (qa, qg), (ka, kg), (va, vg) = q, k, v
    r = qa.shape[0]
    tq, tk = ATT_TQ, ATT_TK
    nq, nk = r // tq, r // tk
    bias = cum_b is not None

    def body(*refs):
        if bias:
            (q_ref, k_ref, v_ref, do_ref, o_ref, lse_ref, cb_ref, ct_ref,
             dq_ref, dk_ref, dv_ref, dcq_ref, dck_ref, dqt_ref, dl_ref) = refs
        else:
            q_ref, k_ref, v_ref, do_ref, o_ref, lse_ref, dq_ref, dk_ref, dv_ref, dqt_ref, dl_ref = refs
        j = pl.program_id(1)

        @pl.when(j == 0)
        def _():
            dqt_ref[...] = jnp.zeros_like(dqt_ref)
            if bias:
                dcq_ref[...] = jnp.zeros_like(dcq_ref)
            for hh in range(hg):
                for i in range(nq):
                    rows = slice(i * tq, (i + 1) * tq)
                    prod = do_ref[rows, _hs(hh)].astype(F32) * o_ref[rows, _hs(hh)]
                    dl_ref[hh, i] = jnp.sum(prod.T, axis=0, keepdims=True)

        kts = [k_ref[:, _hs(hh)] for hh in range(hg)]
        vts = [v_ref[:, _hs(hh)] for hh in range(hg)]
        k_trs = [kt.astype(F32).T.astype(kt.dtype) for kt in kts]
        cks = [cb_ref[:, hh * HP:hh * HP + 1] for hh in range(hg)] if bias else None
        diff = lax.broadcasted_iota(jnp.int32, (tk, tq), 0) - lax.broadcasted_iota(jnp.int32, (tk, tq), 1)

        def step(i, carry, masked):
            rows = pl.ds(pl.multiple_of(i * tq, tq), tq)
            out = []
            for hh in range(hg):
                dk_acc, dv_acc, dck_acc = carry[hh]
                qt = q_ref[rows, _hs(hh)]
                dot = do_ref[rows, _hs(hh)]
                s = lax.dot_general(kts[hh], qt, NT, preferred_element_type=F32)
                if bias:
                    s = s + (ct_ref[hh, i] - cks[hh])
                if masked:
                    s = jnp.where(diff <= i * tq - j * tk, s, NEG_INF)
                p = jnp.exp(s - lse_ref[hh, i])
                dp = lax.dot_general(vts[hh], dot, NT, preferred_element_type=F32)
                ds = p * (dp - dl_ref[hh, i])
                pb = p.astype(dot.dtype)
                dsb = ds.astype(qt.dtype)
                dv_acc = dv_acc + jnp.dot(pb, dot, preferred_element_type=F32)
                dk_acc = dk_acc + jnp.dot(dsb, qt, preferred_element_type=F32)
                dqt_ref[hh, i] += jnp.dot(k_trs[hh], dsb, preferred_element_type=F32)
                if bias:
                    dcq_ref[hh, i] += jnp.sum(ds, axis=0, keepdims=True)
                    dck_acc = dck_acc - jnp.sum(ds, axis=1, keepdims=True)
                out.append((dk_acc, dv_acc, dck_acc))
            return tuple(out)

        i_first = (j * tk) // tq
        i_clear = jnp.minimum(((j + 1) * tk + tq - 2) // tq, nq)
        carry = tuple((jnp.zeros((tk, HP), F32), jnp.zeros((tk, HP), F32), jnp.zeros((tk, 1), F32)) for _ in range(hg))
        carry = lax.fori_loop(i_first, i_clear, lambda i, c: step(i, c, True), carry)
        carry = lax.fori_loop(i_clear, nq, lambda i, c: step(i, c, False), carry)
        for hh in range(hg):
            dk_acc, dv_acc, dck_acc = carry[hh]
            dk_ref[:, _hs(hh)] = dk_acc.astype(dk_ref.dtype)
            dv_ref[:, _hs(hh)] = dv_acc.astype(dv_ref.dtype)
            if bias:
                dck_ref[:, _hs(hh)] = jnp.broadcast_to(dck_acc, (tk, HP))

        @pl.when(j == nk - 1)
        def _():
            for hh in range(hg):
                for i in range(nq):
                    dq_ref[i * tq:(i + 1) * tq, _hs(hh)] = dqt_ref[hh, i].T.astype(dq_ref.dtype)

    hg = ATT_HEADS
    w = hg * HP
    gpw = HW // w
    res = lambda g: pl.BlockSpec((r, w), lambda h, j: (0, g * gpw + h))
    tile = lambda g: pl.BlockSpec((tk, w), lambda h, j: (j, g * gpw + h))
    rowv = pl.BlockSpec((hg, nq, 1, tq), lambda h, j: (h, 0, 0, 0))
    as_rows = lambda a: a.reshape(HEADS, nq, 1, tq)
    ins = [(qa, res(qg)), (ka, tile(kg)), (va, tile(vg)), (do_b, res(0)), (o, res(0)), (as_rows(lse_t), rowv)]
    outs = [(_sds((r, HW), out_dtype), res(0)), (_sds((r, HW), out_dtype), tile(0)), (_sds((r, HW), out_dtype), tile(0))]
    if bias:
        ins += [(cum_b, tile(0)), (as_rows(cum_t), rowv)]
        outs += [(_sds((HEADS, nq, 1, tq), F32), rowv), (_sds((r, HW), F32), tile(0))]
    res_out = _call(name, body, (gpw, nk), ins, outs,
                    scratch=[pltpu.VMEM((hg, nq, HP, tq), F32), pltpu.VMEM((hg, nq, 1, tq), F32)],
                    sem=("parallel", "arbitrary"), after=after)
    if bias:
        dq, dk, dv, dcq, dck = res_out
        return dq, dk, dv, dcq.reshape(HEADS, r), dck
    return res_out


MESH_ID = pl.DeviceIdType.MESH
ANY = pl.BlockSpec(memory_space=pl.ANY)


N_GATHER_COPIES = 8


def _allgather(name, shards):
    n = len(shards)

    def body(*refs):
        x_refs, out_refs = refs[:n], refs[n:2 * n]
        send_sems, recv_sems, local_sems = refs[2 * n:]
        x, y, c = lax.axis_index("x"), lax.axis_index("y"), lax.axis_index("c")
        me, sibling = (x, y, c), (x, y, 1 - c)
        xn, yn, dg = (1 - x, y, c), (x, 1 - y, c), (1 - x, 1 - y, c)
        other = lambda dev: (dev[0], dev[1], 1 - c)

        def slot(ti, dev, half=None):
            ref = out_refs[ti].at[4 * dev[0] + 2 * dev[1] + dev[2]]
            if half is None:
                return ref
            rows = shards[ti].shape[0] // 2
            return ref.at[pl.ds(half * rows, rows)]

        def copy(ti, k, block, to, half=None, src=None):
            return pltpu.make_async_remote_copy(
                src_ref=slot(ti, block, half) if src is None else src, dst_ref=slot(ti, block, half),
                send_sem=send_sems.at[ti, k], recv_sem=recv_sems.at[ti, k], device_id=to, device_id_type=MESH_ID)

        mine = [pltpu.make_async_copy(x_refs[ti], slot(ti, me), local_sems.at[ti]) for ti in range(n)]
        for cp in mine:
            cp.start()
        started = []

        def go(cp):
            cp.start()
            started.append(cp)

        for ti in range(n):
            go(copy(ti, 0, me, sibling, src=x_refs[ti]))
            go(copy(ti, 1, me, xn, src=x_refs[ti]))
            go(copy(ti, 2, me, yn, src=x_refs[ti]))
        for ti in range(n):
            copy(ti, 1, xn, me).wait_recv()
            go(copy(ti, 3, xn, yn, half=0))
            go(copy(ti, 5, xn, sibling))
            copy(ti, 2, yn, me).wait_recv()
            go(copy(ti, 4, yn, xn, half=1))
            go(copy(ti, 6, yn, sibling))
        for ti in range(n):
            copy(ti, 3, dg, me, half=0).wait_recv()
            copy(ti, 4, dg, me, half=1).wait_recv()
            go(copy(ti, 7, dg, sibling))
        for ti in range(n):
            copy(ti, 0, sibling, me).wait_recv()
            for k, dev in ((5, xn), (6, yn), (7, dg)):
                copy(ti, k, other(dev), me).wait_recv()
        for cp in started:
            cp.wait_send()
        for cp in mine:
            cp.wait()

    sems = pltpu.SemaphoreType.DMA((n, N_GATHER_COPIES))
    return pl.pallas_call(
        body, name=name, out_shape=[_sds((N_DEV,) + s.shape, s.dtype) for s in shards],
        in_specs=[ANY] * n, out_specs=[ANY] * n,
        scratch_shapes=[sems, sems, pltpu.SemaphoreType.DMA((n,))],
    )(*shards)


HBM = pl.BlockSpec(memory_space=pltpu.HBM)
SEM = pl.BlockSpec(memory_space=pltpu.SEMAPHORE)
EFFECT = pltpu.SideEffectType.DATAFLOW_SIDE_EFFECTING
N_PEER = N_DEV - 1


def _my_id():
    return 4 * lax.axis_index("x") + 2 * lax.axis_index("y") + lax.axis_index("c")


def _peers():
    x, y, c = lax.axis_index("x"), lax.axis_index("y"), lax.axis_index("c")
    out = []
    for k in range(1, N_DEV):
        px, py, pc = (1 - x if k & 4 else x, 1 - y if k & 2 else y, 1 - c if k & 1 else c)
        out.append(((px, py, pc), 4 * px + 2 * py + pc))
    return out


def _push_copies(src_refs, land_refs, send_sems, recv_sems, scatter, landing):
    me = _my_id()
    out = []
    for ti, (src, land) in enumerate(zip(src_refs, land_refs)):
        for k, (dev, pid) in enumerate(_peers()):
            out.append(pltpu.make_async_remote_copy(
                src_ref=src.at[pid] if scatter else src, dst_ref=land.at[pid if landing else me],
                send_sem=send_sems.at[ti * N_PEER + k], recv_sem=recv_sems.at[ti * N_PEER + k],
                device_id=dev, device_id_type=MESH_ID))
    return out


def _push_start(name, groups, scatter, after=None):
    sizes = [len(g) for g in groups]
    srcs = [a for g in groups for a in g]
    n = len(srcs)
    slot = lambda s: s.shape[1:] if scatter else s.shape
    lands = [lax.empty((N_DEV,) + slot(s), s.dtype) for s in srcs]
    n_after = 0 if after is None else 1
    n_grp = len(groups)

    def body(*refs):
        src_refs, land_refs = refs[:n], refs[n:2 * n]
        sems = refs[2 * n + n_after:2 * n + n_after + 2 * n_grp]
        token = refs[-1]
        lo = 0
        for gi, sz in enumerate(sizes):
            for cp in _push_copies(src_refs[lo:lo + sz], land_refs[lo:lo + sz], sems[2 * gi], sems[2 * gi + 1], scatter, False):
                cp.start()
            lo += sz
        token[...] = jnp.zeros_like(token)

    hbm = lambda a: pltpu.with_memory_space_constraint(a, pltpu.HBM)
    operands = [hbm(a) for a in srcs + lands] + ([after] if n_after else [])
    sem_shapes = [pltpu.SemaphoreType.DMA((sz * N_PEER,)) for sz in sizes for _ in range(2)]
    res = pl.pallas_call(
        body, name=name,
        out_shape=sem_shapes + [pltpu.HBM(a.shape, a.dtype) for a in srcs + lands] + [_sds((8, 128), F32)],
        in_specs=[HBM] * (2 * n) + [ANY] * n_after,
        out_specs=[SEM] * (2 * n_grp) + [HBM] * (2 * n) + [pl.BlockSpec(memory_space=pltpu.VMEM)],
        input_output_aliases={i: 2 * n_grp + i for i in range(2 * n)},
        compiler_params=pltpu.CompilerParams(has_side_effects=EFFECT),
    )(*operands)
    thru = res[2 * n_grp:2 * n_grp + 2 * n]
    handles, lo = [], 0
    for gi, sz in enumerate(sizes):
        handles.append((res[2 * gi], res[2 * gi + 1], list(thru[lo:lo + sz]), list(thru[n + lo:n + lo + sz]), scatter))
        lo += sz
    return handles, res[-1]


def _push_wait(name, handle, after):
    send_sems, recv_sems, srcs, lands, scatter = handle
    n = len(srcs)

    def body(*refs):
        src_refs, land_refs = refs[:n], refs[n:2 * n]
        s_sems, r_sems = refs[2 * n], refs[2 * n + 1]
        for cp in _push_copies(src_refs, land_refs, s_sems, r_sems, scatter, True):
            cp.wait_send()
            cp.wait_recv()

    res = pl.pallas_call(
        body, name=name,
        out_shape=[pltpu.HBM(a.shape, a.dtype) for a in srcs + lands],
        in_specs=[HBM] * (2 * n) + [SEM, SEM, ANY], out_specs=[HBM] * (2 * n),
        input_output_aliases={i: i for i in range(2 * n)},
        compiler_params=pltpu.CompilerParams(has_side_effects=EFFECT),
    )(*srcs, *lands, send_sems, recv_sems, after)
    return list(res[n:])


def _adamw(name, parts, w, m, v, own=None):
    r, c = w.shape
    br = _pick(r, 256, 16)
    has_own = own is not None

    def body(*refs):
        if has_own:
            p_ref, own_ref, w_ref, m_ref, v_ref, g_ref, d_ref, nm_ref, nv_ref = refs
            me = _my_id()
            mine = own_ref[...].astype(F32)
        else:
            p_ref, w_ref, m_ref, v_ref, g_ref, d_ref, nm_ref, nv_ref = refs
        g = None
        for k in range(N_DEV):
            t = p_ref[k].astype(F32)
            if has_own:
                t = jnp.where(me == k, mine, t)
            g = t if g is None else g + t
        mm = ADAM_B1 * m_ref[...] + (1.0 - ADAM_B1) * g
        vv = ADAM_B2 * v_ref[...] + (1.0 - ADAM_B2) * (g * g)
        m_hat = mm / (1.0 - ADAM_B1 ** ADAM_STEP)
        v_hat = vv / (1.0 - ADAM_B2 ** ADAM_STEP)
        g_ref[...] = g
        d_ref[...] = -ADAM_LR * (m_hat / (jnp.sqrt(v_hat) + ADAM_EPS) + ADAM_WD * w_ref[...])
        nm_ref[...] = mm
        nv_ref[...] = vv

    spec = _rows(br, c)
    out = (_sds((r, c), F32), spec)
    ins = [(parts, pl.BlockSpec((N_DEV, br, c), lambda i: (0, i, 0)))] + ([(own, spec)] if has_own else [])
    return _call(name, body, (r // br,), ins + [(w, spec), (m, spec), (v, spec)], [out] * 4, sem=("parallel",))


def _pad_head_cols(w, d):
    k = w.shape[0]
    return jnp.pad(w.reshape(k, HEADS, d), ((0, 0), (0, 0), (0, HP - d))).reshape(k, HW)


def _unpad_head_cols(wp, d):
    k = wp.shape[0]
    return wp.reshape(k, HEADS, HP)[:, :, :d].reshape(k, HEADS * d)


def _pad_head_rows(w, d):
    n = w.shape[1]
    return jnp.pad(w.reshape(HEADS, d, n), ((0, 0), (0, HP - d), (0, 0))).reshape(HW, n)


def _w_in_runs():
    nat = {}
    o = 0
    for nm, wd in (("q", Q_RANK), ("kv", KV_RANK), ("kr", ROPE), ("fq", FOX_W), ("fk", FOX_W), ("fv", FOX_W),
                   ("fl", HEADS), ("gate", 2 * D_MODEL)):
        nat[nm] = o
        o += wd
    runs = [(1, R_QLAT, nat["q"], Q_RANK, 1.0), (1, R_KVLAT, nat["kv"], KV_RANK, 1.0),
            (1, R_LAST + LANE_FL, nat["fl"], HEADS, 1.0), (1, R_LAST + LANE_PE, nat["kr"], ROPE, 1.0),
            (1, R_GATE, nat["gate"], 2 * D_MODEL, 1.0)]
    for grp, (nm, sc) in enumerate((("fq", FOX_SCALE), ("fk", 1.0), ("fv", 1.0))):
        runs.append((0, grp * FOX_W, nat[nm], FOX_W, sc))
    return runs


def _head_pad_moves(pad):
    moves = []
    for grp in range(3):
        for h in range(HEADS):
            narrow, wide = grp * FOX_W + h * FOX_DIM, h * HP
            if pad:
                moves.append((0, None, grp * HW + wide, 0, None, narrow, FOX_DIM, 1.0))
            else:
                moves.append((0, None, narrow, grp, None, wide, FOX_DIM, 1.0))
    return moves


def _sharded_runs(runs, shard_cols):
    out = []
    for half, col, ncol, width, sc in runs:
        while width > 0:
            d, local = divmod(ncol, shard_cols)
            wd = min(width, shard_cols - local)
            out.append((half, col, d, local, wd, sc))
            col, ncol, width = col + wd, ncol + wd, width - wd
    return out


def _remap(name, srcs, out_shapes, moves):
    rows = srcs[0].shape[-2]
    br = _pick(rows, 256, 16)
    ns = len(srcs)

    def spec(shape):
        if len(shape) == 2:
            return pl.BlockSpec((br, shape[1]), lambda i: (i, 0))
        return pl.BlockSpec((shape[0], br, shape[2]), lambda i: (0, i, 0))

    covered = [sum(m[6] for m in moves if m[0] == di) for di in range(len(out_shapes))]
    has_gaps = [cov < (shape[1] if len(shape) == 2 else shape[0] * shape[2])
                for cov, (shape, _) in zip(covered, out_shapes)]

    def body(*refs):
        s_refs, o_refs = refs[:ns], refs[ns:]
        for o, gaps in zip(o_refs, has_gaps):
            if gaps:
                o[...] = jnp.zeros_like(o)
        for di, dl, dc, si, sl, sc0, wd, scale in moves:
            v = s_refs[si][:, sc0:sc0 + wd] if sl is None else s_refs[si][sl, :, sc0:sc0 + wd]
            if scale != 1.0:
                v = v * jnp.asarray(scale, v.dtype)
            v = v.astype(o_refs[di].dtype)
            if dl is None:
                o_refs[di][:, dc:dc + wd] = v
            else:
                o_refs[di][dl, :, dc:dc + wd] = v

    return _call(name, body, (rows // br,), [(a, spec(a.shape)) for a in srcs],
                 [(_sds(shape, dt), spec(shape)) for shape, dt in out_shapes], sem=("parallel",))


def _w_in_from_shards(g3):
    n, rows, c = g3.shape
    moves = [(half, None, col, 0, d, local, wd, sc) for half, col, d, local, wd, sc in _sharded_runs(_w_in_runs(), c)]
    return _remap("w_in_repack", [g3], [((rows, F_W), g3.dtype), ((rows, R_W), g3.dtype)], moves)


def _w_in_grad_to_shards(d_fused, d_rest, n, c):
    rows = d_fused.shape[0]
    moves = [(0, d, local, half, None, col, wd, sc) for half, col, d, local, wd, sc in _sharded_runs(_w_in_runs(), c)]
    return _remap("w_in_grad_unpack", [d_fused, d_rest], [((n, rows, c), d_fused.dtype)], moves)[0]


def _rows_from_shards(name, land, own):
    n, rows, c = land.shape

    def body(land_ref, own_ref, o_ref):
        o_ref[...] = jnp.where(_my_id() == pl.program_id(0), own_ref[...], land_ref[...])

    return _call(name, body, (n,),
                 [(land, pl.BlockSpec((None, rows, c), lambda d: (d, 0, 0))), (own, _whole((rows, c)))],
                 [(_sds((n * rows, c), land.dtype), pl.BlockSpec((rows, c), lambda d: (d, 0)))], sem=("parallel",))[0]


def _cols_from_shards(name, land, own):
    n, rows, c = land.shape
    br = _pick(rows, 256, 16)

    def body(land_ref, own_ref, o_ref):
        me = _my_id()
        for d in range(n):
            o_ref[:, c * d:c * (d + 1)] = jnp.where(me == d, own_ref[...], land_ref[d])

    return _call(name, body, (rows // br,),
                 [(land, pl.BlockSpec((n, br, c), lambda i: (0, i, 0))), (own, _rows(br, c))],
                 [(_sds((rows, n * c), land.dtype), _rows(br, n * c))], sem=("parallel",))[0]


def _cols_to_shards(name, full, n):
    rows, nc = full.shape
    c = nc // n
    return _remap(name, [full], [((n, rows, c), full.dtype)], [(0, d, 0, 0, None, c * d, c, 1.0) for d in range(n)])[0]


def _split_w_kv(w):
    k = w.shape[0]
    w3 = w.reshape(k, HEADS, NOPE + V_DIM)
    padl = lambda a: jnp.pad(a, ((0, 0), (0, 0), (0, HP - a.shape[-1]))).reshape(k, HW)
    return padl(w3[..., :NOPE]), padl(w3[..., NOPE:])


def _merge_w_kv(wk, wv):
    k = wk.shape[0]
    return jnp.concatenate([wk.reshape(k, HEADS, HP)[..., :NOPE], wv.reshape(k, HEADS, HP)[..., :V_DIM]],
                           axis=-1).reshape(k, HEADS * (NOPE + V_DIM))


class _NoComm:
    first_token = ()

    def late_weights(self, group, after):
        return {}

    def send(self, name, grads):
        return ()


def _local_step(x, tgt, p, comm=_NoComm()):
    seq = x.shape[0]
    r = -(-(N_META + seq) // ROW_ALIGN) * ROW_ALIGN
    cd = MXU_DTYPE
    p = dict(p)

    w_f, w_r = p["w_in"]

    pos = jnp.arange(r, dtype=F32)
    inv_freq = ROPE_THETA ** (-jnp.arange(HALF, dtype=F32) / HALF)
    ang = pos[:, None] * inv_freq[None, :]
    cos_t = jnp.tile(jnp.cos(ang), (1, HP // HALF))
    sin_t = jnp.tile(jnp.sin(ang), (1, HP // HALF))
    bf_row = jnp.zeros((1, HP), F32).at[0, LANE_FL:LANE_FL + HEADS].set(p["b_forget"])

    h0, h0b = _ln_emb_fwd(x, p["meta_tokens"], p["ln_emb_g"], p["ln_emb_b"], r, after=comm.first_token)
    proj_f = _matmul("in_proj_f", h0b, w_f, out_dtype=cd)
    proj_f = _remap("proj_f_pad", [proj_f], [((r, 3 * HW), cd)], _head_pad_moves(True))[0]
    proj_r = _matmul("in_proj_r", h0b, w_r)
    latent_gains = (p["q_norm_g"], p["kv_norm_g"])
    ql, kvl = _latent_norm_fwd(proj_r, latent_gains)
    p.update(comm.late_weights("qkv", ql))
    w_q = _pad_head_cols(p["w_q_up"], QK_DIM)
    w_kv = jnp.concatenate(_split_w_kv(p["w_kv_up"]), axis=1)
    q_raw = _matmul("q_up", ql, w_q)
    kv = _matmul("kv_up", kvl, w_kv, out_dtype=cd)
    q_mla, k_mla = _rope_fwd(q_raw, kv, proj_r, cos_t, sin_t)
    o_mla, o_mla_b, lse_mla = _attn_fwd("mla_fwd", (q_mla, 0), (k_mla, 0), (kv, 1))

    cum, cum_t = _forget_fwd(proj_r, bf_row)
    o_fox, o_fox_b, lse_fox = _attn_fwd("fox_fwd", (proj_f, 0), (proj_f, 1), (proj_f, 2), cum, cum_t)

    p.update(comm.late_weights("mix", o_fox_b))
    w_bm = _pad_head_rows(p["w_branch_mla"], V_DIM)
    w_bf = _pad_head_rows(p["w_branch_fox"], FOX_DIM)
    bm = _matmul("branch_mla", o_mla_b, p["w_branch_mla"], out_dtype=cd)
    bfx = _matmul("branch_fox", o_fox_b, p["w_branch_fox"], out_dtype=cd)
    merged = _gate_fwd(proj_r, p["b_gate"], bm, bfx)
    mix = _matmul("out_proj", merged, p["w_out"])
    h1, h1b = _ln_fwd("ln_mix_fwd", h0, mix, p["ln_mix_g"], p["ln_mix_b"])
    p.update(comm.late_weights("ffn", h1b))
    up = _matmul("ffn_up", h1b, p["w_ffn_up"], out_dtype=cd)
    act = _glu_fwd(up, p["conv_w"], p["conv_b"])
    f = _matmul("ffn_down", act, p["w_ffn_down"])
    loss = _ln_ffn_loss(h1, f, tgt, p["ln_ffn_g"], p["ln_ffn_b"])

    g = {}
    dz2, dz2b, g["ln_ffn_g"], g["ln_ffn_b"] = _ln_ffn_bwd(h1, f, tgt, p["ln_ffn_g"], p["ln_ffn_b"])
    d_act = _matmul("ffn_down_dx", dz2b, p["w_ffn_down"], tb=True, out_dtype=cd)
    g["w_ffn_down"] = _matmul("ffn_down_dw", act, dz2b, ta=True, out_dtype=cd)
    d_up, dcw, g["conv_b"] = _glu_bwd(up, p["conv_w"], p["conv_b"], d_act)
    g["conv_w"] = dcw[:3]
    dh1 = _matmul("ffn_up_dx", d_up, p["w_ffn_up"], tb=True, addend=dz2, alpha=ALPHA)
    g["w_ffn_up"] = _matmul("ffn_up_dw", h1b, d_up, ta=True, out_dtype=cd)
    sent = comm.send("ffn", {n: g[n] for n in ("w_ffn_down", "w_ffn_up", "conv_w")})
    dz1, dz1b, g["ln_mix_g"], g["ln_mix_b"] = _ln_bwd("ln_mix_bwd", h0, mix, dh1, p["ln_mix_g"], after=sent)
    dmerged = _matmul("out_proj_dx", dz1b, p["w_out"], tb=True, out_dtype=cd)
    g["w_out"] = _matmul("out_proj_dw", merged, dz1b, ta=True, out_dtype=cd)
    d_bm, d_bf, d_gl, g["b_gate"] = _gate_bwd(proj_r, p["b_gate"], bm, bfx, dmerged)
    do_mla_b = _matmul("branch_mla_dx", d_bm, w_bm, tb=True, out_dtype=cd)
    g["w_branch_mla"] = _matmul("branch_mla_dw", o_mla_b, d_bm, ta=True, out_dtype=cd)
    do_fox_b = _matmul("branch_fox_dx", d_bf, w_bf, tb=True, out_dtype=cd)
    g["w_branch_fox"] = _matmul("branch_fox_dw", o_fox_b, d_bf, ta=True, out_dtype=cd)

    sent = comm.send("mix", {n: g[n] for n in ("w_out", "w_branch_mla", "w_branch_fox")})
    dq_m, dk_m, dv_m = _attn_bwd("mla_bwd", (q_mla, 0), (k_mla, 0), (kv, 1), do_mla_b, o_mla, lse_mla, after=sent)
    dfq, dfk, dfv, dcq, dck = _attn_bwd("fox_bwd", (proj_f, 0), (proj_f, 1), (proj_f, 2), do_fox_b, o_fox, lse_fox,
                                        cum, cum_t, out_dtype=cd)
    dfl, dbf = _forget_bwd(proj_r, bf_row, dcq, dck)
    g["b_forget"] = dbf[:, LANE_FL:LANE_FL + HEADS]

    dq_b, dkv_b, dlast = _rope_bwd(dq_m, dk_m, dv_m, dfl, cos_t, sin_t)
    d_ql = _matmul("q_up_dx", dq_b, w_q, tb=True)
    d_kvl = _matmul("kv_up_dx", dkv_b, w_kv, tb=True)
    d_qlat, d_kvlat, g["q_norm_g"], g["kv_norm_g"] = _latent_norm_bwd(proj_r, (d_ql, d_kvl), latent_gains)
    side_by_side = lambda parts, cols: [(0, None, c0, si, None, 0, a.shape[1], 1.0) for si, (a, c0) in enumerate(zip(parts, cols))]
    dproj_f = _remap("dproj_f_pack", [dfq, dfk, dfv], [((r, F_W), cd)], _head_pad_moves(False))[0]
    rest_parts = [d_qlat, d_kvlat, dlast, d_gl]
    dproj_r = _remap("dproj_r_pack", rest_parts, [((r, R_W), cd)],
                     side_by_side(rest_parts, (R_QLAT, R_KVLAT, R_LAST, R_GATE)))[0]
    g["w_in"] = (_matmul("in_proj_f_dw", h0b, dproj_f, ta=True, out_dtype=cd),
                 _matmul("in_proj_r_dw", h0b, dproj_r, ta=True, out_dtype=cd))
    sent = comm.send("in", {"w_in": g["w_in"]})
    dh0 = _matmul("in_proj_f_dx", dproj_f, w_f, tb=True, addend=dz1, alpha=ALPHA, after=sent)
    g["w_q_up"] = _unpad_head_cols(_matmul("q_up_dw", ql, dq_b, ta=True, out_dtype=cd, after=sent), QK_DIM)
    dw_kv = _matmul("kv_up_dw", kvl, dkv_b, ta=True, out_dtype=cd, after=sent)
    g["w_kv_up"] = _merge_w_kv(dw_kv[:, :HW], dw_kv[:, HW:])
    sent = comm.send("qkv", {n: g[n] for n in ("w_q_up", "w_kv_up")})
    dh0 = _matmul("in_proj_r_dx", dproj_r, w_r, tb=True, addend=dh0, after=sent)
    grad_x, d_meta, g["ln_emb_g"], g["ln_emb_b"] = _ln_emb_bwd(x, p["meta_tokens"], dh0, p["ln_emb_g"])
    return loss, grad_x, d_meta, g


BIG = (("w_in", 1), ("w_q_up", 1), ("w_kv_up", 1), ("w_branch_mla", 1), ("w_branch_fox", 1), ("w_out", 0),
       ("w_ffn_up", 1), ("w_ffn_down", 0))
SMALL_SHARDED = (("meta_tokens", 1), ("conv_w", 1))
EARLY = ("w_in", "meta_tokens")
LATE = {"qkv": ("w_q_up", "w_kv_up", "conv_w"),
        "mix": ("w_branch_mla", "w_branch_fox", "w_out"),
        "ffn": ("w_ffn_up", "w_ffn_down")}
REPLICATED = ("ln_emb_g", "ln_emb_b", "b_gate", "b_forget", "q_norm_g", "kv_norm_g", "ln_mix_g", "ln_mix_b",
              "conv_b", "ln_ffn_g", "ln_ffn_b")
PACK_COLS = 1024


def _pack(flat_list):
    cat = jnp.concatenate(flat_list)
    n = cat.shape[0]
    rows = -(-n // (8 * PACK_COLS)) * 8
    return jnp.pad(cat, (0, rows * PACK_COLS - n)).reshape(rows, PACK_COLS)


def _gathered_full(g3, axis):
    n, r, c = g3.shape
    if axis == 0:
        return g3.reshape(n * r, c)
    return g3.transpose(1, 0, 2).reshape(r, n * c)


def _shard_major(full, axis):
    r, c = full.shape
    if axis == 0:
        return full.reshape(N_DEV, r // N_DEV, c)
    return full.reshape(r, N_DEV, c // N_DEV).transpose(1, 0, 2)


def kernel(x, meta_tokens, ln_emb_g, ln_emb_b, w_in, b_gate, b_forget, q_norm_g, w_q_up, kv_norm_g, w_kv_up, w_branch_mla, w_branch_fox, w_out, ln_mix_g, ln_mix_b, w_ffn_up, conv_w, conv_b, w_ffn_down, ln_ffn_g, ln_ffn_b, loss_target, m_meta_tokens, m_ln_emb_g, m_ln_emb_b, m_w_in, m_b_gate, m_b_forget, m_q_norm_g, m_w_q_up, m_kv_norm_g, m_w_kv_up, m_w_branch_mla, m_w_branch_fox, m_w_out, m_ln_mix_g, m_ln_mix_b, m_w_ffn_up, m_conv_w, m_conv_b, m_w_ffn_down, m_ln_ffn_g, m_ln_ffn_b, v_meta_tokens, v_ln_emb_g, v_ln_emb_b, v_w_in, v_b_gate, v_b_forget, v_q_norm_g, v_w_q_up, v_kv_norm_g, v_w_kv_up, v_w_branch_mla, v_w_branch_fox, v_w_out, v_ln_mix_g, v_ln_mix_b, v_w_ffn_up, v_conv_w, v_conv_b, v_w_ffn_down, v_ln_ffn_g, v_ln_ffn_b):
    names = ("meta_tokens", "ln_emb_g", "ln_emb_b", "w_in", "b_gate", "b_forget", "q_norm_g", "w_q_up", "kv_norm_g",
             "w_kv_up", "w_branch_mla", "w_branch_fox", "w_out", "ln_mix_g", "ln_mix_b", "w_ffn_up", "conv_w", "conv_b",
             "w_ffn_down", "ln_ffn_g", "ln_ffn_b")
    w_args = (meta_tokens, ln_emb_g, ln_emb_b, w_in, b_gate, b_forget, q_norm_g, w_q_up, kv_norm_g, w_kv_up,
              w_branch_mla, w_branch_fox, w_out, ln_mix_g, ln_mix_b, w_ffn_up, conv_w, conv_b, w_ffn_down, ln_ffn_g, ln_ffn_b)
    m_args = (m_meta_tokens, m_ln_emb_g, m_ln_emb_b, m_w_in, m_b_gate, m_b_forget, m_q_norm_g, m_w_q_up, m_kv_norm_g,
              m_w_kv_up, m_w_branch_mla, m_w_branch_fox, m_w_out, m_ln_mix_g, m_ln_mix_b, m_w_ffn_up, m_conv_w, m_conv_b,
              m_w_ffn_down, m_ln_ffn_g, m_ln_ffn_b)
    v_args = (v_meta_tokens, v_ln_emb_g, v_ln_emb_b, v_w_in, v_b_gate, v_b_forget, v_q_norm_g, v_w_q_up, v_kv_norm_g,
              v_w_kv_up, v_w_branch_mla, v_w_branch_fox, v_w_out, v_ln_mix_g, v_ln_mix_b, v_w_ffn_up, v_conv_w, v_conv_b,
              v_w_ffn_down, v_ln_ffn_g, v_ln_ffn_b)
    as2d = lambda a: a.reshape((-1, a.shape[-1])) if a.ndim != 1 else a.reshape(1, -1)
    w = {n: as2d(a) for n, a in zip(names, w_args)}
    m = {n: as2d(a) for n, a in zip(names, m_args)}
    v = {n: as2d(a) for n, a in zip(names, v_args)}
    out_shape = {n: a.shape for n, a in zip(names, w_args)}

    axis_of = dict(BIG + SMALL_SHARDED)
    big = set(n for n, _ in BIG)
    wire = lambda n, a: a.astype(MXU_DTYPE) if n in big else a
    my_id = _my_id()

    early = _allgather("gather_early", [wire(n, w[n]) for n in EARLY])
    p = {n: _gathered_full(g3, axis_of[n]) for n, g3 in zip(EARLY, early) if n != "w_in"}
    p["w_in"] = _w_in_from_shards(early[EARLY.index("w_in")])
    for n in REPLICATED:
        p[n] = w[n].reshape(-1)
    late_src = [[wire(n, w[n]) for n in members] for members in LATE.values()]
    late_handles, late_token = _push_start("gather_late_start", late_src, False, after=early[0])
    late = {group: (members, src, handle)
            for (group, members), src, handle in zip(LATE.items(), late_src, late_handles)}
    sent = {}

    class Comm:
        first_token = (late_token,)

        def late_weights(self, group, after):
            members, src, handle = late[group]
            lands = _push_wait("gather_" + group + "_wait", handle, after)
            out = {}
            for n, own, land in zip(members, src, lands):
                if own.shape[0] % 16:
                    out[n] = _gathered_full(lax.dynamic_update_index_in_dim(land, own, my_id, 0), axis_of[n])
                elif axis_of[n] == 1:
                    out[n] = _cols_from_shards(n + "_repack", land, own)
                else:
                    out[n] = _rows_from_shards(n + "_repack", land, own)
            return out

        def send(self, name, grads):
            names_ = tuple(grads)
            parts = []
            for n in names_:
                if n == "w_in":
                    parts.append(_w_in_grad_to_shards(*grads[n], N_DEV, w[n].shape[1]))
                elif n == "w_ffn_up":
                    parts.append(_cols_to_shards(n + "_grad_unpack", grads[n], N_DEV))
                else:
                    parts.append(_shard_major(grads[n], axis_of[n]).astype(MXU_DTYPE))
            (handle,), token = _push_start("send_" + name + "_start", [parts], True)
            sent[name] = (names_, parts, handle)
            return (token,)

    loss_part, grad_x, d_meta, g = _local_step(x[0], loss_target[0], p, Comm())
    grad_x = grad_x[None]

    small = _pack([d_meta.reshape(-1)] + [g[n].reshape(-1) for n in REPLICATED] + [loss_part.reshape(-1)])
    (small_handle,), small_token = _push_start("send_small_start", [[small]], False)

    res = {}
    prev = small_token
    for name, (names_, parts, handle) in sent.items():
        lands = _push_wait("send_" + name + "_wait", handle, prev)
        for n, part, land in zip(names_, parts, lands):
            own = lax.dynamic_index_in_dim(part, my_id, axis=0, keepdims=False)
            res[n] = _adamw("adamw_" + n, land, w[n], m[n], v[n], own=own)
            prev = res[n][0]
    small_all = _push_wait("send_small_wait", small_handle, prev)[0]
    head = jnp.zeros((d_meta.size,), F32)
    rep_w = _pack([head] + [w[n].reshape(-1) for n in REPLICATED])
    rep_m = _pack([head] + [m[n].reshape(-1) for n in REPLICATED])
    rep_v = _pack([head] + [v[n].reshape(-1) for n in REPLICATED])
    rep_res = _adamw("adamw_replicated", small_all, rep_w, rep_m, rep_v, own=small)
    off = d_meta.size
    for n in REPLICATED:
        sz = w[n].size
        res[n] = tuple(a.reshape(-1)[off:off + sz] for a in rep_res)
        off += sz
    loss = rep_res[0].reshape(-1)[off]
    cols = w["meta_tokens"].shape[1]
    meta_rows = lambda a: a.reshape(a.shape[:-2] + (-1,))[..., :d_meta.size].reshape(a.shape[:-2] + d_meta.shape)
    my_cols = lambda a: lax.dynamic_slice_in_dim(a, my_id * cols, cols, axis=a.ndim - 1)
    res["meta_tokens"] = _adamw("adamw_meta_tokens", my_cols(meta_rows(small_all)), w["meta_tokens"],
                                m["meta_tokens"], v["meta_tokens"], own=my_cols(d_meta))

    outs = [loss, grad_x]
    for idx in range(4):
        outs += [res[n][idx].reshape(out_shape[n]) for n in names]
    return tuple(outs)
```

```python
import jax
import jax.numpy as jnp
from jax import lax
from jax.experimental import pallas as pl
from jax.experimental.pallas import tpu as pltpu

F32 = jnp.float32
BF16 = jnp.bfloat16
MXU_DTYPE = BF16

N_DEV = 8
N_META = 16
D_MODEL = 1024
HEADS = 8
Q_RANK = 384
KV_RANK = 128
NOPE = 64
ROPE = 32
HALF = ROPE // 2
QK_DIM = NOPE + ROPE
V_DIM = 64
FOX_DIM = 64
FOX_W = HEADS * FOX_DIM
D_FF = 2816
ROPE_THETA = 10000.0
LN_EPS = 1e-5
RMS_EPS = 1e-6
ALPHA = 2.0 ** 0.25
MLA_SCALE = QK_DIM ** -0.5
FOX_SCALE = FOX_DIM ** -0.5
NEG_INF = -1e30

HP = 128
HW = HEADS * HP
F_W = 3 * FOX_W
R_GATE = 0
R_KVLAT = R_GATE + 2 * D_MODEL
R_LAST = R_KVLAT + KV_RANK
R_QLAT = R_LAST + HP
R_W = R_QLAT + Q_RANK
assert R_QLAT % Q_RANK == 0 and R_KVLAT % KV_RANK == 0 and R_GATE % D_MODEL == 0 and R_W % HP == 0
LANE_FL = 0
LANE_PE = NOPE

ADAM_LR = 0.001
ADAM_B1 = 0.9
ADAM_B2 = 0.999
ADAM_EPS = 1e-08
ADAM_WD = 0.01
ADAM_STEP = 10

ROW_BLOCK = 256
ATT_TQ = 768
ATT_TK = 768
ATT_HEADS = 2
ATT_HEADS_FWD = 4
ROW_ALIGN = 768
MM_BLOCK_CAP = 1408
VMEM_LIMIT = 56 * 1024 * 1024
HIGHEST = lax.Precision.HIGHEST
NT = (((1,), (1,)), ((), ()))
TN = (((0,), (0,)), ((), ()))


def _params(sem=None):
    return pltpu.CompilerParams(dimension_semantics=sem, vmem_limit_bytes=VMEM_LIMIT)


def _call(name, body, grid, ins, outs, scratch=(), sem=None, after=()):
    n_in = len(ins)
    n_tok = len(after)

    def run(*refs):
        body(*refs[:n_in], *refs[n_in + n_tok:])

    tok_spec = pl.BlockSpec((8, 128), lambda *_: (0, 0))
    return pl.pallas_call(
        run, name=name, grid=grid,
        in_specs=[s for _, s in ins] + [tok_spec] * n_tok,
        out_specs=[s for _, s in outs],
        out_shape=[o for o, _ in outs],
        scratch_shapes=list(scratch),
        compiler_params=_params(sem),
    )(*[a for a, _ in ins], *after)


def _sds(shape, dtype):
    return jax.ShapeDtypeStruct(shape, dtype)


def _rows(br, c, cb=0):
    return pl.BlockSpec((br, c), lambda i: (i, cb))


def _whole(shape):
    n = len(shape)
    return pl.BlockSpec(shape, lambda i: (0,) * n)


def _pick(dim, cap, mult):
    best = None
    d = mult
    while d <= min(dim, cap):
        if dim % d == 0:
            best = d
        d += mult
    return best if best is not None else dim


def _hs(h):
    return slice(h * HP, (h + 1) * HP)


def _matmul(name, a, b, *, ta=False, tb=False, out_dtype=F32, addend=None, alpha=1.0, after=()):
    if ta:
        k, m = a.shape
    else:
        m, k = a.shape
    if tb:
        n, k2 = b.shape
    else:
        k2, n = b.shape
    assert k == k2, (name, a.shape, b.shape)
    bm = _pick(m, MM_BLOCK_CAP, 128 if ta else 16)
    bn = _pick(n, MM_BLOCK_CAP, 128)
    bk = _pick(k, MM_BLOCK_CAP, 128 if (not ta or tb) else 16)
    nk = k // bk
    dims = (((0 if ta else 1,), (1 if tb else 0,)), ((), ()))
    has_add = addend is not None

    def body(*refs):
        a_ref, b_ref = refs[:2]
        add_ref = refs[2] if has_add else None
        o_ref = refs[3 if has_add else 2]

        def finish(r):
            if has_add:
                r = r + alpha * add_ref[...]
            o_ref[...] = r.astype(o_ref.dtype)

        part = lax.dot_general(a_ref[...], b_ref[...], dims, preferred_element_type=F32)
        if nk == 1:
            finish(part)
            return
        acc_ref = refs[-1]
        kk = pl.program_id(2)

        @pl.when(kk == 0)
        def _():
            acc_ref[...] = part

        @pl.when(kk > 0)
        def _():
            acc_ref[...] += part

        @pl.when(kk == nk - 1)
        def _():
            finish(acc_ref[...])

    a_spec = pl.BlockSpec((bk, bm), lambda i, j, l: (l, i)) if ta else pl.BlockSpec((bm, bk), lambda i, j, l: (i, l))
    b_spec = pl.BlockSpec((bn, bk), lambda i, j, l: (j, l)) if tb else pl.BlockSpec((bk, bn), lambda i, j, l: (l, j))
    o_spec = pl.BlockSpec((bm, bn), lambda i, j, l: (i, j))
    ins = [(a, a_spec), (b, b_spec)]
    if has_add:
        ins.append((addend, o_spec))
    return _call(name, body, (m // bm, n // bn, nk), ins, [(_sds((m, n), out_dtype), o_spec)],
                 scratch=[pltpu.VMEM((bm, bn), F32)] if nk > 1 else [],
                 sem=("parallel", "parallel", "arbitrary"), after=after)[0]


def _ln_stats(z):
    mu = jnp.mean(z, axis=-1, keepdims=True)
    zc = z - mu
    var = jnp.mean(zc * zc, axis=-1, keepdims=True)
    rstd = lax.rsqrt(var + LN_EPS)
    return zc * rstd, rstd


def _ln_fwd(name, a, res, g, b, after=()):
    r, d = a.shape
    br = ROW_BLOCK
    has_res = res is not None

    def body(*refs):
        if has_res:
            a_ref, r_ref, g_ref, b_ref, y_ref, yb_ref = refs
            z = ALPHA * a_ref[...] + r_ref[...]
        else:
            a_ref, g_ref, b_ref, y_ref, yb_ref = refs
            z = a_ref[...]
        xhat, _ = _ln_stats(z)
        y = xhat * g_ref[...] + b_ref[...]
        y_ref[...] = y
        yb_ref[...] = y.astype(yb_ref.dtype)

    ins = [(a, _rows(br, d))]
    if has_res:
        ins.append((res, _rows(br, d)))
    ins += [(g.reshape(1, d), _whole((1, d))), (b.reshape(1, d), _whole((1, d)))]
    outs = [(_sds((r, d), F32), _rows(br, d)), (_sds((r, d), MXU_DTYPE), _rows(br, d))]
    return _call(name, body, (r // br,), ins, outs, sem=("parallel",), after=after)


def _ln_bwd(name, a, res, dy, g, after=()):
    r, d = a.shape
    br = ROW_BLOCK
    has_res = res is not None

    def body(*refs):
        if has_res:
            a_ref, r_ref, dy_ref, g_ref, dz_ref, dzb_ref, dg_ref, db_ref = refs
            z = ALPHA * a_ref[...] + r_ref[...]
        else:
            a_ref, dy_ref, g_ref, dz_ref, dzb_ref, dg_ref, db_ref = refs
            z = a_ref[...]
        xhat, rstd = _ln_stats(z)
        dyv = dy_ref[...]
        dyg = dyv * g_ref[...]
        m1 = jnp.mean(dyg, axis=-1, keepdims=True)
        m2 = jnp.mean(dyg * xhat, axis=-1, keepdims=True)
        dz = rstd * (dyg - m1 - xhat * m2)
        dz_ref[...] = dz
        dzb_ref[...] = dz.astype(dzb_ref.dtype)

        @pl.when(pl.program_id(0) == 0)
        def _():
            dg_ref[...] = jnp.zeros_like(dg_ref)
            db_ref[...] = jnp.zeros_like(db_ref)

        dg_ref[...] += jnp.sum(dyv * xhat, axis=0, keepdims=True)
        db_ref[...] += jnp.sum(dyv, axis=0, keepdims=True)

    ins = [(a, _rows(br, d))]
    if has_res:
        ins.append((res, _rows(br, d)))
    ins += [(dy, _rows(br, d)), (g.reshape(1, d), _whole((1, d)))]
    outs = [(_sds((r, d), F32), _rows(br, d)), (_sds((r, d), MXU_DTYPE), _rows(br, d)),
            (_sds((1, d), F32), _whole((1, d))), (_sds((1, d), F32), _whole((1, d)))]
    return _call(name, body, (r // br,), ins, outs, sem=("arbitrary",), after=after)


LATENTS = ((R_QLAT // Q_RANK, Q_RANK), (R_KVLAT // KV_RANK, KV_RANK))


def _latent_norm_fwd(proj_r, gains):
    r = proj_r.shape[0]
    br = ROW_BLOCK

    def body(xq_ref, xk_ref, gq_ref, gk_ref, yq_ref, yk_ref):
        for x_ref, g_ref, y_ref in ((xq_ref, gq_ref, yq_ref), (xk_ref, gk_ref, yk_ref)):
            x = x_ref[...]
            rstd = lax.rsqrt(jnp.mean(x * x, axis=-1, keepdims=True) + RMS_EPS)
            y_ref[...] = (x * rstd * g_ref[...]).astype(y_ref.dtype)

    return _call("latent_norm_fwd", body, (r // br,),
                 [(proj_r, _rows(br, wd, cb)) for cb, wd in LATENTS]
                 + [(g.reshape(1, wd), _whole((1, wd))) for g, (_, wd) in zip(gains, LATENTS)],
                 [(_sds((r, wd), MXU_DTYPE), _rows(br, wd)) for _, wd in LATENTS], sem=("parallel",))


def _latent_norm_bwd(proj_r, dys, gains):
    r = proj_r.shape[0]
    br = ROW_BLOCK

    def body(xq_ref, xk_ref, dq_ref, dk_ref, gq_ref, gk_ref, oq_ref, ok_ref, dgq_ref, dgk_ref):
        @pl.when(pl.program_id(0) == 0)
        def _():
            dgq_ref[...] = jnp.zeros_like(dgq_ref)
            dgk_ref[...] = jnp.zeros_like(dgk_ref)

        for x_ref, dy_ref, g_ref, dx_ref, dg_ref in ((xq_ref, dq_ref, gq_ref, oq_ref, dgq_ref),
                                                     (xk_ref, dk_ref, gk_ref, ok_ref, dgk_ref)):
            x = x_ref[...]
            rstd = lax.rsqrt(jnp.mean(x * x, axis=-1, keepdims=True) + RMS_EPS)
            nrm = x * rstd
            dyv = dy_ref[...]
            dyg = dyv * g_ref[...]
            dx_ref[...] = (rstd * (dyg - nrm * jnp.mean(dyg * nrm, axis=-1, keepdims=True))).astype(dx_ref.dtype)
            dg_ref[...] += jnp.sum(dyv * nrm, axis=0, keepdims=True)

    return _call("latent_norm_bwd", body, (r // br,),
                 [(proj_r, _rows(br, wd, cb)) for cb, wd in LATENTS]
                 + [(dy, _rows(br, wd)) for dy, (_, wd) in zip(dys, LATENTS)]
                 + [(g.reshape(1, wd), _whole((1, wd))) for g, (_, wd) in zip(gains, LATENTS)],
                 [(_sds((r, wd), MXU_DTYPE), _rows(br, wd)) for _, wd in LATENTS]
                 + [(_sds((1, wd), F32), _whole((1, wd))) for _, wd in LATENTS], sem=("arbitrary",))


def _lane_iota(shape):
    return lax.broadcasted_iota(jnp.int32, shape, 1)


def _rotary(t, c, s, lane, sign):
    second = pltpu.roll(t, HP - HALF, axis=1)
    first = pltpu.roll(t, HALF, axis=1)
    lo = (lane >= LANE_PE) & (lane < LANE_PE + HALF)
    hi = (lane >= LANE_PE + HALF) & (lane < LANE_PE + ROPE)
    return jnp.where(lo, t * c - sign * second * s, jnp.where(hi, t * c + sign * first * s, t))


def _rope_fwd(q_raw, k_part, proj_r, cos_t, sin_t):
    r = q_raw.shape[0]
    br = ROW_BLOCK

    def body(q_ref, k_ref, t_ref, c_ref, s_ref, qo_ref, ko_ref):
        c = c_ref[...]
        s = s_ref[...]
        lane = _lane_iota((br, HP))
        pe = (lane >= LANE_PE) & (lane < LANE_PE + ROPE)
        kp = jnp.where(pe, _rotary(t_ref[...], c, s, lane, 1.0), 0.0)
        for h in range(HEADS):
            qo_ref[:, _hs(h)] = (_rotary(q_ref[:, _hs(h)], c, s, lane, 1.0) * MLA_SCALE).astype(qo_ref.dtype)
            ko_ref[:, _hs(h)] = (k_ref[:, _hs(h)] + kp).astype(ko_ref.dtype)

    blk = _rows(br, HP)
    wide = _rows(br, HW)
    return _call("rope_fwd", body, (r // br,),
                 [(q_raw, wide), (k_part, wide), (proj_r, _rows(br, HP, R_LAST // HP)), (cos_t, blk), (sin_t, blk)],
                 [(_sds((r, HW), MXU_DTYPE), wide)] * 2, sem=("parallel",))


def _rope_bwd(dq, dk, dv, dfl, cos_t, sin_t):
    r = dq.shape[0]
    br = ROW_BLOCK

    def body(dq_ref, dk_ref, dv_ref, fl_ref, c_ref, s_ref, dqo_ref, dkv_ref, dl_ref):
        c = c_ref[...]
        s = s_ref[...]
        lane = _lane_iota((br, HP))
        pe = (lane >= LANE_PE) & (lane < LANE_PE + ROPE)
        acc = jnp.zeros((br, HP), F32)
        for h in range(HEADS):
            dqo_ref[:, _hs(h)] = (_rotary(dq_ref[:, _hs(h)], c, s, lane, -1.0) * MLA_SCALE).astype(dqo_ref.dtype)
            dkh = dk_ref[:, _hs(h)]
            acc = acc + dkh
            dkv_ref[:, _hs(h)] = dkh.astype(dkv_ref.dtype)
            dkv_ref[:, _hs(HEADS + h)] = dv_ref[:, _hs(h)].astype(dkv_ref.dtype)
        dl_ref[...] = (jnp.where(pe, _rotary(acc, c, s, lane, -1.0), 0.0) + fl_ref[...]).astype(dl_ref.dtype)

    blk = _rows(br, HP)
    wide = _rows(br, HW)
    return _call("rope_bwd", body, (r // br,),
                 [(dq, wide), (dk, wide), (dv, wide), (dfl, blk), (cos_t, blk), (sin_t, blk)],
                 [(_sds((r, HW), MXU_DTYPE), wide), (_sds((r, 2 * HW), MXU_DTYPE), _rows(br, 2 * HW)),
                  (_sds((r, HP), MXU_DTYPE), blk)],
                 sem=("parallel",))


def _log_sigmoid(x):
    return jnp.minimum(x, 0.0) - jnp.log(1.0 + jnp.exp(-jnp.abs(x)))


def _head_lane(x, h, lane):
    return jnp.sum(jnp.where(lane == h, x, 0.0), axis=1, keepdims=True)


def _forget_fwd(proj_r, bf_row):
    r = proj_r.shape[0]
    br = ROW_BLOCK

    def body(t_ref, b_ref, ob_ref, ot_ref, carry_ref):
        @pl.when(pl.program_id(0) == 0)
        def _():
            carry_ref[...] = jnp.zeros_like(carry_ref)

        x = t_ref[...] + b_ref[...]
        lane = _lane_iota(x.shape)
        lf = jnp.where((lane >= LANE_FL) & (lane < LANE_FL + HEADS), _log_sigmoid(x), 0.0)
        tri = (lax.broadcasted_iota(jnp.int32, (br, br), 0) >= lax.broadcasted_iota(jnp.int32, (br, br), 1)).astype(F32)
        cum = jnp.dot(tri, lf, precision=HIGHEST, preferred_element_type=F32) + carry_ref[0:1, :]
        for h in range(HEADS):
            ob_ref[:, _hs(h)] = jnp.broadcast_to(_head_lane(cum, LANE_FL + h, lane), (br, HP))
        ot_ref[...] = cum.T[LANE_FL:LANE_FL + HEADS, :]
        carry_ref[...] = jnp.broadcast_to(cum[br - 1:br, :], carry_ref.shape)

    return _call("forget_fwd", body, (r // br,),
                 [(proj_r, _rows(br, HP, R_LAST // HP)), (bf_row, _whole((1, HP)))],
                 [(_sds((r, HW), F32), _rows(br, HW)), (_sds((HEADS, r), F32), pl.BlockSpec((HEADS, br), lambda i: (0, i)))],
                 scratch=[pltpu.VMEM((8, HP), F32)], sem=("arbitrary",))


def _forget_bwd(proj_r, bf_row, dcq_t, dck_b):
    r = proj_r.shape[0]
    br = ROW_BLOCK
    nb = r // br

    def body(t_ref, b_ref, dcq_ref, dck_ref, o_ref, db_ref, carry_ref):
        @pl.when(pl.program_id(0) == 0)
        def _():
            carry_ref[...] = jnp.zeros_like(carry_ref)
            db_ref[...] = jnp.zeros_like(db_ref)

        lane = _lane_iota((br, HP))
        dc = jnp.concatenate([dcq_ref[...], jnp.zeros((HP - HEADS, br), F32)], axis=0).T
        for h in range(HEADS):
            dc = dc + jnp.where(lane == LANE_FL + h, dck_ref[:, h * HP:h * HP + 1], 0.0)
        triu = (lax.broadcasted_iota(jnp.int32, (br, br), 0) <= lax.broadcasted_iota(jnp.int32, (br, br), 1)).astype(F32)
        dlf = jnp.dot(triu, dc, precision=HIGHEST, preferred_element_type=F32) + carry_ref[0:1, :]
        carry_ref[...] = jnp.broadcast_to(dlf[0:1, :], carry_ref.shape)
        x = t_ref[...] + b_ref[...]
        dfl = jnp.where((lane >= LANE_FL) & (lane < LANE_FL + HEADS), dlf * jax.nn.sigmoid(-x), 0.0)
        o_ref[...] = dfl
        db_ref[...] += jnp.sum(dfl, axis=0, keepdims=True)

    rev = pl.BlockSpec((br, HP), lambda i: (nb - 1 - i, 0))
    return _call("forget_bwd", body, (nb,),
                 [(proj_r, pl.BlockSpec((br, HP), lambda i: (nb - 1 - i, R_LAST // HP))), (bf_row, _whole((1, HP))),
                  (dcq_t, pl.BlockSpec((HEADS, br), lambda i: (0, nb - 1 - i))),
                  (dck_b, pl.BlockSpec((br, HW), lambda i: (nb - 1 - i, 0)))],
                 [(_sds((r, HP), F32), rev), (_sds((1, HP), F32), _whole((1, HP)))],
                 scratch=[pltpu.VMEM((8, HP), F32)], sem=("arbitrary",))


def _gate_fwd(proj_r, b_gate, bm, bfx):
    r, d = bm.shape
    br = ROW_BLOCK
    cb = R_GATE // d

    def body(gm_ref, gf_ref, b1_ref, b2_ref, bm_ref, bf_ref, o_ref):
        g1 = jax.nn.sigmoid(gm_ref[...] + b1_ref[...])
        g2 = jax.nn.sigmoid(gf_ref[...] + b2_ref[...])
        o_ref[...] = (g1 * bm_ref[...].astype(F32) + g2 * bf_ref[...].astype(F32)).astype(o_ref.dtype)

    b1 = b_gate[:d].reshape(1, d)
    b2 = b_gate[d:].reshape(1, d)
    return _call("gate_fwd", body, (r // br,),
                 [(proj_r, _rows(br, d, cb)), (proj_r, _rows(br, d, cb + 1)), (b1, _whole((1, d))), (b2, _whole((1, d))),
                  (bm, _rows(br, d)), (bfx, _rows(br, d))],
                 [(_sds((r, d), MXU_DTYPE), _rows(br, d))], sem=("parallel",))[0]


def _gate_bwd(proj_r, b_gate, bm, bfx, dmerged):
    r, d = bm.shape
    br = ROW_BLOCK
    cb = R_GATE // d

    def body(gm_ref, gf_ref, b1_ref, b2_ref, bm_ref, bf_ref, dm_ref, dbm_ref, dbf_ref, dgl_ref, dbg_ref):
        g1 = jax.nn.sigmoid(gm_ref[...] + b1_ref[...])
        g2 = jax.nn.sigmoid(gf_ref[...] + b2_ref[...])
        dm = dm_ref[...].astype(F32)
        dbm_ref[...] = (dm * g1).astype(dbm_ref.dtype)
        dbf_ref[...] = (dm * g2).astype(dbf_ref.dtype)
        dl1 = dm * bm_ref[...].astype(F32) * (g1 * (1.0 - g1))
        dl2 = dm * bf_ref[...].astype(F32) * (g2 * (1.0 - g2))
        dgl_ref[:, 0:d] = dl1.astype(dgl_ref.dtype)
        dgl_ref[:, d:2 * d] = dl2.astype(dgl_ref.dtype)

        @pl.when(pl.program_id(0) == 0)
        def _():
            dbg_ref[...] = jnp.zeros_like(dbg_ref)

        dbg_ref[:, 0:d] += jnp.sum(dl1, axis=0, keepdims=True)
        dbg_ref[:, d:2 * d] += jnp.sum(dl2, axis=0, keepdims=True)

    b1 = b_gate[:d].reshape(1, d)
    b2 = b_gate[d:].reshape(1, d)
    return _call("gate_bwd", body, (r // br,),
                 [(proj_r, _rows(br, d, cb)), (proj_r, _rows(br, d, cb + 1)), (b1, _whole((1, d))), (b2, _whole((1, d))),
                  (bm, _rows(br, d)), (bfx, _rows(br, d)), (dmerged, _rows(br, d))],
                 [(_sds((r, d), MXU_DTYPE), _rows(br, d)), (_sds((r, d), MXU_DTYPE), _rows(br, d)),
                  (_sds((r, 2 * d), MXU_DTYPE), _rows(br, 2 * d)), (_sds((1, 2 * d), F32), _whole((1, 2 * d)))],
                 sem=("arbitrary",))


HALO = 16
GLU_BWD_BLOCK = 256
COPY_ROWS = 512


def _conv_taps(gp, halo, first_block):
    halo = jnp.where(first_block, 0.0, halo.astype(F32))
    rid = lax.broadcasted_iota(jnp.int32, gp.shape, 0)
    last, prev = halo[HALO - 1:HALO, :], halo[HALO - 2:HALO - 1, :]
    g1 = jnp.where(rid == 0, last, pltpu.roll(gp, 1, axis=0))
    g2 = jnp.where(rid == 0, prev, jnp.where(rid == 1, last, pltpu.roll(gp, 2, axis=0)))
    return g1, g2


def _prev_halo(br, c):
    return pl.BlockSpec((HALO, c), lambda i: (jnp.maximum(i * (br // HALO) - 1, 0), 0))


def _glu_fwd(up, conv_w, conv_b):
    r = up.shape[0]
    c = D_FF
    br = ROW_BLOCK

    def body(gp_ref, halo_ref, val_ref, w_ref, b_ref, o_ref):
        gp = gp_ref[...].astype(F32)
        g1, g2 = _conv_taps(gp, halo_ref[...], pl.program_id(0) == 0)
        gate = w_ref[0:1, :] * g2 + w_ref[1:2, :] * g1 + w_ref[2:3, :] * gp + b_ref[...]
        o_ref[...] = (gate * jax.nn.sigmoid(gate) * val_ref[...].astype(F32)).astype(o_ref.dtype)

    return _call("glu_fwd", body, (r // br,),
                 [(up, _rows(br, c, 0)), (up, _prev_halo(br, c)), (up, _rows(br, c, 1)),
                  (conv_w, _whole((3, c))), (conv_b.reshape(1, c), _whole((1, c)))],
                 [(_sds((r, c), MXU_DTYPE), _rows(br, c))], sem=("parallel",))[0]


def _glu_bwd(up, conv_w, conv_b, d_act):
    r = up.shape[0]
    c = D_FF
    br = GLU_BWD_BLOCK
    nb = r // br

    def body(gp_ref, halo_ref, val_ref, da_ref, gpn_ref, valn_ref, dan_ref, w_ref, b_ref, o_ref, dw_ref, db_ref):
        i = pl.program_id(0)
        w0, w1, w2, bias = w_ref[0:1, :], w_ref[1:2, :], w_ref[2:3, :], b_ref[...]

        def d_gate(gp, g1, g2, val, da):
            gate = w0 * g2 + w1 * g1 + w2 * gp + bias
            sg = jax.nn.sigmoid(gate)
            return da * val * (sg * (1.0 + gate * (1.0 - sg))), da * (gate * sg)

        gp = gp_ref[...].astype(F32)
        g1, g2 = _conv_taps(gp, halo_ref[...], i == 0)
        dg, dv = d_gate(gp, g1, g2, val_ref[...].astype(F32), da_ref[...].astype(F32))
        gpn = gpn_ref[...].astype(F32)
        g1n, g2n = _conv_taps(gpn, gp[br - HALO:, :], False)
        dgn, _ = d_gate(gpn, g1n, g2n, valn_ref[...].astype(F32), dan_ref[...].astype(F32))
        dgn = jnp.where(i == nb - 1, 0.0, dgn)
        rid = lax.broadcasted_iota(jnp.int32, dg.shape, 0)
        u1 = jnp.where(rid == br - 1, dgn[0:1, :], pltpu.roll(dg, br - 1, axis=0))
        u2 = jnp.where(rid == br - 1, dgn[1:2, :], jnp.where(rid == br - 2, dgn[0:1, :], pltpu.roll(dg, br - 2, axis=0)))
        o_ref[:, 0:c] = (w2 * dg + w1 * u1 + w0 * u2).astype(o_ref.dtype)
        o_ref[:, c:2 * c] = dv.astype(o_ref.dtype)

        @pl.when(i == 0)
        def _():
            dw_ref[...] = jnp.zeros_like(dw_ref)
            db_ref[...] = jnp.zeros_like(db_ref)

        dw_ref[0:1, :] += jnp.sum(dg * g2, axis=0, keepdims=True)
        dw_ref[1:2, :] += jnp.sum(dg * g1, axis=0, keepdims=True)
        dw_ref[2:3, :] += jnp.sum(dg * gp, axis=0, keepdims=True)
        db_ref[...] += jnp.sum(dg, axis=0, keepdims=True)

    nxt = lambda cb: pl.BlockSpec((HALO, c), lambda i: (jnp.minimum((i + 1) * (br // HALO), r // HALO - 1), cb))
    return _call("glu_bwd", body, (nb,),
                 [(up, _rows(br, c, 0)), (up, _prev_halo(br, c)), (up, _rows(br, c, 1)), (d_act, _rows(br, c)),
                  (up, nxt(0)), (up, nxt(1)), (d_act, nxt(0)),
                  (conv_w, _whole((3, c))), (conv_b.reshape(1, c), _whole((1, c)))],
                 [(_sds((r, 2 * c), MXU_DTYPE), _rows(br, 2 * c)),
                  (_sds((8, c), F32), _whole((8, c))), (_sds((1, c), F32), _whole((1, c)))],
                 sem=("arbitrary",))


def _token_specs(seq, d):
    br = ROW_BLOCK
    nxb = seq // br
    main = pl.BlockSpec((br, d), lambda i: (jnp.minimum(i, nxb - 1), 0))
    tail = pl.BlockSpec((N_META, d), lambda i: (jnp.clip(i * (br // N_META) - 1, 0, seq // N_META - 1), 0))
    return main, tail


def _padded_block(main_ref, tail_ref, first, seq):
    br = ROW_BLOCK
    i = pl.program_id(0)
    nxb = seq // br
    main = jnp.where(i < nxb, main_ref[...], 0.0)
    head = jnp.where(i == 0, first, jnp.where(i <= nxb, tail_ref[...], 0.0))
    return jnp.concatenate([head, main[:br - N_META]], axis=0)


def _ln_emb_fwd(x, meta, g, b, rows, after=()):
    seq, d = x.shape
    br = ROW_BLOCK
    assert seq % br == 0 and br % N_META == 0 and rows % br == 0

    def body(x_ref, tail_ref, meta_ref, g_ref, b_ref, y_ref, yb_ref):
        z = _padded_block(x_ref, tail_ref, meta_ref[...], seq)
        xhat, _ = _ln_stats(z)
        y = xhat * g_ref[...] + b_ref[...]
        y_ref[...] = y
        yb_ref[...] = y.astype(yb_ref.dtype)

    main, tail = _token_specs(seq, d)
    return _call("ln_emb_fwd", body, (rows // br,),
                 [(x, main), (x, tail), (meta, _whole((N_META, d))), (g.reshape(1, d), _whole((1, d))),
                  (b.reshape(1, d), _whole((1, d)))],
                 [(_sds((rows, d), F32), _rows(br, d)), (_sds((rows, d), MXU_DTYPE), _rows(br, d))],
                 sem=("parallel",), after=after)


def _ln_emb_bwd(x, meta, dh0, g):
    seq, d = x.shape
    br = ROW_BLOCK
    step = br // N_META

    def ln_bwd(z, dy, gv):
        xhat, rstd = _ln_stats(z)
        dyg = dy * gv
        m1 = jnp.mean(dyg, axis=-1, keepdims=True)
        m2 = jnp.mean(dyg * xhat, axis=-1, keepdims=True)
        dz = rstd * (dyg - m1 - xhat * m2)
        return dz, jnp.sum(dy * xhat, axis=0, keepdims=True), jnp.sum(dy, axis=0, keepdims=True)

    def body(x_ref, dh_ref, nxt_ref, meta_ref, top_ref, g_ref, dx_ref, dm_ref, dg_ref, db_ref):
        gv = g_ref[...]
        dy = jnp.concatenate([dh_ref[N_META:, :], nxt_ref[...]], axis=0)
        dz, dg, db = ln_bwd(x_ref[...], dy, gv)
        dx_ref[...] = dz

        @pl.when(pl.program_id(0) == 0)
        def _():
            dzm, dgm, dbm = ln_bwd(meta_ref[...], top_ref[...], gv)
            dm_ref[...] = dzm
            dg_ref[...] = dgm
            db_ref[...] = dbm

        dg_ref[...] += dg
        db_ref[...] += db

    small = _whole((N_META, d))
    return _call("ln_emb_bwd", body, (seq // br,),
                 [(x, _rows(br, d)), (dh0, _rows(br, d)), (dh0, pl.BlockSpec((N_META, d), lambda i: ((i + 1) * step, 0))),
                  (meta, small), (dh0, small), (g.reshape(1, d), _whole((1, d)))],
                 [(_sds((seq, d), F32), _rows(br, d)), (_sds((N_META, d), F32), small),
                  (_sds((1, d), F32), _whole((1, d))), (_sds((1, d), F32), _whole((1, d)))], sem=("arbitrary",))


def _ln_ffn_loss(h1, f, tgt, g, b):
    r, d = h1.shape
    seq = tgt.shape[0]
    br = ROW_BLOCK

    def body(a_ref, r_ref, t_ref, tail_ref, g_ref, b_ref, l_ref):
        err = _loss_err(a_ref, r_ref, t_ref, tail_ref, g_ref, b_ref, seq)[0]

        @pl.when(pl.program_id(0) == 0)
        def _():
            l_ref[...] = jnp.zeros_like(l_ref)

        l_ref[...] += jnp.sum(jnp.sum(err * err, axis=1, keepdims=True), axis=0, keepdims=True) * (0.5 / d)

    main, tail = _token_specs(seq, d)
    return _call("ln_ffn_loss", body, (r // br,),
                 [(h1, _rows(br, d)), (f, _rows(br, d)), (tgt, main), (tgt, tail),
                  (g.reshape(1, d), _whole((1, d))), (b.reshape(1, d), _whole((1, d)))],
                 [(_sds((1, 1), F32), _whole((1, 1)))], sem=("arbitrary",))[0]


def _loss_err(a_ref, r_ref, t_ref, tail_ref, g_ref, b_ref, seq):
    br, d = a_ref.shape
    xhat, rstd = _ln_stats(ALPHA * a_ref[...] + r_ref[...])
    y = xhat * g_ref[...] + b_ref[...]
    t = _padded_block(t_ref, tail_ref, jnp.zeros((N_META, d), F32), seq)
    rid = lax.broadcasted_iota(jnp.int32, (br, d), 0) + pl.program_id(0) * br
    valid = (rid >= N_META) & (rid < N_META + seq)
    return jnp.where(valid, y - t, 0.0), xhat, rstd


def _ln_ffn_bwd(h1, f, tgt, g, b):
    r, d = h1.shape
    seq = tgt.shape[0]
    br = ROW_BLOCK

    def body(a_ref, r_ref, t_ref, tail_ref, g_ref, b_ref, dz_ref, dzb_ref, dg_ref, db_ref):
        err, xhat, rstd = _loss_err(a_ref, r_ref, t_ref, tail_ref, g_ref, b_ref, seq)
        dyv = err * (1.0 / d)
        dyg = dyv * g_ref[...]
        m1 = jnp.mean(dyg, axis=-1, keepdims=True)
        m2 = jnp.mean(dyg * xhat, axis=-1, keepdims=True)
        dz = rstd * (dyg - m1 - xhat * m2)
        dz_ref[...] = dz
        dzb_ref[...] = dz.astype(dzb_ref.dtype)

        @pl.when(pl.program_id(0) == 0)
        def _():
            dg_ref[...] = jnp.zeros_like(dg_ref)
            db_ref[...] = jnp.zeros_like(db_ref)

        dg_ref[...] += jnp.sum(dyv * xhat, axis=0, keepdims=True)
        db_ref[...] += jnp.sum(dyv, axis=0, keepdims=True)

    main, tail = _token_specs(seq, d)
    return _call("ln_ffn_bwd", body, (r // br,),
                 [(h1, _rows(br, d)), (f, _rows(br, d)), (tgt, main), (tgt, tail),
                  (g.reshape(1, d), _whole((1, d))), (b.reshape(1, d), _whole((1, d)))],
                 [(_sds((r, d), F32), _rows(br, d)), (_sds((r, d), MXU_DTYPE), _rows(br, d)),
                  (_sds((1, d), F32), _whole((1, d))), (_sds((1, d), F32), _whole((1, d)))], sem=("arbitrary",))


def _attn_fwd(name, q, k, v, cum_b=None, cum_t=None):
    (qa, qg), (ka, kg), (va, vg) = q, k, v
    r = qa.shape[0]
    tq, tk = ATT_TQ, ATT_TK
    nq, nk = r // tq, r // tk
    bias = cum_b is not None

    def body(*refs):
        if bias:
            q_ref, k_ref, vt_ref, cb_ref, ct_ref, o_ref, ob_ref, lse_ref = refs
        else:
            q_ref, k_ref, vt_ref, o_ref, ob_ref, lse_ref = refs
        i = pl.program_id(1)
        qs = [q_ref[:, _hs(hh)] for hh in range(hg)]
        cqs = [ct_ref[hh] for hh in range(hg)] if bias else None
        diff = lax.broadcasted_iota(jnp.int32, (tk, tq), 0) - lax.broadcasted_iota(jnp.int32, (tk, tq), 1)

        def step(j, carry, masked):
            keys = pl.ds(pl.multiple_of(j * tk, tk), tk)
            out = []
            for hh in range(hg):
                m, l, acc = carry[hh]
                kt = k_ref[keys, _hs(hh)]
                s = lax.dot_general(kt, qs[hh], NT, preferred_element_type=F32)
                if bias:
                    s = s + (cqs[hh] - cb_ref[keys, hh * HP:hh * HP + 1])
                if masked:
                    s = jnp.where(diff <= i * tq - j * tk, s, NEG_INF)
                m_new = jnp.maximum(m, jnp.max(s, axis=0, keepdims=True))
                p = jnp.exp(s - m_new)
                a = jnp.exp(m - m_new)
                l = a * l + jnp.sum(p, axis=0, keepdims=True)
                acc = a * acc + jnp.dot(vt_ref[j, _hs(hh), :], p.astype(kt.dtype), preferred_element_type=F32)
                out.append((m_new, l, acc))
            return tuple(out)

        n_clear = (i * tq + 1) // tk
        n_all = ((i + 1) * tq - 1) // tk + 1
        carry = tuple((jnp.full((1, tq), NEG_INF, F32), jnp.zeros((1, tq), F32), jnp.zeros((HP, tq), F32))
                      for _ in range(hg))
        carry = lax.fori_loop(0, n_clear, lambda j, c: step(j, c, False), carry)
        carry = lax.fori_loop(n_clear, n_all, lambda j, c: step(j, c, True), carry)
        for hh in range(hg):
            m, l, acc = carry[hh]
            o = (acc / l).T
            o_ref[:, _hs(hh)] = o
            ob_ref[:, hh * V_DIM:(hh + 1) * V_DIM] = o[:, :V_DIM].astype(ob_ref.dtype)
            lse_ref[hh] = m + jnp.log(l)

    hg = ATT_HEADS_FWD
    w = hg * HP
    gpw = HW // w
    tile = lambda g: pl.BlockSpec((tq, w), lambda h, i: (i, g * gpw + h))
    res = lambda g: pl.BlockSpec((r, w), lambda h, i: (0, g * gpw + h))
    v_t = _key_tiles_transposed(name + "_vt", va, vg)
    ins = [(qa, tile(qg)), (ka, res(kg)), (v_t, pl.BlockSpec((nk, w, tk), lambda h, i: (0, h, 0)))]
    if bias:
        ins += [(cum_b, res(0)),
                (cum_t.reshape(HEADS, nq, 1, tq), pl.BlockSpec((hg, None, 1, tq), lambda h, i: (h, i, 0, 0)))]
    outs = [(_sds((r, HW), F32), tile(0)),
            (_sds((r, HEADS * V_DIM), MXU_DTYPE), pl.BlockSpec((tq, hg * V_DIM), lambda h, i: (i, h))),
            (_sds((HEADS, nq, 1, tq), F32), pl.BlockSpec((hg, None, 1, tq), lambda h, i: (h, i, 0, 0)))]
    o, ob, lse = _call(name, body, (gpw, nq), ins, outs, sem=("parallel", "parallel"))
    return o, ob, lse.reshape(HEADS, r)


def _key_tiles_transposed(name, a, group):
    r = a.shape[0]
    tk = ATT_TK

    def body(x_ref, o_ref):
        for h in range(HEADS):
            o_ref[_hs(h), :] = x_ref[:, _hs(h)].astype(F32).T.astype(o_ref.dtype)

    return _call(name, body, (r // tk,),
                 [(a, pl.BlockSpec((tk, HW), lambda j: (j, group)))],
                 [(_sds((r // tk, HW, tk), a.dtype), pl.BlockSpec((None, HW, tk), lambda j: (j, 0, 0)))],
                 sem=("parallel",))[0]


def _attn_bwd(name, q, k, v, do_b, o, lse_t, cum_b=None, cum_t=None, out_dtype=F32, after=()):
    (qa, qg), (ka, kg), (va, vg) = q, k, v
    r = qa.shape[0]
    tq, tk = ATT_TQ, ATT_TK
    nq, nk = r // tq, r // tk
    bias = cum_b is not None

    def body(*refs):
        if bias:
            (q_ref, k_ref, v_ref, do_ref, o_ref, lse_ref, cb_ref, ct_ref,
             dq_ref, dk_ref, dv_ref, dcq_ref, dck_ref, dqt_ref, dl_ref) = refs
        else:
            q_ref, k_ref, v_ref, do_ref, o_ref, lse_ref, dq_ref, dk_ref, dv_ref, dqt_ref, dl_ref = refs
        j = pl.program_id(1)

        @pl.when(j == 0)
        def _():
            dqt_ref[...] = jnp.zeros_like(dqt_ref)
            if bias:
                dcq_ref[...] = jnp.zeros_like(dcq_ref)
            for hh in range(hg):
                for i in range(nq):
                    rows = slice(i * tq, (i + 1) * tq)
                    prod = do_ref[rows, _hs(hh)].astype(F32) * o_ref[rows, _hs(hh)]
                    dl_ref[hh, i] = jnp.sum(prod.T, axis=0, keepdims=True)

        kts = [k_ref[:, _hs(hh)] for hh in range(hg)]
        vts = [v_ref[:, _hs(hh)] for hh in range(hg)]
        k_trs = [kt.astype(F32).T.astype(kt.dtype) for kt in kts]
        cks = [cb_ref[:, hh * HP:hh * HP + 1] for hh in range(hg)] if bias else None
        diff = lax.broadcasted_iota(jnp.int32, (tk, tq), 0) - lax.broadcasted_iota(jnp.int32, (tk, tq), 1)

        def step(i, carry, masked):
            rows = pl.ds(pl.multiple_of(i * tq, tq), tq)
            out = []
            for hh in range(hg):
                dk_acc, dv_acc, dck_acc = carry[hh]
                qt = q_ref[rows, _hs(hh)]
                dot = do_ref[rows, _hs(hh)]
                s = lax.dot_general(kts[hh], qt, NT, preferred_element_type=F32)
                if bias:
                    s = s + (ct_ref[hh, i] - cks[hh])
                if masked:
                    s = jnp.where(diff <= i * tq - j * tk, s, NEG_INF)
                p = jnp.exp(s - lse_ref[hh, i])
                dp = lax.dot_general(vts[hh], dot, NT, preferred_element_type=F32)
                ds = p * (dp - dl_ref[hh, i])
                pb = p.astype(dot.dtype)
                dsb = ds.astype(qt.dtype)
                dv_acc = dv_acc + jnp.dot(pb, dot, preferred_element_type=F32)
                dk_acc = dk_acc + jnp.dot(dsb, qt, preferred_element_type=F32)
                dqt_ref[hh, i] += jnp.dot(k_trs[hh], dsb, preferred_element_type=F32)
                if bias:
                    dcq_ref[hh, i] += jnp.sum(ds, axis=0, keepdims=True)
                    dck_acc = dck_acc - jnp.sum(ds, axis=1, keepdims=True)
                out.append((dk_acc, dv_acc, dck_acc))
            return tuple(out)

        i_first = (j * tk) // tq
        i_clear = jnp.minimum(((j + 1) * tk + tq - 2) // tq, nq)
        carry = tuple((jnp.zeros((tk, HP), F32), jnp.zeros((tk, HP), F32), jnp.zeros((tk, 1), F32)) for _ in range(hg))
        carry = lax.fori_loop(i_first, i_clear, lambda i, c: step(i, c, True), carry)
        carry = lax.fori_loop(i_clear, nq, lambda i, c: step(i, c, False), carry)
        for hh in range(hg):
            dk_acc, dv_acc, dck_acc = carry[hh]
            dk_ref[:, _hs(hh)] = dk_acc.astype(dk_ref.dtype)
            dv_ref[:, _hs(hh)] = dv_acc.astype(dv_ref.dtype)
            if bias:
                dck_ref[:, _hs(hh)] = jnp.broadcast_to(dck_acc, (tk, HP))

        @pl.when(j == nk - 1)
        def _():
            for hh in range(hg):
                for i in range(nq):
                    dq_ref[i * tq:(i + 1) * tq, _hs(hh)] = dqt_ref[hh, i].T.astype(dq_ref.dtype)

    hg = ATT_HEADS
    w = hg * HP
    gpw = HW // w
    res = lambda g: pl.BlockSpec((r, w), lambda h, j: (0, g * gpw + h))
    tile = lambda g: pl.BlockSpec((tk, w), lambda h, j: (j, g * gpw + h))
    rowv = pl.BlockSpec((hg, nq, 1, tq), lambda h, j: (h, 0, 0, 0))
    as_rows = lambda a: a.reshape(HEADS, nq, 1, tq)
    ins = [(qa, res(qg)), (ka, tile(kg)), (va, tile(vg)), (do_b, res(0)), (o, res(0)), (as_rows(lse_t), rowv)]
    outs = [(_sds((r, HW), out_dtype), res(0)), (_sds((r, HW), out_dtype), tile(0)), (_sds((r, HW), out_dtype), tile(0))]
    if bias:
        ins += [(cum_b, tile(0)), (as_rows(cum_t), rowv)]
        outs += [(_sds((HEADS, nq, 1, tq), F32), rowv), (_sds((r, HW), F32), tile(0))]
    res_out = _call(name, body, (gpw, nk), ins, outs,
                    scratch=[pltpu.VMEM((hg, nq, HP, tq), F32), pltpu.VMEM((hg, nq, 1, tq), F32)],
                    sem=("parallel", "arbitrary"), after=after)
    if bias:
        dq, dk, dv, dcq, dck = res_out
        return dq, dk, dv, dcq.reshape(HEADS, r), dck
    return res_out


MESH_ID = pl.DeviceIdType.MESH
ANY = pl.BlockSpec(memory_space=pl.ANY)


N_GATHER_COPIES = 8


def _allgather(name, shards):
    n = len(shards)

    def body(*refs):
        x_refs, out_refs = refs[:n], refs[n:2 * n]
        send_sems, recv_sems, local_sems = refs[2 * n:]
        x, y, c = lax.axis_index("x"), lax.axis_index("y"), lax.axis_index("c")
        me, sibling = (x, y, c), (x, y, 1 - c)
        xn, yn, dg = (1 - x, y, c), (x, 1 - y, c), (1 - x, 1 - y, c)
        other = lambda dev: (dev[0], dev[1], 1 - c)

        def slot(ti, dev, half=None):
            ref = out_refs[ti].at[4 * dev[0] + 2 * dev[1] + dev[2]]
            if half is None:
                return ref
            rows = shards[ti].shape[0] // 2
            return ref.at[pl.ds(half * rows, rows)]

        def copy(ti, k, block, to, half=None, src=None):
            return pltpu.make_async_remote_copy(
                src_ref=slot(ti, block, half) if src is None else src, dst_ref=slot(ti, block, half),
                send_sem=send_sems.at[ti, k], recv_sem=recv_sems.at[ti, k], device_id=to, device_id_type=MESH_ID)

        mine = [pltpu.make_async_copy(x_refs[ti], slot(ti, me), local_sems.at[ti]) for ti in range(n)]
        for cp in mine:
            cp.start()
        started = []

        def go(cp):
            cp.start()
            started.append(cp)

        for ti in range(n):
            go(copy(ti, 0, me, sibling, src=x_refs[ti]))
            go(copy(ti, 1, me, xn, src=x_refs[ti]))
            go(copy(ti, 2, me, yn, src=x_refs[ti]))
        for ti in range(n):
            copy(ti, 1, xn, me).wait_recv()
            go(copy(ti, 3, xn, yn, half=0))
            go(copy(ti, 5, xn, sibling))
            copy(ti, 2, yn, me).wait_recv()
            go(copy(ti, 4, yn, xn, half=1))
            go(copy(ti, 6, yn, sibling))
        for ti in range(n):
            copy(ti, 3, dg, me, half=0).wait_recv()
            copy(ti, 4, dg, me, half=1).wait_recv()
            go(copy(ti, 7, dg, sibling))
        for ti in range(n):
            copy(ti, 0, sibling, me).wait_recv()
            for k, dev in ((5, xn), (6, yn), (7, dg)):
                copy(ti, k, other(dev), me).wait_recv()
        for cp in started:
            cp.wait_send()
        for cp in mine:
            cp.wait()

    sems = pltpu.SemaphoreType.DMA((n, N_GATHER_COPIES))
    return pl.pallas_call(
        body, name=name, out_shape=[_sds((N_DEV,) + s.shape, s.dtype) for s in shards],
        in_specs=[ANY] * n, out_specs=[ANY] * n,
        scratch_shapes=[sems, sems, pltpu.SemaphoreType.DMA((n,))],
    )(*shards)


HBM = pl.BlockSpec(memory_space=pltpu.HBM)
SEM = pl.BlockSpec(memory_space=pltpu.SEMAPHORE)
EFFECT = pltpu.SideEffectType.DATAFLOW_SIDE_EFFECTING
N_PEER = N_DEV - 1


def _my_id():
    return 4 * lax.axis_index("x") + 2 * lax.axis_index("y") + lax.axis_index("c")


def _peers():
    x, y, c = lax.axis_index("x"), lax.axis_index("y"), lax.axis_index("c")
    out = []
    for k in range(1, N_DEV):
        px, py, pc = (1 - x if k & 4 else x, 1 - y if k & 2 else y, 1 - c if k & 1 else c)
        out.append(((px, py, pc), 4 * px + 2 * py + pc))
    return out


def _push_copies(src_refs, land_refs, send_sems, recv_sems, scatter, landing):
    me = _my_id()
    out = []
    for ti, (src, land) in enumerate(zip(src_refs, land_refs)):
        for k, (dev, pid) in enumerate(_peers()):
            out.append(pltpu.make_async_remote_copy(
                src_ref=src.at[pid] if scatter else src, dst_ref=land.at[pid if landing else me],
                send_sem=send_sems.at[ti * N_PEER + k], recv_sem=recv_sems.at[ti * N_PEER + k],
                device_id=dev, device_id_type=MESH_ID))
    return out


def _push_start(name, groups, scatter, after=None):
    sizes = [len(g) for g in groups]
    srcs = [a for g in groups for a in g]
    n = len(srcs)
    slot = lambda s: s.shape[1:] if scatter else s.shape
    lands = [lax.empty((N_DEV,) + slot(s), s.dtype) for s in srcs]
    n_after = 0 if after is None else 1
    n_grp = len(groups)

    def body(*refs):
        src_refs, land_refs = refs[:n], refs[n:2 * n]
        sems = refs[2 * n + n_after:2 * n + n_after + 2 * n_grp]
        token = refs[-1]
        lo = 0
        for gi, sz in enumerate(sizes):
            for cp in _push_copies(src_refs[lo:lo + sz], land_refs[lo:lo + sz], sems[2 * gi], sems[2 * gi + 1], scatter, False):
                cp.start()
            lo += sz
        token[...] = jnp.zeros_like(token)

    hbm = lambda a: pltpu.with_memory_space_constraint(a, pltpu.HBM)
    operands = [hbm(a) for a in srcs + lands] + ([after] if n_after else [])
    sem_shapes = [pltpu.SemaphoreType.DMA((sz * N_PEER,)) for sz in sizes for _ in range(2)]
    res = pl.pallas_call(
        body, name=name,
        out_shape=sem_shapes + [pltpu.HBM(a.shape, a.dtype) for a in srcs + lands] + [_sds((8, 128), F32)],
        in_specs=[HBM] * (2 * n) + [ANY] * n_after,
        out_specs=[SEM] * (2 * n_grp) + [HBM] * (2 * n) + [pl.BlockSpec(memory_space=pltpu.VMEM)],
        input_output_aliases={i: 2 * n_grp + i for i in range(2 * n)},
        compiler_params=pltpu.CompilerParams(has_side_effects=EFFECT),
    )(*operands)
    thru = res[2 * n_grp:2 * n_grp + 2 * n]
    handles, lo = [], 0
    for gi, sz in enumerate(sizes):
        handles.append((res[2 * gi], res[2 * gi + 1], list(thru[lo:lo + sz]), list(thru[n + lo:n + lo + sz]), scatter))
        lo += sz
    return handles, res[-1]


def _push_wait(name, handle, after):
    send_sems, recv_sems, srcs, lands, scatter = handle
    n = len(srcs)

    def body(*refs):
        src_refs, land_refs = refs[:n], refs[n:2 * n]
        s_sems, r_sems = refs[2 * n], refs[2 * n + 1]
        for cp in _push_copies(src_refs, land_refs, s_sems, r_sems, scatter, True):
            cp.wait_send()
            cp.wait_recv()

    res = pl.pallas_call(
        body, name=name,
        out_shape=[pltpu.HBM(a.shape, a.dtype) for a in srcs + lands],
        in_specs=[HBM] * (2 * n) + [SEM, SEM, ANY], out_specs=[HBM] * (2 * n),
        input_output_aliases={i: i for i in range(2 * n)},
        compiler_params=pltpu.CompilerParams(has_side_effects=EFFECT),
    )(*srcs, *lands, send_sems, recv_sems, after)
    return list(res[n:])


def _adamw(name, parts, w, m, v, own=None):
    r, c = w.shape
    br = _pick(r, COPY_ROWS, 16)
    has_own = own is not None

    def body(*refs):
        if has_own:
            p_ref, own_ref, w_ref, m_ref, v_ref, g_ref, d_ref, nm_ref, nv_ref = refs
            me = _my_id()
            mine = own_ref[...].astype(F32)
        else:
            p_ref, w_ref, m_ref, v_ref, g_ref, d_ref, nm_ref, nv_ref = refs
        g = None
        for k in range(N_DEV):
            t = p_ref[k].astype(F32)
            if has_own:
                t = jnp.where(me == k, mine, t)
            g = t if g is None else g + t
        mm = ADAM_B1 * m_ref[...] + (1.0 - ADAM_B1) * g
        vv = ADAM_B2 * v_ref[...] + (1.0 - ADAM_B2) * (g * g)
        m_hat = mm / (1.0 - ADAM_B1 ** ADAM_STEP)
        v_hat = vv / (1.0 - ADAM_B2 ** ADAM_STEP)
        g_ref[...] = g
        d_ref[...] = -ADAM_LR * (m_hat / (jnp.sqrt(v_hat) + ADAM_EPS) + ADAM_WD * w_ref[...])
        nm_ref[...] = mm
        nv_ref[...] = vv

    spec = _rows(br, c)
    out = (_sds((r, c), F32), spec)
    ins = [(parts, pl.BlockSpec((N_DEV, br, c), lambda i: (0, i, 0)))] + ([(own, spec)] if has_own else [])
    return _call(name, body, (r // br,), ins + [(w, spec), (m, spec), (v, spec)], [out] * 4, sem=("parallel",))


def _pad_head_cols(w, d):
    k = w.shape[0]
    return jnp.pad(w.reshape(k, HEADS, d), ((0, 0), (0, 0), (0, HP - d))).reshape(k, HW)


def _unpad_head_cols(wp, d):
    k = wp.shape[0]
    return wp.reshape(k, HEADS, HP)[:, :, :d].reshape(k, HEADS * d)


def _pad_head_rows(w, d):
    n = w.shape[1]
    return jnp.pad(w.reshape(HEADS, d, n), ((0, 0), (0, HP - d), (0, 0))).reshape(HW, n)


def _w_in_runs():
    nat = {}
    o = 0
    for nm, wd in (("q", Q_RANK), ("kv", KV_RANK), ("kr", ROPE), ("fq", FOX_W), ("fk", FOX_W), ("fv", FOX_W),
                   ("fl", HEADS), ("gate", 2 * D_MODEL)):
        nat[nm] = o
        o += wd
    runs = [(1, R_QLAT, nat["q"], Q_RANK, 1.0), (1, R_KVLAT, nat["kv"], KV_RANK, 1.0),
            (1, R_LAST + LANE_FL, nat["fl"], HEADS, 1.0), (1, R_LAST + LANE_PE, nat["kr"], ROPE, 1.0),
            (1, R_GATE, nat["gate"], 2 * D_MODEL, 1.0)]
    for grp, (nm, sc) in enumerate((("fq", FOX_SCALE), ("fk", 1.0), ("fv", 1.0))):
        runs.append((0, grp * FOX_W, nat[nm], FOX_W, sc))
    return runs


def _head_pad_moves(pad):
    moves = []
    for grp in range(3):
        for h in range(HEADS):
            narrow, wide = grp * FOX_W + h * FOX_DIM, h * HP
            if pad:
                moves.append((0, None, grp * HW + wide, 0, None, narrow, FOX_DIM, 1.0))
            else:
                moves.append((0, None, narrow, grp, None, wide, FOX_DIM, 1.0))
    return moves


def _sharded_runs(runs, shard_cols):
    out = []
    for half, col, ncol, width, sc in runs:
        while width > 0:
            d, local = divmod(ncol, shard_cols)
            wd = min(width, shard_cols - local)
            out.append((half, col, d, local, wd, sc))
            col, ncol, width = col + wd, ncol + wd, width - wd
    return out


def _remap(name, srcs, out_shapes, moves):
    rows = srcs[0].shape[-2]
    br = _pick(rows, COPY_ROWS, 16)
    ns = len(srcs)

    def spec(shape):
        if len(shape) == 2:
            return pl.BlockSpec((br, shape[1]), lambda i: (i, 0))
        return pl.BlockSpec((shape[0], br, shape[2]), lambda i: (0, i, 0))

    covered = [sum(m[6] for m in moves if m[0] == di) for di in range(len(out_shapes))]
    has_gaps = [cov < (shape[1] if len(shape) == 2 else shape[0] * shape[2])
                for cov, (shape, _) in zip(covered, out_shapes)]

    def body(*refs):
        s_refs, o_refs = refs[:ns], refs[ns:]
        for o, gaps in zip(o_refs, has_gaps):
            if gaps:
                o[...] = jnp.zeros_like(o)
        for di, dl, dc, si, sl, sc0, wd, scale in moves:
            v = s_refs[si][:, sc0:sc0 + wd] if sl is None else s_refs[si][sl, :, sc0:sc0 + wd]
            if scale != 1.0:
                v = v * jnp.asarray(scale, v.dtype)
            v = v.astype(o_refs[di].dtype)
            if dl is None:
                o_refs[di][:, dc:dc + wd] = v
            else:
                o_refs[di][dl, :, dc:dc + wd] = v

    return _call(name, body, (rows // br,), [(a, spec(a.shape)) for a in srcs],
                 [(_sds(shape, dt), spec(shape)) for shape, dt in out_shapes], sem=("parallel",))


def _w_in_from_shards(g3):
    n, rows, c = g3.shape
    moves = [(half, None, col, 0, d, local, wd, sc) for half, col, d, local, wd, sc in _sharded_runs(_w_in_runs(), c)]
    return _remap("w_in_repack", [g3], [((rows, F_W), g3.dtype), ((rows, R_W), g3.dtype)], moves)


def _w_in_grad_to_shards(d_fused, d_rest, n, c):
    rows = d_fused.shape[0]
    moves = [(0, d, local, half, None, col, wd, sc) for half, col, d, local, wd, sc in _sharded_runs(_w_in_runs(), c)]
    return _remap("w_in_grad_unpack", [d_fused, d_rest], [((n, rows, c), d_fused.dtype)], moves)[0]


def _rows_from_shards(name, land, own):
    n, rows, c = land.shape

    def body(land_ref, own_ref, o_ref):
        o_ref[...] = jnp.where(_my_id() == pl.program_id(0), own_ref[...], land_ref[...])

    return _call(name, body, (n,),
                 [(land, pl.BlockSpec((None, rows, c), lambda d: (d, 0, 0))), (own, _whole((rows, c)))],
                 [(_sds((n * rows, c), land.dtype), pl.BlockSpec((rows, c), lambda d: (d, 0)))], sem=("parallel",))[0]


def _cols_from_shards(name, land, own):
    n, rows, c = land.shape
    br = _pick(rows, COPY_ROWS, 16)

    def body(land_ref, own_ref, o_ref):
        me = _my_id()
        for d in range(n):
            o_ref[:, c * d:c * (d + 1)] = jnp.where(me == d, own_ref[...], land_ref[d])

    return _call(name, body, (rows // br,),
                 [(land, pl.BlockSpec((n, br, c), lambda i: (0, i, 0))), (own, _rows(br, c))],
                 [(_sds((rows, n * c), land.dtype), _rows(br, n * c))], sem=("parallel",))[0]


def _cols_to_shards(name, full, n):
    rows, nc = full.shape
    c = nc // n
    return _remap(name, [full], [((n, rows, c), full.dtype)], [(0, d, 0, 0, None, c * d, c, 1.0) for d in range(n)])[0]


def _split_w_kv(w):
    k = w.shape[0]
    w3 = w.reshape(k, HEADS, NOPE + V_DIM)
    padl = lambda a: jnp.pad(a, ((0, 0), (0, 0), (0, HP - a.shape[-1]))).reshape(k, HW)
    return padl(w3[..., :NOPE]), padl(w3[..., NOPE:])


def _merge_w_kv(wk, wv):
    k = wk.shape[0]
    return jnp.concatenate([wk.reshape(k, HEADS, HP)[..., :NOPE], wv.reshape(k, HEADS, HP)[..., :V_DIM]],
                           axis=-1).reshape(k, HEADS * (NOPE + V_DIM))


class _NoComm:
    first_token = ()

    def late_weights(self, group, after):
        return {}

    def send(self, name, grads):
        return ()


def _local_step(x, tgt, p, comm=_NoComm()):
    seq = x.shape[0]
    r = -(-(N_META + seq) // ROW_ALIGN) * ROW_ALIGN
    cd = MXU_DTYPE
    p = dict(p)

    w_f, w_r = p["w_in"]

    pos = jnp.arange(r, dtype=F32)
    inv_freq = ROPE_THETA ** (-jnp.arange(HALF, dtype=F32) / HALF)
    ang = pos[:, None] * inv_freq[None, :]
    cos_t = jnp.tile(jnp.cos(ang), (1, HP // HALF))
    sin_t = jnp.tile(jnp.sin(ang), (1, HP // HALF))
    bf_row = jnp.zeros((1, HP), F32).at[0, LANE_FL:LANE_FL + HEADS].set(p["b_forget"])

    h0, h0b = _ln_emb_fwd(x, p["meta_tokens"], p["ln_emb_g"], p["ln_emb_b"], r, after=comm.first_token)
    proj_f = _matmul("in_proj_f", h0b, w_f, out_dtype=cd)
    proj_f = _remap("proj_f_pad", [proj_f], [((r, 3 * HW), cd)], _head_pad_moves(True))[0]
    proj_r = _matmul("in_proj_r", h0b, w_r)
    latent_gains = (p["q_norm_g"], p["kv_norm_g"])
    ql, kvl = _latent_norm_fwd(proj_r, latent_gains)
    p.update(comm.late_weights("qkv", ql))
    w_q = _pad_head_cols(p["w_q_up"], QK_DIM)
    w_kv = jnp.concatenate(_split_w_kv(p["w_kv_up"]), axis=1)
    q_raw = _matmul("q_up", ql, w_q)
    kv = _matmul("kv_up", kvl, w_kv, out_dtype=cd)
    q_mla, k_mla = _rope_fwd(q_raw, kv, proj_r, cos_t, sin_t)
    o_mla, o_mla_b, lse_mla = _attn_fwd("mla_fwd", (q_mla, 0), (k_mla, 0), (kv, 1))

    cum, cum_t = _forget_fwd(proj_r, bf_row)
    o_fox, o_fox_b, lse_fox = _attn_fwd("fox_fwd", (proj_f, 0), (proj_f, 1), (proj_f, 2), cum, cum_t)

    p.update(comm.late_weights("mix", o_fox_b))
    w_bm = _pad_head_rows(p["w_branch_mla"], V_DIM)
    w_bf = _pad_head_rows(p["w_branch_fox"], FOX_DIM)
    bm = _matmul("branch_mla", o_mla_b, p["w_branch_mla"], out_dtype=cd)
    bfx = _matmul("branch_fox", o_fox_b, p["w_branch_fox"], out_dtype=cd)
    merged = _gate_fwd(proj_r, p["b_gate"], bm, bfx)
    mix = _matmul("out_proj", merged, p["w_out"])
    h1, h1b = _ln_fwd("ln_mix_fwd", h0, mix, p["ln_mix_g"], p["ln_mix_b"])
    p.update(comm.late_weights("ffn", h1b))
    up = _matmul("ffn_up", h1b, p["w_ffn_up"], out_dtype=cd)
    act = _glu_fwd(up, p["conv_w"], p["conv_b"])
    f = _matmul("ffn_down", act, p["w_ffn_down"])
    loss = _ln_ffn_loss(h1, f, tgt, p["ln_ffn_g"], p["ln_ffn_b"])

    g = {}
    dz2, dz2b, g["ln_ffn_g"], g["ln_ffn_b"] = _ln_ffn_bwd(h1, f, tgt, p["ln_ffn_g"], p["ln_ffn_b"])
    d_act = _matmul("ffn_down_dx", dz2b, p["w_ffn_down"], tb=True, out_dtype=cd)
    g["w_ffn_down"] = _matmul("ffn_down_dw", act, dz2b, ta=True, out_dtype=cd)
    d_up, dcw, g["conv_b"] = _glu_bwd(up, p["conv_w"], p["conv_b"], d_act)
    g["conv_w"] = dcw[:3]
    dh1 = _matmul("ffn_up_dx", d_up, p["w_ffn_up"], tb=True, addend=dz2, alpha=ALPHA)
    g["w_ffn_up"] = _matmul("ffn_up_dw", h1b, d_up, ta=True, out_dtype=cd)
    sent = comm.send("ffn", {n: g[n] for n in ("w_ffn_down", "w_ffn_up", "conv_w")})
    dz1, dz1b, g["ln_mix_g"], g["ln_mix_b"] = _ln_bwd("ln_mix_bwd", h0, mix, dh1, p["ln_mix_g"], after=sent)
    dmerged = _matmul("out_proj_dx", dz1b, p["w_out"], tb=True, out_dtype=cd)
    g["w_out"] = _matmul("out_proj_dw", merged, dz1b, ta=True, out_dtype=cd)
    d_bm, d_bf, d_gl, g["b_gate"] = _gate_bwd(proj_r, p["b_gate"], bm, bfx, dmerged)
    do_mla_b = _matmul("branch_mla_dx", d_bm, w_bm, tb=True, out_dtype=cd)
    g["w_branch_mla"] = _matmul("branch_mla_dw", o_mla_b, d_bm, ta=True, out_dtype=cd)
    do_fox_b = _matmul("branch_fox_dx", d_bf, w_bf, tb=True, out_dtype=cd)
    g["w_branch_fox"] = _matmul("branch_fox_dw", o_fox_b, d_bf, ta=True, out_dtype=cd)

    sent = comm.send("mix", {n: g[n] for n in ("w_out", "w_branch_mla", "w_branch_fox")})
    dq_m, dk_m, dv_m = _attn_bwd("mla_bwd", (q_mla, 0), (k_mla, 0), (kv, 1), do_mla_b, o_mla, lse_mla, after=sent)
    dfq, dfk, dfv, dcq, dck = _attn_bwd("fox_bwd", (proj_f, 0), (proj_f, 1), (proj_f, 2), do_fox_b, o_fox, lse_fox,
                                        cum, cum_t, out_dtype=cd)
    dfl, dbf = _forget_bwd(proj_r, bf_row, dcq, dck)
    g["b_forget"] = dbf[:, LANE_FL:LANE_FL + HEADS]

    dq_b, dkv_b, dlast = _rope_bwd(dq_m, dk_m, dv_m, dfl, cos_t, sin_t)
    d_ql = _matmul("q_up_dx", dq_b, w_q, tb=True)
    d_kvl = _matmul("kv_up_dx", dkv_b, w_kv, tb=True)
    d_qlat, d_kvlat, g["q_norm_g"], g["kv_norm_g"] = _latent_norm_bwd(proj_r, (d_ql, d_kvl), latent_gains)
    side_by_side = lambda parts, cols: [(0, None, c0, si, None, 0, a.shape[1], 1.0) for si, (a, c0) in enumerate(zip(parts, cols))]
    dproj_f = _remap("dproj_f_pack", [dfq, dfk, dfv], [((r, F_W), cd)], _head_pad_moves(False))[0]
    rest_parts = [d_qlat, d_kvlat, dlast, d_gl]
    dproj_r = _remap("dproj_r_pack", rest_parts, [((r, R_W), cd)],
                     side_by_side(rest_parts, (R_QLAT, R_KVLAT, R_LAST, R_GATE)))[0]
    g["w_in"] = (_matmul("in_proj_f_dw", h0b, dproj_f, ta=True, out_dtype=cd),
                 _matmul("in_proj_r_dw", h0b, dproj_r, ta=True, out_dtype=cd))
    sent = comm.send("in", {"w_in": g["w_in"]})
    dh0 = _matmul("in_proj_f_dx", dproj_f, w_f, tb=True, addend=dz1, alpha=ALPHA, after=sent)
    g["w_q_up"] = _unpad_head_cols(_matmul("q_up_dw", ql, dq_b, ta=True, out_dtype=cd, after=sent), QK_DIM)
    dw_kv = _matmul("kv_up_dw", kvl, dkv_b, ta=True, out_dtype=cd, after=sent)
    g["w_kv_up"] = _merge_w_kv(dw_kv[:, :HW], dw_kv[:, HW:])
    sent = comm.send("qkv", {n: g[n] for n in ("w_q_up", "w_kv_up")})
    dh0 = _matmul("in_proj_r_dx", dproj_r, w_r, tb=True, addend=dh0, after=sent)
    grad_x, d_meta, g["ln_emb_g"], g["ln_emb_b"] = _ln_emb_bwd(x, p["meta_tokens"], dh0, p["ln_emb_g"])
    return loss, grad_x, d_meta, g


BIG = (("w_in", 1), ("w_q_up", 1), ("w_kv_up", 1), ("w_branch_mla", 1), ("w_branch_fox", 1), ("w_out", 0),
       ("w_ffn_up", 1), ("w_ffn_down", 0))
SMALL_SHARDED = (("meta_tokens", 1), ("conv_w", 1))
EARLY = ("w_in", "meta_tokens")
LATE = {"qkv": ("w_q_up", "w_kv_up", "conv_w"),
        "mix": ("w_branch_mla", "w_branch_fox", "w_out"),
        "ffn": ("w_ffn_up", "w_ffn_down")}
REPLICATED = ("ln_emb_g", "ln_emb_b", "b_gate", "b_forget", "q_norm_g", "kv_norm_g", "ln_mix_g", "ln_mix_b",
              "conv_b", "ln_ffn_g", "ln_ffn_b")
PACK_COLS = 1024


def _pack(flat_list):
    cat = jnp.concatenate(flat_list)
    n = cat.shape[0]
    rows = -(-n // (8 * PACK_COLS)) * 8
    return jnp.pad(cat, (0, rows * PACK_COLS - n)).reshape(rows, PACK_COLS)


def _gathered_full(g3, axis):
    n, r, c = g3.shape
    if axis == 0:
        return g3.reshape(n * r, c)
    return g3.transpose(1, 0, 2).reshape(r, n * c)


def _shard_major(full, axis):
    r, c = full.shape
    if axis == 0:
        return full.reshape(N_DEV, r // N_DEV, c)
    return full.reshape(r, N_DEV, c // N_DEV).transpose(1, 0, 2)


def kernel(x, meta_tokens, ln_emb_g, ln_emb_b, w_in, b_gate, b_forget, q_norm_g, w_q_up, kv_norm_g, w_kv_up, w_branch_mla, w_branch_fox, w_out, ln_mix_g, ln_mix_b, w_ffn_up, conv_w, conv_b, w_ffn_down, ln_ffn_g, ln_ffn_b, loss_target, m_meta_tokens, m_ln_emb_g, m_ln_emb_b, m_w_in, m_b_gate, m_b_forget, m_q_norm_g, m_w_q_up, m_kv_norm_g, m_w_kv_up, m_w_branch_mla, m_w_branch_fox, m_w_out, m_ln_mix_g, m_ln_mix_b, m_w_ffn_up, m_conv_w, m_conv_b, m_w_ffn_down, m_ln_ffn_g, m_ln_ffn_b, v_meta_tokens, v_ln_emb_g, v_ln_emb_b, v_w_in, v_b_gate, v_b_forget, v_q_norm_g, v_w_q_up, v_kv_norm_g, v_w_kv_up, v_w_branch_mla, v_w_branch_fox, v_w_out, v_ln_mix_g, v_ln_mix_b, v_w_ffn_up, v_conv_w, v_conv_b, v_w_ffn_down, v_ln_ffn_g, v_ln_ffn_b):
    names = ("meta_tokens", "ln_emb_g", "ln_emb_b", "w_in", "b_gate", "b_forget", "q_norm_g", "w_q_up", "kv_norm_g",
             "w_kv_up", "w_branch_mla", "w_branch_fox", "w_out", "ln_mix_g", "ln_mix_b", "w_ffn_up", "conv_w", "conv_b",
             "w_ffn_down", "ln_ffn_g", "ln_ffn_b")
    w_args = (meta_tokens, ln_emb_g, ln_emb_b, w_in, b_gate, b_forget, q_norm_g, w_q_up, kv_norm_g, w_kv_up,
              w_branch_mla, w_branch_fox, w_out, ln_mix_g, ln_mix_b, w_ffn_up, conv_w, conv_b, w_ffn_down, ln_ffn_g, ln_ffn_b)
    m_args = (m_meta_tokens, m_ln_emb_g, m_ln_emb_b, m_w_in, m_b_gate, m_b_forget, m_q_norm_g, m_w_q_up, m_kv_norm_g,
              m_w_kv_up, m_w_branch_mla, m_w_branch_fox, m_w_out, m_ln_mix_g, m_ln_mix_b, m_w_ffn_up, m_conv_w, m_conv_b,
              m_w_ffn_down, m_ln_ffn_g, m_ln_ffn_b)
    v_args = (v_meta_tokens, v_ln_emb_g, v_ln_emb_b, v_w_in, v_b_gate, v_b_forget, v_q_norm_g, v_w_q_up, v_kv_norm_g,
              v_w_kv_up, v_w_branch_mla, v_w_branch_fox, v_w_out, v_ln_mix_g, v_ln_mix_b, v_w_ffn_up, v_conv_w, v_conv_b,
              v_w_ffn_down, v_ln_ffn_g, v_ln_ffn_b)
    as2d = lambda a: a.reshape((-1, a.shape[-1])) if a.ndim != 1 else a.reshape(1, -1)
    w = {n: as2d(a) for n, a in zip(names, w_args)}
    m = {n: as2d(a) for n, a in zip(names, m_args)}
    v = {n: as2d(a) for n, a in zip(names, v_args)}
    out_shape = {n: a.shape for n, a in zip(names, w_args)}

    axis_of = dict(BIG + SMALL_SHARDED)
    big = set(n for n, _ in BIG)
    wire = lambda n, a: a.astype(MXU_DTYPE) if n in big else a
    my_id = _my_id()

    early = _allgather("gather_early", [wire(n, w[n]) for n in EARLY])
    p = {n: _gathered_full(g3, axis_of[n]) for n, g3 in zip(EARLY, early) if n != "w_in"}
    p["w_in"] = _w_in_from_shards(early[EARLY.index("w_in")])
    for n in REPLICATED:
        p[n] = w[n].reshape(-1)
    late_src = [[wire(n, w[n]) for n in members] for members in LATE.values()]
    late_handles, late_token = _push_start("gather_late_start", late_src, False, after=early[0])
    late = {group: (members, src, handle)
            for (group, members), src, handle in zip(LATE.items(), late_src, late_handles)}
    sent = {}

    class Comm:
        first_token = (late_token,)

        def late_weights(self, group, after):
            members, src, handle = late[group]
            lands = _push_wait("gather_" + group + "_wait", handle, after)
            out = {}
            for n, own, land in zip(members, src, lands):
                if own.shape[0] % 16:
                    out[n] = _gathered_full(lax.dynamic_update_index_in_dim(land, own, my_id, 0), axis_of[n])
                elif axis_of[n] == 1:
                    out[n] = _cols_from_shards(n + "_repack", land, own)
                else:
                    out[n] = _rows_from_shards(n + "_repack", land, own)
            return out

        def send(self, name, grads):
            names_ = tuple(grads)
            parts = []
            for n in names_:
                if n == "w_in":
                    parts.append(_w_in_grad_to_shards(*grads[n], N_DEV, w[n].shape[1]))
                elif n == "w_ffn_up":
                    parts.append(_cols_to_shards(n + "_grad_unpack", grads[n], N_DEV))
                else:
                    parts.append(_shard_major(grads[n], axis_of[n]).astype(MXU_DTYPE))
            (handle,), token = _push_start("send_" + name + "_start", [parts], True)
            sent[name] = (names_, parts, handle)
            return (token,)

    loss_part, grad_x, d_meta, g = _local_step(x[0], loss_target[0], p, Comm())
    grad_x = grad_x[None]

    small = _pack([d_meta.reshape(-1)] + [g[n].reshape(-1) for n in REPLICATED] + [loss_part.reshape(-1)])
    (small_handle,), small_token = _push_start("send_small_start", [[small]], False)

    res = {}
    prev = small_token
    for name, (names_, parts, handle) in sent.items():
        lands = _push_wait("send_" + name + "_wait", handle, prev)
        for n, part, land in zip(names_, parts, lands):
            own = lax.dynamic_index_in_dim(part, my_id, axis=0, keepdims=False)
            res[n] = _adamw("adamw_" + n, land, w[n], m[n], v[n], own=own)
            prev = res[n][0]
    small_all = _push_wait("send_small_wait", small_handle, prev)[0]
    head = jnp.zeros((d_meta.size,), F32)
    rep_w = _pack([head] + [w[n].reshape(-1) for n in REPLICATED])
    rep_m = _pack([head] + [m[n].reshape(-1) for n in REPLICATED])
    rep_v = _pack([head] + [v[n].reshape(-1) for n in REPLICATED])
    rep_res = _adamw("adamw_replicated", small_all, rep_w, rep_m, rep_v, own=small)
    off = d_meta.size
    for n in REPLICATED:
        sz = w[n].size
        res[n] = tuple(a.reshape(-1)[off:off + sz] for a in rep_res)
        off += sz
    loss = rep_res[0].reshape(-1)[off]
    cols = w["meta_tokens"].shape[1]
    meta_rows = lambda a: a.reshape(a.shape[:-2] + (-1,))[..., :d_meta.size].reshape(a.shape[:-2] + d_meta.shape)
    my_cols = lambda a: lax.dynamic_slice_in_dim(a, my_id * cols, cols, axis=a.ndim - 1)
    res["meta_tokens"] = _adamw("adamw_meta_tokens", my_cols(meta_rows(small_all)), w["meta_tokens"],
                                m["meta_tokens"], v["meta_tokens"], own=my_cols(d_meta))

    outs = [loss, grad_x]
    for idx in range(4):
        outs += [res[n][idx].reshape(out_shape[n]) for n in names]
    return tuple(outs)
```

```python
import jax
import jax.numpy as jnp
from jax import lax
from jax.experimental import pallas as pl
from jax.experimental.pallas import tpu as pltpu

F32 = jnp.float32
BF16 = jnp.bfloat16
MXU_DTYPE = BF16

N_DEV = 8
N_META = 16
D_MODEL = 1024
HEADS = 8
Q_RANK = 384
KV_RANK = 128
NOPE = 64
ROPE = 32
HALF = ROPE // 2
QK_DIM = NOPE + ROPE
V_DIM = 64
FOX_DIM = 64
FOX_W = HEADS * FOX_DIM
D_FF = 2816
ROPE_THETA = 10000.0
LN_EPS = 1e-5
RMS_EPS = 1e-6
ALPHA = 2.0 ** 0.25
MLA_SCALE = QK_DIM ** -0.5
FOX_SCALE = FOX_DIM ** -0.5
NEG_INF = -1e30

HP = 128
HW = HEADS * HP
F_W = 3 * FOX_W
R_GATE = 0
R_KVLAT = R_GATE + 2 * D_MODEL
R_LAST = R_KVLAT + KV_RANK
R_QLAT = R_LAST + HP
R_W = R_QLAT + Q_RANK
assert R_QLAT % Q_RANK == 0 and R_KVLAT % KV_RANK == 0 and R_GATE % D_MODEL == 0 and R_W % HP == 0
LANE_FL = 0
LANE_PE = NOPE

ADAM_LR = 0.001
ADAM_B1 = 0.9
ADAM_B2 = 0.999
ADAM_EPS = 1e-08
ADAM_WD = 0.01
ADAM_STEP = 10

ROW_BLOCK = 384
TOKEN_BLOCK = 256
ATT_TQ = 768
ATT_TK = 768
ATT_HEADS = 2
ATT_HEADS_FWD = 4
ROW_ALIGN = 768
MM_BLOCK_CAP = 1408
VMEM_LIMIT = 56 * 1024 * 1024
HIGHEST = lax.Precision.HIGHEST
NT = (((1,), (1,)), ((), ()))
TN = (((0,), (0,)), ((), ()))


def _params(sem=None):
    return pltpu.CompilerParams(dimension_semantics=sem, vmem_limit_bytes=VMEM_LIMIT)


def _call(name, body, grid, ins, outs, scratch=(), sem=None, after=()):
    n_in = len(ins)
    n_tok = len(after)

    def run(*refs):
        body(*refs[:n_in], *refs[n_in + n_tok:])

    tok_spec = pl.BlockSpec((8, 128), lambda *_: (0, 0))
    return pl.pallas_call(
        run, name=name, grid=grid,
        in_specs=[s for _, s in ins] + [tok_spec] * n_tok,
        out_specs=[s for _, s in outs],
        out_shape=[o for o, _ in outs],
        scratch_shapes=list(scratch),
        compiler_params=_params(sem),
    )(*[a for a, _ in ins], *after)


def _sds(shape, dtype):
    return jax.ShapeDtypeStruct(shape, dtype)


def _rows(br, c, cb=0):
    return pl.BlockSpec((br, c), lambda i: (i, cb))


def _whole(shape):
    n = len(shape)
    return pl.BlockSpec(shape, lambda i: (0,) * n)


def _pick(dim, cap, mult):
    best = None
    d = mult
    while d <= min(dim, cap):
        if dim % d == 0:
            best = d
        d += mult
    return best if best is not None else dim


def _hs(h):
    return slice(h * HP, (h + 1) * HP)


def _matmul(name, a, b, *, ta=False, tb=False, out_dtype=F32, addend=None, alpha=1.0, after=()):
    if ta:
        k, m = a.shape
    else:
        m, k = a.shape
    if tb:
        n, k2 = b.shape
    else:
        k2, n = b.shape
    assert k == k2, (name, a.shape, b.shape)
    bm = _pick(m, MM_BLOCK_CAP, 128 if ta else 16)
    bn = _pick(n, MM_BLOCK_CAP, 128)
    bk = _pick(k, MM_BLOCK_CAP, 128 if (not ta or tb) else 16)
    nk = k // bk
    dims = (((0 if ta else 1,), (1 if tb else 0,)), ((), ()))
    has_add = addend is not None

    def body(*refs):
        a_ref, b_ref = refs[:2]
        add_ref = refs[2] if has_add else None
        o_ref = refs[3 if has_add else 2]

        def finish(r):
            if has_add:
                r = r + alpha * add_ref[...]
            o_ref[...] = r.astype(o_ref.dtype)

        part = lax.dot_general(a_ref[...], b_ref[...], dims, preferred_element_type=F32)
        if nk == 1:
            finish(part)
            return
        acc_ref = refs[-1]
        kk = pl.program_id(2)

        @pl.when(kk == 0)
        def _():
            acc_ref[...] = part

        @pl.when(kk > 0)
        def _():
            acc_ref[...] += part

        @pl.when(kk == nk - 1)
        def _():
            finish(acc_ref[...])

    a_spec = pl.BlockSpec((bk, bm), lambda i, j, l: (l, i)) if ta else pl.BlockSpec((bm, bk), lambda i, j, l: (i, l))
    b_spec = pl.BlockSpec((bn, bk), lambda i, j, l: (j, l)) if tb else pl.BlockSpec((bk, bn), lambda i, j, l: (l, j))
    o_spec = pl.BlockSpec((bm, bn), lambda i, j, l: (i, j))
    ins = [(a, a_spec), (b, b_spec)]
    if has_add:
        ins.append((addend, o_spec))
    return _call(name, body, (m // bm, n // bn, nk), ins, [(_sds((m, n), out_dtype), o_spec)],
                 scratch=[pltpu.VMEM((bm, bn), F32)] if nk > 1 else [],
                 sem=("parallel", "parallel", "arbitrary"), after=after)[0]


def _ln_stats(z):
    mu = jnp.mean(z, axis=-1, keepdims=True)
    zc = z - mu
    var = jnp.mean(zc * zc, axis=-1, keepdims=True)
    rstd = lax.rsqrt(var + LN_EPS)
    return zc * rstd, rstd


def _ln_fwd(name, a, res, g, b, after=()):
    r, d = a.shape
    br = ROW_BLOCK
    has_res = res is not None

    def body(*refs):
        if has_res:
            a_ref, r_ref, g_ref, b_ref, y_ref, yb_ref = refs
            z = ALPHA * a_ref[...] + r_ref[...]
        else:
            a_ref, g_ref, b_ref, y_ref, yb_ref = refs
            z = a_ref[...]
        xhat, _ = _ln_stats(z)
        y = xhat * g_ref[...] + b_ref[...]
        y_ref[...] = y
        yb_ref[...] = y.astype(yb_ref.dtype)

    ins = [(a, _rows(br, d))]
    if has_res:
        ins.append((res, _rows(br, d)))
    ins += [(g.reshape(1, d), _whole((1, d))), (b.reshape(1, d), _whole((1, d)))]
    outs = [(_sds((r, d), F32), _rows(br, d)), (_sds((r, d), MXU_DTYPE), _rows(br, d))]
    return _call(name, body, (r // br,), ins, outs, sem=("parallel",), after=after)


def _ln_bwd(name, a, res, dy, g, after=()):
    r, d = a.shape
    br = ROW_BLOCK
    has_res = res is not None

    def body(*refs):
        if has_res:
            a_ref, r_ref, dy_ref, g_ref, dz_ref, dzb_ref, dg_ref, db_ref = refs
            z = ALPHA * a_ref[...] + r_ref[...]
        else:
            a_ref, dy_ref, g_ref, dz_ref, dzb_ref, dg_ref, db_ref = refs
            z = a_ref[...]
        xhat, rstd = _ln_stats(z)
        dyv = dy_ref[...]
        dyg = dyv * g_ref[...]
        m1 = jnp.mean(dyg, axis=-1, keepdims=True)
        m2 = jnp.mean(dyg * xhat, axis=-1, keepdims=True)
        dz = rstd * (dyg - m1 - xhat * m2)
        dz_ref[...] = dz
        dzb_ref[...] = dz.astype(dzb_ref.dtype)

        @pl.when(pl.program_id(0) == 0)
        def _():
            dg_ref[...] = jnp.zeros_like(dg_ref)
            db_ref[...] = jnp.zeros_like(db_ref)

        dg_ref[...] += jnp.sum(dyv * xhat, axis=0, keepdims=True)
        db_ref[...] += jnp.sum(dyv, axis=0, keepdims=True)

    ins = [(a, _rows(br, d))]
    if has_res:
        ins.append((res, _rows(br, d)))
    ins += [(dy, _rows(br, d)), (g.reshape(1, d), _whole((1, d)))]
    outs = [(_sds((r, d), F32), _rows(br, d)), (_sds((r, d), MXU_DTYPE), _rows(br, d)),
            (_sds((1, d), F32), _whole((1, d))), (_sds((1, d), F32), _whole((1, d)))]
    return _call(name, body, (r // br,), ins, outs, sem=("arbitrary",), after=after)


LATENTS = ((R_QLAT // Q_RANK, Q_RANK), (R_KVLAT // KV_RANK, KV_RANK))


def _latent_norm_fwd(proj_r, gains):
    r = proj_r.shape[0]
    br = ROW_BLOCK

    def body(xq_ref, xk_ref, gq_ref, gk_ref, yq_ref, yk_ref):
        for x_ref, g_ref, y_ref in ((xq_ref, gq_ref, yq_ref), (xk_ref, gk_ref, yk_ref)):
            x = x_ref[...]
            rstd = lax.rsqrt(jnp.mean(x * x, axis=-1, keepdims=True) + RMS_EPS)
            y_ref[...] = (x * rstd * g_ref[...]).astype(y_ref.dtype)

    return _call("latent_norm_fwd", body, (r // br,),
                 [(proj_r, _rows(br, wd, cb)) for cb, wd in LATENTS]
                 + [(g.reshape(1, wd), _whole((1, wd))) for g, (_, wd) in zip(gains, LATENTS)],
                 [(_sds((r, wd), MXU_DTYPE), _rows(br, wd)) for _, wd in LATENTS], sem=("parallel",))


def _latent_norm_bwd(proj_r, dys, gains):
    r = proj_r.shape[0]
    br = ROW_BLOCK

    def body(xq_ref, xk_ref, dq_ref, dk_ref, gq_ref, gk_ref, oq_ref, ok_ref, dgq_ref, dgk_ref):
        @pl.when(pl.program_id(0) == 0)
        def _():
            dgq_ref[...] = jnp.zeros_like(dgq_ref)
            dgk_ref[...] = jnp.zeros_like(dgk_ref)

        for x_ref, dy_ref, g_ref, dx_ref, dg_ref in ((xq_ref, dq_ref, gq_ref, oq_ref, dgq_ref),
                                                     (xk_ref, dk_ref, gk_ref, ok_ref, dgk_ref)):
            x = x_ref[...]
            rstd = lax.rsqrt(jnp.mean(x * x, axis=-1, keepdims=True) + RMS_EPS)
            nrm = x * rstd
            dyv = dy_ref[...]
            dyg = dyv * g_ref[...]
            dx_ref[...] = (rstd * (dyg - nrm * jnp.mean(dyg * nrm, axis=-1, keepdims=True))).astype(dx_ref.dtype)
            dg_ref[...] += jnp.sum(dyv * nrm, axis=0, keepdims=True)

    return _call("latent_norm_bwd", body, (r // br,),
                 [(proj_r, _rows(br, wd, cb)) for cb, wd in LATENTS]
                 + [(dy, _rows(br, wd)) for dy, (_, wd) in zip(dys, LATENTS)]
                 + [(g.reshape(1, wd), _whole((1, wd))) for g, (_, wd) in zip(gains, LATENTS)],
                 [(_sds((r, wd), MXU_DTYPE), _rows(br, wd)) for _, wd in LATENTS]
                 + [(_sds((1, wd), F32), _whole((1, wd))) for _, wd in LATENTS], sem=("arbitrary",))


def _lane_iota(shape):
    return lax.broadcasted_iota(jnp.int32, shape, 1)


def _rotary(t, c, s, lane, sign):
    second = pltpu.roll(t, HP - HALF, axis=1)
    first = pltpu.roll(t, HALF, axis=1)
    lo = (lane >= LANE_PE) & (lane < LANE_PE + HALF)
    hi = (lane >= LANE_PE + HALF) & (lane < LANE_PE + ROPE)
    return jnp.where(lo, t * c - sign * second * s, jnp.where(hi, t * c + sign * first * s, t))


def _rope_fwd(q_raw, k_part, proj_r, cos_t, sin_t):
    r = q_raw.shape[0]
    br = ROW_BLOCK

    def body(q_ref, k_ref, t_ref, c_ref, s_ref, qo_ref, ko_ref):
        c = c_ref[...]
        s = s_ref[...]
        lane = _lane_iota((br, HP))
        pe = (lane >= LANE_PE) & (lane < LANE_PE + ROPE)
        kp = jnp.where(pe, _rotary(t_ref[...], c, s, lane, 1.0), 0.0)
        for h in range(HEADS):
            qo_ref[:, _hs(h)] = (_rotary(q_ref[:, _hs(h)], c, s, lane, 1.0) * MLA_SCALE).astype(qo_ref.dtype)
            ko_ref[:, _hs(h)] = (k_ref[:, _hs(h)] + kp).astype(ko_ref.dtype)

    blk = _rows(br, HP)
    wide = _rows(br, HW)
    return _call("rope_fwd", body, (r // br,),
                 [(q_raw, wide), (k_part, wide), (proj_r, _rows(br, HP, R_LAST // HP)), (cos_t, blk), (sin_t, blk)],
                 [(_sds((r, HW), MXU_DTYPE), wide)] * 2, sem=("parallel",))


def _rope_bwd(dq, dk, dv, dfl, cos_t, sin_t):
    r = dq.shape[0]
    br = ROW_BLOCK

    def body(dq_ref, dk_ref, dv_ref, fl_ref, c_ref, s_ref, dqo_ref, dkv_ref, dl_ref):
        c = c_ref[...]
        s = s_ref[...]
        lane = _lane_iota((br, HP))
        pe = (lane >= LANE_PE) & (lane < LANE_PE + ROPE)
        acc = jnp.zeros((br, HP), F32)
        for h in range(HEADS):
            dqo_ref[:, _hs(h)] = (_rotary(dq_ref[:, _hs(h)], c, s, lane, -1.0) * MLA_SCALE).astype(dqo_ref.dtype)
            dkh = dk_ref[:, _hs(h)]
            acc = acc + dkh
            dkv_ref[:, _hs(h)] = dkh.astype(dkv_ref.dtype)
            dkv_ref[:, _hs(HEADS + h)] = dv_ref[:, _hs(h)].astype(dkv_ref.dtype)
        dl_ref[...] = (jnp.where(pe, _rotary(acc, c, s, lane, -1.0), 0.0) + fl_ref[...]).astype(dl_ref.dtype)

    blk = _rows(br, HP)
    wide = _rows(br, HW)
    return _call("rope_bwd", body, (r // br,),
                 [(dq, wide), (dk, wide), (dv, wide), (dfl, blk), (cos_t, blk), (sin_t, blk)],
                 [(_sds((r, HW), MXU_DTYPE), wide), (_sds((r, 2 * HW), MXU_DTYPE), _rows(br, 2 * HW)),
                  (_sds((r, HP), MXU_DTYPE), blk)],
                 sem=("parallel",))


def _log_sigmoid(x):
    return jnp.minimum(x, 0.0) - jnp.log(1.0 + jnp.exp(-jnp.abs(x)))


def _head_lane(x, h, lane):
    return jnp.sum(jnp.where(lane == h, x, 0.0), axis=1, keepdims=True)


def _forget_fwd(proj_r, bf_row):
    r = proj_r.shape[0]
    br = ROW_BLOCK

    def body(t_ref, b_ref, ob_ref, ot_ref, carry_ref):
        @pl.when(pl.program_id(0) == 0)
        def _():
            carry_ref[...] = jnp.zeros_like(carry_ref)

        x = t_ref[...] + b_ref[...]
        lane = _lane_iota(x.shape)
        lf = jnp.where((lane >= LANE_FL) & (lane < LANE_FL + HEADS), _log_sigmoid(x), 0.0)
        tri = (lax.broadcasted_iota(jnp.int32, (br, br), 0) >= lax.broadcasted_iota(jnp.int32, (br, br), 1)).astype(F32)
        cum = jnp.dot(tri, lf, precision=HIGHEST, preferred_element_type=F32) + carry_ref[0:1, :]
        for h in range(HEADS):
            ob_ref[:, _hs(h)] = jnp.broadcast_to(_head_lane(cum, LANE_FL + h, lane), (br, HP))
        ot_ref[...] = cum.T[LANE_FL:LANE_FL + HEADS, :]
        carry_ref[...] = jnp.broadcast_to(cum[br - 1:br, :], carry_ref.shape)

    return _call("forget_fwd", body, (r // br,),
                 [(proj_r, _rows(br, HP, R_LAST // HP)), (bf_row, _whole((1, HP)))],
                 [(_sds((r, HW), F32), _rows(br, HW)), (_sds((HEADS, r), F32), pl.BlockSpec((HEADS, br), lambda i: (0, i)))],
                 scratch=[pltpu.VMEM((8, HP), F32)], sem=("arbitrary",))


def _forget_bwd(proj_r, bf_row, dcq_t, dck_b):
    r = proj_r.shape[0]
    br = ROW_BLOCK
    nb = r // br

    def body(t_ref, b_ref, dcq_ref, dck_ref, o_ref, db_ref, carry_ref):
        @pl.when(pl.program_id(0) == 0)
        def _():
            carry_ref[...] = jnp.zeros_like(carry_ref)
            db_ref[...] = jnp.zeros_like(db_ref)

        lane = _lane_iota((br, HP))
        dc = jnp.concatenate([dcq_ref[...], jnp.zeros((HP - HEADS, br), F32)], axis=0).T
        for h in range(HEADS):
            dc = dc + jnp.where(lane == LANE_FL + h, dck_ref[:, h * HP:h * HP + 1], 0.0)
        triu = (lax.broadcasted_iota(jnp.int32, (br, br), 0) <= lax.broadcasted_iota(jnp.int32, (br, br), 1)).astype(F32)
        dlf = jnp.dot(triu, dc, precision=HIGHEST, preferred_element_type=F32) + carry_ref[0:1, :]
        carry_ref[...] = jnp.broadcast_to(dlf[0:1, :], carry_ref.shape)
        x = t_ref[...] + b_ref[...]
        dfl = jnp.where((lane >= LANE_FL) & (lane < LANE_FL + HEADS), dlf * jax.nn.sigmoid(-x), 0.0)
        o_ref[...] = dfl
        db_ref[...] += jnp.sum(dfl, axis=0, keepdims=True)

    rev = pl.BlockSpec((br, HP), lambda i: (nb - 1 - i, 0))
    return _call("forget_bwd", body, (nb,),
                 [(proj_r, pl.BlockSpec((br, HP), lambda i: (nb - 1 - i, R_LAST // HP))), (bf_row, _whole((1, HP))),
                  (dcq_t, pl.BlockSpec((HEADS, br), lambda i: (0, nb - 1 - i))),
                  (dck_b, pl.BlockSpec((br, HW), lambda i: (nb - 1 - i, 0)))],
                 [(_sds((r, HP), F32), rev), (_sds((1, HP), F32), _whole((1, HP)))],
                 scratch=[pltpu.VMEM((8, HP), F32)], sem=("arbitrary",))


def _gate_fwd(proj_r, b_gate, bm, bfx):
    r, d = bm.shape
    br = ROW_BLOCK
    cb = R_GATE // d

    def body(gm_ref, gf_ref, b1_ref, b2_ref, bm_ref, bf_ref, o_ref):
        g1 = jax.nn.sigmoid(gm_ref[...] + b1_ref[...])
        g2 = jax.nn.sigmoid(gf_ref[...] + b2_ref[...])
        o_ref[...] = (g1 * bm_ref[...].astype(F32) + g2 * bf_ref[...].astype(F32)).astype(o_ref.dtype)

    b1 = b_gate[:d].reshape(1, d)
    b2 = b_gate[d:].reshape(1, d)
    return _call("gate_fwd", body, (r // br,),
                 [(proj_r, _rows(br, d, cb)), (proj_r, _rows(br, d, cb + 1)), (b1, _whole((1, d))), (b2, _whole((1, d))),
                  (bm, _rows(br, d)), (bfx, _rows(br, d))],
                 [(_sds((r, d), MXU_DTYPE), _rows(br, d))], sem=("parallel",))[0]


def _gate_bwd(proj_r, b_gate, bm, bfx, dmerged):
    r, d = bm.shape
    br = ROW_BLOCK
    cb = R_GATE // d

    def body(gm_ref, gf_ref, b1_ref, b2_ref, bm_ref, bf_ref, dm_ref, dbm_ref, dbf_ref, dgl_ref, dbg_ref):
        g1 = jax.nn.sigmoid(gm_ref[...] + b1_ref[...])
        g2 = jax.nn.sigmoid(gf_ref[...] + b2_ref[...])
        dm = dm_ref[...].astype(F32)
        dbm_ref[...] = (dm * g1).astype(dbm_ref.dtype)
        dbf_ref[...] = (dm * g2).astype(dbf_ref.dtype)
        dl1 = dm * bm_ref[...].astype(F32) * (g1 * (1.0 - g1))
        dl2 = dm * bf_ref[...].astype(F32) * (g2 * (1.0 - g2))
        dgl_ref[:, 0:d] = dl1.astype(dgl_ref.dtype)
        dgl_ref[:, d:2 * d] = dl2.astype(dgl_ref.dtype)

        @pl.when(pl.program_id(0) == 0)
        def _():
            dbg_ref[...] = jnp.zeros_like(dbg_ref)

        dbg_ref[:, 0:d] += jnp.sum(dl1, axis=0, keepdims=True)
        dbg_ref[:, d:2 * d] += jnp.sum(dl2, axis=0, keepdims=True)

    b1 = b_gate[:d].reshape(1, d)
    b2 = b_gate[d:].reshape(1, d)
    return _call("gate_bwd", body, (r // br,),
                 [(proj_r, _rows(br, d, cb)), (proj_r, _rows(br, d, cb + 1)), (b1, _whole((1, d))), (b2, _whole((1, d))),
                  (bm, _rows(br, d)), (bfx, _rows(br, d)), (dmerged, _rows(br, d))],
                 [(_sds((r, d), MXU_DTYPE), _rows(br, d)), (_sds((r, d), MXU_DTYPE), _rows(br, d)),
                  (_sds((r, 2 * d), MXU_DTYPE), _rows(br, 2 * d)), (_sds((1, 2 * d), F32), _whole((1, 2 * d)))],
                 sem=("arbitrary",))


HALO = 16
GLU_BWD_BLOCK = 256
COPY_ROWS = 512


def _conv_taps(gp, halo, first_block):
    halo = jnp.where(first_block, 0.0, halo.astype(F32))
    rid = lax.broadcasted_iota(jnp.int32, gp.shape, 0)
    last, prev = halo[HALO - 1:HALO, :], halo[HALO - 2:HALO - 1, :]
    g1 = jnp.where(rid == 0, last, pltpu.roll(gp, 1, axis=0))
    g2 = jnp.where(rid == 0, prev, jnp.where(rid == 1, last, pltpu.roll(gp, 2, axis=0)))
    return g1, g2


def _prev_halo(br, c):
    return pl.BlockSpec((HALO, c), lambda i: (jnp.maximum(i * (br // HALO) - 1, 0), 0))


def _glu_fwd(up, conv_w, conv_b):
    r = up.shape[0]
    c = D_FF
    br = ROW_BLOCK

    def body(gp_ref, halo_ref, val_ref, w_ref, b_ref, o_ref):
        gp = gp_ref[...].astype(F32)
        g1, g2 = _conv_taps(gp, halo_ref[...], pl.program_id(0) == 0)
        gate = w_ref[0:1, :] * g2 + w_ref[1:2, :] * g1 + w_ref[2:3, :] * gp + b_ref[...]
        o_ref[...] = (gate * jax.nn.sigmoid(gate) * val_ref[...].astype(F32)).astype(o_ref.dtype)

    return _call("glu_fwd", body, (r // br,),
                 [(up, _rows(br, c, 0)), (up, _prev_halo(br, c)), (up, _rows(br, c, 1)),
                  (conv_w, _whole((3, c))), (conv_b.reshape(1, c), _whole((1, c)))],
                 [(_sds((r, c), MXU_DTYPE), _rows(br, c))], sem=("parallel",))[0]


def _glu_bwd(up, conv_w, conv_b, d_act):
    r = up.shape[0]
    c = D_FF
    br = GLU_BWD_BLOCK
    nb = r // br

    def body(gp_ref, halo_ref, val_ref, da_ref, gpn_ref, valn_ref, dan_ref, w_ref, b_ref, o_ref, dw_ref, db_ref):
        i = pl.program_id(0)
        w0, w1, w2, bias = w_ref[0:1, :], w_ref[1:2, :], w_ref[2:3, :], b_ref[...]

        def d_gate(gp, g1, g2, val, da):
            gate = w0 * g2 + w1 * g1 + w2 * gp + bias
            sg = jax.nn.sigmoid(gate)
            return da * val * (sg * (1.0 + gate * (1.0 - sg))), da * (gate * sg)

        gp = gp_ref[...].astype(F32)
        g1, g2 = _conv_taps(gp, halo_ref[...], i == 0)
        dg, dv = d_gate(gp, g1, g2, val_ref[...].astype(F32), da_ref[...].astype(F32))
        gpn = gpn_ref[...].astype(F32)
        g1n, g2n = _conv_taps(gpn, gp[br - HALO:, :], False)
        dgn, _ = d_gate(gpn, g1n, g2n, valn_ref[...].astype(F32), dan_ref[...].astype(F32))
        dgn = jnp.where(i == nb - 1, 0.0, dgn)
        rid = lax.broadcasted_iota(jnp.int32, dg.shape, 0)
        u1 = jnp.where(rid == br - 1, dgn[0:1, :], pltpu.roll(dg, br - 1, axis=0))
        u2 = jnp.where(rid == br - 1, dgn[1:2, :], jnp.where(rid == br - 2, dgn[0:1, :], pltpu.roll(dg, br - 2, axis=0)))
        o_ref[:, 0:c] = (w2 * dg + w1 * u1 + w0 * u2).astype(o_ref.dtype)
        o_ref[:, c:2 * c] = dv.astype(o_ref.dtype)

        @pl.when(i == 0)
        def _():
            dw_ref[...] = jnp.zeros_like(dw_ref)
            db_ref[...] = jnp.zeros_like(db_ref)

        dw_ref[0:1, :] += jnp.sum(dg * g2, axis=0, keepdims=True)
        dw_ref[1:2, :] += jnp.sum(dg * g1, axis=0, keepdims=True)
        dw_ref[2:3, :] += jnp.sum(dg * gp, axis=0, keepdims=True)
        db_ref[...] += jnp.sum(dg, axis=0, keepdims=True)

    nxt = lambda cb: pl.BlockSpec((HALO, c), lambda i: (jnp.minimum((i + 1) * (br // HALO), r // HALO - 1), cb))
    return _call("glu_bwd", body, (nb,),
                 [(up, _rows(br, c, 0)), (up, _prev_halo(br, c)), (up, _rows(br, c, 1)), (d_act, _rows(br, c)),
                  (up, nxt(0)), (up, nxt(1)), (d_act, nxt(0)),
                  (conv_w, _whole((3, c))), (conv_b.reshape(1, c), _whole((1, c)))],
                 [(_sds((r, 2 * c), MXU_DTYPE), _rows(br, 2 * c)),
                  (_sds((8, c), F32), _whole((8, c))), (_sds((1, c), F32), _whole((1, c)))],
                 sem=("arbitrary",))


def _token_specs(seq, d):
    br = TOKEN_BLOCK
    nxb = seq // br
    main = pl.BlockSpec((br, d), lambda i: (jnp.minimum(i, nxb - 1), 0))
    tail = pl.BlockSpec((N_META, d), lambda i: (jnp.clip(i * (br // N_META) - 1, 0, seq // N_META - 1), 0))
    return main, tail


def _padded_block(main_ref, tail_ref, first, seq):
    br = TOKEN_BLOCK
    i = pl.program_id(0)
    nxb = seq // br
    main = jnp.where(i < nxb, main_ref[...], 0.0)
    head = jnp.where(i == 0, first, jnp.where(i <= nxb, tail_ref[...], 0.0))
    return jnp.concatenate([head, main[:br - N_META]], axis=0)


def _ln_emb_fwd(x, meta, g, b, rows, after=()):
    seq, d = x.shape
    br = TOKEN_BLOCK
    assert seq % br == 0 and br % N_META == 0 and rows % br == 0

    def body(x_ref, tail_ref, meta_ref, g_ref, b_ref, y_ref, yb_ref):
        z = _padded_block(x_ref, tail_ref, meta_ref[...], seq)
        xhat, _ = _ln_stats(z)
        y = xhat * g_ref[...] + b_ref[...]
        y_ref[...] = y
        yb_ref[...] = y.astype(yb_ref.dtype)

    main, tail = _token_specs(seq, d)
    return _call("ln_emb_fwd", body, (rows // br,),
                 [(x, main), (x, tail), (meta, _whole((N_META, d))), (g.reshape(1, d), _whole((1, d))),
                  (b.reshape(1, d), _whole((1, d)))],
                 [(_sds((rows, d), F32), _rows(br, d)), (_sds((rows, d), MXU_DTYPE), _rows(br, d))],
                 sem=("parallel",), after=after)


def _ln_emb_bwd(x, meta, dh0, g):
    seq, d = x.shape
    br = TOKEN_BLOCK
    step = br // N_META

    def ln_bwd(z, dy, gv):
        xhat, rstd = _ln_stats(z)
        dyg = dy * gv
        m1 = jnp.mean(dyg, axis=-1, keepdims=True)
        m2 = jnp.mean(dyg * xhat, axis=-1, keepdims=True)
        dz = rstd * (dyg - m1 - xhat * m2)
        return dz, jnp.sum(dy * xhat, axis=0, keepdims=True), jnp.sum(dy, axis=0, keepdims=True)

    def body(x_ref, dh_ref, nxt_ref, meta_ref, top_ref, g_ref, dx_ref, dm_ref, dg_ref, db_ref):
        gv = g_ref[...]
        dy = jnp.concatenate([dh_ref[N_META:, :], nxt_ref[...]], axis=0)
        dz, dg, db = ln_bwd(x_ref[...], dy, gv)
        dx_ref[...] = dz

        @pl.when(pl.program_id(0) == 0)
        def _():
            dzm, dgm, dbm = ln_bwd(meta_ref[...], top_ref[...], gv)
            dm_ref[...] = dzm
            dg_ref[...] = dgm
            db_ref[...] = dbm

        dg_ref[...] += dg
        db_ref[...] += db

    small = _whole((N_META, d))
    return _call("ln_emb_bwd", body, (seq // br,),
                 [(x, _rows(br, d)), (dh0, _rows(br, d)), (dh0, pl.BlockSpec((N_META, d), lambda i: ((i + 1) * step, 0))),
                  (meta, small), (dh0, small), (g.reshape(1, d), _whole((1, d)))],
                 [(_sds((seq, d), F32), _rows(br, d)), (_sds((N_META, d), F32), small),
                  (_sds((1, d), F32), _whole((1, d))), (_sds((1, d), F32), _whole((1, d)))], sem=("arbitrary",))


def _ln_ffn_loss(h1, f, tgt, g, b):
    r, d = h1.shape
    seq = tgt.shape[0]
    br = TOKEN_BLOCK

    def body(a_ref, r_ref, t_ref, tail_ref, g_ref, b_ref, l_ref):
        err = _loss_err(a_ref, r_ref, t_ref, tail_ref, g_ref, b_ref, seq)[0]

        @pl.when(pl.program_id(0) == 0)
        def _():
            l_ref[...] = jnp.zeros_like(l_ref)

        l_ref[...] += jnp.sum(jnp.sum(err * err, axis=1, keepdims=True), axis=0, keepdims=True) * (0.5 / d)

    main, tail = _token_specs(seq, d)
    return _call("ln_ffn_loss", body, (r // br,),
                 [(h1, _rows(br, d)), (f, _rows(br, d)), (tgt, main), (tgt, tail),
                  (g.reshape(1, d), _whole((1, d))), (b.reshape(1, d), _whole((1, d)))],
                 [(_sds((1, 1), F32), _whole((1, 1)))], sem=("arbitrary",))[0]


def _loss_err(a_ref, r_ref, t_ref, tail_ref, g_ref, b_ref, seq):
    br, d = a_ref.shape
    xhat, rstd = _ln_stats(ALPHA * a_ref[...] + r_ref[...])
    y = xhat * g_ref[...] + b_ref[...]
    t = _padded_block(t_ref, tail_ref, jnp.zeros((N_META, d), F32), seq)
    rid = lax.broadcasted_iota(jnp.int32, (br, d), 0) + pl.program_id(0) * br
    valid = (rid >= N_META) & (rid < N_META + seq)
    return jnp.where(valid, y - t, 0.0), xhat, rstd


def _ln_ffn_bwd(h1, f, tgt, g, b):
    r, d = h1.shape
    seq = tgt.shape[0]
    br = TOKEN_BLOCK

    def body(a_ref, r_ref, t_ref, tail_ref, g_ref, b_ref, dz_ref, dzb_ref, dg_ref, db_ref):
        err, xhat, rstd = _loss_err(a_ref, r_ref, t_ref, tail_ref, g_ref, b_ref, seq)
        dyv = err * (1.0 / d)
        dyg = dyv * g_ref[...]
        m1 = jnp.mean(dyg, axis=-1, keepdims=True)
        m2 = jnp.mean(dyg * xhat, axis=-1, keepdims=True)
        dz = rstd * (dyg - m1 - xhat * m2)
        dz_ref[...] = dz
        dzb_ref[...] = dz.astype(dzb_ref.dtype)

        @pl.when(pl.program_id(0) == 0)
        def _():
            dg_ref[...] = jnp.zeros_like(dg_ref)
            db_ref[...] = jnp.zeros_like(db_ref)

        dg_ref[...] += jnp.sum(dyv * xhat, axis=0, keepdims=True)
        db_ref[...] += jnp.sum(dyv, axis=0, keepdims=True)

    main, tail = _token_specs(seq, d)
    return _call("ln_ffn_bwd", body, (r // br,),
                 [(h1, _rows(br, d)), (f, _rows(br, d)), (tgt, main), (tgt, tail),
                  (g.reshape(1, d), _whole((1, d))), (b.reshape(1, d), _whole((1, d)))],
                 [(_sds((r, d), F32), _rows(br, d)), (_sds((r, d), MXU_DTYPE), _rows(br, d)),
                  (_sds((1, d), F32), _whole((1, d))), (_sds((1, d), F32), _whole((1, d)))], sem=("arbitrary",))


def _attn_fwd(name, q, k, v, cum_b=None, cum_t=None):
    (qa, qg), (ka, kg), (va, vg) = q, k, v
    r = qa.shape[0]
    tq, tk = ATT_TQ, ATT_TK
    nq, nk = r // tq, r // tk
    bias = cum_b is not None

    def body(*refs):
        if bias:
            q_ref, k_ref, vt_ref, cb_ref, ct_ref, o_ref, ob_ref, lse_ref = refs
        else:
            q_ref, k_ref, vt_ref, o_ref, ob_ref, lse_ref = refs
        i = pl.program_id(1)
        qs = [q_ref[:, _hs(hh)] for hh in range(hg)]
        cqs = [ct_ref[hh] for hh in range(hg)] if bias else None
        diff = lax.broadcasted_iota(jnp.int32, (tk, tq), 0) - lax.broadcasted_iota(jnp.int32, (tk, tq), 1)

        def step(j, carry, masked):
            keys = pl.ds(pl.multiple_of(j * tk, tk), tk)
            out = []
            for hh in range(hg):
                m, l, acc = carry[hh]
                kt = k_ref[keys, _hs(hh)]
                s = lax.dot_general(kt, qs[hh], NT, preferred_element_type=F32)
                if bias:
                    s = s + (cqs[hh] - cb_ref[keys, hh * HP:hh * HP + 1])
                if masked:
                    s = jnp.where(diff <= i * tq - j * tk, s, NEG_INF)
                m_new = jnp.maximum(m, jnp.max(s, axis=0, keepdims=True))
                p = jnp.exp(s - m_new)
                a = jnp.exp(m - m_new)
                l = a * l + jnp.sum(p, axis=0, keepdims=True)
                acc = a * acc + jnp.dot(vt_ref[j, _hs(hh), :], p.astype(kt.dtype), preferred_element_type=F32)
                out.append((m_new, l, acc))
            return tuple(out)

        n_clear = (i * tq + 1) // tk
        n_all = ((i + 1) * tq - 1) // tk + 1
        carry = tuple((jnp.full((1, tq), NEG_INF, F32), jnp.zeros((1, tq), F32), jnp.zeros((HP, tq), F32))
                      for _ in range(hg))
        carry = lax.fori_loop(0, n_clear, lambda j, c: step(j, c, False), carry)
        carry = lax.fori_loop(n_clear, n_all, lambda j, c: step(j, c, True), carry)
        for hh in range(hg):
            m, l, acc = carry[hh]
            o = (acc / l).T
            o_ref[:, _hs(hh)] = o
            ob_ref[:, hh * V_DIM:(hh + 1) * V_DIM] = o[:, :V_DIM].astype(ob_ref.dtype)
            lse_ref[hh] = m + jnp.log(l)

    hg = ATT_HEADS_FWD
    w = hg * HP
    gpw = HW // w
    tile = lambda g: pl.BlockSpec((tq, w), lambda h, i: (i, g * gpw + h))
    res = lambda g: pl.BlockSpec((r, w), lambda h, i: (0, g * gpw + h))
    v_t = _key_tiles_transposed(name + "_vt", va, vg)
    ins = [(qa, tile(qg)), (ka, res(kg)), (v_t, pl.BlockSpec((nk, w, tk), lambda h, i: (0, h, 0)))]
    if bias:
        ins += [(cum_b, res(0)),
                (cum_t.reshape(HEADS, nq, 1, tq), pl.BlockSpec((hg, None, 1, tq), lambda h, i: (h, i, 0, 0)))]
    outs = [(_sds((r, HW), F32), tile(0)),
            (_sds((r, HEADS * V_DIM), MXU_DTYPE), pl.BlockSpec((tq, hg * V_DIM), lambda h, i: (i, h))),
            (_sds((HEADS, nq, 1, tq), F32), pl.BlockSpec((hg, None, 1, tq), lambda h, i: (h, i, 0, 0)))]
    o, ob, lse = _call(name, body, (gpw, nq), ins, outs, sem=("parallel", "parallel"))
    return o, ob, lse.reshape(HEADS, r)


def _key_tiles_transposed(name, a, group):
    r = a.shape[0]
    tk = ATT_TK

    def body(x_ref, o_ref):
        for h in range(HEADS):
            o_ref[_hs(h), :] = x_ref[:, _hs(h)].astype(F32).T.astype(o_ref.dtype)

    return _call(name, body, (r // tk,),
                 [(a, pl.BlockSpec((tk, HW), lambda j: (j, group)))],
                 [(_sds((r // tk, HW, tk), a.dtype), pl.BlockSpec((None, HW, tk), lambda j: (j, 0, 0)))],
                 sem=("parallel",))[0]


def _attn_bwd(name, q, k, v, do_b, o, lse_t, cum_b=None, cum_t=None, out_dtype=F32, after=()):
    (qa, qg), (ka, kg), (va, vg) = q, k, v
    r = qa.shape[0]
    tq, tk = ATT_TQ, ATT_TK
    nq, nk = r // tq, r // tk
    bias = cum_b is not None

    def body(*refs):
        if bias:
            (q_ref, k_ref, v_ref, do_ref, o_ref, lse_ref, cb_ref, ct_ref,
             dq_ref, dk_ref, dv_ref, dcq_ref, dck_ref, dqt_ref, dl_ref) = refs
        else:
            q_ref, k_ref, v_ref, do_ref, o_ref, lse_ref, dq_ref, dk_ref, dv_ref, dqt_ref, dl_ref = refs
        j = pl.program_id(1)

        @pl.when(j == 0)
        def _():
            dqt_ref[...] = jnp.zeros_like(dqt_ref)
            if bias:
                dcq_ref[...] = jnp.zeros_like(dcq_ref)
            for hh in range(hg):
                for i in range(nq):
                    rows = slice(i * tq, (i + 1) * tq)
                    prod = do_ref[rows, _hs(hh)].astype(F32) * o_ref[rows, _hs(hh)]
                    dl_ref[hh, i] = jnp.sum(prod.T, axis=0, keepdims=True)

        kts = [k_ref[:, _hs(hh)] for hh in range(hg)]
        vts = [v_ref[:, _hs(hh)] for hh in range(hg)]
        k_trs = [kt.astype(F32).T.astype(kt.dtype) for kt in kts]
        cks = [cb_ref[:, hh * HP:hh * HP + 1] for hh in range(hg)] if bias else None
        diff = lax.broadcasted_iota(jnp.int32, (tk, tq), 0) - lax.broadcasted_iota(jnp.int32, (tk, tq), 1)

        def step(i, carry, masked):
            rows = pl.ds(pl.multiple_of(i * tq, tq), tq)
            out = []
            for hh in range(hg):
                dk_acc, dv_acc, dck_acc = carry[hh]
                qt = q_ref[rows, _hs(hh)]
                dot = do_ref[rows, _hs(hh)]
                s = lax.dot_general(kts[hh], qt, NT, preferred_element_type=F32)
                if bias:
                    s = s + (ct_ref[hh, i] - cks[hh])
                if masked:
                    s = jnp.where(diff <= i * tq - j * tk, s, NEG_INF)
                p = jnp.exp(s - lse_ref[hh, i])
                dp = lax.dot_general(vts[hh], dot, NT, preferred_element_type=F32)
                ds = p * (dp - dl_ref[hh, i])
                pb = p.astype(dot.dtype)
                dsb = ds.astype(qt.dtype)
                dv_acc = dv_acc + jnp.dot(pb, dot, preferred_element_type=F32)
                dk_acc = dk_acc + jnp.dot(dsb, qt, preferred_element_type=F32)
                dqt_ref[hh, i] += jnp.dot(k_trs[hh], dsb, preferred_element_type=F32)
                if bias:
                    dcq_ref[hh, i] += jnp.sum(ds, axis=0, keepdims=True)
                    dck_acc = dck_acc - jnp.sum(ds, axis=1, keepdims=True)
                out.append((dk_acc, dv_acc, dck_acc))
            return tuple(out)

        i_first = (j * tk) // tq
        i_clear = jnp.minimum(((j + 1) * tk + tq - 2) // tq, nq)
        carry = tuple((jnp.zeros((tk, HP), F32), jnp.zeros((tk, HP), F32), jnp.zeros((tk, 1), F32)) for _ in range(hg))
        carry = lax.fori_loop(i_first, i_clear, lambda i, c: step(i, c, True), carry)
        carry = lax.fori_loop(i_clear, nq, lambda i, c: step(i, c, False), carry)
        for hh in range(hg):
            dk_acc, dv_acc, dck_acc = carry[hh]
            dk_ref[:, _hs(hh)] = dk_acc.astype(dk_ref.dtype)
            dv_ref[:, _hs(hh)] = dv_acc.astype(dv_ref.dtype)
            if bias:
                dck_ref[:, _hs(hh)] = jnp.broadcast_to(dck_acc, (tk, HP))

        @pl.when(j == nk - 1)
        def _():
            for hh in range(hg):
                for i in range(nq):
                    dq_ref[i * tq:(i + 1) * tq, _hs(hh)] = dqt_ref[hh, i].T.astype(dq_ref.dtype)

    hg = ATT_HEADS
    w = hg * HP
    gpw = HW // w
    res = lambda g: pl.BlockSpec((r, w), lambda h, j: (0, g * gpw + h))
    tile = lambda g: pl.BlockSpec((tk, w), lambda h, j: (j, g * gpw + h))
    rowv = pl.BlockSpec((hg, nq, 1, tq), lambda h, j: (h, 0, 0, 0))
    as_rows = lambda a: a.reshape(HEADS, nq, 1, tq)
    ins = [(qa, res(qg)), (ka, tile(kg)), (va, tile(vg)), (do_b, res(0)), (o, res(0)), (as_rows(lse_t), rowv)]
    outs = [(_sds((r, HW), out_dtype), res(0)), (_sds((r, HW), out_dtype), tile(0)), (_sds((r, HW), out_dtype), tile(0))]
    if bias:
        ins += [(cum_b, tile(0)), (as_rows(cum_t), rowv)]
        outs += [(_sds((HEADS, nq, 1, tq), F32), rowv), (_sds((r, HW), F32), tile(0))]
    res_out = _call(name, body, (gpw, nk), ins, outs,
                    scratch=[pltpu.VMEM((hg, nq, HP, tq), F32), pltpu.VMEM((hg, nq, 1, tq), F32)],
                    sem=("parallel", "arbitrary"), after=after)
    if bias:
        dq, dk, dv, dcq, dck = res_out
        return dq, dk, dv, dcq.reshape(HEADS, r), dck
    return res_out


MESH_ID = pl.DeviceIdType.MESH
ANY = pl.BlockSpec(memory_space=pl.ANY)


N_GATHER_COPIES = 8


def _allgather(name, shards):
    n = len(shards)

    def body(*refs):
        x_refs, out_refs = refs[:n], refs[n:2 * n]
        send_sems, recv_sems, local_sems = refs[2 * n:]
        x, y, c = lax.axis_index("x"), lax.axis_index("y"), lax.axis_index("c")
        me, sibling = (x, y, c), (x, y, 1 - c)
        xn, yn, dg = (1 - x, y, c), (x, 1 - y, c), (1 - x, 1 - y, c)
        other = lambda dev: (dev[0], dev[1], 1 - c)

        def slot(ti, dev, half=None):
            ref = out_refs[ti].at[4 * dev[0] + 2 * dev[1] + dev[2]]
            if half is None:
                return ref
            rows = shards[ti].shape[0] // 2
            return ref.at[pl.ds(half * rows, rows)]

        def copy(ti, k, block, to, half=None, src=None):
            return pltpu.make_async_remote_copy(
                src_ref=slot(ti, block, half) if src is None else src, dst_ref=slot(ti, block, half),
                send_sem=send_sems.at[ti, k], recv_sem=recv_sems.at[ti, k], device_id=to, device_id_type=MESH_ID)

        mine = [pltpu.make_async_copy(x_refs[ti], slot(ti, me), local_sems.at[ti]) for ti in range(n)]
        for cp in mine:
            cp.start()
        started = []

        def go(cp):
            cp.start()
            started.append(cp)

        for ti in range(n):
            go(copy(ti, 0, me, sibling, src=x_refs[ti]))
            go(copy(ti, 1, me, xn, src=x_refs[ti]))
            go(copy(ti, 2, me, yn, src=x_refs[ti]))
        for ti in range(n):
            copy(ti, 1, xn, me).wait_recv()
            go(copy(ti, 3, xn, yn, half=0))
            go(copy(ti, 5, xn, sibling))
            copy(ti, 2, yn, me).wait_recv()
            go(copy(ti, 4, yn, xn, half=1))
            go(copy(ti, 6, yn, sibling))
        for ti in range(n):
            copy(ti, 3, dg, me, half=0).wait_recv()
            copy(ti, 4, dg, me, half=1).wait_recv()
            go(copy(ti, 7, dg, sibling))
        for ti in range(n):
            copy(ti, 0, sibling, me).wait_recv()
            for k, dev in ((5, xn), (6, yn), (7, dg)):
                copy(ti, k, other(dev), me).wait_recv()
        for cp in started:
            cp.wait_send()
        for cp in mine:
            cp.wait()

    sems = pltpu.SemaphoreType.DMA((n, N_GATHER_COPIES))
    return pl.pallas_call(
        body, name=name, out_shape=[_sds((N_DEV,) + s.shape, s.dtype) for s in shards],
        in_specs=[ANY] * n, out_specs=[ANY] * n,
        scratch_shapes=[sems, sems, pltpu.SemaphoreType.DMA((n,))],
    )(*shards)


HBM = pl.BlockSpec(memory_space=pltpu.HBM)
SEM = pl.BlockSpec(memory_space=pltpu.SEMAPHORE)
EFFECT = pltpu.SideEffectType.DATAFLOW_SIDE_EFFECTING
N_PEER = N_DEV - 1


def _my_id():
    return 4 * lax.axis_index("x") + 2 * lax.axis_index("y") + lax.axis_index("c")


def _peers():
    x, y, c = lax.axis_index("x"), lax.axis_index("y"), lax.axis_index("c")
    out = []
    for k in range(1, N_DEV):
        px, py, pc = (1 - x if k & 4 else x, 1 - y if k & 2 else y, 1 - c if k & 1 else c)
        out.append(((px, py, pc), 4 * px + 2 * py + pc))
    return out


def _push_copies(src_refs, land_refs, send_sems, recv_sems, scatter, landing):
    me = _my_id()
    out = []
    for ti, (src, land) in enumerate(zip(src_refs, land_refs)):
        for k, (dev, pid) in enumerate(_peers()):
            out.append(pltpu.make_async_remote_copy(
                src_ref=src.at[pid] if scatter else src, dst_ref=land.at[pid if landing else me],
                send_sem=send_sems.at[ti * N_PEER + k], recv_sem=recv_sems.at[ti * N_PEER + k],
                device_id=dev, device_id_type=MESH_ID))
    return out


def _push_start(name, groups, scatter, after=None):
    sizes = [len(g) for g in groups]
    srcs = [a for g in groups for a in g]
    n = len(srcs)
    slot = lambda s: s.shape[1:] if scatter else s.shape
    lands = [lax.empty((N_DEV,) + slot(s), s.dtype) for s in srcs]
    n_after = 0 if after is None else 1
    n_grp = len(groups)

    def body(*refs):
        src_refs, land_refs = refs[:n], refs[n:2 * n]
        sems = refs[2 * n + n_after:2 * n + n_after + 2 * n_grp]
        token = refs[-1]
        lo = 0
        for gi, sz in enumerate(sizes):
            for cp in _push_copies(src_refs[lo:lo + sz], land_refs[lo:lo + sz], sems[2 * gi], sems[2 * gi + 1], scatter, False):
                cp.start()
            lo += sz
        token[...] = jnp.zeros_like(token)

    hbm = lambda a: pltpu.with_memory_space_constraint(a, pltpu.HBM)
    operands = [hbm(a) for a in srcs + lands] + ([after] if n_after else [])
    sem_shapes = [pltpu.SemaphoreType.DMA((sz * N_PEER,)) for sz in sizes for _ in range(2)]
    res = pl.pallas_call(
        body, name=name,
        out_shape=sem_shapes + [pltpu.HBM(a.shape, a.dtype) for a in srcs + lands] + [_sds((8, 128), F32)],
        in_specs=[HBM] * (2 * n) + [ANY] * n_after,
        out_specs=[SEM] * (2 * n_grp) + [HBM] * (2 * n) + [pl.BlockSpec(memory_space=pltpu.VMEM)],
        input_output_aliases={i: 2 * n_grp + i for i in range(2 * n)},
        compiler_params=pltpu.CompilerParams(has_side_effects=EFFECT),
    )(*operands)
    thru = res[2 * n_grp:2 * n_grp + 2 * n]
    handles, lo = [], 0
    for gi, sz in enumerate(sizes):
        handles.append((res[2 * gi], res[2 * gi + 1], list(thru[lo:lo + sz]), list(thru[n + lo:n + lo + sz]), scatter))
        lo += sz
    return handles, res[-1]


def _push_wait(name, handle, after):
    send_sems, recv_sems, srcs, lands, scatter = handle
    n = len(srcs)

    def body(*refs):
        src_refs, land_refs = refs[:n], refs[n:2 * n]
        s_sems, r_sems = refs[2 * n], refs[2 * n + 1]
        for cp in _push_copies(src_refs, land_refs, s_sems, r_sems, scatter, True):
            cp.wait_send()
            cp.wait_recv()

    res = pl.pallas_call(
        body, name=name,
        out_shape=[pltpu.HBM(a.shape, a.dtype) for a in srcs + lands],
        in_specs=[HBM] * (2 * n) + [SEM, SEM, ANY], out_specs=[HBM] * (2 * n),
        input_output_aliases={i: i for i in range(2 * n)},
        compiler_params=pltpu.CompilerParams(has_side_effects=EFFECT),
    )(*srcs, *lands, send_sems, recv_sems, after)
    return list(res[n:])


def _adamw(name, parts, w, m, v, own=None):
    r, c = w.shape
    br = _pick(r, COPY_ROWS, 16)
    has_own = own is not None

    def body(*refs):
        if has_own:
            p_ref, own_ref, w_ref, m_ref, v_ref, g_ref, d_ref, nm_ref, nv_ref = refs
            me = _my_id()
            mine = own_ref[...].astype(F32)
        else:
            p_ref, w_ref, m_ref, v_ref, g_ref, d_ref, nm_ref, nv_ref = refs
        g = None
        for k in range(N_DEV):
            t = p_ref[k].astype(F32)
            if has_own:
                t = jnp.where(me == k, mine, t)
            g = t if g is None else g + t
        mm = ADAM_B1 * m_ref[...] + (1.0 - ADAM_B1) * g
        vv = ADAM_B2 * v_ref[...] + (1.0 - ADAM_B2) * (g * g)
        m_hat = mm / (1.0 - ADAM_B1 ** ADAM_STEP)
        v_hat = vv / (1.0 - ADAM_B2 ** ADAM_STEP)
        g_ref[...] = g
        d_ref[...] = -ADAM_LR * (m_hat / (jnp.sqrt(v_hat) + ADAM_EPS) + ADAM_WD * w_ref[...])
        nm_ref[...] = mm
        nv_ref[...] = vv

    spec = _rows(br, c)
    out = (_sds((r, c), F32), spec)
    ins = [(parts, pl.BlockSpec((N_DEV, br, c), lambda i: (0, i, 0)))] + ([(own, spec)] if has_own else [])
    return _call(name, body, (r // br,), ins + [(w, spec), (m, spec), (v, spec)], [out] * 4, sem=("parallel",))


def _pad_head_cols(w, d):
    k = w.shape[0]
    return jnp.pad(w.reshape(k, HEADS, d), ((0, 0), (0, 0), (0, HP - d))).reshape(k, HW)


def _unpad_head_cols(wp, d):
    k = wp.shape[0]
    return wp.reshape(k, HEADS, HP)[:, :, :d].reshape(k, HEADS * d)


def _pad_head_rows(w, d):
    n = w.shape[1]
    return jnp.pad(w.reshape(HEADS, d, n), ((0, 0), (0, HP - d), (0, 0))).reshape(HW, n)


def _w_in_runs():
    nat = {}
    o = 0
    for nm, wd in (("q", Q_RANK), ("kv", KV_RANK), ("kr", ROPE), ("fq", FOX_W), ("fk", FOX_W), ("fv", FOX_W),
                   ("fl", HEADS), ("gate", 2 * D_MODEL)):
        nat[nm] = o
        o += wd
    runs = [(1, R_QLAT, nat["q"], Q_RANK, 1.0), (1, R_KVLAT, nat["kv"], KV_RANK, 1.0),
            (1, R_LAST + LANE_FL, nat["fl"], HEADS, 1.0), (1, R_LAST + LANE_PE, nat["kr"], ROPE, 1.0),
            (1, R_GATE, nat["gate"], 2 * D_MODEL, 1.0)]
    for grp, (nm, sc) in enumerate((("fq", FOX_SCALE), ("fk", 1.0), ("fv", 1.0))):
        runs.append((0, grp * FOX_W, nat[nm], FOX_W, sc))
    return runs


def _head_pad_moves(pad):
    moves = []
    for grp in range(3):
        for h in range(HEADS):
            narrow, wide = grp * FOX_W + h * FOX_DIM, h * HP
            if pad:
                moves.append((0, None, grp * HW + wide, 0, None, narrow, FOX_DIM, 1.0))
            else:
                moves.append((0, None, narrow, grp, None, wide, FOX_DIM, 1.0))
    return moves


def _sharded_runs(runs, shard_cols):
    out = []
    for half, col, ncol, width, sc in runs:
        while width > 0:
            d, local = divmod(ncol, shard_cols)
            wd = min(width, shard_cols - local)
            out.append((half, col, d, local, wd, sc))
            col, ncol, width = col + wd, ncol + wd, width - wd
    return out


def _remap(name, srcs, out_shapes, moves):
    rows = srcs[0].shape[-2]
    br = _pick(rows, COPY_ROWS, 16)
    ns = len(srcs)

    def spec(shape):
        if len(shape) == 2:
            return pl.BlockSpec((br, shape[1]), lambda i: (i, 0))
        return pl.BlockSpec((shape[0], br, shape[2]), lambda i: (0, i, 0))

    covered = [sum(m[6] for m in moves if m[0] == di) for di in range(len(out_shapes))]
    has_gaps = [cov < (shape[1] if len(shape) == 2 else shape[0] * shape[2])
                for cov, (shape, _) in zip(covered, out_shapes)]

    def body(*refs):
        s_refs, o_refs = refs[:ns], refs[ns:]
        for o, gaps in zip(o_refs, has_gaps):
            if gaps:
                o[...] = jnp.zeros_like(o)
        for di, dl, dc, si, sl, sc0, wd, scale in moves:
            v = s_refs[si][:, sc0:sc0 + wd] if sl is None else s_refs[si][sl, :, sc0:sc0 + wd]
            if scale != 1.0:
                v = v * jnp.asarray(scale, v.dtype)
            v = v.astype(o_refs[di].dtype)
            if dl is None:
                o_refs[di][:, dc:dc + wd] = v
            else:
                o_refs[di][dl, :, dc:dc + wd] = v

    return _call(name, body, (rows // br,), [(a, spec(a.shape)) for a in srcs],
                 [(_sds(shape, dt), spec(shape)) for shape, dt in out_shapes], sem=("parallel",))


def _w_in_from_shards(g3):
    n, rows, c = g3.shape
    moves = [(half, None, col, 0, d, local, wd, sc) for half, col, d, local, wd, sc in _sharded_runs(_w_in_runs(), c)]
    return _remap("w_in_repack", [g3], [((rows, F_W), g3.dtype), ((rows, R_W), g3.dtype)], moves)


def _w_in_grad_to_shards(d_fused, d_rest, n, c):
    rows = d_fused.shape[0]
    moves = [(0, d, local, half, None, col, wd, sc) for half, col, d, local, wd, sc in _sharded_runs(_w_in_runs(), c)]
    return _remap("w_in_grad_unpack", [d_fused, d_rest], [((n, rows, c), d_fused.dtype)], moves)[0]


def _rows_from_shards(name, land, own):
    n, rows, c = land.shape

    def body(land_ref, own_ref, o_ref):
        o_ref[...] = jnp.where(_my_id() == pl.program_id(0), own_ref[...], land_ref[...])

    return _call(name, body, (n,),
                 [(land, pl.BlockSpec((None, rows, c), lambda d: (d, 0, 0))), (own, _whole((rows, c)))],
                 [(_sds((n * rows, c), land.dtype), pl.BlockSpec((rows, c), lambda d: (d, 0)))], sem=("parallel",))[0]


def _cols_from_shards(name, land, own):
    n, rows, c = land.shape
    br = _pick(rows, COPY_ROWS, 16)

    def body(land_ref, own_ref, o_ref):
        me = _my_id()
        for d in range(n):
            o_ref[:, c * d:c * (d + 1)] = jnp.where(me == d, own_ref[...], land_ref[d])

    return _call(name, body, (rows // br,),
                 [(land, pl.BlockSpec((n, br, c), lambda i: (0, i, 0))), (own, _rows(br, c))],
                 [(_sds((rows, n * c), land.dtype), _rows(br, n * c))], sem=("parallel",))[0]


def _cols_to_shards(name, full, n):
    rows, nc = full.shape
    c = nc // n
    return _remap(name, [full], [((n, rows, c), full.dtype)], [(0, d, 0, 0, None, c * d, c, 1.0) for d in range(n)])[0]


def _split_w_kv(w):
    k = w.shape[0]
    w3 = w.reshape(k, HEADS, NOPE + V_DIM)
    padl = lambda a: jnp.pad(a, ((0, 0), (0, 0), (0, HP - a.shape[-1]))).reshape(k, HW)
    return padl(w3[..., :NOPE]), padl(w3[..., NOPE:])


def _merge_w_kv(wk, wv):
    k = wk.shape[0]
    return jnp.concatenate([wk.reshape(k, HEADS, HP)[..., :NOPE], wv.reshape(k, HEADS, HP)[..., :V_DIM]],
                           axis=-1).reshape(k, HEADS * (NOPE + V_DIM))


class _NoComm:
    first_token = ()

    def late_weights(self, group, after):
        return {}

    def send(self, name, grads):
        return ()


def _local_step(x, tgt, p, comm=_NoComm()):
    seq = x.shape[0]
    r = -(-(N_META + seq) // ROW_ALIGN) * ROW_ALIGN
    cd = MXU_DTYPE
    p = dict(p)

    w_f, w_r = p["w_in"]

    pos = jnp.arange(r, dtype=F32)
    inv_freq = ROPE_THETA ** (-jnp.arange(HALF, dtype=F32) / HALF)
    ang = pos[:, None] * inv_freq[None, :]
    cos_t = jnp.tile(jnp.cos(ang), (1, HP // HALF))
    sin_t = jnp.tile(jnp.sin(ang), (1, HP // HALF))
    bf_row = jnp.zeros((1, HP), F32).at[0, LANE_FL:LANE_FL + HEADS].set(p["b_forget"])

    h0, h0b = _ln_emb_fwd(x, p["meta_tokens"], p["ln_emb_g"], p["ln_emb_b"], r, after=comm.first_token)
    proj_f = _matmul("in_proj_f", h0b, w_f, out_dtype=cd)
    proj_f = _remap("proj_f_pad", [proj_f], [((r, 3 * HW), cd)], _head_pad_moves(True))[0]
    proj_r = _matmul("in_proj_r", h0b, w_r)
    latent_gains = (p["q_norm_g"], p["kv_norm_g"])
    ql, kvl = _latent_norm_fwd(proj_r, latent_gains)
    p.update(comm.late_weights("qkv", ql))
    w_q = _pad_head_cols(p["w_q_up"], QK_DIM)
    w_kv = jnp.concatenate(_split_w_kv(p["w_kv_up"]), axis=1)
    q_raw = _matmul("q_up", ql, w_q)
    kv = _matmul("kv_up", kvl, w_kv, out_dtype=cd)
    q_mla, k_mla = _rope_fwd(q_raw, kv, proj_r, cos_t, sin_t)
    o_mla, o_mla_b, lse_mla = _attn_fwd("mla_fwd", (q_mla, 0), (k_mla, 0), (kv, 1))

    cum, cum_t = _forget_fwd(proj_r, bf_row)
    o_fox, o_fox_b, lse_fox = _attn_fwd("fox_fwd", (proj_f, 0), (proj_f, 1), (proj_f, 2), cum, cum_t)

    p.update(comm.late_weights("mix", o_fox_b))
    w_bm = _pad_head_rows(p["w_branch_mla"], V_DIM)
    w_bf = _pad_head_rows(p["w_branch_fox"], FOX_DIM)
    bm = _matmul("branch_mla", o_mla_b, p["w_branch_mla"], out_dtype=cd)
    bfx = _matmul("branch_fox", o_fox_b, p["w_branch_fox"], out_dtype=cd)
    merged = _gate_fwd(proj_r, p["b_gate"], bm, bfx)
    mix = _matmul("out_proj", merged, p["w_out"])
    h1, h1b = _ln_fwd("ln_mix_fwd", h0, mix, p["ln_mix_g"], p["ln_mix_b"])
    p.update(comm.late_weights("ffn", h1b))
    up = _matmul("ffn_up", h1b, p["w_ffn_up"], out_dtype=cd)
    act = _glu_fwd(up, p["conv_w"], p["conv_b"])
    f = _matmul("ffn_down", act, p["w_ffn_down"])
    loss = _ln_ffn_loss(h1, f, tgt, p["ln_ffn_g"], p["ln_ffn_b"])

    g = {}
    dz2, dz2b, g["ln_ffn_g"], g["ln_ffn_b"] = _ln_ffn_bwd(h1, f, tgt, p["ln_ffn_g"], p["ln_ffn_b"])
    d_act = _matmul("ffn_down_dx", dz2b, p["w_ffn_down"], tb=True, out_dtype=cd)
    g["w_ffn_down"] = _matmul("ffn_down_dw", act, dz2b, ta=True, out_dtype=cd)
    d_up, dcw, g["conv_b"] = _glu_bwd(up, p["conv_w"], p["conv_b"], d_act)
    g["conv_w"] = dcw[:3]
    dh1 = _matmul("ffn_up_dx", d_up, p["w_ffn_up"], tb=True, addend=dz2, alpha=ALPHA)
    g["w_ffn_up"] = _matmul("ffn_up_dw", h1b, d_up, ta=True, out_dtype=cd)
    sent = comm.send("ffn", {n: g[n] for n in ("w_ffn_down", "w_ffn_up", "conv_w")})
    dz1, dz1b, g["ln_mix_g"], g["ln_mix_b"] = _ln_bwd("ln_mix_bwd", h0, mix, dh1, p["ln_mix_g"], after=sent)
    dmerged = _matmul("out_proj_dx", dz1b, p["w_out"], tb=True, out_dtype=cd)
    g["w_out"] = _matmul("out_proj_dw", merged, dz1b, ta=True, out_dtype=cd)
    d_bm, d_bf, d_gl, g["b_gate"] = _gate_bwd(proj_r, p["b_gate"], bm, bfx, dmerged)
    do_mla_b = _matmul("branch_mla_dx", d_bm, w_bm, tb=True, out_dtype=cd)
    g["w_branch_mla"] = _matmul("branch_mla_dw", o_mla_b, d_bm, ta=True, out_dtype=cd)
    do_fox_b = _matmul("branch_fox_dx", d_bf, w_bf, tb=True, out_dtype=cd)
    g["w_branch_fox"] = _matmul("branch_fox_dw", o_fox_b, d_bf, ta=True, out_dtype=cd)

    sent = comm.send("mix", {n: g[n] for n in ("w_out", "w_branch_mla", "w_branch_fox")})
    dq_m, dk_m, dv_m = _attn_bwd("mla_bwd", (q_mla, 0), (k_mla, 0), (kv, 1), do_mla_b, o_mla, lse_mla, after=sent)
    dfq, dfk, dfv, dcq, dck = _attn_bwd("fox_bwd", (proj_f, 0), (proj_f, 1), (proj_f, 2), do_fox_b, o_fox, lse_fox,
                                        cum, cum_t, out_dtype=cd)
    dfl, dbf = _forget_bwd(proj_r, bf_row, dcq, dck)
    g["b_forget"] = dbf[:, LANE_FL:LANE_FL + HEADS]

    dq_b, dkv_b, dlast = _rope_bwd(dq_m, dk_m, dv_m, dfl, cos_t, sin_t)
    d_ql = _matmul("q_up_dx", dq_b, w_q, tb=True)
    d_kvl = _matmul("kv_up_dx", dkv_b, w_kv, tb=True)
    d_qlat, d_kvlat, g["q_norm_g"], g["kv_norm_g"] = _latent_norm_bwd(proj_r, (d_ql, d_kvl), latent_gains)
    side_by_side = lambda parts, cols: [(0, None, c0, si, None, 0, a.shape[1], 1.0) for si, (a, c0) in enumerate(zip(parts, cols))]
    dproj_f = _remap("dproj_f_pack", [dfq, dfk, dfv], [((r, F_W), cd)], _head_pad_moves(False))[0]
    rest_parts = [d_qlat, d_kvlat, dlast, d_gl]
    dproj_r = _remap("dproj_r_pack", rest_parts, [((r, R_W), cd)],
                     side_by_side(rest_parts, (R_QLAT, R_KVLAT, R_LAST, R_GATE)))[0]
    g["w_in"] = (_matmul("in_proj_f_dw", h0b, dproj_f, ta=True, out_dtype=cd),
                 _matmul("in_proj_r_dw", h0b, dproj_r, ta=True, out_dtype=cd))
    sent = comm.send("in", {"w_in": g["w_in"]})
    dh0 = _matmul("in_proj_f_dx", dproj_f, w_f, tb=True, addend=dz1, alpha=ALPHA, after=sent)
    g["w_q_up"] = _unpad_head_cols(_matmul("q_up_dw", ql, dq_b, ta=True, out_dtype=cd, after=sent), QK_DIM)
    dw_kv = _matmul("kv_up_dw", kvl, dkv_b, ta=True, out_dtype=cd, after=sent)
    g["w_kv_up"] = _merge_w_kv(dw_kv[:, :HW], dw_kv[:, HW:])
    sent = comm.send("qkv", {n: g[n] for n in ("w_q_up", "w_kv_up")})
    dh0 = _matmul("in_proj_r_dx", dproj_r, w_r, tb=True, addend=dh0, after=sent)
    grad_x, d_meta, g["ln_emb_g"], g["ln_emb_b"] = _ln_emb_bwd(x, p["meta_tokens"], dh0, p["ln_emb_g"])
    return loss, grad_x, d_meta, g


BIG = (("w_in", 1), ("w_q_up", 1), ("w_kv_up", 1), ("w_branch_mla", 1), ("w_branch_fox", 1), ("w_out", 0),
       ("w_ffn_up", 1), ("w_ffn_down", 0))
SMALL_SHARDED = (("meta_tokens", 1), ("conv_w", 1))
EARLY = ("w_in", "meta_tokens")
LATE = {"qkv": ("w_q_up", "w_kv_up", "conv_w"),
        "mix": ("w_branch_mla", "w_branch_fox", "w_out"),
        "ffn": ("w_ffn_up", "w_ffn_down")}
REPLICATED = ("ln_emb_g", "ln_emb_b", "b_gate", "b_forget", "q_norm_g", "kv_norm_g", "ln_mix_g", "ln_mix_b",
              "conv_b", "ln_ffn_g", "ln_ffn_b")
PACK_COLS = 1024


def _pack(flat_list):
    cat = jnp.concatenate(flat_list)
    n = cat.shape[0]
    rows = -(-n // (8 * PACK_COLS)) * 8
    return jnp.pad(cat, (0, rows * PACK_COLS - n)).reshape(rows, PACK_COLS)


def _gathered_full(g3, axis):
    n, r, c = g3.shape
    if axis == 0:
        return g3.reshape(n * r, c)
    return g3.transpose(1, 0, 2).reshape(r, n * c)


def _shard_major(full, axis):
    r, c = full.shape
    if axis == 0:
        return full.reshape(N_DEV, r // N_DEV, c)
    return full.reshape(r, N_DEV, c // N_DEV).transpose(1, 0, 2)


def kernel(x, meta_tokens, ln_emb_g, ln_emb_b, w_in, b_gate, b_forget, q_norm_g, w_q_up, kv_norm_g, w_kv_up, w_branch_mla, w_branch_fox, w_out, ln_mix_g, ln_mix_b, w_ffn_up, conv_w, conv_b, w_ffn_down, ln_ffn_g, ln_ffn_b, loss_target, m_meta_tokens, m_ln_emb_g, m_ln_emb_b, m_w_in, m_b_gate, m_b_forget, m_q_norm_g, m_w_q_up, m_kv_norm_g, m_w_kv_up, m_w_branch_mla, m_w_branch_fox, m_w_out, m_ln_mix_g, m_ln_mix_b, m_w_ffn_up, m_conv_w, m_conv_b, m_w_ffn_down, m_ln_ffn_g, m_ln_ffn_b, v_meta_tokens, v_ln_emb_g, v_ln_emb_b, v_w_in, v_b_gate, v_b_forget, v_q_norm_g, v_w_q_up, v_kv_norm_g, v_w_kv_up, v_w_branch_mla, v_w_branch_fox, v_w_out, v_ln_mix_g, v_ln_mix_b, v_w_ffn_up, v_conv_w, v_conv_b, v_w_ffn_down, v_ln_ffn_g, v_ln_ffn_b):
    names = ("meta_tokens", "ln_emb_g", "ln_emb_b", "w_in", "b_gate", "b_forget", "q_norm_g", "w_q_up", "kv_norm_g",
             "w_kv_up", "w_branch_mla", "w_branch_fox", "w_out", "ln_mix_g", "ln_mix_b", "w_ffn_up", "conv_w", "conv_b",
             "w_ffn_down", "ln_ffn_g", "ln_ffn_b")
    w_args = (meta_tokens, ln_emb_g, ln_emb_b, w_in, b_gate, b_forget, q_norm_g, w_q_up, kv_norm_g, w_kv_up,
              w_branch_mla, w_branch_fox, w_out, ln_mix_g, ln_mix_b, w_ffn_up, conv_w, conv_b, w_ffn_down, ln_ffn_g, ln_ffn_b)
    m_args = (m_meta_tokens, m_ln_emb_g, m_ln_emb_b, m_w_in, m_b_gate, m_b_forget, m_q_norm_g, m_w_q_up, m_kv_norm_g,
              m_w_kv_up, m_w_branch_mla, m_w_branch_fox, m_w_out, m_ln_mix_g, m_ln_mix_b, m_w_ffn_up, m_conv_w, m_conv_b,
              m_w_ffn_down, m_ln_ffn_g, m_ln_ffn_b)
    v_args = (v_meta_tokens, v_ln_emb_g, v_ln_emb_b, v_w_in, v_b_gate, v_b_forget, v_q_norm_g, v_w_q_up, v_kv_norm_g,
              v_w_kv_up, v_w_branch_mla, v_w_branch_fox, v_w_out, v_ln_mix_g, v_ln_mix_b, v_w_ffn_up, v_conv_w, v_conv_b,
              v_w_ffn_down, v_ln_ffn_g, v_ln_ffn_b)
    as2d = lambda a: a.reshape((-1, a.shape[-1])) if a.ndim != 1 else a.reshape(1, -1)
    w = {n: as2d(a) for n, a in zip(names, w_args)}
    m = {n: as2d(a) for n, a in zip(names, m_args)}
    v = {n: as2d(a) for n, a in zip(names, v_args)}
    out_shape = {n: a.shape for n, a in zip(names, w_args)}

    axis_of = dict(BIG + SMALL_SHARDED)
    big = set(n for n, _ in BIG)
    wire = lambda n, a: a.astype(MXU_DTYPE) if n in big else a
    my_id = _my_id()

    early = _allgather("gather_early", [wire(n, w[n]) for n in EARLY])
    p = {n: _gathered_full(g3, axis_of[n]) for n, g3 in zip(EARLY, early) if n != "w_in"}
    p["w_in"] = _w_in_from_shards(early[EARLY.index("w_in")])
    for n in REPLICATED:
        p[n] = w[n].reshape(-1)
    late_src = [[wire(n, w[n]) for n in members] for members in LATE.values()]
    late_handles, late_token = _push_start("gather_late_start", late_src, False, after=early[0])
    late = {group: (members, src, handle)
            for (group, members), src, handle in zip(LATE.items(), late_src, late_handles)}
    sent = {}

    class Comm:
        first_token = (late_token,)

        def late_weights(self, group, after):
            members, src, handle = late[group]
            lands = _push_wait("gather_" + group + "_wait", handle, after)
            out = {}
            for n, own, land in zip(members, src, lands):
                if own.shape[0] % 16:
                    out[n] = _gathered_full(lax.dynamic_update_index_in_dim(land, own, my_id, 0), axis_of[n])
                elif axis_of[n] == 1:
                    out[n] = _cols_from_shards(n + "_repack", land, own)
                else:
                    out[n] = _rows_from_shards(n + "_repack", land, own)
            return out

        def send(self, name, grads):
            names_ = tuple(grads)
            parts = []
            for n in names_:
                if n == "w_in":
                    parts.append(_w_in_grad_to_shards(*grads[n], N_DEV, w[n].shape[1]))
                elif n == "w_ffn_up":
                    parts.append(_cols_to_shards(n + "_grad_unpack", grads[n], N_DEV))
                else:
                    parts.append(_shard_major(grads[n], axis_of[n]).astype(MXU_DTYPE))
            (handle,), token = _push_start("send_" + name + "_start", [parts], True)
            sent[name] = (names_, parts, handle)
            return (token,)

    loss_part, grad_x, d_meta, g = _local_step(x[0], loss_target[0], p, Comm())
    grad_x = grad_x[None]

    small = _pack([d_meta.reshape(-1)] + [g[n].reshape(-1) for n in REPLICATED] + [loss_part.reshape(-1)])
    (small_handle,), small_token = _push_start("send_small_start", [[small]], False)

    res = {}
    prev = small_token
    for name, (names_, parts, handle) in sent.items():
        lands = _push_wait("send_" + name + "_wait", handle, prev)
        for n, part, land in zip(names_, parts, lands):
            own = lax.dynamic_index_in_dim(part, my_id, axis=0, keepdims=False)
            res[n] = _adamw("adamw_" + n, land, w[n], m[n], v[n], own=own)
            prev = res[n][0]
    small_all = _push_wait("send_small_wait", small_handle, prev)[0]
    head = jnp.zeros((d_meta.size,), F32)
    rep_w = _pack([head] + [w[n].reshape(-1) for n in REPLICATED])
    rep_m = _pack([head] + [m[n].reshape(-1) for n in REPLICATED])
    rep_v = _pack([head] + [v[n].reshape(-1) for n in REPLICATED])
    rep_res = _adamw("adamw_replicated", small_all, rep_w, rep_m, rep_v, own=small)
    off = d_meta.size
    for n in REPLICATED:
        sz = w[n].size
        res[n] = tuple(a.reshape(-1)[off:off + sz] for a in rep_res)
        off += sz
    loss = rep_res[0].reshape(-1)[off]
    cols = w["meta_tokens"].shape[1]
    meta_rows = lambda a: a.reshape(a.shape[:-2] + (-1,))[..., :d_meta.size].reshape(a.shape[:-2] + d_meta.shape)
    my_cols = lambda a: lax.dynamic_slice_in_dim(a, my_id * cols, cols, axis=a.ndim - 1)
    res["meta_tokens"] = _adamw("adamw_meta_tokens", my_cols(meta_rows(small_all)), w["meta_tokens"],
                                m["meta_tokens"], v["meta_tokens"], own=my_cols(d_meta))

    outs = [loss, grad_x]
    for idx in range(4):
        outs += [res[n][idx].reshape(out_shape[n]) for n in names]
    return tuple(outs)
```

```python
import jax
import jax.numpy as jnp
from jax import lax
from jax.experimental import pallas as pl
from jax.experimental.pallas import tpu as pltpu

F32 = jnp.float32
BF16 = jnp.bfloat16
MXU_DTYPE = BF16

N_DEV = 8
N_META = 16
D_MODEL = 1024
HEADS = 8
Q_RANK = 384
KV_RANK = 128
NOPE = 64
ROPE = 32
HALF = ROPE // 2
QK_DIM = NOPE + ROPE
V_DIM = 64
FOX_DIM = 64
FOX_W = HEADS * FOX_DIM
D_FF = 2816
ROPE_THETA = 10000.0
LN_EPS = 1e-5
RMS_EPS = 1e-6
ALPHA = 2.0 ** 0.25
MLA_SCALE = QK_DIM ** -0.5
FOX_SCALE = FOX_DIM ** -0.5
NEG_INF = -1e30

HP = 128
HW = HEADS * HP
F_W = 3 * FOX_W
R_GATE = 0
R_KVLAT = R_GATE + 2 * D_MODEL
R_LAST = R_KVLAT + KV_RANK
R_QLAT = R_LAST + HP
R_W = R_QLAT + Q_RANK
assert R_QLAT % Q_RANK == 0 and R_KVLAT % KV_RANK == 0 and R_GATE % D_MODEL == 0 and R_W % HP == 0
LANE_FL = 0
LANE_PE = NOPE

ADAM_LR = 0.001
ADAM_B1 = 0.9
ADAM_B2 = 0.999
ADAM_EPS = 1e-08
ADAM_WD = 0.01
ADAM_STEP = 10

ROW_BLOCK = 768
GLU_FWD_BLOCK = 384
SCAN_BLOCK = 384
TOKEN_BLOCK = 256
ATT_TQ = 768
ATT_TK = 768
ATT_HEADS = 2
ATT_HEADS_FWD = 4
ROW_ALIGN = 768
MM_BLOCK_CAP = 1408
VMEM_LIMIT = 56 * 1024 * 1024
HIGHEST = lax.Precision.HIGHEST
NT = (((1,), (1,)), ((), ()))
TN = (((0,), (0,)), ((), ()))


def _params(sem=None):
    return pltpu.CompilerParams(dimension_semantics=sem, vmem_limit_bytes=VMEM_LIMIT)


def _call(name, body, grid, ins, outs, scratch=(), sem=None, after=()):
    n_in = len(ins)
    n_tok = len(after)

    def run(*refs):
        body(*refs[:n_in], *refs[n_in + n_tok:])

    tok_spec = pl.BlockSpec((8, 128), lambda *_: (0, 0))
    return pl.pallas_call(
        run, name=name, grid=grid,
        in_specs=[s for _, s in ins] + [tok_spec] * n_tok,
        out_specs=[s for _, s in outs],
        out_shape=[o for o, _ in outs],
        scratch_shapes=list(scratch),
        compiler_params=_params(sem),
    )(*[a for a, _ in ins], *after)


def _sds(shape, dtype):
    return jax.ShapeDtypeStruct(shape, dtype)


def _rows(br, c, cb=0):
    return pl.BlockSpec((br, c), lambda i: (i, cb))


def _whole(shape):
    n = len(shape)
    return pl.BlockSpec(shape, lambda i: (0,) * n)


def _pick(dim, cap, mult):
    best = None
    d = mult
    while d <= min(dim, cap):
        if dim % d == 0:
            best = d
        d += mult
    return best if best is not None else dim


def _hs(h):
    return slice(h * HP, (h + 1) * HP)


def _matmul(name, a, b, *, ta=False, tb=False, out_dtype=F32, addend=None, alpha=1.0, after=()):
    if ta:
        k, m = a.shape
    else:
        m, k = a.shape
    if tb:
        n, k2 = b.shape
    else:
        k2, n = b.shape
    assert k == k2, (name, a.shape, b.shape)
    bm = _pick(m, MM_BLOCK_CAP, 128 if ta else 16)
    bn = _pick(n, MM_BLOCK_CAP, 128)
    bk = _pick(k, MM_BLOCK_CAP, 128 if (not ta or tb) else 16)
    nk = k // bk
    dims = (((0 if ta else 1,), (1 if tb else 0,)), ((), ()))
    has_add = addend is not None

    def body(*refs):
        a_ref, b_ref = refs[:2]
        add_ref = refs[2] if has_add else None
        o_ref = refs[3 if has_add else 2]

        def finish(r):
            if has_add:
                r = r + alpha * add_ref[...]
            o_ref[...] = r.astype(o_ref.dtype)

        part = lax.dot_general(a_ref[...], b_ref[...], dims, preferred_element_type=F32)
        if nk == 1:
            finish(part)
            return
        acc_ref = refs[-1]
        kk = pl.program_id(2)

        @pl.when(kk == 0)
        def _():
            acc_ref[...] = part

        @pl.when(kk > 0)
        def _():
            acc_ref[...] += part

        @pl.when(kk == nk - 1)
        def _():
            finish(acc_ref[...])

    a_spec = pl.BlockSpec((bk, bm), lambda i, j, l: (l, i)) if ta else pl.BlockSpec((bm, bk), lambda i, j, l: (i, l))
    b_spec = pl.BlockSpec((bn, bk), lambda i, j, l: (j, l)) if tb else pl.BlockSpec((bk, bn), lambda i, j, l: (l, j))
    o_spec = pl.BlockSpec((bm, bn), lambda i, j, l: (i, j))
    ins = [(a, a_spec), (b, b_spec)]
    if has_add:
        ins.append((addend, o_spec))
    return _call(name, body, (m // bm, n // bn, nk), ins, [(_sds((m, n), out_dtype), o_spec)],
                 scratch=[pltpu.VMEM((bm, bn), F32)] if nk > 1 else [],
                 sem=("parallel", "parallel", "arbitrary"), after=after)[0]


def _ln_stats(z):
    mu = jnp.mean(z, axis=-1, keepdims=True)
    zc = z - mu
    var = jnp.mean(zc * zc, axis=-1, keepdims=True)
    rstd = lax.rsqrt(var + LN_EPS)
    return zc * rstd, rstd


def _ln_fwd(name, a, res, g, b, after=()):
    r, d = a.shape
    br = ROW_BLOCK
    has_res = res is not None

    def body(*refs):
        if has_res:
            a_ref, r_ref, g_ref, b_ref, y_ref, yb_ref = refs
            z = ALPHA * a_ref[...] + r_ref[...]
        else:
            a_ref, g_ref, b_ref, y_ref, yb_ref = refs
            z = a_ref[...]
        xhat, _ = _ln_stats(z)
        y = xhat * g_ref[...] + b_ref[...]
        y_ref[...] = y
        yb_ref[...] = y.astype(yb_ref.dtype)

    ins = [(a, _rows(br, d))]
    if has_res:
        ins.append((res, _rows(br, d)))
    ins += [(g.reshape(1, d), _whole((1, d))), (b.reshape(1, d), _whole((1, d)))]
    outs = [(_sds((r, d), F32), _rows(br, d)), (_sds((r, d), MXU_DTYPE), _rows(br, d))]
    return _call(name, body, (r // br,), ins, outs, sem=("parallel",), after=after)


def _ln_bwd(name, a, res, dy, g, after=()):
    r, d = a.shape
    br = ROW_BLOCK
    has_res = res is not None

    def body(*refs):
        if has_res:
            a_ref, r_ref, dy_ref, g_ref, dz_ref, dzb_ref, dg_ref, db_ref = refs
            z = ALPHA * a_ref[...] + r_ref[...]
        else:
            a_ref, dy_ref, g_ref, dz_ref, dzb_ref, dg_ref, db_ref = refs
            z = a_ref[...]
        xhat, rstd = _ln_stats(z)
        dyv = dy_ref[...]
        dyg = dyv * g_ref[...]
        m1 = jnp.mean(dyg, axis=-1, keepdims=True)
        m2 = jnp.mean(dyg * xhat, axis=-1, keepdims=True)
        dz = rstd * (dyg - m1 - xhat * m2)
        dz_ref[...] = dz
        dzb_ref[...] = dz.astype(dzb_ref.dtype)

        @pl.when(pl.program_id(0) == 0)
        def _():
            dg_ref[...] = jnp.zeros_like(dg_ref)
            db_ref[...] = jnp.zeros_like(db_ref)

        dg_ref[...] += jnp.sum(dyv * xhat, axis=0, keepdims=True)
        db_ref[...] += jnp.sum(dyv, axis=0, keepdims=True)

    ins = [(a, _rows(br, d))]
    if has_res:
        ins.append((res, _rows(br, d)))
    ins += [(dy, _rows(br, d)), (g.reshape(1, d), _whole((1, d)))]
    outs = [(_sds((r, d), F32), _rows(br, d)), (_sds((r, d), MXU_DTYPE), _rows(br, d)),
            (_sds((1, d), F32), _whole((1, d))), (_sds((1, d), F32), _whole((1, d)))]
    return _call(name, body, (r // br,), ins, outs, sem=("arbitrary",), after=after)


LATENTS = ((R_QLAT // Q_RANK, Q_RANK), (R_KVLAT // KV_RANK, KV_RANK))


def _latent_norm_fwd(proj_r, gains):
    r = proj_r.shape[0]
    br = ROW_BLOCK

    def body(xq_ref, xk_ref, gq_ref, gk_ref, yq_ref, yk_ref):
        for x_ref, g_ref, y_ref in ((xq_ref, gq_ref, yq_ref), (xk_ref, gk_ref, yk_ref)):
            x = x_ref[...]
            rstd = lax.rsqrt(jnp.mean(x * x, axis=-1, keepdims=True) + RMS_EPS)
            y_ref[...] = (x * rstd * g_ref[...]).astype(y_ref.dtype)

    return _call("latent_norm_fwd", body, (r // br,),
                 [(proj_r, _rows(br, wd, cb)) for cb, wd in LATENTS]
                 + [(g.reshape(1, wd), _whole((1, wd))) for g, (_, wd) in zip(gains, LATENTS)],
                 [(_sds((r, wd), MXU_DTYPE), _rows(br, wd)) for _, wd in LATENTS], sem=("parallel",))


def _latent_norm_bwd(proj_r, dys, gains):
    r = proj_r.shape[0]
    br = ROW_BLOCK

    def body(xq_ref, xk_ref, dq_ref, dk_ref, gq_ref, gk_ref, oq_ref, ok_ref, dgq_ref, dgk_ref):
        @pl.when(pl.program_id(0) == 0)
        def _():
            dgq_ref[...] = jnp.zeros_like(dgq_ref)
            dgk_ref[...] = jnp.zeros_like(dgk_ref)

        for x_ref, dy_ref, g_ref, dx_ref, dg_ref in ((xq_ref, dq_ref, gq_ref, oq_ref, dgq_ref),
                                                     (xk_ref, dk_ref, gk_ref, ok_ref, dgk_ref)):
            x = x_ref[...]
            rstd = lax.rsqrt(jnp.mean(x * x, axis=-1, keepdims=True) + RMS_EPS)
            nrm = x * rstd
            dyv = dy_ref[...]
            dyg = dyv * g_ref[...]
            dx_ref[...] = (rstd * (dyg - nrm * jnp.mean(dyg * nrm, axis=-1, keepdims=True))).astype(dx_ref.dtype)
            dg_ref[...] += jnp.sum(dyv * nrm, axis=0, keepdims=True)

    return _call("latent_norm_bwd", body, (r // br,),
                 [(proj_r, _rows(br, wd, cb)) for cb, wd in LATENTS]
                 + [(dy, _rows(br, wd)) for dy, (_, wd) in zip(dys, LATENTS)]
                 + [(g.reshape(1, wd), _whole((1, wd))) for g, (_, wd) in zip(gains, LATENTS)],
                 [(_sds((r, wd), MXU_DTYPE), _rows(br, wd)) for _, wd in LATENTS]
                 + [(_sds((1, wd), F32), _whole((1, wd))) for _, wd in LATENTS], sem=("arbitrary",))


def _lane_iota(shape):
    return lax.broadcasted_iota(jnp.int32, shape, 1)


def _rotary(t, c, s, lane, sign):
    second = pltpu.roll(t, HP - HALF, axis=1)
    first = pltpu.roll(t, HALF, axis=1)
    lo = (lane >= LANE_PE) & (lane < LANE_PE + HALF)
    hi = (lane >= LANE_PE + HALF) & (lane < LANE_PE + ROPE)
    return jnp.where(lo, t * c - sign * second * s, jnp.where(hi, t * c + sign * first * s, t))


def _rope_fwd(q_raw, k_part, proj_r, cos_t, sin_t):
    r = q_raw.shape[0]
    br = ROW_BLOCK

    def body(q_ref, k_ref, t_ref, c_ref, s_ref, qo_ref, ko_ref):
        c = c_ref[...]
        s = s_ref[...]
        lane = _lane_iota((br, HP))
        pe = (lane >= LANE_PE) & (lane < LANE_PE + ROPE)
        kp = jnp.where(pe, _rotary(t_ref[...], c, s, lane, 1.0), 0.0)
        for h in range(HEADS):
            qo_ref[:, _hs(h)] = (_rotary(q_ref[:, _hs(h)], c, s, lane, 1.0) * MLA_SCALE).astype(qo_ref.dtype)
            ko_ref[:, _hs(h)] = (k_ref[:, _hs(h)] + kp).astype(ko_ref.dtype)

    blk = _rows(br, HP)
    wide = _rows(br, HW)
    return _call("rope_fwd", body, (r // br,),
                 [(q_raw, wide), (k_part, wide), (proj_r, _rows(br, HP, R_LAST // HP)), (cos_t, blk), (sin_t, blk)],
                 [(_sds((r, HW), MXU_DTYPE), wide)] * 2, sem=("parallel",))


def _rope_bwd(dq, dk, dv, dfl, cos_t, sin_t):
    r = dq.shape[0]
    br = ROW_BLOCK

    def body(dq_ref, dk_ref, dv_ref, fl_ref, c_ref, s_ref, dqo_ref, dkv_ref, dl_ref):
        c = c_ref[...]
        s = s_ref[...]
        lane = _lane_iota((br, HP))
        pe = (lane >= LANE_PE) & (lane < LANE_PE + ROPE)
        acc = jnp.zeros((br, HP), F32)
        for h in range(HEADS):
            dqo_ref[:, _hs(h)] = (_rotary(dq_ref[:, _hs(h)], c, s, lane, -1.0) * MLA_SCALE).astype(dqo_ref.dtype)
            dkh = dk_ref[:, _hs(h)]
            acc = acc + dkh
            dkv_ref[:, _hs(h)] = dkh.astype(dkv_ref.dtype)
            dkv_ref[:, _hs(HEADS + h)] = dv_ref[:, _hs(h)].astype(dkv_ref.dtype)
        dl_ref[...] = (jnp.where(pe, _rotary(acc, c, s, lane, -1.0), 0.0) + fl_ref[...]).astype(dl_ref.dtype)

    blk = _rows(br, HP)
    wide = _rows(br, HW)
    return _call("rope_bwd", body, (r // br,),
                 [(dq, wide), (dk, wide), (dv, wide), (dfl, blk), (cos_t, blk), (sin_t, blk)],
                 [(_sds((r, HW), MXU_DTYPE), wide), (_sds((r, 2 * HW), MXU_DTYPE), _rows(br, 2 * HW)),
                  (_sds((r, HP), MXU_DTYPE), blk)],
                 sem=("parallel",))


def _log_sigmoid(x):
    return jnp.minimum(x, 0.0) - jnp.log(1.0 + jnp.exp(-jnp.abs(x)))


def _head_lane(x, h, lane):
    return jnp.sum(jnp.where(lane == h, x, 0.0), axis=1, keepdims=True)


def _forget_fwd(proj_r, bf_row):
    r = proj_r.shape[0]
    br = SCAN_BLOCK

    def body(t_ref, b_ref, ob_ref, ot_ref, carry_ref):
        @pl.when(pl.program_id(0) == 0)
        def _():
            carry_ref[...] = jnp.zeros_like(carry_ref)

        x = t_ref[...] + b_ref[...]
        lane = _lane_iota(x.shape)
        lf = jnp.where((lane >= LANE_FL) & (lane < LANE_FL + HEADS), _log_sigmoid(x), 0.0)
        tri = (lax.broadcasted_iota(jnp.int32, (br, br), 0) >= lax.broadcasted_iota(jnp.int32, (br, br), 1)).astype(F32)
        cum = jnp.dot(tri, lf, precision=HIGHEST, preferred_element_type=F32) + carry_ref[0:1, :]
        for h in range(HEADS):
            ob_ref[:, _hs(h)] = jnp.broadcast_to(_head_lane(cum, LANE_FL + h, lane), (br, HP))
        ot_ref[...] = cum.T[LANE_FL:LANE_FL + HEADS, :]
        carry_ref[...] = jnp.broadcast_to(cum[br - 1:br, :], carry_ref.shape)

    return _call("forget_fwd", body, (r // br,),
                 [(proj_r, _rows(br, HP, R_LAST // HP)), (bf_row, _whole((1, HP)))],
                 [(_sds((r, HW), F32), _rows(br, HW)), (_sds((HEADS, r), F32), pl.BlockSpec((HEADS, br), lambda i: (0, i)))],
                 scratch=[pltpu.VMEM((8, HP), F32)], sem=("arbitrary",))


def _forget_bwd(proj_r, bf_row, dcq_t, dck_b):
    r = proj_r.shape[0]
    br = SCAN_BLOCK
    nb = r // br

    def body(t_ref, b_ref, dcq_ref, dck_ref, o_ref, db_ref, carry_ref):
        @pl.when(pl.program_id(0) == 0)
        def _():
            carry_ref[...] = jnp.zeros_like(carry_ref)
            db_ref[...] = jnp.zeros_like(db_ref)

        lane = _lane_iota((br, HP))
        dc = jnp.concatenate([dcq_ref[...], jnp.zeros((HP - HEADS, br), F32)], axis=0).T
        for h in range(HEADS):
            dc = dc + jnp.where(lane == LANE_FL + h, dck_ref[:, h * HP:h * HP + 1], 0.0)
        triu = (lax.broadcasted_iota(jnp.int32, (br, br), 0) <= lax.broadcasted_iota(jnp.int32, (br, br), 1)).astype(F32)
        dlf = jnp.dot(triu, dc, precision=HIGHEST, preferred_element_type=F32) + carry_ref[0:1, :]
        carry_ref[...] = jnp.broadcast_to(dlf[0:1, :], carry_ref.shape)
        x = t_ref[...] + b_ref[...]
        dfl = jnp.where((lane >= LANE_FL) & (lane < LANE_FL + HEADS), dlf * jax.nn.sigmoid(-x), 0.0)
        o_ref[...] = dfl
        db_ref[...] += jnp.sum(dfl, axis=0, keepdims=True)

    rev = pl.BlockSpec((br, HP), lambda i: (nb - 1 - i, 0))
    return _call("forget_bwd", body, (nb,),
                 [(proj_r, pl.BlockSpec((br, HP), lambda i: (nb - 1 - i, R_LAST // HP))), (bf_row, _whole((1, HP))),
                  (dcq_t, pl.BlockSpec((HEADS, br), lambda i: (0, nb - 1 - i))),
                  (dck_b, pl.BlockSpec((br, HW), lambda i: (nb - 1 - i, 0)))],
                 [(_sds((r, HP), F32), rev), (_sds((1, HP), F32), _whole((1, HP)))],
                 scratch=[pltpu.VMEM((8, HP), F32)], sem=("arbitrary",))


def _gate_fwd(proj_r, b_gate, bm, bfx):
    r, d = bm.shape
    br = ROW_BLOCK
    cb = R_GATE // d

    def body(gm_ref, gf_ref, b1_ref, b2_ref, bm_ref, bf_ref, o_ref):
        g1 = jax.nn.sigmoid(gm_ref[...] + b1_ref[...])
        g2 = jax.nn.sigmoid(gf_ref[...] + b2_ref[...])
        o_ref[...] = (g1 * bm_ref[...].astype(F32) + g2 * bf_ref[...].astype(F32)).astype(o_ref.dtype)

    b1 = b_gate[:d].reshape(1, d)
    b2 = b_gate[d:].reshape(1, d)
    return _call("gate_fwd", body, (r // br,),
                 [(proj_r, _rows(br, d, cb)), (proj_r, _rows(br, d, cb + 1)), (b1, _whole((1, d))), (b2, _whole((1, d))),
                  (bm, _rows(br, d)), (bfx, _rows(br, d))],
                 [(_sds((r, d), MXU_DTYPE), _rows(br, d))], sem=("parallel",))[0]


def _gate_bwd(proj_r, b_gate, bm, bfx, dmerged):
    r, d = bm.shape
    br = ROW_BLOCK
    cb = R_GATE // d

    def body(gm_ref, gf_ref, b1_ref, b2_ref, bm_ref, bf_ref, dm_ref, dbm_ref, dbf_ref, dgl_ref, dbg_ref):
        g1 = jax.nn.sigmoid(gm_ref[...] + b1_ref[...])
        g2 = jax.nn.sigmoid(gf_ref[...] + b2_ref[...])
        dm = dm_ref[...].astype(F32)
        dbm_ref[...] = (dm * g1).astype(dbm_ref.dtype)
        dbf_ref[...] = (dm * g2).astype(dbf_ref.dtype)
        dl1 = dm * bm_ref[...].astype(F32) * (g1 * (1.0 - g1))
        dl2 = dm * bf_ref[...].astype(F32) * (g2 * (1.0 - g2))
        dgl_ref[:, 0:d] = dl1.astype(dgl_ref.dtype)
        dgl_ref[:, d:2 * d] = dl2.astype(dgl_ref.dtype)

        @pl.when(pl.program_id(0) == 0)
        def _():
            dbg_ref[...] = jnp.zeros_like(dbg_ref)

        dbg_ref[:, 0:d] += jnp.sum(dl1, axis=0, keepdims=True)
        dbg_ref[:, d:2 * d] += jnp.sum(dl2, axis=0, keepdims=True)

    b1 = b_gate[:d].reshape(1, d)
    b2 = b_gate[d:].reshape(1, d)
    return _call("gate_bwd", body, (r // br,),
                 [(proj_r, _rows(br, d, cb)), (proj_r, _rows(br, d, cb + 1)), (b1, _whole((1, d))), (b2, _whole((1, d))),
                  (bm, _rows(br, d)), (bfx, _rows(br, d)), (dmerged, _rows(br, d))],
                 [(_sds((r, d), MXU_DTYPE), _rows(br, d)), (_sds((r, d), MXU_DTYPE), _rows(br, d)),
                  (_sds((r, 2 * d), MXU_DTYPE), _rows(br, 2 * d)), (_sds((1, 2 * d), F32), _whole((1, 2 * d)))],
                 sem=("arbitrary",))


HALO = 16
GLU_BWD_BLOCK = 256
COPY_ROWS = 512


def _conv_taps(gp, halo, first_block):
    halo = jnp.where(first_block, 0.0, halo.astype(F32))
    rid = lax.broadcasted_iota(jnp.int32, gp.shape, 0)
    last, prev = halo[HALO - 1:HALO, :], halo[HALO - 2:HALO - 1, :]
    g1 = jnp.where(rid == 0, last, pltpu.roll(gp, 1, axis=0))
    g2 = jnp.where(rid == 0, prev, jnp.where(rid == 1, last, pltpu.roll(gp, 2, axis=0)))
    return g1, g2


def _prev_halo(br, c):
    return pl.BlockSpec((HALO, c), lambda i: (jnp.maximum(i * (br // HALO) - 1, 0), 0))


def _glu_fwd(up, conv_w, conv_b):
    r = up.shape[0]
    c = D_FF
    br = GLU_FWD_BLOCK

    def body(gp_ref, halo_ref, val_ref, w_ref, b_ref, o_ref):
        gp = gp_ref[...].astype(F32)
        g1, g2 = _conv_taps(gp, halo_ref[...], pl.program_id(0) == 0)
        gate = w_ref[0:1, :] * g2 + w_ref[1:2, :] * g1 + w_ref[2:3, :] * gp + b_ref[...]
        o_ref[...] = (gate * jax.nn.sigmoid(gate) * val_ref[...].astype(F32)).astype(o_ref.dtype)

    return _call("glu_fwd", body, (r // br,),
                 [(up, _rows(br, c, 0)), (up, _prev_halo(br, c)), (up, _rows(br, c, 1)),
                  (conv_w, _whole((3, c))), (conv_b.reshape(1, c), _whole((1, c)))],
                 [(_sds((r, c), MXU_DTYPE), _rows(br, c))], sem=("parallel",))[0]


def _glu_bwd(up, conv_w, conv_b, d_act):
    r = up.shape[0]
    c = D_FF
    br = GLU_BWD_BLOCK
    nb = r // br

    def body(gp_ref, halo_ref, val_ref, da_ref, gpn_ref, valn_ref, dan_ref, w_ref, b_ref, o_ref, dw_ref, db_ref):
        i = pl.program_id(0)
        w0, w1, w2, bias = w_ref[0:1, :], w_ref[1:2, :], w_ref[2:3, :], b_ref[...]

        def d_gate(gp, g1, g2, val, da):
            gate = w0 * g2 + w1 * g1 + w2 * gp + bias
            sg = jax.nn.sigmoid(gate)
            return da * val * (sg * (1.0 + gate * (1.0 - sg))), da * (gate * sg)

        gp = gp_ref[...].astype(F32)
        g1, g2 = _conv_taps(gp, halo_ref[...], i == 0)
        dg, dv = d_gate(gp, g1, g2, val_ref[...].astype(F32), da_ref[...].astype(F32))
        gpn = gpn_ref[...].astype(F32)
        g1n, g2n = _conv_taps(gpn, gp[br - HALO:, :], False)
        dgn, _ = d_gate(gpn, g1n, g2n, valn_ref[...].astype(F32), dan_ref[...].astype(F32))
        dgn = jnp.where(i == nb - 1, 0.0, dgn)
        rid = lax.broadcasted_iota(jnp.int32, dg.shape, 0)
        u1 = jnp.where(rid == br - 1, dgn[0:1, :], pltpu.roll(dg, br - 1, axis=0))
        u2 = jnp.where(rid == br - 1, dgn[1:2, :], jnp.where(rid == br - 2, dgn[0:1, :], pltpu.roll(dg, br - 2, axis=0)))
        o_ref[:, 0:c] = (w2 * dg + w1 * u1 + w0 * u2).astype(o_ref.dtype)
        o_ref[:, c:2 * c] = dv.astype(o_ref.dtype)

        @pl.when(i == 0)
        def _():
            dw_ref[...] = jnp.zeros_like(dw_ref)
            db_ref[...] = jnp.zeros_like(db_ref)

        dw_ref[0:1, :] += jnp.sum(dg * g2, axis=0, keepdims=True)
        dw_ref[1:2, :] += jnp.sum(dg * g1, axis=0, keepdims=True)
        dw_ref[2:3, :] += jnp.sum(dg * gp, axis=0, keepdims=True)
        db_ref[...] += jnp.sum(dg, axis=0, keepdims=True)

    nxt = lambda cb: pl.BlockSpec((HALO, c), lambda i: (jnp.minimum((i + 1) * (br // HALO), r // HALO - 1), cb))
    return _call("glu_bwd", body, (nb,),
                 [(up, _rows(br, c, 0)), (up, _prev_halo(br, c)), (up, _rows(br, c, 1)), (d_act, _rows(br, c)),
                  (up, nxt(0)), (up, nxt(1)), (d_act, nxt(0)),
                  (conv_w, _whole((3, c))), (conv_b.reshape(1, c), _whole((1, c)))],
                 [(_sds((r, 2 * c), MXU_DTYPE), _rows(br, 2 * c)),
                  (_sds((8, c), F32), _whole((8, c))), (_sds((1, c), F32), _whole((1, c)))],
                 sem=("arbitrary",))


def _token_specs(seq, d):
    br = TOKEN_BLOCK
    nxb = seq // br
    main = pl.BlockSpec((br, d), lambda i: (jnp.minimum(i, nxb - 1), 0))
    tail = pl.BlockSpec((N_META, d), lambda i: (jnp.clip(i * (br // N_META) - 1, 0, seq // N_META - 1), 0))
    return main, tail


def _padded_block(main_ref, tail_ref, first, seq):
    br = TOKEN_BLOCK
    i = pl.program_id(0)
    nxb = seq // br
    main = jnp.where(i < nxb, main_ref[...], 0.0)
    head = jnp.where(i == 0, first, jnp.where(i <= nxb, tail_ref[...], 0.0))
    return jnp.concatenate([head, main[:br - N_META]], axis=0)


def _ln_emb_fwd(x, meta, g, b, rows, after=()):
    seq, d = x.shape
    br = TOKEN_BLOCK
    assert seq % br == 0 and br % N_META == 0 and rows % br == 0

    def body(x_ref, tail_ref, meta_ref, g_ref, b_ref, y_ref, yb_ref):
        z = _padded_block(x_ref, tail_ref, meta_ref[...], seq)
        xhat, _ = _ln_stats(z)
        y = xhat * g_ref[...] + b_ref[...]
        y_ref[...] = y
        yb_ref[...] = y.astype(yb_ref.dtype)

    main, tail = _token_specs(seq, d)
    return _call("ln_emb_fwd", body, (rows // br,),
                 [(x, main), (x, tail), (meta, _whole((N_META, d))), (g.reshape(1, d), _whole((1, d))),
                  (b.reshape(1, d), _whole((1, d)))],
                 [(_sds((rows, d), F32), _rows(br, d)), (_sds((rows, d), MXU_DTYPE), _rows(br, d))],
                 sem=("parallel",), after=after)


def _ln_emb_bwd(x, meta, dh0, g):
    seq, d = x.shape
    br = TOKEN_BLOCK
    step = br // N_META

    def ln_bwd(z, dy, gv):
        xhat, rstd = _ln_stats(z)
        dyg = dy * gv
        m1 = jnp.mean(dyg, axis=-1, keepdims=True)
        m2 = jnp.mean(dyg * xhat, axis=-1, keepdims=True)
        dz = rstd * (dyg - m1 - xhat * m2)
        return dz, jnp.sum(dy * xhat, axis=0, keepdims=True), jnp.sum(dy, axis=0, keepdims=True)

    def body(x_ref, dh_ref, nxt_ref, meta_ref, top_ref, g_ref, dx_ref, dm_ref, dg_ref, db_ref):
        gv = g_ref[...]
        dy = jnp.concatenate([dh_ref[N_META:, :], nxt_ref[...]], axis=0)
        dz, dg, db = ln_bwd(x_ref[...], dy, gv)
        dx_ref[...] = dz

        @pl.when(pl.program_id(0) == 0)
        def _():
            dzm, dgm, dbm = ln_bwd(meta_ref[...], top_ref[...], gv)
            dm_ref[...] = dzm
            dg_ref[...] = dgm
            db_ref[...] = dbm

        dg_ref[...] += dg
        db_ref[...] += db

    small = _whole((N_META, d))
    return _call("ln_emb_bwd", body, (seq // br,),
                 [(x, _rows(br, d)), (dh0, _rows(br, d)), (dh0, pl.BlockSpec((N_META, d), lambda i: ((i + 1) * step, 0))),
                  (meta, small), (dh0, small), (g.reshape(1, d), _whole((1, d)))],
                 [(_sds((seq, d), F32), _rows(br, d)), (_sds((N_META, d), F32), small),
                  (_sds((1, d), F32), _whole((1, d))), (_sds((1, d), F32), _whole((1, d)))], sem=("arbitrary",))


def _ln_ffn_loss(h1, f, tgt, g, b):
    r, d = h1.shape
    seq = tgt.shape[0]
    br = TOKEN_BLOCK

    def body(a_ref, r_ref, t_ref, tail_ref, g_ref, b_ref, l_ref):
        err = _loss_err(a_ref, r_ref, t_ref, tail_ref, g_ref, b_ref, seq)[0]

        @pl.when(pl.program_id(0) == 0)
        def _():
            l_ref[...] = jnp.zeros_like(l_ref)

        l_ref[...] += jnp.sum(jnp.sum(err * err, axis=1, keepdims=True), axis=0, keepdims=True) * (0.5 / d)

    main, tail = _token_specs(seq, d)
    return _call("ln_ffn_loss", body, (r // br,),
                 [(h1, _rows(br, d)), (f, _rows(br, d)), (tgt, main), (tgt, tail),
                  (g.reshape(1, d), _whole((1, d))), (b.reshape(1, d), _whole((1, d)))],
                 [(_sds((1, 1), F32), _whole((1, 1)))], sem=("arbitrary",))[0]


def _loss_err(a_ref, r_ref, t_ref, tail_ref, g_ref, b_ref, seq):
    br, d = a_ref.shape
    xhat, rstd = _ln_stats(ALPHA * a_ref[...] + r_ref[...])
    y = xhat * g_ref[...] + b_ref[...]
    t = _padded_block(t_ref, tail_ref, jnp.zeros((N_META, d), F32), seq)
    rid = lax.broadcasted_iota(jnp.int32, (br, d), 0) + pl.program_id(0) * br
    valid = (rid >= N_META) & (rid < N_META + seq)
    return jnp.where(valid, y - t, 0.0), xhat, rstd


def _ln_ffn_bwd(h1, f, tgt, g, b):
    r, d = h1.shape
    seq = tgt.shape[0]
    br = TOKEN_BLOCK

    def body(a_ref, r_ref, t_ref, tail_ref, g_ref, b_ref, dz_ref, dzb_ref, dg_ref, db_ref):
        err, xhat, rstd = _loss_err(a_ref, r_ref, t_ref, tail_ref, g_ref, b_ref, seq)
        dyv = err * (1.0 / d)
        dyg = dyv * g_ref[...]
        m1 = jnp.mean(dyg, axis=-1, keepdims=True)
        m2 = jnp.mean(dyg * xhat, axis=-1, keepdims=True)
        dz = rstd * (dyg - m1 - xhat * m2)
        dz_ref[...] = dz
        dzb_ref[...] = dz.astype(dzb_ref.dtype)

        @pl.when(pl.program_id(0) == 0)
        def _():
            dg_ref[...] = jnp.zeros_like(dg_ref)
            db_ref[...] = jnp.zeros_like(db_ref)

        dg_ref[...] += jnp.sum(dyv * xhat, axis=0, keepdims=True)
        db_ref[...] += jnp.sum(dyv, axis=0, keepdims=True)

    main, tail = _token_specs(seq, d)
    return _call("ln_ffn_bwd", body, (r // br,),
                 [(h1, _rows(br, d)), (f, _rows(br, d)), (tgt, main), (tgt, tail),
                  (g.reshape(1, d), _whole((1, d))), (b.reshape(1, d), _whole((1, d)))],
                 [(_sds((r, d), F32), _rows(br, d)), (_sds((r, d), MXU_DTYPE), _rows(br, d)),
                  (_sds((1, d), F32), _whole((1, d))), (_sds((1, d), F32), _whole((1, d)))], sem=("arbitrary",))


def _attn_fwd(name, q, k, v, cum_b=None, cum_t=None):
    (qa, qg), (ka, kg), (va, vg) = q, k, v
    r = qa.shape[0]
    tq, tk = ATT_TQ, ATT_TK
    nq, nk = r // tq, r // tk
    bias = cum_b is not None

    def body(*refs):
        if bias:
            q_ref, k_ref, vt_ref, cb_ref, ct_ref, o_ref, ob_ref, lse_ref = refs
        else:
            q_ref, k_ref, vt_ref, o_ref, ob_ref, lse_ref = refs
        i = pl.program_id(1)
        qs = [q_ref[:, _hs(hh)] for hh in range(hg)]
        cqs = [ct_ref[hh] for hh in range(hg)] if bias else None
        diff = lax.broadcasted_iota(jnp.int32, (tk, tq), 0) - lax.broadcasted_iota(jnp.int32, (tk, tq), 1)

        def step(j, carry, masked):
            keys = pl.ds(pl.multiple_of(j * tk, tk), tk)
            out = []
            for hh in range(hg):
                m, l, acc = carry[hh]
                kt = k_ref[keys, _hs(hh)]
                s = lax.dot_general(kt, qs[hh], NT, preferred_element_type=F32)
                if bias:
                    s = s + (cqs[hh] - cb_ref[keys, hh * HP:hh * HP + 1])
                if masked:
                    s = jnp.where(diff <= i * tq - j * tk, s, NEG_INF)
                m_new = jnp.maximum(m, jnp.max(s, axis=0, keepdims=True))
                p = jnp.exp(s - m_new)
                a = jnp.exp(m - m_new)
                l = a * l + jnp.sum(p, axis=0, keepdims=True)
                acc = a * acc + jnp.dot(vt_ref[j, _hs(hh), :], p.astype(kt.dtype), preferred_element_type=F32)
                out.append((m_new, l, acc))
            return tuple(out)

        n_clear = (i * tq + 1) // tk
        n_all = ((i + 1) * tq - 1) // tk + 1
        carry = tuple((jnp.full((1, tq), NEG_INF, F32), jnp.zeros((1, tq), F32), jnp.zeros((HP, tq), F32))
                      for _ in range(hg))
        carry = lax.fori_loop(0, n_clear, lambda j, c: step(j, c, False), carry)
        carry = lax.fori_loop(n_clear, n_all, lambda j, c: step(j, c, True), carry)
        for hh in range(hg):
            m, l, acc = carry[hh]
            o = (acc / l).T
            o_ref[:, _hs(hh)] = o
            ob_ref[:, hh * V_DIM:(hh + 1) * V_DIM] = o[:, :V_DIM].astype(ob_ref.dtype)
            lse_ref[hh] = m + jnp.log(l)

    hg = ATT_HEADS_FWD
    w = hg * HP
    gpw = HW // w
    tile = lambda g: pl.BlockSpec((tq, w), lambda h, i: (i, g * gpw + h))
    res = lambda g: pl.BlockSpec((r, w), lambda h, i: (0, g * gpw + h))
    v_t = _key_tiles_transposed(name + "_vt", va, vg)
    ins = [(qa, tile(qg)), (ka, res(kg)), (v_t, pl.BlockSpec((nk, w, tk), lambda h, i: (0, h, 0)))]
    if bias:
        ins += [(cum_b, res(0)),
                (cum_t.reshape(HEADS, nq, 1, tq), pl.BlockSpec((hg, None, 1, tq), lambda h, i: (h, i, 0, 0)))]
    outs = [(_sds((r, HW), F32), tile(0)),
            (_sds((r, HEADS * V_DIM), MXU_DTYPE), pl.BlockSpec((tq, hg * V_DIM), lambda h, i: (i, h))),
            (_sds((HEADS, nq, 1, tq), F32), pl.BlockSpec((hg, None, 1, tq), lambda h, i: (h, i, 0, 0)))]
    o, ob, lse = _call(name, body, (gpw, nq), ins, outs, sem=("parallel", "parallel"))
    return o, ob, lse.reshape(HEADS, r)


def _key_tiles_transposed(name, a, group):
    r = a.shape[0]
    tk = ATT_TK

    def body(x_ref, o_ref):
        for h in range(HEADS):
            o_ref[_hs(h), :] = x_ref[:, _hs(h)].astype(F32).T.astype(o_ref.dtype)

    return _call(name, body, (r // tk,),
                 [(a, pl.BlockSpec((tk, HW), lambda j: (j, group)))],
                 [(_sds((r // tk, HW, tk), a.dtype), pl.BlockSpec((None, HW, tk), lambda j: (j, 0, 0)))],
                 sem=("parallel",))[0]


def _attn_bwd(name, q, k, v, do_b, o, lse_t, cum_b=None, cum_t=None, out_dtype=F32, after=()):
    (qa, qg), (ka, kg), (va, vg) = q, k, v
    r = qa.shape[0]
    tq, tk = ATT_TQ, ATT_TK
    nq, nk = r // tq, r // tk
    bias = cum_b is not None

    def body(*refs):
        if bias:
            (q_ref, k_ref, v_ref, do_ref, o_ref, lse_ref, cb_ref, ct_ref,
             dq_ref, dk_ref, dv_ref, dcq_ref, dck_ref, dqt_ref, dl_ref) = refs
        else:
            q_ref, k_ref, v_ref, do_ref, o_ref, lse_ref, dq_ref, dk_ref, dv_ref, dqt_ref, dl_ref = refs
        j = pl.program_id(1)

        @pl.when(j == 0)
        def _():
            dqt_ref[...] = jnp.zeros_like(dqt_ref)
            if bias:
                dcq_ref[...] = jnp.zeros_like(dcq_ref)
            for hh in range(hg):
                for i in range(nq):
                    rows = slice(i * tq, (i + 1) * tq)
                    prod = do_ref[rows, _hs(hh)].astype(F32) * o_ref[rows, _hs(hh)]
                    dl_ref[hh, i] = jnp.sum(prod.T, axis=0, keepdims=True)

        kts = [k_ref[:, _hs(hh)] for hh in range(hg)]
        vts = [v_ref[:, _hs(hh)] for hh in range(hg)]
        k_trs = [kt.astype(F32).T.astype(kt.dtype) for kt in kts]
        cks = [cb_ref[:, hh * HP:hh * HP + 1] for hh in range(hg)] if bias else None
        diff = lax.broadcasted_iota(jnp.int32, (tk, tq), 0) - lax.broadcasted_iota(jnp.int32, (tk, tq), 1)

        def step(i, carry, masked):
            rows = pl.ds(pl.multiple_of(i * tq, tq), tq)
            out = []
            for hh in range(hg):
                dk_acc, dv_acc, dck_acc = carry[hh]
                qt = q_ref[rows, _hs(hh)]
                dot = do_ref[rows, _hs(hh)]
                s = lax.dot_general(kts[hh], qt, NT, preferred_element_type=F32)
                if bias:
                    s = s + (ct_ref[hh, i] - cks[hh])
                if masked:
                    s = jnp.where(diff <= i * tq - j * tk, s, NEG_INF)
                p = jnp.exp(s - lse_ref[hh, i])
                dp = lax.dot_general(vts[hh], dot, NT, preferred_element_type=F32)
                ds = p * (dp - dl_ref[hh, i])
                pb = p.astype(dot.dtype)
                dsb = ds.astype(qt.dtype)
                dv_acc = dv_acc + jnp.dot(pb, dot, preferred_element_type=F32)
                dk_acc = dk_acc + jnp.dot(dsb, qt, preferred_element_type=F32)
                dqt_ref[hh, i] += jnp.dot(k_trs[hh], dsb, preferred_element_type=F32)
                if bias:
                    dcq_ref[hh, i] += jnp.sum(ds, axis=0, keepdims=True)
                    dck_acc = dck_acc - jnp.sum(ds, axis=1, keepdims=True)
                out.append((dk_acc, dv_acc, dck_acc))
            return tuple(out)

        i_first = (j * tk) // tq
        i_clear = jnp.minimum(((j + 1) * tk + tq - 2) // tq, nq)
        carry = tuple((jnp.zeros((tk, HP), F32), jnp.zeros((tk, HP), F32), jnp.zeros((tk, 1), F32)) for _ in range(hg))
        carry = lax.fori_loop(i_first, i_clear, lambda i, c: step(i, c, True), carry)
        carry = lax.fori_loop(i_clear, nq, lambda i, c: step(i, c, False), carry)
        for hh in range(hg):
            dk_acc, dv_acc, dck_acc = carry[hh]
            dk_ref[:, _hs(hh)] = dk_acc.astype(dk_ref.dtype)
            dv_ref[:, _hs(hh)] = dv_acc.astype(dv_ref.dtype)
            if bias:
                dck_ref[:, _hs(hh)] = jnp.broadcast_to(dck_acc, (tk, HP))

        @pl.when(j == nk - 1)
        def _():
            for hh in range(hg):
                for i in range(nq):
                    dq_ref[i * tq:(i + 1) * tq, _hs(hh)] = dqt_ref[hh, i].T.astype(dq_ref.dtype)

    hg = ATT_HEADS
    w = hg * HP
    gpw = HW // w
    res = lambda g: pl.BlockSpec((r, w), lambda h, j: (0, g * gpw + h))
    tile = lambda g: pl.BlockSpec((tk, w), lambda h, j: (j, g * gpw + h))
    rowv = pl.BlockSpec((hg, nq, 1, tq), lambda h, j: (h, 0, 0, 0))
    as_rows = lambda a: a.reshape(HEADS, nq, 1, tq)
    ins = [(qa, res(qg)), (ka, tile(kg)), (va, tile(vg)), (do_b, res(0)), (o, res(0)), (as_rows(lse_t), rowv)]
    outs = [(_sds((r, HW), out_dtype), res(0)), (_sds((r, HW), out_dtype), tile(0)), (_sds((r, HW), out_dtype), tile(0))]
    if bias:
        ins += [(cum_b, tile(0)), (as_rows(cum_t), rowv)]
        outs += [(_sds((HEADS, nq, 1, tq), F32), rowv), (_sds((r, HW), F32), tile(0))]
    res_out = _call(name, body, (gpw, nk), ins, outs,
                    scratch=[pltpu.VMEM((hg, nq, HP, tq), F32), pltpu.VMEM((hg, nq, 1, tq), F32)],
                    sem=("parallel", "arbitrary"), after=after)
    if bias:
        dq, dk, dv, dcq, dck = res_out
        return dq, dk, dv, dcq.reshape(HEADS, r), dck
    return res_out


MESH_ID = pl.DeviceIdType.MESH
ANY = pl.BlockSpec(memory_space=pl.ANY)


N_GATHER_COPIES = 8


def _allgather(name, shards):
    n = len(shards)

    def body(*refs):
        x_refs, out_refs = refs[:n], refs[n:2 * n]
        send_sems, recv_sems, local_sems = refs[2 * n:]
        x, y, c = lax.axis_index("x"), lax.axis_index("y"), lax.axis_index("c")
        me, sibling = (x, y, c), (x, y, 1 - c)
        xn, yn, dg = (1 - x, y, c), (x, 1 - y, c), (1 - x, 1 - y, c)
        other = lambda dev: (dev[0], dev[1], 1 - c)

        def slot(ti, dev, half=None):
            ref = out_refs[ti].at[4 * dev[0] + 2 * dev[1] + dev[2]]
            if half is None:
                return ref
            rows = shards[ti].shape[0] // 2
            return ref.at[pl.ds(half * rows, rows)]

        def copy(ti, k, block, to, half=None, src=None):
            return pltpu.make_async_remote_copy(
                src_ref=slot(ti, block, half) if src is None else src, dst_ref=slot(ti, block, half),
                send_sem=send_sems.at[ti, k], recv_sem=recv_sems.at[ti, k], device_id=to, device_id_type=MESH_ID)

        mine = [pltpu.make_async_copy(x_refs[ti], slot(ti, me), local_sems.at[ti]) for ti in range(n)]
        for cp in mine:
            cp.start()
        started = []

        def go(cp):
            cp.start()
            started.append(cp)

        for ti in range(n):
            go(copy(ti, 0, me, sibling, src=x_refs[ti]))
            go(copy(ti, 1, me, xn, src=x_refs[ti]))
            go(copy(ti, 2, me, yn, src=x_refs[ti]))
        for ti in range(n):
            copy(ti, 1, xn, me).wait_recv()
            go(copy(ti, 3, xn, yn, half=0))
            go(copy(ti, 5, xn, sibling))
            copy(ti, 2, yn, me).wait_recv()
            go(copy(ti, 4, yn, xn, half=1))
            go(copy(ti, 6, yn, sibling))
        for ti in range(n):
            copy(ti, 3, dg, me, half=0).wait_recv()
            copy(ti, 4, dg, me, half=1).wait_recv()
            go(copy(ti, 7, dg, sibling))
        for ti in range(n):
            copy(ti, 0, sibling, me).wait_recv()
            for k, dev in ((5, xn), (6, yn), (7, dg)):
                copy(ti, k, other(dev), me).wait_recv()
        for cp in started:
            cp.wait_send()
        for cp in mine:
            cp.wait()

    sems = pltpu.SemaphoreType.DMA((n, N_GATHER_COPIES))
    return pl.pallas_call(
        body, name=name, out_shape=[_sds((N_DEV,) + s.shape, s.dtype) for s in shards],
        in_specs=[ANY] * n, out_specs=[ANY] * n,
        scratch_shapes=[sems, sems, pltpu.SemaphoreType.DMA((n,))],
    )(*shards)


HBM = pl.BlockSpec(memory_space=pltpu.HBM)
SEM = pl.BlockSpec(memory_space=pltpu.SEMAPHORE)
EFFECT = pltpu.SideEffectType.DATAFLOW_SIDE_EFFECTING
N_PEER = N_DEV - 1


def _my_id():
    return 4 * lax.axis_index("x") + 2 * lax.axis_index("y") + lax.axis_index("c")


def _peers():
    x, y, c = lax.axis_index("x"), lax.axis_index("y"), lax.axis_index("c")
    out = []
    for k in range(1, N_DEV):
        px, py, pc = (1 - x if k & 4 else x, 1 - y if k & 2 else y, 1 - c if k & 1 else c)
        out.append(((px, py, pc), 4 * px + 2 * py + pc))
    return out


def _push_copies(src_refs, land_refs, send_sems, recv_sems, scatter, landing):
    me = _my_id()
    out = []
    for ti, (src, land) in enumerate(zip(src_refs, land_refs)):
        for k, (dev, pid) in enumerate(_peers()):
            out.append(pltpu.make_async_remote_copy(
                src_ref=src.at[pid] if scatter else src, dst_ref=land.at[pid if landing else me],
                send_sem=send_sems.at[ti * N_PEER + k], recv_sem=recv_sems.at[ti * N_PEER + k],
                device_id=dev, device_id_type=MESH_ID))
    return out


def _push_start(name, groups, scatter, after=None):
    sizes = [len(g) for g in groups]
    srcs = [a for g in groups for a in g]
    n = len(srcs)
    slot = lambda s: s.shape[1:] if scatter else s.shape
    lands = [lax.empty((N_DEV,) + slot(s), s.dtype) for s in srcs]
    n_after = 0 if after is None else 1
    n_grp = len(groups)

    def body(*refs):
        src_refs, land_refs = refs[:n], refs[n:2 * n]
        sems = refs[2 * n + n_after:2 * n + n_after + 2 * n_grp]
        token = refs[-1]
        lo = 0
        for gi, sz in enumerate(sizes):
            for cp in _push_copies(src_refs[lo:lo + sz], land_refs[lo:lo + sz], sems[2 * gi], sems[2 * gi + 1], scatter, False):
                cp.start()
            lo += sz
        token[...] = jnp.zeros_like(token)

    hbm = lambda a: pltpu.with_memory_space_constraint(a, pltpu.HBM)
    operands = [hbm(a) for a in srcs + lands] + ([after] if n_after else [])
    sem_shapes = [pltpu.SemaphoreType.DMA((sz * N_PEER,)) for sz in sizes for _ in range(2)]
    res = pl.pallas_call(
        body, name=name,
        out_shape=sem_shapes + [pltpu.HBM(a.shape, a.dtype) for a in srcs + lands] + [_sds((8, 128), F32)],
        in_specs=[HBM] * (2 * n) + [ANY] * n_after,
        out_specs=[SEM] * (2 * n_grp) + [HBM] * (2 * n) + [pl.BlockSpec(memory_space=pltpu.VMEM)],
        input_output_aliases={i: 2 * n_grp + i for i in range(2 * n)},
        compiler_params=pltpu.CompilerParams(has_side_effects=EFFECT),
    )(*operands)
    thru = res[2 * n_grp:2 * n_grp + 2 * n]
    handles, lo = [], 0
    for gi, sz in enumerate(sizes):
        handles.append((res[2 * gi], res[2 * gi + 1], list(thru[lo:lo + sz]), list(thru[n + lo:n + lo + sz]), scatter))
        lo += sz
    return handles, res[-1]


def _push_wait(name, handle, after):
    send_sems, recv_sems, srcs, lands, scatter = handle
    n = len(srcs)

    def body(*refs):
        src_refs, land_refs = refs[:n], refs[n:2 * n]
        s_sems, r_sems = refs[2 * n], refs[2 * n + 1]
        for cp in _push_copies(src_refs, land_refs, s_sems, r_sems, scatter, True):
            cp.wait_send()
            cp.wait_recv()

    res = pl.pallas_call(
        body, name=name,
        out_shape=[pltpu.HBM(a.shape, a.dtype) for a in srcs + lands],
        in_specs=[HBM] * (2 * n) + [SEM, SEM, ANY], out_specs=[HBM] * (2 * n),
        input_output_aliases={i: i for i in range(2 * n)},
        compiler_params=pltpu.CompilerParams(has_side_effects=EFFECT),
    )(*srcs, *lands, send_sems, recv_sems, after)
    return list(res[n:])


def _adamw(name, parts, w, m, v, own=None):
    r, c = w.shape
    br = _pick(r, COPY_ROWS, 16)
    has_own = own is not None

    def body(*refs):
        if has_own:
            p_ref, own_ref, w_ref, m_ref, v_ref, g_ref, d_ref, nm_ref, nv_ref = refs
            me = _my_id()
            mine = own_ref[...].astype(F32)
        else:
            p_ref, w_ref, m_ref, v_ref, g_ref, d_ref, nm_ref, nv_ref = refs
        g = None
        for k in range(N_DEV):
            t = p_ref[k].astype(F32)
            if has_own:
                t = jnp.where(me == k, mine, t)
            g = t if g is None else g + t
        mm = ADAM_B1 * m_ref[...] + (1.0 - ADAM_B1) * g
        vv = ADAM_B2 * v_ref[...] + (1.0 - ADAM_B2) * (g * g)
        m_hat = mm / (1.0 - ADAM_B1 ** ADAM_STEP)
        v_hat = vv / (1.0 - ADAM_B2 ** ADAM_STEP)
        g_ref[...] = g
        d_ref[...] = -ADAM_LR * (m_hat / (jnp.sqrt(v_hat) + ADAM_EPS) + ADAM_WD * w_ref[...])
        nm_ref[...] = mm
        nv_ref[...] = vv

    spec = _rows(br, c)
    out = (_sds((r, c), F32), spec)
    ins = [(parts, pl.BlockSpec((N_DEV, br, c), lambda i: (0, i, 0)))] + ([(own, spec)] if has_own else [])
    return _call(name, body, (r // br,), ins + [(w, spec), (m, spec), (v, spec)], [out] * 4, sem=("parallel",))


def _pad_head_cols(w, d):
    k = w.shape[0]
    return jnp.pad(w.reshape(k, HEADS, d), ((0, 0), (0, 0), (0, HP - d))).reshape(k, HW)


def _unpad_head_cols(wp, d):
    k = wp.shape[0]
    return wp.reshape(k, HEADS, HP)[:, :, :d].reshape(k, HEADS * d)


def _pad_head_rows(w, d):
    n = w.shape[1]
    return jnp.pad(w.reshape(HEADS, d, n), ((0, 0), (0, HP - d), (0, 0))).reshape(HW, n)


def _w_in_runs():
    nat = {}
    o = 0
    for nm, wd in (("q", Q_RANK), ("kv", KV_RANK), ("kr", ROPE), ("fq", FOX_W), ("fk", FOX_W), ("fv", FOX_W),
                   ("fl", HEADS), ("gate", 2 * D_MODEL)):
        nat[nm] = o
        o += wd
    runs = [(1, R_QLAT, nat["q"], Q_RANK, 1.0), (1, R_KVLAT, nat["kv"], KV_RANK, 1.0),
            (1, R_LAST + LANE_FL, nat["fl"], HEADS, 1.0), (1, R_LAST + LANE_PE, nat["kr"], ROPE, 1.0),
            (1, R_GATE, nat["gate"], 2 * D_MODEL, 1.0)]
    for grp, (nm, sc) in enumerate((("fq", FOX_SCALE), ("fk", 1.0), ("fv", 1.0))):
        runs.append((0, grp * FOX_W, nat[nm], FOX_W, sc))
    return runs


def _head_pad_moves(pad):
    moves = []
    for grp in range(3):
        for h in range(HEADS):
            narrow, wide = grp * FOX_W + h * FOX_DIM, h * HP
            if pad:
                moves.append((0, None, grp * HW + wide, 0, None, narrow, FOX_DIM, 1.0))
            else:
                moves.append((0, None, narrow, grp, None, wide, FOX_DIM, 1.0))
    return moves


def _sharded_runs(runs, shard_cols):
    out = []
    for half, col, ncol, width, sc in runs:
        while width > 0:
            d, local = divmod(ncol, shard_cols)
            wd = min(width, shard_cols - local)
            out.append((half, col, d, local, wd, sc))
            col, ncol, width = col + wd, ncol + wd, width - wd
    return out


def _remap(name, srcs, out_shapes, moves):
    rows = srcs[0].shape[-2]
    br = _pick(rows, COPY_ROWS, 16)
    ns = len(srcs)

    def spec(shape):
        if len(shape) == 2:
            return pl.BlockSpec((br, shape[1]), lambda i: (i, 0))
        return pl.BlockSpec((shape[0], br, shape[2]), lambda i: (0, i, 0))

    covered = [sum(m[6] for m in moves if m[0] == di) for di in range(len(out_shapes))]
    has_gaps = [cov < (shape[1] if len(shape) == 2 else shape[0] * shape[2])
                for cov, (shape, _) in zip(covered, out_shapes)]

    def body(*refs):
        s_refs, o_refs = refs[:ns], refs[ns:]
        for o, gaps in zip(o_refs, has_gaps):
            if gaps:
                o[...] = jnp.zeros_like(o)
        for di, dl, dc, si, sl, sc0, wd, scale in moves:
            v = s_refs[si][:, sc0:sc0 + wd] if sl is None else s_refs[si][sl, :, sc0:sc0 + wd]
            if scale != 1.0:
                v = v * jnp.asarray(scale, v.dtype)
            v = v.astype(o_refs[di].dtype)
            if dl is None:
                o_refs[di][:, dc:dc + wd] = v
            else:
                o_refs[di][dl, :, dc:dc + wd] = v

    return _call(name, body, (rows // br,), [(a, spec(a.shape)) for a in srcs],
                 [(_sds(shape, dt), spec(shape)) for shape, dt in out_shapes], sem=("parallel",))


def _w_in_from_shards(g3):
    n, rows, c = g3.shape
    moves = [(half, None, col, 0, d, local, wd, sc) for half, col, d, local, wd, sc in _sharded_runs(_w_in_runs(), c)]
    return _remap("w_in_repack", [g3], [((rows, F_W), g3.dtype), ((rows, R_W), g3.dtype)], moves)


def _w_in_grad_to_shards(d_fused, d_rest, n, c):
    rows = d_fused.shape[0]
    moves = [(0, d, local, half, None, col, wd, sc) for half, col, d, local, wd, sc in _sharded_runs(_w_in_runs(), c)]
    return _remap("w_in_grad_unpack", [d_fused, d_rest], [((n, rows, c), d_fused.dtype)], moves)[0]


def _rows_from_shards(name, land, own):
    n, rows, c = land.shape

    def body(land_ref, own_ref, o_ref):
        o_ref[...] = jnp.where(_my_id() == pl.program_id(0), own_ref[...], land_ref[...])

    return _call(name, body, (n,),
                 [(land, pl.BlockSpec((None, rows, c), lambda d: (d, 0, 0))), (own, _whole((rows, c)))],
                 [(_sds((n * rows, c), land.dtype), pl.BlockSpec((rows, c), lambda d: (d, 0)))], sem=("parallel",))[0]


def _cols_from_shards(name, land, own):
    n, rows, c = land.shape
    br = _pick(rows, COPY_ROWS, 16)

    def body(land_ref, own_ref, o_ref):
        me = _my_id()
        for d in range(n):
            o_ref[:, c * d:c * (d + 1)] = jnp.where(me == d, own_ref[...], land_ref[d])

    return _call(name, body, (rows // br,),
                 [(land, pl.BlockSpec((n, br, c), lambda i: (0, i, 0))), (own, _rows(br, c))],
                 [(_sds((rows, n * c), land.dtype), _rows(br, n * c))], sem=("parallel",))[0]


def _cols_to_shards(name, full, n):
    rows, nc = full.shape
    c = nc // n
    return _remap(name, [full], [((n, rows, c), full.dtype)], [(0, d, 0, 0, None, c * d, c, 1.0) for d in range(n)])[0]


def _split_w_kv(w):
    k = w.shape[0]
    w3 = w.reshape(k, HEADS, NOPE + V_DIM)
    padl = lambda a: jnp.pad(a, ((0, 0), (0, 0), (0, HP - a.shape[-1]))).reshape(k, HW)
    return padl(w3[..., :NOPE]), padl(w3[..., NOPE:])


def _merge_w_kv(wk, wv):
    k = wk.shape[0]
    return jnp.concatenate([wk.reshape(k, HEADS, HP)[..., :NOPE], wv.reshape(k, HEADS, HP)[..., :V_DIM]],
                           axis=-1).reshape(k, HEADS * (NOPE + V_DIM))


class _NoComm:
    first_token = ()

    def late_weights(self, group, after):
        return {}

    def send(self, name, grads):
        return ()


def _local_step(x, tgt, p, comm=_NoComm()):
    seq = x.shape[0]
    r = -(-(N_META + seq) // ROW_ALIGN) * ROW_ALIGN
    cd = MXU_DTYPE
    p = dict(p)

    w_f, w_r = p["w_in"]

    pos = jnp.arange(r, dtype=F32)
    inv_freq = ROPE_THETA ** (-jnp.arange(HALF, dtype=F32) / HALF)
    ang = pos[:, None] * inv_freq[None, :]
    cos_t = jnp.tile(jnp.cos(ang), (1, HP // HALF))
    sin_t = jnp.tile(jnp.sin(ang), (1, HP // HALF))
    bf_row = jnp.zeros((1, HP), F32).at[0, LANE_FL:LANE_FL + HEADS].set(p["b_forget"])

    h0, h0b = _ln_emb_fwd(x, p["meta_tokens"], p["ln_emb_g"], p["ln_emb_b"], r, after=comm.first_token)
    proj_f = _matmul("in_proj_f", h0b, w_f, out_dtype=cd)
    proj_f = _remap("proj_f_pad", [proj_f], [((r, 3 * HW), cd)], _head_pad_moves(True))[0]
    proj_r = _matmul("in_proj_r", h0b, w_r)
    latent_gains = (p["q_norm_g"], p["kv_norm_g"])
    ql, kvl = _latent_norm_fwd(proj_r, latent_gains)
    p.update(comm.late_weights("qkv", ql))
    w_q = _pad_head_cols(p["w_q_up"], QK_DIM)
    w_kv = jnp.concatenate(_split_w_kv(p["w_kv_up"]), axis=1)
    q_raw = _matmul("q_up", ql, w_q)
    kv = _matmul("kv_up", kvl, w_kv, out_dtype=cd)
    q_mla, k_mla = _rope_fwd(q_raw, kv, proj_r, cos_t, sin_t)
    o_mla, o_mla_b, lse_mla = _attn_fwd("mla_fwd", (q_mla, 0), (k_mla, 0), (kv, 1))

    cum, cum_t = _forget_fwd(proj_r, bf_row)
    o_fox, o_fox_b, lse_fox = _attn_fwd("fox_fwd", (proj_f, 0), (proj_f, 1), (proj_f, 2), cum, cum_t)

    p.update(comm.late_weights("mix", o_fox_b))
    w_bm = _pad_head_rows(p["w_branch_mla"], V_DIM)
    w_bf = _pad_head_rows(p["w_branch_fox"], FOX_DIM)
    bm = _matmul("branch_mla", o_mla_b, p["w_branch_mla"], out_dtype=cd)
    bfx = _matmul("branch_fox", o_fox_b, p["w_branch_fox"], out_dtype=cd)
    merged = _gate_fwd(proj_r, p["b_gate"], bm, bfx)
    mix = _matmul("out_proj", merged, p["w_out"])
    h1, h1b = _ln_fwd("ln_mix_fwd", h0, mix, p["ln_mix_g"], p["ln_mix_b"])
    p.update(comm.late_weights("ffn", h1b))
    up = _matmul("ffn_up", h1b, p["w_ffn_up"], out_dtype=cd)
    act = _glu_fwd(up, p["conv_w"], p["conv_b"])
    f = _matmul("ffn_down", act, p["w_ffn_down"])
    loss = _ln_ffn_loss(h1, f, tgt, p["ln_ffn_g"], p["ln_ffn_b"])

    g = {}
    dz2, dz2b, g["ln_ffn_g"], g["ln_ffn_b"] = _ln_ffn_bwd(h1, f, tgt, p["ln_ffn_g"], p["ln_ffn_b"])
    d_act = _matmul("ffn_down_dx", dz2b, p["w_ffn_down"], tb=True, out_dtype=cd)
    g["w_ffn_down"] = _matmul("ffn_down_dw", act, dz2b, ta=True, out_dtype=cd)
    d_up, dcw, g["conv_b"] = _glu_bwd(up, p["conv_w"], p["conv_b"], d_act)
    g["conv_w"] = dcw[:3]
    dh1 = _matmul("ffn_up_dx", d_up, p["w_ffn_up"], tb=True, addend=dz2, alpha=ALPHA)
    g["w_ffn_up"] = _matmul("ffn_up_dw", h1b, d_up, ta=True, out_dtype=cd)
    sent = comm.send("ffn", {n: g[n] for n in ("w_ffn_down", "w_ffn_up", "conv_w")})
    dz1, dz1b, g["ln_mix_g"], g["ln_mix_b"] = _ln_bwd("ln_mix_bwd", h0, mix, dh1, p["ln_mix_g"], after=sent)
    dmerged = _matmul("out_proj_dx", dz1b, p["w_out"], tb=True, out_dtype=cd)
    g["w_out"] = _matmul("out_proj_dw", merged, dz1b, ta=True, out_dtype=cd)
    d_bm, d_bf, d_gl, g["b_gate"] = _gate_bwd(proj_r, p["b_gate"], bm, bfx, dmerged)
    do_mla_b = _matmul("branch_mla_dx", d_bm, w_bm, tb=True, out_dtype=cd)
    g["w_branch_mla"] = _matmul("branch_mla_dw", o_mla_b, d_bm, ta=True, out_dtype=cd)
    do_fox_b = _matmul("branch_fox_dx", d_bf, w_bf, tb=True, out_dtype=cd)
    g["w_branch_fox"] = _matmul("branch_fox_dw", o_fox_b, d_bf, ta=True, out_dtype=cd)

    sent = comm.send("mix", {n: g[n] for n in ("w_out", "w_branch_mla", "w_branch_fox")})
    dq_m, dk_m, dv_m = _attn_bwd("mla_bwd", (q_mla, 0), (k_mla, 0), (kv, 1), do_mla_b, o_mla, lse_mla, after=sent)
    dfq, dfk, dfv, dcq, dck = _attn_bwd("fox_bwd", (proj_f, 0), (proj_f, 1), (proj_f, 2), do_fox_b, o_fox, lse_fox,
                                        cum, cum_t, out_dtype=cd)
    dfl, dbf = _forget_bwd(proj_r, bf_row, dcq, dck)
    g["b_forget"] = dbf[:, LANE_FL:LANE_FL + HEADS]

    dq_b, dkv_b, dlast = _rope_bwd(dq_m, dk_m, dv_m, dfl, cos_t, sin_t)
    d_ql = _matmul("q_up_dx", dq_b, w_q, tb=True)
    d_kvl = _matmul("kv_up_dx", dkv_b, w_kv, tb=True)
    d_qlat, d_kvlat, g["q_norm_g"], g["kv_norm_g"] = _latent_norm_bwd(proj_r, (d_ql, d_kvl), latent_gains)
    side_by_side = lambda parts, cols: [(0, None, c0, si, None, 0, a.shape[1], 1.0) for si, (a, c0) in enumerate(zip(parts, cols))]
    dproj_f = _remap("dproj_f_pack", [dfq, dfk, dfv], [((r, F_W), cd)], _head_pad_moves(False))[0]
    rest_parts = [d_qlat, d_kvlat, dlast, d_gl]
    dproj_r = _remap("dproj_r_pack", rest_parts, [((r, R_W), cd)],
                     side_by_side(rest_parts, (R_QLAT, R_KVLAT, R_LAST, R_GATE)))[0]
    g["w_in"] = (_matmul("in_proj_f_dw", h0b, dproj_f, ta=True, out_dtype=cd),
                 _matmul("in_proj_r_dw", h0b, dproj_r, ta=True, out_dtype=cd))
    sent = comm.send("in", {"w_in": g["w_in"]})
    dh0 = _matmul("in_proj_f_dx", dproj_f, w_f, tb=True, addend=dz1, alpha=ALPHA, after=sent)
    g["w_q_up"] = _unpad_head_cols(_matmul("q_up_dw", ql, dq_b, ta=True, out_dtype=cd, after=sent), QK_DIM)
    dw_kv = _matmul("kv_up_dw", kvl, dkv_b, ta=True, out_dtype=cd, after=sent)
    g["w_kv_up"] = _merge_w_kv(dw_kv[:, :HW], dw_kv[:, HW:])
    sent = comm.send("qkv", {n: g[n] for n in ("w_q_up", "w_kv_up")})
    dh0 = _matmul("in_proj_r_dx", dproj_r, w_r, tb=True, addend=dh0, after=sent)
    grad_x, d_meta, g["ln_emb_g"], g["ln_emb_b"] = _ln_emb_bwd(x, p["meta_tokens"], dh0, p["ln_emb_g"])
    return loss, grad_x, d_meta, g


BIG = (("w_in", 1), ("w_q_up", 1), ("w_kv_up", 1), ("w_branch_mla", 1), ("w_branch_fox", 1), ("w_out", 0),
       ("w_ffn_up", 1), ("w_ffn_down", 0))
SMALL_SHARDED = (("meta_tokens", 1), ("conv_w", 1))
EARLY = ("w_in", "meta_tokens")
LATE = {"qkv": ("w_q_up", "w_kv_up", "conv_w"),
        "mix": ("w_branch_mla", "w_branch_fox", "w_out"),
        "ffn": ("w_ffn_up", "w_ffn_down")}
REPLICATED = ("ln_emb_g", "ln_emb_b", "b_gate", "b_forget", "q_norm_g", "kv_norm_g", "ln_mix_g", "ln_mix_b",
              "conv_b", "ln_ffn_g", "ln_ffn_b")
PACK_COLS = 1024


def _pack(flat_list):
    cat = jnp.concatenate(flat_list)
    n = cat.shape[0]
    rows = -(-n // (8 * PACK_COLS)) * 8
    return jnp.pad(cat, (0, rows * PACK_COLS - n)).reshape(rows, PACK_COLS)


def _gathered_full(g3, axis):
    n, r, c = g3.shape
    if axis == 0:
        return g3.reshape(n * r, c)
    return g3.transpose(1, 0, 2).reshape(r, n * c)


def _shard_major(full, axis):
    r, c = full.shape
    if axis == 0:
        return full.reshape(N_DEV, r // N_DEV, c)
    return full.reshape(r, N_DEV, c // N_DEV).transpose(1, 0, 2)


def kernel(x, meta_tokens, ln_emb_g, ln_emb_b, w_in, b_gate, b_forget, q_norm_g, w_q_up, kv_norm_g, w_kv_up, w_branch_mla, w_branch_fox, w_out, ln_mix_g, ln_mix_b, w_ffn_up, conv_w, conv_b, w_ffn_down, ln_ffn_g, ln_ffn_b, loss_target, m_meta_tokens, m_ln_emb_g, m_ln_emb_b, m_w_in, m_b_gate, m_b_forget, m_q_norm_g, m_w_q_up, m_kv_norm_g, m_w_kv_up, m_w_branch_mla, m_w_branch_fox, m_w_out, m_ln_mix_g, m_ln_mix_b, m_w_ffn_up, m_conv_w, m_conv_b, m_w_ffn_down, m_ln_ffn_g, m_ln_ffn_b, v_meta_tokens, v_ln_emb_g, v_ln_emb_b, v_w_in, v_b_gate, v_b_forget, v_q_norm_g, v_w_q_up, v_kv_norm_g, v_w_kv_up, v_w_branch_mla, v_w_branch_fox, v_w_out, v_ln_mix_g, v_ln_mix_b, v_w_ffn_up, v_conv_w, v_conv_b, v_w_ffn_down, v_ln_ffn_g, v_ln_ffn_b):
    names = ("meta_tokens", "ln_emb_g", "ln_emb_b", "w_in", "b_gate", "b_forget", "q_norm_g", "w_q_up", "kv_norm_g",
             "w_kv_up", "w_branch_mla", "w_branch_fox", "w_out", "ln_mix_g", "ln_mix_b", "w_ffn_up", "conv_w", "conv_b",
             "w_ffn_down", "ln_ffn_g", "ln_ffn_b")
    w_args = (meta_tokens, ln_emb_g, ln_emb_b, w_in, b_gate, b_forget, q_norm_g, w_q_up, kv_norm_g, w_kv_up,
              w_branch_mla, w_branch_fox, w_out, ln_mix_g, ln_mix_b, w_ffn_up, conv_w, conv_b, w_ffn_down, ln_ffn_g, ln_ffn_b)
    m_args = (m_meta_tokens, m_ln_emb_g, m_ln_emb_b, m_w_in, m_b_gate, m_b_forget, m_q_norm_g, m_w_q_up, m_kv_norm_g,
              m_w_kv_up, m_w_branch_mla, m_w_branch_fox, m_w_out, m_ln_mix_g, m_ln_mix_b, m_w_ffn_up, m_conv_w, m_conv_b,
              m_w_ffn_down, m_ln_ffn_g, m_ln_ffn_b)
    v_args = (v_meta_tokens, v_ln_emb_g, v_ln_emb_b, v_w_in, v_b_gate, v_b_forget, v_q_norm_g, v_w_q_up, v_kv_norm_g,
              v_w_kv_up, v_w_branch_mla, v_w_branch_fox, v_w_out, v_ln_mix_g, v_ln_mix_b, v_w_ffn_up, v_conv_w, v_conv_b,
              v_w_ffn_down, v_ln_ffn_g, v_ln_ffn_b)
    as2d = lambda a: a.reshape((-1, a.shape[-1])) if a.ndim != 1 else a.reshape(1, -1)
    w = {n: as2d(a) for n, a in zip(names, w_args)}
    m = {n: as2d(a) for n, a in zip(names, m_args)}
    v = {n: as2d(a) for n, a in zip(names, v_args)}
    out_shape = {n: a.shape for n, a in zip(names, w_args)}

    axis_of = dict(BIG + SMALL_SHARDED)
    big = set(n for n, _ in BIG)
    wire = lambda n, a: a.astype(MXU_DTYPE) if n in big else a
    my_id = _my_id()

    early = _allgather("gather_early", [wire(n, w[n]) for n in EARLY])
    p = {n: _gathered_full(g3, axis_of[n]) for n, g3 in zip(EARLY, early) if n != "w_in"}
    p["w_in"] = _w_in_from_shards(early[EARLY.index("w_in")])
    for n in REPLICATED:
        p[n] = w[n].reshape(-1)
    late_src = [[wire(n, w[n]) for n in members] for members in LATE.values()]
    late_handles, late_token = _push_start("gather_late_start", late_src, False, after=early[0])
    late = {group: (members, src, handle)
            for (group, members), src, handle in zip(LATE.items(), late_src, late_handles)}
    sent = {}

    class Comm:
        first_token = (late_token,)

        def late_weights(self, group, after):
            members, src, handle = late[group]
            lands = _push_wait("gather_" + group + "_wait", handle, after)
            out = {}
            for n, own, land in zip(members, src, lands):
                if own.shape[0] % 16:
                    out[n] = _gathered_full(lax.dynamic_update_index_in_dim(land, own, my_id, 0), axis_of[n])
                elif axis_of[n] == 1:
                    out[n] = _cols_from_shards(n + "_repack", land, own)
                else:
                    out[n] = _rows_from_shards(n + "_repack", land, own)
            return out

        def send(self, name, grads):
            names_ = tuple(grads)
            parts = []
            for n in names_:
                if n == "w_in":
                    parts.append(_w_in_grad_to_shards(*grads[n], N_DEV, w[n].shape[1]))
                elif n == "w_ffn_up":
                    parts.append(_cols_to_shards(n + "_grad_unpack", grads[n], N_DEV))
                else:
                    parts.append(_shard_major(grads[n], axis_of[n]).astype(MXU_DTYPE))
            (handle,), token = _push_start("send_" + name + "_start", [parts], True)
            sent[name] = (names_, parts, handle)
            return (token,)

    loss_part, grad_x, d_meta, g = _local_step(x[0], loss_target[0], p, Comm())
    grad_x = grad_x[None]

    small = _pack([d_meta.reshape(-1)] + [g[n].reshape(-1) for n in REPLICATED] + [loss_part.reshape(-1)])
    (small_handle,), small_token = _push_start("send_small_start", [[small]], False)

    res = {}
    prev = small_token
    for name, (names_, parts, handle) in sent.items():
        lands = _push_wait("send_" + name + "_wait", handle, prev)
        for n, part, land in zip(names_, parts, lands):
            own = lax.dynamic_index_in_dim(part, my_id, axis=0, keepdims=False)
            res[n] = _adamw("adamw_" + n, land, w[n], m[n], v[n], own=own)
            prev = res[n][0]
    small_all = _push_wait("send_small_wait", small_handle, prev)[0]
    head = jnp.zeros((d_meta.size,), F32)
    rep_w = _pack([head] + [w[n].reshape(-1) for n in REPLICATED])
    rep_m = _pack([head] + [m[n].reshape(-1) for n in REPLICATED])
    rep_v = _pack([head] + [v[n].reshape(-1) for n in REPLICATED])
    rep_res = _adamw("adamw_replicated", small_all, rep_w, rep_m, rep_v, own=small)
    off = d_meta.size
    for n in REPLICATED:
        sz = w[n].size
        res[n] = tuple(a.reshape(-1)[off:off + sz] for a in rep_res)
        off += sz
    loss = rep_res[0].reshape(-1)[off]
    cols = w["meta_tokens"].shape[1]
    meta_rows = lambda a: a.reshape(a.shape[:-2] + (-1,))[..., :d_meta.size].reshape(a.shape[:-2] + d_meta.shape)
    my_cols = lambda a: lax.dynamic_slice_in_dim(a, my_id * cols, cols, axis=a.ndim - 1)
    res["meta_tokens"] = _adamw("adamw_meta_tokens", my_cols(meta_rows(small_all)), w["meta_tokens"],
                                m["meta_tokens"], v["meta_tokens"], own=my_cols(d_meta))

    outs = [loss, grad_x]
    for idx in range(4):
        outs += [res[n][idx].reshape(out_shape[n]) for n in names]
    return tuple(outs)
```

```python
import jax
import jax.numpy as jnp
from jax import lax
from jax.experimental import pallas as pl
from jax.experimental.pallas import tpu as pltpu

F32 = jnp.float32
BF16 = jnp.bfloat16
MXU_DTYPE = BF16

N_DEV = 8
N_META = 16
D_MODEL = 1024
HEADS = 8
Q_RANK = 384
KV_RANK = 128
NOPE = 64
ROPE = 32
HALF = ROPE // 2
QK_DIM = NOPE + ROPE
V_DIM = 64
FOX_DIM = 64
FOX_W = HEADS * FOX_DIM
D_FF = 2816
ROPE_THETA = 10000.0
LN_EPS = 1e-5
RMS_EPS = 1e-6
ALPHA = 2.0 ** 0.25
MLA_SCALE = QK_DIM ** -0.5
FOX_SCALE = FOX_DIM ** -0.5
NEG_INF = -1e30

HP = 128
HW = HEADS * HP
F_W = 3 * FOX_W
R_GATE = 0
R_KVLAT = R_GATE + 2 * D_MODEL
R_LAST = R_KVLAT + KV_RANK
R_QLAT = R_LAST + HP
R_W = R_QLAT + Q_RANK
assert R_QLAT % Q_RANK == 0 and R_KVLAT % KV_RANK == 0 and R_GATE % D_MODEL == 0 and R_W % HP == 0
LANE_FL = 0
LANE_PE = NOPE

ADAM_LR = 0.001
ADAM_B1 = 0.9
ADAM_B2 = 0.999
ADAM_EPS = 1e-08
ADAM_WD = 0.01
ADAM_STEP = 10

ROW_BLOCK = 384
TOKEN_BLOCK = 256
ATT_TQ = 768
ATT_TK = 768
ATT_HEADS = 2
ATT_HEADS_FWD = 4
ROW_ALIGN = 768
MM_BLOCK_CAP = 1408
VMEM_LIMIT = 56 * 1024 * 1024
HIGHEST = lax.Precision.HIGHEST
NT = (((1,), (1,)), ((), ()))
TN = (((0,), (0,)), ((), ()))


def _params(sem=None):
    return pltpu.CompilerParams(dimension_semantics=sem, vmem_limit_bytes=VMEM_LIMIT)


def _call(name, body, grid, ins, outs, scratch=(), sem=None, after=()):
    n_in = len(ins)
    n_tok = len(after)

    def run(*refs):
        body(*refs[:n_in], *refs[n_in + n_tok:])

    tok_spec = pl.BlockSpec((8, 128), lambda *_: (0, 0))
    return pl.pallas_call(
        run, name=name, grid=grid,
        in_specs=[s for _, s in ins] + [tok_spec] * n_tok,
        out_specs=[s for _, s in outs],
        out_shape=[o for o, _ in outs],
        scratch_shapes=list(scratch),
        compiler_params=_params(sem),
    )(*[a for a, _ in ins], *after)


def _sds(shape, dtype):
    return jax.ShapeDtypeStruct(shape, dtype)


def _rows(br, c, cb=0):
    return pl.BlockSpec((br, c), lambda i: (i, cb))


def _whole(shape):
    n = len(shape)
    return pl.BlockSpec(shape, lambda i: (0,) * n)


def _pick(dim, cap, mult):
    best = None
    d = mult
    while d <= min(dim, cap):
        if dim % d == 0:
            best = d
        d += mult
    return best if best is not None else dim


def _hs(h):
    return slice(h * HP, (h + 1) * HP)


def _matmul(name, a, b, *, ta=False, tb=False, out_dtype=F32, addend=None, alpha=1.0, after=()):
    if ta:
        k, m = a.shape
    else:
        m, k = a.shape
    if tb:
        n, k2 = b.shape
    else:
        k2, n = b.shape
    assert k == k2, (name, a.shape, b.shape)
    bm = _pick(m, MM_BLOCK_CAP, 128 if ta else 16)
    bn = _pick(n, MM_BLOCK_CAP, 128)
    bk = _pick(k, MM_BLOCK_CAP, 128 if (not ta or tb) else 16)
    nk = k // bk
    dims = (((0 if ta else 1,), (1 if tb else 0,)), ((), ()))
    has_add = addend is not None

    def body(*refs):
        a_ref, b_ref = refs[:2]
        add_ref = refs[2] if has_add else None
        o_ref = refs[3 if has_add else 2]

        def finish(r):
            if has_add:
                r = r + alpha * add_ref[...]
            o_ref[...] = r.astype(o_ref.dtype)

        part = lax.dot_general(a_ref[...], b_ref[...], dims, preferred_element_type=F32)
        if nk == 1:
            finish(part)
            return
        acc_ref = refs[-1]
        kk = pl.program_id(2)

        @pl.when(kk == 0)
        def _():
            acc_ref[...] = part

        @pl.when(kk > 0)
        def _():
            acc_ref[...] += part

        @pl.when(kk == nk - 1)
        def _():
            finish(acc_ref[...])

    a_spec = pl.BlockSpec((bk, bm), lambda i, j, l: (l, i)) if ta else pl.BlockSpec((bm, bk), lambda i, j, l: (i, l))
    b_spec = pl.BlockSpec((bn, bk), lambda i, j, l: (j, l)) if tb else pl.BlockSpec((bk, bn), lambda i, j, l: (l, j))
    o_spec = pl.BlockSpec((bm, bn), lambda i, j, l: (i, j))
    ins = [(a, a_spec), (b, b_spec)]
    if has_add:
        ins.append((addend, o_spec))
    return _call(name, body, (m // bm, n // bn, nk), ins, [(_sds((m, n), out_dtype), o_spec)],
                 scratch=[pltpu.VMEM((bm, bn), F32)] if nk > 1 else [],
                 sem=("parallel", "parallel", "arbitrary"), after=after)[0]


def _ln_stats(z):
    mu = jnp.mean(z, axis=-1, keepdims=True)
    zc = z - mu
    var = jnp.mean(zc * zc, axis=-1, keepdims=True)
    rstd = lax.rsqrt(var + LN_EPS)
    return zc * rstd, rstd


def _ln_fwd(name, a, res, g, b, after=()):
    r, d = a.shape
    br = ROW_BLOCK
    has_res = res is not None

    def body(*refs):
        if has_res:
            a_ref, r_ref, g_ref, b_ref, y_ref, yb_ref = refs
            z = ALPHA * a_ref[...] + r_ref[...]
        else:
            a_ref, g_ref, b_ref, y_ref, yb_ref = refs
            z = a_ref[...]
        xhat, _ = _ln_stats(z)
        y = xhat * g_ref[...] + b_ref[...]
        y_ref[...] = y
        yb_ref[...] = y.astype(yb_ref.dtype)

    ins = [(a, _rows(br, d))]
    if has_res:
        ins.append((res, _rows(br, d)))
    ins += [(g.reshape(1, d), _whole((1, d))), (b.reshape(1, d), _whole((1, d)))]
    outs = [(_sds((r, d), F32), _rows(br, d)), (_sds((r, d), MXU_DTYPE), _rows(br, d))]
    return _call(name, body, (r // br,), ins, outs, sem=("parallel",), after=after)


def _ln_bwd(name, a, res, dy, g, after=()):
    r, d = a.shape
    br = ROW_BLOCK
    has_res = res is not None

    def body(*refs):
        if has_res:
            a_ref, r_ref, dy_ref, g_ref, dz_ref, dzb_ref, dg_ref, db_ref = refs
            z = ALPHA * a_ref[...] + r_ref[...]
        else:
            a_ref, dy_ref, g_ref, dz_ref, dzb_ref, dg_ref, db_ref = refs
            z = a_ref[...]
        xhat, rstd = _ln_stats(z)
        dyv = dy_ref[...]
        dyg = dyv * g_ref[...]
        m1 = jnp.mean(dyg, axis=-1, keepdims=True)
        m2 = jnp.mean(dyg * xhat, axis=-1, keepdims=True)
        dz = rstd * (dyg - m1 - xhat * m2)
        dz_ref[...] = dz
        dzb_ref[...] = dz.astype(dzb_ref.dtype)

        @pl.when(pl.program_id(0) == 0)
        def _():
            dg_ref[...] = jnp.zeros_like(dg_ref)
            db_ref[...] = jnp.zeros_like(db_ref)

        dg_ref[...] += jnp.sum(dyv * xhat, axis=0, keepdims=True)
        db_ref[...] += jnp.sum(dyv, axis=0, keepdims=True)

    ins = [(a, _rows(br, d))]
    if has_res:
        ins.append((res, _rows(br, d)))
    ins += [(dy, _rows(br, d)), (g.reshape(1, d), _whole((1, d)))]
    outs = [(_sds((r, d), F32), _rows(br, d)), (_sds((r, d), MXU_DTYPE), _rows(br, d)),
            (_sds((1, d), F32), _whole((1, d))), (_sds((1, d), F32), _whole((1, d)))]
    return _call(name, body, (r // br,), ins, outs, sem=("arbitrary",), after=after)


LATENTS = ((R_QLAT // Q_RANK, Q_RANK), (R_KVLAT // KV_RANK, KV_RANK))


def _latent_norm_fwd(proj_r, gains):
    r = proj_r.shape[0]
    br = ROW_BLOCK

    def body(xq_ref, xk_ref, gq_ref, gk_ref, yq_ref, yk_ref):
        for x_ref, g_ref, y_ref in ((xq_ref, gq_ref, yq_ref), (xk_ref, gk_ref, yk_ref)):
            x = x_ref[...]
            rstd = lax.rsqrt(jnp.mean(x * x, axis=-1, keepdims=True) + RMS_EPS)
            y_ref[...] = (x * rstd * g_ref[...]).astype(y_ref.dtype)

    return _call("latent_norm_fwd", body, (r // br,),
                 [(proj_r, _rows(br, wd, cb)) for cb, wd in LATENTS]
                 + [(g.reshape(1, wd), _whole((1, wd))) for g, (_, wd) in zip(gains, LATENTS)],
                 [(_sds((r, wd), MXU_DTYPE), _rows(br, wd)) for _, wd in LATENTS], sem=("parallel",))


def _latent_norm_bwd(proj_r, dys, gains):
    r = proj_r.shape[0]
    br = ROW_BLOCK

    def body(xq_ref, xk_ref, dq_ref, dk_ref, gq_ref, gk_ref, oq_ref, ok_ref, dgq_ref, dgk_ref):
        @pl.when(pl.program_id(0) == 0)
        def _():
            dgq_ref[...] = jnp.zeros_like(dgq_ref)
            dgk_ref[...] = jnp.zeros_like(dgk_ref)

        for x_ref, dy_ref, g_ref, dx_ref, dg_ref in ((xq_ref, dq_ref, gq_ref, oq_ref, dgq_ref),
                                                     (xk_ref, dk_ref, gk_ref, ok_ref, dgk_ref)):
            x = x_ref[...]
            rstd = lax.rsqrt(jnp.mean(x * x, axis=-1, keepdims=True) + RMS_EPS)
            nrm = x * rstd
            dyv = dy_ref[...]
            dyg = dyv * g_ref[...]
            dx_ref[...] = (rstd * (dyg - nrm * jnp.mean(dyg * nrm, axis=-1, keepdims=True))).astype(dx_ref.dtype)
            dg_ref[...] += jnp.sum(dyv * nrm, axis=0, keepdims=True)

    return _call("latent_norm_bwd", body, (r // br,),
                 [(proj_r, _rows(br, wd, cb)) for cb, wd in LATENTS]
                 + [(dy, _rows(br, wd)) for dy, (_, wd) in zip(dys, LATENTS)]
                 + [(g.reshape(1, wd), _whole((1, wd))) for g, (_, wd) in zip(gains, LATENTS)],
                 [(_sds((r, wd), MXU_DTYPE), _rows(br, wd)) for _, wd in LATENTS]
                 + [(_sds((1, wd), F32), _whole((1, wd))) for _, wd in LATENTS], sem=("arbitrary",))


def _lane_iota(shape):
    return lax.broadcasted_iota(jnp.int32, shape, 1)


def _rotary(t, c, s, lane, sign):
    second = pltpu.roll(t, HP - HALF, axis=1)
    first = pltpu.roll(t, HALF, axis=1)
    lo = (lane >= LANE_PE) & (lane < LANE_PE + HALF)
    hi = (lane >= LANE_PE + HALF) & (lane < LANE_PE + ROPE)
    return jnp.where(lo, t * c - sign * second * s, jnp.where(hi, t * c + sign * first * s, t))


def _rope_fwd(q_raw, kv, proj_r, cos_t, sin_t):
    r = q_raw.shape[0]
    br = ROW_BLOCK

    def body(q_ref, kv_ref, t_ref, c_ref, s_ref, qo_ref, ko_ref, vo_ref):
        c = c_ref[...]
        s = s_ref[...]
        lane = _lane_iota((br, HP))
        pe = (lane >= LANE_PE) & (lane < LANE_PE + ROPE)
        left = lane < NOPE
        kp = jnp.where(pe, _rotary(t_ref[...], c, s, lane, 1.0), 0.0)
        for h in range(HEADS):
            qo_ref[:, _hs(h)] = (_rotary(q_ref[:, _hs(h)], c, s, lane, 1.0) * MLA_SCALE).astype(qo_ref.dtype)
            t = kv_ref[:, _hs(h)].astype(F32)
            ko_ref[:, _hs(h)] = (jnp.where(left, t, 0.0) + kp).astype(ko_ref.dtype)
            vo_ref[:, _hs(h)] = jnp.where(left, pltpu.roll(t, HP - NOPE, axis=1), 0.0).astype(vo_ref.dtype)

    blk = _rows(br, HP)
    wide = _rows(br, HW)
    return _call("rope_fwd", body, (r // br,),
                 [(q_raw, wide), (kv, wide), (proj_r, _rows(br, HP, R_LAST // HP)), (cos_t, blk), (sin_t, blk)],
                 [(_sds((r, HW), MXU_DTYPE), wide)] * 3, sem=("parallel",))


def _rope_bwd(dq, dk, dv, dfl, cos_t, sin_t):
    r = dq.shape[0]
    br = ROW_BLOCK

    def body(dq_ref, dk_ref, dv_ref, fl_ref, c_ref, s_ref, dqo_ref, dkv_ref, dl_ref):
        c = c_ref[...]
        s = s_ref[...]
        lane = _lane_iota((br, HP))
        pe = (lane >= LANE_PE) & (lane < LANE_PE + ROPE)
        left = lane < NOPE
        acc = jnp.zeros((br, HP), F32)
        for h in range(HEADS):
            dqo_ref[:, _hs(h)] = (_rotary(dq_ref[:, _hs(h)], c, s, lane, -1.0) * MLA_SCALE).astype(dqo_ref.dtype)
            dkh = dk_ref[:, _hs(h)]
            acc = acc + dkh
            dkv_ref[:, _hs(h)] = jnp.where(left, dkh, pltpu.roll(dv_ref[:, _hs(h)], NOPE, axis=1)).astype(dkv_ref.dtype)
        dl_ref[...] = (jnp.where(pe, _rotary(acc, c, s, lane, -1.0), 0.0) + fl_ref[...]).astype(dl_ref.dtype)

    blk = _rows(br, HP)
    wide = _rows(br, HW)
    return _call("rope_bwd", body, (r // br,),
                 [(dq, wide), (dk, wide), (dv, wide), (dfl, blk), (cos_t, blk), (sin_t, blk)],
                 [(_sds((r, HW), MXU_DTYPE), wide), (_sds((r, HW), MXU_DTYPE), wide), (_sds((r, HP), MXU_DTYPE), blk)],
                 sem=("parallel",))


def _log_sigmoid(x):
    return jnp.minimum(x, 0.0) - jnp.log(1.0 + jnp.exp(-jnp.abs(x)))


def _head_lane(x, h, lane):
    return jnp.sum(jnp.where(lane == h, x, 0.0), axis=1, keepdims=True)


def _forget_fwd(proj_r, bf_row):
    r = proj_r.shape[0]
    br = ROW_BLOCK

    def body(t_ref, b_ref, ob_ref, ot_ref, carry_ref):
        @pl.when(pl.program_id(0) == 0)
        def _():
            carry_ref[...] = jnp.zeros_like(carry_ref)

        x = t_ref[...] + b_ref[...]
        lane = _lane_iota(x.shape)
        lf = jnp.where((lane >= LANE_FL) & (lane < LANE_FL + HEADS), _log_sigmoid(x), 0.0)
        tri = (lax.broadcasted_iota(jnp.int32, (br, br), 0) >= lax.broadcasted_iota(jnp.int32, (br, br), 1)).astype(F32)
        cum = jnp.dot(tri, lf, precision=HIGHEST, preferred_element_type=F32) + carry_ref[0:1, :]
        for h in range(HEADS):
            ob_ref[:, _hs(h)] = jnp.broadcast_to(_head_lane(cum, LANE_FL + h, lane), (br, HP))
        ot_ref[...] = cum.T[LANE_FL:LANE_FL + HEADS, :]
        carry_ref[...] = jnp.broadcast_to(cum[br - 1:br, :], carry_ref.shape)

    return _call("forget_fwd", body, (r // br,),
                 [(proj_r, _rows(br, HP, R_LAST // HP)), (bf_row, _whole((1, HP)))],
                 [(_sds((r, HW), F32), _rows(br, HW)), (_sds((HEADS, r), F32), pl.BlockSpec((HEADS, br), lambda i: (0, i)))],
                 scratch=[pltpu.VMEM((8, HP), F32)], sem=("arbitrary",))


def _forget_bwd(proj_r, bf_row, dcq_t, dck_b):
    r = proj_r.shape[0]
    br = ROW_BLOCK
    nb = r // br

    def body(t_ref, b_ref, dcq_ref, dck_ref, o_ref, db_ref, carry_ref):
        @pl.when(pl.program_id(0) == 0)
        def _():
            carry_ref[...] = jnp.zeros_like(carry_ref)
            db_ref[...] = jnp.zeros_like(db_ref)

        lane = _lane_iota((br, HP))
        dc = jnp.concatenate([dcq_ref[...], jnp.zeros((HP - HEADS, br), F32)], axis=0).T
        for h in range(HEADS):
            dc = dc + jnp.where(lane == LANE_FL + h, dck_ref[:, h * HP:h * HP + 1], 0.0)
        triu = (lax.broadcasted_iota(jnp.int32, (br, br), 0) <= lax.broadcasted_iota(jnp.int32, (br, br), 1)).astype(F32)
        dlf = jnp.dot(triu, dc, precision=HIGHEST, preferred_element_type=F32) + carry_ref[0:1, :]
        carry_ref[...] = jnp.broadcast_to(dlf[0:1, :], carry_ref.shape)
        x = t_ref[...] + b_ref[...]
        dfl = jnp.where((lane >= LANE_FL) & (lane < LANE_FL + HEADS), dlf * jax.nn.sigmoid(-x), 0.0)
        o_ref[...] = dfl
        db_ref[...] += jnp.sum(dfl, axis=0, keepdims=True)

    rev = pl.BlockSpec((br, HP), lambda i: (nb - 1 - i, 0))
    return _call("forget_bwd", body, (nb,),
                 [(proj_r, pl.BlockSpec((br, HP), lambda i: (nb - 1 - i, R_LAST // HP))), (bf_row, _whole((1, HP))),
                  (dcq_t, pl.BlockSpec((HEADS, br), lambda i: (0, nb - 1 - i))),
                  (dck_b, pl.BlockSpec((br, HW), lambda i: (nb - 1 - i, 0)))],
                 [(_sds((r, HP), F32), rev), (_sds((1, HP), F32), _whole((1, HP)))],
                 scratch=[pltpu.VMEM((8, HP), F32)], sem=("arbitrary",))


def _gate_fwd(proj_r, b_gate, bm, bfx):
    r, d = bm.shape
    br = ROW_BLOCK
    cb = R_GATE // d

    def body(gm_ref, gf_ref, b1_ref, b2_ref, bm_ref, bf_ref, o_ref):
        g1 = jax.nn.sigmoid(gm_ref[...] + b1_ref[...])
        g2 = jax.nn.sigmoid(gf_ref[...] + b2_ref[...])
        o_ref[...] = (g1 * bm_ref[...].astype(F32) + g2 * bf_ref[...].astype(F32)).astype(o_ref.dtype)

    b1 = b_gate[:d].reshape(1, d)
    b2 = b_gate[d:].reshape(1, d)
    return _call("gate_fwd", body, (r // br,),
                 [(proj_r, _rows(br, d, cb)), (proj_r, _rows(br, d, cb + 1)), (b1, _whole((1, d))), (b2, _whole((1, d))),
                  (bm, _rows(br, d)), (bfx, _rows(br, d))],
                 [(_sds((r, d), MXU_DTYPE), _rows(br, d))], sem=("parallel",))[0]


def _gate_bwd(proj_r, b_gate, bm, bfx, dmerged):
    r, d = bm.shape
    br = ROW_BLOCK
    cb = R_GATE // d

    def body(gm_ref, gf_ref, b1_ref, b2_ref, bm_ref, bf_ref, dm_ref, dbm_ref, dbf_ref, dgl_ref, dbg_ref):
        g1 = jax.nn.sigmoid(gm_ref[...] + b1_ref[...])
        g2 = jax.nn.sigmoid(gf_ref[...] + b2_ref[...])
        dm = dm_ref[...].astype(F32)
        dbm_ref[...] = (dm * g1).astype(dbm_ref.dtype)
        dbf_ref[...] = (dm * g2).astype(dbf_ref.dtype)
        dl1 = dm * bm_ref[...].astype(F32) * (g1 * (1.0 - g1))
        dl2 = dm * bf_ref[...].astype(F32) * (g2 * (1.0 - g2))
        dgl_ref[:, 0:d] = dl1.astype(dgl_ref.dtype)
        dgl_ref[:, d:2 * d] = dl2.astype(dgl_ref.dtype)

        @pl.when(pl.program_id(0) == 0)
        def _():
            dbg_ref[...] = jnp.zeros_like(dbg_ref)

        dbg_ref[:, 0:d] += jnp.sum(dl1, axis=0, keepdims=True)
        dbg_ref[:, d:2 * d] += jnp.sum(dl2, axis=0, keepdims=True)

    b1 = b_gate[:d].reshape(1, d)
    b2 = b_gate[d:].reshape(1, d)
    return _call("gate_bwd", body, (r // br,),
                 [(proj_r, _rows(br, d, cb)), (proj_r, _rows(br, d, cb + 1)), (b1, _whole((1, d))), (b2, _whole((1, d))),
                  (bm, _rows(br, d)), (bfx, _rows(br, d)), (dmerged, _rows(br, d))],
                 [(_sds((r, d), MXU_DTYPE), _rows(br, d)), (_sds((r, d), MXU_DTYPE), _rows(br, d)),
                  (_sds((r, 2 * d), MXU_DTYPE), _rows(br, 2 * d)), (_sds((1, 2 * d), F32), _whole((1, 2 * d)))],
                 sem=("arbitrary",))


HALO = 16
GLU_BWD_BLOCK = 256
COPY_ROWS = 512


def _conv_taps(gp, halo, first_block):
    halo = jnp.where(first_block, 0.0, halo.astype(F32))
    rid = lax.broadcasted_iota(jnp.int32, gp.shape, 0)
    last, prev = halo[HALO - 1:HALO, :], halo[HALO - 2:HALO - 1, :]
    g1 = jnp.where(rid == 0, last, pltpu.roll(gp, 1, axis=0))
    g2 = jnp.where(rid == 0, prev, jnp.where(rid == 1, last, pltpu.roll(gp, 2, axis=0)))
    return g1, g2


def _prev_halo(br, c):
    return pl.BlockSpec((HALO, c), lambda i: (jnp.maximum(i * (br // HALO) - 1, 0), 0))


def _glu_fwd(up, conv_w, conv_b):
    r = up.shape[0]
    c = D_FF
    br = ROW_BLOCK

    def body(gp_ref, halo_ref, val_ref, w_ref, b_ref, o_ref):
        gp = gp_ref[...].astype(F32)
        g1, g2 = _conv_taps(gp, halo_ref[...], pl.program_id(0) == 0)
        gate = w_ref[0:1, :] * g2 + w_ref[1:2, :] * g1 + w_ref[2:3, :] * gp + b_ref[...]
        o_ref[...] = (gate * jax.nn.sigmoid(gate) * val_ref[...].astype(F32)).astype(o_ref.dtype)

    return _call("glu_fwd", body, (r // br,),
                 [(up, _rows(br, c, 0)), (up, _prev_halo(br, c)), (up, _rows(br, c, 1)),
                  (conv_w, _whole((3, c))), (conv_b.reshape(1, c), _whole((1, c)))],
                 [(_sds((r, c), MXU_DTYPE), _rows(br, c))], sem=("parallel",))[0]


def _glu_bwd(up, conv_w, conv_b, d_act):
    r = up.shape[0]
    c = D_FF
    br = GLU_BWD_BLOCK
    nb = r // br

    def body(gp_ref, halo_ref, val_ref, da_ref, gpn_ref, valn_ref, dan_ref, w_ref, b_ref, o_ref, dw_ref, db_ref):
        i = pl.program_id(0)
        w0, w1, w2, bias = w_ref[0:1, :], w_ref[1:2, :], w_ref[2:3, :], b_ref[...]

        def d_gate(gp, g1, g2, val, da):
            gate = w0 * g2 + w1 * g1 + w2 * gp + bias
            sg = jax.nn.sigmoid(gate)
            return da * val * (sg * (1.0 + gate * (1.0 - sg))), da * (gate * sg)

        gp = gp_ref[...].astype(F32)
        g1, g2 = _conv_taps(gp, halo_ref[...], i == 0)
        dg, dv = d_gate(gp, g1, g2, val_ref[...].astype(F32), da_ref[...].astype(F32))
        gpn = gpn_ref[...].astype(F32)
        g1n, g2n = _conv_taps(gpn, gp[br - HALO:, :], False)
        dgn, _ = d_gate(gpn, g1n, g2n, valn_ref[...].astype(F32), dan_ref[...].astype(F32))
        dgn = jnp.where(i == nb - 1, 0.0, dgn)
        rid = lax.broadcasted_iota(jnp.int32, dg.shape, 0)
        u1 = jnp.where(rid == br - 1, dgn[0:1, :], pltpu.roll(dg, br - 1, axis=0))
        u2 = jnp.where(rid == br - 1, dgn[1:2, :], jnp.where(rid == br - 2, dgn[0:1, :], pltpu.roll(dg, br - 2, axis=0)))
        o_ref[:, 0:c] = (w2 * dg + w1 * u1 + w0 * u2).astype(o_ref.dtype)
        o_ref[:, c:2 * c] = dv.astype(o_ref.dtype)

        @pl.when(i == 0)
        def _():
            dw_ref[...] = jnp.zeros_like(dw_ref)
            db_ref[...] = jnp.zeros_like(db_ref)

        dw_ref[0:1, :] += jnp.sum(dg * g2, axis=0, keepdims=True)
        dw_ref[1:2, :] += jnp.sum(dg * g1, axis=0, keepdims=True)
        dw_ref[2:3, :] += jnp.sum(dg * gp, axis=0, keepdims=True)
        db_ref[...] += jnp.sum(dg, axis=0, keepdims=True)

    nxt = lambda cb: pl.BlockSpec((HALO, c), lambda i: (jnp.minimum((i + 1) * (br // HALO), r // HALO - 1), cb))
    return _call("glu_bwd", body, (nb,),
                 [(up, _rows(br, c, 0)), (up, _prev_halo(br, c)), (up, _rows(br, c, 1)), (d_act, _rows(br, c)),
                  (up, nxt(0)), (up, nxt(1)), (d_act, nxt(0)),
                  (conv_w, _whole((3, c))), (conv_b.reshape(1, c), _whole((1, c)))],
                 [(_sds((r, 2 * c), MXU_DTYPE), _rows(br, 2 * c)),
                  (_sds((8, c), F32), _whole((8, c))), (_sds((1, c), F32), _whole((1, c)))],
                 sem=("arbitrary",))


def _token_specs(seq, d):
    br = TOKEN_BLOCK
    nxb = seq // br
    main = pl.BlockSpec((br, d), lambda i: (jnp.minimum(i, nxb - 1), 0))
    tail = pl.BlockSpec((N_META, d), lambda i: (jnp.clip(i * (br // N_META) - 1, 0, seq // N_META - 1), 0))
    return main, tail


def _padded_block(main_ref, tail_ref, first, seq):
    br = TOKEN_BLOCK
    i = pl.program_id(0)
    nxb = seq // br
    main = jnp.where(i < nxb, main_ref[...], 0.0)
    head = jnp.where(i == 0, first, jnp.where(i <= nxb, tail_ref[...], 0.0))
    return jnp.concatenate([head, main[:br - N_META]], axis=0)


def _ln_emb_fwd(x, meta, g, b, rows, after=()):
    seq, d = x.shape
    br = TOKEN_BLOCK
    assert seq % br == 0 and br % N_META == 0 and rows % br == 0

    def body(x_ref, tail_ref, meta_ref, g_ref, b_ref, y_ref, yb_ref):
        z = _padded_block(x_ref, tail_ref, meta_ref[...], seq)
        xhat, _ = _ln_stats(z)
        y = xhat * g_ref[...] + b_ref[...]
        y_ref[...] = y
        yb_ref[...] = y.astype(yb_ref.dtype)

    main, tail = _token_specs(seq, d)
    return _call("ln_emb_fwd", body, (rows // br,),
                 [(x, main), (x, tail), (meta, _whole((N_META, d))), (g.reshape(1, d), _whole((1, d))),
                  (b.reshape(1, d), _whole((1, d)))],
                 [(_sds((rows, d), F32), _rows(br, d)), (_sds((rows, d), MXU_DTYPE), _rows(br, d))],
                 sem=("parallel",), after=after)


def _ln_emb_bwd(x, meta, dh0, g):
    seq, d = x.shape
    br = TOKEN_BLOCK
    step = br // N_META

    def ln_bwd(z, dy, gv):
        xhat, rstd = _ln_stats(z)
        dyg = dy * gv
        m1 = jnp.mean(dyg, axis=-1, keepdims=True)
        m2 = jnp.mean(dyg * xhat, axis=-1, keepdims=True)
        dz = rstd * (dyg - m1 - xhat * m2)
        return dz, jnp.sum(dy * xhat, axis=0, keepdims=True), jnp.sum(dy, axis=0, keepdims=True)

    def body(x_ref, dh_ref, nxt_ref, meta_ref, top_ref, g_ref, dx_ref, dm_ref, dg_ref, db_ref):
        gv = g_ref[...]
        dy = jnp.concatenate([dh_ref[N_META:, :], nxt_ref[...]], axis=0)
        dz, dg, db = ln_bwd(x_ref[...], dy, gv)
        dx_ref[...] = dz

        @pl.when(pl.program_id(0) == 0)
        def _():
            dzm, dgm, dbm = ln_bwd(meta_ref[...], top_ref[...], gv)
            dm_ref[...] = dzm
            dg_ref[...] = dgm
            db_ref[...] = dbm

        dg_ref[...] += dg
        db_ref[...] += db

    small = _whole((N_META, d))
    return _call("ln_emb_bwd", body, (seq // br,),
                 [(x, _rows(br, d)), (dh0, _rows(br, d)), (dh0, pl.BlockSpec((N_META, d), lambda i: ((i + 1) * step, 0))),
                  (meta, small), (dh0, small), (g.reshape(1, d), _whole((1, d)))],
                 [(_sds((seq, d), F32), _rows(br, d)), (_sds((N_META, d), F32), small),
                  (_sds((1, d), F32), _whole((1, d))), (_sds((1, d), F32), _whole((1, d)))], sem=("arbitrary",))


def _ln_ffn_loss(h1, f, tgt, g, b):
    r, d = h1.shape
    seq = tgt.shape[0]
    br = TOKEN_BLOCK

    def body(a_ref, r_ref, t_ref, tail_ref, g_ref, b_ref, l_ref):
        err = _loss_err(a_ref, r_ref, t_ref, tail_ref, g_ref, b_ref, seq)[0]

        @pl.when(pl.program_id(0) == 0)
        def _():
            l_ref[...] = jnp.zeros_like(l_ref)

        l_ref[...] += jnp.sum(jnp.sum(err * err, axis=1, keepdims=True), axis=0, keepdims=True) * (0.5 / d)

    main, tail = _token_specs(seq, d)
    return _call("ln_ffn_loss", body, (r // br,),
                 [(h1, _rows(br, d)), (f, _rows(br, d)), (tgt, main), (tgt, tail),
                  (g.reshape(1, d), _whole((1, d))), (b.reshape(1, d), _whole((1, d)))],
                 [(_sds((1, 1), F32), _whole((1, 1)))], sem=("arbitrary",))[0]


def _loss_err(a_ref, r_ref, t_ref, tail_ref, g_ref, b_ref, seq):
    br, d = a_ref.shape
    xhat, rstd = _ln_stats(ALPHA * a_ref[...] + r_ref[...])
    y = xhat * g_ref[...] + b_ref[...]
    t = _padded_block(t_ref, tail_ref, jnp.zeros((N_META, d), F32), seq)
    rid = lax.broadcasted_iota(jnp.int32, (br, d), 0) + pl.program_id(0) * br
    valid = (rid >= N_META) & (rid < N_META + seq)
    return jnp.where(valid, y - t, 0.0), xhat, rstd


def _ln_ffn_bwd(h1, f, tgt, g, b):
    r, d = h1.shape
    seq = tgt.shape[0]
    br = TOKEN_BLOCK

    def body(a_ref, r_ref, t_ref, tail_ref, g_ref, b_ref, dz_ref, dzb_ref, dg_ref, db_ref):
        err, xhat, rstd = _loss_err(a_ref, r_ref, t_ref, tail_ref, g_ref, b_ref, seq)
        dyv = err * (1.0 / d)
        dyg = dyv * g_ref[...]
        m1 = jnp.mean(dyg, axis=-1, keepdims=True)
        m2 = jnp.mean(dyg * xhat, axis=-1, keepdims=True)
        dz = rstd * (dyg - m1 - xhat * m2)
        dz_ref[...] = dz
        dzb_ref[...] = dz.astype(dzb_ref.dtype)

        @pl.when(pl.program_id(0) == 0)
        def _():
            dg_ref[...] = jnp.zeros_like(dg_ref)
            db_ref[...] = jnp.zeros_like(db_ref)

        dg_ref[...] += jnp.sum(dyv * xhat, axis=0, keepdims=True)
        db_ref[...] += jnp.sum(dyv, axis=0, keepdims=True)

    main, tail = _token_specs(seq, d)
    return _call("ln_ffn_bwd", body, (r // br,),
                 [(h1, _rows(br, d)), (f, _rows(br, d)), (tgt, main), (tgt, tail),
                  (g.reshape(1, d), _whole((1, d))), (b.reshape(1, d), _whole((1, d)))],
                 [(_sds((r, d), F32), _rows(br, d)), (_sds((r, d), MXU_DTYPE), _rows(br, d)),
                  (_sds((1, d), F32), _whole((1, d))), (_sds((1, d), F32), _whole((1, d)))], sem=("arbitrary",))


def _attn_fwd(name, q, k, v, cum_b=None, cum_t=None):
    (qa, qg), (ka, kg), (va, vg) = q, k, v
    r = qa.shape[0]
    tq, tk = ATT_TQ, ATT_TK
    nq, nk = r // tq, r // tk
    bias = cum_b is not None

    def body(*refs):
        if bias:
            q_ref, k_ref, vt_ref, cb_ref, ct_ref, o_ref, ob_ref, lse_ref = refs
        else:
            q_ref, k_ref, vt_ref, o_ref, ob_ref, lse_ref = refs
        i = pl.program_id(1)
        qs = [q_ref[:, _hs(hh)] for hh in range(hg)]
        cqs = [ct_ref[hh] for hh in range(hg)] if bias else None
        diff = lax.broadcasted_iota(jnp.int32, (tk, tq), 0) - lax.broadcasted_iota(jnp.int32, (tk, tq), 1)

        def step(j, carry, masked):
            keys = pl.ds(pl.multiple_of(j * tk, tk), tk)
            out = []
            for hh in range(hg):
                m, l, acc = carry[hh]
                kt = k_ref[keys, _hs(hh)]
                s = lax.dot_general(kt, qs[hh], NT, preferred_element_type=F32)
                if bias:
                    s = s + (cqs[hh] - cb_ref[keys, hh * HP:hh * HP + 1])
                if masked:
                    s = jnp.where(diff <= i * tq - j * tk, s, NEG_INF)
                m_new = jnp.maximum(m, jnp.max(s, axis=0, keepdims=True))
                p = jnp.exp(s - m_new)
                a = jnp.exp(m - m_new)
                l = a * l + jnp.sum(p, axis=0, keepdims=True)
                acc = a * acc + jnp.dot(vt_ref[j, _hs(hh), :], p.astype(kt.dtype), preferred_element_type=F32)
                out.append((m_new, l, acc))
            return tuple(out)

        n_clear = (i * tq + 1) // tk
        n_all = ((i + 1) * tq - 1) // tk + 1
        carry = tuple((jnp.full((1, tq), NEG_INF, F32), jnp.zeros((1, tq), F32), jnp.zeros((HP, tq), F32))
                      for _ in range(hg))
        carry = lax.fori_loop(0, n_clear, lambda j, c: step(j, c, False), carry)
        carry = lax.fori_loop(n_clear, n_all, lambda j, c: step(j, c, True), carry)
        for hh in range(hg):
            m, l, acc = carry[hh]
            o = (acc / l).T
            o_ref[:, _hs(hh)] = o
            ob_ref[:, hh * V_DIM:(hh + 1) * V_DIM] = o[:, :V_DIM].astype(ob_ref.dtype)
            lse_ref[hh] = m + jnp.log(l)

    hg = ATT_HEADS_FWD
    w = hg * HP
    gpw = HW // w
    tile = lambda g: pl.BlockSpec((tq, w), lambda h, i: (i, g * gpw + h))
    res = lambda g: pl.BlockSpec((r, w), lambda h, i: (0, g * gpw + h))
    v_t = _key_tiles_transposed(name + "_vt", va, vg)
    ins = [(qa, tile(qg)), (ka, res(kg)), (v_t, pl.BlockSpec((nk, w, tk), lambda h, i: (0, h, 0)))]
    if bias:
        ins += [(cum_b, res(0)),
                (cum_t.reshape(HEADS, nq, 1, tq), pl.BlockSpec((hg, None, 1, tq), lambda h, i: (h, i, 0, 0)))]
    outs = [(_sds((r, HW), F32), tile(0)),
            (_sds((r, HEADS * V_DIM), MXU_DTYPE), pl.BlockSpec((tq, hg * V_DIM), lambda h, i: (i, h))),
            (_sds((HEADS, nq, 1, tq), F32), pl.BlockSpec((hg, None, 1, tq), lambda h, i: (h, i, 0, 0)))]
    o, ob, lse = _call(name, body, (gpw, nq), ins, outs, sem=("parallel", "parallel"))
    return o, ob, lse.reshape(HEADS, r)


def _key_tiles_transposed(name, a, group):
    r = a.shape[0]
    tk = ATT_TK

    def body(x_ref, o_ref):
        for h in range(HEADS):
            o_ref[_hs(h), :] = x_ref[:, _hs(h)].astype(F32).T.astype(o_ref.dtype)

    return _call(name, body, (r // tk,),
                 [(a, pl.BlockSpec((tk, HW), lambda j: (j, group)))],
                 [(_sds((r // tk, HW, tk), a.dtype), pl.BlockSpec((None, HW, tk), lambda j: (j, 0, 0)))],
                 sem=("parallel",))[0]


def _attn_bwd(name, q, k, v, do_b, o, lse_t, cum_b=None, cum_t=None, out_dtype=F32, after=()):
    (qa, qg), (ka, kg), (va, vg) = q, k, v
    r = qa.shape[0]
    tq, tk = ATT_TQ, ATT_TK
    nq, nk = r // tq, r // tk
    bias = cum_b is not None

    def body(*refs):
        if bias:
            (q_ref, k_ref, v_ref, do_ref, o_ref, lse_ref, cb_ref, ct_ref,
             dq_ref, dk_ref, dv_ref, dcq_ref, dck_ref, dqt_ref, dl_ref) = refs
        else:
            q_ref, k_ref, v_ref, do_ref, o_ref, lse_ref, dq_ref, dk_ref, dv_ref, dqt_ref, dl_ref = refs
        j = pl.program_id(1)

        @pl.when(j == 0)
        def _():
            dqt_ref[...] = jnp.zeros_like(dqt_ref)
            if bias:
                dcq_ref[...] = jnp.zeros_like(dcq_ref)
            for hh in range(hg):
                for i in range(nq):
                    rows = slice(i * tq, (i + 1) * tq)
                    prod = do_ref[rows, _hs(hh)].astype(F32) * o_ref[rows, _hs(hh)]
                    dl_ref[hh, i] = jnp.sum(prod.T, axis=0, keepdims=True)

        kts = [k_ref[:, _hs(hh)] for hh in range(hg)]
        vts = [v_ref[:, _hs(hh)] for hh in range(hg)]
        k_trs = [kt.astype(F32).T.astype(kt.dtype) for kt in kts]
        cks = [cb_ref[:, hh * HP:hh * HP + 1] for hh in range(hg)] if bias else None
        diff = lax.broadcasted_iota(jnp.int32, (tk, tq), 0) - lax.broadcasted_iota(jnp.int32, (tk, tq), 1)

        def step(i, carry, masked):
            rows = pl.ds(pl.multiple_of(i * tq, tq), tq)
            out = []
            for hh in range(hg):
                dk_acc, dv_acc, dck_acc = carry[hh]
                qt = q_ref[rows, _hs(hh)]
                dot = do_ref[rows, _hs(hh)]
                s = lax.dot_general(kts[hh], qt, NT, preferred_element_type=F32)
                if bias:
                    s = s + (ct_ref[hh, i] - cks[hh])
                if masked:
                    s = jnp.where(diff <= i * tq - j * tk, s, NEG_INF)
                p = jnp.exp(s - lse_ref[hh, i])
                dp = lax.dot_general(vts[hh], dot, NT, preferred_element_type=F32)
                ds = p * (dp - dl_ref[hh, i])
                pb = p.astype(dot.dtype)
                dsb = ds.astype(qt.dtype)
                dv_acc = dv_acc + jnp.dot(pb, dot, preferred_element_type=F32)
                dk_acc = dk_acc + jnp.dot(dsb, qt, preferred_element_type=F32)
                dqt_ref[hh, i] += jnp.dot(k_trs[hh], dsb, preferred_element_type=F32)
                if bias:
                    dcq_ref[hh, i] += jnp.sum(ds, axis=0, keepdims=True)
                    dck_acc = dck_acc - jnp.sum(ds, axis=1, keepdims=True)
                out.append((dk_acc, dv_acc, dck_acc))
            return tuple(out)

        i_first = (j * tk) // tq
        i_clear = jnp.minimum(((j + 1) * tk + tq - 2) // tq, nq)
        carry = tuple((jnp.zeros((tk, HP), F32), jnp.zeros((tk, HP), F32), jnp.zeros((tk, 1), F32)) for _ in range(hg))
        carry = lax.fori_loop(i_first, i_clear, lambda i, c: step(i, c, True), carry)
        carry = lax.fori_loop(i_clear, nq, lambda i, c: step(i, c, False), carry)
        for hh in range(hg):
            dk_acc, dv_acc, dck_acc = carry[hh]
            dk_ref[:, _hs(hh)] = dk_acc.astype(dk_ref.dtype)
            dv_ref[:, _hs(hh)] = dv_acc.astype(dv_ref.dtype)
            if bias:
                dck_ref[:, _hs(hh)] = jnp.broadcast_to(dck_acc, (tk, HP))

        @pl.when(j == nk - 1)
        def _():
            for hh in range(hg):
                for i in range(nq):
                    dq_ref[i * tq:(i + 1) * tq, _hs(hh)] = dqt_ref[hh, i].T.astype(dq_ref.dtype)

    hg = ATT_HEADS
    w = hg * HP
    gpw = HW // w
    res = lambda g: pl.BlockSpec((r, w), lambda h, j: (0, g * gpw + h))
    tile = lambda g: pl.BlockSpec((tk, w), lambda h, j: (j, g * gpw + h))
    rowv = pl.BlockSpec((hg, nq, 1, tq), lambda h, j: (h, 0, 0, 0))
    as_rows = lambda a: a.reshape(HEADS, nq, 1, tq)
    ins = [(qa, res(qg)), (ka, tile(kg)), (va, tile(vg)), (do_b, res(0)), (o, res(0)), (as_rows(lse_t), rowv)]
    outs = [(_sds((r, HW), out_dtype), res(0)), (_sds((r, HW), out_dtype), tile(0)), (_sds((r, HW), out_dtype), tile(0))]
    if bias:
        ins += [(cum_b, tile(0)), (as_rows(cum_t), rowv)]
        outs += [(_sds((HEADS, nq, 1, tq), F32), rowv), (_sds((r, HW), F32), tile(0))]
    res_out = _call(name, body, (gpw, nk), ins, outs,
                    scratch=[pltpu.VMEM((hg, nq, HP, tq), F32), pltpu.VMEM((hg, nq, 1, tq), F32)],
                    sem=("parallel", "arbitrary"), after=after)
    if bias:
        dq, dk, dv, dcq, dck = res_out
        return dq, dk, dv, dcq.reshape(HEADS, r), dck
    return res_out


MESH_ID = pl.DeviceIdType.MESH
ANY = pl.BlockSpec(memory_space=pl.ANY)


N_GATHER_COPIES = 8


def _allgather(name, shards):
    n = len(shards)

    def body(*refs):
        x_refs, out_refs = refs[:n], refs[n:2 * n]
        send_sems, recv_sems, local_sems = refs[2 * n:]
        x, y, c = lax.axis_index("x"), lax.axis_index("y"), lax.axis_index("c")
        me, sibling = (x, y, c), (x, y, 1 - c)
        xn, yn, dg = (1 - x, y, c), (x, 1 - y, c), (1 - x, 1 - y, c)
        other = lambda dev: (dev[0], dev[1], 1 - c)

        def slot(ti, dev, half=None):
            ref = out_refs[ti].at[4 * dev[0] + 2 * dev[1] + dev[2]]
            if half is None:
                return ref
            rows = shards[ti].shape[0] // 2
            return ref.at[pl.ds(half * rows, rows)]

        def copy(ti, k, block, to, half=None, src=None):
            return pltpu.make_async_remote_copy(
                src_ref=slot(ti, block, half) if src is None else src, dst_ref=slot(ti, block, half),
                send_sem=send_sems.at[ti, k], recv_sem=recv_sems.at[ti, k], device_id=to, device_id_type=MESH_ID)

        mine = [pltpu.make_async_copy(x_refs[ti], slot(ti, me), local_sems.at[ti]) for ti in range(n)]
        for cp in mine:
            cp.start()
        started = []

        def go(cp):
            cp.start()
            started.append(cp)

        for ti in range(n):
            go(copy(ti, 0, me, sibling, src=x_refs[ti]))
            go(copy(ti, 1, me, xn, src=x_refs[ti]))
            go(copy(ti, 2, me, yn, src=x_refs[ti]))
        for ti in range(n):
            copy(ti, 1, xn, me).wait_recv()
            go(copy(ti, 3, xn, yn, half=0))
            go(copy(ti, 5, xn, sibling))
            copy(ti, 2, yn, me).wait_recv()
            go(copy(ti, 4, yn, xn, half=1))
            go(copy(ti, 6, yn, sibling))
        for ti in range(n):
            copy(ti, 3, dg, me, half=0).wait_recv()
            copy(ti, 4, dg, me, half=1).wait_recv()
            go(copy(ti, 7, dg, sibling))
        for ti in range(n):
            copy(ti, 0, sibling, me).wait_recv()
            for k, dev in ((5, xn), (6, yn), (7, dg)):
                copy(ti, k, other(dev), me).wait_recv()
        for cp in started:
            cp.wait_send()
        for cp in mine:
            cp.wait()

    sems = pltpu.SemaphoreType.DMA((n, N_GATHER_COPIES))
    return pl.pallas_call(
        body, name=name, out_shape=[_sds((N_DEV,) + s.shape, s.dtype) for s in shards],
        in_specs=[ANY] * n, out_specs=[ANY] * n,
        scratch_shapes=[sems, sems, pltpu.SemaphoreType.DMA((n,))],
    )(*shards)


HBM = pl.BlockSpec(memory_space=pltpu.HBM)
SEM = pl.BlockSpec(memory_space=pltpu.SEMAPHORE)
EFFECT = pltpu.SideEffectType.DATAFLOW_SIDE_EFFECTING
N_PEER = N_DEV - 1


def _my_id():
    return 4 * lax.axis_index("x") + 2 * lax.axis_index("y") + lax.axis_index("c")


def _peers():
    x, y, c = lax.axis_index("x"), lax.axis_index("y"), lax.axis_index("c")
    out = []
    for k in range(1, N_DEV):
        px, py, pc = (1 - x if k & 4 else x, 1 - y if k & 2 else y, 1 - c if k & 1 else c)
        out.append(((px, py, pc), 4 * px + 2 * py + pc))
    return out


def _push_copies(src_refs, land_refs, send_sems, recv_sems, scatter, landing):
    me = _my_id()
    out = []
    for ti, (src, land) in enumerate(zip(src_refs, land_refs)):
        for k, (dev, pid) in enumerate(_peers()):
            out.append(pltpu.make_async_remote_copy(
                src_ref=src.at[pid] if scatter else src, dst_ref=land.at[pid if landing else me],
                send_sem=send_sems.at[ti * N_PEER + k], recv_sem=recv_sems.at[ti * N_PEER + k],
                device_id=dev, device_id_type=MESH_ID))
    return out


def _push_start(name, groups, scatter, after=None):
    sizes = [len(g) for g in groups]
    srcs = [a for g in groups for a in g]
    n = len(srcs)
    slot = lambda s: s.shape[1:] if scatter else s.shape
    lands = [lax.empty((N_DEV,) + slot(s), s.dtype) for s in srcs]
    n_after = 0 if after is None else 1
    n_grp = len(groups)

    def body(*refs):
        src_refs, land_refs = refs[:n], refs[n:2 * n]
        sems = refs[2 * n + n_after:2 * n + n_after + 2 * n_grp]
        token = refs[-1]
        lo = 0
        for gi, sz in enumerate(sizes):
            for cp in _push_copies(src_refs[lo:lo + sz], land_refs[lo:lo + sz], sems[2 * gi], sems[2 * gi + 1], scatter, False):
                cp.start()
            lo += sz
        token[...] = jnp.zeros_like(token)

    hbm = lambda a: pltpu.with_memory_space_constraint(a, pltpu.HBM)
    operands = [hbm(a) for a in srcs + lands] + ([after] if n_after else [])
    sem_shapes = [pltpu.SemaphoreType.DMA((sz * N_PEER,)) for sz in sizes for _ in range(2)]
    res = pl.pallas_call(
        body, name=name,
        out_shape=sem_shapes + [pltpu.HBM(a.shape, a.dtype) for a in srcs + lands] + [_sds((8, 128), F32)],
        in_specs=[HBM] * (2 * n) + [ANY] * n_after,
        out_specs=[SEM] * (2 * n_grp) + [HBM] * (2 * n) + [pl.BlockSpec(memory_space=pltpu.VMEM)],
        input_output_aliases={i: 2 * n_grp + i for i in range(2 * n)},
        compiler_params=pltpu.CompilerParams(has_side_effects=EFFECT),
    )(*operands)
    thru = res[2 * n_grp:2 * n_grp + 2 * n]
    handles, lo = [], 0
    for gi, sz in enumerate(sizes):
        handles.append((res[2 * gi], res[2 * gi + 1], list(thru[lo:lo + sz]), list(thru[n + lo:n + lo + sz]), scatter))
        lo += sz
    return handles, res[-1]


def _push_wait(name, handle, after):
    send_sems, recv_sems, srcs, lands, scatter = handle
    n = len(srcs)

    def body(*refs):
        src_refs, land_refs = refs[:n], refs[n:2 * n]
        s_sems, r_sems = refs[2 * n], refs[2 * n + 1]
        for cp in _push_copies(src_refs, land_refs, s_sems, r_sems, scatter, True):
            cp.wait_send()
            cp.wait_recv()

    res = pl.pallas_call(
        body, name=name,
        out_shape=[pltpu.HBM(a.shape, a.dtype) for a in srcs + lands],
        in_specs=[HBM] * (2 * n) + [SEM, SEM, ANY], out_specs=[HBM] * (2 * n),
        input_output_aliases={i: i for i in range(2 * n)},
        compiler_params=pltpu.CompilerParams(has_side_effects=EFFECT),
    )(*srcs, *lands, send_sems, recv_sems, after)
    return list(res[n:])


def _adamw(name, parts, w, m, v, own=None):
    r, c = w.shape
    br = _pick(r, COPY_ROWS, 16)
    has_own = own is not None

    def body(*refs):
        if has_own:
            p_ref, own_ref, w_ref, m_ref, v_ref, g_ref, d_ref, nm_ref, nv_ref = refs
            me = _my_id()
            mine = own_ref[...].astype(F32)
        else:
            p_ref, w_ref, m_ref, v_ref, g_ref, d_ref, nm_ref, nv_ref = refs
        g = None
        for k in range(N_DEV):
            t = p_ref[k].astype(F32)
            if has_own:
                t = jnp.where(me == k, mine, t)
            g = t if g is None else g + t
        mm = ADAM_B1 * m_ref[...] + (1.0 - ADAM_B1) * g
        vv = ADAM_B2 * v_ref[...] + (1.0 - ADAM_B2) * (g * g)
        m_hat = mm / (1.0 - ADAM_B1 ** ADAM_STEP)
        v_hat = vv / (1.0 - ADAM_B2 ** ADAM_STEP)
        g_ref[...] = g
        d_ref[...] = -ADAM_LR * (m_hat / (jnp.sqrt(v_hat) + ADAM_EPS) + ADAM_WD * w_ref[...])
        nm_ref[...] = mm
        nv_ref[...] = vv

    spec = _rows(br, c)
    out = (_sds((r, c), F32), spec)
    ins = [(parts, pl.BlockSpec((N_DEV, br, c), lambda i: (0, i, 0)))] + ([(own, spec)] if has_own else [])
    return _call(name, body, (r // br,), ins + [(w, spec), (m, spec), (v, spec)], [out] * 4, sem=("parallel",))


def _pad_head_cols(w, d):
    k = w.shape[0]
    return jnp.pad(w.reshape(k, HEADS, d), ((0, 0), (0, 0), (0, HP - d))).reshape(k, HW)


def _unpad_head_cols(wp, d):
    k = wp.shape[0]
    return wp.reshape(k, HEADS, HP)[:, :, :d].reshape(k, HEADS * d)


def _pad_head_rows(w, d):
    n = w.shape[1]
    return jnp.pad(w.reshape(HEADS, d, n), ((0, 0), (0, HP - d), (0, 0))).reshape(HW, n)


def _w_in_runs():
    nat = {}
    o = 0
    for nm, wd in (("q", Q_RANK), ("kv", KV_RANK), ("kr", ROPE), ("fq", FOX_W), ("fk", FOX_W), ("fv", FOX_W),
                   ("fl", HEADS), ("gate", 2 * D_MODEL)):
        nat[nm] = o
        o += wd
    runs = [(1, R_QLAT, nat["q"], Q_RANK, 1.0), (1, R_KVLAT, nat["kv"], KV_RANK, 1.0),
            (1, R_LAST + LANE_FL, nat["fl"], HEADS, 1.0), (1, R_LAST + LANE_PE, nat["kr"], ROPE, 1.0),
            (1, R_GATE, nat["gate"], 2 * D_MODEL, 1.0)]
    for grp, (nm, sc) in enumerate((("fq", FOX_SCALE), ("fk", 1.0), ("fv", 1.0))):
        runs.append((0, grp * FOX_W, nat[nm], FOX_W, sc))
    return runs


def _head_pad_moves(pad):
    moves = []
    for grp in range(3):
        for h in range(HEADS):
            narrow, wide = grp * FOX_W + h * FOX_DIM, h * HP
            if pad:
                moves.append((0, None, grp * HW + wide, 0, None, narrow, FOX_DIM, 1.0))
            else:
                moves.append((0, None, narrow, grp, None, wide, FOX_DIM, 1.0))
    return moves


def _sharded_runs(runs, shard_cols):
    out = []
    for half, col, ncol, width, sc in runs:
        while width > 0:
            d, local = divmod(ncol, shard_cols)
            wd = min(width, shard_cols - local)
            out.append((half, col, d, local, wd, sc))
            col, ncol, width = col + wd, ncol + wd, width - wd
    return out


def _remap(name, srcs, out_shapes, moves):
    rows = srcs[0].shape[-2]
    br = _pick(rows, COPY_ROWS, 16)
    ns = len(srcs)

    def spec(shape):
        if len(shape) == 2:
            return pl.BlockSpec((br, shape[1]), lambda i: (i, 0))
        return pl.BlockSpec((shape[0], br, shape[2]), lambda i: (0, i, 0))

    covered = [sum(m[6] for m in moves if m[0] == di) for di in range(len(out_shapes))]
    has_gaps = [cov < (shape[1] if len(shape) == 2 else shape[0] * shape[2])
                for cov, (shape, _) in zip(covered, out_shapes)]

    def body(*refs):
        s_refs, o_refs = refs[:ns], refs[ns:]
        for o, gaps in zip(o_refs, has_gaps):
            if gaps:
                o[...] = jnp.zeros_like(o)
        for di, dl, dc, si, sl, sc0, wd, scale in moves:
            v = s_refs[si][:, sc0:sc0 + wd] if sl is None else s_refs[si][sl, :, sc0:sc0 + wd]
            if scale != 1.0:
                v = v * jnp.asarray(scale, v.dtype)
            v = v.astype(o_refs[di].dtype)
            if dl is None:
                o_refs[di][:, dc:dc + wd] = v
            else:
                o_refs[di][dl, :, dc:dc + wd] = v

    return _call(name, body, (rows // br,), [(a, spec(a.shape)) for a in srcs],
                 [(_sds(shape, dt), spec(shape)) for shape, dt in out_shapes], sem=("parallel",))


def _w_in_from_shards(g3):
    n, rows, c = g3.shape
    moves = [(half, None, col, 0, d, local, wd, sc) for half, col, d, local, wd, sc in _sharded_runs(_w_in_runs(), c)]
    return _remap("w_in_repack", [g3], [((rows, F_W), g3.dtype), ((rows, R_W), g3.dtype)], moves)


def _w_in_grad_to_shards(d_fused, d_rest, n, c):
    rows = d_fused.shape[0]
    moves = [(0, d, local, half, None, col, wd, sc) for half, col, d, local, wd, sc in _sharded_runs(_w_in_runs(), c)]
    return _remap("w_in_grad_unpack", [d_fused, d_rest], [((n, rows, c), d_fused.dtype)], moves)[0]


def _rows_from_shards(name, land, own):
    n, rows, c = land.shape

    def body(land_ref, own_ref, o_ref):
        o_ref[...] = jnp.where(_my_id() == pl.program_id(0), own_ref[...], land_ref[...])

    return _call(name, body, (n,),
                 [(land, pl.BlockSpec((None, rows, c), lambda d: (d, 0, 0))), (own, _whole((rows, c)))],
                 [(_sds((n * rows, c), land.dtype), pl.BlockSpec((rows, c), lambda d: (d, 0)))], sem=("parallel",))[0]


def _cols_from_shards(name, land, own):
    n, rows, c = land.shape
    br = _pick(rows, COPY_ROWS, 16)

    def body(land_ref, own_ref, o_ref):
        me = _my_id()
        for d in range(n):
            o_ref[:, c * d:c * (d + 1)] = jnp.where(me == d, own_ref[...], land_ref[d])

    return _call(name, body, (rows // br,),
                 [(land, pl.BlockSpec((n, br, c), lambda i: (0, i, 0))), (own, _rows(br, c))],
                 [(_sds((rows, n * c), land.dtype), _rows(br, n * c))], sem=("parallel",))[0]


def _cols_to_shards(name, full, n):
    rows, nc = full.shape
    c = nc // n
    return _remap(name, [full], [((n, rows, c), full.dtype)], [(0, d, 0, 0, None, c * d, c, 1.0) for d in range(n)])[0]


class _NoComm:
    first_token = ()

    def late_weights(self, group, after):
        return {}

    def send(self, name, grads):
        return ()


def _local_step(x, tgt, p, comm=_NoComm()):
    seq = x.shape[0]
    r = -(-(N_META + seq) // ROW_ALIGN) * ROW_ALIGN
    cd = MXU_DTYPE
    p = dict(p)

    w_f, w_r = p["w_in"]

    pos = jnp.arange(r, dtype=F32)
    inv_freq = ROPE_THETA ** (-jnp.arange(HALF, dtype=F32) / HALF)
    ang = pos[:, None] * inv_freq[None, :]
    cos_t = jnp.tile(jnp.cos(ang), (1, HP // HALF))
    sin_t = jnp.tile(jnp.sin(ang), (1, HP // HALF))
    bf_row = jnp.zeros((1, HP), F32).at[0, LANE_FL:LANE_FL + HEADS].set(p["b_forget"])

    h0, h0b = _ln_emb_fwd(x, p["meta_tokens"], p["ln_emb_g"], p["ln_emb_b"], r, after=comm.first_token)
    proj_f = _matmul("in_proj_f", h0b, w_f, out_dtype=cd)
    proj_f = _remap("proj_f_pad", [proj_f], [((r, 3 * HW), cd)], _head_pad_moves(True))[0]
    proj_r = _matmul("in_proj_r", h0b, w_r)
    latent_gains = (p["q_norm_g"], p["kv_norm_g"])
    ql, kvl = _latent_norm_fwd(proj_r, latent_gains)
    p.update(comm.late_weights("qkv", ql))
    w_q = _pad_head_cols(p["w_q_up"], QK_DIM)
    w_kv = p["w_kv_up"]
    q_raw = _matmul("q_up", ql, w_q)
    kv = _matmul("kv_up", kvl, w_kv, out_dtype=cd)
    q_mla, k_mla, v_mla = _rope_fwd(q_raw, kv, proj_r, cos_t, sin_t)
    o_mla, o_mla_b, lse_mla = _attn_fwd("mla_fwd", (q_mla, 0), (k_mla, 0), (v_mla, 0))

    cum, cum_t = _forget_fwd(proj_r, bf_row)
    o_fox, o_fox_b, lse_fox = _attn_fwd("fox_fwd", (proj_f, 0), (proj_f, 1), (proj_f, 2), cum, cum_t)

    p.update(comm.late_weights("mix", o_fox_b))
    w_bm = _pad_head_rows(p["w_branch_mla"], V_DIM)
    w_bf = _pad_head_rows(p["w_branch_fox"], FOX_DIM)
    bm = _matmul("branch_mla", o_mla_b, p["w_branch_mla"], out_dtype=cd)
    bfx = _matmul("branch_fox", o_fox_b, p["w_branch_fox"], out_dtype=cd)
    merged = _gate_fwd(proj_r, p["b_gate"], bm, bfx)
    mix = _matmul("out_proj", merged, p["w_out"])
    h1, h1b = _ln_fwd("ln_mix_fwd", h0, mix, p["ln_mix_g"], p["ln_mix_b"])
    p.update(comm.late_weights("ffn", h1b))
    up = _matmul("ffn_up", h1b, p["w_ffn_up"], out_dtype=cd)
    act = _glu_fwd(up, p["conv_w"], p["conv_b"])
    f = _matmul("ffn_down", act, p["w_ffn_down"])
    loss = _ln_ffn_loss(h1, f, tgt, p["ln_ffn_g"], p["ln_ffn_b"])

    g = {}
    dz2, dz2b, g["ln_ffn_g"], g["ln_ffn_b"] = _ln_ffn_bwd(h1, f, tgt, p["ln_ffn_g"], p["ln_ffn_b"])
    d_act = _matmul("ffn_down_dx", dz2b, p["w_ffn_down"], tb=True, out_dtype=cd)
    g["w_ffn_down"] = _matmul("ffn_down_dw", act, dz2b, ta=True, out_dtype=cd)
    d_up, dcw, g["conv_b"] = _glu_bwd(up, p["conv_w"], p["conv_b"], d_act)
    g["conv_w"] = dcw[:3]
    dh1 = _matmul("ffn_up_dx", d_up, p["w_ffn_up"], tb=True, addend=dz2, alpha=ALPHA)
    g["w_ffn_up"] = _matmul("ffn_up_dw", h1b, d_up, ta=True, out_dtype=cd)
    sent = comm.send("ffn", {n: g[n] for n in ("w_ffn_down", "w_ffn_up", "conv_w")})
    dz1, dz1b, g["ln_mix_g"], g["ln_mix_b"] = _ln_bwd("ln_mix_bwd", h0, mix, dh1, p["ln_mix_g"], after=sent)
    dmerged = _matmul("out_proj_dx", dz1b, p["w_out"], tb=True, out_dtype=cd)
    g["w_out"] = _matmul("out_proj_dw", merged, dz1b, ta=True, out_dtype=cd)
    d_bm, d_bf, d_gl, g["b_gate"] = _gate_bwd(proj_r, p["b_gate"], bm, bfx, dmerged)
    do_mla_b = _matmul("branch_mla_dx", d_bm, w_bm, tb=True, out_dtype=cd)
    g["w_branch_mla"] = _matmul("branch_mla_dw", o_mla_b, d_bm, ta=True, out_dtype=cd)
    do_fox_b = _matmul("branch_fox_dx", d_bf, w_bf, tb=True, out_dtype=cd)
    g["w_branch_fox"] = _matmul("branch_fox_dw", o_fox_b, d_bf, ta=True, out_dtype=cd)

    sent = comm.send("mix", {n: g[n] for n in ("w_out", "w_branch_mla", "w_branch_fox")})
    dq_m, dk_m, dv_m = _attn_bwd("mla_bwd", (q_mla, 0), (k_mla, 0), (v_mla, 0), do_mla_b, o_mla, lse_mla, after=sent)
    dfq, dfk, dfv, dcq, dck = _attn_bwd("fox_bwd", (proj_f, 0), (proj_f, 1), (proj_f, 2), do_fox_b, o_fox, lse_fox,
                                        cum, cum_t, out_dtype=cd)
    dfl, dbf = _forget_bwd(proj_r, bf_row, dcq, dck)
    g["b_forget"] = dbf[:, LANE_FL:LANE_FL + HEADS]

    dq_b, dkv_b, dlast = _rope_bwd(dq_m, dk_m, dv_m, dfl, cos_t, sin_t)
    d_ql = _matmul("q_up_dx", dq_b, w_q, tb=True)
    d_kvl = _matmul("kv_up_dx", dkv_b, w_kv, tb=True)
    d_qlat, d_kvlat, g["q_norm_g"], g["kv_norm_g"] = _latent_norm_bwd(proj_r, (d_ql, d_kvl), latent_gains)
    side_by_side = lambda parts, cols: [(0, None, c0, si, None, 0, a.shape[1], 1.0) for si, (a, c0) in enumerate(zip(parts, cols))]
    dproj_f = _remap("dproj_f_pack", [dfq, dfk, dfv], [((r, F_W), cd)], _head_pad_moves(False))[0]
    rest_parts = [d_qlat, d_kvlat, dlast, d_gl]
    dproj_r = _remap("dproj_r_pack", rest_parts, [((r, R_W), cd)],
                     side_by_side(rest_parts, (R_QLAT, R_KVLAT, R_LAST, R_GATE)))[0]
    g["w_in"] = (_matmul("in_proj_f_dw", h0b, dproj_f, ta=True, out_dtype=cd),
                 _matmul("in_proj_r_dw", h0b, dproj_r, ta=True, out_dtype=cd))
    sent = comm.send("in", {"w_in": g["w_in"]})
    dh0 = _matmul("in_proj_f_dx", dproj_f, w_f, tb=True, addend=dz1, alpha=ALPHA, after=sent)
    g["w_q_up"] = _unpad_head_cols(_matmul("q_up_dw", ql, dq_b, ta=True, out_dtype=cd, after=sent), QK_DIM)
    g["w_kv_up"] = _matmul("kv_up_dw", kvl, dkv_b, ta=True, out_dtype=cd, after=sent)
    sent = comm.send("qkv", {n: g[n] for n in ("w_q_up", "w_kv_up")})
    dh0 = _matmul("in_proj_r_dx", dproj_r, w_r, tb=True, addend=dh0, after=sent)
    grad_x, d_meta, g["ln_emb_g"], g["ln_emb_b"] = _ln_emb_bwd(x, p["meta_tokens"], dh0, p["ln_emb_g"])
    return loss, grad_x, d_meta, g


BIG = (("w_in", 1), ("w_q_up", 1), ("w_kv_up", 1), ("w_branch_mla", 1), ("w_branch_fox", 1), ("w_out", 0),
       ("w_ffn_up", 1), ("w_ffn_down", 0))
SMALL_SHARDED = (("meta_tokens", 1), ("conv_w", 1))
EARLY = ("w_in", "meta_tokens")
LATE = {"qkv": ("w_q_up", "w_kv_up", "conv_w"),
        "mix": ("w_branch_mla", "w_branch_fox", "w_out"),
        "ffn": ("w_ffn_up", "w_ffn_down")}
REPLICATED = ("ln_emb_g", "ln_emb_b", "b_gate", "b_forget", "q_norm_g", "kv_norm_g", "ln_mix_g", "ln_mix_b",
              "conv_b", "ln_ffn_g", "ln_ffn_b")
PACK_COLS = 1024


def _pack(flat_list):
    cat = jnp.concatenate(flat_list)
    n = cat.shape[0]
    rows = -(-n // (8 * PACK_COLS)) * 8
    return jnp.pad(cat, (0, rows * PACK_COLS - n)).reshape(rows, PACK_COLS)


def _gathered_full(g3, axis):
    n, r, c = g3.shape
    if axis == 0:
        return g3.reshape(n * r, c)
    return g3.transpose(1, 0, 2).reshape(r, n * c)


def _shard_major(full, axis):
    r, c = full.shape
    if axis == 0:
        return full.reshape(N_DEV, r // N_DEV, c)
    return full.reshape(r, N_DEV, c // N_DEV).transpose(1, 0, 2)


def kernel(x, meta_tokens, ln_emb_g, ln_emb_b, w_in, b_gate, b_forget, q_norm_g, w_q_up, kv_norm_g, w_kv_up, w_branch_mla, w_branch_fox, w_out, ln_mix_g, ln_mix_b, w_ffn_up, conv_w, conv_b, w_ffn_down, ln_ffn_g, ln_ffn_b, loss_target, m_meta_tokens, m_ln_emb_g, m_ln_emb_b, m_w_in, m_b_gate, m_b_forget, m_q_norm_g, m_w_q_up, m_kv_norm_g, m_w_kv_up, m_w_branch_mla, m_w_branch_fox, m_w_out, m_ln_mix_g, m_ln_mix_b, m_w_ffn_up, m_conv_w, m_conv_b, m_w_ffn_down, m_ln_ffn_g, m_ln_ffn_b, v_meta_tokens, v_ln_emb_g, v_ln_emb_b, v_w_in, v_b_gate, v_b_forget, v_q_norm_g, v_w_q_up, v_kv_norm_g, v_w_kv_up, v_w_branch_mla, v_w_branch_fox, v_w_out, v_ln_mix_g, v_ln_mix_b, v_w_ffn_up, v_conv_w, v_conv_b, v_w_ffn_down, v_ln_ffn_g, v_ln_ffn_b):
    names = ("meta_tokens", "ln_emb_g", "ln_emb_b", "w_in", "b_gate", "b_forget", "q_norm_g", "w_q_up", "kv_norm_g",
             "w_kv_up", "w_branch_mla", "w_branch_fox", "w_out", "ln_mix_g", "ln_mix_b", "w_ffn_up", "conv_w", "conv_b",
             "w_ffn_down", "ln_ffn_g", "ln_ffn_b")
    w_args = (meta_tokens, ln_emb_g, ln_emb_b, w_in, b_gate, b_forget, q_norm_g, w_q_up, kv_norm_g, w_kv_up,
              w_branch_mla, w_branch_fox, w_out, ln_mix_g, ln_mix_b, w_ffn_up, conv_w, conv_b, w_ffn_down, ln_ffn_g, ln_ffn_b)
    m_args = (m_meta_tokens, m_ln_emb_g, m_ln_emb_b, m_w_in, m_b_gate, m_b_forget, m_q_norm_g, m_w_q_up, m_kv_norm_g,
              m_w_kv_up, m_w_branch_mla, m_w_branch_fox, m_w_out, m_ln_mix_g, m_ln_mix_b, m_w_ffn_up, m_conv_w, m_conv_b,
              m_w_ffn_down, m_ln_ffn_g, m_ln_ffn_b)
    v_args = (v_meta_tokens, v_ln_emb_g, v_ln_emb_b, v_w_in, v_b_gate, v_b_forget, v_q_norm_g, v_w_q_up, v_kv_norm_g,
              v_w_kv_up, v_w_branch_mla, v_w_branch_fox, v_w_out, v_ln_mix_g, v_ln_mix_b, v_w_ffn_up, v_conv_w, v_conv_b,
              v_w_ffn_down, v_ln_ffn_g, v_ln_ffn_b)
    as2d = lambda a: a.reshape((-1, a.shape[-1])) if a.ndim != 1 else a.reshape(1, -1)
    w = {n: as2d(a) for n, a in zip(names, w_args)}
    m = {n: as2d(a) for n, a in zip(names, m_args)}
    v = {n: as2d(a) for n, a in zip(names, v_args)}
    out_shape = {n: a.shape for n, a in zip(names, w_args)}

    axis_of = dict(BIG + SMALL_SHARDED)
    big = set(n for n, _ in BIG)
    wire = lambda n, a: a.astype(MXU_DTYPE) if n in big else a
    my_id = _my_id()

    early = _allgather("gather_early", [wire(n, w[n]) for n in EARLY])
    p = {n: _gathered_full(g3, axis_of[n]) for n, g3 in zip(EARLY, early) if n != "w_in"}
    p["w_in"] = _w_in_from_shards(early[EARLY.index("w_in")])
    for n in REPLICATED:
        p[n] = w[n].reshape(-1)
    late_src = [[wire(n, w[n]) for n in members] for members in LATE.values()]
    late_handles, late_token = _push_start("gather_late_start", late_src, False, after=early[0])
    late = {group: (members, src, handle)
            for (group, members), src, handle in zip(LATE.items(), late_src, late_handles)}
    sent = {}

    class Comm:
        first_token = (late_token,)

        def late_weights(self, group, after):
            members, src, handle = late[group]
            lands = _push_wait("gather_" + group + "_wait", handle, after)
            out = {}
            for n, own, land in zip(members, src, lands):
                if own.shape[0] % 16:
                    out[n] = _gathered_full(lax.dynamic_update_index_in_dim(land, own, my_id, 0), axis_of[n])
                elif axis_of[n] == 1:
                    out[n] = _cols_from_shards(n + "_repack", land, own)
                else:
                    out[n] = _rows_from_shards(n + "_repack", land, own)
            return out

        def send(self, name, grads):
            names_ = tuple(grads)
            parts = []
            for n in names_:
                if n == "w_in":
                    parts.append(_w_in_grad_to_shards(*grads[n], N_DEV, w[n].shape[1]))
                elif n == "w_ffn_up":
                    parts.append(_cols_to_shards(n + "_grad_unpack", grads[n], N_DEV))
                else:
                    parts.append(_shard_major(grads[n], axis_of[n]).astype(MXU_DTYPE))
            (handle,), token = _push_start("send_" + name + "_start", [parts], True)
            sent[name] = (names_, parts, handle)
            return (token,)

    loss_part, grad_x, d_meta, g = _local_step(x[0], loss_target[0], p, Comm())
    grad_x = grad_x[None]

    small = _pack([d_meta.reshape(-1)] + [g[n].reshape(-1) for n in REPLICATED] + [loss_part.reshape(-1)])
    (small_handle,), small_token = _push_start("send_small_start", [[small]], False)

    res = {}
    prev = small_token
    for name, (names_, parts, handle) in sent.items():
        lands = _push_wait("send_" + name + "_wait", handle, prev)
        for n, part, land in zip(names_, parts, lands):
            own = lax.dynamic_index_in_dim(part, my_id, axis=0, keepdims=False)
            res[n] = _adamw("adamw_" + n, land, w[n], m[n], v[n], own=own)
            prev = res[n][0]
    small_all = _push_wait("send_small_wait", small_handle, prev)[0]
    head = jnp.zeros((d_meta.size,), F32)
    rep_w = _pack([head] + [w[n].reshape(-1) for n in REPLICATED])
    rep_m = _pack([head] + [m[n].reshape(-1) for n in REPLICATED])
    rep_v = _pack([head] + [v[n].reshape(-1) for n in REPLICATED])
    rep_res = _adamw("adamw_replicated", small_all, rep_w, rep_m, rep_v, own=small)
    off = d_meta.size
    for n in REPLICATED:
        sz = w[n].size
        res[n] = tuple(a.reshape(-1)[off:off + sz] for a in rep_res)
        off += sz
    loss = rep_res[0].reshape(-1)[off]
    cols = w["meta_tokens"].shape[1]
    meta_rows = lambda a: a.reshape(a.shape[:-2] + (-1,))[..., :d_meta.size].reshape(a.shape[:-2] + d_meta.shape)
    my_cols = lambda a: lax.dynamic_slice_in_dim(a, my_id * cols, cols, axis=a.ndim - 1)
    res["meta_tokens"] = _adamw("adamw_meta_tokens", my_cols(meta_rows(small_all)), w["meta_tokens"],
                                m["meta_tokens"], v["meta_tokens"], own=my_cols(d_meta))

    outs = [loss, grad_x]
    for idx in range(4):
        outs += [res[n][idx].reshape(out_shape[n]) for n in names]
    return tuple(outs)
```

```python
import jax
import jax.numpy as jnp
from jax import lax
from jax.experimental import pallas as pl
from jax.experimental.pallas import tpu as pltpu

F32 = jnp.float32
BF16 = jnp.bfloat16
MXU_DTYPE = BF16

N_DEV = 8
N_META = 16
D_MODEL = 1024
HEADS = 8
Q_RANK = 384
KV_RANK = 128
NOPE = 64
ROPE = 32
HALF = ROPE // 2
QK_DIM = NOPE + ROPE
V_DIM = 64
FOX_DIM = 64
FOX_W = HEADS * FOX_DIM
D_FF = 2816
ROPE_THETA = 10000.0
LN_EPS = 1e-5
RMS_EPS = 1e-6
ALPHA = 2.0 ** 0.25
MLA_SCALE = QK_DIM ** -0.5
FOX_SCALE = FOX_DIM ** -0.5
NEG_INF = -1e30

HP = 128
HW = HEADS * HP
F_W = 3 * FOX_W
R_GATE = 0
R_KVLAT = R_GATE + 2 * D_MODEL
R_LAST = R_KVLAT + KV_RANK
R_QLAT = R_LAST + HP
R_W = R_QLAT + Q_RANK
assert R_QLAT % Q_RANK == 0 and R_KVLAT % KV_RANK == 0 and R_GATE % D_MODEL == 0 and R_W % HP == 0
LANE_FL = 0
LANE_PE = NOPE

ADAM_LR = 0.001
ADAM_B1 = 0.9
ADAM_B2 = 0.999
ADAM_EPS = 1e-08
ADAM_WD = 0.01
ADAM_STEP = 10

ROW_BLOCK = 384
TOKEN_BLOCK = 256
ATT_TQ = 768
ATT_TK = 768
ATT_HEADS = 2
ATT_HEADS_FWD = 4
ROW_ALIGN = 768
MM_BLOCK_CAP = 1408
VMEM_LIMIT = 56 * 1024 * 1024
HIGHEST = lax.Precision.HIGHEST
NT = (((1,), (1,)), ((), ()))
TN = (((0,), (0,)), ((), ()))


def _params(sem=None):
    return pltpu.CompilerParams(dimension_semantics=sem, vmem_limit_bytes=VMEM_LIMIT)


def _call(name, body, grid, ins, outs, scratch=(), sem=None, after=()):
    n_in = len(ins)
    n_tok = len(after)

    def run(*refs):
        body(*refs[:n_in], *refs[n_in + n_tok:])

    tok_spec = pl.BlockSpec((8, 128), lambda *_: (0, 0))
    return pl.pallas_call(
        run, name=name, grid=grid,
        in_specs=[s for _, s in ins] + [tok_spec] * n_tok,
        out_specs=[s for _, s in outs],
        out_shape=[o for o, _ in outs],
        scratch_shapes=list(scratch),
        compiler_params=_params(sem),
    )(*[a for a, _ in ins], *after)


def _sds(shape, dtype):
    return jax.ShapeDtypeStruct(shape, dtype)


def _rows(br, c, cb=0):
    return pl.BlockSpec((br, c), lambda i: (i, cb))


def _whole(shape):
    n = len(shape)
    return pl.BlockSpec(shape, lambda i: (0,) * n)


def _pick(dim, cap, mult):
    best = None
    d = mult
    while d <= min(dim, cap):
        if dim % d == 0:
            best = d
        d += mult
    return best if best is not None else dim


def _hs(h):
    return slice(h * HP, (h + 1) * HP)


def _matmul(name, a, b, *, ta=False, tb=False, out_dtype=F32, addend=None, alpha=1.0, after=()):
    if ta:
        k, m = a.shape
    else:
        m, k = a.shape
    if tb:
        n, k2 = b.shape
    else:
        k2, n = b.shape
    assert k == k2, (name, a.shape, b.shape)
    bm = _pick(m, MM_BLOCK_CAP, 128 if ta else 16)
    bn = _pick(n, MM_BLOCK_CAP, 128)
    bk = _pick(k, MM_BLOCK_CAP, 128 if (not ta or tb) else 16)
    nk = k // bk
    dims = (((0 if ta else 1,), (1 if tb else 0,)), ((), ()))
    has_add = addend is not None

    def body(*refs):
        a_ref, b_ref = refs[:2]
        add_ref = refs[2] if has_add else None
        o_ref = refs[3 if has_add else 2]

        def finish(r):
            if has_add:
                r = r + alpha * add_ref[...]
            o_ref[...] = r.astype(o_ref.dtype)

        part = lax.dot_general(a_ref[...], b_ref[...], dims, preferred_element_type=F32)
        if nk == 1:
            finish(part)
            return
        acc_ref = refs[-1]
        kk = pl.program_id(2)

        @pl.when(kk == 0)
        def _():
            acc_ref[...] = part

        @pl.when(kk > 0)
        def _():
            acc_ref[...] += part

        @pl.when(kk == nk - 1)
        def _():
            finish(acc_ref[...])

    a_spec = pl.BlockSpec((bk, bm), lambda i, j, l: (l, i)) if ta else pl.BlockSpec((bm, bk), lambda i, j, l: (i, l))
    b_spec = pl.BlockSpec((bn, bk), lambda i, j, l: (j, l)) if tb else pl.BlockSpec((bk, bn), lambda i, j, l: (l, j))
    o_spec = pl.BlockSpec((bm, bn), lambda i, j, l: (i, j))
    ins = [(a, a_spec), (b, b_spec)]
    if has_add:
        ins.append((addend, o_spec))
    return _call(name, body, (m // bm, n // bn, nk), ins, [(_sds((m, n), out_dtype), o_spec)],
                 scratch=[pltpu.VMEM((bm, bn), F32)] if nk > 1 else [],
                 sem=("parallel", "parallel", "arbitrary"), after=after)[0]


def _ln_stats(z):
    mu = jnp.mean(z, axis=-1, keepdims=True)
    zc = z - mu
    var = jnp.mean(zc * zc, axis=-1, keepdims=True)
    rstd = lax.rsqrt(var + LN_EPS)
    return zc * rstd, rstd


def _ln_fwd(name, a, res, g, b, after=()):
    r, d = a.shape
    br = ROW_BLOCK
    has_res = res is not None

    def body(*refs):
        if has_res:
            a_ref, r_ref, g_ref, b_ref, y_ref, yb_ref = refs
            z = ALPHA * a_ref[...] + r_ref[...]
        else:
            a_ref, g_ref, b_ref, y_ref, yb_ref = refs
            z = a_ref[...]
        xhat, _ = _ln_stats(z)
        y = xhat * g_ref[...] + b_ref[...]
        y_ref[...] = y
        yb_ref[...] = y.astype(yb_ref.dtype)

    ins = [(a, _rows(br, d))]
    if has_res:
        ins.append((res, _rows(br, d)))
    ins += [(g.reshape(1, d), _whole((1, d))), (b.reshape(1, d), _whole((1, d)))]
    outs = [(_sds((r, d), F32), _rows(br, d)), (_sds((r, d), MXU_DTYPE), _rows(br, d))]
    return _call(name, body, (r // br,), ins, outs, sem=("parallel",), after=after)


def _ln_bwd(name, a, res, dy, g, after=()):
    r, d = a.shape
    br = ROW_BLOCK
    has_res = res is not None

    def body(*refs):
        if has_res:
            a_ref, r_ref, dy_ref, g_ref, dz_ref, dzb_ref, dg_ref, db_ref = refs
            z = ALPHA * a_ref[...] + r_ref[...]
        else:
            a_ref, dy_ref, g_ref, dz_ref, dzb_ref, dg_ref, db_ref = refs
            z = a_ref[...]
        xhat, rstd = _ln_stats(z)
        dyv = dy_ref[...]
        dyg = dyv * g_ref[...]
        m1 = jnp.mean(dyg, axis=-1, keepdims=True)
        m2 = jnp.mean(dyg * xhat, axis=-1, keepdims=True)
        dz = rstd * (dyg - m1 - xhat * m2)
        dz_ref[...] = dz
        dzb_ref[...] = dz.astype(dzb_ref.dtype)

        @pl.when(pl.program_id(0) == 0)
        def _():
            dg_ref[...] = jnp.zeros_like(dg_ref)
            db_ref[...] = jnp.zeros_like(db_ref)

        dg_ref[...] += jnp.sum(dyv * xhat, axis=0, keepdims=True)
        db_ref[...] += jnp.sum(dyv, axis=0, keepdims=True)

    ins = [(a, _rows(br, d))]
    if has_res:
        ins.append((res, _rows(br, d)))
    ins += [(dy, _rows(br, d)), (g.reshape(1, d), _whole((1, d)))]
    outs = [(_sds((r, d), F32), _rows(br, d)), (_sds((r, d), MXU_DTYPE), _rows(br, d)),
            (_sds((1, d), F32), _whole((1, d))), (_sds((1, d), F32), _whole((1, d)))]
    return _call(name, body, (r // br,), ins, outs, sem=("arbitrary",), after=after)


LATENTS = ((R_QLAT // Q_RANK, Q_RANK), (R_KVLAT // KV_RANK, KV_RANK))


def _latent_norm_fwd(proj_r, gains):
    r = proj_r.shape[0]
    br = ROW_BLOCK

    def body(xq_ref, xk_ref, gq_ref, gk_ref, yq_ref, yk_ref):
        for x_ref, g_ref, y_ref in ((xq_ref, gq_ref, yq_ref), (xk_ref, gk_ref, yk_ref)):
            x = x_ref[...]
            rstd = lax.rsqrt(jnp.mean(x * x, axis=-1, keepdims=True) + RMS_EPS)
            y_ref[...] = (x * rstd * g_ref[...]).astype(y_ref.dtype)

    return _call("latent_norm_fwd", body, (r // br,),
                 [(proj_r, _rows(br, wd, cb)) for cb, wd in LATENTS]
                 + [(g.reshape(1, wd), _whole((1, wd))) for g, (_, wd) in zip(gains, LATENTS)],
                 [(_sds((r, wd), MXU_DTYPE), _rows(br, wd)) for _, wd in LATENTS], sem=("parallel",))


def _latent_norm_bwd(proj_r, dys, gains):
    r = proj_r.shape[0]
    br = ROW_BLOCK

    def body(xq_ref, xk_ref, dq_ref, dk_ref, gq_ref, gk_ref, oq_ref, ok_ref, dgq_ref, dgk_ref):
        @pl.when(pl.program_id(0) == 0)
        def _():
            dgq_ref[...] = jnp.zeros_like(dgq_ref)
            dgk_ref[...] = jnp.zeros_like(dgk_ref)

        for x_ref, dy_ref, g_ref, dx_ref, dg_ref in ((xq_ref, dq_ref, gq_ref, oq_ref, dgq_ref),
                                                     (xk_ref, dk_ref, gk_ref, ok_ref, dgk_ref)):
            x = x_ref[...]
            rstd = lax.rsqrt(jnp.mean(x * x, axis=-1, keepdims=True) + RMS_EPS)
            nrm = x * rstd
            dyv = dy_ref[...]
            dyg = dyv * g_ref[...]
            dx_ref[...] = (rstd * (dyg - nrm * jnp.mean(dyg * nrm, axis=-1, keepdims=True))).astype(dx_ref.dtype)
            dg_ref[...] += jnp.sum(dyv * nrm, axis=0, keepdims=True)

    return _call("latent_norm_bwd", body, (r // br,),
                 [(proj_r, _rows(br, wd, cb)) for cb, wd in LATENTS]
                 + [(dy, _rows(br, wd)) for dy, (_, wd) in zip(dys, LATENTS)]
                 + [(g.reshape(1, wd), _whole((1, wd))) for g, (_, wd) in zip(gains, LATENTS)],
                 [(_sds((r, wd), MXU_DTYPE), _rows(br, wd)) for _, wd in LATENTS]
                 + [(_sds((1, wd), F32), _whole((1, wd))) for _, wd in LATENTS], sem=("arbitrary",))


def _lane_iota(shape):
    return lax.broadcasted_iota(jnp.int32, shape, 1)


def _rotary(t, c, s, lane, sign):
    second = pltpu.roll(t, HP - HALF, axis=1)
    first = pltpu.roll(t, HALF, axis=1)
    lo = (lane >= LANE_PE) & (lane < LANE_PE + HALF)
    hi = (lane >= LANE_PE + HALF) & (lane < LANE_PE + ROPE)
    return jnp.where(lo, t * c - sign * second * s, jnp.where(hi, t * c + sign * first * s, t))


def _rope_fwd(q_raw, kv, proj_r, cos_t, sin_t):
    r = q_raw.shape[0]
    br = ROW_BLOCK

    def body(q_ref, kv_ref, t_ref, c_ref, s_ref, qo_ref, ko_ref, vo_ref):
        c = c_ref[...]
        s = s_ref[...]
        lane = _lane_iota((br, HP))
        pe = (lane >= LANE_PE) & (lane < LANE_PE + ROPE)
        left = lane < NOPE
        kp = jnp.where(pe, _rotary(t_ref[...], c, s, lane, 1.0), 0.0)
        for h in range(HEADS):
            qo_ref[:, _hs(h)] = (_rotary(q_ref[:, _hs(h)], c, s, lane, 1.0) * MLA_SCALE).astype(qo_ref.dtype)
            t = kv_ref[:, _hs(h)].astype(F32)
            ko_ref[:, _hs(h)] = (jnp.where(left, t, 0.0) + kp).astype(ko_ref.dtype)
            vo_ref[:, _hs(h)] = jnp.where(left, pltpu.roll(t, HP - NOPE, axis=1), 0.0).astype(vo_ref.dtype)

    blk = _rows(br, HP)
    wide = _rows(br, HW)
    return _call("rope_fwd", body, (r // br,),
                 [(q_raw, wide), (kv, wide), (proj_r, _rows(br, HP, R_LAST // HP)), (cos_t, blk), (sin_t, blk)],
                 [(_sds((r, HW), MXU_DTYPE), wide)] * 3, sem=("parallel",))


def _rope_bwd(dq, dk, dv, dfl, cos_t, sin_t):
    r = dq.shape[0]
    br = ROW_BLOCK

    def body(dq_ref, dk_ref, dv_ref, fl_ref, c_ref, s_ref, dqo_ref, dkv_ref, dl_ref):
        c = c_ref[...]
        s = s_ref[...]
        lane = _lane_iota((br, HP))
        pe = (lane >= LANE_PE) & (lane < LANE_PE + ROPE)
        left = lane < NOPE
        acc = jnp.zeros((br, HP), F32)
        for h in range(HEADS):
            dqo_ref[:, _hs(h)] = (_rotary(dq_ref[:, _hs(h)], c, s, lane, -1.0) * MLA_SCALE).astype(dqo_ref.dtype)
            dkh = dk_ref[:, _hs(h)]
            acc = acc + dkh
            dkv_ref[:, _hs(h)] = jnp.where(left, dkh, pltpu.roll(dv_ref[:, _hs(h)], NOPE, axis=1)).astype(dkv_ref.dtype)
        dl_ref[...] = (jnp.where(pe, _rotary(acc, c, s, lane, -1.0), 0.0) + fl_ref[...]).astype(dl_ref.dtype)

    blk = _rows(br, HP)
    wide = _rows(br, HW)
    return _call("rope_bwd", body, (r // br,),
                 [(dq, wide), (dk, wide), (dv, wide), (dfl, blk), (cos_t, blk), (sin_t, blk)],
                 [(_sds((r, HW), MXU_DTYPE), wide), (_sds((r, HW), MXU_DTYPE), wide), (_sds((r, HP), MXU_DTYPE), blk)],
                 sem=("parallel",))


def _log_sigmoid(x):
    return jnp.minimum(x, 0.0) - jnp.log(1.0 + jnp.exp(-jnp.abs(x)))


def _head_lane(x, h, lane):
    return jnp.sum(jnp.where(lane == h, x, 0.0), axis=1, keepdims=True)


def _forget_fwd(proj_r, bf_row):
    r = proj_r.shape[0]
    br = ROW_BLOCK

    def body(t_ref, b_ref, ob_ref, ot_ref, carry_ref):
        @pl.when(pl.program_id(0) == 0)
        def _():
            carry_ref[...] = jnp.zeros_like(carry_ref)

        x = t_ref[...] + b_ref[...]
        lane = _lane_iota(x.shape)
        lf = jnp.where((lane >= LANE_FL) & (lane < LANE_FL + HEADS), _log_sigmoid(x), 0.0)
        tri = (lax.broadcasted_iota(jnp.int32, (br, br), 0) >= lax.broadcasted_iota(jnp.int32, (br, br), 1)).astype(F32)
        cum = jnp.dot(tri, lf, precision=HIGHEST, preferred_element_type=F32) + carry_ref[0:1, :]
        for h in range(HEADS):
            ob_ref[:, _hs(h)] = jnp.broadcast_to(_head_lane(cum, LANE_FL + h, lane), (br, HP))
        ot_ref[...] = cum.T[LANE_FL:LANE_FL + HEADS, :]
        carry_ref[...] = jnp.broadcast_to(cum[br - 1:br, :], carry_ref.shape)

    return _call("forget_fwd", body, (r // br,),
                 [(proj_r, _rows(br, HP, R_LAST // HP)), (bf_row, _whole((1, HP)))],
                 [(_sds((r, HW), F32), _rows(br, HW)), (_sds((HEADS, r), F32), pl.BlockSpec((HEADS, br), lambda i: (0, i)))],
                 scratch=[pltpu.VMEM((8, HP), F32)], sem=("arbitrary",))


def _forget_bwd(proj_r, bf_row, dcq_t, dck_b):
    r = proj_r.shape[0]
    br = ROW_BLOCK
    nb = r // br

    def body(t_ref, b_ref, dcq_ref, dck_ref, o_ref, db_ref, carry_ref):
        @pl.when(pl.program_id(0) == 0)
        def _():
            carry_ref[...] = jnp.zeros_like(carry_ref)
            db_ref[...] = jnp.zeros_like(db_ref)

        lane = _lane_iota((br, HP))
        dc = jnp.concatenate([dcq_ref[...], jnp.zeros((HP - HEADS, br), F32)], axis=0).T
        for h in range(HEADS):
            dc = dc + jnp.where(lane == LANE_FL + h, dck_ref[:, h * HP:h * HP + 1], 0.0)
        triu = (lax.broadcasted_iota(jnp.int32, (br, br), 0) <= lax.broadcasted_iota(jnp.int32, (br, br), 1)).astype(F32)
        dlf = jnp.dot(triu, dc, precision=HIGHEST, preferred_element_type=F32) + carry_ref[0:1, :]
        carry_ref[...] = jnp.broadcast_to(dlf[0:1, :], carry_ref.shape)
        x = t_ref[...] + b_ref[...]
        dfl = jnp.where((lane >= LANE_FL) & (lane < LANE_FL + HEADS), dlf * jax.nn.sigmoid(-x), 0.0)
        o_ref[...] = dfl
        db_ref[...] += jnp.sum(dfl, axis=0, keepdims=True)

    rev = pl.BlockSpec((br, HP), lambda i: (nb - 1 - i, 0))
    return _call("forget_bwd", body, (nb,),
                 [(proj_r, pl.BlockSpec((br, HP), lambda i: (nb - 1 - i, R_LAST // HP))), (bf_row, _whole((1, HP))),
                  (dcq_t, pl.BlockSpec((HEADS, br), lambda i: (0, nb - 1 - i))),
                  (dck_b, pl.BlockSpec((br, HW), lambda i: (nb - 1 - i, 0)))],
                 [(_sds((r, HP), F32), rev), (_sds((1, HP), F32), _whole((1, HP)))],
                 scratch=[pltpu.VMEM((8, HP), F32)], sem=("arbitrary",))


def _gate_fwd(proj_r, b_gate, bm, bfx):
    r, d = bm.shape
    br = ROW_BLOCK
    cb = R_GATE // d

    def body(gm_ref, gf_ref, b1_ref, b2_ref, bm_ref, bf_ref, o_ref):
        g1 = jax.nn.sigmoid(gm_ref[...] + b1_ref[...])
        g2 = jax.nn.sigmoid(gf_ref[...] + b2_ref[...])
        o_ref[...] = (g1 * bm_ref[...].astype(F32) + g2 * bf_ref[...].astype(F32)).astype(o_ref.dtype)

    b1 = b_gate[:d].reshape(1, d)
    b2 = b_gate[d:].reshape(1, d)
    return _call("gate_fwd", body, (r // br,),
                 [(proj_r, _rows(br, d, cb)), (proj_r, _rows(br, d, cb + 1)), (b1, _whole((1, d))), (b2, _whole((1, d))),
                  (bm, _rows(br, d)), (bfx, _rows(br, d))],
                 [(_sds((r, d), MXU_DTYPE), _rows(br, d))], sem=("parallel",))[0]


def _gate_bwd(proj_r, b_gate, bm, bfx, dmerged):
    r, d = bm.shape
    br = ROW_BLOCK
    cb = R_GATE // d

    def body(gm_ref, gf_ref, b1_ref, b2_ref, bm_ref, bf_ref, dm_ref, dbm_ref, dbf_ref, dgl_ref, dbg_ref):
        g1 = jax.nn.sigmoid(gm_ref[...] + b1_ref[...])
        g2 = jax.nn.sigmoid(gf_ref[...] + b2_ref[...])
        dm = dm_ref[...].astype(F32)
        dbm_ref[...] = (dm * g1).astype(dbm_ref.dtype)
        dbf_ref[...] = (dm * g2).astype(dbf_ref.dtype)
        dl1 = dm * bm_ref[...].astype(F32) * (g1 * (1.0 - g1))
        dl2 = dm * bf_ref[...].astype(F32) * (g2 * (1.0 - g2))
        dgl_ref[:, 0:d] = dl1.astype(dgl_ref.dtype)
        dgl_ref[:, d:2 * d] = dl2.astype(dgl_ref.dtype)

        @pl.when(pl.program_id(0) == 0)
        def _():
            dbg_ref[...] = jnp.zeros_like(dbg_ref)

        dbg_ref[:, 0:d] += jnp.sum(dl1, axis=0, keepdims=True)
        dbg_ref[:, d:2 * d] += jnp.sum(dl2, axis=0, keepdims=True)

    b1 = b_gate[:d].reshape(1, d)
    b2 = b_gate[d:].reshape(1, d)
    return _call("gate_bwd", body, (r // br,),
                 [(proj_r, _rows(br, d, cb)), (proj_r, _rows(br, d, cb + 1)), (b1, _whole((1, d))), (b2, _whole((1, d))),
                  (bm, _rows(br, d)), (bfx, _rows(br, d)), (dmerged, _rows(br, d))],
                 [(_sds((r, d), MXU_DTYPE), _rows(br, d)), (_sds((r, d), MXU_DTYPE), _rows(br, d)),
                  (_sds((r, 2 * d), MXU_DTYPE), _rows(br, 2 * d)), (_sds((1, 2 * d), F32), _whole((1, 2 * d)))],
                 sem=("arbitrary",))


HALO = 16
GLU_BWD_BLOCK = 256
COPY_ROWS = 512


def _conv_taps(gp, halo, first_block):
    halo = jnp.where(first_block, 0.0, halo.astype(F32))
    rid = lax.broadcasted_iota(jnp.int32, gp.shape, 0)
    last, prev = halo[HALO - 1:HALO, :], halo[HALO - 2:HALO - 1, :]
    g1 = jnp.where(rid == 0, last, pltpu.roll(gp, 1, axis=0))
    g2 = jnp.where(rid == 0, prev, jnp.where(rid == 1, last, pltpu.roll(gp, 2, axis=0)))
    return g1, g2


def _prev_halo(br, c):
    return pl.BlockSpec((HALO, c), lambda i: (jnp.maximum(i * (br // HALO) - 1, 0), 0))


def _glu_fwd(up, conv_w, conv_b):
    r = up.shape[0]
    c = D_FF
    br = ROW_BLOCK

    def body(gp_ref, halo_ref, val_ref, w_ref, b_ref, o_ref):
        gp = gp_ref[...].astype(F32)
        g1, g2 = _conv_taps(gp, halo_ref[...], pl.program_id(0) == 0)
        gate = w_ref[0:1, :] * g2 + w_ref[1:2, :] * g1 + w_ref[2:3, :] * gp + b_ref[...]
        o_ref[...] = (gate * jax.nn.sigmoid(gate) * val_ref[...].astype(F32)).astype(o_ref.dtype)

    return _call("glu_fwd", body, (r // br,),
                 [(up, _rows(br, c, 0)), (up, _prev_halo(br, c)), (up, _rows(br, c, 1)),
                  (conv_w, _whole((3, c))), (conv_b.reshape(1, c), _whole((1, c)))],
                 [(_sds((r, c), MXU_DTYPE), _rows(br, c))], sem=("parallel",))[0]


def _glu_bwd(up, conv_w, conv_b, d_act):
    r = up.shape[0]
    c = D_FF
    br = GLU_BWD_BLOCK
    nb = r // br

    def body(gp_ref, halo_ref, val_ref, da_ref, gpn_ref, valn_ref, dan_ref, w_ref, b_ref, o_ref, dw_ref, db_ref):
        i = pl.program_id(0)
        w0, w1, w2, bias = w_ref[0:1, :], w_ref[1:2, :], w_ref[2:3, :], b_ref[...]

        def d_gate(gp, g1, g2, val, da):
            gate = w0 * g2 + w1 * g1 + w2 * gp + bias
            sg = jax.nn.sigmoid(gate)
            return da * val * (sg * (1.0 + gate * (1.0 - sg))), da * (gate * sg)

        gp = gp_ref[...].astype(F32)
        g1, g2 = _conv_taps(gp, halo_ref[...], i == 0)
        dg, dv = d_gate(gp, g1, g2, val_ref[...].astype(F32), da_ref[...].astype(F32))
        gpn = gpn_ref[...].astype(F32)
        g1n, g2n = _conv_taps(gpn, gp[br - HALO:, :], False)
        dgn, _ = d_gate(gpn, g1n, g2n, valn_ref[...].astype(F32), dan_ref[...].astype(F32))
        dgn = jnp.where(i == nb - 1, 0.0, dgn)
        rid = lax.broadcasted_iota(jnp.int32, dg.shape, 0)
        u1 = jnp.where(rid == br - 1, dgn[0:1, :], pltpu.roll(dg, br - 1, axis=0))
        u2 = jnp.where(rid == br - 1, dgn[1:2, :], jnp.where(rid == br - 2, dgn[0:1, :], pltpu.roll(dg, br - 2, axis=0)))
        o_ref[:, 0:c] = (w2 * dg + w1 * u1 + w0 * u2).astype(o_ref.dtype)
        o_ref[:, c:2 * c] = dv.astype(o_ref.dtype)

        @pl.when(i == 0)
        def _():
            dw_ref[...] = jnp.zeros_like(dw_ref)
            db_ref[...] = jnp.zeros_like(db_ref)

        dw_ref[0:1, :] += jnp.sum(dg * g2, axis=0, keepdims=True)
        dw_ref[1:2, :] += jnp.sum(dg * g1, axis=0, keepdims=True)
        dw_ref[2:3, :] += jnp.sum(dg * gp, axis=0, keepdims=True)
        db_ref[...] += jnp.sum(dg, axis=0, keepdims=True)

    nxt = lambda cb: pl.BlockSpec((HALO, c), lambda i: (jnp.minimum((i + 1) * (br // HALO), r // HALO - 1), cb))
    return _call("glu_bwd", body, (nb,),
                 [(up, _rows(br, c, 0)), (up, _prev_halo(br, c)), (up, _rows(br, c, 1)), (d_act, _rows(br, c)),
                  (up, nxt(0)), (up, nxt(1)), (d_act, nxt(0)),
                  (conv_w, _whole((3, c))), (conv_b.reshape(1, c), _whole((1, c)))],
                 [(_sds((r, 2 * c), MXU_DTYPE), _rows(br, 2 * c)),
                  (_sds((8, c), F32), _whole((8, c))), (_sds((1, c), F32), _whole((1, c)))],
                 sem=("arbitrary",))


def _token_specs(seq, d):
    br = TOKEN_BLOCK
    nxb = seq // br
    main = pl.BlockSpec((br, d), lambda i: (jnp.minimum(i, nxb - 1), 0))
    tail = pl.BlockSpec((N_META, d), lambda i: (jnp.clip(i * (br // N_META) - 1, 0, seq // N_META - 1), 0))
    return main, tail


def _padded_block(main_ref, tail_ref, first, seq):
    br = TOKEN_BLOCK
    i = pl.program_id(0)
    nxb = seq // br
    main = jnp.where(i < nxb, main_ref[...], 0.0)
    head = jnp.where(i == 0, first, jnp.where(i <= nxb, tail_ref[...], 0.0))
    return jnp.concatenate([head, main[:br - N_META]], axis=0)


def _ln_emb_fwd(x, meta, g, b, rows, after=()):
    seq, d = x.shape
    br = TOKEN_BLOCK
    assert seq % br == 0 and br % N_META == 0 and rows % br == 0

    def body(x_ref, tail_ref, meta_ref, g_ref, b_ref, y_ref, yb_ref):
        z = _padded_block(x_ref, tail_ref, meta_ref[...], seq)
        xhat, _ = _ln_stats(z)
        y = xhat * g_ref[...] + b_ref[...]
        y_ref[...] = y
        yb_ref[...] = y.astype(yb_ref.dtype)

    main, tail = _token_specs(seq, d)
    return _call("ln_emb_fwd", body, (rows // br,),
                 [(x, main), (x, tail), (meta, _whole((N_META, d))), (g.reshape(1, d), _whole((1, d))),
                  (b.reshape(1, d), _whole((1, d)))],
                 [(_sds((rows, d), F32), _rows(br, d)), (_sds((rows, d), MXU_DTYPE), _rows(br, d))],
                 sem=("parallel",), after=after)


def _ln_emb_bwd(x, meta, dh0, g):
    seq, d = x.shape
    br = TOKEN_BLOCK
    step = br // N_META

    def ln_bwd(z, dy, gv):
        xhat, rstd = _ln_stats(z)
        dyg = dy * gv
        m1 = jnp.mean(dyg, axis=-1, keepdims=True)
        m2 = jnp.mean(dyg * xhat, axis=-1, keepdims=True)
        dz = rstd * (dyg - m1 - xhat * m2)
        return dz, jnp.sum(dy * xhat, axis=0, keepdims=True), jnp.sum(dy, axis=0, keepdims=True)

    def body(x_ref, dh_ref, nxt_ref, meta_ref, top_ref, g_ref, dx_ref, dm_ref, dg_ref, db_ref):
        gv = g_ref[...]
        dy = jnp.concatenate([dh_ref[N_META:, :], nxt_ref[...]], axis=0)
        dz, dg, db = ln_bwd(x_ref[...], dy, gv)
        dx_ref[...] = dz

        @pl.when(pl.program_id(0) == 0)
        def _():
            dzm, dgm, dbm = ln_bwd(meta_ref[...], top_ref[...], gv)
            dm_ref[...] = dzm
            dg_ref[...] = dgm
            db_ref[...] = dbm

        dg_ref[...] += dg
        db_ref[...] += db

    small = _whole((N_META, d))
    return _call("ln_emb_bwd", body, (seq // br,),
                 [(x, _rows(br, d)), (dh0, _rows(br, d)), (dh0, pl.BlockSpec((N_META, d), lambda i: ((i + 1) * step, 0))),
                  (meta, small), (dh0, small), (g.reshape(1, d), _whole((1, d)))],
                 [(_sds((seq, d), F32), _rows(br, d)), (_sds((N_META, d), F32), small),
                  (_sds((1, d), F32), _whole((1, d))), (_sds((1, d), F32), _whole((1, d)))], sem=("arbitrary",))


def _loss_err(a_ref, r_ref, t_ref, tail_ref, g_ref, b_ref, seq):
    br, d = a_ref.shape
    xhat, rstd = _ln_stats(ALPHA * a_ref[...] + r_ref[...])
    y = xhat * g_ref[...] + b_ref[...]
    t = _padded_block(t_ref, tail_ref, jnp.zeros((N_META, d), F32), seq)
    rid = lax.broadcasted_iota(jnp.int32, (br, d), 0) + pl.program_id(0) * br
    valid = (rid >= N_META) & (rid < N_META + seq)
    return jnp.where(valid, y - t, 0.0), xhat, rstd


def _ln_ffn_bwd(h1, f, tgt, g, b):
    r, d = h1.shape
    seq = tgt.shape[0]
    br = TOKEN_BLOCK

    def body(a_ref, r_ref, t_ref, tail_ref, g_ref, b_ref, dz_ref, dzb_ref, dg_ref, db_ref, l_ref):
        err, xhat, rstd = _loss_err(a_ref, r_ref, t_ref, tail_ref, g_ref, b_ref, seq)
        dyv = err * (1.0 / d)
        dyg = dyv * g_ref[...]
        m1 = jnp.mean(dyg, axis=-1, keepdims=True)
        m2 = jnp.mean(dyg * xhat, axis=-1, keepdims=True)
        dz = rstd * (dyg - m1 - xhat * m2)
        dz_ref[...] = dz
        dzb_ref[...] = dz.astype(dzb_ref.dtype)

        @pl.when(pl.program_id(0) == 0)
        def _():
            dg_ref[...] = jnp.zeros_like(dg_ref)
            db_ref[...] = jnp.zeros_like(db_ref)
            l_ref[...] = jnp.zeros_like(l_ref)

        dg_ref[...] += jnp.sum(dyv * xhat, axis=0, keepdims=True)
        db_ref[...] += jnp.sum(dyv, axis=0, keepdims=True)
        l_ref[...] += jnp.sum(jnp.sum(err * err, axis=1, keepdims=True), axis=0, keepdims=True) * (0.5 / d)

    main, tail = _token_specs(seq, d)
    return _call("ln_ffn_bwd", body, (r // br,),
                 [(h1, _rows(br, d)), (f, _rows(br, d)), (tgt, main), (tgt, tail),
                  (g.reshape(1, d), _whole((1, d))), (b.reshape(1, d), _whole((1, d)))],
                 [(_sds((r, d), F32), _rows(br, d)), (_sds((r, d), MXU_DTYPE), _rows(br, d)),
                  (_sds((1, d), F32), _whole((1, d))), (_sds((1, d), F32), _whole((1, d))),
                  (_sds((1, 1), F32), _whole((1, 1)))], sem=("arbitrary",))


def _attn_fwd(name, q, k, v, cum_b=None, cum_t=None):
    (qa, qg), (ka, kg), (va, vg) = q, k, v
    r = qa.shape[0]
    tq, tk = ATT_TQ, ATT_TK
    nq, nk = r // tq, r // tk
    bias = cum_b is not None

    def body(*refs):
        if bias:
            q_ref, k_ref, vt_ref, cb_ref, ct_ref, o_ref, ob_ref, lse_ref = refs
        else:
            q_ref, k_ref, vt_ref, o_ref, ob_ref, lse_ref = refs
        i = pl.program_id(1)
        qs = [q_ref[:, _hs(hh)] for hh in range(hg)]
        cqs = [ct_ref[hh] for hh in range(hg)] if bias else None
        diff = lax.broadcasted_iota(jnp.int32, (tk, tq), 0) - lax.broadcasted_iota(jnp.int32, (tk, tq), 1)

        def step(j, carry, masked):
            keys = pl.ds(pl.multiple_of(j * tk, tk), tk)
            out = []
            for hh in range(hg):
                m, l, acc = carry[hh]
                kt = k_ref[keys, _hs(hh)]
                s = lax.dot_general(kt, qs[hh], NT, preferred_element_type=F32)
                if bias:
                    s = s + (cqs[hh] - cb_ref[keys, hh * HP:hh * HP + 1])
                if masked:
                    s = jnp.where(diff <= i * tq - j * tk, s, NEG_INF)
                m_new = jnp.maximum(m, jnp.max(s, axis=0, keepdims=True))
                p = jnp.exp(s - m_new)
                a = jnp.exp(m - m_new)
                l = a * l + jnp.sum(p, axis=0, keepdims=True)
                acc = a * acc + jnp.dot(vt_ref[j, _hs(hh), :], p.astype(kt.dtype), preferred_element_type=F32)
                out.append((m_new, l, acc))
            return tuple(out)

        n_clear = (i * tq + 1) // tk
        n_all = ((i + 1) * tq - 1) // tk + 1
        carry = tuple((jnp.full((1, tq), NEG_INF, F32), jnp.zeros((1, tq), F32), jnp.zeros((HP, tq), F32))
                      for _ in range(hg))
        carry = lax.fori_loop(0, n_clear, lambda j, c: step(j, c, False), carry)
        carry = lax.fori_loop(n_clear, n_all, lambda j, c: step(j, c, True), carry)
        for hh in range(hg):
            m, l, acc = carry[hh]
            o = (acc / l).T
            o_ref[:, _hs(hh)] = o
            ob_ref[:, hh * V_DIM:(hh + 1) * V_DIM] = o[:, :V_DIM].astype(ob_ref.dtype)
            lse_ref[hh] = m + jnp.log(l)

    hg = ATT_HEADS_FWD
    w = hg * HP
    gpw = HW // w
    tile = lambda g: pl.BlockSpec((tq, w), lambda h, i: (i, g * gpw + h))
    res = lambda g: pl.BlockSpec((r, w), lambda h, i: (0, g * gpw + h))
    v_t = _key_tiles_transposed(name + "_vt", va, vg)
    ins = [(qa, tile(qg)), (ka, res(kg)), (v_t, pl.BlockSpec((nk, w, tk), lambda h, i: (0, h, 0)))]
    if bias:
        ins += [(cum_b, res(0)),
                (cum_t.reshape(HEADS, nq, 1, tq), pl.BlockSpec((hg, None, 1, tq), lambda h, i: (h, i, 0, 0)))]
    outs = [(_sds((r, HW), F32), tile(0)),
            (_sds((r, HEADS * V_DIM), MXU_DTYPE), pl.BlockSpec((tq, hg * V_DIM), lambda h, i: (i, h))),
            (_sds((HEADS, nq, 1, tq), F32), pl.BlockSpec((hg, None, 1, tq), lambda h, i: (h, i, 0, 0)))]
    o, ob, lse = _call(name, body, (gpw, nq), ins, outs, sem=("parallel", "parallel"))
    return o, ob, lse.reshape(HEADS, r)


def _key_tiles_transposed(name, a, group):
    r = a.shape[0]
    tk = ATT_TK

    def body(x_ref, o_ref):
        for h in range(HEADS):
            o_ref[_hs(h), :] = x_ref[:, _hs(h)].astype(F32).T.astype(o_ref.dtype)

    return _call(name, body, (r // tk,),
                 [(a, pl.BlockSpec((tk, HW), lambda j: (j, group)))],
                 [(_sds((r // tk, HW, tk), a.dtype), pl.BlockSpec((None, HW, tk), lambda j: (j, 0, 0)))],
                 sem=("parallel",))[0]


def _attn_bwd(name, q, k, v, do_b, o, lse_t, cum_b=None, cum_t=None, out_dtype=F32, after=()):
    (qa, qg), (ka, kg), (va, vg) = q, k, v
    r = qa.shape[0]
    tq, tk = ATT_TQ, ATT_TK
    nq, nk = r // tq, r // tk
    bias = cum_b is not None

    def body(*refs):
        if bias:
            (q_ref, k_ref, v_ref, do_ref, o_ref, lse_ref, cb_ref, ct_ref,
             dq_ref, dk_ref, dv_ref, dcq_ref, dck_ref, dqt_ref, dl_ref) = refs
        else:
            q_ref, k_ref, v_ref, do_ref, o_ref, lse_ref, dq_ref, dk_ref, dv_ref, dqt_ref, dl_ref = refs
        j = pl.program_id(1)

        @pl.when(j == 0)
        def _():
            dqt_ref[...] = jnp.zeros_like(dqt_ref)
            if bias:
                dcq_ref[...] = jnp.zeros_like(dcq_ref)
            for hh in range(hg):
                for i in range(nq):
                    rows = slice(i * tq, (i + 1) * tq)
                    prod = do_ref[rows, _hs(hh)].astype(F32) * o_ref[rows, _hs(hh)]
                    dl_ref[hh, i] = jnp.sum(prod.T, axis=0, keepdims=True)

        kts = [k_ref[:, _hs(hh)] for hh in range(hg)]
        vts = [v_ref[:, _hs(hh)] for hh in range(hg)]
        k_trs = [kt.astype(F32).T.astype(kt.dtype) for kt in kts]
        cks = [cb_ref[:, hh * HP:hh * HP + 1] for hh in range(hg)] if bias else None
        diff = lax.broadcasted_iota(jnp.int32, (tk, tq), 0) - lax.broadcasted_iota(jnp.int32, (tk, tq), 1)

        def step(i, carry, masked):
            rows = pl.ds(pl.multiple_of(i * tq, tq), tq)
            out = []
            for hh in range(hg):
                dk_acc, dv_acc, dck_acc = carry[hh]
                qt = q_ref[rows, _hs(hh)]
                dot = do_ref[rows, _hs(hh)]
                s = lax.dot_general(kts[hh], qt, NT, preferred_element_type=F32)
                if bias:
                    s = s + (ct_ref[hh, i] - cks[hh])
                if masked:
                    s = jnp.where(diff <= i * tq - j * tk, s, NEG_INF)
                p = jnp.exp(s - lse_ref[hh, i])
                dp = lax.dot_general(vts[hh], dot, NT, preferred_element_type=F32)
                ds = p * (dp - dl_ref[hh, i])
                pb = p.astype(dot.dtype)
                dsb = ds.astype(qt.dtype)
                dv_acc = dv_acc + jnp.dot(pb, dot, preferred_element_type=F32)
                dk_acc = dk_acc + jnp.dot(dsb, qt, preferred_element_type=F32)
                dqt_ref[hh, i] += jnp.dot(k_trs[hh], dsb, preferred_element_type=F32)
                if bias:
                    dcq_ref[hh, i] += jnp.sum(ds, axis=0, keepdims=True)
                    dck_acc = dck_acc - jnp.sum(ds, axis=1, keepdims=True)
                out.append((dk_acc, dv_acc, dck_acc))
            return tuple(out)

        i_first = (j * tk) // tq
        i_clear = jnp.minimum(((j + 1) * tk + tq - 2) // tq, nq)
        carry = tuple((jnp.zeros((tk, HP), F32), jnp.zeros((tk, HP), F32), jnp.zeros((tk, 1), F32)) for _ in range(hg))
        carry = lax.fori_loop(i_first, i_clear, lambda i, c: step(i, c, True), carry)
        carry = lax.fori_loop(i_clear, nq, lambda i, c: step(i, c, False), carry)
        for hh in range(hg):
            dk_acc, dv_acc, dck_acc = carry[hh]
            dk_ref[:, _hs(hh)] = dk_acc.astype(dk_ref.dtype)
            dv_ref[:, _hs(hh)] = dv_acc.astype(dv_ref.dtype)
            if bias:
                dck_ref[:, _hs(hh)] = jnp.broadcast_to(dck_acc, (tk, HP))

        @pl.when(j == nk - 1)
        def _():
            for hh in range(hg):
                for i in range(nq):
                    dq_ref[i * tq:(i + 1) * tq, _hs(hh)] = dqt_ref[hh, i].T.astype(dq_ref.dtype)

    hg = ATT_HEADS
    w = hg * HP
    gpw = HW // w
    res = lambda g: pl.BlockSpec((r, w), lambda h, j: (0, g * gpw + h))
    tile = lambda g: pl.BlockSpec((tk, w), lambda h, j: (j, g * gpw + h))
    rowv = pl.BlockSpec((hg, nq, 1, tq), lambda h, j: (h, 0, 0, 0))
    as_rows = lambda a: a.reshape(HEADS, nq, 1, tq)
    ins = [(qa, res(qg)), (ka, tile(kg)), (va, tile(vg)), (do_b, res(0)), (o, res(0)), (as_rows(lse_t), rowv)]
    outs = [(_sds((r, HW), out_dtype), res(0)), (_sds((r, HW), out_dtype), tile(0)), (_sds((r, HW), out_dtype), tile(0))]
    if bias:
        ins += [(cum_b, tile(0)), (as_rows(cum_t), rowv)]
        outs += [(_sds((HEADS, nq, 1, tq), F32), rowv), (_sds((r, HW), F32), tile(0))]
    res_out = _call(name, body, (gpw, nk), ins, outs,
                    scratch=[pltpu.VMEM((hg, nq, HP, tq), F32), pltpu.VMEM((hg, nq, 1, tq), F32)],
                    sem=("parallel", "arbitrary"), after=after)
    if bias:
        dq, dk, dv, dcq, dck = res_out
        return dq, dk, dv, dcq.reshape(HEADS, r), dck
    return res_out


MESH_ID = pl.DeviceIdType.MESH
ANY = pl.BlockSpec(memory_space=pl.ANY)


N_GATHER_COPIES = 8


def _allgather(name, shards):
    n = len(shards)

    def body(*refs):
        x_refs, out_refs = refs[:n], refs[n:2 * n]
        send_sems, recv_sems, local_sems = refs[2 * n:]
        x, y, c = lax.axis_index("x"), lax.axis_index("y"), lax.axis_index("c")
        me, sibling = (x, y, c), (x, y, 1 - c)
        xn, yn, dg = (1 - x, y, c), (x, 1 - y, c), (1 - x, 1 - y, c)
        other = lambda dev: (dev[0], dev[1], 1 - c)

        def slot(ti, dev, half=None):
            ref = out_refs[ti].at[4 * dev[0] + 2 * dev[1] + dev[2]]
            if half is None:
                return ref
            rows = shards[ti].shape[0] // 2
            return ref.at[pl.ds(half * rows, rows)]

        def copy(ti, k, block, to, half=None, src=None):
            return pltpu.make_async_remote_copy(
                src_ref=slot(ti, block, half) if src is None else src, dst_ref=slot(ti, block, half),
                send_sem=send_sems.at[ti, k], recv_sem=recv_sems.at[ti, k], device_id=to, device_id_type=MESH_ID)

        mine = [pltpu.make_async_copy(x_refs[ti], slot(ti, me), local_sems.at[ti]) for ti in range(n)]
        for cp in mine:
            cp.start()
        started = []

        def go(cp):
            cp.start()
            started.append(cp)

        for ti in range(n):
            go(copy(ti, 0, me, sibling, src=x_refs[ti]))
            go(copy(ti, 1, me, xn, src=x_refs[ti]))
            go(copy(ti, 2, me, yn, src=x_refs[ti]))
        for ti in range(n):
            copy(ti, 1, xn, me).wait_recv()
            go(copy(ti, 3, xn, yn, half=0))
            go(copy(ti, 5, xn, sibling))
            copy(ti, 2, yn, me).wait_recv()
            go(copy(ti, 4, yn, xn, half=1))
            go(copy(ti, 6, yn, sibling))
        for ti in range(n):
            copy(ti, 3, dg, me, half=0).wait_recv()
            copy(ti, 4, dg, me, half=1).wait_recv()
            go(copy(ti, 7, dg, sibling))
        for ti in range(n):
            copy(ti, 0, sibling, me).wait_recv()
            for k, dev in ((5, xn), (6, yn), (7, dg)):
                copy(ti, k, other(dev), me).wait_recv()
        for cp in started:
            cp.wait_send()
        for cp in mine:
            cp.wait()

    sems = pltpu.SemaphoreType.DMA((n, N_GATHER_COPIES))
    return pl.pallas_call(
        body, name=name, out_shape=[_sds((N_DEV,) + s.shape, s.dtype) for s in shards],
        in_specs=[ANY] * n, out_specs=[ANY] * n,
        scratch_shapes=[sems, sems, pltpu.SemaphoreType.DMA((n,))],
    )(*shards)


HBM = pl.BlockSpec(memory_space=pltpu.HBM)
SEM = pl.BlockSpec(memory_space=pltpu.SEMAPHORE)
EFFECT = pltpu.SideEffectType.DATAFLOW_SIDE_EFFECTING
N_PEER = N_DEV - 1


def _my_id():
    return 4 * lax.axis_index("x") + 2 * lax.axis_index("y") + lax.axis_index("c")


def _peers():
    x, y, c = lax.axis_index("x"), lax.axis_index("y"), lax.axis_index("c")
    out = []
    for k in range(1, N_DEV):
        px, py, pc = (1 - x if k & 4 else x, 1 - y if k & 2 else y, 1 - c if k & 1 else c)
        out.append(((px, py, pc), 4 * px + 2 * py + pc))
    return out


def _push_copies(src_refs, land_refs, send_sems, recv_sems, scatter, landing):
    me = _my_id()
    out = []
    for ti, (src, land) in enumerate(zip(src_refs, land_refs)):
        for k, (dev, pid) in enumerate(_peers()):
            out.append(pltpu.make_async_remote_copy(
                src_ref=src.at[pid] if scatter else src, dst_ref=land.at[pid if landing else me],
                send_sem=send_sems.at[ti * N_PEER + k], recv_sem=recv_sems.at[ti * N_PEER + k],
                device_id=dev, device_id_type=MESH_ID))
    return out


def _push_start(name, groups, scatter, after=None):
    sizes = [len(g) for g in groups]
    srcs = [a for g in groups for a in g]
    n = len(srcs)
    slot = lambda s: s.shape[1:] if scatter else s.shape
    lands = [lax.empty((N_DEV,) + slot(s), s.dtype) for s in srcs]
    n_after = 0 if after is None else 1
    n_grp = len(groups)

    def body(*refs):
        src_refs, land_refs = refs[:n], refs[n:2 * n]
        sems = refs[2 * n + n_after:2 * n + n_after + 2 * n_grp]
        token = refs[-1]
        lo = 0
        for gi, sz in enumerate(sizes):
            for cp in _push_copies(src_refs[lo:lo + sz], land_refs[lo:lo + sz], sems[2 * gi], sems[2 * gi + 1], scatter, False):
                cp.start()
            lo += sz
        token[...] = jnp.zeros_like(token)

    hbm = lambda a: pltpu.with_memory_space_constraint(a, pltpu.HBM)
    operands = [hbm(a) for a in srcs + lands] + ([after] if n_after else [])
    sem_shapes = [pltpu.SemaphoreType.DMA((sz * N_PEER,)) for sz in sizes for _ in range(2)]
    res = pl.pallas_call(
        body, name=name,
        out_shape=sem_shapes + [pltpu.HBM(a.shape, a.dtype) for a in srcs + lands] + [_sds((8, 128), F32)],
        in_specs=[HBM] * (2 * n) + [ANY] * n_after,
        out_specs=[SEM] * (2 * n_grp) + [HBM] * (2 * n) + [pl.BlockSpec(memory_space=pltpu.VMEM)],
        input_output_aliases={i: 2 * n_grp + i for i in range(2 * n)},
        compiler_params=pltpu.CompilerParams(has_side_effects=EFFECT),
    )(*operands)
    thru = res[2 * n_grp:2 * n_grp + 2 * n]
    handles, lo = [], 0
    for gi, sz in enumerate(sizes):
        handles.append((res[2 * gi], res[2 * gi + 1], list(thru[lo:lo + sz]), list(thru[n + lo:n + lo + sz]), scatter))
        lo += sz
    return handles, res[-1]


def _push_wait(name, handle, after):
    send_sems, recv_sems, srcs, lands, scatter = handle
    n = len(srcs)

    def body(*refs):
        src_refs, land_refs = refs[:n], refs[n:2 * n]
        s_sems, r_sems = refs[2 * n], refs[2 * n + 1]
        for cp in _push_copies(src_refs, land_refs, s_sems, r_sems, scatter, True):
            cp.wait_send()
            cp.wait_recv()

    res = pl.pallas_call(
        body, name=name,
        out_shape=[pltpu.HBM(a.shape, a.dtype) for a in srcs + lands],
        in_specs=[HBM] * (2 * n) + [SEM, SEM, ANY], out_specs=[HBM] * (2 * n),
        input_output_aliases={i: i for i in range(2 * n)},
        compiler_params=pltpu.CompilerParams(has_side_effects=EFFECT),
    )(*srcs, *lands, send_sems, recv_sems, after)
    return list(res[n:])


def _adamw(name, parts, w, m, v, own=None):
    r, c = w.shape
    br = _pick(r, COPY_ROWS, 16)
    has_own = own is not None

    def body(*refs):
        if has_own:
            p_ref, own_ref, w_ref, m_ref, v_ref, g_ref, d_ref, nm_ref, nv_ref = refs
            me = _my_id()
            mine = own_ref[...].astype(F32)
        else:
            p_ref, w_ref, m_ref, v_ref, g_ref, d_ref, nm_ref, nv_ref = refs
        g = None
        for k in range(N_DEV):
            t = p_ref[k].astype(F32)
            if has_own:
                t = jnp.where(me == k, mine, t)
            g = t if g is None else g + t
        mm = ADAM_B1 * m_ref[...] + (1.0 - ADAM_B1) * g
        vv = ADAM_B2 * v_ref[...] + (1.0 - ADAM_B2) * (g * g)
        m_hat = mm / (1.0 - ADAM_B1 ** ADAM_STEP)
        v_hat = vv / (1.0 - ADAM_B2 ** ADAM_STEP)
        g_ref[...] = g
        d_ref[...] = -ADAM_LR * (m_hat / (jnp.sqrt(v_hat) + ADAM_EPS) + ADAM_WD * w_ref[...])
        nm_ref[...] = mm
        nv_ref[...] = vv

    spec = _rows(br, c)
    out = (_sds((r, c), F32), spec)
    ins = [(parts, pl.BlockSpec((N_DEV, br, c), lambda i: (0, i, 0)))] + ([(own, spec)] if has_own else [])
    return _call(name, body, (r // br,), ins + [(w, spec), (m, spec), (v, spec)], [out] * 4, sem=("parallel",))


def _pad_head_cols(w, d):
    k = w.shape[0]
    return jnp.pad(w.reshape(k, HEADS, d), ((0, 0), (0, 0), (0, HP - d))).reshape(k, HW)


def _unpad_head_cols(wp, d):
    k = wp.shape[0]
    return wp.reshape(k, HEADS, HP)[:, :, :d].reshape(k, HEADS * d)


def _pad_head_rows(w, d):
    n = w.shape[1]
    return jnp.pad(w.reshape(HEADS, d, n), ((0, 0), (0, HP - d), (0, 0))).reshape(HW, n)


def _w_in_runs():
    nat = {}
    o = 0
    for nm, wd in (("q", Q_RANK), ("kv", KV_RANK), ("kr", ROPE), ("fq", FOX_W), ("fk", FOX_W), ("fv", FOX_W),
                   ("fl", HEADS), ("gate", 2 * D_MODEL)):
        nat[nm] = o
        o += wd
    runs = [(1, R_QLAT, nat["q"], Q_RANK, 1.0), (1, R_KVLAT, nat["kv"], KV_RANK, 1.0),
            (1, R_LAST + LANE_FL, nat["fl"], HEADS, 1.0), (1, R_LAST + LANE_PE, nat["kr"], ROPE, 1.0),
            (1, R_GATE, nat["gate"], 2 * D_MODEL, 1.0)]
    for grp, (nm, sc) in enumerate((("fq", FOX_SCALE), ("fk", 1.0), ("fv", 1.0))):
        runs.append((0, grp * FOX_W, nat[nm], FOX_W, sc))
    return runs


def _head_pad_moves(pad):
    moves = []
    for grp in range(3):
        for h in range(HEADS):
            narrow, wide = grp * FOX_W + h * FOX_DIM, h * HP
            if pad:
                moves.append((0, None, grp * HW + wide, 0, None, narrow, FOX_DIM, 1.0))
            else:
                moves.append((0, None, narrow, grp, None, wide, FOX_DIM, 1.0))
    return moves


def _sharded_runs(runs, shard_cols):
    out = []
    for half, col, ncol, width, sc in runs:
        while width > 0:
            d, local = divmod(ncol, shard_cols)
            wd = min(width, shard_cols - local)
            out.append((half, col, d, local, wd, sc))
            col, ncol, width = col + wd, ncol + wd, width - wd
    return out


def _remap(name, srcs, out_shapes, moves):
    rows = srcs[0].shape[-2]
    br = _pick(rows, COPY_ROWS, 16)
    ns = len(srcs)

    def spec(shape):
        if len(shape) == 2:
            return pl.BlockSpec((br, shape[1]), lambda i: (i, 0))
        return pl.BlockSpec((shape[0], br, shape[2]), lambda i: (0, i, 0))

    covered = [sum(m[6] for m in moves if m[0] == di) for di in range(len(out_shapes))]
    has_gaps = [cov < (shape[1] if len(shape) == 2 else shape[0] * shape[2])
                for cov, (shape, _) in zip(covered, out_shapes)]

    def body(*refs):
        s_refs, o_refs = refs[:ns], refs[ns:]
        for o, gaps in zip(o_refs, has_gaps):
            if gaps:
                o[...] = jnp.zeros_like(o)
        for di, dl, dc, si, sl, sc0, wd, scale in moves:
            v = s_refs[si][:, sc0:sc0 + wd] if sl is None else s_refs[si][sl, :, sc0:sc0 + wd]
            if scale != 1.0:
                v = v * jnp.asarray(scale, v.dtype)
            v = v.astype(o_refs[di].dtype)
            if dl is None:
                o_refs[di][:, dc:dc + wd] = v
            else:
                o_refs[di][dl, :, dc:dc + wd] = v

    return _call(name, body, (rows // br,), [(a, spec(a.shape)) for a in srcs],
                 [(_sds(shape, dt), spec(shape)) for shape, dt in out_shapes], sem=("parallel",))


def _w_in_from_shards(g3):
    n, rows, c = g3.shape
    moves = [(half, None, col, 0, d, local, wd, sc) for half, col, d, local, wd, sc in _sharded_runs(_w_in_runs(), c)]
    return _remap("w_in_repack", [g3], [((rows, F_W), g3.dtype), ((rows, R_W), g3.dtype)], moves)


def _w_in_grad_to_shards(d_fused, d_rest, n, c):
    rows = d_fused.shape[0]
    moves = [(0, d, local, half, None, col, wd, sc) for half, col, d, local, wd, sc in _sharded_runs(_w_in_runs(), c)]
    return _remap("w_in_grad_unpack", [d_fused, d_rest], [((n, rows, c), d_fused.dtype)], moves)[0]


def _rows_from_shards(name, land, own):
    n, rows, c = land.shape

    def body(land_ref, own_ref, o_ref):
        o_ref[...] = jnp.where(_my_id() == pl.program_id(0), own_ref[...], land_ref[...])

    return _call(name, body, (n,),
                 [(land, pl.BlockSpec((None, rows, c), lambda d: (d, 0, 0))), (own, _whole((rows, c)))],
                 [(_sds((n * rows, c), land.dtype), pl.BlockSpec((rows, c), lambda d: (d, 0)))], sem=("parallel",))[0]


def _cols_from_shards(name, land, own):
    n, rows, c = land.shape
    br = _pick(rows, COPY_ROWS, 16)

    def body(land_ref, own_ref, o_ref):
        me = _my_id()
        for d in range(n):
            o_ref[:, c * d:c * (d + 1)] = jnp.where(me == d, own_ref[...], land_ref[d])

    return _call(name, body, (rows // br,),
                 [(land, pl.BlockSpec((n, br, c), lambda i: (0, i, 0))), (own, _rows(br, c))],
                 [(_sds((rows, n * c), land.dtype), _rows(br, n * c))], sem=("parallel",))[0]


def _cols_to_shards(name, full, n):
    rows, nc = full.shape
    c = nc // n
    return _remap(name, [full], [((n, rows, c), full.dtype)], [(0, d, 0, 0, None, c * d, c, 1.0) for d in range(n)])[0]


class _NoComm:
    first_token = ()

    def late_weights(self, group, after):
        return {}

    def send(self, name, grads):
        return ()


def _local_step(x, tgt, p, comm=_NoComm()):
    seq = x.shape[0]
    r = -(-(N_META + seq) // ROW_ALIGN) * ROW_ALIGN
    cd = MXU_DTYPE
    p = dict(p)

    w_f, w_r = p["w_in"]

    pos = jnp.arange(r, dtype=F32)
    inv_freq = ROPE_THETA ** (-jnp.arange(HALF, dtype=F32) / HALF)
    ang = pos[:, None] * inv_freq[None, :]
    cos_t = jnp.tile(jnp.cos(ang), (1, HP // HALF))
    sin_t = jnp.tile(jnp.sin(ang), (1, HP // HALF))
    bf_row = jnp.zeros((1, HP), F32).at[0, LANE_FL:LANE_FL + HEADS].set(p["b_forget"])

    h0, h0b = _ln_emb_fwd(x, p["meta_tokens"], p["ln_emb_g"], p["ln_emb_b"], r, after=comm.first_token)
    proj_f = _matmul("in_proj_f", h0b, w_f, out_dtype=cd)
    proj_f = _remap("proj_f_pad", [proj_f], [((r, 3 * HW), cd)], _head_pad_moves(True))[0]
    proj_r = _matmul("in_proj_r", h0b, w_r)
    latent_gains = (p["q_norm_g"], p["kv_norm_g"])
    ql, kvl = _latent_norm_fwd(proj_r, latent_gains)
    p.update(comm.late_weights("qkv", ql))
    w_q = _pad_head_cols(p["w_q_up"], QK_DIM)
    w_kv = p["w_kv_up"]
    q_raw = _matmul("q_up", ql, w_q)
    kv = _matmul("kv_up", kvl, w_kv, out_dtype=cd)
    q_mla, k_mla, v_mla = _rope_fwd(q_raw, kv, proj_r, cos_t, sin_t)
    o_mla, o_mla_b, lse_mla = _attn_fwd("mla_fwd", (q_mla, 0), (k_mla, 0), (v_mla, 0))

    cum, cum_t = _forget_fwd(proj_r, bf_row)
    o_fox, o_fox_b, lse_fox = _attn_fwd("fox_fwd", (proj_f, 0), (proj_f, 1), (proj_f, 2), cum, cum_t)

    p.update(comm.late_weights("mix", o_fox_b))
    w_bm = _pad_head_rows(p["w_branch_mla"], V_DIM)
    w_bf = _pad_head_rows(p["w_branch_fox"], FOX_DIM)
    bm = _matmul("branch_mla", o_mla_b, p["w_branch_mla"], out_dtype=cd)
    bfx = _matmul("branch_fox", o_fox_b, p["w_branch_fox"], out_dtype=cd)
    merged = _gate_fwd(proj_r, p["b_gate"], bm, bfx)
    mix = _matmul("out_proj", merged, p["w_out"])
    h1, h1b = _ln_fwd("ln_mix_fwd", h0, mix, p["ln_mix_g"], p["ln_mix_b"])
    p.update(comm.late_weights("ffn", h1b))
    up = _matmul("ffn_up", h1b, p["w_ffn_up"], out_dtype=cd)
    act = _glu_fwd(up, p["conv_w"], p["conv_b"])
    f = _matmul("ffn_down", act, p["w_ffn_down"])

    g = {}
    dz2, dz2b, g["ln_ffn_g"], g["ln_ffn_b"], loss = _ln_ffn_bwd(h1, f, tgt, p["ln_ffn_g"], p["ln_ffn_b"])
    d_act = _matmul("ffn_down_dx", dz2b, p["w_ffn_down"], tb=True, out_dtype=cd)
    g["w_ffn_down"] = _matmul("ffn_down_dw", act, dz2b, ta=True, out_dtype=cd)
    d_up, dcw, g["conv_b"] = _glu_bwd(up, p["conv_w"], p["conv_b"], d_act)
    g["conv_w"] = dcw[:3]
    dh1 = _matmul("ffn_up_dx", d_up, p["w_ffn_up"], tb=True, addend=dz2, alpha=ALPHA)
    g["w_ffn_up"] = _matmul("ffn_up_dw", h1b, d_up, ta=True, out_dtype=cd)
    sent = comm.send("ffn", {n: g[n] for n in ("w_ffn_down", "w_ffn_up", "conv_w")})
    dz1, dz1b, g["ln_mix_g"], g["ln_mix_b"] = _ln_bwd("ln_mix_bwd", h0, mix, dh1, p["ln_mix_g"], after=sent)
    dmerged = _matmul("out_proj_dx", dz1b, p["w_out"], tb=True, out_dtype=cd)
    g["w_out"] = _matmul("out_proj_dw", merged, dz1b, ta=True, out_dtype=cd)
    d_bm, d_bf, d_gl, g["b_gate"] = _gate_bwd(proj_r, p["b_gate"], bm, bfx, dmerged)
    do_mla_b = _matmul("branch_mla_dx", d_bm, w_bm, tb=True, out_dtype=cd)
    g["w_branch_mla"] = _matmul("branch_mla_dw", o_mla_b, d_bm, ta=True, out_dtype=cd)
    do_fox_b = _matmul("branch_fox_dx", d_bf, w_bf, tb=True, out_dtype=cd)
    g["w_branch_fox"] = _matmul("branch_fox_dw", o_fox_b, d_bf, ta=True, out_dtype=cd)

    sent = comm.send("mix", {n: g[n] for n in ("w_out", "w_branch_mla", "w_branch_fox")})
    dq_m, dk_m, dv_m = _attn_bwd("mla_bwd", (q_mla, 0), (k_mla, 0), (v_mla, 0), do_mla_b, o_mla, lse_mla, after=sent)
    dfq, dfk, dfv, dcq, dck = _attn_bwd("fox_bwd", (proj_f, 0), (proj_f, 1), (proj_f, 2), do_fox_b, o_fox, lse_fox,
                                        cum, cum_t, out_dtype=cd)
    dfl, dbf = _forget_bwd(proj_r, bf_row, dcq, dck)
    g["b_forget"] = dbf[:, LANE_FL:LANE_FL + HEADS]

    dq_b, dkv_b, dlast = _rope_bwd(dq_m, dk_m, dv_m, dfl, cos_t, sin_t)
    d_ql = _matmul("q_up_dx", dq_b, w_q, tb=True)
    d_kvl = _matmul("kv_up_dx", dkv_b, w_kv, tb=True)
    d_qlat, d_kvlat, g["q_norm_g"], g["kv_norm_g"] = _latent_norm_bwd(proj_r, (d_ql, d_kvl), latent_gains)
    side_by_side = lambda parts, cols: [(0, None, c0, si, None, 0, a.shape[1], 1.0) for si, (a, c0) in enumerate(zip(parts, cols))]
    dproj_f = _remap("dproj_f_pack", [dfq, dfk, dfv], [((r, F_W), cd)], _head_pad_moves(False))[0]
    rest_parts = [d_qlat, d_kvlat, dlast, d_gl]
    dproj_r = _remap("dproj_r_pack", rest_parts, [((r, R_W), cd)],
                     side_by_side(rest_parts, (R_QLAT, R_KVLAT, R_LAST, R_GATE)))[0]
    g["w_in"] = (_matmul("in_proj_f_dw", h0b, dproj_f, ta=True, out_dtype=cd),
                 _matmul("in_proj_r_dw", h0b, dproj_r, ta=True, out_dtype=cd))
    sent = comm.send("in", {"w_in": g["w_in"]})
    dh0 = _matmul("in_proj_f_dx", dproj_f, w_f, tb=True, addend=dz1, alpha=ALPHA, after=sent)
    g["w_q_up"] = _unpad_head_cols(_matmul("q_up_dw", ql, dq_b, ta=True, out_dtype=cd, after=sent), QK_DIM)
    g["w_kv_up"] = _matmul("kv_up_dw", kvl, dkv_b, ta=True, out_dtype=cd, after=sent)
    sent = comm.send("qkv", {n: g[n] for n in ("w_q_up", "w_kv_up")})
    dh0 = _matmul("in_proj_r_dx", dproj_r, w_r, tb=True, addend=dh0, after=sent)
    grad_x, d_meta, g["ln_emb_g"], g["ln_emb_b"] = _ln_emb_bwd(x, p["meta_tokens"], dh0, p["ln_emb_g"])
    return loss, grad_x, d_meta, g


BIG = (("w_in", 1), ("w_q_up", 1), ("w_kv_up", 1), ("w_branch_mla", 1), ("w_branch_fox", 1), ("w_out", 0),
       ("w_ffn_up", 1), ("w_ffn_down", 0))
SMALL_SHARDED = (("meta_tokens", 1), ("conv_w", 1))
EARLY = ("w_in", "meta_tokens")
LATE = {"qkv": ("w_q_up", "w_kv_up", "conv_w"),
        "mix": ("w_branch_mla", "w_branch_fox", "w_out"),
        "ffn": ("w_ffn_up", "w_ffn_down")}
REPLICATED = ("ln_emb_g", "ln_emb_b", "b_gate", "b_forget", "q_norm_g", "kv_norm_g", "ln_mix_g", "ln_mix_b",
              "conv_b", "ln_ffn_g", "ln_ffn_b")
PACK_COLS = 1024


def _pack(flat_list):
    cat = jnp.concatenate(flat_list)
    n = cat.shape[0]
    rows = -(-n // (8 * PACK_COLS)) * 8
    return jnp.pad(cat, (0, rows * PACK_COLS - n)).reshape(rows, PACK_COLS)


def _gathered_full(g3, axis):
    n, r, c = g3.shape
    if axis == 0:
        return g3.reshape(n * r, c)
    return g3.transpose(1, 0, 2).reshape(r, n * c)


def _shard_major(full, axis):
    r, c = full.shape
    if axis == 0:
        return full.reshape(N_DEV, r // N_DEV, c)
    return full.reshape(r, N_DEV, c // N_DEV).transpose(1, 0, 2)


def kernel(x, meta_tokens, ln_emb_g, ln_emb_b, w_in, b_gate, b_forget, q_norm_g, w_q_up, kv_norm_g, w_kv_up, w_branch_mla, w_branch_fox, w_out, ln_mix_g, ln_mix_b, w_ffn_up, conv_w, conv_b, w_ffn_down, ln_ffn_g, ln_ffn_b, loss_target, m_meta_tokens, m_ln_emb_g, m_ln_emb_b, m_w_in, m_b_gate, m_b_forget, m_q_norm_g, m_w_q_up, m_kv_norm_g, m_w_kv_up, m_w_branch_mla, m_w_branch_fox, m_w_out, m_ln_mix_g, m_ln_mix_b, m_w_ffn_up, m_conv_w, m_conv_b, m_w_ffn_down, m_ln_ffn_g, m_ln_ffn_b, v_meta_tokens, v_ln_emb_g, v_ln_emb_b, v_w_in, v_b_gate, v_b_forget, v_q_norm_g, v_w_q_up, v_kv_norm_g, v_w_kv_up, v_w_branch_mla, v_w_branch_fox, v_w_out, v_ln_mix_g, v_ln_mix_b, v_w_ffn_up, v_conv_w, v_conv_b, v_w_ffn_down, v_ln_ffn_g, v_ln_ffn_b):
    names = ("meta_tokens", "ln_emb_g", "ln_emb_b", "w_in", "b_gate", "b_forget", "q_norm_g", "w_q_up", "kv_norm_g",
             "w_kv_up", "w_branch_mla", "w_branch_fox", "w_out", "ln_mix_g", "ln_mix_b", "w_ffn_up", "conv_w", "conv_b",
             "w_ffn_down", "ln_ffn_g", "ln_ffn_b")
    w_args = (meta_tokens, ln_emb_g, ln_emb_b, w_in, b_gate, b_forget, q_norm_g, w_q_up, kv_norm_g, w_kv_up,
              w_branch_mla, w_branch_fox, w_out, ln_mix_g, ln_mix_b, w_ffn_up, conv_w, conv_b, w_ffn_down, ln_ffn_g, ln_ffn_b)
    m_args = (m_meta_tokens, m_ln_emb_g, m_ln_emb_b, m_w_in, m_b_gate, m_b_forget, m_q_norm_g, m_w_q_up, m_kv_norm_g,
              m_w_kv_up, m_w_branch_mla, m_w_branch_fox, m_w_out, m_ln_mix_g, m_ln_mix_b, m_w_ffn_up, m_conv_w, m_conv_b,
              m_w_ffn_down, m_ln_ffn_g, m_ln_ffn_b)
    v_args = (v_meta_tokens, v_ln_emb_g, v_ln_emb_b, v_w_in, v_b_gate, v_b_forget, v_q_norm_g, v_w_q_up, v_kv_norm_g,
              v_w_kv_up, v_w_branch_mla, v_w_branch_fox, v_w_out, v_ln_mix_g, v_ln_mix_b, v_w_ffn_up, v_conv_w, v_conv_b,
              v_w_ffn_down, v_ln_ffn_g, v_ln_ffn_b)
    as2d = lambda a: a.reshape((-1, a.shape[-1])) if a.ndim != 1 else a.reshape(1, -1)
    w = {n: as2d(a) for n, a in zip(names, w_args)}
    m = {n: as2d(a) for n, a in zip(names, m_args)}
    v = {n: as2d(a) for n, a in zip(names, v_args)}
    out_shape = {n: a.shape for n, a in zip(names, w_args)}

    axis_of = dict(BIG + SMALL_SHARDED)
    big = set(n for n, _ in BIG)
    wire = lambda n, a: a.astype(MXU_DTYPE) if n in big else a
    my_id = _my_id()

    early = _allgather("gather_early", [wire(n, w[n]) for n in EARLY])
    p = {n: _gathered_full(g3, axis_of[n]) for n, g3 in zip(EARLY, early) if n != "w_in"}
    p["w_in"] = _w_in_from_shards(early[EARLY.index("w_in")])
    for n in REPLICATED:
        p[n] = w[n].reshape(-1)
    late_src = [[wire(n, w[n]) for n in members] for members in LATE.values()]
    late_handles, late_token = _push_start("gather_late_start", late_src, False, after=early[0])
    late = {group: (members, src, handle)
            for (group, members), src, handle in zip(LATE.items(), late_src, late_handles)}
    sent = {}

    class Comm:
        first_token = (late_token,)

        def late_weights(self, group, after):
            members, src, handle = late[group]
            lands = _push_wait("gather_" + group + "_wait", handle, after)
            out = {}
            for n, own, land in zip(members, src, lands):
                if own.shape[0] % 16:
                    out[n] = _gathered_full(lax.dynamic_update_index_in_dim(land, own, my_id, 0), axis_of[n])
                elif axis_of[n] == 1:
                    out[n] = _cols_from_shards(n + "_repack", land, own)
                else:
                    out[n] = _rows_from_shards(n + "_repack", land, own)
            return out

        def send(self, name, grads):
            names_ = tuple(grads)
            parts = []
            for n in names_:
                if n == "w_in":
                    parts.append(_w_in_grad_to_shards(*grads[n], N_DEV, w[n].shape[1]))
                elif n == "w_ffn_up":
                    parts.append(_cols_to_shards(n + "_grad_unpack", grads[n], N_DEV))
                else:
                    parts.append(_shard_major(grads[n], axis_of[n]).astype(MXU_DTYPE))
            (handle,), token = _push_start("send_" + name + "_start", [parts], True)
            sent[name] = (names_, parts, handle)
            return (token,)

    loss_part, grad_x, d_meta, g = _local_step(x[0], loss_target[0], p, Comm())
    grad_x = grad_x[None]

    small = _pack([d_meta.reshape(-1)] + [g[n].reshape(-1) for n in REPLICATED] + [loss_part.reshape(-1)])
    (small_handle,), small_token = _push_start("send_small_start", [[small]], False)

    res = {}
    prev = small_token
    for name, (names_, parts, handle) in sent.items():
        lands = _push_wait("send_" + name + "_wait", handle, prev)
        for n, part, land in zip(names_, parts, lands):
            own = lax.dynamic_index_in_dim(part, my_id, axis=0, keepdims=False)
            res[n] = _adamw("adamw_" + n, land, w[n], m[n], v[n], own=own)
            prev = res[n][0]
    small_all = _push_wait("send_small_wait", small_handle, prev)[0]
    head = jnp.zeros((d_meta.size,), F32)
    rep_w = _pack([head] + [w[n].reshape(-1) for n in REPLICATED])
    rep_m = _pack([head] + [m[n].reshape(-1) for n in REPLICATED])
    rep_v = _pack([head] + [v[n].reshape(-1) for n in REPLICATED])
    rep_res = _adamw("adamw_replicated", small_all, rep_w, rep_m, rep_v, own=small)
    off = d_meta.size
    for n in REPLICATED:
        sz = w[n].size
        res[n] = tuple(a.reshape(-1)[off:off + sz] for a in rep_res)
        off += sz
    loss = rep_res[0].reshape(-1)[off]
    cols = w["meta_tokens"].shape[1]
    meta_rows = lambda a: a.reshape(a.shape[:-2] + (-1,))[..., :d_meta.size].reshape(a.shape[:-2] + d_meta.shape)
    my_cols = lambda a: lax.dynamic_slice_in_dim(a, my_id * cols, cols, axis=a.ndim - 1)
    res["meta_tokens"] = _adamw("adamw_meta_tokens", my_cols(meta_rows(small_all)), w["meta_tokens"],
                                m["meta_tokens"], v["meta_tokens"], own=my_cols(d_meta))

    outs = [loss, grad_x]
    for idx in range(4):
        outs += [res[n][idx].reshape(out_shape[n]) for n in names]
    return tuple(outs)
```

```python
import jax
import jax.numpy as jnp
from jax import lax
from jax.experimental import pallas as pl
from jax.experimental.pallas import tpu as pltpu

F32 = jnp.float32
BF16 = jnp.bfloat16
MXU_DTYPE = BF16

N_DEV = 8
N_META = 16
D_MODEL = 1024
HEADS = 8
Q_RANK = 384
KV_RANK = 128
NOPE = 64
ROPE = 32
HALF = ROPE // 2
QK_DIM = NOPE + ROPE
V_DIM = 64
FOX_DIM = 64
FOX_W = HEADS * FOX_DIM
D_FF = 2816
ROPE_THETA = 10000.0
LN_EPS = 1e-5
RMS_EPS = 1e-6
ALPHA = 2.0 ** 0.25
MLA_SCALE = QK_DIM ** -0.5
FOX_SCALE = FOX_DIM ** -0.5
NEG_INF = -1e30

HP = 128
HW = HEADS * HP
F_W = 3 * FOX_W
R_GATE = 0
R_KVLAT = R_GATE + 2 * D_MODEL
R_LAST = R_KVLAT + KV_RANK
R_QLAT = R_LAST + HP
R_W = R_QLAT + Q_RANK
assert R_QLAT % Q_RANK == 0 and R_KVLAT % KV_RANK == 0 and R_GATE % D_MODEL == 0 and R_W % HP == 0
LANE_FL = 0
LANE_PE = NOPE

ADAM_LR = 0.001
ADAM_B1 = 0.9
ADAM_B2 = 0.999
ADAM_EPS = 1e-08
ADAM_WD = 0.01
ADAM_STEP = 10

ROW_BLOCK = 384
TOKEN_BLOCK = 256
ATT_TQ = 768
ATT_TK = 768
ATT_HEADS = 2
ATT_HEADS_FWD = 4
ROW_ALIGN = 768
MM_BLOCK_CAP = 1408
VMEM_LIMIT = 56 * 1024 * 1024
HIGHEST = lax.Precision.HIGHEST
NT = (((1,), (1,)), ((), ()))
TN = (((0,), (0,)), ((), ()))


def _params(sem=None):
    return pltpu.CompilerParams(dimension_semantics=sem, vmem_limit_bytes=VMEM_LIMIT)


def _call(name, body, grid, ins, outs, scratch=(), sem=None, after=()):
    n_in = len(ins)
    n_tok = len(after)

    def run(*refs):
        body(*refs[:n_in], *refs[n_in + n_tok:])

    tok_spec = pl.BlockSpec((8, 128), lambda *_: (0, 0))
    return pl.pallas_call(
        run, name=name, grid=grid,
        in_specs=[s for _, s in ins] + [tok_spec] * n_tok,
        out_specs=[s for _, s in outs],
        out_shape=[o for o, _ in outs],
        scratch_shapes=list(scratch),
        compiler_params=_params(sem),
    )(*[a for a, _ in ins], *after)


def _sds(shape, dtype):
    return jax.ShapeDtypeStruct(shape, dtype)


def _rows(br, c, cb=0):
    return pl.BlockSpec((br, c), lambda i: (i, cb))


def _whole(shape):
    n = len(shape)
    return pl.BlockSpec(shape, lambda i: (0,) * n)


def _pick(dim, cap, mult):
    best = None
    d = mult
    while d <= min(dim, cap):
        if dim % d == 0:
            best = d
        d += mult
    return best if best is not None else dim


def _hs(h):
    return slice(h * HP, (h + 1) * HP)


def _matmul(name, a, b, *, ta=False, tb=False, out_dtype=F32, addend=None, alpha=1.0, after=()):
    if ta:
        k, m = a.shape
    else:
        m, k = a.shape
    if tb:
        n, k2 = b.shape
    else:
        k2, n = b.shape
    assert k == k2, (name, a.shape, b.shape)
    bm = _pick(m, MM_BLOCK_CAP, 128 if ta else 16)
    bn = _pick(n, MM_BLOCK_CAP, 128)
    bk = _pick(k, MM_BLOCK_CAP, 128 if (not ta or tb) else 16)
    nk = k // bk
    dims = (((0 if ta else 1,), (1 if tb else 0,)), ((), ()))
    has_add = addend is not None

    def body(*refs):
        a_ref, b_ref = refs[:2]
        add_ref = refs[2] if has_add else None
        o_ref = refs[3 if has_add else 2]

        def finish(r):
            if has_add:
                r = r + alpha * add_ref[...]
            o_ref[...] = r.astype(o_ref.dtype)

        part = lax.dot_general(a_ref[...], b_ref[...], dims, preferred_element_type=F32)
        if nk == 1:
            finish(part)
            return
        acc_ref = refs[-1]
        kk = pl.program_id(2)

        @pl.when(kk == 0)
        def _():
            acc_ref[...] = part

        @pl.when(kk > 0)
        def _():
            acc_ref[...] += part

        @pl.when(kk == nk - 1)
        def _():
            finish(acc_ref[...])

    a_spec = pl.BlockSpec((bk, bm), lambda i, j, l: (l, i)) if ta else pl.BlockSpec((bm, bk), lambda i, j, l: (i, l))
    b_spec = pl.BlockSpec((bn, bk), lambda i, j, l: (j, l)) if tb else pl.BlockSpec((bk, bn), lambda i, j, l: (l, j))
    o_spec = pl.BlockSpec((bm, bn), lambda i, j, l: (i, j))
    ins = [(a, a_spec), (b, b_spec)]
    if has_add:
        ins.append((addend, o_spec))
    return _call(name, body, (m // bm, n // bn, nk), ins, [(_sds((m, n), out_dtype), o_spec)],
                 scratch=[pltpu.VMEM((bm, bn), F32)] if nk > 1 else [],
                 sem=("parallel", "parallel", "arbitrary"), after=after)[0]


def _ln_stats(z):
    mu = jnp.mean(z, axis=-1, keepdims=True)
    zc = z - mu
    var = jnp.mean(zc * zc, axis=-1, keepdims=True)
    rstd = lax.rsqrt(var + LN_EPS)
    return zc * rstd, rstd


def _ln_fwd(name, a, res, g, b, after=()):
    r, d = a.shape
    br = ROW_BLOCK
    has_res = res is not None

    def body(*refs):
        if has_res:
            a_ref, r_ref, g_ref, b_ref, y_ref, yb_ref = refs
            z = ALPHA * a_ref[...] + r_ref[...]
        else:
            a_ref, g_ref, b_ref, y_ref, yb_ref = refs
            z = a_ref[...]
        xhat, _ = _ln_stats(z)
        y = xhat * g_ref[...] + b_ref[...]
        y_ref[...] = y
        yb_ref[...] = y.astype(yb_ref.dtype)

    ins = [(a, _rows(br, d))]
    if has_res:
        ins.append((res, _rows(br, d)))
    ins += [(g.reshape(1, d), _whole((1, d))), (b.reshape(1, d), _whole((1, d)))]
    outs = [(_sds((r, d), F32), _rows(br, d)), (_sds((r, d), MXU_DTYPE), _rows(br, d))]
    return _call(name, body, (r // br,), ins, outs, sem=("parallel",), after=after)


def _ln_bwd(name, a, res, dy, g, after=()):
    r, d = a.shape
    br = ROW_BLOCK
    has_res = res is not None

    def body(*refs):
        if has_res:
            a_ref, r_ref, dy_ref, g_ref, dz_ref, dzb_ref, dg_ref, db_ref = refs
            z = ALPHA * a_ref[...] + r_ref[...]
        else:
            a_ref, dy_ref, g_ref, dz_ref, dzb_ref, dg_ref, db_ref = refs
            z = a_ref[...]
        xhat, rstd = _ln_stats(z)
        dyv = dy_ref[...]
        dyg = dyv * g_ref[...]
        m1 = jnp.mean(dyg, axis=-1, keepdims=True)
        m2 = jnp.mean(dyg * xhat, axis=-1, keepdims=True)
        dz = rstd * (dyg - m1 - xhat * m2)
        dz_ref[...] = dz
        dzb_ref[...] = dz.astype(dzb_ref.dtype)

        @pl.when(pl.program_id(0) == 0)
        def _():
            dg_ref[...] = jnp.zeros_like(dg_ref)
            db_ref[...] = jnp.zeros_like(db_ref)

        dg_ref[...] += jnp.sum(dyv * xhat, axis=0, keepdims=True)
        db_ref[...] += jnp.sum(dyv, axis=0, keepdims=True)

    ins = [(a, _rows(br, d))]
    if has_res:
        ins.append((res, _rows(br, d)))
    ins += [(dy, _rows(br, d)), (g.reshape(1, d), _whole((1, d)))]
    outs = [(_sds((r, d), F32), _rows(br, d)), (_sds((r, d), MXU_DTYPE), _rows(br, d)),
            (_sds((1, d), F32), _whole((1, d))), (_sds((1, d), F32), _whole((1, d)))]
    return _call(name, body, (r // br,), ins, outs, sem=("arbitrary",), after=after)


LATENTS = ((R_QLAT // Q_RANK, Q_RANK), (R_KVLAT // KV_RANK, KV_RANK))


def _latent_norm_fwd(proj_r, gains):
    r = proj_r.shape[0]
    br = ROW_BLOCK

    def body(xq_ref, xk_ref, gq_ref, gk_ref, yq_ref, yk_ref):
        for x_ref, g_ref, y_ref in ((xq_ref, gq_ref, yq_ref), (xk_ref, gk_ref, yk_ref)):
            x = x_ref[...]
            rstd = lax.rsqrt(jnp.mean(x * x, axis=-1, keepdims=True) + RMS_EPS)
            y_ref[...] = (x * rstd * g_ref[...]).astype(y_ref.dtype)

    return _call("latent_norm_fwd", body, (r // br,),
                 [(proj_r, _rows(br, wd, cb)) for cb, wd in LATENTS]
                 + [(g.reshape(1, wd), _whole((1, wd))) for g, (_, wd) in zip(gains, LATENTS)],
                 [(_sds((r, wd), MXU_DTYPE), _rows(br, wd)) for _, wd in LATENTS], sem=("parallel",))


def _latent_norm_bwd(proj_r, dys, gains):
    r = proj_r.shape[0]
    br = ROW_BLOCK

    def body(xq_ref, xk_ref, dq_ref, dk_ref, gq_ref, gk_ref, oq_ref, ok_ref, dgq_ref, dgk_ref):
        @pl.when(pl.program_id(0) == 0)
        def _():
            dgq_ref[...] = jnp.zeros_like(dgq_ref)
            dgk_ref[...] = jnp.zeros_like(dgk_ref)

        for x_ref, dy_ref, g_ref, dx_ref, dg_ref in ((xq_ref, dq_ref, gq_ref, oq_ref, dgq_ref),
                                                     (xk_ref, dk_ref, gk_ref, ok_ref, dgk_ref)):
            x = x_ref[...]
            rstd = lax.rsqrt(jnp.mean(x * x, axis=-1, keepdims=True) + RMS_EPS)
            nrm = x * rstd
            dyv = dy_ref[...]
            dyg = dyv * g_ref[...]
            dx_ref[...] = (rstd * (dyg - nrm * jnp.mean(dyg * nrm, axis=-1, keepdims=True))).astype(dx_ref.dtype)
            dg_ref[...] += jnp.sum(dyv * nrm, axis=0, keepdims=True)

    return _call("latent_norm_bwd", body, (r // br,),
                 [(proj_r, _rows(br, wd, cb)) for cb, wd in LATENTS]
                 + [(dy, _rows(br, wd)) for dy, (_, wd) in zip(dys, LATENTS)]
                 + [(g.reshape(1, wd), _whole((1, wd))) for g, (_, wd) in zip(gains, LATENTS)],
                 [(_sds((r, wd), MXU_DTYPE), _rows(br, wd)) for _, wd in LATENTS]
                 + [(_sds((1, wd), F32), _whole((1, wd))) for _, wd in LATENTS], sem=("arbitrary",))


def _lane_iota(shape):
    return lax.broadcasted_iota(jnp.int32, shape, 1)


def _rotary(t, c, s, lane, sign):
    second = pltpu.roll(t, HP - HALF, axis=1)
    first = pltpu.roll(t, HALF, axis=1)
    lo = (lane >= LANE_PE) & (lane < LANE_PE + HALF)
    hi = (lane >= LANE_PE + HALF) & (lane < LANE_PE + ROPE)
    return jnp.where(lo, t * c - sign * second * s, jnp.where(hi, t * c + sign * first * s, t))


def _rope_fwd(q_raw, kv, proj_r, cos_t, sin_t):
    r = q_raw.shape[0]
    br = ROW_BLOCK

    def body(q_ref, kv_ref, t_ref, c_ref, s_ref, qo_ref, ko_ref, vo_ref):
        c = c_ref[...]
        s = s_ref[...]
        lane = _lane_iota((br, HP))
        pe = (lane >= LANE_PE) & (lane < LANE_PE + ROPE)
        left = lane < NOPE
        kp = jnp.where(pe, _rotary(t_ref[...], c, s, lane, 1.0), 0.0)
        for h in range(HEADS):
            qo_ref[:, _hs(h)] = (_rotary(q_ref[:, _hs(h)], c, s, lane, 1.0) * MLA_SCALE).astype(qo_ref.dtype)
            t = kv_ref[:, _hs(h)].astype(F32)
            ko_ref[:, _hs(h)] = (jnp.where(left, t, 0.0) + kp).astype(ko_ref.dtype)
            vo_ref[:, _hs(h)] = jnp.where(left, pltpu.roll(t, HP - NOPE, axis=1), 0.0).astype(vo_ref.dtype)

    blk = _rows(br, HP)
    wide = _rows(br, HW)
    return _call("rope_fwd", body, (r // br,),
                 [(q_raw, wide), (kv, wide), (proj_r, _rows(br, HP, R_LAST // HP)), (cos_t, blk), (sin_t, blk)],
                 [(_sds((r, HW), MXU_DTYPE), wide)] * 3, sem=("parallel",))


def _rope_bwd(dq, dk, dv, dfl, cos_t, sin_t):
    r = dq.shape[0]
    br = ROW_BLOCK

    def body(dq_ref, dk_ref, dv_ref, fl_ref, c_ref, s_ref, dqo_ref, dkv_ref, dl_ref):
        c = c_ref[...]
        s = s_ref[...]
        lane = _lane_iota((br, HP))
        pe = (lane >= LANE_PE) & (lane < LANE_PE + ROPE)
        left = lane < NOPE
        acc = jnp.zeros((br, HP), F32)
        for h in range(HEADS):
            dqo_ref[:, _hs(h)] = (_rotary(dq_ref[:, _hs(h)], c, s, lane, -1.0) * MLA_SCALE).astype(dqo_ref.dtype)
            dkh = dk_ref[:, _hs(h)]
            acc = acc + dkh
            dkv_ref[:, _hs(h)] = jnp.where(left, dkh, pltpu.roll(dv_ref[:, _hs(h)], NOPE, axis=1)).astype(dkv_ref.dtype)
        dl_ref[...] = (jnp.where(pe, _rotary(acc, c, s, lane, -1.0), 0.0) + fl_ref[...]).astype(dl_ref.dtype)

    blk = _rows(br, HP)
    wide = _rows(br, HW)
    return _call("rope_bwd", body, (r // br,),
                 [(dq, wide), (dk, wide), (dv, wide), (dfl, blk), (cos_t, blk), (sin_t, blk)],
                 [(_sds((r, HW), MXU_DTYPE), wide), (_sds((r, HW), MXU_DTYPE), wide), (_sds((r, HP), MXU_DTYPE), blk)],
                 sem=("parallel",))


def _log_sigmoid(x):
    return jnp.minimum(x, 0.0) - jnp.log(1.0 + jnp.exp(-jnp.abs(x)))


def _head_lane(x, h, lane):
    return jnp.sum(jnp.where(lane == h, x, 0.0), axis=1, keepdims=True)


def _forget_fwd(proj_r, bf_row):
    r = proj_r.shape[0]
    br = ROW_BLOCK

    def body(t_ref, b_ref, ob_ref, ot_ref, carry_ref):
        @pl.when(pl.program_id(0) == 0)
        def _():
            carry_ref[...] = jnp.zeros_like(carry_ref)

        x = t_ref[...] + b_ref[...]
        lane = _lane_iota(x.shape)
        lf = jnp.where((lane >= LANE_FL) & (lane < LANE_FL + HEADS), _log_sigmoid(x), 0.0)
        tri = (lax.broadcasted_iota(jnp.int32, (br, br), 0) >= lax.broadcasted_iota(jnp.int32, (br, br), 1)).astype(F32)
        cum = jnp.dot(tri, lf, precision=HIGHEST, preferred_element_type=F32) + carry_ref[0:1, :]
        for h in range(HEADS):
            ob_ref[:, _hs(h)] = jnp.broadcast_to(_head_lane(cum, LANE_FL + h, lane), (br, HP))
        ot_ref[...] = cum.T[LANE_FL:LANE_FL + HEADS, :]
        carry_ref[...] = jnp.broadcast_to(cum[br - 1:br, :], carry_ref.shape)

    return _call("forget_fwd", body, (r // br,),
                 [(proj_r, _rows(br, HP, R_LAST // HP)), (bf_row, _whole((1, HP)))],
                 [(_sds((r, HW), F32), _rows(br, HW)), (_sds((HEADS, r), F32), pl.BlockSpec((HEADS, br), lambda i: (0, i)))],
                 scratch=[pltpu.VMEM((8, HP), F32)], sem=("arbitrary",))


def _forget_bwd(proj_r, bf_row, dcq_t, dck_b):
    r = proj_r.shape[0]
    br = ROW_BLOCK
    nb = r // br

    def body(t_ref, b_ref, dcq_ref, dck_ref, o_ref, db_ref, carry_ref):
        @pl.when(pl.program_id(0) == 0)
        def _():
            carry_ref[...] = jnp.zeros_like(carry_ref)
            db_ref[...] = jnp.zeros_like(db_ref)

        lane = _lane_iota((br, HP))
        dc = jnp.concatenate([dcq_ref[...], jnp.zeros((HP - HEADS, br), F32)], axis=0).T
        for h in range(HEADS):
            dc = dc + jnp.where(lane == LANE_FL + h, dck_ref[:, h * HP:h * HP + 1], 0.0)
        triu = (lax.broadcasted_iota(jnp.int32, (br, br), 0) <= lax.broadcasted_iota(jnp.int32, (br, br), 1)).astype(F32)
        dlf = jnp.dot(triu, dc, precision=HIGHEST, preferred_element_type=F32) + carry_ref[0:1, :]
        carry_ref[...] = jnp.broadcast_to(dlf[0:1, :], carry_ref.shape)
        x = t_ref[...] + b_ref[...]
        dfl = jnp.where((lane >= LANE_FL) & (lane < LANE_FL + HEADS), dlf * jax.nn.sigmoid(-x), 0.0)
        o_ref[...] = dfl
        db_ref[...] += jnp.sum(dfl, axis=0, keepdims=True)

    rev = pl.BlockSpec((br, HP), lambda i: (nb - 1 - i, 0))
    return _call("forget_bwd", body, (nb,),
                 [(proj_r, pl.BlockSpec((br, HP), lambda i: (nb - 1 - i, R_LAST // HP))), (bf_row, _whole((1, HP))),
                  (dcq_t, pl.BlockSpec((HEADS, br), lambda i: (0, nb - 1 - i))),
                  (dck_b, pl.BlockSpec((br, HW), lambda i: (nb - 1 - i, 0)))],
                 [(_sds((r, HP), F32), rev), (_sds((1, HP), F32), _whole((1, HP)))],
                 scratch=[pltpu.VMEM((8, HP), F32)], sem=("arbitrary",))


def _gate_fwd(proj_r, b_gate, bm, bfx):
    r, d = bm.shape
    br = ROW_BLOCK
    cb = R_GATE // d

    def body(gm_ref, gf_ref, b1_ref, b2_ref, bm_ref, bf_ref, o_ref):
        g1 = jax.nn.sigmoid(gm_ref[...] + b1_ref[...])
        g2 = jax.nn.sigmoid(gf_ref[...] + b2_ref[...])
        o_ref[...] = (g1 * bm_ref[...].astype(F32) + g2 * bf_ref[...].astype(F32)).astype(o_ref.dtype)

    b1 = b_gate[:d].reshape(1, d)
    b2 = b_gate[d:].reshape(1, d)
    return _call("gate_fwd", body, (r // br,),
                 [(proj_r, _rows(br, d, cb)), (proj_r, _rows(br, d, cb + 1)), (b1, _whole((1, d))), (b2, _whole((1, d))),
                  (bm, _rows(br, d)), (bfx, _rows(br, d))],
                 [(_sds((r, d), MXU_DTYPE), _rows(br, d))], sem=("parallel",))[0]


def _gate_bwd(proj_r, b_gate, bm, bfx, dmerged):
    r, d = bm.shape
    br = ROW_BLOCK
    cb = R_GATE // d

    def body(gm_ref, gf_ref, b1_ref, b2_ref, bm_ref, bf_ref, dm_ref, dbm_ref, dbf_ref, dgl_ref, dbg_ref):
        g1 = jax.nn.sigmoid(gm_ref[...] + b1_ref[...])
        g2 = jax.nn.sigmoid(gf_ref[...] + b2_ref[...])
        dm = dm_ref[...].astype(F32)
        dbm_ref[...] = (dm * g1).astype(dbm_ref.dtype)
        dbf_ref[...] = (dm * g2).astype(dbf_ref.dtype)
        dl1 = dm * bm_ref[...].astype(F32) * (g1 * (1.0 - g1))
        dl2 = dm * bf_ref[...].astype(F32) * (g2 * (1.0 - g2))
        dgl_ref[:, 0:d] = dl1.astype(dgl_ref.dtype)
        dgl_ref[:, d:2 * d] = dl2.astype(dgl_ref.dtype)

        @pl.when(pl.program_id(0) == 0)
        def _():
            dbg_ref[...] = jnp.zeros_like(dbg_ref)

        dbg_ref[:, 0:d] += jnp.sum(dl1, axis=0, keepdims=True)
        dbg_ref[:, d:2 * d] += jnp.sum(dl2, axis=0, keepdims=True)

    b1 = b_gate[:d].reshape(1, d)
    b2 = b_gate[d:].reshape(1, d)
    return _call("gate_bwd", body, (r // br,),
                 [(proj_r, _rows(br, d, cb)), (proj_r, _rows(br, d, cb + 1)), (b1, _whole((1, d))), (b2, _whole((1, d))),
                  (bm, _rows(br, d)), (bfx, _rows(br, d)), (dmerged, _rows(br, d))],
                 [(_sds((r, d), MXU_DTYPE), _rows(br, d)), (_sds((r, d), MXU_DTYPE), _rows(br, d)),
                  (_sds((r, 2 * d), MXU_DTYPE), _rows(br, 2 * d)), (_sds((1, 2 * d), F32), _whole((1, 2 * d)))],
                 sem=("arbitrary",))


HALO = 16
GLU_BWD_BLOCK = 256
COPY_ROWS = 512


def _conv_taps(gp, halo, first_block):
    halo = jnp.where(first_block, 0.0, halo.astype(F32))
    rid = lax.broadcasted_iota(jnp.int32, gp.shape, 0)
    last, prev = halo[HALO - 1:HALO, :], halo[HALO - 2:HALO - 1, :]
    g1 = jnp.where(rid == 0, last, pltpu.roll(gp, 1, axis=0))
    g2 = jnp.where(rid == 0, prev, jnp.where(rid == 1, last, pltpu.roll(gp, 2, axis=0)))
    return g1, g2


def _prev_halo(br, c):
    return pl.BlockSpec((HALO, c), lambda i: (jnp.maximum(i * (br // HALO) - 1, 0), 0))


def _glu_fwd(up, conv_w, conv_b):
    r = up.shape[0]
    c = D_FF
    br = ROW_BLOCK

    def body(gp_ref, halo_ref, val_ref, w_ref, b_ref, o_ref):
        gp = gp_ref[...].astype(F32)
        g1, g2 = _conv_taps(gp, halo_ref[...], pl.program_id(0) == 0)
        gate = w_ref[0:1, :] * g2 + w_ref[1:2, :] * g1 + w_ref[2:3, :] * gp + b_ref[...]
        o_ref[...] = (gate * jax.nn.sigmoid(gate) * val_ref[...].astype(F32)).astype(o_ref.dtype)

    return _call("glu_fwd", body, (r // br,),
                 [(up, _rows(br, c, 0)), (up, _prev_halo(br, c)), (up, _rows(br, c, 1)),
                  (conv_w, _whole((3, c))), (conv_b.reshape(1, c), _whole((1, c)))],
                 [(_sds((r, c), MXU_DTYPE), _rows(br, c))], sem=("parallel",))[0]


def _glu_bwd(up, conv_w, conv_b, d_act):
    r = up.shape[0]
    c = D_FF
    br = GLU_BWD_BLOCK
    nb = r // br

    def body(gp_ref, halo_ref, val_ref, da_ref, gpn_ref, valn_ref, dan_ref, w_ref, b_ref, o_ref, dw_ref, db_ref):
        i = pl.program_id(0)
        w0, w1, w2, bias = w_ref[0:1, :], w_ref[1:2, :], w_ref[2:3, :], b_ref[...]

        def d_gate(gp, g1, g2, val, da):
            gate = w0 * g2 + w1 * g1 + w2 * gp + bias
            sg = jax.nn.sigmoid(gate)
            return da * val * (sg * (1.0 + gate * (1.0 - sg))), da * (gate * sg)

        gp = gp_ref[...].astype(F32)
        g1, g2 = _conv_taps(gp, halo_ref[...], i == 0)
        dg, dv = d_gate(gp, g1, g2, val_ref[...].astype(F32), da_ref[...].astype(F32))
        gpn = gpn_ref[...].astype(F32)
        g1n, g2n = _conv_taps(gpn, gp[br - HALO:, :], False)
        dgn, _ = d_gate(gpn, g1n, g2n, valn_ref[...].astype(F32), dan_ref[...].astype(F32))
        dgn = jnp.where(i == nb - 1, 0.0, dgn)
        rid = lax.broadcasted_iota(jnp.int32, dg.shape, 0)
        u1 = jnp.where(rid == br - 1, dgn[0:1, :], pltpu.roll(dg, br - 1, axis=0))
        u2 = jnp.where(rid == br - 1, dgn[1:2, :], jnp.where(rid == br - 2, dgn[0:1, :], pltpu.roll(dg, br - 2, axis=0)))
        o_ref[:, 0:c] = (w2 * dg + w1 * u1 + w0 * u2).astype(o_ref.dtype)
        o_ref[:, c:2 * c] = dv.astype(o_ref.dtype)

        @pl.when(i == 0)
        def _():
            dw_ref[...] = jnp.zeros_like(dw_ref)
            db_ref[...] = jnp.zeros_like(db_ref)

        dw_ref[0:1, :] += jnp.sum(dg * g2, axis=0, keepdims=True)
        dw_ref[1:2, :] += jnp.sum(dg * g1, axis=0, keepdims=True)
        dw_ref[2:3, :] += jnp.sum(dg * gp, axis=0, keepdims=True)
        db_ref[...] += jnp.sum(dg, axis=0, keepdims=True)

    nxt = lambda cb: pl.BlockSpec((HALO, c), lambda i: (jnp.minimum((i + 1) * (br // HALO), r // HALO - 1), cb))
    return _call("glu_bwd", body, (nb,),
                 [(up, _rows(br, c, 0)), (up, _prev_halo(br, c)), (up, _rows(br, c, 1)), (d_act, _rows(br, c)),
                  (up, nxt(0)), (up, nxt(1)), (d_act, nxt(0)),
                  (conv_w, _whole((3, c))), (conv_b.reshape(1, c), _whole((1, c)))],
                 [(_sds((r, 2 * c), MXU_DTYPE), _rows(br, 2 * c)),
                  (_sds((8, c), F32), _whole((8, c))), (_sds((1, c), F32), _whole((1, c)))],
                 sem=("arbitrary",))


def _token_specs(seq, d):
    br = TOKEN_BLOCK
    nxb = seq // br
    main = pl.BlockSpec((br, d), lambda i: (jnp.minimum(i, nxb - 1), 0))
    tail = pl.BlockSpec((N_META, d), lambda i: (jnp.clip(i * (br // N_META) - 1, 0, seq // N_META - 1), 0))
    return main, tail


def _padded_block(main_ref, tail_ref, first, seq):
    br = TOKEN_BLOCK
    i = pl.program_id(0)
    nxb = seq // br
    main = jnp.where(i < nxb, main_ref[...], 0.0)
    head = jnp.where(i == 0, first, jnp.where(i <= nxb, tail_ref[...], 0.0))
    return jnp.concatenate([head, main[:br - N_META]], axis=0)


def _ln_emb_fwd(x, meta, g, b, rows, after=()):
    seq, d = x.shape
    br = TOKEN_BLOCK
    assert seq % br == 0 and br % N_META == 0 and rows % br == 0

    def body(x_ref, tail_ref, meta_ref, g_ref, b_ref, y_ref, yb_ref):
        z = _padded_block(x_ref, tail_ref, meta_ref[...], seq)
        xhat, _ = _ln_stats(z)
        y = xhat * g_ref[...] + b_ref[...]
        y_ref[...] = y
        yb_ref[...] = y.astype(yb_ref.dtype)

    main, tail = _token_specs(seq, d)
    return _call("ln_emb_fwd", body, (rows // br,),
                 [(x, main), (x, tail), (meta, _whole((N_META, d))), (g.reshape(1, d), _whole((1, d))),
                  (b.reshape(1, d), _whole((1, d)))],
                 [(_sds((rows, d), F32), _rows(br, d)), (_sds((rows, d), MXU_DTYPE), _rows(br, d))],
                 sem=("parallel",), after=after)


def _ln_emb_bwd(x, meta, dh0, g):
    seq, d = x.shape
    br = TOKEN_BLOCK
    step = br // N_META

    def ln_bwd(z, dy, gv):
        xhat, rstd = _ln_stats(z)
        dyg = dy * gv
        m1 = jnp.mean(dyg, axis=-1, keepdims=True)
        m2 = jnp.mean(dyg * xhat, axis=-1, keepdims=True)
        dz = rstd * (dyg - m1 - xhat * m2)
        return dz, jnp.sum(dy * xhat, axis=0, keepdims=True), jnp.sum(dy, axis=0, keepdims=True)

    def body(x_ref, dh_ref, nxt_ref, meta_ref, top_ref, g_ref, dx_ref, dm_ref, dg_ref, db_ref):
        gv = g_ref[...]
        dy = jnp.concatenate([dh_ref[N_META:, :], nxt_ref[...]], axis=0)
        dz, dg, db = ln_bwd(x_ref[...], dy, gv)
        dx_ref[...] = dz

        @pl.when(pl.program_id(0) == 0)
        def _():
            dzm, dgm, dbm = ln_bwd(meta_ref[...], top_ref[...], gv)
            dm_ref[...] = dzm
            dg_ref[...] = dgm
            db_ref[...] = dbm

        dg_ref[...] += dg
        db_ref[...] += db

    small = _whole((N_META, d))
    return _call("ln_emb_bwd", body, (seq // br,),
                 [(x, _rows(br, d)), (dh0, _rows(br, d)), (dh0, pl.BlockSpec((N_META, d), lambda i: ((i + 1) * step, 0))),
                  (meta, small), (dh0, small), (g.reshape(1, d), _whole((1, d)))],
                 [(_sds((seq, d), F32), _rows(br, d)), (_sds((N_META, d), F32), small),
                  (_sds((1, d), F32), _whole((1, d))), (_sds((1, d), F32), _whole((1, d)))], sem=("arbitrary",))


def _loss_err(a_ref, r_ref, t_ref, tail_ref, g_ref, b_ref, seq):
    br, d = a_ref.shape
    xhat, rstd = _ln_stats(ALPHA * a_ref[...] + r_ref[...])
    y = xhat * g_ref[...] + b_ref[...]
    t = _padded_block(t_ref, tail_ref, jnp.zeros((N_META, d), F32), seq)
    rid = lax.broadcasted_iota(jnp.int32, (br, d), 0) + pl.program_id(0) * br
    valid = (rid >= N_META) & (rid < N_META + seq)
    return jnp.where(valid, y - t, 0.0), xhat, rstd


def _ln_ffn_bwd(h1, f, tgt, g, b):
    r, d = h1.shape
    seq = tgt.shape[0]
    br = TOKEN_BLOCK

    def body(a_ref, r_ref, t_ref, tail_ref, g_ref, b_ref, dz_ref, dzb_ref, dg_ref, db_ref, l_ref):
        err, xhat, rstd = _loss_err(a_ref, r_ref, t_ref, tail_ref, g_ref, b_ref, seq)
        dyv = err * (1.0 / d)
        dyg = dyv * g_ref[...]
        m1 = jnp.mean(dyg, axis=-1, keepdims=True)
        m2 = jnp.mean(dyg * xhat, axis=-1, keepdims=True)
        dz = rstd * (dyg - m1 - xhat * m2)
        dz_ref[...] = dz
        dzb_ref[...] = dz.astype(dzb_ref.dtype)

        @pl.when(pl.program_id(0) == 0)
        def _():
            dg_ref[...] = jnp.zeros_like(dg_ref)
            db_ref[...] = jnp.zeros_like(db_ref)
            l_ref[...] = jnp.zeros_like(l_ref)

        dg_ref[...] += jnp.sum(dyv * xhat, axis=0, keepdims=True)
        db_ref[...] += jnp.sum(dyv, axis=0, keepdims=True)
        l_ref[...] += jnp.sum(jnp.sum(err * err, axis=1, keepdims=True), axis=0, keepdims=True) * (0.5 / d)

    main, tail = _token_specs(seq, d)
    return _call("ln_ffn_bwd", body, (r // br,),
                 [(h1, _rows(br, d)), (f, _rows(br, d)), (tgt, main), (tgt, tail),
                  (g.reshape(1, d), _whole((1, d))), (b.reshape(1, d), _whole((1, d)))],
                 [(_sds((r, d), F32), _rows(br, d)), (_sds((r, d), MXU_DTYPE), _rows(br, d)),
                  (_sds((1, d), F32), _whole((1, d))), (_sds((1, d), F32), _whole((1, d))),
                  (_sds((1, 1), F32), _whole((1, 1)))], sem=("arbitrary",))


def _attn_fwd(name, q, k, v, cum_b=None, cum_t=None):
    (qa, qg), (ka, kg), (va, vg) = q, k, v
    r = qa.shape[0]
    tq, tk = ATT_TQ, ATT_TK
    nq, nk = r // tq, r // tk
    bias = cum_b is not None

    def body(*refs):
        if bias:
            q_ref, k_ref, vt_ref, cb_ref, ct_ref, o_ref, ob_ref, lse_ref = refs
        else:
            q_ref, k_ref, vt_ref, o_ref, ob_ref, lse_ref = refs
        i = pl.program_id(1)
        qs = [q_ref[:, _hs(hh)] for hh in range(hg)]
        cqs = [ct_ref[hh] for hh in range(hg)] if bias else None
        diff = lax.broadcasted_iota(jnp.int32, (tk, tq), 0) - lax.broadcasted_iota(jnp.int32, (tk, tq), 1)

        def step(j, carry, masked):
            keys = pl.ds(pl.multiple_of(j * tk, tk), tk)
            out = []
            for hh in range(hg):
                m, l, acc = carry[hh]
                kt = k_ref[keys, _hs(hh)]
                s = lax.dot_general(kt, qs[hh], NT, preferred_element_type=F32)
                if bias:
                    s = s + (cqs[hh] - cb_ref[keys, hh * HP:hh * HP + 1])
                if masked:
                    s = jnp.where(diff <= i * tq - j * tk, s, NEG_INF)
                m_new = jnp.maximum(m, jnp.max(s, axis=0, keepdims=True))
                p = jnp.exp(s - m_new)
                a = jnp.exp(m - m_new)
                l = a * l + jnp.sum(p, axis=0, keepdims=True)
                acc = a * acc + jnp.dot(vt_ref[j, _hs(hh), :], p.astype(kt.dtype), preferred_element_type=F32)
                out.append((m_new, l, acc))
            return tuple(out)

        n_clear = (i * tq + 1) // tk
        n_all = ((i + 1) * tq - 1) // tk + 1
        carry = tuple((jnp.full((1, tq), NEG_INF, F32), jnp.zeros((1, tq), F32), jnp.zeros((HP, tq), F32))
                      for _ in range(hg))
        carry = lax.fori_loop(0, n_clear, lambda j, c: step(j, c, False), carry)
        carry = lax.fori_loop(n_clear, n_all, lambda j, c: step(j, c, True), carry)
        for hh in range(hg):
            m, l, acc = carry[hh]
            o = (acc / l).T
            o_ref[:, _hs(hh)] = o
            ob_ref[:, hh * V_DIM:(hh + 1) * V_DIM] = o[:, :V_DIM].astype(ob_ref.dtype)
            lse_ref[hh] = m + jnp.log(l)

    hg = ATT_HEADS_FWD
    w = hg * HP
    gpw = HW // w
    tile = lambda g: pl.BlockSpec((tq, w), lambda h, i: (i, g * gpw + h))
    res = lambda g: pl.BlockSpec((r, w), lambda h, i: (0, g * gpw + h))
    v_t = _key_tiles_transposed(name + "_vt", va, vg)
    ins = [(qa, tile(qg)), (ka, res(kg)), (v_t, pl.BlockSpec((nk, w, tk), lambda h, i: (0, h, 0)))]
    if bias:
        ins += [(cum_b, res(0)),
                (cum_t.reshape(HEADS, nq, 1, tq), pl.BlockSpec((hg, None, 1, tq), lambda h, i: (h, i, 0, 0)))]
    outs = [(_sds((r, HW), F32), tile(0)),
            (_sds((r, HEADS * V_DIM), MXU_DTYPE), pl.BlockSpec((tq, hg * V_DIM), lambda h, i: (i, h))),
            (_sds((HEADS, nq, 1, tq), F32), pl.BlockSpec((hg, None, 1, tq), lambda h, i: (h, i, 0, 0)))]
    o, ob, lse = _call(name, body, (gpw, nq), ins, outs, sem=("parallel", "parallel"))
    return o, ob, lse.reshape(HEADS, r)


def _key_tiles_transposed(name, a, group):
    r = a.shape[0]
    tk = ATT_TK

    def body(x_ref, o_ref):
        for h in range(HEADS):
            o_ref[_hs(h), :] = x_ref[:, _hs(h)].astype(F32).T.astype(o_ref.dtype)

    return _call(name, body, (r // tk,),
                 [(a, pl.BlockSpec((tk, HW), lambda j: (j, group)))],
                 [(_sds((r // tk, HW, tk), a.dtype), pl.BlockSpec((None, HW, tk), lambda j: (j, 0, 0)))],
                 sem=("parallel",))[0]


def _attn_bwd(name, q, k, v, do_b, o, lse_t, cum_b=None, cum_t=None, out_dtype=F32, after=()):
    (qa, qg), (ka, kg), (va, vg) = q, k, v
    r = qa.shape[0]
    tq, tk = ATT_TQ, ATT_TK
    nq, nk = r // tq, r // tk
    bias = cum_b is not None

    def body(*refs):
        if bias:
            (q_ref, k_ref, v_ref, do_ref, o_ref, lse_ref, cb_ref, ct_ref,
             dq_ref, dk_ref, dv_ref, dcq_ref, dck_ref, dqt_ref, dl_ref) = refs
        else:
            q_ref, k_ref, v_ref, do_ref, o_ref, lse_ref, dq_ref, dk_ref, dv_ref, dqt_ref, dl_ref = refs
        j = pl.program_id(1)

        @pl.when(j == 0)
        def _():
            dqt_ref[...] = jnp.zeros_like(dqt_ref)
            if bias:
                dcq_ref[...] = jnp.zeros_like(dcq_ref)
            for hh in range(hg):
                for i in range(nq):
                    rows = slice(i * tq, (i + 1) * tq)
                    prod = do_ref[rows, _hs(hh)].astype(F32) * o_ref[rows, _hs(hh)]
                    dl_ref[hh, i] = jnp.sum(prod.T, axis=0, keepdims=True)

        kts = [k_ref[:, _hs(hh)] for hh in range(hg)]
        vts = [v_ref[:, _hs(hh)] for hh in range(hg)]
        k_trs = [kt.astype(F32).T.astype(kt.dtype) for kt in kts]
        cks = [cb_ref[:, hh * HP:hh * HP + 1] for hh in range(hg)] if bias else None
        diff = lax.broadcasted_iota(jnp.int32, (tk, tq), 0) - lax.broadcasted_iota(jnp.int32, (tk, tq), 1)

        def step(i, carry, masked):
            rows = pl.ds(pl.multiple_of(i * tq, tq), tq)
            out = []
            for hh in range(hg):
                dk_acc, dv_acc, dck_acc = carry[hh]
                qt = q_ref[rows, _hs(hh)]
                dot = do_ref[rows, _hs(hh)]
                s = lax.dot_general(kts[hh], qt, NT, preferred_element_type=F32)
                if bias:
                    s = s + (ct_ref[hh, i] - cks[hh])
                if masked:
                    s = jnp.where(diff <= i * tq - j * tk, s, NEG_INF)
                p = jnp.exp(s - lse_ref[hh, i])
                dp = lax.dot_general(vts[hh], dot, NT, preferred_element_type=F32)
                ds = p * (dp - dl_ref[hh, i])
                pb = p.astype(dot.dtype)
                dsb = ds.astype(qt.dtype)
                dv_acc = dv_acc + jnp.dot(pb, dot, preferred_element_type=F32)
                dk_acc = dk_acc + jnp.dot(dsb, qt, preferred_element_type=F32)
                dqt_ref[hh, i] += jnp.dot(k_trs[hh], dsb, preferred_element_type=F32)
                if bias:
                    dcq_ref[hh, i] += jnp.sum(ds, axis=0, keepdims=True)
                    dck_acc = dck_acc - jnp.sum(ds, axis=1, keepdims=True)
                out.append((dk_acc, dv_acc, dck_acc))
            return tuple(out)

        i_first = (j * tk) // tq
        i_clear = jnp.minimum(((j + 1) * tk + tq - 2) // tq, nq)
        carry = tuple((jnp.zeros((tk, HP), F32), jnp.zeros((tk, HP), F32), jnp.zeros((tk, 1), F32)) for _ in range(hg))
        carry = lax.fori_loop(i_first, i_clear, lambda i, c: step(i, c, True), carry)
        carry = lax.fori_loop(i_clear, nq, lambda i, c: step(i, c, False), carry)
        for hh in range(hg):
            dk_acc, dv_acc, dck_acc = carry[hh]
            dk_ref[:, _hs(hh)] = dk_acc.astype(dk_ref.dtype)
            dv_ref[:, _hs(hh)] = dv_acc.astype(dv_ref.dtype)
            if bias:
                dck_ref[:, _hs(hh)] = jnp.broadcast_to(dck_acc, (tk, HP))

        @pl.when(j == nk - 1)
        def _():
            for hh in range(hg):
                for i in range(nq):
                    dq_ref[i * tq:(i + 1) * tq, _hs(hh)] = dqt_ref[hh, i].T.astype(dq_ref.dtype)

    hg = ATT_HEADS
    w = hg * HP
    gpw = HW // w
    res = lambda g: pl.BlockSpec((r, w), lambda h, j: (0, g * gpw + h))
    tile = lambda g: pl.BlockSpec((tk, w), lambda h, j: (j, g * gpw + h))
    rowv = pl.BlockSpec((hg, nq, 1, tq), lambda h, j: (h, 0, 0, 0))
    as_rows = lambda a: a.reshape(HEADS, nq, 1, tq)
    ins = [(qa, res(qg)), (ka, tile(kg)), (va, tile(vg)), (do_b, res(0)), (o, res(0)), (as_rows(lse_t), rowv)]
    outs = [(_sds((r, HW), out_dtype), res(0)), (_sds((r, HW), out_dtype), tile(0)), (_sds((r, HW), out_dtype), tile(0))]
    if bias:
        ins += [(cum_b, tile(0)), (as_rows(cum_t), rowv)]
        outs += [(_sds((HEADS, nq, 1, tq), F32), rowv), (_sds((r, HW), F32), tile(0))]
    res_out = _call(name, body, (gpw, nk), ins, outs,
                    scratch=[pltpu.VMEM((hg, nq, HP, tq), F32), pltpu.VMEM((hg, nq, 1, tq), F32)],
                    sem=("parallel", "arbitrary"), after=after)
    if bias:
        dq, dk, dv, dcq, dck = res_out
        return dq, dk, dv, dcq.reshape(HEADS, r), dck
    return res_out


MESH_ID = pl.DeviceIdType.MESH
ANY = pl.BlockSpec(memory_space=pl.ANY)


N_GATHER_COPIES = 8


def _allgather(name, shards):
    n = len(shards)

    def body(*refs):
        x_refs, out_refs = refs[:n], refs[n:2 * n]
        send_sems, recv_sems, local_sems = refs[2 * n:]
        x, y, c = lax.axis_index("x"), lax.axis_index("y"), lax.axis_index("c")
        me, sibling = (x, y, c), (x, y, 1 - c)
        xn, yn, dg = (1 - x, y, c), (x, 1 - y, c), (1 - x, 1 - y, c)
        other = lambda dev: (dev[0], dev[1], 1 - c)

        def slot(ti, dev, half=None):
            ref = out_refs[ti].at[4 * dev[0] + 2 * dev[1] + dev[2]]
            if half is None:
                return ref
            rows = shards[ti].shape[0] // 2
            return ref.at[pl.ds(half * rows, rows)]

        def copy(ti, k, block, to, half=None, src=None):
            return pltpu.make_async_remote_copy(
                src_ref=slot(ti, block, half) if src is None else src, dst_ref=slot(ti, block, half),
                send_sem=send_sems.at[ti, k], recv_sem=recv_sems.at[ti, k], device_id=to, device_id_type=MESH_ID)

        mine = [pltpu.make_async_copy(x_refs[ti], slot(ti, me), local_sems.at[ti]) for ti in range(n)]
        for cp in mine:
            cp.start()
        started = []

        def go(cp):
            cp.start()
            started.append(cp)

        for ti in range(n):
            go(copy(ti, 0, me, sibling, src=x_refs[ti]))
            go(copy(ti, 1, me, xn, src=x_refs[ti]))
            go(copy(ti, 2, me, yn, src=x_refs[ti]))
        for ti in range(n):
            copy(ti, 1, xn, me).wait_recv()
            go(copy(ti, 3, xn, yn, half=0))
            go(copy(ti, 5, xn, sibling))
            copy(ti, 2, yn, me).wait_recv()
            go(copy(ti, 4, yn, xn, half=1))
            go(copy(ti, 6, yn, sibling))
        for ti in range(n):
            copy(ti, 3, dg, me, half=0).wait_recv()
            copy(ti, 4, dg, me, half=1).wait_recv()
            go(copy(ti, 7, dg, sibling))
        for ti in range(n):
            copy(ti, 0, sibling, me).wait_recv()
            for k, dev in ((5, xn), (6, yn), (7, dg)):
                copy(ti, k, other(dev), me).wait_recv()
        for cp in started:
            cp.wait_send()
        for cp in mine:
            cp.wait()

    sems = pltpu.SemaphoreType.DMA((n, N_GATHER_COPIES))
    return pl.pallas_call(
        body, name=name, out_shape=[_sds((N_DEV,) + s.shape, s.dtype) for s in shards],
        in_specs=[ANY] * n, out_specs=[ANY] * n,
        scratch_shapes=[sems, sems, pltpu.SemaphoreType.DMA((n,))],
    )(*shards)


HBM = pl.BlockSpec(memory_space=pltpu.HBM)
SEM = pl.BlockSpec(memory_space=pltpu.SEMAPHORE)
EFFECT = pltpu.SideEffectType.DATAFLOW_SIDE_EFFECTING
N_PEER = N_DEV - 1


def _my_id():
    return 4 * lax.axis_index("x") + 2 * lax.axis_index("y") + lax.axis_index("c")


def _peers():
    x, y, c = lax.axis_index("x"), lax.axis_index("y"), lax.axis_index("c")
    out = []
    for k in range(1, N_DEV):
        px, py, pc = (1 - x if k & 4 else x, 1 - y if k & 2 else y, 1 - c if k & 1 else c)
        out.append(((px, py, pc), 4 * px + 2 * py + pc))
    return out


def _push_copies(src_refs, land_refs, send_sems, recv_sems, scatter, landing):
    me = _my_id()
    out = []
    for ti, (src, land) in enumerate(zip(src_refs, land_refs)):
        for k, (dev, pid) in enumerate(_peers()):
            out.append(pltpu.make_async_remote_copy(
                src_ref=src.at[pid] if scatter else src, dst_ref=land.at[pid if landing else me],
                send_sem=send_sems.at[ti * N_PEER + k], recv_sem=recv_sems.at[ti * N_PEER + k],
                device_id=dev, device_id_type=MESH_ID))
    return out


def _push_start(name, groups, scatter, after=None):
    sizes = [len(g) for g in groups]
    srcs = [a for g in groups for a in g]
    n = len(srcs)
    slot = lambda s: s.shape[1:] if scatter else s.shape
    lands = [lax.empty((N_DEV,) + slot(s), s.dtype) for s in srcs]
    n_after = 0 if after is None else 1
    n_grp = len(groups)

    def body(*refs):
        src_refs, land_refs = refs[:n], refs[n:2 * n]
        sems = refs[2 * n + n_after:2 * n + n_after + 2 * n_grp]
        token = refs[-1]
        lo = 0
        for gi, sz in enumerate(sizes):
            for cp in _push_copies(src_refs[lo:lo + sz], land_refs[lo:lo + sz], sems[2 * gi], sems[2 * gi + 1], scatter, False):
                cp.start()
            lo += sz
        token[...] = jnp.zeros_like(token)

    hbm = lambda a: pltpu.with_memory_space_constraint(a, pltpu.HBM)
    operands = [hbm(a) for a in srcs + lands] + ([after] if n_after else [])
    sem_shapes = [pltpu.SemaphoreType.DMA((sz * N_PEER,)) for sz in sizes for _ in range(2)]
    res = pl.pallas_call(
        body, name=name,
        out_shape=sem_shapes + [pltpu.HBM(a.shape, a.dtype) for a in srcs + lands] + [_sds((8, 128), F32)],
        in_specs=[HBM] * (2 * n) + [ANY] * n_after,
        out_specs=[SEM] * (2 * n_grp) + [HBM] * (2 * n) + [pl.BlockSpec(memory_space=pltpu.VMEM)],
        input_output_aliases={i: 2 * n_grp + i for i in range(2 * n)},
        compiler_params=pltpu.CompilerParams(has_side_effects=EFFECT),
    )(*operands)
    thru = res[2 * n_grp:2 * n_grp + 2 * n]
    handles, lo = [], 0
    for gi, sz in enumerate(sizes):
        handles.append((res[2 * gi], res[2 * gi + 1], list(thru[lo:lo + sz]), list(thru[n + lo:n + lo + sz]), scatter))
        lo += sz
    return handles, res[-1]


def _push_wait(name, handle, after):
    send_sems, recv_sems, srcs, lands, scatter = handle
    n = len(srcs)

    def body(*refs):
        src_refs, land_refs = refs[:n], refs[n:2 * n]
        s_sems, r_sems = refs[2 * n], refs[2 * n + 1]
        for cp in _push_copies(src_refs, land_refs, s_sems, r_sems, scatter, True):
            cp.wait_send()
            cp.wait_recv()

    res = pl.pallas_call(
        body, name=name,
        out_shape=[pltpu.HBM(a.shape, a.dtype) for a in srcs + lands],
        in_specs=[HBM] * (2 * n) + [SEM, SEM, ANY], out_specs=[HBM] * (2 * n),
        input_output_aliases={i: i for i in range(2 * n)},
        compiler_params=pltpu.CompilerParams(has_side_effects=EFFECT),
    )(*srcs, *lands, send_sems, recv_sems, after)
    return list(res[n:])


def _adamw(name, parts, w, m, v, own=None):
    r, c = w.shape
    br = _pick(r, COPY_ROWS, 16)
    has_own = own is not None

    def body(*refs):
        if has_own:
            p_ref, own_ref, w_ref, m_ref, v_ref, g_ref, d_ref, nm_ref, nv_ref = refs
            me = _my_id()
            mine = own_ref[...].astype(F32)
        else:
            p_ref, w_ref, m_ref, v_ref, g_ref, d_ref, nm_ref, nv_ref = refs
        g = None
        for k in range(N_DEV):
            t = p_ref[k].astype(F32)
            if has_own:
                t = jnp.where(me == k, mine, t)
            g = t if g is None else g + t
        mm = ADAM_B1 * m_ref[...] + (1.0 - ADAM_B1) * g
        vv = ADAM_B2 * v_ref[...] + (1.0 - ADAM_B2) * (g * g)
        m_hat = mm / (1.0 - ADAM_B1 ** ADAM_STEP)
        v_hat = vv / (1.0 - ADAM_B2 ** ADAM_STEP)
        g_ref[...] = g
        d_ref[...] = -ADAM_LR * (m_hat / (jnp.sqrt(v_hat) + ADAM_EPS) + ADAM_WD * w_ref[...])
        nm_ref[...] = mm
        nv_ref[...] = vv

    spec = _rows(br, c)
    out = (_sds((r, c), F32), spec)
    ins = [(parts, pl.BlockSpec((N_DEV, br, c), lambda i: (0, i, 0)))] + ([(own, spec)] if has_own else [])
    return _call(name, body, (r // br,), ins + [(w, spec), (m, spec), (v, spec)], [out] * 4, sem=("parallel",))


def _grad_sum(name, parts, own):
    r, c = own.shape
    br = _pick(r, COPY_ROWS, 16)

    def body(p_ref, own_ref, g_ref):
        me = _my_id()
        mine = own_ref[...].astype(F32)
        g = None
        for k in range(N_DEV):
            t = jnp.where(me == k, mine, p_ref[k].astype(F32))
            g = t if g is None else g + t
        g_ref[...] = g

    spec = _rows(br, c)
    return _call(name, body, (r // br,), [(parts, pl.BlockSpec((N_DEV, br, c), lambda i: (0, i, 0))), (own, spec)],
                 [(_sds((r, c), F32), spec)], sem=("parallel",))[0]


def _adamw_transposed(name, parts, w, m, v, own):
    r, c = w.shape
    bl = _pick(r, COPY_ROWS, HP)
    g = _grad_sum(name + "_sum", parts, own)
    gt = g.T

    def body(g_ref, w_ref, m_ref, v_ref, d_ref, nm_ref, nv_ref):
        g = g_ref[...]
        mm = ADAM_B1 * m_ref[...] + (1.0 - ADAM_B1) * g
        vv = ADAM_B2 * v_ref[...] + (1.0 - ADAM_B2) * (g * g)
        m_hat = mm / (1.0 - ADAM_B1 ** ADAM_STEP)
        v_hat = vv / (1.0 - ADAM_B2 ** ADAM_STEP)
        d_ref[...] = -ADAM_LR * (m_hat / (jnp.sqrt(v_hat) + ADAM_EPS) + ADAM_WD * w_ref[...])
        nm_ref[...] = mm
        nv_ref[...] = vv

    spec = pl.BlockSpec((c, bl), lambda i: (0, i))
    out = (_sds((c, r), F32), spec)
    d, nm, nv = _call(name, body, (r // bl,), [(gt, spec), (w.T, spec), (m.T, spec), (v.T, spec)], [out] * 3,
                      sem=("parallel",))
    return gt.T, d.T, nm.T, nv.T, g


def _pad_head_cols(w, d):
    k = w.shape[0]
    return jnp.pad(w.reshape(k, HEADS, d), ((0, 0), (0, 0), (0, HP - d))).reshape(k, HW)


def _unpad_head_cols(wp, d):
    k = wp.shape[0]
    return wp.reshape(k, HEADS, HP)[:, :, :d].reshape(k, HEADS * d)


def _pad_head_rows(w, d):
    n = w.shape[1]
    return jnp.pad(w.reshape(HEADS, d, n), ((0, 0), (0, HP - d), (0, 0))).reshape(HW, n)


def _w_in_runs():
    nat = {}
    o = 0
    for nm, wd in (("q", Q_RANK), ("kv", KV_RANK), ("kr", ROPE), ("fq", FOX_W), ("fk", FOX_W), ("fv", FOX_W),
                   ("fl", HEADS), ("gate", 2 * D_MODEL)):
        nat[nm] = o
        o += wd
    runs = [(1, R_QLAT, nat["q"], Q_RANK, 1.0), (1, R_KVLAT, nat["kv"], KV_RANK, 1.0),
            (1, R_LAST + LANE_FL, nat["fl"], HEADS, 1.0), (1, R_LAST + LANE_PE, nat["kr"], ROPE, 1.0),
            (1, R_GATE, nat["gate"], 2 * D_MODEL, 1.0)]
    for grp, (nm, sc) in enumerate((("fq", FOX_SCALE), ("fk", 1.0), ("fv", 1.0))):
        runs.append((0, grp * FOX_W, nat[nm], FOX_W, sc))
    return runs


def _head_pad_moves(pad):
    moves = []
    for grp in range(3):
        for h in range(HEADS):
            narrow, wide = grp * FOX_W + h * FOX_DIM, h * HP
            if pad:
                moves.append((0, None, grp * HW + wide, 0, None, narrow, FOX_DIM, 1.0))
            else:
                moves.append((0, None, narrow, grp, None, wide, FOX_DIM, 1.0))
    return moves


def _sharded_runs(runs, shard_cols):
    out = []
    for half, col, ncol, width, sc in runs:
        while width > 0:
            d, local = divmod(ncol, shard_cols)
            wd = min(width, shard_cols - local)
            out.append((half, col, d, local, wd, sc))
            col, ncol, width = col + wd, ncol + wd, width - wd
    return out


def _remap(name, srcs, out_shapes, moves):
    rows = srcs[0].shape[-2]
    br = _pick(rows, COPY_ROWS, 16)
    ns = len(srcs)

    def spec(shape):
        if len(shape) == 2:
            return pl.BlockSpec((br, shape[1]), lambda i: (i, 0))
        return pl.BlockSpec((shape[0], br, shape[2]), lambda i: (0, i, 0))

    covered = [sum(m[6] for m in moves if m[0] == di) for di in range(len(out_shapes))]
    has_gaps = [cov < (shape[1] if len(shape) == 2 else shape[0] * shape[2])
                for cov, (shape, _) in zip(covered, out_shapes)]

    def body(*refs):
        s_refs, o_refs = refs[:ns], refs[ns:]
        for o, gaps in zip(o_refs, has_gaps):
            if gaps:
                o[...] = jnp.zeros_like(o)
        for di, dl, dc, si, sl, sc0, wd, scale in moves:
            v = s_refs[si][:, sc0:sc0 + wd] if sl is None else s_refs[si][sl, :, sc0:sc0 + wd]
            if scale != 1.0:
                v = v * jnp.asarray(scale, v.dtype)
            v = v.astype(o_refs[di].dtype)
            if dl is None:
                o_refs[di][:, dc:dc + wd] = v
            else:
                o_refs[di][dl, :, dc:dc + wd] = v

    return _call(name, body, (rows // br,), [(a, spec(a.shape)) for a in srcs],
                 [(_sds(shape, dt), spec(shape)) for shape, dt in out_shapes], sem=("parallel",))


def _w_in_from_shards(g3):
    n, rows, c = g3.shape
    moves = [(half, None, col, 0, d, local, wd, sc) for half, col, d, local, wd, sc in _sharded_runs(_w_in_runs(), c)]
    return _remap("w_in_repack", [g3], [((rows, F_W), g3.dtype), ((rows, R_W), g3.dtype)], moves)


def _w_in_grad_to_shards(d_fused, d_rest, n, c):
    rows = d_fused.shape[0]
    moves = [(0, d, local, half, None, col, wd, sc) for half, col, d, local, wd, sc in _sharded_runs(_w_in_runs(), c)]
    return _remap("w_in_grad_unpack", [d_fused, d_rest], [((n, rows, c), d_fused.dtype)], moves)[0]


def _rows_from_shards(name, land, own):
    n, rows, c = land.shape

    def body(land_ref, own_ref, o_ref):
        o_ref[...] = jnp.where(_my_id() == pl.program_id(0), own_ref[...], land_ref[...])

    return _call(name, body, (n,),
                 [(land, pl.BlockSpec((None, rows, c), lambda d: (d, 0, 0))), (own, _whole((rows, c)))],
                 [(_sds((n * rows, c), land.dtype), pl.BlockSpec((rows, c), lambda d: (d, 0)))], sem=("parallel",))[0]


def _cols_from_shards(name, land, own):
    n, rows, c = land.shape
    br = _pick(rows, COPY_ROWS, 16)

    def body(land_ref, own_ref, o_ref):
        me = _my_id()
        for d in range(n):
            o_ref[:, c * d:c * (d + 1)] = jnp.where(me == d, own_ref[...], land_ref[d])

    return _call(name, body, (rows // br,),
                 [(land, pl.BlockSpec((n, br, c), lambda i: (0, i, 0))), (own, _rows(br, c))],
                 [(_sds((rows, n * c), land.dtype), _rows(br, n * c))], sem=("parallel",))[0]


def _cols_to_shards(name, full, n):
    rows, nc = full.shape
    c = nc // n
    return _remap(name, [full], [((n, rows, c), full.dtype)], [(0, d, 0, 0, None, c * d, c, 1.0) for d in range(n)])[0]


class _NoComm:
    first_token = ()

    def late_weights(self, group, after):
        return {}

    def send(self, name, grads):
        return ()


def _local_step(x, tgt, p, comm=_NoComm()):
    seq = x.shape[0]
    r = -(-(N_META + seq) // ROW_ALIGN) * ROW_ALIGN
    cd = MXU_DTYPE
    p = dict(p)

    w_f, w_r = p["w_in"]

    pos = jnp.arange(r, dtype=F32)
    inv_freq = ROPE_THETA ** (-jnp.arange(HALF, dtype=F32) / HALF)
    ang = pos[:, None] * inv_freq[None, :]
    cos_t = jnp.tile(jnp.cos(ang), (1, HP // HALF))
    sin_t = jnp.tile(jnp.sin(ang), (1, HP // HALF))
    bf_row = jnp.zeros((1, HP), F32).at[0, LANE_FL:LANE_FL + HEADS].set(p["b_forget"])

    h0, h0b = _ln_emb_fwd(x, p["meta_tokens"], p["ln_emb_g"], p["ln_emb_b"], r, after=comm.first_token)
    proj_f = _matmul("in_proj_f", h0b, w_f, out_dtype=cd)
    proj_f = _remap("proj_f_pad", [proj_f], [((r, 3 * HW), cd)], _head_pad_moves(True))[0]
    proj_r = _matmul("in_proj_r", h0b, w_r)
    latent_gains = (p["q_norm_g"], p["kv_norm_g"])
    ql, kvl = _latent_norm_fwd(proj_r, latent_gains)
    p.update(comm.late_weights("qkv", ql))
    w_q = _pad_head_cols(p["w_q_up"], QK_DIM)
    w_kv = p["w_kv_up"]
    q_raw = _matmul("q_up", ql, w_q)
    kv = _matmul("kv_up", kvl, w_kv, out_dtype=cd)
    q_mla, k_mla, v_mla = _rope_fwd(q_raw, kv, proj_r, cos_t, sin_t)
    o_mla, o_mla_b, lse_mla = _attn_fwd("mla_fwd", (q_mla, 0), (k_mla, 0), (v_mla, 0))

    cum, cum_t = _forget_fwd(proj_r, bf_row)
    o_fox, o_fox_b, lse_fox = _attn_fwd("fox_fwd", (proj_f, 0), (proj_f, 1), (proj_f, 2), cum, cum_t)

    p.update(comm.late_weights("mix", o_fox_b))
    w_bm = _pad_head_rows(p["w_branch_mla"], V_DIM)
    w_bf = _pad_head_rows(p["w_branch_fox"], FOX_DIM)
    bm = _matmul("branch_mla", o_mla_b, p["w_branch_mla"], out_dtype=cd)
    bfx = _matmul("branch_fox", o_fox_b, p["w_branch_fox"], out_dtype=cd)
    merged = _gate_fwd(proj_r, p["b_gate"], bm, bfx)
    mix = _matmul("out_proj", merged, p["w_out"])
    h1, h1b = _ln_fwd("ln_mix_fwd", h0, mix, p["ln_mix_g"], p["ln_mix_b"])
    p.update(comm.late_weights("ffn", h1b))
    up = _matmul("ffn_up", h1b, p["w_ffn_up"], out_dtype=cd)
    act = _glu_fwd(up, p["conv_w"], p["conv_b"])
    f = _matmul("ffn_down", act, p["w_ffn_down"])

    g = {}
    dz2, dz2b, g["ln_ffn_g"], g["ln_ffn_b"], loss = _ln_ffn_bwd(h1, f, tgt, p["ln_ffn_g"], p["ln_ffn_b"])
    d_act = _matmul("ffn_down_dx", dz2b, p["w_ffn_down"], tb=True, out_dtype=cd)
    g["w_ffn_down"] = _matmul("ffn_down_dw", act, dz2b, ta=True, out_dtype=cd)
    d_up, dcw, g["conv_b"] = _glu_bwd(up, p["conv_w"], p["conv_b"], d_act)
    g["conv_w"] = dcw[:3]
    dh1 = _matmul("ffn_up_dx", d_up, p["w_ffn_up"], tb=True, addend=dz2, alpha=ALPHA)
    g["w_ffn_up"] = _matmul("ffn_up_dw", h1b, d_up, ta=True, out_dtype=cd)
    sent = comm.send("ffn", {n: g[n] for n in ("w_ffn_down", "w_ffn_up", "conv_w")})
    dz1, dz1b, g["ln_mix_g"], g["ln_mix_b"] = _ln_bwd("ln_mix_bwd", h0, mix, dh1, p["ln_mix_g"], after=sent)
    dmerged = _matmul("out_proj_dx", dz1b, p["w_out"], tb=True, out_dtype=cd)
    g["w_out"] = _matmul("out_proj_dw", merged, dz1b, ta=True, out_dtype=cd)
    d_bm, d_bf, d_gl, g["b_gate"] = _gate_bwd(proj_r, p["b_gate"], bm, bfx, dmerged)
    do_mla_b = _matmul("branch_mla_dx", d_bm, w_bm, tb=True, out_dtype=cd)
    g["w_branch_mla"] = _matmul("branch_mla_dw", o_mla_b, d_bm, ta=True, out_dtype=cd)
    do_fox_b = _matmul("branch_fox_dx", d_bf, w_bf, tb=True, out_dtype=cd)
    g["w_branch_fox"] = _matmul("branch_fox_dw", o_fox_b, d_bf, ta=True, out_dtype=cd)

    sent = comm.send("mix", {n: g[n] for n in ("w_out", "w_branch_mla", "w_branch_fox")})
    dq_m, dk_m, dv_m = _attn_bwd("mla_bwd", (q_mla, 0), (k_mla, 0), (v_mla, 0), do_mla_b, o_mla, lse_mla, after=sent)
    dfq, dfk, dfv, dcq, dck = _attn_bwd("fox_bwd", (proj_f, 0), (proj_f, 1), (proj_f, 2), do_fox_b, o_fox, lse_fox,
                                        cum, cum_t, out_dtype=cd)
    dfl, dbf = _forget_bwd(proj_r, bf_row, dcq, dck)
    g["b_forget"] = dbf[:, LANE_FL:LANE_FL + HEADS]

    dq_b, dkv_b, dlast = _rope_bwd(dq_m, dk_m, dv_m, dfl, cos_t, sin_t)
    d_ql = _matmul("q_up_dx", dq_b, w_q, tb=True)
    d_kvl = _matmul("kv_up_dx", dkv_b, w_kv, tb=True)
    d_qlat, d_kvlat, g["q_norm_g"], g["kv_norm_g"] = _latent_norm_bwd(proj_r, (d_ql, d_kvl), latent_gains)
    side_by_side = lambda parts, cols: [(0, None, c0, si, None, 0, a.shape[1], 1.0) for si, (a, c0) in enumerate(zip(parts, cols))]
    dproj_f = _remap("dproj_f_pack", [dfq, dfk, dfv], [((r, F_W), cd)], _head_pad_moves(False))[0]
    rest_parts = [d_qlat, d_kvlat, dlast, d_gl]
    dproj_r = _remap("dproj_r_pack", rest_parts, [((r, R_W), cd)],
                     side_by_side(rest_parts, (R_QLAT, R_KVLAT, R_LAST, R_GATE)))[0]
    g["w_in"] = (_matmul("in_proj_f_dw", h0b, dproj_f, ta=True, out_dtype=cd),
                 _matmul("in_proj_r_dw", h0b, dproj_r, ta=True, out_dtype=cd))
    sent = comm.send("in", {"w_in": g["w_in"]})
    dh0 = _matmul("in_proj_f_dx", dproj_f, w_f, tb=True, addend=dz1, alpha=ALPHA, after=sent)
    g["w_q_up"] = _unpad_head_cols(_matmul("q_up_dw", ql, dq_b, ta=True, out_dtype=cd, after=sent), QK_DIM)
    g["w_kv_up"] = _matmul("kv_up_dw", kvl, dkv_b, ta=True, out_dtype=cd, after=sent)
    sent = comm.send("qkv", {n: g[n] for n in ("w_q_up", "w_kv_up")})
    dh0 = _matmul("in_proj_r_dx", dproj_r, w_r, tb=True, addend=dh0, after=sent)
    grad_x, d_meta, g["ln_emb_g"], g["ln_emb_b"] = _ln_emb_bwd(x, p["meta_tokens"], dh0, p["ln_emb_g"])
    return loss, grad_x, d_meta, g


BIG = (("w_in", 1), ("w_q_up", 1), ("w_kv_up", 1), ("w_branch_mla", 1), ("w_branch_fox", 1), ("w_out", 0),
       ("w_ffn_up", 1), ("w_ffn_down", 0))
SMALL_SHARDED = (("meta_tokens", 1), ("conv_w", 1))
EARLY = ("w_in", "meta_tokens")
LATE = {"qkv": ("w_q_up", "w_kv_up", "conv_w"),
        "mix": ("w_branch_mla", "w_branch_fox", "w_out"),
        "ffn": ("w_ffn_up", "w_ffn_down")}
REPLICATED = ("ln_emb_g", "ln_emb_b", "b_gate", "b_forget", "q_norm_g", "kv_norm_g", "ln_mix_g", "ln_mix_b",
              "conv_b", "ln_ffn_g", "ln_ffn_b")
PACK_COLS = 1024


def _pack(flat_list):
    cat = jnp.concatenate(flat_list)
    n = cat.shape[0]
    rows = -(-n // (8 * PACK_COLS)) * 8
    return jnp.pad(cat, (0, rows * PACK_COLS - n)).reshape(rows, PACK_COLS)


def _gathered_full(g3, axis):
    n, r, c = g3.shape
    if axis == 0:
        return g3.reshape(n * r, c)
    return g3.transpose(1, 0, 2).reshape(r, n * c)


def _shard_major(full, axis):
    r, c = full.shape
    if axis == 0:
        return full.reshape(N_DEV, r // N_DEV, c)
    return full.reshape(r, N_DEV, c // N_DEV).transpose(1, 0, 2)


def kernel(x, meta_tokens, ln_emb_g, ln_emb_b, w_in, b_gate, b_forget, q_norm_g, w_q_up, kv_norm_g, w_kv_up, w_branch_mla, w_branch_fox, w_out, ln_mix_g, ln_mix_b, w_ffn_up, conv_w, conv_b, w_ffn_down, ln_ffn_g, ln_ffn_b, loss_target, m_meta_tokens, m_ln_emb_g, m_ln_emb_b, m_w_in, m_b_gate, m_b_forget, m_q_norm_g, m_w_q_up, m_kv_norm_g, m_w_kv_up, m_w_branch_mla, m_w_branch_fox, m_w_out, m_ln_mix_g, m_ln_mix_b, m_w_ffn_up, m_conv_w, m_conv_b, m_w_ffn_down, m_ln_ffn_g, m_ln_ffn_b, v_meta_tokens, v_ln_emb_g, v_ln_emb_b, v_w_in, v_b_gate, v_b_forget, v_q_norm_g, v_w_q_up, v_kv_norm_g, v_w_kv_up, v_w_branch_mla, v_w_branch_fox, v_w_out, v_ln_mix_g, v_ln_mix_b, v_w_ffn_up, v_conv_w, v_conv_b, v_w_ffn_down, v_ln_ffn_g, v_ln_ffn_b):
    names = ("meta_tokens", "ln_emb_g", "ln_emb_b", "w_in", "b_gate", "b_forget", "q_norm_g", "w_q_up", "kv_norm_g",
             "w_kv_up", "w_branch_mla", "w_branch_fox", "w_out", "ln_mix_g", "ln_mix_b", "w_ffn_up", "conv_w", "conv_b",
             "w_ffn_down", "ln_ffn_g", "ln_ffn_b")
    w_args = (meta_tokens, ln_emb_g, ln_emb_b, w_in, b_gate, b_forget, q_norm_g, w_q_up, kv_norm_g, w_kv_up,
              w_branch_mla, w_branch_fox, w_out, ln_mix_g, ln_mix_b, w_ffn_up, conv_w, conv_b, w_ffn_down, ln_ffn_g, ln_ffn_b)
    m_args = (m_meta_tokens, m_ln_emb_g, m_ln_emb_b, m_w_in, m_b_gate, m_b_forget, m_q_norm_g, m_w_q_up, m_kv_norm_g,
              m_w_kv_up, m_w_branch_mla, m_w_branch_fox, m_w_out, m_ln_mix_g, m_ln_mix_b, m_w_ffn_up, m_conv_w, m_conv_b,
              m_w_ffn_down, m_ln_ffn_g, m_ln_ffn_b)
    v_args = (v_meta_tokens, v_ln_emb_g, v_ln_emb_b, v_w_in, v_b_gate, v_b_forget, v_q_norm_g, v_w_q_up, v_kv_norm_g,
              v_w_kv_up, v_w_branch_mla, v_w_branch_fox, v_w_out, v_ln_mix_g, v_ln_mix_b, v_w_ffn_up, v_conv_w, v_conv_b,
              v_w_ffn_down, v_ln_ffn_g, v_ln_ffn_b)
    as2d = lambda a: a.reshape((-1, a.shape[-1])) if a.ndim != 1 else a.reshape(1, -1)
    w = {n: as2d(a) for n, a in zip(names, w_args)}
    m = {n: as2d(a) for n, a in zip(names, m_args)}
    v = {n: as2d(a) for n, a in zip(names, v_args)}
    out_shape = {n: a.shape for n, a in zip(names, w_args)}

    axis_of = dict(BIG + SMALL_SHARDED)
    big = set(n for n, _ in BIG)
    wire = lambda n, a: a.astype(MXU_DTYPE) if n in big else a
    my_id = _my_id()

    early = _allgather("gather_early", [wire(n, w[n]) for n in EARLY])
    p = {n: _gathered_full(g3, axis_of[n]) for n, g3 in zip(EARLY, early) if n != "w_in"}
    p["w_in"] = _w_in_from_shards(early[EARLY.index("w_in")])
    for n in REPLICATED:
        p[n] = w[n].reshape(-1)
    late_src = [[wire(n, w[n]) for n in members] for members in LATE.values()]
    late_handles, late_token = _push_start("gather_late_start", late_src, False, after=early[0])
    late = {group: (members, src, handle)
            for (group, members), src, handle in zip(LATE.items(), late_src, late_handles)}
    sent = {}

    class Comm:
        first_token = (late_token,)

        def late_weights(self, group, after):
            members, src, handle = late[group]
            lands = _push_wait("gather_" + group + "_wait", handle, after)
            out = {}
            for n, own, land in zip(members, src, lands):
                if own.shape[0] % 16:
                    out[n] = _gathered_full(lax.dynamic_update_index_in_dim(land, own, my_id, 0), axis_of[n])
                elif axis_of[n] == 1:
                    out[n] = _cols_from_shards(n + "_repack", land, own)
                else:
                    out[n] = _rows_from_shards(n + "_repack", land, own)
            return out

        def send(self, name, grads):
            names_ = tuple(grads)
            parts = []
            for n in names_:
                if n == "w_in":
                    parts.append(_w_in_grad_to_shards(*grads[n], N_DEV, w[n].shape[1]))
                elif n == "w_ffn_up":
                    parts.append(_cols_to_shards(n + "_grad_unpack", grads[n], N_DEV))
                else:
                    parts.append(_shard_major(grads[n], axis_of[n]).astype(MXU_DTYPE))
            (handle,), token = _push_start("send_" + name + "_start", [parts], True)
            sent[name] = (names_, parts, handle)
            return (token,)

    loss_part, grad_x, d_meta, g = _local_step(x[0], loss_target[0], p, Comm())
    grad_x = grad_x[None]

    small = _pack([d_meta.reshape(-1)] + [g[n].reshape(-1) for n in REPLICATED] + [loss_part.reshape(-1)])
    (small_handle,), small_token = _push_start("send_small_start", [[small]], False)

    res = {}
    prev = small_token
    for name, (names_, parts, handle) in sent.items():
        lands = _push_wait("send_" + name + "_wait", handle, prev)
        for n, part, land in zip(names_, parts, lands):
            own = lax.dynamic_index_in_dim(part, my_id, axis=0, keepdims=False)
            r_, c_ = w[n].shape
            update = _adamw_transposed if (c_ % HP and not r_ % HP) else _adamw
            res[n] = update("adamw_" + n, land, w[n], m[n], v[n], own=own)
            prev = res[n][-1]
    small_all = _push_wait("send_small_wait", small_handle, prev)[0]
    head = jnp.zeros((d_meta.size,), F32)
    rep_w = _pack([head] + [w[n].reshape(-1) for n in REPLICATED])
    rep_m = _pack([head] + [m[n].reshape(-1) for n in REPLICATED])
    rep_v = _pack([head] + [v[n].reshape(-1) for n in REPLICATED])
    rep_res = _adamw("adamw_replicated", small_all, rep_w, rep_m, rep_v, own=small)
    off = d_meta.size
    for n in REPLICATED:
        sz = w[n].size
        res[n] = tuple(a.reshape(-1)[off:off + sz] for a in rep_res)
        off += sz
    loss = rep_res[0].reshape(-1)[off]
    cols = w["meta_tokens"].shape[1]
    meta_rows = lambda a: a.reshape(a.shape[:-2] + (-1,))[..., :d_meta.size].reshape(a.shape[:-2] + d_meta.shape)
    my_cols = lambda a: lax.dynamic_slice_in_dim(a, my_id * cols, cols, axis=a.ndim - 1)
    res["meta_tokens"] = _adamw("adamw_meta_tokens", my_cols(meta_rows(small_all)), w["meta_tokens"],
                                m["meta_tokens"], v["meta_tokens"], own=my_cols(d_meta))

    outs = [loss, grad_x]
    for idx in range(4):
        outs += [res[n][idx].reshape(out_shape[n]) for n in names]
    return tuple(outs)
```

```python
import jax
import jax.numpy as jnp
from jax import lax
from jax.experimental import pallas as pl
from jax.experimental.pallas import tpu as pltpu

F32 = jnp.float32
BF16 = jnp.bfloat16
MXU_DTYPE = BF16

N_DEV = 8
N_META = 16
D_MODEL = 1024
HEADS = 8
Q_RANK = 384
KV_RANK = 128
NOPE = 64
ROPE = 32
HALF = ROPE // 2
QK_DIM = NOPE + ROPE
V_DIM = 64
FOX_DIM = 64
FOX_W = HEADS * FOX_DIM
D_FF = 2816
ROPE_THETA = 10000.0
LN_EPS = 1e-5
RMS_EPS = 1e-6
ALPHA = 2.0 ** 0.25
MLA_SCALE = QK_DIM ** -0.5
FOX_SCALE = FOX_DIM ** -0.5
NEG_INF = -1e30

HP = 128
HW = HEADS * HP
F_W = 3 * FOX_W
R_GATE = 0
R_KVLAT = R_GATE + 2 * D_MODEL
R_LAST = R_KVLAT + KV_RANK
R_QLAT = R_LAST + HP
R_W = R_QLAT + Q_RANK
assert R_QLAT % Q_RANK == 0 and R_KVLAT % KV_RANK == 0 and R_GATE % D_MODEL == 0 and R_W % HP == 0
LANE_FL = 0
LANE_PE = NOPE

ADAM_LR = 0.001
ADAM_B1 = 0.9
ADAM_B2 = 0.999
ADAM_EPS = 1e-08
ADAM_WD = 0.01
ADAM_STEP = 10

ROW_BLOCK = 384
TOKEN_BLOCK = 256
ATT_TQ = 768
ATT_TK = 768
ATT_HEADS = 2
ATT_HEADS_FWD = 4
ROW_ALIGN = 768
MM_BLOCK_CAP = 1408
VMEM_LIMIT = 56 * 1024 * 1024
HIGHEST = lax.Precision.HIGHEST
NT = (((1,), (1,)), ((), ()))
TN = (((0,), (0,)), ((), ()))


def _params(sem=None):
    return pltpu.CompilerParams(dimension_semantics=sem, vmem_limit_bytes=VMEM_LIMIT)


def _call(name, body, grid, ins, outs, scratch=(), sem=None, after=()):
    n_in = len(ins)
    n_tok = len(after)

    def run(*refs):
        body(*refs[:n_in], *refs[n_in + n_tok:])

    tok_spec = pl.BlockSpec((8, 128), lambda *_: (0, 0))
    return pl.pallas_call(
        run, name=name, grid=grid,
        in_specs=[s for _, s in ins] + [tok_spec] * n_tok,
        out_specs=[s for _, s in outs],
        out_shape=[o for o, _ in outs],
        scratch_shapes=list(scratch),
        compiler_params=_params(sem),
    )(*[a for a, _ in ins], *after)


def _sds(shape, dtype):
    return jax.ShapeDtypeStruct(shape, dtype)


def _rows(br, c, cb=0):
    return pl.BlockSpec((br, c), lambda i: (i, cb))


def _whole(shape):
    n = len(shape)
    return pl.BlockSpec(shape, lambda i: (0,) * n)


def _pick(dim, cap, mult):
    best = None
    d = mult
    while d <= min(dim, cap):
        if dim % d == 0:
            best = d
        d += mult
    return best if best is not None else dim


def _hs(h):
    return slice(h * HP, (h + 1) * HP)


def _matmul(name, a, b, *, ta=False, tb=False, out_dtype=F32, addend=None, alpha=1.0, after=()):
    if ta:
        k, m = a.shape
    else:
        m, k = a.shape
    if tb:
        n, k2 = b.shape
    else:
        k2, n = b.shape
    assert k == k2, (name, a.shape, b.shape)
    bm = _pick(m, MM_BLOCK_CAP, 128 if ta else 16)
    bn = _pick(n, MM_BLOCK_CAP, 128)
    bk = _pick(k, MM_BLOCK_CAP, 128 if (not ta or tb) else 16)
    nk = k // bk
    dims = (((0 if ta else 1,), (1 if tb else 0,)), ((), ()))
    has_add = addend is not None

    def body(*refs):
        a_ref, b_ref = refs[:2]
        add_ref = refs[2] if has_add else None
        o_ref = refs[3 if has_add else 2]

        def finish(r):
            if has_add:
                r = r + alpha * add_ref[...]
            o_ref[...] = r.astype(o_ref.dtype)

        part = lax.dot_general(a_ref[...], b_ref[...], dims, preferred_element_type=F32)
        if nk == 1:
            finish(part)
            return
        acc_ref = refs[-1]
        kk = pl.program_id(2)

        @pl.when(kk == 0)
        def _():
            acc_ref[...] = part

        @pl.when(kk > 0)
        def _():
            acc_ref[...] += part

        @pl.when(kk == nk - 1)
        def _():
            finish(acc_ref[...])

    a_spec = pl.BlockSpec((bk, bm), lambda i, j, l: (l, i)) if ta else pl.BlockSpec((bm, bk), lambda i, j, l: (i, l))
    b_spec = pl.BlockSpec((bn, bk), lambda i, j, l: (j, l)) if tb else pl.BlockSpec((bk, bn), lambda i, j, l: (l, j))
    o_spec = pl.BlockSpec((bm, bn), lambda i, j, l: (i, j))
    ins = [(a, a_spec), (b, b_spec)]
    if has_add:
        ins.append((addend, o_spec))
    return _call(name, body, (m // bm, n // bn, nk), ins, [(_sds((m, n), out_dtype), o_spec)],
                 scratch=[pltpu.VMEM((bm, bn), F32)] if nk > 1 else [],
                 sem=("parallel", "parallel", "arbitrary"), after=after)[0]


def _ln_stats(z):
    mu = jnp.mean(z, axis=-1, keepdims=True)
    zc = z - mu
    var = jnp.mean(zc * zc, axis=-1, keepdims=True)
    rstd = lax.rsqrt(var + LN_EPS)
    return zc * rstd, rstd


def _ln_fwd(name, a, res, g, b, after=()):
    r, d = a.shape
    br = ROW_BLOCK
    has_res = res is not None

    def body(*refs):
        if has_res:
            a_ref, r_ref, g_ref, b_ref, y_ref, yb_ref = refs
            z = ALPHA * a_ref[...] + r_ref[...]
        else:
            a_ref, g_ref, b_ref, y_ref, yb_ref = refs
            z = a_ref[...]
        xhat, _ = _ln_stats(z)
        y = xhat * g_ref[...] + b_ref[...]
        y_ref[...] = y
        yb_ref[...] = y.astype(yb_ref.dtype)

    ins = [(a, _rows(br, d))]
    if has_res:
        ins.append((res, _rows(br, d)))
    ins += [(g.reshape(1, d), _whole((1, d))), (b.reshape(1, d), _whole((1, d)))]
    outs = [(_sds((r, d), F32), _rows(br, d)), (_sds((r, d), MXU_DTYPE), _rows(br, d))]
    return _call(name, body, (r // br,), ins, outs, sem=("parallel",), after=after)


def _ln_bwd(name, a, res, dy, g, after=()):
    r, d = a.shape
    br = ROW_BLOCK
    has_res = res is not None

    def body(*refs):
        if has_res:
            a_ref, r_ref, dy_ref, g_ref, dz_ref, dzb_ref, dg_ref, db_ref = refs
            z = ALPHA * a_ref[...] + r_ref[...]
        else:
            a_ref, dy_ref, g_ref, dz_ref, dzb_ref, dg_ref, db_ref = refs
            z = a_ref[...]
        xhat, rstd = _ln_stats(z)
        dyv = dy_ref[...]
        dyg = dyv * g_ref[...]
        m1 = jnp.mean(dyg, axis=-1, keepdims=True)
        m2 = jnp.mean(dyg * xhat, axis=-1, keepdims=True)
        dz = rstd * (dyg - m1 - xhat * m2)
        dz_ref[...] = dz
        dzb_ref[...] = dz.astype(dzb_ref.dtype)

        @pl.when(pl.program_id(0) == 0)
        def _():
            dg_ref[...] = jnp.zeros_like(dg_ref)
            db_ref[...] = jnp.zeros_like(db_ref)

        dg_ref[...] += jnp.sum(dyv * xhat, axis=0, keepdims=True)
        db_ref[...] += jnp.sum(dyv, axis=0, keepdims=True)

    ins = [(a, _rows(br, d))]
    if has_res:
        ins.append((res, _rows(br, d)))
    ins += [(dy, _rows(br, d)), (g.reshape(1, d), _whole((1, d)))]
    outs = [(_sds((r, d), F32), _rows(br, d)), (_sds((r, d), MXU_DTYPE), _rows(br, d)),
            (_sds((1, d), F32), _whole((1, d))), (_sds((1, d), F32), _whole((1, d)))]
    return _call(name, body, (r // br,), ins, outs, sem=("arbitrary",), after=after)


LATENTS = ((R_QLAT // Q_RANK, Q_RANK), (R_KVLAT // KV_RANK, KV_RANK))


def _latent_norm_fwd(proj_r, gains):
    r = proj_r.shape[0]
    br = ROW_BLOCK

    def body(xq_ref, xk_ref, gq_ref, gk_ref, yq_ref, yk_ref):
        for x_ref, g_ref, y_ref in ((xq_ref, gq_ref, yq_ref), (xk_ref, gk_ref, yk_ref)):
            x = x_ref[...]
            rstd = lax.rsqrt(jnp.mean(x * x, axis=-1, keepdims=True) + RMS_EPS)
            y_ref[...] = (x * rstd * g_ref[...]).astype(y_ref.dtype)

    return _call("latent_norm_fwd", body, (r // br,),
                 [(proj_r, _rows(br, wd, cb)) for cb, wd in LATENTS]
                 + [(g.reshape(1, wd), _whole((1, wd))) for g, (_, wd) in zip(gains, LATENTS)],
                 [(_sds((r, wd), MXU_DTYPE), _rows(br, wd)) for _, wd in LATENTS], sem=("parallel",))


def _latent_norm_bwd(proj_r, dys, gains):
    r = proj_r.shape[0]
    br = ROW_BLOCK

    def body(xq_ref, xk_ref, dq_ref, dk_ref, gq_ref, gk_ref, oq_ref, ok_ref, dgq_ref, dgk_ref):
        @pl.when(pl.program_id(0) == 0)
        def _():
            dgq_ref[...] = jnp.zeros_like(dgq_ref)
            dgk_ref[...] = jnp.zeros_like(dgk_ref)

        for x_ref, dy_ref, g_ref, dx_ref, dg_ref in ((xq_ref, dq_ref, gq_ref, oq_ref, dgq_ref),
                                                     (xk_ref, dk_ref, gk_ref, ok_ref, dgk_ref)):
            x = x_ref[...]
            rstd = lax.rsqrt(jnp.mean(x * x, axis=-1, keepdims=True) + RMS_EPS)
            nrm = x * rstd
            dyv = dy_ref[...]
            dyg = dyv * g_ref[...]
            dx_ref[...] = (rstd * (dyg - nrm * jnp.mean(dyg * nrm, axis=-1, keepdims=True))).astype(dx_ref.dtype)
            dg_ref[...] += jnp.sum(dyv * nrm, axis=0, keepdims=True)

    return _call("latent_norm_bwd", body, (r // br,),
                 [(proj_r, _rows(br, wd, cb)) for cb, wd in LATENTS]
                 + [(dy, _rows(br, wd)) for dy, (_, wd) in zip(dys, LATENTS)]
                 + [(g.reshape(1, wd), _whole((1, wd))) for g, (_, wd) in zip(gains, LATENTS)],
                 [(_sds((r, wd), MXU_DTYPE), _rows(br, wd)) for _, wd in LATENTS]
                 + [(_sds((1, wd), F32), _whole((1, wd))) for _, wd in LATENTS], sem=("arbitrary",))


def _lane_iota(shape):
    return lax.broadcasted_iota(jnp.int32, shape, 1)


def _rotary(t, c, s, lane, sign):
    second = pltpu.roll(t, HP - HALF, axis=1)
    first = pltpu.roll(t, HALF, axis=1)
    lo = (lane >= LANE_PE) & (lane < LANE_PE + HALF)
    hi = (lane >= LANE_PE + HALF) & (lane < LANE_PE + ROPE)
    return jnp.where(lo, t * c - sign * second * s, jnp.where(hi, t * c + sign * first * s, t))


def _rope_fwd(q_raw, kv, proj_r, cos_t, sin_t):
    r = q_raw.shape[0]
    br = ROW_BLOCK

    def body(q_ref, kv_ref, t_ref, c_ref, s_ref, qo_ref, ko_ref, vo_ref):
        c = c_ref[...]
        s = s_ref[...]
        lane = _lane_iota((br, HP))
        pe = (lane >= LANE_PE) & (lane < LANE_PE + ROPE)
        left = lane < NOPE
        kp = jnp.where(pe, _rotary(t_ref[...], c, s, lane, 1.0), 0.0)
        for h in range(HEADS):
            qo_ref[:, _hs(h)] = (_rotary(q_ref[:, _hs(h)], c, s, lane, 1.0) * MLA_SCALE).astype(qo_ref.dtype)
            t = kv_ref[:, _hs(h)].astype(F32)
            ko_ref[:, _hs(h)] = (jnp.where(left, t, 0.0) + kp).astype(ko_ref.dtype)
            vo_ref[:, _hs(h)] = jnp.where(left, pltpu.roll(t, HP - NOPE, axis=1), 0.0).astype(vo_ref.dtype)

    blk = _rows(br, HP)
    wide = _rows(br, HW)
    return _call("rope_fwd", body, (r // br,),
                 [(q_raw, wide), (kv, wide), (proj_r, _rows(br, HP, R_LAST // HP)), (cos_t, blk), (sin_t, blk)],
                 [(_sds((r, HW), MXU_DTYPE), wide)] * 3, sem=("parallel",))


def _rope_bwd(dq, dk, dv, dfl, cos_t, sin_t):
    r = dq.shape[0]
    br = ROW_BLOCK

    def body(dq_ref, dk_ref, dv_ref, fl_ref, c_ref, s_ref, dqo_ref, dkv_ref, dl_ref):
        c = c_ref[...]
        s = s_ref[...]
        lane = _lane_iota((br, HP))
        pe = (lane >= LANE_PE) & (lane < LANE_PE + ROPE)
        left = lane < NOPE
        acc = jnp.zeros((br, HP), F32)
        for h in range(HEADS):
            dqo_ref[:, _hs(h)] = (_rotary(dq_ref[:, _hs(h)], c, s, lane, -1.0) * MLA_SCALE).astype(dqo_ref.dtype)
            dkh = dk_ref[:, _hs(h)]
            acc = acc + dkh
            dkv_ref[:, _hs(h)] = jnp.where(left, dkh, pltpu.roll(dv_ref[:, _hs(h)], NOPE, axis=1)).astype(dkv_ref.dtype)
        dl_ref[...] = (jnp.where(pe, _rotary(acc, c, s, lane, -1.0), 0.0) + fl_ref[...]).astype(dl_ref.dtype)

    blk = _rows(br, HP)
    wide = _rows(br, HW)
    return _call("rope_bwd", body, (r // br,),
                 [(dq, wide), (dk, wide), (dv, wide), (dfl, blk), (cos_t, blk), (sin_t, blk)],
                 [(_sds((r, HW), MXU_DTYPE), wide), (_sds((r, HW), MXU_DTYPE), wide), (_sds((r, HP), MXU_DTYPE), blk)],
                 sem=("parallel",))


def _log_sigmoid(x):
    return jnp.minimum(x, 0.0) - jnp.log(1.0 + jnp.exp(-jnp.abs(x)))


def _head_lane(x, h, lane):
    return jnp.sum(jnp.where(lane == h, x, 0.0), axis=1, keepdims=True)


def _forget_fwd(proj_r, bf_row):
    r = proj_r.shape[0]
    br = ROW_BLOCK

    def body(t_ref, b_ref, ob_ref, ot_ref, carry_ref):
        @pl.when(pl.program_id(0) == 0)
        def _():
            carry_ref[...] = jnp.zeros_like(carry_ref)

        x = t_ref[...] + b_ref[...]
        lane = _lane_iota(x.shape)
        lf = jnp.where((lane >= LANE_FL) & (lane < LANE_FL + HEADS), _log_sigmoid(x), 0.0)
        tri = (lax.broadcasted_iota(jnp.int32, (br, br), 0) >= lax.broadcasted_iota(jnp.int32, (br, br), 1)).astype(F32)
        cum = jnp.dot(tri, lf, precision=HIGHEST, preferred_element_type=F32) + carry_ref[0:1, :]
        for h in range(HEADS):
            ob_ref[:, _hs(h)] = jnp.broadcast_to(_head_lane(cum, LANE_FL + h, lane), (br, HP))
        ot_ref[...] = cum.T[LANE_FL:LANE_FL + HEADS, :]
        carry_ref[...] = jnp.broadcast_to(cum[br - 1:br, :], carry_ref.shape)

    return _call("forget_fwd", body, (r // br,),
                 [(proj_r, _rows(br, HP, R_LAST // HP)), (bf_row, _whole((1, HP)))],
                 [(_sds((r, HW), F32), _rows(br, HW)), (_sds((HEADS, r), F32), pl.BlockSpec((HEADS, br), lambda i: (0, i)))],
                 scratch=[pltpu.VMEM((8, HP), F32)], sem=("arbitrary",))


def _forget_bwd(proj_r, bf_row, dcq_t, dck_b):
    r = proj_r.shape[0]
    br = ROW_BLOCK
    nb = r // br

    def body(t_ref, b_ref, dcq_ref, dck_ref, o_ref, db_ref, carry_ref):
        @pl.when(pl.program_id(0) == 0)
        def _():
            carry_ref[...] = jnp.zeros_like(carry_ref)
            db_ref[...] = jnp.zeros_like(db_ref)

        lane = _lane_iota((br, HP))
        dc = jnp.concatenate([dcq_ref[...], jnp.zeros((HP - HEADS, br), F32)], axis=0).T
        for h in range(HEADS):
            dc = dc + jnp.where(lane == LANE_FL + h, dck_ref[:, h * HP:h * HP + 1], 0.0)
        triu = (lax.broadcasted_iota(jnp.int32, (br, br), 0) <= lax.broadcasted_iota(jnp.int32, (br, br), 1)).astype(F32)
        dlf = jnp.dot(triu, dc, precision=HIGHEST, preferred_element_type=F32) + carry_ref[0:1, :]
        carry_ref[...] = jnp.broadcast_to(dlf[0:1, :], carry_ref.shape)
        x = t_ref[...] + b_ref[...]
        dfl = jnp.where((lane >= LANE_FL) & (lane < LANE_FL + HEADS), dlf * jax.nn.sigmoid(-x), 0.0)
        o_ref[...] = dfl
        db_ref[...] += jnp.sum(dfl, axis=0, keepdims=True)

    rev = pl.BlockSpec((br, HP), lambda i: (nb - 1 - i, 0))
    return _call("forget_bwd", body, (nb,),
                 [(proj_r, pl.BlockSpec((br, HP), lambda i: (nb - 1 - i, R_LAST // HP))), (bf_row, _whole((1, HP))),
                  (dcq_t, pl.BlockSpec((HEADS, br), lambda i: (0, nb - 1 - i))),
                  (dck_b, pl.BlockSpec((br, HW), lambda i: (nb - 1 - i, 0)))],
                 [(_sds((r, HP), F32), rev), (_sds((1, HP), F32), _whole((1, HP)))],
                 scratch=[pltpu.VMEM((8, HP), F32)], sem=("arbitrary",))


def _gate_fwd(proj_r, b_gate, bm, bfx):
    r, d = bm.shape
    br = ROW_BLOCK
    cb = R_GATE // d

    def body(gm_ref, gf_ref, b1_ref, b2_ref, bm_ref, bf_ref, o_ref):
        g1 = jax.nn.sigmoid(gm_ref[...] + b1_ref[...])
        g2 = jax.nn.sigmoid(gf_ref[...] + b2_ref[...])
        o_ref[...] = (g1 * bm_ref[...].astype(F32) + g2 * bf_ref[...].astype(F32)).astype(o_ref.dtype)

    b1 = b_gate[:d].reshape(1, d)
    b2 = b_gate[d:].reshape(1, d)
    return _call("gate_fwd", body, (r // br,),
                 [(proj_r, _rows(br, d, cb)), (proj_r, _rows(br, d, cb + 1)), (b1, _whole((1, d))), (b2, _whole((1, d))),
                  (bm, _rows(br, d)), (bfx, _rows(br, d))],
                 [(_sds((r, d), MXU_DTYPE), _rows(br, d))], sem=("parallel",))[0]


def _gate_bwd(proj_r, b_gate, bm, bfx, dmerged):
    r, d = bm.shape
    br = ROW_BLOCK
    cb = R_GATE // d

    def body(gm_ref, gf_ref, b1_ref, b2_ref, bm_ref, bf_ref, dm_ref, dbm_ref, dbf_ref, dgl_ref, dbg_ref):
        g1 = jax.nn.sigmoid(gm_ref[...] + b1_ref[...])
        g2 = jax.nn.sigmoid(gf_ref[...] + b2_ref[...])
        dm = dm_ref[...].astype(F32)
        dbm_ref[...] = (dm * g1).astype(dbm_ref.dtype)
        dbf_ref[...] = (dm * g2).astype(dbf_ref.dtype)
        dl1 = dm * bm_ref[...].astype(F32) * (g1 * (1.0 - g1))
        dl2 = dm * bf_ref[...].astype(F32) * (g2 * (1.0 - g2))
        dgl_ref[:, 0:d] = dl1.astype(dgl_ref.dtype)
        dgl_ref[:, d:2 * d] = dl2.astype(dgl_ref.dtype)

        @pl.when(pl.program_id(0) == 0)
        def _():
            dbg_ref[...] = jnp.zeros_like(dbg_ref)

        dbg_ref[:, 0:d] += jnp.sum(dl1, axis=0, keepdims=True)
        dbg_ref[:, d:2 * d] += jnp.sum(dl2, axis=0, keepdims=True)

    b1 = b_gate[:d].reshape(1, d)
    b2 = b_gate[d:].reshape(1, d)
    return _call("gate_bwd", body, (r // br,),
                 [(proj_r, _rows(br, d, cb)), (proj_r, _rows(br, d, cb + 1)), (b1, _whole((1, d))), (b2, _whole((1, d))),
                  (bm, _rows(br, d)), (bfx, _rows(br, d)), (dmerged, _rows(br, d))],
                 [(_sds((r, d), MXU_DTYPE), _rows(br, d)), (_sds((r, d), MXU_DTYPE), _rows(br, d)),
                  (_sds((r, 2 * d), MXU_DTYPE), _rows(br, 2 * d)), (_sds((1, 2 * d), F32), _whole((1, 2 * d)))],
                 sem=("arbitrary",))


HALO = 16
GLU_BWD_BLOCK = 256
COPY_ROWS = 512


def _conv_taps(gp, halo, first_block):
    halo = jnp.where(first_block, 0.0, halo.astype(F32))
    rid = lax.broadcasted_iota(jnp.int32, gp.shape, 0)
    last, prev = halo[HALO - 1:HALO, :], halo[HALO - 2:HALO - 1, :]
    g1 = jnp.where(rid == 0, last, pltpu.roll(gp, 1, axis=0))
    g2 = jnp.where(rid == 0, prev, jnp.where(rid == 1, last, pltpu.roll(gp, 2, axis=0)))
    return g1, g2


def _prev_halo(br, c):
    return pl.BlockSpec((HALO, c), lambda i: (jnp.maximum(i * (br // HALO) - 1, 0), 0))


def _glu_fwd(up, conv_w, conv_b):
    r = up.shape[0]
    c = D_FF
    br = ROW_BLOCK

    def body(gp_ref, halo_ref, val_ref, w_ref, b_ref, o_ref):
        gp = gp_ref[...].astype(F32)
        g1, g2 = _conv_taps(gp, halo_ref[...], pl.program_id(0) == 0)
        gate = w_ref[0:1, :] * g2 + w_ref[1:2, :] * g1 + w_ref[2:3, :] * gp + b_ref[...]
        o_ref[...] = (gate * jax.nn.sigmoid(gate) * val_ref[...].astype(F32)).astype(o_ref.dtype)

    return _call("glu_fwd", body, (r // br,),
                 [(up, _rows(br, c, 0)), (up, _prev_halo(br, c)), (up, _rows(br, c, 1)),
                  (conv_w, _whole((3, c))), (conv_b.reshape(1, c), _whole((1, c)))],
                 [(_sds((r, c), MXU_DTYPE), _rows(br, c))], sem=("parallel",))[0]


def _glu_bwd(up, conv_w, conv_b, d_act):
    r = up.shape[0]
    c = D_FF
    br = GLU_BWD_BLOCK
    nb = r // br

    def body(gp_ref, halo_ref, val_ref, da_ref, gpn_ref, valn_ref, dan_ref, w_ref, b_ref, o_ref, dw_ref, db_ref):
        i = pl.program_id(0)
        w0, w1, w2, bias = w_ref[0:1, :], w_ref[1:2, :], w_ref[2:3, :], b_ref[...]

        def d_gate(gp, g1, g2, val, da):
            gate = w0 * g2 + w1 * g1 + w2 * gp + bias
            sg = jax.nn.sigmoid(gate)
            return da * val * (sg * (1.0 + gate * (1.0 - sg))), da * (gate * sg)

        gp = gp_ref[...].astype(F32)
        g1, g2 = _conv_taps(gp, halo_ref[...], i == 0)
        dg, dv = d_gate(gp, g1, g2, val_ref[...].astype(F32), da_ref[...].astype(F32))
        gpn = gpn_ref[...].astype(F32)
        g1n, g2n = _conv_taps(gpn, gp[br - HALO:, :], False)
        dgn, _ = d_gate(gpn, g1n, g2n, valn_ref[...].astype(F32), dan_ref[...].astype(F32))
        dgn = jnp.where(i == nb - 1, 0.0, dgn)
        rid = lax.broadcasted_iota(jnp.int32, dg.shape, 0)
        u1 = jnp.where(rid == br - 1, dgn[0:1, :], pltpu.roll(dg, br - 1, axis=0))
        u2 = jnp.where(rid == br - 1, dgn[1:2, :], jnp.where(rid == br - 2, dgn[0:1, :], pltpu.roll(dg, br - 2, axis=0)))
        o_ref[:, 0:c] = (w2 * dg + w1 * u1 + w0 * u2).astype(o_ref.dtype)
        o_ref[:, c:2 * c] = dv.astype(o_ref.dtype)

        @pl.when(i == 0)
        def _():
            dw_ref[...] = jnp.zeros_like(dw_ref)
            db_ref[...] = jnp.zeros_like(db_ref)

        dw_ref[0:1, :] += jnp.sum(dg * g2, axis=0, keepdims=True)
        dw_ref[1:2, :] += jnp.sum(dg * g1, axis=0, keepdims=True)
        dw_ref[2:3, :] += jnp.sum(dg * gp, axis=0, keepdims=True)
        db_ref[...] += jnp.sum(dg, axis=0, keepdims=True)

    nxt = lambda cb: pl.BlockSpec((HALO, c), lambda i: (jnp.minimum((i + 1) * (br // HALO), r // HALO - 1), cb))
    return _call("glu_bwd", body, (nb,),
                 [(up, _rows(br, c, 0)), (up, _prev_halo(br, c)), (up, _rows(br, c, 1)), (d_act, _rows(br, c)),
                  (up, nxt(0)), (up, nxt(1)), (d_act, nxt(0)),
                  (conv_w, _whole((3, c))), (conv_b.reshape(1, c), _whole((1, c)))],
                 [(_sds((r, 2 * c), MXU_DTYPE), _rows(br, 2 * c)),
                  (_sds((8, c), F32), _whole((8, c))), (_sds((1, c), F32), _whole((1, c)))],
                 sem=("arbitrary",))


def _token_specs(seq, d):
    br = TOKEN_BLOCK
    nxb = seq // br
    main = pl.BlockSpec((br, d), lambda i: (jnp.minimum(i, nxb - 1), 0))
    tail = pl.BlockSpec((N_META, d), lambda i: (jnp.clip(i * (br // N_META) - 1, 0, seq // N_META - 1), 0))
    return main, tail


def _padded_block(main_ref, tail_ref, first, seq):
    br = TOKEN_BLOCK
    i = pl.program_id(0)
    nxb = seq // br
    main = jnp.where(i < nxb, main_ref[...], 0.0)
    head = jnp.where(i == 0, first, jnp.where(i <= nxb, tail_ref[...], 0.0))
    return jnp.concatenate([head, main[:br - N_META]], axis=0)


def _ln_emb_fwd(x, meta, g, b, rows, after=()):
    seq, d = x.shape
    br = TOKEN_BLOCK
    assert seq % br == 0 and br % N_META == 0 and rows % br == 0

    def body(x_ref, tail_ref, meta_ref, g_ref, b_ref, y_ref, yb_ref):
        z = _padded_block(x_ref, tail_ref, meta_ref[...], seq)
        xhat, _ = _ln_stats(z)
        y = xhat * g_ref[...] + b_ref[...]
        y_ref[...] = y
        yb_ref[...] = y.astype(yb_ref.dtype)

    main, tail = _token_specs(seq, d)
    return _call("ln_emb_fwd", body, (rows // br,),
                 [(x, main), (x, tail), (meta, _whole((N_META, d))), (g.reshape(1, d), _whole((1, d))),
                  (b.reshape(1, d), _whole((1, d)))],
                 [(_sds((rows, d), F32), _rows(br, d)), (_sds((rows, d), MXU_DTYPE), _rows(br, d))],
                 sem=("parallel",), after=after)


def _ln_emb_bwd(x, meta, dh0, g):
    seq, d = x.shape
    br = TOKEN_BLOCK
    step = br // N_META

    def ln_bwd(z, dy, gv):
        xhat, rstd = _ln_stats(z)
        dyg = dy * gv
        m1 = jnp.mean(dyg, axis=-1, keepdims=True)
        m2 = jnp.mean(dyg * xhat, axis=-1, keepdims=True)
        dz = rstd * (dyg - m1 - xhat * m2)
        return dz, jnp.sum(dy * xhat, axis=0, keepdims=True), jnp.sum(dy, axis=0, keepdims=True)

    def body(x_ref, dh_ref, nxt_ref, meta_ref, top_ref, g_ref, dx_ref, dm_ref, dg_ref, db_ref):
        gv = g_ref[...]
        dy = jnp.concatenate([dh_ref[N_META:, :], nxt_ref[...]], axis=0)
        dz, dg, db = ln_bwd(x_ref[...], dy, gv)
        dx_ref[...] = dz

        @pl.when(pl.program_id(0) == 0)
        def _():
            dzm, dgm, dbm = ln_bwd(meta_ref[...], top_ref[...], gv)
            dm_ref[...] = dzm
            dg_ref[...] = dgm
            db_ref[...] = dbm

        dg_ref[...] += dg
        db_ref[...] += db

    small = _whole((N_META, d))
    return _call("ln_emb_bwd", body, (seq // br,),
                 [(x, _rows(br, d)), (dh0, _rows(br, d)), (dh0, pl.BlockSpec((N_META, d), lambda i: ((i + 1) * step, 0))),
                  (meta, small), (dh0, small), (g.reshape(1, d), _whole((1, d)))],
                 [(_sds((seq, d), F32), _rows(br, d)), (_sds((N_META, d), F32), small),
                  (_sds((1, d), F32), _whole((1, d))), (_sds((1, d), F32), _whole((1, d)))], sem=("arbitrary",))


def _loss_err(a_ref, r_ref, t_ref, tail_ref, g_ref, b_ref, seq):
    br, d = a_ref.shape
    xhat, rstd = _ln_stats(ALPHA * a_ref[...] + r_ref[...])
    y = xhat * g_ref[...] + b_ref[...]
    t = _padded_block(t_ref, tail_ref, jnp.zeros((N_META, d), F32), seq)
    rid = lax.broadcasted_iota(jnp.int32, (br, d), 0) + pl.program_id(0) * br
    valid = (rid >= N_META) & (rid < N_META + seq)
    return jnp.where(valid, y - t, 0.0), xhat, rstd


def _ln_ffn_bwd(h1, f, tgt, g, b):
    r, d = h1.shape
    seq = tgt.shape[0]
    br = TOKEN_BLOCK

    def body(a_ref, r_ref, t_ref, tail_ref, g_ref, b_ref, dz_ref, dzb_ref, dg_ref, db_ref, l_ref):
        err, xhat, rstd = _loss_err(a_ref, r_ref, t_ref, tail_ref, g_ref, b_ref, seq)
        dyv = err * (1.0 / d)
        dyg = dyv * g_ref[...]
        m1 = jnp.mean(dyg, axis=-1, keepdims=True)
        m2 = jnp.mean(dyg * xhat, axis=-1, keepdims=True)
        dz = rstd * (dyg - m1 - xhat * m2)
        dz_ref[...] = dz
        dzb_ref[...] = dz.astype(dzb_ref.dtype)

        @pl.when(pl.program_id(0) == 0)
        def _():
            dg_ref[...] = jnp.zeros_like(dg_ref)
            db_ref[...] = jnp.zeros_like(db_ref)
            l_ref[...] = jnp.zeros_like(l_ref)

        dg_ref[...] += jnp.sum(dyv * xhat, axis=0, keepdims=True)
        db_ref[...] += jnp.sum(dyv, axis=0, keepdims=True)
        l_ref[...] += jnp.sum(jnp.sum(err * err, axis=1, keepdims=True), axis=0, keepdims=True) * (0.5 / d)

    main, tail = _token_specs(seq, d)
    return _call("ln_ffn_bwd", body, (r // br,),
                 [(h1, _rows(br, d)), (f, _rows(br, d)), (tgt, main), (tgt, tail),
                  (g.reshape(1, d), _whole((1, d))), (b.reshape(1, d), _whole((1, d)))],
                 [(_sds((r, d), F32), _rows(br, d)), (_sds((r, d), MXU_DTYPE), _rows(br, d)),
                  (_sds((1, d), F32), _whole((1, d))), (_sds((1, d), F32), _whole((1, d))),
                  (_sds((1, 1), F32), _whole((1, 1)))], sem=("arbitrary",))


def _attn_fwd(name, q, k, v, cum_b=None, cum_t=None):
    (qa, qg), (ka, kg), (va, vg) = q, k, v
    r = qa.shape[0]
    tq, tk = ATT_TQ, ATT_TK
    nq, nk = r // tq, r // tk
    bias = cum_b is not None

    def body(*refs):
        if bias:
            q_ref, k_ref, vt_ref, cb_ref, ct_ref, o_ref, ob_ref, lse_ref = refs
        else:
            q_ref, k_ref, vt_ref, o_ref, ob_ref, lse_ref = refs
        i = pl.program_id(1)
        qs = [q_ref[:, _hs(hh)] for hh in range(hg)]
        cqs = [ct_ref[hh] for hh in range(hg)] if bias else None
        diff = lax.broadcasted_iota(jnp.int32, (tk, tq), 0) - lax.broadcasted_iota(jnp.int32, (tk, tq), 1)

        def step(j, carry, masked):
            keys = pl.ds(pl.multiple_of(j * tk, tk), tk)
            out = []
            for hh in range(hg):
                m, l, acc = carry[hh]
                kt = k_ref[keys, _hs(hh)]
                s = lax.dot_general(kt, qs[hh], NT, preferred_element_type=F32)
                if bias:
                    s = s + (cqs[hh] - cb_ref[keys, hh * HP:hh * HP + 1])
                if masked:
                    s = jnp.where(diff <= i * tq - j * tk, s, NEG_INF)
                m_new = jnp.maximum(m, jnp.max(s, axis=0, keepdims=True))
                p = jnp.exp(s - m_new)
                a = jnp.exp(m - m_new)
                l = a * l + jnp.sum(p, axis=0, keepdims=True)
                acc = a * acc + jnp.dot(vt_ref[j, _hs(hh), :], p.astype(kt.dtype), preferred_element_type=F32)
                out.append((m_new, l, acc))
            return tuple(out)

        n_clear = (i * tq + 1) // tk
        n_all = ((i + 1) * tq - 1) // tk + 1
        carry = tuple((jnp.full((1, tq), NEG_INF, F32), jnp.zeros((1, tq), F32), jnp.zeros((HP, tq), F32))
                      for _ in range(hg))
        carry = lax.fori_loop(0, n_clear, lambda j, c: step(j, c, False), carry)
        carry = lax.fori_loop(n_clear, n_all, lambda j, c: step(j, c, True), carry)
        for hh in range(hg):
            m, l, acc = carry[hh]
            o = (acc / l).T
            o_ref[:, _hs(hh)] = o
            ob_ref[:, hh * V_DIM:(hh + 1) * V_DIM] = o[:, :V_DIM].astype(ob_ref.dtype)
            lse_ref[hh] = m + jnp.log(l)

    hg = ATT_HEADS_FWD
    w = hg * HP
    gpw = HW // w
    tile = lambda g: pl.BlockSpec((tq, w), lambda h, i: (i, g * gpw + h))
    res = lambda g: pl.BlockSpec((r, w), lambda h, i: (0, g * gpw + h))
    v_t = _key_tiles_transposed(name + "_vt", va, vg)
    ins = [(qa, tile(qg)), (ka, res(kg)), (v_t, pl.BlockSpec((nk, w, tk), lambda h, i: (0, h, 0)))]
    if bias:
        ins += [(cum_b, res(0)),
                (cum_t.reshape(HEADS, nq, 1, tq), pl.BlockSpec((hg, None, 1, tq), lambda h, i: (h, i, 0, 0)))]
    outs = [(_sds((r, HW), F32), tile(0)),
            (_sds((r, HEADS * V_DIM), MXU_DTYPE), pl.BlockSpec((tq, hg * V_DIM), lambda h, i: (i, h))),
            (_sds((HEADS, nq, 1, tq), F32), pl.BlockSpec((hg, None, 1, tq), lambda h, i: (h, i, 0, 0)))]
    o, ob, lse = _call(name, body, (gpw, nq), ins, outs, sem=("parallel", "parallel"))
    return o, ob, lse.reshape(HEADS, r)


def _key_tiles_transposed(name, a, group):
    r = a.shape[0]
    tk = ATT_TK

    def body(x_ref, o_ref):
        for h in range(HEADS):
            o_ref[_hs(h), :] = x_ref[:, _hs(h)].astype(F32).T.astype(o_ref.dtype)

    return _call(name, body, (r // tk,),
                 [(a, pl.BlockSpec((tk, HW), lambda j: (j, group)))],
                 [(_sds((r // tk, HW, tk), a.dtype), pl.BlockSpec((None, HW, tk), lambda j: (j, 0, 0)))],
                 sem=("parallel",))[0]


def _attn_bwd(name, q, k, v, do_b, o, lse_t, cum_b=None, cum_t=None, out_dtype=F32, after=()):
    (qa, qg), (ka, kg), (va, vg) = q, k, v
    r = qa.shape[0]
    tq, tk = ATT_TQ, ATT_TK
    nq, nk = r // tq, r // tk
    bias = cum_b is not None

    def body(*refs):
        if bias:
            (q_ref, k_ref, v_ref, do_ref, o_ref, lse_ref, cb_ref, ct_ref,
             dq_ref, dk_ref, dv_ref, dcq_ref, dck_ref, dqt_ref, dl_ref) = refs
        else:
            q_ref, k_ref, v_ref, do_ref, o_ref, lse_ref, dq_ref, dk_ref, dv_ref, dqt_ref, dl_ref = refs
        j = pl.program_id(1)

        @pl.when(j == 0)
        def _():
            dqt_ref[...] = jnp.zeros_like(dqt_ref)
            if bias:
                dcq_ref[...] = jnp.zeros_like(dcq_ref)
            for hh in range(hg):
                for i in range(nq):
                    rows = slice(i * tq, (i + 1) * tq)
                    prod = do_ref[rows, _hs(hh)].astype(F32) * o_ref[rows, _hs(hh)]
                    dl_ref[hh, i] = jnp.sum(prod.T, axis=0, keepdims=True)

        kts = [k_ref[:, _hs(hh)] for hh in range(hg)]
        vts = [v_ref[:, _hs(hh)] for hh in range(hg)]
        k_trs = [kt.astype(F32).T.astype(kt.dtype) for kt in kts]
        cks = [cb_ref[:, hh * HP:hh * HP + 1] for hh in range(hg)] if bias else None
        diff = lax.broadcasted_iota(jnp.int32, (tk, tq), 0) - lax.broadcasted_iota(jnp.int32, (tk, tq), 1)

        def step(i, carry, masked):
            rows = pl.ds(pl.multiple_of(i * tq, tq), tq)
            out = []
            for hh in range(hg):
                dk_acc, dv_acc, dck_acc = carry[hh]
                qt = q_ref[rows, _hs(hh)]
                dot = do_ref[rows, _hs(hh)]
                s = lax.dot_general(kts[hh], qt, NT, preferred_element_type=F32)
                if bias:
                    s = s + (ct_ref[hh, i] - cks[hh])
                if masked:
                    s = jnp.where(diff <= i * tq - j * tk, s, NEG_INF)
                p = jnp.exp(s - lse_ref[hh, i])
                dp = lax.dot_general(vts[hh], dot, NT, preferred_element_type=F32)
                ds = p * (dp - dl_ref[hh, i])
                pb = p.astype(dot.dtype)
                dsb = ds.astype(qt.dtype)
                dv_acc = dv_acc + jnp.dot(pb, dot, preferred_element_type=F32)
                dk_acc = dk_acc + jnp.dot(dsb, qt, preferred_element_type=F32)
                dqt_ref[hh, i] += jnp.dot(k_trs[hh], dsb, preferred_element_type=F32)
                if bias:
                    dcq_ref[hh, i] += jnp.sum(ds, axis=0, keepdims=True)
                    dck_acc = dck_acc - jnp.sum(ds, axis=1, keepdims=True)
                out.append((dk_acc, dv_acc, dck_acc))
            return tuple(out)

        i_first = (j * tk) // tq
        i_clear = jnp.minimum(((j + 1) * tk + tq - 2) // tq, nq)
        carry = tuple((jnp.zeros((tk, HP), F32), jnp.zeros((tk, HP), F32), jnp.zeros((tk, 1), F32)) for _ in range(hg))
        carry = lax.fori_loop(i_first, i_clear, lambda i, c: step(i, c, True), carry)
        carry = lax.fori_loop(i_clear, nq, lambda i, c: step(i, c, False), carry)
        for hh in range(hg):
            dk_acc, dv_acc, dck_acc = carry[hh]
            dk_ref[:, _hs(hh)] = dk_acc.astype(dk_ref.dtype)
            dv_ref[:, _hs(hh)] = dv_acc.astype(dv_ref.dtype)
            if bias:
                dck_ref[:, _hs(hh)] = jnp.broadcast_to(dck_acc, (tk, HP))

        @pl.when(j == nk - 1)
        def _():
            for hh in range(hg):
                for i in range(nq):
                    dq_ref[i * tq:(i + 1) * tq, _hs(hh)] = dqt_ref[hh, i].T.astype(dq_ref.dtype)

    hg = ATT_HEADS
    w = hg * HP
    gpw = HW // w
    res = lambda g: pl.BlockSpec((r, w), lambda h, j: (0, g * gpw + h))
    tile = lambda g: pl.BlockSpec((tk, w), lambda h, j: (j, g * gpw + h))
    rowv = pl.BlockSpec((hg, nq, 1, tq), lambda h, j: (h, 0, 0, 0))
    as_rows = lambda a: a.reshape(HEADS, nq, 1, tq)
    ins = [(qa, res(qg)), (ka, tile(kg)), (va, tile(vg)), (do_b, res(0)), (o, res(0)), (as_rows(lse_t), rowv)]
    outs = [(_sds((r, HW), out_dtype), res(0)), (_sds((r, HW), out_dtype), tile(0)), (_sds((r, HW), out_dtype), tile(0))]
    if bias:
        ins += [(cum_b, tile(0)), (as_rows(cum_t), rowv)]
        outs += [(_sds((HEADS, nq, 1, tq), F32), rowv), (_sds((r, HW), F32), tile(0))]
    res_out = _call(name, body, (gpw, nk), ins, outs,
                    scratch=[pltpu.VMEM((hg, nq, HP, tq), F32), pltpu.VMEM((hg, nq, 1, tq), F32)],
                    sem=("parallel", "arbitrary"), after=after)
    if bias:
        dq, dk, dv, dcq, dck = res_out
        return dq, dk, dv, dcq.reshape(HEADS, r), dck
    return res_out


MESH_ID = pl.DeviceIdType.MESH
ANY = pl.BlockSpec(memory_space=pl.ANY)


N_GATHER_COPIES = 8


def _allgather(name, shards):
    n = len(shards)

    def body(*refs):
        x_refs, out_refs = refs[:n], refs[n:2 * n]
        send_sems, recv_sems, local_sems = refs[2 * n:]
        x, y, c = lax.axis_index("x"), lax.axis_index("y"), lax.axis_index("c")
        me, sibling = (x, y, c), (x, y, 1 - c)
        xn, yn, dg = (1 - x, y, c), (x, 1 - y, c), (1 - x, 1 - y, c)
        other = lambda dev: (dev[0], dev[1], 1 - c)

        def slot(ti, dev, half=None):
            ref = out_refs[ti].at[4 * dev[0] + 2 * dev[1] + dev[2]]
            if half is None:
                return ref
            rows = shards[ti].shape[0] // 2
            return ref.at[pl.ds(half * rows, rows)]

        def copy(ti, k, block, to, half=None, src=None):
            return pltpu.make_async_remote_copy(
                src_ref=slot(ti, block, half) if src is None else src, dst_ref=slot(ti, block, half),
                send_sem=send_sems.at[ti, k], recv_sem=recv_sems.at[ti, k], device_id=to, device_id_type=MESH_ID)

        mine = [pltpu.make_async_copy(x_refs[ti], slot(ti, me), local_sems.at[ti]) for ti in range(n)]
        for cp in mine:
            cp.start()
        started = []

        def go(cp):
            cp.start()
            started.append(cp)

        for ti in range(n):
            go(copy(ti, 0, me, sibling, src=x_refs[ti]))
            go(copy(ti, 1, me, xn, src=x_refs[ti]))
            go(copy(ti, 2, me, yn, src=x_refs[ti]))
        for ti in range(n):
            copy(ti, 1, xn, me).wait_recv()
            go(copy(ti, 3, xn, yn, half=0))
            go(copy(ti, 5, xn, sibling))
            copy(ti, 2, yn, me).wait_recv()
            go(copy(ti, 4, yn, xn, half=1))
            go(copy(ti, 6, yn, sibling))
        for ti in range(n):
            copy(ti, 3, dg, me, half=0).wait_recv()
            copy(ti, 4, dg, me, half=1).wait_recv()
            go(copy(ti, 7, dg, sibling))
        for ti in range(n):
            copy(ti, 0, sibling, me).wait_recv()
            for k, dev in ((5, xn), (6, yn), (7, dg)):
                copy(ti, k, other(dev), me).wait_recv()
        for cp in started:
            cp.wait_send()
        for cp in mine:
            cp.wait()

    sems = pltpu.SemaphoreType.DMA((n, N_GATHER_COPIES))
    return pl.pallas_call(
        body, name=name, out_shape=[_sds((N_DEV,) + s.shape, s.dtype) for s in shards],
        in_specs=[ANY] * n, out_specs=[ANY] * n,
        scratch_shapes=[sems, sems, pltpu.SemaphoreType.DMA((n,))],
    )(*shards)


HBM = pl.BlockSpec(memory_space=pltpu.HBM)
SEM = pl.BlockSpec(memory_space=pltpu.SEMAPHORE)
EFFECT = pltpu.SideEffectType.DATAFLOW_SIDE_EFFECTING
N_PEER = N_DEV - 1


def _my_id():
    return 4 * lax.axis_index("x") + 2 * lax.axis_index("y") + lax.axis_index("c")


def _peers():
    x, y, c = lax.axis_index("x"), lax.axis_index("y"), lax.axis_index("c")
    out = []
    for k in range(1, N_DEV):
        px, py, pc = (1 - x if k & 4 else x, 1 - y if k & 2 else y, 1 - c if k & 1 else c)
        out.append(((px, py, pc), 4 * px + 2 * py + pc))
    return out


def _push_copies(src_refs, land_refs, send_sems, recv_sems, scatter, landing):
    me = _my_id()
    out = []
    for ti, (src, land) in enumerate(zip(src_refs, land_refs)):
        for k, (dev, pid) in enumerate(_peers()):
            out.append(pltpu.make_async_remote_copy(
                src_ref=src.at[pid] if scatter else src, dst_ref=land.at[pid if landing else me],
                send_sem=send_sems.at[ti * N_PEER + k], recv_sem=recv_sems.at[ti * N_PEER + k],
                device_id=dev, device_id_type=MESH_ID))
    return out


def _push_start(name, groups, scatter, after=None):
    sizes = [len(g) for g in groups]
    srcs = [a for g in groups for a in g]
    n = len(srcs)
    slot = lambda s: s.shape[1:] if scatter else s.shape
    lands = [lax.empty((N_DEV,) + slot(s), s.dtype) for s in srcs]
    n_after = 0 if after is None else 1
    n_grp = len(groups)

    def body(*refs):
        src_refs, land_refs = refs[:n], refs[n:2 * n]
        sems = refs[2 * n + n_after:2 * n + n_after + 2 * n_grp]
        token = refs[-1]
        lo = 0
        for gi, sz in enumerate(sizes):
            for cp in _push_copies(src_refs[lo:lo + sz], land_refs[lo:lo + sz], sems[2 * gi], sems[2 * gi + 1], scatter, False):
                cp.start()
            lo += sz
        token[...] = jnp.zeros_like(token)

    hbm = lambda a: pltpu.with_memory_space_constraint(a, pltpu.HBM)
    operands = [hbm(a) for a in srcs + lands] + ([after] if n_after else [])
    sem_shapes = [pltpu.SemaphoreType.DMA((sz * N_PEER,)) for sz in sizes for _ in range(2)]
    res = pl.pallas_call(
        body, name=name,
        out_shape=sem_shapes + [pltpu.HBM(a.shape, a.dtype) for a in srcs + lands] + [_sds((8, 128), F32)],
        in_specs=[HBM] * (2 * n) + [ANY] * n_after,
        out_specs=[SEM] * (2 * n_grp) + [HBM] * (2 * n) + [pl.BlockSpec(memory_space=pltpu.VMEM)],
        input_output_aliases={i: 2 * n_grp + i for i in range(2 * n)},
        compiler_params=pltpu.CompilerParams(has_side_effects=EFFECT),
    )(*operands)
    thru = res[2 * n_grp:2 * n_grp + 2 * n]
    handles, lo = [], 0
    for gi, sz in enumerate(sizes):
        handles.append((res[2 * gi], res[2 * gi + 1], list(thru[lo:lo + sz]), list(thru[n + lo:n + lo + sz]), scatter))
        lo += sz
    return handles, res[-1]


def _push_wait(name, handle, after):
    send_sems, recv_sems, srcs, lands, scatter = handle
    n = len(srcs)

    def body(*refs):
        src_refs, land_refs = refs[:n], refs[n:2 * n]
        s_sems, r_sems = refs[2 * n], refs[2 * n + 1]
        for cp in _push_copies(src_refs, land_refs, s_sems, r_sems, scatter, True):
            cp.wait_send()
            cp.wait_recv()

    res = pl.pallas_call(
        body, name=name,
        out_shape=[pltpu.HBM(a.shape, a.dtype) for a in srcs + lands],
        in_specs=[HBM] * (2 * n) + [SEM, SEM, ANY], out_specs=[HBM] * (2 * n),
        input_output_aliases={i: i for i in range(2 * n)},
        compiler_params=pltpu.CompilerParams(has_side_effects=EFFECT),
    )(*srcs, *lands, send_sems, recv_sems, after)
    return list(res[n:])


def _adamw(name, parts, w, m, v, own=None):
    r, c = w.shape
    br = _pick(r, COPY_ROWS, 16)
    has_own = own is not None

    def body(*refs):
        if has_own:
            p_ref, own_ref, w_ref, m_ref, v_ref, g_ref, d_ref, nm_ref, nv_ref = refs
            me = _my_id()
            mine = own_ref[...].astype(F32)
        else:
            p_ref, w_ref, m_ref, v_ref, g_ref, d_ref, nm_ref, nv_ref = refs
        g = None
        for k in range(N_DEV):
            t = p_ref[k].astype(F32)
            if has_own:
                t = jnp.where(me == k, mine, t)
            g = t if g is None else g + t
        mm = ADAM_B1 * m_ref[...] + (1.0 - ADAM_B1) * g
        vv = ADAM_B2 * v_ref[...] + (1.0 - ADAM_B2) * (g * g)
        m_hat = mm / (1.0 - ADAM_B1 ** ADAM_STEP)
        v_hat = vv / (1.0 - ADAM_B2 ** ADAM_STEP)
        g_ref[...] = g
        d_ref[...] = -ADAM_LR * (m_hat / (jnp.sqrt(v_hat) + ADAM_EPS) + ADAM_WD * w_ref[...])
        nm_ref[...] = mm
        nv_ref[...] = vv

    spec = _rows(br, c)
    out = (_sds((r, c), F32), spec)
    ins = [(parts, pl.BlockSpec((N_DEV, br, c), lambda i: (0, i, 0)))] + ([(own, spec)] if has_own else [])
    return _call(name, body, (r // br,), ins + [(w, spec), (m, spec), (v, spec)], [out] * 4, sem=("parallel",))


def _grad_sum(name, parts, own):
    r, c = own.shape
    br = _pick(r, COPY_ROWS, 16)

    def body(p_ref, own_ref, g_ref):
        me = _my_id()
        mine = own_ref[...].astype(F32)
        g = None
        for k in range(N_DEV):
            t = jnp.where(me == k, mine, p_ref[k].astype(F32))
            g = t if g is None else g + t
        g_ref[...] = g.T

    spec = _rows(br, c)
    return _call(name, body, (r // br,), [(parts, pl.BlockSpec((N_DEV, br, c), lambda i: (0, i, 0))), (own, spec)],
                 [(_sds((c, r), F32), pl.BlockSpec((c, br), lambda i: (0, i)))], sem=("parallel",))[0]


def _adamw_transposed(name, parts, w, m, v, own):
    r, c = w.shape
    bl = _pick(r, COPY_ROWS, HP)
    gt = _grad_sum(name + "_sum", parts, own)

    def body(g_ref, w_ref, m_ref, v_ref, d_ref, nm_ref, nv_ref):
        g = g_ref[...]
        mm = ADAM_B1 * m_ref[...] + (1.0 - ADAM_B1) * g
        vv = ADAM_B2 * v_ref[...] + (1.0 - ADAM_B2) * (g * g)
        m_hat = mm / (1.0 - ADAM_B1 ** ADAM_STEP)
        v_hat = vv / (1.0 - ADAM_B2 ** ADAM_STEP)
        d_ref[...] = -ADAM_LR * (m_hat / (jnp.sqrt(v_hat) + ADAM_EPS) + ADAM_WD * w_ref[...])
        nm_ref[...] = mm
        nv_ref[...] = vv

    spec = pl.BlockSpec((c, bl), lambda i: (0, i))
    out = (_sds((c, r), F32), spec)
    d, nm, nv = _call(name, body, (r // bl,), [(gt, spec), (w.T, spec), (m.T, spec), (v.T, spec)], [out] * 3,
                      sem=("parallel",))
    return gt.T, d.T, nm.T, nv.T, gt


def _pad_head_cols(w, d):
    k = w.shape[0]
    return jnp.pad(w.reshape(k, HEADS, d), ((0, 0), (0, 0), (0, HP - d))).reshape(k, HW)


def _unpad_head_cols(wp, d):
    k = wp.shape[0]
    return wp.reshape(k, HEADS, HP)[:, :, :d].reshape(k, HEADS * d)


def _pad_head_rows(w, d):
    n = w.shape[1]
    return jnp.pad(w.reshape(HEADS, d, n), ((0, 0), (0, HP - d), (0, 0))).reshape(HW, n)


def _w_in_runs():
    nat = {}
    o = 0
    for nm, wd in (("q", Q_RANK), ("kv", KV_RANK), ("kr", ROPE), ("fq", FOX_W), ("fk", FOX_W), ("fv", FOX_W),
                   ("fl", HEADS), ("gate", 2 * D_MODEL)):
        nat[nm] = o
        o += wd
    runs = [(1, R_QLAT, nat["q"], Q_RANK, 1.0), (1, R_KVLAT, nat["kv"], KV_RANK, 1.0),
            (1, R_LAST + LANE_FL, nat["fl"], HEADS, 1.0), (1, R_LAST + LANE_PE, nat["kr"], ROPE, 1.0),
            (1, R_GATE, nat["gate"], 2 * D_MODEL, 1.0)]
    for grp, (nm, sc) in enumerate((("fq", FOX_SCALE), ("fk", 1.0), ("fv", 1.0))):
        runs.append((0, grp * FOX_W, nat[nm], FOX_W, sc))
    return runs


def _head_pad_moves(pad):
    moves = []
    for grp in range(3):
        for h in range(HEADS):
            narrow, wide = grp * FOX_W + h * FOX_DIM, h * HP
            if pad:
                moves.append((0, None, grp * HW + wide, 0, None, narrow, FOX_DIM, 1.0))
            else:
                moves.append((0, None, narrow, grp, None, wide, FOX_DIM, 1.0))
    return moves


def _sharded_runs(runs, shard_cols):
    out = []
    for half, col, ncol, width, sc in runs:
        while width > 0:
            d, local = divmod(ncol, shard_cols)
            wd = min(width, shard_cols - local)
            out.append((half, col, d, local, wd, sc))
            col, ncol, width = col + wd, ncol + wd, width - wd
    return out


def _remap(name, srcs, out_shapes, moves):
    rows = srcs[0].shape[-2]
    br = _pick(rows, COPY_ROWS, 16)
    ns = len(srcs)

    def spec(shape):
        if len(shape) == 2:
            return pl.BlockSpec((br, shape[1]), lambda i: (i, 0))
        return pl.BlockSpec((shape[0], br, shape[2]), lambda i: (0, i, 0))

    covered = [sum(m[6] for m in moves if m[0] == di) for di in range(len(out_shapes))]
    has_gaps = [cov < (shape[1] if len(shape) == 2 else shape[0] * shape[2])
                for cov, (shape, _) in zip(covered, out_shapes)]

    def body(*refs):
        s_refs, o_refs = refs[:ns], refs[ns:]
        for o, gaps in zip(o_refs, has_gaps):
            if gaps:
                o[...] = jnp.zeros_like(o)
        for di, dl, dc, si, sl, sc0, wd, scale in moves:
            v = s_refs[si][:, sc0:sc0 + wd] if sl is None else s_refs[si][sl, :, sc0:sc0 + wd]
            if scale != 1.0:
                v = v * jnp.asarray(scale, v.dtype)
            v = v.astype(o_refs[di].dtype)
            if dl is None:
                o_refs[di][:, dc:dc + wd] = v
            else:
                o_refs[di][dl, :, dc:dc + wd] = v

    return _call(name, body, (rows // br,), [(a, spec(a.shape)) for a in srcs],
                 [(_sds(shape, dt), spec(shape)) for shape, dt in out_shapes], sem=("parallel",))


def _w_in_from_shards(g3):
    n, rows, c = g3.shape
    moves = [(half, None, col, 0, d, local, wd, sc) for half, col, d, local, wd, sc in _sharded_runs(_w_in_runs(), c)]
    return _remap("w_in_repack", [g3], [((rows, F_W), g3.dtype), ((rows, R_W), g3.dtype)], moves)


def _w_in_grad_to_shards(d_fused, d_rest, n, c):
    rows = d_fused.shape[0]
    moves = [(0, d, local, half, None, col, wd, sc) for half, col, d, local, wd, sc in _sharded_runs(_w_in_runs(), c)]
    return _remap("w_in_grad_unpack", [d_fused, d_rest], [((n, rows, c), d_fused.dtype)], moves)[0]


def _rows_from_shards(name, land, own):
    n, rows, c = land.shape

    def body(land_ref, own_ref, o_ref):
        o_ref[...] = jnp.where(_my_id() == pl.program_id(0), own_ref[...], land_ref[...])

    return _call(name, body, (n,),
                 [(land, pl.BlockSpec((None, rows, c), lambda d: (d, 0, 0))), (own, _whole((rows, c)))],
                 [(_sds((n * rows, c), land.dtype), pl.BlockSpec((rows, c), lambda d: (d, 0)))], sem=("parallel",))[0]


def _cols_from_shards(name, land, own):
    n, rows, c = land.shape
    br = _pick(rows, COPY_ROWS, 16)

    def body(land_ref, own_ref, o_ref):
        me = _my_id()
        for d in range(n):
            o_ref[:, c * d:c * (d + 1)] = jnp.where(me == d, own_ref[...], land_ref[d])

    return _call(name, body, (rows // br,),
                 [(land, pl.BlockSpec((n, br, c), lambda i: (0, i, 0))), (own, _rows(br, c))],
                 [(_sds((rows, n * c), land.dtype), _rows(br, n * c))], sem=("parallel",))[0]


def _cols_to_shards(name, full, n):
    rows, nc = full.shape
    c = nc // n
    return _remap(name, [full], [((n, rows, c), full.dtype)], [(0, d, 0, 0, None, c * d, c, 1.0) for d in range(n)])[0]


class _NoComm:
    first_token = ()

    def late_weights(self, group, after):
        return {}

    def send(self, name, grads):
        return ()


def _local_step(x, tgt, p, comm=_NoComm()):
    seq = x.shape[0]
    r = -(-(N_META + seq) // ROW_ALIGN) * ROW_ALIGN
    cd = MXU_DTYPE
    p = dict(p)

    w_f, w_r = p["w_in"]

    pos = jnp.arange(r, dtype=F32)
    inv_freq = ROPE_THETA ** (-jnp.arange(HALF, dtype=F32) / HALF)
    ang = pos[:, None] * inv_freq[None, :]
    cos_t = jnp.tile(jnp.cos(ang), (1, HP // HALF))
    sin_t = jnp.tile(jnp.sin(ang), (1, HP // HALF))
    bf_row = jnp.zeros((1, HP), F32).at[0, LANE_FL:LANE_FL + HEADS].set(p["b_forget"])

    h0, h0b = _ln_emb_fwd(x, p["meta_tokens"], p["ln_emb_g"], p["ln_emb_b"], r, after=comm.first_token)
    proj_f = _matmul("in_proj_f", h0b, w_f, out_dtype=cd)
    proj_f = _remap("proj_f_pad", [proj_f], [((r, 3 * HW), cd)], _head_pad_moves(True))[0]
    proj_r = _matmul("in_proj_r", h0b, w_r)
    latent_gains = (p["q_norm_g"], p["kv_norm_g"])
    ql, kvl = _latent_norm_fwd(proj_r, latent_gains)
    p.update(comm.late_weights("qkv", ql))
    w_q = _pad_head_cols(p["w_q_up"], QK_DIM)
    w_kv = p["w_kv_up"]
    q_raw = _matmul("q_up", ql, w_q)
    kv = _matmul("kv_up", kvl, w_kv, out_dtype=cd)
    q_mla, k_mla, v_mla = _rope_fwd(q_raw, kv, proj_r, cos_t, sin_t)
    o_mla, o_mla_b, lse_mla = _attn_fwd("mla_fwd", (q_mla, 0), (k_mla, 0), (v_mla, 0))

    cum, cum_t = _forget_fwd(proj_r, bf_row)
    o_fox, o_fox_b, lse_fox = _attn_fwd("fox_fwd", (proj_f, 0), (proj_f, 1), (proj_f, 2), cum, cum_t)

    p.update(comm.late_weights("mix", o_fox_b))
    w_bm = _pad_head_rows(p["w_branch_mla"], V_DIM)
    w_bf = _pad_head_rows(p["w_branch_fox"], FOX_DIM)
    bm = _matmul("branch_mla", o_mla_b, p["w_branch_mla"], out_dtype=cd)
    bfx = _matmul("branch_fox", o_fox_b, p["w_branch_fox"], out_dtype=cd)
    merged = _gate_fwd(proj_r, p["b_gate"], bm, bfx)
    mix = _matmul("out_proj", merged, p["w_out"])
    h1, h1b = _ln_fwd("ln_mix_fwd", h0, mix, p["ln_mix_g"], p["ln_mix_b"])
    p.update(comm.late_weights("ffn", h1b))
    up = _matmul("ffn_up", h1b, p["w_ffn_up"], out_dtype=cd)
    act = _glu_fwd(up, p["conv_w"], p["conv_b"])
    f = _matmul("ffn_down", act, p["w_ffn_down"])

    g = {}
    dz2, dz2b, g["ln_ffn_g"], g["ln_ffn_b"], loss = _ln_ffn_bwd(h1, f, tgt, p["ln_ffn_g"], p["ln_ffn_b"])
    d_act = _matmul("ffn_down_dx", dz2b, p["w_ffn_down"], tb=True, out_dtype=cd)
    g["w_ffn_down"] = _matmul("ffn_down_dw", act, dz2b, ta=True, out_dtype=cd)
    d_up, dcw, g["conv_b"] = _glu_bwd(up, p["conv_w"], p["conv_b"], d_act)
    g["conv_w"] = dcw[:3]
    dh1 = _matmul("ffn_up_dx", d_up, p["w_ffn_up"], tb=True, addend=dz2, alpha=ALPHA)
    g["w_ffn_up"] = _matmul("ffn_up_dw", h1b, d_up, ta=True, out_dtype=cd)
    sent = comm.send("ffn", {n: g[n] for n in ("w_ffn_down", "w_ffn_up", "conv_w")})
    dz1, dz1b, g["ln_mix_g"], g["ln_mix_b"] = _ln_bwd("ln_mix_bwd", h0, mix, dh1, p["ln_mix_g"], after=sent)
    dmerged = _matmul("out_proj_dx", dz1b, p["w_out"], tb=True, out_dtype=cd)
    g["w_out"] = _matmul("out_proj_dw", merged, dz1b, ta=True, out_dtype=cd)
    d_bm, d_bf, d_gl, g["b_gate"] = _gate_bwd(proj_r, p["b_gate"], bm, bfx, dmerged)
    do_mla_b = _matmul("branch_mla_dx", d_bm, w_bm, tb=True, out_dtype=cd)
    g["w_branch_mla"] = _matmul("branch_mla_dw", o_mla_b, d_bm, ta=True, out_dtype=cd)
    do_fox_b = _matmul("branch_fox_dx", d_bf, w_bf, tb=True, out_dtype=cd)
    g["w_branch_fox"] = _matmul("branch_fox_dw", o_fox_b, d_bf, ta=True, out_dtype=cd)

    sent = comm.send("mix", {n: g[n] for n in ("w_out", "w_branch_mla", "w_branch_fox")})
    dq_m, dk_m, dv_m = _attn_bwd("mla_bwd", (q_mla, 0), (k_mla, 0), (v_mla, 0), do_mla_b, o_mla, lse_mla, after=sent)
    dfq, dfk, dfv, dcq, dck = _attn_bwd("fox_bwd", (proj_f, 0), (proj_f, 1), (proj_f, 2), do_fox_b, o_fox, lse_fox,
                                        cum, cum_t, out_dtype=cd)
    dfl, dbf = _forget_bwd(proj_r, bf_row, dcq, dck)
    g["b_forget"] = dbf[:, LANE_FL:LANE_FL + HEADS]

    dq_b, dkv_b, dlast = _rope_bwd(dq_m, dk_m, dv_m, dfl, cos_t, sin_t)
    d_ql = _matmul("q_up_dx", dq_b, w_q, tb=True)
    d_kvl = _matmul("kv_up_dx", dkv_b, w_kv, tb=True)
    d_qlat, d_kvlat, g["q_norm_g"], g["kv_norm_g"] = _latent_norm_bwd(proj_r, (d_ql, d_kvl), latent_gains)
    side_by_side = lambda parts, cols: [(0, None, c0, si, None, 0, a.shape[1], 1.0) for si, (a, c0) in enumerate(zip(parts, cols))]
    dproj_f = _remap("dproj_f_pack", [dfq, dfk, dfv], [((r, F_W), cd)], _head_pad_moves(False))[0]
    rest_parts = [d_qlat, d_kvlat, dlast, d_gl]
    dproj_r = _remap("dproj_r_pack", rest_parts, [((r, R_W), cd)],
                     side_by_side(rest_parts, (R_QLAT, R_KVLAT, R_LAST, R_GATE)))[0]
    g["w_in"] = (_matmul("in_proj_f_dw", h0b, dproj_f, ta=True, out_dtype=cd),
                 _matmul("in_proj_r_dw", h0b, dproj_r, ta=True, out_dtype=cd))
    sent = comm.send("in", {"w_in": g["w_in"]})
    dh0 = _matmul("in_proj_f_dx", dproj_f, w_f, tb=True, addend=dz1, alpha=ALPHA, after=sent)
    g["w_q_up"] = _unpad_head_cols(_matmul("q_up_dw", ql, dq_b, ta=True, out_dtype=cd, after=sent), QK_DIM)
    g["w_kv_up"] = _matmul("kv_up_dw", kvl, dkv_b, ta=True, out_dtype=cd, after=sent)
    sent = comm.send("qkv", {n: g[n] for n in ("w_q_up", "w_kv_up")})
    dh0 = _matmul("in_proj_r_dx", dproj_r, w_r, tb=True, addend=dh0, after=sent)
    grad_x, d_meta, g["ln_emb_g"], g["ln_emb_b"] = _ln_emb_bwd(x, p["meta_tokens"], dh0, p["ln_emb_g"])
    return loss, grad_x, d_meta, g


BIG = (("w_in", 1), ("w_q_up", 1), ("w_kv_up", 1), ("w_branch_mla", 1), ("w_branch_fox", 1), ("w_out", 0),
       ("w_ffn_up", 1), ("w_ffn_down", 0))
SMALL_SHARDED = (("meta_tokens", 1), ("conv_w", 1))
EARLY = ("w_in", "meta_tokens")
LATE = {"qkv": ("w_q_up", "w_kv_up", "conv_w"),
        "mix": ("w_branch_mla", "w_branch_fox", "w_out"),
        "ffn": ("w_ffn_up", "w_ffn_down")}
REPLICATED = ("ln_emb_g", "ln_emb_b", "b_gate", "b_forget", "q_norm_g", "kv_norm_g", "ln_mix_g", "ln_mix_b",
              "conv_b", "ln_ffn_g", "ln_ffn_b")
PACK_COLS = 1024


def _pack(flat_list):
    cat = jnp.concatenate(flat_list)
    n = cat.shape[0]
    rows = -(-n // (8 * PACK_COLS)) * 8
    return jnp.pad(cat, (0, rows * PACK_COLS - n)).reshape(rows, PACK_COLS)


def _gathered_full(g3, axis):
    n, r, c = g3.shape
    if axis == 0:
        return g3.reshape(n * r, c)
    return g3.transpose(1, 0, 2).reshape(r, n * c)


def _shard_major(full, axis):
    r, c = full.shape
    if axis == 0:
        return full.reshape(N_DEV, r // N_DEV, c)
    return full.reshape(r, N_DEV, c // N_DEV).transpose(1, 0, 2)


def kernel(x, meta_tokens, ln_emb_g, ln_emb_b, w_in, b_gate, b_forget, q_norm_g, w_q_up, kv_norm_g, w_kv_up, w_branch_mla, w_branch_fox, w_out, ln_mix_g, ln_mix_b, w_ffn_up, conv_w, conv_b, w_ffn_down, ln_ffn_g, ln_ffn_b, loss_target, m_meta_tokens, m_ln_emb_g, m_ln_emb_b, m_w_in, m_b_gate, m_b_forget, m_q_norm_g, m_w_q_up, m_kv_norm_g, m_w_kv_up, m_w_branch_mla, m_w_branch_fox, m_w_out, m_ln_mix_g, m_ln_mix_b, m_w_ffn_up, m_conv_w, m_conv_b, m_w_ffn_down, m_ln_ffn_g, m_ln_ffn_b, v_meta_tokens, v_ln_emb_g, v_ln_emb_b, v_w_in, v_b_gate, v_b_forget, v_q_norm_g, v_w_q_up, v_kv_norm_g, v_w_kv_up, v_w_branch_mla, v_w_branch_fox, v_w_out, v_ln_mix_g, v_ln_mix_b, v_w_ffn_up, v_conv_w, v_conv_b, v_w_ffn_down, v_ln_ffn_g, v_ln_ffn_b):
    names = ("meta_tokens", "ln_emb_g", "ln_emb_b", "w_in", "b_gate", "b_forget", "q_norm_g", "w_q_up", "kv_norm_g",
             "w_kv_up", "w_branch_mla", "w_branch_fox", "w_out", "ln_mix_g", "ln_mix_b", "w_ffn_up", "conv_w", "conv_b",
             "w_ffn_down", "ln_ffn_g", "ln_ffn_b")
    w_args = (meta_tokens, ln_emb_g, ln_emb_b, w_in, b_gate, b_forget, q_norm_g, w_q_up, kv_norm_g, w_kv_up,
              w_branch_mla, w_branch_fox, w_out, ln_mix_g, ln_mix_b, w_ffn_up, conv_w, conv_b, w_ffn_down, ln_ffn_g, ln_ffn_b)
    m_args = (m_meta_tokens, m_ln_emb_g, m_ln_emb_b, m_w_in, m_b_gate, m_b_forget, m_q_norm_g, m_w_q_up, m_kv_norm_g,
              m_w_kv_up, m_w_branch_mla, m_w_branch_fox, m_w_out, m_ln_mix_g, m_ln_mix_b, m_w_ffn_up, m_conv_w, m_conv_b,
              m_w_ffn_down, m_ln_ffn_g, m_ln_ffn_b)
    v_args = (v_meta_tokens, v_ln_emb_g, v_ln_emb_b, v_w_in, v_b_gate, v_b_forget, v_q_norm_g, v_w_q_up, v_kv_norm_g,
              v_w_kv_up, v_w_branch_mla, v_w_branch_fox, v_w_out, v_ln_mix_g, v_ln_mix_b, v_w_ffn_up, v_conv_w, v_conv_b,
              v_w_ffn_down, v_ln_ffn_g, v_ln_ffn_b)
    as2d = lambda a: a.reshape((-1, a.shape[-1])) if a.ndim != 1 else a.reshape(1, -1)
    w = {n: as2d(a) for n, a in zip(names, w_args)}
    m = {n: as2d(a) for n, a in zip(names, m_args)}
    v = {n: as2d(a) for n, a in zip(names, v_args)}
    out_shape = {n: a.shape for n, a in zip(names, w_args)}

    axis_of = dict(BIG + SMALL_SHARDED)
    big = set(n for n, _ in BIG)
    wire = lambda n, a: a.astype(MXU_DTYPE) if n in big else a
    my_id = _my_id()

    early = _allgather("gather_early", [wire(n, w[n]) for n in EARLY])
    p = {n: _gathered_full(g3, axis_of[n]) for n, g3 in zip(EARLY, early) if n != "w_in"}
    p["w_in"] = _w_in_from_shards(early[EARLY.index("w_in")])
    for n in REPLICATED:
        p[n] = w[n].reshape(-1)
    late_src = [[wire(n, w[n]) for n in members] for members in LATE.values()]
    late_handles, late_token = _push_start("gather_late_start", late_src, False, after=early[0])
    late = {group: (members, src, handle)
            for (group, members), src, handle in zip(LATE.items(), late_src, late_handles)}
    sent = {}

    class Comm:
        first_token = (late_token,)

        def late_weights(self, group, after):
            members, src, handle = late[group]
            lands = _push_wait("gather_" + group + "_wait", handle, after)
            out = {}
            for n, own, land in zip(members, src, lands):
                if own.shape[0] % 16:
                    out[n] = _gathered_full(lax.dynamic_update_index_in_dim(land, own, my_id, 0), axis_of[n])
                elif axis_of[n] == 1:
                    out[n] = _cols_from_shards(n + "_repack", land, own)
                else:
                    out[n] = _rows_from_shards(n + "_repack", land, own)
            return out

        def send(self, name, grads):
            names_ = tuple(grads)
            parts = []
            for n in names_:
                if n == "w_in":
                    parts.append(_w_in_grad_to_shards(*grads[n], N_DEV, w[n].shape[1]))
                elif n == "w_ffn_up":
                    parts.append(_cols_to_shards(n + "_grad_unpack", grads[n], N_DEV))
                else:
                    parts.append(_shard_major(grads[n], axis_of[n]).astype(MXU_DTYPE))
            (handle,), token = _push_start("send_" + name + "_start", [parts], True)
            sent[name] = (names_, parts, handle)
            return (token,)

    loss_part, grad_x, d_meta, g = _local_step(x[0], loss_target[0], p, Comm())
    grad_x = grad_x[None]

    small = _pack([d_meta.reshape(-1)] + [g[n].reshape(-1) for n in REPLICATED] + [loss_part.reshape(-1)])
    (small_handle,), small_token = _push_start("send_small_start", [[small]], False)

    res = {}
    prev = small_token
    for name, (names_, parts, handle) in sent.items():
        lands = _push_wait("send_" + name + "_wait", handle, prev)
        for n, part, land in zip(names_, parts, lands):
            own = lax.dynamic_index_in_dim(part, my_id, axis=0, keepdims=False)
            r_, c_ = w[n].shape
            update = _adamw_transposed if (c_ % HP and not r_ % HP) else _adamw
            res[n] = update("adamw_" + n, land, w[n], m[n], v[n], own=own)
            prev = res[n][-1]
    small_all = _push_wait("send_small_wait", small_handle, prev)[0]
    head = jnp.zeros((d_meta.size,), F32)
    rep_w = _pack([head] + [w[n].reshape(-1) for n in REPLICATED])
    rep_m = _pack([head] + [m[n].reshape(-1) for n in REPLICATED])
    rep_v = _pack([head] + [v[n].reshape(-1) for n in REPLICATED])
    rep_res = _adamw("adamw_replicated", small_all, rep_w, rep_m, rep_v, own=small)
    off = d_meta.size
    for n in REPLICATED:
        sz = w[n].size
        res[n] = tuple(a.reshape(-1)[off:off + sz] for a in rep_res)
        off += sz
    loss = rep_res[0].reshape(-1)[off]
    cols = w["meta_tokens"].shape[1]
    meta_rows = lambda a: a.reshape(a.shape[:-2] + (-1,))[..., :d_meta.size].reshape(a.shape[:-2] + d_meta.shape)
    my_cols = lambda a: lax.dynamic_slice_in_dim(a, my_id * cols, cols, axis=a.ndim - 1)
    res["meta_tokens"] = _adamw("adamw_meta_tokens", my_cols(meta_rows(small_all)), w["meta_tokens"],
                                m["meta_tokens"], v["meta_tokens"], own=my_cols(d_meta))

    outs = [loss, grad_x]
    for idx in range(4):
        outs += [res[n][idx].reshape(out_shape[n]) for n in names]
    return tuple(outs)
```
